```python
import math
import jax, jax.numpy as jnp
from jax import lax
import numpy as np

D_MODEL = 1024
BATCH = 16
SEQ = 2048
DEPTH = 1

ATT_HEAD_DIM = 64
ATT_HEADS_PER_GROUP = 12
DILATED_PATTERNS = ((128, 1), (512, 4), (2048, 16))
N_ATT_GROUPS = 3
ATT_HEADS = N_ATT_GROUPS * ATT_HEADS_PER_GROUP
ATT_QKV = ATT_HEADS * ATT_HEAD_DIM
ATT_OUT = ATT_HEADS_PER_GROUP * ATT_HEAD_DIM
BAND_BLOCK = 128
NUM_BUCKETS = 32
MAX_DISTANCE = 2048
SSM_EXPAND = 2
D_INNER = SSM_EXPAND * D_MODEL
SSM_HEAD_DIM = 64
SSM_HEADS = D_INNER // SSM_HEAD_DIM
SSM_GROUPS = 4
D_STATE = 128
CONV_WIDTH = 4
CONV_DIM = D_INNER + 2 * SSM_GROUPS * D_STATE
SSD_CHUNK = 128
PLE_DIM = 256
ALPHA = (2.0 * DEPTH) ** 0.25
BETA = (8.0 * DEPTH) ** -0.25
LN_EPS = 1e-5
RMS_EPS = 1e-5
Q_END = ATT_QKV
K_END = Q_END + ATT_QKV
V_END = K_END + ATT_QKV
GATT_END = V_END + ATT_OUT
Z_END = GATT_END + D_INNER
XBC_END = Z_END + CONV_DIM
DT_END = XBC_END + SSM_HEADS
GMERGE_END = DT_END + 2 * D_MODEL
IN_COLS = GMERGE_END + D_MODEL
BRANCH_ROWS = ATT_OUT + D_INNER

kernel_name = "hybrid_dilated_attn_mamba2_deepnorm"


def _layer_norm(x, g, b):
    xf = x.astype(jnp.float32)
    mu = jnp.mean(xf, -1, keepdims=True)
    var = jnp.mean(jnp.square(xf - mu), -1, keepdims=True)
    return ((xf - mu) * lax.rsqrt(var + LN_EPS) * g.astype(jnp.float32) + b.astype(jnp.float32)).astype(x.dtype)


def _t5_bucket(dist):
    max_exact = NUM_BUCKETS // 2
    d_f = jnp.maximum(dist, 1).astype(jnp.float32)
    large = max_exact + (jnp.log(d_f / max_exact) / math.log(MAX_DISTANCE / max_exact)
                         * (NUM_BUCKETS - max_exact)).astype(jnp.int32)
    large = jnp.minimum(large, NUM_BUCKETS - 1)
    return jnp.where(dist < max_exact, dist, large)


def _dilated_group(q, k, v, bias_table, window, dilation):
    b, s, h, dh = q.shape
    span = dilation * BAND_BLOCK
    s_pad = -(-s // span) * span
    sub_len = s_pad // dilation
    nb = sub_len // BAND_BLOCK

    def to_blocks(t):
        t = jnp.pad(t, ((0, 0), (0, s_pad - s), (0, 0), (0, 0)))
        t = t.reshape(b, sub_len, dilation, h, dh).transpose(0, 2, 3, 1, 4)
        return t.reshape(b, dilation, h, nb, BAND_BLOCK, dh)

    def band(t):
        prev = jnp.pad(t, ((0, 0), (0, 0), (0, 0), (1, 0), (0, 0), (0, 0)))[:, :, :, :-1]
        return jnp.concatenate([prev, t], axis=4)

    qb = to_blocks(q)
    kk = band(to_blocks(k))
    vv = band(to_blocks(v))
    scores = jnp.einsum('brhnqd,brhnkd->brhnqk', qb, kk,
                        preferred_element_type=jnp.float32) * (dh ** -0.5)
    qi = jnp.arange(BAND_BLOCK)[:, None]
    kj = jnp.arange(2 * BAND_BLOCK)[None, :]
    delta = qi + BAND_BLOCK - kj
    blk = jnp.arange(nb)[:, None, None]
    valid = (delta >= 0) & (delta <= window // dilation) & (blk * BAND_BLOCK + kj - BAND_BLOCK >= 0)
    bucket = _t5_bucket(jnp.maximum(delta, 0) * dilation)
    bias = bias_table[bucket].astype(jnp.float32).transpose(2, 0, 1)
    scores = jnp.where(valid, scores + bias[:, None], -jnp.inf)
    m = jnp.max(scores, -1, keepdims=True)
    e = jnp.exp(scores - m)
    den = jnp.sum(e, -1)
    out = jnp.einsum('brhnqk,brhnkd->brhnqd', e, vv.astype(jnp.float32)) / den[..., None]
    lse = m[..., 0] + jnp.log(den)
    out = out.reshape(b, dilation, h, sub_len, dh).transpose(0, 3, 1, 2, 4).reshape(b, s_pad, h, dh)[:, :s]
    lse = lse.reshape(b, dilation, h, sub_len).transpose(0, 3, 1, 2).reshape(b, s_pad, h)[:, :s]
    return out, lse


def _causal_conv(u, w, bias):
    c = u.shape[-1]
    out = lax.conv_general_dilated(u, w[:, None, :].astype(u.dtype), window_strides=(1,),
                                   padding=[(CONV_WIDTH - 1, 0)],
                                   dimension_numbers=('NWC', 'WIO', 'NWC'),
                                   feature_group_count=c)
    return out + bias.astype(u.dtype)


def _ssd(xh, dt, a_head, bm, cm):
    b, s, h, p = xh.shape
    g, n = bm.shape[2], bm.shape[3]
    r = h // g
    nc = s // SSD_CHUNK
    lq = SSD_CHUNK
    a = (dt * a_head).reshape(b, nc, lq, h)
    a_cs = jnp.cumsum(a, axis=2)
    xdt = (xh * dt[..., None]).reshape(b, nc, lq, g, r, p)
    bc = bm.reshape(b, nc, lq, g, n)
    cc = cm.reshape(b, nc, lq, g, n)
    seg = a_cs[:, :, :, None, :] - a_cs[:, :, None, :, :]
    causal = jnp.tril(jnp.ones((lq, lq), dtype=bool))
    lmat = jnp.exp(jnp.where(causal[:, :, None], seg, -jnp.inf)).reshape(b, nc, lq, lq, g, r)
    cb = jnp.einsum('bclgn,bcsgn->bclsg', cc, bc)
    y_diag = jnp.einsum('bclsgr,bcsgrp->bclgrp', cb[..., None] * lmat, xdt)
    decay_to_end = jnp.exp(a_cs[:, :, -1:, :] - a_cs).reshape(b, nc, lq, g, r)
    states = jnp.einsum('bclgn,bclgrp->bcgrpn', bc, xdt * decay_to_end[..., None])
    chunk_decay = jnp.exp(a_cs[:, :, -1, :]).reshape(b, nc, g, r)

    def step(carry, inp):
        st, dec = inp
        return carry * dec[..., None, None] + st, carry

    init = jnp.zeros((b, g, r, p, n), jnp.float32)
    _, prev_states = lax.scan(step, init, (jnp.moveaxis(states, 1, 0), jnp.moveaxis(chunk_decay, 1, 0)))
    prev_states = jnp.moveaxis(prev_states, 0, 1)
    decay_from_start = jnp.exp(a_cs).reshape(b, nc, lq, g, r)
    y_off = jnp.einsum('bclgn,bcgrpn->bclgrp', cc, prev_states) * decay_from_start[..., None]
    return (y_diag + y_off).reshape(b, s, h, p)


def _hybrid_layer(x, p_i, w_in, b_gate, conv_w, conv_b, dt_bias, a_log, d_skip, ssm_norm_w,
                  w_branch, w_out, w_ple, ln_g, ln_b, rel_bias):
    b, s, _ = x.shape
    hcat = jnp.einsum('bsd,de->bse', x, w_in)
    q, k, v, g_att, z, xbc, dt_raw, g_merge, g_ple = jnp.split(
        hcat, [Q_END, K_END, V_END, GATT_END, Z_END, XBC_END, DT_END, GMERGE_END], axis=-1)

    q = q.reshape(b, s, N_ATT_GROUPS, ATT_HEADS_PER_GROUP, ATT_HEAD_DIM)
    k = k.reshape(b, s, N_ATT_GROUPS, ATT_HEADS_PER_GROUP, ATT_HEAD_DIM)
    v = v.reshape(b, s, N_ATT_GROUPS, ATT_HEADS_PER_GROUP, ATT_HEAD_DIM)
    outs, lses = [], []
    for gi, (win, dil) in enumerate(DILATED_PATTERNS):
        hs = slice(gi * ATT_HEADS_PER_GROUP, (gi + 1) * ATT_HEADS_PER_GROUP)
        o, l = _dilated_group(q[:, :, gi], k[:, :, gi], v[:, :, gi], rel_bias[:, hs], win, dil)
        outs.append(o)
        lses.append(l)
    wts = jax.nn.softmax(jnp.stack(lses), axis=0)
    o_att = jnp.sum(wts[..., None] * jnp.stack(outs), axis=0).reshape(b, s, ATT_OUT).astype(x.dtype)
    o_att = o_att * jax.nn.silu(g_att)

    xbc = jax.nn.silu(_causal_conv(xbc, conv_w, conv_b))
    xs, bm, cm = jnp.split(xbc, [D_INNER, D_INNER + SSM_GROUPS * D_STATE], axis=-1)
    xh = xs.astype(jnp.float32).reshape(b, s, SSM_HEADS, SSM_HEAD_DIM)
    bm = bm.astype(jnp.float32).reshape(b, s, SSM_GROUPS, D_STATE)
    cm = cm.astype(jnp.float32).reshape(b, s, SSM_GROUPS, D_STATE)
    dt = jax.nn.softplus(dt_raw.astype(jnp.float32) + dt_bias.astype(jnp.float32))
    a_head = -jnp.exp(a_log.astype(jnp.float32))
    y = _ssd(xh, dt, a_head, bm, cm) + d_skip.astype(jnp.float32)[:, None] * xh
    u = (y.reshape(b, s, D_INNER) * jax.nn.silu(z.astype(jnp.float32))).reshape(b, s, SSM_GROUPS, -1)
    u = u * lax.rsqrt(jnp.mean(jnp.square(u), -1, keepdims=True) + RMS_EPS)
    y_ssm = (u.reshape(b, s, D_INNER) * ssm_norm_w.astype(jnp.float32)).astype(x.dtype)

    y_a = jnp.einsum('bse,ed->bsd', o_att, w_branch[:ATT_OUT])
    y_b = jnp.einsum('bse,ed->bsd', y_ssm, w_branch[ATT_OUT:])
    g_a, g_b = jnp.split(g_merge, 2, axis=-1)
    merged = jax.nn.sigmoid(g_a + b_gate[0]) * y_a + jax.nn.sigmoid(g_b + b_gate[1]) * y_b
    mix = jnp.einsum('bsd,de->bse', merged, w_out)
    ple = jax.nn.sigmoid(g_ple + b_gate[2]) * jnp.einsum('bsq,qd->bsd', p_i, w_ple)
    return _layer_norm(ALPHA * x + mix + ple, ln_g, ln_b)


def _fwd_setup_inputs(seed: int = 0) -> dict:
    key = jax.random.key(seed)
    ks = jax.random.split(key, 16)
    f32 = jnp.float32
    x = jax.random.normal(ks[0], (BATCH, SEQ, D_MODEL), f32)
    p = jax.random.normal(ks[1], (DEPTH, BATCH, SEQ, PLE_DIM), f32)
    col_scale = jnp.ones((IN_COLS,), f32).at[K_END:V_END].set(BETA).at[Z_END:Z_END + D_INNER].set(BETA)
    w_in = jax.random.normal(ks[2], (DEPTH, D_MODEL, IN_COLS), f32) * (D_MODEL ** -0.5) * col_scale
    b_gate = 0.1 * jax.random.normal(ks[3], (DEPTH, 3, D_MODEL), f32)
    conv_w = 0.5 * jax.random.normal(ks[4], (DEPTH, CONV_WIDTH, CONV_DIM), f32)
    conv_b = 0.05 * jax.random.normal(ks[5], (DEPTH, CONV_DIM), f32)
    dt0 = jnp.exp(jax.random.uniform(ks[6], (DEPTH, SSM_HEADS), f32, math.log(1e-3), math.log(1e-1)))
    dt_bias = dt0 + jnp.log(-jnp.expm1(-dt0))
    a_log = jnp.log(jax.random.uniform(ks[7], (DEPTH, SSM_HEADS), f32, 1.0, 16.0))
    d_skip = 1.0 + 0.1 * jax.random.normal(ks[8], (DEPTH, SSM_HEADS), f32)
    ssm_norm_w = 1.0 + 0.05 * jax.random.normal(ks[9], (DEPTH, D_INNER), f32)
    w_branch = jnp.concatenate([
        jax.random.normal(ks[10], (DEPTH, ATT_OUT, D_MODEL), f32) * (ATT_OUT ** -0.5),
        jax.random.normal(ks[11], (DEPTH, D_INNER, D_MODEL), f32) * (D_INNER ** -0.5)], axis=1) * BETA
    w_out = jax.random.normal(ks[12], (DEPTH, D_MODEL, D_MODEL), f32) * (D_MODEL ** -0.5) * BETA
    w_ple = jax.random.normal(ks[13], (DEPTH, PLE_DIM, D_MODEL), f32) * (PLE_DIM ** -0.5) * BETA
    kg, kb = jax.random.split(ks[14])
    ln_g = 1.0 + 0.05 * jax.random.normal(kg, (DEPTH, D_MODEL), f32)
    ln_b = 0.02 * jax.random.normal(kb, (DEPTH, D_MODEL), f32)
    rel_bias = 0.2 * jax.random.normal(ks[15], (NUM_BUCKETS, ATT_HEADS), f32)
    return {"x": x, "p": p, "w_in": w_in, "b_gate": b_gate, "conv_w": conv_w, "conv_b": conv_b,
            "dt_bias": dt_bias, "a_log": a_log, "d_skip": d_skip, "ssm_norm_w": ssm_norm_w,
            "w_branch": w_branch, "w_out": w_out, "w_ple": w_ple, "ln_g": ln_g, "ln_b": ln_b,
            "rel_bias": rel_bias}


def _fwd_reference(x, p, w_in, b_gate, conv_w, conv_b, dt_bias, a_log, d_skip, ssm_norm_w,
              w_branch, w_out, w_ple, ln_g, ln_b, rel_bias):
    for i in range(DEPTH):
        x = _hybrid_layer(x, p[i], w_in[i], b_gate[i], conv_w[i], conv_b[i], dt_bias[i], a_log[i],
                          d_skip[i], ssm_norm_w[i], w_branch[i], w_out[i], w_ple[i], ln_g[i], ln_b[i],
                          rel_bias)
    return x


import jax as _jax
import jax.numpy as _jnp

TWIN_FORMAT = 'train_step'
FWD_PARAMS = ['x', 'p', 'w_in', 'b_gate', 'conv_w', 'conv_b', 'dt_bias', 'a_log', 'd_skip', 'ssm_norm_w', 'w_branch', 'w_out', 'w_ple', 'ln_g', 'ln_b', 'rel_bias']
TWIN_WEIGHTS = ['w_in', 'b_gate', 'conv_w', 'conv_b', 'dt_bias', 'a_log', 'd_skip', 'ssm_norm_w', 'w_branch', 'w_out', 'w_ple', 'ln_g', 'ln_b', 'rel_bias']
TWIN_DIFF_INPUT = 'x'
TWIN_INPUTS = ['x', 'p', 'w_in', 'b_gate', 'conv_w', 'conv_b', 'dt_bias', 'a_log', 'd_skip', 'ssm_norm_w', 'w_branch', 'w_out', 'w_ple', 'ln_g', 'ln_b', 'rel_bias', 'loss_target', 'm_w_in', 'm_b_gate', 'm_conv_w', 'm_conv_b', 'm_dt_bias', 'm_a_log', 'm_d_skip', 'm_ssm_norm_w', 'm_w_branch', 'm_w_out', 'm_w_ple', 'm_ln_g', 'm_ln_b', 'm_rel_bias', 'v_w_in', 'v_b_gate', 'v_conv_w', 'v_conv_b', 'v_dt_bias', 'v_a_log', 'v_d_skip', 'v_ssm_norm_w', 'v_w_branch', 'v_w_out', 'v_w_ple', 'v_ln_g', 'v_ln_b', 'v_rel_bias']
TWIN_OUTPUTS = ['loss', 'grad_x', 'grad_w_in', 'grad_b_gate', 'grad_conv_w', 'grad_conv_b', 'grad_dt_bias', 'grad_a_log', 'grad_d_skip', 'grad_ssm_norm_w', 'grad_w_branch', 'grad_w_out', 'grad_w_ple', 'grad_ln_g', 'grad_ln_b', 'grad_rel_bias', 'delta_w_in', 'delta_b_gate', 'delta_conv_w', 'delta_conv_b', 'delta_dt_bias', 'delta_a_log', 'delta_d_skip', 'delta_ssm_norm_w', 'delta_w_branch', 'delta_w_out', 'delta_w_ple', 'delta_ln_g', 'delta_ln_b', 'delta_rel_bias', 'new_m_w_in', 'new_m_b_gate', 'new_m_conv_w', 'new_m_conv_b', 'new_m_dt_bias', 'new_m_a_log', 'new_m_d_skip', 'new_m_ssm_norm_w', 'new_m_w_branch', 'new_m_w_out', 'new_m_w_ple', 'new_m_ln_g', 'new_m_ln_b', 'new_m_rel_bias', 'new_v_w_in', 'new_v_b_gate', 'new_v_conv_w', 'new_v_conv_b', 'new_v_dt_bias', 'new_v_a_log', 'new_v_d_skip', 'new_v_ssm_norm_w', 'new_v_w_branch', 'new_v_w_out', 'new_v_w_ple', 'new_v_ln_g', 'new_v_ln_b', 'new_v_rel_bias']
TWIN_LEAF_KINDS = {'loss': 'loss', 'grad_x': 'grad_x', 'grad_w_in': 'grad_w', 'grad_b_gate': 'grad_w', 'grad_conv_w': 'grad_w', 'grad_conv_b': 'grad_w', 'grad_dt_bias': 'grad_w', 'grad_a_log': 'grad_w', 'grad_d_skip': 'grad_w', 'grad_ssm_norm_w': 'grad_w', 'grad_w_branch': 'grad_w', 'grad_w_out': 'grad_w', 'grad_w_ple': 'grad_w', 'grad_ln_g': 'grad_w', 'grad_ln_b': 'grad_w', 'grad_rel_bias': 'grad_w', 'delta_w_in': 'delta_w', 'delta_b_gate': 'delta_w', 'delta_conv_w': 'delta_w', 'delta_conv_b': 'delta_w', 'delta_dt_bias': 'delta_w', 'delta_a_log': 'delta_w', 'delta_d_skip': 'delta_w', 'delta_ssm_norm_w': 'delta_w', 'delta_w_branch': 'delta_w', 'delta_w_out': 'delta_w', 'delta_w_ple': 'delta_w', 'delta_ln_g': 'delta_w', 'delta_ln_b': 'delta_w', 'delta_rel_bias': 'delta_w', 'new_m_w_in': 'new_m', 'new_m_b_gate': 'new_m', 'new_m_conv_w': 'new_m', 'new_m_conv_b': 'new_m', 'new_m_dt_bias': 'new_m', 'new_m_a_log': 'new_m', 'new_m_d_skip': 'new_m', 'new_m_ssm_norm_w': 'new_m', 'new_m_w_branch': 'new_m', 'new_m_w_out': 'new_m', 'new_m_w_ple': 'new_m', 'new_m_ln_g': 'new_m', 'new_m_ln_b': 'new_m', 'new_m_rel_bias': 'new_m', 'new_v_w_in': 'new_v', 'new_v_b_gate': 'new_v', 'new_v_conv_w': 'new_v', 'new_v_conv_b': 'new_v', 'new_v_dt_bias': 'new_v', 'new_v_a_log': 'new_v', 'new_v_d_skip': 'new_v', 'new_v_ssm_norm_w': 'new_v', 'new_v_w_branch': 'new_v', 'new_v_w_out': 'new_v', 'new_v_w_ple': 'new_v', 'new_v_ln_g': 'new_v', 'new_v_ln_b': 'new_v', 'new_v_rel_bias': 'new_v'}


def _forward(args):
    return _fwd_reference(*[args[k] for k in FWD_PARAMS])


def _output_shape():
    out = _jax.eval_shape(lambda: _forward(_fwd_setup_inputs(0)))
    return out.shape, out.dtype

N_MICROBATCH = 1
ADAM_LR = 0.001
ADAM_B1 = 0.9
ADAM_B2 = 0.999
ADAM_EPS = 1e-08
ADAM_WD = 0.01
ADAM_STEP = 10
PER_EXAMPLE_BATCH_AXIS = {'x': 0, 'p': 1, 'loss_target': 0}
SHARED_INPUTS = []
_WEIGHT_DTYPES = {'w_in': _jnp.float32, 'b_gate': _jnp.float32, 'conv_w': _jnp.float32, 'conv_b': _jnp.float32, 'dt_bias': _jnp.float32, 'a_log': _jnp.float32, 'd_skip': _jnp.float32, 'ssm_norm_w': _jnp.float32, 'w_branch': _jnp.float32, 'w_out': _jnp.float32, 'w_ple': _jnp.float32, 'ln_g': _jnp.float32, 'ln_b': _jnp.float32, 'rel_bias': _jnp.float32}
MOMENT_SCALE = {'w_in': 1.496837e-02, 'b_gate': 3.243592e-02, 'conv_w': 1.697356e-02, 'conv_b': 5.435886e-02, 'dt_bias': 5.577908e-02, 'a_log': 6.495707e-02, 'd_skip': 1.017205e-01, 'ssm_norm_w': 2.073553e-02, 'w_branch': 4.077721e-02, 'w_out': 4.803110e-02, 'w_ple': 7.757451e-02, 'ln_g': 3.187174e+01, 'ln_b': 7.532662e-01, 'rel_bias': 1.430785e-03}


def _to_microbatches(a, axis):
    t = _jnp.moveaxis(a, axis, 0)
    t = t.reshape((N_MICROBATCH, t.shape[0] // N_MICROBATCH) + t.shape[1:])
    return _jnp.moveaxis(t, 1, axis + 1)


def setup_inputs(seed: int = 0) -> dict:
    inp = _fwd_setup_inputs(seed)
    key = _jax.random.fold_in(_jax.random.key(seed), 7919)
    shape, _ = _output_shape()
    out = dict(inp)
    out["loss_target"] = _jax.random.normal(_jax.random.fold_in(key, 0), shape, _jnp.float32)
    for i, name in enumerate(TWIN_WEIGHTS):
        w = inp[name].astype(_jnp.float32)
        if MOMENT_SCALE is None:
            s = _jnp.sqrt(_jnp.mean(_jnp.square(w)) + 1e-30)
        else:
            s = MOMENT_SCALE[name]
        km, kv = _jax.random.split(_jax.random.fold_in(key, i + 1))
        out[name] = w
        out["m_" + name] = s * _jax.random.normal(km, w.shape, _jnp.float32)
        out["v_" + name] = (s * s) * _jax.random.uniform(kv, w.shape, _jnp.float32, 0.5, 1.5)
    if N_MICROBATCH > 1:
        for name, axis in PER_EXAMPLE_BATCH_AXIS.items():
            out[name] = _to_microbatches(out[name], axis)
    return {'x': out['x'], 'p': out['p'], 'w_in': out['w_in'], 'b_gate': out['b_gate'], 'conv_w': out['conv_w'], 'conv_b': out['conv_b'], 'dt_bias': out['dt_bias'], 'a_log': out['a_log'], 'd_skip': out['d_skip'], 'ssm_norm_w': out['ssm_norm_w'], 'w_branch': out['w_branch'], 'w_out': out['w_out'], 'w_ple': out['w_ple'], 'ln_g': out['ln_g'], 'ln_b': out['ln_b'], 'rel_bias': out['rel_bias'], 'loss_target': out['loss_target'], 'm_w_in': out['m_w_in'], 'm_b_gate': out['m_b_gate'], 'm_conv_w': out['m_conv_w'], 'm_conv_b': out['m_conv_b'], 'm_dt_bias': out['m_dt_bias'], 'm_a_log': out['m_a_log'], 'm_d_skip': out['m_d_skip'], 'm_ssm_norm_w': out['m_ssm_norm_w'], 'm_w_branch': out['m_w_branch'], 'm_w_out': out['m_w_out'], 'm_w_ple': out['m_w_ple'], 'm_ln_g': out['m_ln_g'], 'm_ln_b': out['m_ln_b'], 'm_rel_bias': out['m_rel_bias'], 'v_w_in': out['v_w_in'], 'v_b_gate': out['v_b_gate'], 'v_conv_w': out['v_conv_w'], 'v_conv_b': out['v_conv_b'], 'v_dt_bias': out['v_dt_bias'], 'v_a_log': out['v_a_log'], 'v_d_skip': out['v_d_skip'], 'v_ssm_norm_w': out['v_ssm_norm_w'], 'v_w_branch': out['v_w_branch'], 'v_w_out': out['v_w_out'], 'v_w_ple': out['v_w_ple'], 'v_ln_g': out['v_ln_g'], 'v_ln_b': out['v_ln_b'], 'v_rel_bias': out['v_rel_bias']}


def _loss(weights, diff, rest, loss_target):
    with _jax.named_scope("forward"):
        args = {**rest, TWIN_DIFF_INPUT: diff, **{k: w.astype(_WEIGHT_DTYPES[k]) for k, w in weights.items()}}
        y = _forward(args)
    with _jax.named_scope("loss_head"):
        err = _jnp.square(y.astype(_jnp.float32) - loss_target)
        return 0.5 * _jnp.sum(_jnp.mean(err, axis=-1)) if err.ndim else 0.5 * err


def _adamw(w, g, m, v):
    m = ADAM_B1 * m + (1.0 - ADAM_B1) * g
    v = ADAM_B2 * v + (1.0 - ADAM_B2) * _jnp.square(g)
    m_hat = m / (1.0 - ADAM_B1 ** ADAM_STEP)
    v_hat = v / (1.0 - ADAM_B2 ** ADAM_STEP)
    delta = -ADAM_LR * (m_hat / (_jnp.sqrt(v_hat) + ADAM_EPS) + ADAM_WD * w)
    return delta, m, v


def reference(x, p, w_in, b_gate, conv_w, conv_b, dt_bias, a_log, d_skip, ssm_norm_w, w_branch, w_out, w_ple, ln_g, ln_b, rel_bias, loss_target, m_w_in, m_b_gate, m_conv_w, m_conv_b, m_dt_bias, m_a_log, m_d_skip, m_ssm_norm_w, m_w_branch, m_w_out, m_w_ple, m_ln_g, m_ln_b, m_rel_bias, v_w_in, v_b_gate, v_conv_w, v_conv_b, v_dt_bias, v_a_log, v_d_skip, v_ssm_norm_w, v_w_branch, v_w_out, v_w_ple, v_ln_g, v_ln_b, v_rel_bias):
    given = dict(x=x, p=p, w_in=w_in, b_gate=b_gate, conv_w=conv_w, conv_b=conv_b, dt_bias=dt_bias, a_log=a_log, d_skip=d_skip, ssm_norm_w=ssm_norm_w, w_branch=w_branch, w_out=w_out, w_ple=w_ple, ln_g=ln_g, ln_b=ln_b, rel_bias=rel_bias, loss_target=loss_target, m_w_in=m_w_in, m_b_gate=m_b_gate, m_conv_w=m_conv_w, m_conv_b=m_conv_b, m_dt_bias=m_dt_bias, m_a_log=m_a_log, m_d_skip=m_d_skip, m_ssm_norm_w=m_ssm_norm_w, m_w_branch=m_w_branch, m_w_out=m_w_out, m_w_ple=m_w_ple, m_ln_g=m_ln_g, m_ln_b=m_ln_b, m_rel_bias=m_rel_bias, v_w_in=v_w_in, v_b_gate=v_b_gate, v_conv_w=v_conv_w, v_conv_b=v_conv_b, v_dt_bias=v_dt_bias, v_a_log=v_a_log, v_d_skip=v_d_skip, v_ssm_norm_w=v_ssm_norm_w, v_w_branch=v_w_branch, v_w_out=v_w_out, v_w_ple=v_w_ple, v_ln_g=v_ln_g, v_ln_b=v_ln_b, v_rel_bias=v_rel_bias)
    weights = {n: given[n] for n in TWIN_WEIGHTS}
    shared = {n: given[n] for n in SHARED_INPUTS}
    per_example = {n: given[n] for n in ['x', 'p']}
    grad_fn = _jax.value_and_grad(_loss, argnums=(0, 1))

    def one_microbatch(ex, loss_target):
        ex = dict(ex)
        diff = ex.pop(TWIN_DIFF_INPUT)
        return grad_fn(weights, diff, {**shared, **ex}, loss_target)

    if N_MICROBATCH == 1:
        loss, (grad_w, grad_x) = one_microbatch(per_example, given["loss_target"])
    else:
        def body(carry, xs):
            loss_sum, grad_sum = carry
            l_k, (gw_k, gx_k) = one_microbatch(xs[0], xs[1])
            with _jax.named_scope("update"):
                return (loss_sum + l_k, _jax.tree.map(_jnp.add, grad_sum, gw_k)), gx_k

        init = (_jnp.zeros((), _jnp.float32), _jax.tree.map(_jnp.zeros_like, weights))
        (loss, grad_w), grad_x = _jax.lax.scan(body, init, (per_example, given["loss_target"]))
    with _jax.named_scope("update"):
        delta_w, new_m, new_v = {}, {}, {}
        for n in TWIN_WEIGHTS:
            delta_w[n], new_m[n], new_v[n] = _adamw(weights[n], grad_w[n], given["m_" + n], given["v_" + n])
    return (loss, grad_x, *[grad_w[n] for n in TWIN_WEIGHTS], *[delta_w[n] for n in TWIN_WEIGHTS],
            *[new_m[n] for n in TWIN_WEIGHTS], *[new_v[n] for n in TWIN_WEIGHTS])
```

```python
import functools
import math

import numpy as np
import jax
import jax.numpy as jnp
from jax import lax
from jax.experimental import pallas as pl
from jax.experimental.pallas import tpu as pltpu

F32, BF16 = jnp.float32, jnp.bfloat16
HIGHEST = lax.Precision.HIGHEST

D_MODEL = 1024
HEAD_DIM = 64
GROUP_HEADS = 12
ATT_OUT = GROUP_HEADS * HEAD_DIM
PATTERNS = ((128, 1), (512, 4), (2048, 16))
BAND = 128
NUM_BUCKETS = 32
MAX_DISTANCE = 2048
D_INNER = 2048
SSM_HEADS = 32
SSM_GROUPS = 4
GROUP_SSM_HEADS = SSM_HEADS // SSM_GROUPS
D_STATE = 128
CHUNK = 128
PLE_DIM = 256
ALPHA = 2.0 ** 0.25
LN_EPS = 1e-5
RMS_EPS = 1e-5
ADAM_LR, ADAM_B1, ADAM_B2, ADAM_EPS, ADAM_WD, ADAM_STEP = 0.001, 0.9, 0.999, 1e-08, 0.01, 10
NEG = -1e30

QKV_W = 3 * ATT_OUT
IN_COLS = 15904
SHARD_COLS = IN_COLS // 4
DT_COL = 12800
DT_PAD = 96
INT_COLS = IN_COLS + DT_PAD
WIN_STRIDE = 31 * 128
WIN_COLS = 4096
SHARD3_SPLIT = DT_COL + 32 - 3 * SHARD_COLS

VMEM_LIMIT_BYTES = 56 * 1024 * 1024
LANE = 128
MESH = pl.DeviceIdType.MESH


def _pcall(body, **kw):
    return pl.pallas_call(body, **kw)


def _params(*sem):
    return pltpu.CompilerParams(dimension_semantics=sem, vmem_limit_bytes=VMEM_LIMIT_BYTES)


def _sigmoid(v):
    return jax.nn.sigmoid(v)


def _pick_tn(n):
    for t in (1024, 768, 512, 256, 128):
        if n % t == 0:
            return t
    raise ValueError(n)


def _token_tiling(seq, d):
    sub = seq // d
    tm = min(512, sub)
    res = max(1, min(d, 512 // sub))
    return sub, tm, res


def _proj(a3, w, d, out_dtype, name):
    nb, seq, kdim = a3.shape
    n = w.shape[1]
    sub, tm, res = _token_tiling(seq, d)
    tn = _pick_tn(n)

    def body(a_ref, w_ref, o_ref):
        for r in range(res):
            a = a_ref[:, r * kdim:(r + 1) * kdim].astype(BF16)
            o_ref[r] = jnp.dot(a, w_ref[...], preferred_element_type=F32).astype(out_dtype)

    return _pcall(
        body, name=name, grid=(n // tn, nb, d // res, sub // tm),
        in_specs=[pl.BlockSpec((None, tm, res * kdim), lambda j, b, rb, i: (b, i, rb)),
                  pl.BlockSpec((kdim, tn), lambda j, b, rb, i: (0, j))],
        out_specs=pl.BlockSpec((None, res, tm, tn), lambda j, b, rb, i: (b, rb, i, j)),
        out_shape=jax.ShapeDtypeStruct((nb, d, sub, n), out_dtype),
        compiler_params=_params("parallel", "parallel", "parallel", "parallel"),
    )(a3.reshape(nb, sub, d * kdim), w)


def _dx(dh4, w, acc3, d, name):
    nb, _, sub, wd = dh4.shape
    kout = w.shape[0]
    seq = sub * d
    _, tm, res = _token_tiling(seq, d)
    has_acc = acc3 is not None

    def body(*refs):
        dh_ref, w_ref = refs[0], refs[1]
        o_ref = refs[-1]
        for r in range(res):
            v = lax.dot_general(dh_ref[r].astype(BF16), w_ref[...], (((1,), (1,)), ((), ())),
                                preferred_element_type=F32)
            if has_acc:
                v = v + refs[2][:, r * kout:(r + 1) * kout]
            o_ref[:, r * kout:(r + 1) * kout] = v

    tok_spec = pl.BlockSpec((None, tm, res * kout), lambda b, rb, i: (b, i, rb))
    in_specs = [pl.BlockSpec((None, res, tm, wd), lambda b, rb, i: (b, rb, i, 0)),
                pl.BlockSpec((kout, wd), lambda b, rb, i: (0, 0))]
    args = [dh4, w]
    if has_acc:
        in_specs.append(tok_spec)
        args.append(acc3.reshape(nb, sub, d * kout))
    out = _pcall(
        body, name=name, grid=(nb, d // res, sub // tm), in_specs=in_specs, out_specs=tok_spec,
        out_shape=jax.ShapeDtypeStruct((nb, sub, d * kout), F32),
        input_output_aliases={2: 0} if has_acc else {},
        compiler_params=_params("parallel", "parallel", "parallel"),
    )(*args)
    return out.reshape(nb, seq, kout)


def _dw(a3, dh4, d, out_dtype, name):
    nb, seq, kdim = a3.shape
    n = dh4.shape[-1]
    sub, tm, res = _token_tiling(seq, d)
    tn = _pick_tn(n)
    grid = (n // tn, nb, d // res, sub // tm)

    def body(a_ref, dh_ref, o_ref, acc_ref):
        b, rb, i = pl.program_id(1), pl.program_id(2), pl.program_id(3)

        @pl.when((b == 0) & (rb == 0) & (i == 0))
        def _():
            acc_ref[...] = jnp.zeros_like(acc_ref)

        for r in range(res):
            a = a_ref[:, r * kdim:(r + 1) * kdim].astype(BF16)
            acc_ref[...] += lax.dot_general(a, dh_ref[r].astype(BF16), (((0,), (0,)), ((), ())),
                                            preferred_element_type=F32)

        @pl.when((b == grid[1] - 1) & (rb == grid[2] - 1) & (i == grid[3] - 1))
        def _():
            o_ref[...] = acc_ref[...].astype(out_dtype)

    return _pcall(
        body, name=name, grid=grid,
        in_specs=[pl.BlockSpec((None, tm, res * kdim), lambda j, b, rb, i: (b, i, rb)),
                  pl.BlockSpec((None, res, tm, tn), lambda j, b, rb, i: (b, rb, i, j))],
        out_specs=pl.BlockSpec((kdim, tn), lambda j, b, rb, i: (0, j)),
        out_shape=jax.ShapeDtypeStruct((kdim, n), out_dtype),
        scratch_shapes=[pltpu.VMEM((kdim, tn), F32)],
        compiler_params=_params("parallel", "arbitrary", "arbitrary", "arbitrary"),
    )(a3.reshape(nb, sub, d * kdim), dh4)


def _qkv_cols(g):
    cols = []
    for hp in range(ATT_OUT // LANE):
        for part in range(3):
            cols.append(part * QKV_W + g * ATT_OUT + hp * LANE)
    return cols


def _segments():
    segs = [("qkv%d" % g, [(c, LANE) for c in _qkv_cols(g)]) for g in range(3)]
    segs += [("gatt", [(3 * QKV_W, ATT_OUT)]),
             ("z", [(3 * QKV_W + ATT_OUT, D_INNER)]),
             ("xs", [(9728, D_INNER)]), ("bm", [(9728 + D_INNER, 512)]), ("cm", [(9728 + D_INNER + 512, 512)]),
             ("dt", [(DT_COL, LANE)]),
             ("gm", [(DT_COL + LANE, 2 * D_MODEL)]),
             ("gp", [(DT_COL + LANE + 2 * D_MODEL, D_MODEL)])]
    return segs


def _assemble(win):
    segs = _segments()
    tr = 128

    def body(win_ref, *outs):
        def cols(start, width):
            parts = []
            t = start
            while t < start + width:
                k = min(t // WIN_STRIDE, 3)
                nxt = min(start + width, (k + 1) * WIN_STRIDE if k < 3 else INT_COLS)
                if t % WIN_STRIDE == 0 and 0 < k and t // WIN_STRIDE == k:
                    tile = win_ref[k, :, 0:LANE] + win_ref[k - 1, :, WIN_STRIDE:WIN_STRIDE + LANE]
                    parts.append(tile)
                    t += LANE
                    continue
                lo = t - k * WIN_STRIDE
                parts.append(win_ref[k, :, lo:lo + (nxt - t)])
                t = nxt
            return parts

        for (_, pieces), o_ref in zip(segs, outs):
            off = 0
            for start, width in pieces:
                for part in cols(start, width):
                    o_ref[:, off:off + part.shape[1]] = part
                    off += part.shape[1]

    widths = [sum(w for _, w in pieces) for _, pieces in segs]
    outs = _pcall(
        body, name="assemble_w_in", grid=(D_MODEL // tr,),
        in_specs=[pl.BlockSpec((4, tr, WIN_COLS), lambda i: (0, i, 0))],
        out_specs=[pl.BlockSpec((tr, w), lambda i: (i, 0)) for w in widths],
        out_shape=[jax.ShapeDtypeStruct((D_MODEL, w), BF16) for w in widths],
        compiler_params=_params("parallel"),
    )(win)
    return {name: o for (name, _), o in zip(segs, outs)}


def _pack(dsegs):
    segs = _segments()
    tr = 128

    def body(*refs):
        ins, o_ref = refs[:-1], refs[-1]
        o_ref[...] = jnp.zeros_like(o_ref)
        for (_, pieces), s_ref in zip(segs, ins):
            off = 0
            for start, width in pieces:
                for k in range(4):
                    lo, hi = k * WIN_STRIDE, k * WIN_STRIDE + WIN_COLS
                    a, b = max(start, lo), min(start + width, hi)
                    if a < b:
                        o_ref[k, :, a - lo:b - lo] = s_ref[:, off + a - start:off + b - start]
                off += width

    widths = [sum(w for _, w in pieces) for _, pieces in segs]
    return _pcall(
        body, name="pack_dw_in", grid=(D_MODEL // tr,),
        in_specs=[pl.BlockSpec((tr, w), lambda i: (i, 0)) for w in widths],
        out_specs=pl.BlockSpec((4, tr, WIN_COLS), lambda i: (0, i, 0)),
        out_shape=jax.ShapeDtypeStruct((4, D_MODEL, WIN_COLS), BF16),
        compiler_params=_params("parallel"),
    )(*[dsegs[name] for name, _ in segs])


def _shard_to_window(w_shard, k):
    def plain(kk):
        return lambda w: jnp.pad(w, ((0, 0), (8 * kk, WIN_COLS - SHARD_COLS - 8 * kk)))

    def last(w):
        z = lambda n: jnp.zeros((w.shape[0], n), w.dtype)
        return jnp.concatenate([z(24), w[:, :SHARD3_SPLIT], z(DT_PAD), w[:, SHARD3_SPLIT:]], axis=1)

    return lax.switch(k, [plain(0), plain(1), plain(2), last], w_shard)


def _window_to_shard(win, k):
    def plain(kk):
        return lambda w: w[:, 8 * kk:8 * kk + SHARD_COLS]

    def last(w):
        return jnp.concatenate([w[:, 24:24 + SHARD3_SPLIT], w[:, 24 + SHARD3_SPLIT + DT_PAD:]], axis=1)

    return lax.switch(k, [plain(0), plain(1), plain(2), last], win)


def _bucket_maps():
    qi = np.arange(BAND)[:, None]
    kj = np.arange(2 * BAND)[None, :]
    delta = qi + BAND - kj
    maps = []
    for window, dil in PATTERNS:
        valid = (delta >= 0) & (delta <= window // dil)
        dist = np.maximum(delta, 0) * dil
        max_exact = NUM_BUCKETS // 2
        d_f = np.maximum(dist, 1).astype(np.float32)
        large = max_exact + (np.log(d_f / np.float32(max_exact)) / np.float32(math.log(MAX_DISTANCE / max_exact))
                             * np.float32(NUM_BUCKETS - max_exact)).astype(np.int32)
        large = np.minimum(large, NUM_BUCKETS - 1)
        bucket = np.where(dist < max_exact, dist, large)
        maps.append(np.where(valid, bucket, -1).astype(np.int32))
    return np.stack(maps)


def _bias_tables(rel_bias, bmaps):
    def body(rb_ref, bm_ref, o_ref):
        h = pl.program_id(0)
        bm = bm_ref[...]
        acc = jnp.full(bm.shape, NEG, F32)
        for b in range(NUM_BUCKETS):
            acc = jnp.where(bm == b, rb_ref[b, h], acc)
        o_ref[...] = acc

    return _pcall(
        body, name="bias_tables", grid=(3 * GROUP_HEADS,),
        in_specs=[pl.BlockSpec(memory_space=pltpu.SMEM),
                  pl.BlockSpec((None, BAND, 2 * BAND), lambda h: (h // GROUP_HEADS, 0, 0))],
        out_specs=pl.BlockSpec((None, BAND, 2 * BAND), lambda h: (h, 0, 0)),
        out_shape=jax.ShapeDtypeStruct((3 * GROUP_HEADS, BAND, 2 * BAND), F32),
        compiler_params=_params("parallel"),
    )(rel_bias, bmaps)


def _bias_grad(dbias, bmaps):
    def body(db_ref, bm_ref, o_ref):
        bm = bm_ref[...]
        db = db_ref[...]
        lane = lax.broadcasted_iota(jnp.int32, (1, LANE), 1)
        vec = jnp.zeros((1, LANE), F32)
        for b in range(NUM_BUCKETS):
            s = jnp.sum(jnp.where(bm == b, db, 0.0), keepdims=True)
            vec = jnp.where(lane == b, s, vec)
        o_ref[...] = vec

    return _pcall(
        body, name="bias_grad", grid=(3 * GROUP_HEADS,),
        in_specs=[pl.BlockSpec((None, BAND, 2 * BAND), lambda h: (h, 0, 0)),
                  pl.BlockSpec((None, BAND, 2 * BAND), lambda h: (h // GROUP_HEADS, 0, 0))],
        out_specs=pl.BlockSpec((None, 1, LANE), lambda h: (h, 0, 0)),
        out_shape=jax.ShapeDtypeStruct((3 * GROUP_HEADS, 1, LANE), F32),
        compiler_params=_params("parallel"),
    )(dbias, bmaps)


def _rows(n):
    return pl.ds(pl.multiple_of(n * BAND, BAND), BAND)


def _attn_fwd(qkv4, bias, d, name):
    nb, _, sub, _ = qkv4.shape
    nblk = sub // BAND
    scale = HEAD_DIM ** -0.5
    npair = ATT_OUT // LANE

    def body(q_ref, k_ref, v_ref, bias_ref, o_ref, l_ref):
        for h in range(2):
            hs = slice(h * HEAD_DIM, (h + 1) * HEAD_DIM)

            def block(n, with_prev):
                q = q_ref[_rows(n), hs]
                s_c = lax.dot_general(q, k_ref[_rows(n), hs], (((1,), (1,)), ((), ())),
                                      preferred_element_type=F32) * scale + bias_ref[h, :, BAND:]
                m = jnp.max(s_c, -1, keepdims=True)
                if with_prev:
                    s_p = lax.dot_general(q, k_ref[_rows(n - 1), hs], (((1,), (1,)), ((), ())),
                                          preferred_element_type=F32) * scale + bias_ref[h, :, :BAND]
                    m = jnp.maximum(m, jnp.max(s_p, -1, keepdims=True))
                e_c = jnp.exp(s_c - m)
                den = jnp.sum(e_c, -1, keepdims=True)
                acc = jnp.dot(e_c.astype(BF16), v_ref[_rows(n), hs], preferred_element_type=F32)
                if with_prev:
                    e_p = jnp.exp(s_p - m)
                    den = den + jnp.sum(e_p, -1, keepdims=True)
                    acc = acc + jnp.dot(e_p.astype(BF16), v_ref[_rows(n - 1), hs], preferred_element_type=F32)
                o_ref[_rows(n), hs] = acc / den
                l_ref[_rows(n), hs] = jnp.broadcast_to(m + jnp.log(den), (BAND, HEAD_DIM))

            block(0, False)
            if nblk > 1:
                def loop(n, carry):
                    block(n, True)
                    return carry
                lax.fori_loop(1, nblk, loop, 0)

    qspec = lambda part: pl.BlockSpec((None, None, sub, LANE), lambda hp, b, r: (b, r, 0, 3 * hp + part))
    ospec = pl.BlockSpec((None, sub, LANE), lambda hp, b, r: (b, 0, r * npair + hp))
    o, l = _pcall(
        body, name=name, grid=(npair, nb, d),
        in_specs=[qspec(0), qspec(1), qspec(2),
                  pl.BlockSpec((2, BAND, 2 * BAND), lambda hp, b, r: (hp, 0, 0))],
        out_specs=[ospec, ospec],
        out_shape=[jax.ShapeDtypeStruct((nb, sub, d * ATT_OUT), F32)] * 2,
        compiler_params=_params("parallel", "parallel", "parallel"),
    )(qkv4, qkv4, qkv4, bias)
    return o.reshape(nb, sub * d, ATT_OUT), l.reshape(nb, sub * d, ATT_OUT)


def _attn_bwd(qkv4, bias, do_att, o_att, lse, d, name):
    nb, _, sub, _ = qkv4.shape
    nblk = sub // BAND
    scale = HEAD_DIM ** -0.5
    npair = ATT_OUT // LANE
    nt = (((1,), (1,)), ((), ()))
    tn = (((0,), (0,)), ((), ()))

    def body(q_ref, k_ref, v_ref, bias_ref, do_ref, o_ref, l_ref, dqkv_ref, db_ref, dk_acc, dv_acc):
        b, r = pl.program_id(1), pl.program_id(2)

        @pl.when((b == 0) & (r == 0))
        def _():
            db_ref[...] = jnp.zeros_like(db_ref)

        dk_acc[...] = jnp.zeros_like(dk_acc)
        dv_acc[...] = jnp.zeros_like(dv_acc)
        for h in range(2):
            hs = slice(h * HEAD_DIM, (h + 1) * HEAD_DIM)

            def block(n, with_prev):
                q = q_ref[_rows(n), hs]
                do = do_ref[_rows(n), hs]
                do16 = do.astype(BF16)
                ebar = jnp.sum(do * o_ref[_rows(n), hs], -1, keepdims=True)
                lcol = l_ref[_rows(n), h * HEAD_DIM:h * HEAD_DIM + 1]

                def side(kn, bias_blk):
                    k = k_ref[_rows(kn), hs]
                    v = v_ref[_rows(kn), hs]
                    s = lax.dot_general(q, k, nt, preferred_element_type=F32) * scale + bias_blk
                    p = jnp.exp(s - lcol)
                    dp = lax.dot_general(do16, v, nt, preferred_element_type=F32)
                    ds = p * (dp - ebar)
                    ds16 = ds.astype(BF16)
                    dq = jnp.dot(ds16, k, preferred_element_type=F32)
                    dk_acc[_rows(kn), hs] += lax.dot_general(ds16, q, tn, preferred_element_type=F32) * scale
                    dv_acc[_rows(kn), hs] += lax.dot_general(p.astype(BF16), do16, tn, preferred_element_type=F32)
                    return dq, ds

                dq, ds_c = side(n, bias_ref[h, :, BAND:])
                db_ref[h, :, BAND:] += ds_c
                if with_prev:
                    dq_p, ds_p = side(n - 1, bias_ref[h, :, :BAND])
                    dq = dq + dq_p
                    db_ref[h, :, :BAND] += ds_p
                dqkv_ref[_rows(n), hs] = (dq * scale).astype(BF16)

            block(0, False)
            if nblk > 1:
                def loop(n, carry):
                    block(n, True)
                    return carry
                lax.fori_loop(1, nblk, loop, 0)
        dqkv_ref[:, LANE:2 * LANE] = dk_acc[...].astype(BF16)
        dqkv_ref[:, 2 * LANE:3 * LANE] = dv_acc[...].astype(BF16)

    qspec = lambda part: pl.BlockSpec((None, None, sub, LANE), lambda hp, b, r: (b, r, 0, 3 * hp + part))
    nspec = pl.BlockSpec((None, sub, LANE), lambda hp, b, r: (b, 0, r * npair + hp))
    view = lambda t: t.reshape(nb, sub, d * ATT_OUT)
    return _pcall(
        body, name=name, grid=(npair, nb, d),
        in_specs=[qspec(0), qspec(1), qspec(2),
                  pl.BlockSpec((2, BAND, 2 * BAND), lambda hp, b, r: (hp, 0, 0)), nspec, nspec, nspec],
        out_specs=[pl.BlockSpec((None, None, sub, 3 * LANE), lambda hp, b, r: (b, r, 0, hp)),
                   pl.BlockSpec((2, BAND, 2 * BAND), lambda hp, b, r: (hp, 0, 0))],
        out_shape=[jax.ShapeDtypeStruct(qkv4.shape, BF16),
                   jax.ShapeDtypeStruct((GROUP_HEADS, BAND, 2 * BAND), F32)],
        scratch_shapes=[pltpu.VMEM((sub, LANE), F32), pltpu.VMEM((sub, LANE), F32)],
        compiler_params=_params("parallel", "arbitrary", "arbitrary"),
    )(qkv4, qkv4, qkv4, bias, view(do_att), view(o_att), view(lse))


def _combine_fwd(os, ls, gatt):
    nb, seq, _ = gatt.shape
    tm = 512

    def body(o0, o1, o2, l0, l1, l2, g_ref, oa_ref, oatt_ref, lse_ref):
        m = jnp.maximum(jnp.maximum(l0[...], l1[...]), l2[...])
        tot = m + jnp.log(jnp.exp(l0[...] - m) + jnp.exp(l1[...] - m) + jnp.exp(l2[...] - m))
        o = (jnp.exp(l0[...] - tot) * o0[...] + jnp.exp(l1[...] - tot) * o1[...]
             + jnp.exp(l2[...] - tot) * o2[...])
        g = g_ref[...]
        oa_ref[...] = (o * (g * _sigmoid(g))).astype(BF16)
        oatt_ref[...] = o
        lse_ref[...] = tot

    spec = pl.BlockSpec((None, tm, ATT_OUT), lambda b, i: (b, i, 0))
    return _pcall(
        body, name="attn_combine", grid=(nb, seq // tm), in_specs=[spec] * 7, out_specs=[spec] * 3,
        out_shape=[jax.ShapeDtypeStruct((nb, seq, ATT_OUT), BF16), jax.ShapeDtypeStruct((nb, seq, ATT_OUT), F32),
                   jax.ShapeDtypeStruct((nb, seq, ATT_OUT), F32)],
        compiler_params=_params("parallel", "parallel"),
    )(*os, *ls, gatt)


def _combine_bwd(doa, gatt, o_att):
    nb, seq, _ = gatt.shape
    tm = 512

    def body(doa_ref, g_ref, o_ref, do_ref, dg_ref):
        g = g_ref[...]
        sg = _sigmoid(g)
        do_ref[...] = doa_ref[...] * (g * sg)
        dg_ref[...] = (doa_ref[...] * o_ref[...] * (sg * (1.0 + g * (1.0 - sg)))).astype(BF16)

    spec = pl.BlockSpec((None, tm, ATT_OUT), lambda b, i: (b, i, 0))
    return _pcall(
        body, name="attn_combine_bwd", grid=(nb, seq // tm), in_specs=[spec] * 3, out_specs=[spec] * 2,
        out_shape=[jax.ShapeDtypeStruct((nb, seq, ATT_OUT), F32), jax.ShapeDtypeStruct((nb, seq, ATT_OUT), BF16)],
        compiler_params=_params("parallel", "parallel"),
    )(doa, gatt, o_att)


CONV_TM = 512
CONV_TC = 512


def _shift_down(cur, halo, k):
    rolled = pltpu.roll(cur, k, 0)
    hro = pltpu.roll(halo, k, 0)
    row = lax.broadcasted_iota(jnp.int32, hro.shape, 0)
    return jnp.concatenate([jnp.where(row < k, hro, rolled[:8]), rolled[8:]], axis=0)


def _shift_up(cur, halo, k):
    n = cur.shape[0]
    rolled = pltpu.roll(cur, n - k, 0)
    hro = pltpu.roll(halo, 8 - k, 0)
    row = lax.broadcasted_iota(jnp.int32, hro.shape, 0)
    return jnp.concatenate([rolled[:n - 8], jnp.where(row >= 8 - k, hro, rolled[n - 8:])], axis=0)


def _conv_pre(cur, halo, w_ref, b_ref):
    acc = cur * w_ref[3:4, :] + b_ref[...]
    for k in range(1, 4):
        acc = acc + _shift_down(cur, halo, k) * w_ref[3 - k:4 - k, :]
    return acc


def _conv_specs(seq):
    nblk = seq // CONV_TM
    cur = pl.BlockSpec((None, CONV_TM, CONV_TC), lambda cb, b, i: (b, i, cb))
    prev = pl.BlockSpec((None, 8, CONV_TC), lambda cb, b, i: (b, jnp.maximum(i * (CONV_TM // 8) - 1, 0), cb))
    nxt = pl.BlockSpec((None, 8, CONV_TC),
                       lambda cb, b, i: (b, jnp.minimum((i + 1) * (CONV_TM // 8), seq // 8 - 1), cb))
    wspec = pl.BlockSpec((4, CONV_TC), lambda cb, b, i: (0, cb))
    bspec = pl.BlockSpec((1, CONV_TC), lambda cb, b, i: (0, cb))
    return nblk, cur, prev, nxt, wspec, bspec


def _conv_fwd(xin, w4, bias, name):
    nb, seq, ch = xin.shape
    _, cur, prev, _, wspec, bspec = _conv_specs(seq)

    def body(x_ref, h_ref, w_ref, b_ref, o_ref):
        halo = jnp.where(pl.program_id(2) > 0, h_ref[...], 0.0)
        pre = _conv_pre(x_ref[...], halo, w_ref, b_ref)
        o_ref[...] = pre * _sigmoid(pre)

    return _pcall(
        body, name=name, grid=(ch // CONV_TC, nb, seq // CONV_TM),
        in_specs=[cur, prev, wspec, bspec], out_specs=cur,
        out_shape=jax.ShapeDtypeStruct(xin.shape, F32),
        compiler_params=_params("parallel", "parallel", "parallel"),
    )(xin, xin, w4, bias)


def _conv_bwd_pre(dact, xin, w4, bias, name):
    nb, seq, ch = xin.shape
    _, cur, prev, _, wspec, bspec = _conv_specs(seq)

    def body(da_ref, x_ref, h_ref, w_ref, b_ref, dp_ref, s_ref):
        b, i = pl.program_id(1), pl.program_id(2)

        @pl.when((b == 0) & (i == 0))
        def _():
            s_ref[...] = jnp.zeros_like(s_ref)

        halo = jnp.where(i > 0, h_ref[...], 0.0)
        x = x_ref[...]
        pre = _conv_pre(x, halo, w_ref, b_ref)
        sg = _sigmoid(pre)
        dpre = da_ref[...] * (sg * (1.0 + pre * (1.0 - sg)))
        dp_ref[...] = dpre
        s_ref[3:4, :] += jnp.sum(dpre * x, 0, keepdims=True)
        for k in range(1, 4):
            s_ref[3 - k:4 - k, :] += jnp.sum(dpre * _shift_down(x, halo, k), 0, keepdims=True)
        s_ref[4:5, :] += jnp.sum(dpre, 0, keepdims=True)

    return _pcall(
        body, name=name, grid=(ch // CONV_TC, nb, seq // CONV_TM),
        in_specs=[cur, cur, prev, wspec, bspec],
        out_specs=[cur, pl.BlockSpec((8, CONV_TC), lambda cb, b, i: (0, cb))],
        out_shape=[jax.ShapeDtypeStruct(xin.shape, F32), jax.ShapeDtypeStruct((8, ch), F32)],
        compiler_params=_params("parallel", "arbitrary", "arbitrary"),
    )(dact, xin, xin, w4, bias)


def _conv_bwd_x(dpre, w4, name):
    nb, seq, ch = dpre.shape
    nblk, cur, _, nxt, wspec, _ = _conv_specs(seq)

    def body(d_ref, n_ref, w_ref, o_ref):
        halo = jnp.where(pl.program_id(2) < nblk - 1, n_ref[...], 0.0)
        cur_v = d_ref[...]
        acc = cur_v * w_ref[3:4, :]
        for j in range(1, 4):
            acc = acc + _shift_up(cur_v, halo, j) * w_ref[3 - j:4 - j, :]
        o_ref[...] = acc.astype(BF16)

    out = _pcall(
        body, name=name, grid=(ch // CONV_TC, nb, seq // CONV_TM),
        in_specs=[cur, nxt, wspec], out_specs=cur,
        out_shape=jax.ShapeDtypeStruct(dpre.shape, BF16),
        compiler_params=_params("parallel", "parallel", "parallel"),
    )(dpre, dpre, w4)
    return out.reshape(nb, 1, seq, ch)


def _softplus_sig(dt_raw, dt_bias_row):
    nb, seq, _ = dt_raw.shape
    tm = 512

    def body(r_ref, b_ref, sp_ref, sg_ref):
        v = r_ref[...] + b_ref[...]
        sp_ref[...] = jnp.maximum(v, 0.0) + jnp.log1p(jnp.exp(-jnp.abs(v)))
        sg_ref[...] = _sigmoid(v)

    spec = pl.BlockSpec((None, tm, LANE), lambda b, i: (b, i, 0))
    return _pcall(
        body, name="dt_softplus", grid=(nb, seq // tm),
        in_specs=[spec, pl.BlockSpec((1, LANE), lambda b, i: (0, 0))], out_specs=[spec, spec],
        out_shape=[jax.ShapeDtypeStruct(dt_raw.shape, F32)] * 2,
        compiler_params=_params("parallel", "parallel"),
    )(dt_raw, dt_bias_row)


def _group_lanes(t):
    pads = [(0, 0)] * (t.ndim - 1) + [(0, LANE - GROUP_SSM_HEADS)]
    return jnp.stack([jnp.pad(t[..., GROUP_SSM_HEADS * g:GROUP_SSM_HEADS * (g + 1)], pads) for g in range(SSM_GROUPS)])


def _ungroup_lanes(t):
    return jnp.concatenate([t[g][..., :GROUP_SSM_HEADS] for g in range(SSM_GROUPS)], axis=-1)


def _decays(dt, al_ref):
    row = lax.broadcasted_iota(jnp.int32, (CHUNK, CHUNK), 0)
    col = lax.broadcasted_iota(jnp.int32, (CHUNK, CHUNK), 1)
    tril = (row >= col).astype(F32)
    triu = (row <= col).astype(F32)
    arow = -jnp.exp(al_ref[...])
    a = dt * arow
    acs = jnp.dot(tril, a, precision=HIGHEST, preferred_element_type=F32)
    acs_t = lax.dot_general(a, triu, (((0,), (0,)), ((), ())), precision=HIGHEST, preferred_element_type=F32)
    return arow, acs, acs_t, row >= col, triu


def _ssd_specs(nb, seq):
    nc = seq // CHUNK
    hw = GROUP_SSM_HEADS * HEAD_DIM

    def mk(rev):
        cidx = (lambda c: nc - 1 - c) if rev else (lambda c: c)
        wide = pl.BlockSpec((None, CHUNK, hw), lambda g, b, c: (b, cidx(c), g))
        state = pl.BlockSpec((None, CHUNK, D_STATE), lambda g, b, c: (b, cidx(c), g))
        lanes = pl.BlockSpec((None, None, CHUNK, LANE), lambda g, b, c: (g, b, cidx(c), 0))
        prev = pl.BlockSpec((None, None, None, GROUP_SSM_HEADS, D_STATE, HEAD_DIM),
                            lambda g, b, c: (b, cidx(c), g, 0, 0, 0))
        return wide, state, lanes, prev

    grow = pl.BlockSpec((None, 1, LANE), lambda g, b, c: (g, 0, 0))
    nwspec = pl.BlockSpec((1, hw), lambda g, b, c: (0, g))
    return nc, hw, mk, grow, nwspec


def _ssd_fwd(xs, bm, cm, dtg, z, alog_g, dskip_g, normw):
    nb, seq, _ = xs.shape
    nc, hw, mk, grow, nwspec = _ssd_specs(nb, seq)
    wide, state, lanes, prev = mk(False)

    def body(xs_ref, b_ref, c_ref, dt_ref, z_ref, al_ref, ds_ref, nw_ref, ys_ref, y_ref, sp_ref, st_ref, ybuf):
        @pl.when(pl.program_id(2) == 0)
        def _():
            st_ref[...] = jnp.zeros_like(st_ref)

        dt = dt_ref[...]
        _, acs, acs_t, causal, _ = _decays(dt, al_ref)
        last = acs[CHUNK - 1:CHUNK, :]
        bmat = b_ref[...].astype(BF16)
        cmat = c_ref[...].astype(BF16)
        cb = lax.dot_general(cmat, bmat, (((1,), (1,)), ((), ())), preferred_element_type=F32)
        for j in range(GROUP_SSM_HEADS):
            hs = slice(j * HEAD_DIM, (j + 1) * HEAD_DIM)
            colv = acs[:, j:j + 1]
            lmat = jnp.exp(jnp.where(causal, colv - acs_t[j:j + 1, :], -jnp.inf))
            xh = xs_ref[:, hs]
            xdt = xh * dt[:, j:j + 1]
            yd = jnp.dot((cb * lmat).astype(BF16), xdt.astype(BF16), preferred_element_type=F32)
            lastj = last[:, j:j + 1]
            dte = jnp.exp(lastj - colv)
            sts = lax.dot_general(bmat, (xdt * dte).astype(BF16), (((0,), (0,)), ((), ())),
                                  preferred_element_type=F32)
            s_prev = st_ref[j]
            s16 = s_prev.astype(BF16)
            sp_ref[j] = s16
            yo = jnp.dot(cmat, s16, preferred_element_type=F32) * jnp.exp(colv)
            ybuf[:, hs] = yd + yo + ds_ref[:, j:j + 1] * xh
            st_ref[j] = s_prev * jnp.exp(lastj) + sts
        y = ybuf[...]
        zz = z_ref[...]
        u = y * (zz * _sigmoid(zz))
        rn = lax.rsqrt(jnp.mean(u * u, -1, keepdims=True) + RMS_EPS)
        ys_ref[...] = (u * rn * nw_ref[...]).astype(BF16)
        y_ref[...] = y

    return _pcall(
        body, name="ssd_fwd", grid=(SSM_GROUPS, nb, nc),
        in_specs=[wide, state, state, lanes, wide, grow, grow, nwspec],
        out_specs=[wide, wide, prev],
        out_shape=[jax.ShapeDtypeStruct((nb, seq, D_INNER), BF16), jax.ShapeDtypeStruct((nb, seq, D_INNER), F32),
                   jax.ShapeDtypeStruct((nb, nc, SSM_GROUPS, GROUP_SSM_HEADS, D_STATE, HEAD_DIM), BF16)],
        scratch_shapes=[pltpu.VMEM((GROUP_SSM_HEADS, D_STATE, HEAD_DIM), F32), pltpu.VMEM((CHUNK, hw), F32)],
        compiler_params=_params("parallel", "parallel", "arbitrary"),
    )(xs, bm, cm, dtg, z, alog_g, dskip_g, normw)


def _ssd_bwd(xs, bm, cm, dtg, sgg, z, y, dys, sprev, alog_g, dskip_g, normw):
    nb, seq, _ = xs.shape
    nc, hw, mk, grow, nwspec = _ssd_specs(nb, seq)
    wide, state, lanes, prev = mk(True)
    nt = (((1,), (1,)), ((), ()))
    tn = (((0,), (0,)), ((), ()))

    def body(xs_ref, b_ref, c_ref, dt_ref, sg_ref, z_ref, y_ref, dys_ref, sp_ref, al_ref, ds_ref, nw_ref,
             dxs_ref, db_ref, dc_ref, ddt_ref, dz_ref, small_ref, dnw_ref, g_ref, dybuf, dxbuf):
        b, c = pl.program_id(1), pl.program_id(2)

        @pl.when((b == 0) & (c == 0))
        def _():
            small_ref[...] = jnp.zeros_like(small_ref)
            dnw_ref[...] = jnp.zeros_like(dnw_ref)

        @pl.when(c == 0)
        def _():
            g_ref[...] = jnp.zeros_like(g_ref)

        yv, zz, dys_v, nw = y_ref[...], z_ref[...], dys_ref[...], nw_ref[...]
        sz = _sigmoid(zz)
        silu = zz * sz
        u = yv * silu
        rn = lax.rsqrt(jnp.mean(u * u, -1, keepdims=True) + RMS_EPS)
        gn = dys_v * nw
        du = rn * gn - u * (rn * rn * rn) * jnp.mean(u * gn, -1, keepdims=True)
        dnw_ref[...] += jnp.sum(dys_v * u * rn, 0, keepdims=True)
        dybuf[...] = du * silu
        dz_ref[...] = du * yv * (sz * (1.0 + zz * (1.0 - sz)))

        dt = dt_ref[...]
        arow, acs, acs_t, causal, triu = _decays(dt, al_ref)
        last = acs[CHUNK - 1:CHUNK, :]
        bmat = b_ref[...].astype(BF16)
        cmat = c_ref[...].astype(BF16)
        cb = lax.dot_general(cmat, bmat, nt, preferred_element_type=F32)
        lane = lax.broadcasted_iota(jnp.int32, (CHUNK, LANE), 1)
        lane1 = lax.broadcasted_iota(jnp.int32, (1, LANE), 1)
        rowc = lax.broadcasted_iota(jnp.int32, (CHUNK, 1), 0)
        dacs = jnp.zeros((CHUNK, LANE), F32)
        ddt_x = jnp.zeros((CHUNK, LANE), F32)
        dcb = jnp.zeros((CHUNK, CHUNK), F32)
        dc_acc = jnp.zeros((CHUNK, D_STATE), F32)
        db_acc = jnp.zeros((CHUNK, D_STATE), F32)
        dsk_row = jnp.zeros((1, LANE), F32)
        for j in range(GROUP_SSM_HEADS):
            hs = slice(j * HEAD_DIM, (j + 1) * HEAD_DIM)
            colv = acs[:, j:j + 1]
            lmat = jnp.exp(jnp.where(causal, colv - acs_t[j:j + 1, :], -jnp.inf))
            xh = xs_ref[:, hs]
            dtc = dt[:, j:j + 1]
            xdt = xh * dtc
            xdt16 = xdt.astype(BF16)
            mf = cb * lmat
            dyj = dybuf[:, hs]
            dy16 = dyj.astype(BF16)
            dyd = dyj * jnp.exp(colv)
            dyd16 = dyd.astype(BF16)
            s16 = sp_ref[j]
            gj = g_ref[j]
            g16 = gj.astype(BF16)
            lastj = last[:, j:j + 1]
            dte = jnp.exp(lastj - colv)
            cd = jnp.exp(lastj)
            cs = jnp.dot(cmat, s16, preferred_element_type=F32)
            dc_acc = dc_acc + lax.dot_general(dyd16, s16, nt, preferred_element_type=F32)
            g_here = lax.dot_general(cmat, dyd16, tn, preferred_element_type=F32)
            bg = jnp.dot(bmat, g16, preferred_element_type=F32)
            dxdt = bg * dte
            ddte = jnp.sum(bg * xdt, -1, keepdims=True)
            db_acc = db_acc + lax.dot_general((xdt * dte).astype(BF16), g16, nt, preferred_element_type=F32)
            dcd = jnp.sum(gj * s16.astype(F32), keepdims=True)
            dm = lax.dot_general(dy16, xdt16, nt, preferred_element_type=F32)
            dxdt = dxdt + lax.dot_general(mf.astype(BF16), dy16, tn, preferred_element_type=F32)
            wmat = dm * mf
            dcb = dcb + dm * lmat
            dac = (jnp.sum(wmat, -1, keepdims=True) - jnp.sum(wmat.T, -1, keepdims=True)
                   + jnp.sum(dyd * cs, -1, keepdims=True) - ddte * dte)
            tail = jnp.sum(ddte * dte, keepdims=True) + dcd * cd
            dac = dac + jnp.where(rowc == CHUNK - 1, tail, 0.0)
            dacs = jnp.where(lane == j, dac, dacs)
            ddt_x = jnp.where(lane == j, jnp.sum(dxdt * xh, -1, keepdims=True), ddt_x)
            dxbuf[:, hs] = dxdt * dtc + ds_ref[:, j:j + 1] * dyj
            dsk_row = jnp.where(lane1 == j, jnp.sum(dyj * xh, keepdims=True), dsk_row)
            g_ref[j] = gj * cd + g_here
        da = jnp.dot(triu, dacs, precision=HIGHEST, preferred_element_type=F32)
        ddt_raw = (da * arow + ddt_x) * sg_ref[...]
        ddt_ref[...] = ddt_raw
        small_ref[0:1, :] += jnp.sum(da * dt, 0, keepdims=True) * arow
        small_ref[1:2, :] += dsk_row
        small_ref[2:3, :] += jnp.sum(ddt_raw, 0, keepdims=True)
        dcb16 = dcb.astype(BF16)
        dc_ref[...] = dc_acc + jnp.dot(dcb16, bmat, preferred_element_type=F32)
        db_ref[...] = db_acc + lax.dot_general(dcb16, cmat, tn, preferred_element_type=F32)
        dxs_ref[...] = dxbuf[...]

    return _pcall(
        body, name="ssd_bwd", grid=(SSM_GROUPS, nb, nc),
        in_specs=[wide, state, state, lanes, lanes, wide, wide, wide, prev, grow, grow, nwspec],
        out_specs=[wide, state, state, lanes, wide,
                   pl.BlockSpec((None, 8, LANE), lambda g, b, c: (g, 0, 0)), nwspec],
        out_shape=[jax.ShapeDtypeStruct((nb, seq, D_INNER), F32),
                   jax.ShapeDtypeStruct((nb, seq, SSM_GROUPS * D_STATE), F32),
                   jax.ShapeDtypeStruct((nb, seq, SSM_GROUPS * D_STATE), F32),
                   jax.ShapeDtypeStruct((SSM_GROUPS, nb, seq, LANE), F32),
                   jax.ShapeDtypeStruct((nb, seq, D_INNER), F32),
                   jax.ShapeDtypeStruct((SSM_GROUPS, 8, LANE), F32),
                   jax.ShapeDtypeStruct((1, D_INNER), F32)],
        scratch_shapes=[pltpu.VMEM((GROUP_SSM_HEADS, D_STATE, HEAD_DIM), F32),
                        pltpu.VMEM((CHUNK, hw), F32), pltpu.VMEM((CHUNK, hw), F32)],
        compiler_params=_params("parallel", "arbitrary", "arbitrary"),
    )(xs, bm, cm, dtg, sgg, z, y, dys, sprev, alog_g, dskip_g, normw)


EW_TM = 256


def _merge_fwd(y_a, y_b, gm, bgate):
    nb, seq, _ = y_a.shape

    def body(a_ref, b_ref, ga_ref, gb_ref, bg_ref, o_ref):
        sa = _sigmoid(ga_ref[...] + bg_ref[0:1, :])
        sb = _sigmoid(gb_ref[...] + bg_ref[1:2, :])
        o_ref[...] = (sa * a_ref[...] + sb * b_ref[...]).astype(BF16)

    spec = pl.BlockSpec((None, EW_TM, D_MODEL), lambda b, i: (b, i, 0))
    spec1 = pl.BlockSpec((None, EW_TM, D_MODEL), lambda b, i: (b, i, 1))
    return _pcall(
        body, name="merge_fwd", grid=(nb, seq // EW_TM),
        in_specs=[spec, spec, spec, spec1, pl.BlockSpec((8, D_MODEL), lambda b, i: (0, 0))], out_specs=spec,
        out_shape=jax.ShapeDtypeStruct((nb, seq, D_MODEL), BF16),
        compiler_params=_params("parallel", "parallel"),
    )(y_a, y_b, gm, gm, bgate)


def _merge_bwd(dmerged, y_a, y_b, gm, bgate):
    nb, seq, _ = y_a.shape

    def body(dm_ref, a_ref, b_ref, ga_ref, gb_ref, bg_ref, dya_ref, dyb_ref, dg_ref, s_ref):
        @pl.when((pl.program_id(0) == 0) & (pl.program_id(1) == 0))
        def _():
            s_ref[...] = jnp.zeros_like(s_ref)

        dm = dm_ref[...]
        sa = _sigmoid(ga_ref[...] + bg_ref[0:1, :])
        sb = _sigmoid(gb_ref[...] + bg_ref[1:2, :])
        dya_ref[...] = (dm * sa).astype(BF16)
        dyb_ref[...] = (dm * sb).astype(BF16)
        dga = dm * a_ref[...] * (sa * (1.0 - sa))
        dgb = dm * b_ref[...] * (sb * (1.0 - sb))
        dg_ref[:, :D_MODEL] = dga.astype(BF16)
        dg_ref[:, D_MODEL:] = dgb.astype(BF16)
        s_ref[0:1, :] += jnp.sum(dga, 0, keepdims=True)
        s_ref[1:2, :] += jnp.sum(dgb, 0, keepdims=True)

    spec = pl.BlockSpec((None, EW_TM, D_MODEL), lambda b, i: (b, i, 0))
    spec1 = pl.BlockSpec((None, EW_TM, D_MODEL), lambda b, i: (b, i, 1))
    small = pl.BlockSpec((8, D_MODEL), lambda b, i: (0, 0))
    dya, dyb, dgm, sums = _pcall(
        body, name="merge_bwd", grid=(nb, seq // EW_TM),
        in_specs=[spec, spec, spec, spec, spec1, small],
        out_specs=[spec, spec, pl.BlockSpec((None, EW_TM, 2 * D_MODEL), lambda b, i: (b, i, 0)), small],
        out_shape=[jax.ShapeDtypeStruct((nb, seq, D_MODEL), BF16), jax.ShapeDtypeStruct((nb, seq, D_MODEL), BF16),
                   jax.ShapeDtypeStruct((nb, seq, 2 * D_MODEL), BF16), jax.ShapeDtypeStruct((8, D_MODEL), F32)],
        compiler_params=_params("arbitrary", "arbitrary"),
    )(dmerged, y_a, y_b, gm, gm, bgate)
    r4 = lambda t: t.reshape(nb, 1, seq, t.shape[-1])
    return r4(dya), r4(dyb), r4(dgm), sums


def _ln_loss(x, mix, gp, pw, target, bgate, ln_g, ln_b):
    nb, seq, _ = x.shape

    def body(x_ref, mix_ref, gp_ref, pw_ref, t_ref, bg_ref, g_ref, b_ref, dx_ref, dp_ref, dpw_ref, dgp_ref, s_ref):
        @pl.when((pl.program_id(0) == 0) & (pl.program_id(1) == 0))
        def _():
            s_ref[...] = jnp.zeros_like(s_ref)

        sp = _sigmoid(gp_ref[...] + bg_ref[2:3, :])
        pw = pw_ref[...]
        pre = ALPHA * x_ref[...] + mix_ref[...] + sp * pw
        mu = jnp.mean(pre, -1, keepdims=True)
        cen = pre - mu
        rstd = lax.rsqrt(jnp.mean(cen * cen, -1, keepdims=True) + LN_EPS)
        xhat = cen * rstd
        err = xhat * g_ref[...] + b_ref[...] - t_ref[...]
        dy = err * (1.0 / D_MODEL)
        dxh = dy * g_ref[...]
        dpre = rstd * (dxh - jnp.mean(dxh, -1, keepdims=True) - xhat * jnp.mean(dxh * xhat, -1, keepdims=True))
        dx_ref[...] = ALPHA * dpre
        dp_ref[...] = dpre.astype(BF16)
        dpw_ref[...] = (dpre * sp).astype(BF16)
        dgp = dpre * pw * (sp * (1.0 - sp))
        dgp_ref[...] = dgp.astype(BF16)
        s_ref[0:1, :] += jnp.sum(dy * xhat, 0, keepdims=True)
        s_ref[1:2, :] += jnp.sum(dy, 0, keepdims=True)
        s_ref[2:3, :] += jnp.sum(dgp, 0, keepdims=True)
        s_ref[3:4, :] += jnp.sum(err * err, 0, keepdims=True)

    spec = pl.BlockSpec((None, EW_TM, D_MODEL), lambda b, i: (b, i, 0))
    small = pl.BlockSpec((8, D_MODEL), lambda b, i: (0, 0))
    row = pl.BlockSpec((1, D_MODEL), lambda b, i: (0, 0))
    dx0, dpre16, dpw16, dgp16, sums = _pcall(
        body, name="ln_loss", grid=(nb, seq // EW_TM),
        in_specs=[spec] * 5 + [small, row, row], out_specs=[spec] * 4 + [small],
        out_shape=[jax.ShapeDtypeStruct((nb, seq, D_MODEL), F32)] + [jax.ShapeDtypeStruct((nb, seq, D_MODEL), BF16)] * 3
        + [jax.ShapeDtypeStruct((8, D_MODEL), F32)],
        compiler_params=_params("arbitrary", "arbitrary"),
    )(x, mix, gp, pw, target, bgate, ln_g, ln_b)
    r4 = lambda t: t.reshape(nb, 1, seq, D_MODEL)
    return dx0, r4(dpre16), r4(dpw16), r4(dgp16), sums


def _adamw(w, g, m, v, name):
    rows, cols = w.shape
    tr = rows
    for cand in range(8, rows, 8):
        if rows % cand == 0 and cand * cols * 4 <= (1 << 20):
            tr = cand
    if rows * cols * 4 <= (1 << 20):
        tr = rows
    c1 = 1.0 - ADAM_B1 ** ADAM_STEP
    c2 = 1.0 - ADAM_B2 ** ADAM_STEP

    def body(w_ref, g_ref, m_ref, v_ref, d_ref, nm_ref, nv_ref):
        gv = g_ref[...]
        nm = ADAM_B1 * m_ref[...] + (1.0 - ADAM_B1) * gv
        nv = ADAM_B2 * v_ref[...] + (1.0 - ADAM_B2) * (gv * gv)
        d_ref[...] = -ADAM_LR * ((nm / c1) / (jnp.sqrt(nv / c2) + ADAM_EPS) + ADAM_WD * w_ref[...])
        nm_ref[...] = nm
        nv_ref[...] = nv

    spec = pl.BlockSpec((tr, cols), lambda i: (i, 0))
    return _pcall(
        body, name=name, grid=(rows // tr,), in_specs=[spec] * 4, out_specs=[spec] * 3,
        out_shape=[jax.ShapeDtypeStruct(w.shape, F32)] * 3, compiler_params=_params("parallel"),
    )(w, g, m, v)


def _sum_rows(parts, out_dtype, name):
    rows, cols = parts[0].shape
    tr = rows
    for cand in range(16, rows, 16):
        if rows % cand == 0 and cand * cols * 4 <= (1 << 20):
            tr = cand
    n = len(parts)

    def body(*refs):
        acc = refs[0][...].astype(F32)
        for r in refs[1:n]:
            acc = acc + r[...].astype(F32)
        refs[n][...] = acc.astype(out_dtype)

    spec = pl.BlockSpec((tr, cols), lambda i: (i, 0))
    return _pcall(
        body, name=name, grid=(rows // tr,), in_specs=[spec] * n, out_specs=spec,
        out_shape=jax.ShapeDtypeStruct((rows, cols), out_dtype), compiler_params=_params("parallel"),
    )(*parts)


def _place():
    return lax.axis_index("x"), lax.axis_index("y"), lax.axis_index("c")


def _other_chips(x, y):
    return [(1 - x, y), (x, 1 - y), (1 - x, 1 - y)]


def _remote(src, dst, send_sem, recv_sem, to):
    return pltpu.make_async_remote_copy(src_ref=src, dst_ref=dst, send_sem=send_sem, recv_sem=recv_sem,
                                        device_id=to, device_id_type=MESH)


ANY = pl.BlockSpec(memory_space=pl.ANY)


def _allgather_pieces(pieces):
    n = len(pieces)

    def body(*refs):
        ins, outs = refs[:n], refs[n:2 * n]
        send_sems, recv_sems, local_sems = refs[2 * n:]
        x, y, c = _place()
        me = 2 * x + y
        sibling = (x, y, 1 - c)
        chips = _other_chips(x, y)
        started = []
        locals_ = []
        for a in range(n):
            half = ins[a].shape[0] // 2
            loc = pltpu.make_async_copy(ins[a], outs[a].at[me], local_sems.at[a])
            loc.start()
            locals_.append(loc)
            for j, (cx, cy) in enumerate(chips):
                cp = _remote(ins[a].at[pl.ds(c * half, half)], outs[a].at[me, pl.ds(c * half, half)],
                             send_sems.at[6 * a + j], recv_sems.at[6 * a + j], (cx, cy, c))
                cp.start()
                started.append(cp)
        for a in range(n):
            half = ins[a].shape[0] // 2
            for j, (cx, cy) in enumerate(chips):
                blk = outs[a].at[2 * cx + cy, pl.ds(c * half, half)]
                _remote(blk, blk, send_sems.at[6 * a + j], recv_sems.at[6 * a + j], (cx, cy, c)).wait_recv()
                fw = _remote(blk, blk, send_sems.at[6 * a + 3 + j], recv_sems.at[6 * a + 3 + j], sibling)
                fw.start()
                started.append(fw)
        for a in range(n):
            half = ins[a].shape[0] // 2
            for j, (cx, cy) in enumerate(chips):
                blk = outs[a].at[2 * cx + cy, pl.ds((1 - c) * half, half)]
                _remote(blk, blk, send_sems.at[6 * a + 3 + j], recv_sems.at[6 * a + 3 + j], sibling).wait_recv()
        for cp in started:
            cp.wait_send()
        for loc in locals_:
            loc.wait()

    return _pcall(
        body, name="allgather_weights", in_specs=[ANY] * n, out_specs=[ANY] * n,
        out_shape=[jax.ShapeDtypeStruct((4,) + p.shape, p.dtype) for p in pieces],
        scratch_shapes=[pltpu.SemaphoreType.DMA((6 * n,)), pltpu.SemaphoreType.DMA((6 * n,)),
                        pltpu.SemaphoreType.DMA((n,))],
        compiler_params=pltpu.CompilerParams(has_side_effects=True),
    )(*pieces)


def _sibling_exchange(grads):
    n = len(grads)

    def body(*refs):
        ins, keeps, gots = refs[:n], refs[n:2 * n], refs[2 * n:3 * n]
        send_sems, recv_sems, local_sems = refs[3 * n:]
        x, y, c = _place()
        sibling = (x, y, 1 - c)
        work = []
        for a in range(n):
            half = ins[a].shape[1] // 2
            loc = pltpu.make_async_copy(ins[a].at[:, pl.ds(c * half, half)], keeps[a], local_sems.at[a])
            loc.start()
            cp = _remote(ins[a].at[:, pl.ds((1 - c) * half, half)], gots[a], send_sems.at[a], recv_sems.at[a], sibling)
            cp.start()
            work.append((loc, cp))
        for loc, cp in work:
            cp.wait()
            loc.wait()

    halves = [jax.ShapeDtypeStruct((4, g.shape[1] // 2, g.shape[2]), g.dtype) for g in grads]
    outs = _pcall(
        body, name="grad_sibling_exchange", in_specs=[ANY] * n, out_specs=[ANY] * (2 * n),
        out_shape=halves + halves,
        scratch_shapes=[pltpu.SemaphoreType.DMA((n,)), pltpu.SemaphoreType.DMA((n,)), pltpu.SemaphoreType.DMA((n,))],
        compiler_params=pltpu.CompilerParams(has_side_effects=True),
    )(*grads)
    return outs[:n], outs[n:]


def _chip_scatter(sums):
    n = len(sums)

    def body(*refs):
        ins, owns, gots = refs[:n], refs[n:2 * n], refs[2 * n:3 * n]
        send_sems, recv_sems, local_sems = refs[3 * n:]
        x, y, c = _place()
        me = 2 * x + y
        chips = _other_chips(x, y)
        work = []
        for a in range(n):
            loc = pltpu.make_async_copy(ins[a].at[me], owns[a], local_sems.at[a])
            loc.start()
            work.append(loc)
            for j, (cx, cy) in enumerate(chips):
                cp = _remote(ins[a].at[2 * cx + cy], gots[a].at[j], send_sems.at[3 * a + j], recv_sems.at[3 * a + j],
                             (cx, cy, c))
                cp.start()
                work.append(cp)
        for w in work:
            w.wait()

    outs = _pcall(
        body, name="grad_chip_scatter", in_specs=[ANY] * n, out_specs=[ANY] * (2 * n),
        out_shape=[jax.ShapeDtypeStruct(s.shape[1:], s.dtype) for s in sums]
        + [jax.ShapeDtypeStruct((3,) + s.shape[1:], s.dtype) for s in sums],
        scratch_shapes=[pltpu.SemaphoreType.DMA((3 * n,)), pltpu.SemaphoreType.DMA((3 * n,)),
                        pltpu.SemaphoreType.DMA((n,))],
        compiler_params=pltpu.CompilerParams(has_side_effects=True),
    )(*sums)
    return outs[:n], outs[n:]


def _sibling_gather(halves):
    n = len(halves)

    def body(*refs):
        ins, outs = refs[:n], refs[n:2 * n]
        send_sems, recv_sems, local_sems = refs[2 * n:]
        x, y, c = _place()
        sibling = (x, y, 1 - c)
        work = []
        for a in range(n):
            h = ins[a].shape[0]
            loc = pltpu.make_async_copy(ins[a], outs[a].at[pl.ds(c * h, h)], local_sems.at[a])
            loc.start()
            cp = _remote(ins[a], outs[a].at[pl.ds(c * h, h)], send_sems.at[a], recv_sems.at[a], sibling)
            cp.start()
            work.append((loc, cp))
        for a, (loc, cp) in enumerate(work):
            h = ins[a].shape[0]
            cp.wait_send()
            theirs = outs[a].at[pl.ds((1 - c) * h, h)]
            _remote(theirs, theirs, send_sems.at[a], recv_sems.at[a], sibling).wait_recv()
            loc.wait()

    return _pcall(
        body, name="grad_sibling_gather", in_specs=[ANY] * n, out_specs=[ANY] * n,
        out_shape=[jax.ShapeDtypeStruct((2 * h.shape[0], h.shape[1]), h.dtype) for h in halves],
        scratch_shapes=[pltpu.SemaphoreType.DMA((n,)), pltpu.SemaphoreType.DMA((n,)), pltpu.SemaphoreType.DMA((n,))],
        compiler_params=pltpu.CompilerParams(has_side_effects=True),
    )(*halves)


def _allgather8(buf, name):
    rows = buf.shape[0]

    def body(in_ref, out_ref, send_sems, recv_sems):
        x, y, c = _place()
        me = 4 * x + 2 * y + c
        out_ref[me] = in_ref[...]
        work = []
        for rel in range(1, 8):
            fx, fy, fc = (rel >> 2) & 1, (rel >> 1) & 1, rel & 1
            to = (x ^ fx, y ^ fy, c ^ fc)
            cp = _remote(in_ref, out_ref.at[me], send_sems.at[rel - 1], recv_sems.at[rel - 1], to)
            cp.start()
            work.append((cp, 4 * to[0] + 2 * to[1] + to[2]))
        for rel, (cp, frm) in enumerate(work):
            cp.wait_send()
            blk = out_ref.at[frm]
            _remote(blk, blk, send_sems.at[rel], recv_sems.at[rel], (x, y, c)).wait_recv()

    return _pcall(
        body, name=name, in_specs=[pl.BlockSpec(memory_space=pltpu.VMEM)],
        out_specs=pl.BlockSpec(memory_space=pltpu.VMEM),
        out_shape=jax.ShapeDtypeStruct((8, rows, LANE), F32),
        scratch_shapes=[pltpu.SemaphoreType.DMA((7,)), pltpu.SemaphoreType.DMA((7,))],
        compiler_params=pltpu.CompilerParams(has_side_effects=True),
    )(buf)


def _reduce_scatter(grads):
    keep, got = _sibling_exchange(grads)
    flat = lambda t: t.reshape(-1, t.shape[-1])
    chip_sums = [_sum_rows([flat(k), flat(g)], BF16, "grad_pair_sum_%d" % i).reshape(k.shape)
                 for i, (k, g) in enumerate(zip(keep, got))]
    own, others = _chip_scatter(chip_sums)
    halves = [_sum_rows([o, t[0], t[1], t[2]], F32, "grad_chip_sum_%d" % i) for i, (o, t) in enumerate(zip(own, others))]
    return _sibling_gather(halves)


def _pack_rows(arrs):
    parts = []
    for a in arrs:
        f = a.reshape(-1).astype(F32)
        parts.append(jnp.pad(f, (0, (-f.shape[0]) % LANE)))
    flat = jnp.concatenate(parts)
    rows = -(-flat.shape[0] // LANE)
    rows8 = -(-rows // 8) * 8
    return jnp.pad(flat, (0, rows8 * LANE - flat.shape[0])).reshape(rows8, LANE)


def _unpack_rows(buf, shapes):
    flat = buf.reshape(-1)
    outs, off = [], 0
    for s in shapes:
        n = int(np.prod(s))
        outs.append(flat[off:off + n].reshape(s))
        off += -(-n // LANE) * LANE
    return outs


def _local_grads(x, p, target, wseg, w_br16, w_out16, w_ple16, b_gate, conv_w, conv_b, dt_bias, a_log, d_skip,
                 ssm_norm_w, ln_g, ln_b, rel_bias):
    nb, seq, _ = x.shape
    bmaps = jnp.asarray(_bucket_maps())
    bias = _bias_tables(rel_bias, bmaps)
    bgate8 = jnp.pad(b_gate, ((0, 5), (0, 0)))
    dils = [d for _, d in PATTERNS]

    qkv = [_proj(x, wseg["qkv%d" % g], dils[g], BF16, "proj_qkv%d" % g) for g in range(3)]
    nat = {s: _proj(x, wseg[s], 1, F32, "proj_" + s).reshape(nb, seq, -1)
           for s in ("gatt", "z", "xs", "bm", "cm", "dt", "gm", "gp")}
    att = [_attn_fwd(qkv[g], bias[g * GROUP_HEADS:(g + 1) * GROUP_HEADS], dils[g], "attn_fwd%d" % g) for g in range(3)]
    oa, o_att, lse = _combine_fwd([a[0] for a in att], [a[1] for a in att], nat["gatt"])

    cw = {"xs": (conv_w[:, :D_INNER], conv_b[:, :D_INNER]),
          "bm": (conv_w[:, D_INNER:D_INNER + 512], conv_b[:, D_INNER:D_INNER + 512]),
          "cm": (conv_w[:, D_INNER + 512:], conv_b[:, D_INNER + 512:])}
    act = {s: _conv_fwd(nat[s], cw[s][0], cw[s][1], "conv_fwd_" + s) for s in ("xs", "bm", "cm")}
    dt_sp, dt_sg = _softplus_sig(nat["dt"], jnp.pad(dt_bias, ((0, 0), (0, LANE - SSM_HEADS))))
    dtg, sgg = _group_lanes(dt_sp), _group_lanes(dt_sg)
    alog_g, dskip_g = _group_lanes(a_log), _group_lanes(d_skip)
    y_ssm, y_all, sprev = _ssd_fwd(act["xs"], act["bm"], act["cm"], dtg, nat["z"], alog_g, dskip_g, ssm_norm_w)

    w_bra, w_brb = w_br16[:ATT_OUT], w_br16[ATT_OUT:]
    y_a = _proj(oa, w_bra, 1, F32, "proj_ya").reshape(nb, seq, D_MODEL)
    y_b = _proj(y_ssm, w_brb, 1, F32, "proj_yb").reshape(nb, seq, D_MODEL)
    merged = _merge_fwd(y_a, y_b, nat["gm"], bgate8)
    mix = _proj(merged, w_out16, 1, F32, "proj_mix").reshape(nb, seq, D_MODEL)
    pw = _proj(p, w_ple16, 1, F32, "proj_ple").reshape(nb, seq, D_MODEL)

    dx, dpre16, dpw16, dgp16, ln_sums = _ln_loss(x, mix, nat["gp"], pw, target, bgate8, ln_g, ln_b)
    loss_sum = (0.5 / D_MODEL) * jnp.sum(ln_sums[3])
    dmerged = _dx(dpre16, w_out16, None, 1, "dx_merged")
    dya16, dyb16, dgm16, mg_sums = _merge_bwd(dmerged, y_a, y_b, nat["gm"], bgate8)
    doa = _dx(dya16, w_bra, None, 1, "dx_oa")
    dys = _dx(dyb16, w_brb, None, 1, "dx_yssm")
    g_w_out = _dw(merged, dpre16, 1, BF16, "dw_out")
    g_w_br = jnp.concatenate([_dw(oa, dya16, 1, BF16, "dw_bra"), _dw(y_ssm, dyb16, 1, BF16, "dw_brb")], axis=0)
    g_w_ple = _dw(p, dpw16, 1, BF16, "dw_ple")

    do_att, dgatt16 = _combine_bwd(doa, nat["gatt"], o_att)
    dseg = {"gatt": dgatt16.reshape(nb, 1, seq, ATT_OUT), "gm": dgm16, "gp": dgp16}
    dbias = []
    for g in range(3):
        dqkv, db = _attn_bwd(qkv[g], bias[g * GROUP_HEADS:(g + 1) * GROUP_HEADS], do_att, o_att, lse, dils[g],
                             "attn_bwd%d" % g)
        dseg["qkv%d" % g] = dqkv
        dbias.append(db)
    g_rel = _bias_grad(jnp.concatenate(dbias, axis=0), bmaps)[:, 0, :NUM_BUCKETS].T

    dxs, dbm, dcm, ddtg, dz, ssd_small, g_normw = _ssd_bwd(
        act["xs"], act["bm"], act["cm"], dtg, sgg, nat["z"], y_all, dys, sprev, alog_g, dskip_g, ssm_norm_w)
    dseg["z"] = dz.reshape(nb, 1, seq, D_INNER)
    ddt = _ungroup_lanes(ddtg)
    dseg["dt"] = jnp.pad(ddt, ((0, 0), (0, 0), (0, LANE - SSM_HEADS))).reshape(nb, 1, seq, LANE)
    conv_sums = {}
    for s, dact in (("xs", dxs), ("bm", dbm), ("cm", dcm)):
        dpre, conv_sums[s] = _conv_bwd_pre(dact, nat[s], cw[s][0], cw[s][1], "conv_bwd_" + s)
        dseg[s] = _conv_bwd_x(dpre, cw[s][0], "conv_bwd_x_" + s)
    csum = jnp.concatenate([conv_sums["xs"], conv_sums["bm"], conv_sums["cm"]], axis=1)

    dwseg = {}
    for name in ["qkv0", "qkv1", "qkv2", "gatt", "z", "xs", "bm", "cm", "dt", "gm", "gp"]:
        d = dils[int(name[3])] if name.startswith("qkv") else 1
        dx = _dx(dseg[name], wseg[name], dx, d, "dx_" + name)
        dwseg[name] = _dw(x, dseg[name], d, BF16, "dw_" + name)

    small = dict(
        b_gate=jnp.stack([mg_sums[0], mg_sums[1], ln_sums[2]]),
        conv_w=csum[0:4], conv_b=csum[4:5],
        dt_bias=_ungroup_lanes(ssd_small[:, 2:3, :]), a_log=_ungroup_lanes(ssd_small[:, 0:1, :]),
        d_skip=_ungroup_lanes(ssd_small[:, 1:2, :]), ssm_norm_w=g_normw,
        ln_g=ln_sums[0:1], ln_b=ln_sums[1:2], rel_bias=g_rel)
    return loss_sum, dx, dwseg, g_w_br, g_w_out, g_w_ple, small


SMALL_ORDER = ("b_gate", "conv_w", "conv_b", "dt_bias", "a_log", "d_skip", "ssm_norm_w", "ln_g", "ln_b", "rel_bias")
SMALL_FULL_SHAPES = dict(b_gate=(3, 1024), conv_w=(4, 3072), conv_b=(1, 3072), dt_bias=(1, 32), a_log=(1, 32),
                         d_skip=(1, 32), ssm_norm_w=(1, 2048), ln_g=(1, 1024), ln_b=(1, 1024), rel_bias=(32, 36))


def kernel(x, p, w_in, b_gate, conv_w, conv_b, dt_bias, a_log, d_skip, ssm_norm_w, w_branch, w_out, w_ple, ln_g, ln_b, rel_bias, loss_target, m_w_in, m_b_gate, m_conv_w, m_conv_b, m_dt_bias, m_a_log, m_d_skip, m_ssm_norm_w, m_w_branch, m_w_out, m_w_ple, m_ln_g, m_ln_b, m_rel_bias, v_w_in, v_b_gate, v_conv_w, v_conv_b, v_dt_bias, v_a_log, v_d_skip, v_ssm_norm_w, v_w_branch, v_w_out, v_w_ple, v_ln_g, v_ln_b, v_rel_bias):
    cx, cy, cc = _place()
    chip = 2 * cx + cy
    dev = 4 * cx + 2 * cy + cc

    win16 = _shard_to_window(w_in[0].astype(BF16), chip)
    g_win, g_br, g_out, g_ple = _allgather_pieces(
        [win16, w_branch[0].astype(BF16), w_out[0].astype(BF16), w_ple[0].astype(BF16)])
    wseg = _assemble(g_win)
    w_br16 = g_br.reshape(4 * 704, D_MODEL)
    w_out16 = g_out.reshape(D_MODEL, D_MODEL)
    w_ple16 = jnp.transpose(g_ple, (1, 0, 2)).reshape(PLE_DIM, D_MODEL)
    shards = _allgather8(_pack_rows([b_gate[0], conv_w[0]]), "allgather_small_params")
    per_chip = [_unpack_rows(shards[2 * k], [(3, 256), (4, 768)]) for k in range(4)]
    b_gate_full = jnp.concatenate([pc[0] for pc in per_chip], axis=1)
    conv_w_full = jnp.concatenate([pc[1] for pc in per_chip], axis=1)

    loss_sum, grad_x, dwseg, g_br, g_out, g_ple, small = _local_grads(
        x, p[0], loss_target, wseg, w_br16, w_out16, w_ple16, b_gate_full, conv_w_full, conv_b, dt_bias, a_log,
        d_skip, ssm_norm_w, ln_g, ln_b, rel_bias)
    loss = lax.psum(loss_sum, ("x", "y", "c"))

    big = _reduce_scatter([
        _pack(dwseg), g_br.reshape(4, 704, D_MODEL), g_out.reshape(4, 256, D_MODEL),
        jnp.transpose(g_ple.reshape(PLE_DIM, 4, 256), (1, 0, 2))])
    g_w_in = _window_to_shard(big[0], chip)
    g_w_branch, g_w_out, g_w_ple = big[1], big[2], big[3]
    parts = _allgather8(_pack_rows([small[n] for n in SMALL_ORDER]), "allgather_small_grads")
    small_sum = _sum_rows([parts[i] for i in range(8)], F32, "small_grad_sum")
    sg = dict(zip(SMALL_ORDER, _unpack_rows(small_sum, [SMALL_FULL_SHAPES[n] for n in SMALL_ORDER])))
    sg["b_gate"] = lax.dynamic_slice_in_dim(sg["b_gate"], chip * 256, 256, axis=1)
    sg["conv_w"] = lax.dynamic_slice_in_dim(sg["conv_w"], chip * 768, 768, axis=1)
    del dev

    upd = {}
    upd["w_in"] = _adamw(w_in[0], g_w_in, m_w_in[0], v_w_in[0], "adamw_w_in")
    upd["w_branch"] = _adamw(w_branch[0], g_w_branch, m_w_branch[0], v_w_branch[0], "adamw_w_branch")
    upd["w_out"] = _adamw(w_out[0], g_w_out, m_w_out[0], v_w_out[0], "adamw_w_out")
    upd["w_ple"] = _adamw(w_ple[0], g_w_ple, m_w_ple[0], v_w_ple[0], "adamw_w_ple")
    small_w = dict(b_gate=b_gate, conv_w=conv_w, conv_b=conv_b, dt_bias=dt_bias, a_log=a_log, d_skip=d_skip,
                   ssm_norm_w=ssm_norm_w, ln_g=ln_g, ln_b=ln_b, rel_bias=rel_bias)
    small_m = dict(b_gate=m_b_gate, conv_w=m_conv_w, conv_b=m_conv_b, dt_bias=m_dt_bias, a_log=m_a_log,
                   d_skip=m_d_skip, ssm_norm_w=m_ssm_norm_w, ln_g=m_ln_g, ln_b=m_ln_b, rel_bias=m_rel_bias)
    small_v = dict(b_gate=v_b_gate, conv_w=v_conv_w, conv_b=v_conv_b, dt_bias=v_dt_bias, a_log=v_a_log,
                   d_skip=v_d_skip, ssm_norm_w=v_ssm_norm_w, ln_g=v_ln_g, ln_b=v_ln_b, rel_bias=v_rel_bias)
    shapes = [small_w[n].shape for n in SMALL_ORDER]
    s_delta, s_m, s_v = _adamw(_pack_rows([small_w[n] for n in SMALL_ORDER]), _pack_rows([sg[n] for n in SMALL_ORDER]),
                               _pack_rows([small_m[n] for n in SMALL_ORDER]), _pack_rows([small_v[n] for n in SMALL_ORDER]),
                               "adamw_small")
    for i, n in enumerate(SMALL_ORDER):
        upd[n] = tuple(_unpack_rows(t, shapes)[i] for t in (s_delta, s_m, s_v))
        sg[n] = sg[n].reshape(small_w[n].shape)

    order = ("w_in", "b_gate", "conv_w", "conv_b", "dt_bias", "a_log", "d_skip", "ssm_norm_w", "w_branch", "w_out",
             "w_ple", "ln_g", "ln_b", "rel_bias")
    grads = dict(sg, w_in=g_w_in[None], w_branch=g_w_branch[None], w_out=g_w_out[None], w_ple=g_w_ple[None])
    lead = lambda n, t: t[None] if n in ("w_in", "w_branch", "w_out", "w_ple") else t
    return (loss, grad_x, *[grads[n] for n in order], *[lead(n, upd[n][0]) for n in order],
            *[lead(n, upd[n][1]) for n in order], *[lead(n, upd[n][2]) for n in order])
```

```python
import functools
import math

import numpy as np
import jax
import jax.numpy as jnp
from jax import lax
from jax.experimental import pallas as pl
from jax.experimental.pallas import tpu as pltpu

F32, BF16 = jnp.float32, jnp.bfloat16
HIGHEST = lax.Precision.HIGHEST

D_MODEL = 1024
HEAD_DIM = 64
GROUP_HEADS = 12
ATT_OUT = GROUP_HEADS * HEAD_DIM
PATTERNS = ((128, 1), (512, 4), (2048, 16))
BAND = 128
NUM_BUCKETS = 32
MAX_DISTANCE = 2048
D_INNER = 2048
SSM_HEADS = 32
SSM_GROUPS = 4
GROUP_SSM_HEADS = SSM_HEADS // SSM_GROUPS
D_STATE = 128
CHUNK = 128
PLE_DIM = 256
ALPHA = 2.0 ** 0.25
LN_EPS = 1e-5
RMS_EPS = 1e-5
ADAM_LR, ADAM_B1, ADAM_B2, ADAM_EPS, ADAM_WD, ADAM_STEP = 0.001, 0.9, 0.999, 1e-08, 0.01, 10
NEG = -1e30

QKV_W = 3 * ATT_OUT
IN_COLS = 15904
SHARD_COLS = IN_COLS // 4
DT_COL = 12800
DT_PAD = 96
INT_COLS = IN_COLS + DT_PAD
WIN_STRIDE = 31 * 128
WIN_COLS = 4096
SHARD3_SPLIT = DT_COL + 32 - 3 * SHARD_COLS

VMEM_LIMIT_BYTES = 56 * 1024 * 1024
LANE = 128
MESH = pl.DeviceIdType.MESH


def _pcall(body, **kw):
    return pl.pallas_call(body, **kw)


def _params(*sem):
    return pltpu.CompilerParams(dimension_semantics=sem, vmem_limit_bytes=VMEM_LIMIT_BYTES)


def _sigmoid(v):
    return jax.nn.sigmoid(v)


def _pick_tn(n):
    for t in (1024, 768, 512, 256, 128):
        if n % t == 0:
            return t
    raise ValueError(n)


def _token_tiling(seq, d):
    sub = seq // d
    tm = min(512, sub)
    res = max(1, min(d, 512 // sub))
    return sub, tm, res


def _proj(a3, w, d, out_dtype, name):
    nb, seq, kdim = a3.shape
    n = w.shape[1]
    sub, tm, res = _token_tiling(seq, d)
    tn = _pick_tn(n)

    def body(a_ref, w_ref, o_ref):
        for r in range(res):
            a = a_ref[:, r * kdim:(r + 1) * kdim].astype(BF16)
            o_ref[r] = jnp.dot(a, w_ref[...], preferred_element_type=F32).astype(out_dtype)

    return _pcall(
        body, name=name, grid=(n // tn, nb, d // res, sub // tm),
        in_specs=[pl.BlockSpec((None, tm, res * kdim), lambda j, b, rb, i: (b, i, rb)),
                  pl.BlockSpec((kdim, tn), lambda j, b, rb, i: (0, j))],
        out_specs=pl.BlockSpec((None, res, tm, tn), lambda j, b, rb, i: (b, rb, i, j)),
        out_shape=jax.ShapeDtypeStruct((nb, d, sub, n), out_dtype),
        compiler_params=_params("parallel", "parallel", "parallel", "parallel"),
    )(a3.reshape(nb, sub, d * kdim), w)


def _dx(dh4, w, acc3, d, name):
    nb, _, sub, wd = dh4.shape
    kout = w.shape[0]
    seq = sub * d
    _, tm, res = _token_tiling(seq, d)
    has_acc = acc3 is not None

    def body(*refs):
        dh_ref, w_ref = refs[0], refs[1]
        o_ref = refs[-1]
        for r in range(res):
            v = lax.dot_general(dh_ref[r].astype(BF16), w_ref[...], (((1,), (1,)), ((), ())),
                                preferred_element_type=F32)
            if has_acc:
                v = v + refs[2][:, r * kout:(r + 1) * kout]
            o_ref[:, r * kout:(r + 1) * kout] = v

    tok_spec = pl.BlockSpec((None, tm, res * kout), lambda b, rb, i: (b, i, rb))
    in_specs = [pl.BlockSpec((None, res, tm, wd), lambda b, rb, i: (b, rb, i, 0)),
                pl.BlockSpec((kout, wd), lambda b, rb, i: (0, 0))]
    args = [dh4, w]
    if has_acc:
        in_specs.append(tok_spec)
        args.append(acc3.reshape(nb, sub, d * kout))
    out = _pcall(
        body, name=name, grid=(nb, d // res, sub // tm), in_specs=in_specs, out_specs=tok_spec,
        out_shape=jax.ShapeDtypeStruct((nb, sub, d * kout), F32),
        input_output_aliases={2: 0} if has_acc else {},
        compiler_params=_params("parallel", "parallel", "parallel"),
    )(*args)
    return out.reshape(nb, seq, kout)


def _dw(a3, dh4, d, out_dtype, name):
    nb, seq, kdim = a3.shape
    n = dh4.shape[-1]
    sub, tm, res = _token_tiling(seq, d)
    tn = _pick_tn(n)
    grid = (n // tn, nb, d // res, sub // tm)

    def body(a_ref, dh_ref, o_ref, acc_ref):
        b, rb, i = pl.program_id(1), pl.program_id(2), pl.program_id(3)

        @pl.when((b == 0) & (rb == 0) & (i == 0))
        def _():
            acc_ref[...] = jnp.zeros_like(acc_ref)

        for r in range(res):
            a = a_ref[:, r * kdim:(r + 1) * kdim].astype(BF16)
            acc_ref[...] += lax.dot_general(a, dh_ref[r].astype(BF16), (((0,), (0,)), ((), ())),
                                            preferred_element_type=F32)

        @pl.when((b == grid[1] - 1) & (rb == grid[2] - 1) & (i == grid[3] - 1))
        def _():
            o_ref[...] = acc_ref[...].astype(out_dtype)

    return _pcall(
        body, name=name, grid=grid,
        in_specs=[pl.BlockSpec((None, tm, res * kdim), lambda j, b, rb, i: (b, i, rb)),
                  pl.BlockSpec((None, res, tm, tn), lambda j, b, rb, i: (b, rb, i, j))],
        out_specs=pl.BlockSpec((kdim, tn), lambda j, b, rb, i: (0, j)),
        out_shape=jax.ShapeDtypeStruct((kdim, n), out_dtype),
        scratch_shapes=[pltpu.VMEM((kdim, tn), F32)],
        compiler_params=_params("parallel", "arbitrary", "arbitrary", "arbitrary"),
    )(a3.reshape(nb, sub, d * kdim), dh4)


def _qkv_cols(g):
    cols = []
    for hp in range(ATT_OUT // LANE):
        for part in range(3):
            cols.append(part * QKV_W + g * ATT_OUT + hp * LANE)
    return cols


def _segments():
    segs = [("qkv%d" % g, [(c, LANE) for c in _qkv_cols(g)]) for g in range(3)]
    segs += [("gatt", [(3 * QKV_W, ATT_OUT)]),
             ("z", [(3 * QKV_W + ATT_OUT, D_INNER)]),
             ("xs", [(9728, D_INNER)]), ("bm", [(9728 + D_INNER, 512)]), ("cm", [(9728 + D_INNER + 512, 512)]),
             ("dt", [(DT_COL, LANE)]),
             ("gm", [(DT_COL + LANE, 2 * D_MODEL)]),
             ("gp", [(DT_COL + LANE + 2 * D_MODEL, D_MODEL)])]
    return segs


def _assemble(win):
    segs = _segments()
    tr = 128

    def body(win_ref, *outs):
        def cols(start, width):
            parts = []
            t = start
            while t < start + width:
                k = min(t // WIN_STRIDE, 3)
                nxt = min(start + width, (k + 1) * WIN_STRIDE if k < 3 else INT_COLS)
                if t % WIN_STRIDE == 0 and 0 < k and t // WIN_STRIDE == k:
                    tile = win_ref[k, :, 0:LANE] + win_ref[k - 1, :, WIN_STRIDE:WIN_STRIDE + LANE]
                    parts.append(tile)
                    t += LANE
                    continue
                lo = t - k * WIN_STRIDE
                parts.append(win_ref[k, :, lo:lo + (nxt - t)])
                t = nxt
            return parts

        for (_, pieces), o_ref in zip(segs, outs):
            off = 0
            for start, width in pieces:
                for part in cols(start, width):
                    o_ref[:, off:off + part.shape[1]] = part
                    off += part.shape[1]

    widths = [sum(w for _, w in pieces) for _, pieces in segs]
    outs = _pcall(
        body, name="assemble_w_in", grid=(D_MODEL // tr,),
        in_specs=[pl.BlockSpec((4, tr, WIN_COLS), lambda i: (0, i, 0))],
        out_specs=[pl.BlockSpec((tr, w), lambda i: (i, 0)) for w in widths],
        out_shape=[jax.ShapeDtypeStruct((D_MODEL, w), BF16) for w in widths],
        compiler_params=_params("parallel"),
    )(win)
    return {name: o for (name, _), o in zip(segs, outs)}


def _pack(dsegs):
    segs = _segments()
    tr = 128

    def body(*refs):
        ins, o_ref = refs[:-1], refs[-1]
        o_ref[...] = jnp.zeros_like(o_ref)
        for (_, pieces), s_ref in zip(segs, ins):
            off = 0
            for start, width in pieces:
                for k in range(4):
                    lo, hi = k * WIN_STRIDE, k * WIN_STRIDE + WIN_COLS
                    a, b = max(start, lo), min(start + width, hi)
                    if a < b:
                        o_ref[k, :, a - lo:b - lo] = s_ref[:, off + a - start:off + b - start]
                off += width

    widths = [sum(w for _, w in pieces) for _, pieces in segs]
    return _pcall(
        body, name="pack_dw_in", grid=(D_MODEL // tr,),
        in_specs=[pl.BlockSpec((tr, w), lambda i: (i, 0)) for w in widths],
        out_specs=pl.BlockSpec((4, tr, WIN_COLS), lambda i: (0, i, 0)),
        out_shape=jax.ShapeDtypeStruct((4, D_MODEL, WIN_COLS), BF16),
        compiler_params=_params("parallel"),
    )(*[dsegs[name] for name, _ in segs])


def _shard_to_window(w_shard, k):
    def plain(kk):
        return lambda w: jnp.pad(w, ((0, 0), (8 * kk, WIN_COLS - SHARD_COLS - 8 * kk)))

    def last(w):
        z = lambda n: jnp.zeros((w.shape[0], n), w.dtype)
        return jnp.concatenate([z(24), w[:, :SHARD3_SPLIT], z(DT_PAD), w[:, SHARD3_SPLIT:]], axis=1)

    return lax.switch(k, [plain(0), plain(1), plain(2), last], w_shard)


def _window_to_shard(win, k):
    def plain(kk):
        return lambda w: w[:, 8 * kk:8 * kk + SHARD_COLS]

    def last(w):
        return jnp.concatenate([w[:, 24:24 + SHARD3_SPLIT], w[:, 24 + SHARD3_SPLIT + DT_PAD:]], axis=1)

    return lax.switch(k, [plain(0), plain(1), plain(2), last], win)


def _bucket_maps():
    qi = np.arange(BAND)[:, None]
    kj = np.arange(2 * BAND)[None, :]
    delta = qi + BAND - kj
    maps = []
    for window, dil in PATTERNS:
        valid = (delta >= 0) & (delta <= window // dil)
        dist = np.maximum(delta, 0) * dil
        max_exact = NUM_BUCKETS // 2
        d_f = np.maximum(dist, 1).astype(np.float32)
        large = max_exact + (np.log(d_f / np.float32(max_exact)) / np.float32(math.log(MAX_DISTANCE / max_exact))
                             * np.float32(NUM_BUCKETS - max_exact)).astype(np.int32)
        large = np.minimum(large, NUM_BUCKETS - 1)
        bucket = np.where(dist < max_exact, dist, large)
        maps.append(np.where(valid, bucket, -1).astype(np.int32))
    return np.stack(maps)


def _bias_tables(rel_bias, bmaps):
    def body(rb_ref, bm_ref, o_ref):
        h = pl.program_id(0)
        bm = bm_ref[...]
        acc = jnp.full(bm.shape, NEG, F32)
        for b in range(NUM_BUCKETS):
            acc = jnp.where(bm == b, rb_ref[b, h], acc)
        o_ref[...] = acc

    return _pcall(
        body, name="bias_tables", grid=(3 * GROUP_HEADS,),
        in_specs=[pl.BlockSpec(memory_space=pltpu.SMEM),
                  pl.BlockSpec((None, BAND, 2 * BAND), lambda h: (h // GROUP_HEADS, 0, 0))],
        out_specs=pl.BlockSpec((None, BAND, 2 * BAND), lambda h: (h, 0, 0)),
        out_shape=jax.ShapeDtypeStruct((3 * GROUP_HEADS, BAND, 2 * BAND), F32),
        compiler_params=_params("parallel"),
    )(rel_bias, bmaps)


def _bias_grad(dbias, bmaps):
    def body(db_ref, bm_ref, o_ref):
        bm = bm_ref[...]
        db = db_ref[...]
        lane = lax.broadcasted_iota(jnp.int32, (1, LANE), 1)
        vec = jnp.zeros((1, LANE), F32)
        for b in range(NUM_BUCKETS):
            s = jnp.sum(jnp.where(bm == b, db, 0.0), keepdims=True)
            vec = jnp.where(lane == b, s, vec)
        o_ref[...] = vec

    return _pcall(
        body, name="bias_grad", grid=(3 * GROUP_HEADS,),
        in_specs=[pl.BlockSpec((None, BAND, 2 * BAND), lambda h: (h, 0, 0)),
                  pl.BlockSpec((None, BAND, 2 * BAND), lambda h: (h // GROUP_HEADS, 0, 0))],
        out_specs=pl.BlockSpec((None, 1, LANE), lambda h: (h, 0, 0)),
        out_shape=jax.ShapeDtypeStruct((3 * GROUP_HEADS, 1, LANE), F32),
        compiler_params=_params("parallel"),
    )(dbias, bmaps)


def _rows(n):
    return pl.ds(pl.multiple_of(n * BAND, BAND), BAND)


def _attn_fwd(qkv4, bias, d, name):
    nb, _, sub, _ = qkv4.shape
    nblk = sub // BAND
    scale = HEAD_DIM ** -0.5
    npair = ATT_OUT // LANE

    def body(q_ref, k_ref, v_ref, bias_ref, o_ref, l_ref):
        for h in range(2):
            hs = slice(h * HEAD_DIM, (h + 1) * HEAD_DIM)

            def block(n, with_prev):
                q = q_ref[_rows(n), hs]
                s_c = lax.dot_general(q, k_ref[_rows(n), hs], (((1,), (1,)), ((), ())),
                                      preferred_element_type=F32) * scale + bias_ref[h, :, BAND:]
                m = jnp.max(s_c, -1, keepdims=True)
                if with_prev:
                    s_p = lax.dot_general(q, k_ref[_rows(n - 1), hs], (((1,), (1,)), ((), ())),
                                          preferred_element_type=F32) * scale + bias_ref[h, :, :BAND]
                    m = jnp.maximum(m, jnp.max(s_p, -1, keepdims=True))
                e_c = jnp.exp(s_c - m)
                den = jnp.sum(e_c, -1, keepdims=True)
                acc = jnp.dot(e_c.astype(BF16), v_ref[_rows(n), hs], preferred_element_type=F32)
                if with_prev:
                    e_p = jnp.exp(s_p - m)
                    den = den + jnp.sum(e_p, -1, keepdims=True)
                    acc = acc + jnp.dot(e_p.astype(BF16), v_ref[_rows(n - 1), hs], preferred_element_type=F32)
                o_ref[_rows(n), hs] = acc / den
                l_ref[_rows(n), hs] = jnp.broadcast_to(m + jnp.log(den), (BAND, HEAD_DIM))

            block(0, False)
            if nblk > 1:
                def loop(n, carry):
                    block(n, True)
                    return carry
                lax.fori_loop(1, nblk, loop, 0)

    qspec = lambda part: pl.BlockSpec((None, None, sub, LANE), lambda hp, b, r: (b, r, 0, 3 * hp + part))
    ospec = pl.BlockSpec((None, sub, LANE), lambda hp, b, r: (b, 0, r * npair + hp))
    o, l = _pcall(
        body, name=name, grid=(npair, nb, d),
        in_specs=[qspec(0), qspec(1), qspec(2),
                  pl.BlockSpec((2, BAND, 2 * BAND), lambda hp, b, r: (hp, 0, 0))],
        out_specs=[ospec, ospec],
        out_shape=[jax.ShapeDtypeStruct((nb, sub, d * ATT_OUT), F32)] * 2,
        compiler_params=_params("parallel", "parallel", "parallel"),
    )(qkv4, qkv4, qkv4, bias)
    return o.reshape(nb, sub * d, ATT_OUT), l.reshape(nb, sub * d, ATT_OUT)


def _attn_bwd(qkv4, bias, do_att, o_att, lse, d, name):
    nb, _, sub, _ = qkv4.shape
    nblk = sub // BAND
    scale = HEAD_DIM ** -0.5
    npair = ATT_OUT // LANE
    nt = (((1,), (1,)), ((), ()))
    tn = (((0,), (0,)), ((), ()))

    def body(q_ref, k_ref, v_ref, bias_ref, do_ref, o_ref, l_ref, dqkv_ref, db_ref, dk_acc, dv_acc):
        b, r = pl.program_id(1), pl.program_id(2)

        @pl.when((b == 0) & (r == 0))
        def _():
            db_ref[...] = jnp.zeros_like(db_ref)

        dk_acc[...] = jnp.zeros_like(dk_acc)
        dv_acc[...] = jnp.zeros_like(dv_acc)
        for h in range(2):
            hs = slice(h * HEAD_DIM, (h + 1) * HEAD_DIM)

            def block(n, with_prev):
                q = q_ref[_rows(n), hs]
                do = do_ref[_rows(n), hs]
                do16 = do.astype(BF16)
                ebar = jnp.sum(do * o_ref[_rows(n), hs], -1, keepdims=True)
                lcol = l_ref[_rows(n), h * HEAD_DIM:h * HEAD_DIM + 1]

                def side(kn, bias_blk):
                    k = k_ref[_rows(kn), hs]
                    v = v_ref[_rows(kn), hs]
                    s = lax.dot_general(q, k, nt, preferred_element_type=F32) * scale + bias_blk
                    p = jnp.exp(s - lcol)
                    dp = lax.dot_general(do16, v, nt, preferred_element_type=F32)
                    ds = p * (dp - ebar)
                    ds16 = ds.astype(BF16)
                    dq = jnp.dot(ds16, k, preferred_element_type=F32)
                    dk_acc[_rows(kn), hs] += lax.dot_general(ds16, q, tn, preferred_element_type=F32) * scale
                    dv_acc[_rows(kn), hs] += lax.dot_general(p.astype(BF16), do16, tn, preferred_element_type=F32)
                    return dq, ds

                dq, ds_c = side(n, bias_ref[h, :, BAND:])
                db_ref[h, :, BAND:] += ds_c
                if with_prev:
                    dq_p, ds_p = side(n - 1, bias_ref[h, :, :BAND])
                    dq = dq + dq_p
                    db_ref[h, :, :BAND] += ds_p
                dqkv_ref[_rows(n), hs] = (dq * scale).astype(BF16)

            block(0, False)
            if nblk > 1:
                def loop(n, carry):
                    block(n, True)
                    return carry
                lax.fori_loop(1, nblk, loop, 0)
        dqkv_ref[:, LANE:2 * LANE] = dk_acc[...].astype(BF16)
        dqkv_ref[:, 2 * LANE:3 * LANE] = dv_acc[...].astype(BF16)

    qspec = lambda part: pl.BlockSpec((None, None, sub, LANE), lambda hp, b, r: (b, r, 0, 3 * hp + part))
    nspec = pl.BlockSpec((None, sub, LANE), lambda hp, b, r: (b, 0, r * npair + hp))
    view = lambda t: t.reshape(nb, sub, d * ATT_OUT)
    return _pcall(
        body, name=name, grid=(npair, nb, d),
        in_specs=[qspec(0), qspec(1), qspec(2),
                  pl.BlockSpec((2, BAND, 2 * BAND), lambda hp, b, r: (hp, 0, 0)), nspec, nspec, nspec],
        out_specs=[pl.BlockSpec((None, None, sub, 3 * LANE), lambda hp, b, r: (b, r, 0, hp)),
                   pl.BlockSpec((2, BAND, 2 * BAND), lambda hp, b, r: (hp, 0, 0))],
        out_shape=[jax.ShapeDtypeStruct(qkv4.shape, BF16),
                   jax.ShapeDtypeStruct((GROUP_HEADS, BAND, 2 * BAND), F32)],
        scratch_shapes=[pltpu.VMEM((sub, LANE), F32), pltpu.VMEM((sub, LANE), F32)],
        compiler_params=_params("parallel", "arbitrary", "arbitrary"),
    )(qkv4, qkv4, qkv4, bias, view(do_att), view(o_att), view(lse))


def _combine_fwd(os, ls, gatt):
    nb, seq, _ = gatt.shape
    tm = 512

    def body(o0, o1, o2, l0, l1, l2, g_ref, oa_ref, oatt_ref, lse_ref):
        m = jnp.maximum(jnp.maximum(l0[...], l1[...]), l2[...])
        tot = m + jnp.log(jnp.exp(l0[...] - m) + jnp.exp(l1[...] - m) + jnp.exp(l2[...] - m))
        o = (jnp.exp(l0[...] - tot) * o0[...] + jnp.exp(l1[...] - tot) * o1[...]
             + jnp.exp(l2[...] - tot) * o2[...])
        g = g_ref[...]
        oa_ref[...] = (o * (g * _sigmoid(g))).astype(BF16)
        oatt_ref[...] = o
        lse_ref[...] = tot

    spec = pl.BlockSpec((None, tm, ATT_OUT), lambda b, i: (b, i, 0))
    return _pcall(
        body, name="attn_combine", grid=(nb, seq // tm), in_specs=[spec] * 7, out_specs=[spec] * 3,
        out_shape=[jax.ShapeDtypeStruct((nb, seq, ATT_OUT), BF16), jax.ShapeDtypeStruct((nb, seq, ATT_OUT), F32),
                   jax.ShapeDtypeStruct((nb, seq, ATT_OUT), F32)],
        compiler_params=_params("parallel", "parallel"),
    )(*os, *ls, gatt)


def _combine_bwd(doa, gatt, o_att):
    nb, seq, _ = gatt.shape
    tm = 512

    def body(doa_ref, g_ref, o_ref, do_ref, dg_ref):
        g = g_ref[...]
        sg = _sigmoid(g)
        do_ref[...] = doa_ref[...] * (g * sg)
        dg_ref[...] = (doa_ref[...] * o_ref[...] * (sg * (1.0 + g * (1.0 - sg)))).astype(BF16)

    spec = pl.BlockSpec((None, tm, ATT_OUT), lambda b, i: (b, i, 0))
    return _pcall(
        body, name="attn_combine_bwd", grid=(nb, seq // tm), in_specs=[spec] * 3, out_specs=[spec] * 2,
        out_shape=[jax.ShapeDtypeStruct((nb, seq, ATT_OUT), F32), jax.ShapeDtypeStruct((nb, seq, ATT_OUT), BF16)],
        compiler_params=_params("parallel", "parallel"),
    )(doa, gatt, o_att)


CONV_TM = 512
CONV_TC = 512


def _shift_down(cur, halo, k):
    rolled = pltpu.roll(cur, k, 0)
    hro = pltpu.roll(halo, k, 0)
    row = lax.broadcasted_iota(jnp.int32, hro.shape, 0)
    return jnp.concatenate([jnp.where(row < k, hro, rolled[:8]), rolled[8:]], axis=0)


def _shift_up(cur, halo, k):
    n = cur.shape[0]
    rolled = pltpu.roll(cur, n - k, 0)
    hro = pltpu.roll(halo, 8 - k, 0)
    row = lax.broadcasted_iota(jnp.int32, hro.shape, 0)
    return jnp.concatenate([rolled[:n - 8], jnp.where(row >= 8 - k, hro, rolled[n - 8:])], axis=0)


def _conv_pre(cur, halo, w_ref, b_ref):
    acc = cur * w_ref[3:4, :] + b_ref[...]
    for k in range(1, 4):
        acc = acc + _shift_down(cur, halo, k) * w_ref[3 - k:4 - k, :]
    return acc


def _conv_specs(seq):
    nblk = seq // CONV_TM
    cur = pl.BlockSpec((None, CONV_TM, CONV_TC), lambda cb, b, i: (b, i, cb))
    prev = pl.BlockSpec((None, 8, CONV_TC), lambda cb, b, i: (b, jnp.maximum(i * (CONV_TM // 8) - 1, 0), cb))
    nxt = pl.BlockSpec((None, 8, CONV_TC),
                       lambda cb, b, i: (b, jnp.minimum((i + 1) * (CONV_TM // 8), seq // 8 - 1), cb))
    wspec = pl.BlockSpec((4, CONV_TC), lambda cb, b, i: (0, cb))
    bspec = pl.BlockSpec((1, CONV_TC), lambda cb, b, i: (0, cb))
    return nblk, cur, prev, nxt, wspec, bspec


def _conv_fwd(xin, w4, bias, name):
    nb, seq, ch = xin.shape
    _, cur, prev, _, wspec, bspec = _conv_specs(seq)

    def body(x_ref, h_ref, w_ref, b_ref, o_ref):
        halo = jnp.where(pl.program_id(2) > 0, h_ref[...], 0.0)
        pre = _conv_pre(x_ref[...], halo, w_ref, b_ref)
        o_ref[...] = pre * _sigmoid(pre)

    return _pcall(
        body, name=name, grid=(ch // CONV_TC, nb, seq // CONV_TM),
        in_specs=[cur, prev, wspec, bspec], out_specs=cur,
        out_shape=jax.ShapeDtypeStruct(xin.shape, F32),
        compiler_params=_params("parallel", "parallel", "parallel"),
    )(xin, xin, w4, bias)


def _conv_bwd_pre(dact, xin, w4, bias, name):
    nb, seq, ch = xin.shape
    _, cur, prev, _, wspec, bspec = _conv_specs(seq)

    def body(da_ref, x_ref, h_ref, w_ref, b_ref, dp_ref, s_ref):
        b, i = pl.program_id(1), pl.program_id(2)

        @pl.when((b == 0) & (i == 0))
        def _():
            s_ref[...] = jnp.zeros_like(s_ref)

        halo = jnp.where(i > 0, h_ref[...], 0.0)
        x = x_ref[...]
        pre = _conv_pre(x, halo, w_ref, b_ref)
        sg = _sigmoid(pre)
        dpre = da_ref[...] * (sg * (1.0 + pre * (1.0 - sg)))
        dp_ref[...] = dpre
        s_ref[3:4, :] += jnp.sum(dpre * x, 0, keepdims=True)
        for k in range(1, 4):
            s_ref[3 - k:4 - k, :] += jnp.sum(dpre * _shift_down(x, halo, k), 0, keepdims=True)
        s_ref[4:5, :] += jnp.sum(dpre, 0, keepdims=True)

    return _pcall(
        body, name=name, grid=(ch // CONV_TC, nb, seq // CONV_TM),
        in_specs=[cur, cur, prev, wspec, bspec],
        out_specs=[cur, pl.BlockSpec((8, CONV_TC), lambda cb, b, i: (0, cb))],
        out_shape=[jax.ShapeDtypeStruct(xin.shape, F32), jax.ShapeDtypeStruct((8, ch), F32)],
        compiler_params=_params("parallel", "arbitrary", "arbitrary"),
    )(dact, xin, xin, w4, bias)


def _conv_bwd_x(dpre, w4, name):
    nb, seq, ch = dpre.shape
    nblk, cur, _, nxt, wspec, _ = _conv_specs(seq)

    def body(d_ref, n_ref, w_ref, o_ref):
        halo = jnp.where(pl.program_id(2) < nblk - 1, n_ref[...], 0.0)
        cur_v = d_ref[...]
        acc = cur_v * w_ref[3:4, :]
        for j in range(1, 4):
            acc = acc + _shift_up(cur_v, halo, j) * w_ref[3 - j:4 - j, :]
        o_ref[...] = acc.astype(BF16)

    out = _pcall(
        body, name=name, grid=(ch // CONV_TC, nb, seq // CONV_TM),
        in_specs=[cur, nxt, wspec], out_specs=cur,
        out_shape=jax.ShapeDtypeStruct(dpre.shape, BF16),
        compiler_params=_params("parallel", "parallel", "parallel"),
    )(dpre, dpre, w4)
    return out.reshape(nb, 1, seq, ch)


def _softplus_sig(dt_raw, dt_bias_row):
    nb, seq, _ = dt_raw.shape
    tm = 512

    def body(r_ref, b_ref, sp_ref, sg_ref):
        v = r_ref[...] + b_ref[...]
        sp_ref[...] = jnp.maximum(v, 0.0) + jnp.log1p(jnp.exp(-jnp.abs(v)))
        sg_ref[...] = _sigmoid(v)

    spec = pl.BlockSpec((None, tm, LANE), lambda b, i: (b, i, 0))
    return _pcall(
        body, name="dt_softplus", grid=(nb, seq // tm),
        in_specs=[spec, pl.BlockSpec((1, LANE), lambda b, i: (0, 0))], out_specs=[spec, spec],
        out_shape=[jax.ShapeDtypeStruct(dt_raw.shape, F32)] * 2,
        compiler_params=_params("parallel", "parallel"),
    )(dt_raw, dt_bias_row)


def _group_lanes(t):
    pads = [(0, 0)] * (t.ndim - 1) + [(0, LANE - GROUP_SSM_HEADS)]
    return jnp.stack([jnp.pad(t[..., GROUP_SSM_HEADS * g:GROUP_SSM_HEADS * (g + 1)], pads) for g in range(SSM_GROUPS)])


def _ungroup_lanes(t):
    return jnp.concatenate([t[g][..., :GROUP_SSM_HEADS] for g in range(SSM_GROUPS)], axis=-1)


def _decays(dt, al_ref):
    row = lax.broadcasted_iota(jnp.int32, (CHUNK, CHUNK), 0)
    col = lax.broadcasted_iota(jnp.int32, (CHUNK, CHUNK), 1)
    tril = (row >= col).astype(F32)
    triu = (row <= col).astype(F32)
    arow = -jnp.exp(al_ref[...])
    a = dt * arow
    acs = jnp.dot(tril, a, precision=HIGHEST, preferred_element_type=F32)
    acs_t = lax.dot_general(a, triu, (((0,), (0,)), ((), ())), precision=HIGHEST, preferred_element_type=F32)
    return arow, acs, acs_t, row >= col, triu


def _ssd_specs(nb, seq):
    nc = seq // CHUNK
    hw = GROUP_SSM_HEADS * HEAD_DIM

    def mk(rev):
        cidx = (lambda c: nc - 1 - c) if rev else (lambda c: c)
        wide = pl.BlockSpec((None, CHUNK, hw), lambda g, b, c: (b, cidx(c), g))
        state = pl.BlockSpec((None, CHUNK, D_STATE), lambda g, b, c: (b, cidx(c), g))
        lanes = pl.BlockSpec((None, None, CHUNK, LANE), lambda g, b, c: (g, b, cidx(c), 0))
        prev = pl.BlockSpec((None, None, None, GROUP_SSM_HEADS, D_STATE, HEAD_DIM),
                            lambda g, b, c: (b, cidx(c), g, 0, 0, 0))
        return wide, state, lanes, prev

    grow = pl.BlockSpec((None, 1, LANE), lambda g, b, c: (g, 0, 0))
    nwspec = pl.BlockSpec((1, hw), lambda g, b, c: (0, g))
    return nc, hw, mk, grow, nwspec


def _ssd_fwd(xs, bm, cm, dtg, z, alog_g, dskip_g, normw):
    nb, seq, _ = xs.shape
    nc, hw, mk, grow, nwspec = _ssd_specs(nb, seq)
    wide, state, lanes, prev = mk(False)

    def body(xs_ref, b_ref, c_ref, dt_ref, z_ref, al_ref, ds_ref, nw_ref, ys_ref, y_ref, sp_ref, st_ref, ybuf):
        @pl.when(pl.program_id(2) == 0)
        def _():
            st_ref[...] = jnp.zeros_like(st_ref)

        dt = dt_ref[...]
        _, acs, acs_t, causal, _ = _decays(dt, al_ref)
        last = acs[CHUNK - 1:CHUNK, :]
        bmat = b_ref[...].astype(BF16)
        cmat = c_ref[...].astype(BF16)
        cb = lax.dot_general(cmat, bmat, (((1,), (1,)), ((), ())), preferred_element_type=F32)
        for j in range(GROUP_SSM_HEADS):
            hs = slice(j * HEAD_DIM, (j + 1) * HEAD_DIM)
            colv = acs[:, j:j + 1]
            lmat = jnp.exp(jnp.where(causal, colv - acs_t[j:j + 1, :], -jnp.inf))
            xh = xs_ref[:, hs]
            xdt = xh * dt[:, j:j + 1]
            yd = jnp.dot((cb * lmat).astype(BF16), xdt.astype(BF16), preferred_element_type=F32)
            lastj = last[:, j:j + 1]
            dte = jnp.exp(lastj - colv)
            sts = lax.dot_general(bmat, (xdt * dte).astype(BF16), (((0,), (0,)), ((), ())),
                                  preferred_element_type=F32)
            s_prev = st_ref[j]
            s16 = s_prev.astype(BF16)
            sp_ref[j] = s16
            yo = jnp.dot(cmat, s16, preferred_element_type=F32) * jnp.exp(colv)
            ybuf[:, hs] = yd + yo + ds_ref[:, j:j + 1] * xh
            st_ref[j] = s_prev * jnp.exp(lastj) + sts
        y = ybuf[...]
        zz = z_ref[...]
        u = y * (zz * _sigmoid(zz))
        rn = lax.rsqrt(jnp.mean(u * u, -1, keepdims=True) + RMS_EPS)
        ys_ref[...] = (u * rn * nw_ref[...]).astype(BF16)
        y_ref[...] = y

    return _pcall(
        body, name="ssd_fwd", grid=(SSM_GROUPS, nb, nc),
        in_specs=[wide, state, state, lanes, wide, grow, grow, nwspec],
        out_specs=[wide, wide, prev],
        out_shape=[jax.ShapeDtypeStruct((nb, seq, D_INNER), BF16), jax.ShapeDtypeStruct((nb, seq, D_INNER), F32),
                   jax.ShapeDtypeStruct((nb, nc, SSM_GROUPS, GROUP_SSM_HEADS, D_STATE, HEAD_DIM), BF16)],
        scratch_shapes=[pltpu.VMEM((GROUP_SSM_HEADS, D_STATE, HEAD_DIM), F32), pltpu.VMEM((CHUNK, hw), F32)],
        compiler_params=_params("parallel", "parallel", "arbitrary"),
    )(xs, bm, cm, dtg, z, alog_g, dskip_g, normw)


def _ssd_bwd(xs, bm, cm, dtg, sgg, z, y, dys, sprev, alog_g, dskip_g, normw):
    nb, seq, _ = xs.shape
    nc, hw, mk, grow, nwspec = _ssd_specs(nb, seq)
    wide, state, lanes, prev = mk(True)
    nt = (((1,), (1,)), ((), ()))
    tn = (((0,), (0,)), ((), ()))

    def body(xs_ref, b_ref, c_ref, dt_ref, sg_ref, z_ref, y_ref, dys_ref, sp_ref, al_ref, ds_ref, nw_ref,
             dxs_ref, db_ref, dc_ref, ddt_ref, dz_ref, small_ref, dnw_ref, g_ref, dybuf, dxbuf):
        b, c = pl.program_id(1), pl.program_id(2)

        @pl.when((b == 0) & (c == 0))
        def _():
            small_ref[...] = jnp.zeros_like(small_ref)
            dnw_ref[...] = jnp.zeros_like(dnw_ref)

        @pl.when(c == 0)
        def _():
            g_ref[...] = jnp.zeros_like(g_ref)

        yv, zz, dys_v, nw = y_ref[...], z_ref[...], dys_ref[...], nw_ref[...]
        sz = _sigmoid(zz)
        silu = zz * sz
        u = yv * silu
        rn = lax.rsqrt(jnp.mean(u * u, -1, keepdims=True) + RMS_EPS)
        gn = dys_v * nw
        du = rn * gn - u * (rn * rn * rn) * jnp.mean(u * gn, -1, keepdims=True)
        dnw_ref[...] += jnp.sum(dys_v * u * rn, 0, keepdims=True)
        dybuf[...] = du * silu
        dz_ref[...] = du * yv * (sz * (1.0 + zz * (1.0 - sz)))

        dt = dt_ref[...]
        arow, acs, acs_t, causal, triu = _decays(dt, al_ref)
        last = acs[CHUNK - 1:CHUNK, :]
        bmat = b_ref[...].astype(BF16)
        cmat = c_ref[...].astype(BF16)
        cb = lax.dot_general(cmat, bmat, nt, preferred_element_type=F32)
        lane = lax.broadcasted_iota(jnp.int32, (CHUNK, LANE), 1)
        lane1 = lax.broadcasted_iota(jnp.int32, (1, LANE), 1)
        rowc = lax.broadcasted_iota(jnp.int32, (CHUNK, 1), 0)
        dacs = jnp.zeros((CHUNK, LANE), F32)
        ddt_x = jnp.zeros((CHUNK, LANE), F32)
        dcb = jnp.zeros((CHUNK, CHUNK), F32)
        dc_acc = jnp.zeros((CHUNK, D_STATE), F32)
        db_acc = jnp.zeros((CHUNK, D_STATE), F32)
        dsk_row = jnp.zeros((1, LANE), F32)
        for j in range(GROUP_SSM_HEADS):
            hs = slice(j * HEAD_DIM, (j + 1) * HEAD_DIM)
            colv = acs[:, j:j + 1]
            lmat = jnp.exp(jnp.where(causal, colv - acs_t[j:j + 1, :], -jnp.inf))
            xh = xs_ref[:, hs]
            dtc = dt[:, j:j + 1]
            xdt = xh * dtc
            xdt16 = xdt.astype(BF16)
            mf = cb * lmat
            dyj = dybuf[:, hs]
            dy16 = dyj.astype(BF16)
            dyd = dyj * jnp.exp(colv)
            dyd16 = dyd.astype(BF16)
            s16 = sp_ref[j]
            gj = g_ref[j]
            g16 = gj.astype(BF16)
            lastj = last[:, j:j + 1]
            dte = jnp.exp(lastj - colv)
            cd = jnp.exp(lastj)
            cs = jnp.dot(cmat, s16, preferred_element_type=F32)
            dc_acc = dc_acc + lax.dot_general(dyd16, s16, nt, preferred_element_type=F32)
            g_here = lax.dot_general(cmat, dyd16, tn, preferred_element_type=F32)
            bg = jnp.dot(bmat, g16, preferred_element_type=F32)
            dxdt = bg * dte
            ddte = jnp.sum(bg * xdt, -1, keepdims=True)
            db_acc = db_acc + lax.dot_general((xdt * dte).astype(BF16), g16, nt, preferred_element_type=F32)
            dcd = jnp.sum(gj * s16.astype(F32), keepdims=True)
            dm = lax.dot_general(dy16, xdt16, nt, preferred_element_type=F32)
            dxdt = dxdt + lax.dot_general(mf.astype(BF16), dy16, tn, preferred_element_type=F32)
            wmat = dm * mf
            dcb = dcb + dm * lmat
            dac = (jnp.sum(wmat, -1, keepdims=True) - jnp.sum(wmat.T, -1, keepdims=True)
                   + jnp.sum(dyd * cs, -1, keepdims=True) - ddte * dte)
            tail = jnp.sum(ddte * dte, keepdims=True) + dcd * cd
            dac = dac + jnp.where(rowc == CHUNK - 1, tail, 0.0)
            dacs = jnp.where(lane == j, dac, dacs)
            ddt_x = jnp.where(lane == j, jnp.sum(dxdt * xh, -1, keepdims=True), ddt_x)
            dxbuf[:, hs] = dxdt * dtc + ds_ref[:, j:j + 1] * dyj
            dsk_row = jnp.where(lane1 == j, jnp.sum(dyj * xh, keepdims=True), dsk_row)
            g_ref[j] = gj * cd + g_here
        da = jnp.dot(triu, dacs, precision=HIGHEST, preferred_element_type=F32)
        ddt_raw = (da * arow + ddt_x) * sg_ref[...]
        ddt_ref[...] = ddt_raw
        small_ref[0:1, :] += jnp.sum(da * dt, 0, keepdims=True) * arow
        small_ref[1:2, :] += dsk_row
        small_ref[2:3, :] += jnp.sum(ddt_raw, 0, keepdims=True)
        dcb16 = dcb.astype(BF16)
        dc_ref[...] = dc_acc + jnp.dot(dcb16, bmat, preferred_element_type=F32)
        db_ref[...] = db_acc + lax.dot_general(dcb16, cmat, tn, preferred_element_type=F32)
        dxs_ref[...] = dxbuf[...]

    return _pcall(
        body, name="ssd_bwd", grid=(SSM_GROUPS, nb, nc),
        in_specs=[wide, state, state, lanes, lanes, wide, wide, wide, prev, grow, grow, nwspec],
        out_specs=[wide, state, state, lanes, wide,
                   pl.BlockSpec((None, 8, LANE), lambda g, b, c: (g, 0, 0)), nwspec],
        out_shape=[jax.ShapeDtypeStruct((nb, seq, D_INNER), F32),
                   jax.ShapeDtypeStruct((nb, seq, SSM_GROUPS * D_STATE), F32),
                   jax.ShapeDtypeStruct((nb, seq, SSM_GROUPS * D_STATE), F32),
                   jax.ShapeDtypeStruct((SSM_GROUPS, nb, seq, LANE), F32),
                   jax.ShapeDtypeStruct((nb, seq, D_INNER), F32),
                   jax.ShapeDtypeStruct((SSM_GROUPS, 8, LANE), F32),
                   jax.ShapeDtypeStruct((1, D_INNER), F32)],
        scratch_shapes=[pltpu.VMEM((GROUP_SSM_HEADS, D_STATE, HEAD_DIM), F32),
                        pltpu.VMEM((CHUNK, hw), F32), pltpu.VMEM((CHUNK, hw), F32)],
        compiler_params=_params("parallel", "arbitrary", "arbitrary"),
    )(xs, bm, cm, dtg, sgg, z, y, dys, sprev, alog_g, dskip_g, normw)


EW_TM = 256


def _merge_fwd(y_a, y_b, gm, bgate):
    nb, seq, _ = y_a.shape

    def body(a_ref, b_ref, ga_ref, gb_ref, bg_ref, o_ref):
        sa = _sigmoid(ga_ref[...] + bg_ref[0:1, :])
        sb = _sigmoid(gb_ref[...] + bg_ref[1:2, :])
        o_ref[...] = (sa * a_ref[...] + sb * b_ref[...]).astype(BF16)

    spec = pl.BlockSpec((None, EW_TM, D_MODEL), lambda b, i: (b, i, 0))
    spec1 = pl.BlockSpec((None, EW_TM, D_MODEL), lambda b, i: (b, i, 1))
    return _pcall(
        body, name="merge_fwd", grid=(nb, seq // EW_TM),
        in_specs=[spec, spec, spec, spec1, pl.BlockSpec((8, D_MODEL), lambda b, i: (0, 0))], out_specs=spec,
        out_shape=jax.ShapeDtypeStruct((nb, seq, D_MODEL), BF16),
        compiler_params=_params("parallel", "parallel"),
    )(y_a, y_b, gm, gm, bgate)


def _merge_bwd(dmerged, y_a, y_b, gm, bgate):
    nb, seq, _ = y_a.shape

    def body(dm_ref, a_ref, b_ref, ga_ref, gb_ref, bg_ref, dya_ref, dyb_ref, dg_ref, s_ref):
        @pl.when((pl.program_id(0) == 0) & (pl.program_id(1) == 0))
        def _():
            s_ref[...] = jnp.zeros_like(s_ref)

        dm = dm_ref[...]
        sa = _sigmoid(ga_ref[...] + bg_ref[0:1, :])
        sb = _sigmoid(gb_ref[...] + bg_ref[1:2, :])
        dya_ref[...] = (dm * sa).astype(BF16)
        dyb_ref[...] = (dm * sb).astype(BF16)
        dga = dm * a_ref[...] * (sa * (1.0 - sa))
        dgb = dm * b_ref[...] * (sb * (1.0 - sb))
        dg_ref[:, :D_MODEL] = dga.astype(BF16)
        dg_ref[:, D_MODEL:] = dgb.astype(BF16)
        s_ref[0:1, :] += jnp.sum(dga, 0, keepdims=True)
        s_ref[1:2, :] += jnp.sum(dgb, 0, keepdims=True)

    spec = pl.BlockSpec((None, EW_TM, D_MODEL), lambda b, i: (b, i, 0))
    spec1 = pl.BlockSpec((None, EW_TM, D_MODEL), lambda b, i: (b, i, 1))
    small = pl.BlockSpec((8, D_MODEL), lambda b, i: (0, 0))
    dya, dyb, dgm, sums = _pcall(
        body, name="merge_bwd", grid=(nb, seq // EW_TM),
        in_specs=[spec, spec, spec, spec, spec1, small],
        out_specs=[spec, spec, pl.BlockSpec((None, EW_TM, 2 * D_MODEL), lambda b, i: (b, i, 0)), small],
        out_shape=[jax.ShapeDtypeStruct((nb, seq, D_MODEL), BF16), jax.ShapeDtypeStruct((nb, seq, D_MODEL), BF16),
                   jax.ShapeDtypeStruct((nb, seq, 2 * D_MODEL), BF16), jax.ShapeDtypeStruct((8, D_MODEL), F32)],
        compiler_params=_params("arbitrary", "arbitrary"),
    )(dmerged, y_a, y_b, gm, gm, bgate)
    r4 = lambda t: t.reshape(nb, 1, seq, t.shape[-1])
    return r4(dya), r4(dyb), r4(dgm), sums


def _ln_loss(x, mix, gp, pw, target, bgate, ln_g, ln_b):
    nb, seq, _ = x.shape

    def body(x_ref, mix_ref, gp_ref, pw_ref, t_ref, bg_ref, g_ref, b_ref, dx_ref, dp_ref, dpw_ref, dgp_ref, s_ref):
        @pl.when((pl.program_id(0) == 0) & (pl.program_id(1) == 0))
        def _():
            s_ref[...] = jnp.zeros_like(s_ref)

        sp = _sigmoid(gp_ref[...] + bg_ref[2:3, :])
        pw = pw_ref[...]
        pre = ALPHA * x_ref[...] + mix_ref[...] + sp * pw
        mu = jnp.mean(pre, -1, keepdims=True)
        cen = pre - mu
        rstd = lax.rsqrt(jnp.mean(cen * cen, -1, keepdims=True) + LN_EPS)
        xhat = cen * rstd
        err = xhat * g_ref[...] + b_ref[...] - t_ref[...]
        dy = err * (1.0 / D_MODEL)
        dxh = dy * g_ref[...]
        dpre = rstd * (dxh - jnp.mean(dxh, -1, keepdims=True) - xhat * jnp.mean(dxh * xhat, -1, keepdims=True))
        dx_ref[...] = ALPHA * dpre
        dp_ref[...] = dpre.astype(BF16)
        dpw_ref[...] = (dpre * sp).astype(BF16)
        dgp = dpre * pw * (sp * (1.0 - sp))
        dgp_ref[...] = dgp.astype(BF16)
        s_ref[0:1, :] += jnp.sum(dy * xhat, 0, keepdims=True)
        s_ref[1:2, :] += jnp.sum(dy, 0, keepdims=True)
        s_ref[2:3, :] += jnp.sum(dgp, 0, keepdims=True)
        s_ref[3:4, :] += jnp.sum(err * err, 0, keepdims=True)

    spec = pl.BlockSpec((None, EW_TM, D_MODEL), lambda b, i: (b, i, 0))
    small = pl.BlockSpec((8, D_MODEL), lambda b, i: (0, 0))
    row = pl.BlockSpec((1, D_MODEL), lambda b, i: (0, 0))
    dx0, dpre16, dpw16, dgp16, sums = _pcall(
        body, name="ln_loss", grid=(nb, seq // EW_TM),
        in_specs=[spec] * 5 + [small, row, row], out_specs=[spec] * 4 + [small],
        out_shape=[jax.ShapeDtypeStruct((nb, seq, D_MODEL), F32)] + [jax.ShapeDtypeStruct((nb, seq, D_MODEL), BF16)] * 3
        + [jax.ShapeDtypeStruct((8, D_MODEL), F32)],
        compiler_params=_params("arbitrary", "arbitrary"),
    )(x, mix, gp, pw, target, bgate, ln_g, ln_b)
    r4 = lambda t: t.reshape(nb, 1, seq, D_MODEL)
    return dx0, r4(dpre16), r4(dpw16), r4(dgp16), sums


def _adamw(w, g, m, v, name):
    rows, cols = w.shape
    tr = rows
    for cand in range(8, rows, 8):
        if rows % cand == 0 and cand * cols * 4 <= (1 << 20):
            tr = cand
    if rows * cols * 4 <= (1 << 20):
        tr = rows
    c1 = 1.0 - ADAM_B1 ** ADAM_STEP
    c2 = 1.0 - ADAM_B2 ** ADAM_STEP

    def body(w_ref, g_ref, m_ref, v_ref, d_ref, nm_ref, nv_ref):
        gv = g_ref[...]
        nm = ADAM_B1 * m_ref[...] + (1.0 - ADAM_B1) * gv
        nv = ADAM_B2 * v_ref[...] + (1.0 - ADAM_B2) * (gv * gv)
        d_ref[...] = -ADAM_LR * ((nm / c1) / (jnp.sqrt(nv / c2) + ADAM_EPS) + ADAM_WD * w_ref[...])
        nm_ref[...] = nm
        nv_ref[...] = nv

    spec = pl.BlockSpec((tr, cols), lambda i: (i, 0))
    return _pcall(
        body, name=name, grid=(rows // tr,), in_specs=[spec] * 4, out_specs=[spec] * 3,
        out_shape=[jax.ShapeDtypeStruct(w.shape, F32)] * 3, compiler_params=_params("parallel"),
    )(w, g, m, v)


def _sum_rows(parts, out_dtype, name):
    rows, cols = parts[0].shape
    tr = rows
    for cand in range(16, rows, 16):
        if rows % cand == 0 and cand * cols * 4 <= (1 << 20):
            tr = cand
    n = len(parts)

    def body(*refs):
        acc = refs[0][...].astype(F32)
        for r in refs[1:n]:
            acc = acc + r[...].astype(F32)
        refs[n][...] = acc.astype(out_dtype)

    spec = pl.BlockSpec((tr, cols), lambda i: (i, 0))
    return _pcall(
        body, name=name, grid=(rows // tr,), in_specs=[spec] * n, out_specs=spec,
        out_shape=jax.ShapeDtypeStruct((rows, cols), out_dtype), compiler_params=_params("parallel"),
    )(*parts)


def _place():
    return lax.axis_index("x"), lax.axis_index("y"), lax.axis_index("c")


def _other_chips(x, y):
    return [(1 - x, y), (x, 1 - y), (1 - x, 1 - y)]


def _remote(src, dst, send_sem, recv_sem, to):
    return pltpu.make_async_remote_copy(src_ref=src, dst_ref=dst, send_sem=send_sem, recv_sem=recv_sem,
                                        device_id=to, device_id_type=MESH)


ANY = pl.BlockSpec(memory_space=pl.ANY)
DMA_CHUNK_BYTES = 512 * 1024


def _row_chunks(rows, row_bytes):
    per = max(16, DMA_CHUNK_BYTES // row_bytes // 16 * 16)
    return [(s, min(per, rows - s)) for s in range(0, rows, per)]


def _row_tile(rows, cols, align):
    best = None
    for cand in range(align, rows + 1, align):
        if rows % cand == 0 and cand * cols * 4 <= (1 << 20):
            best = cand
    return best or rows


def _allgather_pieces(pieces):
    n = len(pieces)
    halves = [_row_chunks(p.shape[0] // 2, p.shape[1] * p.dtype.itemsize) for p in pieces]
    wholes = [_row_chunks(p.shape[0], p.shape[1] * p.dtype.itemsize) for p in pieces]
    n_ici = 3 * sum(len(h) for h in halves)
    n_loc = sum(len(w) for w in wholes)

    def body(*refs):
        ins, outs = refs[:n], refs[n:2 * n]
        send_sems, recv_sems, local_sems = refs[2 * n:]
        x, y, c = _place()
        me = 2 * x + y
        sibling = (x, y, 1 - c)
        chips = _other_chips(x, y)
        locals_ = []
        for a in range(n):
            for s, m in wholes[a]:
                loc = pltpu.make_async_copy(ins[a].at[pl.ds(s, m)], outs[a].at[me, pl.ds(s, m)],
                                            local_sems.at[len(locals_)])
                loc.start()
                locals_.append(loc)
        ici = []
        for a in range(n):
            half = ins[a].shape[0] // 2
            for s, m in halves[a]:
                for j, (cx, cy) in enumerate(chips):
                    k = len(ici)
                    cp = _remote(ins[a].at[pl.ds(c * half + s, m)], outs[a].at[me, pl.ds(c * half + s, m)],
                                 send_sems.at[k], recv_sems.at[k], (cx, cy, c))
                    cp.start()
                    ici.append((a, s, m, j, cp))
        passed = []
        for k, (a, s, m, j, _) in enumerate(ici):
            half = ins[a].shape[0] // 2
            cx, cy = chips[j]
            blk = outs[a].at[2 * cx + cy, pl.ds(c * half + s, m)]
            _remote(blk, blk, send_sems.at[k], recv_sems.at[k], (cx, cy, c)).wait_recv()
            fw = _remote(blk, blk, send_sems.at[n_ici + k], recv_sems.at[n_ici + k], sibling)
            fw.start()
            passed.append(fw)
        for k, (a, s, m, j, _) in enumerate(ici):
            half = ins[a].shape[0] // 2
            cx, cy = chips[j]
            blk = outs[a].at[2 * cx + cy, pl.ds((1 - c) * half + s, m)]
            _remote(blk, blk, send_sems.at[n_ici + k], recv_sems.at[n_ici + k], sibling).wait_recv()
        for item in ici:
            item[4].wait_send()
        for fw in passed:
            fw.wait_send()
        for loc in locals_:
            loc.wait()

    return _pcall(
        body, name="allgather_weights", in_specs=[ANY] * n, out_specs=[ANY] * n,
        out_shape=[jax.ShapeDtypeStruct((4,) + p.shape, p.dtype) for p in pieces],
        scratch_shapes=[pltpu.SemaphoreType.DMA((2 * n_ici,)), pltpu.SemaphoreType.DMA((2 * n_ici,)),
                        pltpu.SemaphoreType.DMA((n_loc,))],
        compiler_params=pltpu.CompilerParams(has_side_effects=True),
    )(*pieces)


def _sibling_exchange(grads):
    n = len(grads)
    chunks = [_row_chunks(g.shape[1] // 2, g.shape[2] * g.dtype.itemsize) for g in grads]
    n_sem = 4 * sum(len(ch) for ch in chunks)

    def body(*refs):
        ins, gots = refs[:n], refs[n:2 * n]
        send_sems, recv_sems = refs[2 * n:]
        x, y, c = _place()
        sibling = (x, y, 1 - c)
        work = []
        for a in range(n):
            half = ins[a].shape[1] // 2
            for piece in range(4):
                for s, m in chunks[a]:
                    k = len(work)
                    cp = _remote(ins[a].at[piece, pl.ds((1 - c) * half + s, m)], gots[a].at[piece, pl.ds(s, m)],
                                 send_sems.at[k], recv_sems.at[k], sibling)
                    cp.start()
                    work.append(cp)
        for cp in work:
            cp.wait()

    return _pcall(
        body, name="grad_sibling_exchange", in_specs=[ANY] * n, out_specs=[ANY] * n,
        out_shape=[jax.ShapeDtypeStruct((4, g.shape[1] // 2, g.shape[2]), g.dtype) for g in grads],
        scratch_shapes=[pltpu.SemaphoreType.DMA((n_sem,)), pltpu.SemaphoreType.DMA((n_sem,))],
        compiler_params=pltpu.CompilerParams(has_side_effects=True),
    )(*grads)


def _chip_scatter(sums):
    n = len(sums)
    chunks = [_row_chunks(s.shape[1], s.shape[2] * s.dtype.itemsize) for s in sums]
    n_sem = 3 * sum(len(ch) for ch in chunks)

    def body(*refs):
        ins, gots = refs[:n], refs[n:2 * n]
        send_sems, recv_sems = refs[2 * n:]
        x, y, c = _place()
        chips = _other_chips(x, y)
        work = []
        for a in range(n):
            for s, m in chunks[a]:
                for j, (cx, cy) in enumerate(chips):
                    k = len(work)
                    cp = _remote(ins[a].at[2 * cx + cy, pl.ds(s, m)], gots[a].at[j, pl.ds(s, m)],
                                 send_sems.at[k], recv_sems.at[k], (cx, cy, c))
                    cp.start()
                    work.append(cp)
        for cp in work:
            cp.wait()

    return _pcall(
        body, name="grad_chip_scatter", in_specs=[ANY] * n, out_specs=[ANY] * n,
        out_shape=[jax.ShapeDtypeStruct((3,) + s.shape[1:], s.dtype) for s in sums],
        scratch_shapes=[pltpu.SemaphoreType.DMA((n_sem,)), pltpu.SemaphoreType.DMA((n_sem,))],
        compiler_params=pltpu.CompilerParams(has_side_effects=True),
    )(*sums)


def _sibling_gather(fulls):
    n = len(fulls)
    chunks = [_row_chunks(f.shape[0] // 2, f.shape[1] * f.dtype.itemsize) for f in fulls]
    n_sem = sum(len(ch) for ch in chunks)

    def body(*refs):
        outs = refs[n:2 * n]
        send_sems, recv_sems = refs[2 * n:]
        x, y, c = _place()
        sibling = (x, y, 1 - c)
        work = []
        for a in range(n):
            h = outs[a].shape[0] // 2
            for s, m in chunks[a]:
                k = len(work)
                mine = outs[a].at[pl.ds(c * h + s, m)]
                cp = _remote(mine, mine, send_sems.at[k], recv_sems.at[k], sibling)
                cp.start()
                work.append((a, s, m, cp))
        for k, (a, s, m, cp) in enumerate(work):
            h = outs[a].shape[0] // 2
            cp.wait_send()
            theirs = outs[a].at[pl.ds((1 - c) * h + s, m)]
            _remote(theirs, theirs, send_sems.at[k], recv_sems.at[k], sibling).wait_recv()

    return _pcall(
        body, name="grad_sibling_gather", in_specs=[ANY] * n, out_specs=[ANY] * n,
        out_shape=[jax.ShapeDtypeStruct(f.shape, f.dtype) for f in fulls],
        input_output_aliases={a: a for a in range(n)},
        scratch_shapes=[pltpu.SemaphoreType.DMA((n_sem,)), pltpu.SemaphoreType.DMA((n_sem,))],
        compiler_params=pltpu.CompilerParams(has_side_effects=True),
    )(*fulls)


def _pair_sum(grad, got, place, name):
    _, rows, cols = grad.shape
    half = rows // 2
    tr = _row_tile(half, cols, 16)

    def body(p_ref, a_ref, b_ref, o_ref):
        o_ref[...] = (a_ref[...].astype(F32) + b_ref[...].astype(F32)).astype(BF16)

    return _pcall(
        body, name=name,
        grid_spec=pltpu.PrefetchScalarGridSpec(
            num_scalar_prefetch=1, grid=(4, half // tr),
            in_specs=[pl.BlockSpec((None, tr, cols), lambda k, i, p: (k, p[1] * (half // tr) + i, 0)),
                      pl.BlockSpec((None, tr, cols), lambda k, i, p: (k, i, 0))],
            out_specs=pl.BlockSpec((None, tr, cols), lambda k, i, p: (k, i, 0))),
        out_shape=jax.ShapeDtypeStruct((4, half, cols), BF16),
        compiler_params=_params("parallel", "parallel"),
    )(place, grad, got)


def _chip_sum(sums, got, place, name):
    _, h, cols = sums.shape
    tr = _row_tile(h, cols, 16)

    def body(p_ref, own_ref, g0, g1, g2, o_ref):
        o_ref[...] = ((own_ref[...].astype(F32) + g0[...].astype(F32)) + g1[...].astype(F32)) + g2[...].astype(F32)

    gspec = lambda j: pl.BlockSpec((None, tr, cols), lambda i, p: (j, i, 0))
    return _pcall(
        body, name=name,
        grid_spec=pltpu.PrefetchScalarGridSpec(
            num_scalar_prefetch=1, grid=(h // tr,),
            in_specs=[pl.BlockSpec((None, tr, cols), lambda i, p: (p[0], i, 0)), gspec(0), gspec(1), gspec(2)],
            out_specs=pl.BlockSpec((tr, cols), lambda i, p: (p[1] * (h // tr) + i, 0))),
        out_shape=jax.ShapeDtypeStruct((2 * h, cols), F32),
        compiler_params=_params("parallel"),
    )(place, sums, got, got, got)


def _allgather8(buf, name):
    rows = buf.shape[0]

    def body(in_ref, out_ref, send_sems, recv_sems):
        x, y, c = _place()
        me = 4 * x + 2 * y + c
        out_ref[me] = in_ref[...]
        work = []
        for rel in range(1, 8):
            fx, fy, fc = (rel >> 2) & 1, (rel >> 1) & 1, rel & 1
            to = (x ^ fx, y ^ fy, c ^ fc)
            cp = _remote(in_ref, out_ref.at[me], send_sems.at[rel - 1], recv_sems.at[rel - 1], to)
            cp.start()
            work.append((cp, 4 * to[0] + 2 * to[1] + to[2]))
        for rel, (cp, frm) in enumerate(work):
            cp.wait_send()
            blk = out_ref.at[frm]
            _remote(blk, blk, send_sems.at[rel], recv_sems.at[rel], (x, y, c)).wait_recv()

    return _pcall(
        body, name=name, in_specs=[pl.BlockSpec(memory_space=pltpu.VMEM)],
        out_specs=pl.BlockSpec(memory_space=pltpu.VMEM),
        out_shape=jax.ShapeDtypeStruct((8, rows, LANE), F32),
        scratch_shapes=[pltpu.SemaphoreType.DMA((7,)), pltpu.SemaphoreType.DMA((7,))],
        compiler_params=pltpu.CompilerParams(has_side_effects=True),
    )(buf)


def _reduce_scatter(grads):
    x, y, c = _place()
    place = jnp.stack([2 * x + y, c]).astype(jnp.int32)
    got = _sibling_exchange(grads)
    chip_sums = [_pair_sum(g, t, place, "grad_pair_sum_%d" % i) for i, (g, t) in enumerate(zip(grads, got))]
    others = _chip_scatter(chip_sums)
    fulls = [_chip_sum(s, t, place, "grad_chip_sum_%d" % i) for i, (s, t) in enumerate(zip(chip_sums, others))]
    return _sibling_gather(fulls)


def _pack_rows(arrs):
    parts = []
    for a in arrs:
        f = a.reshape(-1).astype(F32)
        parts.append(jnp.pad(f, (0, (-f.shape[0]) % LANE)))
    flat = jnp.concatenate(parts)
    rows = -(-flat.shape[0] // LANE)
    rows8 = -(-rows // 8) * 8
    return jnp.pad(flat, (0, rows8 * LANE - flat.shape[0])).reshape(rows8, LANE)


def _unpack_rows(buf, shapes):
    flat = buf.reshape(-1)
    outs, off = [], 0
    for s in shapes:
        n = int(np.prod(s))
        outs.append(flat[off:off + n].reshape(s))
        off += -(-n // LANE) * LANE
    return outs


def _local_grads(x, p, target, wseg, w_br16, w_out16, w_ple16, b_gate, conv_w, conv_b, dt_bias, a_log, d_skip,
                 ssm_norm_w, ln_g, ln_b, rel_bias):
    nb, seq, _ = x.shape
    bmaps = jnp.asarray(_bucket_maps())
    bias = _bias_tables(rel_bias, bmaps)
    bgate8 = jnp.pad(b_gate, ((0, 5), (0, 0)))
    dils = [d for _, d in PATTERNS]

    qkv = [_proj(x, wseg["qkv%d" % g], dils[g], BF16, "proj_qkv%d" % g) for g in range(3)]
    nat = {s: _proj(x, wseg[s], 1, F32, "proj_" + s).reshape(nb, seq, -1)
           for s in ("gatt", "z", "xs", "bm", "cm", "dt", "gm", "gp")}
    att = [_attn_fwd(qkv[g], bias[g * GROUP_HEADS:(g + 1) * GROUP_HEADS], dils[g], "attn_fwd%d" % g) for g in range(3)]
    oa, o_att, lse = _combine_fwd([a[0] for a in att], [a[1] for a in att], nat["gatt"])

    cw = {"xs": (conv_w[:, :D_INNER], conv_b[:, :D_INNER]),
          "bm": (conv_w[:, D_INNER:D_INNER + 512], conv_b[:, D_INNER:D_INNER + 512]),
          "cm": (conv_w[:, D_INNER + 512:], conv_b[:, D_INNER + 512:])}
    act = {s: _conv_fwd(nat[s], cw[s][0], cw[s][1], "conv_fwd_" + s) for s in ("xs", "bm", "cm")}
    dt_sp, dt_sg = _softplus_sig(nat["dt"], jnp.pad(dt_bias, ((0, 0), (0, LANE - SSM_HEADS))))
    dtg, sgg = _group_lanes(dt_sp), _group_lanes(dt_sg)
    alog_g, dskip_g = _group_lanes(a_log), _group_lanes(d_skip)
    y_ssm, y_all, sprev = _ssd_fwd(act["xs"], act["bm"], act["cm"], dtg, nat["z"], alog_g, dskip_g, ssm_norm_w)

    w_bra, w_brb = w_br16[:ATT_OUT], w_br16[ATT_OUT:]
    y_a = _proj(oa, w_bra, 1, F32, "proj_ya").reshape(nb, seq, D_MODEL)
    y_b = _proj(y_ssm, w_brb, 1, F32, "proj_yb").reshape(nb, seq, D_MODEL)
    merged = _merge_fwd(y_a, y_b, nat["gm"], bgate8)
    mix = _proj(merged, w_out16, 1, F32, "proj_mix").reshape(nb, seq, D_MODEL)
    pw = _proj(p, w_ple16, 1, F32, "proj_ple").reshape(nb, seq, D_MODEL)

    dx, dpre16, dpw16, dgp16, ln_sums = _ln_loss(x, mix, nat["gp"], pw, target, bgate8, ln_g, ln_b)
    loss_sum = (0.5 / D_MODEL) * jnp.sum(ln_sums[3])
    dmerged = _dx(dpre16, w_out16, None, 1, "dx_merged")
    dya16, dyb16, dgm16, mg_sums = _merge_bwd(dmerged, y_a, y_b, nat["gm"], bgate8)
    doa = _dx(dya16, w_bra, None, 1, "dx_oa")
    dys = _dx(dyb16, w_brb, None, 1, "dx_yssm")
    g_w_out = _dw(merged, dpre16, 1, BF16, "dw_out")
    g_w_br = jnp.concatenate([_dw(oa, dya16, 1, BF16, "dw_bra"), _dw(y_ssm, dyb16, 1, BF16, "dw_brb")], axis=0)
    g_w_ple = _dw(p, dpw16, 1, BF16, "dw_ple")

    do_att, dgatt16 = _combine_bwd(doa, nat["gatt"], o_att)
    dseg = {"gatt": dgatt16.reshape(nb, 1, seq, ATT_OUT), "gm": dgm16, "gp": dgp16}
    dbias = []
    for g in range(3):
        dqkv, db = _attn_bwd(qkv[g], bias[g * GROUP_HEADS:(g + 1) * GROUP_HEADS], do_att, o_att, lse, dils[g],
                             "attn_bwd%d" % g)
        dseg["qkv%d" % g] = dqkv
        dbias.append(db)
    g_rel = _bias_grad(jnp.concatenate(dbias, axis=0), bmaps)[:, 0, :NUM_BUCKETS].T

    dxs, dbm, dcm, ddtg, dz, ssd_small, g_normw = _ssd_bwd(
        act["xs"], act["bm"], act["cm"], dtg, sgg, nat["z"], y_all, dys, sprev, alog_g, dskip_g, ssm_norm_w)
    dseg["z"] = dz.reshape(nb, 1, seq, D_INNER)
    ddt = _ungroup_lanes(ddtg)
    dseg["dt"] = jnp.pad(ddt, ((0, 0), (0, 0), (0, LANE - SSM_HEADS))).reshape(nb, 1, seq, LANE)
    conv_sums = {}
    for s, dact in (("xs", dxs), ("bm", dbm), ("cm", dcm)):
        dpre, conv_sums[s] = _conv_bwd_pre(dact, nat[s], cw[s][0], cw[s][1], "conv_bwd_" + s)
        dseg[s] = _conv_bwd_x(dpre, cw[s][0], "conv_bwd_x_" + s)
    csum = jnp.concatenate([conv_sums["xs"], conv_sums["bm"], conv_sums["cm"]], axis=1)

    dwseg = {}
    for name in ["qkv0", "qkv1", "qkv2", "gatt", "z", "xs", "bm", "cm", "dt", "gm", "gp"]:
        d = dils[int(name[3])] if name.startswith("qkv") else 1
        dx = _dx(dseg[name], wseg[name], dx, d, "dx_" + name)
        dwseg[name] = _dw(x, dseg[name], d, BF16, "dw_" + name)

    small = dict(
        b_gate=jnp.stack([mg_sums[0], mg_sums[1], ln_sums[2]]),
        conv_w=csum[0:4], conv_b=csum[4:5],
        dt_bias=_ungroup_lanes(ssd_small[:, 2:3, :]), a_log=_ungroup_lanes(ssd_small[:, 0:1, :]),
        d_skip=_ungroup_lanes(ssd_small[:, 1:2, :]), ssm_norm_w=g_normw,
        ln_g=ln_sums[0:1], ln_b=ln_sums[1:2], rel_bias=g_rel)
    return loss_sum, dx, dwseg, g_w_br, g_w_out, g_w_ple, small


SMALL_ORDER = ("b_gate", "conv_w", "conv_b", "dt_bias", "a_log", "d_skip", "ssm_norm_w", "ln_g", "ln_b", "rel_bias")
SMALL_FULL_SHAPES = dict(b_gate=(3, 1024), conv_w=(4, 3072), conv_b=(1, 3072), dt_bias=(1, 32), a_log=(1, 32),
                         d_skip=(1, 32), ssm_norm_w=(1, 2048), ln_g=(1, 1024), ln_b=(1, 1024), rel_bias=(32, 36))


def kernel(x, p, w_in, b_gate, conv_w, conv_b, dt_bias, a_log, d_skip, ssm_norm_w, w_branch, w_out, w_ple, ln_g, ln_b, rel_bias, loss_target, m_w_in, m_b_gate, m_conv_w, m_conv_b, m_dt_bias, m_a_log, m_d_skip, m_ssm_norm_w, m_w_branch, m_w_out, m_w_ple, m_ln_g, m_ln_b, m_rel_bias, v_w_in, v_b_gate, v_conv_w, v_conv_b, v_dt_bias, v_a_log, v_d_skip, v_ssm_norm_w, v_w_branch, v_w_out, v_w_ple, v_ln_g, v_ln_b, v_rel_bias):
    cx, cy, cc = _place()
    chip = 2 * cx + cy
    dev = 4 * cx + 2 * cy + cc

    win16 = _shard_to_window(w_in[0].astype(BF16), chip)
    g_win, g_br, g_out, g_ple = _allgather_pieces(
        [win16, w_branch[0].astype(BF16), w_out[0].astype(BF16), w_ple[0].astype(BF16)])
    wseg = _assemble(g_win)
    w_br16 = g_br.reshape(4 * 704, D_MODEL)
    w_out16 = g_out.reshape(D_MODEL, D_MODEL)
    w_ple16 = jnp.transpose(g_ple, (1, 0, 2)).reshape(PLE_DIM, D_MODEL)
    shards = _allgather8(_pack_rows([b_gate[0], conv_w[0]]), "allgather_small_params")
    per_chip = [_unpack_rows(shards[2 * k], [(3, 256), (4, 768)]) for k in range(4)]
    b_gate_full = jnp.concatenate([pc[0] for pc in per_chip], axis=1)
    conv_w_full = jnp.concatenate([pc[1] for pc in per_chip], axis=1)

    loss_sum, grad_x, dwseg, g_br, g_out, g_ple, small = _local_grads(
        x, p[0], loss_target, wseg, w_br16, w_out16, w_ple16, b_gate_full, conv_w_full, conv_b, dt_bias, a_log,
        d_skip, ssm_norm_w, ln_g, ln_b, rel_bias)
    loss = lax.psum(loss_sum, ("x", "y", "c"))

    big = _reduce_scatter([
        _pack(dwseg), g_br.reshape(4, 704, D_MODEL), g_out.reshape(4, 256, D_MODEL),
        jnp.transpose(g_ple.reshape(PLE_DIM, 4, 256), (1, 0, 2))])
    g_w_in = _window_to_shard(big[0], chip)
    g_w_branch, g_w_out, g_w_ple = big[1], big[2], big[3]
    parts = _allgather8(_pack_rows([small[n] for n in SMALL_ORDER]), "allgather_small_grads")
    small_sum = _sum_rows([parts[i] for i in range(8)], F32, "small_grad_sum")
    sg = dict(zip(SMALL_ORDER, _unpack_rows(small_sum, [SMALL_FULL_SHAPES[n] for n in SMALL_ORDER])))
    sg["b_gate"] = lax.dynamic_slice_in_dim(sg["b_gate"], chip * 256, 256, axis=1)
    sg["conv_w"] = lax.dynamic_slice_in_dim(sg["conv_w"], chip * 768, 768, axis=1)
    del dev

    upd = {}
    upd["w_in"] = _adamw(w_in[0], g_w_in, m_w_in[0], v_w_in[0], "adamw_w_in")
    upd["w_branch"] = _adamw(w_branch[0], g_w_branch, m_w_branch[0], v_w_branch[0], "adamw_w_branch")
    upd["w_out"] = _adamw(w_out[0], g_w_out, m_w_out[0], v_w_out[0], "adamw_w_out")
    upd["w_ple"] = _adamw(w_ple[0], g_w_ple, m_w_ple[0], v_w_ple[0], "adamw_w_ple")
    small_w = dict(b_gate=b_gate, conv_w=conv_w, conv_b=conv_b, dt_bias=dt_bias, a_log=a_log, d_skip=d_skip,
                   ssm_norm_w=ssm_norm_w, ln_g=ln_g, ln_b=ln_b, rel_bias=rel_bias)
    small_m = dict(b_gate=m_b_gate, conv_w=m_conv_w, conv_b=m_conv_b, dt_bias=m_dt_bias, a_log=m_a_log,
                   d_skip=m_d_skip, ssm_norm_w=m_ssm_norm_w, ln_g=m_ln_g, ln_b=m_ln_b, rel_bias=m_rel_bias)
    small_v = dict(b_gate=v_b_gate, conv_w=v_conv_w, conv_b=v_conv_b, dt_bias=v_dt_bias, a_log=v_a_log,
                   d_skip=v_d_skip, ssm_norm_w=v_ssm_norm_w, ln_g=v_ln_g, ln_b=v_ln_b, rel_bias=v_rel_bias)
    shapes = [small_w[n].shape for n in SMALL_ORDER]
    s_delta, s_m, s_v = _adamw(_pack_rows([small_w[n] for n in SMALL_ORDER]), _pack_rows([sg[n] for n in SMALL_ORDER]),
                               _pack_rows([small_m[n] for n in SMALL_ORDER]), _pack_rows([small_v[n] for n in SMALL_ORDER]),
                               "adamw_small")
    for i, n in enumerate(SMALL_ORDER):
        upd[n] = tuple(_unpack_rows(t, shapes)[i] for t in (s_delta, s_m, s_v))
        sg[n] = sg[n].reshape(small_w[n].shape)

    order = ("w_in", "b_gate", "conv_w", "conv_b", "dt_bias", "a_log", "d_skip", "ssm_norm_w", "w_branch", "w_out",
             "w_ple", "ln_g", "ln_b", "rel_bias")
    grads = dict(sg, w_in=g_w_in[None], w_branch=g_w_branch[None], w_out=g_w_out[None], w_ple=g_w_ple[None])
    lead = lambda n, t: t[None] if n in ("w_in", "w_branch", "w_out", "w_ple") else t
    return (loss, grad_x, *[grads[n] for n in order], *[lead(n, upd[n][0]) for n in order],
            *[lead(n, upd[n][1]) for n in order], *[lead(n, upd[n][2]) for n in order])
```

```python
import functools
import math

import numpy as np
import jax
import jax.numpy as jnp
from jax import lax
from jax.experimental import pallas as pl
from jax.experimental.pallas import tpu as pltpu

F32, BF16 = jnp.float32, jnp.bfloat16
HIGHEST = lax.Precision.HIGHEST

D_MODEL = 1024
HEAD_DIM = 64
GROUP_HEADS = 12
ATT_OUT = GROUP_HEADS * HEAD_DIM
PATTERNS = ((128, 1), (512, 4), (2048, 16))
BAND = 128
NUM_BUCKETS = 32
MAX_DISTANCE = 2048
D_INNER = 2048
SSM_HEADS = 32
SSM_GROUPS = 4
GROUP_SSM_HEADS = SSM_HEADS // SSM_GROUPS
D_STATE = 128
CHUNK = 128
PLE_DIM = 256
ALPHA = 2.0 ** 0.25
LN_EPS = 1e-5
RMS_EPS = 1e-5
ADAM_LR, ADAM_B1, ADAM_B2, ADAM_EPS, ADAM_WD, ADAM_STEP = 0.001, 0.9, 0.999, 1e-08, 0.01, 10
NEG = -1e30

QKV_W = 3 * ATT_OUT
IN_COLS = 15904
SHARD_COLS = IN_COLS // 4
DT_COL = 12800
DT_PAD = 96
INT_COLS = IN_COLS + DT_PAD
WIN_STRIDE = 31 * 128
WIN_COLS = 4096
SHARD3_SPLIT = DT_COL + 32 - 3 * SHARD_COLS

VMEM_LIMIT_BYTES = 56 * 1024 * 1024
LANE = 128
MESH = pl.DeviceIdType.MESH


def _pcall(body, **kw):
    return pl.pallas_call(body, **kw)


def _params(*sem):
    return pltpu.CompilerParams(dimension_semantics=sem, vmem_limit_bytes=VMEM_LIMIT_BYTES)


def _sigmoid(v):
    return jax.nn.sigmoid(v)


def _pick_tn(n):
    for t in (1024, 768, 512, 256, 128):
        if n % t == 0:
            return t
    raise ValueError(n)


def _token_tiling(seq, d):
    sub = seq // d
    tm = min(512, sub)
    res = max(1, min(d, 512 // sub))
    return sub, tm, res


def _proj(a3, w, d, out_dtype, name):
    nb, seq, kdim = a3.shape
    n = w.shape[1]
    sub, tm, res = _token_tiling(seq, d)
    tn = _pick_tn(n)

    def body(a_ref, w_ref, o_ref):
        for r in range(res):
            a = a_ref[:, r * kdim:(r + 1) * kdim].astype(BF16)
            o_ref[r] = jnp.dot(a, w_ref[...], preferred_element_type=F32).astype(out_dtype)

    return _pcall(
        body, name=name, grid=(n // tn, nb, d // res, sub // tm),
        in_specs=[pl.BlockSpec((None, tm, res * kdim), lambda j, b, rb, i: (b, i, rb)),
                  pl.BlockSpec((kdim, tn), lambda j, b, rb, i: (0, j))],
        out_specs=pl.BlockSpec((None, res, tm, tn), lambda j, b, rb, i: (b, rb, i, j)),
        out_shape=jax.ShapeDtypeStruct((nb, d, sub, n), out_dtype),
        compiler_params=_params("parallel", "parallel", "parallel", "parallel"),
    )(a3.reshape(nb, sub, d * kdim), w)


def _dx(dh4, w, acc3, d, name):
    nb, _, sub, wd = dh4.shape
    kout = w.shape[0]
    seq = sub * d
    _, tm, res = _token_tiling(seq, d)
    has_acc = acc3 is not None

    def body(*refs):
        dh_ref, w_ref = refs[0], refs[1]
        o_ref = refs[-1]
        for r in range(res):
            v = lax.dot_general(dh_ref[r].astype(BF16), w_ref[...], (((1,), (1,)), ((), ())),
                                preferred_element_type=F32)
            if has_acc:
                v = v + refs[2][:, r * kout:(r + 1) * kout]
            o_ref[:, r * kout:(r + 1) * kout] = v

    tok_spec = pl.BlockSpec((None, tm, res * kout), lambda b, rb, i: (b, i, rb))
    in_specs = [pl.BlockSpec((None, res, tm, wd), lambda b, rb, i: (b, rb, i, 0)),
                pl.BlockSpec((kout, wd), lambda b, rb, i: (0, 0))]
    args = [dh4, w]
    if has_acc:
        in_specs.append(tok_spec)
        args.append(acc3.reshape(nb, sub, d * kout))
    out = _pcall(
        body, name=name, grid=(nb, d // res, sub // tm), in_specs=in_specs, out_specs=tok_spec,
        out_shape=jax.ShapeDtypeStruct((nb, sub, d * kout), F32),
        input_output_aliases={2: 0} if has_acc else {},
        compiler_params=_params("parallel", "parallel", "parallel"),
    )(*args)
    return out.reshape(nb, seq, kout)


def _dw(a3, dh4, d, out_dtype, name):
    nb, seq, kdim = a3.shape
    n = dh4.shape[-1]
    sub, tm, res = _token_tiling(seq, d)
    tn = _pick_tn(n)
    grid = (n // tn, nb, d // res, sub // tm)

    def body(a_ref, dh_ref, o_ref, acc_ref):
        b, rb, i = pl.program_id(1), pl.program_id(2), pl.program_id(3)

        @pl.when((b == 0) & (rb == 0) & (i == 0))
        def _():
            acc_ref[...] = jnp.zeros_like(acc_ref)

        for r in range(res):
            a = a_ref[:, r * kdim:(r + 1) * kdim].astype(BF16)
            acc_ref[...] += lax.dot_general(a, dh_ref[r].astype(BF16), (((0,), (0,)), ((), ())),
                                            preferred_element_type=F32)

        @pl.when((b == grid[1] - 1) & (rb == grid[2] - 1) & (i == grid[3] - 1))
        def _():
            o_ref[...] = acc_ref[...].astype(out_dtype)

    return _pcall(
        body, name=name, grid=grid,
        in_specs=[pl.BlockSpec((None, tm, res * kdim), lambda j, b, rb, i: (b, i, rb)),
                  pl.BlockSpec((None, res, tm, tn), lambda j, b, rb, i: (b, rb, i, j))],
        out_specs=pl.BlockSpec((kdim, tn), lambda j, b, rb, i: (0, j)),
        out_shape=jax.ShapeDtypeStruct((kdim, n), out_dtype),
        scratch_shapes=[pltpu.VMEM((kdim, tn), F32)],
        compiler_params=_params("parallel", "arbitrary", "arbitrary", "arbitrary"),
    )(a3.reshape(nb, sub, d * kdim), dh4)


def _qkv_cols(g):
    cols = []
    for hp in range(ATT_OUT // LANE):
        for part in range(3):
            cols.append(part * QKV_W + g * ATT_OUT + hp * LANE)
    return cols


def _segments():
    segs = [("qkv%d" % g, [(c, LANE) for c in _qkv_cols(g)]) for g in range(3)]
    segs += [("gatt", [(3 * QKV_W, ATT_OUT)]),
             ("z", [(3 * QKV_W + ATT_OUT, D_INNER)]),
             ("xs", [(9728, D_INNER)]), ("bm", [(9728 + D_INNER, 512)]), ("cm", [(9728 + D_INNER + 512, 512)]),
             ("dt", [(DT_COL, LANE)]),
             ("gm", [(DT_COL + LANE, 2 * D_MODEL)]),
             ("gp", [(DT_COL + LANE + 2 * D_MODEL, D_MODEL)])]
    return segs


def _assemble(win):
    segs = _segments()
    tr = 128

    def body(win_ref, *outs):
        def cols(start, width):
            parts = []
            t = start
            while t < start + width:
                k = min(t // WIN_STRIDE, 3)
                nxt = min(start + width, (k + 1) * WIN_STRIDE if k < 3 else INT_COLS)
                if t % WIN_STRIDE == 0 and 0 < k and t // WIN_STRIDE == k:
                    tile = win_ref[k, :, 0:LANE] + win_ref[k - 1, :, WIN_STRIDE:WIN_STRIDE + LANE]
                    parts.append(tile)
                    t += LANE
                    continue
                lo = t - k * WIN_STRIDE
                parts.append(win_ref[k, :, lo:lo + (nxt - t)])
                t = nxt
            return parts

        for (_, pieces), o_ref in zip(segs, outs):
            off = 0
            for start, width in pieces:
                for part in cols(start, width):
                    o_ref[:, off:off + part.shape[1]] = part
                    off += part.shape[1]

    widths = [sum(w for _, w in pieces) for _, pieces in segs]
    outs = _pcall(
        body, name="assemble_w_in", grid=(D_MODEL // tr,),
        in_specs=[pl.BlockSpec((4, tr, WIN_COLS), lambda i: (0, i, 0))],
        out_specs=[pl.BlockSpec((tr, w), lambda i: (i, 0)) for w in widths],
        out_shape=[jax.ShapeDtypeStruct((D_MODEL, w), BF16) for w in widths],
        compiler_params=_params("parallel"),
    )(win)
    return {name: o for (name, _), o in zip(segs, outs)}


def _pack(dsegs):
    segs = _segments()
    tr = 128

    def body(*refs):
        ins, o_ref = refs[:-1], refs[-1]
        o_ref[...] = jnp.zeros_like(o_ref)
        for (_, pieces), s_ref in zip(segs, ins):
            off = 0
            for start, width in pieces:
                for k in range(4):
                    lo, hi = k * WIN_STRIDE, k * WIN_STRIDE + WIN_COLS
                    a, b = max(start, lo), min(start + width, hi)
                    if a < b:
                        o_ref[k, :, a - lo:b - lo] = s_ref[:, off + a - start:off + b - start]
                off += width

    widths = [sum(w for _, w in pieces) for _, pieces in segs]
    return _pcall(
        body, name="pack_dw_in", grid=(D_MODEL // tr,),
        in_specs=[pl.BlockSpec((tr, w), lambda i: (i, 0)) for w in widths],
        out_specs=pl.BlockSpec((4, tr, WIN_COLS), lambda i: (0, i, 0)),
        out_shape=jax.ShapeDtypeStruct((4, D_MODEL, WIN_COLS), BF16),
        compiler_params=_params("parallel"),
    )(*[dsegs[name] for name, _ in segs])


def _shard_to_window(w_shard, k):
    def plain(kk):
        return lambda w: jnp.pad(w, ((0, 0), (8 * kk, WIN_COLS - SHARD_COLS - 8 * kk)))

    def last(w):
        z = lambda n: jnp.zeros((w.shape[0], n), w.dtype)
        return jnp.concatenate([z(24), w[:, :SHARD3_SPLIT], z(DT_PAD), w[:, SHARD3_SPLIT:]], axis=1)

    return lax.switch(k, [plain(0), plain(1), plain(2), last], w_shard)


def _window_to_shard(win, k):
    def plain(kk):
        return lambda w: w[:, 8 * kk:8 * kk + SHARD_COLS]

    def last(w):
        return jnp.concatenate([w[:, 24:24 + SHARD3_SPLIT], w[:, 24 + SHARD3_SPLIT + DT_PAD:]], axis=1)

    return lax.switch(k, [plain(0), plain(1), plain(2), last], win)


def _bucket_maps():
    qi = np.arange(BAND)[:, None]
    kj = np.arange(2 * BAND)[None, :]
    delta = qi + BAND - kj
    maps = []
    for window, dil in PATTERNS:
        valid = (delta >= 0) & (delta <= window // dil)
        dist = np.maximum(delta, 0) * dil
        max_exact = NUM_BUCKETS // 2
        d_f = np.maximum(dist, 1).astype(np.float32)
        large = max_exact + (np.log(d_f / np.float32(max_exact)) / np.float32(math.log(MAX_DISTANCE / max_exact))
                             * np.float32(NUM_BUCKETS - max_exact)).astype(np.int32)
        large = np.minimum(large, NUM_BUCKETS - 1)
        bucket = np.where(dist < max_exact, dist, large)
        maps.append(np.where(valid, bucket, -1).astype(np.int32))
    return np.stack(maps)


def _bias_tables(rel_bias, bmaps):
    def body(rb_ref, bm_ref, o_ref):
        h = pl.program_id(0)
        bm = bm_ref[...]
        acc = jnp.full(bm.shape, NEG, F32)
        for b in range(NUM_BUCKETS):
            acc = jnp.where(bm == b, rb_ref[b, h], acc)
        o_ref[...] = acc

    return _pcall(
        body, name="bias_tables", grid=(3 * GROUP_HEADS,),
        in_specs=[pl.BlockSpec(memory_space=pltpu.SMEM),
                  pl.BlockSpec((None, BAND, 2 * BAND), lambda h: (h // GROUP_HEADS, 0, 0))],
        out_specs=pl.BlockSpec((None, BAND, 2 * BAND), lambda h: (h, 0, 0)),
        out_shape=jax.ShapeDtypeStruct((3 * GROUP_HEADS, BAND, 2 * BAND), F32),
        compiler_params=_params("parallel"),
    )(rel_bias, bmaps)


def _bias_grad(dbias, bmaps):
    def body(db_ref, bm_ref, o_ref):
        bm = bm_ref[...]
        db = db_ref[...]
        lane = lax.broadcasted_iota(jnp.int32, (1, LANE), 1)
        vec = jnp.zeros((1, LANE), F32)
        for b in range(NUM_BUCKETS):
            s = jnp.sum(jnp.where(bm == b, db, 0.0), keepdims=True)
            vec = jnp.where(lane == b, s, vec)
        o_ref[...] = vec

    return _pcall(
        body, name="bias_grad", grid=(3 * GROUP_HEADS,),
        in_specs=[pl.BlockSpec((None, BAND, 2 * BAND), lambda h: (h, 0, 0)),
                  pl.BlockSpec((None, BAND, 2 * BAND), lambda h: (h // GROUP_HEADS, 0, 0))],
        out_specs=pl.BlockSpec((None, 1, LANE), lambda h: (h, 0, 0)),
        out_shape=jax.ShapeDtypeStruct((3 * GROUP_HEADS, 1, LANE), F32),
        compiler_params=_params("parallel"),
    )(dbias, bmaps)


def _rows(n):
    if isinstance(n, int):
        return pl.ds(n * BAND, BAND)
    return pl.ds(pl.multiple_of(n * BAND, BAND), BAND)


def _for_blocks(blocks, nblk, per, carry):
    carry = blocks([0], carry, False)
    start = 1 + (nblk - 1) % per
    for n in range(1, start):
        carry = blocks([n], carry, True)
    trips = (nblk - start) // per
    if trips > 0:
        carry = lax.fori_loop(
            0, trips, lambda t, c: blocks([start + t * per + u for u in range(per)], c, True), carry)
    return carry


NT = (((1,), (1,)), ((), ()))
TN = (((0,), (0,)), ((), ()))


def _pairs_per_step(d):
    return {1: 1, 4: 3, 16: 6}[d]


def _head_cols(i, h, part):
    base = 3 * LANE * i + part * LANE + h * HEAD_DIM
    return slice(base, base + HEAD_DIM)


def _attn_fwd(qkv4, bias, d, name):
    nb, _, sub, _ = qkv4.shape
    nblk = sub // BAND
    scale = HEAD_DIM ** -0.5
    npair = ATT_OUT // LANE
    hps = _pairs_per_step(d)

    def body(qkv_ref, bias_ref, o_ref, l_ref):
        def blocks(ns, carry, with_prev):
            chains = [(bi, i, h) for bi in range(len(ns)) for i in range(hps) for h in range(2)]
            scores = []
            for bi, i, h in chains:
                n = ns[bi]
                q = qkv_ref[_rows(n), _head_cols(i, h, 0)]
                s_c = lax.dot_general(q, qkv_ref[_rows(n), _head_cols(i, h, 1)], NT,
                                      preferred_element_type=F32) * scale + bias_ref[2 * i + h, :, BAND:]
                s_p = None
                if with_prev:
                    s_p = lax.dot_general(q, qkv_ref[_rows(n - 1), _head_cols(i, h, 1)], NT,
                                          preferred_element_type=F32) * scale + bias_ref[2 * i + h, :, :BAND]
                scores.append((s_c, s_p))
            probs = []
            for s_c, s_p in scores:
                m = jnp.max(s_c, -1, keepdims=True)
                if with_prev:
                    m = jnp.maximum(m, jnp.max(s_p, -1, keepdims=True))
                e_c = jnp.exp(s_c - m)
                den = jnp.sum(e_c, -1, keepdims=True)
                e_p = None
                if with_prev:
                    e_p = jnp.exp(s_p - m)
                    den = den + jnp.sum(e_p, -1, keepdims=True)
                    e_p = e_p.astype(BF16)
                probs.append((e_c.astype(BF16), e_p, den, m))
            outs = {}
            for (bi, i, h), (e_c, e_p, den, m) in zip(chains, probs):
                n = ns[bi]
                acc = jnp.dot(e_c, qkv_ref[_rows(n), _head_cols(i, h, 2)], preferred_element_type=F32)
                if with_prev:
                    acc = acc + jnp.dot(e_p, qkv_ref[_rows(n - 1), _head_cols(i, h, 2)], preferred_element_type=F32)
                outs[(bi, i, h)] = (acc / den, jnp.broadcast_to(m + jnp.log(den), (BAND, HEAD_DIM)))
            for bi, n in enumerate(ns):
                for i in range(hps):
                    o_ref[_rows(n), i * LANE:(i + 1) * LANE] = jnp.concatenate(
                        [outs[(bi, i, 0)][0], outs[(bi, i, 1)][0]], axis=1)
                    l_ref[_rows(n), i * LANE:(i + 1) * LANE] = jnp.concatenate(
                        [outs[(bi, i, 0)][1], outs[(bi, i, 1)][1]], axis=1)
            return carry

        _for_blocks(blocks, nblk, 2 if hps == 1 else 1, 0)

    ospec = pl.BlockSpec((None, sub, hps * LANE), lambda hp, b, r: (b, 0, r * (npair // hps) + hp))
    o, l = _pcall(
        body, name=name, grid=(npair // hps, nb, d),
        in_specs=[pl.BlockSpec((None, None, sub, 3 * LANE * hps), lambda hp, b, r: (b, r, 0, hp)),
                  pl.BlockSpec((2 * hps, BAND, 2 * BAND), lambda hp, b, r: (hp, 0, 0))],
        out_specs=[ospec, ospec],
        out_shape=[jax.ShapeDtypeStruct((nb, sub, d * ATT_OUT), F32)] * 2,
        compiler_params=_params("parallel", "parallel", "parallel"),
    )(qkv4, bias)
    return o.reshape(nb, sub * d, ATT_OUT), l.reshape(nb, sub * d, ATT_OUT)


def _attn_bwd(qkv4, bias, do_att, o_att, lse, d, name):
    nb, _, sub, _ = qkv4.shape
    nblk = sub // BAND
    scale = HEAD_DIM ** -0.5
    npair = ATT_OUT // LANE
    hps = _pairs_per_step(d)

    def body(qkv_ref, bias_ref, do_ref, o_ref, l_ref, dqkv_ref, db_ref):
        b, r = pl.program_id(1), pl.program_id(2)

        @pl.when((b == 0) & (r == 0))
        def _():
            db_ref[...] = jnp.zeros_like(db_ref)

        def blocks(ns, carry, with_prev):
            sides = (0, 1) if with_prev else (0,)
            chains = [(bi, i, h, sd) for bi in range(len(ns)) for i in range(hps) for h in range(2) for sd in sides]
            key_rows = lambda bi, sd: _rows(ns[bi] - sd)
            qs = {}
            for bi in range(len(ns)):
                for i in range(hps):
                    for h in range(2):
                        hl = slice(i * LANE + h * HEAD_DIM, i * LANE + (h + 1) * HEAD_DIM)
                        do = do_ref[_rows(ns[bi]), hl]
                        qs[(bi, i, h)] = (
                            qkv_ref[_rows(ns[bi]), _head_cols(i, h, 0)], do.astype(BF16),
                            jnp.sum(do * o_ref[_rows(ns[bi]), hl], -1, keepdims=True),
                            l_ref[_rows(ns[bi]), i * LANE + h * HEAD_DIM:i * LANE + h * HEAD_DIM + 1])
            raw = []
            for bi, i, h, sd in chains:
                q, do16, _, _ = qs[(bi, i, h)]
                k = qkv_ref[key_rows(bi, sd), _head_cols(i, h, 1)]
                v = qkv_ref[key_rows(bi, sd), _head_cols(i, h, 2)]
                bias_blk = bias_ref[2 * i + h, :, :BAND] if sd else bias_ref[2 * i + h, :, BAND:]
                s = lax.dot_general(q, k, NT, preferred_element_type=F32) * scale + bias_blk
                dp = lax.dot_general(do16, v, NT, preferred_element_type=F32)
                raw.append((s, dp))
            soft = []
            for (bi, i, h, sd), (s, dp) in zip(chains, raw):
                _, _, ebar, lcol = qs[(bi, i, h)]
                p = jnp.exp(s - lcol)
                ds = p * (dp - ebar)
                if sd:
                    db_ref[2 * i + h, :, :BAND] += ds
                else:
                    db_ref[2 * i + h, :, BAND:] += ds
                soft.append((p.astype(BF16), ds.astype(BF16)))
            grads = {}
            for (bi, i, h, sd), (p16, ds16) in zip(chains, soft):
                q, do16, _, _ = qs[(bi, i, h)]
                k = qkv_ref[key_rows(bi, sd), _head_cols(i, h, 1)]
                grads[(bi, i, h, sd)] = (
                    jnp.dot(ds16, k, preferred_element_type=F32),
                    lax.dot_general(ds16, q, TN, preferred_element_type=F32) * scale,
                    lax.dot_general(p16, do16, TN, preferred_element_type=F32))
            both = lambda bi, i, sd, which: jnp.concatenate(
                [grads[(bi, i, 0, sd)][which], grads[(bi, i, 1, sd)][which]], axis=1)
            carry = list(carry) if carry is not None else None
            for bi, n in enumerate(ns):
                for i in range(hps):
                    base = 3 * LANE * i
                    dq = both(bi, i, 0, 0)
                    if with_prev:
                        dq = dq + both(bi, i, 1, 0)
                        dqkv_ref[_rows(n - 1), base + LANE:base + 2 * LANE] = (
                            carry[2 * i] + both(bi, i, 1, 1)).astype(BF16)
                        dqkv_ref[_rows(n - 1), base + 2 * LANE:base + 3 * LANE] = (
                            carry[2 * i + 1] + both(bi, i, 1, 2)).astype(BF16)
                    dqkv_ref[_rows(n), base:base + LANE] = (dq * scale).astype(BF16)
                carry = [t for i in range(hps) for t in (both(bi, i, 0, 1), both(bi, i, 0, 2))]
            return tuple(carry)

        carry = _for_blocks(blocks, nblk, 2 if hps == 1 else 1, None)
        for i in range(hps):
            base = 3 * LANE * i
            dqkv_ref[_rows(nblk - 1), base + LANE:base + 2 * LANE] = carry[2 * i].astype(BF16)
            dqkv_ref[_rows(nblk - 1), base + 2 * LANE:base + 3 * LANE] = carry[2 * i + 1].astype(BF16)

    nspec = pl.BlockSpec((None, sub, hps * LANE), lambda hp, b, r: (b, 0, r * (npair // hps) + hp))
    qspec = pl.BlockSpec((None, None, sub, 3 * LANE * hps), lambda hp, b, r: (b, r, 0, hp))
    bspec = pl.BlockSpec((2 * hps, BAND, 2 * BAND), lambda hp, b, r: (hp, 0, 0))
    view = lambda t: t.reshape(nb, sub, d * ATT_OUT)
    return _pcall(
        body, name=name, grid=(npair // hps, nb, d),
        in_specs=[qspec, bspec, nspec, nspec, nspec], out_specs=[qspec, bspec],
        out_shape=[jax.ShapeDtypeStruct(qkv4.shape, BF16),
                   jax.ShapeDtypeStruct((GROUP_HEADS, BAND, 2 * BAND), F32)],
        compiler_params=_params("parallel", "arbitrary", "arbitrary"),
    )(qkv4, bias, view(do_att), view(o_att), view(lse))


def _combine_fwd(os, ls, gatt):
    nb, seq, _ = gatt.shape
    tm = 512

    def body(o0, o1, o2, l0, l1, l2, g_ref, oa_ref, oatt_ref, lse_ref):
        m = jnp.maximum(jnp.maximum(l0[...], l1[...]), l2[...])
        tot = m + jnp.log(jnp.exp(l0[...] - m) + jnp.exp(l1[...] - m) + jnp.exp(l2[...] - m))
        o = (jnp.exp(l0[...] - tot) * o0[...] + jnp.exp(l1[...] - tot) * o1[...]
             + jnp.exp(l2[...] - tot) * o2[...])
        g = g_ref[...]
        oa_ref[...] = (o * (g * _sigmoid(g))).astype(BF16)
        oatt_ref[...] = o
        lse_ref[...] = tot

    spec = pl.BlockSpec((None, tm, ATT_OUT), lambda b, i: (b, i, 0))
    return _pcall(
        body, name="attn_combine", grid=(nb, seq // tm), in_specs=[spec] * 7, out_specs=[spec] * 3,
        out_shape=[jax.ShapeDtypeStruct((nb, seq, ATT_OUT), BF16), jax.ShapeDtypeStruct((nb, seq, ATT_OUT), F32),
                   jax.ShapeDtypeStruct((nb, seq, ATT_OUT), F32)],
        compiler_params=_params("parallel", "parallel"),
    )(*os, *ls, gatt)


def _combine_bwd(doa, gatt, o_att):
    nb, seq, _ = gatt.shape
    tm = 512

    def body(doa_ref, g_ref, o_ref, do_ref, dg_ref):
        g = g_ref[...]
        sg = _sigmoid(g)
        do_ref[...] = doa_ref[...] * (g * sg)
        dg_ref[...] = (doa_ref[...] * o_ref[...] * (sg * (1.0 + g * (1.0 - sg)))).astype(BF16)

    spec = pl.BlockSpec((None, tm, ATT_OUT), lambda b, i: (b, i, 0))
    return _pcall(
        body, name="attn_combine_bwd", grid=(nb, seq // tm), in_specs=[spec] * 3, out_specs=[spec] * 2,
        out_shape=[jax.ShapeDtypeStruct((nb, seq, ATT_OUT), F32), jax.ShapeDtypeStruct((nb, seq, ATT_OUT), BF16)],
        compiler_params=_params("parallel", "parallel"),
    )(doa, gatt, o_att)


CONV_TM = 512
CONV_TC = 512


def _shift_down(cur, halo, k):
    rolled = pltpu.roll(cur, k, 0)
    hro = pltpu.roll(halo, k, 0)
    row = lax.broadcasted_iota(jnp.int32, hro.shape, 0)
    return jnp.concatenate([jnp.where(row < k, hro, rolled[:8]), rolled[8:]], axis=0)


def _shift_up(cur, halo, k):
    n = cur.shape[0]
    rolled = pltpu.roll(cur, n - k, 0)
    hro = pltpu.roll(halo, 8 - k, 0)
    row = lax.broadcasted_iota(jnp.int32, hro.shape, 0)
    return jnp.concatenate([rolled[:n - 8], jnp.where(row >= 8 - k, hro, rolled[n - 8:])], axis=0)


def _conv_pre(cur, halo, w_ref, b_ref):
    acc = cur * w_ref[3:4, :] + b_ref[...]
    for k in range(1, 4):
        acc = acc + _shift_down(cur, halo, k) * w_ref[3 - k:4 - k, :]
    return acc


def _conv_specs(seq):
    nblk = seq // CONV_TM
    cur = pl.BlockSpec((None, CONV_TM, CONV_TC), lambda cb, b, i: (b, i, cb))
    prev = pl.BlockSpec((None, 8, CONV_TC), lambda cb, b, i: (b, jnp.maximum(i * (CONV_TM // 8) - 1, 0), cb))
    nxt = pl.BlockSpec((None, 8, CONV_TC),
                       lambda cb, b, i: (b, jnp.minimum((i + 1) * (CONV_TM // 8), seq // 8 - 1), cb))
    wspec = pl.BlockSpec((4, CONV_TC), lambda cb, b, i: (0, cb))
    bspec = pl.BlockSpec((1, CONV_TC), lambda cb, b, i: (0, cb))
    return nblk, cur, prev, nxt, wspec, bspec


def _conv_fwd(xin, w4, bias, name):
    nb, seq, ch = xin.shape
    _, cur, prev, _, wspec, bspec = _conv_specs(seq)

    def body(x_ref, h_ref, w_ref, b_ref, o_ref):
        halo = jnp.where(pl.program_id(2) > 0, h_ref[...], 0.0)
        pre = _conv_pre(x_ref[...], halo, w_ref, b_ref)
        o_ref[...] = pre * _sigmoid(pre)

    return _pcall(
        body, name=name, grid=(ch // CONV_TC, nb, seq // CONV_TM),
        in_specs=[cur, prev, wspec, bspec], out_specs=cur,
        out_shape=jax.ShapeDtypeStruct(xin.shape, F32),
        compiler_params=_params("parallel", "parallel", "parallel"),
    )(xin, xin, w4, bias)


def _conv_bwd_pre(dact, xin, w4, bias, name):
    nb, seq, ch = xin.shape
    _, cur, prev, _, wspec, bspec = _conv_specs(seq)

    def body(da_ref, x_ref, h_ref, w_ref, b_ref, dp_ref, s_ref):
        b, i = pl.program_id(1), pl.program_id(2)

        @pl.when((b == 0) & (i == 0))
        def _():
            s_ref[...] = jnp.zeros_like(s_ref)

        halo = jnp.where(i > 0, h_ref[...], 0.0)
        x = x_ref[...]
        pre = _conv_pre(x, halo, w_ref, b_ref)
        sg = _sigmoid(pre)
        dpre = da_ref[...] * (sg * (1.0 + pre * (1.0 - sg)))
        dp_ref[...] = dpre
        s_ref[3:4, :] += jnp.sum(dpre * x, 0, keepdims=True)
        for k in range(1, 4):
            s_ref[3 - k:4 - k, :] += jnp.sum(dpre * _shift_down(x, halo, k), 0, keepdims=True)
        s_ref[4:5, :] += jnp.sum(dpre, 0, keepdims=True)

    return _pcall(
        body, name=name, grid=(ch // CONV_TC, nb, seq // CONV_TM),
        in_specs=[cur, cur, prev, wspec, bspec],
        out_specs=[cur, pl.BlockSpec((8, CONV_TC), lambda cb, b, i: (0, cb))],
        out_shape=[jax.ShapeDtypeStruct(xin.shape, F32), jax.ShapeDtypeStruct((8, ch), F32)],
        compiler_params=_params("parallel", "arbitrary", "arbitrary"),
    )(dact, xin, xin, w4, bias)


def _conv_bwd_x(dpre, w4, name):
    nb, seq, ch = dpre.shape
    nblk, cur, _, nxt, wspec, _ = _conv_specs(seq)

    def body(d_ref, n_ref, w_ref, o_ref):
        halo = jnp.where(pl.program_id(2) < nblk - 1, n_ref[...], 0.0)
        cur_v = d_ref[...]
        acc = cur_v * w_ref[3:4, :]
        for j in range(1, 4):
            acc = acc + _shift_up(cur_v, halo, j) * w_ref[3 - j:4 - j, :]
        o_ref[...] = acc.astype(BF16)

    out = _pcall(
        body, name=name, grid=(ch // CONV_TC, nb, seq // CONV_TM),
        in_specs=[cur, nxt, wspec], out_specs=cur,
        out_shape=jax.ShapeDtypeStruct(dpre.shape, BF16),
        compiler_params=_params("parallel", "parallel", "parallel"),
    )(dpre, dpre, w4)
    return out.reshape(nb, 1, seq, ch)


def _softplus_sig(dt_raw, dt_bias_row):
    nb, seq, _ = dt_raw.shape
    tm = 512

    def body(r_ref, b_ref, sp_ref, sg_ref):
        v = r_ref[...] + b_ref[...]
        sp_ref[...] = jnp.maximum(v, 0.0) + jnp.log1p(jnp.exp(-jnp.abs(v)))
        sg_ref[...] = _sigmoid(v)

    spec = pl.BlockSpec((None, tm, LANE), lambda b, i: (b, i, 0))
    return _pcall(
        body, name="dt_softplus", grid=(nb, seq // tm),
        in_specs=[spec, pl.BlockSpec((1, LANE), lambda b, i: (0, 0))], out_specs=[spec, spec],
        out_shape=[jax.ShapeDtypeStruct(dt_raw.shape, F32)] * 2,
        compiler_params=_params("parallel", "parallel"),
    )(dt_raw, dt_bias_row)


def _group_lanes(t):
    pads = [(0, 0)] * (t.ndim - 1) + [(0, LANE - GROUP_SSM_HEADS)]
    return jnp.stack([jnp.pad(t[..., GROUP_SSM_HEADS * g:GROUP_SSM_HEADS * (g + 1)], pads) for g in range(SSM_GROUPS)])


def _ungroup_lanes(t):
    return jnp.concatenate([t[g][..., :GROUP_SSM_HEADS] for g in range(SSM_GROUPS)], axis=-1)


def _decays(dt, al_ref):
    row = lax.broadcasted_iota(jnp.int32, (CHUNK, CHUNK), 0)
    col = lax.broadcasted_iota(jnp.int32, (CHUNK, CHUNK), 1)
    tril = (row >= col).astype(F32)
    triu = (row <= col).astype(F32)
    arow = -jnp.exp(al_ref[...])
    a = dt * arow
    acs = jnp.dot(tril, a, precision=HIGHEST, preferred_element_type=F32)
    acs_t = lax.dot_general(a, triu, (((0,), (0,)), ((), ())), precision=HIGHEST, preferred_element_type=F32)
    return arow, acs, acs_t, row >= col, triu


def _ssd_specs(nb, seq):
    nc = seq // CHUNK
    hw = GROUP_SSM_HEADS * HEAD_DIM

    def mk(rev):
        cidx = (lambda c: nc - 1 - c) if rev else (lambda c: c)
        wide = pl.BlockSpec((None, CHUNK, hw), lambda g, b, c: (b, cidx(c), g))
        state = pl.BlockSpec((None, CHUNK, D_STATE), lambda g, b, c: (b, cidx(c), g))
        lanes = pl.BlockSpec((None, None, CHUNK, LANE), lambda g, b, c: (g, b, cidx(c), 0))
        prev = pl.BlockSpec((None, None, None, D_STATE, hw), lambda g, b, c: (b, cidx(c), g, 0, 0))
        return wide, state, lanes, prev

    grow = pl.BlockSpec((None, 1, LANE), lambda g, b, c: (g, 0, 0))
    nwspec = pl.BlockSpec((1, hw), lambda g, b, c: (0, g))
    return nc, hw, mk, grow, nwspec


def _head_expand():
    hw = GROUP_SSM_HEADS * HEAD_DIM
    r = lax.broadcasted_iota(jnp.int32, (LANE, hw), 0)
    c = lax.broadcasted_iota(jnp.int32, (LANE, hw), 1)
    return ((c // HEAD_DIM) == r).astype(BF16)


def _split3(v):
    hi = v.astype(BF16)
    rest = v - hi.astype(F32)
    mid = rest.astype(BF16)
    return hi, mid, (rest - mid.astype(F32)).astype(BF16)


def _to_channels(v, e):
    hi, mid, lo = _split3(v)
    dot = lambda t: jnp.dot(t, e, preferred_element_type=F32)
    return (dot(hi) + dot(mid)) + dot(lo)


def _to_heads(w, e):
    hi, mid, lo = _split3(w)
    dot = lambda t: lax.dot_general(t, e, (((1,), (1,)), ((), ())), preferred_element_type=F32)
    return (dot(hi) + dot(mid)) + dot(lo)


def _row8(v):
    return jnp.broadcast_to(v, (8, v.shape[1]))


def _ssd_chunk_setup(dt, al_ref, ds_ref):
    arow, acs, acs_t, causal, triu = _decays(dt, al_ref)
    e = _head_expand()
    dtx = _to_channels(dt, e)
    acsx = _to_channels(acs, e)
    lastx = acsx[CHUNK - 1:CHUNK, :]
    dskx = _to_channels(_row8(ds_ref[...]), e)[0:1, :]
    return arow, acs, acs_t, causal, triu, e, dtx, acsx, lastx, dskx


def _ssd_fwd(xs, bm, cm, dtg, z, alog_g, dskip_g, normw):
    nb, seq, _ = xs.shape
    nc, hw, mk, grow, nwspec = _ssd_specs(nb, seq)
    wide, state, lanes, prev = mk(False)
    tn = (((0,), (0,)), ((), ()))

    def body(xs_ref, b_ref, c_ref, dt_ref, z_ref, al_ref, ds_ref, nw_ref, ys_ref, y_ref, sp_ref, st_ref):
        @pl.when(pl.program_id(2) == 0)
        def _():
            st_ref[...] = jnp.zeros_like(st_ref)

        dt = dt_ref[...]
        _, acs, acs_t, causal, _, _, dtx, acsx, lastx, dskx = _ssd_chunk_setup(dt, al_ref, ds_ref)
        bmat = b_ref[...].astype(BF16)
        cmat = c_ref[...].astype(BF16)
        cb = lax.dot_general(cmat, bmat, (((1,), (1,)), ((), ())), preferred_element_type=F32)
        x = xs_ref[...]
        xdt = x * dtx
        xdt16 = xdt.astype(BF16)
        first_head = lax.broadcasted_iota(jnp.int32, (CHUNK, LANE), 1) < HEAD_DIM
        pairs = []
        for hp in range(GROUP_SSM_HEADS // 2):
            xp = xdt16[:, hp * LANE:(hp + 1) * LANE]
            two = []
            for j in (2 * hp, 2 * hp + 1):
                lmat = jnp.exp(jnp.where(causal, acs[:, j:j + 1] - acs_t[j:j + 1, :], -jnp.inf))
                two.append(jnp.dot((cb * lmat).astype(BF16), xp, preferred_element_type=F32))
            pairs.append(jnp.where(first_head, two[0], two[1]))
        yd = jnp.concatenate(pairs, axis=1)
        s_prev = st_ref[...]
        s16 = s_prev.astype(BF16)
        sp_ref[...] = s16
        yo = jnp.dot(cmat, s16, preferred_element_type=F32) * jnp.exp(acsx)
        sts = lax.dot_general(bmat, (xdt * jnp.exp(lastx - acsx)).astype(BF16), tn, preferred_element_type=F32)
        st_ref[...] = s_prev * jnp.exp(lastx) + sts
        y = yd + yo + dskx * x
        zz = z_ref[...]
        u = y * (zz * _sigmoid(zz))
        rn = lax.rsqrt(jnp.mean(u * u, -1, keepdims=True) + RMS_EPS)
        ys_ref[...] = (u * rn * nw_ref[...]).astype(BF16)
        y_ref[...] = y

    return _pcall(
        body, name="ssd_fwd", grid=(SSM_GROUPS, nb, nc),
        in_specs=[wide, state, state, lanes, wide, grow, grow, nwspec],
        out_specs=[wide, wide, prev],
        out_shape=[jax.ShapeDtypeStruct((nb, seq, D_INNER), BF16), jax.ShapeDtypeStruct((nb, seq, D_INNER), F32),
                   jax.ShapeDtypeStruct((nb, nc, SSM_GROUPS, D_STATE, hw), BF16)],
        scratch_shapes=[pltpu.VMEM((D_STATE, hw), F32)],
        compiler_params=_params("parallel", "parallel", "arbitrary"),
    )(xs, bm, cm, dtg, z, alog_g, dskip_g, normw)


def _ssd_bwd(xs, bm, cm, dtg, sgg, z, y, dys, sprev, alog_g, dskip_g, normw):
    nb, seq, _ = xs.shape
    nc, hw, mk, grow, nwspec = _ssd_specs(nb, seq)
    wide, state, lanes, prev = mk(True)
    nt = (((1,), (1,)), ((), ()))
    tn = (((0,), (0,)), ((), ()))

    def body(xs_ref, b_ref, c_ref, dt_ref, sg_ref, z_ref, y_ref, dys_ref, sp_ref, al_ref, ds_ref, nw_ref,
             dxs_ref, db_ref, dc_ref, ddt_ref, dz_ref, small_ref, dnw_ref, g_ref):
        b, c = pl.program_id(1), pl.program_id(2)

        @pl.when((b == 0) & (c == 0))
        def _():
            small_ref[...] = jnp.zeros_like(small_ref)
            dnw_ref[...] = jnp.zeros_like(dnw_ref)

        @pl.when(c == 0)
        def _():
            g_ref[...] = jnp.zeros_like(g_ref)

        yv, zz, dys_v, nw = y_ref[...], z_ref[...], dys_ref[...], nw_ref[...]
        sz = _sigmoid(zz)
        silu = zz * sz
        u = yv * silu
        rn = lax.rsqrt(jnp.mean(u * u, -1, keepdims=True) + RMS_EPS)
        gn = dys_v * nw
        du = rn * gn - u * (rn * rn * rn) * jnp.mean(u * gn, -1, keepdims=True)
        dnw_ref[...] += jnp.sum(dys_v * u * rn, 0, keepdims=True)
        dy = du * silu
        dz_ref[...] = du * yv * (sz * (1.0 + zz * (1.0 - sz)))

        dt = dt_ref[...]
        arow, acs, acs_t, causal, triu, e, dtx, acsx, lastx, dskx = _ssd_chunk_setup(dt, al_ref, ds_ref)
        dfsx = jnp.exp(acsx)
        dtex = jnp.exp(lastx - acsx)
        bmat = b_ref[...].astype(BF16)
        cmat = c_ref[...].astype(BF16)
        cb = lax.dot_general(cmat, bmat, nt, preferred_element_type=F32)
        x = xs_ref[...]
        xdt = x * dtx
        xdt16 = xdt.astype(BF16)
        xdte = xdt * dtex
        dy16 = dy.astype(BF16)
        dyd = dy * dfsx
        dyd16 = dyd.astype(BF16)
        s16 = sp_ref[...]
        g = g_ref[...]
        g16 = g.astype(BF16)
        cs = jnp.dot(cmat, s16, preferred_element_type=F32)
        dc_off = lax.dot_general(dyd16, s16, nt, preferred_element_type=F32)
        g_here = lax.dot_general(cmat, dyd16, tn, preferred_element_type=F32)
        bg = jnp.dot(bmat, g16, preferred_element_type=F32)
        db_st = lax.dot_general(xdte.astype(BF16), g16, nt, preferred_element_type=F32)
        ddte_w = bg * xdte
        dcd = _to_heads(_row8(jnp.sum(g * s16.astype(F32), 0, keepdims=True)), e)[0:1, :]
        lane = lax.broadcasted_iota(jnp.int32, (CHUNK, LANE), 1)
        first_head = lane < HEAD_DIM
        sub = lax.broadcasted_iota(jnp.int32, (CHUNK, LANE), 0)
        dacs = jnp.zeros((CHUNK, LANE), F32)
        colsums = jnp.zeros((CHUNK, LANE), F32)
        dcb = jnp.zeros((CHUNK, CHUNK), F32)
        pairs = []
        for hp in range(GROUP_SSM_HEADS // 2):
            xp = xdt16[:, hp * LANE:(hp + 1) * LANE]
            dyp = dy16[:, hp * LANE:(hp + 1) * LANE]
            two = []
            for idx, j in enumerate((2 * hp, 2 * hp + 1)):
                lmat = jnp.exp(jnp.where(causal, acs[:, j:j + 1] - acs_t[j:j + 1, :], -jnp.inf))
                mf = cb * lmat
                dy_h = jnp.where(first_head if idx == 0 else jnp.logical_not(first_head), dyp, jnp.zeros_like(dyp))
                dm = lax.dot_general(dy_h, xp, nt, preferred_element_type=F32)
                two.append(lax.dot_general(mf.astype(BF16), dyp, tn, preferred_element_type=F32))
                wmat = dm * mf
                dcb = dcb + dm * lmat
                dacs = jnp.where(lane == j, jnp.sum(wmat, -1, keepdims=True), dacs)
                colsums = jnp.where(sub == j, jnp.sum(wmat, 0, keepdims=True), colsums)
            pairs.append(jnp.where(first_head, two[0], two[1]))
        dxdt = bg * dtex + jnp.concatenate(pairs, axis=1)
        dacs = dacs - colsums.T + _to_heads(dyd * cs - ddte_w, e)
        cd_row = jnp.exp(acs[CHUNK - 1:CHUNK, :])
        tail = _to_heads(_row8(jnp.sum(ddte_w, 0, keepdims=True)), e)[0:1, :] + dcd * cd_row
        dacs = dacs + jnp.where(sub == CHUNK - 1, tail, 0.0)
        da = jnp.dot(triu, dacs, precision=HIGHEST, preferred_element_type=F32)
        ddt_raw = (da * arow + _to_heads(dxdt * x, e)) * sg_ref[...]
        ddt_ref[...] = ddt_raw
        small_ref[0:1, :] += jnp.sum(da * dt, 0, keepdims=True) * arow
        small_ref[1:2, :] += _to_heads(_row8(jnp.sum(dy * x, 0, keepdims=True)), e)[0:1, :]
        small_ref[2:3, :] += jnp.sum(ddt_raw, 0, keepdims=True)
        dcb16 = dcb.astype(BF16)
        dc_ref[...] = dc_off + jnp.dot(dcb16, bmat, preferred_element_type=F32)
        db_ref[...] = db_st + lax.dot_general(dcb16, cmat, tn, preferred_element_type=F32)
        dxs_ref[...] = dxdt * dtx + dskx * dy
        g_ref[...] = g * jnp.exp(lastx) + g_here

    return _pcall(
        body, name="ssd_bwd", grid=(SSM_GROUPS, nb, nc),
        in_specs=[wide, state, state, lanes, lanes, wide, wide, wide, prev, grow, grow, nwspec],
        out_specs=[wide, state, state, lanes, wide,
                   pl.BlockSpec((None, 8, LANE), lambda g, b, c: (g, 0, 0)), nwspec],
        out_shape=[jax.ShapeDtypeStruct((nb, seq, D_INNER), F32),
                   jax.ShapeDtypeStruct((nb, seq, SSM_GROUPS * D_STATE), F32),
                   jax.ShapeDtypeStruct((nb, seq, SSM_GROUPS * D_STATE), F32),
                   jax.ShapeDtypeStruct((SSM_GROUPS, nb, seq, LANE), F32),
                   jax.ShapeDtypeStruct((nb, seq, D_INNER), F32),
                   jax.ShapeDtypeStruct((SSM_GROUPS, 8, LANE), F32),
                   jax.ShapeDtypeStruct((1, D_INNER), F32)],
        scratch_shapes=[pltpu.VMEM((D_STATE, hw), F32)],
        compiler_params=_params("parallel", "arbitrary", "arbitrary"),
    )(xs, bm, cm, dtg, sgg, z, y, dys, sprev, alog_g, dskip_g, normw)


EW_TM = 256


def _merge_fwd(y_a, y_b, gm, bgate):
    nb, seq, _ = y_a.shape

    def body(a_ref, b_ref, ga_ref, gb_ref, bg_ref, o_ref):
        sa = _sigmoid(ga_ref[...] + bg_ref[0:1, :])
        sb = _sigmoid(gb_ref[...] + bg_ref[1:2, :])
        o_ref[...] = (sa * a_ref[...] + sb * b_ref[...]).astype(BF16)

    spec = pl.BlockSpec((None, EW_TM, D_MODEL), lambda b, i: (b, i, 0))
    spec1 = pl.BlockSpec((None, EW_TM, D_MODEL), lambda b, i: (b, i, 1))
    return _pcall(
        body, name="merge_fwd", grid=(nb, seq // EW_TM),
        in_specs=[spec, spec, spec, spec1, pl.BlockSpec((8, D_MODEL), lambda b, i: (0, 0))], out_specs=spec,
        out_shape=jax.ShapeDtypeStruct((nb, seq, D_MODEL), BF16),
        compiler_params=_params("parallel", "parallel"),
    )(y_a, y_b, gm, gm, bgate)


def _merge_bwd(dmerged, y_a, y_b, gm, bgate):
    nb, seq, _ = y_a.shape

    def body(dm_ref, a_ref, b_ref, ga_ref, gb_ref, bg_ref, dya_ref, dyb_ref, dg_ref, s_ref):
        @pl.when((pl.program_id(0) == 0) & (pl.program_id(1) == 0))
        def _():
            s_ref[...] = jnp.zeros_like(s_ref)

        dm = dm_ref[...]
        sa = _sigmoid(ga_ref[...] + bg_ref[0:1, :])
        sb = _sigmoid(gb_ref[...] + bg_ref[1:2, :])
        dya_ref[...] = (dm * sa).astype(BF16)
        dyb_ref[...] = (dm * sb).astype(BF16)
        dga = dm * a_ref[...] * (sa * (1.0 - sa))
        dgb = dm * b_ref[...] * (sb * (1.0 - sb))
        dg_ref[:, :D_MODEL] = dga.astype(BF16)
        dg_ref[:, D_MODEL:] = dgb.astype(BF16)
        s_ref[0:1, :] += jnp.sum(dga, 0, keepdims=True)
        s_ref[1:2, :] += jnp.sum(dgb, 0, keepdims=True)

    spec = pl.BlockSpec((None, EW_TM, D_MODEL), lambda b, i: (b, i, 0))
    spec1 = pl.BlockSpec((None, EW_TM, D_MODEL), lambda b, i: (b, i, 1))
    small = pl.BlockSpec((8, D_MODEL), lambda b, i: (0, 0))
    dya, dyb, dgm, sums = _pcall(
        body, name="merge_bwd", grid=(nb, seq // EW_TM),
        in_specs=[spec, spec, spec, spec, spec1, small],
        out_specs=[spec, spec, pl.BlockSpec((None, EW_TM, 2 * D_MODEL), lambda b, i: (b, i, 0)), small],
        out_shape=[jax.ShapeDtypeStruct((nb, seq, D_MODEL), BF16), jax.ShapeDtypeStruct((nb, seq, D_MODEL), BF16),
                   jax.ShapeDtypeStruct((nb, seq, 2 * D_MODEL), BF16), jax.ShapeDtypeStruct((8, D_MODEL), F32)],
        compiler_params=_params("arbitrary", "arbitrary"),
    )(dmerged, y_a, y_b, gm, gm, bgate)
    r4 = lambda t: t.reshape(nb, 1, seq, t.shape[-1])
    return r4(dya), r4(dyb), r4(dgm), sums


def _ln_loss(x, mix, gp, pw, target, bgate, ln_g, ln_b):
    nb, seq, _ = x.shape

    def body(x_ref, mix_ref, gp_ref, pw_ref, t_ref, bg_ref, g_ref, b_ref, dx_ref, dp_ref, dpw_ref, dgp_ref, s_ref):
        @pl.when((pl.program_id(0) == 0) & (pl.program_id(1) == 0))
        def _():
            s_ref[...] = jnp.zeros_like(s_ref)

        sp = _sigmoid(gp_ref[...] + bg_ref[2:3, :])
        pw = pw_ref[...]
        pre = ALPHA * x_ref[...] + mix_ref[...] + sp * pw
        mu = jnp.mean(pre, -1, keepdims=True)
        cen = pre - mu
        rstd = lax.rsqrt(jnp.mean(cen * cen, -1, keepdims=True) + LN_EPS)
        xhat = cen * rstd
        err = xhat * g_ref[...] + b_ref[...] - t_ref[...]
        dy = err * (1.0 / D_MODEL)
        dxh = dy * g_ref[...]
        dpre = rstd * (dxh - jnp.mean(dxh, -1, keepdims=True) - xhat * jnp.mean(dxh * xhat, -1, keepdims=True))
        dx_ref[...] = ALPHA * dpre
        dp_ref[...] = dpre.astype(BF16)
        dpw_ref[...] = (dpre * sp).astype(BF16)
        dgp = dpre * pw * (sp * (1.0 - sp))
        dgp_ref[...] = dgp.astype(BF16)
        s_ref[0:1, :] += jnp.sum(dy * xhat, 0, keepdims=True)
        s_ref[1:2, :] += jnp.sum(dy, 0, keepdims=True)
        s_ref[2:3, :] += jnp.sum(dgp, 0, keepdims=True)
        s_ref[3:4, :] += jnp.sum(err * err, 0, keepdims=True)

    spec = pl.BlockSpec((None, EW_TM, D_MODEL), lambda b, i: (b, i, 0))
    small = pl.BlockSpec((8, D_MODEL), lambda b, i: (0, 0))
    row = pl.BlockSpec((1, D_MODEL), lambda b, i: (0, 0))
    dx0, dpre16, dpw16, dgp16, sums = _pcall(
        body, name="ln_loss", grid=(nb, seq // EW_TM),
        in_specs=[spec] * 5 + [small, row, row], out_specs=[spec] * 4 + [small],
        out_shape=[jax.ShapeDtypeStruct((nb, seq, D_MODEL), F32)] + [jax.ShapeDtypeStruct((nb, seq, D_MODEL), BF16)] * 3
        + [jax.ShapeDtypeStruct((8, D_MODEL), F32)],
        compiler_params=_params("arbitrary", "arbitrary"),
    )(x, mix, gp, pw, target, bgate, ln_g, ln_b)
    r4 = lambda t: t.reshape(nb, 1, seq, D_MODEL)
    return dx0, r4(dpre16), r4(dpw16), r4(dgp16), sums


def _adamw(w, g, m, v, name):
    rows, cols = w.shape
    tr = rows
    for cand in range(8, rows, 8):
        if rows % cand == 0 and cand * cols * 4 <= (1 << 20):
            tr = cand
    if rows * cols * 4 <= (1 << 20):
        tr = rows
    c1 = 1.0 - ADAM_B1 ** ADAM_STEP
    c2 = 1.0 - ADAM_B2 ** ADAM_STEP

    def body(w_ref, g_ref, m_ref, v_ref, d_ref, nm_ref, nv_ref):
        gv = g_ref[...]
        nm = ADAM_B1 * m_ref[...] + (1.0 - ADAM_B1) * gv
        nv = ADAM_B2 * v_ref[...] + (1.0 - ADAM_B2) * (gv * gv)
        d_ref[...] = -ADAM_LR * ((nm / c1) / (jnp.sqrt(nv / c2) + ADAM_EPS) + ADAM_WD * w_ref[...])
        nm_ref[...] = nm
        nv_ref[...] = nv

    spec = pl.BlockSpec((tr, cols), lambda i: (i, 0))
    return _pcall(
        body, name=name, grid=(rows // tr,), in_specs=[spec] * 4, out_specs=[spec] * 3,
        out_shape=[jax.ShapeDtypeStruct(w.shape, F32)] * 3, compiler_params=_params("parallel"),
    )(w, g, m, v)


def _sum_rows(parts, out_dtype, name):
    rows, cols = parts[0].shape
    tr = rows
    for cand in range(16, rows, 16):
        if rows % cand == 0 and cand * cols * 4 <= (1 << 20):
            tr = cand
    n = len(parts)

    def body(*refs):
        acc = refs[0][...].astype(F32)
        for r in refs[1:n]:
            acc = acc + r[...].astype(F32)
        refs[n][...] = acc.astype(out_dtype)

    spec = pl.BlockSpec((tr, cols), lambda i: (i, 0))
    return _pcall(
        body, name=name, grid=(rows // tr,), in_specs=[spec] * n, out_specs=spec,
        out_shape=jax.ShapeDtypeStruct((rows, cols), out_dtype), compiler_params=_params("parallel"),
    )(*parts)


def _place():
    return lax.axis_index("x"), lax.axis_index("y"), lax.axis_index("c")


def _other_chips(x, y):
    return [(1 - x, y), (x, 1 - y), (1 - x, 1 - y)]


def _remote(src, dst, send_sem, recv_sem, to):
    return pltpu.make_async_remote_copy(src_ref=src, dst_ref=dst, send_sem=send_sem, recv_sem=recv_sem,
                                        device_id=to, device_id_type=MESH)


ANY = pl.BlockSpec(memory_space=pl.ANY)
DMA_CHUNK_BYTES = 512 * 1024


def _row_chunks(rows, row_bytes):
    per = max(16, DMA_CHUNK_BYTES // row_bytes // 16 * 16)
    return [(s, min(per, rows - s)) for s in range(0, rows, per)]


def _row_tile(rows, cols, align):
    best = None
    for cand in range(align, rows + 1, align):
        if rows % cand == 0 and cand * cols * 4 <= (1 << 20):
            best = cand
    return best or rows


def _allgather_pieces(pieces):
    n = len(pieces)
    halves = [_row_chunks(p.shape[0] // 2, p.shape[1] * p.dtype.itemsize) for p in pieces]
    wholes = [_row_chunks(p.shape[0], p.shape[1] * p.dtype.itemsize) for p in pieces]
    n_ici = 3 * sum(len(h) for h in halves)
    n_loc = sum(len(w) for w in wholes)

    def body(*refs):
        ins, outs = refs[:n], refs[n:2 * n]
        send_sems, recv_sems, local_sems = refs[2 * n:]
        x, y, c = _place()
        me = 2 * x + y
        sibling = (x, y, 1 - c)
        chips = _other_chips(x, y)
        locals_ = []
        for a in range(n):
            for s, m in wholes[a]:
                loc = pltpu.make_async_copy(ins[a].at[pl.ds(s, m)], outs[a].at[me, pl.ds(s, m)],
                                            local_sems.at[len(locals_)])
                loc.start()
                locals_.append(loc)
        ici = []
        for a in range(n):
            half = ins[a].shape[0] // 2
            for s, m in halves[a]:
                for j, (cx, cy) in enumerate(chips):
                    k = len(ici)
                    cp = _remote(ins[a].at[pl.ds(c * half + s, m)], outs[a].at[me, pl.ds(c * half + s, m)],
                                 send_sems.at[k], recv_sems.at[k], (cx, cy, c))
                    cp.start()
                    ici.append((a, s, m, j, cp))
        passed = []
        for k, (a, s, m, j, _) in enumerate(ici):
            half = ins[a].shape[0] // 2
            cx, cy = chips[j]
            blk = outs[a].at[2 * cx + cy, pl.ds(c * half + s, m)]
            _remote(blk, blk, send_sems.at[k], recv_sems.at[k], (cx, cy, c)).wait_recv()
            fw = _remote(blk, blk, send_sems.at[n_ici + k], recv_sems.at[n_ici + k], sibling)
            fw.start()
            passed.append(fw)
        for k, (a, s, m, j, _) in enumerate(ici):
            half = ins[a].shape[0] // 2
            cx, cy = chips[j]
            blk = outs[a].at[2 * cx + cy, pl.ds((1 - c) * half + s, m)]
            _remote(blk, blk, send_sems.at[n_ici + k], recv_sems.at[n_ici + k], sibling).wait_recv()
        for item in ici:
            item[4].wait_send()
        for fw in passed:
            fw.wait_send()
        for loc in locals_:
            loc.wait()

    return _pcall(
        body, name="allgather_weights", in_specs=[ANY] * n, out_specs=[ANY] * n,
        out_shape=[jax.ShapeDtypeStruct((4,) + p.shape, p.dtype) for p in pieces],
        scratch_shapes=[pltpu.SemaphoreType.DMA((2 * n_ici,)), pltpu.SemaphoreType.DMA((2 * n_ici,)),
                        pltpu.SemaphoreType.DMA((n_loc,))],
        compiler_params=pltpu.CompilerParams(has_side_effects=True),
    )(*pieces)


def _sibling_exchange(grads):
    n = len(grads)
    chunks = [_row_chunks(g.shape[1] // 2, g.shape[2] * g.dtype.itemsize) for g in grads]
    n_sem = 4 * sum(len(ch) for ch in chunks)

    def body(*refs):
        ins, gots = refs[:n], refs[n:2 * n]
        send_sems, recv_sems = refs[2 * n:]
        x, y, c = _place()
        sibling = (x, y, 1 - c)
        work = []
        for a in range(n):
            half = ins[a].shape[1] // 2
            for piece in range(4):
                for s, m in chunks[a]:
                    k = len(work)
                    cp = _remote(ins[a].at[piece, pl.ds((1 - c) * half + s, m)], gots[a].at[piece, pl.ds(s, m)],
                                 send_sems.at[k], recv_sems.at[k], sibling)
                    cp.start()
                    work.append(cp)
        for cp in work:
            cp.wait()

    return _pcall(
        body, name="grad_sibling_exchange", in_specs=[ANY] * n, out_specs=[ANY] * n,
        out_shape=[jax.ShapeDtypeStruct((4, g.shape[1] // 2, g.shape[2]), g.dtype) for g in grads],
        scratch_shapes=[pltpu.SemaphoreType.DMA((n_sem,)), pltpu.SemaphoreType.DMA((n_sem,))],
        compiler_params=pltpu.CompilerParams(has_side_effects=True),
    )(*grads)


def _chip_scatter(sums):
    n = len(sums)
    chunks = [_row_chunks(s.shape[1], s.shape[2] * s.dtype.itemsize) for s in sums]
    n_sem = 3 * sum(len(ch) for ch in chunks)

    def body(*refs):
        ins, gots = refs[:n], refs[n:2 * n]
        send_sems, recv_sems = refs[2 * n:]
        x, y, c = _place()
        chips = _other_chips(x, y)
        work = []
        for a in range(n):
            for s, m in chunks[a]:
                for j, (cx, cy) in enumerate(chips):
                    k = len(work)
                    cp = _remote(ins[a].at[2 * cx + cy, pl.ds(s, m)], gots[a].at[j, pl.ds(s, m)],
                                 send_sems.at[k], recv_sems.at[k], (cx, cy, c))
                    cp.start()
                    work.append(cp)
        for cp in work:
            cp.wait()

    return _pcall(
        body, name="grad_chip_scatter", in_specs=[ANY] * n, out_specs=[ANY] * n,
        out_shape=[jax.ShapeDtypeStruct((3,) + s.shape[1:], s.dtype) for s in sums],
        scratch_shapes=[pltpu.SemaphoreType.DMA((n_sem,)), pltpu.SemaphoreType.DMA((n_sem,))],
        compiler_params=pltpu.CompilerParams(has_side_effects=True),
    )(*sums)


def _sibling_gather(fulls):
    n = len(fulls)
    chunks = [_row_chunks(f.shape[0] // 2, f.shape[1] * f.dtype.itemsize) for f in fulls]
    n_sem = sum(len(ch) for ch in chunks)

    def body(*refs):
        outs = refs[n:2 * n]
        send_sems, recv_sems = refs[2 * n:]
        x, y, c = _place()
        sibling = (x, y, 1 - c)
        work = []
        for a in range(n):
            h = outs[a].shape[0] // 2
            for s, m in chunks[a]:
                k = len(work)
                mine = outs[a].at[pl.ds(c * h + s, m)]
                cp = _remote(mine, mine, send_sems.at[k], recv_sems.at[k], sibling)
                cp.start()
                work.append((a, s, m, cp))
        for k, (a, s, m, cp) in enumerate(work):
            h = outs[a].shape[0] // 2
            cp.wait_send()
            theirs = outs[a].at[pl.ds((1 - c) * h + s, m)]
            _remote(theirs, theirs, send_sems.at[k], recv_sems.at[k], sibling).wait_recv()

    return _pcall(
        body, name="grad_sibling_gather", in_specs=[ANY] * n, out_specs=[ANY] * n,
        out_shape=[jax.ShapeDtypeStruct(f.shape, f.dtype) for f in fulls],
        input_output_aliases={a: a for a in range(n)},
        scratch_shapes=[pltpu.SemaphoreType.DMA((n_sem,)), pltpu.SemaphoreType.DMA((n_sem,))],
        compiler_params=pltpu.CompilerParams(has_side_effects=True),
    )(*fulls)


def _pair_sum(grad, got, place, name):
    _, rows, cols = grad.shape
    half = rows // 2
    tr = _row_tile(half, cols, 16)

    def body(p_ref, a_ref, b_ref, o_ref):
        o_ref[...] = (a_ref[...].astype(F32) + b_ref[...].astype(F32)).astype(BF16)

    return _pcall(
        body, name=name,
        grid_spec=pltpu.PrefetchScalarGridSpec(
            num_scalar_prefetch=1, grid=(4, half // tr),
            in_specs=[pl.BlockSpec((None, tr, cols), lambda k, i, p: (k, p[1] * (half // tr) + i, 0)),
                      pl.BlockSpec((None, tr, cols), lambda k, i, p: (k, i, 0))],
            out_specs=pl.BlockSpec((None, tr, cols), lambda k, i, p: (k, i, 0))),
        out_shape=jax.ShapeDtypeStruct((4, half, cols), BF16),
        compiler_params=_params("parallel", "parallel"),
    )(place, grad, got)


def _chip_sum(sums, got, place, name):
    _, h, cols = sums.shape
    tr = _row_tile(h, cols, 16)

    def body(p_ref, own_ref, g0, g1, g2, o_ref):
        o_ref[...] = ((own_ref[...].astype(F32) + g0[...].astype(F32)) + g1[...].astype(F32)) + g2[...].astype(F32)

    gspec = lambda j: pl.BlockSpec((None, tr, cols), lambda i, p: (j, i, 0))
    return _pcall(
        body, name=name,
        grid_spec=pltpu.PrefetchScalarGridSpec(
            num_scalar_prefetch=1, grid=(h // tr,),
            in_specs=[pl.BlockSpec((None, tr, cols), lambda i, p: (p[0], i, 0)), gspec(0), gspec(1), gspec(2)],
            out_specs=pl.BlockSpec((tr, cols), lambda i, p: (p[1] * (h // tr) + i, 0))),
        out_shape=jax.ShapeDtypeStruct((2 * h, cols), F32),
        compiler_params=_params("parallel"),
    )(place, sums, got, got, got)


def _allgather8(buf, name):
    rows = buf.shape[0]

    def body(in_ref, out_ref, send_sems, recv_sems):
        x, y, c = _place()
        me = 4 * x + 2 * y + c
        out_ref[me] = in_ref[...]
        work = []
        for rel in range(1, 8):
            fx, fy, fc = (rel >> 2) & 1, (rel >> 1) & 1, rel & 1
            to = (x ^ fx, y ^ fy, c ^ fc)
            cp = _remote(in_ref, out_ref.at[me], send_sems.at[rel - 1], recv_sems.at[rel - 1], to)
            cp.start()
            work.append((cp, 4 * to[0] + 2 * to[1] + to[2]))
        for rel, (cp, frm) in enumerate(work):
            cp.wait_send()
            blk = out_ref.at[frm]
            _remote(blk, blk, send_sems.at[rel], recv_sems.at[rel], (x, y, c)).wait_recv()

    return _pcall(
        body, name=name, in_specs=[pl.BlockSpec(memory_space=pltpu.VMEM)],
        out_specs=pl.BlockSpec(memory_space=pltpu.VMEM),
        out_shape=jax.ShapeDtypeStruct((8, rows, LANE), F32),
        scratch_shapes=[pltpu.SemaphoreType.DMA((7,)), pltpu.SemaphoreType.DMA((7,))],
        compiler_params=pltpu.CompilerParams(has_side_effects=True),
    )(buf)


def _reduce_scatter(grads):
    x, y, c = _place()
    place = jnp.stack([2 * x + y, c]).astype(jnp.int32)
    got = _sibling_exchange(grads)
    chip_sums = [_pair_sum(g, t, place, "grad_pair_sum_%d" % i) for i, (g, t) in enumerate(zip(grads, got))]
    others = _chip_scatter(chip_sums)
    fulls = [_chip_sum(s, t, place, "grad_chip_sum_%d" % i) for i, (s, t) in enumerate(zip(chip_sums, others))]
    return _sibling_gather(fulls)


def _pack_rows(arrs):
    parts = []
    for a in arrs:
        f = a.reshape(-1).astype(F32)
        parts.append(jnp.pad(f, (0, (-f.shape[0]) % LANE)))
    flat = jnp.concatenate(parts)
    rows = -(-flat.shape[0] // LANE)
    rows8 = -(-rows // 8) * 8
    return jnp.pad(flat, (0, rows8 * LANE - flat.shape[0])).reshape(rows8, LANE)


def _unpack_rows(buf, shapes):
    flat = buf.reshape(-1)
    outs, off = [], 0
    for s in shapes:
        n = int(np.prod(s))
        outs.append(flat[off:off + n].reshape(s))
        off += -(-n // LANE) * LANE
    return outs


def _local_grads(x, p, target, wseg, w_br16, w_out16, w_ple16, b_gate, conv_w, conv_b, dt_bias, a_log, d_skip,
                 ssm_norm_w, ln_g, ln_b, rel_bias):
    nb, seq, _ = x.shape
    bmaps = jnp.asarray(_bucket_maps())
    bias = _bias_tables(rel_bias, bmaps)
    bgate8 = jnp.pad(b_gate, ((0, 5), (0, 0)))
    dils = [d for _, d in PATTERNS]

    qkv = [_proj(x, wseg["qkv%d" % g], dils[g], BF16, "proj_qkv%d" % g) for g in range(3)]
    nat = {s: _proj(x, wseg[s], 1, F32, "proj_" + s).reshape(nb, seq, -1)
           for s in ("gatt", "z", "xs", "bm", "cm", "dt", "gm", "gp")}
    att = [_attn_fwd(qkv[g], bias[g * GROUP_HEADS:(g + 1) * GROUP_HEADS], dils[g], "attn_fwd%d" % g) for g in range(3)]
    oa, o_att, lse = _combine_fwd([a[0] for a in att], [a[1] for a in att], nat["gatt"])

    cw = {"xs": (conv_w[:, :D_INNER], conv_b[:, :D_INNER]),
          "bm": (conv_w[:, D_INNER:D_INNER + 512], conv_b[:, D_INNER:D_INNER + 512]),
          "cm": (conv_w[:, D_INNER + 512:], conv_b[:, D_INNER + 512:])}
    act = {s: _conv_fwd(nat[s], cw[s][0], cw[s][1], "conv_fwd_" + s) for s in ("xs", "bm", "cm")}
    dt_sp, dt_sg = _softplus_sig(nat["dt"], jnp.pad(dt_bias, ((0, 0), (0, LANE - SSM_HEADS))))
    dtg, sgg = _group_lanes(dt_sp), _group_lanes(dt_sg)
    alog_g, dskip_g = _group_lanes(a_log), _group_lanes(d_skip)
    y_ssm, y_all, sprev = _ssd_fwd(act["xs"], act["bm"], act["cm"], dtg, nat["z"], alog_g, dskip_g, ssm_norm_w)

    w_bra, w_brb = w_br16[:ATT_OUT], w_br16[ATT_OUT:]
    y_a = _proj(oa, w_bra, 1, F32, "proj_ya").reshape(nb, seq, D_MODEL)
    y_b = _proj(y_ssm, w_brb, 1, F32, "proj_yb").reshape(nb, seq, D_MODEL)
    merged = _merge_fwd(y_a, y_b, nat["gm"], bgate8)
    mix = _proj(merged, w_out16, 1, F32, "proj_mix").reshape(nb, seq, D_MODEL)
    pw = _proj(p, w_ple16, 1, F32, "proj_ple").reshape(nb, seq, D_MODEL)

    dx, dpre16, dpw16, dgp16, ln_sums = _ln_loss(x, mix, nat["gp"], pw, target, bgate8, ln_g, ln_b)
    loss_sum = (0.5 / D_MODEL) * jnp.sum(ln_sums[3])
    dmerged = _dx(dpre16, w_out16, None, 1, "dx_merged")
    dya16, dyb16, dgm16, mg_sums = _merge_bwd(dmerged, y_a, y_b, nat["gm"], bgate8)
    doa = _dx(dya16, w_bra, None, 1, "dx_oa")
    dys = _dx(dyb16, w_brb, None, 1, "dx_yssm")
    g_w_out = _dw(merged, dpre16, 1, BF16, "dw_out")
    g_w_br = jnp.concatenate([_dw(oa, dya16, 1, BF16, "dw_bra"), _dw(y_ssm, dyb16, 1, BF16, "dw_brb")], axis=0)
    g_w_ple = _dw(p, dpw16, 1, BF16, "dw_ple")

    do_att, dgatt16 = _combine_bwd(doa, nat["gatt"], o_att)
    dseg = {"gatt": dgatt16.reshape(nb, 1, seq, ATT_OUT), "gm": dgm16, "gp": dgp16}
    dbias = []
    for g in range(3):
        dqkv, db = _attn_bwd(qkv[g], bias[g * GROUP_HEADS:(g + 1) * GROUP_HEADS], do_att, o_att, lse, dils[g],
                             "attn_bwd%d" % g)
        dseg["qkv%d" % g] = dqkv
        dbias.append(db)
    g_rel = _bias_grad(jnp.concatenate(dbias, axis=0), bmaps)[:, 0, :NUM_BUCKETS].T

    dxs, dbm, dcm, ddtg, dz, ssd_small, g_normw = _ssd_bwd(
        act["xs"], act["bm"], act["cm"], dtg, sgg, nat["z"], y_all, dys, sprev, alog_g, dskip_g, ssm_norm_w)
    dseg["z"] = dz.reshape(nb, 1, seq, D_INNER)
    ddt = _ungroup_lanes(ddtg)
    dseg["dt"] = jnp.pad(ddt, ((0, 0), (0, 0), (0, LANE - SSM_HEADS))).reshape(nb, 1, seq, LANE)
    conv_sums = {}
    for s, dact in (("xs", dxs), ("bm", dbm), ("cm", dcm)):
        dpre, conv_sums[s] = _conv_bwd_pre(dact, nat[s], cw[s][0], cw[s][1], "conv_bwd_" + s)
        dseg[s] = _conv_bwd_x(dpre, cw[s][0], "conv_bwd_x_" + s)
    csum = jnp.concatenate([conv_sums["xs"], conv_sums["bm"], conv_sums["cm"]], axis=1)

    dwseg = {}
    for name in ["qkv0", "qkv1", "qkv2", "gatt", "z", "xs", "bm", "cm", "dt", "gm", "gp"]:
        d = dils[int(name[3])] if name.startswith("qkv") else 1
        dx = _dx(dseg[name], wseg[name], dx, d, "dx_" + name)
        dwseg[name] = _dw(x, dseg[name], d, BF16, "dw_" + name)

    small = dict(
        b_gate=jnp.stack([mg_sums[0], mg_sums[1], ln_sums[2]]),
        conv_w=csum[0:4], conv_b=csum[4:5],
        dt_bias=_ungroup_lanes(ssd_small[:, 2:3, :]), a_log=_ungroup_lanes(ssd_small[:, 0:1, :]),
        d_skip=_ungroup_lanes(ssd_small[:, 1:2, :]), ssm_norm_w=g_normw,
        ln_g=ln_sums[0:1], ln_b=ln_sums[1:2], rel_bias=g_rel)
    return loss_sum, dx, dwseg, g_w_br, g_w_out, g_w_ple, small


SMALL_ORDER = ("b_gate", "conv_w", "conv_b", "dt_bias", "a_log", "d_skip", "ssm_norm_w", "ln_g", "ln_b", "rel_bias")
SMALL_FULL_SHAPES = dict(b_gate=(3, 1024), conv_w=(4, 3072), conv_b=(1, 3072), dt_bias=(1, 32), a_log=(1, 32),
                         d_skip=(1, 32), ssm_norm_w=(1, 2048), ln_g=(1, 1024), ln_b=(1, 1024), rel_bias=(32, 36))


def kernel(x, p, w_in, b_gate, conv_w, conv_b, dt_bias, a_log, d_skip, ssm_norm_w, w_branch, w_out, w_ple, ln_g, ln_b, rel_bias, loss_target, m_w_in, m_b_gate, m_conv_w, m_conv_b, m_dt_bias, m_a_log, m_d_skip, m_ssm_norm_w, m_w_branch, m_w_out, m_w_ple, m_ln_g, m_ln_b, m_rel_bias, v_w_in, v_b_gate, v_conv_w, v_conv_b, v_dt_bias, v_a_log, v_d_skip, v_ssm_norm_w, v_w_branch, v_w_out, v_w_ple, v_ln_g, v_ln_b, v_rel_bias):
    cx, cy, cc = _place()
    chip = 2 * cx + cy
    dev = 4 * cx + 2 * cy + cc

    win16 = _shard_to_window(w_in[0].astype(BF16), chip)
    g_win, g_br, g_out, g_ple = _allgather_pieces(
        [win16, w_branch[0].astype(BF16), w_out[0].astype(BF16), w_ple[0].astype(BF16)])
    wseg = _assemble(g_win)
    w_br16 = g_br.reshape(4 * 704, D_MODEL)
    w_out16 = g_out.reshape(D_MODEL, D_MODEL)
    w_ple16 = jnp.transpose(g_ple, (1, 0, 2)).reshape(PLE_DIM, D_MODEL)
    shards = _allgather8(_pack_rows([b_gate[0], conv_w[0]]), "allgather_small_params")
    per_chip = [_unpack_rows(shards[2 * k], [(3, 256), (4, 768)]) for k in range(4)]
    b_gate_full = jnp.concatenate([pc[0] for pc in per_chip], axis=1)
    conv_w_full = jnp.concatenate([pc[1] for pc in per_chip], axis=1)

    loss_sum, grad_x, dwseg, g_br, g_out, g_ple, small = _local_grads(
        x, p[0], loss_target, wseg, w_br16, w_out16, w_ple16, b_gate_full, conv_w_full, conv_b, dt_bias, a_log,
        d_skip, ssm_norm_w, ln_g, ln_b, rel_bias)
    loss = lax.psum(loss_sum, ("x", "y", "c"))

    big = _reduce_scatter([
        _pack(dwseg), g_br.reshape(4, 704, D_MODEL), g_out.reshape(4, 256, D_MODEL),
        jnp.transpose(g_ple.reshape(PLE_DIM, 4, 256), (1, 0, 2))])
    g_w_in = _window_to_shard(big[0], chip)
    g_w_branch, g_w_out, g_w_ple = big[1], big[2], big[3]
    parts = _allgather8(_pack_rows([small[n] for n in SMALL_ORDER]), "allgather_small_grads")
    small_sum = _sum_rows([parts[i] for i in range(8)], F32, "small_grad_sum")
    sg = dict(zip(SMALL_ORDER, _unpack_rows(small_sum, [SMALL_FULL_SHAPES[n] for n in SMALL_ORDER])))
    sg["b_gate"] = lax.dynamic_slice_in_dim(sg["b_gate"], chip * 256, 256, axis=1)
    sg["conv_w"] = lax.dynamic_slice_in_dim(sg["conv_w"], chip * 768, 768, axis=1)
    del dev

    upd = {}
    upd["w_in"] = _adamw(w_in[0], g_w_in, m_w_in[0], v_w_in[0], "adamw_w_in")
    upd["w_branch"] = _adamw(w_branch[0], g_w_branch, m_w_branch[0], v_w_branch[0], "adamw_w_branch")
    upd["w_out"] = _adamw(w_out[0], g_w_out, m_w_out[0], v_w_out[0], "adamw_w_out")
    upd["w_ple"] = _adamw(w_ple[0], g_w_ple, m_w_ple[0], v_w_ple[0], "adamw_w_ple")
    small_w = dict(b_gate=b_gate, conv_w=conv_w, conv_b=conv_b, dt_bias=dt_bias, a_log=a_log, d_skip=d_skip,
                   ssm_norm_w=ssm_norm_w, ln_g=ln_g, ln_b=ln_b, rel_bias=rel_bias)
    small_m = dict(b_gate=m_b_gate, conv_w=m_conv_w, conv_b=m_conv_b, dt_bias=m_dt_bias, a_log=m_a_log,
                   d_skip=m_d_skip, ssm_norm_w=m_ssm_norm_w, ln_g=m_ln_g, ln_b=m_ln_b, rel_bias=m_rel_bias)
    small_v = dict(b_gate=v_b_gate, conv_w=v_conv_w, conv_b=v_conv_b, dt_bias=v_dt_bias, a_log=v_a_log,
                   d_skip=v_d_skip, ssm_norm_w=v_ssm_norm_w, ln_g=v_ln_g, ln_b=v_ln_b, rel_bias=v_rel_bias)
    shapes = [small_w[n].shape for n in SMALL_ORDER]
    s_delta, s_m, s_v = _adamw(_pack_rows([small_w[n] for n in SMALL_ORDER]), _pack_rows([sg[n] for n in SMALL_ORDER]),
                               _pack_rows([small_m[n] for n in SMALL_ORDER]), _pack_rows([small_v[n] for n in SMALL_ORDER]),
                               "adamw_small")
    for i, n in enumerate(SMALL_ORDER):
        upd[n] = tuple(_unpack_rows(t, shapes)[i] for t in (s_delta, s_m, s_v))
        sg[n] = sg[n].reshape(small_w[n].shape)

    order = ("w_in", "b_gate", "conv_w", "conv_b", "dt_bias", "a_log", "d_skip", "ssm_norm_w", "w_branch", "w_out",
             "w_ple", "ln_g", "ln_b", "rel_bias")
    grads = dict(sg, w_in=g_w_in[None], w_branch=g_w_branch[None], w_out=g_w_out[None], w_ple=g_w_ple[None])
    lead = lambda n, t: t[None] if n in ("w_in", "w_branch", "w_out", "w_ple") else t
    return (loss, grad_x, *[grads[n] for n in order], *[lead(n, upd[n][0]) for n in order],
            *[lead(n, upd[n][1]) for n in order], *[lead(n, upd[n][2]) for n in order])
```

```python
import functools
import math

import numpy as np
import jax
import jax.numpy as jnp
from jax import lax
from jax.experimental import pallas as pl
from jax.experimental.pallas import tpu as pltpu

F32, BF16 = jnp.float32, jnp.bfloat16
HIGHEST = lax.Precision.HIGHEST

D_MODEL = 1024
HEAD_DIM = 64
GROUP_HEADS = 12
ATT_OUT = GROUP_HEADS * HEAD_DIM
PATTERNS = ((128, 1), (512, 4), (2048, 16))
BAND = 128
NUM_BUCKETS = 32
MAX_DISTANCE = 2048
D_INNER = 2048
SSM_HEADS = 32
SSM_GROUPS = 4
GROUP_SSM_HEADS = SSM_HEADS // SSM_GROUPS
D_STATE = 128
CHUNK = 128
PLE_DIM = 256
ALPHA = 2.0 ** 0.25
LN_EPS = 1e-5
RMS_EPS = 1e-5
ADAM_LR, ADAM_B1, ADAM_B2, ADAM_EPS, ADAM_WD, ADAM_STEP = 0.001, 0.9, 0.999, 1e-08, 0.01, 10
NEG = -1e30

QKV_W = 3 * ATT_OUT
IN_COLS = 15904
SHARD_COLS = IN_COLS // 4
DT_COL = 12800
DT_PAD = 96
INT_COLS = IN_COLS + DT_PAD
WIN_STRIDE = 31 * 128
WIN_COLS = 4096
SHARD3_SPLIT = DT_COL + 32 - 3 * SHARD_COLS

VMEM_LIMIT_BYTES = 56 * 1024 * 1024
LANE = 128
MESH = pl.DeviceIdType.MESH
NT = (((1,), (1,)), ((), ()))
TN = (((0,), (0,)), ((), ()))


def _pcall(body, **kw):
    return pl.pallas_call(body, **kw)


def _params(*sem):
    return pltpu.CompilerParams(dimension_semantics=sem, vmem_limit_bytes=VMEM_LIMIT_BYTES)


def _sigmoid(v):
    return jax.nn.sigmoid(v)


MM_TM = 512


def _permute(t, d):
    nb, seq, ch = t.shape
    return t if d == 1 else t.reshape(nb, seq // d, d, ch).transpose(0, 2, 1, 3).reshape(nb, seq, ch)


def _unpermute(t, d):
    nb, seq, ch = t.shape
    return t if d == 1 else t.reshape(nb, d, seq // d, ch).transpose(0, 2, 1, 3).reshape(nb, seq, ch)


def _proj(a3, w, out_dtype, name):
    nb, seq, kdim = a3.shape
    n = w.shape[1]

    def body(a_ref, w_ref, o_ref):
        o_ref[...] = jnp.dot(a_ref[...].astype(BF16), w_ref[...], preferred_element_type=F32).astype(out_dtype)

    return _pcall(
        body, name=name, grid=(nb, seq // MM_TM),
        in_specs=[pl.BlockSpec((None, MM_TM, kdim), lambda b, i: (b, i, 0)),
                  pl.BlockSpec((kdim, n), lambda b, i: (0, 0))],
        out_specs=pl.BlockSpec((None, MM_TM, n), lambda b, i: (b, i, 0)),
        out_shape=jax.ShapeDtypeStruct((nb, seq, n), out_dtype),
        compiler_params=_params("parallel", "parallel"),
    )(a3, w)


def _dx(dh3, w, accs, name):
    nb, seq, wd = dh3.shape
    kout = w.shape[0]
    nacc = len(accs)

    def body(*refs):
        v = lax.dot_general(refs[0][...].astype(BF16), refs[1][...], NT, preferred_element_type=F32)
        for a_ref in refs[2:2 + nacc]:
            v = v + a_ref[...]
        refs[-1][...] = v

    tok_spec = pl.BlockSpec((None, MM_TM, kout), lambda b, i: (b, i, 0))
    return _pcall(
        body, name=name, grid=(nb, seq // MM_TM),
        in_specs=[pl.BlockSpec((None, MM_TM, wd), lambda b, i: (b, i, 0)),
                  pl.BlockSpec((kout, wd), lambda b, i: (0, 0))] + [tok_spec] * nacc,
        out_specs=tok_spec, out_shape=jax.ShapeDtypeStruct((nb, seq, kout), F32),
        input_output_aliases={2: 0} if nacc else {},
        compiler_params=_params("parallel", "parallel"),
    )(dh3, w, *accs)


def _dw(a3, dh3, out_dtype, name):
    nb, seq, kdim = a3.shape
    n = dh3.shape[-1]
    grid = (nb, seq // MM_TM)

    def body(a_ref, dh_ref, o_ref, acc_ref):
        b, i = pl.program_id(0), pl.program_id(1)

        @pl.when((b == 0) & (i == 0))
        def _():
            acc_ref[...] = jnp.zeros_like(acc_ref)

        acc_ref[...] += lax.dot_general(a_ref[...].astype(BF16), dh_ref[...].astype(BF16), TN,
                                        preferred_element_type=F32)

        @pl.when((b == grid[0] - 1) & (i == grid[1] - 1))
        def _():
            o_ref[...] = acc_ref[...].astype(out_dtype)

    return _pcall(
        body, name=name, grid=grid,
        in_specs=[pl.BlockSpec((None, MM_TM, kdim), lambda b, i: (b, i, 0)),
                  pl.BlockSpec((None, MM_TM, n), lambda b, i: (b, i, 0))],
        out_specs=pl.BlockSpec((kdim, n), lambda b, i: (0, 0)),
        out_shape=jax.ShapeDtypeStruct((kdim, n), out_dtype),
        scratch_shapes=[pltpu.VMEM((kdim, n), F32)],
        compiler_params=_params("arbitrary", "arbitrary"),
    )(a3, dh3)


def _qkv_cols(g):
    cols = []
    for hp in range(ATT_OUT // LANE):
        for part in range(3):
            cols.append(part * QKV_W + g * ATT_OUT + hp * LANE)
    return cols


def _segments():
    segs = [("qkv%d" % g, [(c, LANE) for c in _qkv_cols(g)]) for g in range(3)]
    segs += [("gatt", [(3 * QKV_W, ATT_OUT)]),
             ("z", [(3 * QKV_W + ATT_OUT, D_INNER)]),
             ("xs", [(9728, D_INNER)]), ("bm", [(9728 + D_INNER, 512)]), ("cm", [(9728 + D_INNER + 512, 512)]),
             ("dt", [(DT_COL, LANE)]),
             ("gm", [(DT_COL + LANE, 2 * D_MODEL)]),
             ("gp", [(DT_COL + LANE + 2 * D_MODEL, D_MODEL)])]
    return segs


def _assemble(win):
    segs = _segments()
    tr = 128

    def body(win_ref, *outs):
        def cols(start, width):
            parts = []
            t = start
            while t < start + width:
                k = min(t // WIN_STRIDE, 3)
                nxt = min(start + width, (k + 1) * WIN_STRIDE if k < 3 else INT_COLS)
                if t % WIN_STRIDE == 0 and 0 < k and t // WIN_STRIDE == k:
                    tile = win_ref[k, :, 0:LANE] + win_ref[k - 1, :, WIN_STRIDE:WIN_STRIDE + LANE]
                    parts.append(tile)
                    t += LANE
                    continue
                lo = t - k * WIN_STRIDE
                parts.append(win_ref[k, :, lo:lo + (nxt - t)])
                t = nxt
            return parts

        for (_, pieces), o_ref in zip(segs, outs):
            off = 0
            for start, width in pieces:
                for part in cols(start, width):
                    o_ref[:, off:off + part.shape[1]] = part
                    off += part.shape[1]

    widths = [sum(w for _, w in pieces) for _, pieces in segs]
    outs = _pcall(
        body, name="assemble_w_in", grid=(D_MODEL // tr,),
        in_specs=[pl.BlockSpec((4, tr, WIN_COLS), lambda i: (0, i, 0))],
        out_specs=[pl.BlockSpec((tr, w), lambda i: (i, 0)) for w in widths],
        out_shape=[jax.ShapeDtypeStruct((D_MODEL, w), BF16) for w in widths],
        compiler_params=_params("parallel"),
    )(win)
    return {name: o for (name, _), o in zip(segs, outs)}


def _pack(dsegs):
    segs = _segments()
    tr = 128

    def body(*refs):
        ins, o_ref = refs[:-1], refs[-1]
        o_ref[...] = jnp.zeros_like(o_ref)
        for (_, pieces), s_ref in zip(segs, ins):
            off = 0
            for start, width in pieces:
                for k in range(4):
                    lo, hi = k * WIN_STRIDE, k * WIN_STRIDE + WIN_COLS
                    a, b = max(start, lo), min(start + width, hi)
                    if a < b:
                        o_ref[k, :, a - lo:b - lo] = s_ref[:, off + a - start:off + b - start]
                off += width

    widths = [sum(w for _, w in pieces) for _, pieces in segs]
    return _pcall(
        body, name="pack_dw_in", grid=(D_MODEL // tr,),
        in_specs=[pl.BlockSpec((tr, w), lambda i: (i, 0)) for w in widths],
        out_specs=pl.BlockSpec((4, tr, WIN_COLS), lambda i: (0, i, 0)),
        out_shape=jax.ShapeDtypeStruct((4, D_MODEL, WIN_COLS), BF16),
        compiler_params=_params("parallel"),
    )(*[dsegs[name] for name, _ in segs])


def _shard_to_window(w_shard, k):
    def plain(kk):
        return lambda w: jnp.pad(w, ((0, 0), (8 * kk, WIN_COLS - SHARD_COLS - 8 * kk)))

    def last(w):
        z = lambda n: jnp.zeros((w.shape[0], n), w.dtype)
        return jnp.concatenate([z(24), w[:, :SHARD3_SPLIT], z(DT_PAD), w[:, SHARD3_SPLIT:]], axis=1)

    return lax.switch(k, [plain(0), plain(1), plain(2), last], w_shard)


def _window_to_shard(win, k):
    def plain(kk):
        return lambda w: w[:, 8 * kk:8 * kk + SHARD_COLS]

    def last(w):
        return jnp.concatenate([w[:, 24:24 + SHARD3_SPLIT], w[:, 24 + SHARD3_SPLIT + DT_PAD:]], axis=1)

    return lax.switch(k, [plain(0), plain(1), plain(2), last], win)


def _bucket_maps():
    qi = np.arange(BAND)[:, None]
    kj = np.arange(2 * BAND)[None, :]
    delta = qi + BAND - kj
    maps = []
    for window, dil in PATTERNS:
        valid = (delta >= 0) & (delta <= window // dil)
        dist = np.maximum(delta, 0) * dil
        max_exact = NUM_BUCKETS // 2
        d_f = np.maximum(dist, 1).astype(np.float32)
        large = max_exact + (np.log(d_f / np.float32(max_exact)) / np.float32(math.log(MAX_DISTANCE / max_exact))
                             * np.float32(NUM_BUCKETS - max_exact)).astype(np.int32)
        large = np.minimum(large, NUM_BUCKETS - 1)
        bucket = np.where(dist < max_exact, dist, large)
        maps.append(np.where(valid, bucket, -1).astype(np.int32))
    return np.stack(maps)


def _bias_tables(rel_bias, bmaps):
    def body(rb_ref, bm_ref, o_ref):
        h = pl.program_id(0)
        bm = bm_ref[...]
        acc = jnp.full(bm.shape, NEG, F32)
        for b in range(NUM_BUCKETS):
            acc = jnp.where(bm == b, rb_ref[b, h], acc)
        o_ref[...] = acc

    return _pcall(
        body, name="bias_tables", grid=(3 * GROUP_HEADS,),
        in_specs=[pl.BlockSpec(memory_space=pltpu.SMEM),
                  pl.BlockSpec((None, BAND, 2 * BAND), lambda h: (h // GROUP_HEADS, 0, 0))],
        out_specs=pl.BlockSpec((None, BAND, 2 * BAND), lambda h: (h, 0, 0)),
        out_shape=jax.ShapeDtypeStruct((3 * GROUP_HEADS, BAND, 2 * BAND), F32),
        compiler_params=_params("parallel"),
    )(rel_bias, bmaps)


def _bias_grad(dbias, bmaps):
    def body(db_ref, bm_ref, o_ref):
        bm = bm_ref[...]
        db = db_ref[...]
        lane = lax.broadcasted_iota(jnp.int32, (1, LANE), 1)
        vec = jnp.zeros((1, LANE), F32)
        for b in range(NUM_BUCKETS):
            s = jnp.sum(jnp.where(bm == b, db, 0.0), keepdims=True)
            vec = jnp.where(lane == b, s, vec)
        o_ref[...] = vec

    return _pcall(
        body, name="bias_grad", grid=(3 * GROUP_HEADS,),
        in_specs=[pl.BlockSpec((None, BAND, 2 * BAND), lambda h: (h, 0, 0)),
                  pl.BlockSpec((None, BAND, 2 * BAND), lambda h: (h // GROUP_HEADS, 0, 0))],
        out_specs=pl.BlockSpec((None, 1, LANE), lambda h: (h, 0, 0)),
        out_shape=jax.ShapeDtypeStruct((3 * GROUP_HEADS, 1, LANE), F32),
        compiler_params=_params("parallel"),
    )(dbias, bmaps)


def _rows(n):
    if isinstance(n, int):
        return pl.ds(n * BAND, BAND)
    return pl.ds(pl.multiple_of(n * BAND, BAND), BAND)


def _for_blocks(blocks, nblk, per, carry):
    carry = blocks([0], carry, False)
    start = 1 + (nblk - 1) % per
    for n in range(1, start):
        carry = blocks([n], carry, True)
    trips = (nblk - start) // per
    if trips > 0:
        carry = lax.fori_loop(
            0, trips, lambda t, c: blocks([start + t * per + u for u in range(per)], c, True), carry)
    return carry


def _pairs_per_step(d):
    return {1: 1, 4: 3, 16: 6}[d]


def _head_cols(i, h, part):
    base = 3 * LANE * i + part * LANE + h * HEAD_DIM
    return slice(base, base + HEAD_DIM)


def _attn_fwd(qkv4, bias, d, name):
    nb, _, sub, _ = qkv4.shape
    nblk = sub // BAND
    scale = HEAD_DIM ** -0.5
    npair = ATT_OUT // LANE
    hps = _pairs_per_step(d)

    def body(qkv_ref, bias_ref, o_ref, l_ref):
        def blocks(ns, carry, with_prev):
            chains = [(bi, i, h) for bi in range(len(ns)) for i in range(hps) for h in range(2)]
            scores = []
            for bi, i, h in chains:
                n = ns[bi]
                q = qkv_ref[_rows(n), _head_cols(i, h, 0)]
                s_c = lax.dot_general(q, qkv_ref[_rows(n), _head_cols(i, h, 1)], NT,
                                      preferred_element_type=F32) * scale + bias_ref[2 * i + h, :, BAND:]
                s_p = None
                if with_prev:
                    s_p = lax.dot_general(q, qkv_ref[_rows(n - 1), _head_cols(i, h, 1)], NT,
                                          preferred_element_type=F32) * scale + bias_ref[2 * i + h, :, :BAND]
                scores.append((s_c, s_p))
            probs = []
            for s_c, s_p in scores:
                m = jnp.max(s_c, -1, keepdims=True)
                if with_prev:
                    m = jnp.maximum(m, jnp.max(s_p, -1, keepdims=True))
                e_c = jnp.exp(s_c - m)
                den = jnp.sum(e_c, -1, keepdims=True)
                e_p = None
                if with_prev:
                    e_p = jnp.exp(s_p - m)
                    den = den + jnp.sum(e_p, -1, keepdims=True)
                    e_p = e_p.astype(BF16)
                probs.append((e_c.astype(BF16), e_p, den, m))
            outs = {}
            for (bi, i, h), (e_c, e_p, den, m) in zip(chains, probs):
                n = ns[bi]
                acc = jnp.dot(e_c, qkv_ref[_rows(n), _head_cols(i, h, 2)], preferred_element_type=F32)
                if with_prev:
                    acc = acc + jnp.dot(e_p, qkv_ref[_rows(n - 1), _head_cols(i, h, 2)], preferred_element_type=F32)
                outs[(bi, i, h)] = (acc / den, jnp.broadcast_to(m + jnp.log(den), (BAND, HEAD_DIM)))
            for bi, n in enumerate(ns):
                for i in range(hps):
                    o_ref[_rows(n), i * LANE:(i + 1) * LANE] = jnp.concatenate(
                        [outs[(bi, i, 0)][0], outs[(bi, i, 1)][0]], axis=1)
                    l_ref[_rows(n), i * LANE:(i + 1) * LANE] = jnp.concatenate(
                        [outs[(bi, i, 0)][1], outs[(bi, i, 1)][1]], axis=1)
            return carry

        _for_blocks(blocks, nblk, 2 if hps == 1 else 1, 0)

    ospec = pl.BlockSpec((None, sub, hps * LANE), lambda hp, b, r: (b, 0, r * (npair // hps) + hp))
    o, l = _pcall(
        body, name=name, grid=(npair // hps, nb, d),
        in_specs=[pl.BlockSpec((None, None, sub, 3 * LANE * hps), lambda hp, b, r: (b, r, 0, hp)),
                  pl.BlockSpec((2 * hps, BAND, 2 * BAND), lambda hp, b, r: (hp, 0, 0))],
        out_specs=[ospec, ospec],
        out_shape=[jax.ShapeDtypeStruct((nb, sub, d * ATT_OUT), F32)] * 2,
        compiler_params=_params("parallel", "parallel", "parallel"),
    )(qkv4, bias)
    return o.reshape(nb, sub * d, ATT_OUT), l.reshape(nb, sub * d, ATT_OUT)


def _attn_bwd(qkv4, bias, do_att, o_att, lse, d, name):
    nb, _, sub, _ = qkv4.shape
    nblk = sub // BAND
    scale = HEAD_DIM ** -0.5
    npair = ATT_OUT // LANE
    hps = _pairs_per_step(d)

    def body(qkv_ref, bias_ref, do_ref, o_ref, l_ref, dqkv_ref, db_ref):
        b, r = pl.program_id(1), pl.program_id(2)

        @pl.when((b == 0) & (r == 0))
        def _():
            db_ref[...] = jnp.zeros_like(db_ref)

        def blocks(ns, carry, with_prev):
            sides = (0, 1) if with_prev else (0,)
            chains = [(bi, i, h, sd) for bi in range(len(ns)) for i in range(hps) for h in range(2) for sd in sides]
            key_rows = lambda bi, sd: _rows(ns[bi] - sd)
            qs = {}
            for bi in range(len(ns)):
                for i in range(hps):
                    for h in range(2):
                        hl = slice(i * LANE + h * HEAD_DIM, i * LANE + (h + 1) * HEAD_DIM)
                        do = do_ref[_rows(ns[bi]), hl]
                        qs[(bi, i, h)] = (
                            qkv_ref[_rows(ns[bi]), _head_cols(i, h, 0)], do.astype(BF16),
                            jnp.sum(do * o_ref[_rows(ns[bi]), hl], -1, keepdims=True),
                            l_ref[_rows(ns[bi]), i * LANE + h * HEAD_DIM:i * LANE + h * HEAD_DIM + 1])
            raw = []
            for bi, i, h, sd in chains:
                q, do16, _, _ = qs[(bi, i, h)]
                k = qkv_ref[key_rows(bi, sd), _head_cols(i, h, 1)]
                v = qkv_ref[key_rows(bi, sd), _head_cols(i, h, 2)]
                bias_blk = bias_ref[2 * i + h, :, :BAND] if sd else bias_ref[2 * i + h, :, BAND:]
                s = lax.dot_general(q, k, NT, preferred_element_type=F32) * scale + bias_blk
                dp = lax.dot_general(do16, v, NT, preferred_element_type=F32)
                raw.append((s, dp))
            soft = []
            for (bi, i, h, sd), (s, dp) in zip(chains, raw):
                _, _, ebar, lcol = qs[(bi, i, h)]
                p = jnp.exp(s - lcol)
                ds = p * (dp - ebar)
                if sd:
                    db_ref[2 * i + h, :, :BAND] += ds
                else:
                    db_ref[2 * i + h, :, BAND:] += ds
                soft.append((p.astype(BF16), ds.astype(BF16)))
            grads = {}
            for (bi, i, h, sd), (p16, ds16) in zip(chains, soft):
                q, do16, _, _ = qs[(bi, i, h)]
                k = qkv_ref[key_rows(bi, sd), _head_cols(i, h, 1)]
                grads[(bi, i, h, sd)] = (
                    jnp.dot(ds16, k, preferred_element_type=F32),
                    lax.dot_general(ds16, q, TN, preferred_element_type=F32) * scale,
                    lax.dot_general(p16, do16, TN, preferred_element_type=F32))
            both = lambda bi, i, sd, which: jnp.concatenate(
                [grads[(bi, i, 0, sd)][which], grads[(bi, i, 1, sd)][which]], axis=1)
            carry = list(carry) if carry is not None else None
            for bi, n in enumerate(ns):
                for i in range(hps):
                    base = 3 * LANE * i
                    dq = both(bi, i, 0, 0)
                    if with_prev:
                        dq = dq + both(bi, i, 1, 0)
                        dqkv_ref[_rows(n - 1), base + LANE:base + 2 * LANE] = (
                            carry[2 * i] + both(bi, i, 1, 1)).astype(BF16)
                        dqkv_ref[_rows(n - 1), base + 2 * LANE:base + 3 * LANE] = (
                            carry[2 * i + 1] + both(bi, i, 1, 2)).astype(BF16)
                    dqkv_ref[_rows(n), base:base + LANE] = (dq * scale).astype(BF16)
                carry = [t for i in range(hps) for t in (both(bi, i, 0, 1), both(bi, i, 0, 2))]
            return tuple(carry)

        carry = _for_blocks(blocks, nblk, 2 if hps == 1 else 1, None)
        for i in range(hps):
            base = 3 * LANE * i
            dqkv_ref[_rows(nblk - 1), base + LANE:base + 2 * LANE] = carry[2 * i].astype(BF16)
            dqkv_ref[_rows(nblk - 1), base + 2 * LANE:base + 3 * LANE] = carry[2 * i + 1].astype(BF16)

    nspec = pl.BlockSpec((None, sub, hps * LANE), lambda hp, b, r: (b, 0, r * (npair // hps) + hp))
    qspec = pl.BlockSpec((None, None, sub, 3 * LANE * hps), lambda hp, b, r: (b, r, 0, hp))
    bspec = pl.BlockSpec((2 * hps, BAND, 2 * BAND), lambda hp, b, r: (hp, 0, 0))
    view = lambda t: t.reshape(nb, sub, d * ATT_OUT)
    return _pcall(
        body, name=name, grid=(npair // hps, nb, d),
        in_specs=[qspec, bspec, nspec, nspec, nspec], out_specs=[qspec, bspec],
        out_shape=[jax.ShapeDtypeStruct(qkv4.shape, BF16),
                   jax.ShapeDtypeStruct((GROUP_HEADS, BAND, 2 * BAND), F32)],
        compiler_params=_params("parallel", "arbitrary", "arbitrary"),
    )(qkv4, bias, view(do_att), view(o_att), view(lse))


def _combine_fwd(os, ls, gatt):
    nb, seq, _ = gatt.shape
    tm = 512

    def body(o0, o1, o2, l0, l1, l2, g_ref, oa_ref, oatt_ref, lse_ref):
        m = jnp.maximum(jnp.maximum(l0[...], l1[...]), l2[...])
        tot = m + jnp.log(jnp.exp(l0[...] - m) + jnp.exp(l1[...] - m) + jnp.exp(l2[...] - m))
        o = (jnp.exp(l0[...] - tot) * o0[...] + jnp.exp(l1[...] - tot) * o1[...]
             + jnp.exp(l2[...] - tot) * o2[...])
        g = g_ref[...]
        oa_ref[...] = (o * (g * _sigmoid(g))).astype(BF16)
        oatt_ref[...] = o
        lse_ref[...] = tot

    spec = pl.BlockSpec((None, tm, ATT_OUT), lambda b, i: (b, i, 0))
    return _pcall(
        body, name="attn_combine", grid=(nb, seq // tm), in_specs=[spec] * 7, out_specs=[spec] * 3,
        out_shape=[jax.ShapeDtypeStruct((nb, seq, ATT_OUT), BF16), jax.ShapeDtypeStruct((nb, seq, ATT_OUT), F32),
                   jax.ShapeDtypeStruct((nb, seq, ATT_OUT), F32)],
        compiler_params=_params("parallel", "parallel"),
    )(*os, *ls, gatt)


def _combine_bwd(doa, gatt, o_att):
    nb, seq, _ = gatt.shape
    tm = 512

    def body(doa_ref, g_ref, o_ref, do_ref, dg_ref):
        g = g_ref[...]
        sg = _sigmoid(g)
        do_ref[...] = doa_ref[...] * (g * sg)
        dg_ref[...] = (doa_ref[...] * o_ref[...] * (sg * (1.0 + g * (1.0 - sg)))).astype(BF16)

    spec = pl.BlockSpec((None, tm, ATT_OUT), lambda b, i: (b, i, 0))
    return _pcall(
        body, name="attn_combine_bwd", grid=(nb, seq // tm), in_specs=[spec] * 3, out_specs=[spec] * 2,
        out_shape=[jax.ShapeDtypeStruct((nb, seq, ATT_OUT), F32), jax.ShapeDtypeStruct((nb, seq, ATT_OUT), BF16)],
        compiler_params=_params("parallel", "parallel"),
    )(doa, gatt, o_att)


CONV_TM = 512
CONV_TC = 512


def _shift_down(cur, halo, k):
    rolled = pltpu.roll(cur, k, 0)
    hro = pltpu.roll(halo, k, 0)
    row = lax.broadcasted_iota(jnp.int32, hro.shape, 0)
    return jnp.concatenate([jnp.where(row < k, hro, rolled[:8]), rolled[8:]], axis=0)


def _shift_up(cur, halo, k):
    n = cur.shape[0]
    rolled = pltpu.roll(cur, n - k, 0)
    hro = pltpu.roll(halo, 8 - k, 0)
    row = lax.broadcasted_iota(jnp.int32, hro.shape, 0)
    return jnp.concatenate([rolled[:n - 8], jnp.where(row >= 8 - k, hro, rolled[n - 8:])], axis=0)


def _conv_pre(cur, halo, w_ref, b_ref):
    acc = cur * w_ref[3:4, :] + b_ref[...]
    for k in range(1, 4):
        acc = acc + _shift_down(cur, halo, k) * w_ref[3 - k:4 - k, :]
    return acc


def _conv_specs(seq):
    nblk = seq // CONV_TM
    cur = pl.BlockSpec((None, CONV_TM, CONV_TC), lambda cb, b, i: (b, i, cb))
    prev = pl.BlockSpec((None, 8, CONV_TC), lambda cb, b, i: (b, jnp.maximum(i * (CONV_TM // 8) - 1, 0), cb))
    nxt = pl.BlockSpec((None, 8, CONV_TC),
                       lambda cb, b, i: (b, jnp.minimum((i + 1) * (CONV_TM // 8), seq // 8 - 1), cb))
    wspec = pl.BlockSpec((4, CONV_TC), lambda cb, b, i: (0, cb))
    bspec = pl.BlockSpec((1, CONV_TC), lambda cb, b, i: (0, cb))
    return nblk, cur, prev, nxt, wspec, bspec


def _conv_fwd(xin, w4, bias, name):
    nb, seq, ch = xin.shape
    _, cur, prev, _, wspec, bspec = _conv_specs(seq)

    def body(x_ref, h_ref, w_ref, b_ref, o_ref):
        halo = jnp.where(pl.program_id(2) > 0, h_ref[...], 0.0)
        pre = _conv_pre(x_ref[...], halo, w_ref, b_ref)
        o_ref[...] = pre * _sigmoid(pre)

    return _pcall(
        body, name=name, grid=(ch // CONV_TC, nb, seq // CONV_TM),
        in_specs=[cur, prev, wspec, bspec], out_specs=cur,
        out_shape=jax.ShapeDtypeStruct(xin.shape, F32),
        compiler_params=_params("parallel", "parallel", "parallel"),
    )(xin, xin, w4, bias)


def _conv_bwd_pre(dact, xin, w4, bias, name):
    nb, seq, ch = xin.shape
    _, cur, prev, _, wspec, bspec = _conv_specs(seq)

    def body(da_ref, x_ref, h_ref, w_ref, b_ref, dp_ref, s_ref):
        b, i = pl.program_id(1), pl.program_id(2)

        @pl.when((b == 0) & (i == 0))
        def _():
            s_ref[...] = jnp.zeros_like(s_ref)

        halo = jnp.where(i > 0, h_ref[...], 0.0)
        x = x_ref[...]
        pre = _conv_pre(x, halo, w_ref, b_ref)
        sg = _sigmoid(pre)
        dpre = da_ref[...] * (sg * (1.0 + pre * (1.0 - sg)))
        dp_ref[...] = dpre
        s_ref[3:4, :] += jnp.sum(dpre * x, 0, keepdims=True)
        for k in range(1, 4):
            s_ref[3 - k:4 - k, :] += jnp.sum(dpre * _shift_down(x, halo, k), 0, keepdims=True)
        s_ref[4:5, :] += jnp.sum(dpre, 0, keepdims=True)

    return _pcall(
        body, name=name, grid=(ch // CONV_TC, nb, seq // CONV_TM),
        in_specs=[cur, cur, prev, wspec, bspec],
        out_specs=[cur, pl.BlockSpec((8, CONV_TC), lambda cb, b, i: (0, cb))],
        out_shape=[jax.ShapeDtypeStruct(xin.shape, F32), jax.ShapeDtypeStruct((8, ch), F32)],
        compiler_params=_params("parallel", "arbitrary", "arbitrary"),
    )(dact, xin, xin, w4, bias)


def _conv_bwd_x(dpre, w4, name):
    nb, seq, ch = dpre.shape
    nblk, cur, _, nxt, wspec, _ = _conv_specs(seq)

    def body(d_ref, n_ref, w_ref, o_ref):
        halo = jnp.where(pl.program_id(2) < nblk - 1, n_ref[...], 0.0)
        cur_v = d_ref[...]
        acc = cur_v * w_ref[3:4, :]
        for j in range(1, 4):
            acc = acc + _shift_up(cur_v, halo, j) * w_ref[3 - j:4 - j, :]
        o_ref[...] = acc.astype(BF16)

    return _pcall(
        body, name=name, grid=(ch // CONV_TC, nb, seq // CONV_TM),
        in_specs=[cur, nxt, wspec], out_specs=cur,
        out_shape=jax.ShapeDtypeStruct(dpre.shape, BF16),
        compiler_params=_params("parallel", "parallel", "parallel"),
    )(dpre, dpre, w4)


def _softplus_sig(dt_raw, dt_bias_row):
    nb, seq, _ = dt_raw.shape
    tm = 512

    def body(r_ref, b_ref, sp_ref, sg_ref):
        v = r_ref[...] + b_ref[...]
        sp_ref[...] = jnp.maximum(v, 0.0) + jnp.log1p(jnp.exp(-jnp.abs(v)))
        sg_ref[...] = _sigmoid(v)

    spec = pl.BlockSpec((None, tm, LANE), lambda b, i: (b, i, 0))
    return _pcall(
        body, name="dt_softplus", grid=(nb, seq // tm),
        in_specs=[spec, pl.BlockSpec((1, LANE), lambda b, i: (0, 0))], out_specs=[spec, spec],
        out_shape=[jax.ShapeDtypeStruct(dt_raw.shape, F32)] * 2,
        compiler_params=_params("parallel", "parallel"),
    )(dt_raw, dt_bias_row)


def _group_lanes(t):
    pads = [(0, 0)] * (t.ndim - 1) + [(0, LANE - GROUP_SSM_HEADS)]
    return jnp.stack([jnp.pad(t[..., GROUP_SSM_HEADS * g:GROUP_SSM_HEADS * (g + 1)], pads) for g in range(SSM_GROUPS)])


def _ungroup_lanes(t):
    return jnp.concatenate([t[g][..., :GROUP_SSM_HEADS] for g in range(SSM_GROUPS)], axis=-1)


def _decays(dt, al_ref):
    row = lax.broadcasted_iota(jnp.int32, (CHUNK, CHUNK), 0)
    col = lax.broadcasted_iota(jnp.int32, (CHUNK, CHUNK), 1)
    tril = (row >= col).astype(F32)
    triu = (row <= col).astype(F32)
    arow = -jnp.exp(al_ref[...])
    a = dt * arow
    acs = jnp.dot(tril, a, precision=HIGHEST, preferred_element_type=F32)
    acs_t = lax.dot_general(a, triu, (((0,), (0,)), ((), ())), precision=HIGHEST, preferred_element_type=F32)
    return arow, acs, acs_t, row >= col, triu


def _ssd_specs(nb, seq):
    nc = seq // CHUNK
    hw = GROUP_SSM_HEADS * HEAD_DIM

    def mk(rev):
        cidx = (lambda c: nc - 1 - c) if rev else (lambda c: c)
        wide = pl.BlockSpec((None, CHUNK, hw), lambda g, b, c: (b, cidx(c), g))
        state = pl.BlockSpec((None, CHUNK, D_STATE), lambda g, b, c: (b, cidx(c), g))
        lanes = pl.BlockSpec((None, None, CHUNK, LANE), lambda g, b, c: (g, b, cidx(c), 0))
        prev = pl.BlockSpec((None, None, None, D_STATE, hw), lambda g, b, c: (b, cidx(c), g, 0, 0))
        return wide, state, lanes, prev

    grow = pl.BlockSpec((None, 1, LANE), lambda g, b, c: (g, 0, 0))
    nwspec = pl.BlockSpec((1, hw), lambda g, b, c: (0, g))
    return nc, hw, mk, grow, nwspec


def _head_expand():
    hw = GROUP_SSM_HEADS * HEAD_DIM
    r = lax.broadcasted_iota(jnp.int32, (LANE, hw), 0)
    c = lax.broadcasted_iota(jnp.int32, (LANE, hw), 1)
    return ((c // HEAD_DIM) == r).astype(BF16)


def _split3(v):
    hi = v.astype(BF16)
    rest = v - hi.astype(F32)
    mid = rest.astype(BF16)
    return hi, mid, (rest - mid.astype(F32)).astype(BF16)


def _to_channels(v, e):
    hi, mid, lo = _split3(v)
    dot = lambda t: jnp.dot(t, e, preferred_element_type=F32)
    return (dot(hi) + dot(mid)) + dot(lo)


def _to_heads(w, e):
    hi, mid, lo = _split3(w)
    dot = lambda t: lax.dot_general(t, e, (((1,), (1,)), ((), ())), preferred_element_type=F32)
    return (dot(hi) + dot(mid)) + dot(lo)


def _row8(v):
    return jnp.broadcast_to(v, (8, v.shape[1]))


def _ssd_chunk_setup(dt, al_ref, ds_ref):
    arow, acs, acs_t, causal, triu = _decays(dt, al_ref)
    e = _head_expand()
    dtx = _to_channels(dt, e)
    acsx = _to_channels(acs, e)
    lastx = acsx[CHUNK - 1:CHUNK, :]
    dskx = _to_channels(_row8(ds_ref[...]), e)[0:1, :]
    return arow, acs, acs_t, causal, triu, e, dtx, acsx, lastx, dskx


def _ssd_fwd(xs, bm, cm, dtg, z, alog_g, dskip_g, normw):
    nb, seq, _ = xs.shape
    nc, hw, mk, grow, nwspec = _ssd_specs(nb, seq)
    wide, state, lanes, prev = mk(False)
    tn = (((0,), (0,)), ((), ()))

    def body(xs_ref, b_ref, c_ref, dt_ref, z_ref, al_ref, ds_ref, nw_ref, ys_ref, y_ref, sp_ref, st_ref):
        @pl.when(pl.program_id(2) == 0)
        def _():
            st_ref[...] = jnp.zeros_like(st_ref)

        dt = dt_ref[...]
        _, acs, acs_t, causal, _, _, dtx, acsx, lastx, dskx = _ssd_chunk_setup(dt, al_ref, ds_ref)
        bmat = b_ref[...].astype(BF16)
        cmat = c_ref[...].astype(BF16)
        cb = lax.dot_general(cmat, bmat, (((1,), (1,)), ((), ())), preferred_element_type=F32)
        x = xs_ref[...]
        xdt = x * dtx
        xdt16 = xdt.astype(BF16)
        first_head = lax.broadcasted_iota(jnp.int32, (CHUNK, LANE), 1) < HEAD_DIM
        pairs = []
        for hp in range(GROUP_SSM_HEADS // 2):
            xp = xdt16[:, hp * LANE:(hp + 1) * LANE]
            two = []
            for j in (2 * hp, 2 * hp + 1):
                lmat = jnp.exp(jnp.where(causal, acs[:, j:j + 1] - acs_t[j:j + 1, :], -jnp.inf))
                two.append(jnp.dot((cb * lmat).astype(BF16), xp, preferred_element_type=F32))
            pairs.append(jnp.where(first_head, two[0], two[1]))
        yd = jnp.concatenate(pairs, axis=1)
        s_prev = st_ref[...]
        s16 = s_prev.astype(BF16)
        sp_ref[...] = s16
        yo = jnp.dot(cmat, s16, preferred_element_type=F32) * jnp.exp(acsx)
        sts = lax.dot_general(bmat, (xdt * jnp.exp(lastx - acsx)).astype(BF16), tn, preferred_element_type=F32)
        st_ref[...] = s_prev * jnp.exp(lastx) + sts
        y = yd + yo + dskx * x
        zz = z_ref[...]
        u = y * (zz * _sigmoid(zz))
        rn = lax.rsqrt(jnp.mean(u * u, -1, keepdims=True) + RMS_EPS)
        ys_ref[...] = (u * rn * nw_ref[...]).astype(BF16)
        y_ref[...] = y

    return _pcall(
        body, name="ssd_fwd", grid=(SSM_GROUPS, nb, nc),
        in_specs=[wide, state, state, lanes, wide, grow, grow, nwspec],
        out_specs=[wide, wide, prev],
        out_shape=[jax.ShapeDtypeStruct((nb, seq, D_INNER), BF16), jax.ShapeDtypeStruct((nb, seq, D_INNER), F32),
                   jax.ShapeDtypeStruct((nb, nc, SSM_GROUPS, D_STATE, hw), BF16)],
        scratch_shapes=[pltpu.VMEM((D_STATE, hw), F32)],
        compiler_params=_params("parallel", "parallel", "arbitrary"),
    )(xs, bm, cm, dtg, z, alog_g, dskip_g, normw)


def _ssd_bwd(xs, bm, cm, dtg, sgg, z, y, dys, sprev, alog_g, dskip_g, normw):
    nb, seq, _ = xs.shape
    nc, hw, mk, grow, nwspec = _ssd_specs(nb, seq)
    wide, state, lanes, prev = mk(True)
    nt = (((1,), (1,)), ((), ()))
    tn = (((0,), (0,)), ((), ()))

    def body(xs_ref, b_ref, c_ref, dt_ref, sg_ref, z_ref, y_ref, dys_ref, sp_ref, al_ref, ds_ref, nw_ref,
             dxs_ref, db_ref, dc_ref, ddt_ref, dz_ref, small_ref, dnw_ref, g_ref):
        b, c = pl.program_id(1), pl.program_id(2)

        @pl.when((b == 0) & (c == 0))
        def _():
            small_ref[...] = jnp.zeros_like(small_ref)
            dnw_ref[...] = jnp.zeros_like(dnw_ref)

        @pl.when(c == 0)
        def _():
            g_ref[...] = jnp.zeros_like(g_ref)

        yv, zz, dys_v, nw = y_ref[...], z_ref[...], dys_ref[...], nw_ref[...]
        sz = _sigmoid(zz)
        silu = zz * sz
        u = yv * silu
        rn = lax.rsqrt(jnp.mean(u * u, -1, keepdims=True) + RMS_EPS)
        gn = dys_v * nw
        du = rn * gn - u * (rn * rn * rn) * jnp.mean(u * gn, -1, keepdims=True)
        dnw_ref[...] += jnp.sum(dys_v * u * rn, 0, keepdims=True)
        dy = du * silu
        dz_ref[...] = du * yv * (sz * (1.0 + zz * (1.0 - sz)))

        dt = dt_ref[...]
        arow, acs, acs_t, causal, triu, e, dtx, acsx, lastx, dskx = _ssd_chunk_setup(dt, al_ref, ds_ref)
        dfsx = jnp.exp(acsx)
        dtex = jnp.exp(lastx - acsx)
        bmat = b_ref[...].astype(BF16)
        cmat = c_ref[...].astype(BF16)
        cb = lax.dot_general(cmat, bmat, nt, preferred_element_type=F32)
        x = xs_ref[...]
        xdt = x * dtx
        xdt16 = xdt.astype(BF16)
        xdte = xdt * dtex
        dy16 = dy.astype(BF16)
        dyd = dy * dfsx
        dyd16 = dyd.astype(BF16)
        s16 = sp_ref[...]
        g = g_ref[...]
        g16 = g.astype(BF16)
        cs = jnp.dot(cmat, s16, preferred_element_type=F32)
        dc_off = lax.dot_general(dyd16, s16, nt, preferred_element_type=F32)
        g_here = lax.dot_general(cmat, dyd16, tn, preferred_element_type=F32)
        bg = jnp.dot(bmat, g16, preferred_element_type=F32)
        db_st = lax.dot_general(xdte.astype(BF16), g16, nt, preferred_element_type=F32)
        ddte_w = bg * xdte
        dcd = _to_heads(_row8(jnp.sum(g * s16.astype(F32), 0, keepdims=True)), e)[0:1, :]
        lane = lax.broadcasted_iota(jnp.int32, (CHUNK, LANE), 1)
        first_head = lane < HEAD_DIM
        sub = lax.broadcasted_iota(jnp.int32, (CHUNK, LANE), 0)
        dacs = jnp.zeros((CHUNK, LANE), F32)
        colsums = jnp.zeros((CHUNK, LANE), F32)
        dcb = jnp.zeros((CHUNK, CHUNK), F32)
        pairs = []
        for hp in range(GROUP_SSM_HEADS // 2):
            xp = xdt16[:, hp * LANE:(hp + 1) * LANE]
            dyp = dy16[:, hp * LANE:(hp + 1) * LANE]
            two = []
            for idx, j in enumerate((2 * hp, 2 * hp + 1)):
                lmat = jnp.exp(jnp.where(causal, acs[:, j:j + 1] - acs_t[j:j + 1, :], -jnp.inf))
                mf = cb * lmat
                dy_h = jnp.where(first_head if idx == 0 else jnp.logical_not(first_head), dyp, jnp.zeros_like(dyp))
                dm = lax.dot_general(dy_h, xp, nt, preferred_element_type=F32)
                two.append(lax.dot_general(mf.astype(BF16), dyp, tn, preferred_element_type=F32))
                wmat = dm * mf
                dcb = dcb + dm * lmat
                dacs = jnp.where(lane == j, jnp.sum(wmat, -1, keepdims=True), dacs)
                colsums = jnp.where(sub == j, jnp.sum(wmat, 0, keepdims=True), colsums)
            pairs.append(jnp.where(first_head, two[0], two[1]))
        dxdt = bg * dtex + jnp.concatenate(pairs, axis=1)
        dacs = dacs - colsums.T + _to_heads(dyd * cs - ddte_w, e)
        cd_row = jnp.exp(acs[CHUNK - 1:CHUNK, :])
        tail = _to_heads(_row8(jnp.sum(ddte_w, 0, keepdims=True)), e)[0:1, :] + dcd * cd_row
        dacs = dacs + jnp.where(sub == CHUNK - 1, tail, 0.0)
        da = jnp.dot(triu, dacs, precision=HIGHEST, preferred_element_type=F32)
        ddt_raw = (da * arow + _to_heads(dxdt * x, e)) * sg_ref[...]
        ddt_ref[...] = ddt_raw
        small_ref[0:1, :] += jnp.sum(da * dt, 0, keepdims=True) * arow
        small_ref[1:2, :] += _to_heads(_row8(jnp.sum(dy * x, 0, keepdims=True)), e)[0:1, :]
        small_ref[2:3, :] += jnp.sum(ddt_raw, 0, keepdims=True)
        dcb16 = dcb.astype(BF16)
        dc_ref[...] = dc_off + jnp.dot(dcb16, bmat, preferred_element_type=F32)
        db_ref[...] = db_st + lax.dot_general(dcb16, cmat, tn, preferred_element_type=F32)
        dxs_ref[...] = dxdt * dtx + dskx * dy
        g_ref[...] = g * jnp.exp(lastx) + g_here

    return _pcall(
        body, name="ssd_bwd", grid=(SSM_GROUPS, nb, nc),
        in_specs=[wide, state, state, lanes, lanes, wide, wide, wide, prev, grow, grow, nwspec],
        out_specs=[wide, state, state, lanes, wide,
                   pl.BlockSpec((None, 8, LANE), lambda g, b, c: (g, 0, 0)), nwspec],
        out_shape=[jax.ShapeDtypeStruct((nb, seq, D_INNER), F32),
                   jax.ShapeDtypeStruct((nb, seq, SSM_GROUPS * D_STATE), F32),
                   jax.ShapeDtypeStruct((nb, seq, SSM_GROUPS * D_STATE), F32),
                   jax.ShapeDtypeStruct((SSM_GROUPS, nb, seq, LANE), F32),
                   jax.ShapeDtypeStruct((nb, seq, D_INNER), F32),
                   jax.ShapeDtypeStruct((SSM_GROUPS, 8, LANE), F32),
                   jax.ShapeDtypeStruct((1, D_INNER), F32)],
        scratch_shapes=[pltpu.VMEM((D_STATE, hw), F32)],
        compiler_params=_params("parallel", "arbitrary", "arbitrary"),
    )(xs, bm, cm, dtg, sgg, z, y, dys, sprev, alog_g, dskip_g, normw)


EW_TM = 256


def _merge_fwd(y_a, y_b, gm, bgate):
    nb, seq, _ = y_a.shape

    def body(a_ref, b_ref, ga_ref, gb_ref, bg_ref, o_ref):
        sa = _sigmoid(ga_ref[...] + bg_ref[0:1, :])
        sb = _sigmoid(gb_ref[...] + bg_ref[1:2, :])
        o_ref[...] = (sa * a_ref[...] + sb * b_ref[...]).astype(BF16)

    spec = pl.BlockSpec((None, EW_TM, D_MODEL), lambda b, i: (b, i, 0))
    spec1 = pl.BlockSpec((None, EW_TM, D_MODEL), lambda b, i: (b, i, 1))
    return _pcall(
        body, name="merge_fwd", grid=(nb, seq // EW_TM),
        in_specs=[spec, spec, spec, spec1, pl.BlockSpec((8, D_MODEL), lambda b, i: (0, 0))], out_specs=spec,
        out_shape=jax.ShapeDtypeStruct((nb, seq, D_MODEL), BF16),
        compiler_params=_params("parallel", "parallel"),
    )(y_a, y_b, gm, gm, bgate)


def _merge_bwd(dmerged, y_a, y_b, gm, bgate):
    nb, seq, _ = y_a.shape

    def body(dm_ref, a_ref, b_ref, ga_ref, gb_ref, bg_ref, dya_ref, dyb_ref, dg_ref, s_ref):
        @pl.when((pl.program_id(0) == 0) & (pl.program_id(1) == 0))
        def _():
            s_ref[...] = jnp.zeros_like(s_ref)

        dm = dm_ref[...]
        sa = _sigmoid(ga_ref[...] + bg_ref[0:1, :])
        sb = _sigmoid(gb_ref[...] + bg_ref[1:2, :])
        dya_ref[...] = (dm * sa).astype(BF16)
        dyb_ref[...] = (dm * sb).astype(BF16)
        dga = dm * a_ref[...] * (sa * (1.0 - sa))
        dgb = dm * b_ref[...] * (sb * (1.0 - sb))
        dg_ref[:, :D_MODEL] = dga.astype(BF16)
        dg_ref[:, D_MODEL:] = dgb.astype(BF16)
        s_ref[0:1, :] += jnp.sum(dga, 0, keepdims=True)
        s_ref[1:2, :] += jnp.sum(dgb, 0, keepdims=True)

    spec = pl.BlockSpec((None, EW_TM, D_MODEL), lambda b, i: (b, i, 0))
    spec1 = pl.BlockSpec((None, EW_TM, D_MODEL), lambda b, i: (b, i, 1))
    small = pl.BlockSpec((8, D_MODEL), lambda b, i: (0, 0))
    return _pcall(
        body, name="merge_bwd", grid=(nb, seq // EW_TM),
        in_specs=[spec, spec, spec, spec, spec1, small],
        out_specs=[spec, spec, pl.BlockSpec((None, EW_TM, 2 * D_MODEL), lambda b, i: (b, i, 0)), small],
        out_shape=[jax.ShapeDtypeStruct((nb, seq, D_MODEL), BF16), jax.ShapeDtypeStruct((nb, seq, D_MODEL), BF16),
                   jax.ShapeDtypeStruct((nb, seq, 2 * D_MODEL), BF16), jax.ShapeDtypeStruct((8, D_MODEL), F32)],
        compiler_params=_params("arbitrary", "arbitrary"),
    )(dmerged, y_a, y_b, gm, gm, bgate)


def _ln_loss(x, mix, gp, pw, target, bgate, ln_g, ln_b):
    nb, seq, _ = x.shape

    def body(x_ref, mix_ref, gp_ref, pw_ref, t_ref, bg_ref, g_ref, b_ref, dx_ref, dp_ref, dpw_ref, dgp_ref, s_ref):
        @pl.when((pl.program_id(0) == 0) & (pl.program_id(1) == 0))
        def _():
            s_ref[...] = jnp.zeros_like(s_ref)

        sp = _sigmoid(gp_ref[...] + bg_ref[2:3, :])
        pw = pw_ref[...]
        pre = ALPHA * x_ref[...] + mix_ref[...] + sp * pw
        mu = jnp.mean(pre, -1, keepdims=True)
        cen = pre - mu
        rstd = lax.rsqrt(jnp.mean(cen * cen, -1, keepdims=True) + LN_EPS)
        xhat = cen * rstd
        err = xhat * g_ref[...] + b_ref[...] - t_ref[...]
        dy = err * (1.0 / D_MODEL)
        dxh = dy * g_ref[...]
        dpre = rstd * (dxh - jnp.mean(dxh, -1, keepdims=True) - xhat * jnp.mean(dxh * xhat, -1, keepdims=True))
        dx_ref[...] = ALPHA * dpre
        dp_ref[...] = dpre.astype(BF16)
        dpw_ref[...] = (dpre * sp).astype(BF16)
        dgp = dpre * pw * (sp * (1.0 - sp))
        dgp_ref[...] = dgp.astype(BF16)
        s_ref[0:1, :] += jnp.sum(dy * xhat, 0, keepdims=True)
        s_ref[1:2, :] += jnp.sum(dy, 0, keepdims=True)
        s_ref[2:3, :] += jnp.sum(dgp, 0, keepdims=True)
        s_ref[3:4, :] += jnp.sum(err * err, 0, keepdims=True)

    spec = pl.BlockSpec((None, EW_TM, D_MODEL), lambda b, i: (b, i, 0))
    small = pl.BlockSpec((8, D_MODEL), lambda b, i: (0, 0))
    row = pl.BlockSpec((1, D_MODEL), lambda b, i: (0, 0))
    return _pcall(
        body, name="ln_loss", grid=(nb, seq // EW_TM),
        in_specs=[spec] * 5 + [small, row, row], out_specs=[spec] * 4 + [small],
        out_shape=[jax.ShapeDtypeStruct((nb, seq, D_MODEL), F32)] + [jax.ShapeDtypeStruct((nb, seq, D_MODEL), BF16)] * 3
        + [jax.ShapeDtypeStruct((8, D_MODEL), F32)],
        compiler_params=_params("arbitrary", "arbitrary"),
    )(x, mix, gp, pw, target, bgate, ln_g, ln_b)


def _adamw(w, g, m, v, name):
    rows, cols = w.shape
    tr = rows
    for cand in range(8, rows, 8):
        if rows % cand == 0 and cand * cols * 4 <= (1 << 20):
            tr = cand
    if rows * cols * 4 <= (1 << 20):
        tr = rows
    c1 = 1.0 - ADAM_B1 ** ADAM_STEP
    c2 = 1.0 - ADAM_B2 ** ADAM_STEP

    def body(w_ref, g_ref, m_ref, v_ref, d_ref, nm_ref, nv_ref):
        gv = g_ref[...]
        nm = ADAM_B1 * m_ref[...] + (1.0 - ADAM_B1) * gv
        nv = ADAM_B2 * v_ref[...] + (1.0 - ADAM_B2) * (gv * gv)
        d_ref[...] = -ADAM_LR * ((nm / c1) / (jnp.sqrt(nv / c2) + ADAM_EPS) + ADAM_WD * w_ref[...])
        nm_ref[...] = nm
        nv_ref[...] = nv

    spec = pl.BlockSpec((tr, cols), lambda i: (i, 0))
    return _pcall(
        body, name=name, grid=(rows // tr,), in_specs=[spec] * 4, out_specs=[spec] * 3,
        out_shape=[jax.ShapeDtypeStruct(w.shape, F32)] * 3, compiler_params=_params("parallel"),
    )(w, g, m, v)


def _sum_rows(parts, out_dtype, name):
    rows, cols = parts[0].shape
    tr = rows
    for cand in range(16, rows, 16):
        if rows % cand == 0 and cand * cols * 4 <= (1 << 20):
            tr = cand
    n = len(parts)

    def body(*refs):
        acc = refs[0][...].astype(F32)
        for r in refs[1:n]:
            acc = acc + r[...].astype(F32)
        refs[n][...] = acc.astype(out_dtype)

    spec = pl.BlockSpec((tr, cols), lambda i: (i, 0))
    return _pcall(
        body, name=name, grid=(rows // tr,), in_specs=[spec] * n, out_specs=spec,
        out_shape=jax.ShapeDtypeStruct((rows, cols), out_dtype), compiler_params=_params("parallel"),
    )(*parts)


def _place():
    return lax.axis_index("x"), lax.axis_index("y"), lax.axis_index("c")


def _other_chips(x, y):
    return [(1 - x, y), (x, 1 - y), (1 - x, 1 - y)]


def _remote(src, dst, send_sem, recv_sem, to):
    return pltpu.make_async_remote_copy(src_ref=src, dst_ref=dst, send_sem=send_sem, recv_sem=recv_sem,
                                        device_id=to, device_id_type=MESH)


ANY = pl.BlockSpec(memory_space=pl.ANY)
DMA_CHUNK_BYTES = 512 * 1024


def _row_chunks(rows, row_bytes):
    per = max(16, DMA_CHUNK_BYTES // row_bytes // 16 * 16)
    return [(s, min(per, rows - s)) for s in range(0, rows, per)]


def _row_tile(rows, cols, align):
    best = None
    for cand in range(align, rows + 1, align):
        if rows % cand == 0 and cand * cols * 4 <= (1 << 20):
            best = cand
    return best or rows


def _allgather_pieces(pieces):
    n = len(pieces)
    halves = [_row_chunks(p.shape[0] // 2, p.shape[1] * p.dtype.itemsize) for p in pieces]
    wholes = [_row_chunks(p.shape[0], p.shape[1] * p.dtype.itemsize) for p in pieces]
    n_ici = 3 * sum(len(h) for h in halves)
    n_loc = sum(len(w) for w in wholes)

    def body(*refs):
        ins, outs = refs[:n], refs[n:2 * n]
        send_sems, recv_sems, local_sems = refs[2 * n:]
        x, y, c = _place()
        me = 2 * x + y
        sibling = (x, y, 1 - c)
        chips = _other_chips(x, y)
        locals_ = []
        for a in range(n):
            for s, m in wholes[a]:
                loc = pltpu.make_async_copy(ins[a].at[pl.ds(s, m)], outs[a].at[me, pl.ds(s, m)],
                                            local_sems.at[len(locals_)])
                loc.start()
                locals_.append(loc)
        ici = []
        for a in range(n):
            half = ins[a].shape[0] // 2
            for s, m in halves[a]:
                for j, (cx, cy) in enumerate(chips):
                    k = len(ici)
                    cp = _remote(ins[a].at[pl.ds(c * half + s, m)], outs[a].at[me, pl.ds(c * half + s, m)],
                                 send_sems.at[k], recv_sems.at[k], (cx, cy, c))
                    cp.start()
                    ici.append((a, s, m, j, cp))
        passed = []
        for k, (a, s, m, j, _) in enumerate(ici):
            half = ins[a].shape[0] // 2
            cx, cy = chips[j]
            blk = outs[a].at[2 * cx + cy, pl.ds(c * half + s, m)]
            _remote(blk, blk, send_sems.at[k], recv_sems.at[k], (cx, cy, c)).wait_recv()
            fw = _remote(blk, blk, send_sems.at[n_ici + k], recv_sems.at[n_ici + k], sibling)
            fw.start()
            passed.append(fw)
        for k, (a, s, m, j, _) in enumerate(ici):
            half = ins[a].shape[0] // 2
            cx, cy = chips[j]
            blk = outs[a].at[2 * cx + cy, pl.ds((1 - c) * half + s, m)]
            _remote(blk, blk, send_sems.at[n_ici + k], recv_sems.at[n_ici + k], sibling).wait_recv()
        for item in ici:
            item[4].wait_send()
        for fw in passed:
            fw.wait_send()
        for loc in locals_:
            loc.wait()

    return _pcall(
        body, name="allgather_weights", in_specs=[ANY] * n, out_specs=[ANY] * n,
        out_shape=[jax.ShapeDtypeStruct((4,) + p.shape, p.dtype) for p in pieces],
        scratch_shapes=[pltpu.SemaphoreType.DMA((2 * n_ici,)), pltpu.SemaphoreType.DMA((2 * n_ici,)),
                        pltpu.SemaphoreType.DMA((n_loc,))],
        compiler_params=pltpu.CompilerParams(has_side_effects=True),
    )(*pieces)


def _sibling_exchange(grads):
    n = len(grads)
    chunks = [_row_chunks(g.shape[1] // 2, g.shape[2] * g.dtype.itemsize) for g in grads]
    n_sem = 4 * sum(len(ch) for ch in chunks)

    def body(*refs):
        ins, gots = refs[:n], refs[n:2 * n]
        send_sems, recv_sems = refs[2 * n:]
        x, y, c = _place()
        sibling = (x, y, 1 - c)
        work = []
        for a in range(n):
            half = ins[a].shape[1] // 2
            for piece in range(4):
                for s, m in chunks[a]:
                    k = len(work)
                    cp = _remote(ins[a].at[piece, pl.ds((1 - c) * half + s, m)], gots[a].at[piece, pl.ds(s, m)],
                                 send_sems.at[k], recv_sems.at[k], sibling)
                    cp.start()
                    work.append(cp)
        for cp in work:
            cp.wait()

    return _pcall(
        body, name="grad_sibling_exchange", in_specs=[ANY] * n, out_specs=[ANY] * n,
        out_shape=[jax.ShapeDtypeStruct((4, g.shape[1] // 2, g.shape[2]), g.dtype) for g in grads],
        scratch_shapes=[pltpu.SemaphoreType.DMA((n_sem,)), pltpu.SemaphoreType.DMA((n_sem,))],
        compiler_params=pltpu.CompilerParams(has_side_effects=True),
    )(*grads)


def _chip_scatter(sums):
    n = len(sums)
    chunks = [_row_chunks(s.shape[1], s.shape[2] * s.dtype.itemsize) for s in sums]
    n_sem = 3 * sum(len(ch) for ch in chunks)

    def body(*refs):
        ins, gots = refs[:n], refs[n:2 * n]
        send_sems, recv_sems = refs[2 * n:]
        x, y, c = _place()
        chips = _other_chips(x, y)
        work = []
        for a in range(n):
            for s, m in chunks[a]:
                for j, (cx, cy) in enumerate(chips):
                    k = len(work)
                    cp = _remote(ins[a].at[2 * cx + cy, pl.ds(s, m)], gots[a].at[j, pl.ds(s, m)],
                                 send_sems.at[k], recv_sems.at[k], (cx, cy, c))
                    cp.start()
                    work.append(cp)
        for cp in work:
            cp.wait()

    return _pcall(
        body, name="grad_chip_scatter", in_specs=[ANY] * n, out_specs=[ANY] * n,
        out_shape=[jax.ShapeDtypeStruct((3,) + s.shape[1:], s.dtype) for s in sums],
        scratch_shapes=[pltpu.SemaphoreType.DMA((n_sem,)), pltpu.SemaphoreType.DMA((n_sem,))],
        compiler_params=pltpu.CompilerParams(has_side_effects=True),
    )(*sums)


def _sibling_gather(fulls):
    n = len(fulls)
    chunks = [_row_chunks(f.shape[0] // 2, f.shape[1] * f.dtype.itemsize) for f in fulls]
    n_sem = sum(len(ch) for ch in chunks)

    def body(*refs):
        outs = refs[n:2 * n]
        send_sems, recv_sems = refs[2 * n:]
        x, y, c = _place()
        sibling = (x, y, 1 - c)
        work = []
        for a in range(n):
            h = outs[a].shape[0] // 2
            for s, m in chunks[a]:
                k = len(work)
                mine = outs[a].at[pl.ds(c * h + s, m)]
                cp = _remote(mine, mine, send_sems.at[k], recv_sems.at[k], sibling)
                cp.start()
                work.append((a, s, m, cp))
        for k, (a, s, m, cp) in enumerate(work):
            h = outs[a].shape[0] // 2
            cp.wait_send()
            theirs = outs[a].at[pl.ds((1 - c) * h + s, m)]
            _remote(theirs, theirs, send_sems.at[k], recv_sems.at[k], sibling).wait_recv()

    return _pcall(
        body, name="grad_sibling_gather", in_specs=[ANY] * n, out_specs=[ANY] * n,
        out_shape=[jax.ShapeDtypeStruct(f.shape, f.dtype) for f in fulls],
        input_output_aliases={a: a for a in range(n)},
        scratch_shapes=[pltpu.SemaphoreType.DMA((n_sem,)), pltpu.SemaphoreType.DMA((n_sem,))],
        compiler_params=pltpu.CompilerParams(has_side_effects=True),
    )(*fulls)


def _pair_sum(grad, got, place, name):
    _, rows, cols = grad.shape
    half = rows // 2
    tr = _row_tile(half, cols, 16)

    def body(p_ref, a_ref, b_ref, o_ref):
        o_ref[...] = (a_ref[...].astype(F32) + b_ref[...].astype(F32)).astype(BF16)

    return _pcall(
        body, name=name,
        grid_spec=pltpu.PrefetchScalarGridSpec(
            num_scalar_prefetch=1, grid=(4, half // tr),
            in_specs=[pl.BlockSpec((None, tr, cols), lambda k, i, p: (k, p[1] * (half // tr) + i, 0)),
                      pl.BlockSpec((None, tr, cols), lambda k, i, p: (k, i, 0))],
            out_specs=pl.BlockSpec((None, tr, cols), lambda k, i, p: (k, i, 0))),
        out_shape=jax.ShapeDtypeStruct((4, half, cols), BF16),
        compiler_params=_params("parallel", "parallel"),
    )(place, grad, got)


def _chip_sum(sums, got, place, name):
    _, h, cols = sums.shape
    tr = _row_tile(h, cols, 16)

    def body(p_ref, own_ref, g0, g1, g2, o_ref):
        o_ref[...] = ((own_ref[...].astype(F32) + g0[...].astype(F32)) + g1[...].astype(F32)) + g2[...].astype(F32)

    gspec = lambda j: pl.BlockSpec((None, tr, cols), lambda i, p: (j, i, 0))
    return _pcall(
        body, name=name,
        grid_spec=pltpu.PrefetchScalarGridSpec(
            num_scalar_prefetch=1, grid=(h // tr,),
            in_specs=[pl.BlockSpec((None, tr, cols), lambda i, p: (p[0], i, 0)), gspec(0), gspec(1), gspec(2)],
            out_specs=pl.BlockSpec((tr, cols), lambda i, p: (p[1] * (h // tr) + i, 0))),
        out_shape=jax.ShapeDtypeStruct((2 * h, cols), F32),
        compiler_params=_params("parallel"),
    )(place, sums, got, got, got)


def _allgather8(buf, name):
    rows = buf.shape[0]

    def body(in_ref, out_ref, send_sems, recv_sems):
        x, y, c = _place()
        me = 4 * x + 2 * y + c
        out_ref[me] = in_ref[...]
        work = []
        for rel in range(1, 8):
            fx, fy, fc = (rel >> 2) & 1, (rel >> 1) & 1, rel & 1
            to = (x ^ fx, y ^ fy, c ^ fc)
            cp = _remote(in_ref, out_ref.at[me], send_sems.at[rel - 1], recv_sems.at[rel - 1], to)
            cp.start()
            work.append((cp, 4 * to[0] + 2 * to[1] + to[2]))
        for rel, (cp, frm) in enumerate(work):
            cp.wait_send()
            blk = out_ref.at[frm]
            _remote(blk, blk, send_sems.at[rel], recv_sems.at[rel], (x, y, c)).wait_recv()

    return _pcall(
        body, name=name, in_specs=[pl.BlockSpec(memory_space=pltpu.VMEM)],
        out_specs=pl.BlockSpec(memory_space=pltpu.VMEM),
        out_shape=jax.ShapeDtypeStruct((8, rows, LANE), F32),
        scratch_shapes=[pltpu.SemaphoreType.DMA((7,)), pltpu.SemaphoreType.DMA((7,))],
        compiler_params=pltpu.CompilerParams(has_side_effects=True),
    )(buf)


def _reduce_scatter(grads):
    x, y, c = _place()
    place = jnp.stack([2 * x + y, c]).astype(jnp.int32)
    got = _sibling_exchange(grads)
    chip_sums = [_pair_sum(g, t, place, "grad_pair_sum_%d" % i) for i, (g, t) in enumerate(zip(grads, got))]
    others = _chip_scatter(chip_sums)
    fulls = [_chip_sum(s, t, place, "grad_chip_sum_%d" % i) for i, (s, t) in enumerate(zip(chip_sums, others))]
    return _sibling_gather(fulls)


def _pack_rows(arrs):
    parts = []
    for a in arrs:
        f = a.reshape(-1).astype(F32)
        parts.append(jnp.pad(f, (0, (-f.shape[0]) % LANE)))
    flat = jnp.concatenate(parts)
    rows = -(-flat.shape[0] // LANE)
    rows8 = -(-rows // 8) * 8
    return jnp.pad(flat, (0, rows8 * LANE - flat.shape[0])).reshape(rows8, LANE)


def _unpack_rows(buf, shapes):
    flat = buf.reshape(-1)
    outs, off = [], 0
    for s in shapes:
        n = int(np.prod(s))
        outs.append(flat[off:off + n].reshape(s))
        off += -(-n // LANE) * LANE
    return outs


def _local_grads(x, p, target, wseg, w_br16, w_out16, w_ple16, b_gate, conv_w, conv_b, dt_bias, a_log, d_skip,
                 ssm_norm_w, ln_g, ln_b, rel_bias):
    nb, seq, _ = x.shape
    bmaps = jnp.asarray(_bucket_maps())
    bias = _bias_tables(rel_bias, bmaps)
    bgate8 = jnp.pad(b_gate, ((0, 5), (0, 0)))
    dils = [d for _, d in PATTERNS]

    x16 = x.astype(BF16)
    p16 = p.astype(BF16)
    x16p = [_permute(x16, d) for d in dils]
    qkv = [_proj(x16p[g], wseg["qkv%d" % g], BF16, "proj_qkv%d" % g).reshape(nb, dils[g], seq // dils[g], -1)
           for g in range(3)]
    nat = {s: _proj(x16, wseg[s], F32, "proj_" + s) for s in ("gatt", "z", "xs", "bm", "cm", "dt", "gm", "gp")}
    att = [_attn_fwd(qkv[g], bias[g * GROUP_HEADS:(g + 1) * GROUP_HEADS], dils[g], "attn_fwd%d" % g) for g in range(3)]
    oa, o_att, lse = _combine_fwd([a[0] for a in att], [a[1] for a in att], nat["gatt"])

    cw = {"xs": (conv_w[:, :D_INNER], conv_b[:, :D_INNER]),
          "bm": (conv_w[:, D_INNER:D_INNER + 512], conv_b[:, D_INNER:D_INNER + 512]),
          "cm": (conv_w[:, D_INNER + 512:], conv_b[:, D_INNER + 512:])}
    act = {s: _conv_fwd(nat[s], cw[s][0], cw[s][1], "conv_fwd_" + s) for s in ("xs", "bm", "cm")}
    dt_sp, dt_sg = _softplus_sig(nat["dt"], jnp.pad(dt_bias, ((0, 0), (0, LANE - SSM_HEADS))))
    dtg, sgg = _group_lanes(dt_sp), _group_lanes(dt_sg)
    alog_g, dskip_g = _group_lanes(a_log), _group_lanes(d_skip)
    y_ssm, y_all, sprev = _ssd_fwd(act["xs"], act["bm"], act["cm"], dtg, nat["z"], alog_g, dskip_g, ssm_norm_w)

    w_bra, w_brb = w_br16[:ATT_OUT], w_br16[ATT_OUT:]
    y_a = _proj(oa, w_bra, F32, "proj_ya")
    y_b = _proj(y_ssm, w_brb, F32, "proj_yb")
    merged = _merge_fwd(y_a, y_b, nat["gm"], bgate8)
    mix = _proj(merged, w_out16, F32, "proj_mix")
    pw = _proj(p16, w_ple16, F32, "proj_ple")

    dx, dpre16, dpw16, dgp16, ln_sums = _ln_loss(x, mix, nat["gp"], pw, target, bgate8, ln_g, ln_b)
    loss_sum = (0.5 / D_MODEL) * jnp.sum(ln_sums[3])
    dmerged = _dx(dpre16, w_out16, [], "dx_merged")
    dya16, dyb16, dgm16, mg_sums = _merge_bwd(dmerged, y_a, y_b, nat["gm"], bgate8)
    doa = _dx(dya16, w_bra, [], "dx_oa")
    dys = _dx(dyb16, w_brb, [], "dx_yssm")
    g_w_out = _dw(merged, dpre16, BF16, "dw_out")
    g_w_br = jnp.concatenate([_dw(oa, dya16, BF16, "dw_bra"), _dw(y_ssm, dyb16, BF16, "dw_brb")], axis=0)
    g_w_ple = _dw(p16, dpw16, BF16, "dw_ple")

    do_att, dgatt16 = _combine_bwd(doa, nat["gatt"], o_att)
    dseg = {"gatt": dgatt16, "gm": dgm16, "gp": dgp16}
    dbias = []
    for g in range(3):
        dqkv, db = _attn_bwd(qkv[g], bias[g * GROUP_HEADS:(g + 1) * GROUP_HEADS], do_att, o_att, lse, dils[g],
                             "attn_bwd%d" % g)
        dseg["qkv%d" % g] = dqkv.reshape(nb, seq, -1)
        dbias.append(db)
    g_rel = _bias_grad(jnp.concatenate(dbias, axis=0), bmaps)[:, 0, :NUM_BUCKETS].T

    dxs, dbm, dcm, ddtg, dz, ssd_small, g_normw = _ssd_bwd(
        act["xs"], act["bm"], act["cm"], dtg, sgg, nat["z"], y_all, dys, sprev, alog_g, dskip_g, ssm_norm_w)
    dseg["z"] = dz
    dseg["dt"] = jnp.pad(_ungroup_lanes(ddtg), ((0, 0), (0, 0), (0, LANE - SSM_HEADS)))
    conv_sums = {}
    for s, dact in (("xs", dxs), ("bm", dbm), ("cm", dcm)):
        dpre, conv_sums[s] = _conv_bwd_pre(dact, nat[s], cw[s][0], cw[s][1], "conv_bwd_" + s)
        dseg[s] = _conv_bwd_x(dpre, cw[s][0], "conv_bwd_x_" + s)
    csum = jnp.concatenate([conv_sums["xs"], conv_sums["bm"], conv_sums["cm"]], axis=1)

    dx_perm = [_unpermute(_dx(dseg["qkv%d" % g], wseg["qkv%d" % g], [], "dx_qkv%d" % g), dils[g]) for g in (1, 2)]
    dwseg = {"qkv%d" % g: _dw(x16p[g], dseg["qkv%d" % g], BF16, "dw_qkv%d" % g) for g in range(3)}
    names = ["qkv0", "gatt", "z", "xs", "bm", "cm", "dt", "gm", "gp"]
    for name in names:
        dx = _dx(dseg[name], wseg[name], [dx] + (dx_perm if name == names[-1] else []), "dx_" + name)
        if not name.startswith("qkv"):
            dwseg[name] = _dw(x16, dseg[name], BF16, "dw_" + name)

    small = dict(
        b_gate=jnp.stack([mg_sums[0], mg_sums[1], ln_sums[2]]),
        conv_w=csum[0:4], conv_b=csum[4:5],
        dt_bias=_ungroup_lanes(ssd_small[:, 2:3, :]), a_log=_ungroup_lanes(ssd_small[:, 0:1, :]),
        d_skip=_ungroup_lanes(ssd_small[:, 1:2, :]), ssm_norm_w=g_normw,
        ln_g=ln_sums[0:1], ln_b=ln_sums[1:2], rel_bias=g_rel)
    return loss_sum, dx, dwseg, g_w_br, g_w_out, g_w_ple, small


SMALL_ORDER = ("b_gate", "conv_w", "conv_b", "dt_bias", "a_log", "d_skip", "ssm_norm_w", "ln_g", "ln_b", "rel_bias")
SMALL_FULL_SHAPES = dict(b_gate=(3, 1024), conv_w=(4, 3072), conv_b=(1, 3072), dt_bias=(1, 32), a_log=(1, 32),
                         d_skip=(1, 32), ssm_norm_w=(1, 2048), ln_g=(1, 1024), ln_b=(1, 1024), rel_bias=(32, 36))


def kernel(x, p, w_in, b_gate, conv_w, conv_b, dt_bias, a_log, d_skip, ssm_norm_w, w_branch, w_out, w_ple, ln_g, ln_b, rel_bias, loss_target, m_w_in, m_b_gate, m_conv_w, m_conv_b, m_dt_bias, m_a_log, m_d_skip, m_ssm_norm_w, m_w_branch, m_w_out, m_w_ple, m_ln_g, m_ln_b, m_rel_bias, v_w_in, v_b_gate, v_conv_w, v_conv_b, v_dt_bias, v_a_log, v_d_skip, v_ssm_norm_w, v_w_branch, v_w_out, v_w_ple, v_ln_g, v_ln_b, v_rel_bias):
    cx, cy, cc = _place()
    chip = 2 * cx + cy
    dev = 4 * cx + 2 * cy + cc

    win16 = _shard_to_window(w_in[0].astype(BF16), chip)
    g_win, g_br, g_out, g_ple = _allgather_pieces(
        [win16, w_branch[0].astype(BF16), w_out[0].astype(BF16), w_ple[0].astype(BF16)])
    wseg = _assemble(g_win)
    w_br16 = g_br.reshape(4 * 704, D_MODEL)
    w_out16 = g_out.reshape(D_MODEL, D_MODEL)
    w_ple16 = jnp.transpose(g_ple, (1, 0, 2)).reshape(PLE_DIM, D_MODEL)
    shards = _allgather8(_pack_rows([b_gate[0], conv_w[0]]), "allgather_small_params")
    per_chip = [_unpack_rows(shards[2 * k], [(3, 256), (4, 768)]) for k in range(4)]
    b_gate_full = jnp.concatenate([pc[0] for pc in per_chip], axis=1)
    conv_w_full = jnp.concatenate([pc[1] for pc in per_chip], axis=1)

    loss_sum, grad_x, dwseg, g_br, g_out, g_ple, small = _local_grads(
        x, p[0], loss_target, wseg, w_br16, w_out16, w_ple16, b_gate_full, conv_w_full, conv_b, dt_bias, a_log,
        d_skip, ssm_norm_w, ln_g, ln_b, rel_bias)
    loss = lax.psum(loss_sum, ("x", "y", "c"))

    big = _reduce_scatter([
        _pack(dwseg), g_br.reshape(4, 704, D_MODEL), g_out.reshape(4, 256, D_MODEL),
        jnp.transpose(g_ple.reshape(PLE_DIM, 4, 256), (1, 0, 2))])
    g_w_in = _window_to_shard(big[0], chip)
    g_w_branch, g_w_out, g_w_ple = big[1], big[2], big[3]
    parts = _allgather8(_pack_rows([small[n] for n in SMALL_ORDER]), "allgather_small_grads")
    small_sum = _sum_rows([parts[i] for i in range(8)], F32, "small_grad_sum")
    sg = dict(zip(SMALL_ORDER, _unpack_rows(small_sum, [SMALL_FULL_SHAPES[n] for n in SMALL_ORDER])))
    sg["b_gate"] = lax.dynamic_slice_in_dim(sg["b_gate"], chip * 256, 256, axis=1)
    sg["conv_w"] = lax.dynamic_slice_in_dim(sg["conv_w"], chip * 768, 768, axis=1)
    del dev

    upd = {}
    upd["w_in"] = _adamw(w_in[0], g_w_in, m_w_in[0], v_w_in[0], "adamw_w_in")
    upd["w_branch"] = _adamw(w_branch[0], g_w_branch, m_w_branch[0], v_w_branch[0], "adamw_w_branch")
    upd["w_out"] = _adamw(w_out[0], g_w_out, m_w_out[0], v_w_out[0], "adamw_w_out")
    upd["w_ple"] = _adamw(w_ple[0], g_w_ple, m_w_ple[0], v_w_ple[0], "adamw_w_ple")
    small_w = dict(b_gate=b_gate, conv_w=conv_w, conv_b=conv_b, dt_bias=dt_bias, a_log=a_log, d_skip=d_skip,
                   ssm_norm_w=ssm_norm_w, ln_g=ln_g, ln_b=ln_b, rel_bias=rel_bias)
    small_m = dict(b_gate=m_b_gate, conv_w=m_conv_w, conv_b=m_conv_b, dt_bias=m_dt_bias, a_log=m_a_log,
                   d_skip=m_d_skip, ssm_norm_w=m_ssm_norm_w, ln_g=m_ln_g, ln_b=m_ln_b, rel_bias=m_rel_bias)
    small_v = dict(b_gate=v_b_gate, conv_w=v_conv_w, conv_b=v_conv_b, dt_bias=v_dt_bias, a_log=v_a_log,
                   d_skip=v_d_skip, ssm_norm_w=v_ssm_norm_w, ln_g=v_ln_g, ln_b=v_ln_b, rel_bias=v_rel_bias)
    shapes = [small_w[n].shape for n in SMALL_ORDER]
    s_delta, s_m, s_v = _adamw(_pack_rows([small_w[n] for n in SMALL_ORDER]), _pack_rows([sg[n] for n in SMALL_ORDER]),
                               _pack_rows([small_m[n] for n in SMALL_ORDER]), _pack_rows([small_v[n] for n in SMALL_ORDER]),
                               "adamw_small")
    for i, n in enumerate(SMALL_ORDER):
        upd[n] = tuple(_unpack_rows(t, shapes)[i] for t in (s_delta, s_m, s_v))
        sg[n] = sg[n].reshape(small_w[n].shape)

    order = ("w_in", "b_gate", "conv_w", "conv_b", "dt_bias", "a_log", "d_skip", "ssm_norm_w", "w_branch", "w_out",
             "w_ple", "ln_g", "ln_b", "rel_bias")
    grads = dict(sg, w_in=g_w_in[None], w_branch=g_w_branch[None], w_out=g_w_out[None], w_ple=g_w_ple[None])
    lead = lambda n, t: t[None] if n in ("w_in", "w_branch", "w_out", "w_ple") else t
    return (loss, grad_x, *[grads[n] for n in order], *[lead(n, upd[n][0]) for n in order],
            *[lead(n, upd[n][1]) for n in order], *[lead(n, upd[n][2]) for n in order])
```

```python
import functools
import math

import numpy as np
import jax
import jax.numpy as jnp
from jax import lax
from jax.experimental import pallas as pl
from jax.experimental.pallas import tpu as pltpu

F32, BF16 = jnp.float32, jnp.bfloat16
HIGHEST = lax.Precision.HIGHEST

D_MODEL = 1024
HEAD_DIM = 64
GROUP_HEADS = 12
ATT_OUT = GROUP_HEADS * HEAD_DIM
PATTERNS = ((128, 1), (512, 4), (2048, 16))
BAND = 128
NUM_BUCKETS = 32
MAX_DISTANCE = 2048
D_INNER = 2048
SSM_HEADS = 32
SSM_GROUPS = 4
GROUP_SSM_HEADS = SSM_HEADS // SSM_GROUPS
D_STATE = 128
CHUNK = 128
PLE_DIM = 256
ALPHA = 2.0 ** 0.25
LN_EPS = 1e-5
RMS_EPS = 1e-5
ADAM_LR, ADAM_B1, ADAM_B2, ADAM_EPS, ADAM_WD, ADAM_STEP = 0.001, 0.9, 0.999, 1e-08, 0.01, 10
NEG = -1e30

QKV_W = 3 * ATT_OUT
IN_COLS = 15904
SHARD_COLS = IN_COLS // 4
DT_COL = 12800
ROW_TILE = 16
WIN_ROWS = 4000


def _win_offset(k):
    return (k * SHARD_COLS) % ROW_TILE


def _win_start(k):
    return k * SHARD_COLS - _win_offset(k)

VMEM_LIMIT_BYTES = 56 * 1024 * 1024
LANE = 128
MESH = pl.DeviceIdType.MESH
NT = (((1,), (1,)), ((), ()))
TN = (((0,), (0,)), ((), ()))


def _pcall(body, **kw):
    return pl.pallas_call(body, **kw)


def _params(*sem):
    return pltpu.CompilerParams(dimension_semantics=sem, vmem_limit_bytes=VMEM_LIMIT_BYTES)


def _sigmoid(v):
    return jax.nn.sigmoid(v)


MM_TM = 512


def _permute(t, d):
    nb, seq, ch = t.shape
    return t if d == 1 else t.reshape(nb, seq // d, d, ch).transpose(0, 2, 1, 3).reshape(nb, seq, ch)


def _unpermute(t, d):
    nb, seq, ch = t.shape
    return t if d == 1 else t.reshape(nb, d, seq // d, ch).transpose(0, 2, 1, 3).reshape(nb, seq, ch)


def _tok_spec(tm, width):
    return pl.BlockSpec((None, tm, width), lambda b, i: (b, i, 0))


def _whole(arr):
    return pl.BlockSpec(arr.shape, lambda b, i: (0,) * arr.ndim)


def _proj(a3, ws, out_dtype, name, w_rows_are_outputs=False, tm=MM_TM):
    nb, seq, kdim = a3.shape
    nw = len(ws)
    widths = [w.shape[0] if w_rows_are_outputs else w.shape[1] for w in ws]

    def body(*refs):
        a = refs[0][...].astype(BF16)
        for w_ref, o_ref in zip(refs[1:1 + nw], refs[1 + nw:]):
            if w_rows_are_outputs:
                v = lax.dot_general(a, w_ref[...], NT, preferred_element_type=F32)
            else:
                v = jnp.dot(a, w_ref[...], preferred_element_type=F32)
            o_ref[...] = v.astype(out_dtype)

    return _pcall(
        body, name=name, grid=(nb, seq // tm),
        in_specs=[_tok_spec(tm, kdim)] + [_whole(w) for w in ws],
        out_specs=[_tok_spec(tm, n) for n in widths],
        out_shape=[jax.ShapeDtypeStruct((nb, seq, n), out_dtype) for n in widths],
        compiler_params=_params("parallel", "parallel"),
    )(a3, *ws)


def _dx(dhs, ws, accs, name, w_rows_are_outputs=False, tm=MM_TM):
    nb, seq, _ = dhs[0].shape
    nd, nacc = len(dhs), len(accs)
    kout = ws[0].shape[1] if w_rows_are_outputs else ws[0].shape[0]

    def body(*refs):
        v = None
        for dh_ref, w_ref in zip(refs[:nd], refs[nd:2 * nd]):
            dh = dh_ref[...].astype(BF16)
            if w_rows_are_outputs:
                t = jnp.dot(dh, w_ref[...], preferred_element_type=F32)
            else:
                t = lax.dot_general(dh, w_ref[...], NT, preferred_element_type=F32)
            v = t if v is None else v + t
        for a_ref in refs[2 * nd:2 * nd + nacc]:
            v = v + a_ref[...]
        refs[-1][...] = v

    return _pcall(
        body, name=name, grid=(nb, seq // tm),
        in_specs=[_tok_spec(tm, dh.shape[-1]) for dh in dhs] + [_whole(w) for w in ws] + [_tok_spec(tm, kout)] * nacc,
        out_specs=_tok_spec(tm, kout), out_shape=jax.ShapeDtypeStruct((nb, seq, kout), F32),
        input_output_aliases={2 * nd: 0} if nacc else {},
        compiler_params=_params("parallel", "parallel"),
    )(*dhs, *ws, *accs)


def _dw(a3, dhs, out_dtype, name, rows_are_outputs=False):
    nb, seq, kdim = a3.shape
    nd = len(dhs)
    grid = (nb, seq // MM_TM)
    shapes = [(dh.shape[-1], kdim) if rows_are_outputs else (kdim, dh.shape[-1]) for dh in dhs]

    def body(*refs):
        b, i = pl.program_id(0), pl.program_id(1)
        dh_refs, o_refs, acc_refs = refs[1:1 + nd], refs[1 + nd:1 + 2 * nd], refs[1 + 2 * nd:]

        @pl.when((b == 0) & (i == 0))
        def _():
            for acc_ref in acc_refs:
                acc_ref[...] = jnp.zeros_like(acc_ref)

        a = refs[0][...].astype(BF16)
        for dh_ref, acc_ref in zip(dh_refs, acc_refs):
            dh = dh_ref[...].astype(BF16)
            acc_ref[...] += lax.dot_general(*((dh, a) if rows_are_outputs else (a, dh)), TN,
                                            preferred_element_type=F32)

        @pl.when((b == grid[0] - 1) & (i == grid[1] - 1))
        def _():
            for o_ref, acc_ref in zip(o_refs, acc_refs):
                o_ref[...] = acc_ref[...].astype(out_dtype)

    return _pcall(
        body, name=name, grid=grid,
        in_specs=[_tok_spec(MM_TM, kdim)] + [_tok_spec(MM_TM, dh.shape[-1]) for dh in dhs],
        out_specs=[pl.BlockSpec(s, lambda b, i: (0, 0)) for s in shapes],
        out_shape=[jax.ShapeDtypeStruct(s, out_dtype) for s in shapes],
        scratch_shapes=[pltpu.VMEM(s, F32) for s in shapes],
        compiler_params=_params("arbitrary", "arbitrary"),
    )(a3, *dhs)


def _qkv_rows(g):
    return [(part * QKV_W + g * ATT_OUT + hp * LANE, LANE) for hp in range(ATT_OUT // LANE) for part in range(3)]


def _segments():
    one = lambda name, start, rows: (name, [(start, rows)], max(rows, LANE))
    return [("qkv%d" % g, _qkv_rows(g), QKV_W) for g in range(3)] + [
        one("gatt", 3 * QKV_W, ATT_OUT), one("z", 3 * QKV_W + ATT_OUT, D_INNER), one("xs", 9728, D_INNER),
        one("bm", 9728 + D_INNER, 512), one("cm", 9728 + D_INNER + 512, 512), one("dt", DT_COL, SSM_HEADS),
        one("gm", DT_COL + SSM_HEADS, 2 * D_MODEL), one("gp", DT_COL + SSM_HEADS + 2 * D_MODEL, D_MODEL)]


LAYOUT_TC = 256
NAT_GROUPS = ((("gatt", "z", "dt", "bm", "cm"), 512), (("xs", "gm", "gp"), 256))
DW_GROUPS = (("gatt", "z", "dt", "bm", "cm"), ("xs", "gp"), ("gm",))


def _assemble(win):
    segs = _segments()

    def body(win_ref, *outs):
        def pieces(start, rows):
            t, end = start, start + rows
            while t < end:
                k = min(t // SHARD_COLS, 3)
                shard_end = (k + 1) * SHARD_COLS
                if k < 3 and shard_end % ROW_TILE and t == shard_end - shard_end % ROW_TILE:
                    lo = t - _win_start(k)
                    yield win_ref[k, lo:lo + ROW_TILE, :] + win_ref[k + 1, 0:ROW_TILE, :]
                    t += ROW_TILE
                    continue
                upto = min(end, shard_end - shard_end % ROW_TILE if k < 3 else end)
                yield win_ref[k, t - _win_start(k):upto - _win_start(k), :]
                t = upto

        for (_, ranges, total), o_ref in zip(segs, outs):
            off = 0
            for start, rows in ranges:
                for part in pieces(start, rows):
                    o_ref[off:off + part.shape[0], :] = part
                    off += part.shape[0]
            if off < total:
                o_ref[off:total, :] = jnp.zeros((total - off, o_ref.shape[1]), BF16)

    outs = _pcall(
        body, name="assemble_w_in", grid=(D_MODEL // LAYOUT_TC,),
        in_specs=[pl.BlockSpec((4, WIN_ROWS, LAYOUT_TC), lambda i: (0, 0, i))],
        out_specs=[pl.BlockSpec((total, LAYOUT_TC), lambda i: (0, i)) for _, _, total in segs],
        out_shape=[jax.ShapeDtypeStruct((total, D_MODEL), BF16) for _, _, total in segs],
        compiler_params=_params("parallel"),
    )(win)
    return {name: o for (name, _, _), o in zip(segs, outs)}


def _pack(dsegs):
    segs = _segments()

    def body(*refs):
        ins, o_ref = refs[:-1], refs[-1]
        tail = IN_COLS - _win_start(3)
        o_ref[3, tail:, :] = jnp.zeros((WIN_ROWS - tail, o_ref.shape[2]), BF16)
        for (_, ranges, _), s_ref in zip(segs, ins):
            off = 0
            for start, rows in ranges:
                for k in range(4):
                    lo = _win_start(k)
                    a, b = max(start, lo), min(start + rows, lo + WIN_ROWS)
                    if a < b:
                        o_ref[k, a - lo:b - lo, :] = s_ref[off + a - start:off + b - start, :]
                off += rows

    return _pcall(
        body, name="pack_dw_in", grid=(D_MODEL // LAYOUT_TC,),
        in_specs=[pl.BlockSpec((total, LAYOUT_TC), lambda i: (0, i)) for _, _, total in segs],
        out_specs=pl.BlockSpec((4, WIN_ROWS, LAYOUT_TC), lambda i: (0, 0, i)),
        out_shape=jax.ShapeDtypeStruct((4, WIN_ROWS, D_MODEL), BF16),
        compiler_params=_params("parallel"),
    )(*[dsegs[name] for name, _, _ in segs])


def _shard_to_window(shard_t, k):
    off = (k % 2) * _win_offset(1)
    return lax.dynamic_update_slice(jnp.zeros((WIN_ROWS, D_MODEL), shard_t.dtype), shard_t, (off, 0))


def _window_to_shard(win, k):
    return lax.dynamic_slice(win, ((k % 2) * _win_offset(1), 0), (SHARD_COLS, D_MODEL))


def _bucket_maps():
    qi = np.arange(BAND)[:, None]
    kj = np.arange(2 * BAND)[None, :]
    delta = qi + BAND - kj
    maps = []
    for window, dil in PATTERNS:
        valid = (delta >= 0) & (delta <= window // dil)
        dist = np.maximum(delta, 0) * dil
        max_exact = NUM_BUCKETS // 2
        d_f = np.maximum(dist, 1).astype(np.float32)
        large = max_exact + (np.log(d_f / np.float32(max_exact)) / np.float32(math.log(MAX_DISTANCE / max_exact))
                             * np.float32(NUM_BUCKETS - max_exact)).astype(np.int32)
        large = np.minimum(large, NUM_BUCKETS - 1)
        bucket = np.where(dist < max_exact, dist, large)
        maps.append(np.where(valid, bucket, -1).astype(np.int32))
    return np.stack(maps)


def _bias_tables(rel_bias, bmaps):
    def body(rb_ref, bm_ref, o_ref):
        h = pl.program_id(0)
        bm = bm_ref[...]
        acc = jnp.full(bm.shape, NEG, F32)
        for b in range(NUM_BUCKETS):
            acc = jnp.where(bm == b, rb_ref[b, h], acc)
        o_ref[...] = acc

    return _pcall(
        body, name="bias_tables", grid=(3 * GROUP_HEADS,),
        in_specs=[pl.BlockSpec(memory_space=pltpu.SMEM),
                  pl.BlockSpec((None, BAND, 2 * BAND), lambda h: (h // GROUP_HEADS, 0, 0))],
        out_specs=pl.BlockSpec((None, BAND, 2 * BAND), lambda h: (h, 0, 0)),
        out_shape=jax.ShapeDtypeStruct((3 * GROUP_HEADS, BAND, 2 * BAND), F32),
        compiler_params=_params("parallel"),
    )(rel_bias, bmaps)


def _bias_grad(dbias, bmaps):
    def body(db_ref, bm_ref, o_ref):
        bm = bm_ref[...]
        db = db_ref[...]
        lane = lax.broadcasted_iota(jnp.int32, (1, LANE), 1)
        vec = jnp.zeros((1, LANE), F32)
        for b in range(NUM_BUCKETS):
            s = jnp.sum(jnp.where(bm == b, db, 0.0), keepdims=True)
            vec = jnp.where(lane == b, s, vec)
        o_ref[...] = vec

    return _pcall(
        body, name="bias_grad", grid=(3 * GROUP_HEADS,),
        in_specs=[pl.BlockSpec((None, BAND, 2 * BAND), lambda h: (h, 0, 0)),
                  pl.BlockSpec((None, BAND, 2 * BAND), lambda h: (h // GROUP_HEADS, 0, 0))],
        out_specs=pl.BlockSpec((None, 1, LANE), lambda h: (h, 0, 0)),
        out_shape=jax.ShapeDtypeStruct((3 * GROUP_HEADS, 1, LANE), F32),
        compiler_params=_params("parallel"),
    )(dbias, bmaps)


def _rows(n):
    if isinstance(n, int):
        return pl.ds(n * BAND, BAND)
    return pl.ds(pl.multiple_of(n * BAND, BAND), BAND)


def _for_blocks(blocks, nblk, per, carry):
    carry = blocks([0], carry, False)
    start = 1 + (nblk - 1) % per
    for n in range(1, start):
        carry = blocks([n], carry, True)
    trips = (nblk - start) // per
    if trips > 0:
        carry = lax.fori_loop(
            0, trips, lambda t, c: blocks([start + t * per + u for u in range(per)], c, True), carry)
    return carry


def _pairs_per_step(d):
    return {1: 1, 4: 3, 16: 6}[d]


def _head_cols(i, h, part):
    base = 3 * LANE * i + part * LANE + h * HEAD_DIM
    return slice(base, base + HEAD_DIM)


def _attn_fwd(qkv4, bias, d, name):
    nb, _, sub, _ = qkv4.shape
    nblk = sub // BAND
    scale = HEAD_DIM ** -0.5
    npair = ATT_OUT // LANE
    hps = _pairs_per_step(d)

    def body(qkv_ref, bias_ref, o_ref, l_ref):
        def blocks(ns, carry, with_prev):
            chains = [(bi, i, h) for bi in range(len(ns)) for i in range(hps) for h in range(2)]
            scores = []
            for bi, i, h in chains:
                n = ns[bi]
                q = qkv_ref[_rows(n), _head_cols(i, h, 0)]
                s_c = lax.dot_general(q, qkv_ref[_rows(n), _head_cols(i, h, 1)], NT,
                                      preferred_element_type=F32) * scale + bias_ref[2 * i + h, :, BAND:]
                s_p = None
                if with_prev:
                    s_p = lax.dot_general(q, qkv_ref[_rows(n - 1), _head_cols(i, h, 1)], NT,
                                          preferred_element_type=F32) * scale + bias_ref[2 * i + h, :, :BAND]
                scores.append((s_c, s_p))
            probs = []
            for s_c, s_p in scores:
                m = jnp.max(s_c, -1, keepdims=True)
                if with_prev:
                    m = jnp.maximum(m, jnp.max(s_p, -1, keepdims=True))
                e_c = jnp.exp(s_c - m)
                den = jnp.sum(e_c, -1, keepdims=True)
                e_p = None
                if with_prev:
                    e_p = jnp.exp(s_p - m)
                    den = den + jnp.sum(e_p, -1, keepdims=True)
                    e_p = e_p.astype(BF16)
                probs.append((e_c.astype(BF16), e_p, den, m))
            outs = {}
            for (bi, i, h), (e_c, e_p, den, m) in zip(chains, probs):
                n = ns[bi]
                acc = jnp.dot(e_c, qkv_ref[_rows(n), _head_cols(i, h, 2)], preferred_element_type=F32)
                if with_prev:
                    acc = acc + jnp.dot(e_p, qkv_ref[_rows(n - 1), _head_cols(i, h, 2)], preferred_element_type=F32)
                outs[(bi, i, h)] = (acc / den, jnp.broadcast_to(m + jnp.log(den), (BAND, HEAD_DIM)))
            for bi, n in enumerate(ns):
                for i in range(hps):
                    o_ref[_rows(n), i * LANE:(i + 1) * LANE] = jnp.concatenate(
                        [outs[(bi, i, 0)][0], outs[(bi, i, 1)][0]], axis=1)
                    l_ref[_rows(n), i * LANE:(i + 1) * LANE] = jnp.concatenate(
                        [outs[(bi, i, 0)][1], outs[(bi, i, 1)][1]], axis=1)
            return carry

        _for_blocks(blocks, nblk, 2 if hps == 1 else 1, 0)

    ospec = pl.BlockSpec((None, sub, hps * LANE), lambda hp, b, r: (b, 0, r * (npair // hps) + hp))
    o, l = _pcall(
        body, name=name, grid=(npair // hps, nb, d),
        in_specs=[pl.BlockSpec((None, None, sub, 3 * LANE * hps), lambda hp, b, r: (b, r, 0, hp)),
                  pl.BlockSpec((2 * hps, BAND, 2 * BAND), lambda hp, b, r: (hp, 0, 0))],
        out_specs=[ospec, ospec],
        out_shape=[jax.ShapeDtypeStruct((nb, sub, d * ATT_OUT), F32)] * 2,
        compiler_params=_params("parallel", "parallel", "parallel"),
    )(qkv4, bias)
    return o.reshape(nb, sub * d, ATT_OUT), l.reshape(nb, sub * d, ATT_OUT)


def _attn_bwd(qkv4, bias, do_att, o_att, lse, d, name):
    nb, _, sub, _ = qkv4.shape
    nblk = sub // BAND
    scale = HEAD_DIM ** -0.5
    npair = ATT_OUT // LANE
    hps = _pairs_per_step(d)

    def body(qkv_ref, bias_ref, do_ref, o_ref, l_ref, dqkv_ref, db_ref):
        b, r = pl.program_id(1), pl.program_id(2)

        @pl.when((b == 0) & (r == 0))
        def _():
            db_ref[...] = jnp.zeros_like(db_ref)

        def blocks(ns, carry, with_prev):
            sides = (0, 1) if with_prev else (0,)
            chains = [(bi, i, h, sd) for bi in range(len(ns)) for i in range(hps) for h in range(2) for sd in sides]
            key_rows = lambda bi, sd: _rows(ns[bi] - sd)
            qs = {}
            for bi in range(len(ns)):
                for i in range(hps):
                    for h in range(2):
                        hl = slice(i * LANE + h * HEAD_DIM, i * LANE + (h + 1) * HEAD_DIM)
                        do = do_ref[_rows(ns[bi]), hl]
                        qs[(bi, i, h)] = (
                            qkv_ref[_rows(ns[bi]), _head_cols(i, h, 0)], do.astype(BF16),
                            jnp.sum(do * o_ref[_rows(ns[bi]), hl], -1, keepdims=True),
                            l_ref[_rows(ns[bi]), i * LANE + h * HEAD_DIM:i * LANE + h * HEAD_DIM + 1])
            raw = []
            for bi, i, h, sd in chains:
                q, do16, _, _ = qs[(bi, i, h)]
                k = qkv_ref[key_rows(bi, sd), _head_cols(i, h, 1)]
                v = qkv_ref[key_rows(bi, sd), _head_cols(i, h, 2)]
                bias_blk = bias_ref[2 * i + h, :, :BAND] if sd else bias_ref[2 * i + h, :, BAND:]
                s = lax.dot_general(q, k, NT, preferred_element_type=F32) * scale + bias_blk
                dp = lax.dot_general(do16, v, NT, preferred_element_type=F32)
                raw.append((s, dp))
            soft = []
            for (bi, i, h, sd), (s, dp) in zip(chains, raw):
                _, _, ebar, lcol = qs[(bi, i, h)]
                p = jnp.exp(s - lcol)
                ds = p * (dp - ebar)
                if sd:
                    db_ref[2 * i + h, :, :BAND] += ds
                else:
                    db_ref[2 * i + h, :, BAND:] += ds
                soft.append((p.astype(BF16), ds.astype(BF16)))
            grads = {}
            for (bi, i, h, sd), (p16, ds16) in zip(chains, soft):
                q, do16, _, _ = qs[(bi, i, h)]
                k = qkv_ref[key_rows(bi, sd), _head_cols(i, h, 1)]
                grads[(bi, i, h, sd)] = (
                    jnp.dot(ds16, k, preferred_element_type=F32),
                    lax.dot_general(ds16, q, TN, preferred_element_type=F32) * scale,
                    lax.dot_general(p16, do16, TN, preferred_element_type=F32))
            both = lambda bi, i, sd, which: jnp.concatenate(
                [grads[(bi, i, 0, sd)][which], grads[(bi, i, 1, sd)][which]], axis=1)
            carry = list(carry) if carry is not None else None
            for bi, n in enumerate(ns):
                for i in range(hps):
                    base = 3 * LANE * i
                    dq = both(bi, i, 0, 0)
                    if with_prev:
                        dq = dq + both(bi, i, 1, 0)
                        dqkv_ref[_rows(n - 1), base + LANE:base + 2 * LANE] = (
                            carry[2 * i] + both(bi, i, 1, 1)).astype(BF16)
                        dqkv_ref[_rows(n - 1), base + 2 * LANE:base + 3 * LANE] = (
                            carry[2 * i + 1] + both(bi, i, 1, 2)).astype(BF16)
                    dqkv_ref[_rows(n), base:base + LANE] = (dq * scale).astype(BF16)
                carry = [t for i in range(hps) for t in (both(bi, i, 0, 1), both(bi, i, 0, 2))]
            return tuple(carry)

        carry = _for_blocks(blocks, nblk, 2 if hps == 1 else 1, None)
        for i in range(hps):
            base = 3 * LANE * i
            dqkv_ref[_rows(nblk - 1), base + LANE:base + 2 * LANE] = carry[2 * i].astype(BF16)
            dqkv_ref[_rows(nblk - 1), base + 2 * LANE:base + 3 * LANE] = carry[2 * i + 1].astype(BF16)

    nspec = pl.BlockSpec((None, sub, hps * LANE), lambda hp, b, r: (b, 0, r * (npair // hps) + hp))
    qspec = pl.BlockSpec((None, None, sub, 3 * LANE * hps), lambda hp, b, r: (b, r, 0, hp))
    bspec = pl.BlockSpec((2 * hps, BAND, 2 * BAND), lambda hp, b, r: (hp, 0, 0))
    view = lambda t: t.reshape(nb, sub, d * ATT_OUT)
    return _pcall(
        body, name=name, grid=(npair // hps, nb, d),
        in_specs=[qspec, bspec, nspec, nspec, nspec], out_specs=[qspec, bspec],
        out_shape=[jax.ShapeDtypeStruct(qkv4.shape, BF16),
                   jax.ShapeDtypeStruct((GROUP_HEADS, BAND, 2 * BAND), F32)],
        compiler_params=_params("parallel", "arbitrary", "arbitrary"),
    )(qkv4, bias, view(do_att), view(o_att), view(lse))


def _combine_fwd(os, ls, gatt):
    nb, seq, _ = gatt.shape
    tm = 512

    def body(o0, o1, o2, l0, l1, l2, g_ref, oa_ref, oatt_ref, lse_ref):
        m = jnp.maximum(jnp.maximum(l0[...], l1[...]), l2[...])
        tot = m + jnp.log(jnp.exp(l0[...] - m) + jnp.exp(l1[...] - m) + jnp.exp(l2[...] - m))
        o = (jnp.exp(l0[...] - tot) * o0[...] + jnp.exp(l1[...] - tot) * o1[...]
             + jnp.exp(l2[...] - tot) * o2[...])
        g = g_ref[...]
        oa_ref[...] = (o * (g * _sigmoid(g))).astype(BF16)
        oatt_ref[...] = o
        lse_ref[...] = tot

    spec = pl.BlockSpec((None, tm, ATT_OUT), lambda b, i: (b, i, 0))
    return _pcall(
        body, name="attn_combine", grid=(nb, seq // tm), in_specs=[spec] * 7, out_specs=[spec] * 3,
        out_shape=[jax.ShapeDtypeStruct((nb, seq, ATT_OUT), BF16), jax.ShapeDtypeStruct((nb, seq, ATT_OUT), F32),
                   jax.ShapeDtypeStruct((nb, seq, ATT_OUT), F32)],
        compiler_params=_params("parallel", "parallel"),
    )(*os, *ls, gatt)


def _combine_bwd(doa, gatt, o_att):
    nb, seq, _ = gatt.shape
    tm = 512

    def body(doa_ref, g_ref, o_ref, do_ref, dg_ref):
        g = g_ref[...]
        sg = _sigmoid(g)
        do_ref[...] = doa_ref[...] * (g * sg)
        dg_ref[...] = (doa_ref[...] * o_ref[...] * (sg * (1.0 + g * (1.0 - sg)))).astype(BF16)

    spec = pl.BlockSpec((None, tm, ATT_OUT), lambda b, i: (b, i, 0))
    return _pcall(
        body, name="attn_combine_bwd", grid=(nb, seq // tm), in_specs=[spec] * 3, out_specs=[spec] * 2,
        out_shape=[jax.ShapeDtypeStruct((nb, seq, ATT_OUT), F32), jax.ShapeDtypeStruct((nb, seq, ATT_OUT), BF16)],
        compiler_params=_params("parallel", "parallel"),
    )(doa, gatt, o_att)


CONV_TM = 512
CONV_TC = 512


def _shift_down(cur, halo, k):
    rolled = pltpu.roll(cur, k, 0)
    hro = pltpu.roll(halo, k, 0)
    row = lax.broadcasted_iota(jnp.int32, hro.shape, 0)
    return jnp.concatenate([jnp.where(row < k, hro, rolled[:8]), rolled[8:]], axis=0)


def _shift_up(cur, halo, k):
    n = cur.shape[0]
    rolled = pltpu.roll(cur, n - k, 0)
    hro = pltpu.roll(halo, 8 - k, 0)
    row = lax.broadcasted_iota(jnp.int32, hro.shape, 0)
    return jnp.concatenate([rolled[:n - 8], jnp.where(row >= 8 - k, hro, rolled[n - 8:])], axis=0)


def _conv_pre(cur, halo, w_ref, b_ref):
    acc = cur * w_ref[3:4, :] + b_ref[...]
    for k in range(1, 4):
        acc = acc + _shift_down(cur, halo, k) * w_ref[3 - k:4 - k, :]
    return acc


def _conv_specs(seq):
    nblk = seq // CONV_TM
    cur = pl.BlockSpec((None, CONV_TM, CONV_TC), lambda cb, b, i: (b, i, cb))
    prev = pl.BlockSpec((None, 8, CONV_TC), lambda cb, b, i: (b, jnp.maximum(i * (CONV_TM // 8) - 1, 0), cb))
    nxt = pl.BlockSpec((None, 8, CONV_TC),
                       lambda cb, b, i: (b, jnp.minimum((i + 1) * (CONV_TM // 8), seq // 8 - 1), cb))
    wspec = pl.BlockSpec((4, CONV_TC), lambda cb, b, i: (0, cb))
    bspec = pl.BlockSpec((1, CONV_TC), lambda cb, b, i: (0, cb))
    return nblk, cur, prev, nxt, wspec, bspec


def _conv_fwd(xin, w4, bias, name):
    nb, seq, ch = xin.shape
    _, cur, prev, _, wspec, bspec = _conv_specs(seq)

    def body(x_ref, h_ref, w_ref, b_ref, o_ref):
        halo = jnp.where(pl.program_id(2) > 0, h_ref[...], 0.0)
        pre = _conv_pre(x_ref[...], halo, w_ref, b_ref)
        o_ref[...] = pre * _sigmoid(pre)

    return _pcall(
        body, name=name, grid=(ch // CONV_TC, nb, seq // CONV_TM),
        in_specs=[cur, prev, wspec, bspec], out_specs=cur,
        out_shape=jax.ShapeDtypeStruct(xin.shape, F32),
        compiler_params=_params("parallel", "parallel", "parallel"),
    )(xin, xin, w4, bias)


def _conv_bwd_pre(dact, xin, w4, bias, name):
    nb, seq, ch = xin.shape
    _, cur, prev, _, wspec, bspec = _conv_specs(seq)

    def body(da_ref, x_ref, h_ref, w_ref, b_ref, dp_ref, s_ref):
        b, i = pl.program_id(1), pl.program_id(2)

        @pl.when((b == 0) & (i == 0))
        def _():
            s_ref[...] = jnp.zeros_like(s_ref)

        halo = jnp.where(i > 0, h_ref[...], 0.0)
        x = x_ref[...]
        pre = _conv_pre(x, halo, w_ref, b_ref)
        sg = _sigmoid(pre)
        dpre = da_ref[...] * (sg * (1.0 + pre * (1.0 - sg)))
        dp_ref[...] = dpre
        s_ref[3:4, :] += jnp.sum(dpre * x, 0, keepdims=True)
        for k in range(1, 4):
            s_ref[3 - k:4 - k, :] += jnp.sum(dpre * _shift_down(x, halo, k), 0, keepdims=True)
        s_ref[4:5, :] += jnp.sum(dpre, 0, keepdims=True)

    return _pcall(
        body, name=name, grid=(ch // CONV_TC, nb, seq // CONV_TM),
        in_specs=[cur, cur, prev, wspec, bspec],
        out_specs=[cur, pl.BlockSpec((8, CONV_TC), lambda cb, b, i: (0, cb))],
        out_shape=[jax.ShapeDtypeStruct(xin.shape, F32), jax.ShapeDtypeStruct((8, ch), F32)],
        compiler_params=_params("parallel", "arbitrary", "arbitrary"),
    )(dact, xin, xin, w4, bias)


def _conv_bwd_x(dpre, w4, name):
    nb, seq, ch = dpre.shape
    nblk, cur, _, nxt, wspec, _ = _conv_specs(seq)

    def body(d_ref, n_ref, w_ref, o_ref):
        halo = jnp.where(pl.program_id(2) < nblk - 1, n_ref[...], 0.0)
        cur_v = d_ref[...]
        acc = cur_v * w_ref[3:4, :]
        for j in range(1, 4):
            acc = acc + _shift_up(cur_v, halo, j) * w_ref[3 - j:4 - j, :]
        o_ref[...] = acc.astype(BF16)

    return _pcall(
        body, name=name, grid=(ch // CONV_TC, nb, seq // CONV_TM),
        in_specs=[cur, nxt, wspec], out_specs=cur,
        out_shape=jax.ShapeDtypeStruct(dpre.shape, BF16),
        compiler_params=_params("parallel", "parallel", "parallel"),
    )(dpre, dpre, w4)


def _softplus_sig(dt_raw, dt_bias_row):
    nb, seq, _ = dt_raw.shape
    tm = 512

    def body(r_ref, b_ref, sp_ref, sg_ref):
        v = r_ref[...] + b_ref[...]
        sp_ref[...] = jnp.maximum(v, 0.0) + jnp.log1p(jnp.exp(-jnp.abs(v)))
        sg_ref[...] = _sigmoid(v)

    spec = pl.BlockSpec((None, tm, LANE), lambda b, i: (b, i, 0))
    return _pcall(
        body, name="dt_softplus", grid=(nb, seq // tm),
        in_specs=[spec, pl.BlockSpec((1, LANE), lambda b, i: (0, 0))], out_specs=[spec, spec],
        out_shape=[jax.ShapeDtypeStruct(dt_raw.shape, F32)] * 2,
        compiler_params=_params("parallel", "parallel"),
    )(dt_raw, dt_bias_row)


def _group_lanes(t):
    pads = [(0, 0)] * (t.ndim - 1) + [(0, LANE - GROUP_SSM_HEADS)]
    return jnp.stack([jnp.pad(t[..., GROUP_SSM_HEADS * g:GROUP_SSM_HEADS * (g + 1)], pads) for g in range(SSM_GROUPS)])


def _ungroup_lanes(t):
    return jnp.concatenate([t[g][..., :GROUP_SSM_HEADS] for g in range(SSM_GROUPS)], axis=-1)


def _decays(dt, al_ref):
    row = lax.broadcasted_iota(jnp.int32, (CHUNK, CHUNK), 0)
    col = lax.broadcasted_iota(jnp.int32, (CHUNK, CHUNK), 1)
    tril = (row >= col).astype(F32)
    triu = (row <= col).astype(F32)
    arow = -jnp.exp(al_ref[...])
    a = dt * arow
    acs = jnp.dot(tril, a, precision=HIGHEST, preferred_element_type=F32)
    acs_t = lax.dot_general(a, triu, (((0,), (0,)), ((), ())), precision=HIGHEST, preferred_element_type=F32)
    return arow, acs, acs_t, row >= col, triu


def _ssd_specs(nb, seq):
    nc = seq // CHUNK
    hw = GROUP_SSM_HEADS * HEAD_DIM

    def mk(rev):
        cidx = (lambda c: nc - 1 - c) if rev else (lambda c: c)
        wide = pl.BlockSpec((None, CHUNK, hw), lambda g, b, c: (b, cidx(c), g))
        state = pl.BlockSpec((None, CHUNK, D_STATE), lambda g, b, c: (b, cidx(c), g))
        lanes = pl.BlockSpec((None, None, CHUNK, LANE), lambda g, b, c: (g, b, cidx(c), 0))
        prev = pl.BlockSpec((None, None, None, D_STATE, hw), lambda g, b, c: (b, cidx(c), g, 0, 0))
        return wide, state, lanes, prev

    grow = pl.BlockSpec((None, 1, LANE), lambda g, b, c: (g, 0, 0))
    nwspec = pl.BlockSpec((1, hw), lambda g, b, c: (0, g))
    return nc, hw, mk, grow, nwspec


def _head_expand():
    hw = GROUP_SSM_HEADS * HEAD_DIM
    r = lax.broadcasted_iota(jnp.int32, (LANE, hw), 0)
    c = lax.broadcasted_iota(jnp.int32, (LANE, hw), 1)
    return ((c // HEAD_DIM) == r).astype(BF16)


def _split3(v):
    hi = v.astype(BF16)
    rest = v - hi.astype(F32)
    mid = rest.astype(BF16)
    return hi, mid, (rest - mid.astype(F32)).astype(BF16)


def _to_channels(v, e):
    hi, mid, lo = _split3(v)
    dot = lambda t: jnp.dot(t, e, preferred_element_type=F32)
    return (dot(hi) + dot(mid)) + dot(lo)


def _to_heads(w, e):
    hi, mid, lo = _split3(w)
    dot = lambda t: lax.dot_general(t, e, (((1,), (1,)), ((), ())), preferred_element_type=F32)
    return (dot(hi) + dot(mid)) + dot(lo)


def _row8(v):
    return jnp.broadcast_to(v, (8, v.shape[1]))


def _ssd_chunk_setup(dt, al_ref, ds_ref):
    arow, acs, acs_t, causal, triu = _decays(dt, al_ref)
    e = _head_expand()
    dtx = _to_channels(dt, e)
    acsx = _to_channels(acs, e)
    lastx = acsx[CHUNK - 1:CHUNK, :]
    dskx = _to_channels(_row8(ds_ref[...]), e)[0:1, :]
    return arow, acs, acs_t, causal, triu, e, dtx, acsx, lastx, dskx


def _ssd_fwd(xs, bm, cm, dtg, z, alog_g, dskip_g, normw):
    nb, seq, _ = xs.shape
    nc, hw, mk, grow, nwspec = _ssd_specs(nb, seq)
    wide, state, lanes, prev = mk(False)
    tn = (((0,), (0,)), ((), ()))

    def body(xs_ref, b_ref, c_ref, dt_ref, z_ref, al_ref, ds_ref, nw_ref, ys_ref, y_ref, sp_ref, st_ref):
        @pl.when(pl.program_id(2) == 0)
        def _():
            st_ref[...] = jnp.zeros_like(st_ref)

        dt = dt_ref[...]
        _, acs, acs_t, causal, _, _, dtx, acsx, lastx, dskx = _ssd_chunk_setup(dt, al_ref, ds_ref)
        bmat = b_ref[...].astype(BF16)
        cmat = c_ref[...].astype(BF16)
        cb = lax.dot_general(cmat, bmat, (((1,), (1,)), ((), ())), preferred_element_type=F32)
        x = xs_ref[...]
        xdt = x * dtx
        xdt16 = xdt.astype(BF16)
        first_head = lax.broadcasted_iota(jnp.int32, (CHUNK, LANE), 1) < HEAD_DIM
        pairs = []
        for hp in range(GROUP_SSM_HEADS // 2):
            xp = xdt16[:, hp * LANE:(hp + 1) * LANE]
            two = []
            for j in (2 * hp, 2 * hp + 1):
                lmat = jnp.exp(jnp.where(causal, acs[:, j:j + 1] - acs_t[j:j + 1, :], -jnp.inf))
                two.append(jnp.dot((cb * lmat).astype(BF16), xp, preferred_element_type=F32))
            pairs.append(jnp.where(first_head, two[0], two[1]))
        yd = jnp.concatenate(pairs, axis=1)
        s_prev = st_ref[...]
        s16 = s_prev.astype(BF16)
        sp_ref[...] = s16
        yo = jnp.dot(cmat, s16, preferred_element_type=F32) * jnp.exp(acsx)
        sts = lax.dot_general(bmat, (xdt * jnp.exp(lastx - acsx)).astype(BF16), tn, preferred_element_type=F32)
        st_ref[...] = s_prev * jnp.exp(lastx) + sts
        y = yd + yo + dskx * x
        zz = z_ref[...]
        u = y * (zz * _sigmoid(zz))
        rn = lax.rsqrt(jnp.mean(u * u, -1, keepdims=True) + RMS_EPS)
        ys_ref[...] = (u * rn * nw_ref[...]).astype(BF16)
        y_ref[...] = y

    return _pcall(
        body, name="ssd_fwd", grid=(SSM_GROUPS, nb, nc),
        in_specs=[wide, state, state, lanes, wide, grow, grow, nwspec],
        out_specs=[wide, wide, prev],
        out_shape=[jax.ShapeDtypeStruct((nb, seq, D_INNER), BF16), jax.ShapeDtypeStruct((nb, seq, D_INNER), F32),
                   jax.ShapeDtypeStruct((nb, nc, SSM_GROUPS, D_STATE, hw), BF16)],
        scratch_shapes=[pltpu.VMEM((D_STATE, hw), F32)],
        compiler_params=_params("parallel", "parallel", "arbitrary"),
    )(xs, bm, cm, dtg, z, alog_g, dskip_g, normw)


def _ssd_bwd(xs, bm, cm, dtg, sgg, z, y, dys, sprev, alog_g, dskip_g, normw):
    nb, seq, _ = xs.shape
    nc, hw, mk, grow, nwspec = _ssd_specs(nb, seq)
    wide, state, lanes, prev = mk(True)
    nt = (((1,), (1,)), ((), ()))
    tn = (((0,), (0,)), ((), ()))

    def body(xs_ref, b_ref, c_ref, dt_ref, sg_ref, z_ref, y_ref, dys_ref, sp_ref, al_ref, ds_ref, nw_ref,
             dxs_ref, db_ref, dc_ref, ddt_ref, dz_ref, small_ref, dnw_ref, g_ref):
        b, c = pl.program_id(1), pl.program_id(2)

        @pl.when((b == 0) & (c == 0))
        def _():
            small_ref[...] = jnp.zeros_like(small_ref)
            dnw_ref[...] = jnp.zeros_like(dnw_ref)

        @pl.when(c == 0)
        def _():
            g_ref[...] = jnp.zeros_like(g_ref)

        yv, zz, dys_v, nw = y_ref[...], z_ref[...], dys_ref[...], nw_ref[...]
        sz = _sigmoid(zz)
        silu = zz * sz
        u = yv * silu
        rn = lax.rsqrt(jnp.mean(u * u, -1, keepdims=True) + RMS_EPS)
        gn = dys_v * nw
        du = rn * gn - u * (rn * rn * rn) * jnp.mean(u * gn, -1, keepdims=True)
        dnw_ref[...] += jnp.sum(dys_v * u * rn, 0, keepdims=True)
        dy = du * silu
        dz_ref[...] = du * yv * (sz * (1.0 + zz * (1.0 - sz)))

        dt = dt_ref[...]
        arow, acs, acs_t, causal, triu, e, dtx, acsx, lastx, dskx = _ssd_chunk_setup(dt, al_ref, ds_ref)
        dfsx = jnp.exp(acsx)
        dtex = jnp.exp(lastx - acsx)
        bmat = b_ref[...].astype(BF16)
        cmat = c_ref[...].astype(BF16)
        cb = lax.dot_general(cmat, bmat, nt, preferred_element_type=F32)
        x = xs_ref[...]
        xdt = x * dtx
        xdt16 = xdt.astype(BF16)
        xdte = xdt * dtex
        dy16 = dy.astype(BF16)
        dyd = dy * dfsx
        dyd16 = dyd.astype(BF16)
        s16 = sp_ref[...]
        g = g_ref[...]
        g16 = g.astype(BF16)
        cs = jnp.dot(cmat, s16, preferred_element_type=F32)
        dc_off = lax.dot_general(dyd16, s16, nt, preferred_element_type=F32)
        g_here = lax.dot_general(cmat, dyd16, tn, preferred_element_type=F32)
        bg = jnp.dot(bmat, g16, preferred_element_type=F32)
        db_st = lax.dot_general(xdte.astype(BF16), g16, nt, preferred_element_type=F32)
        ddte_w = bg * xdte
        dcd = _to_heads(_row8(jnp.sum(g * s16.astype(F32), 0, keepdims=True)), e)[0:1, :]
        lane = lax.broadcasted_iota(jnp.int32, (CHUNK, LANE), 1)
        first_head = lane < HEAD_DIM
        sub = lax.broadcasted_iota(jnp.int32, (CHUNK, LANE), 0)
        dacs = jnp.zeros((CHUNK, LANE), F32)
        colsums = jnp.zeros((CHUNK, LANE), F32)
        dcb = jnp.zeros((CHUNK, CHUNK), F32)
        pairs = []
        for hp in range(GROUP_SSM_HEADS // 2):
            xp = xdt16[:, hp * LANE:(hp + 1) * LANE]
            dyp = dy16[:, hp * LANE:(hp + 1) * LANE]
            two = []
            for idx, j in enumerate((2 * hp, 2 * hp + 1)):
                lmat = jnp.exp(jnp.where(causal, acs[:, j:j + 1] - acs_t[j:j + 1, :], -jnp.inf))
                mf = cb * lmat
                dy_h = jnp.where(first_head if idx == 0 else jnp.logical_not(first_head), dyp, jnp.zeros_like(dyp))
                dm = lax.dot_general(dy_h, xp, nt, preferred_element_type=F32)
                two.append(lax.dot_general(mf.astype(BF16), dyp, tn, preferred_element_type=F32))
                wmat = dm * mf
                dcb = dcb + dm * lmat
                dacs = jnp.where(lane == j, jnp.sum(wmat, -1, keepdims=True), dacs)
                colsums = jnp.where(sub == j, jnp.sum(wmat, 0, keepdims=True), colsums)
            pairs.append(jnp.where(first_head, two[0], two[1]))
        dxdt = bg * dtex + jnp.concatenate(pairs, axis=1)
        dacs = dacs - colsums.T + _to_heads(dyd * cs - ddte_w, e)
        cd_row = jnp.exp(acs[CHUNK - 1:CHUNK, :])
        tail = _to_heads(_row8(jnp.sum(ddte_w, 0, keepdims=True)), e)[0:1, :] + dcd * cd_row
        dacs = dacs + jnp.where(sub == CHUNK - 1, tail, 0.0)
        da = jnp.dot(triu, dacs, precision=HIGHEST, preferred_element_type=F32)
        ddt_raw = (da * arow + _to_heads(dxdt * x, e)) * sg_ref[...]
        ddt_ref[...] = ddt_raw
        small_ref[0:1, :] += jnp.sum(da * dt, 0, keepdims=True) * arow
        small_ref[1:2, :] += _to_heads(_row8(jnp.sum(dy * x, 0, keepdims=True)), e)[0:1, :]
        small_ref[2:3, :] += jnp.sum(ddt_raw, 0, keepdims=True)
        dcb16 = dcb.astype(BF16)
        dc_ref[...] = dc_off + jnp.dot(dcb16, bmat, preferred_element_type=F32)
        db_ref[...] = db_st + lax.dot_general(dcb16, cmat, tn, preferred_element_type=F32)
        dxs_ref[...] = dxdt * dtx + dskx * dy
        g_ref[...] = g * jnp.exp(lastx) + g_here

    return _pcall(
        body, name="ssd_bwd", grid=(SSM_GROUPS, nb, nc),
        in_specs=[wide, state, state, lanes, lanes, wide, wide, wide, prev, grow, grow, nwspec],
        out_specs=[wide, state, state, lanes, wide,
                   pl.BlockSpec((None, 8, LANE), lambda g, b, c: (g, 0, 0)), nwspec],
        out_shape=[jax.ShapeDtypeStruct((nb, seq, D_INNER), F32),
                   jax.ShapeDtypeStruct((nb, seq, SSM_GROUPS * D_STATE), F32),
                   jax.ShapeDtypeStruct((nb, seq, SSM_GROUPS * D_STATE), F32),
                   jax.ShapeDtypeStruct((SSM_GROUPS, nb, seq, LANE), F32),
                   jax.ShapeDtypeStruct((nb, seq, D_INNER), F32),
                   jax.ShapeDtypeStruct((SSM_GROUPS, 8, LANE), F32),
                   jax.ShapeDtypeStruct((1, D_INNER), F32)],
        scratch_shapes=[pltpu.VMEM((D_STATE, hw), F32)],
        compiler_params=_params("parallel", "arbitrary", "arbitrary"),
    )(xs, bm, cm, dtg, sgg, z, y, dys, sprev, alog_g, dskip_g, normw)


EW_TM = 256


def _merge_fwd(y_a, y_b, gm, bgate):
    nb, seq, _ = y_a.shape

    def body(a_ref, b_ref, ga_ref, gb_ref, bg_ref, o_ref):
        sa = _sigmoid(ga_ref[...] + bg_ref[0:1, :])
        sb = _sigmoid(gb_ref[...] + bg_ref[1:2, :])
        o_ref[...] = (sa * a_ref[...] + sb * b_ref[...]).astype(BF16)

    spec = pl.BlockSpec((None, EW_TM, D_MODEL), lambda b, i: (b, i, 0))
    spec1 = pl.BlockSpec((None, EW_TM, D_MODEL), lambda b, i: (b, i, 1))
    return _pcall(
        body, name="merge_fwd", grid=(nb, seq // EW_TM),
        in_specs=[spec, spec, spec, spec1, pl.BlockSpec((8, D_MODEL), lambda b, i: (0, 0))], out_specs=spec,
        out_shape=jax.ShapeDtypeStruct((nb, seq, D_MODEL), BF16),
        compiler_params=_params("parallel", "parallel"),
    )(y_a, y_b, gm, gm, bgate)


def _merge_bwd(dmerged, y_a, y_b, gm, bgate):
    nb, seq, _ = y_a.shape

    def body(dm_ref, a_ref, b_ref, ga_ref, gb_ref, bg_ref, dya_ref, dyb_ref, dg_ref, s_ref):
        @pl.when((pl.program_id(0) == 0) & (pl.program_id(1) == 0))
        def _():
            s_ref[...] = jnp.zeros_like(s_ref)

        dm = dm_ref[...]
        sa = _sigmoid(ga_ref[...] + bg_ref[0:1, :])
        sb = _sigmoid(gb_ref[...] + bg_ref[1:2, :])
        dya_ref[...] = (dm * sa).astype(BF16)
        dyb_ref[...] = (dm * sb).astype(BF16)
        dga = dm * a_ref[...] * (sa * (1.0 - sa))
        dgb = dm * b_ref[...] * (sb * (1.0 - sb))
        dg_ref[:, :D_MODEL] = dga.astype(BF16)
        dg_ref[:, D_MODEL:] = dgb.astype(BF16)
        s_ref[0:1, :] += jnp.sum(dga, 0, keepdims=True)
        s_ref[1:2, :] += jnp.sum(dgb, 0, keepdims=True)

    spec = pl.BlockSpec((None, EW_TM, D_MODEL), lambda b, i: (b, i, 0))
    spec1 = pl.BlockSpec((None, EW_TM, D_MODEL), lambda b, i: (b, i, 1))
    small = pl.BlockSpec((8, D_MODEL), lambda b, i: (0, 0))
    return _pcall(
        body, name="merge_bwd", grid=(nb, seq // EW_TM),
        in_specs=[spec, spec, spec, spec, spec1, small],
        out_specs=[spec, spec, pl.BlockSpec((None, EW_TM, 2 * D_MODEL), lambda b, i: (b, i, 0)), small],
        out_shape=[jax.ShapeDtypeStruct((nb, seq, D_MODEL), BF16), jax.ShapeDtypeStruct((nb, seq, D_MODEL), BF16),
                   jax.ShapeDtypeStruct((nb, seq, 2 * D_MODEL), BF16), jax.ShapeDtypeStruct((8, D_MODEL), F32)],
        compiler_params=_params("arbitrary", "arbitrary"),
    )(dmerged, y_a, y_b, gm, gm, bgate)


def _ln_loss(x, mix, gp, pw, target, bgate, ln_g, ln_b):
    nb, seq, _ = x.shape

    def body(x_ref, mix_ref, gp_ref, pw_ref, t_ref, bg_ref, g_ref, b_ref, dx_ref, dp_ref, dpw_ref, dgp_ref, s_ref):
        @pl.when((pl.program_id(0) == 0) & (pl.program_id(1) == 0))
        def _():
            s_ref[...] = jnp.zeros_like(s_ref)

        sp = _sigmoid(gp_ref[...] + bg_ref[2:3, :])
        pw = pw_ref[...]
        pre = ALPHA * x_ref[...] + mix_ref[...] + sp * pw
        mu = jnp.mean(pre, -1, keepdims=True)
        cen = pre - mu
        rstd = lax.rsqrt(jnp.mean(cen * cen, -1, keepdims=True) + LN_EPS)
        xhat = cen * rstd
        err = xhat * g_ref[...] + b_ref[...] - t_ref[...]
        dy = err * (1.0 / D_MODEL)
        dxh = dy * g_ref[...]
        dpre = rstd * (dxh - jnp.mean(dxh, -1, keepdims=True) - xhat * jnp.mean(dxh * xhat, -1, keepdims=True))
        dx_ref[...] = ALPHA * dpre
        dp_ref[...] = dpre.astype(BF16)
        dpw_ref[...] = (dpre * sp).astype(BF16)
        dgp = dpre * pw * (sp * (1.0 - sp))
        dgp_ref[...] = dgp.astype(BF16)
        s_ref[0:1, :] += jnp.sum(dy * xhat, 0, keepdims=True)
        s_ref[1:2, :] += jnp.sum(dy, 0, keepdims=True)
        s_ref[2:3, :] += jnp.sum(dgp, 0, keepdims=True)
        s_ref[3:4, :] += jnp.sum(err * err, 0, keepdims=True)

    spec = pl.BlockSpec((None, EW_TM, D_MODEL), lambda b, i: (b, i, 0))
    small = pl.BlockSpec((8, D_MODEL), lambda b, i: (0, 0))
    row = pl.BlockSpec((1, D_MODEL), lambda b, i: (0, 0))
    return _pcall(
        body, name="ln_loss", grid=(nb, seq // EW_TM),
        in_specs=[spec] * 5 + [small, row, row], out_specs=[spec] * 4 + [small],
        out_shape=[jax.ShapeDtypeStruct((nb, seq, D_MODEL), F32)] + [jax.ShapeDtypeStruct((nb, seq, D_MODEL), BF16)] * 3
        + [jax.ShapeDtypeStruct((8, D_MODEL), F32)],
        compiler_params=_params("arbitrary", "arbitrary"),
    )(x, mix, gp, pw, target, bgate, ln_g, ln_b)


def _adamw(w, g, m, v, name):
    rows, cols = w.shape
    tr = _row_tile(rows, cols, 8, 5 << 19)
    c1 = 1.0 - ADAM_B1 ** ADAM_STEP
    c2 = 1.0 - ADAM_B2 ** ADAM_STEP

    def body(w_ref, g_ref, m_ref, v_ref, d_ref, nm_ref, nv_ref):
        gv = g_ref[...]
        nm = ADAM_B1 * m_ref[...] + (1.0 - ADAM_B1) * gv
        nv = ADAM_B2 * v_ref[...] + (1.0 - ADAM_B2) * (gv * gv)
        d_ref[...] = -ADAM_LR * ((nm / c1) / (jnp.sqrt(nv / c2) + ADAM_EPS) + ADAM_WD * w_ref[...])
        nm_ref[...] = nm
        nv_ref[...] = nv

    spec = pl.BlockSpec((tr, cols), lambda i: (i, 0))
    return _pcall(
        body, name=name, grid=(rows // tr,), in_specs=[spec] * 4, out_specs=[spec] * 3,
        out_shape=[jax.ShapeDtypeStruct(w.shape, F32)] * 3, compiler_params=_params("parallel"),
    )(w, g, m, v)


def _sum_rows(parts, out_dtype, name):
    rows, cols = parts[0].shape
    tr = rows
    for cand in range(16, rows, 16):
        if rows % cand == 0 and cand * cols * 4 <= (1 << 20):
            tr = cand
    n = len(parts)

    def body(*refs):
        acc = refs[0][...].astype(F32)
        for r in refs[1:n]:
            acc = acc + r[...].astype(F32)
        refs[n][...] = acc.astype(out_dtype)

    spec = pl.BlockSpec((tr, cols), lambda i: (i, 0))
    return _pcall(
        body, name=name, grid=(rows // tr,), in_specs=[spec] * n, out_specs=spec,
        out_shape=jax.ShapeDtypeStruct((rows, cols), out_dtype), compiler_params=_params("parallel"),
    )(*parts)


def _place():
    return lax.axis_index("x"), lax.axis_index("y"), lax.axis_index("c")


def _other_chips(x, y):
    return [(1 - x, y), (x, 1 - y), (1 - x, 1 - y)]


def _remote(src, dst, send_sem, recv_sem, to):
    return pltpu.make_async_remote_copy(src_ref=src, dst_ref=dst, send_sem=send_sem, recv_sem=recv_sem,
                                        device_id=to, device_id_type=MESH)


ANY = pl.BlockSpec(memory_space=pl.ANY)
DMA_CHUNK_BYTES = 512 * 1024


def _row_chunks(rows, row_bytes):
    per = max(16, DMA_CHUNK_BYTES // row_bytes // 16 * 16)
    return [(s, min(per, rows - s)) for s in range(0, rows, per)]


def _row_tile(rows, cols, align, limit=1 << 21):
    best = None
    for cand in range(align, rows + 1, align):
        if rows % cand == 0 and cand * cols * 4 <= limit:
            best = cand
    return best or rows


def _allgather_pieces(pieces):
    n = len(pieces)
    halves = [_row_chunks(p.shape[0] // 2, p.shape[1] * p.dtype.itemsize) for p in pieces]
    wholes = [_row_chunks(p.shape[0], p.shape[1] * p.dtype.itemsize) for p in pieces]
    n_ici = 3 * sum(len(h) for h in halves)
    n_loc = sum(len(w) for w in wholes)

    def body(*refs):
        ins, outs = refs[:n], refs[n:2 * n]
        send_sems, recv_sems, local_sems = refs[2 * n:]
        x, y, c = _place()
        me = 2 * x + y
        sibling = (x, y, 1 - c)
        chips = _other_chips(x, y)
        locals_ = []
        for a in range(n):
            for s, m in wholes[a]:
                loc = pltpu.make_async_copy(ins[a].at[pl.ds(s, m)], outs[a].at[me, pl.ds(s, m)],
                                            local_sems.at[len(locals_)])
                loc.start()
                locals_.append(loc)
        ici = []
        for a in range(n):
            half = ins[a].shape[0] // 2
            for s, m in halves[a]:
                for j, (cx, cy) in enumerate(chips):
                    k = len(ici)
                    cp = _remote(ins[a].at[pl.ds(c * half + s, m)], outs[a].at[me, pl.ds(c * half + s, m)],
                                 send_sems.at[k], recv_sems.at[k], (cx, cy, c))
                    cp.start()
                    ici.append((a, s, m, j, cp))
        passed = []
        for k, (a, s, m, j, _) in enumerate(ici):
            half = ins[a].shape[0] // 2
            cx, cy = chips[j]
            blk = outs[a].at[2 * cx + cy, pl.ds(c * half + s, m)]
            _remote(blk, blk, send_sems.at[k], recv_sems.at[k], (cx, cy, c)).wait_recv()
            fw = _remote(blk, blk, send_sems.at[n_ici + k], recv_sems.at[n_ici + k], sibling)
            fw.start()
            passed.append(fw)
        for k, (a, s, m, j, _) in enumerate(ici):
            half = ins[a].shape[0] // 2
            cx, cy = chips[j]
            blk = outs[a].at[2 * cx + cy, pl.ds((1 - c) * half + s, m)]
            _remote(blk, blk, send_sems.at[n_ici + k], recv_sems.at[n_ici + k], sibling).wait_recv()
        for item in ici:
            item[4].wait_send()
        for fw in passed:
            fw.wait_send()
        for loc in locals_:
            loc.wait()

    return _pcall(
        body, name="allgather_weights", in_specs=[ANY] * n, out_specs=[ANY] * n,
        out_shape=[jax.ShapeDtypeStruct((4,) + p.shape, p.dtype) for p in pieces],
        scratch_shapes=[pltpu.SemaphoreType.DMA((2 * n_ici,)), pltpu.SemaphoreType.DMA((2 * n_ici,)),
                        pltpu.SemaphoreType.DMA((n_loc,))],
        compiler_params=pltpu.CompilerParams(has_side_effects=True),
    )(*pieces)


def _sibling_exchange(grads):
    n = len(grads)
    chunks = [_row_chunks(g.shape[1] // 2, g.shape[2] * g.dtype.itemsize) for g in grads]
    n_sem = 4 * sum(len(ch) for ch in chunks)

    def body(*refs):
        ins, gots = refs[:n], refs[n:2 * n]
        send_sems, recv_sems = refs[2 * n:]
        x, y, c = _place()
        sibling = (x, y, 1 - c)
        work = []
        for a in range(n):
            half = ins[a].shape[1] // 2
            for piece in range(4):
                for s, m in chunks[a]:
                    k = len(work)
                    cp = _remote(ins[a].at[piece, pl.ds((1 - c) * half + s, m)], gots[a].at[piece, pl.ds(s, m)],
                                 send_sems.at[k], recv_sems.at[k], sibling)
                    cp.start()
                    work.append(cp)
        for cp in work:
            cp.wait()

    return _pcall(
        body, name="grad_sibling_exchange", in_specs=[ANY] * n, out_specs=[ANY] * n,
        out_shape=[jax.ShapeDtypeStruct((4, g.shape[1] // 2, g.shape[2]), g.dtype) for g in grads],
        scratch_shapes=[pltpu.SemaphoreType.DMA((n_sem,)), pltpu.SemaphoreType.DMA((n_sem,))],
        compiler_params=pltpu.CompilerParams(has_side_effects=True),
    )(*grads)


def _chip_scatter(sums):
    n = len(sums)
    chunks = [_row_chunks(s.shape[1], s.shape[2] * s.dtype.itemsize) for s in sums]
    n_sem = 3 * sum(len(ch) for ch in chunks)

    def body(*refs):
        ins, gots = refs[:n], refs[n:2 * n]
        send_sems, recv_sems = refs[2 * n:]
        x, y, c = _place()
        chips = _other_chips(x, y)
        work = []
        for a in range(n):
            for s, m in chunks[a]:
                for j, (cx, cy) in enumerate(chips):
                    k = len(work)
                    cp = _remote(ins[a].at[2 * cx + cy, pl.ds(s, m)], gots[a].at[j, pl.ds(s, m)],
                                 send_sems.at[k], recv_sems.at[k], (cx, cy, c))
                    cp.start()
                    work.append(cp)
        for cp in work:
            cp.wait()

    return _pcall(
        body, name="grad_chip_scatter", in_specs=[ANY] * n, out_specs=[ANY] * n,
        out_shape=[jax.ShapeDtypeStruct((3,) + s.shape[1:], s.dtype) for s in sums],
        scratch_shapes=[pltpu.SemaphoreType.DMA((n_sem,)), pltpu.SemaphoreType.DMA((n_sem,))],
        compiler_params=pltpu.CompilerParams(has_side_effects=True),
    )(*sums)


def _sibling_gather(fulls):
    n = len(fulls)
    chunks = [_row_chunks(f.shape[0] // 2, f.shape[1] * f.dtype.itemsize) for f in fulls]
    n_sem = sum(len(ch) for ch in chunks)

    def body(*refs):
        outs = refs[n:2 * n]
        send_sems, recv_sems = refs[2 * n:]
        x, y, c = _place()
        sibling = (x, y, 1 - c)
        work = []
        for a in range(n):
            h = outs[a].shape[0] // 2
            for s, m in chunks[a]:
                k = len(work)
                mine = outs[a].at[pl.ds(c * h + s, m)]
                cp = _remote(mine, mine, send_sems.at[k], recv_sems.at[k], sibling)
                cp.start()
                work.append((a, s, m, cp))
        for k, (a, s, m, cp) in enumerate(work):
            h = outs[a].shape[0] // 2
            cp.wait_send()
            theirs = outs[a].at[pl.ds((1 - c) * h + s, m)]
            _remote(theirs, theirs, send_sems.at[k], recv_sems.at[k], sibling).wait_recv()

    return _pcall(
        body, name="grad_sibling_gather", in_specs=[ANY] * n, out_specs=[ANY] * n,
        out_shape=[jax.ShapeDtypeStruct(f.shape, f.dtype) for f in fulls],
        input_output_aliases={a: a for a in range(n)},
        scratch_shapes=[pltpu.SemaphoreType.DMA((n_sem,)), pltpu.SemaphoreType.DMA((n_sem,))],
        compiler_params=pltpu.CompilerParams(has_side_effects=True),
    )(*fulls)


def _pair_sum(grad, got, place, name):
    _, rows, cols = grad.shape
    half = rows // 2
    tr = _row_tile(half, cols, 16)

    def body(p_ref, a_ref, b_ref, o_ref):
        o_ref[...] = (a_ref[...].astype(F32) + b_ref[...].astype(F32)).astype(BF16)

    return _pcall(
        body, name=name,
        grid_spec=pltpu.PrefetchScalarGridSpec(
            num_scalar_prefetch=1, grid=(4, half // tr),
            in_specs=[pl.BlockSpec((None, tr, cols), lambda k, i, p: (k, p[1] * (half // tr) + i, 0)),
                      pl.BlockSpec((None, tr, cols), lambda k, i, p: (k, i, 0))],
            out_specs=pl.BlockSpec((None, tr, cols), lambda k, i, p: (k, i, 0))),
        out_shape=jax.ShapeDtypeStruct((4, half, cols), BF16),
        compiler_params=_params("parallel", "parallel"),
    )(place, grad, got)


def _chip_sum(sums, got, place, name):
    _, h, cols = sums.shape
    tr = _row_tile(h, cols, 16)

    def body(p_ref, own_ref, g0, g1, g2, o_ref):
        o_ref[...] = ((own_ref[...].astype(F32) + g0[...].astype(F32)) + g1[...].astype(F32)) + g2[...].astype(F32)

    gspec = lambda j: pl.BlockSpec((None, tr, cols), lambda i, p: (j, i, 0))
    return _pcall(
        body, name=name,
        grid_spec=pltpu.PrefetchScalarGridSpec(
            num_scalar_prefetch=1, grid=(h // tr,),
            in_specs=[pl.BlockSpec((None, tr, cols), lambda i, p: (p[0], i, 0)), gspec(0), gspec(1), gspec(2)],
            out_specs=pl.BlockSpec((tr, cols), lambda i, p: (p[1] * (h // tr) + i, 0))),
        out_shape=jax.ShapeDtypeStruct((2 * h, cols), F32),
        compiler_params=_params("parallel"),
    )(place, sums, got, got, got)


def _allgather8(buf, name):
    rows = buf.shape[0]

    def body(in_ref, out_ref, send_sems, recv_sems):
        x, y, c = _place()
        me = 4 * x + 2 * y + c
        out_ref[me] = in_ref[...]
        work = []
        for rel in range(1, 8):
            fx, fy, fc = (rel >> 2) & 1, (rel >> 1) & 1, rel & 1
            to = (x ^ fx, y ^ fy, c ^ fc)
            cp = _remote(in_ref, out_ref.at[me], send_sems.at[rel - 1], recv_sems.at[rel - 1], to)
            cp.start()
            work.append((cp, 4 * to[0] + 2 * to[1] + to[2]))
        for rel, (cp, frm) in enumerate(work):
            cp.wait_send()
            blk = out_ref.at[frm]
            _remote(blk, blk, send_sems.at[rel], recv_sems.at[rel], (x, y, c)).wait_recv()

    return _pcall(
        body, name=name, in_specs=[pl.BlockSpec(memory_space=pltpu.VMEM)],
        out_specs=pl.BlockSpec(memory_space=pltpu.VMEM),
        out_shape=jax.ShapeDtypeStruct((8, rows, LANE), F32),
        scratch_shapes=[pltpu.SemaphoreType.DMA((7,)), pltpu.SemaphoreType.DMA((7,))],
        compiler_params=pltpu.CompilerParams(has_side_effects=True),
    )(buf)


def _reduce_scatter(grads):
    x, y, c = _place()
    place = jnp.stack([2 * x + y, c]).astype(jnp.int32)
    got = _sibling_exchange(grads)
    chip_sums = [_pair_sum(g, t, place, "grad_pair_sum_%d" % i) for i, (g, t) in enumerate(zip(grads, got))]
    others = _chip_scatter(chip_sums)
    fulls = [_chip_sum(s, t, place, "grad_chip_sum_%d" % i) for i, (s, t) in enumerate(zip(chip_sums, others))]
    return _sibling_gather(fulls)


def _pack_rows(arrs):
    parts = []
    for a in arrs:
        f = a.reshape(-1).astype(F32)
        parts.append(jnp.pad(f, (0, (-f.shape[0]) % LANE)))
    flat = jnp.concatenate(parts)
    rows = -(-flat.shape[0] // LANE)
    rows8 = -(-rows // 8) * 8
    return jnp.pad(flat, (0, rows8 * LANE - flat.shape[0])).reshape(rows8, LANE)


def _unpack_rows(buf, shapes):
    flat = buf.reshape(-1)
    outs, off = [], 0
    for s in shapes:
        n = int(np.prod(s))
        outs.append(flat[off:off + n].reshape(s))
        off += -(-n // LANE) * LANE
    return outs


def _local_grads(x, p, target, wseg, w_br16, w_out16, w_ple16, b_gate, conv_w, conv_b, dt_bias, a_log, d_skip,
                 ssm_norm_w, ln_g, ln_b, rel_bias):
    nb, seq, _ = x.shape
    bmaps = jnp.asarray(_bucket_maps())
    bias = _bias_tables(rel_bias, bmaps)
    bgate8 = jnp.pad(b_gate, ((0, 5), (0, 0)))
    dils = [d for _, d in PATTERNS]

    x16 = x.astype(BF16)
    p16 = p.astype(BF16)
    x16p = [_permute(x16, d) for d in dils]
    qkv = [_proj(x16p[g], [wseg["qkv%d" % g]], BF16, "proj_qkv%d" % g, True)[0].reshape(
        nb, dils[g], seq // dils[g], -1) for g in range(3)]
    nat = {}
    for gi, (group, tm) in enumerate(NAT_GROUPS):
        outs = _proj(x16, [wseg[s] for s in group], F32, "proj_nat%d" % gi, True, tm)
        nat.update(zip(group, outs))
    att = [_attn_fwd(qkv[g], bias[g * GROUP_HEADS:(g + 1) * GROUP_HEADS], dils[g], "attn_fwd%d" % g) for g in range(3)]
    oa, o_att, lse = _combine_fwd([a[0] for a in att], [a[1] for a in att], nat["gatt"])

    cw = {"xs": (conv_w[:, :D_INNER], conv_b[:, :D_INNER]),
          "bm": (conv_w[:, D_INNER:D_INNER + 512], conv_b[:, D_INNER:D_INNER + 512]),
          "cm": (conv_w[:, D_INNER + 512:], conv_b[:, D_INNER + 512:])}
    act = {s: _conv_fwd(nat[s], cw[s][0], cw[s][1], "conv_fwd_" + s) for s in ("xs", "bm", "cm")}
    dt_sp, dt_sg = _softplus_sig(nat["dt"], jnp.pad(dt_bias, ((0, 0), (0, LANE - SSM_HEADS))))
    dtg, sgg = _group_lanes(dt_sp), _group_lanes(dt_sg)
    alog_g, dskip_g = _group_lanes(a_log), _group_lanes(d_skip)
    y_ssm, y_all, sprev = _ssd_fwd(act["xs"], act["bm"], act["cm"], dtg, nat["z"], alog_g, dskip_g, ssm_norm_w)

    w_bra, w_brb = w_br16[:ATT_OUT], w_br16[ATT_OUT:]
    y_a, = _proj(oa, [w_bra], F32, "proj_ya")
    y_b, = _proj(y_ssm, [w_brb], F32, "proj_yb")
    merged = _merge_fwd(y_a, y_b, nat["gm"], bgate8)
    mix, = _proj(merged, [w_out16], F32, "proj_mix")
    pw, = _proj(p16, [w_ple16], F32, "proj_ple")

    dx, dpre16, dpw16, dgp16, ln_sums = _ln_loss(x, mix, nat["gp"], pw, target, bgate8, ln_g, ln_b)
    loss_sum = (0.5 / D_MODEL) * jnp.sum(ln_sums[3])
    dmerged = _dx([dpre16], [w_out16], [], "dx_merged")
    dya16, dyb16, dgm16, mg_sums = _merge_bwd(dmerged, y_a, y_b, nat["gm"], bgate8)
    doa = _dx([dya16], [w_bra], [], "dx_oa")
    dys = _dx([dyb16], [w_brb], [], "dx_yssm")
    g_w_out, = _dw(merged, [dpre16], BF16, "dw_out")
    g_w_br = jnp.concatenate([_dw(oa, [dya16], BF16, "dw_bra")[0], _dw(y_ssm, [dyb16], BF16, "dw_brb")[0]], axis=0)
    g_w_ple, = _dw(p16, [dpw16], BF16, "dw_ple")

    do_att, dgatt16 = _combine_bwd(doa, nat["gatt"], o_att)
    dseg = {"gatt": dgatt16, "gm": dgm16, "gp": dgp16}
    dbias = []
    for g in range(3):
        dqkv, db = _attn_bwd(qkv[g], bias[g * GROUP_HEADS:(g + 1) * GROUP_HEADS], do_att, o_att, lse, dils[g],
                             "attn_bwd%d" % g)
        dseg["qkv%d" % g] = dqkv.reshape(nb, seq, -1)
        dbias.append(db)
    g_rel = _bias_grad(jnp.concatenate(dbias, axis=0), bmaps)[:, 0, :NUM_BUCKETS].T

    dxs, dbm, dcm, ddtg, dz, ssd_small, g_normw = _ssd_bwd(
        act["xs"], act["bm"], act["cm"], dtg, sgg, nat["z"], y_all, dys, sprev, alog_g, dskip_g, ssm_norm_w)
    dseg["z"] = dz
    dseg["dt"] = jnp.pad(_ungroup_lanes(ddtg), ((0, 0), (0, 0), (0, LANE - SSM_HEADS)))
    conv_sums = {}
    for s, dact in (("xs", dxs), ("bm", dbm), ("cm", dcm)):
        dpre, conv_sums[s] = _conv_bwd_pre(dact, nat[s], cw[s][0], cw[s][1], "conv_bwd_" + s)
        dseg[s] = _conv_bwd_x(dpre, cw[s][0], "conv_bwd_x_" + s)
    csum = jnp.concatenate([conv_sums["xs"], conv_sums["bm"], conv_sums["cm"]], axis=1)

    dx_perm = [_unpermute(_dx([dseg["qkv%d" % g]], [wseg["qkv%d" % g]], [], "dx_qkv%d" % g, True), dils[g])
               for g in (1, 2)]
    dwseg = {"qkv%d" % g: _dw(x16p[g], [dseg["qkv%d" % g]], BF16, "dw_qkv%d" % g, True)[0] for g in range(3)}
    dx = _dx([dseg["qkv0"]], [wseg["qkv0"]], [dx], "dx_qkv0", True)
    for gi, (group, tm) in enumerate(NAT_GROUPS):
        last = gi == len(NAT_GROUPS) - 1
        dx = _dx([dseg[s] for s in group], [wseg[s] for s in group], [dx] + (dx_perm if last else []),
                 "dx_nat%d" % gi, True, tm)
    for gi, group in enumerate(DW_GROUPS):
        dwseg.update(zip(group, _dw(x16, [dseg[s] for s in group], BF16, "dw_nat%d" % gi, True)))

    small = dict(
        b_gate=jnp.stack([mg_sums[0], mg_sums[1], ln_sums[2]]),
        conv_w=csum[0:4], conv_b=csum[4:5],
        dt_bias=_ungroup_lanes(ssd_small[:, 2:3, :]), a_log=_ungroup_lanes(ssd_small[:, 0:1, :]),
        d_skip=_ungroup_lanes(ssd_small[:, 1:2, :]), ssm_norm_w=g_normw,
        ln_g=ln_sums[0:1], ln_b=ln_sums[1:2], rel_bias=g_rel)
    return loss_sum, dx, dwseg, g_w_br, g_w_out, g_w_ple, small


SMALL_ORDER = ("b_gate", "conv_w", "conv_b", "dt_bias", "a_log", "d_skip", "ssm_norm_w", "ln_g", "ln_b", "rel_bias")
SMALL_FULL_SHAPES = dict(b_gate=(3, 1024), conv_w=(4, 3072), conv_b=(1, 3072), dt_bias=(1, 32), a_log=(1, 32),
                         d_skip=(1, 32), ssm_norm_w=(1, 2048), ln_g=(1, 1024), ln_b=(1, 1024), rel_bias=(32, 36))


def kernel(x, p, w_in, b_gate, conv_w, conv_b, dt_bias, a_log, d_skip, ssm_norm_w, w_branch, w_out, w_ple, ln_g, ln_b, rel_bias, loss_target, m_w_in, m_b_gate, m_conv_w, m_conv_b, m_dt_bias, m_a_log, m_d_skip, m_ssm_norm_w, m_w_branch, m_w_out, m_w_ple, m_ln_g, m_ln_b, m_rel_bias, v_w_in, v_b_gate, v_conv_w, v_conv_b, v_dt_bias, v_a_log, v_d_skip, v_ssm_norm_w, v_w_branch, v_w_out, v_w_ple, v_ln_g, v_ln_b, v_rel_bias):
    cx, cy, cc = _place()
    chip = 2 * cx + cy
    dev = 4 * cx + 2 * cy + cc

    w_in_t = jnp.transpose(w_in[0])
    win16 = _shard_to_window(w_in_t.astype(BF16), chip)
    g_win, g_br, g_out, g_ple = _allgather_pieces(
        [win16, w_branch[0].astype(BF16), w_out[0].astype(BF16), w_ple[0].astype(BF16)])
    wseg = _assemble(g_win)
    w_br16 = g_br.reshape(4 * 704, D_MODEL)
    w_out16 = g_out.reshape(D_MODEL, D_MODEL)
    w_ple16 = jnp.transpose(g_ple, (1, 0, 2)).reshape(PLE_DIM, D_MODEL)
    shards = _allgather8(_pack_rows([b_gate[0], conv_w[0]]), "allgather_small_params")
    per_chip = [_unpack_rows(shards[2 * k], [(3, 256), (4, 768)]) for k in range(4)]
    b_gate_full = jnp.concatenate([pc[0] for pc in per_chip], axis=1)
    conv_w_full = jnp.concatenate([pc[1] for pc in per_chip], axis=1)

    loss_sum, grad_x, dwseg, g_br, g_out, g_ple, small = _local_grads(
        x, p[0], loss_target, wseg, w_br16, w_out16, w_ple16, b_gate_full, conv_w_full, conv_b, dt_bias, a_log,
        d_skip, ssm_norm_w, ln_g, ln_b, rel_bias)
    loss = lax.psum(loss_sum, ("x", "y", "c"))

    big = _reduce_scatter([
        _pack(dwseg), g_br.reshape(4, 704, D_MODEL), g_out.reshape(4, 256, D_MODEL),
        jnp.transpose(g_ple.reshape(PLE_DIM, 4, 256), (1, 0, 2))])
    g_w_in = _window_to_shard(big[0], chip)
    g_w_branch, g_w_out, g_w_ple = big[1], big[2], big[3]
    parts = _allgather8(_pack_rows([small[n] for n in SMALL_ORDER]), "allgather_small_grads")
    small_sum = _sum_rows([parts[i] for i in range(8)], F32, "small_grad_sum")
    sg = dict(zip(SMALL_ORDER, _unpack_rows(small_sum, [SMALL_FULL_SHAPES[n] for n in SMALL_ORDER])))
    sg["b_gate"] = lax.dynamic_slice_in_dim(sg["b_gate"], chip * 256, 256, axis=1)
    sg["conv_w"] = lax.dynamic_slice_in_dim(sg["conv_w"], chip * 768, 768, axis=1)
    del dev

    upd = {}
    upd["w_in"] = [jnp.transpose(t) for t in _adamw(w_in_t, g_w_in, jnp.transpose(m_w_in[0]),
                                                      jnp.transpose(v_w_in[0]), "adamw_w_in")]
    upd["w_branch"] = _adamw(w_branch[0], g_w_branch, m_w_branch[0], v_w_branch[0], "adamw_w_branch")
    upd["w_out"] = _adamw(w_out[0], g_w_out, m_w_out[0], v_w_out[0], "adamw_w_out")
    upd["w_ple"] = _adamw(w_ple[0], g_w_ple, m_w_ple[0], v_w_ple[0], "adamw_w_ple")
    small_w = dict(b_gate=b_gate, conv_w=conv_w, conv_b=conv_b, dt_bias=dt_bias, a_log=a_log, d_skip=d_skip,
                   ssm_norm_w=ssm_norm_w, ln_g=ln_g, ln_b=ln_b, rel_bias=rel_bias)
    small_m = dict(b_gate=m_b_gate, conv_w=m_conv_w, conv_b=m_conv_b, dt_bias=m_dt_bias, a_log=m_a_log,
                   d_skip=m_d_skip, ssm_norm_w=m_ssm_norm_w, ln_g=m_ln_g, ln_b=m_ln_b, rel_bias=m_rel_bias)
    small_v = dict(b_gate=v_b_gate, conv_w=v_conv_w, conv_b=v_conv_b, dt_bias=v_dt_bias, a_log=v_a_log,
                   d_skip=v_d_skip, ssm_norm_w=v_ssm_norm_w, ln_g=v_ln_g, ln_b=v_ln_b, rel_bias=v_rel_bias)
    shapes = [small_w[n].shape for n in SMALL_ORDER]
    s_delta, s_m, s_v = _adamw(_pack_rows([small_w[n] for n in SMALL_ORDER]), _pack_rows([sg[n] for n in SMALL_ORDER]),
                               _pack_rows([small_m[n] for n in SMALL_ORDER]), _pack_rows([small_v[n] for n in SMALL_ORDER]),
                               "adamw_small")
    for i, n in enumerate(SMALL_ORDER):
        upd[n] = tuple(_unpack_rows(t, shapes)[i] for t in (s_delta, s_m, s_v))
        sg[n] = sg[n].reshape(small_w[n].shape)

    order = ("w_in", "b_gate", "conv_w", "conv_b", "dt_bias", "a_log", "d_skip", "ssm_norm_w", "w_branch", "w_out",
             "w_ple", "ln_g", "ln_b", "rel_bias")
    grads = dict(sg, w_in=jnp.transpose(g_w_in)[None],w_branch=g_w_branch[None], w_out=g_w_out[None], w_ple=g_w_ple[None])
    lead = lambda n, t: t[None] if n in ("w_in", "w_branch", "w_out", "w_ple") else t
    return (loss, grad_x, *[grads[n] for n in order], *[lead(n, upd[n][0]) for n in order],
            *[lead(n, upd[n][1]) for n in order], *[lead(n, upd[n][2]) for n in order])
```

```python
import functools
import math

import numpy as np
import jax
import jax.numpy as jnp
from jax import lax
from jax.experimental import pallas as pl
from jax.experimental.pallas import tpu as pltpu

F32, BF16 = jnp.float32, jnp.bfloat16
HIGHEST = lax.Precision.HIGHEST

D_MODEL = 1024
HEAD_DIM = 64
GROUP_HEADS = 12
ATT_OUT = GROUP_HEADS * HEAD_DIM
PATTERNS = ((128, 1), (512, 4), (2048, 16))
BAND = 128
NUM_BUCKETS = 32
MAX_DISTANCE = 2048
D_INNER = 2048
SSM_HEADS = 32
SSM_GROUPS = 4
GROUP_SSM_HEADS = SSM_HEADS // SSM_GROUPS
D_STATE = 128
CHUNK = 128
PLE_DIM = 256
ALPHA = 2.0 ** 0.25
LN_EPS = 1e-5
RMS_EPS = 1e-5
ADAM_LR, ADAM_B1, ADAM_B2, ADAM_EPS, ADAM_WD, ADAM_STEP = 0.001, 0.9, 0.999, 1e-08, 0.01, 10
NEG = -1e30

QKV_W = 3 * ATT_OUT
IN_COLS = 15904
SHARD_COLS = IN_COLS // 4
DT_COL = 12800
ROW_TILE = 16
WIN_ROWS = 4000


def _win_offset(k):
    return (k * SHARD_COLS) % ROW_TILE


def _win_start(k):
    return k * SHARD_COLS - _win_offset(k)

VMEM_LIMIT_BYTES = 56 * 1024 * 1024
LANE = 128
MESH = pl.DeviceIdType.MESH
NT = (((1,), (1,)), ((), ()))
TN = (((0,), (0,)), ((), ()))


def _pcall(body, **kw):
    return pl.pallas_call(body, **kw)


def _params(*sem):
    return pltpu.CompilerParams(dimension_semantics=sem, vmem_limit_bytes=VMEM_LIMIT_BYTES)


def _sigmoid(v):
    return jax.nn.sigmoid(v)


MM_TM = 512


def _permute(t, d):
    nb, seq, ch = t.shape
    return t if d == 1 else t.reshape(nb, seq // d, d, ch).transpose(0, 2, 1, 3).reshape(nb, seq, ch)


def _unpermute(t, d):
    nb, seq, ch = t.shape
    return t if d == 1 else t.reshape(nb, d, seq // d, ch).transpose(0, 2, 1, 3).reshape(nb, seq, ch)


def _tok_spec(tm, width):
    return pl.BlockSpec((None, tm, width), lambda b, i: (b, i, 0))


def _whole(arr, single_buffer=False):
    mode = dict(pipeline_mode=pl.Buffered(1)) if single_buffer else {}
    return pl.BlockSpec(arr.shape, lambda b, i: (0,) * arr.ndim, **mode)


def _proj(a3, ws, out_dtype, name, w_rows_are_outputs=False, tm=MM_TM):
    nb, seq, kdim = a3.shape
    nw = len(ws)
    widths = [w.shape[0] if w_rows_are_outputs else w.shape[1] for w in ws]

    def body(*refs):
        a = refs[0][...].astype(BF16)
        for w_ref, o_ref in zip(refs[1:1 + nw], refs[1 + nw:]):
            if w_rows_are_outputs:
                v = lax.dot_general(a, w_ref[...], NT, preferred_element_type=F32)
            else:
                v = jnp.dot(a, w_ref[...], preferred_element_type=F32)
            o_ref[...] = v.astype(out_dtype)

    return _pcall(
        body, name=name, grid=(nb, seq // tm),
        in_specs=[_tok_spec(tm, kdim)] + [_whole(w) for w in ws],
        out_specs=[_tok_spec(tm, n) for n in widths],
        out_shape=[jax.ShapeDtypeStruct((nb, seq, n), out_dtype) for n in widths],
        compiler_params=_params("parallel", "parallel"),
    )(a3, *ws)


def _dx(dhs, ws, accs, name, w_rows_are_outputs=False, tm=MM_TM, scatter=None):
    nb, seq, _ = dhs[0].shape
    nd, nacc = len(dhs), len(accs)
    kout = ws[0].shape[1] if w_rows_are_outputs else ws[0].shape[0]
    sums = scatter or []
    ns = len(sums)
    chunks = [_row_chunks(s.shape[1], s.shape[2] * s.dtype.itemsize) for s in sums]
    n_sem = 3 * sum(len(ch) for ch in chunks)
    grid = (nb, seq // tm)

    def body(*refs):
        n_in = 2 * nd + nacc
        sum_refs, o_ref, got_refs = refs[n_in:n_in + ns], refs[n_in + ns], refs[n_in + ns + 1:n_in + 2 * ns + 1]

        def copies():
            send_sems, recv_sems = refs[-2], refs[-1]
            x, y, c = _place()
            out = []
            for a in range(ns):
                for s, m in chunks[a]:
                    for j, (cx, cy) in enumerate(_other_chips(x, y)):
                        k = len(out)
                        out.append(_remote(sum_refs[a].at[2 * cx + cy, pl.ds(s, m)], got_refs[a].at[j, pl.ds(s, m)],
                                           send_sems.at[k], recv_sems.at[k], (cx, cy, c)))
            return out

        if ns:
            @pl.when((pl.program_id(0) == 0) & (pl.program_id(1) == 0))
            def _():
                for cp in copies():
                    cp.start()

        v = None
        for dh_ref, w_ref in zip(refs[:nd], refs[nd:2 * nd]):
            dh = dh_ref[...].astype(BF16)
            if w_rows_are_outputs:
                t = jnp.dot(dh, w_ref[...], preferred_element_type=F32)
            else:
                t = lax.dot_general(dh, w_ref[...], NT, preferred_element_type=F32)
            v = t if v is None else v + t
        for a_ref in refs[2 * nd:n_in]:
            v = v + a_ref[...]
        o_ref[...] = v

        if ns:
            @pl.when((pl.program_id(0) == grid[0] - 1) & (pl.program_id(1) == grid[1] - 1))
            def _():
                for cp in copies():
                    cp.wait()

    out = _pcall(
        body, name=name, grid=grid,
        in_specs=[_tok_spec(tm, dh.shape[-1]) for dh in dhs] + [_whole(w, bool(ns)) for w in ws]
        + [_tok_spec(tm, kout)] * nacc + [ANY] * ns,
        out_specs=[_tok_spec(tm, kout)] + [ANY] * ns,
        out_shape=[jax.ShapeDtypeStruct((nb, seq, kout), F32)]
        + [jax.ShapeDtypeStruct((3,) + s.shape[1:], s.dtype) for s in sums],
        input_output_aliases={2 * nd: 0} if nacc else {},
        scratch_shapes=[pltpu.SemaphoreType.DMA((n_sem,)), pltpu.SemaphoreType.DMA((n_sem,))] if ns else [],
        compiler_params=pltpu.CompilerParams(
            dimension_semantics=("arbitrary", "arbitrary") if ns else ("parallel", "parallel"),
            vmem_limit_bytes=VMEM_LIMIT_BYTES, has_side_effects=bool(ns)),
    )(*dhs, *ws, *accs, *sums)
    return (out[0], list(out[1:])) if ns else out[0]


def _dw(a3, dhs, out_dtype, name, rows_are_outputs=False):
    nb, seq, kdim = a3.shape
    nd = len(dhs)
    grid = (nb, seq // MM_TM)
    shapes = [(dh.shape[-1], kdim) if rows_are_outputs else (kdim, dh.shape[-1]) for dh in dhs]

    def body(*refs):
        b, i = pl.program_id(0), pl.program_id(1)
        dh_refs, o_refs, acc_refs = refs[1:1 + nd], refs[1 + nd:1 + 2 * nd], refs[1 + 2 * nd:]

        @pl.when((b == 0) & (i == 0))
        def _():
            for acc_ref in acc_refs:
                acc_ref[...] = jnp.zeros_like(acc_ref)

        a = refs[0][...].astype(BF16)
        for dh_ref, acc_ref in zip(dh_refs, acc_refs):
            dh = dh_ref[...].astype(BF16)
            acc_ref[...] += lax.dot_general(*((dh, a) if rows_are_outputs else (a, dh)), TN,
                                            preferred_element_type=F32)

        @pl.when((b == grid[0] - 1) & (i == grid[1] - 1))
        def _():
            for o_ref, acc_ref in zip(o_refs, acc_refs):
                o_ref[...] = acc_ref[...].astype(out_dtype)

    return _pcall(
        body, name=name, grid=grid,
        in_specs=[_tok_spec(MM_TM, kdim)] + [_tok_spec(MM_TM, dh.shape[-1]) for dh in dhs],
        out_specs=[pl.BlockSpec(s, lambda b, i: (0, 0)) for s in shapes],
        out_shape=[jax.ShapeDtypeStruct(s, out_dtype) for s in shapes],
        scratch_shapes=[pltpu.VMEM(s, F32) for s in shapes],
        compiler_params=_params("arbitrary", "arbitrary"),
    )(a3, *dhs)


def _qkv_rows(g):
    return [(part * QKV_W + g * ATT_OUT + hp * LANE, LANE) for hp in range(ATT_OUT // LANE) for part in range(3)]


def _segments():
    one = lambda name, start, rows: (name, [(start, rows)], max(rows, LANE))
    return [("qkv%d" % g, _qkv_rows(g), QKV_W) for g in range(3)] + [
        one("gatt", 3 * QKV_W, ATT_OUT), one("z", 3 * QKV_W + ATT_OUT, D_INNER), one("xs", 9728, D_INNER),
        one("bm", 9728 + D_INNER, 512), one("cm", 9728 + D_INNER + 512, 512), one("dt", DT_COL, SSM_HEADS),
        one("gm", DT_COL + SSM_HEADS, 2 * D_MODEL), one("gp", DT_COL + SSM_HEADS + 2 * D_MODEL, D_MODEL)]


LAYOUT_TC = 256
NAT_GROUPS = ((("gatt", "z", "dt", "bm", "cm"), 512), (("xs", "gm", "gp"), 256))
DW_GROUPS = (("gatt", "z", "dt", "bm", "cm"), ("xs", "gp"), ("gm",))


def _assemble(win):
    segs = _segments()

    def body(win_ref, *outs):
        def pieces(start, rows):
            t, end = start, start + rows
            while t < end:
                k = min(t // SHARD_COLS, 3)
                shard_end = (k + 1) * SHARD_COLS
                if k < 3 and shard_end % ROW_TILE and t == shard_end - shard_end % ROW_TILE:
                    lo = t - _win_start(k)
                    yield win_ref[k, lo:lo + ROW_TILE, :] + win_ref[k + 1, 0:ROW_TILE, :]
                    t += ROW_TILE
                    continue
                upto = min(end, shard_end - shard_end % ROW_TILE if k < 3 else end)
                yield win_ref[k, t - _win_start(k):upto - _win_start(k), :]
                t = upto

        for (_, ranges, total), o_ref in zip(segs, outs):
            off = 0
            for start, rows in ranges:
                for part in pieces(start, rows):
                    o_ref[off:off + part.shape[0], :] = part
                    off += part.shape[0]
            if off < total:
                o_ref[off:total, :] = jnp.zeros((total - off, o_ref.shape[1]), BF16)

    outs = _pcall(
        body, name="assemble_w_in", grid=(D_MODEL // LAYOUT_TC,),
        in_specs=[pl.BlockSpec((4, WIN_ROWS, LAYOUT_TC), lambda i: (0, 0, i))],
        out_specs=[pl.BlockSpec((total, LAYOUT_TC), lambda i: (0, i)) for _, _, total in segs],
        out_shape=[jax.ShapeDtypeStruct((total, D_MODEL), BF16) for _, _, total in segs],
        compiler_params=_params("parallel"),
    )(win)
    return {name: o for (name, _, _), o in zip(segs, outs)}


def _pack(dsegs):
    segs = _segments()

    def body(*refs):
        ins, o_ref = refs[:-1], refs[-1]
        tail = IN_COLS - _win_start(3)
        o_ref[3, tail:, :] = jnp.zeros((WIN_ROWS - tail, o_ref.shape[2]), BF16)
        for (_, ranges, _), s_ref in zip(segs, ins):
            off = 0
            for start, rows in ranges:
                for k in range(4):
                    lo = _win_start(k)
                    a, b = max(start, lo), min(start + rows, lo + WIN_ROWS)
                    if a < b:
                        o_ref[k, a - lo:b - lo, :] = s_ref[off + a - start:off + b - start, :]
                off += rows

    return _pcall(
        body, name="pack_dw_in", grid=(D_MODEL // LAYOUT_TC,),
        in_specs=[pl.BlockSpec((total, LAYOUT_TC), lambda i: (0, i)) for _, _, total in segs],
        out_specs=pl.BlockSpec((4, WIN_ROWS, LAYOUT_TC), lambda i: (0, 0, i)),
        out_shape=jax.ShapeDtypeStruct((4, WIN_ROWS, D_MODEL), BF16),
        compiler_params=_params("parallel"),
    )(*[dsegs[name] for name, _, _ in segs])


def _shard_to_window(shard_t, k):
    def at(off):
        return lambda w: jnp.pad(w.astype(BF16), ((off, WIN_ROWS - SHARD_COLS - off), (0, 0)))

    return lax.cond(k % 2 == 1, at(_win_offset(1)), at(_win_offset(0)), shard_t)


def _window_to_shard(win, k):
    return lax.dynamic_slice(win, ((k % 2) * _win_offset(1), 0), (SHARD_COLS, D_MODEL))


def _bucket_maps():
    qi = np.arange(BAND)[:, None]
    kj = np.arange(2 * BAND)[None, :]
    delta = qi + BAND - kj
    maps = []
    for window, dil in PATTERNS:
        valid = (delta >= 0) & (delta <= window // dil)
        dist = np.maximum(delta, 0) * dil
        max_exact = NUM_BUCKETS // 2
        d_f = np.maximum(dist, 1).astype(np.float32)
        large = max_exact + (np.log(d_f / np.float32(max_exact)) / np.float32(math.log(MAX_DISTANCE / max_exact))
                             * np.float32(NUM_BUCKETS - max_exact)).astype(np.int32)
        large = np.minimum(large, NUM_BUCKETS - 1)
        bucket = np.where(dist < max_exact, dist, large)
        maps.append(np.where(valid, bucket, -1).astype(np.int32))
    return np.stack(maps)


def _bias_tables(rel_bias, bmaps):
    def body(rb_ref, bm_ref, o_ref):
        h = pl.program_id(0)
        bm = bm_ref[...]
        acc = jnp.full(bm.shape, NEG, F32)
        for b in range(NUM_BUCKETS):
            acc = jnp.where(bm == b, rb_ref[b, h], acc)
        o_ref[...] = acc

    return _pcall(
        body, name="bias_tables", grid=(3 * GROUP_HEADS,),
        in_specs=[pl.BlockSpec(memory_space=pltpu.SMEM),
                  pl.BlockSpec((None, BAND, 2 * BAND), lambda h: (h // GROUP_HEADS, 0, 0))],
        out_specs=pl.BlockSpec((None, BAND, 2 * BAND), lambda h: (h, 0, 0)),
        out_shape=jax.ShapeDtypeStruct((3 * GROUP_HEADS, BAND, 2 * BAND), F32),
        compiler_params=_params("parallel"),
    )(rel_bias, bmaps)


def _bias_grad(dbias, bmaps):
    def body(db_ref, bm_ref, o_ref):
        bm = bm_ref[...]
        db = db_ref[...]
        lane = lax.broadcasted_iota(jnp.int32, (1, LANE), 1)
        vec = jnp.zeros((1, LANE), F32)
        for b in range(NUM_BUCKETS):
            s = jnp.sum(jnp.where(bm == b, db, 0.0), keepdims=True)
            vec = jnp.where(lane == b, s, vec)
        o_ref[...] = vec

    return _pcall(
        body, name="bias_grad", grid=(3 * GROUP_HEADS,),
        in_specs=[pl.BlockSpec((None, BAND, 2 * BAND), lambda h: (h, 0, 0)),
                  pl.BlockSpec((None, BAND, 2 * BAND), lambda h: (h // GROUP_HEADS, 0, 0))],
        out_specs=pl.BlockSpec((None, 1, LANE), lambda h: (h, 0, 0)),
        out_shape=jax.ShapeDtypeStruct((3 * GROUP_HEADS, 1, LANE), F32),
        compiler_params=_params("parallel"),
    )(dbias, bmaps)


def _rows(n):
    if isinstance(n, int):
        return pl.ds(n * BAND, BAND)
    return pl.ds(pl.multiple_of(n * BAND, BAND), BAND)


def _for_blocks(blocks, nblk, per, carry):
    carry = blocks([0], carry, False)
    start = 1 + (nblk - 1) % per
    for n in range(1, start):
        carry = blocks([n], carry, True)
    trips = (nblk - start) // per
    if trips > 0:
        carry = lax.fori_loop(
            0, trips, lambda t, c: blocks([start + t * per + u for u in range(per)], c, True), carry)
    return carry


def _pairs_per_step(d):
    return {1: 1, 4: 3, 16: 6}[d]


def _head_cols(i, h, part):
    base = 3 * LANE * i + part * LANE + h * HEAD_DIM
    return slice(base, base + HEAD_DIM)


def _attn_fwd(qkv4, bias, d, name):
    nb, _, sub, _ = qkv4.shape
    nblk = sub // BAND
    scale = HEAD_DIM ** -0.5
    npair = ATT_OUT // LANE
    hps = _pairs_per_step(d)

    def body(qkv_ref, bias_ref, o_ref, l_ref):
        def blocks(ns, carry, with_prev):
            chains = [(bi, i, h) for bi in range(len(ns)) for i in range(hps) for h in range(2)]
            scores = []
            for bi, i, h in chains:
                n = ns[bi]
                q = qkv_ref[_rows(n), _head_cols(i, h, 0)]
                s_c = lax.dot_general(q, qkv_ref[_rows(n), _head_cols(i, h, 1)], NT,
                                      preferred_element_type=F32) * scale + bias_ref[2 * i + h, :, BAND:]
                s_p = None
                if with_prev:
                    s_p = lax.dot_general(q, qkv_ref[_rows(n - 1), _head_cols(i, h, 1)], NT,
                                          preferred_element_type=F32) * scale + bias_ref[2 * i + h, :, :BAND]
                scores.append((s_c, s_p))
            probs = []
            for s_c, s_p in scores:
                m = jnp.max(s_c, -1, keepdims=True)
                if with_prev:
                    m = jnp.maximum(m, jnp.max(s_p, -1, keepdims=True))
                e_c = jnp.exp(s_c - m)
                den = jnp.sum(e_c, -1, keepdims=True)
                e_p = None
                if with_prev:
                    e_p = jnp.exp(s_p - m)
                    den = den + jnp.sum(e_p, -1, keepdims=True)
                    e_p = e_p.astype(BF16)
                probs.append((e_c.astype(BF16), e_p, den, m))
            outs = {}
            for (bi, i, h), (e_c, e_p, den, m) in zip(chains, probs):
                n = ns[bi]
                acc = jnp.dot(e_c, qkv_ref[_rows(n), _head_cols(i, h, 2)], preferred_element_type=F32)
                if with_prev:
                    acc = acc + jnp.dot(e_p, qkv_ref[_rows(n - 1), _head_cols(i, h, 2)], preferred_element_type=F32)
                outs[(bi, i, h)] = (acc / den, jnp.broadcast_to(m + jnp.log(den), (BAND, HEAD_DIM)))
            for bi, n in enumerate(ns):
                for i in range(hps):
                    o_ref[_rows(n), i * LANE:(i + 1) * LANE] = jnp.concatenate(
                        [outs[(bi, i, 0)][0], outs[(bi, i, 1)][0]], axis=1)
                    l_ref[_rows(n), i * LANE:(i + 1) * LANE] = jnp.concatenate(
                        [outs[(bi, i, 0)][1], outs[(bi, i, 1)][1]], axis=1)
            return carry

        _for_blocks(blocks, nblk, 2 if hps == 1 else 1, 0)

    ospec = pl.BlockSpec((None, sub, hps * LANE), lambda hp, b, r: (b, 0, r * (npair // hps) + hp))
    o, l = _pcall(
        body, name=name, grid=(npair // hps, nb, d),
        in_specs=[pl.BlockSpec((None, None, sub, 3 * LANE * hps), lambda hp, b, r: (b, r, 0, hp)),
                  pl.BlockSpec((2 * hps, BAND, 2 * BAND), lambda hp, b, r: (hp, 0, 0))],
        out_specs=[ospec, ospec],
        out_shape=[jax.ShapeDtypeStruct((nb, sub, d * ATT_OUT), F32)] * 2,
        compiler_params=_params("parallel", "parallel", "parallel"),
    )(qkv4, bias)
    return o.reshape(nb, sub * d, ATT_OUT), l.reshape(nb, sub * d, ATT_OUT)


def _attn_bwd(qkv4, bias, do_att, o_att, lse, d, name):
    nb, _, sub, _ = qkv4.shape
    nblk = sub // BAND
    scale = HEAD_DIM ** -0.5
    npair = ATT_OUT // LANE
    hps = _pairs_per_step(d)

    def body(qkv_ref, bias_ref, do_ref, o_ref, l_ref, dqkv_ref, db_ref):
        b, r = pl.program_id(1), pl.program_id(2)

        @pl.when((b == 0) & (r == 0))
        def _():
            db_ref[...] = jnp.zeros_like(db_ref)

        def blocks(ns, carry, with_prev):
            sides = (0, 1) if with_prev else (0,)
            chains = [(bi, i, h, sd) for bi in range(len(ns)) for i in range(hps) for h in range(2) for sd in sides]
            key_rows = lambda bi, sd: _rows(ns[bi] - sd)
            qs = {}
            for bi in range(len(ns)):
                for i in range(hps):
                    for h in range(2):
                        hl = slice(i * LANE + h * HEAD_DIM, i * LANE + (h + 1) * HEAD_DIM)
                        do = do_ref[_rows(ns[bi]), hl]
                        qs[(bi, i, h)] = (
                            qkv_ref[_rows(ns[bi]), _head_cols(i, h, 0)], do.astype(BF16),
                            jnp.sum(do * o_ref[_rows(ns[bi]), hl], -1, keepdims=True),
                            l_ref[_rows(ns[bi]), i * LANE + h * HEAD_DIM:i * LANE + h * HEAD_DIM + 1])
            raw = []
            for bi, i, h, sd in chains:
                q, do16, _, _ = qs[(bi, i, h)]
                k = qkv_ref[key_rows(bi, sd), _head_cols(i, h, 1)]
                v = qkv_ref[key_rows(bi, sd), _head_cols(i, h, 2)]
                bias_blk = bias_ref[2 * i + h, :, :BAND] if sd else bias_ref[2 * i + h, :, BAND:]
                s = lax.dot_general(q, k, NT, preferred_element_type=F32) * scale + bias_blk
                dp = lax.dot_general(do16, v, NT, preferred_element_type=F32)
                raw.append((s, dp))
            soft = []
            for (bi, i, h, sd), (s, dp) in zip(chains, raw):
                _, _, ebar, lcol = qs[(bi, i, h)]
                p = jnp.exp(s - lcol)
                ds = p * (dp - ebar)
                if sd:
                    db_ref[2 * i + h, :, :BAND] += ds
                else:
                    db_ref[2 * i + h, :, BAND:] += ds
                soft.append((p.astype(BF16), ds.astype(BF16)))
            grads = {}
            for (bi, i, h, sd), (p16, ds16) in zip(chains, soft):
                q, do16, _, _ = qs[(bi, i, h)]
                k = qkv_ref[key_rows(bi, sd), _head_cols(i, h, 1)]
                grads[(bi, i, h, sd)] = (
                    jnp.dot(ds16, k, preferred_element_type=F32),
                    lax.dot_general(ds16, q, TN, preferred_element_type=F32) * scale,
                    lax.dot_general(p16, do16, TN, preferred_element_type=F32))
            both = lambda bi, i, sd, which: jnp.concatenate(
                [grads[(bi, i, 0, sd)][which], grads[(bi, i, 1, sd)][which]], axis=1)
            carry = list(carry) if carry is not None else None
            for bi, n in enumerate(ns):
                for i in range(hps):
                    base = 3 * LANE * i
                    dq = both(bi, i, 0, 0)
                    if with_prev:
                        dq = dq + both(bi, i, 1, 0)
                        dqkv_ref[_rows(n - 1), base + LANE:base + 2 * LANE] = (
                            carry[2 * i] + both(bi, i, 1, 1)).astype(BF16)
                        dqkv_ref[_rows(n - 1), base + 2 * LANE:base + 3 * LANE] = (
                            carry[2 * i + 1] + both(bi, i, 1, 2)).astype(BF16)
                    dqkv_ref[_rows(n), base:base + LANE] = (dq * scale).astype(BF16)
                carry = [t for i in range(hps) for t in (both(bi, i, 0, 1), both(bi, i, 0, 2))]
            return tuple(carry)

        carry = _for_blocks(blocks, nblk, 2 if hps == 1 else 1, None)
        for i in range(hps):
            base = 3 * LANE * i
            dqkv_ref[_rows(nblk - 1), base + LANE:base + 2 * LANE] = carry[2 * i].astype(BF16)
            dqkv_ref[_rows(nblk - 1), base + 2 * LANE:base + 3 * LANE] = carry[2 * i + 1].astype(BF16)

    nspec = pl.BlockSpec((None, sub, hps * LANE), lambda hp, b, r: (b, 0, r * (npair // hps) + hp))
    qspec = pl.BlockSpec((None, None, sub, 3 * LANE * hps), lambda hp, b, r: (b, r, 0, hp))
    bspec = pl.BlockSpec((2 * hps, BAND, 2 * BAND), lambda hp, b, r: (hp, 0, 0))
    view = lambda t: t.reshape(nb, sub, d * ATT_OUT)
    return _pcall(
        body, name=name, grid=(npair // hps, nb, d),
        in_specs=[qspec, bspec, nspec, nspec, nspec], out_specs=[qspec, bspec],
        out_shape=[jax.ShapeDtypeStruct(qkv4.shape, BF16),
                   jax.ShapeDtypeStruct((GROUP_HEADS, BAND, 2 * BAND), F32)],
        compiler_params=_params("parallel", "arbitrary", "arbitrary"),
    )(qkv4, bias, view(do_att), view(o_att), view(lse))


def _combine_fwd(os, ls, gatt):
    nb, seq, _ = gatt.shape
    tm = 512

    def body(o0, o1, o2, l0, l1, l2, g_ref, oa_ref, oatt_ref, lse_ref):
        m = jnp.maximum(jnp.maximum(l0[...], l1[...]), l2[...])
        tot = m + jnp.log(jnp.exp(l0[...] - m) + jnp.exp(l1[...] - m) + jnp.exp(l2[...] - m))
        o = (jnp.exp(l0[...] - tot) * o0[...] + jnp.exp(l1[...] - tot) * o1[...]
             + jnp.exp(l2[...] - tot) * o2[...])
        g = g_ref[...]
        oa_ref[...] = (o * (g * _sigmoid(g))).astype(BF16)
        oatt_ref[...] = o
        lse_ref[...] = tot

    spec = pl.BlockSpec((None, tm, ATT_OUT), lambda b, i: (b, i, 0))
    return _pcall(
        body, name="attn_combine", grid=(nb, seq // tm), in_specs=[spec] * 7, out_specs=[spec] * 3,
        out_shape=[jax.ShapeDtypeStruct((nb, seq, ATT_OUT), BF16), jax.ShapeDtypeStruct((nb, seq, ATT_OUT), F32),
                   jax.ShapeDtypeStruct((nb, seq, ATT_OUT), F32)],
        compiler_params=_params("parallel", "parallel"),
    )(*os, *ls, gatt)


def _combine_bwd(doa, gatt, o_att):
    nb, seq, _ = gatt.shape
    tm = 512

    def body(doa_ref, g_ref, o_ref, do_ref, dg_ref):
        g = g_ref[...]
        sg = _sigmoid(g)
        do_ref[...] = doa_ref[...] * (g * sg)
        dg_ref[...] = (doa_ref[...] * o_ref[...] * (sg * (1.0 + g * (1.0 - sg)))).astype(BF16)

    spec = pl.BlockSpec((None, tm, ATT_OUT), lambda b, i: (b, i, 0))
    return _pcall(
        body, name="attn_combine_bwd", grid=(nb, seq // tm), in_specs=[spec] * 3, out_specs=[spec] * 2,
        out_shape=[jax.ShapeDtypeStruct((nb, seq, ATT_OUT), F32), jax.ShapeDtypeStruct((nb, seq, ATT_OUT), BF16)],
        compiler_params=_params("parallel", "parallel"),
    )(doa, gatt, o_att)


CONV_TM = 512
CONV_TC = 512


def _shift_down(cur, halo, k):
    rolled = pltpu.roll(cur, k, 0)
    hro = pltpu.roll(halo, k, 0)
    row = lax.broadcasted_iota(jnp.int32, hro.shape, 0)
    return jnp.concatenate([jnp.where(row < k, hro, rolled[:8]), rolled[8:]], axis=0)


def _shift_up(cur, halo, k):
    n = cur.shape[0]
    rolled = pltpu.roll(cur, n - k, 0)
    hro = pltpu.roll(halo, 8 - k, 0)
    row = lax.broadcasted_iota(jnp.int32, hro.shape, 0)
    return jnp.concatenate([rolled[:n - 8], jnp.where(row >= 8 - k, hro, rolled[n - 8:])], axis=0)


def _conv_pre(cur, halo, w_ref, b_ref):
    acc = cur * w_ref[3:4, :] + b_ref[...]
    for k in range(1, 4):
        acc = acc + _shift_down(cur, halo, k) * w_ref[3 - k:4 - k, :]
    return acc


def _conv_specs(seq):
    nblk = seq // CONV_TM
    cur = pl.BlockSpec((None, CONV_TM, CONV_TC), lambda cb, b, i: (b, i, cb))
    prev = pl.BlockSpec((None, 8, CONV_TC), lambda cb, b, i: (b, jnp.maximum(i * (CONV_TM // 8) - 1, 0), cb))
    nxt = pl.BlockSpec((None, 8, CONV_TC),
                       lambda cb, b, i: (b, jnp.minimum((i + 1) * (CONV_TM // 8), seq // 8 - 1), cb))
    wspec = pl.BlockSpec((4, CONV_TC), lambda cb, b, i: (0, cb))
    bspec = pl.BlockSpec((1, CONV_TC), lambda cb, b, i: (0, cb))
    return nblk, cur, prev, nxt, wspec, bspec


def _conv_fwd(xin, w4, bias, name):
    nb, seq, ch = xin.shape
    _, cur, prev, _, wspec, bspec = _conv_specs(seq)

    def body(x_ref, h_ref, w_ref, b_ref, o_ref):
        halo = jnp.where(pl.program_id(2) > 0, h_ref[...], 0.0)
        pre = _conv_pre(x_ref[...], halo, w_ref, b_ref)
        o_ref[...] = pre * _sigmoid(pre)

    return _pcall(
        body, name=name, grid=(ch // CONV_TC, nb, seq // CONV_TM),
        in_specs=[cur, prev, wspec, bspec], out_specs=cur,
        out_shape=jax.ShapeDtypeStruct(xin.shape, F32),
        compiler_params=_params("parallel", "parallel", "parallel"),
    )(xin, xin, w4, bias)


def _conv_bwd_pre(dact, xin, w4, bias, name):
    nb, seq, ch = xin.shape
    _, cur, prev, _, wspec, bspec = _conv_specs(seq)

    def body(da_ref, x_ref, h_ref, w_ref, b_ref, dp_ref, s_ref):
        b, i = pl.program_id(1), pl.program_id(2)

        @pl.when((b == 0) & (i == 0))
        def _():
            s_ref[...] = jnp.zeros_like(s_ref)

        halo = jnp.where(i > 0, h_ref[...], 0.0)
        x = x_ref[...]
        pre = _conv_pre(x, halo, w_ref, b_ref)
        sg = _sigmoid(pre)
        dpre = da_ref[...] * (sg * (1.0 + pre * (1.0 - sg)))
        dp_ref[...] = dpre
        s_ref[3:4, :] += jnp.sum(dpre * x, 0, keepdims=True)
        for k in range(1, 4):
            s_ref[3 - k:4 - k, :] += jnp.sum(dpre * _shift_down(x, halo, k), 0, keepdims=True)
        s_ref[4:5, :] += jnp.sum(dpre, 0, keepdims=True)

    return _pcall(
        body, name=name, grid=(ch // CONV_TC, nb, seq // CONV_TM),
        in_specs=[cur, cur, prev, wspec, bspec],
        out_specs=[cur, pl.BlockSpec((8, CONV_TC), lambda cb, b, i: (0, cb))],
        out_shape=[jax.ShapeDtypeStruct(xin.shape, F32), jax.ShapeDtypeStruct((8, ch), F32)],
        compiler_params=_params("parallel", "arbitrary", "arbitrary"),
    )(dact, xin, xin, w4, bias)


def _conv_bwd_x(dpre, w4, name):
    nb, seq, ch = dpre.shape
    nblk, cur, _, nxt, wspec, _ = _conv_specs(seq)

    def body(d_ref, n_ref, w_ref, o_ref):
        halo = jnp.where(pl.program_id(2) < nblk - 1, n_ref[...], 0.0)
        cur_v = d_ref[...]
        acc = cur_v * w_ref[3:4, :]
        for j in range(1, 4):
            acc = acc + _shift_up(cur_v, halo, j) * w_ref[3 - j:4 - j, :]
        o_ref[...] = acc.astype(BF16)

    return _pcall(
        body, name=name, grid=(ch // CONV_TC, nb, seq // CONV_TM),
        in_specs=[cur, nxt, wspec], out_specs=cur,
        out_shape=jax.ShapeDtypeStruct(dpre.shape, BF16),
        compiler_params=_params("parallel", "parallel", "parallel"),
    )(dpre, dpre, w4)


def _softplus_sig(dt_raw, dt_bias_row):
    nb, seq, _ = dt_raw.shape
    tm = 512

    def body(r_ref, b_ref, sp_ref, sg_ref):
        v = r_ref[...] + b_ref[...]
        sp_ref[...] = jnp.maximum(v, 0.0) + jnp.log1p(jnp.exp(-jnp.abs(v)))
        sg_ref[...] = _sigmoid(v)

    spec = pl.BlockSpec((None, tm, LANE), lambda b, i: (b, i, 0))
    return _pcall(
        body, name="dt_softplus", grid=(nb, seq // tm),
        in_specs=[spec, pl.BlockSpec((1, LANE), lambda b, i: (0, 0))], out_specs=[spec, spec],
        out_shape=[jax.ShapeDtypeStruct(dt_raw.shape, F32)] * 2,
        compiler_params=_params("parallel", "parallel"),
    )(dt_raw, dt_bias_row)


def _group_lanes(t):
    pads = [(0, 0)] * (t.ndim - 1) + [(0, LANE - GROUP_SSM_HEADS)]
    return jnp.stack([jnp.pad(t[..., GROUP_SSM_HEADS * g:GROUP_SSM_HEADS * (g + 1)], pads) for g in range(SSM_GROUPS)])


def _ungroup_lanes(t):
    return jnp.concatenate([t[g][..., :GROUP_SSM_HEADS] for g in range(SSM_GROUPS)], axis=-1)


def _decays(dt, al_ref):
    row = lax.broadcasted_iota(jnp.int32, (CHUNK, CHUNK), 0)
    col = lax.broadcasted_iota(jnp.int32, (CHUNK, CHUNK), 1)
    tril = (row >= col).astype(F32)
    triu = (row <= col).astype(F32)
    arow = -jnp.exp(al_ref[...])
    a = dt * arow
    acs = jnp.dot(tril, a, precision=HIGHEST, preferred_element_type=F32)
    acs_t = lax.dot_general(a, triu, (((0,), (0,)), ((), ())), precision=HIGHEST, preferred_element_type=F32)
    return arow, acs, acs_t, row >= col, triu


def _ssd_specs(nb, seq):
    nc = seq // CHUNK
    hw = GROUP_SSM_HEADS * HEAD_DIM

    def mk(rev):
        cidx = (lambda c: nc - 1 - c) if rev else (lambda c: c)
        wide = pl.BlockSpec((None, CHUNK, hw), lambda g, b, c: (b, cidx(c), g))
        state = pl.BlockSpec((None, CHUNK, D_STATE), lambda g, b, c: (b, cidx(c), g))
        lanes = pl.BlockSpec((None, None, CHUNK, LANE), lambda g, b, c: (g, b, cidx(c), 0))
        prev = pl.BlockSpec((None, None, None, D_STATE, hw), lambda g, b, c: (b, cidx(c), g, 0, 0))
        return wide, state, lanes, prev

    grow = pl.BlockSpec((None, 1, LANE), lambda g, b, c: (g, 0, 0))
    nwspec = pl.BlockSpec((1, hw), lambda g, b, c: (0, g))
    return nc, hw, mk, grow, nwspec


def _head_expand():
    hw = GROUP_SSM_HEADS * HEAD_DIM
    r = lax.broadcasted_iota(jnp.int32, (LANE, hw), 0)
    c = lax.broadcasted_iota(jnp.int32, (LANE, hw), 1)
    return ((c // HEAD_DIM) == r).astype(BF16)


def _split3(v):
    hi = v.astype(BF16)
    rest = v - hi.astype(F32)
    mid = rest.astype(BF16)
    return hi, mid, (rest - mid.astype(F32)).astype(BF16)


def _to_channels(v, e):
    hi, mid, lo = _split3(v)
    dot = lambda t: jnp.dot(t, e, preferred_element_type=F32)
    return (dot(hi) + dot(mid)) + dot(lo)


def _to_heads(w, e):
    hi, mid, lo = _split3(w)
    dot = lambda t: lax.dot_general(t, e, (((1,), (1,)), ((), ())), preferred_element_type=F32)
    return (dot(hi) + dot(mid)) + dot(lo)


def _row8(v):
    return jnp.broadcast_to(v, (8, v.shape[1]))


def _ssd_chunk_setup(dt, al_ref, ds_ref):
    arow, acs, acs_t, causal, triu = _decays(dt, al_ref)
    e = _head_expand()
    dtx = _to_channels(dt, e)
    acsx = _to_channels(acs, e)
    lastx = acsx[CHUNK - 1:CHUNK, :]
    dskx = _to_channels(_row8(ds_ref[...]), e)[0:1, :]
    return arow, acs, acs_t, causal, triu, e, dtx, acsx, lastx, dskx


def _ssd_fwd(xs, bm, cm, dtg, z, alog_g, dskip_g, normw):
    nb, seq, _ = xs.shape
    nc, hw, mk, grow, nwspec = _ssd_specs(nb, seq)
    wide, state, lanes, prev = mk(False)
    tn = (((0,), (0,)), ((), ()))

    def body(xs_ref, b_ref, c_ref, dt_ref, z_ref, al_ref, ds_ref, nw_ref, ys_ref, y_ref, sp_ref, st_ref):
        @pl.when(pl.program_id(2) == 0)
        def _():
            st_ref[...] = jnp.zeros_like(st_ref)

        dt = dt_ref[...]
        _, acs, acs_t, causal, _, _, dtx, acsx, lastx, dskx = _ssd_chunk_setup(dt, al_ref, ds_ref)
        bmat = b_ref[...].astype(BF16)
        cmat = c_ref[...].astype(BF16)
        cb = lax.dot_general(cmat, bmat, (((1,), (1,)), ((), ())), preferred_element_type=F32)
        x = xs_ref[...]
        xdt = x * dtx
        xdt16 = xdt.astype(BF16)
        first_head = lax.broadcasted_iota(jnp.int32, (CHUNK, LANE), 1) < HEAD_DIM
        pairs = []
        for hp in range(GROUP_SSM_HEADS // 2):
            xp = xdt16[:, hp * LANE:(hp + 1) * LANE]
            two = []
            for j in (2 * hp, 2 * hp + 1):
                lmat = jnp.exp(jnp.where(causal, acs[:, j:j + 1] - acs_t[j:j + 1, :], -jnp.inf))
                two.append(jnp.dot((cb * lmat).astype(BF16), xp, preferred_element_type=F32))
            pairs.append(jnp.where(first_head, two[0], two[1]))
        yd = jnp.concatenate(pairs, axis=1)
        s_prev = st_ref[...]
        s16 = s_prev.astype(BF16)
        sp_ref[...] = s16
        yo = jnp.dot(cmat, s16, preferred_element_type=F32) * jnp.exp(acsx)
        sts = lax.dot_general(bmat, (xdt * jnp.exp(lastx - acsx)).astype(BF16), tn, preferred_element_type=F32)
        st_ref[...] = s_prev * jnp.exp(lastx) + sts
        y = yd + yo + dskx * x
        zz = z_ref[...]
        u = y * (zz * _sigmoid(zz))
        rn = lax.rsqrt(jnp.mean(u * u, -1, keepdims=True) + RMS_EPS)
        ys_ref[...] = (u * rn * nw_ref[...]).astype(BF16)
        y_ref[...] = y

    return _pcall(
        body, name="ssd_fwd", grid=(SSM_GROUPS, nb, nc),
        in_specs=[wide, state, state, lanes, wide, grow, grow, nwspec],
        out_specs=[wide, wide, prev],
        out_shape=[jax.ShapeDtypeStruct((nb, seq, D_INNER), BF16), jax.ShapeDtypeStruct((nb, seq, D_INNER), F32),
                   jax.ShapeDtypeStruct((nb, nc, SSM_GROUPS, D_STATE, hw), BF16)],
        scratch_shapes=[pltpu.VMEM((D_STATE, hw), F32)],
        compiler_params=_params("parallel", "parallel", "arbitrary"),
    )(xs, bm, cm, dtg, z, alog_g, dskip_g, normw)


def _ssd_bwd(xs, bm, cm, dtg, sgg, z, y, dys, sprev, alog_g, dskip_g, normw):
    nb, seq, _ = xs.shape
    nc, hw, mk, grow, nwspec = _ssd_specs(nb, seq)
    wide, state, lanes, prev = mk(True)
    nt = (((1,), (1,)), ((), ()))
    tn = (((0,), (0,)), ((), ()))

    def body(xs_ref, b_ref, c_ref, dt_ref, sg_ref, z_ref, y_ref, dys_ref, sp_ref, al_ref, ds_ref, nw_ref,
             dxs_ref, db_ref, dc_ref, ddt_ref, dz_ref, small_ref, dnw_ref, g_ref):
        b, c = pl.program_id(1), pl.program_id(2)

        @pl.when((b == 0) & (c == 0))
        def _():
            small_ref[...] = jnp.zeros_like(small_ref)
            dnw_ref[...] = jnp.zeros_like(dnw_ref)

        @pl.when(c == 0)
        def _():
            g_ref[...] = jnp.zeros_like(g_ref)

        yv, zz, dys_v, nw = y_ref[...], z_ref[...], dys_ref[...], nw_ref[...]
        sz = _sigmoid(zz)
        silu = zz * sz
        u = yv * silu
        rn = lax.rsqrt(jnp.mean(u * u, -1, keepdims=True) + RMS_EPS)
        gn = dys_v * nw
        du = rn * gn - u * (rn * rn * rn) * jnp.mean(u * gn, -1, keepdims=True)
        dnw_ref[...] += jnp.sum(dys_v * u * rn, 0, keepdims=True)
        dy = du * silu
        dz_ref[...] = du * yv * (sz * (1.0 + zz * (1.0 - sz)))

        dt = dt_ref[...]
        arow, acs, acs_t, causal, triu, e, dtx, acsx, lastx, dskx = _ssd_chunk_setup(dt, al_ref, ds_ref)
        dfsx = jnp.exp(acsx)
        dtex = jnp.exp(lastx - acsx)
        bmat = b_ref[...].astype(BF16)
        cmat = c_ref[...].astype(BF16)
        cb = lax.dot_general(cmat, bmat, nt, preferred_element_type=F32)
        x = xs_ref[...]
        xdt = x * dtx
        xdt16 = xdt.astype(BF16)
        xdte = xdt * dtex
        dy16 = dy.astype(BF16)
        dyd = dy * dfsx
        dyd16 = dyd.astype(BF16)
        s16 = sp_ref[...]
        g = g_ref[...]
        g16 = g.astype(BF16)
        cs = jnp.dot(cmat, s16, preferred_element_type=F32)
        dc_off = lax.dot_general(dyd16, s16, nt, preferred_element_type=F32)
        g_here = lax.dot_general(cmat, dyd16, tn, preferred_element_type=F32)
        bg = jnp.dot(bmat, g16, preferred_element_type=F32)
        db_st = lax.dot_general(xdte.astype(BF16), g16, nt, preferred_element_type=F32)
        ddte_w = bg * xdte
        dcd = _to_heads(_row8(jnp.sum(g * s16.astype(F32), 0, keepdims=True)), e)[0:1, :]
        lane = lax.broadcasted_iota(jnp.int32, (CHUNK, LANE), 1)
        first_head = lane < HEAD_DIM
        sub = lax.broadcasted_iota(jnp.int32, (CHUNK, LANE), 0)
        dacs = jnp.zeros((CHUNK, LANE), F32)
        colsums = jnp.zeros((CHUNK, LANE), F32)
        dcb = jnp.zeros((CHUNK, CHUNK), F32)
        pairs = []
        for hp in range(GROUP_SSM_HEADS // 2):
            xp = xdt16[:, hp * LANE:(hp + 1) * LANE]
            dyp = dy16[:, hp * LANE:(hp + 1) * LANE]
            two = []
            for idx, j in enumerate((2 * hp, 2 * hp + 1)):
                lmat = jnp.exp(jnp.where(causal, acs[:, j:j + 1] - acs_t[j:j + 1, :], -jnp.inf))
                mf = cb * lmat
                dy_h = jnp.where(first_head if idx == 0 else jnp.logical_not(first_head), dyp, jnp.zeros_like(dyp))
                dm = lax.dot_general(dy_h, xp, nt, preferred_element_type=F32)
                two.append(lax.dot_general(mf.astype(BF16), dyp, tn, preferred_element_type=F32))
                wmat = dm * mf
                dcb = dcb + dm * lmat
                dacs = jnp.where(lane == j, jnp.sum(wmat, -1, keepdims=True), dacs)
                colsums = jnp.where(sub == j, jnp.sum(wmat, 0, keepdims=True), colsums)
            pairs.append(jnp.where(first_head, two[0], two[1]))
        dxdt = bg * dtex + jnp.concatenate(pairs, axis=1)
        dacs = dacs - colsums.T + _to_heads(dyd * cs - ddte_w, e)
        cd_row = jnp.exp(acs[CHUNK - 1:CHUNK, :])
        tail = _to_heads(_row8(jnp.sum(ddte_w, 0, keepdims=True)), e)[0:1, :] + dcd * cd_row
        dacs = dacs + jnp.where(sub == CHUNK - 1, tail, 0.0)
        da = jnp.dot(triu, dacs, precision=HIGHEST, preferred_element_type=F32)
        ddt_raw = (da * arow + _to_heads(dxdt * x, e)) * sg_ref[...]
        ddt_ref[...] = ddt_raw
        small_ref[0:1, :] += jnp.sum(da * dt, 0, keepdims=True) * arow
        small_ref[1:2, :] += _to_heads(_row8(jnp.sum(dy * x, 0, keepdims=True)), e)[0:1, :]
        small_ref[2:3, :] += jnp.sum(ddt_raw, 0, keepdims=True)
        dcb16 = dcb.astype(BF16)
        dc_ref[...] = dc_off + jnp.dot(dcb16, bmat, preferred_element_type=F32)
        db_ref[...] = db_st + lax.dot_general(dcb16, cmat, tn, preferred_element_type=F32)
        dxs_ref[...] = dxdt * dtx + dskx * dy
        g_ref[...] = g * jnp.exp(lastx) + g_here

    return _pcall(
        body, name="ssd_bwd", grid=(SSM_GROUPS, nb, nc),
        in_specs=[wide, state, state, lanes, lanes, wide, wide, wide, prev, grow, grow, nwspec],
        out_specs=[wide, state, state, lanes, wide,
                   pl.BlockSpec((None, 8, LANE), lambda g, b, c: (g, 0, 0)), nwspec],
        out_shape=[jax.ShapeDtypeStruct((nb, seq, D_INNER), F32),
                   jax.ShapeDtypeStruct((nb, seq, SSM_GROUPS * D_STATE), F32),
                   jax.ShapeDtypeStruct((nb, seq, SSM_GROUPS * D_STATE), F32),
                   jax.ShapeDtypeStruct((SSM_GROUPS, nb, seq, LANE), F32),
                   jax.ShapeDtypeStruct((nb, seq, D_INNER), F32),
                   jax.ShapeDtypeStruct((SSM_GROUPS, 8, LANE), F32),
                   jax.ShapeDtypeStruct((1, D_INNER), F32)],
        scratch_shapes=[pltpu.VMEM((D_STATE, hw), F32)],
        compiler_params=_params("parallel", "arbitrary", "arbitrary"),
    )(xs, bm, cm, dtg, sgg, z, y, dys, sprev, alog_g, dskip_g, normw)


EW_TM = 256


def _merge_fwd(y_a, y_b, gm, bgate):
    nb, seq, _ = y_a.shape

    def body(a_ref, b_ref, ga_ref, gb_ref, bg_ref, o_ref):
        sa = _sigmoid(ga_ref[...] + bg_ref[0:1, :])
        sb = _sigmoid(gb_ref[...] + bg_ref[1:2, :])
        o_ref[...] = (sa * a_ref[...] + sb * b_ref[...]).astype(BF16)

    spec = pl.BlockSpec((None, EW_TM, D_MODEL), lambda b, i: (b, i, 0))
    spec1 = pl.BlockSpec((None, EW_TM, D_MODEL), lambda b, i: (b, i, 1))
    return _pcall(
        body, name="merge_fwd", grid=(nb, seq // EW_TM),
        in_specs=[spec, spec, spec, spec1, pl.BlockSpec((8, D_MODEL), lambda b, i: (0, 0))], out_specs=spec,
        out_shape=jax.ShapeDtypeStruct((nb, seq, D_MODEL), BF16),
        compiler_params=_params("parallel", "parallel"),
    )(y_a, y_b, gm, gm, bgate)


def _merge_bwd(dmerged, y_a, y_b, gm, bgate):
    nb, seq, _ = y_a.shape

    def body(dm_ref, a_ref, b_ref, ga_ref, gb_ref, bg_ref, dya_ref, dyb_ref, dg_ref, s_ref):
        @pl.when((pl.program_id(0) == 0) & (pl.program_id(1) == 0))
        def _():
            s_ref[...] = jnp.zeros_like(s_ref)

        dm = dm_ref[...]
        sa = _sigmoid(ga_ref[...] + bg_ref[0:1, :])
        sb = _sigmoid(gb_ref[...] + bg_ref[1:2, :])
        dya_ref[...] = (dm * sa).astype(BF16)
        dyb_ref[...] = (dm * sb).astype(BF16)
        dga = dm * a_ref[...] * (sa * (1.0 - sa))
        dgb = dm * b_ref[...] * (sb * (1.0 - sb))
        dg_ref[:, :D_MODEL] = dga.astype(BF16)
        dg_ref[:, D_MODEL:] = dgb.astype(BF16)
        s_ref[0:1, :] += jnp.sum(dga, 0, keepdims=True)
        s_ref[1:2, :] += jnp.sum(dgb, 0, keepdims=True)

    spec = pl.BlockSpec((None, EW_TM, D_MODEL), lambda b, i: (b, i, 0))
    spec1 = pl.BlockSpec((None, EW_TM, D_MODEL), lambda b, i: (b, i, 1))
    small = pl.BlockSpec((8, D_MODEL), lambda b, i: (0, 0))
    return _pcall(
        body, name="merge_bwd", grid=(nb, seq // EW_TM),
        in_specs=[spec, spec, spec, spec, spec1, small],
        out_specs=[spec, spec, pl.BlockSpec((None, EW_TM, 2 * D_MODEL), lambda b, i: (b, i, 0)), small],
        out_shape=[jax.ShapeDtypeStruct((nb, seq, D_MODEL), BF16), jax.ShapeDtypeStruct((nb, seq, D_MODEL), BF16),
                   jax.ShapeDtypeStruct((nb, seq, 2 * D_MODEL), BF16), jax.ShapeDtypeStruct((8, D_MODEL), F32)],
        compiler_params=_params("arbitrary", "arbitrary"),
    )(dmerged, y_a, y_b, gm, gm, bgate)


def _ln_loss(x, mix, gp, pw, target, bgate, ln_g, ln_b):
    nb, seq, _ = x.shape

    def body(x_ref, mix_ref, gp_ref, pw_ref, t_ref, bg_ref, g_ref, b_ref, dx_ref, dp_ref, dpw_ref, dgp_ref, s_ref):
        @pl.when((pl.program_id(0) == 0) & (pl.program_id(1) == 0))
        def _():
            s_ref[...] = jnp.zeros_like(s_ref)

        sp = _sigmoid(gp_ref[...] + bg_ref[2:3, :])
        pw = pw_ref[...]
        pre = ALPHA * x_ref[...] + mix_ref[...] + sp * pw
        mu = jnp.mean(pre, -1, keepdims=True)
        cen = pre - mu
        rstd = lax.rsqrt(jnp.mean(cen * cen, -1, keepdims=True) + LN_EPS)
        xhat = cen * rstd
        err = xhat * g_ref[...] + b_ref[...] - t_ref[...]
        dy = err * (1.0 / D_MODEL)
        dxh = dy * g_ref[...]
        dpre = rstd * (dxh - jnp.mean(dxh, -1, keepdims=True) - xhat * jnp.mean(dxh * xhat, -1, keepdims=True))
        dx_ref[...] = ALPHA * dpre
        dp_ref[...] = dpre.astype(BF16)
        dpw_ref[...] = (dpre * sp).astype(BF16)
        dgp = dpre * pw * (sp * (1.0 - sp))
        dgp_ref[...] = dgp.astype(BF16)
        s_ref[0:1, :] += jnp.sum(dy * xhat, 0, keepdims=True)
        s_ref[1:2, :] += jnp.sum(dy, 0, keepdims=True)
        s_ref[2:3, :] += jnp.sum(dgp, 0, keepdims=True)
        s_ref[3:4, :] += jnp.sum(err * err, 0, keepdims=True)

    spec = pl.BlockSpec((None, EW_TM, D_MODEL), lambda b, i: (b, i, 0))
    small = pl.BlockSpec((8, D_MODEL), lambda b, i: (0, 0))
    row = pl.BlockSpec((1, D_MODEL), lambda b, i: (0, 0))
    return _pcall(
        body, name="ln_loss", grid=(nb, seq // EW_TM),
        in_specs=[spec] * 5 + [small, row, row], out_specs=[spec] * 4 + [small],
        out_shape=[jax.ShapeDtypeStruct((nb, seq, D_MODEL), F32)] + [jax.ShapeDtypeStruct((nb, seq, D_MODEL), BF16)] * 3
        + [jax.ShapeDtypeStruct((8, D_MODEL), F32)],
        compiler_params=_params("arbitrary", "arbitrary"),
    )(x, mix, gp, pw, target, bgate, ln_g, ln_b)


def _adamw(w, g, m, v, name):
    rows, cols = w.shape
    tr = _row_tile(rows, cols, 8, 5 << 19)
    c1 = 1.0 - ADAM_B1 ** ADAM_STEP
    c2 = 1.0 - ADAM_B2 ** ADAM_STEP

    def body(w_ref, g_ref, m_ref, v_ref, d_ref, nm_ref, nv_ref):
        gv = g_ref[...]
        nm = ADAM_B1 * m_ref[...] + (1.0 - ADAM_B1) * gv
        nv = ADAM_B2 * v_ref[...] + (1.0 - ADAM_B2) * (gv * gv)
        d_ref[...] = -ADAM_LR * ((nm / c1) / (jnp.sqrt(nv / c2) + ADAM_EPS) + ADAM_WD * w_ref[...])
        nm_ref[...] = nm
        nv_ref[...] = nv

    spec = pl.BlockSpec((tr, cols), lambda i: (i, 0))
    return _pcall(
        body, name=name, grid=(rows // tr,), in_specs=[spec] * 4, out_specs=[spec] * 3,
        out_shape=[jax.ShapeDtypeStruct(w.shape, F32)] * 3, compiler_params=_params("parallel"),
    )(w, g, m, v)


def _sum_rows(parts, out_dtype, name):
    rows, cols = parts[0].shape
    tr = rows
    for cand in range(16, rows, 16):
        if rows % cand == 0 and cand * cols * 4 <= (1 << 20):
            tr = cand
    n = len(parts)

    def body(*refs):
        acc = refs[0][...].astype(F32)
        for r in refs[1:n]:
            acc = acc + r[...].astype(F32)
        refs[n][...] = acc.astype(out_dtype)

    spec = pl.BlockSpec((tr, cols), lambda i: (i, 0))
    return _pcall(
        body, name=name, grid=(rows // tr,), in_specs=[spec] * n, out_specs=spec,
        out_shape=jax.ShapeDtypeStruct((rows, cols), out_dtype), compiler_params=_params("parallel"),
    )(*parts)


def _place():
    return lax.axis_index("x"), lax.axis_index("y"), lax.axis_index("c")


def _other_chips(x, y):
    return [(1 - x, y), (x, 1 - y), (1 - x, 1 - y)]


def _remote(src, dst, send_sem, recv_sem, to):
    return pltpu.make_async_remote_copy(src_ref=src, dst_ref=dst, send_sem=send_sem, recv_sem=recv_sem,
                                        device_id=to, device_id_type=MESH)


ANY = pl.BlockSpec(memory_space=pl.ANY)
DMA_CHUNK_BYTES = 512 * 1024


def _row_chunks(rows, row_bytes):
    per = max(16, DMA_CHUNK_BYTES // row_bytes // 16 * 16)
    return [(s, min(per, rows - s)) for s in range(0, rows, per)]


def _row_tile(rows, cols, align, limit=1 << 21):
    best = None
    for cand in range(align, rows + 1, align):
        if rows % cand == 0 and cand * cols * 4 <= limit:
            best = cand
    return best or rows


def _allgather_pieces(pieces):
    n = len(pieces)
    halves = [_row_chunks(p.shape[0] // 2, p.shape[1] * p.dtype.itemsize) for p in pieces]
    wholes = [_row_chunks(p.shape[0], p.shape[1] * p.dtype.itemsize) for p in pieces]
    n_ici = 3 * sum(len(h) for h in halves)
    n_loc = sum(len(w) for w in wholes)

    def body(*refs):
        ins, outs = refs[:n], refs[n:2 * n]
        send_sems, recv_sems, local_sems = refs[2 * n:]
        x, y, c = _place()
        me = 2 * x + y
        sibling = (x, y, 1 - c)
        chips = _other_chips(x, y)
        locals_ = []
        for a in range(n):
            for s, m in wholes[a]:
                loc = pltpu.make_async_copy(ins[a].at[pl.ds(s, m)], outs[a].at[me, pl.ds(s, m)],
                                            local_sems.at[len(locals_)])
                loc.start()
                locals_.append(loc)
        ici = []
        for a in range(n):
            half = ins[a].shape[0] // 2
            for s, m in halves[a]:
                for j, (cx, cy) in enumerate(chips):
                    k = len(ici)
                    cp = _remote(ins[a].at[pl.ds(c * half + s, m)], outs[a].at[me, pl.ds(c * half + s, m)],
                                 send_sems.at[k], recv_sems.at[k], (cx, cy, c))
                    cp.start()
                    ici.append((a, s, m, j, cp))
        passed = []
        for k, (a, s, m, j, _) in enumerate(ici):
            half = ins[a].shape[0] // 2
            cx, cy = chips[j]
            blk = outs[a].at[2 * cx + cy, pl.ds(c * half + s, m)]
            _remote(blk, blk, send_sems.at[k], recv_sems.at[k], (cx, cy, c)).wait_recv()
            fw = _remote(blk, blk, send_sems.at[n_ici + k], recv_sems.at[n_ici + k], sibling)
            fw.start()
            passed.append(fw)
        for k, (a, s, m, j, _) in enumerate(ici):
            half = ins[a].shape[0] // 2
            cx, cy = chips[j]
            blk = outs[a].at[2 * cx + cy, pl.ds((1 - c) * half + s, m)]
            _remote(blk, blk, send_sems.at[n_ici + k], recv_sems.at[n_ici + k], sibling).wait_recv()
        for item in ici:
            item[4].wait_send()
        for fw in passed:
            fw.wait_send()
        for loc in locals_:
            loc.wait()

    return _pcall(
        body, name="allgather_weights", in_specs=[ANY] * n, out_specs=[ANY] * n,
        out_shape=[jax.ShapeDtypeStruct((4,) + p.shape, p.dtype) for p in pieces],
        scratch_shapes=[pltpu.SemaphoreType.DMA((2 * n_ici,)), pltpu.SemaphoreType.DMA((2 * n_ici,)),
                        pltpu.SemaphoreType.DMA((n_loc,))],
        compiler_params=pltpu.CompilerParams(has_side_effects=True),
    )(*pieces)


def _sibling_exchange(grads):
    n = len(grads)
    chunks = [_row_chunks(g.shape[1] // 2, g.shape[2] * g.dtype.itemsize) for g in grads]
    n_sem = 4 * sum(len(ch) for ch in chunks)

    def body(*refs):
        ins, gots = refs[:n], refs[n:2 * n]
        send_sems, recv_sems = refs[2 * n:]
        x, y, c = _place()
        sibling = (x, y, 1 - c)
        work = []
        for a in range(n):
            half = ins[a].shape[1] // 2
            for piece in range(4):
                for s, m in chunks[a]:
                    k = len(work)
                    cp = _remote(ins[a].at[piece, pl.ds((1 - c) * half + s, m)], gots[a].at[piece, pl.ds(s, m)],
                                 send_sems.at[k], recv_sems.at[k], sibling)
                    cp.start()
                    work.append(cp)
        for cp in work:
            cp.wait()

    return _pcall(
        body, name="grad_sibling_exchange", in_specs=[ANY] * n, out_specs=[ANY] * n,
        out_shape=[jax.ShapeDtypeStruct((4, g.shape[1] // 2, g.shape[2]), g.dtype) for g in grads],
        scratch_shapes=[pltpu.SemaphoreType.DMA((n_sem,)), pltpu.SemaphoreType.DMA((n_sem,))],
        compiler_params=pltpu.CompilerParams(has_side_effects=True),
    )(*grads)


def _sibling_gather(fulls):
    n = len(fulls)
    chunks = [_row_chunks(f.shape[0] // 2, f.shape[1] * f.dtype.itemsize) for f in fulls]
    n_sem = sum(len(ch) for ch in chunks)

    def body(*refs):
        outs = refs[n:2 * n]
        send_sems, recv_sems = refs[2 * n:]
        x, y, c = _place()
        sibling = (x, y, 1 - c)
        work = []
        for a in range(n):
            h = outs[a].shape[0] // 2
            for s, m in chunks[a]:
                k = len(work)
                mine = outs[a].at[pl.ds(c * h + s, m)]
                cp = _remote(mine, mine, send_sems.at[k], recv_sems.at[k], sibling)
                cp.start()
                work.append((a, s, m, cp))
        for k, (a, s, m, cp) in enumerate(work):
            h = outs[a].shape[0] // 2
            cp.wait_send()
            theirs = outs[a].at[pl.ds((1 - c) * h + s, m)]
            _remote(theirs, theirs, send_sems.at[k], recv_sems.at[k], sibling).wait_recv()

    return _pcall(
        body, name="grad_sibling_gather", in_specs=[ANY] * n, out_specs=[ANY] * n,
        out_shape=[jax.ShapeDtypeStruct(f.shape, f.dtype) for f in fulls],
        input_output_aliases={a: a for a in range(n)},
        scratch_shapes=[pltpu.SemaphoreType.DMA((n_sem,)), pltpu.SemaphoreType.DMA((n_sem,))],
        compiler_params=pltpu.CompilerParams(has_side_effects=True),
    )(*fulls)


def _pair_sum(grad, got, place, name):
    _, rows, cols = grad.shape
    half = rows // 2
    tr = _row_tile(half, cols, 16)

    def body(p_ref, a_ref, b_ref, o_ref):
        o_ref[...] = (a_ref[...].astype(F32) + b_ref[...].astype(F32)).astype(BF16)

    return _pcall(
        body, name=name,
        grid_spec=pltpu.PrefetchScalarGridSpec(
            num_scalar_prefetch=1, grid=(4, half // tr),
            in_specs=[pl.BlockSpec((None, tr, cols), lambda k, i, p: (k, p[1] * (half // tr) + i, 0)),
                      pl.BlockSpec((None, tr, cols), lambda k, i, p: (k, i, 0))],
            out_specs=pl.BlockSpec((None, tr, cols), lambda k, i, p: (k, i, 0))),
        out_shape=jax.ShapeDtypeStruct((4, half, cols), BF16),
        compiler_params=_params("parallel", "parallel"),
    )(place, grad, got)


def _chip_sum(sums, got, place, name):
    _, h, cols = sums.shape
    tr = _row_tile(h, cols, 16)

    def body(p_ref, own_ref, g0, g1, g2, o_ref):
        o_ref[...] = ((own_ref[...].astype(F32) + g0[...].astype(F32)) + g1[...].astype(F32)) + g2[...].astype(F32)

    gspec = lambda j: pl.BlockSpec((None, tr, cols), lambda i, p: (j, i, 0))
    return _pcall(
        body, name=name,
        grid_spec=pltpu.PrefetchScalarGridSpec(
            num_scalar_prefetch=1, grid=(h // tr,),
            in_specs=[pl.BlockSpec((None, tr, cols), lambda i, p: (p[0], i, 0)), gspec(0), gspec(1), gspec(2)],
            out_specs=pl.BlockSpec((tr, cols), lambda i, p: (p[1] * (h // tr) + i, 0))),
        out_shape=jax.ShapeDtypeStruct((2 * h, cols), F32),
        compiler_params=_params("parallel"),
    )(place, sums, got, got, got)


def _allgather8(buf, name):
    rows = buf.shape[0]

    def body(in_ref, out_ref, send_sems, recv_sems):
        x, y, c = _place()
        me = 4 * x + 2 * y + c
        out_ref[me] = in_ref[...]
        work = []
        for rel in range(1, 8):
            fx, fy, fc = (rel >> 2) & 1, (rel >> 1) & 1, rel & 1
            to = (x ^ fx, y ^ fy, c ^ fc)
            cp = _remote(in_ref, out_ref.at[me], send_sems.at[rel - 1], recv_sems.at[rel - 1], to)
            cp.start()
            work.append((cp, 4 * to[0] + 2 * to[1] + to[2]))
        for rel, (cp, frm) in enumerate(work):
            cp.wait_send()
            blk = out_ref.at[frm]
            _remote(blk, blk, send_sems.at[rel], recv_sems.at[rel], (x, y, c)).wait_recv()

    return _pcall(
        body, name=name, in_specs=[pl.BlockSpec(memory_space=pltpu.VMEM)],
        out_specs=pl.BlockSpec(memory_space=pltpu.VMEM),
        out_shape=jax.ShapeDtypeStruct((8, rows, LANE), F32),
        scratch_shapes=[pltpu.SemaphoreType.DMA((7,)), pltpu.SemaphoreType.DMA((7,))],
        compiler_params=pltpu.CompilerParams(has_side_effects=True),
    )(buf)


def _pack_rows(arrs):
    parts = []
    for a in arrs:
        f = a.reshape(-1).astype(F32)
        parts.append(jnp.pad(f, (0, (-f.shape[0]) % LANE)))
    flat = jnp.concatenate(parts)
    rows = -(-flat.shape[0] // LANE)
    rows8 = -(-rows // 8) * 8
    return jnp.pad(flat, (0, rows8 * LANE - flat.shape[0])).reshape(rows8, LANE)


def _unpack_rows(buf, shapes):
    flat = buf.reshape(-1)
    outs, off = [], 0
    for s in shapes:
        n = int(np.prod(s))
        outs.append(flat[off:off + n].reshape(s))
        off += -(-n // LANE) * LANE
    return outs


def _local_grads(x, p, target, wseg, w_br16, w_out16, w_ple16, b_gate, conv_w, conv_b, dt_bias, a_log, d_skip,
                 ssm_norm_w, ln_g, ln_b, rel_bias, finish_dx):
    nb, seq, _ = x.shape
    bmaps = jnp.asarray(_bucket_maps())
    bias = _bias_tables(rel_bias, bmaps)
    bgate8 = jnp.pad(b_gate, ((0, 5), (0, 0)))
    dils = [d for _, d in PATTERNS]

    x16 = x.astype(BF16)
    p16 = p.astype(BF16)
    x16p = [_permute(x16, d) for d in dils]
    qkv = [_proj(x16p[g], [wseg["qkv%d" % g]], BF16, "proj_qkv%d" % g, True)[0].reshape(
        nb, dils[g], seq // dils[g], -1) for g in range(3)]
    nat = {}
    for gi, (group, tm) in enumerate(NAT_GROUPS):
        outs = _proj(x16, [wseg[s] for s in group], F32, "proj_nat%d" % gi, True, tm)
        nat.update(zip(group, outs))
    att = [_attn_fwd(qkv[g], bias[g * GROUP_HEADS:(g + 1) * GROUP_HEADS], dils[g], "attn_fwd%d" % g) for g in range(3)]
    oa, o_att, lse = _combine_fwd([a[0] for a in att], [a[1] for a in att], nat["gatt"])

    cw = {"xs": (conv_w[:, :D_INNER], conv_b[:, :D_INNER]),
          "bm": (conv_w[:, D_INNER:D_INNER + 512], conv_b[:, D_INNER:D_INNER + 512]),
          "cm": (conv_w[:, D_INNER + 512:], conv_b[:, D_INNER + 512:])}
    act = {s: _conv_fwd(nat[s], cw[s][0], cw[s][1], "conv_fwd_" + s) for s in ("xs", "bm", "cm")}
    dt_sp, dt_sg = _softplus_sig(nat["dt"], jnp.pad(dt_bias, ((0, 0), (0, LANE - SSM_HEADS))))
    dtg, sgg = _group_lanes(dt_sp), _group_lanes(dt_sg)
    alog_g, dskip_g = _group_lanes(a_log), _group_lanes(d_skip)
    y_ssm, y_all, sprev = _ssd_fwd(act["xs"], act["bm"], act["cm"], dtg, nat["z"], alog_g, dskip_g, ssm_norm_w)

    w_bra, w_brb = w_br16[:ATT_OUT], w_br16[ATT_OUT:]
    y_a, = _proj(oa, [w_bra], F32, "proj_ya")
    y_b, = _proj(y_ssm, [w_brb], F32, "proj_yb")
    merged = _merge_fwd(y_a, y_b, nat["gm"], bgate8)
    mix, = _proj(merged, [w_out16], F32, "proj_mix")
    pw, = _proj(p16, [w_ple16], F32, "proj_ple")

    dx, dpre16, dpw16, dgp16, ln_sums = _ln_loss(x, mix, nat["gp"], pw, target, bgate8, ln_g, ln_b)
    loss_sum = (0.5 / D_MODEL) * jnp.sum(ln_sums[3])
    dmerged = _dx([dpre16], [w_out16], [], "dx_merged")
    dya16, dyb16, dgm16, mg_sums = _merge_bwd(dmerged, y_a, y_b, nat["gm"], bgate8)
    doa = _dx([dya16], [w_bra], [], "dx_oa")
    dys = _dx([dyb16], [w_brb], [], "dx_yssm")
    g_w_out, = _dw(merged, [dpre16], BF16, "dw_out")
    g_w_br = jnp.concatenate([_dw(oa, [dya16], BF16, "dw_bra")[0], _dw(y_ssm, [dyb16], BF16, "dw_brb")[0]], axis=0)
    g_w_ple, = _dw(p16, [dpw16], BF16, "dw_ple")

    do_att, dgatt16 = _combine_bwd(doa, nat["gatt"], o_att)
    dseg = {"gatt": dgatt16, "gm": dgm16, "gp": dgp16}
    dbias = []
    for g in range(3):
        dqkv, db = _attn_bwd(qkv[g], bias[g * GROUP_HEADS:(g + 1) * GROUP_HEADS], do_att, o_att, lse, dils[g],
                             "attn_bwd%d" % g)
        dseg["qkv%d" % g] = dqkv.reshape(nb, seq, -1)
        dbias.append(db)
    g_rel = _bias_grad(jnp.concatenate(dbias, axis=0), bmaps)[:, 0, :NUM_BUCKETS].T

    dxs, dbm, dcm, ddtg, dz, ssd_small, g_normw = _ssd_bwd(
        act["xs"], act["bm"], act["cm"], dtg, sgg, nat["z"], y_all, dys, sprev, alog_g, dskip_g, ssm_norm_w)
    dseg["z"] = dz
    dseg["dt"] = jnp.pad(_ungroup_lanes(ddtg), ((0, 0), (0, 0), (0, LANE - SSM_HEADS)))
    conv_sums = {}
    for s, dact in (("xs", dxs), ("bm", dbm), ("cm", dcm)):
        dpre, conv_sums[s] = _conv_bwd_pre(dact, nat[s], cw[s][0], cw[s][1], "conv_bwd_" + s)
        dseg[s] = _conv_bwd_x(dpre, cw[s][0], "conv_bwd_x_" + s)
    csum = jnp.concatenate([conv_sums["xs"], conv_sums["bm"], conv_sums["cm"]], axis=1)

    dx_perm = [_unpermute(_dx([dseg["qkv%d" % g]], [wseg["qkv%d" % g]], [], "dx_qkv%d" % g, True), dils[g])
               for g in (1, 2)]
    dwseg = {"qkv%d" % g: _dw(x16p[g], [dseg["qkv%d" % g]], BF16, "dw_qkv%d" % g, True)[0] for g in range(3)}
    for gi, group in enumerate(DW_GROUPS):
        dwseg.update(zip(group, _dw(x16, [dseg[s] for s in group], BF16, "dw_nat%d" % gi, True)))
    names = ["qkv0"] + [s for group, _ in NAT_GROUPS for s in group]
    dx = finish_dx([dseg[s] for s in names], [wseg[s] for s in names], [dx] + dx_perm, dwseg, g_w_br, g_w_out, g_w_ple)

    small = dict(
        b_gate=jnp.stack([mg_sums[0], mg_sums[1], ln_sums[2]]),
        conv_w=csum[0:4], conv_b=csum[4:5],
        dt_bias=_ungroup_lanes(ssd_small[:, 2:3, :]), a_log=_ungroup_lanes(ssd_small[:, 0:1, :]),
        d_skip=_ungroup_lanes(ssd_small[:, 1:2, :]), ssm_norm_w=g_normw,
        ln_g=ln_sums[0:1], ln_b=ln_sums[1:2], rel_bias=g_rel)
    return loss_sum, dx, small


DX_TM = 256
SMALL_ORDER = ("b_gate", "conv_w", "conv_b", "dt_bias", "a_log", "d_skip", "ssm_norm_w", "ln_g", "ln_b", "rel_bias")
SMALL_FULL_SHAPES = dict(b_gate=(3, 1024), conv_w=(4, 3072), conv_b=(1, 3072), dt_bias=(1, 32), a_log=(1, 32),
                         d_skip=(1, 32), ssm_norm_w=(1, 2048), ln_g=(1, 1024), ln_b=(1, 1024), rel_bias=(32, 36))


def kernel(x, p, w_in, b_gate, conv_w, conv_b, dt_bias, a_log, d_skip, ssm_norm_w, w_branch, w_out, w_ple, ln_g, ln_b, rel_bias, loss_target, m_w_in, m_b_gate, m_conv_w, m_conv_b, m_dt_bias, m_a_log, m_d_skip, m_ssm_norm_w, m_w_branch, m_w_out, m_w_ple, m_ln_g, m_ln_b, m_rel_bias, v_w_in, v_b_gate, v_conv_w, v_conv_b, v_dt_bias, v_a_log, v_d_skip, v_ssm_norm_w, v_w_branch, v_w_out, v_w_ple, v_ln_g, v_ln_b, v_rel_bias):
    cx, cy, cc = _place()
    chip = 2 * cx + cy
    dev = 4 * cx + 2 * cy + cc

    w_in_t = jnp.transpose(w_in[0])
    win16 = _shard_to_window(w_in_t, chip)
    g_win, g_br, g_out, g_ple = _allgather_pieces(
        [win16, w_branch[0].astype(BF16), w_out[0].astype(BF16), w_ple[0].astype(BF16)])
    wseg = _assemble(g_win)
    w_br16 = g_br.reshape(4 * 704, D_MODEL)
    w_out16 = g_out.reshape(D_MODEL, D_MODEL)
    w_ple16 = jnp.transpose(g_ple, (1, 0, 2)).reshape(PLE_DIM, D_MODEL)
    shards = _allgather8(_pack_rows([b_gate[0], conv_w[0]]), "allgather_small_params")
    per_chip = [_unpack_rows(shards[2 * k], [(3, 256), (4, 768)]) for k in range(4)]
    b_gate_full = jnp.concatenate([pc[0] for pc in per_chip], axis=1)
    conv_w_full = jnp.concatenate([pc[1] for pc in per_chip], axis=1)

    place = jnp.stack([chip, cc]).astype(jnp.int32)
    reduced = []

    def finish_dx(dhs, ws, accs, dwseg, d_br, d_out, d_ple):
        grads = [_pack(dwseg), d_br.reshape(4, 704, D_MODEL), d_out.reshape(4, 256, D_MODEL),
                 jnp.transpose(d_ple.reshape(PLE_DIM, 4, 256), (1, 0, 2))]
        got = _sibling_exchange(grads)
        chip_sums = [_pair_sum(g, t, place, "grad_pair_sum_%d" % i) for i, (g, t) in enumerate(zip(grads, got))]
        dx, others = _dx(dhs, ws, accs, "dx_w_in_and_grad_chip_scatter", True, DX_TM, chip_sums)
        fulls = [_chip_sum(s, t, place, "grad_chip_sum_%d" % i) for i, (s, t) in enumerate(zip(chip_sums, others))]
        reduced.extend(_sibling_gather(fulls))
        return dx

    loss_sum, grad_x, small = _local_grads(
        x, p[0], loss_target, wseg, w_br16, w_out16, w_ple16, b_gate_full, conv_w_full, conv_b, dt_bias, a_log,
        d_skip, ssm_norm_w, ln_g, ln_b, rel_bias, finish_dx)
    loss = lax.psum(loss_sum, ("x", "y", "c"))
    big = reduced
    g_w_in = _window_to_shard(big[0], chip)
    g_w_branch, g_w_out, g_w_ple = big[1], big[2], big[3]
    parts = _allgather8(_pack_rows([small[n] for n in SMALL_ORDER]), "allgather_small_grads")
    small_sum = _sum_rows([parts[i] for i in range(8)], F32, "small_grad_sum")
    sg = dict(zip(SMALL_ORDER, _unpack_rows(small_sum, [SMALL_FULL_SHAPES[n] for n in SMALL_ORDER])))
    sg["b_gate"] = lax.dynamic_slice_in_dim(sg["b_gate"], chip * 256, 256, axis=1)
    sg["conv_w"] = lax.dynamic_slice_in_dim(sg["conv_w"], chip * 768, 768, axis=1)
    del dev

    upd = {}
    upd["w_in"] = [jnp.transpose(t) for t in _adamw(w_in_t, g_w_in, jnp.transpose(m_w_in[0]),
                                                      jnp.transpose(v_w_in[0]), "adamw_w_in")]
    upd["w_branch"] = _adamw(w_branch[0], g_w_branch, m_w_branch[0], v_w_branch[0], "adamw_w_branch")
    upd["w_out"] = _adamw(w_out[0], g_w_out, m_w_out[0], v_w_out[0], "adamw_w_out")
    upd["w_ple"] = _adamw(w_ple[0], g_w_ple, m_w_ple[0], v_w_ple[0], "adamw_w_ple")
    small_w = dict(b_gate=b_gate, conv_w=conv_w, conv_b=conv_b, dt_bias=dt_bias, a_log=a_log, d_skip=d_skip,
                   ssm_norm_w=ssm_norm_w, ln_g=ln_g, ln_b=ln_b, rel_bias=rel_bias)
    small_m = dict(b_gate=m_b_gate, conv_w=m_conv_w, conv_b=m_conv_b, dt_bias=m_dt_bias, a_log=m_a_log,
                   d_skip=m_d_skip, ssm_norm_w=m_ssm_norm_w, ln_g=m_ln_g, ln_b=m_ln_b, rel_bias=m_rel_bias)
    small_v = dict(b_gate=v_b_gate, conv_w=v_conv_w, conv_b=v_conv_b, dt_bias=v_dt_bias, a_log=v_a_log,
                   d_skip=v_d_skip, ssm_norm_w=v_ssm_norm_w, ln_g=v_ln_g, ln_b=v_ln_b, rel_bias=v_rel_bias)
    shapes = [small_w[n].shape for n in SMALL_ORDER]
    s_delta, s_m, s_v = _adamw(_pack_rows([small_w[n] for n in SMALL_ORDER]), _pack_rows([sg[n] for n in SMALL_ORDER]),
                               _pack_rows([small_m[n] for n in SMALL_ORDER]), _pack_rows([small_v[n] for n in SMALL_ORDER]),
                               "adamw_small")
    for i, n in enumerate(SMALL_ORDER):
        upd[n] = tuple(_unpack_rows(t, shapes)[i] for t in (s_delta, s_m, s_v))
        sg[n] = sg[n].reshape(small_w[n].shape)

    order = ("w_in", "b_gate", "conv_w", "conv_b", "dt_bias", "a_log", "d_skip", "ssm_norm_w", "w_branch", "w_out",
             "w_ple", "ln_g", "ln_b", "rel_bias")
    grads = dict(sg, w_in=jnp.transpose(g_w_in)[None],w_branch=g_w_branch[None], w_out=g_w_out[None], w_ple=g_w_ple[None])
    lead = lambda n, t: t[None] if n in ("w_in", "w_branch", "w_out", "w_ple") else t
    return (loss, grad_x, *[grads[n] for n in order], *[lead(n, upd[n][0]) for n in order],
            *[lead(n, upd[n][1]) for n in order], *[lead(n, upd[n][2]) for n in order])
```

```python
import functools
import math

import numpy as np
import jax
import jax.numpy as jnp
from jax import lax
from jax.experimental import pallas as pl
from jax.experimental.pallas import tpu as pltpu

F32, BF16 = jnp.float32, jnp.bfloat16
HIGHEST = lax.Precision.HIGHEST

D_MODEL = 1024
HEAD_DIM = 64
GROUP_HEADS = 12
ATT_OUT = GROUP_HEADS * HEAD_DIM
PATTERNS = ((128, 1), (512, 4), (2048, 16))
BAND = 128
NUM_BUCKETS = 32
MAX_DISTANCE = 2048
D_INNER = 2048
SSM_HEADS = 32
SSM_GROUPS = 4
GROUP_SSM_HEADS = SSM_HEADS // SSM_GROUPS
D_STATE = 128
CHUNK = 128
PLE_DIM = 256
ALPHA = 2.0 ** 0.25
LN_EPS = 1e-5
RMS_EPS = 1e-5
ADAM_LR, ADAM_B1, ADAM_B2, ADAM_EPS, ADAM_WD, ADAM_STEP = 0.001, 0.9, 0.999, 1e-08, 0.01, 10
NEG = -1e30

QKV_W = 3 * ATT_OUT
IN_COLS = 15904
SHARD_COLS = IN_COLS // 4
DT_COL = 12800
ROW_TILE = 16
WIN_ROWS = 4000


def _win_offset(k):
    return (k * SHARD_COLS) % ROW_TILE


def _win_start(k):
    return k * SHARD_COLS - _win_offset(k)

VMEM_LIMIT_BYTES = 56 * 1024 * 1024
LANE = 128
MESH = pl.DeviceIdType.MESH
NT = (((1,), (1,)), ((), ()))
TN = (((0,), (0,)), ((), ()))


def _pcall(body, **kw):
    return pl.pallas_call(body, **kw)


def _params(*sem):
    return pltpu.CompilerParams(dimension_semantics=sem, vmem_limit_bytes=VMEM_LIMIT_BYTES)


def _sigmoid(v):
    return jax.nn.sigmoid(v)


MM_TM = 512


def _permute(t, d):
    nb, seq, ch = t.shape
    return t if d == 1 else t.reshape(nb, seq // d, d, ch).transpose(0, 2, 1, 3).reshape(nb, seq, ch)


def _unpermute(t, d):
    nb, seq, ch = t.shape
    return t if d == 1 else t.reshape(nb, d, seq // d, ch).transpose(0, 2, 1, 3).reshape(nb, seq, ch)


def _tok_spec(tm, width):
    return pl.BlockSpec((None, tm, width), lambda b, i: (b, i, 0))


def _whole(arr, single_buffer=False):
    mode = dict(pipeline_mode=pl.Buffered(1)) if single_buffer else {}
    return pl.BlockSpec(arr.shape, lambda b, i: (0,) * arr.ndim, **mode)


def _proj(a3, ws, out_dtype, name, w_rows_are_outputs=False, tm=MM_TM):
    nb, seq, kdim = a3.shape
    nw = len(ws)
    widths = [w.shape[0] if w_rows_are_outputs else w.shape[1] for w in ws]

    def body(*refs):
        a = refs[0][...].astype(BF16)
        for w_ref, o_ref in zip(refs[1:1 + nw], refs[1 + nw:]):
            if w_rows_are_outputs:
                v = lax.dot_general(a, w_ref[...], NT, preferred_element_type=F32)
            else:
                v = jnp.dot(a, w_ref[...], preferred_element_type=F32)
            o_ref[...] = v.astype(out_dtype)

    return _pcall(
        body, name=name, grid=(nb, seq // tm),
        in_specs=[_tok_spec(tm, kdim)] + [_whole(w) for w in ws],
        out_specs=[_tok_spec(tm, n) for n in widths],
        out_shape=[jax.ShapeDtypeStruct((nb, seq, n), out_dtype) for n in widths],
        compiler_params=_params("parallel", "parallel"),
    )(a3, *ws)


def _dx(dhs, ws, accs, name, w_rows_are_outputs=False, tm=MM_TM, scatter=None):
    nb, seq, _ = dhs[0].shape
    nd, nacc = len(dhs), len(accs)
    kout = ws[0].shape[1] if w_rows_are_outputs else ws[0].shape[0]
    sums = scatter or []
    ns = len(sums)
    chunks = [_row_chunks(s.shape[1], s.shape[2] * s.dtype.itemsize) for s in sums]
    n_sem = 3 * sum(len(ch) for ch in chunks)
    grid = (nb, seq // tm)

    def body(*refs):
        n_in = 2 * nd + nacc
        sum_refs, o_ref, got_refs = refs[n_in:n_in + ns], refs[n_in + ns], refs[n_in + ns + 1:n_in + 2 * ns + 1]

        def copies():
            send_sems, recv_sems = refs[-2], refs[-1]
            x, y, c = _place()
            out = []
            for a in range(ns):
                for s, m in chunks[a]:
                    for j, (cx, cy) in enumerate(_other_chips(x, y)):
                        k = len(out)
                        out.append(_remote(sum_refs[a].at[2 * cx + cy, pl.ds(s, m)], got_refs[a].at[j, pl.ds(s, m)],
                                           send_sems.at[k], recv_sems.at[k], (cx, cy, c)))
            return out

        if ns:
            @pl.when((pl.program_id(0) == 0) & (pl.program_id(1) == 0))
            def _():
                for cp in copies():
                    cp.start()

        v = None
        for dh_ref, w_ref in zip(refs[:nd], refs[nd:2 * nd]):
            dh = dh_ref[...].astype(BF16)
            if w_rows_are_outputs:
                t = jnp.dot(dh, w_ref[...], preferred_element_type=F32)
            else:
                t = lax.dot_general(dh, w_ref[...], NT, preferred_element_type=F32)
            v = t if v is None else v + t
        for a_ref in refs[2 * nd:n_in]:
            v = v + a_ref[...]
        o_ref[...] = v

        if ns:
            @pl.when((pl.program_id(0) == grid[0] - 1) & (pl.program_id(1) == grid[1] - 1))
            def _():
                for cp in copies():
                    cp.wait()

    out = _pcall(
        body, name=name, grid=grid,
        in_specs=[_tok_spec(tm, dh.shape[-1]) for dh in dhs] + [_whole(w, bool(ns)) for w in ws]
        + [_tok_spec(tm, kout)] * nacc + [ANY] * ns,
        out_specs=[_tok_spec(tm, kout)] + [ANY] * ns,
        out_shape=[jax.ShapeDtypeStruct((nb, seq, kout), F32)]
        + [jax.ShapeDtypeStruct((3,) + s.shape[1:], s.dtype) for s in sums],
        input_output_aliases={2 * nd: 0} if nacc else {},
        scratch_shapes=[pltpu.SemaphoreType.DMA((n_sem,)), pltpu.SemaphoreType.DMA((n_sem,))] if ns else [],
        compiler_params=pltpu.CompilerParams(
            dimension_semantics=("arbitrary", "arbitrary") if ns else ("parallel", "parallel"),
            vmem_limit_bytes=VMEM_LIMIT_BYTES, has_side_effects=bool(ns)),
    )(*dhs, *ws, *accs, *sums)
    return (out[0], list(out[1:])) if ns else out[0]


def _dw(a3, dhs, out_dtype, name, rows_are_outputs=False):
    nb, seq, kdim = a3.shape
    nd = len(dhs)
    grid = (nb, seq // MM_TM)
    shapes = [(dh.shape[-1], kdim) if rows_are_outputs else (kdim, dh.shape[-1]) for dh in dhs]

    def body(*refs):
        b, i = pl.program_id(0), pl.program_id(1)
        dh_refs, o_refs, acc_refs = refs[1:1 + nd], refs[1 + nd:1 + 2 * nd], refs[1 + 2 * nd:]

        @pl.when((b == 0) & (i == 0))
        def _():
            for acc_ref in acc_refs:
                acc_ref[...] = jnp.zeros_like(acc_ref)

        a = refs[0][...].astype(BF16)
        for dh_ref, acc_ref in zip(dh_refs, acc_refs):
            dh = dh_ref[...].astype(BF16)
            acc_ref[...] += lax.dot_general(*((dh, a) if rows_are_outputs else (a, dh)), TN,
                                            preferred_element_type=F32)

        @pl.when((b == grid[0] - 1) & (i == grid[1] - 1))
        def _():
            for o_ref, acc_ref in zip(o_refs, acc_refs):
                o_ref[...] = acc_ref[...].astype(out_dtype)

    return _pcall(
        body, name=name, grid=grid,
        in_specs=[_tok_spec(MM_TM, kdim)] + [_tok_spec(MM_TM, dh.shape[-1]) for dh in dhs],
        out_specs=[pl.BlockSpec(s, lambda b, i: (0, 0)) for s in shapes],
        out_shape=[jax.ShapeDtypeStruct(s, out_dtype) for s in shapes],
        scratch_shapes=[pltpu.VMEM(s, F32) for s in shapes],
        compiler_params=_params("arbitrary", "arbitrary"),
    )(a3, *dhs)


def _qkv_rows(g):
    return [(part * QKV_W + g * ATT_OUT + hp * LANE, LANE) for hp in range(ATT_OUT // LANE) for part in range(3)]


def _segments():
    one = lambda name, start, rows: (name, [(start, rows)], max(rows, LANE))
    return [("qkv%d" % g, _qkv_rows(g), QKV_W) for g in range(3)] + [
        one("gatt", 3 * QKV_W, ATT_OUT), one("z", 3 * QKV_W + ATT_OUT, D_INNER), one("xs", 9728, D_INNER),
        one("bm", 9728 + D_INNER, 512), one("cm", 9728 + D_INNER + 512, 512), one("dt", DT_COL, SSM_HEADS),
        one("gm", DT_COL + SSM_HEADS, 2 * D_MODEL), one("gp", DT_COL + SSM_HEADS + 2 * D_MODEL, D_MODEL)]


LAYOUT_TC = 256
NAT_GROUPS = ((("gatt", "z", "dt", "bm", "cm"), 512), (("xs", "gm", "gp"), 256))
DW_GROUPS = (("gatt", "z", "dt", "bm", "cm"), ("xs", "gp"), ("gm",))


def _assemble(win):
    segs = _segments()

    def body(win_ref, *outs):
        def pieces(start, rows):
            t, end = start, start + rows
            while t < end:
                k = min(t // SHARD_COLS, 3)
                shard_end = (k + 1) * SHARD_COLS
                if k < 3 and shard_end % ROW_TILE and t == shard_end - shard_end % ROW_TILE:
                    lo = t - _win_start(k)
                    yield win_ref[k, lo:lo + ROW_TILE, :] + win_ref[k + 1, 0:ROW_TILE, :]
                    t += ROW_TILE
                    continue
                upto = min(end, shard_end - shard_end % ROW_TILE if k < 3 else end)
                yield win_ref[k, t - _win_start(k):upto - _win_start(k), :]
                t = upto

        for (_, ranges, total), o_ref in zip(segs, outs):
            off = 0
            for start, rows in ranges:
                for part in pieces(start, rows):
                    o_ref[off:off + part.shape[0], :] = part
                    off += part.shape[0]
            if off < total:
                o_ref[off:total, :] = jnp.zeros((total - off, o_ref.shape[1]), BF16)

    outs = _pcall(
        body, name="assemble_w_in", grid=(D_MODEL // LAYOUT_TC,),
        in_specs=[pl.BlockSpec((4, WIN_ROWS, LAYOUT_TC), lambda i: (0, 0, i))],
        out_specs=[pl.BlockSpec((total, LAYOUT_TC), lambda i: (0, i)) for _, _, total in segs],
        out_shape=[jax.ShapeDtypeStruct((total, D_MODEL), BF16) for _, _, total in segs],
        compiler_params=_params("parallel"),
    )(win)
    return {name: o for (name, _, _), o in zip(segs, outs)}


def _pack(dsegs):
    segs = _segments()

    def body(*refs):
        ins, o_ref = refs[:-1], refs[-1]
        tail = IN_COLS - _win_start(3)
        o_ref[3, tail:, :] = jnp.zeros((WIN_ROWS - tail, o_ref.shape[2]), BF16)
        for (_, ranges, _), s_ref in zip(segs, ins):
            off = 0
            for start, rows in ranges:
                for k in range(4):
                    lo = _win_start(k)
                    a, b = max(start, lo), min(start + rows, lo + WIN_ROWS)
                    if a < b:
                        o_ref[k, a - lo:b - lo, :] = s_ref[off + a - start:off + b - start, :]
                off += rows

    return _pcall(
        body, name="pack_dw_in", grid=(D_MODEL // LAYOUT_TC,),
        in_specs=[pl.BlockSpec((total, LAYOUT_TC), lambda i: (0, i)) for _, _, total in segs],
        out_specs=pl.BlockSpec((4, WIN_ROWS, LAYOUT_TC), lambda i: (0, 0, i)),
        out_shape=jax.ShapeDtypeStruct((4, WIN_ROWS, D_MODEL), BF16),
        compiler_params=_params("parallel"),
    )(*[dsegs[name] for name, _, _ in segs])


def _shard_to_window(shard_t, k):
    def at(off):
        return lambda w: jnp.pad(w.astype(BF16), ((off, WIN_ROWS - SHARD_COLS - off), (0, 0)))

    return lax.cond(k % 2 == 1, at(_win_offset(1)), at(_win_offset(0)), shard_t)


def _window_to_shard(win, k):
    return lax.dynamic_slice(win, ((k % 2) * _win_offset(1), 0), (SHARD_COLS, D_MODEL))


def _bucket_maps():
    qi = np.arange(BAND)[:, None]
    kj = np.arange(2 * BAND)[None, :]
    delta = qi + BAND - kj
    maps = []
    for window, dil in PATTERNS:
        valid = (delta >= 0) & (delta <= window // dil)
        dist = np.maximum(delta, 0) * dil
        max_exact = NUM_BUCKETS // 2
        d_f = np.maximum(dist, 1).astype(np.float32)
        large = max_exact + (np.log(d_f / np.float32(max_exact)) / np.float32(math.log(MAX_DISTANCE / max_exact))
                             * np.float32(NUM_BUCKETS - max_exact)).astype(np.int32)
        large = np.minimum(large, NUM_BUCKETS - 1)
        bucket = np.where(dist < max_exact, dist, large)
        maps.append(np.where(valid, bucket, -1).astype(np.int32))
    return np.stack(maps)


def _bias_tables(rel_bias, bmaps):
    def body(rb_ref, bm_ref, o_ref):
        h = pl.program_id(0)
        bm = bm_ref[...]
        acc = jnp.full(bm.shape, NEG, F32)
        for b in range(NUM_BUCKETS):
            acc = jnp.where(bm == b, rb_ref[b, h], acc)
        o_ref[...] = acc

    return _pcall(
        body, name="bias_tables", grid=(3 * GROUP_HEADS,),
        in_specs=[pl.BlockSpec(memory_space=pltpu.SMEM),
                  pl.BlockSpec((None, BAND, 2 * BAND), lambda h: (h // GROUP_HEADS, 0, 0))],
        out_specs=pl.BlockSpec((None, BAND, 2 * BAND), lambda h: (h, 0, 0)),
        out_shape=jax.ShapeDtypeStruct((3 * GROUP_HEADS, BAND, 2 * BAND), F32),
        compiler_params=_params("parallel"),
    )(rel_bias, bmaps)


def _bias_grad(dbias, bmaps):
    def body(db_ref, bm_ref, o_ref):
        bm = bm_ref[...]
        db = db_ref[...]
        lane = lax.broadcasted_iota(jnp.int32, (1, LANE), 1)
        vec = jnp.zeros((1, LANE), F32)
        for b in range(NUM_BUCKETS):
            s = jnp.sum(jnp.where(bm == b, db, 0.0), keepdims=True)
            vec = jnp.where(lane == b, s, vec)
        o_ref[...] = vec

    return _pcall(
        body, name="bias_grad", grid=(3 * GROUP_HEADS,),
        in_specs=[pl.BlockSpec((None, BAND, 2 * BAND), lambda h: (h, 0, 0)),
                  pl.BlockSpec((None, BAND, 2 * BAND), lambda h: (h // GROUP_HEADS, 0, 0))],
        out_specs=pl.BlockSpec((None, 1, LANE), lambda h: (h, 0, 0)),
        out_shape=jax.ShapeDtypeStruct((3 * GROUP_HEADS, 1, LANE), F32),
        compiler_params=_params("parallel"),
    )(dbias, bmaps)


def _rows(n):
    if isinstance(n, int):
        return pl.ds(n * BAND, BAND)
    return pl.ds(pl.multiple_of(n * BAND, BAND), BAND)


def _for_blocks(blocks, nblk, per, carry):
    carry = blocks([0], carry, False)
    start = 1 + (nblk - 1) % per
    for n in range(1, start):
        carry = blocks([n], carry, True)
    trips = (nblk - start) // per
    if trips > 0:
        carry = lax.fori_loop(
            0, trips, lambda t, c: blocks([start + t * per + u for u in range(per)], c, True), carry)
    return carry


def _pairs_per_step(d):
    return {1: 1, 4: 6, 16: 6}[d]


def _head_cols(i, h, part):
    base = 3 * LANE * i + part * LANE + h * HEAD_DIM
    return slice(base, base + HEAD_DIM)


def _attn_fwd(qkv4, bias, d, name):
    nb, _, sub, _ = qkv4.shape
    nblk = sub // BAND
    scale = HEAD_DIM ** -0.5
    npair = ATT_OUT // LANE
    hps = _pairs_per_step(d)
    compact = d > 1

    def body(qkv_ref, bias_ref, o_ref, l_ref):
        def blocks(ns, carry, with_prev):
            chains = [(bi, i, h) for bi in range(len(ns)) for i in range(hps) for h in range(2)]
            scores = []
            for bi, i, h in chains:
                n = ns[bi]
                q = qkv_ref[_rows(n), _head_cols(i, h, 0)] * scale
                s_c = lax.dot_general(q, qkv_ref[_rows(n), _head_cols(i, h, 1)], NT,
                                      preferred_element_type=F32) + bias_ref[2 * i + h, :, BAND:]
                s_p = None
                if with_prev:
                    s_p = lax.dot_general(q, qkv_ref[_rows(n - 1), _head_cols(i, h, 1)], NT,
                                          preferred_element_type=F32) + bias_ref[2 * i + h, :, :BAND]
                scores.append((s_c, s_p))
            probs = []
            for s_c, s_p in scores:
                m = jnp.max(s_c, -1, keepdims=True)
                if with_prev:
                    m = jnp.maximum(m, jnp.max(s_p, -1, keepdims=True))
                e_c = jnp.exp(s_c - m)
                den = jnp.sum(e_c, -1, keepdims=True)
                e_p = None
                if with_prev:
                    e_p = jnp.exp(s_p - m)
                    den = den + jnp.sum(e_p, -1, keepdims=True)
                    e_p = e_p.astype(BF16)
                probs.append((e_c.astype(BF16), e_p, den, m))
            outs = {}
            for (bi, i, h), (e_c, e_p, den, m) in zip(chains, probs):
                n = ns[bi]
                acc = jnp.dot(e_c, qkv_ref[_rows(n), _head_cols(i, h, 2)], preferred_element_type=F32)
                if with_prev:
                    acc = acc + jnp.dot(e_p, qkv_ref[_rows(n - 1), _head_cols(i, h, 2)], preferred_element_type=F32)
                outs[(bi, i, h)] = (acc / den, jnp.broadcast_to(m + jnp.log(den), (BAND, HEAD_DIM)))
            lane = lax.broadcasted_iota(jnp.int32, (BAND, LANE), 1)
            for bi, n in enumerate(ns):
                per_head = jnp.zeros((BAND, LANE), F32)
                for i in range(hps):
                    o_ref[_rows(n), i * LANE:(i + 1) * LANE] = jnp.concatenate(
                        [outs[(bi, i, 0)][0], outs[(bi, i, 1)][0]], axis=1)
                    if compact:
                        for h in range(2):
                            per_head = jnp.where(lane == 2 * i + h, outs[(bi, i, h)][1][:, :1], per_head)
                    else:
                        l_ref[_rows(n), i * LANE:(i + 1) * LANE] = jnp.concatenate(
                            [outs[(bi, i, 0)][1], outs[(bi, i, 1)][1]], axis=1)
                if compact:
                    l_ref[_rows(n), :] = per_head
            return carry

        _for_blocks(blocks, nblk, 2 if hps == 1 else 1, 0)

    in_specs = [pl.BlockSpec((None, None, sub, 3 * LANE * hps), lambda hp, b, r: (b, r, 0, hp)),
                pl.BlockSpec((2 * hps, BAND, 2 * BAND), lambda hp, b, r: (hp, 0, 0))]
    if compact:
        return _pcall(
            body, name=name, grid=(1, nb, d), in_specs=in_specs,
            out_specs=[pl.BlockSpec((None, None, sub, ATT_OUT), lambda hp, b, r: (b, r, 0, 0)),
                       pl.BlockSpec((None, None, sub, LANE), lambda hp, b, r: (b, r, 0, 0))],
            out_shape=[jax.ShapeDtypeStruct((nb, d, sub, ATT_OUT), F32), jax.ShapeDtypeStruct((nb, d, sub, LANE), F32)],
            compiler_params=_params("parallel", "parallel", "parallel"),
        )(qkv4, bias)
    ospec = pl.BlockSpec((None, sub, hps * LANE), lambda hp, b, r: (b, 0, r * (npair // hps) + hp))
    return _pcall(
        body, name=name, grid=(npair // hps, nb, d), in_specs=in_specs, out_specs=[ospec, ospec],
        out_shape=[jax.ShapeDtypeStruct((nb, sub, d * ATT_OUT), F32)] * 2,
        compiler_params=_params("parallel", "parallel", "parallel"),
    )(qkv4, bias)


STAT_LSE_LANE = 16


def _attn_bwd(qkv4, bias, cotangent, d, name):
    nb, _, sub, _ = qkv4.shape
    nblk = sub // BAND
    scale = HEAD_DIM ** -0.5
    npair = ATT_OUT // LANE
    hps = _pairs_per_step(d)
    compact = d > 1

    def body(qkv_ref, bias_ref, *rest):
        do_ref, dqkv_ref, db_ref = rest[0], rest[-2], rest[-1]
        b, r = pl.program_id(1), pl.program_id(2)

        @pl.when((b == 0) & (r == 0))
        def _():
            db_ref[...] = jnp.zeros_like(db_ref)

        def blocks(ns, carry, with_prev):
            sides = (0, 1) if with_prev else (0,)
            chains = [(bi, i, h, sd) for bi in range(len(ns)) for i in range(hps) for h in range(2) for sd in sides]
            key_rows = lambda bi, sd: _rows(ns[bi] - sd)
            qs = {}
            for bi in range(len(ns)):
                for i in range(hps):
                    for h in range(2):
                        hl = slice(i * LANE + h * HEAD_DIM, i * LANE + (h + 1) * HEAD_DIM)
                        do = do_ref[_rows(ns[bi]), hl]
                        if compact:
                            st_ref, head = rest[1], 2 * i + h
                            ebar = st_ref[_rows(ns[bi]), head:head + 1]
                            lcol = st_ref[_rows(ns[bi]), STAT_LSE_LANE + head:STAT_LSE_LANE + head + 1]
                        else:
                            ebar = jnp.sum(do * rest[1][_rows(ns[bi]), hl], -1, keepdims=True)
                            lcol = rest[2][_rows(ns[bi]), i * LANE + h * HEAD_DIM:i * LANE + h * HEAD_DIM + 1]
                        q_scaled = qkv_ref[_rows(ns[bi]), _head_cols(i, h, 0)] * scale
                        qs[(bi, i, h)] = (q_scaled, do.astype(BF16), ebar, lcol)
            raw = []
            for bi, i, h, sd in chains:
                q, do16, _, _ = qs[(bi, i, h)]
                k = qkv_ref[key_rows(bi, sd), _head_cols(i, h, 1)]
                v = qkv_ref[key_rows(bi, sd), _head_cols(i, h, 2)]
                bias_blk = bias_ref[2 * i + h, :, :BAND] if sd else bias_ref[2 * i + h, :, BAND:]
                s = lax.dot_general(q, k, NT, preferred_element_type=F32) + bias_blk
                dp = lax.dot_general(do16, v, NT, preferred_element_type=F32)
                raw.append((s, dp))
            soft = []
            for (bi, i, h, sd), (s, dp) in zip(chains, raw):
                _, _, ebar, lcol = qs[(bi, i, h)]
                p = jnp.exp(s - lcol)
                ds = p * (dp - ebar)
                if sd:
                    db_ref[2 * i + h, :, :BAND] += ds
                else:
                    db_ref[2 * i + h, :, BAND:] += ds
                soft.append((p.astype(BF16), ds.astype(BF16)))
            grads = {}
            for (bi, i, h, sd), (p16, ds16) in zip(chains, soft):
                q, do16, _, _ = qs[(bi, i, h)]
                k = qkv_ref[key_rows(bi, sd), _head_cols(i, h, 1)]
                grads[(bi, i, h, sd)] = (
                    jnp.dot(ds16, k, preferred_element_type=F32),
                    lax.dot_general(ds16, q, TN, preferred_element_type=F32),
                    lax.dot_general(p16, do16, TN, preferred_element_type=F32))
            both = lambda bi, i, sd, which: jnp.concatenate(
                [grads[(bi, i, 0, sd)][which], grads[(bi, i, 1, sd)][which]], axis=1)
            carry = list(carry) if carry is not None else None
            for bi, n in enumerate(ns):
                for i in range(hps):
                    base = 3 * LANE * i
                    dq = both(bi, i, 0, 0)
                    if with_prev:
                        dq = dq + both(bi, i, 1, 0)
                        dqkv_ref[_rows(n - 1), base + LANE:base + 2 * LANE] = (
                            carry[2 * i] + both(bi, i, 1, 1)).astype(BF16)
                        dqkv_ref[_rows(n - 1), base + 2 * LANE:base + 3 * LANE] = (
                            carry[2 * i + 1] + both(bi, i, 1, 2)).astype(BF16)
                    dqkv_ref[_rows(n), base:base + LANE] = (dq * scale).astype(BF16)
                carry = [t for i in range(hps) for t in (both(bi, i, 0, 1), both(bi, i, 0, 2))]
            return tuple(carry)

        carry = _for_blocks(blocks, nblk, 2 if hps == 1 else 1, None)
        for i in range(hps):
            base = 3 * LANE * i
            dqkv_ref[_rows(nblk - 1), base + LANE:base + 2 * LANE] = carry[2 * i].astype(BF16)
            dqkv_ref[_rows(nblk - 1), base + 2 * LANE:base + 3 * LANE] = carry[2 * i + 1].astype(BF16)

    qspec = pl.BlockSpec((None, None, sub, 3 * LANE * hps), lambda hp, b, r: (b, r, 0, hp))
    bspec = pl.BlockSpec((2 * hps, BAND, 2 * BAND), lambda hp, b, r: (hp, 0, 0))
    if compact:
        cspecs = [pl.BlockSpec((None, None, sub, ATT_OUT), lambda hp, b, r: (b, r, 0, 0)),
                  pl.BlockSpec((None, None, sub, LANE), lambda hp, b, r: (b, r, 0, 0))]
    else:
        cspecs = [pl.BlockSpec((None, sub, hps * LANE), lambda hp, b, r: (b, 0, r * (npair // hps) + hp))] * 3
    return _pcall(
        body, name=name, grid=(npair // hps, nb, d),
        in_specs=[qspec, bspec] + cspecs, out_specs=[qspec, bspec],
        out_shape=[jax.ShapeDtypeStruct(qkv4.shape, BF16),
                   jax.ShapeDtypeStruct((GROUP_HEADS, BAND, 2 * BAND), F32)],
        compiler_params=_params("parallel", "arbitrary", "arbitrary"),
    )(qkv4, bias, *cotangent)


def _head_lanes(first_lane, one_channel):
    c = lax.broadcasted_iota(jnp.int32, (ATT_OUT, LANE), 0)
    lane = lax.broadcasted_iota(jnp.int32, (ATT_OUT, LANE), 1)
    hit = lane == first_lane + c // HEAD_DIM
    if one_channel:
        hit = hit & (c % HEAD_DIM == 0)
    return hit.astype(BF16)


def _exact_dot(v, m01, dims=None):
    parts = _split3(v)
    if dims is None:
        dot = lambda t: jnp.dot(t, m01, preferred_element_type=F32)
    else:
        dot = lambda t: lax.dot_general(t, m01, dims, preferred_element_type=F32)
    return (dot(parts[0]) + dot(parts[1])) + dot(parts[2])


def _combine_fwd(o0, l0, o1, l1, o2, l2, gatt):
    nb, seq, _ = gatt.shape
    tm = 512

    def body(o0_ref, l0_ref, o1_ref, l1_ref, o2_ref, l2_ref, g_ref, oa_ref, oatt_ref, lse_ref):
        spread = _head_lanes(0, False)
        l0v = l0_ref[...]
        l1v = _exact_dot(l1_ref[...], spread, NT)
        l2v = _exact_dot(l2_ref[...], spread, NT)
        m = jnp.maximum(jnp.maximum(l0v, l1v), l2v)
        tot = m + jnp.log(jnp.exp(l0v - m) + jnp.exp(l1v - m) + jnp.exp(l2v - m))
        o = (jnp.exp(l0v - tot) * o0_ref[...] + jnp.exp(l1v - tot) * o1_ref[...]
             + jnp.exp(l2v - tot) * o2_ref[...])
        g = g_ref[...]
        oa_ref[...] = (o * (g * _sigmoid(g))).astype(BF16)
        oatt_ref[...] = o
        lse_ref[...] = tot

    spec = pl.BlockSpec((None, tm, ATT_OUT), lambda b, i: (b, i, 0))
    lspec = pl.BlockSpec((None, tm, LANE), lambda b, i: (b, i, 0))
    return _pcall(
        body, name="attn_combine", grid=(nb, seq // tm),
        in_specs=[spec, spec, spec, lspec, spec, lspec, spec], out_specs=[spec] * 3,
        out_shape=[jax.ShapeDtypeStruct((nb, seq, ATT_OUT), BF16), jax.ShapeDtypeStruct((nb, seq, ATT_OUT), F32),
                   jax.ShapeDtypeStruct((nb, seq, ATT_OUT), F32)],
        compiler_params=_params("parallel", "parallel"),
    )(o0, l0, o1, l1, o2, l2, gatt)


def _combine_bwd(doa, gatt, o_att, lse):
    nb, seq, _ = gatt.shape
    tm = 512

    def body(doa_ref, g_ref, o_ref, l_ref, do_ref, do16_ref, st_ref, dg_ref):
        g = g_ref[...]
        sg = _sigmoid(g)
        do = doa_ref[...] * (g * sg)
        do_ref[...] = do
        do16_ref[...] = do.astype(BF16)
        st_ref[...] = (_exact_dot(do * o_ref[...], _head_lanes(0, False))
                       + _exact_dot(l_ref[...], _head_lanes(STAT_LSE_LANE, True)))
        dg_ref[...] = (doa_ref[...] * o_ref[...] * (sg * (1.0 + g * (1.0 - sg)))).astype(BF16)

    spec = pl.BlockSpec((None, tm, ATT_OUT), lambda b, i: (b, i, 0))
    lspec = pl.BlockSpec((None, tm, LANE), lambda b, i: (b, i, 0))
    return _pcall(
        body, name="attn_combine_bwd", grid=(nb, seq // tm), in_specs=[spec] * 4,
        out_specs=[spec, spec, lspec, spec],
        out_shape=[jax.ShapeDtypeStruct((nb, seq, ATT_OUT), F32), jax.ShapeDtypeStruct((nb, seq, ATT_OUT), BF16),
                   jax.ShapeDtypeStruct((nb, seq, LANE), F32), jax.ShapeDtypeStruct((nb, seq, ATT_OUT), BF16)],
        compiler_params=_params("parallel", "parallel"),
    )(doa, gatt, o_att, lse)


CONV_TM = 512
CONV_TC = 512


def _shift_down(cur, halo, k):
    rolled = pltpu.roll(cur, k, 0)
    hro = pltpu.roll(halo, k, 0)
    row = lax.broadcasted_iota(jnp.int32, hro.shape, 0)
    return jnp.concatenate([jnp.where(row < k, hro, rolled[:8]), rolled[8:]], axis=0)


def _shift_up(cur, halo, k):
    n = cur.shape[0]
    rolled = pltpu.roll(cur, n - k, 0)
    hro = pltpu.roll(halo, 8 - k, 0)
    row = lax.broadcasted_iota(jnp.int32, hro.shape, 0)
    return jnp.concatenate([rolled[:n - 8], jnp.where(row >= 8 - k, hro, rolled[n - 8:])], axis=0)


def _conv_pre(cur, halo, w_ref, b_ref):
    acc = cur * w_ref[3:4, :] + b_ref[...]
    for k in range(1, 4):
        acc = acc + _shift_down(cur, halo, k) * w_ref[3 - k:4 - k, :]
    return acc


def _conv_specs(seq):
    nblk = seq // CONV_TM
    cur = pl.BlockSpec((None, CONV_TM, CONV_TC), lambda cb, b, i: (b, i, cb))
    prev = pl.BlockSpec((None, 8, CONV_TC), lambda cb, b, i: (b, jnp.maximum(i * (CONV_TM // 8) - 1, 0), cb))
    nxt = pl.BlockSpec((None, 8, CONV_TC),
                       lambda cb, b, i: (b, jnp.minimum((i + 1) * (CONV_TM // 8), seq // 8 - 1), cb))
    wspec = pl.BlockSpec((4, CONV_TC), lambda cb, b, i: (0, cb))
    bspec = pl.BlockSpec((1, CONV_TC), lambda cb, b, i: (0, cb))
    return nblk, cur, prev, nxt, wspec, bspec


def _conv_fwd(xin, w4, bias, name):
    nb, seq, ch = xin.shape
    _, cur, prev, _, wspec, bspec = _conv_specs(seq)

    def body(x_ref, h_ref, w_ref, b_ref, o_ref):
        halo = jnp.where(pl.program_id(2) > 0, h_ref[...], 0.0)
        pre = _conv_pre(x_ref[...], halo, w_ref, b_ref)
        o_ref[...] = pre * _sigmoid(pre)

    return _pcall(
        body, name=name, grid=(ch // CONV_TC, nb, seq // CONV_TM),
        in_specs=[cur, prev, wspec, bspec], out_specs=cur,
        out_shape=jax.ShapeDtypeStruct(xin.shape, F32),
        compiler_params=_params("parallel", "parallel", "parallel"),
    )(xin, xin, w4, bias)


def _conv_bwd_pre(dact, xin, w4, bias, name):
    nb, seq, ch = xin.shape
    _, cur, prev, _, wspec, bspec = _conv_specs(seq)

    def body(da_ref, x_ref, h_ref, w_ref, b_ref, dp_ref, s_ref):
        b, i = pl.program_id(1), pl.program_id(2)

        @pl.when((b == 0) & (i == 0))
        def _():
            s_ref[...] = jnp.zeros_like(s_ref)

        halo = jnp.where(i > 0, h_ref[...], 0.0)
        x = x_ref[...]
        pre = _conv_pre(x, halo, w_ref, b_ref)
        sg = _sigmoid(pre)
        dpre = da_ref[...] * (sg * (1.0 + pre * (1.0 - sg)))
        dp_ref[...] = dpre
        s_ref[3:4, :] += jnp.sum(dpre * x, 0, keepdims=True)
        for k in range(1, 4):
            s_ref[3 - k:4 - k, :] += jnp.sum(dpre * _shift_down(x, halo, k), 0, keepdims=True)
        s_ref[4:5, :] += jnp.sum(dpre, 0, keepdims=True)

    return _pcall(
        body, name=name, grid=(ch // CONV_TC, nb, seq // CONV_TM),
        in_specs=[cur, cur, prev, wspec, bspec],
        out_specs=[cur, pl.BlockSpec((8, CONV_TC), lambda cb, b, i: (0, cb))],
        out_shape=[jax.ShapeDtypeStruct(xin.shape, F32), jax.ShapeDtypeStruct((8, ch), F32)],
        compiler_params=_params("parallel", "arbitrary", "arbitrary"),
    )(dact, xin, xin, w4, bias)


def _conv_bwd_x(dpre, w4, name):
    nb, seq, ch = dpre.shape
    nblk, cur, _, nxt, wspec, _ = _conv_specs(seq)

    def body(d_ref, n_ref, w_ref, o_ref):
        halo = jnp.where(pl.program_id(2) < nblk - 1, n_ref[...], 0.0)
        cur_v = d_ref[...]
        acc = cur_v * w_ref[3:4, :]
        for j in range(1, 4):
            acc = acc + _shift_up(cur_v, halo, j) * w_ref[3 - j:4 - j, :]
        o_ref[...] = acc.astype(BF16)

    return _pcall(
        body, name=name, grid=(ch // CONV_TC, nb, seq // CONV_TM),
        in_specs=[cur, nxt, wspec], out_specs=cur,
        out_shape=jax.ShapeDtypeStruct(dpre.shape, BF16),
        compiler_params=_params("parallel", "parallel", "parallel"),
    )(dpre, dpre, w4)


def _softplus_sig(dt_raw, dt_bias_row):
    nb, seq, _ = dt_raw.shape
    tm = 512

    def body(r_ref, b_ref, sp_ref, sg_ref):
        v = r_ref[...] + b_ref[...]
        sp_ref[...] = jnp.maximum(v, 0.0) + jnp.log1p(jnp.exp(-jnp.abs(v)))
        sg_ref[...] = _sigmoid(v)

    spec = pl.BlockSpec((None, tm, LANE), lambda b, i: (b, i, 0))
    return _pcall(
        body, name="dt_softplus", grid=(nb, seq // tm),
        in_specs=[spec, pl.BlockSpec((1, LANE), lambda b, i: (0, 0))], out_specs=[spec, spec],
        out_shape=[jax.ShapeDtypeStruct(dt_raw.shape, F32)] * 2,
        compiler_params=_params("parallel", "parallel"),
    )(dt_raw, dt_bias_row)


def _group_lanes(t):
    pads = [(0, 0)] * (t.ndim - 1) + [(0, LANE - GROUP_SSM_HEADS)]
    return jnp.stack([jnp.pad(t[..., GROUP_SSM_HEADS * g:GROUP_SSM_HEADS * (g + 1)], pads) for g in range(SSM_GROUPS)])


def _ungroup_lanes(t):
    return jnp.concatenate([t[g][..., :GROUP_SSM_HEADS] for g in range(SSM_GROUPS)], axis=-1)


def _decays(dt, al_ref):
    row = lax.broadcasted_iota(jnp.int32, (CHUNK, CHUNK), 0)
    col = lax.broadcasted_iota(jnp.int32, (CHUNK, CHUNK), 1)
    tril = (row >= col).astype(F32)
    triu = (row <= col).astype(F32)
    arow = -jnp.exp(al_ref[...])
    a = dt * arow
    acs = jnp.dot(tril, a, precision=HIGHEST, preferred_element_type=F32)
    acs_t = lax.dot_general(a, triu, (((0,), (0,)), ((), ())), precision=HIGHEST, preferred_element_type=F32)
    return arow, acs, acs_t, row >= col, triu


def _ssd_specs(nb, seq):
    nc = seq // CHUNK
    hw = GROUP_SSM_HEADS * HEAD_DIM

    def mk(rev):
        cidx = (lambda c: nc - 1 - c) if rev else (lambda c: c)
        wide = pl.BlockSpec((None, CHUNK, hw), lambda g, b, c: (b, cidx(c), g))
        state = pl.BlockSpec((None, CHUNK, D_STATE), lambda g, b, c: (b, cidx(c), g))
        lanes = pl.BlockSpec((None, None, CHUNK, LANE), lambda g, b, c: (g, b, cidx(c), 0))
        prev = pl.BlockSpec((None, None, None, D_STATE, hw), lambda g, b, c: (b, cidx(c), g, 0, 0))
        return wide, state, lanes, prev

    grow = pl.BlockSpec((None, 1, LANE), lambda g, b, c: (g, 0, 0))
    nwspec = pl.BlockSpec((1, hw), lambda g, b, c: (0, g))
    return nc, hw, mk, grow, nwspec


def _head_expand():
    hw = GROUP_SSM_HEADS * HEAD_DIM
    r = lax.broadcasted_iota(jnp.int32, (LANE, hw), 0)
    c = lax.broadcasted_iota(jnp.int32, (LANE, hw), 1)
    return ((c // HEAD_DIM) == r).astype(BF16)


def _split3(v):
    hi = v.astype(BF16)
    rest = v - hi.astype(F32)
    mid = rest.astype(BF16)
    return hi, mid, (rest - mid.astype(F32)).astype(BF16)


def _to_channels(v, e):
    hi, mid, lo = _split3(v)
    dot = lambda t: jnp.dot(t, e, preferred_element_type=F32)
    return (dot(hi) + dot(mid)) + dot(lo)


def _to_heads(w, e):
    hi, mid, lo = _split3(w)
    dot = lambda t: lax.dot_general(t, e, (((1,), (1,)), ((), ())), preferred_element_type=F32)
    return (dot(hi) + dot(mid)) + dot(lo)


def _row8(v):
    return jnp.broadcast_to(v, (8, v.shape[1]))


def _ssd_chunk_setup(dt, al_ref, ds_ref):
    arow, acs, acs_t, causal, triu = _decays(dt, al_ref)
    e = _head_expand()
    dtx = _to_channels(dt, e)
    acsx = _to_channels(acs, e)
    lastx = acsx[CHUNK - 1:CHUNK, :]
    dskx = _to_channels(_row8(ds_ref[...]), e)[0:1, :]
    return arow, acs, acs_t, causal, triu, e, dtx, acsx, lastx, dskx


def _ssd_fwd(xs, bm, cm, dtg, z, alog_g, dskip_g, normw):
    nb, seq, _ = xs.shape
    nc, hw, mk, grow, nwspec = _ssd_specs(nb, seq)
    wide, state, lanes, prev = mk(False)
    tn = (((0,), (0,)), ((), ()))

    def body(xs_ref, b_ref, c_ref, dt_ref, z_ref, al_ref, ds_ref, nw_ref, ys_ref, y_ref, sp_ref, st_ref):
        @pl.when(pl.program_id(2) == 0)
        def _():
            st_ref[...] = jnp.zeros_like(st_ref)

        dt = dt_ref[...]
        _, acs, acs_t, causal, _, _, dtx, acsx, lastx, dskx = _ssd_chunk_setup(dt, al_ref, ds_ref)
        bmat = b_ref[...].astype(BF16)
        cmat = c_ref[...].astype(BF16)
        cb = lax.dot_general(cmat, bmat, (((1,), (1,)), ((), ())), preferred_element_type=F32)
        x = xs_ref[...]
        xdt = x * dtx
        xdt16 = xdt.astype(BF16)
        first_head = lax.broadcasted_iota(jnp.int32, (CHUNK, LANE), 1) < HEAD_DIM
        pairs = []
        for hp in range(GROUP_SSM_HEADS // 2):
            xp = xdt16[:, hp * LANE:(hp + 1) * LANE]
            two = []
            for j in (2 * hp, 2 * hp + 1):
                lmat = jnp.exp(jnp.where(causal, acs[:, j:j + 1] - acs_t[j:j + 1, :], -jnp.inf))
                two.append(jnp.dot((cb * lmat).astype(BF16), xp, preferred_element_type=F32))
            pairs.append(jnp.where(first_head, two[0], two[1]))
        yd = jnp.concatenate(pairs, axis=1)
        s_prev = st_ref[...]
        s16 = s_prev.astype(BF16)
        sp_ref[...] = s16
        yo = jnp.dot(cmat, s16, preferred_element_type=F32) * jnp.exp(acsx)
        sts = lax.dot_general(bmat, (xdt * jnp.exp(lastx - acsx)).astype(BF16), tn, preferred_element_type=F32)
        st_ref[...] = s_prev * jnp.exp(lastx) + sts
        y = yd + yo + dskx * x
        zz = z_ref[...]
        u = y * (zz * _sigmoid(zz))
        rn = lax.rsqrt(jnp.mean(u * u, -1, keepdims=True) + RMS_EPS)
        ys_ref[...] = (u * rn * nw_ref[...]).astype(BF16)
        y_ref[...] = y

    return _pcall(
        body, name="ssd_fwd", grid=(SSM_GROUPS, nb, nc),
        in_specs=[wide, state, state, lanes, wide, grow, grow, nwspec],
        out_specs=[wide, wide, prev],
        out_shape=[jax.ShapeDtypeStruct((nb, seq, D_INNER), BF16), jax.ShapeDtypeStruct((nb, seq, D_INNER), F32),
                   jax.ShapeDtypeStruct((nb, nc, SSM_GROUPS, D_STATE, hw), BF16)],
        scratch_shapes=[pltpu.VMEM((D_STATE, hw), F32)],
        compiler_params=_params("parallel", "parallel", "arbitrary"),
    )(xs, bm, cm, dtg, z, alog_g, dskip_g, normw)


def _ssd_bwd(xs, bm, cm, dtg, sgg, z, y, dys, sprev, alog_g, dskip_g, normw):
    nb, seq, _ = xs.shape
    nc, hw, mk, grow, nwspec = _ssd_specs(nb, seq)
    wide, state, lanes, prev = mk(True)
    nt = (((1,), (1,)), ((), ()))
    tn = (((0,), (0,)), ((), ()))

    def body(xs_ref, b_ref, c_ref, dt_ref, sg_ref, z_ref, y_ref, dys_ref, sp_ref, al_ref, ds_ref, nw_ref,
             dxs_ref, db_ref, dc_ref, ddt_ref, dz_ref, small_ref, dnw_ref, g_ref):
        b, c = pl.program_id(1), pl.program_id(2)

        @pl.when((b == 0) & (c == 0))
        def _():
            small_ref[...] = jnp.zeros_like(small_ref)
            dnw_ref[...] = jnp.zeros_like(dnw_ref)

        @pl.when(c == 0)
        def _():
            g_ref[...] = jnp.zeros_like(g_ref)

        yv, zz, dys_v, nw = y_ref[...], z_ref[...], dys_ref[...], nw_ref[...]
        sz = _sigmoid(zz)
        silu = zz * sz
        u = yv * silu
        rn = lax.rsqrt(jnp.mean(u * u, -1, keepdims=True) + RMS_EPS)
        gn = dys_v * nw
        du = rn * gn - u * (rn * rn * rn) * jnp.mean(u * gn, -1, keepdims=True)
        dnw_ref[...] += jnp.sum(dys_v * u * rn, 0, keepdims=True)
        dy = du * silu
        dz_ref[...] = du * yv * (sz * (1.0 + zz * (1.0 - sz)))

        dt = dt_ref[...]
        arow, acs, acs_t, causal, triu, e, dtx, acsx, lastx, dskx = _ssd_chunk_setup(dt, al_ref, ds_ref)
        dfsx = jnp.exp(acsx)
        dtex = jnp.exp(lastx - acsx)
        bmat = b_ref[...].astype(BF16)
        cmat = c_ref[...].astype(BF16)
        cb = lax.dot_general(cmat, bmat, nt, preferred_element_type=F32)
        x = xs_ref[...]
        xdt = x * dtx
        xdt16 = xdt.astype(BF16)
        xdte = xdt * dtex
        dy16 = dy.astype(BF16)
        dyd = dy * dfsx
        dyd16 = dyd.astype(BF16)
        s16 = sp_ref[...]
        g = g_ref[...]
        g16 = g.astype(BF16)
        cs = jnp.dot(cmat, s16, preferred_element_type=F32)
        dc_off = lax.dot_general(dyd16, s16, nt, preferred_element_type=F32)
        g_here = lax.dot_general(cmat, dyd16, tn, preferred_element_type=F32)
        bg = jnp.dot(bmat, g16, preferred_element_type=F32)
        db_st = lax.dot_general(xdte.astype(BF16), g16, nt, preferred_element_type=F32)
        ddte_w = bg * xdte
        dcd = _to_heads(_row8(jnp.sum(g * s16.astype(F32), 0, keepdims=True)), e)[0:1, :]
        lane = lax.broadcasted_iota(jnp.int32, (CHUNK, LANE), 1)
        first_head = lane < HEAD_DIM
        sub = lax.broadcasted_iota(jnp.int32, (CHUNK, LANE), 0)
        dacs = jnp.zeros((CHUNK, LANE), F32)
        colsums = jnp.zeros((CHUNK, LANE), F32)
        dcb = jnp.zeros((CHUNK, CHUNK), F32)
        pairs = []
        for hp in range(GROUP_SSM_HEADS // 2):
            xp = xdt16[:, hp * LANE:(hp + 1) * LANE]
            dyp = dy16[:, hp * LANE:(hp + 1) * LANE]
            two = []
            for idx, j in enumerate((2 * hp, 2 * hp + 1)):
                lmat = jnp.exp(jnp.where(causal, acs[:, j:j + 1] - acs_t[j:j + 1, :], -jnp.inf))
                mf = cb * lmat
                dy_h = jnp.where(first_head if idx == 0 else jnp.logical_not(first_head), dyp, jnp.zeros_like(dyp))
                dm = lax.dot_general(dy_h, xp, nt, preferred_element_type=F32)
                two.append(lax.dot_general(mf.astype(BF16), dyp, tn, preferred_element_type=F32))
                wmat = dm * mf
                dcb = dcb + dm * lmat
                dacs = jnp.where(lane == j, jnp.sum(wmat, -1, keepdims=True), dacs)
                colsums = jnp.where(sub == j, jnp.sum(wmat, 0, keepdims=True), colsums)
            pairs.append(jnp.where(first_head, two[0], two[1]))
        dxdt = bg * dtex + jnp.concatenate(pairs, axis=1)
        dacs = dacs - colsums.T + _to_heads(dyd * cs - ddte_w, e)
        cd_row = jnp.exp(acs[CHUNK - 1:CHUNK, :])
        tail = _to_heads(_row8(jnp.sum(ddte_w, 0, keepdims=True)), e)[0:1, :] + dcd * cd_row
        dacs = dacs + jnp.where(sub == CHUNK - 1, tail, 0.0)
        da = jnp.dot(triu, dacs, precision=HIGHEST, preferred_element_type=F32)
        ddt_raw = (da * arow + _to_heads(dxdt * x, e)) * sg_ref[...]
        ddt_ref[...] = ddt_raw
        small_ref[0:1, :] += jnp.sum(da * dt, 0, keepdims=True) * arow
        small_ref[1:2, :] += _to_heads(_row8(jnp.sum(dy * x, 0, keepdims=True)), e)[0:1, :]
        small_ref[2:3, :] += jnp.sum(ddt_raw, 0, keepdims=True)
        dcb16 = dcb.astype(BF16)
        dc_ref[...] = dc_off + jnp.dot(dcb16, bmat, preferred_element_type=F32)
        db_ref[...] = db_st + lax.dot_general(dcb16, cmat, tn, preferred_element_type=F32)
        dxs_ref[...] = dxdt * dtx + dskx * dy
        g_ref[...] = g * jnp.exp(lastx) + g_here

    return _pcall(
        body, name="ssd_bwd", grid=(SSM_GROUPS, nb, nc),
        in_specs=[wide, state, state, lanes, lanes, wide, wide, wide, prev, grow, grow, nwspec],
        out_specs=[wide, state, state, lanes, wide,
                   pl.BlockSpec((None, 8, LANE), lambda g, b, c: (g, 0, 0)), nwspec],
        out_shape=[jax.ShapeDtypeStruct((nb, seq, D_INNER), F32),
                   jax.ShapeDtypeStruct((nb, seq, SSM_GROUPS * D_STATE), F32),
                   jax.ShapeDtypeStruct((nb, seq, SSM_GROUPS * D_STATE), F32),
                   jax.ShapeDtypeStruct((SSM_GROUPS, nb, seq, LANE), F32),
                   jax.ShapeDtypeStruct((nb, seq, D_INNER), F32),
                   jax.ShapeDtypeStruct((SSM_GROUPS, 8, LANE), F32),
                   jax.ShapeDtypeStruct((1, D_INNER), F32)],
        scratch_shapes=[pltpu.VMEM((D_STATE, hw), F32)],
        compiler_params=_params("parallel", "arbitrary", "arbitrary"),
    )(xs, bm, cm, dtg, sgg, z, y, dys, sprev, alog_g, dskip_g, normw)


EW_TM = 256


def _merge_fwd(y_a, y_b, gm, bgate):
    nb, seq, _ = y_a.shape

    def body(a_ref, b_ref, ga_ref, gb_ref, bg_ref, o_ref):
        sa = _sigmoid(ga_ref[...] + bg_ref[0:1, :])
        sb = _sigmoid(gb_ref[...] + bg_ref[1:2, :])
        o_ref[...] = (sa * a_ref[...] + sb * b_ref[...]).astype(BF16)

    spec = pl.BlockSpec((None, EW_TM, D_MODEL), lambda b, i: (b, i, 0))
    spec1 = pl.BlockSpec((None, EW_TM, D_MODEL), lambda b, i: (b, i, 1))
    return _pcall(
        body, name="merge_fwd", grid=(nb, seq // EW_TM),
        in_specs=[spec, spec, spec, spec1, pl.BlockSpec((8, D_MODEL), lambda b, i: (0, 0))], out_specs=spec,
        out_shape=jax.ShapeDtypeStruct((nb, seq, D_MODEL), BF16),
        compiler_params=_params("parallel", "parallel"),
    )(y_a, y_b, gm, gm, bgate)


def _merge_bwd(dmerged, y_a, y_b, gm, bgate):
    nb, seq, _ = y_a.shape

    def body(dm_ref, a_ref, b_ref, ga_ref, gb_ref, bg_ref, dya_ref, dyb_ref, dg_ref, s_ref):
        @pl.when((pl.program_id(0) == 0) & (pl.program_id(1) == 0))
        def _():
            s_ref[...] = jnp.zeros_like(s_ref)

        dm = dm_ref[...]
        sa = _sigmoid(ga_ref[...] + bg_ref[0:1, :])
        sb = _sigmoid(gb_ref[...] + bg_ref[1:2, :])
        dya_ref[...] = (dm * sa).astype(BF16)
        dyb_ref[...] = (dm * sb).astype(BF16)
        dga = dm * a_ref[...] * (sa * (1.0 - sa))
        dgb = dm * b_ref[...] * (sb * (1.0 - sb))
        dg_ref[:, :D_MODEL] = dga.astype(BF16)
        dg_ref[:, D_MODEL:] = dgb.astype(BF16)
        s_ref[0:1, :] += jnp.sum(dga, 0, keepdims=True)
        s_ref[1:2, :] += jnp.sum(dgb, 0, keepdims=True)

    spec = pl.BlockSpec((None, EW_TM, D_MODEL), lambda b, i: (b, i, 0))
    spec1 = pl.BlockSpec((None, EW_TM, D_MODEL), lambda b, i: (b, i, 1))
    small = pl.BlockSpec((8, D_MODEL), lambda b, i: (0, 0))
    return _pcall(
        body, name="merge_bwd", grid=(nb, seq // EW_TM),
        in_specs=[spec, spec, spec, spec, spec1, small],
        out_specs=[spec, spec, pl.BlockSpec((None, EW_TM, 2 * D_MODEL), lambda b, i: (b, i, 0)), small],
        out_shape=[jax.ShapeDtypeStruct((nb, seq, D_MODEL), BF16), jax.ShapeDtypeStruct((nb, seq, D_MODEL), BF16),
                   jax.ShapeDtypeStruct((nb, seq, 2 * D_MODEL), BF16), jax.ShapeDtypeStruct((8, D_MODEL), F32)],
        compiler_params=_params("arbitrary", "arbitrary"),
    )(dmerged, y_a, y_b, gm, gm, bgate)


def _ln_loss(x, mix, gp, pw, target, bgate, ln_g, ln_b):
    nb, seq, _ = x.shape

    def body(x_ref, mix_ref, gp_ref, pw_ref, t_ref, bg_ref, g_ref, b_ref, dx_ref, dp_ref, dpw_ref, dgp_ref, s_ref):
        @pl.when((pl.program_id(0) == 0) & (pl.program_id(1) == 0))
        def _():
            s_ref[...] = jnp.zeros_like(s_ref)

        sp = _sigmoid(gp_ref[...] + bg_ref[2:3, :])
        pw = pw_ref[...]
        pre = ALPHA * x_ref[...] + mix_ref[...] + sp * pw
        mu = jnp.mean(pre, -1, keepdims=True)
        cen = pre - mu
        rstd = lax.rsqrt(jnp.mean(cen * cen, -1, keepdims=True) + LN_EPS)
        xhat = cen * rstd
        err = xhat * g_ref[...] + b_ref[...] - t_ref[...]
        dy = err * (1.0 / D_MODEL)
        dxh = dy * g_ref[...]
        dpre = rstd * (dxh - jnp.mean(dxh, -1, keepdims=True) - xhat * jnp.mean(dxh * xhat, -1, keepdims=True))
        dx_ref[...] = ALPHA * dpre
        dp_ref[...] = dpre.astype(BF16)
        dpw_ref[...] = (dpre * sp).astype(BF16)
        dgp = dpre * pw * (sp * (1.0 - sp))
        dgp_ref[...] = dgp.astype(BF16)
        s_ref[0:1, :] += jnp.sum(dy * xhat, 0, keepdims=True)
        s_ref[1:2, :] += jnp.sum(dy, 0, keepdims=True)
        s_ref[2:3, :] += jnp.sum(dgp, 0, keepdims=True)
        s_ref[3:4, :] += jnp.sum(err * err, 0, keepdims=True)

    spec = pl.BlockSpec((None, EW_TM, D_MODEL), lambda b, i: (b, i, 0))
    small = pl.BlockSpec((8, D_MODEL), lambda b, i: (0, 0))
    row = pl.BlockSpec((1, D_MODEL), lambda b, i: (0, 0))
    return _pcall(
        body, name="ln_loss", grid=(nb, seq // EW_TM),
        in_specs=[spec] * 5 + [small, row, row], out_specs=[spec] * 4 + [small],
        out_shape=[jax.ShapeDtypeStruct((nb, seq, D_MODEL), F32)] + [jax.ShapeDtypeStruct((nb, seq, D_MODEL), BF16)] * 3
        + [jax.ShapeDtypeStruct((8, D_MODEL), F32)],
        compiler_params=_params("arbitrary", "arbitrary"),
    )(x, mix, gp, pw, target, bgate, ln_g, ln_b)


def _adamw(w, g, m, v, name):
    rows, cols = w.shape
    tr = _row_tile(rows, cols, 8, 5 << 19)
    c1 = 1.0 - ADAM_B1 ** ADAM_STEP
    c2 = 1.0 - ADAM_B2 ** ADAM_STEP

    def body(w_ref, g_ref, m_ref, v_ref, d_ref, nm_ref, nv_ref):
        gv = g_ref[...]
        nm = ADAM_B1 * m_ref[...] + (1.0 - ADAM_B1) * gv
        nv = ADAM_B2 * v_ref[...] + (1.0 - ADAM_B2) * (gv * gv)
        d_ref[...] = -ADAM_LR * ((nm / c1) / (jnp.sqrt(nv / c2) + ADAM_EPS) + ADAM_WD * w_ref[...])
        nm_ref[...] = nm
        nv_ref[...] = nv

    spec = pl.BlockSpec((tr, cols), lambda i: (i, 0))
    return _pcall(
        body, name=name, grid=(rows // tr,), in_specs=[spec] * 4, out_specs=[spec] * 3,
        out_shape=[jax.ShapeDtypeStruct(w.shape, F32)] * 3, compiler_params=_params("parallel"),
    )(w, g, m, v)


def _sum_rows(parts, out_dtype, name):
    rows, cols = parts[0].shape
    tr = rows
    for cand in range(16, rows, 16):
        if rows % cand == 0 and cand * cols * 4 <= (1 << 20):
            tr = cand
    n = len(parts)

    def body(*refs):
        acc = refs[0][...].astype(F32)
        for r in refs[1:n]:
            acc = acc + r[...].astype(F32)
        refs[n][...] = acc.astype(out_dtype)

    spec = pl.BlockSpec((tr, cols), lambda i: (i, 0))
    return _pcall(
        body, name=name, grid=(rows // tr,), in_specs=[spec] * n, out_specs=spec,
        out_shape=jax.ShapeDtypeStruct((rows, cols), out_dtype), compiler_params=_params("parallel"),
    )(*parts)


def _place():
    return lax.axis_index("x"), lax.axis_index("y"), lax.axis_index("c")


def _other_chips(x, y):
    return [(1 - x, y), (x, 1 - y), (1 - x, 1 - y)]


def _remote(src, dst, send_sem, recv_sem, to):
    return pltpu.make_async_remote_copy(src_ref=src, dst_ref=dst, send_sem=send_sem, recv_sem=recv_sem,
                                        device_id=to, device_id_type=MESH)


ANY = pl.BlockSpec(memory_space=pl.ANY)
DMA_CHUNK_BYTES = 512 * 1024


def _row_chunks(rows, row_bytes):
    per = max(16, DMA_CHUNK_BYTES // row_bytes // 16 * 16)
    return [(s, min(per, rows - s)) for s in range(0, rows, per)]


def _row_tile(rows, cols, align, limit=1 << 21):
    best = None
    for cand in range(align, rows + 1, align):
        if rows % cand == 0 and cand * cols * 4 <= limit:
            best = cand
    return best or rows


def _allgather_pieces(pieces):
    n = len(pieces)
    halves = [_row_chunks(p.shape[0] // 2, p.shape[1] * p.dtype.itemsize) for p in pieces]
    wholes = [_row_chunks(p.shape[0], p.shape[1] * p.dtype.itemsize) for p in pieces]
    n_ici = 3 * sum(len(h) for h in halves)
    n_loc = sum(len(w) for w in wholes)

    def body(*refs):
        ins, outs = refs[:n], refs[n:2 * n]
        send_sems, recv_sems, local_sems = refs[2 * n:]
        x, y, c = _place()
        me = 2 * x + y
        sibling = (x, y, 1 - c)
        chips = _other_chips(x, y)
        locals_ = []
        for a in range(n):
            for s, m in wholes[a]:
                loc = pltpu.make_async_copy(ins[a].at[pl.ds(s, m)], outs[a].at[me, pl.ds(s, m)],
                                            local_sems.at[len(locals_)])
                loc.start()
                locals_.append(loc)
        ici = []
        for a in range(n):
            half = ins[a].shape[0] // 2
            for s, m in halves[a]:
                for j, (cx, cy) in enumerate(chips):
                    k = len(ici)
                    cp = _remote(ins[a].at[pl.ds(c * half + s, m)], outs[a].at[me, pl.ds(c * half + s, m)],
                                 send_sems.at[k], recv_sems.at[k], (cx, cy, c))
                    cp.start()
                    ici.append((a, s, m, j, cp))
        passed = []
        for k, (a, s, m, j, _) in enumerate(ici):
            half = ins[a].shape[0] // 2
            cx, cy = chips[j]
            blk = outs[a].at[2 * cx + cy, pl.ds(c * half + s, m)]
            _remote(blk, blk, send_sems.at[k], recv_sems.at[k], (cx, cy, c)).wait_recv()
            fw = _remote(blk, blk, send_sems.at[n_ici + k], recv_sems.at[n_ici + k], sibling)
            fw.start()
            passed.append(fw)
        for k, (a, s, m, j, _) in enumerate(ici):
            half = ins[a].shape[0] // 2
            cx, cy = chips[j]
            blk = outs[a].at[2 * cx + cy, pl.ds((1 - c) * half + s, m)]
            _remote(blk, blk, send_sems.at[n_ici + k], recv_sems.at[n_ici + k], sibling).wait_recv()
        for item in ici:
            item[4].wait_send()
        for fw in passed:
            fw.wait_send()
        for loc in locals_:
            loc.wait()

    return _pcall(
        body, name="allgather_weights", in_specs=[ANY] * n, out_specs=[ANY] * n,
        out_shape=[jax.ShapeDtypeStruct((4,) + p.shape, p.dtype) for p in pieces],
        scratch_shapes=[pltpu.SemaphoreType.DMA((2 * n_ici,)), pltpu.SemaphoreType.DMA((2 * n_ici,)),
                        pltpu.SemaphoreType.DMA((n_loc,))],
        compiler_params=pltpu.CompilerParams(has_side_effects=True),
    )(*pieces)


def _sibling_exchange(grads):
    n = len(grads)
    chunks = [_row_chunks(g.shape[1] // 2, g.shape[2] * g.dtype.itemsize) for g in grads]
    n_sem = 4 * sum(len(ch) for ch in chunks)

    def body(*refs):
        ins, gots = refs[:n], refs[n:2 * n]
        send_sems, recv_sems = refs[2 * n:]
        x, y, c = _place()
        sibling = (x, y, 1 - c)
        work = []
        for a in range(n):
            half = ins[a].shape[1] // 2
            for piece in range(4):
                for s, m in chunks[a]:
                    k = len(work)
                    cp = _remote(ins[a].at[piece, pl.ds((1 - c) * half + s, m)], gots[a].at[piece, pl.ds(s, m)],
                                 send_sems.at[k], recv_sems.at[k], sibling)
                    cp.start()
                    work.append(cp)
        for cp in work:
            cp.wait()

    return _pcall(
        body, name="grad_sibling_exchange", in_specs=[ANY] * n, out_specs=[ANY] * n,
        out_shape=[jax.ShapeDtypeStruct((4, g.shape[1] // 2, g.shape[2]), g.dtype) for g in grads],
        scratch_shapes=[pltpu.SemaphoreType.DMA((n_sem,)), pltpu.SemaphoreType.DMA((n_sem,))],
        compiler_params=pltpu.CompilerParams(has_side_effects=True),
    )(*grads)


def _sibling_gather(fulls):
    n = len(fulls)
    chunks = [_row_chunks(f.shape[0] // 2, f.shape[1] * f.dtype.itemsize) for f in fulls]
    n_sem = sum(len(ch) for ch in chunks)

    def body(*refs):
        outs = refs[n:2 * n]
        send_sems, recv_sems = refs[2 * n:]
        x, y, c = _place()
        sibling = (x, y, 1 - c)
        work = []
        for a in range(n):
            h = outs[a].shape[0] // 2
            for s, m in chunks[a]:
                k = len(work)
                mine = outs[a].at[pl.ds(c * h + s, m)]
                cp = _remote(mine, mine, send_sems.at[k], recv_sems.at[k], sibling)
                cp.start()
                work.append((a, s, m, cp))
        for k, (a, s, m, cp) in enumerate(work):
            h = outs[a].shape[0] // 2
            cp.wait_send()
            theirs = outs[a].at[pl.ds((1 - c) * h + s, m)]
            _remote(theirs, theirs, send_sems.at[k], recv_sems.at[k], sibling).wait_recv()

    return _pcall(
        body, name="grad_sibling_gather", in_specs=[ANY] * n, out_specs=[ANY] * n,
        out_shape=[jax.ShapeDtypeStruct(f.shape, f.dtype) for f in fulls],
        input_output_aliases={a: a for a in range(n)},
        scratch_shapes=[pltpu.SemaphoreType.DMA((n_sem,)), pltpu.SemaphoreType.DMA((n_sem,))],
        compiler_params=pltpu.CompilerParams(has_side_effects=True),
    )(*fulls)


def _pair_sum(grad, got, place, name):
    _, rows, cols = grad.shape
    half = rows // 2
    tr = _row_tile(half, cols, 16)

    def body(p_ref, a_ref, b_ref, o_ref):
        o_ref[...] = (a_ref[...].astype(F32) + b_ref[...].astype(F32)).astype(BF16)

    return _pcall(
        body, name=name,
        grid_spec=pltpu.PrefetchScalarGridSpec(
            num_scalar_prefetch=1, grid=(4, half // tr),
            in_specs=[pl.BlockSpec((None, tr, cols), lambda k, i, p: (k, p[1] * (half // tr) + i, 0)),
                      pl.BlockSpec((None, tr, cols), lambda k, i, p: (k, i, 0))],
            out_specs=pl.BlockSpec((None, tr, cols), lambda k, i, p: (k, i, 0))),
        out_shape=jax.ShapeDtypeStruct((4, half, cols), BF16),
        compiler_params=_params("parallel", "parallel"),
    )(place, grad, got)


def _chip_sum(sums, got, place, name):
    _, h, cols = sums.shape
    tr = _row_tile(h, cols, 16)

    def body(p_ref, own_ref, g0, g1, g2, o_ref):
        o_ref[...] = ((own_ref[...].astype(F32) + g0[...].astype(F32)) + g1[...].astype(F32)) + g2[...].astype(F32)

    gspec = lambda j: pl.BlockSpec((None, tr, cols), lambda i, p: (j, i, 0))
    return _pcall(
        body, name=name,
        grid_spec=pltpu.PrefetchScalarGridSpec(
            num_scalar_prefetch=1, grid=(h // tr,),
            in_specs=[pl.BlockSpec((None, tr, cols), lambda i, p: (p[0], i, 0)), gspec(0), gspec(1), gspec(2)],
            out_specs=pl.BlockSpec((tr, cols), lambda i, p: (p[1] * (h // tr) + i, 0))),
        out_shape=jax.ShapeDtypeStruct((2 * h, cols), F32),
        compiler_params=_params("parallel"),
    )(place, sums, got, got, got)


def _allgather8(buf, name):
    rows = buf.shape[0]

    def body(in_ref, out_ref, send_sems, recv_sems):
        x, y, c = _place()
        me = 4 * x + 2 * y + c
        out_ref[me] = in_ref[...]
        work = []
        for rel in range(1, 8):
            fx, fy, fc = (rel >> 2) & 1, (rel >> 1) & 1, rel & 1
            to = (x ^ fx, y ^ fy, c ^ fc)
            cp = _remote(in_ref, out_ref.at[me], send_sems.at[rel - 1], recv_sems.at[rel - 1], to)
            cp.start()
            work.append((cp, 4 * to[0] + 2 * to[1] + to[2]))
        for rel, (cp, frm) in enumerate(work):
            cp.wait_send()
            blk = out_ref.at[frm]
            _remote(blk, blk, send_sems.at[rel], recv_sems.at[rel], (x, y, c)).wait_recv()

    return _pcall(
        body, name=name, in_specs=[pl.BlockSpec(memory_space=pltpu.VMEM)],
        out_specs=pl.BlockSpec(memory_space=pltpu.VMEM),
        out_shape=jax.ShapeDtypeStruct((8, rows, LANE), F32),
        scratch_shapes=[pltpu.SemaphoreType.DMA((7,)), pltpu.SemaphoreType.DMA((7,))],
        compiler_params=pltpu.CompilerParams(has_side_effects=True),
    )(buf)


def _pack_rows(arrs):
    parts = []
    for a in arrs:
        f = a.reshape(-1).astype(F32)
        parts.append(jnp.pad(f, (0, (-f.shape[0]) % LANE)))
    flat = jnp.concatenate(parts)
    rows = -(-flat.shape[0] // LANE)
    rows8 = -(-rows // 8) * 8
    return jnp.pad(flat, (0, rows8 * LANE - flat.shape[0])).reshape(rows8, LANE)


def _unpack_rows(buf, shapes):
    flat = buf.reshape(-1)
    outs, off = [], 0
    for s in shapes:
        n = int(np.prod(s))
        outs.append(flat[off:off + n].reshape(s))
        off += -(-n // LANE) * LANE
    return outs


def _local_grads(x, p, target, wseg, w_br16, w_out16, w_ple16, b_gate, conv_w, conv_b, dt_bias, a_log, d_skip,
                 ssm_norm_w, ln_g, ln_b, rel_bias, finish_dx):
    nb, seq, _ = x.shape
    bmaps = jnp.asarray(_bucket_maps())
    bias = _bias_tables(rel_bias, bmaps)
    bgate8 = jnp.pad(b_gate, ((0, 5), (0, 0)))
    dils = [d for _, d in PATTERNS]

    x16 = x.astype(BF16)
    p16 = p.astype(BF16)
    x16p = [_permute(x16, d) for d in dils]
    qkv = [_proj(x16p[g], [wseg["qkv%d" % g]], BF16, "proj_qkv%d" % g, True)[0].reshape(
        nb, dils[g], seq // dils[g], -1) for g in range(3)]
    nat = {}
    for gi, (group, tm) in enumerate(NAT_GROUPS):
        outs = _proj(x16, [wseg[s] for s in group], F32, "proj_nat%d" % gi, True, tm)
        nat.update(zip(group, outs))
    att = [_attn_fwd(qkv[g], bias[g * GROUP_HEADS:(g + 1) * GROUP_HEADS], dils[g], "attn_fwd%d" % g) for g in range(3)]
    natural = lambda t, g: _unpermute(t.reshape(nb, seq, t.shape[-1]), dils[g])
    oa, o_att, lse = _combine_fwd(att[0][0], att[0][1], natural(att[1][0], 1), natural(att[1][1], 1),
                                  natural(att[2][0], 2), natural(att[2][1], 2), nat["gatt"])

    cw = {"xs": (conv_w[:, :D_INNER], conv_b[:, :D_INNER]),
          "bm": (conv_w[:, D_INNER:D_INNER + 512], conv_b[:, D_INNER:D_INNER + 512]),
          "cm": (conv_w[:, D_INNER + 512:], conv_b[:, D_INNER + 512:])}
    act = {s: _conv_fwd(nat[s], cw[s][0], cw[s][1], "conv_fwd_" + s) for s in ("xs", "bm", "cm")}
    dt_sp, dt_sg = _softplus_sig(nat["dt"], jnp.pad(dt_bias, ((0, 0), (0, LANE - SSM_HEADS))))
    dtg, sgg = _group_lanes(dt_sp), _group_lanes(dt_sg)
    alog_g, dskip_g = _group_lanes(a_log), _group_lanes(d_skip)
    y_ssm, y_all, sprev = _ssd_fwd(act["xs"], act["bm"], act["cm"], dtg, nat["z"], alog_g, dskip_g, ssm_norm_w)

    w_bra, w_brb = w_br16[:ATT_OUT], w_br16[ATT_OUT:]
    y_a, = _proj(oa, [w_bra], F32, "proj_ya")
    y_b, = _proj(y_ssm, [w_brb], F32, "proj_yb")
    merged = _merge_fwd(y_a, y_b, nat["gm"], bgate8)
    mix, = _proj(merged, [w_out16], F32, "proj_mix")
    pw, = _proj(p16, [w_ple16], F32, "proj_ple")

    dx, dpre16, dpw16, dgp16, ln_sums = _ln_loss(x, mix, nat["gp"], pw, target, bgate8, ln_g, ln_b)
    loss_sum = (0.5 / D_MODEL) * jnp.sum(ln_sums[3])
    dmerged = _dx([dpre16], [w_out16], [], "dx_merged")
    dya16, dyb16, dgm16, mg_sums = _merge_bwd(dmerged, y_a, y_b, nat["gm"], bgate8)
    doa = _dx([dya16], [w_bra], [], "dx_oa")
    dys = _dx([dyb16], [w_brb], [], "dx_yssm")
    g_w_out, = _dw(merged, [dpre16], BF16, "dw_out")
    g_w_br = jnp.concatenate([_dw(oa, [dya16], BF16, "dw_bra")[0], _dw(y_ssm, [dyb16], BF16, "dw_brb")[0]], axis=0)
    g_w_ple, = _dw(p16, [dpw16], BF16, "dw_ple")

    do_att, do16, stats, dgatt16 = _combine_bwd(doa, nat["gatt"], o_att, lse)
    dseg = {"gatt": dgatt16, "gm": dgm16, "gp": dgp16}
    dbias = []
    for g in range(3):
        own_order = lambda t: _permute(t, dils[g]).reshape(nb, dils[g], seq // dils[g], t.shape[-1])
        cotangent = (do_att, o_att, lse) if g == 0 else (own_order(do16), own_order(stats))
        dqkv, db = _attn_bwd(qkv[g], bias[g * GROUP_HEADS:(g + 1) * GROUP_HEADS], cotangent, dils[g],
                             "attn_bwd%d" % g)
        dseg["qkv%d" % g] = dqkv.reshape(nb, seq, -1)
        dbias.append(db)
    g_rel = _bias_grad(jnp.concatenate(dbias, axis=0), bmaps)[:, 0, :NUM_BUCKETS].T

    dxs, dbm, dcm, ddtg, dz, ssd_small, g_normw = _ssd_bwd(
        act["xs"], act["bm"], act["cm"], dtg, sgg, nat["z"], y_all, dys, sprev, alog_g, dskip_g, ssm_norm_w)
    dseg["z"] = dz
    dseg["dt"] = jnp.pad(_ungroup_lanes(ddtg), ((0, 0), (0, 0), (0, LANE - SSM_HEADS)))
    conv_sums = {}
    for s, dact in (("xs", dxs), ("bm", dbm), ("cm", dcm)):
        dpre, conv_sums[s] = _conv_bwd_pre(dact, nat[s], cw[s][0], cw[s][1], "conv_bwd_" + s)
        dseg[s] = _conv_bwd_x(dpre, cw[s][0], "conv_bwd_x_" + s)
    csum = jnp.concatenate([conv_sums["xs"], conv_sums["bm"], conv_sums["cm"]], axis=1)

    dx_perm = [_unpermute(_dx([dseg["qkv%d" % g]], [wseg["qkv%d" % g]], [], "dx_qkv%d" % g, True), dils[g])
               for g in (1, 2)]
    dwseg = {"qkv%d" % g: _dw(x16p[g], [dseg["qkv%d" % g]], BF16, "dw_qkv%d" % g, True)[0] for g in range(3)}
    for gi, group in enumerate(DW_GROUPS):
        dwseg.update(zip(group, _dw(x16, [dseg[s] for s in group], BF16, "dw_nat%d" % gi, True)))
    names = ["qkv0"] + [s for group, _ in NAT_GROUPS for s in group]
    dx = finish_dx([dseg[s] for s in names], [wseg[s] for s in names], [dx] + dx_perm, dwseg, g_w_br, g_w_out, g_w_ple)

    small = dict(
        b_gate=jnp.stack([mg_sums[0], mg_sums[1], ln_sums[2]]),
        conv_w=csum[0:4], conv_b=csum[4:5],
        dt_bias=_ungroup_lanes(ssd_small[:, 2:3, :]), a_log=_ungroup_lanes(ssd_small[:, 0:1, :]),
        d_skip=_ungroup_lanes(ssd_small[:, 1:2, :]), ssm_norm_w=g_normw,
        ln_g=ln_sums[0:1], ln_b=ln_sums[1:2], rel_bias=g_rel)
    return loss_sum, dx, small


DX_TM = 256
SMALL_ORDER = ("b_gate", "conv_w", "conv_b", "dt_bias", "a_log", "d_skip", "ssm_norm_w", "ln_g", "ln_b", "rel_bias")
SMALL_FULL_SHAPES = dict(b_gate=(3, 1024), conv_w=(4, 3072), conv_b=(1, 3072), dt_bias=(1, 32), a_log=(1, 32),
                         d_skip=(1, 32), ssm_norm_w=(1, 2048), ln_g=(1, 1024), ln_b=(1, 1024), rel_bias=(32, 36))


def kernel(x, p, w_in, b_gate, conv_w, conv_b, dt_bias, a_log, d_skip, ssm_norm_w, w_branch, w_out, w_ple, ln_g, ln_b, rel_bias, loss_target, m_w_in, m_b_gate, m_conv_w, m_conv_b, m_dt_bias, m_a_log, m_d_skip, m_ssm_norm_w, m_w_branch, m_w_out, m_w_ple, m_ln_g, m_ln_b, m_rel_bias, v_w_in, v_b_gate, v_conv_w, v_conv_b, v_dt_bias, v_a_log, v_d_skip, v_ssm_norm_w, v_w_branch, v_w_out, v_w_ple, v_ln_g, v_ln_b, v_rel_bias):
    cx, cy, cc = _place()
    chip = 2 * cx + cy
    dev = 4 * cx + 2 * cy + cc

    w_in_t = jnp.transpose(w_in[0])
    win16 = _shard_to_window(w_in_t, chip)
    g_win, g_br, g_out, g_ple = _allgather_pieces(
        [win16, w_branch[0].astype(BF16), w_out[0].astype(BF16), w_ple[0].astype(BF16)])
    wseg = _assemble(g_win)
    w_br16 = g_br.reshape(4 * 704, D_MODEL)
    w_out16 = g_out.reshape(D_MODEL, D_MODEL)
    w_ple16 = jnp.transpose(g_ple, (1, 0, 2)).reshape(PLE_DIM, D_MODEL)
    shards = _allgather8(_pack_rows([b_gate[0], conv_w[0]]), "allgather_small_params")
    per_chip = [_unpack_rows(shards[2 * k], [(3, 256), (4, 768)]) for k in range(4)]
    b_gate_full = jnp.concatenate([pc[0] for pc in per_chip], axis=1)
    conv_w_full = jnp.concatenate([pc[1] for pc in per_chip], axis=1)

    place = jnp.stack([chip, cc]).astype(jnp.int32)
    reduced = []

    def finish_dx(dhs, ws, accs, dwseg, d_br, d_out, d_ple):
        grads = [_pack(dwseg), d_br.reshape(4, 704, D_MODEL), d_out.reshape(4, 256, D_MODEL),
                 jnp.transpose(d_ple.reshape(PLE_DIM, 4, 256), (1, 0, 2))]
        got = _sibling_exchange(grads)
        chip_sums = [_pair_sum(g, t, place, "grad_pair_sum_%d" % i) for i, (g, t) in enumerate(zip(grads, got))]
        dx, others = _dx(dhs, ws, accs, "dx_w_in_and_grad_chip_scatter", True, DX_TM, chip_sums)
        fulls = [_chip_sum(s, t, place, "grad_chip_sum_%d" % i) for i, (s, t) in enumerate(zip(chip_sums, others))]
        reduced.extend(_sibling_gather(fulls))
        return dx

    loss_sum, grad_x, small = _local_grads(
        x, p[0], loss_target, wseg, w_br16, w_out16, w_ple16, b_gate_full, conv_w_full, conv_b, dt_bias, a_log,
        d_skip, ssm_norm_w, ln_g, ln_b, rel_bias, finish_dx)
    loss = lax.psum(loss_sum, ("x", "y", "c"))
    big = reduced
    g_w_in = _window_to_shard(big[0], chip)
    g_w_branch, g_w_out, g_w_ple = big[1], big[2], big[3]
    parts = _allgather8(_pack_rows([small[n] for n in SMALL_ORDER]), "allgather_small_grads")
    small_sum = _sum_rows([parts[i] for i in range(8)], F32, "small_grad_sum")
    sg = dict(zip(SMALL_ORDER, _unpack_rows(small_sum, [SMALL_FULL_SHAPES[n] for n in SMALL_ORDER])))
    sg["b_gate"] = lax.dynamic_slice_in_dim(sg["b_gate"], chip * 256, 256, axis=1)
    sg["conv_w"] = lax.dynamic_slice_in_dim(sg["conv_w"], chip * 768, 768, axis=1)
    del dev

    upd = {}
    upd["w_in"] = [jnp.transpose(t) for t in _adamw(w_in_t, g_w_in, jnp.transpose(m_w_in[0]),
                                                      jnp.transpose(v_w_in[0]), "adamw_w_in")]
    upd["w_branch"] = _adamw(w_branch[0], g_w_branch, m_w_branch[0], v_w_branch[0], "adamw_w_branch")
    upd["w_out"] = _adamw(w_out[0], g_w_out, m_w_out[0], v_w_out[0], "adamw_w_out")
    upd["w_ple"] = _adamw(w_ple[0], g_w_ple, m_w_ple[0], v_w_ple[0], "adamw_w_ple")
    small_w = dict(b_gate=b_gate, conv_w=conv_w, conv_b=conv_b, dt_bias=dt_bias, a_log=a_log, d_skip=d_skip,
                   ssm_norm_w=ssm_norm_w, ln_g=ln_g, ln_b=ln_b, rel_bias=rel_bias)
    small_m = dict(b_gate=m_b_gate, conv_w=m_conv_w, conv_b=m_conv_b, dt_bias=m_dt_bias, a_log=m_a_log,
                   d_skip=m_d_skip, ssm_norm_w=m_ssm_norm_w, ln_g=m_ln_g, ln_b=m_ln_b, rel_bias=m_rel_bias)
    small_v = dict(b_gate=v_b_gate, conv_w=v_conv_w, conv_b=v_conv_b, dt_bias=v_dt_bias, a_log=v_a_log,
                   d_skip=v_d_skip, ssm_norm_w=v_ssm_norm_w, ln_g=v_ln_g, ln_b=v_ln_b, rel_bias=v_rel_bias)
    shapes = [small_w[n].shape for n in SMALL_ORDER]
    s_delta, s_m, s_v = _adamw(_pack_rows([small_w[n] for n in SMALL_ORDER]), _pack_rows([sg[n] for n in SMALL_ORDER]),
                               _pack_rows([small_m[n] for n in SMALL_ORDER]), _pack_rows([small_v[n] for n in SMALL_ORDER]),
                               "adamw_small")
    for i, n in enumerate(SMALL_ORDER):
        upd[n] = tuple(_unpack_rows(t, shapes)[i] for t in (s_delta, s_m, s_v))
        sg[n] = sg[n].reshape(small_w[n].shape)

    order = ("w_in", "b_gate", "conv_w", "conv_b", "dt_bias", "a_log", "d_skip", "ssm_norm_w", "w_branch", "w_out",
             "w_ple", "ln_g", "ln_b", "rel_bias")
    grads = dict(sg, w_in=jnp.transpose(g_w_in)[None],w_branch=g_w_branch[None], w_out=g_w_out[None], w_ple=g_w_ple[None])
    lead = lambda n, t: t[None] if n in ("w_in", "w_branch", "w_out", "w_ple") else t
    return (loss, grad_x, *[grads[n] for n in order], *[lead(n, upd[n][0]) for n in order],
            *[lead(n, upd[n][1]) for n in order], *[lead(n, upd[n][2]) for n in order])
```

```python
import functools
import math

import numpy as np
import jax
import jax.numpy as jnp
from jax import lax
from jax.experimental import pallas as pl
from jax.experimental.pallas import tpu as pltpu

F32, BF16 = jnp.float32, jnp.bfloat16

D_MODEL = 1024
HEAD_DIM = 64
GROUP_HEADS = 12
ATT_OUT = GROUP_HEADS * HEAD_DIM
PATTERNS = ((128, 1), (512, 4), (2048, 16))
BAND = 128
NUM_BUCKETS = 32
MAX_DISTANCE = 2048
D_INNER = 2048
SSM_HEADS = 32
SSM_GROUPS = 4
GROUP_SSM_HEADS = SSM_HEADS // SSM_GROUPS
D_STATE = 128
CHUNK = 128
PLE_DIM = 256
ALPHA = 2.0 ** 0.25
LN_EPS = 1e-5
RMS_EPS = 1e-5
ADAM_LR, ADAM_B1, ADAM_B2, ADAM_EPS, ADAM_WD, ADAM_STEP = 0.001, 0.9, 0.999, 1e-08, 0.01, 10
NEG = -1e30

QKV_W = 3 * ATT_OUT
IN_COLS = 15904
SHARD_COLS = IN_COLS // 4
DT_COL = 12800
ROW_TILE = 16
WIN_ROWS = 4000


def _win_offset(k):
    return (k * SHARD_COLS) % ROW_TILE


def _win_start(k):
    return k * SHARD_COLS - _win_offset(k)

VMEM_LIMIT_BYTES = 56 * 1024 * 1024
LANE = 128
MESH = pl.DeviceIdType.MESH
NT = (((1,), (1,)), ((), ()))
TN = (((0,), (0,)), ((), ()))


def _pcall(body, **kw):
    return pl.pallas_call(body, **kw)


def _params(*sem):
    return pltpu.CompilerParams(dimension_semantics=sem, vmem_limit_bytes=VMEM_LIMIT_BYTES)


def _sigmoid(v):
    return jax.nn.sigmoid(v)


MM_TM = 512


def _permute(t, d):
    nb, seq, ch = t.shape
    return t if d == 1 else t.reshape(nb, seq // d, d, ch).transpose(0, 2, 1, 3).reshape(nb, seq, ch)


def _unpermute(t, d):
    nb, seq, ch = t.shape
    return t if d == 1 else t.reshape(nb, d, seq // d, ch).transpose(0, 2, 1, 3).reshape(nb, seq, ch)


def _tok_spec(tm, width):
    return pl.BlockSpec((None, tm, width), lambda b, i: (b, i, 0))


def _whole(arr, single_buffer=False):
    mode = dict(pipeline_mode=pl.Buffered(1)) if single_buffer else {}
    return pl.BlockSpec(arr.shape, lambda b, i: (0,) * arr.ndim, **mode)


def _proj(a3, ws, out_dtype, name, w_rows_are_outputs=False, tm=MM_TM):
    nb, seq, kdim = a3.shape
    nw = len(ws)
    widths = [w.shape[0] if w_rows_are_outputs else w.shape[1] for w in ws]

    def body(*refs):
        a = refs[0][...].astype(BF16)
        for w_ref, o_ref in zip(refs[1:1 + nw], refs[1 + nw:]):
            if w_rows_are_outputs:
                v = lax.dot_general(a, w_ref[...], NT, preferred_element_type=F32)
            else:
                v = jnp.dot(a, w_ref[...], preferred_element_type=F32)
            o_ref[...] = v.astype(out_dtype)

    return _pcall(
        body, name=name, grid=(nb, seq // tm),
        in_specs=[_tok_spec(tm, kdim)] + [_whole(w) for w in ws],
        out_specs=[_tok_spec(tm, n) for n in widths],
        out_shape=[jax.ShapeDtypeStruct((nb, seq, n), out_dtype) for n in widths],
        compiler_params=_params("parallel", "parallel"),
    )(a3, *ws)


def _dx(dhs, ws, accs, name, w_rows_are_outputs=False, tm=MM_TM, scatter=None):
    nb, seq, _ = dhs[0].shape
    nd, nacc = len(dhs), len(accs)
    kout = ws[0].shape[1] if w_rows_are_outputs else ws[0].shape[0]
    sums = scatter or []
    ns = len(sums)
    chunks = [_row_chunks(s.shape[1], s.shape[2] * s.dtype.itemsize) for s in sums]
    n_sem = 3 * sum(len(ch) for ch in chunks)
    grid = (nb, seq // tm)

    def body(*refs):
        n_in = 2 * nd + nacc
        sum_refs, o_ref, got_refs = refs[n_in:n_in + ns], refs[n_in + ns], refs[n_in + ns + 1:n_in + 2 * ns + 1]

        def copies():
            send_sems, recv_sems = refs[-2], refs[-1]
            x, y, c = _place()
            out = []
            for a in range(ns):
                for s, m in chunks[a]:
                    for j, (cx, cy) in enumerate(_other_chips(x, y)):
                        k = len(out)
                        out.append(_remote(sum_refs[a].at[2 * cx + cy, pl.ds(s, m)], got_refs[a].at[j, pl.ds(s, m)],
                                           send_sems.at[k], recv_sems.at[k], (cx, cy, c)))
            return out

        if ns:
            @pl.when((pl.program_id(0) == 0) & (pl.program_id(1) == 0))
            def _():
                for cp in copies():
                    cp.start()

        v = None
        for dh_ref, w_ref in zip(refs[:nd], refs[nd:2 * nd]):
            dh = dh_ref[...].astype(BF16)
            if w_rows_are_outputs:
                t = jnp.dot(dh, w_ref[...], preferred_element_type=F32)
            else:
                t = lax.dot_general(dh, w_ref[...], NT, preferred_element_type=F32)
            v = t if v is None else v + t
        for a_ref in refs[2 * nd:n_in]:
            v = v + a_ref[...]
        o_ref[...] = v

        if ns:
            @pl.when((pl.program_id(0) == grid[0] - 1) & (pl.program_id(1) == grid[1] - 1))
            def _():
                for cp in copies():
                    cp.wait()

    out = _pcall(
        body, name=name, grid=grid,
        in_specs=[_tok_spec(tm, dh.shape[-1]) for dh in dhs] + [_whole(w, bool(ns)) for w in ws]
        + [_tok_spec(tm, kout)] * nacc + [ANY] * ns,
        out_specs=[_tok_spec(tm, kout)] + [ANY] * ns,
        out_shape=[jax.ShapeDtypeStruct((nb, seq, kout), F32)]
        + [jax.ShapeDtypeStruct((3,) + s.shape[1:], s.dtype) for s in sums],
        input_output_aliases={2 * nd: 0} if nacc else {},
        scratch_shapes=[pltpu.SemaphoreType.DMA((n_sem,)), pltpu.SemaphoreType.DMA((n_sem,))] if ns else [],
        compiler_params=pltpu.CompilerParams(
            dimension_semantics=("arbitrary", "arbitrary") if ns else ("parallel", "parallel"),
            vmem_limit_bytes=VMEM_LIMIT_BYTES, has_side_effects=bool(ns)),
    )(*dhs, *ws, *accs, *sums)
    return (out[0], list(out[1:])) if ns else out[0]


def _dw(a3, dhs, out_dtype, name, rows_are_outputs=False):
    nb, seq, kdim = a3.shape
    nd = len(dhs)
    grid = (nb, seq // MM_TM)
    shapes = [(dh.shape[-1], kdim) if rows_are_outputs else (kdim, dh.shape[-1]) for dh in dhs]

    def body(*refs):
        b, i = pl.program_id(0), pl.program_id(1)
        dh_refs, o_refs, acc_refs = refs[1:1 + nd], refs[1 + nd:1 + 2 * nd], refs[1 + 2 * nd:]

        @pl.when((b == 0) & (i == 0))
        def _():
            for acc_ref in acc_refs:
                acc_ref[...] = jnp.zeros_like(acc_ref)

        a = refs[0][...].astype(BF16)
        for dh_ref, acc_ref in zip(dh_refs, acc_refs):
            dh = dh_ref[...].astype(BF16)
            acc_ref[...] += lax.dot_general(*((dh, a) if rows_are_outputs else (a, dh)), TN,
                                            preferred_element_type=F32)

        @pl.when((b == grid[0] - 1) & (i == grid[1] - 1))
        def _():
            for o_ref, acc_ref in zip(o_refs, acc_refs):
                o_ref[...] = acc_ref[...].astype(out_dtype)

    return _pcall(
        body, name=name, grid=grid,
        in_specs=[_tok_spec(MM_TM, kdim)] + [_tok_spec(MM_TM, dh.shape[-1]) for dh in dhs],
        out_specs=[pl.BlockSpec(s, lambda b, i: (0, 0)) for s in shapes],
        out_shape=[jax.ShapeDtypeStruct(s, out_dtype) for s in shapes],
        scratch_shapes=[pltpu.VMEM(s, F32) for s in shapes],
        compiler_params=_params("arbitrary", "arbitrary"),
    )(a3, *dhs)


def _qkv_rows(g):
    return [(part * QKV_W + g * ATT_OUT + hp * LANE, LANE) for hp in range(ATT_OUT // LANE) for part in range(3)]


XBC_START = 3 * QKV_W + ATT_OUT + D_INNER
GROUP_CH = GROUP_SSM_HEADS * HEAD_DIM
XBC_GROUP = GROUP_CH + 2 * D_STATE
CONV_DIM = SSM_GROUPS * XBC_GROUP


def _xbc_ranges():
    out = []
    for g in range(SSM_GROUPS):
        out += [(g * GROUP_CH, GROUP_CH), (D_INNER + g * D_STATE, D_STATE),
                (D_INNER + SSM_GROUPS * D_STATE + g * D_STATE, D_STATE)]
    return out


def _xbc_group_order(t):
    return jnp.concatenate([t[..., s:s + n] for s, n in _xbc_ranges()], axis=-1)


def _xbc_reference_order(t):
    g = lambda off, n: [t[..., k * XBC_GROUP + off:k * XBC_GROUP + off + n] for k in range(SSM_GROUPS)]
    return jnp.concatenate(g(0, GROUP_CH) + g(GROUP_CH, D_STATE) + g(GROUP_CH + D_STATE, D_STATE), axis=-1)


def _segments():
    one = lambda name, start, rows: (name, [(start, rows)], max(rows, LANE))
    return [("qkv%d" % g, _qkv_rows(g), QKV_W) for g in range(3)] + [
        one("gatt", 3 * QKV_W, ATT_OUT), one("z", 3 * QKV_W + ATT_OUT, D_INNER),
        ("xbc", [(XBC_START + s, n) for s, n in _xbc_ranges()], CONV_DIM), one("dt", DT_COL, SSM_HEADS),
        one("gm", DT_COL + SSM_HEADS, 2 * D_MODEL), one("gp", DT_COL + SSM_HEADS + 2 * D_MODEL, D_MODEL)]


LAYOUT_TC = 256
NAT_GROUPS = ((("gatt", "z", "dt", "gp"), 512), (("xbc", "gm"), 256))
DW_GROUPS = (("gatt", "z", "dt", "gp"), ("xbc",), ("gm",))


def _assemble(win):
    segs = _segments()

    def body(win_ref, *outs):
        def pieces(start, rows):
            t, end = start, start + rows
            while t < end:
                k = min(t // SHARD_COLS, 3)
                shard_end = (k + 1) * SHARD_COLS
                if k < 3 and shard_end % ROW_TILE and t == shard_end - shard_end % ROW_TILE:
                    lo = t - _win_start(k)
                    yield win_ref[k, lo:lo + ROW_TILE, :] + win_ref[k + 1, 0:ROW_TILE, :]
                    t += ROW_TILE
                    continue
                upto = min(end, shard_end - shard_end % ROW_TILE if k < 3 else end)
                yield win_ref[k, t - _win_start(k):upto - _win_start(k), :]
                t = upto

        for (_, ranges, total), o_ref in zip(segs, outs):
            off = 0
            for start, rows in ranges:
                for part in pieces(start, rows):
                    o_ref[off:off + part.shape[0], :] = part
                    off += part.shape[0]
            if off < total:
                o_ref[off:total, :] = jnp.zeros((total - off, o_ref.shape[1]), BF16)

    outs = _pcall(
        body, name="assemble_w_in", grid=(D_MODEL // LAYOUT_TC,),
        in_specs=[pl.BlockSpec((4, WIN_ROWS, LAYOUT_TC), lambda i: (0, 0, i))],
        out_specs=[pl.BlockSpec((total, LAYOUT_TC), lambda i: (0, i)) for _, _, total in segs],
        out_shape=[jax.ShapeDtypeStruct((total, D_MODEL), BF16) for _, _, total in segs],
        compiler_params=_params("parallel"),
    )(win)
    return {name: o for (name, _, _), o in zip(segs, outs)}


def _pack(dsegs):
    segs = _segments()

    def body(*refs):
        ins, o_ref = refs[:-1], refs[-1]
        tail = IN_COLS - _win_start(3)
        o_ref[3, tail:, :] = jnp.zeros((WIN_ROWS - tail, o_ref.shape[2]), BF16)
        for (_, ranges, _), s_ref in zip(segs, ins):
            off = 0
            for start, rows in ranges:
                for k in range(4):
                    lo = _win_start(k)
                    a, b = max(start, lo), min(start + rows, lo + WIN_ROWS)
                    if a < b:
                        o_ref[k, a - lo:b - lo, :] = s_ref[off + a - start:off + b - start, :]
                off += rows

    return _pcall(
        body, name="pack_dw_in", grid=(D_MODEL // LAYOUT_TC,),
        in_specs=[pl.BlockSpec((total, LAYOUT_TC), lambda i: (0, i)) for _, _, total in segs],
        out_specs=pl.BlockSpec((4, WIN_ROWS, LAYOUT_TC), lambda i: (0, 0, i)),
        out_shape=jax.ShapeDtypeStruct((4, WIN_ROWS, D_MODEL), BF16),
        compiler_params=_params("parallel"),
    )(*[dsegs[name] for name, _, _ in segs])


def _shard_to_window(shard_t, k):
    def at(off):
        return lambda w: jnp.pad(w.astype(BF16), ((off, WIN_ROWS - SHARD_COLS - off), (0, 0)))

    return lax.cond(k % 2 == 1, at(_win_offset(1)), at(_win_offset(0)), shard_t)


def _window_to_shard(win, k):
    return lax.dynamic_slice(win, ((k % 2) * _win_offset(1), 0), (SHARD_COLS, D_MODEL))


def _bucket_maps():
    qi = np.arange(BAND)[:, None]
    kj = np.arange(2 * BAND)[None, :]
    delta = qi + BAND - kj
    maps = []
    for window, dil in PATTERNS:
        valid = (delta >= 0) & (delta <= window // dil)
        dist = np.maximum(delta, 0) * dil
        max_exact = NUM_BUCKETS // 2
        d_f = np.maximum(dist, 1).astype(np.float32)
        large = max_exact + (np.log(d_f / np.float32(max_exact)) / np.float32(math.log(MAX_DISTANCE / max_exact))
                             * np.float32(NUM_BUCKETS - max_exact)).astype(np.int32)
        large = np.minimum(large, NUM_BUCKETS - 1)
        bucket = np.where(dist < max_exact, dist, large)
        maps.append(np.where(valid, bucket, -1).astype(np.int32))
    return np.stack(maps)


def _bias_tables(rel_bias, bmaps):
    def body(rb_ref, bm_ref, o_ref):
        h = pl.program_id(0)
        bm = bm_ref[...]
        acc = jnp.full(bm.shape, NEG, F32)
        for b in range(NUM_BUCKETS):
            acc = jnp.where(bm == b, rb_ref[b, h], acc)
        o_ref[...] = acc

    return _pcall(
        body, name="bias_tables", grid=(3 * GROUP_HEADS,),
        in_specs=[pl.BlockSpec(memory_space=pltpu.SMEM),
                  pl.BlockSpec((None, BAND, 2 * BAND), lambda h: (h // GROUP_HEADS, 0, 0))],
        out_specs=pl.BlockSpec((None, BAND, 2 * BAND), lambda h: (h, 0, 0)),
        out_shape=jax.ShapeDtypeStruct((3 * GROUP_HEADS, BAND, 2 * BAND), F32),
        compiler_params=_params("parallel"),
    )(rel_bias, bmaps)


def _bias_grad(dbias, bmaps):
    def body(db_ref, bm_ref, o_ref):
        bm = bm_ref[...]
        db = db_ref[...]
        lane = lax.broadcasted_iota(jnp.int32, (1, LANE), 1)
        vec = jnp.zeros((1, LANE), F32)
        for b in range(NUM_BUCKETS):
            s = jnp.sum(jnp.where(bm == b, db, 0.0), keepdims=True)
            vec = jnp.where(lane == b, s, vec)
        o_ref[...] = vec

    return _pcall(
        body, name="bias_grad", grid=(3 * GROUP_HEADS,),
        in_specs=[pl.BlockSpec((None, BAND, 2 * BAND), lambda h: (h, 0, 0)),
                  pl.BlockSpec((None, BAND, 2 * BAND), lambda h: (h // GROUP_HEADS, 0, 0))],
        out_specs=pl.BlockSpec((None, 1, LANE), lambda h: (h, 0, 0)),
        out_shape=jax.ShapeDtypeStruct((3 * GROUP_HEADS, 1, LANE), F32),
        compiler_params=_params("parallel"),
    )(dbias, bmaps)


def _rows(n):
    if isinstance(n, int):
        return pl.ds(n * BAND, BAND)
    return pl.ds(pl.multiple_of(n * BAND, BAND), BAND)


def _for_blocks(blocks, nblk, per, carry):
    carry = blocks([0], carry, False)
    start = 1 + (nblk - 1) % per
    for n in range(1, start):
        carry = blocks([n], carry, True)
    trips = (nblk - start) // per
    if trips > 0:
        carry = lax.fori_loop(
            0, trips, lambda t, c: blocks([start + t * per + u for u in range(per)], c, True), carry)
    return carry


def _pairs_per_step(d):
    return {1: 1, 4: 6, 16: 6}[d]


def _head_cols(i, h, part):
    base = 3 * LANE * i + part * LANE + h * HEAD_DIM
    return slice(base, base + HEAD_DIM)


def _attn_fwd(qkv4, bias, d, name):
    nb, _, sub, _ = qkv4.shape
    nblk = sub // BAND
    scale = HEAD_DIM ** -0.5
    npair = ATT_OUT // LANE
    hps = _pairs_per_step(d)
    compact = d > 1

    def body(qkv_ref, bias_ref, o_ref, l_ref):
        def blocks(ns, carry, with_prev):
            chains = [(bi, i, h) for bi in range(len(ns)) for i in range(hps) for h in range(2)]
            scores = []
            for bi, i, h in chains:
                n = ns[bi]
                q = qkv_ref[_rows(n), _head_cols(i, h, 0)] * scale
                s_c = lax.dot_general(q, qkv_ref[_rows(n), _head_cols(i, h, 1)], NT,
                                      preferred_element_type=F32) + bias_ref[2 * i + h, :, BAND:]
                s_p = None
                if with_prev:
                    s_p = lax.dot_general(q, qkv_ref[_rows(n - 1), _head_cols(i, h, 1)], NT,
                                          preferred_element_type=F32) + bias_ref[2 * i + h, :, :BAND]
                scores.append((s_c, s_p))
            probs = []
            for s_c, s_p in scores:
                m = jnp.max(s_c, -1, keepdims=True)
                if with_prev:
                    m = jnp.maximum(m, jnp.max(s_p, -1, keepdims=True))
                e_c = jnp.exp(s_c - m)
                den = jnp.sum(e_c, -1, keepdims=True)
                e_p = None
                if with_prev:
                    e_p = jnp.exp(s_p - m)
                    den = den + jnp.sum(e_p, -1, keepdims=True)
                    e_p = e_p.astype(BF16)
                probs.append((e_c.astype(BF16), e_p, den, m))
            outs = {}
            for (bi, i, h), (e_c, e_p, den, m) in zip(chains, probs):
                n = ns[bi]
                acc = jnp.dot(e_c, qkv_ref[_rows(n), _head_cols(i, h, 2)], preferred_element_type=F32)
                if with_prev:
                    acc = acc + jnp.dot(e_p, qkv_ref[_rows(n - 1), _head_cols(i, h, 2)], preferred_element_type=F32)
                outs[(bi, i, h)] = (acc / den, jnp.broadcast_to(m + jnp.log(den), (BAND, HEAD_DIM)))
            lane = lax.broadcasted_iota(jnp.int32, (BAND, LANE), 1)
            for bi, n in enumerate(ns):
                per_head = jnp.zeros((BAND, LANE), F32)
                for i in range(hps):
                    o_ref[_rows(n), i * LANE:(i + 1) * LANE] = jnp.concatenate(
                        [outs[(bi, i, 0)][0], outs[(bi, i, 1)][0]], axis=1)
                    if compact:
                        for h in range(2):
                            per_head = jnp.where(lane == 2 * i + h, outs[(bi, i, h)][1][:, :1], per_head)
                    else:
                        l_ref[_rows(n), i * LANE:(i + 1) * LANE] = jnp.concatenate(
                            [outs[(bi, i, 0)][1], outs[(bi, i, 1)][1]], axis=1)
                if compact:
                    l_ref[_rows(n), :] = per_head
            return carry

        _for_blocks(blocks, nblk, 2 if hps == 1 else 1, 0)

    in_specs = [pl.BlockSpec((None, None, sub, 3 * LANE * hps), lambda hp, b, r: (b, r, 0, hp)),
                pl.BlockSpec((2 * hps, BAND, 2 * BAND), lambda hp, b, r: (hp, 0, 0))]
    if compact:
        return _pcall(
            body, name=name, grid=(1, nb, d), in_specs=in_specs,
            out_specs=[pl.BlockSpec((None, None, sub, ATT_OUT), lambda hp, b, r: (b, r, 0, 0)),
                       pl.BlockSpec((None, None, sub, LANE), lambda hp, b, r: (b, r, 0, 0))],
            out_shape=[jax.ShapeDtypeStruct((nb, d, sub, ATT_OUT), F32), jax.ShapeDtypeStruct((nb, d, sub, LANE), F32)],
            compiler_params=_params("parallel", "parallel", "parallel"),
        )(qkv4, bias)
    ospec = pl.BlockSpec((None, sub, hps * LANE), lambda hp, b, r: (b, 0, r * (npair // hps) + hp))
    return _pcall(
        body, name=name, grid=(npair // hps, nb, d), in_specs=in_specs, out_specs=[ospec, ospec],
        out_shape=[jax.ShapeDtypeStruct((nb, sub, d * ATT_OUT), F32)] * 2,
        compiler_params=_params("parallel", "parallel", "parallel"),
    )(qkv4, bias)


STAT_LSE_LANE = 16


def _attn_bwd(qkv4, bias, cotangent, d, name):
    nb, _, sub, _ = qkv4.shape
    nblk = sub // BAND
    scale = HEAD_DIM ** -0.5
    npair = ATT_OUT // LANE
    hps = _pairs_per_step(d)
    compact = d > 1

    def body(qkv_ref, bias_ref, *rest):
        do_ref, dqkv_ref, db_ref = rest[0], rest[-2], rest[-1]
        b, r = pl.program_id(1), pl.program_id(2)

        @pl.when((b == 0) & (r == 0))
        def _():
            db_ref[...] = jnp.zeros_like(db_ref)

        def blocks(ns, carry, with_prev):
            sides = (0, 1) if with_prev else (0,)
            chains = [(bi, i, h, sd) for bi in range(len(ns)) for i in range(hps) for h in range(2) for sd in sides]
            key_rows = lambda bi, sd: _rows(ns[bi] - sd)
            qs = {}
            for bi in range(len(ns)):
                for i in range(hps):
                    for h in range(2):
                        hl = slice(i * LANE + h * HEAD_DIM, i * LANE + (h + 1) * HEAD_DIM)
                        do = do_ref[_rows(ns[bi]), hl]
                        if compact:
                            st_ref, head = rest[1], 2 * i + h
                            ebar = st_ref[_rows(ns[bi]), head:head + 1]
                            lcol = st_ref[_rows(ns[bi]), STAT_LSE_LANE + head:STAT_LSE_LANE + head + 1]
                        else:
                            ebar = jnp.sum(do * rest[1][_rows(ns[bi]), hl], -1, keepdims=True)
                            lcol = rest[2][_rows(ns[bi]), i * LANE + h * HEAD_DIM:i * LANE + h * HEAD_DIM + 1]
                        q_scaled = qkv_ref[_rows(ns[bi]), _head_cols(i, h, 0)] * scale
                        qs[(bi, i, h)] = (q_scaled, do.astype(BF16), ebar, lcol)
            raw = []
            for bi, i, h, sd in chains:
                q, do16, _, _ = qs[(bi, i, h)]
                k = qkv_ref[key_rows(bi, sd), _head_cols(i, h, 1)]
                v = qkv_ref[key_rows(bi, sd), _head_cols(i, h, 2)]
                bias_blk = bias_ref[2 * i + h, :, :BAND] if sd else bias_ref[2 * i + h, :, BAND:]
                s = lax.dot_general(q, k, NT, preferred_element_type=F32) + bias_blk
                dp = lax.dot_general(do16, v, NT, preferred_element_type=F32)
                raw.append((s, dp))
            soft = []
            for (bi, i, h, sd), (s, dp) in zip(chains, raw):
                _, _, ebar, lcol = qs[(bi, i, h)]
                p = jnp.exp(s - lcol)
                ds = p * (dp - ebar)
                if sd:
                    db_ref[2 * i + h, :, :BAND] += ds
                else:
                    db_ref[2 * i + h, :, BAND:] += ds
                soft.append((p.astype(BF16), ds.astype(BF16)))
            grads = {}
            for (bi, i, h, sd), (p16, ds16) in zip(chains, soft):
                q, do16, _, _ = qs[(bi, i, h)]
                k = qkv_ref[key_rows(bi, sd), _head_cols(i, h, 1)]
                grads[(bi, i, h, sd)] = (
                    jnp.dot(ds16, k, preferred_element_type=F32),
                    lax.dot_general(ds16, q, TN, preferred_element_type=F32),
                    lax.dot_general(p16, do16, TN, preferred_element_type=F32))
            both = lambda bi, i, sd, which: jnp.concatenate(
                [grads[(bi, i, 0, sd)][which], grads[(bi, i, 1, sd)][which]], axis=1)
            carry = list(carry) if carry is not None else None
            for bi, n in enumerate(ns):
                for i in range(hps):
                    base = 3 * LANE * i
                    dq = both(bi, i, 0, 0)
                    if with_prev:
                        dq = dq + both(bi, i, 1, 0)
                        dqkv_ref[_rows(n - 1), base + LANE:base + 2 * LANE] = (
                            carry[2 * i] + both(bi, i, 1, 1)).astype(BF16)
                        dqkv_ref[_rows(n - 1), base + 2 * LANE:base + 3 * LANE] = (
                            carry[2 * i + 1] + both(bi, i, 1, 2)).astype(BF16)
                    dqkv_ref[_rows(n), base:base + LANE] = (dq * scale).astype(BF16)
                carry = [t for i in range(hps) for t in (both(bi, i, 0, 1), both(bi, i, 0, 2))]
            return tuple(carry)

        carry = _for_blocks(blocks, nblk, 2 if hps == 1 else 1, None)
        for i in range(hps):
            base = 3 * LANE * i
            dqkv_ref[_rows(nblk - 1), base + LANE:base + 2 * LANE] = carry[2 * i].astype(BF16)
            dqkv_ref[_rows(nblk - 1), base + 2 * LANE:base + 3 * LANE] = carry[2 * i + 1].astype(BF16)

    qspec = pl.BlockSpec((None, None, sub, 3 * LANE * hps), lambda hp, b, r: (b, r, 0, hp))
    bspec = pl.BlockSpec((2 * hps, BAND, 2 * BAND), lambda hp, b, r: (hp, 0, 0))
    if compact:
        cspecs = [pl.BlockSpec((None, None, sub, ATT_OUT), lambda hp, b, r: (b, r, 0, 0)),
                  pl.BlockSpec((None, None, sub, LANE), lambda hp, b, r: (b, r, 0, 0))]
    else:
        cspecs = [pl.BlockSpec((None, sub, hps * LANE), lambda hp, b, r: (b, 0, r * (npair // hps) + hp))] * 3
    return _pcall(
        body, name=name, grid=(npair // hps, nb, d),
        in_specs=[qspec, bspec] + cspecs, out_specs=[qspec, bspec],
        out_shape=[jax.ShapeDtypeStruct(qkv4.shape, BF16),
                   jax.ShapeDtypeStruct((GROUP_HEADS, BAND, 2 * BAND), F32)],
        compiler_params=_params("parallel", "arbitrary", "arbitrary"),
    )(qkv4, bias, *cotangent)


def _head_lanes(first_lane, one_channel):
    c = lax.broadcasted_iota(jnp.int32, (ATT_OUT, LANE), 0)
    lane = lax.broadcasted_iota(jnp.int32, (ATT_OUT, LANE), 1)
    hit = lane == first_lane + c // HEAD_DIM
    if one_channel:
        hit = hit & (c % HEAD_DIM == 0)
    return hit.astype(BF16)


def _exact_dot(v, m01, dims=None):
    parts = _split3(v)
    if dims is None:
        dot = lambda t: jnp.dot(t, m01, preferred_element_type=F32)
    else:
        dot = lambda t: lax.dot_general(t, m01, dims, preferred_element_type=F32)
    return (dot(parts[0]) + dot(parts[1])) + dot(parts[2])


def _combine_fwd(o0, l0, o1, l1, o2, l2, gatt):
    nb, seq, _ = gatt.shape
    tm = 512

    def body(o0_ref, l0_ref, o1_ref, l1_ref, o2_ref, l2_ref, g_ref, oa_ref, oatt_ref, lse_ref):
        spread = _head_lanes(0, False)
        l0v = l0_ref[...]
        l1v = _exact_dot(l1_ref[...], spread, NT)
        l2v = _exact_dot(l2_ref[...], spread, NT)
        m = jnp.maximum(jnp.maximum(l0v, l1v), l2v)
        tot = m + jnp.log(jnp.exp(l0v - m) + jnp.exp(l1v - m) + jnp.exp(l2v - m))
        o = (jnp.exp(l0v - tot) * o0_ref[...] + jnp.exp(l1v - tot) * o1_ref[...]
             + jnp.exp(l2v - tot) * o2_ref[...])
        g = g_ref[...]
        oa_ref[...] = (o * (g * _sigmoid(g))).astype(BF16)
        oatt_ref[...] = o
        lse_ref[...] = tot

    spec = pl.BlockSpec((None, tm, ATT_OUT), lambda b, i: (b, i, 0))
    lspec = pl.BlockSpec((None, tm, LANE), lambda b, i: (b, i, 0))
    return _pcall(
        body, name="attn_combine", grid=(nb, seq // tm),
        in_specs=[spec, spec, spec, lspec, spec, lspec, spec], out_specs=[spec] * 3,
        out_shape=[jax.ShapeDtypeStruct((nb, seq, ATT_OUT), BF16), jax.ShapeDtypeStruct((nb, seq, ATT_OUT), F32),
                   jax.ShapeDtypeStruct((nb, seq, ATT_OUT), F32)],
        compiler_params=_params("parallel", "parallel"),
    )(o0, l0, o1, l1, o2, l2, gatt)


def _combine_bwd(doa, gatt, o_att, lse):
    nb, seq, _ = gatt.shape
    tm = 512

    def body(doa_ref, g_ref, o_ref, l_ref, do_ref, do16_ref, st_ref, dg_ref):
        g = g_ref[...]
        sg = _sigmoid(g)
        do = doa_ref[...] * (g * sg)
        do_ref[...] = do
        do16_ref[...] = do.astype(BF16)
        st_ref[...] = (_exact_dot(do * o_ref[...], _head_lanes(0, False))
                       + _exact_dot(l_ref[...], _head_lanes(STAT_LSE_LANE, True)))
        dg_ref[...] = (doa_ref[...] * o_ref[...] * (sg * (1.0 + g * (1.0 - sg)))).astype(BF16)

    spec = pl.BlockSpec((None, tm, ATT_OUT), lambda b, i: (b, i, 0))
    lspec = pl.BlockSpec((None, tm, LANE), lambda b, i: (b, i, 0))
    return _pcall(
        body, name="attn_combine_bwd", grid=(nb, seq // tm), in_specs=[spec] * 4,
        out_specs=[spec, spec, lspec, spec],
        out_shape=[jax.ShapeDtypeStruct((nb, seq, ATT_OUT), F32), jax.ShapeDtypeStruct((nb, seq, ATT_OUT), BF16),
                   jax.ShapeDtypeStruct((nb, seq, LANE), F32), jax.ShapeDtypeStruct((nb, seq, ATT_OUT), BF16)],
        compiler_params=_params("parallel", "parallel"),
    )(doa, gatt, o_att, lse)


CONV_TM = 512
CONV_TC = 512


def _shift_down(cur, halo, k):
    rolled = pltpu.roll(cur, k, 0)
    hro = pltpu.roll(halo, k, 0)
    row = lax.broadcasted_iota(jnp.int32, hro.shape, 0)
    return jnp.concatenate([jnp.where(row < k, hro, rolled[:8]), rolled[8:]], axis=0)


def _shift_up(cur, halo, k):
    n = cur.shape[0]
    rolled = pltpu.roll(cur, n - k, 0)
    hro = pltpu.roll(halo, 8 - k, 0)
    row = lax.broadcasted_iota(jnp.int32, hro.shape, 0)
    return jnp.concatenate([rolled[:n - 8], jnp.where(row >= 8 - k, hro, rolled[n - 8:])], axis=0)


def _conv_pre(cur, halo, w_ref, b_ref):
    acc = cur * w_ref[3:4, :] + b_ref[...]
    for k in range(1, 4):
        acc = acc + _shift_down(cur, halo, k) * w_ref[3 - k:4 - k, :]
    return acc


def _conv_specs(seq):
    nblk = seq // CONV_TM
    cur = pl.BlockSpec((None, CONV_TM, CONV_TC), lambda cb, b, i: (b, i, cb))
    prev = pl.BlockSpec((None, 8, CONV_TC), lambda cb, b, i: (b, jnp.maximum(i * (CONV_TM // 8) - 1, 0), cb))
    nxt = pl.BlockSpec((None, 8, CONV_TC),
                       lambda cb, b, i: (b, jnp.minimum((i + 1) * (CONV_TM // 8), seq // 8 - 1), cb))
    wspec = pl.BlockSpec((4, CONV_TC), lambda cb, b, i: (0, cb))
    bspec = pl.BlockSpec((1, CONV_TC), lambda cb, b, i: (0, cb))
    return nblk, cur, prev, nxt, wspec, bspec


def _conv_fwd(xin, w4, bias, name):
    nb, seq, ch = xin.shape
    _, cur, prev, _, wspec, bspec = _conv_specs(seq)

    def body(x_ref, h_ref, w_ref, b_ref, o_ref):
        halo = jnp.where(pl.program_id(2) > 0, h_ref[...], 0.0)
        pre = _conv_pre(x_ref[...], halo, w_ref, b_ref)
        o_ref[...] = pre * _sigmoid(pre)

    return _pcall(
        body, name=name, grid=(ch // CONV_TC, nb, seq // CONV_TM),
        in_specs=[cur, prev, wspec, bspec], out_specs=cur,
        out_shape=jax.ShapeDtypeStruct(xin.shape, F32),
        compiler_params=_params("parallel", "parallel", "parallel"),
    )(xin, xin, w4, bias)


def _conv_bwd_pre(dact, xin, w4, bias, name):
    nb, seq, ch = xin.shape
    _, cur, prev, _, wspec, bspec = _conv_specs(seq)

    def body(da_ref, x_ref, h_ref, w_ref, b_ref, dp_ref, s_ref):
        b, i = pl.program_id(1), pl.program_id(2)

        @pl.when((b == 0) & (i == 0))
        def _():
            s_ref[...] = jnp.zeros_like(s_ref)

        halo = jnp.where(i > 0, h_ref[...], 0.0)
        x = x_ref[...]
        pre = _conv_pre(x, halo, w_ref, b_ref)
        sg = _sigmoid(pre)
        dpre = da_ref[...] * (sg * (1.0 + pre * (1.0 - sg)))
        dp_ref[...] = dpre
        s_ref[3:4, :] += jnp.sum(dpre * x, 0, keepdims=True)
        for k in range(1, 4):
            s_ref[3 - k:4 - k, :] += jnp.sum(dpre * _shift_down(x, halo, k), 0, keepdims=True)
        s_ref[4:5, :] += jnp.sum(dpre, 0, keepdims=True)

    return _pcall(
        body, name=name, grid=(ch // CONV_TC, nb, seq // CONV_TM),
        in_specs=[cur, cur, prev, wspec, bspec],
        out_specs=[cur, pl.BlockSpec((8, CONV_TC), lambda cb, b, i: (0, cb))],
        out_shape=[jax.ShapeDtypeStruct(xin.shape, F32), jax.ShapeDtypeStruct((8, ch), F32)],
        compiler_params=_params("parallel", "arbitrary", "arbitrary"),
    )(dact, xin, xin, w4, bias)


def _conv_bwd_x(dpre, w4, name):
    nb, seq, ch = dpre.shape
    nblk, cur, _, nxt, wspec, _ = _conv_specs(seq)

    def body(d_ref, n_ref, w_ref, o_ref):
        halo = jnp.where(pl.program_id(2) < nblk - 1, n_ref[...], 0.0)
        cur_v = d_ref[...]
        acc = cur_v * w_ref[3:4, :]
        for j in range(1, 4):
            acc = acc + _shift_up(cur_v, halo, j) * w_ref[3 - j:4 - j, :]
        o_ref[...] = acc.astype(BF16)

    return _pcall(
        body, name=name, grid=(ch // CONV_TC, nb, seq // CONV_TM),
        in_specs=[cur, nxt, wspec], out_specs=cur,
        out_shape=jax.ShapeDtypeStruct(dpre.shape, BF16),
        compiler_params=_params("parallel", "parallel", "parallel"),
    )(dpre, dpre, w4)


def _softplus_sig(dt_raw, dt_bias_row):
    nb, seq, _ = dt_raw.shape
    tm = 512

    def body(r_ref, b_ref, sp_ref, sg_ref):
        v = r_ref[...] + b_ref[...]
        sp_ref[...] = jnp.maximum(v, 0.0) + jnp.log1p(jnp.exp(-jnp.abs(v)))
        sg_ref[...] = _sigmoid(v)

    spec = pl.BlockSpec((None, tm, LANE), lambda b, i: (b, i, 0))
    return _pcall(
        body, name="dt_softplus", grid=(nb, seq // tm),
        in_specs=[spec, pl.BlockSpec((1, LANE), lambda b, i: (0, 0))], out_specs=[spec, spec],
        out_shape=[jax.ShapeDtypeStruct(dt_raw.shape, F32)] * 2,
        compiler_params=_params("parallel", "parallel"),
    )(dt_raw, dt_bias_row)


def _group_lanes(t):
    pads = [(0, 0)] * (t.ndim - 1) + [(0, LANE - GROUP_SSM_HEADS)]
    return jnp.stack([jnp.pad(t[..., GROUP_SSM_HEADS * g:GROUP_SSM_HEADS * (g + 1)], pads) for g in range(SSM_GROUPS)])


def _ungroup_lanes(t):
    return jnp.concatenate([t[g][..., :GROUP_SSM_HEADS] for g in range(SSM_GROUPS)], axis=-1)


def _decays(dt, al_ref):
    row = lax.broadcasted_iota(jnp.int32, (CHUNK, CHUNK), 0)
    col = lax.broadcasted_iota(jnp.int32, (CHUNK, CHUNK), 1)
    tril = (row >= col).astype(BF16)
    triu = (row <= col).astype(BF16)
    arow = -jnp.exp(al_ref[...])
    hi, mid, lo = _split3(dt * arow)
    down = lambda t: jnp.dot(tril, t, preferred_element_type=F32)
    across = lambda t: lax.dot_general(t, triu, TN, preferred_element_type=F32)
    acs = (down(hi) + down(mid)) + down(lo)
    acs_t = (across(hi) + across(mid)) + across(lo)
    return arow, acs, acs_t, row >= col, triu


def _ssd_specs(nb, seq):
    nc = seq // CHUNK
    hw = GROUP_SSM_HEADS * HEAD_DIM

    def mk(rev):
        cidx = (lambda c: nc - 1 - c) if rev else (lambda c: c)
        wide = pl.BlockSpec((None, CHUNK, hw), lambda g, b, c: (b, cidx(c), g))
        xbc = pl.BlockSpec((None, CHUNK, XBC_GROUP), lambda g, b, c: (b, cidx(c), g))
        lanes = pl.BlockSpec((None, None, CHUNK, LANE), lambda g, b, c: (g, b, cidx(c), 0))
        prev = pl.BlockSpec((None, None, None, D_STATE, hw), lambda g, b, c: (b, cidx(c), g, 0, 0))
        return wide, xbc, lanes, prev

    grow = pl.BlockSpec((None, 1, LANE), lambda g, b, c: (g, 0, 0))
    nwspec = pl.BlockSpec((1, hw), lambda g, b, c: (0, g))
    return nc, hw, mk, grow, nwspec


def _head_expand():
    hw = GROUP_SSM_HEADS * HEAD_DIM
    r = lax.broadcasted_iota(jnp.int32, (LANE, hw), 0)
    c = lax.broadcasted_iota(jnp.int32, (LANE, hw), 1)
    return ((c // HEAD_DIM) == r).astype(BF16)


def _split3(v):
    hi = v.astype(BF16)
    rest = v - hi.astype(F32)
    mid = rest.astype(BF16)
    return hi, mid, (rest - mid.astype(F32)).astype(BF16)


def _to_channels(v, e):
    hi, mid, lo = _split3(v)
    dot = lambda t: jnp.dot(t, e, preferred_element_type=F32)
    return (dot(hi) + dot(mid)) + dot(lo)


def _to_heads(w, e):
    hi, mid, lo = _split3(w)
    dot = lambda t: lax.dot_general(t, e, (((1,), (1,)), ((), ())), preferred_element_type=F32)
    return (dot(hi) + dot(mid)) + dot(lo)


def _row8(v):
    return jnp.broadcast_to(v, (8, v.shape[1]))


def _ssd_chunk_setup(dt, al_ref, ds_ref):
    arow, acs, acs_t, causal, triu = _decays(dt, al_ref)
    e = _head_expand()
    dtx = _to_channels(dt, e)
    acsx = _to_channels(acs, e)
    lastx = acsx[CHUNK - 1:CHUNK, :]
    dskx = _to_channels(_row8(ds_ref[...]), e)[0:1, :]
    return arow, acs, acs_t, causal, triu, e, dtx, acsx, lastx, dskx


def _ssd_fwd(xbc, dtg, z, alog_g, dskip_g, normw):
    nb, seq, _ = xbc.shape
    nc, hw, mk, grow, nwspec = _ssd_specs(nb, seq)
    wide, xbc_spec, lanes, prev = mk(False)
    tn = (((0,), (0,)), ((), ()))

    def body(xbc_ref, dt_ref, z_ref, al_ref, ds_ref, nw_ref, ys_ref, y_ref, sp_ref, st_ref):
        @pl.when(pl.program_id(2) == 0)
        def _():
            st_ref[...] = jnp.zeros_like(st_ref)

        dt = dt_ref[...]
        _, acs, acs_t, causal, _, _, dtx, acsx, lastx, dskx = _ssd_chunk_setup(dt, al_ref, ds_ref)
        bmat = xbc_ref[:, GROUP_CH:GROUP_CH + D_STATE].astype(BF16)
        cmat = xbc_ref[:, GROUP_CH + D_STATE:].astype(BF16)
        cb = lax.dot_general(cmat, bmat, (((1,), (1,)), ((), ())), preferred_element_type=F32)
        x = xbc_ref[:, :GROUP_CH]
        xdt = x * dtx
        xdt16 = xdt.astype(BF16)
        first_head = lax.broadcasted_iota(jnp.int32, (CHUNK, LANE), 1) < HEAD_DIM
        pairs = []
        for hp in range(GROUP_SSM_HEADS // 2):
            xp = xdt16[:, hp * LANE:(hp + 1) * LANE]
            two = []
            for j in (2 * hp, 2 * hp + 1):
                lmat = jnp.exp(jnp.where(causal, acs[:, j:j + 1] - acs_t[j:j + 1, :], -jnp.inf))
                two.append(jnp.dot((cb * lmat).astype(BF16), xp, preferred_element_type=F32))
            pairs.append(jnp.where(first_head, two[0], two[1]))
        yd = jnp.concatenate(pairs, axis=1)
        s_prev = st_ref[...]
        s16 = s_prev.astype(BF16)
        sp_ref[...] = s16
        yo = jnp.dot(cmat, s16, preferred_element_type=F32) * jnp.exp(acsx)
        sts = lax.dot_general(bmat, (xdt * jnp.exp(lastx - acsx)).astype(BF16), tn, preferred_element_type=F32)
        st_ref[...] = s_prev * jnp.exp(lastx) + sts
        y = yd + yo + dskx * x
        zz = z_ref[...]
        u = y * (zz * _sigmoid(zz))
        rn = lax.rsqrt(jnp.mean(u * u, -1, keepdims=True) + RMS_EPS)
        ys_ref[...] = (u * rn * nw_ref[...]).astype(BF16)
        y_ref[...] = y

    return _pcall(
        body, name="ssd_fwd", grid=(SSM_GROUPS, nb, nc),
        in_specs=[xbc_spec, lanes, wide, grow, grow, nwspec],
        out_specs=[wide, wide, prev],
        out_shape=[jax.ShapeDtypeStruct((nb, seq, D_INNER), BF16), jax.ShapeDtypeStruct((nb, seq, D_INNER), F32),
                   jax.ShapeDtypeStruct((nb, nc, SSM_GROUPS, D_STATE, hw), BF16)],
        scratch_shapes=[pltpu.VMEM((D_STATE, hw), F32)],
        compiler_params=_params("parallel", "parallel", "arbitrary"),
    )(xbc, dtg, z, alog_g, dskip_g, normw)


def _ssd_bwd(xbc, dtg, sgg, z, y, dys, sprev, alog_g, dskip_g, normw):
    nb, seq, _ = xbc.shape
    nc, hw, mk, grow, nwspec = _ssd_specs(nb, seq)
    wide, xbc_spec, lanes, prev = mk(True)
    nt = (((1,), (1,)), ((), ()))
    tn = (((0,), (0,)), ((), ()))

    def body(xbc_ref, dt_ref, sg_ref, z_ref, y_ref, dys_ref, sp_ref, al_ref, ds_ref, nw_ref,
             dxbc_ref, ddt_ref, dz_ref, small_ref, dnw_ref, g_ref):
        b, c = pl.program_id(1), pl.program_id(2)

        @pl.when((b == 0) & (c == 0))
        def _():
            small_ref[...] = jnp.zeros_like(small_ref)
            dnw_ref[...] = jnp.zeros_like(dnw_ref)

        @pl.when(c == 0)
        def _():
            g_ref[...] = jnp.zeros_like(g_ref)

        yv, zz, dys_v, nw = y_ref[...], z_ref[...], dys_ref[...], nw_ref[...]
        sz = _sigmoid(zz)
        silu = zz * sz
        u = yv * silu
        rn = lax.rsqrt(jnp.mean(u * u, -1, keepdims=True) + RMS_EPS)
        gn = dys_v * nw
        du = rn * gn - u * (rn * rn * rn) * jnp.mean(u * gn, -1, keepdims=True)
        dnw_ref[...] += jnp.sum(dys_v * u * rn, 0, keepdims=True)
        dy = du * silu
        dz_ref[...] = du * yv * (sz * (1.0 + zz * (1.0 - sz)))

        dt = dt_ref[...]
        arow, acs, acs_t, causal, triu, e, dtx, acsx, lastx, dskx = _ssd_chunk_setup(dt, al_ref, ds_ref)
        dfsx = jnp.exp(acsx)
        dtex = jnp.exp(lastx - acsx)
        bmat = xbc_ref[:, GROUP_CH:GROUP_CH + D_STATE].astype(BF16)
        cmat = xbc_ref[:, GROUP_CH + D_STATE:].astype(BF16)
        cb = lax.dot_general(cmat, bmat, nt, preferred_element_type=F32)
        x = xbc_ref[:, :GROUP_CH]
        xdt = x * dtx
        xdt16 = xdt.astype(BF16)
        xdte = xdt * dtex
        dy16 = dy.astype(BF16)
        dyd = dy * dfsx
        dyd16 = dyd.astype(BF16)
        s16 = sp_ref[...]
        g = g_ref[...]
        g16 = g.astype(BF16)
        cs = jnp.dot(cmat, s16, preferred_element_type=F32)
        dc_off = lax.dot_general(dyd16, s16, nt, preferred_element_type=F32)
        g_here = lax.dot_general(cmat, dyd16, tn, preferred_element_type=F32)
        bg = jnp.dot(bmat, g16, preferred_element_type=F32)
        db_st = lax.dot_general(xdte.astype(BF16), g16, nt, preferred_element_type=F32)
        ddte_w = bg * xdte
        dcd = _to_heads(_row8(jnp.sum(g * s16.astype(F32), 0, keepdims=True)), e)[0:1, :]
        lane = lax.broadcasted_iota(jnp.int32, (CHUNK, LANE), 1)
        first_head = lane < HEAD_DIM
        sub = lax.broadcasted_iota(jnp.int32, (CHUNK, LANE), 0)
        dacs = jnp.zeros((CHUNK, LANE), F32)
        colsums = jnp.zeros((CHUNK, LANE), F32)
        dcb = jnp.zeros((CHUNK, CHUNK), F32)
        pairs = []
        for hp in range(GROUP_SSM_HEADS // 2):
            xp = xdt16[:, hp * LANE:(hp + 1) * LANE]
            dyp = dy16[:, hp * LANE:(hp + 1) * LANE]
            two = []
            for idx, j in enumerate((2 * hp, 2 * hp + 1)):
                lmat = jnp.exp(jnp.where(causal, acs[:, j:j + 1] - acs_t[j:j + 1, :], -jnp.inf))
                mf = cb * lmat
                dy_h = jnp.where(first_head if idx == 0 else jnp.logical_not(first_head), dyp, jnp.zeros_like(dyp))
                dm = lax.dot_general(dy_h, xp, nt, preferred_element_type=F32)
                two.append(lax.dot_general(mf.astype(BF16), dyp, tn, preferred_element_type=F32))
                wmat = dm * mf
                dcb = dcb + dm * lmat
                dacs = jnp.where(lane == j, jnp.sum(wmat, -1, keepdims=True), dacs)
                colsums = jnp.where(sub == j, jnp.sum(wmat, 0, keepdims=True), colsums)
            pairs.append(jnp.where(first_head, two[0], two[1]))
        dxdt = bg * dtex + jnp.concatenate(pairs, axis=1)
        dacs = dacs - colsums.T + _to_heads(dyd * cs - ddte_w, e)
        cd_row = jnp.exp(acs[CHUNK - 1:CHUNK, :])
        tail = _to_heads(_row8(jnp.sum(ddte_w, 0, keepdims=True)), e)[0:1, :] + dcd * cd_row
        dacs = dacs + jnp.where(sub == CHUNK - 1, tail, 0.0)
        d_hi, d_mid, d_lo = _split3(dacs)
        up = lambda t: jnp.dot(triu, t, preferred_element_type=F32)
        da = (up(d_hi) + up(d_mid)) + up(d_lo)
        ddt_raw = (da * arow + _to_heads(dxdt * x, e)) * sg_ref[...]
        ddt_ref[...] = ddt_raw
        small_ref[0:1, :] += jnp.sum(da * dt, 0, keepdims=True) * arow
        small_ref[1:2, :] += _to_heads(_row8(jnp.sum(dy * x, 0, keepdims=True)), e)[0:1, :]
        small_ref[2:3, :] += jnp.sum(ddt_raw, 0, keepdims=True)
        dcb16 = dcb.astype(BF16)
        dxbc_ref[:, GROUP_CH + D_STATE:] = dc_off + jnp.dot(dcb16, bmat, preferred_element_type=F32)
        dxbc_ref[:, GROUP_CH:GROUP_CH + D_STATE] = db_st + lax.dot_general(dcb16, cmat, tn,
                                                                            preferred_element_type=F32)
        dxbc_ref[:, :GROUP_CH] = dxdt * dtx + dskx * dy
        g_ref[...] = g * jnp.exp(lastx) + g_here

    return _pcall(
        body, name="ssd_bwd", grid=(SSM_GROUPS, nb, nc),
        in_specs=[xbc_spec, lanes, lanes, wide, wide, wide, prev, grow, grow, nwspec],
        out_specs=[xbc_spec, lanes, wide,
                   pl.BlockSpec((None, 8, LANE), lambda g, b, c: (g, 0, 0)), nwspec],
        out_shape=[jax.ShapeDtypeStruct((nb, seq, CONV_DIM), F32),
                   jax.ShapeDtypeStruct((SSM_GROUPS, nb, seq, LANE), F32),
                   jax.ShapeDtypeStruct((nb, seq, D_INNER), F32),
                   jax.ShapeDtypeStruct((SSM_GROUPS, 8, LANE), F32),
                   jax.ShapeDtypeStruct((1, D_INNER), F32)],
        scratch_shapes=[pltpu.VMEM((D_STATE, hw), F32)],
        compiler_params=_params("parallel", "arbitrary", "arbitrary"),
    )(xbc, dtg, sgg, z, y, dys, sprev, alog_g, dskip_g, normw)


EW_TM = 256


def _merge_fwd(y_a, y_b, gm, bgate):
    nb, seq, _ = y_a.shape

    def body(a_ref, b_ref, ga_ref, gb_ref, bg_ref, o_ref):
        sa = _sigmoid(ga_ref[...] + bg_ref[0:1, :])
        sb = _sigmoid(gb_ref[...] + bg_ref[1:2, :])
        o_ref[...] = (sa * a_ref[...] + sb * b_ref[...]).astype(BF16)

    spec = pl.BlockSpec((None, EW_TM, D_MODEL), lambda b, i: (b, i, 0))
    spec1 = pl.BlockSpec((None, EW_TM, D_MODEL), lambda b, i: (b, i, 1))
    return _pcall(
        body, name="merge_fwd", grid=(nb, seq // EW_TM),
        in_specs=[spec, spec, spec, spec1, pl.BlockSpec((8, D_MODEL), lambda b, i: (0, 0))], out_specs=spec,
        out_shape=jax.ShapeDtypeStruct((nb, seq, D_MODEL), BF16),
        compiler_params=_params("parallel", "parallel"),
    )(y_a, y_b, gm, gm, bgate)


def _merge_bwd(dmerged, y_a, y_b, gm, bgate):
    nb, seq, _ = y_a.shape

    def body(dm_ref, a_ref, b_ref, ga_ref, gb_ref, bg_ref, dya_ref, dyb_ref, dg_ref, s_ref):
        @pl.when((pl.program_id(0) == 0) & (pl.program_id(1) == 0))
        def _():
            s_ref[...] = jnp.zeros_like(s_ref)

        dm = dm_ref[...]
        sa = _sigmoid(ga_ref[...] + bg_ref[0:1, :])
        sb = _sigmoid(gb_ref[...] + bg_ref[1:2, :])
        dya_ref[...] = (dm * sa).astype(BF16)
        dyb_ref[...] = (dm * sb).astype(BF16)
        dga = dm * a_ref[...] * (sa * (1.0 - sa))
        dgb = dm * b_ref[...] * (sb * (1.0 - sb))
        dg_ref[:, :D_MODEL] = dga.astype(BF16)
        dg_ref[:, D_MODEL:] = dgb.astype(BF16)
        s_ref[0:1, :] += jnp.sum(dga, 0, keepdims=True)
        s_ref[1:2, :] += jnp.sum(dgb, 0, keepdims=True)

    spec = pl.BlockSpec((None, EW_TM, D_MODEL), lambda b, i: (b, i, 0))
    spec1 = pl.BlockSpec((None, EW_TM, D_MODEL), lambda b, i: (b, i, 1))
    small = pl.BlockSpec((8, D_MODEL), lambda b, i: (0, 0))
    return _pcall(
        body, name="merge_bwd", grid=(nb, seq // EW_TM),
        in_specs=[spec, spec, spec, spec, spec1, small],
        out_specs=[spec, spec, pl.BlockSpec((None, EW_TM, 2 * D_MODEL), lambda b, i: (b, i, 0)), small],
        out_shape=[jax.ShapeDtypeStruct((nb, seq, D_MODEL), BF16), jax.ShapeDtypeStruct((nb, seq, D_MODEL), BF16),
                   jax.ShapeDtypeStruct((nb, seq, 2 * D_MODEL), BF16), jax.ShapeDtypeStruct((8, D_MODEL), F32)],
        compiler_params=_params("arbitrary", "arbitrary"),
    )(dmerged, y_a, y_b, gm, gm, bgate)


def _ln_loss(x, mix, gp, pw, target, bgate, ln_g, ln_b):
    nb, seq, _ = x.shape

    def body(x_ref, mix_ref, gp_ref, pw_ref, t_ref, bg_ref, g_ref, b_ref, dx_ref, dp_ref, dpw_ref, dgp_ref, s_ref):
        @pl.when((pl.program_id(0) == 0) & (pl.program_id(1) == 0))
        def _():
            s_ref[...] = jnp.zeros_like(s_ref)

        sp = _sigmoid(gp_ref[...] + bg_ref[2:3, :])
        pw = pw_ref[...]
        pre = ALPHA * x_ref[...] + mix_ref[...] + sp * pw
        mu = jnp.mean(pre, -1, keepdims=True)
        cen = pre - mu
        rstd = lax.rsqrt(jnp.mean(cen * cen, -1, keepdims=True) + LN_EPS)
        xhat = cen * rstd
        err = xhat * g_ref[...] + b_ref[...] - t_ref[...]
        dy = err * (1.0 / D_MODEL)
        dxh = dy * g_ref[...]
        dpre = rstd * (dxh - jnp.mean(dxh, -1, keepdims=True) - xhat * jnp.mean(dxh * xhat, -1, keepdims=True))
        dx_ref[...] = ALPHA * dpre
        dp_ref[...] = dpre.astype(BF16)
        dpw_ref[...] = (dpre * sp).astype(BF16)
        dgp = dpre * pw * (sp * (1.0 - sp))
        dgp_ref[...] = dgp.astype(BF16)
        s_ref[0:1, :] += jnp.sum(dy * xhat, 0, keepdims=True)
        s_ref[1:2, :] += jnp.sum(dy, 0, keepdims=True)
        s_ref[2:3, :] += jnp.sum(dgp, 0, keepdims=True)
        s_ref[3:4, :] += jnp.sum(err * err, 0, keepdims=True)

    spec = pl.BlockSpec((None, EW_TM, D_MODEL), lambda b, i: (b, i, 0))
    small = pl.BlockSpec((8, D_MODEL), lambda b, i: (0, 0))
    row = pl.BlockSpec((1, D_MODEL), lambda b, i: (0, 0))
    return _pcall(
        body, name="ln_loss", grid=(nb, seq // EW_TM),
        in_specs=[spec] * 5 + [small, row, row], out_specs=[spec] * 4 + [small],
        out_shape=[jax.ShapeDtypeStruct((nb, seq, D_MODEL), F32)] + [jax.ShapeDtypeStruct((nb, seq, D_MODEL), BF16)] * 3
        + [jax.ShapeDtypeStruct((8, D_MODEL), F32)],
        compiler_params=_params("arbitrary", "arbitrary"),
    )(x, mix, gp, pw, target, bgate, ln_g, ln_b)


def _adamw(w, g, m, v, name):
    rows, cols = w.shape
    tr = _row_tile(rows, cols, 8, 5 << 19)
    c1 = 1.0 - ADAM_B1 ** ADAM_STEP
    c2 = 1.0 - ADAM_B2 ** ADAM_STEP

    def body(w_ref, g_ref, m_ref, v_ref, d_ref, nm_ref, nv_ref):
        gv = g_ref[...]
        nm = ADAM_B1 * m_ref[...] + (1.0 - ADAM_B1) * gv
        nv = ADAM_B2 * v_ref[...] + (1.0 - ADAM_B2) * (gv * gv)
        d_ref[...] = -ADAM_LR * ((nm / c1) / (jnp.sqrt(nv / c2) + ADAM_EPS) + ADAM_WD * w_ref[...])
        nm_ref[...] = nm
        nv_ref[...] = nv

    spec = pl.BlockSpec((tr, cols), lambda i: (i, 0))
    return _pcall(
        body, name=name, grid=(rows // tr,), in_specs=[spec] * 4, out_specs=[spec] * 3,
        out_shape=[jax.ShapeDtypeStruct(w.shape, F32)] * 3, compiler_params=_params("parallel"),
    )(w, g, m, v)


def _sum_rows(parts, out_dtype, name):
    rows, cols = parts[0].shape
    tr = rows
    for cand in range(16, rows, 16):
        if rows % cand == 0 and cand * cols * 4 <= (1 << 20):
            tr = cand
    n = len(parts)

    def body(*refs):
        acc = refs[0][...].astype(F32)
        for r in refs[1:n]:
            acc = acc + r[...].astype(F32)
        refs[n][...] = acc.astype(out_dtype)

    spec = pl.BlockSpec((tr, cols), lambda i: (i, 0))
    return _pcall(
        body, name=name, grid=(rows // tr,), in_specs=[spec] * n, out_specs=spec,
        out_shape=jax.ShapeDtypeStruct((rows, cols), out_dtype), compiler_params=_params("parallel"),
    )(*parts)


def _place():
    return lax.axis_index("x"), lax.axis_index("y"), lax.axis_index("c")


def _other_chips(x, y):
    return [(1 - x, y), (x, 1 - y), (1 - x, 1 - y)]


def _remote(src, dst, send_sem, recv_sem, to):
    return pltpu.make_async_remote_copy(src_ref=src, dst_ref=dst, send_sem=send_sem, recv_sem=recv_sem,
                                        device_id=to, device_id_type=MESH)


ANY = pl.BlockSpec(memory_space=pl.ANY)
DMA_CHUNK_BYTES = 512 * 1024


def _row_chunks(rows, row_bytes):
    per = max(16, DMA_CHUNK_BYTES // row_bytes // 16 * 16)
    return [(s, min(per, rows - s)) for s in range(0, rows, per)]


def _row_tile(rows, cols, align, limit=1 << 21):
    best = None
    for cand in range(align, rows + 1, align):
        if rows % cand == 0 and cand * cols * 4 <= limit:
            best = cand
    return best or rows


def _allgather_pieces(pieces):
    n = len(pieces)
    halves = [_row_chunks(p.shape[0] // 2, p.shape[1] * p.dtype.itemsize) for p in pieces]
    wholes = [_row_chunks(p.shape[0], p.shape[1] * p.dtype.itemsize) for p in pieces]
    n_ici = 3 * sum(len(h) for h in halves)
    n_loc = sum(len(w) for w in wholes)

    def body(*refs):
        ins, outs = refs[:n], refs[n:2 * n]
        send_sems, recv_sems, local_sems = refs[2 * n:]
        x, y, c = _place()
        me = 2 * x + y
        sibling = (x, y, 1 - c)
        chips = _other_chips(x, y)
        locals_ = []
        for a in range(n):
            for s, m in wholes[a]:
                loc = pltpu.make_async_copy(ins[a].at[pl.ds(s, m)], outs[a].at[me, pl.ds(s, m)],
                                            local_sems.at[len(locals_)])
                loc.start()
                locals_.append(loc)
        ici = []
        for a in range(n):
            half = ins[a].shape[0] // 2
            for s, m in halves[a]:
                for j, (cx, cy) in enumerate(chips):
                    k = len(ici)
                    cp = _remote(ins[a].at[pl.ds(c * half + s, m)], outs[a].at[me, pl.ds(c * half + s, m)],
                                 send_sems.at[k], recv_sems.at[k], (cx, cy, c))
                    cp.start()
                    ici.append((a, s, m, j, cp))
        passed = []
        for k, (a, s, m, j, _) in enumerate(ici):
            half = ins[a].shape[0] // 2
            cx, cy = chips[j]
            blk = outs[a].at[2 * cx + cy, pl.ds(c * half + s, m)]
            _remote(blk, blk, send_sems.at[k], recv_sems.at[k], (cx, cy, c)).wait_recv()
            fw = _remote(blk, blk, send_sems.at[n_ici + k], recv_sems.at[n_ici + k], sibling)
            fw.start()
            passed.append(fw)
        for k, (a, s, m, j, _) in enumerate(ici):
            half = ins[a].shape[0] // 2
            cx, cy = chips[j]
            blk = outs[a].at[2 * cx + cy, pl.ds((1 - c) * half + s, m)]
            _remote(blk, blk, send_sems.at[n_ici + k], recv_sems.at[n_ici + k], sibling).wait_recv()
        for item in ici:
            item[4].wait_send()
        for fw in passed:
            fw.wait_send()
        for loc in locals_:
            loc.wait()

    return _pcall(
        body, name="allgather_weights", in_specs=[ANY] * n, out_specs=[ANY] * n,
        out_shape=[jax.ShapeDtypeStruct((4,) + p.shape, p.dtype) for p in pieces],
        scratch_shapes=[pltpu.SemaphoreType.DMA((2 * n_ici,)), pltpu.SemaphoreType.DMA((2 * n_ici,)),
                        pltpu.SemaphoreType.DMA((n_loc,))],
        compiler_params=pltpu.CompilerParams(has_side_effects=True),
    )(*pieces)


def _sibling_exchange(grads):
    n = len(grads)
    chunks = [_row_chunks(g.shape[1] // 2, g.shape[2] * g.dtype.itemsize) for g in grads]
    n_sem = 4 * sum(len(ch) for ch in chunks)

    def body(*refs):
        ins, gots = refs[:n], refs[n:2 * n]
        send_sems, recv_sems = refs[2 * n:]
        x, y, c = _place()
        sibling = (x, y, 1 - c)
        work = []
        for a in range(n):
            half = ins[a].shape[1] // 2
            for piece in range(4):
                for s, m in chunks[a]:
                    k = len(work)
                    cp = _remote(ins[a].at[piece, pl.ds((1 - c) * half + s, m)], gots[a].at[piece, pl.ds(s, m)],
                                 send_sems.at[k], recv_sems.at[k], sibling)
                    cp.start()
                    work.append(cp)
        for cp in work:
            cp.wait()

    return _pcall(
        body, name="grad_sibling_exchange", in_specs=[ANY] * n, out_specs=[ANY] * n,
        out_shape=[jax.ShapeDtypeStruct((4, g.shape[1] // 2, g.shape[2]), g.dtype) for g in grads],
        scratch_shapes=[pltpu.SemaphoreType.DMA((n_sem,)), pltpu.SemaphoreType.DMA((n_sem,))],
        compiler_params=pltpu.CompilerParams(has_side_effects=True),
    )(*grads)


def _sibling_gather(fulls):
    n = len(fulls)
    chunks = [_row_chunks(f.shape[0] // 2, f.shape[1] * f.dtype.itemsize) for f in fulls]
    n_sem = sum(len(ch) for ch in chunks)

    def body(*refs):
        outs = refs[n:2 * n]
        send_sems, recv_sems = refs[2 * n:]
        x, y, c = _place()
        sibling = (x, y, 1 - c)
        work = []
        for a in range(n):
            h = outs[a].shape[0] // 2
            for s, m in chunks[a]:
                k = len(work)
                mine = outs[a].at[pl.ds(c * h + s, m)]
                cp = _remote(mine, mine, send_sems.at[k], recv_sems.at[k], sibling)
                cp.start()
                work.append((a, s, m, cp))
        for k, (a, s, m, cp) in enumerate(work):
            h = outs[a].shape[0] // 2
            cp.wait_send()
            theirs = outs[a].at[pl.ds((1 - c) * h + s, m)]
            _remote(theirs, theirs, send_sems.at[k], recv_sems.at[k], sibling).wait_recv()

    return _pcall(
        body, name="grad_sibling_gather", in_specs=[ANY] * n, out_specs=[ANY] * n,
        out_shape=[jax.ShapeDtypeStruct(f.shape, f.dtype) for f in fulls],
        input_output_aliases={a: a for a in range(n)},
        scratch_shapes=[pltpu.SemaphoreType.DMA((n_sem,)), pltpu.SemaphoreType.DMA((n_sem,))],
        compiler_params=pltpu.CompilerParams(has_side_effects=True),
    )(*fulls)


def _pair_sum(grad, got, place, name):
    _, rows, cols = grad.shape
    half = rows // 2
    tr = _row_tile(half, cols, 16)

    def body(p_ref, a_ref, b_ref, o_ref):
        o_ref[...] = (a_ref[...].astype(F32) + b_ref[...].astype(F32)).astype(BF16)

    return _pcall(
        body, name=name,
        grid_spec=pltpu.PrefetchScalarGridSpec(
            num_scalar_prefetch=1, grid=(4, half // tr),
            in_specs=[pl.BlockSpec((None, tr, cols), lambda k, i, p: (k, p[1] * (half // tr) + i, 0)),
                      pl.BlockSpec((None, tr, cols), lambda k, i, p: (k, i, 0))],
            out_specs=pl.BlockSpec((None, tr, cols), lambda k, i, p: (k, i, 0))),
        out_shape=jax.ShapeDtypeStruct((4, half, cols), BF16),
        compiler_params=_params("parallel", "parallel"),
    )(place, grad, got)


def _chip_sum(sums, got, place, name):
    _, h, cols = sums.shape
    tr = _row_tile(h, cols, 16)

    def body(p_ref, own_ref, g0, g1, g2, o_ref):
        o_ref[...] = ((own_ref[...].astype(F32) + g0[...].astype(F32)) + g1[...].astype(F32)) + g2[...].astype(F32)

    gspec = lambda j: pl.BlockSpec((None, tr, cols), lambda i, p: (j, i, 0))
    return _pcall(
        body, name=name,
        grid_spec=pltpu.PrefetchScalarGridSpec(
            num_scalar_prefetch=1, grid=(h // tr,),
            in_specs=[pl.BlockSpec((None, tr, cols), lambda i, p: (p[0], i, 0)), gspec(0), gspec(1), gspec(2)],
            out_specs=pl.BlockSpec((tr, cols), lambda i, p: (p[1] * (h // tr) + i, 0))),
        out_shape=jax.ShapeDtypeStruct((2 * h, cols), F32),
        compiler_params=_params("parallel"),
    )(place, sums, got, got, got)


def _allgather8(buf, name):
    rows = buf.shape[0]

    def body(in_ref, out_ref, send_sems, recv_sems):
        x, y, c = _place()
        me = 4 * x + 2 * y + c
        out_ref[me] = in_ref[...]
        work = []
        for rel in range(1, 8):
            fx, fy, fc = (rel >> 2) & 1, (rel >> 1) & 1, rel & 1
            to = (x ^ fx, y ^ fy, c ^ fc)
            cp = _remote(in_ref, out_ref.at[me], send_sems.at[rel - 1], recv_sems.at[rel - 1], to)
            cp.start()
            work.append((cp, 4 * to[0] + 2 * to[1] + to[2]))
        for rel, (cp, frm) in enumerate(work):
            cp.wait_send()
            blk = out_ref.at[frm]
            _remote(blk, blk, send_sems.at[rel], recv_sems.at[rel], (x, y, c)).wait_recv()

    return _pcall(
        body, name=name, in_specs=[pl.BlockSpec(memory_space=pltpu.VMEM)],
        out_specs=pl.BlockSpec(memory_space=pltpu.VMEM),
        out_shape=jax.ShapeDtypeStruct((8, rows, LANE), F32),
        scratch_shapes=[pltpu.SemaphoreType.DMA((7,)), pltpu.SemaphoreType.DMA((7,))],
        compiler_params=pltpu.CompilerParams(has_side_effects=True),
    )(buf)


def _pack_rows(arrs):
    parts = []
    for a in arrs:
        f = a.reshape(-1).astype(F32)
        parts.append(jnp.pad(f, (0, (-f.shape[0]) % LANE)))
    flat = jnp.concatenate(parts)
    rows = -(-flat.shape[0] // LANE)
    rows8 = -(-rows // 8) * 8
    return jnp.pad(flat, (0, rows8 * LANE - flat.shape[0])).reshape(rows8, LANE)


def _unpack_rows(buf, shapes):
    flat = buf.reshape(-1)
    outs, off = [], 0
    for s in shapes:
        n = int(np.prod(s))
        outs.append(flat[off:off + n].reshape(s))
        off += -(-n // LANE) * LANE
    return outs


def _local_grads(x, p, target, wseg, w_br16, w_out16, w_ple16, b_gate, conv_w, conv_b, dt_bias, a_log, d_skip,
                 ssm_norm_w, ln_g, ln_b, rel_bias, finish_dx):
    nb, seq, _ = x.shape
    bmaps = jnp.asarray(_bucket_maps())
    bias = _bias_tables(rel_bias, bmaps)
    bgate8 = jnp.pad(b_gate, ((0, 5), (0, 0)))
    dils = [d for _, d in PATTERNS]

    x16 = x.astype(BF16)
    p16 = p.astype(BF16)
    x16p = [_permute(x16, d) for d in dils]
    qkv = [_proj(x16p[g], [wseg["qkv%d" % g]], BF16, "proj_qkv%d" % g, True)[0].reshape(
        nb, dils[g], seq // dils[g], -1) for g in range(3)]
    nat = {}
    for gi, (group, tm) in enumerate(NAT_GROUPS):
        outs = _proj(x16, [wseg[s] for s in group], F32, "proj_nat%d" % gi, True, tm)
        nat.update(zip(group, outs))
    att = [_attn_fwd(qkv[g], bias[g * GROUP_HEADS:(g + 1) * GROUP_HEADS], dils[g], "attn_fwd%d" % g) for g in range(3)]
    natural = lambda t, g: _unpermute(t.reshape(nb, seq, t.shape[-1]), dils[g])
    oa, o_att, lse = _combine_fwd(att[0][0], att[0][1], natural(att[1][0], 1), natural(att[1][1], 1),
                                  natural(att[2][0], 2), natural(att[2][1], 2), nat["gatt"])

    conv_wg, conv_bg = _xbc_group_order(conv_w), _xbc_group_order(conv_b)
    act = _conv_fwd(nat["xbc"], conv_wg, conv_bg, "conv_fwd")
    dt_sp, dt_sg = _softplus_sig(nat["dt"], jnp.pad(dt_bias, ((0, 0), (0, LANE - SSM_HEADS))))
    dtg, sgg = _group_lanes(dt_sp), _group_lanes(dt_sg)
    alog_g, dskip_g = _group_lanes(a_log), _group_lanes(d_skip)
    y_ssm, y_all, sprev = _ssd_fwd(act, dtg, nat["z"], alog_g, dskip_g, ssm_norm_w)

    w_bra, w_brb = w_br16[:ATT_OUT], w_br16[ATT_OUT:]
    y_a, = _proj(oa, [w_bra], F32, "proj_ya")
    y_b, = _proj(y_ssm, [w_brb], F32, "proj_yb")
    merged = _merge_fwd(y_a, y_b, nat["gm"], bgate8)
    mix, = _proj(merged, [w_out16], F32, "proj_mix")
    pw, = _proj(p16, [w_ple16], F32, "proj_ple")

    dx, dpre16, dpw16, dgp16, ln_sums = _ln_loss(x, mix, nat["gp"], pw, target, bgate8, ln_g, ln_b)
    loss_sum = (0.5 / D_MODEL) * jnp.sum(ln_sums[3])
    dmerged = _dx([dpre16], [w_out16], [], "dx_merged")
    dya16, dyb16, dgm16, mg_sums = _merge_bwd(dmerged, y_a, y_b, nat["gm"], bgate8)
    doa = _dx([dya16], [w_bra], [], "dx_oa")
    dys = _dx([dyb16], [w_brb], [], "dx_yssm")
    g_w_out, = _dw(merged, [dpre16], BF16, "dw_out")
    g_w_br = jnp.concatenate([_dw(oa, [dya16], BF16, "dw_bra")[0], _dw(y_ssm, [dyb16], BF16, "dw_brb")[0]], axis=0)
    g_w_ple, = _dw(p16, [dpw16], BF16, "dw_ple")

    do_att, do16, stats, dgatt16 = _combine_bwd(doa, nat["gatt"], o_att, lse)
    dseg = {"gatt": dgatt16, "gm": dgm16, "gp": dgp16}
    dbias = []
    for g in range(3):
        own_order = lambda t: _permute(t, dils[g]).reshape(nb, dils[g], seq // dils[g], t.shape[-1])
        cotangent = (do_att, o_att, lse) if g == 0 else (own_order(do16), own_order(stats))
        dqkv, db = _attn_bwd(qkv[g], bias[g * GROUP_HEADS:(g + 1) * GROUP_HEADS], cotangent, dils[g],
                             "attn_bwd%d" % g)
        dseg["qkv%d" % g] = dqkv.reshape(nb, seq, -1)
        dbias.append(db)
    g_rel = _bias_grad(jnp.concatenate(dbias, axis=0), bmaps)[:, 0, :NUM_BUCKETS].T

    dact, ddtg, dz, ssd_small, g_normw = _ssd_bwd(
        act, dtg, sgg, nat["z"], y_all, dys, sprev, alog_g, dskip_g, ssm_norm_w)
    dseg["z"] = dz
    dseg["dt"] = jnp.pad(_ungroup_lanes(ddtg), ((0, 0), (0, 0), (0, LANE - SSM_HEADS)))
    dpre, conv_sums = _conv_bwd_pre(dact, nat["xbc"], conv_wg, conv_bg, "conv_bwd")
    dseg["xbc"] = _conv_bwd_x(dpre, conv_wg, "conv_bwd_x")
    csum = _xbc_reference_order(conv_sums)

    dx_perm = [_unpermute(_dx([dseg["qkv%d" % g]], [wseg["qkv%d" % g]], [], "dx_qkv%d" % g, True), dils[g])
               for g in (1, 2)]
    dwseg = {"qkv%d" % g: _dw(x16p[g], [dseg["qkv%d" % g]], BF16, "dw_qkv%d" % g, True)[0] for g in range(3)}
    for gi, group in enumerate(DW_GROUPS):
        dwseg.update(zip(group, _dw(x16, [dseg[s] for s in group], BF16, "dw_nat%d" % gi, True)))
    names = ["qkv0"] + [s for group, _ in NAT_GROUPS for s in group]
    dx = finish_dx([dseg[s] for s in names], [wseg[s] for s in names], [dx] + dx_perm, dwseg, g_w_br, g_w_out, g_w_ple)

    small = dict(
        b_gate=jnp.stack([mg_sums[0], mg_sums[1], ln_sums[2]]),
        conv_w=csum[0:4], conv_b=csum[4:5],
        dt_bias=_ungroup_lanes(ssd_small[:, 2:3, :]), a_log=_ungroup_lanes(ssd_small[:, 0:1, :]),
        d_skip=_ungroup_lanes(ssd_small[:, 1:2, :]), ssm_norm_w=g_normw,
        ln_g=ln_sums[0:1], ln_b=ln_sums[1:2], rel_bias=g_rel)
    return loss_sum, dx, small


DX_TM = 256
SMALL_ORDER = ("b_gate", "conv_w", "conv_b", "dt_bias", "a_log", "d_skip", "ssm_norm_w", "ln_g", "ln_b", "rel_bias")
SMALL_FULL_SHAPES = dict(b_gate=(3, 1024), conv_w=(4, 3072), conv_b=(1, 3072), dt_bias=(1, 32), a_log=(1, 32),
                         d_skip=(1, 32), ssm_norm_w=(1, 2048), ln_g=(1, 1024), ln_b=(1, 1024), rel_bias=(32, 36))


def kernel(x, p, w_in, b_gate, conv_w, conv_b, dt_bias, a_log, d_skip, ssm_norm_w, w_branch, w_out, w_ple, ln_g, ln_b, rel_bias, loss_target, m_w_in, m_b_gate, m_conv_w, m_conv_b, m_dt_bias, m_a_log, m_d_skip, m_ssm_norm_w, m_w_branch, m_w_out, m_w_ple, m_ln_g, m_ln_b, m_rel_bias, v_w_in, v_b_gate, v_conv_w, v_conv_b, v_dt_bias, v_a_log, v_d_skip, v_ssm_norm_w, v_w_branch, v_w_out, v_w_ple, v_ln_g, v_ln_b, v_rel_bias):
    cx, cy, cc = _place()
    chip = 2 * cx + cy
    dev = 4 * cx + 2 * cy + cc

    w_in_t = jnp.transpose(w_in[0])
    win16 = _shard_to_window(w_in_t, chip)
    g_win, g_br, g_out, g_ple = _allgather_pieces(
        [win16, w_branch[0].astype(BF16), w_out[0].astype(BF16), w_ple[0].astype(BF16)])
    wseg = _assemble(g_win)
    w_br16 = g_br.reshape(4 * 704, D_MODEL)
    w_out16 = g_out.reshape(D_MODEL, D_MODEL)
    w_ple16 = jnp.transpose(g_ple, (1, 0, 2)).reshape(PLE_DIM, D_MODEL)
    shards = _allgather8(_pack_rows([b_gate[0], conv_w[0]]), "allgather_small_params")
    per_chip = [_unpack_rows(shards[2 * k], [(3, 256), (4, 768)]) for k in range(4)]
    b_gate_full = jnp.concatenate([pc[0] for pc in per_chip], axis=1)
    conv_w_full = jnp.concatenate([pc[1] for pc in per_chip], axis=1)

    place = jnp.stack([chip, cc]).astype(jnp.int32)
    reduced = []

    def finish_dx(dhs, ws, accs, dwseg, d_br, d_out, d_ple):
        grads = [_pack(dwseg), d_br.reshape(4, 704, D_MODEL), d_out.reshape(4, 256, D_MODEL),
                 jnp.transpose(d_ple.reshape(PLE_DIM, 4, 256), (1, 0, 2))]
        got = _sibling_exchange(grads)
        chip_sums = [_pair_sum(g, t, place, "grad_pair_sum_%d" % i) for i, (g, t) in enumerate(zip(grads, got))]
        dx, others = _dx(dhs, ws, accs, "dx_w_in_and_grad_chip_scatter", True, DX_TM, chip_sums)
        fulls = [_chip_sum(s, t, place, "grad_chip_sum_%d" % i) for i, (s, t) in enumerate(zip(chip_sums, others))]
        reduced.extend(_sibling_gather(fulls))
        return dx

    loss_sum, grad_x, small = _local_grads(
        x, p[0], loss_target, wseg, w_br16, w_out16, w_ple16, b_gate_full, conv_w_full, conv_b, dt_bias, a_log,
        d_skip, ssm_norm_w, ln_g, ln_b, rel_bias, finish_dx)
    loss = lax.psum(loss_sum, ("x", "y", "c"))
    big = reduced
    g_w_in = _window_to_shard(big[0], chip)
    g_w_branch, g_w_out, g_w_ple = big[1], big[2], big[3]
    parts = _allgather8(_pack_rows([small[n] for n in SMALL_ORDER]), "allgather_small_grads")
    small_sum = _sum_rows([parts[i] for i in range(8)], F32, "small_grad_sum")
    sg = dict(zip(SMALL_ORDER, _unpack_rows(small_sum, [SMALL_FULL_SHAPES[n] for n in SMALL_ORDER])))
    sg["b_gate"] = lax.dynamic_slice_in_dim(sg["b_gate"], chip * 256, 256, axis=1)
    sg["conv_w"] = lax.dynamic_slice_in_dim(sg["conv_w"], chip * 768, 768, axis=1)
    del dev

    upd = {}
    upd["w_in"] = [jnp.transpose(t) for t in _adamw(w_in_t, g_w_in, jnp.transpose(m_w_in[0]),
                                                      jnp.transpose(v_w_in[0]), "adamw_w_in")]
    upd["w_branch"] = _adamw(w_branch[0], g_w_branch, m_w_branch[0], v_w_branch[0], "adamw_w_branch")
    upd["w_out"] = _adamw(w_out[0], g_w_out, m_w_out[0], v_w_out[0], "adamw_w_out")
    upd["w_ple"] = _adamw(w_ple[0], g_w_ple, m_w_ple[0], v_w_ple[0], "adamw_w_ple")
    small_w = dict(b_gate=b_gate, conv_w=conv_w, conv_b=conv_b, dt_bias=dt_bias, a_log=a_log, d_skip=d_skip,
                   ssm_norm_w=ssm_norm_w, ln_g=ln_g, ln_b=ln_b, rel_bias=rel_bias)
    small_m = dict(b_gate=m_b_gate, conv_w=m_conv_w, conv_b=m_conv_b, dt_bias=m_dt_bias, a_log=m_a_log,
                   d_skip=m_d_skip, ssm_norm_w=m_ssm_norm_w, ln_g=m_ln_g, ln_b=m_ln_b, rel_bias=m_rel_bias)
    small_v = dict(b_gate=v_b_gate, conv_w=v_conv_w, conv_b=v_conv_b, dt_bias=v_dt_bias, a_log=v_a_log,
                   d_skip=v_d_skip, ssm_norm_w=v_ssm_norm_w, ln_g=v_ln_g, ln_b=v_ln_b, rel_bias=v_rel_bias)
    shapes = [small_w[n].shape for n in SMALL_ORDER]
    s_delta, s_m, s_v = _adamw(_pack_rows([small_w[n] for n in SMALL_ORDER]), _pack_rows([sg[n] for n in SMALL_ORDER]),
                               _pack_rows([small_m[n] for n in SMALL_ORDER]), _pack_rows([small_v[n] for n in SMALL_ORDER]),
                               "adamw_small")
    for i, n in enumerate(SMALL_ORDER):
        upd[n] = tuple(_unpack_rows(t, shapes)[i] for t in (s_delta, s_m, s_v))
        sg[n] = sg[n].reshape(small_w[n].shape)

    order = ("w_in", "b_gate", "conv_w", "conv_b", "dt_bias", "a_log", "d_skip", "ssm_norm_w", "w_branch", "w_out",
             "w_ple", "ln_g", "ln_b", "rel_bias")
    grads = dict(sg, w_in=jnp.transpose(g_w_in)[None],w_branch=g_w_branch[None], w_out=g_w_out[None], w_ple=g_w_ple[None])
    lead = lambda n, t: t[None] if n in ("w_in", "w_branch", "w_out", "w_ple") else t
    return (loss, grad_x, *[grads[n] for n in order], *[lead(n, upd[n][0]) for n in order],
            *[lead(n, upd[n][1]) for n in order], *[lead(n, upd[n][2]) for n in order])
```

```python
import functools
import math

import numpy as np
import jax
import jax.numpy as jnp
from jax import lax
from jax.experimental import pallas as pl
from jax.experimental.pallas import tpu as pltpu

F32, BF16 = jnp.float32, jnp.bfloat16

D_MODEL = 1024
HEAD_DIM = 64
GROUP_HEADS = 12
ATT_OUT = GROUP_HEADS * HEAD_DIM
PATTERNS = ((128, 1), (512, 4), (2048, 16))
BAND = 128
NUM_BUCKETS = 32
MAX_DISTANCE = 2048
D_INNER = 2048
SSM_HEADS = 32
SSM_GROUPS = 4
GROUP_SSM_HEADS = SSM_HEADS // SSM_GROUPS
D_STATE = 128
CHUNK = 128
PLE_DIM = 256
ALPHA = 2.0 ** 0.25
LN_EPS = 1e-5
RMS_EPS = 1e-5
ADAM_LR, ADAM_B1, ADAM_B2, ADAM_EPS, ADAM_WD, ADAM_STEP = 0.001, 0.9, 0.999, 1e-08, 0.01, 10
NEG = -1e30

QKV_W = 3 * ATT_OUT
IN_COLS = 15904
SHARD_COLS = IN_COLS // 4
DT_COL = 12800
ROW_TILE = 16
WIN_ROWS = 4000


def _win_offset(k):
    return (k * SHARD_COLS) % ROW_TILE


def _win_start(k):
    return k * SHARD_COLS - _win_offset(k)

VMEM_LIMIT_BYTES = 56 * 1024 * 1024
LANE = 128
MESH = pl.DeviceIdType.MESH
NT = (((1,), (1,)), ((), ()))
TN = (((0,), (0,)), ((), ()))


def _pcall(body, **kw):
    return pl.pallas_call(body, **kw)


def _params(*sem):
    return pltpu.CompilerParams(dimension_semantics=sem, vmem_limit_bytes=VMEM_LIMIT_BYTES)


def _sigmoid(v):
    return jax.nn.sigmoid(v)


MM_TM = 512


def _permute(t, d):
    nb, seq, ch = t.shape
    return t if d == 1 else t.reshape(nb, seq // d, d, ch).transpose(0, 2, 1, 3).reshape(nb, seq, ch)


def _unpermute(t, d):
    nb, seq, ch = t.shape
    return t if d == 1 else t.reshape(nb, d, seq // d, ch).transpose(0, 2, 1, 3).reshape(nb, seq, ch)


def _tok_spec(tm, width):
    return pl.BlockSpec((None, tm, width), lambda b, i: (b, i, 0))


def _whole(arr, single_buffer=False):
    mode = dict(pipeline_mode=pl.Buffered(1)) if single_buffer else {}
    return pl.BlockSpec(arr.shape, lambda b, i: (0,) * arr.ndim, **mode)


def _proj(a3, ws, out_dtype, name, w_rows_are_outputs=False, tm=MM_TM):
    nb, seq, kdim = a3.shape
    nw = len(ws)
    widths = [w.shape[0] if w_rows_are_outputs else w.shape[1] for w in ws]

    def body(*refs):
        a = refs[0][...].astype(BF16)
        for w_ref, o_ref in zip(refs[1:1 + nw], refs[1 + nw:]):
            if w_rows_are_outputs:
                v = lax.dot_general(a, w_ref[...], NT, preferred_element_type=F32)
            else:
                v = jnp.dot(a, w_ref[...], preferred_element_type=F32)
            o_ref[...] = v.astype(out_dtype)

    return _pcall(
        body, name=name, grid=(nb, seq // tm),
        in_specs=[_tok_spec(tm, kdim)] + [_whole(w) for w in ws],
        out_specs=[_tok_spec(tm, n) for n in widths],
        out_shape=[jax.ShapeDtypeStruct((nb, seq, n), out_dtype) for n in widths],
        compiler_params=_params("parallel", "parallel"),
    )(a3, *ws)


def _dx(dhs, ws, accs, name, w_rows_are_outputs=False, tm=MM_TM, scatter=None):
    nb, seq, _ = dhs[0].shape
    nd, nacc = len(dhs), len(accs)
    kout = ws[0].shape[1] if w_rows_are_outputs else ws[0].shape[0]
    sums = scatter or []
    ns = len(sums)
    chunks = [_row_chunks(s.shape[1], s.shape[2] * s.dtype.itemsize) for s in sums]
    n_sem = 3 * sum(len(ch) for ch in chunks)
    grid = (nb, seq // tm)

    def body(*refs):
        n_in = 2 * nd + nacc
        sum_refs, o_ref, got_refs = refs[n_in:n_in + ns], refs[n_in + ns], refs[n_in + ns + 1:n_in + 2 * ns + 1]

        def copies():
            send_sems, recv_sems = refs[-2], refs[-1]
            x, y, c = _place()
            out = []
            for a in range(ns):
                for s, m in chunks[a]:
                    for j, (cx, cy) in enumerate(_other_chips(x, y)):
                        k = len(out)
                        out.append(_remote(sum_refs[a].at[2 * cx + cy, pl.ds(s, m)], got_refs[a].at[j, pl.ds(s, m)],
                                           send_sems.at[k], recv_sems.at[k], (cx, cy, c)))
            return out

        if ns:
            @pl.when((pl.program_id(0) == 0) & (pl.program_id(1) == 0))
            def _():
                for cp in copies():
                    cp.start()

        v = None
        for dh_ref, w_ref in zip(refs[:nd], refs[nd:2 * nd]):
            dh = dh_ref[...].astype(BF16)
            if w_rows_are_outputs:
                t = jnp.dot(dh, w_ref[...], preferred_element_type=F32)
            else:
                t = lax.dot_general(dh, w_ref[...], NT, preferred_element_type=F32)
            v = t if v is None else v + t
        for a_ref in refs[2 * nd:n_in]:
            v = v + a_ref[...]
        o_ref[...] = v

        if ns:
            @pl.when((pl.program_id(0) == grid[0] - 1) & (pl.program_id(1) == grid[1] - 1))
            def _():
                for cp in copies():
                    cp.wait()

    out = _pcall(
        body, name=name, grid=grid,
        in_specs=[_tok_spec(tm, dh.shape[-1]) for dh in dhs] + [_whole(w, bool(ns)) for w in ws]
        + [_tok_spec(tm, kout)] * nacc + [ANY] * ns,
        out_specs=[_tok_spec(tm, kout)] + [ANY] * ns,
        out_shape=[jax.ShapeDtypeStruct((nb, seq, kout), F32)]
        + [jax.ShapeDtypeStruct((3,) + s.shape[1:], s.dtype) for s in sums],
        input_output_aliases={2 * nd: 0} if nacc else {},
        scratch_shapes=[pltpu.SemaphoreType.DMA((n_sem,)), pltpu.SemaphoreType.DMA((n_sem,))] if ns else [],
        compiler_params=pltpu.CompilerParams(
            dimension_semantics=("arbitrary", "arbitrary") if ns else ("parallel", "parallel"),
            vmem_limit_bytes=VMEM_LIMIT_BYTES, has_side_effects=bool(ns)),
    )(*dhs, *ws, *accs, *sums)
    return (out[0], list(out[1:])) if ns else out[0]


def _dw(a3, dhs, out_dtype, name, rows_are_outputs=False):
    nb, seq, kdim = a3.shape
    nd = len(dhs)
    grid = (nb, seq // MM_TM)
    shapes = [(dh.shape[-1], kdim) if rows_are_outputs else (kdim, dh.shape[-1]) for dh in dhs]

    def body(*refs):
        b, i = pl.program_id(0), pl.program_id(1)
        dh_refs, o_refs, acc_refs = refs[1:1 + nd], refs[1 + nd:1 + 2 * nd], refs[1 + 2 * nd:]

        @pl.when((b == 0) & (i == 0))
        def _():
            for acc_ref in acc_refs:
                acc_ref[...] = jnp.zeros_like(acc_ref)

        a = refs[0][...].astype(BF16)
        for dh_ref, acc_ref in zip(dh_refs, acc_refs):
            dh = dh_ref[...].astype(BF16)
            acc_ref[...] += lax.dot_general(*((dh, a) if rows_are_outputs else (a, dh)), TN,
                                            preferred_element_type=F32)

        @pl.when((b == grid[0] - 1) & (i == grid[1] - 1))
        def _():
            for o_ref, acc_ref in zip(o_refs, acc_refs):
                o_ref[...] = acc_ref[...].astype(out_dtype)

    return _pcall(
        body, name=name, grid=grid,
        in_specs=[_tok_spec(MM_TM, kdim)] + [_tok_spec(MM_TM, dh.shape[-1]) for dh in dhs],
        out_specs=[pl.BlockSpec(s, lambda b, i: (0, 0)) for s in shapes],
        out_shape=[jax.ShapeDtypeStruct(s, out_dtype) for s in shapes],
        scratch_shapes=[pltpu.VMEM(s, F32) for s in shapes],
        compiler_params=_params("arbitrary", "arbitrary"),
    )(a3, *dhs)


def _qkv_rows(g):
    return [(part * QKV_W + g * ATT_OUT + hp * LANE, LANE) for hp in range(ATT_OUT // LANE) for part in range(3)]


XBC_START = 3 * QKV_W + ATT_OUT + D_INNER
GROUP_CH = GROUP_SSM_HEADS * HEAD_DIM
XBC_GROUP = GROUP_CH + 2 * D_STATE
CONV_DIM = SSM_GROUPS * XBC_GROUP


def _xbc_ranges():
    out = []
    for g in range(SSM_GROUPS):
        out += [(g * GROUP_CH, GROUP_CH), (D_INNER + g * D_STATE, D_STATE),
                (D_INNER + SSM_GROUPS * D_STATE + g * D_STATE, D_STATE)]
    return out


def _xbc_group_order(t):
    return jnp.concatenate([t[..., s:s + n] for s, n in _xbc_ranges()], axis=-1)


def _xbc_reference_order(t):
    g = lambda off, n: [t[..., k * XBC_GROUP + off:k * XBC_GROUP + off + n] for k in range(SSM_GROUPS)]
    return jnp.concatenate(g(0, GROUP_CH) + g(GROUP_CH, D_STATE) + g(GROUP_CH + D_STATE, D_STATE), axis=-1)


def _segments():
    one = lambda name, start, rows: (name, [(start, rows)], max(rows, LANE))
    return [("qkv%d" % g, _qkv_rows(g), QKV_W) for g in range(3)] + [
        one("gatt", 3 * QKV_W, ATT_OUT), one("z", 3 * QKV_W + ATT_OUT, D_INNER),
        ("xbc", [(XBC_START + s, n) for s, n in _xbc_ranges()], CONV_DIM), one("dt", DT_COL, SSM_HEADS),
        one("gm", DT_COL + SSM_HEADS, 2 * D_MODEL), one("gp", DT_COL + SSM_HEADS + 2 * D_MODEL, D_MODEL)]


LAYOUT_TC = 256
NAT_GROUPS = ((("gatt", "z", "dt", "gp"), 512), (("xbc", "gm"), 256))
DW_GROUPS = (("gatt", "z", "dt", "gp"), ("xbc",), ("gm",))


def _assemble(win):
    segs = _segments()

    def body(win_ref, *outs):
        def pieces(start, rows):
            t, end = start, start + rows
            while t < end:
                k = min(t // SHARD_COLS, 3)
                shard_end = (k + 1) * SHARD_COLS
                if k < 3 and shard_end % ROW_TILE and t == shard_end - shard_end % ROW_TILE:
                    lo = t - _win_start(k)
                    yield win_ref[k, lo:lo + ROW_TILE, :] + win_ref[k + 1, 0:ROW_TILE, :]
                    t += ROW_TILE
                    continue
                upto = min(end, shard_end - shard_end % ROW_TILE if k < 3 else end)
                yield win_ref[k, t - _win_start(k):upto - _win_start(k), :]
                t = upto

        for (_, ranges, total), o_ref in zip(segs, outs):
            off = 0
            for start, rows in ranges:
                for part in pieces(start, rows):
                    o_ref[off:off + part.shape[0], :] = part
                    off += part.shape[0]
            if off < total:
                o_ref[off:total, :] = jnp.zeros((total - off, o_ref.shape[1]), BF16)

    outs = _pcall(
        body, name="assemble_w_in", grid=(D_MODEL // LAYOUT_TC,),
        in_specs=[pl.BlockSpec((4, WIN_ROWS, LAYOUT_TC), lambda i: (0, 0, i))],
        out_specs=[pl.BlockSpec((total, LAYOUT_TC), lambda i: (0, i)) for _, _, total in segs],
        out_shape=[jax.ShapeDtypeStruct((total, D_MODEL), BF16) for _, _, total in segs],
        compiler_params=_params("parallel"),
    )(win)
    return {name: o for (name, _, _), o in zip(segs, outs)}


def _pack(dsegs):
    segs = _segments()

    def body(*refs):
        ins, o_ref = refs[:-1], refs[-1]
        tail = IN_COLS - _win_start(3)
        o_ref[3, tail:, :] = jnp.zeros((WIN_ROWS - tail, o_ref.shape[2]), BF16)
        for (_, ranges, _), s_ref in zip(segs, ins):
            off = 0
            for start, rows in ranges:
                for k in range(4):
                    lo = _win_start(k)
                    a, b = max(start, lo), min(start + rows, lo + WIN_ROWS)
                    if a < b:
                        o_ref[k, a - lo:b - lo, :] = s_ref[off + a - start:off + b - start, :]
                off += rows

    return _pcall(
        body, name="pack_dw_in", grid=(D_MODEL // LAYOUT_TC,),
        in_specs=[pl.BlockSpec((total, LAYOUT_TC), lambda i: (0, i)) for _, _, total in segs],
        out_specs=pl.BlockSpec((4, WIN_ROWS, LAYOUT_TC), lambda i: (0, 0, i)),
        out_shape=jax.ShapeDtypeStruct((4, WIN_ROWS, D_MODEL), BF16),
        compiler_params=_params("parallel"),
    )(*[dsegs[name] for name, _, _ in segs])


def _shard_to_window(shard_t, k):
    def at(off):
        return lambda w: jnp.pad(w.astype(BF16), ((off, WIN_ROWS - SHARD_COLS - off), (0, 0)))

    return lax.cond(k % 2 == 1, at(_win_offset(1)), at(_win_offset(0)), shard_t)


def _window_to_shard(win, k):
    return lax.dynamic_slice(win, ((k % 2) * _win_offset(1), 0), (SHARD_COLS, D_MODEL))


def _bucket_maps():
    qi = np.arange(BAND)[:, None]
    kj = np.arange(2 * BAND)[None, :]
    delta = qi + BAND - kj
    maps = []
    for window, dil in PATTERNS:
        valid = (delta >= 0) & (delta <= window // dil)
        dist = np.maximum(delta, 0) * dil
        max_exact = NUM_BUCKETS // 2
        d_f = np.maximum(dist, 1).astype(np.float32)
        large = max_exact + (np.log(d_f / np.float32(max_exact)) / np.float32(math.log(MAX_DISTANCE / max_exact))
                             * np.float32(NUM_BUCKETS - max_exact)).astype(np.int32)
        large = np.minimum(large, NUM_BUCKETS - 1)
        bucket = np.where(dist < max_exact, dist, large)
        maps.append(np.where(valid, bucket, -1).astype(np.int32))
    return np.stack(maps)


def _bias_tables(rel_bias, bmaps):
    def body(rb_ref, bm_ref, o_ref):
        h = pl.program_id(0)
        bm = bm_ref[...]
        acc = jnp.full(bm.shape, NEG, F32)
        for b in range(NUM_BUCKETS):
            acc = jnp.where(bm == b, rb_ref[b, h], acc)
        o_ref[...] = acc

    return _pcall(
        body, name="bias_tables", grid=(3 * GROUP_HEADS,),
        in_specs=[pl.BlockSpec(memory_space=pltpu.SMEM),
                  pl.BlockSpec((None, BAND, 2 * BAND), lambda h: (h // GROUP_HEADS, 0, 0))],
        out_specs=pl.BlockSpec((None, BAND, 2 * BAND), lambda h: (h, 0, 0)),
        out_shape=jax.ShapeDtypeStruct((3 * GROUP_HEADS, BAND, 2 * BAND), F32),
        compiler_params=_params("parallel"),
    )(rel_bias, bmaps)


def _bias_grad(dbias, bmaps):
    def body(db_ref, bm_ref, o_ref):
        bm = bm_ref[...]
        db = db_ref[...]
        lane = lax.broadcasted_iota(jnp.int32, (1, LANE), 1)
        vec = jnp.zeros((1, LANE), F32)
        for b in range(NUM_BUCKETS):
            s = jnp.sum(jnp.where(bm == b, db, 0.0), keepdims=True)
            vec = jnp.where(lane == b, s, vec)
        o_ref[...] = vec

    return _pcall(
        body, name="bias_grad", grid=(3 * GROUP_HEADS,),
        in_specs=[pl.BlockSpec((None, BAND, 2 * BAND), lambda h: (h, 0, 0)),
                  pl.BlockSpec((None, BAND, 2 * BAND), lambda h: (h // GROUP_HEADS, 0, 0))],
        out_specs=pl.BlockSpec((None, 1, LANE), lambda h: (h, 0, 0)),
        out_shape=jax.ShapeDtypeStruct((3 * GROUP_HEADS, 1, LANE), F32),
        compiler_params=_params("parallel"),
    )(dbias, bmaps)


def _rows(n):
    if isinstance(n, int):
        return pl.ds(n * BAND, BAND)
    return pl.ds(pl.multiple_of(n * BAND, BAND), BAND)


def _for_blocks(blocks, nblk, per, carry):
    carry = blocks([0], carry, False)
    start = 1 + (nblk - 1) % per
    for n in range(1, start):
        carry = blocks([n], carry, True)
    trips = (nblk - start) // per
    if trips > 0:
        carry = lax.fori_loop(
            0, trips, lambda t, c: blocks([start + t * per + u for u in range(per)], c, True), carry)
    return carry


def _pairs_per_step(d):
    return {1: 1, 4: 6, 16: 6}[d]


def _head_cols(i, h, part):
    base = 3 * LANE * i + part * LANE + h * HEAD_DIM
    return slice(base, base + HEAD_DIM)


def _attn_fwd(qkv4, bias, d, name):
    nb, _, sub, _ = qkv4.shape
    nblk = sub // BAND
    scale = HEAD_DIM ** -0.5
    npair = ATT_OUT // LANE
    hps = _pairs_per_step(d)
    compact = d > 1

    def body(qkv_ref, bias_ref, o_ref, l_ref):
        def blocks(ns, carry, with_prev):
            chains = [(bi, i, h) for bi in range(len(ns)) for i in range(hps) for h in range(2)]
            scores = []
            for bi, i, h in chains:
                n = ns[bi]
                q = qkv_ref[_rows(n), _head_cols(i, h, 0)] * scale
                s_c = lax.dot_general(q, qkv_ref[_rows(n), _head_cols(i, h, 1)], NT,
                                      preferred_element_type=F32) + bias_ref[2 * i + h, :, BAND:]
                s_p = None
                if with_prev:
                    s_p = lax.dot_general(q, qkv_ref[_rows(n - 1), _head_cols(i, h, 1)], NT,
                                          preferred_element_type=F32) + bias_ref[2 * i + h, :, :BAND]
                scores.append((s_c, s_p))
            probs = []
            for s_c, s_p in scores:
                m = jnp.max(s_c, -1, keepdims=True)
                if with_prev:
                    m = jnp.maximum(m, jnp.max(s_p, -1, keepdims=True))
                e_c = jnp.exp(s_c - m)
                den = jnp.sum(e_c, -1, keepdims=True)
                e_p = None
                if with_prev:
                    e_p = jnp.exp(s_p - m)
                    den = den + jnp.sum(e_p, -1, keepdims=True)
                    e_p = e_p.astype(BF16)
                probs.append((e_c.astype(BF16), e_p, den, m))
            outs = {}
            for (bi, i, h), (e_c, e_p, den, m) in zip(chains, probs):
                n = ns[bi]
                acc = jnp.dot(e_c, qkv_ref[_rows(n), _head_cols(i, h, 2)], preferred_element_type=F32)
                if with_prev:
                    acc = acc + jnp.dot(e_p, qkv_ref[_rows(n - 1), _head_cols(i, h, 2)], preferred_element_type=F32)
                outs[(bi, i, h)] = (acc / den, jnp.broadcast_to(m + jnp.log(den), (BAND, HEAD_DIM)))
            lane = lax.broadcasted_iota(jnp.int32, (BAND, LANE), 1)
            for bi, n in enumerate(ns):
                per_head = jnp.zeros((BAND, LANE), F32)
                for i in range(hps):
                    o_ref[_rows(n), i * LANE:(i + 1) * LANE] = jnp.concatenate(
                        [outs[(bi, i, 0)][0], outs[(bi, i, 1)][0]], axis=1)
                    if compact:
                        for h in range(2):
                            per_head = jnp.where(lane == 2 * i + h, outs[(bi, i, h)][1][:, :1], per_head)
                    else:
                        l_ref[_rows(n), i * LANE:(i + 1) * LANE] = jnp.concatenate(
                            [outs[(bi, i, 0)][1], outs[(bi, i, 1)][1]], axis=1)
                if compact:
                    l_ref[_rows(n), :] = per_head
            return carry

        _for_blocks(blocks, nblk, 2 if hps == 1 else 1, 0)

    in_specs = [pl.BlockSpec((None, None, sub, 3 * LANE * hps), lambda hp, b, r: (b, r, 0, hp)),
                pl.BlockSpec((2 * hps, BAND, 2 * BAND), lambda hp, b, r: (hp, 0, 0))]
    if compact:
        return _pcall(
            body, name=name, grid=(1, nb, d), in_specs=in_specs,
            out_specs=[pl.BlockSpec((None, None, sub, ATT_OUT), lambda hp, b, r: (b, r, 0, 0)),
                       pl.BlockSpec((None, None, sub, LANE), lambda hp, b, r: (b, r, 0, 0))],
            out_shape=[jax.ShapeDtypeStruct((nb, d, sub, ATT_OUT), F32), jax.ShapeDtypeStruct((nb, d, sub, LANE), F32)],
            compiler_params=_params("parallel", "parallel", "parallel"),
        )(qkv4, bias)
    ospec = pl.BlockSpec((None, sub, hps * LANE), lambda hp, b, r: (b, 0, r * (npair // hps) + hp))
    return _pcall(
        body, name=name, grid=(npair // hps, nb, d), in_specs=in_specs, out_specs=[ospec, ospec],
        out_shape=[jax.ShapeDtypeStruct((nb, sub, d * ATT_OUT), F32)] * 2,
        compiler_params=_params("parallel", "parallel", "parallel"),
    )(qkv4, bias)


STAT_LSE_LANE = 16


def _attn_bwd(qkv4, bias, cotangent, d, name):
    nb, _, sub, _ = qkv4.shape
    nblk = sub // BAND
    scale = HEAD_DIM ** -0.5
    npair = ATT_OUT // LANE
    hps = _pairs_per_step(d)
    compact = d > 1

    def body(qkv_ref, bias_ref, *rest):
        do_ref, dqkv_ref, db_ref = rest[0], rest[-2], rest[-1]
        b, r = pl.program_id(1), pl.program_id(2)

        @pl.when((b == 0) & (r == 0))
        def _():
            db_ref[...] = jnp.zeros_like(db_ref)

        def blocks(ns, carry, with_prev):
            sides = (0, 1) if with_prev else (0,)
            chains = [(bi, i, h, sd) for bi in range(len(ns)) for i in range(hps) for h in range(2) for sd in sides]
            key_rows = lambda bi, sd: _rows(ns[bi] - sd)
            qs = {}
            for bi in range(len(ns)):
                for i in range(hps):
                    for h in range(2):
                        hl = slice(i * LANE + h * HEAD_DIM, i * LANE + (h + 1) * HEAD_DIM)
                        do = do_ref[_rows(ns[bi]), hl]
                        if compact:
                            st_ref, head = rest[1], 2 * i + h
                            ebar = st_ref[_rows(ns[bi]), head:head + 1]
                            lcol = st_ref[_rows(ns[bi]), STAT_LSE_LANE + head:STAT_LSE_LANE + head + 1]
                        else:
                            ebar = jnp.sum(do * rest[1][_rows(ns[bi]), hl], -1, keepdims=True)
                            lcol = rest[2][_rows(ns[bi]), i * LANE + h * HEAD_DIM:i * LANE + h * HEAD_DIM + 1]
                        q_scaled = qkv_ref[_rows(ns[bi]), _head_cols(i, h, 0)] * scale
                        qs[(bi, i, h)] = (q_scaled, do.astype(BF16), ebar, lcol)
            raw = []
            for bi, i, h, sd in chains:
                q, do16, _, _ = qs[(bi, i, h)]
                k = qkv_ref[key_rows(bi, sd), _head_cols(i, h, 1)]
                v = qkv_ref[key_rows(bi, sd), _head_cols(i, h, 2)]
                bias_blk = bias_ref[2 * i + h, :, :BAND] if sd else bias_ref[2 * i + h, :, BAND:]
                s = lax.dot_general(q, k, NT, preferred_element_type=F32) + bias_blk
                dp = lax.dot_general(do16, v, NT, preferred_element_type=F32)
                raw.append((s, dp))
            soft = []
            for (bi, i, h, sd), (s, dp) in zip(chains, raw):
                _, _, ebar, lcol = qs[(bi, i, h)]
                p = jnp.exp(s - lcol)
                ds = p * (dp - ebar)
                if sd:
                    db_ref[2 * i + h, :, :BAND] += ds
                else:
                    db_ref[2 * i + h, :, BAND:] += ds
                soft.append((p.astype(BF16), ds.astype(BF16)))
            grads = {}
            for (bi, i, h, sd), (p16, ds16) in zip(chains, soft):
                q, do16, _, _ = qs[(bi, i, h)]
                k = qkv_ref[key_rows(bi, sd), _head_cols(i, h, 1)]
                grads[(bi, i, h, sd)] = (
                    jnp.dot(ds16, k, preferred_element_type=F32),
                    lax.dot_general(ds16, q, TN, preferred_element_type=F32),
                    lax.dot_general(p16, do16, TN, preferred_element_type=F32))
            both = lambda bi, i, sd, which: jnp.concatenate(
                [grads[(bi, i, 0, sd)][which], grads[(bi, i, 1, sd)][which]], axis=1)
            carry = list(carry) if carry is not None else None
            for bi, n in enumerate(ns):
                for i in range(hps):
                    base = 3 * LANE * i
                    dq = both(bi, i, 0, 0)
                    if with_prev:
                        dq = dq + both(bi, i, 1, 0)
                        dqkv_ref[_rows(n - 1), base + LANE:base + 2 * LANE] = (
                            carry[2 * i] + both(bi, i, 1, 1)).astype(BF16)
                        dqkv_ref[_rows(n - 1), base + 2 * LANE:base + 3 * LANE] = (
                            carry[2 * i + 1] + both(bi, i, 1, 2)).astype(BF16)
                    dqkv_ref[_rows(n), base:base + LANE] = (dq * scale).astype(BF16)
                carry = [t for i in range(hps) for t in (both(bi, i, 0, 1), both(bi, i, 0, 2))]
            return tuple(carry)

        carry = _for_blocks(blocks, nblk, 2 if hps == 1 else 1, None)
        for i in range(hps):
            base = 3 * LANE * i
            dqkv_ref[_rows(nblk - 1), base + LANE:base + 2 * LANE] = carry[2 * i].astype(BF16)
            dqkv_ref[_rows(nblk - 1), base + 2 * LANE:base + 3 * LANE] = carry[2 * i + 1].astype(BF16)

    qspec = pl.BlockSpec((None, None, sub, 3 * LANE * hps), lambda hp, b, r: (b, r, 0, hp))
    bspec = pl.BlockSpec((2 * hps, BAND, 2 * BAND), lambda hp, b, r: (hp, 0, 0))
    if compact:
        cspecs = [pl.BlockSpec((None, None, sub, ATT_OUT), lambda hp, b, r: (b, r, 0, 0)),
                  pl.BlockSpec((None, None, sub, LANE), lambda hp, b, r: (b, r, 0, 0))]
    else:
        cspecs = [pl.BlockSpec((None, sub, hps * LANE), lambda hp, b, r: (b, 0, r * (npair // hps) + hp))] * 3
    return _pcall(
        body, name=name, grid=(npair // hps, nb, d),
        in_specs=[qspec, bspec] + cspecs, out_specs=[qspec, bspec],
        out_shape=[jax.ShapeDtypeStruct(qkv4.shape, BF16),
                   jax.ShapeDtypeStruct((GROUP_HEADS, BAND, 2 * BAND), F32)],
        compiler_params=_params("parallel", "arbitrary", "arbitrary"),
    )(qkv4, bias, *cotangent)


def _head_lanes(first_lane, one_channel):
    c = lax.broadcasted_iota(jnp.int32, (ATT_OUT, LANE), 0)
    lane = lax.broadcasted_iota(jnp.int32, (ATT_OUT, LANE), 1)
    hit = lane == first_lane + c // HEAD_DIM
    if one_channel:
        hit = hit & (c % HEAD_DIM == 0)
    return hit.astype(BF16)


def _exact_dot(v, m01, dims=None):
    parts = _split3(v)
    if dims is None:
        dot = lambda t: jnp.dot(t, m01, preferred_element_type=F32)
    else:
        dot = lambda t: lax.dot_general(t, m01, dims, preferred_element_type=F32)
    return (dot(parts[0]) + dot(parts[1])) + dot(parts[2])


def _combine_fwd(o0, l0, o1, l1, o2, l2, gatt):
    nb, seq, _ = gatt.shape
    tm = 512

    def body(o0_ref, l0_ref, o1_ref, l1_ref, o2_ref, l2_ref, g_ref, oa_ref, oatt_ref, lse_ref):
        spread = _head_lanes(0, False)
        l0v = l0_ref[...]
        l1v = _exact_dot(l1_ref[...], spread, NT)
        l2v = _exact_dot(l2_ref[...], spread, NT)
        m = jnp.maximum(jnp.maximum(l0v, l1v), l2v)
        tot = m + jnp.log(jnp.exp(l0v - m) + jnp.exp(l1v - m) + jnp.exp(l2v - m))
        o = (jnp.exp(l0v - tot) * o0_ref[...] + jnp.exp(l1v - tot) * o1_ref[...]
             + jnp.exp(l2v - tot) * o2_ref[...])
        g = g_ref[...]
        oa_ref[...] = (o * (g * _sigmoid(g))).astype(BF16)
        oatt_ref[...] = o
        lse_ref[...] = tot

    spec = pl.BlockSpec((None, tm, ATT_OUT), lambda b, i: (b, i, 0))
    lspec = pl.BlockSpec((None, tm, LANE), lambda b, i: (b, i, 0))
    return _pcall(
        body, name="attn_combine", grid=(nb, seq // tm),
        in_specs=[spec, spec, spec, lspec, spec, lspec, spec], out_specs=[spec] * 3,
        out_shape=[jax.ShapeDtypeStruct((nb, seq, ATT_OUT), BF16), jax.ShapeDtypeStruct((nb, seq, ATT_OUT), F32),
                   jax.ShapeDtypeStruct((nb, seq, ATT_OUT), F32)],
        compiler_params=_params("parallel", "parallel"),
    )(o0, l0, o1, l1, o2, l2, gatt)


def _combine_bwd(doa, gatt, o_att, lse):
    nb, seq, _ = gatt.shape
    tm = 512

    def body(doa_ref, g_ref, o_ref, l_ref, do_ref, do16_ref, st_ref, dg_ref):
        g = g_ref[...]
        sg = _sigmoid(g)
        do = doa_ref[...] * (g * sg)
        do_ref[...] = do
        do16_ref[...] = do.astype(BF16)
        st_ref[...] = (_exact_dot(do * o_ref[...], _head_lanes(0, False))
                       + _exact_dot(l_ref[...], _head_lanes(STAT_LSE_LANE, True)))
        dg_ref[...] = (doa_ref[...] * o_ref[...] * (sg * (1.0 + g * (1.0 - sg)))).astype(BF16)

    spec = pl.BlockSpec((None, tm, ATT_OUT), lambda b, i: (b, i, 0))
    lspec = pl.BlockSpec((None, tm, LANE), lambda b, i: (b, i, 0))
    return _pcall(
        body, name="attn_combine_bwd", grid=(nb, seq // tm), in_specs=[spec] * 4,
        out_specs=[spec, spec, lspec, spec],
        out_shape=[jax.ShapeDtypeStruct((nb, seq, ATT_OUT), F32), jax.ShapeDtypeStruct((nb, seq, ATT_OUT), BF16),
                   jax.ShapeDtypeStruct((nb, seq, LANE), F32), jax.ShapeDtypeStruct((nb, seq, ATT_OUT), BF16)],
        compiler_params=_params("parallel", "parallel"),
    )(doa, gatt, o_att, lse)


CONV_TM = 512
CONV_TC = 512


def _shift_down(cur, halo, k):
    rolled = pltpu.roll(cur, k, 0)
    hro = pltpu.roll(halo, k, 0)
    row = lax.broadcasted_iota(jnp.int32, hro.shape, 0)
    return jnp.concatenate([jnp.where(row < k, hro, rolled[:8]), rolled[8:]], axis=0)


def _shift_up(cur, halo, k):
    n = cur.shape[0]
    rolled = pltpu.roll(cur, n - k, 0)
    hro = pltpu.roll(halo, 8 - k, 0)
    row = lax.broadcasted_iota(jnp.int32, hro.shape, 0)
    return jnp.concatenate([rolled[:n - 8], jnp.where(row >= 8 - k, hro, rolled[n - 8:])], axis=0)


def _conv_pre(cur, halo, w_ref, b_ref):
    acc = cur * w_ref[3:4, :] + b_ref[...]
    for k in range(1, 4):
        acc = acc + _shift_down(cur, halo, k) * w_ref[3 - k:4 - k, :]
    return acc


def _conv_specs(seq):
    nblk = seq // CONV_TM
    cur = pl.BlockSpec((None, CONV_TM, CONV_TC), lambda cb, b, i: (b, i, cb))
    prev = pl.BlockSpec((None, 8, CONV_TC), lambda cb, b, i: (b, jnp.maximum(i * (CONV_TM // 8) - 1, 0), cb))
    nxt = pl.BlockSpec((None, 8, CONV_TC),
                       lambda cb, b, i: (b, jnp.minimum((i + 1) * (CONV_TM // 8), seq // 8 - 1), cb))
    wspec = pl.BlockSpec((4, CONV_TC), lambda cb, b, i: (0, cb))
    bspec = pl.BlockSpec((1, CONV_TC), lambda cb, b, i: (0, cb))
    return nblk, cur, prev, nxt, wspec, bspec


def _conv_fwd(xin, w4, bias, name):
    nb, seq, ch = xin.shape
    _, cur, prev, _, wspec, bspec = _conv_specs(seq)

    def body(x_ref, h_ref, w_ref, b_ref, o_ref):
        halo = jnp.where(pl.program_id(2) > 0, h_ref[...], 0.0)
        pre = _conv_pre(x_ref[...], halo, w_ref, b_ref)
        o_ref[...] = pre * _sigmoid(pre)

    return _pcall(
        body, name=name, grid=(ch // CONV_TC, nb, seq // CONV_TM),
        in_specs=[cur, prev, wspec, bspec], out_specs=cur,
        out_shape=jax.ShapeDtypeStruct(xin.shape, F32),
        compiler_params=_params("parallel", "parallel", "parallel"),
    )(xin, xin, w4, bias)


def _conv_bwd_pre(dact, xin, w4, bias, name):
    nb, seq, ch = xin.shape
    _, cur, prev, _, wspec, bspec = _conv_specs(seq)

    def body(da_ref, x_ref, h_ref, w_ref, b_ref, dp_ref, s_ref):
        b, i = pl.program_id(1), pl.program_id(2)

        @pl.when((b == 0) & (i == 0))
        def _():
            s_ref[...] = jnp.zeros_like(s_ref)

        halo = jnp.where(i > 0, h_ref[...], 0.0)
        x = x_ref[...]
        pre = _conv_pre(x, halo, w_ref, b_ref)
        sg = _sigmoid(pre)
        dpre = da_ref[...] * (sg * (1.0 + pre * (1.0 - sg)))
        dp_ref[...] = dpre
        s_ref[3:4, :] += jnp.sum(dpre * x, 0, keepdims=True)
        for k in range(1, 4):
            s_ref[3 - k:4 - k, :] += jnp.sum(dpre * _shift_down(x, halo, k), 0, keepdims=True)
        s_ref[4:5, :] += jnp.sum(dpre, 0, keepdims=True)

    return _pcall(
        body, name=name, grid=(ch // CONV_TC, nb, seq // CONV_TM),
        in_specs=[cur, cur, prev, wspec, bspec],
        out_specs=[cur, pl.BlockSpec((8, CONV_TC), lambda cb, b, i: (0, cb))],
        out_shape=[jax.ShapeDtypeStruct(xin.shape, F32), jax.ShapeDtypeStruct((8, ch), F32)],
        compiler_params=_params("parallel", "arbitrary", "arbitrary"),
    )(dact, xin, xin, w4, bias)


def _conv_bwd_x(dpre, w4, name):
    nb, seq, ch = dpre.shape
    nblk, cur, _, nxt, wspec, _ = _conv_specs(seq)

    def body(d_ref, n_ref, w_ref, o_ref):
        halo = jnp.where(pl.program_id(2) < nblk - 1, n_ref[...], 0.0)
        cur_v = d_ref[...]
        acc = cur_v * w_ref[3:4, :]
        for j in range(1, 4):
            acc = acc + _shift_up(cur_v, halo, j) * w_ref[3 - j:4 - j, :]
        o_ref[...] = acc.astype(BF16)

    return _pcall(
        body, name=name, grid=(ch // CONV_TC, nb, seq // CONV_TM),
        in_specs=[cur, nxt, wspec], out_specs=cur,
        out_shape=jax.ShapeDtypeStruct(dpre.shape, BF16),
        compiler_params=_params("parallel", "parallel", "parallel"),
    )(dpre, dpre, w4)


def _softplus_sig(dt_raw, dt_bias_row):
    nb, seq, _ = dt_raw.shape
    tm = 512

    def body(r_ref, b_ref, sp_ref, sg_ref):
        v = r_ref[...] + b_ref[...]
        sp_ref[...] = jnp.maximum(v, 0.0) + jnp.log1p(jnp.exp(-jnp.abs(v)))
        sg_ref[...] = _sigmoid(v)

    spec = pl.BlockSpec((None, tm, LANE), lambda b, i: (b, i, 0))
    return _pcall(
        body, name="dt_softplus", grid=(nb, seq // tm),
        in_specs=[spec, pl.BlockSpec((1, LANE), lambda b, i: (0, 0))], out_specs=[spec, spec],
        out_shape=[jax.ShapeDtypeStruct(dt_raw.shape, F32)] * 2,
        compiler_params=_params("parallel", "parallel"),
    )(dt_raw, dt_bias_row)


def _group_lanes(t):
    pads = [(0, 0)] * (t.ndim - 1) + [(0, LANE - GROUP_SSM_HEADS)]
    return jnp.stack([jnp.pad(t[..., GROUP_SSM_HEADS * g:GROUP_SSM_HEADS * (g + 1)], pads) for g in range(SSM_GROUPS)])


def _ungroup_lanes(t):
    return jnp.concatenate([t[g][..., :GROUP_SSM_HEADS] for g in range(SSM_GROUPS)], axis=-1)


def _decays(dt, al_ref):
    row = lax.broadcasted_iota(jnp.int32, (CHUNK, CHUNK), 0)
    col = lax.broadcasted_iota(jnp.int32, (CHUNK, CHUNK), 1)
    tril = (row >= col).astype(BF16)
    triu = (row <= col).astype(BF16)
    arow = -jnp.exp(al_ref[...])
    hi, mid, lo = _split3(dt * arow)
    down = lambda t: jnp.dot(tril, t, preferred_element_type=F32)
    across = lambda t: lax.dot_general(t, triu, TN, preferred_element_type=F32)
    acs = (down(hi) + down(mid)) + down(lo)
    acs_t = (across(hi) + across(mid)) + across(lo)
    return arow, acs, acs_t, row >= col, triu


def _ssd_specs(nb, seq):
    nc = seq // CHUNK
    hw = GROUP_SSM_HEADS * HEAD_DIM

    def mk(rev):
        cidx = (lambda c: nc - 1 - c) if rev else (lambda c: c)
        wide = pl.BlockSpec((None, CHUNK, hw), lambda g, b, c: (b, cidx(c), g))
        xbc = pl.BlockSpec((None, CHUNK, XBC_GROUP), lambda g, b, c: (b, cidx(c), g))
        lanes = pl.BlockSpec((None, None, CHUNK, LANE), lambda g, b, c: (g, b, cidx(c), 0))
        prev = pl.BlockSpec((None, None, None, D_STATE, hw), lambda g, b, c: (b, cidx(c), g, 0, 0))
        return wide, xbc, lanes, prev

    grow = pl.BlockSpec((None, 1, LANE), lambda g, b, c: (g, 0, 0))
    nwspec = pl.BlockSpec((1, hw), lambda g, b, c: (0, g))
    return nc, hw, mk, grow, nwspec


def _head_expand():
    hw = GROUP_SSM_HEADS * HEAD_DIM
    r = lax.broadcasted_iota(jnp.int32, (LANE, hw), 0)
    c = lax.broadcasted_iota(jnp.int32, (LANE, hw), 1)
    return ((c // HEAD_DIM) == r).astype(BF16)


def _split3(v):
    hi = v.astype(BF16)
    rest = v - hi.astype(F32)
    mid = rest.astype(BF16)
    return hi, mid, (rest - mid.astype(F32)).astype(BF16)


def _to_channels(v, e):
    hi, mid, lo = _split3(v)
    dot = lambda t: jnp.dot(t, e, preferred_element_type=F32)
    return (dot(hi) + dot(mid)) + dot(lo)


def _to_heads(w, e):
    hi, mid, lo = _split3(w)
    dot = lambda t: lax.dot_general(t, e, (((1,), (1,)), ((), ())), preferred_element_type=F32)
    return (dot(hi) + dot(mid)) + dot(lo)


def _row8(v):
    return jnp.broadcast_to(v, (8, v.shape[1]))


def _ssd_chunk_setup(dt, al_ref, ds_ref):
    arow, acs, acs_t, causal, triu = _decays(dt, al_ref)
    e = _head_expand()
    dtx = _to_channels(dt, e)
    acsx = _to_channels(acs, e)
    lastx = acsx[CHUNK - 1:CHUNK, :]
    dskx = _to_channels(_row8(ds_ref[...]), e)[0:1, :]
    return arow, acs, acs_t, causal, triu, e, dtx, acsx, lastx, dskx


def _ssd_fwd(xbc, dtg, z, alog_g, dskip_g, normw):
    nb, seq, _ = xbc.shape
    nc, hw, mk, grow, nwspec = _ssd_specs(nb, seq)
    wide, xbc_spec, lanes, prev = mk(False)
    tn = (((0,), (0,)), ((), ()))

    def body(xbc_ref, dt_ref, z_ref, al_ref, ds_ref, nw_ref, ys_ref, y_ref, sp_ref, st_ref):
        @pl.when(pl.program_id(2) == 0)
        def _():
            st_ref[...] = jnp.zeros_like(st_ref)

        dt = dt_ref[...]
        _, acs, acs_t, causal, _, _, dtx, acsx, lastx, dskx = _ssd_chunk_setup(dt, al_ref, ds_ref)
        bmat = xbc_ref[:, GROUP_CH:GROUP_CH + D_STATE].astype(BF16)
        cmat = xbc_ref[:, GROUP_CH + D_STATE:].astype(BF16)
        cb = lax.dot_general(cmat, bmat, (((1,), (1,)), ((), ())), preferred_element_type=F32)
        x = xbc_ref[:, :GROUP_CH]
        xdt = x * dtx
        xdt16 = xdt.astype(BF16)
        first_head = lax.broadcasted_iota(jnp.int32, (CHUNK, LANE), 1) < HEAD_DIM
        pairs = []
        for hp in range(GROUP_SSM_HEADS // 2):
            xp = xdt16[:, hp * LANE:(hp + 1) * LANE]
            two = []
            for j in (2 * hp, 2 * hp + 1):
                lmat = jnp.exp(jnp.where(causal, acs[:, j:j + 1] - acs_t[j:j + 1, :], -jnp.inf))
                two.append(jnp.dot((cb * lmat).astype(BF16), xp, preferred_element_type=F32))
            pairs.append(jnp.where(first_head, two[0], two[1]))
        yd = jnp.concatenate(pairs, axis=1)
        s_prev = st_ref[...]
        s16 = s_prev.astype(BF16)
        sp_ref[...] = s16
        yo = jnp.dot(cmat, s16, preferred_element_type=F32) * jnp.exp(acsx)
        sts = lax.dot_general(bmat, (xdt * jnp.exp(lastx - acsx)).astype(BF16), tn, preferred_element_type=F32)
        st_ref[...] = s_prev * jnp.exp(lastx) + sts
        y = yd + yo + dskx * x
        zz = z_ref[...]
        u = y * (zz * _sigmoid(zz))
        rn = lax.rsqrt(jnp.mean(u * u, -1, keepdims=True) + RMS_EPS)
        ys_ref[...] = (u * rn * nw_ref[...]).astype(BF16)
        y_ref[...] = y

    return _pcall(
        body, name="ssd_fwd", grid=(SSM_GROUPS, nb, nc),
        in_specs=[xbc_spec, lanes, wide, grow, grow, nwspec],
        out_specs=[wide, wide, prev],
        out_shape=[jax.ShapeDtypeStruct((nb, seq, D_INNER), BF16), jax.ShapeDtypeStruct((nb, seq, D_INNER), F32),
                   jax.ShapeDtypeStruct((nb, nc, SSM_GROUPS, D_STATE, hw), BF16)],
        scratch_shapes=[pltpu.VMEM((D_STATE, hw), F32)],
        compiler_params=_params("parallel", "parallel", "arbitrary"),
    )(xbc, dtg, z, alog_g, dskip_g, normw)


def _ssd_bwd(xbc, dtg, sgg, z, y, dys, sprev, alog_g, dskip_g, normw):
    nb, seq, _ = xbc.shape
    nc, hw, mk, grow, nwspec = _ssd_specs(nb, seq)
    wide, xbc_spec, lanes, prev = mk(True)
    nt = (((1,), (1,)), ((), ()))
    tn = (((0,), (0,)), ((), ()))

    def body(xbc_ref, dt_ref, sg_ref, z_ref, y_ref, dys_ref, sp_ref, al_ref, ds_ref, nw_ref,
             dxbc_ref, ddt_ref, dz_ref, small_ref, dnw_ref, g_ref):
        b, c = pl.program_id(1), pl.program_id(2)

        @pl.when((b == 0) & (c == 0))
        def _():
            small_ref[...] = jnp.zeros_like(small_ref)
            dnw_ref[...] = jnp.zeros_like(dnw_ref)

        @pl.when(c == 0)
        def _():
            g_ref[...] = jnp.zeros_like(g_ref)

        yv, zz, dys_v, nw = y_ref[...], z_ref[...], dys_ref[...], nw_ref[...]
        sz = _sigmoid(zz)
        silu = zz * sz
        u = yv * silu
        rn = lax.rsqrt(jnp.mean(u * u, -1, keepdims=True) + RMS_EPS)
        gn = dys_v * nw
        du = rn * gn - u * (rn * rn * rn) * jnp.mean(u * gn, -1, keepdims=True)
        dnw_ref[...] += jnp.sum(dys_v * u * rn, 0, keepdims=True)
        dy = du * silu
        dz_ref[...] = du * yv * (sz * (1.0 + zz * (1.0 - sz)))

        dt = dt_ref[...]
        arow, acs, acs_t, causal, triu, e, dtx, acsx, lastx, dskx = _ssd_chunk_setup(dt, al_ref, ds_ref)
        dfsx = jnp.exp(acsx)
        dtex = jnp.exp(lastx - acsx)
        bmat = xbc_ref[:, GROUP_CH:GROUP_CH + D_STATE].astype(BF16)
        cmat = xbc_ref[:, GROUP_CH + D_STATE:].astype(BF16)
        cb = lax.dot_general(cmat, bmat, nt, preferred_element_type=F32)
        x = xbc_ref[:, :GROUP_CH]
        xdt = x * dtx
        xdt16 = xdt.astype(BF16)
        xdte = xdt * dtex
        dy16 = dy.astype(BF16)
        dyd = dy * dfsx
        dyd16 = dyd.astype(BF16)
        s16 = sp_ref[...]
        g = g_ref[...]
        g16 = g.astype(BF16)
        cs = jnp.dot(cmat, s16, preferred_element_type=F32)
        dc_off = lax.dot_general(dyd16, s16, nt, preferred_element_type=F32)
        g_here = lax.dot_general(cmat, dyd16, tn, preferred_element_type=F32)
        bg = jnp.dot(bmat, g16, preferred_element_type=F32)
        db_st = lax.dot_general(xdte.astype(BF16), g16, nt, preferred_element_type=F32)
        ddte_w = bg * xdte
        dcd = _to_heads(_row8(jnp.sum(g * s16.astype(F32), 0, keepdims=True)), e)[0:1, :]
        lane = lax.broadcasted_iota(jnp.int32, (CHUNK, LANE), 1)
        first_head = lane < HEAD_DIM
        sub = lax.broadcasted_iota(jnp.int32, (CHUNK, LANE), 0)
        dacs = jnp.zeros((CHUNK, LANE), F32)
        colsums = jnp.zeros((CHUNK, LANE), F32)
        dcb = jnp.zeros((CHUNK, CHUNK), F32)
        pairs = []
        for hp in range(GROUP_SSM_HEADS // 2):
            xp = xdt16[:, hp * LANE:(hp + 1) * LANE]
            dyp = dy16[:, hp * LANE:(hp + 1) * LANE]
            two = []
            for idx, j in enumerate((2 * hp, 2 * hp + 1)):
                lmat = jnp.exp(jnp.where(causal, acs[:, j:j + 1] - acs_t[j:j + 1, :], -jnp.inf))
                mf = cb * lmat
                dy_h = jnp.where(first_head if idx == 0 else jnp.logical_not(first_head), dyp, jnp.zeros_like(dyp))
                dm = lax.dot_general(dy_h, xp, nt, preferred_element_type=F32)
                two.append(lax.dot_general(mf.astype(BF16), dyp, tn, preferred_element_type=F32))
                wmat = dm * mf
                dcb = dcb + dm * lmat
                dacs = jnp.where(lane == j, jnp.sum(wmat, -1, keepdims=True), dacs)
                colsums = jnp.where(sub == j, jnp.sum(wmat, 0, keepdims=True), colsums)
            pairs.append(jnp.where(first_head, two[0], two[1]))
        dxdt = bg * dtex + jnp.concatenate(pairs, axis=1)
        dacs = dacs - colsums.T + _to_heads(dyd * cs - ddte_w, e)
        cd_row = jnp.exp(acs[CHUNK - 1:CHUNK, :])
        tail = _to_heads(_row8(jnp.sum(ddte_w, 0, keepdims=True)), e)[0:1, :] + dcd * cd_row
        dacs = dacs + jnp.where(sub == CHUNK - 1, tail, 0.0)
        d_hi, d_mid, d_lo = _split3(dacs)
        up = lambda t: jnp.dot(triu, t, preferred_element_type=F32)
        da = (up(d_hi) + up(d_mid)) + up(d_lo)
        ddt_raw = (da * arow + _to_heads(dxdt * x, e)) * sg_ref[...]
        ddt_ref[...] = ddt_raw
        small_ref[0:1, :] += jnp.sum(da * dt, 0, keepdims=True) * arow
        small_ref[1:2, :] += _to_heads(_row8(jnp.sum(dy * x, 0, keepdims=True)), e)[0:1, :]
        small_ref[2:3, :] += jnp.sum(ddt_raw, 0, keepdims=True)
        dcb16 = dcb.astype(BF16)
        dxbc_ref[:, GROUP_CH + D_STATE:] = dc_off + jnp.dot(dcb16, bmat, preferred_element_type=F32)
        dxbc_ref[:, GROUP_CH:GROUP_CH + D_STATE] = db_st + lax.dot_general(dcb16, cmat, tn,
                                                                            preferred_element_type=F32)
        dxbc_ref[:, :GROUP_CH] = dxdt * dtx + dskx * dy
        g_ref[...] = g * jnp.exp(lastx) + g_here

    return _pcall(
        body, name="ssd_bwd", grid=(SSM_GROUPS, nb, nc),
        in_specs=[xbc_spec, lanes, lanes, wide, wide, wide, prev, grow, grow, nwspec],
        out_specs=[xbc_spec, lanes, wide,
                   pl.BlockSpec((None, 8, LANE), lambda g, b, c: (g, 0, 0)), nwspec],
        out_shape=[jax.ShapeDtypeStruct((nb, seq, CONV_DIM), F32),
                   jax.ShapeDtypeStruct((SSM_GROUPS, nb, seq, LANE), F32),
                   jax.ShapeDtypeStruct((nb, seq, D_INNER), F32),
                   jax.ShapeDtypeStruct((SSM_GROUPS, 8, LANE), F32),
                   jax.ShapeDtypeStruct((1, D_INNER), F32)],
        scratch_shapes=[pltpu.VMEM((D_STATE, hw), F32)],
        compiler_params=_params("parallel", "arbitrary", "arbitrary"),
    )(xbc, dtg, sgg, z, y, dys, sprev, alog_g, dskip_g, normw)


EW_TM = 256


def _merge_fwd(y_a, y_b, gm, bgate):
    nb, seq, _ = y_a.shape

    def body(a_ref, b_ref, ga_ref, gb_ref, bg_ref, o_ref):
        sa = _sigmoid(ga_ref[...] + bg_ref[0:1, :])
        sb = _sigmoid(gb_ref[...] + bg_ref[1:2, :])
        o_ref[...] = (sa * a_ref[...] + sb * b_ref[...]).astype(BF16)

    spec = pl.BlockSpec((None, EW_TM, D_MODEL), lambda b, i: (b, i, 0))
    spec1 = pl.BlockSpec((None, EW_TM, D_MODEL), lambda b, i: (b, i, 1))
    return _pcall(
        body, name="merge_fwd", grid=(nb, seq // EW_TM),
        in_specs=[spec, spec, spec, spec1, pl.BlockSpec((8, D_MODEL), lambda b, i: (0, 0))], out_specs=spec,
        out_shape=jax.ShapeDtypeStruct((nb, seq, D_MODEL), BF16),
        compiler_params=_params("parallel", "parallel"),
    )(y_a, y_b, gm, gm, bgate)


def _merge_bwd(dmerged, y_a, y_b, gm, bgate):
    nb, seq, _ = y_a.shape

    def body(dm_ref, a_ref, b_ref, ga_ref, gb_ref, bg_ref, dya_ref, dyb_ref, dg_ref, s_ref):
        @pl.when((pl.program_id(0) == 0) & (pl.program_id(1) == 0))
        def _():
            s_ref[...] = jnp.zeros_like(s_ref)

        dm = dm_ref[...]
        sa = _sigmoid(ga_ref[...] + bg_ref[0:1, :])
        sb = _sigmoid(gb_ref[...] + bg_ref[1:2, :])
        dya_ref[...] = (dm * sa).astype(BF16)
        dyb_ref[...] = (dm * sb).astype(BF16)
        dga = dm * a_ref[...] * (sa * (1.0 - sa))
        dgb = dm * b_ref[...] * (sb * (1.0 - sb))
        dg_ref[:, :D_MODEL] = dga.astype(BF16)
        dg_ref[:, D_MODEL:] = dgb.astype(BF16)
        s_ref[0:1, :] += jnp.sum(dga, 0, keepdims=True)
        s_ref[1:2, :] += jnp.sum(dgb, 0, keepdims=True)

    spec = pl.BlockSpec((None, EW_TM, D_MODEL), lambda b, i: (b, i, 0))
    spec1 = pl.BlockSpec((None, EW_TM, D_MODEL), lambda b, i: (b, i, 1))
    small = pl.BlockSpec((8, D_MODEL), lambda b, i: (0, 0))
    return _pcall(
        body, name="merge_bwd", grid=(nb, seq // EW_TM),
        in_specs=[spec, spec, spec, spec, spec1, small],
        out_specs=[spec, spec, pl.BlockSpec((None, EW_TM, 2 * D_MODEL), lambda b, i: (b, i, 0)), small],
        out_shape=[jax.ShapeDtypeStruct((nb, seq, D_MODEL), BF16), jax.ShapeDtypeStruct((nb, seq, D_MODEL), BF16),
                   jax.ShapeDtypeStruct((nb, seq, 2 * D_MODEL), BF16), jax.ShapeDtypeStruct((8, D_MODEL), F32)],
        compiler_params=_params("arbitrary", "arbitrary"),
    )(dmerged, y_a, y_b, gm, gm, bgate)


def _ln_loss(x, mix, gp, pw, target, bgate, ln_g, ln_b):
    nb, seq, _ = x.shape

    def body(x_ref, mix_ref, gp_ref, pw_ref, t_ref, bg_ref, g_ref, b_ref, dx_ref, dp_ref, dpw_ref, dgp_ref, s_ref):
        @pl.when((pl.program_id(0) == 0) & (pl.program_id(1) == 0))
        def _():
            s_ref[...] = jnp.zeros_like(s_ref)

        sp = _sigmoid(gp_ref[...] + bg_ref[2:3, :])
        pw = pw_ref[...]
        pre = ALPHA * x_ref[...] + mix_ref[...] + sp * pw
        mu = jnp.mean(pre, -1, keepdims=True)
        cen = pre - mu
        rstd = lax.rsqrt(jnp.mean(cen * cen, -1, keepdims=True) + LN_EPS)
        xhat = cen * rstd
        err = xhat * g_ref[...] + b_ref[...] - t_ref[...]
        dy = err * (1.0 / D_MODEL)
        dxh = dy * g_ref[...]
        dpre = rstd * (dxh - jnp.mean(dxh, -1, keepdims=True) - xhat * jnp.mean(dxh * xhat, -1, keepdims=True))
        dx_ref[...] = ALPHA * dpre
        dp_ref[...] = dpre.astype(BF16)
        dpw_ref[...] = (dpre * sp).astype(BF16)
        dgp = dpre * pw * (sp * (1.0 - sp))
        dgp_ref[...] = dgp.astype(BF16)
        s_ref[0:1, :] += jnp.sum(dy * xhat, 0, keepdims=True)
        s_ref[1:2, :] += jnp.sum(dy, 0, keepdims=True)
        s_ref[2:3, :] += jnp.sum(dgp, 0, keepdims=True)
        s_ref[3:4, :] += jnp.sum(err * err, 0, keepdims=True)

    spec = pl.BlockSpec((None, EW_TM, D_MODEL), lambda b, i: (b, i, 0))
    small = pl.BlockSpec((8, D_MODEL), lambda b, i: (0, 0))
    row = pl.BlockSpec((1, D_MODEL), lambda b, i: (0, 0))
    return _pcall(
        body, name="ln_loss", grid=(nb, seq // EW_TM),
        in_specs=[spec] * 5 + [small, row, row], out_specs=[spec] * 4 + [small],
        out_shape=[jax.ShapeDtypeStruct((nb, seq, D_MODEL), F32)] + [jax.ShapeDtypeStruct((nb, seq, D_MODEL), BF16)] * 3
        + [jax.ShapeDtypeStruct((8, D_MODEL), F32)],
        compiler_params=_params("arbitrary", "arbitrary"),
    )(x, mix, gp, pw, target, bgate, ln_g, ln_b)


def _adamw(w, g, m, v, name):
    rows, cols = w.shape
    tr = _row_tile(rows, cols, 8, 5 << 19)
    c1 = 1.0 - ADAM_B1 ** ADAM_STEP
    c2 = 1.0 - ADAM_B2 ** ADAM_STEP

    def body(w_ref, g_ref, m_ref, v_ref, d_ref, nm_ref, nv_ref):
        gv = g_ref[...]
        nm = ADAM_B1 * m_ref[...] + (1.0 - ADAM_B1) * gv
        nv = ADAM_B2 * v_ref[...] + (1.0 - ADAM_B2) * (gv * gv)
        d_ref[...] = -ADAM_LR * ((nm / c1) / (jnp.sqrt(nv / c2) + ADAM_EPS) + ADAM_WD * w_ref[...])
        nm_ref[...] = nm
        nv_ref[...] = nv

    spec = pl.BlockSpec((tr, cols), lambda i: (i, 0))
    return _pcall(
        body, name=name, grid=(rows // tr,), in_specs=[spec] * 4, out_specs=[spec] * 3,
        out_shape=[jax.ShapeDtypeStruct(w.shape, F32)] * 3, compiler_params=_params("parallel"),
    )(w, g, m, v)


def _sum_rows(parts, out_dtype, name):
    rows, cols = parts[0].shape
    tr = rows
    for cand in range(16, rows, 16):
        if rows % cand == 0 and cand * cols * 4 <= (1 << 20):
            tr = cand
    n = len(parts)

    def body(*refs):
        acc = refs[0][...].astype(F32)
        for r in refs[1:n]:
            acc = acc + r[...].astype(F32)
        refs[n][...] = acc.astype(out_dtype)

    spec = pl.BlockSpec((tr, cols), lambda i: (i, 0))
    return _pcall(
        body, name=name, grid=(rows // tr,), in_specs=[spec] * n, out_specs=spec,
        out_shape=jax.ShapeDtypeStruct((rows, cols), out_dtype), compiler_params=_params("parallel"),
    )(*parts)


def _place():
    return lax.axis_index("x"), lax.axis_index("y"), lax.axis_index("c")


def _other_chips(x, y):
    return [(1 - x, y), (x, 1 - y), (1 - x, 1 - y)]


def _remote(src, dst, send_sem, recv_sem, to):
    return pltpu.make_async_remote_copy(src_ref=src, dst_ref=dst, send_sem=send_sem, recv_sem=recv_sem,
                                        device_id=to, device_id_type=MESH)


ANY = pl.BlockSpec(memory_space=pl.ANY)
DMA_CHUNK_BYTES = 512 * 1024


def _row_chunks(rows, row_bytes):
    per = max(16, DMA_CHUNK_BYTES // row_bytes // 16 * 16)
    return [(s, min(per, rows - s)) for s in range(0, rows, per)]


def _row_tile(rows, cols, align, limit=1 << 21):
    best = None
    for cand in range(align, rows + 1, align):
        if rows % cand == 0 and cand * cols * 4 <= limit:
            best = cand
    return best or rows


def _allgather_pieces(pieces):
    n = len(pieces)
    halves = [_row_chunks(p.shape[0] // 2, p.shape[1] * p.dtype.itemsize) for p in pieces]
    wholes = [_row_chunks(p.shape[0], p.shape[1] * p.dtype.itemsize) for p in pieces]
    entries = [(a, q, s, m, j) for a in range(n) for q, (s, m) in enumerate(halves[a]) for j in range(3)]
    slot = {(a, q, j): k for k, (a, q, _, _, j) in enumerate(entries)}
    n_ici = len(entries)
    n_loc = sum(len(w) for w in wholes)

    def body(*refs):
        ins, outs = refs[:n], refs[n:2 * n]
        send_sems, recv_sems, local_sems = refs[2 * n:]
        x, y, c = _place()
        me = 2 * x + y
        sibling = (x, y, 1 - c)
        chips = _other_chips(x, y)
        locals_ = []
        for a in range(n):
            for s, m in wholes[a]:
                loc = pltpu.make_async_copy(ins[a].at[pl.ds(s, m)], outs[a].at[me, pl.ds(s, m)],
                                            local_sems.at[len(locals_)])
                loc.start()
                locals_.append(loc)

        def landed(a, s, m, j, core):
            half = ins[a].shape[0] // 2
            return outs[a].at[2 * chips[j][0] + chips[j][1], pl.ds(core * half + s, m)]

        sent = []
        for k, (a, q, s, m, j) in enumerate(entries):
            if j < 2:
                half = ins[a].shape[0] // 2
                cp = _remote(ins[a].at[pl.ds(c * half + s, m)], outs[a].at[me, pl.ds(c * half + s, m)],
                             send_sems.at[k], recv_sems.at[k], (*chips[j], c))
                cp.start()
                sent.append(cp)

        def pass_to_sibling(k, blk):
            fw = _remote(blk, blk, send_sems.at[n_ici + k], recv_sems.at[n_ici + k], sibling)
            fw.start()
            sent.append(fw)

        for k, (a, q, s, m, j) in enumerate(entries):
            if j < 2:
                blk = landed(a, s, m, j, c)
                _remote(blk, blk, send_sems.at[k], recv_sems.at[k], (*chips[j], c)).wait_recv()
                pass_to_sibling(k, blk)
                first = q < (len(halves[a]) + 1) // 2
                if (j == 0) == first:
                    on = slot[(a, q, 2)]
                    rl = _remote(blk, blk, send_sems.at[on], recv_sems.at[on], (*chips[1 - j], c))
                    rl.start()
                    sent.append(rl)
        for k, (a, q, s, m, j) in enumerate(entries):
            if j == 2:
                blk = landed(a, s, m, j, c)
                _remote(blk, blk, send_sems.at[k], recv_sems.at[k], (*chips[j], c)).wait_recv()
                pass_to_sibling(k, blk)
        for k, (a, q, s, m, j) in enumerate(entries):
            blk = landed(a, s, m, j, 1 - c)
            _remote(blk, blk, send_sems.at[n_ici + k], recv_sems.at[n_ici + k], sibling).wait_recv()
        for cp in sent:
            cp.wait_send()
        for loc in locals_:
            loc.wait()

    return _pcall(
        body, name="allgather_weights", in_specs=[ANY] * n, out_specs=[ANY] * n,
        out_shape=[jax.ShapeDtypeStruct((4,) + p.shape, p.dtype) for p in pieces],
        scratch_shapes=[pltpu.SemaphoreType.DMA((2 * n_ici,)), pltpu.SemaphoreType.DMA((2 * n_ici,)),
                        pltpu.SemaphoreType.DMA((n_loc,))],
        compiler_params=pltpu.CompilerParams(has_side_effects=True),
    )(*pieces)


def _sibling_exchange(grads):
    n = len(grads)
    chunks = [_row_chunks(g.shape[1] // 2, g.shape[2] * g.dtype.itemsize) for g in grads]
    n_sem = 4 * sum(len(ch) for ch in chunks)

    def body(*refs):
        ins, gots = refs[:n], refs[n:2 * n]
        send_sems, recv_sems = refs[2 * n:]
        x, y, c = _place()
        sibling = (x, y, 1 - c)
        work = []
        for a in range(n):
            half = ins[a].shape[1] // 2
            for piece in range(4):
                for s, m in chunks[a]:
                    k = len(work)
                    cp = _remote(ins[a].at[piece, pl.ds((1 - c) * half + s, m)], gots[a].at[piece, pl.ds(s, m)],
                                 send_sems.at[k], recv_sems.at[k], sibling)
                    cp.start()
                    work.append(cp)
        for cp in work:
            cp.wait()

    return _pcall(
        body, name="grad_sibling_exchange", in_specs=[ANY] * n, out_specs=[ANY] * n,
        out_shape=[jax.ShapeDtypeStruct((4, g.shape[1] // 2, g.shape[2]), g.dtype) for g in grads],
        scratch_shapes=[pltpu.SemaphoreType.DMA((n_sem,)), pltpu.SemaphoreType.DMA((n_sem,))],
        compiler_params=pltpu.CompilerParams(has_side_effects=True),
    )(*grads)


def _sibling_gather(fulls):
    n = len(fulls)
    chunks = [_row_chunks(f.shape[0] // 2, f.shape[1] * f.dtype.itemsize) for f in fulls]
    n_sem = sum(len(ch) for ch in chunks)

    def body(*refs):
        outs = refs[n:2 * n]
        send_sems, recv_sems = refs[2 * n:]
        x, y, c = _place()
        sibling = (x, y, 1 - c)
        work = []
        for a in range(n):
            h = outs[a].shape[0] // 2
            for s, m in chunks[a]:
                k = len(work)
                mine = outs[a].at[pl.ds(c * h + s, m)]
                cp = _remote(mine, mine, send_sems.at[k], recv_sems.at[k], sibling)
                cp.start()
                work.append((a, s, m, cp))
        for k, (a, s, m, cp) in enumerate(work):
            h = outs[a].shape[0] // 2
            cp.wait_send()
            theirs = outs[a].at[pl.ds((1 - c) * h + s, m)]
            _remote(theirs, theirs, send_sems.at[k], recv_sems.at[k], sibling).wait_recv()

    return _pcall(
        body, name="grad_sibling_gather", in_specs=[ANY] * n, out_specs=[ANY] * n,
        out_shape=[jax.ShapeDtypeStruct(f.shape, f.dtype) for f in fulls],
        input_output_aliases={a: a for a in range(n)},
        scratch_shapes=[pltpu.SemaphoreType.DMA((n_sem,)), pltpu.SemaphoreType.DMA((n_sem,))],
        compiler_params=pltpu.CompilerParams(has_side_effects=True),
    )(*fulls)


def _pair_sum(grad, got, place, name):
    _, rows, cols = grad.shape
    half = rows // 2
    tr = _row_tile(half, cols, 16)

    def body(p_ref, a_ref, b_ref, o_ref):
        o_ref[...] = (a_ref[...].astype(F32) + b_ref[...].astype(F32)).astype(BF16)

    return _pcall(
        body, name=name,
        grid_spec=pltpu.PrefetchScalarGridSpec(
            num_scalar_prefetch=1, grid=(4, half // tr),
            in_specs=[pl.BlockSpec((None, tr, cols), lambda k, i, p: (k, p[1] * (half // tr) + i, 0)),
                      pl.BlockSpec((None, tr, cols), lambda k, i, p: (k, i, 0))],
            out_specs=pl.BlockSpec((None, tr, cols), lambda k, i, p: (k, i, 0))),
        out_shape=jax.ShapeDtypeStruct((4, half, cols), BF16),
        compiler_params=_params("parallel", "parallel"),
    )(place, grad, got)


def _chip_sum(sums, got, place, name):
    _, h, cols = sums.shape
    tr = _row_tile(h, cols, 16)

    def body(p_ref, own_ref, g0, g1, g2, o_ref):
        o_ref[...] = ((own_ref[...].astype(F32) + g0[...].astype(F32)) + g1[...].astype(F32)) + g2[...].astype(F32)

    gspec = lambda j: pl.BlockSpec((None, tr, cols), lambda i, p: (j, i, 0))
    return _pcall(
        body, name=name,
        grid_spec=pltpu.PrefetchScalarGridSpec(
            num_scalar_prefetch=1, grid=(h // tr,),
            in_specs=[pl.BlockSpec((None, tr, cols), lambda i, p: (p[0], i, 0)), gspec(0), gspec(1), gspec(2)],
            out_specs=pl.BlockSpec((tr, cols), lambda i, p: (p[1] * (h // tr) + i, 0))),
        out_shape=jax.ShapeDtypeStruct((2 * h, cols), F32),
        compiler_params=_params("parallel"),
    )(place, sums, got, got, got)


def _allgather8(buf, name):
    rows = buf.shape[0]

    def body(in_ref, out_ref, send_sems, recv_sems):
        x, y, c = _place()
        me = 4 * x + 2 * y + c
        out_ref[me] = in_ref[...]
        work = []
        for rel in range(1, 8):
            fx, fy, fc = (rel >> 2) & 1, (rel >> 1) & 1, rel & 1
            to = (x ^ fx, y ^ fy, c ^ fc)
            cp = _remote(in_ref, out_ref.at[me], send_sems.at[rel - 1], recv_sems.at[rel - 1], to)
            cp.start()
            work.append((cp, 4 * to[0] + 2 * to[1] + to[2]))
        for rel, (cp, frm) in enumerate(work):
            cp.wait_send()
            blk = out_ref.at[frm]
            _remote(blk, blk, send_sems.at[rel], recv_sems.at[rel], (x, y, c)).wait_recv()

    return _pcall(
        body, name=name, in_specs=[pl.BlockSpec(memory_space=pltpu.VMEM)],
        out_specs=pl.BlockSpec(memory_space=pltpu.VMEM),
        out_shape=jax.ShapeDtypeStruct((8, rows, LANE), F32),
        scratch_shapes=[pltpu.SemaphoreType.DMA((7,)), pltpu.SemaphoreType.DMA((7,))],
        compiler_params=pltpu.CompilerParams(has_side_effects=True),
    )(buf)


def _pack_rows(arrs):
    parts = []
    for a in arrs:
        f = a.reshape(-1).astype(F32)
        parts.append(jnp.pad(f, (0, (-f.shape[0]) % LANE)))
    flat = jnp.concatenate(parts)
    rows = -(-flat.shape[0] // LANE)
    rows8 = -(-rows // 8) * 8
    return jnp.pad(flat, (0, rows8 * LANE - flat.shape[0])).reshape(rows8, LANE)


def _unpack_rows(buf, shapes):
    flat = buf.reshape(-1)
    outs, off = [], 0
    for s in shapes:
        n = int(np.prod(s))
        outs.append(flat[off:off + n].reshape(s))
        off += -(-n // LANE) * LANE
    return outs


def _local_grads(x, p, target, wseg, w_br16, w_out16, w_ple16, b_gate, conv_w, conv_b, dt_bias, a_log, d_skip,
                 ssm_norm_w, ln_g, ln_b, rel_bias, finish_dx):
    nb, seq, _ = x.shape
    bmaps = jnp.asarray(_bucket_maps())
    bias = _bias_tables(rel_bias, bmaps)
    bgate8 = jnp.pad(b_gate, ((0, 5), (0, 0)))
    dils = [d for _, d in PATTERNS]

    x16 = x.astype(BF16)
    p16 = p.astype(BF16)
    x16p = [_permute(x16, d) for d in dils]
    qkv = [_proj(x16p[g], [wseg["qkv%d" % g]], BF16, "proj_qkv%d" % g, True)[0].reshape(
        nb, dils[g], seq // dils[g], -1) for g in range(3)]
    nat = {}
    for gi, (group, tm) in enumerate(NAT_GROUPS):
        outs = _proj(x16, [wseg[s] for s in group], F32, "proj_nat%d" % gi, True, tm)
        nat.update(zip(group, outs))
    att = [_attn_fwd(qkv[g], bias[g * GROUP_HEADS:(g + 1) * GROUP_HEADS], dils[g], "attn_fwd%d" % g) for g in range(3)]
    natural = lambda t, g: _unpermute(t.reshape(nb, seq, t.shape[-1]), dils[g])
    oa, o_att, lse = _combine_fwd(att[0][0], att[0][1], natural(att[1][0], 1), natural(att[1][1], 1),
                                  natural(att[2][0], 2), natural(att[2][1], 2), nat["gatt"])

    conv_wg, conv_bg = _xbc_group_order(conv_w), _xbc_group_order(conv_b)
    act = _conv_fwd(nat["xbc"], conv_wg, conv_bg, "conv_fwd")
    dt_sp, dt_sg = _softplus_sig(nat["dt"], jnp.pad(dt_bias, ((0, 0), (0, LANE - SSM_HEADS))))
    dtg, sgg = _group_lanes(dt_sp), _group_lanes(dt_sg)
    alog_g, dskip_g = _group_lanes(a_log), _group_lanes(d_skip)
    y_ssm, y_all, sprev = _ssd_fwd(act, dtg, nat["z"], alog_g, dskip_g, ssm_norm_w)

    w_bra, w_brb = w_br16[:ATT_OUT], w_br16[ATT_OUT:]
    y_a, = _proj(oa, [w_bra], F32, "proj_ya")
    y_b, = _proj(y_ssm, [w_brb], F32, "proj_yb")
    merged = _merge_fwd(y_a, y_b, nat["gm"], bgate8)
    mix, = _proj(merged, [w_out16], F32, "proj_mix")
    pw, = _proj(p16, [w_ple16], F32, "proj_ple")

    dx, dpre16, dpw16, dgp16, ln_sums = _ln_loss(x, mix, nat["gp"], pw, target, bgate8, ln_g, ln_b)
    loss_sum = (0.5 / D_MODEL) * jnp.sum(ln_sums[3])
    dmerged = _dx([dpre16], [w_out16], [], "dx_merged")
    dya16, dyb16, dgm16, mg_sums = _merge_bwd(dmerged, y_a, y_b, nat["gm"], bgate8)
    doa = _dx([dya16], [w_bra], [], "dx_oa")
    dys = _dx([dyb16], [w_brb], [], "dx_yssm")
    g_w_out, = _dw(merged, [dpre16], BF16, "dw_out")
    g_w_br = jnp.concatenate([_dw(oa, [dya16], BF16, "dw_bra")[0], _dw(y_ssm, [dyb16], BF16, "dw_brb")[0]], axis=0)
    g_w_ple, = _dw(p16, [dpw16], BF16, "dw_ple")

    do_att, do16, stats, dgatt16 = _combine_bwd(doa, nat["gatt"], o_att, lse)
    dseg = {"gatt": dgatt16, "gm": dgm16, "gp": dgp16}
    dbias = []
    for g in range(3):
        own_order = lambda t: _permute(t, dils[g]).reshape(nb, dils[g], seq // dils[g], t.shape[-1])
        cotangent = (do_att, o_att, lse) if g == 0 else (own_order(do16), own_order(stats))
        dqkv, db = _attn_bwd(qkv[g], bias[g * GROUP_HEADS:(g + 1) * GROUP_HEADS], cotangent, dils[g],
                             "attn_bwd%d" % g)
        dseg["qkv%d" % g] = dqkv.reshape(nb, seq, -1)
        dbias.append(db)
    g_rel = _bias_grad(jnp.concatenate(dbias, axis=0), bmaps)[:, 0, :NUM_BUCKETS].T

    dact, ddtg, dz, ssd_small, g_normw = _ssd_bwd(
        act, dtg, sgg, nat["z"], y_all, dys, sprev, alog_g, dskip_g, ssm_norm_w)
    dseg["z"] = dz
    dseg["dt"] = jnp.pad(_ungroup_lanes(ddtg), ((0, 0), (0, 0), (0, LANE - SSM_HEADS)))
    dpre, conv_sums = _conv_bwd_pre(dact, nat["xbc"], conv_wg, conv_bg, "conv_bwd")
    dseg["xbc"] = _conv_bwd_x(dpre, conv_wg, "conv_bwd_x")
    csum = _xbc_reference_order(conv_sums)

    dx_perm = [_unpermute(_dx([dseg["qkv%d" % g]], [wseg["qkv%d" % g]], [], "dx_qkv%d" % g, True), dils[g])
               for g in (1, 2)]
    dwseg = {"qkv%d" % g: _dw(x16p[g], [dseg["qkv%d" % g]], BF16, "dw_qkv%d" % g, True)[0] for g in range(3)}
    for gi, group in enumerate(DW_GROUPS):
        dwseg.update(zip(group, _dw(x16, [dseg[s] for s in group], BF16, "dw_nat%d" % gi, True)))
    names = ["qkv0"] + [s for group, _ in NAT_GROUPS for s in group]
    dx = finish_dx([dseg[s] for s in names], [wseg[s] for s in names], [dx] + dx_perm, dwseg, g_w_br, g_w_out, g_w_ple)

    small = dict(
        b_gate=jnp.stack([mg_sums[0], mg_sums[1], ln_sums[2]]),
        conv_w=csum[0:4], conv_b=csum[4:5],
        dt_bias=_ungroup_lanes(ssd_small[:, 2:3, :]), a_log=_ungroup_lanes(ssd_small[:, 0:1, :]),
        d_skip=_ungroup_lanes(ssd_small[:, 1:2, :]), ssm_norm_w=g_normw,
        ln_g=ln_sums[0:1], ln_b=ln_sums[1:2], rel_bias=g_rel)
    return loss_sum, dx, small


DX_TM = 256
SMALL_ORDER = ("b_gate", "conv_w", "conv_b", "dt_bias", "a_log", "d_skip", "ssm_norm_w", "ln_g", "ln_b", "rel_bias")
SMALL_FULL_SHAPES = dict(b_gate=(3, 1024), conv_w=(4, 3072), conv_b=(1, 3072), dt_bias=(1, 32), a_log=(1, 32),
                         d_skip=(1, 32), ssm_norm_w=(1, 2048), ln_g=(1, 1024), ln_b=(1, 1024), rel_bias=(32, 36))


def kernel(x, p, w_in, b_gate, conv_w, conv_b, dt_bias, a_log, d_skip, ssm_norm_w, w_branch, w_out, w_ple, ln_g, ln_b, rel_bias, loss_target, m_w_in, m_b_gate, m_conv_w, m_conv_b, m_dt_bias, m_a_log, m_d_skip, m_ssm_norm_w, m_w_branch, m_w_out, m_w_ple, m_ln_g, m_ln_b, m_rel_bias, v_w_in, v_b_gate, v_conv_w, v_conv_b, v_dt_bias, v_a_log, v_d_skip, v_ssm_norm_w, v_w_branch, v_w_out, v_w_ple, v_ln_g, v_ln_b, v_rel_bias):
    cx, cy, cc = _place()
    chip = 2 * cx + cy
    dev = 4 * cx + 2 * cy + cc

    w_in_t = jnp.transpose(w_in[0])
    win16 = _shard_to_window(w_in_t, chip)
    g_win, g_br, g_out, g_ple = _allgather_pieces(
        [win16, w_branch[0].astype(BF16), w_out[0].astype(BF16), w_ple[0].astype(BF16)])
    wseg = _assemble(g_win)
    w_br16 = g_br.reshape(4 * 704, D_MODEL)
    w_out16 = g_out.reshape(D_MODEL, D_MODEL)
    w_ple16 = jnp.transpose(g_ple, (1, 0, 2)).reshape(PLE_DIM, D_MODEL)
    shards = _allgather8(_pack_rows([b_gate[0], conv_w[0]]), "allgather_small_params")
    per_chip = [_unpack_rows(shards[2 * k], [(3, 256), (4, 768)]) for k in range(4)]
    b_gate_full = jnp.concatenate([pc[0] for pc in per_chip], axis=1)
    conv_w_full = jnp.concatenate([pc[1] for pc in per_chip], axis=1)

    place = jnp.stack([chip, cc]).astype(jnp.int32)
    reduced = []

    def finish_dx(dhs, ws, accs, dwseg, d_br, d_out, d_ple):
        grads = [_pack(dwseg), d_br.reshape(4, 704, D_MODEL), d_out.reshape(4, 256, D_MODEL),
                 jnp.transpose(d_ple.reshape(PLE_DIM, 4, 256), (1, 0, 2))]
        got = _sibling_exchange(grads)
        chip_sums = [_pair_sum(g, t, place, "grad_pair_sum_%d" % i) for i, (g, t) in enumerate(zip(grads, got))]
        dx, others = _dx(dhs, ws, accs, "dx_w_in_and_grad_chip_scatter", True, DX_TM, chip_sums)
        fulls = [_chip_sum(s, t, place, "grad_chip_sum_%d" % i) for i, (s, t) in enumerate(zip(chip_sums, others))]
        reduced.extend(_sibling_gather(fulls))
        return dx

    loss_sum, grad_x, small = _local_grads(
        x, p[0], loss_target, wseg, w_br16, w_out16, w_ple16, b_gate_full, conv_w_full, conv_b, dt_bias, a_log,
        d_skip, ssm_norm_w, ln_g, ln_b, rel_bias, finish_dx)
    loss = lax.psum(loss_sum, ("x", "y", "c"))
    big = reduced
    g_w_in = _window_to_shard(big[0], chip)
    g_w_branch, g_w_out, g_w_ple = big[1], big[2], big[3]
    parts = _allgather8(_pack_rows([small[n] for n in SMALL_ORDER]), "allgather_small_grads")
    small_sum = _sum_rows([parts[i] for i in range(8)], F32, "small_grad_sum")
    sg = dict(zip(SMALL_ORDER, _unpack_rows(small_sum, [SMALL_FULL_SHAPES[n] for n in SMALL_ORDER])))
    sg["b_gate"] = lax.dynamic_slice_in_dim(sg["b_gate"], chip * 256, 256, axis=1)
    sg["conv_w"] = lax.dynamic_slice_in_dim(sg["conv_w"], chip * 768, 768, axis=1)
    del dev

    upd = {}
    upd["w_in"] = [jnp.transpose(t) for t in _adamw(w_in_t, g_w_in, jnp.transpose(m_w_in[0]),
                                                      jnp.transpose(v_w_in[0]), "adamw_w_in")]
    upd["w_branch"] = _adamw(w_branch[0], g_w_branch, m_w_branch[0], v_w_branch[0], "adamw_w_branch")
    upd["w_out"] = _adamw(w_out[0], g_w_out, m_w_out[0], v_w_out[0], "adamw_w_out")
    upd["w_ple"] = _adamw(w_ple[0], g_w_ple, m_w_ple[0], v_w_ple[0], "adamw_w_ple")
    small_w = dict(b_gate=b_gate, conv_w=conv_w, conv_b=conv_b, dt_bias=dt_bias, a_log=a_log, d_skip=d_skip,
                   ssm_norm_w=ssm_norm_w, ln_g=ln_g, ln_b=ln_b, rel_bias=rel_bias)
    small_m = dict(b_gate=m_b_gate, conv_w=m_conv_w, conv_b=m_conv_b, dt_bias=m_dt_bias, a_log=m_a_log,
                   d_skip=m_d_skip, ssm_norm_w=m_ssm_norm_w, ln_g=m_ln_g, ln_b=m_ln_b, rel_bias=m_rel_bias)
    small_v = dict(b_gate=v_b_gate, conv_w=v_conv_w, conv_b=v_conv_b, dt_bias=v_dt_bias, a_log=v_a_log,
                   d_skip=v_d_skip, ssm_norm_w=v_ssm_norm_w, ln_g=v_ln_g, ln_b=v_ln_b, rel_bias=v_rel_bias)
    shapes = [small_w[n].shape for n in SMALL_ORDER]
    s_delta, s_m, s_v = _adamw(_pack_rows([small_w[n] for n in SMALL_ORDER]), _pack_rows([sg[n] for n in SMALL_ORDER]),
                               _pack_rows([small_m[n] for n in SMALL_ORDER]), _pack_rows([small_v[n] for n in SMALL_ORDER]),
                               "adamw_small")
    for i, n in enumerate(SMALL_ORDER):
        upd[n] = tuple(_unpack_rows(t, shapes)[i] for t in (s_delta, s_m, s_v))
        sg[n] = sg[n].reshape(small_w[n].shape)

    order = ("w_in", "b_gate", "conv_w", "conv_b", "dt_bias", "a_log", "d_skip", "ssm_norm_w", "w_branch", "w_out",
             "w_ple", "ln_g", "ln_b", "rel_bias")
    grads = dict(sg, w_in=jnp.transpose(g_w_in)[None],w_branch=g_w_branch[None], w_out=g_w_out[None], w_ple=g_w_ple[None])
    lead = lambda n, t: t[None] if n in ("w_in", "w_branch", "w_out", "w_ple") else t
    return (loss, grad_x, *[grads[n] for n in order], *[lead(n, upd[n][0]) for n in order],
            *[lead(n, upd[n][1]) for n in order], *[lead(n, upd[n][2]) for n in order])
```

```python
import functools
import math

import numpy as np
import jax
import jax.numpy as jnp
from jax import lax
from jax.experimental import pallas as pl
from jax.experimental.pallas import tpu as pltpu

F32, BF16 = jnp.float32, jnp.bfloat16

D_MODEL = 1024
HEAD_DIM = 64
GROUP_HEADS = 12
ATT_OUT = GROUP_HEADS * HEAD_DIM
PATTERNS = ((128, 1), (512, 4), (2048, 16))
BAND = 128
NUM_BUCKETS = 32
MAX_DISTANCE = 2048
D_INNER = 2048
SSM_HEADS = 32
SSM_GROUPS = 4
GROUP_SSM_HEADS = SSM_HEADS // SSM_GROUPS
D_STATE = 128
CHUNK = 128
PLE_DIM = 256
ALPHA = 2.0 ** 0.25
LN_EPS = 1e-5
RMS_EPS = 1e-5
ADAM_LR, ADAM_B1, ADAM_B2, ADAM_EPS, ADAM_WD, ADAM_STEP = 0.001, 0.9, 0.999, 1e-08, 0.01, 10
NEG = -1e30

QKV_W = 3 * ATT_OUT
IN_COLS = 15904
SHARD_COLS = IN_COLS // 4
DT_COL = 12800
ROW_TILE = 16
WIN_ROWS = 4000


def _win_offset(k):
    return (k * SHARD_COLS) % ROW_TILE


def _win_start(k):
    return k * SHARD_COLS - _win_offset(k)

VMEM_LIMIT_BYTES = 56 * 1024 * 1024
LANE = 128
MESH = pl.DeviceIdType.MESH
NT = (((1,), (1,)), ((), ()))
TN = (((0,), (0,)), ((), ()))


def _pcall(body, **kw):
    return pl.pallas_call(body, **kw)


def _params(*sem):
    return pltpu.CompilerParams(dimension_semantics=sem, vmem_limit_bytes=VMEM_LIMIT_BYTES)


def _sigmoid(v):
    return jax.nn.sigmoid(v)


MM_TM = 512


def _permute(t, d):
    nb, seq, ch = t.shape
    return t if d == 1 else t.reshape(nb, seq // d, d, ch).transpose(0, 2, 1, 3).reshape(nb, seq, ch)


def _unpermute(t, d):
    nb, seq, ch = t.shape
    return t if d == 1 else t.reshape(nb, d, seq // d, ch).transpose(0, 2, 1, 3).reshape(nb, seq, ch)


def _tok_spec(tm, width):
    return pl.BlockSpec((None, tm, width), lambda b, i: (b, i, 0))


def _whole(arr, single_buffer=False):
    mode = dict(pipeline_mode=pl.Buffered(1)) if single_buffer else {}
    return pl.BlockSpec(arr.shape, lambda b, i: (0,) * arr.ndim, **mode)


def _proj(a3, ws, out_dtype, name, w_rows_are_outputs=False, tm=MM_TM):
    nb, seq, kdim = a3.shape
    nw = len(ws)
    widths = [w.shape[0] if w_rows_are_outputs else w.shape[1] for w in ws]

    def body(*refs):
        a = refs[0][...].astype(BF16)
        for w_ref, o_ref in zip(refs[1:1 + nw], refs[1 + nw:]):
            if w_rows_are_outputs:
                v = lax.dot_general(a, w_ref[...], NT, preferred_element_type=F32)
            else:
                v = jnp.dot(a, w_ref[...], preferred_element_type=F32)
            o_ref[...] = v.astype(out_dtype)

    return _pcall(
        body, name=name, grid=(nb, seq // tm),
        in_specs=[_tok_spec(tm, kdim)] + [_whole(w) for w in ws],
        out_specs=[_tok_spec(tm, n) for n in widths],
        out_shape=[jax.ShapeDtypeStruct((nb, seq, n), out_dtype) for n in widths],
        compiler_params=_params("parallel", "parallel"),
    )(a3, *ws)


def _dx(dhs, ws, accs, name, w_rows_are_outputs=False, tm=MM_TM, scatter=None):
    nb, seq, _ = dhs[0].shape
    nd, nacc = len(dhs), len(accs)
    kout = ws[0].shape[1] if w_rows_are_outputs else ws[0].shape[0]
    sums = scatter or []
    ns = len(sums)
    chunks = [_row_chunks(s.shape[1], s.shape[2] * s.dtype.itemsize) for s in sums]
    n_sem = 3 * sum(len(ch) for ch in chunks)
    grid = (nb, seq // tm)

    def body(*refs):
        n_in = 2 * nd + nacc
        sum_refs, o_ref, got_refs = refs[n_in:n_in + ns], refs[n_in + ns], refs[n_in + ns + 1:n_in + 2 * ns + 1]

        def copies():
            send_sems, recv_sems = refs[-2], refs[-1]
            x, y, c = _place()
            out = []
            for a in range(ns):
                for s, m in chunks[a]:
                    for j, (cx, cy) in enumerate(_other_chips(x, y)):
                        k = len(out)
                        out.append(_remote(sum_refs[a].at[2 * cx + cy, pl.ds(s, m)], got_refs[a].at[j, pl.ds(s, m)],
                                           send_sems.at[k], recv_sems.at[k], (cx, cy, c)))
            return out

        if ns:
            @pl.when((pl.program_id(0) == 0) & (pl.program_id(1) == 0))
            def _():
                for cp in copies():
                    cp.start()

        v = None
        for dh_ref, w_ref in zip(refs[:nd], refs[nd:2 * nd]):
            dh = dh_ref[...].astype(BF16)
            if w_rows_are_outputs:
                t = jnp.dot(dh, w_ref[...], preferred_element_type=F32)
            else:
                t = lax.dot_general(dh, w_ref[...], NT, preferred_element_type=F32)
            v = t if v is None else v + t
        for a_ref in refs[2 * nd:n_in]:
            v = v + a_ref[...]
        o_ref[...] = v

        if ns:
            @pl.when((pl.program_id(0) == grid[0] - 1) & (pl.program_id(1) == grid[1] - 1))
            def _():
                for cp in copies():
                    cp.wait()

    out = _pcall(
        body, name=name, grid=grid,
        in_specs=[_tok_spec(tm, dh.shape[-1]) for dh in dhs] + [_whole(w, bool(ns)) for w in ws]
        + [_tok_spec(tm, kout)] * nacc + [ANY] * ns,
        out_specs=[_tok_spec(tm, kout)] + [ANY] * ns,
        out_shape=[jax.ShapeDtypeStruct((nb, seq, kout), F32)]
        + [jax.ShapeDtypeStruct((3,) + s.shape[1:], s.dtype) for s in sums],
        input_output_aliases={2 * nd: 0} if nacc else {},
        scratch_shapes=[pltpu.SemaphoreType.DMA((n_sem,)), pltpu.SemaphoreType.DMA((n_sem,))] if ns else [],
        compiler_params=pltpu.CompilerParams(
            dimension_semantics=("arbitrary", "arbitrary") if ns else ("parallel", "parallel"),
            vmem_limit_bytes=VMEM_LIMIT_BYTES, has_side_effects=bool(ns)),
    )(*dhs, *ws, *accs, *sums)
    return (out[0], list(out[1:])) if ns else out[0]


def _dw(a3, dhs, out_dtype, name, rows_are_outputs=False):
    nb, seq, kdim = a3.shape
    nd = len(dhs)
    grid = (nb, seq // MM_TM)
    shapes = [(dh.shape[-1], kdim) if rows_are_outputs else (kdim, dh.shape[-1]) for dh in dhs]

    def body(*refs):
        b, i = pl.program_id(0), pl.program_id(1)
        dh_refs, o_refs, acc_refs = refs[1:1 + nd], refs[1 + nd:1 + 2 * nd], refs[1 + 2 * nd:]

        @pl.when((b == 0) & (i == 0))
        def _():
            for acc_ref in acc_refs:
                acc_ref[...] = jnp.zeros_like(acc_ref)

        a = refs[0][...].astype(BF16)
        for dh_ref, acc_ref in zip(dh_refs, acc_refs):
            dh = dh_ref[...].astype(BF16)
            acc_ref[...] += lax.dot_general(*((dh, a) if rows_are_outputs else (a, dh)), TN,
                                            preferred_element_type=F32)

        @pl.when((b == grid[0] - 1) & (i == grid[1] - 1))
        def _():
            for o_ref, acc_ref in zip(o_refs, acc_refs):
                o_ref[...] = acc_ref[...].astype(out_dtype)

    return _pcall(
        body, name=name, grid=grid,
        in_specs=[_tok_spec(MM_TM, kdim)] + [_tok_spec(MM_TM, dh.shape[-1]) for dh in dhs],
        out_specs=[pl.BlockSpec(s, lambda b, i: (0, 0)) for s in shapes],
        out_shape=[jax.ShapeDtypeStruct(s, out_dtype) for s in shapes],
        scratch_shapes=[pltpu.VMEM(s, F32) for s in shapes],
        compiler_params=_params("arbitrary", "arbitrary"),
    )(a3, *dhs)


def _qkv_rows(g):
    return [(part * QKV_W + g * ATT_OUT + hp * LANE, LANE) for hp in range(ATT_OUT // LANE) for part in range(3)]


XBC_START = 3 * QKV_W + ATT_OUT + D_INNER
GROUP_CH = GROUP_SSM_HEADS * HEAD_DIM
XBC_GROUP = GROUP_CH + 2 * D_STATE
CONV_DIM = SSM_GROUPS * XBC_GROUP


def _xbc_ranges():
    out = []
    for g in range(SSM_GROUPS):
        out += [(g * GROUP_CH, GROUP_CH), (D_INNER + g * D_STATE, D_STATE),
                (D_INNER + SSM_GROUPS * D_STATE + g * D_STATE, D_STATE)]
    return out


def _xbc_group_order(t):
    return jnp.concatenate([t[..., s:s + n] for s, n in _xbc_ranges()], axis=-1)


def _xbc_reference_order(t):
    g = lambda off, n: [t[..., k * XBC_GROUP + off:k * XBC_GROUP + off + n] for k in range(SSM_GROUPS)]
    return jnp.concatenate(g(0, GROUP_CH) + g(GROUP_CH, D_STATE) + g(GROUP_CH + D_STATE, D_STATE), axis=-1)


def _segments():
    one = lambda name, start, rows: (name, [(start, rows)], max(rows, LANE))
    return [("qkv%d" % g, _qkv_rows(g), QKV_W) for g in range(3)] + [
        one("gatt", 3 * QKV_W, ATT_OUT), one("z", 3 * QKV_W + ATT_OUT, D_INNER),
        ("xbc", [(XBC_START + s, n) for s, n in _xbc_ranges()], CONV_DIM), one("dt", DT_COL, SSM_HEADS),
        one("gm", DT_COL + SSM_HEADS, 2 * D_MODEL), one("gp", DT_COL + SSM_HEADS + 2 * D_MODEL, D_MODEL)]


LAYOUT_TC = 256
NAT_GROUPS = ((("gatt", "z", "dt", "gp"), 512), (("xbc", "gm"), 256))
DW_GROUPS = (("gatt", "z", "dt", "gp"), ("xbc",), ("gm",))


def _assemble(win):
    segs = _segments()

    def body(win_ref, *outs):
        def pieces(start, rows):
            t, end = start, start + rows
            while t < end:
                k = min(t // SHARD_COLS, 3)
                shard_end = (k + 1) * SHARD_COLS
                if k < 3 and shard_end % ROW_TILE and t == shard_end - shard_end % ROW_TILE:
                    lo = t - _win_start(k)
                    yield win_ref[k, lo:lo + ROW_TILE, :] + win_ref[k + 1, 0:ROW_TILE, :]
                    t += ROW_TILE
                    continue
                upto = min(end, shard_end - shard_end % ROW_TILE if k < 3 else end)
                yield win_ref[k, t - _win_start(k):upto - _win_start(k), :]
                t = upto

        for (_, ranges, total), o_ref in zip(segs, outs):
            off = 0
            for start, rows in ranges:
                for part in pieces(start, rows):
                    o_ref[off:off + part.shape[0], :] = part
                    off += part.shape[0]
            if off < total:
                o_ref[off:total, :] = jnp.zeros((total - off, o_ref.shape[1]), BF16)

    outs = _pcall(
        body, name="assemble_w_in", grid=(D_MODEL // LAYOUT_TC,),
        in_specs=[pl.BlockSpec((4, WIN_ROWS, LAYOUT_TC), lambda i: (0, 0, i))],
        out_specs=[pl.BlockSpec((total, LAYOUT_TC), lambda i: (0, i)) for _, _, total in segs],
        out_shape=[jax.ShapeDtypeStruct((total, D_MODEL), BF16) for _, _, total in segs],
        compiler_params=_params("parallel"),
    )(win)
    return {name: o for (name, _, _), o in zip(segs, outs)}


def _pack(dsegs):
    segs = _segments()

    def body(*refs):
        ins, o_ref = refs[:-1], refs[-1]
        tail = IN_COLS - _win_start(3)
        o_ref[3, tail:, :] = jnp.zeros((WIN_ROWS - tail, o_ref.shape[2]), BF16)
        for (_, ranges, _), s_ref in zip(segs, ins):
            off = 0
            for start, rows in ranges:
                for k in range(4):
                    lo = _win_start(k)
                    a, b = max(start, lo), min(start + rows, lo + WIN_ROWS)
                    if a < b:
                        o_ref[k, a - lo:b - lo, :] = s_ref[off + a - start:off + b - start, :]
                off += rows

    return _pcall(
        body, name="pack_dw_in", grid=(D_MODEL // LAYOUT_TC,),
        in_specs=[pl.BlockSpec((total, LAYOUT_TC), lambda i: (0, i)) for _, _, total in segs],
        out_specs=pl.BlockSpec((4, WIN_ROWS, LAYOUT_TC), lambda i: (0, 0, i)),
        out_shape=jax.ShapeDtypeStruct((4, WIN_ROWS, D_MODEL), BF16),
        compiler_params=_params("parallel"),
    )(*[dsegs[name] for name, _, _ in segs])


def _shard_to_window(shard_t, k):
    def at(off):
        return lambda w: jnp.pad(w.astype(BF16), ((off, WIN_ROWS - SHARD_COLS - off), (0, 0)))

    return lax.cond(k % 2 == 1, at(_win_offset(1)), at(_win_offset(0)), shard_t)


def _window_to_shard(win, k):
    return lax.dynamic_slice(win, ((k % 2) * _win_offset(1), 0), (SHARD_COLS, D_MODEL))


def _bucket_maps():
    qi = np.arange(BAND)[:, None]
    kj = np.arange(2 * BAND)[None, :]
    delta = qi + BAND - kj
    maps = []
    for window, dil in PATTERNS:
        valid = (delta >= 0) & (delta <= window // dil)
        dist = np.maximum(delta, 0) * dil
        max_exact = NUM_BUCKETS // 2
        d_f = np.maximum(dist, 1).astype(np.float32)
        large = max_exact + (np.log(d_f / np.float32(max_exact)) / np.float32(math.log(MAX_DISTANCE / max_exact))
                             * np.float32(NUM_BUCKETS - max_exact)).astype(np.int32)
        large = np.minimum(large, NUM_BUCKETS - 1)
        bucket = np.where(dist < max_exact, dist, large)
        maps.append(np.where(valid, bucket, -1).astype(np.int32))
    return np.stack(maps)


def _bias_tables(rel_bias, bmaps):
    def body(rb_ref, bm_ref, o_ref):
        h = pl.program_id(0)
        bm = bm_ref[...]
        acc = jnp.full(bm.shape, NEG, F32)
        for b in range(NUM_BUCKETS):
            acc = jnp.where(bm == b, rb_ref[b, h], acc)
        o_ref[...] = acc

    return _pcall(
        body, name="bias_tables", grid=(3 * GROUP_HEADS,),
        in_specs=[pl.BlockSpec(memory_space=pltpu.SMEM),
                  pl.BlockSpec((None, BAND, 2 * BAND), lambda h: (h // GROUP_HEADS, 0, 0))],
        out_specs=pl.BlockSpec((None, BAND, 2 * BAND), lambda h: (h, 0, 0)),
        out_shape=jax.ShapeDtypeStruct((3 * GROUP_HEADS, BAND, 2 * BAND), F32),
        compiler_params=_params("parallel"),
    )(rel_bias, bmaps)


def _bias_grad(dbias, bmaps):
    def body(db_ref, bm_ref, o_ref):
        bm = bm_ref[...]
        db = db_ref[...]
        lane = lax.broadcasted_iota(jnp.int32, (1, LANE), 1)
        vec = jnp.zeros((1, LANE), F32)
        for b in range(NUM_BUCKETS):
            s = jnp.sum(jnp.where(bm == b, db, 0.0), keepdims=True)
            vec = jnp.where(lane == b, s, vec)
        o_ref[...] = vec

    return _pcall(
        body, name="bias_grad", grid=(3 * GROUP_HEADS,),
        in_specs=[pl.BlockSpec((None, BAND, 2 * BAND), lambda h: (h, 0, 0)),
                  pl.BlockSpec((None, BAND, 2 * BAND), lambda h: (h // GROUP_HEADS, 0, 0))],
        out_specs=pl.BlockSpec((None, 1, LANE), lambda h: (h, 0, 0)),
        out_shape=jax.ShapeDtypeStruct((3 * GROUP_HEADS, 1, LANE), F32),
        compiler_params=_params("parallel"),
    )(dbias, bmaps)


def _rows(n):
    if isinstance(n, int):
        return pl.ds(n * BAND, BAND)
    return pl.ds(pl.multiple_of(n * BAND, BAND), BAND)


def _for_blocks(blocks, nblk, per, carry):
    carry = blocks([0], carry, False)
    start = 1 + (nblk - 1) % per
    for n in range(1, start):
        carry = blocks([n], carry, True)
    trips = (nblk - start) // per
    if trips > 0:
        carry = lax.fori_loop(
            0, trips, lambda t, c: blocks([start + t * per + u for u in range(per)], c, True), carry)
    return carry


def _pairs_per_step(d):
    return {1: 1, 4: 6, 16: 6}[d]


def _head_cols(i, h, part):
    base = 3 * LANE * i + part * LANE + h * HEAD_DIM
    return slice(base, base + HEAD_DIM)


def _attn_fwd(qkv4, bias, d, name):
    nb, _, sub, _ = qkv4.shape
    nblk = sub // BAND
    scale = HEAD_DIM ** -0.5
    npair = ATT_OUT // LANE
    hps = _pairs_per_step(d)
    compact = d > 1

    def body(qkv_ref, bias_ref, o_ref, l_ref):
        def blocks(ns, carry, with_prev):
            chains = [(bi, i, h) for bi in range(len(ns)) for i in range(hps) for h in range(2)]
            scores = []
            for bi, i, h in chains:
                n = ns[bi]
                q = qkv_ref[_rows(n), _head_cols(i, h, 0)] * scale
                s_c = lax.dot_general(q, qkv_ref[_rows(n), _head_cols(i, h, 1)], NT,
                                      preferred_element_type=F32) + bias_ref[2 * i + h, :, BAND:]
                s_p = None
                if with_prev:
                    s_p = lax.dot_general(q, qkv_ref[_rows(n - 1), _head_cols(i, h, 1)], NT,
                                          preferred_element_type=F32) + bias_ref[2 * i + h, :, :BAND]
                scores.append((s_c, s_p))
            probs = []
            for s_c, s_p in scores:
                m = jnp.max(s_c, -1, keepdims=True)
                if with_prev:
                    m = jnp.maximum(m, jnp.max(s_p, -1, keepdims=True))
                e_c = jnp.exp(s_c - m)
                den = jnp.sum(e_c, -1, keepdims=True)
                e_p = None
                if with_prev:
                    e_p = jnp.exp(s_p - m)
                    den = den + jnp.sum(e_p, -1, keepdims=True)
                    e_p = e_p.astype(BF16)
                probs.append((e_c.astype(BF16), e_p, den, m))
            outs = {}
            for (bi, i, h), (e_c, e_p, den, m) in zip(chains, probs):
                n = ns[bi]
                acc = jnp.dot(e_c, qkv_ref[_rows(n), _head_cols(i, h, 2)], preferred_element_type=F32)
                if with_prev:
                    acc = acc + jnp.dot(e_p, qkv_ref[_rows(n - 1), _head_cols(i, h, 2)], preferred_element_type=F32)
                outs[(bi, i, h)] = (acc / den, jnp.broadcast_to(m + jnp.log(den), (BAND, HEAD_DIM)))
            lane = lax.broadcasted_iota(jnp.int32, (BAND, LANE), 1)
            for bi, n in enumerate(ns):
                per_head = jnp.zeros((BAND, LANE), F32)
                for i in range(hps):
                    o_ref[_rows(n), i * LANE:(i + 1) * LANE] = jnp.concatenate(
                        [outs[(bi, i, 0)][0], outs[(bi, i, 1)][0]], axis=1)
                    if compact:
                        for h in range(2):
                            per_head = jnp.where(lane == 2 * i + h, outs[(bi, i, h)][1][:, :1], per_head)
                    else:
                        l_ref[_rows(n), i * LANE:(i + 1) * LANE] = jnp.concatenate(
                            [outs[(bi, i, 0)][1], outs[(bi, i, 1)][1]], axis=1)
                if compact:
                    l_ref[_rows(n), :] = per_head
            return carry

        _for_blocks(blocks, nblk, 2 if hps == 1 else 1, 0)

    in_specs = [pl.BlockSpec((None, None, sub, 3 * LANE * hps), lambda hp, b, r: (b, r, 0, hp)),
                pl.BlockSpec((2 * hps, BAND, 2 * BAND), lambda hp, b, r: (hp, 0, 0))]
    if compact:
        return _pcall(
            body, name=name, grid=(1, nb, d), in_specs=in_specs,
            out_specs=[pl.BlockSpec((None, None, sub, ATT_OUT), lambda hp, b, r: (b, r, 0, 0)),
                       pl.BlockSpec((None, None, sub, LANE), lambda hp, b, r: (b, r, 0, 0))],
            out_shape=[jax.ShapeDtypeStruct((nb, d, sub, ATT_OUT), F32), jax.ShapeDtypeStruct((nb, d, sub, LANE), F32)],
            compiler_params=_params("parallel", "parallel", "parallel"),
        )(qkv4, bias)
    ospec = pl.BlockSpec((None, sub, hps * LANE), lambda hp, b, r: (b, 0, r * (npair // hps) + hp))
    return _pcall(
        body, name=name, grid=(npair // hps, nb, d), in_specs=in_specs, out_specs=[ospec, ospec],
        out_shape=[jax.ShapeDtypeStruct((nb, sub, d * ATT_OUT), F32)] * 2,
        compiler_params=_params("parallel", "parallel", "parallel"),
    )(qkv4, bias)


STAT_LSE_LANE = 16


def _attn_bwd(qkv4, bias, cotangent, d, name):
    nb, _, sub, _ = qkv4.shape
    nblk = sub // BAND
    scale = HEAD_DIM ** -0.5
    npair = ATT_OUT // LANE
    hps = _pairs_per_step(d)
    compact = d > 1

    def body(qkv_ref, bias_ref, *rest):
        do_ref, dqkv_ref, db_ref = rest[0], rest[-2], rest[-1]
        b, r = pl.program_id(1), pl.program_id(2)

        @pl.when((b == 0) & (r == 0))
        def _():
            db_ref[...] = jnp.zeros_like(db_ref)

        def blocks(ns, carry, with_prev):
            sides = (0, 1) if with_prev else (0,)
            chains = [(bi, i, h, sd) for bi in range(len(ns)) for i in range(hps) for h in range(2) for sd in sides]
            key_rows = lambda bi, sd: _rows(ns[bi] - sd)
            qs = {}
            for bi in range(len(ns)):
                for i in range(hps):
                    for h in range(2):
                        hl = slice(i * LANE + h * HEAD_DIM, i * LANE + (h + 1) * HEAD_DIM)
                        do = do_ref[_rows(ns[bi]), hl]
                        if compact:
                            st_ref, head = rest[1], 2 * i + h
                            ebar = st_ref[_rows(ns[bi]), head:head + 1]
                            lcol = st_ref[_rows(ns[bi]), STAT_LSE_LANE + head:STAT_LSE_LANE + head + 1]
                        else:
                            ebar = jnp.sum(do * rest[1][_rows(ns[bi]), hl], -1, keepdims=True)
                            lcol = rest[2][_rows(ns[bi]), i * LANE + h * HEAD_DIM:i * LANE + h * HEAD_DIM + 1]
                        q_scaled = qkv_ref[_rows(ns[bi]), _head_cols(i, h, 0)] * scale
                        qs[(bi, i, h)] = (q_scaled, do.astype(BF16), ebar, lcol)
            raw = []
            for bi, i, h, sd in chains:
                q, do16, _, _ = qs[(bi, i, h)]
                k = qkv_ref[key_rows(bi, sd), _head_cols(i, h, 1)]
                v = qkv_ref[key_rows(bi, sd), _head_cols(i, h, 2)]
                bias_blk = bias_ref[2 * i + h, :, :BAND] if sd else bias_ref[2 * i + h, :, BAND:]
                s = lax.dot_general(q, k, NT, preferred_element_type=F32) + bias_blk
                dp = lax.dot_general(do16, v, NT, preferred_element_type=F32)
                raw.append((s, dp))
            soft = []
            for (bi, i, h, sd), (s, dp) in zip(chains, raw):
                _, _, ebar, lcol = qs[(bi, i, h)]
                p = jnp.exp(s - lcol)
                ds = p * (dp - ebar)
                if sd:
                    db_ref[2 * i + h, :, :BAND] += ds
                else:
                    db_ref[2 * i + h, :, BAND:] += ds
                soft.append((p.astype(BF16), ds.astype(BF16)))
            grads = {}
            for (bi, i, h, sd), (p16, ds16) in zip(chains, soft):
                q, do16, _, _ = qs[(bi, i, h)]
                k = qkv_ref[key_rows(bi, sd), _head_cols(i, h, 1)]
                grads[(bi, i, h, sd)] = (
                    jnp.dot(ds16, k, preferred_element_type=F32),
                    lax.dot_general(ds16, q, TN, preferred_element_type=F32),
                    lax.dot_general(p16, do16, TN, preferred_element_type=F32))
            both = lambda bi, i, sd, which: jnp.concatenate(
                [grads[(bi, i, 0, sd)][which], grads[(bi, i, 1, sd)][which]], axis=1)
            carry = list(carry) if carry is not None else None
            for bi, n in enumerate(ns):
                for i in range(hps):
                    base = 3 * LANE * i
                    dq = both(bi, i, 0, 0)
                    if with_prev:
                        dq = dq + both(bi, i, 1, 0)
                        dqkv_ref[_rows(n - 1), base + LANE:base + 2 * LANE] = (
                            carry[2 * i] + both(bi, i, 1, 1)).astype(BF16)
                        dqkv_ref[_rows(n - 1), base + 2 * LANE:base + 3 * LANE] = (
                            carry[2 * i + 1] + both(bi, i, 1, 2)).astype(BF16)
                    dqkv_ref[_rows(n), base:base + LANE] = (dq * scale).astype(BF16)
                carry = [t for i in range(hps) for t in (both(bi, i, 0, 1), both(bi, i, 0, 2))]
            return tuple(carry)

        carry = _for_blocks(blocks, nblk, 2 if hps == 1 else 1, None)
        for i in range(hps):
            base = 3 * LANE * i
            dqkv_ref[_rows(nblk - 1), base + LANE:base + 2 * LANE] = carry[2 * i].astype(BF16)
            dqkv_ref[_rows(nblk - 1), base + 2 * LANE:base + 3 * LANE] = carry[2 * i + 1].astype(BF16)

    qspec = pl.BlockSpec((None, None, sub, 3 * LANE * hps), lambda hp, b, r: (b, r, 0, hp))
    bspec = pl.BlockSpec((2 * hps, BAND, 2 * BAND), lambda hp, b, r: (hp, 0, 0))
    if compact:
        cspecs = [pl.BlockSpec((None, None, sub, ATT_OUT), lambda hp, b, r: (b, r, 0, 0)),
                  pl.BlockSpec((None, None, sub, LANE), lambda hp, b, r: (b, r, 0, 0))]
    else:
        cspecs = [pl.BlockSpec((None, sub, hps * LANE), lambda hp, b, r: (b, 0, r * (npair // hps) + hp))] * 3
    return _pcall(
        body, name=name, grid=(npair // hps, nb, d),
        in_specs=[qspec, bspec] + cspecs, out_specs=[qspec, bspec],
        out_shape=[jax.ShapeDtypeStruct(qkv4.shape, BF16),
                   jax.ShapeDtypeStruct((GROUP_HEADS, BAND, 2 * BAND), F32)],
        compiler_params=_params("parallel", "arbitrary", "arbitrary"),
    )(qkv4, bias, *cotangent)


def _head_lanes(first_lane, one_channel):
    c = lax.broadcasted_iota(jnp.int32, (ATT_OUT, LANE), 0)
    lane = lax.broadcasted_iota(jnp.int32, (ATT_OUT, LANE), 1)
    hit = lane == first_lane + c // HEAD_DIM
    if one_channel:
        hit = hit & (c % HEAD_DIM == 0)
    return hit.astype(BF16)


def _exact_dot(v, m01, dims=None):
    parts = _split3(v)
    if dims is None:
        dot = lambda t: jnp.dot(t, m01, preferred_element_type=F32)
    else:
        dot = lambda t: lax.dot_general(t, m01, dims, preferred_element_type=F32)
    return (dot(parts[0]) + dot(parts[1])) + dot(parts[2])


def _combine_fwd(o0, l0, o1, l1, o2, l2, gatt):
    nb, seq, _ = gatt.shape
    tm = 512

    def body(o0_ref, l0_ref, o1_ref, l1_ref, o2_ref, l2_ref, g_ref, oa_ref, oatt_ref, lse_ref):
        spread = _head_lanes(0, False)
        l0v = l0_ref[...]
        l1v = _exact_dot(l1_ref[...], spread, NT)
        l2v = _exact_dot(l2_ref[...], spread, NT)
        m = jnp.maximum(jnp.maximum(l0v, l1v), l2v)
        tot = m + jnp.log(jnp.exp(l0v - m) + jnp.exp(l1v - m) + jnp.exp(l2v - m))
        o = (jnp.exp(l0v - tot) * o0_ref[...] + jnp.exp(l1v - tot) * o1_ref[...]
             + jnp.exp(l2v - tot) * o2_ref[...])
        g = g_ref[...]
        oa_ref[...] = (o * (g * _sigmoid(g))).astype(BF16)
        oatt_ref[...] = o
        lse_ref[...] = tot

    spec = pl.BlockSpec((None, tm, ATT_OUT), lambda b, i: (b, i, 0))
    lspec = pl.BlockSpec((None, tm, LANE), lambda b, i: (b, i, 0))
    return _pcall(
        body, name="attn_combine", grid=(nb, seq // tm),
        in_specs=[spec, spec, spec, lspec, spec, lspec, spec], out_specs=[spec] * 3,
        out_shape=[jax.ShapeDtypeStruct((nb, seq, ATT_OUT), BF16), jax.ShapeDtypeStruct((nb, seq, ATT_OUT), F32),
                   jax.ShapeDtypeStruct((nb, seq, ATT_OUT), F32)],
        compiler_params=_params("parallel", "parallel"),
    )(o0, l0, o1, l1, o2, l2, gatt)


def _combine_bwd(doa, gatt, o_att, lse):
    nb, seq, _ = gatt.shape
    tm = 512

    def body(doa_ref, g_ref, o_ref, l_ref, do_ref, do16_ref, st_ref, dg_ref):
        g = g_ref[...]
        sg = _sigmoid(g)
        do = doa_ref[...] * (g * sg)
        do_ref[...] = do
        do16_ref[...] = do.astype(BF16)
        st_ref[...] = (_exact_dot(do * o_ref[...], _head_lanes(0, False))
                       + _exact_dot(l_ref[...], _head_lanes(STAT_LSE_LANE, True)))
        dg_ref[...] = (doa_ref[...] * o_ref[...] * (sg * (1.0 + g * (1.0 - sg)))).astype(BF16)

    spec = pl.BlockSpec((None, tm, ATT_OUT), lambda b, i: (b, i, 0))
    lspec = pl.BlockSpec((None, tm, LANE), lambda b, i: (b, i, 0))
    return _pcall(
        body, name="attn_combine_bwd", grid=(nb, seq // tm), in_specs=[spec] * 4,
        out_specs=[spec, spec, lspec, spec],
        out_shape=[jax.ShapeDtypeStruct((nb, seq, ATT_OUT), F32), jax.ShapeDtypeStruct((nb, seq, ATT_OUT), BF16),
                   jax.ShapeDtypeStruct((nb, seq, LANE), F32), jax.ShapeDtypeStruct((nb, seq, ATT_OUT), BF16)],
        compiler_params=_params("parallel", "parallel"),
    )(doa, gatt, o_att, lse)


CONV_TM = 512
CONV_TC = 512


def _shift_down(cur, halo, k):
    rolled = pltpu.roll(cur, k, 0)
    hro = pltpu.roll(halo, k, 0)
    row = lax.broadcasted_iota(jnp.int32, hro.shape, 0)
    return jnp.concatenate([jnp.where(row < k, hro, rolled[:8]), rolled[8:]], axis=0)


def _shift_up(cur, halo, k):
    n = cur.shape[0]
    rolled = pltpu.roll(cur, n - k, 0)
    hro = pltpu.roll(halo, 8 - k, 0)
    row = lax.broadcasted_iota(jnp.int32, hro.shape, 0)
    return jnp.concatenate([rolled[:n - 8], jnp.where(row >= 8 - k, hro, rolled[n - 8:])], axis=0)


def _conv_pre(cur, halo, w_ref, b_ref):
    acc = cur * w_ref[3:4, :] + b_ref[...]
    for k in range(1, 4):
        acc = acc + _shift_down(cur, halo, k) * w_ref[3 - k:4 - k, :]
    return acc


def _conv_specs(seq):
    nblk = seq // CONV_TM
    cur = pl.BlockSpec((None, CONV_TM, CONV_TC), lambda cb, b, i: (b, i, cb))
    prev = pl.BlockSpec((None, 8, CONV_TC), lambda cb, b, i: (b, jnp.maximum(i * (CONV_TM // 8) - 1, 0), cb))
    nxt = pl.BlockSpec((None, 8, CONV_TC),
                       lambda cb, b, i: (b, jnp.minimum((i + 1) * (CONV_TM // 8), seq // 8 - 1), cb))
    wspec = pl.BlockSpec((4, CONV_TC), lambda cb, b, i: (0, cb))
    bspec = pl.BlockSpec((1, CONV_TC), lambda cb, b, i: (0, cb))
    return nblk, cur, prev, nxt, wspec, bspec


def _conv_fwd(xin, w4, bias, name):
    nb, seq, ch = xin.shape
    _, cur, prev, _, wspec, bspec = _conv_specs(seq)

    def body(x_ref, h_ref, w_ref, b_ref, o_ref):
        halo = jnp.where(pl.program_id(2) > 0, h_ref[...], 0.0)
        pre = _conv_pre(x_ref[...], halo, w_ref, b_ref)
        o_ref[...] = pre * _sigmoid(pre)

    return _pcall(
        body, name=name, grid=(ch // CONV_TC, nb, seq // CONV_TM),
        in_specs=[cur, prev, wspec, bspec], out_specs=cur,
        out_shape=jax.ShapeDtypeStruct(xin.shape, F32),
        compiler_params=_params("parallel", "parallel", "parallel"),
    )(xin, xin, w4, bias)


def _conv_bwd_pre(dact, xin, w4, bias, name):
    nb, seq, ch = xin.shape
    _, cur, prev, _, wspec, bspec = _conv_specs(seq)

    def body(da_ref, x_ref, h_ref, w_ref, b_ref, dp_ref, s_ref):
        b, i = pl.program_id(1), pl.program_id(2)

        @pl.when((b == 0) & (i == 0))
        def _():
            s_ref[...] = jnp.zeros_like(s_ref)

        halo = jnp.where(i > 0, h_ref[...], 0.0)
        x = x_ref[...]
        pre = _conv_pre(x, halo, w_ref, b_ref)
        sg = _sigmoid(pre)
        dpre = da_ref[...] * (sg * (1.0 + pre * (1.0 - sg)))
        dp_ref[...] = dpre
        s_ref[3:4, :] += jnp.sum(dpre * x, 0, keepdims=True)
        for k in range(1, 4):
            s_ref[3 - k:4 - k, :] += jnp.sum(dpre * _shift_down(x, halo, k), 0, keepdims=True)
        s_ref[4:5, :] += jnp.sum(dpre, 0, keepdims=True)

    return _pcall(
        body, name=name, grid=(ch // CONV_TC, nb, seq // CONV_TM),
        in_specs=[cur, cur, prev, wspec, bspec],
        out_specs=[cur, pl.BlockSpec((8, CONV_TC), lambda cb, b, i: (0, cb))],
        out_shape=[jax.ShapeDtypeStruct(xin.shape, F32), jax.ShapeDtypeStruct((8, ch), F32)],
        compiler_params=_params("parallel", "arbitrary", "arbitrary"),
    )(dact, xin, xin, w4, bias)


def _conv_bwd_x(dpre, w4, name):
    nb, seq, ch = dpre.shape
    nblk, cur, _, nxt, wspec, _ = _conv_specs(seq)

    def body(d_ref, n_ref, w_ref, o_ref):
        halo = jnp.where(pl.program_id(2) < nblk - 1, n_ref[...], 0.0)
        cur_v = d_ref[...]
        acc = cur_v * w_ref[3:4, :]
        for j in range(1, 4):
            acc = acc + _shift_up(cur_v, halo, j) * w_ref[3 - j:4 - j, :]
        o_ref[...] = acc.astype(BF16)

    return _pcall(
        body, name=name, grid=(ch // CONV_TC, nb, seq // CONV_TM),
        in_specs=[cur, nxt, wspec], out_specs=cur,
        out_shape=jax.ShapeDtypeStruct(dpre.shape, BF16),
        compiler_params=_params("parallel", "parallel", "parallel"),
    )(dpre, dpre, w4)


def _softplus_sig(dt_raw, dt_bias_row):
    nb, seq, _ = dt_raw.shape
    tm = 512

    def body(r_ref, b_ref, sp_ref, sg_ref):
        v = r_ref[...] + b_ref[...]
        sp_ref[...] = jnp.maximum(v, 0.0) + jnp.log1p(jnp.exp(-jnp.abs(v)))
        sg_ref[...] = _sigmoid(v)

    spec = pl.BlockSpec((None, tm, LANE), lambda b, i: (b, i, 0))
    return _pcall(
        body, name="dt_softplus", grid=(nb, seq // tm),
        in_specs=[spec, pl.BlockSpec((1, LANE), lambda b, i: (0, 0))], out_specs=[spec, spec],
        out_shape=[jax.ShapeDtypeStruct(dt_raw.shape, F32)] * 2,
        compiler_params=_params("parallel", "parallel"),
    )(dt_raw, dt_bias_row)


def _group_lanes(t):
    pads = [(0, 0)] * (t.ndim - 1) + [(0, LANE - GROUP_SSM_HEADS)]
    return jnp.stack([jnp.pad(t[..., GROUP_SSM_HEADS * g:GROUP_SSM_HEADS * (g + 1)], pads) for g in range(SSM_GROUPS)])


def _ungroup_lanes(t):
    return jnp.concatenate([t[g][..., :GROUP_SSM_HEADS] for g in range(SSM_GROUPS)], axis=-1)


def _decays(dt, al_ref):
    row = lax.broadcasted_iota(jnp.int32, (CHUNK, CHUNK), 0)
    col = lax.broadcasted_iota(jnp.int32, (CHUNK, CHUNK), 1)
    tril = (row >= col).astype(BF16)
    triu = (row <= col).astype(BF16)
    arow = -jnp.exp(al_ref[...])
    hi, mid, lo = _split3(dt * arow)
    down = lambda t: jnp.dot(tril, t, preferred_element_type=F32)
    across = lambda t: lax.dot_general(t, triu, TN, preferred_element_type=F32)
    acs = (down(hi) + down(mid)) + down(lo)
    acs_t = (across(hi) + across(mid)) + across(lo)
    return arow, acs, acs_t, row >= col, triu


def _ssd_specs(nb, seq):
    nc = seq // CHUNK
    hw = GROUP_SSM_HEADS * HEAD_DIM

    def mk(rev):
        cidx = (lambda c: nc - 1 - c) if rev else (lambda c: c)
        wide = pl.BlockSpec((None, CHUNK, hw), lambda g, b, c: (b, cidx(c), g))
        xbc = pl.BlockSpec((None, CHUNK, XBC_GROUP), lambda g, b, c: (b, cidx(c), g))
        lanes = pl.BlockSpec((None, None, CHUNK, LANE), lambda g, b, c: (g, b, cidx(c), 0))
        prev = pl.BlockSpec((None, None, None, D_STATE, hw), lambda g, b, c: (b, cidx(c), g, 0, 0))
        return wide, xbc, lanes, prev

    grow = pl.BlockSpec((None, 1, LANE), lambda g, b, c: (g, 0, 0))
    nwspec = pl.BlockSpec((1, hw), lambda g, b, c: (0, g))
    return nc, hw, mk, grow, nwspec


def _head_expand():
    hw = GROUP_SSM_HEADS * HEAD_DIM
    r = lax.broadcasted_iota(jnp.int32, (LANE, hw), 0)
    c = lax.broadcasted_iota(jnp.int32, (LANE, hw), 1)
    return ((c // HEAD_DIM) == r).astype(BF16)


def _split3(v):
    hi = v.astype(BF16)
    rest = v - hi.astype(F32)
    mid = rest.astype(BF16)
    return hi, mid, (rest - mid.astype(F32)).astype(BF16)


def _to_channels(v, e):
    hi, mid, lo = _split3(v)
    dot = lambda t: jnp.dot(t, e, preferred_element_type=F32)
    return (dot(hi) + dot(mid)) + dot(lo)


def _to_heads(w, e):
    hi, mid, lo = _split3(w)
    dot = lambda t: lax.dot_general(t, e, (((1,), (1,)), ((), ())), preferred_element_type=F32)
    return (dot(hi) + dot(mid)) + dot(lo)


def _row8(v):
    return jnp.broadcast_to(v, (8, v.shape[1]))


def _ssd_chunk_setup(dt, al_ref, ds_ref):
    arow, acs, acs_t, causal, triu = _decays(dt, al_ref)
    e = _head_expand()
    dtx = _to_channels(dt, e)
    acsx = _to_channels(acs, e)
    lastx = acsx[CHUNK - 1:CHUNK, :]
    dskx = _to_channels(_row8(ds_ref[...]), e)[0:1, :]
    return arow, acs, acs_t, causal, triu, e, dtx, acsx, lastx, dskx


def _ssd_fwd(xbc, dtg, z, alog_g, dskip_g, normw):
    nb, seq, _ = xbc.shape
    nc, hw, mk, grow, nwspec = _ssd_specs(nb, seq)
    wide, xbc_spec, lanes, prev = mk(False)
    tn = (((0,), (0,)), ((), ()))

    def body(xbc_ref, dt_ref, z_ref, al_ref, ds_ref, nw_ref, ys_ref, y_ref, sp_ref, st_ref):
        @pl.when(pl.program_id(2) == 0)
        def _():
            st_ref[...] = jnp.zeros_like(st_ref)

        dt = dt_ref[...]
        _, acs, acs_t, causal, _, _, dtx, acsx, lastx, dskx = _ssd_chunk_setup(dt, al_ref, ds_ref)
        bmat = xbc_ref[:, GROUP_CH:GROUP_CH + D_STATE].astype(BF16)
        cmat = xbc_ref[:, GROUP_CH + D_STATE:].astype(BF16)
        cb = lax.dot_general(cmat, bmat, (((1,), (1,)), ((), ())), preferred_element_type=F32)
        x = xbc_ref[:, :GROUP_CH]
        xdt = x * dtx
        xdt16 = xdt.astype(BF16)
        first_head = lax.broadcasted_iota(jnp.int32, (CHUNK, LANE), 1) < HEAD_DIM
        pairs = []
        for hp in range(GROUP_SSM_HEADS // 2):
            xp = xdt16[:, hp * LANE:(hp + 1) * LANE]
            two = []
            for j in (2 * hp, 2 * hp + 1):
                lmat = jnp.exp(jnp.where(causal, acs[:, j:j + 1] - acs_t[j:j + 1, :], -jnp.inf))
                two.append(jnp.dot((cb * lmat).astype(BF16), xp, preferred_element_type=F32))
            pairs.append(jnp.where(first_head, two[0], two[1]))
        yd = jnp.concatenate(pairs, axis=1)
        s_prev = st_ref[...]
        s16 = s_prev.astype(BF16)
        sp_ref[...] = s16
        yo = jnp.dot(cmat, s16, preferred_element_type=F32) * jnp.exp(acsx)
        sts = lax.dot_general(bmat, (xdt * jnp.exp(lastx - acsx)).astype(BF16), tn, preferred_element_type=F32)
        st_ref[...] = s_prev * jnp.exp(lastx) + sts
        y = yd + yo + dskx * x
        zz = z_ref[...]
        u = y * (zz * _sigmoid(zz))
        rn = lax.rsqrt(jnp.mean(u * u, -1, keepdims=True) + RMS_EPS)
        ys_ref[...] = (u * rn * nw_ref[...]).astype(BF16)
        y_ref[...] = y

    return _pcall(
        body, name="ssd_fwd", grid=(SSM_GROUPS, nb, nc),
        in_specs=[xbc_spec, lanes, wide, grow, grow, nwspec],
        out_specs=[wide, wide, prev],
        out_shape=[jax.ShapeDtypeStruct((nb, seq, D_INNER), BF16), jax.ShapeDtypeStruct((nb, seq, D_INNER), F32),
                   jax.ShapeDtypeStruct((nb, nc, SSM_GROUPS, D_STATE, hw), BF16)],
        scratch_shapes=[pltpu.VMEM((D_STATE, hw), F32)],
        compiler_params=_params("parallel", "parallel", "arbitrary"),
    )(xbc, dtg, z, alog_g, dskip_g, normw)


def _ssd_bwd(xbc, dtg, sgg, z, y, dys, sprev, alog_g, dskip_g, normw):
    nb, seq, _ = xbc.shape
    nc, hw, mk, grow, nwspec = _ssd_specs(nb, seq)
    wide, xbc_spec, lanes, prev = mk(True)
    nt = (((1,), (1,)), ((), ()))
    tn = (((0,), (0,)), ((), ()))

    def body(xbc_ref, dt_ref, sg_ref, z_ref, y_ref, dys_ref, sp_ref, al_ref, ds_ref, nw_ref,
             dxbc_ref, ddt_ref, dz_ref, small_ref, dnw_ref, g_ref):
        b, c = pl.program_id(1), pl.program_id(2)

        @pl.when((b == 0) & (c == 0))
        def _():
            small_ref[...] = jnp.zeros_like(small_ref)
            dnw_ref[...] = jnp.zeros_like(dnw_ref)

        @pl.when(c == 0)
        def _():
            g_ref[...] = jnp.zeros_like(g_ref)

        yv, zz, dys_v, nw = y_ref[...], z_ref[...], dys_ref[...], nw_ref[...]
        sz = _sigmoid(zz)
        silu = zz * sz
        u = yv * silu
        rn = lax.rsqrt(jnp.mean(u * u, -1, keepdims=True) + RMS_EPS)
        gn = dys_v * nw
        du = rn * gn - u * (rn * rn * rn) * jnp.mean(u * gn, -1, keepdims=True)
        dnw_ref[...] += jnp.sum(dys_v * u * rn, 0, keepdims=True)
        dy = du * silu
        dz_ref[...] = du * yv * (sz * (1.0 + zz * (1.0 - sz)))

        dt = dt_ref[...]
        arow, acs, acs_t, causal, triu, e, dtx, acsx, lastx, dskx = _ssd_chunk_setup(dt, al_ref, ds_ref)
        dfsx = jnp.exp(acsx)
        dtex = jnp.exp(lastx - acsx)
        bmat = xbc_ref[:, GROUP_CH:GROUP_CH + D_STATE].astype(BF16)
        cmat = xbc_ref[:, GROUP_CH + D_STATE:].astype(BF16)
        cb = lax.dot_general(cmat, bmat, nt, preferred_element_type=F32)
        x = xbc_ref[:, :GROUP_CH]
        xdt = x * dtx
        xdt16 = xdt.astype(BF16)
        xdte = xdt * dtex
        dy16 = dy.astype(BF16)
        dyd = dy * dfsx
        dyd16 = dyd.astype(BF16)
        s16 = sp_ref[...]
        g = g_ref[...]
        g16 = g.astype(BF16)
        cs = jnp.dot(cmat, s16, preferred_element_type=F32)
        dc_off = lax.dot_general(dyd16, s16, nt, preferred_element_type=F32)
        g_here = lax.dot_general(cmat, dyd16, tn, preferred_element_type=F32)
        bg = jnp.dot(bmat, g16, preferred_element_type=F32)
        db_st = lax.dot_general(xdte.astype(BF16), g16, nt, preferred_element_type=F32)
        ddte_w = bg * xdte
        dcd = _to_heads(_row8(jnp.sum(g * s16.astype(F32), 0, keepdims=True)), e)[0:1, :]
        lane = lax.broadcasted_iota(jnp.int32, (CHUNK, LANE), 1)
        first_head = lane < HEAD_DIM
        sub = lax.broadcasted_iota(jnp.int32, (CHUNK, LANE), 0)
        dacs = jnp.zeros((CHUNK, LANE), F32)
        colsums = jnp.zeros((CHUNK, LANE), F32)
        dcb = jnp.zeros((CHUNK, CHUNK), F32)
        pairs = []
        for hp in range(GROUP_SSM_HEADS // 2):
            xp = xdt16[:, hp * LANE:(hp + 1) * LANE]
            dyp = dy16[:, hp * LANE:(hp + 1) * LANE]
            two = []
            for idx, j in enumerate((2 * hp, 2 * hp + 1)):
                lmat = jnp.exp(jnp.where(causal, acs[:, j:j + 1] - acs_t[j:j + 1, :], -jnp.inf))
                mf = cb * lmat
                dy_h = jnp.where(first_head if idx == 0 else jnp.logical_not(first_head), dyp, jnp.zeros_like(dyp))
                dm = lax.dot_general(dy_h, xp, nt, preferred_element_type=F32)
                two.append(lax.dot_general(mf.astype(BF16), dyp, tn, preferred_element_type=F32))
                wmat = dm * mf
                dcb = dcb + dm * lmat
                dacs = jnp.where(lane == j, jnp.sum(wmat, -1, keepdims=True), dacs)
                colsums = jnp.where(sub == j, jnp.sum(wmat, 0, keepdims=True), colsums)
            pairs.append(jnp.where(first_head, two[0], two[1]))
        dxdt = bg * dtex + jnp.concatenate(pairs, axis=1)
        dacs = dacs - colsums.T + _to_heads(dyd * cs - ddte_w, e)
        cd_row = jnp.exp(acs[CHUNK - 1:CHUNK, :])
        tail = _to_heads(_row8(jnp.sum(ddte_w, 0, keepdims=True)), e)[0:1, :] + dcd * cd_row
        dacs = dacs + jnp.where(sub == CHUNK - 1, tail, 0.0)
        d_hi, d_mid, d_lo = _split3(dacs)
        up = lambda t: jnp.dot(triu, t, preferred_element_type=F32)
        da = (up(d_hi) + up(d_mid)) + up(d_lo)
        ddt_raw = (da * arow + _to_heads(dxdt * x, e)) * sg_ref[...]
        ddt_ref[...] = ddt_raw
        small_ref[0:1, :] += jnp.sum(da * dt, 0, keepdims=True) * arow
        small_ref[1:2, :] += _to_heads(_row8(jnp.sum(dy * x, 0, keepdims=True)), e)[0:1, :]
        small_ref[2:3, :] += jnp.sum(ddt_raw, 0, keepdims=True)
        dcb16 = dcb.astype(BF16)
        dxbc_ref[:, GROUP_CH + D_STATE:] = dc_off + jnp.dot(dcb16, bmat, preferred_element_type=F32)
        dxbc_ref[:, GROUP_CH:GROUP_CH + D_STATE] = db_st + lax.dot_general(dcb16, cmat, tn,
                                                                            preferred_element_type=F32)
        dxbc_ref[:, :GROUP_CH] = dxdt * dtx + dskx * dy
        g_ref[...] = g * jnp.exp(lastx) + g_here

    return _pcall(
        body, name="ssd_bwd", grid=(SSM_GROUPS, nb, nc),
        in_specs=[xbc_spec, lanes, lanes, wide, wide, wide, prev, grow, grow, nwspec],
        out_specs=[xbc_spec, lanes, wide,
                   pl.BlockSpec((None, 8, LANE), lambda g, b, c: (g, 0, 0)), nwspec],
        out_shape=[jax.ShapeDtypeStruct((nb, seq, CONV_DIM), F32),
                   jax.ShapeDtypeStruct((SSM_GROUPS, nb, seq, LANE), F32),
                   jax.ShapeDtypeStruct((nb, seq, D_INNER), F32),
                   jax.ShapeDtypeStruct((SSM_GROUPS, 8, LANE), F32),
                   jax.ShapeDtypeStruct((1, D_INNER), F32)],
        scratch_shapes=[pltpu.VMEM((D_STATE, hw), F32)],
        compiler_params=_params("parallel", "arbitrary", "arbitrary"),
    )(xbc, dtg, sgg, z, y, dys, sprev, alog_g, dskip_g, normw)


EW_TM = 256


def _merge_fwd(y_a, y_b, gm, bgate):
    nb, seq, _ = y_a.shape

    def body(a_ref, b_ref, ga_ref, gb_ref, bg_ref, o_ref):
        sa = _sigmoid(ga_ref[...] + bg_ref[0:1, :])
        sb = _sigmoid(gb_ref[...] + bg_ref[1:2, :])
        o_ref[...] = (sa * a_ref[...] + sb * b_ref[...]).astype(BF16)

    spec = pl.BlockSpec((None, EW_TM, D_MODEL), lambda b, i: (b, i, 0))
    spec1 = pl.BlockSpec((None, EW_TM, D_MODEL), lambda b, i: (b, i, 1))
    return _pcall(
        body, name="merge_fwd", grid=(nb, seq // EW_TM),
        in_specs=[spec, spec, spec, spec1, pl.BlockSpec((8, D_MODEL), lambda b, i: (0, 0))], out_specs=spec,
        out_shape=jax.ShapeDtypeStruct((nb, seq, D_MODEL), BF16),
        compiler_params=_params("parallel", "parallel"),
    )(y_a, y_b, gm, gm, bgate)


def _merge_bwd(dmerged, y_a, y_b, gm, bgate):
    nb, seq, _ = y_a.shape

    def body(dm_ref, a_ref, b_ref, ga_ref, gb_ref, bg_ref, dya_ref, dyb_ref, dg_ref, s_ref):
        @pl.when((pl.program_id(0) == 0) & (pl.program_id(1) == 0))
        def _():
            s_ref[...] = jnp.zeros_like(s_ref)

        dm = dm_ref[...]
        sa = _sigmoid(ga_ref[...] + bg_ref[0:1, :])
        sb = _sigmoid(gb_ref[...] + bg_ref[1:2, :])
        dya_ref[...] = (dm * sa).astype(BF16)
        dyb_ref[...] = (dm * sb).astype(BF16)
        dga = dm * a_ref[...] * (sa * (1.0 - sa))
        dgb = dm * b_ref[...] * (sb * (1.0 - sb))
        dg_ref[:, :D_MODEL] = dga.astype(BF16)
        dg_ref[:, D_MODEL:] = dgb.astype(BF16)
        s_ref[0:1, :] += jnp.sum(dga, 0, keepdims=True)
        s_ref[1:2, :] += jnp.sum(dgb, 0, keepdims=True)

    spec = pl.BlockSpec((None, EW_TM, D_MODEL), lambda b, i: (b, i, 0))
    spec1 = pl.BlockSpec((None, EW_TM, D_MODEL), lambda b, i: (b, i, 1))
    small = pl.BlockSpec((8, D_MODEL), lambda b, i: (0, 0))
    return _pcall(
        body, name="merge_bwd", grid=(nb, seq // EW_TM),
        in_specs=[spec, spec, spec, spec, spec1, small],
        out_specs=[spec, spec, pl.BlockSpec((None, EW_TM, 2 * D_MODEL), lambda b, i: (b, i, 0)), small],
        out_shape=[jax.ShapeDtypeStruct((nb, seq, D_MODEL), BF16), jax.ShapeDtypeStruct((nb, seq, D_MODEL), BF16),
                   jax.ShapeDtypeStruct((nb, seq, 2 * D_MODEL), BF16), jax.ShapeDtypeStruct((8, D_MODEL), F32)],
        compiler_params=_params("arbitrary", "arbitrary"),
    )(dmerged, y_a, y_b, gm, gm, bgate)


def _ln_loss(x, mix, gp, pw, target, bgate, ln_g, ln_b):
    nb, seq, _ = x.shape

    def body(x_ref, mix_ref, gp_ref, pw_ref, t_ref, bg_ref, g_ref, b_ref, dx_ref, dp_ref, dpw_ref, dgp_ref, s_ref):
        @pl.when((pl.program_id(0) == 0) & (pl.program_id(1) == 0))
        def _():
            s_ref[...] = jnp.zeros_like(s_ref)

        sp = _sigmoid(gp_ref[...] + bg_ref[2:3, :])
        pw = pw_ref[...]
        pre = ALPHA * x_ref[...] + mix_ref[...] + sp * pw
        mu = jnp.mean(pre, -1, keepdims=True)
        cen = pre - mu
        rstd = lax.rsqrt(jnp.mean(cen * cen, -1, keepdims=True) + LN_EPS)
        xhat = cen * rstd
        err = xhat * g_ref[...] + b_ref[...] - t_ref[...]
        dy = err * (1.0 / D_MODEL)
        dxh = dy * g_ref[...]
        dpre = rstd * (dxh - jnp.mean(dxh, -1, keepdims=True) - xhat * jnp.mean(dxh * xhat, -1, keepdims=True))
        dx_ref[...] = ALPHA * dpre
        dp_ref[...] = dpre.astype(BF16)
        dpw_ref[...] = (dpre * sp).astype(BF16)
        dgp = dpre * pw * (sp * (1.0 - sp))
        dgp_ref[...] = dgp.astype(BF16)
        s_ref[0:1, :] += jnp.sum(dy * xhat, 0, keepdims=True)
        s_ref[1:2, :] += jnp.sum(dy, 0, keepdims=True)
        s_ref[2:3, :] += jnp.sum(dgp, 0, keepdims=True)
        s_ref[3:4, :] += jnp.sum(err * err, 0, keepdims=True)

    spec = pl.BlockSpec((None, EW_TM, D_MODEL), lambda b, i: (b, i, 0))
    small = pl.BlockSpec((8, D_MODEL), lambda b, i: (0, 0))
    row = pl.BlockSpec((1, D_MODEL), lambda b, i: (0, 0))
    return _pcall(
        body, name="ln_loss", grid=(nb, seq // EW_TM),
        in_specs=[spec] * 5 + [small, row, row], out_specs=[spec] * 4 + [small],
        out_shape=[jax.ShapeDtypeStruct((nb, seq, D_MODEL), F32)] + [jax.ShapeDtypeStruct((nb, seq, D_MODEL), BF16)] * 3
        + [jax.ShapeDtypeStruct((8, D_MODEL), F32)],
        compiler_params=_params("arbitrary", "arbitrary"),
    )(x, mix, gp, pw, target, bgate, ln_g, ln_b)


def _adamw(w, g, m, v, name):
    rows, cols = w.shape
    tr = _row_tile(rows, cols, 8, 5 << 19)
    c1 = 1.0 - ADAM_B1 ** ADAM_STEP
    c2 = 1.0 - ADAM_B2 ** ADAM_STEP

    def body(w_ref, g_ref, m_ref, v_ref, d_ref, nm_ref, nv_ref):
        gv = g_ref[...]
        nm = ADAM_B1 * m_ref[...] + (1.0 - ADAM_B1) * gv
        nv = ADAM_B2 * v_ref[...] + (1.0 - ADAM_B2) * (gv * gv)
        d_ref[...] = -ADAM_LR * ((nm / c1) / (jnp.sqrt(nv / c2) + ADAM_EPS) + ADAM_WD * w_ref[...])
        nm_ref[...] = nm
        nv_ref[...] = nv

    spec = pl.BlockSpec((tr, cols), lambda i: (i, 0))
    return _pcall(
        body, name=name, grid=(rows // tr,), in_specs=[spec] * 4, out_specs=[spec] * 3,
        out_shape=[jax.ShapeDtypeStruct(w.shape, F32)] * 3, compiler_params=_params("parallel"),
    )(w, g, m, v)


def _sum_rows(parts, out_dtype, name):
    rows, cols = parts[0].shape
    tr = rows
    for cand in range(16, rows, 16):
        if rows % cand == 0 and cand * cols * 4 <= (1 << 20):
            tr = cand
    n = len(parts)

    def body(*refs):
        acc = refs[0][...].astype(F32)
        for r in refs[1:n]:
            acc = acc + r[...].astype(F32)
        refs[n][...] = acc.astype(out_dtype)

    spec = pl.BlockSpec((tr, cols), lambda i: (i, 0))
    return _pcall(
        body, name=name, grid=(rows // tr,), in_specs=[spec] * n, out_specs=spec,
        out_shape=jax.ShapeDtypeStruct((rows, cols), out_dtype), compiler_params=_params("parallel"),
    )(*parts)


def _place():
    return lax.axis_index("x"), lax.axis_index("y"), lax.axis_index("c")


def _other_chips(x, y):
    return [(1 - x, y), (x, 1 - y), (1 - x, 1 - y)]


def _remote(src, dst, send_sem, recv_sem, to):
    return pltpu.make_async_remote_copy(src_ref=src, dst_ref=dst, send_sem=send_sem, recv_sem=recv_sem,
                                        device_id=to, device_id_type=MESH)


ANY = pl.BlockSpec(memory_space=pl.ANY)
DMA_CHUNK_BYTES = 512 * 1024


def _row_chunks(rows, row_bytes):
    per = max(16, DMA_CHUNK_BYTES // row_bytes // 16 * 16)
    return [(s, min(per, rows - s)) for s in range(0, rows, per)]


def _row_tile(rows, cols, align, limit=1 << 21):
    best = None
    for cand in range(align, rows + 1, align):
        if rows % cand == 0 and cand * cols * 4 <= limit:
            best = cand
    return best or rows


def _allgather_pieces(pieces):
    n = len(pieces)
    halves = [_row_chunks(p.shape[0] // 2, p.shape[1] * p.dtype.itemsize) for p in pieces]
    wholes = [_row_chunks(p.shape[0], p.shape[1] * p.dtype.itemsize) for p in pieces]
    entries = [(a, q, s, m, j) for a in range(n) for q, (s, m) in enumerate(halves[a]) for j in range(3)]
    slot = {(a, q, j): k for k, (a, q, _, _, j) in enumerate(entries)}
    n_ici = len(entries)
    n_loc = sum(len(w) for w in wholes)

    def body(*refs):
        ins, outs = refs[:n], refs[n:2 * n]
        send_sems, recv_sems, local_sems = refs[2 * n:]
        x, y, c = _place()
        me = 2 * x + y
        sibling = (x, y, 1 - c)
        chips = _other_chips(x, y)
        locals_ = []
        for a in range(n):
            for s, m in wholes[a]:
                loc = pltpu.make_async_copy(ins[a].at[pl.ds(s, m)], outs[a].at[me, pl.ds(s, m)],
                                            local_sems.at[len(locals_)])
                loc.start()
                locals_.append(loc)

        def landed(a, s, m, j, core):
            half = ins[a].shape[0] // 2
            return outs[a].at[2 * chips[j][0] + chips[j][1], pl.ds(core * half + s, m)]

        sent = []
        for k, (a, q, s, m, j) in enumerate(entries):
            if j < 2:
                half = ins[a].shape[0] // 2
                cp = _remote(ins[a].at[pl.ds(c * half + s, m)], outs[a].at[me, pl.ds(c * half + s, m)],
                             send_sems.at[k], recv_sems.at[k], (*chips[j], c))
                cp.start()
                sent.append(cp)
        for k, (a, q, s, m, j) in enumerate(entries):
            if j < 2:
                blk = landed(a, s, m, j, c)
                _remote(blk, blk, send_sems.at[k], recv_sems.at[k], (*chips[j], c)).wait_recv()
                first = q < (len(halves[a]) + 1) // 2
                if (j == 0) == first:
                    on = slot[(a, q, 2)]
                    rl = _remote(blk, blk, send_sems.at[on], recv_sems.at[on], (*chips[1 - j], c))
                    rl.start()
                    sent.append(rl)
        for k, (a, q, s, m, j) in enumerate(entries):
            if j == 2:
                blk = landed(a, s, m, j, c)
                _remote(blk, blk, send_sems.at[k], recv_sems.at[k], (*chips[j], c)).wait_recv()
        for cp in sent:
            cp.wait_send()
        for loc in locals_:
            loc.wait()

    gathered = _pcall(
        body, name="allgather_weights", in_specs=[ANY] * n, out_specs=[ANY] * n,
        out_shape=[jax.ShapeDtypeStruct((4,) + p.shape, p.dtype) for p in pieces],
        scratch_shapes=[pltpu.SemaphoreType.DMA((n_ici,)), pltpu.SemaphoreType.DMA((n_ici,)),
                        pltpu.SemaphoreType.DMA((n_loc,))],
        compiler_params=pltpu.CompilerParams(has_side_effects=True),
    )(*pieces)

    def share(*refs):
        outs = refs[n:2 * n]
        send_sems, recv_sems = refs[2 * n:]
        x, y, c = _place()
        sibling = (x, y, 1 - c)
        chips = _other_chips(x, y)

        def landed(a, s, m, j, core):
            half = outs[a].shape[1] // 2
            return outs[a].at[2 * chips[j][0] + chips[j][1], pl.ds(core * half + s, m)]

        work = []
        for k, (a, q, s, m, j) in enumerate(entries):
            mine = landed(a, s, m, j, c)
            cp = _remote(mine, mine, send_sems.at[k], recv_sems.at[k], sibling)
            cp.start()
            work.append(cp)
        for k, (a, q, s, m, j) in enumerate(entries):
            work[k].wait_send()
            theirs = landed(a, s, m, j, 1 - c)
            _remote(theirs, theirs, send_sems.at[k], recv_sems.at[k], sibling).wait_recv()

    return _pcall(
        share, name="allgather_sibling_share", in_specs=[ANY] * n, out_specs=[ANY] * n,
        out_shape=[jax.ShapeDtypeStruct(g.shape, g.dtype) for g in gathered],
        input_output_aliases={a: a for a in range(n)},
        scratch_shapes=[pltpu.SemaphoreType.DMA((n_ici,)), pltpu.SemaphoreType.DMA((n_ici,))],
        compiler_params=pltpu.CompilerParams(has_side_effects=True),
    )(*gathered)


def _sibling_exchange(grads):
    n = len(grads)
    chunks = [_row_chunks(g.shape[1] // 2, g.shape[2] * g.dtype.itemsize) for g in grads]
    n_sem = 4 * sum(len(ch) for ch in chunks)

    def body(*refs):
        ins, gots = refs[:n], refs[n:2 * n]
        send_sems, recv_sems = refs[2 * n:]
        x, y, c = _place()
        sibling = (x, y, 1 - c)
        work = []
        for a in range(n):
            half = ins[a].shape[1] // 2
            for piece in range(4):
                for s, m in chunks[a]:
                    k = len(work)
                    cp = _remote(ins[a].at[piece, pl.ds((1 - c) * half + s, m)], gots[a].at[piece, pl.ds(s, m)],
                                 send_sems.at[k], recv_sems.at[k], sibling)
                    cp.start()
                    work.append(cp)
        for cp in work:
            cp.wait()

    return _pcall(
        body, name="grad_sibling_exchange", in_specs=[ANY] * n, out_specs=[ANY] * n,
        out_shape=[jax.ShapeDtypeStruct((4, g.shape[1] // 2, g.shape[2]), g.dtype) for g in grads],
        scratch_shapes=[pltpu.SemaphoreType.DMA((n_sem,)), pltpu.SemaphoreType.DMA((n_sem,))],
        compiler_params=pltpu.CompilerParams(has_side_effects=True),
    )(*grads)


def _sibling_gather(fulls):
    n = len(fulls)
    chunks = [_row_chunks(f.shape[0] // 2, f.shape[1] * f.dtype.itemsize) for f in fulls]
    n_sem = sum(len(ch) for ch in chunks)

    def body(*refs):
        outs = refs[n:2 * n]
        send_sems, recv_sems = refs[2 * n:]
        x, y, c = _place()
        sibling = (x, y, 1 - c)
        work = []
        for a in range(n):
            h = outs[a].shape[0] // 2
            for s, m in chunks[a]:
                k = len(work)
                mine = outs[a].at[pl.ds(c * h + s, m)]
                cp = _remote(mine, mine, send_sems.at[k], recv_sems.at[k], sibling)
                cp.start()
                work.append((a, s, m, cp))
        for k, (a, s, m, cp) in enumerate(work):
            h = outs[a].shape[0] // 2
            cp.wait_send()
            theirs = outs[a].at[pl.ds((1 - c) * h + s, m)]
            _remote(theirs, theirs, send_sems.at[k], recv_sems.at[k], sibling).wait_recv()

    return _pcall(
        body, name="grad_sibling_gather", in_specs=[ANY] * n, out_specs=[ANY] * n,
        out_shape=[jax.ShapeDtypeStruct(f.shape, f.dtype) for f in fulls],
        input_output_aliases={a: a for a in range(n)},
        scratch_shapes=[pltpu.SemaphoreType.DMA((n_sem,)), pltpu.SemaphoreType.DMA((n_sem,))],
        compiler_params=pltpu.CompilerParams(has_side_effects=True),
    )(*fulls)


def _pair_sum(grad, got, place, name):
    _, rows, cols = grad.shape
    half = rows // 2
    tr = _row_tile(half, cols, 16)

    def body(p_ref, a_ref, b_ref, o_ref):
        o_ref[...] = (a_ref[...].astype(F32) + b_ref[...].astype(F32)).astype(BF16)

    return _pcall(
        body, name=name,
        grid_spec=pltpu.PrefetchScalarGridSpec(
            num_scalar_prefetch=1, grid=(4, half // tr),
            in_specs=[pl.BlockSpec((None, tr, cols), lambda k, i, p: (k, p[1] * (half // tr) + i, 0)),
                      pl.BlockSpec((None, tr, cols), lambda k, i, p: (k, i, 0))],
            out_specs=pl.BlockSpec((None, tr, cols), lambda k, i, p: (k, i, 0))),
        out_shape=jax.ShapeDtypeStruct((4, half, cols), BF16),
        compiler_params=_params("parallel", "parallel"),
    )(place, grad, got)


def _chip_sum(sums, got, place, name):
    _, h, cols = sums.shape
    tr = _row_tile(h, cols, 16)

    def body(p_ref, own_ref, g0, g1, g2, o_ref):
        o_ref[...] = ((own_ref[...].astype(F32) + g0[...].astype(F32)) + g1[...].astype(F32)) + g2[...].astype(F32)

    gspec = lambda j: pl.BlockSpec((None, tr, cols), lambda i, p: (j, i, 0))
    return _pcall(
        body, name=name,
        grid_spec=pltpu.PrefetchScalarGridSpec(
            num_scalar_prefetch=1, grid=(h // tr,),
            in_specs=[pl.BlockSpec((None, tr, cols), lambda i, p: (p[0], i, 0)), gspec(0), gspec(1), gspec(2)],
            out_specs=pl.BlockSpec((tr, cols), lambda i, p: (p[1] * (h // tr) + i, 0))),
        out_shape=jax.ShapeDtypeStruct((2 * h, cols), F32),
        compiler_params=_params("parallel"),
    )(place, sums, got, got, got)


def _allgather8(buf, name):
    rows = buf.shape[0]

    def body(in_ref, out_ref, send_sems, recv_sems):
        x, y, c = _place()
        me = 4 * x + 2 * y + c
        out_ref[me] = in_ref[...]
        work = []
        for rel in range(1, 8):
            fx, fy, fc = (rel >> 2) & 1, (rel >> 1) & 1, rel & 1
            to = (x ^ fx, y ^ fy, c ^ fc)
            cp = _remote(in_ref, out_ref.at[me], send_sems.at[rel - 1], recv_sems.at[rel - 1], to)
            cp.start()
            work.append((cp, 4 * to[0] + 2 * to[1] + to[2]))
        for rel, (cp, frm) in enumerate(work):
            cp.wait_send()
            blk = out_ref.at[frm]
            _remote(blk, blk, send_sems.at[rel], recv_sems.at[rel], (x, y, c)).wait_recv()

    return _pcall(
        body, name=name, in_specs=[pl.BlockSpec(memory_space=pltpu.VMEM)],
        out_specs=pl.BlockSpec(memory_space=pltpu.VMEM),
        out_shape=jax.ShapeDtypeStruct((8, rows, LANE), F32),
        scratch_shapes=[pltpu.SemaphoreType.DMA((7,)), pltpu.SemaphoreType.DMA((7,))],
        compiler_params=pltpu.CompilerParams(has_side_effects=True),
    )(buf)


def _pack_rows(arrs):
    parts = []
    for a in arrs:
        f = a.reshape(-1).astype(F32)
        parts.append(jnp.pad(f, (0, (-f.shape[0]) % LANE)))
    flat = jnp.concatenate(parts)
    rows = -(-flat.shape[0] // LANE)
    rows8 = -(-rows // 8) * 8
    return jnp.pad(flat, (0, rows8 * LANE - flat.shape[0])).reshape(rows8, LANE)


def _unpack_rows(buf, shapes):
    flat = buf.reshape(-1)
    outs, off = [], 0
    for s in shapes:
        n = int(np.prod(s))
        outs.append(flat[off:off + n].reshape(s))
        off += -(-n // LANE) * LANE
    return outs


def _local_grads(x, p, target, wseg, w_br16, w_out16, w_ple16, b_gate, conv_w, conv_b, dt_bias, a_log, d_skip,
                 ssm_norm_w, ln_g, ln_b, rel_bias, finish_dx):
    nb, seq, _ = x.shape
    bmaps = jnp.asarray(_bucket_maps())
    bias = _bias_tables(rel_bias, bmaps)
    bgate8 = jnp.pad(b_gate, ((0, 5), (0, 0)))
    dils = [d for _, d in PATTERNS]

    x16 = x.astype(BF16)
    p16 = p.astype(BF16)
    x16p = [_permute(x16, d) for d in dils]
    qkv = [_proj(x16p[g], [wseg["qkv%d" % g]], BF16, "proj_qkv%d" % g, True)[0].reshape(
        nb, dils[g], seq // dils[g], -1) for g in range(3)]
    nat = {}
    for gi, (group, tm) in enumerate(NAT_GROUPS):
        outs = _proj(x16, [wseg[s] for s in group], F32, "proj_nat%d" % gi, True, tm)
        nat.update(zip(group, outs))
    att = [_attn_fwd(qkv[g], bias[g * GROUP_HEADS:(g + 1) * GROUP_HEADS], dils[g], "attn_fwd%d" % g) for g in range(3)]
    natural = lambda t, g: _unpermute(t.reshape(nb, seq, t.shape[-1]), dils[g])
    oa, o_att, lse = _combine_fwd(att[0][0], att[0][1], natural(att[1][0], 1), natural(att[1][1], 1),
                                  natural(att[2][0], 2), natural(att[2][1], 2), nat["gatt"])

    conv_wg, conv_bg = _xbc_group_order(conv_w), _xbc_group_order(conv_b)
    act = _conv_fwd(nat["xbc"], conv_wg, conv_bg, "conv_fwd")
    dt_sp, dt_sg = _softplus_sig(nat["dt"], jnp.pad(dt_bias, ((0, 0), (0, LANE - SSM_HEADS))))
    dtg, sgg = _group_lanes(dt_sp), _group_lanes(dt_sg)
    alog_g, dskip_g = _group_lanes(a_log), _group_lanes(d_skip)
    y_ssm, y_all, sprev = _ssd_fwd(act, dtg, nat["z"], alog_g, dskip_g, ssm_norm_w)

    w_bra, w_brb = w_br16[:ATT_OUT], w_br16[ATT_OUT:]
    y_a, = _proj(oa, [w_bra], F32, "proj_ya")
    y_b, = _proj(y_ssm, [w_brb], F32, "proj_yb")
    merged = _merge_fwd(y_a, y_b, nat["gm"], bgate8)
    mix, = _proj(merged, [w_out16], F32, "proj_mix")
    pw, = _proj(p16, [w_ple16], F32, "proj_ple")

    dx, dpre16, dpw16, dgp16, ln_sums = _ln_loss(x, mix, nat["gp"], pw, target, bgate8, ln_g, ln_b)
    loss_sum = (0.5 / D_MODEL) * jnp.sum(ln_sums[3])
    dmerged = _dx([dpre16], [w_out16], [], "dx_merged")
    dya16, dyb16, dgm16, mg_sums = _merge_bwd(dmerged, y_a, y_b, nat["gm"], bgate8)
    doa = _dx([dya16], [w_bra], [], "dx_oa")
    dys = _dx([dyb16], [w_brb], [], "dx_yssm")
    g_w_out, = _dw(merged, [dpre16], BF16, "dw_out")
    g_w_br = jnp.concatenate([_dw(oa, [dya16], BF16, "dw_bra")[0], _dw(y_ssm, [dyb16], BF16, "dw_brb")[0]], axis=0)
    g_w_ple, = _dw(p16, [dpw16], BF16, "dw_ple")

    do_att, do16, stats, dgatt16 = _combine_bwd(doa, nat["gatt"], o_att, lse)
    dseg = {"gatt": dgatt16, "gm": dgm16, "gp": dgp16}
    dbias = []
    for g in range(3):
        own_order = lambda t: _permute(t, dils[g]).reshape(nb, dils[g], seq // dils[g], t.shape[-1])
        cotangent = (do_att, o_att, lse) if g == 0 else (own_order(do16), own_order(stats))
        dqkv, db = _attn_bwd(qkv[g], bias[g * GROUP_HEADS:(g + 1) * GROUP_HEADS], cotangent, dils[g],
                             "attn_bwd%d" % g)
        dseg["qkv%d" % g] = dqkv.reshape(nb, seq, -1)
        dbias.append(db)
    g_rel = _bias_grad(jnp.concatenate(dbias, axis=0), bmaps)[:, 0, :NUM_BUCKETS].T

    dact, ddtg, dz, ssd_small, g_normw = _ssd_bwd(
        act, dtg, sgg, nat["z"], y_all, dys, sprev, alog_g, dskip_g, ssm_norm_w)
    dseg["z"] = dz
    dseg["dt"] = jnp.pad(_ungroup_lanes(ddtg), ((0, 0), (0, 0), (0, LANE - SSM_HEADS)))
    dpre, conv_sums = _conv_bwd_pre(dact, nat["xbc"], conv_wg, conv_bg, "conv_bwd")
    dseg["xbc"] = _conv_bwd_x(dpre, conv_wg, "conv_bwd_x")
    csum = _xbc_reference_order(conv_sums)

    dx_perm = [_unpermute(_dx([dseg["qkv%d" % g]], [wseg["qkv%d" % g]], [], "dx_qkv%d" % g, True), dils[g])
               for g in (1, 2)]
    dwseg = {"qkv%d" % g: _dw(x16p[g], [dseg["qkv%d" % g]], BF16, "dw_qkv%d" % g, True)[0] for g in range(3)}
    for gi, group in enumerate(DW_GROUPS):
        dwseg.update(zip(group, _dw(x16, [dseg[s] for s in group], BF16, "dw_nat%d" % gi, True)))
    names = ["qkv0"] + [s for group, _ in NAT_GROUPS for s in group]
    dx = finish_dx([dseg[s] for s in names], [wseg[s] for s in names], [dx] + dx_perm, dwseg, g_w_br, g_w_out, g_w_ple)

    small = dict(
        b_gate=jnp.stack([mg_sums[0], mg_sums[1], ln_sums[2]]),
        conv_w=csum[0:4], conv_b=csum[4:5],
        dt_bias=_ungroup_lanes(ssd_small[:, 2:3, :]), a_log=_ungroup_lanes(ssd_small[:, 0:1, :]),
        d_skip=_ungroup_lanes(ssd_small[:, 1:2, :]), ssm_norm_w=g_normw,
        ln_g=ln_sums[0:1], ln_b=ln_sums[1:2], rel_bias=g_rel)
    return loss_sum, dx, small


DX_TM = 256
SMALL_ORDER = ("b_gate", "conv_w", "conv_b", "dt_bias", "a_log", "d_skip", "ssm_norm_w", "ln_g", "ln_b", "rel_bias")
SMALL_FULL_SHAPES = dict(b_gate=(3, 1024), conv_w=(4, 3072), conv_b=(1, 3072), dt_bias=(1, 32), a_log=(1, 32),
                         d_skip=(1, 32), ssm_norm_w=(1, 2048), ln_g=(1, 1024), ln_b=(1, 1024), rel_bias=(32, 36))


def kernel(x, p, w_in, b_gate, conv_w, conv_b, dt_bias, a_log, d_skip, ssm_norm_w, w_branch, w_out, w_ple, ln_g, ln_b, rel_bias, loss_target, m_w_in, m_b_gate, m_conv_w, m_conv_b, m_dt_bias, m_a_log, m_d_skip, m_ssm_norm_w, m_w_branch, m_w_out, m_w_ple, m_ln_g, m_ln_b, m_rel_bias, v_w_in, v_b_gate, v_conv_w, v_conv_b, v_dt_bias, v_a_log, v_d_skip, v_ssm_norm_w, v_w_branch, v_w_out, v_w_ple, v_ln_g, v_ln_b, v_rel_bias):
    cx, cy, cc = _place()
    chip = 2 * cx + cy
    dev = 4 * cx + 2 * cy + cc

    w_in_t = jnp.transpose(w_in[0])
    win16 = _shard_to_window(w_in_t, chip)
    g_win, g_br, g_out, g_ple = _allgather_pieces(
        [win16, w_branch[0].astype(BF16), w_out[0].astype(BF16), w_ple[0].astype(BF16)])
    wseg = _assemble(g_win)
    w_br16 = g_br.reshape(4 * 704, D_MODEL)
    w_out16 = g_out.reshape(D_MODEL, D_MODEL)
    w_ple16 = jnp.transpose(g_ple, (1, 0, 2)).reshape(PLE_DIM, D_MODEL)
    shards = _allgather8(_pack_rows([b_gate[0], conv_w[0]]), "allgather_small_params")
    per_chip = [_unpack_rows(shards[2 * k], [(3, 256), (4, 768)]) for k in range(4)]
    b_gate_full = jnp.concatenate([pc[0] for pc in per_chip], axis=1)
    conv_w_full = jnp.concatenate([pc[1] for pc in per_chip], axis=1)

    place = jnp.stack([chip, cc]).astype(jnp.int32)
    reduced = []

    def finish_dx(dhs, ws, accs, dwseg, d_br, d_out, d_ple):
        grads = [_pack(dwseg), d_br.reshape(4, 704, D_MODEL), d_out.reshape(4, 256, D_MODEL),
                 jnp.transpose(d_ple.reshape(PLE_DIM, 4, 256), (1, 0, 2))]
        got = _sibling_exchange(grads)
        chip_sums = [_pair_sum(g, t, place, "grad_pair_sum_%d" % i) for i, (g, t) in enumerate(zip(grads, got))]
        dx, others = _dx(dhs, ws, accs, "dx_w_in_and_grad_chip_scatter", True, DX_TM, chip_sums)
        fulls = [_chip_sum(s, t, place, "grad_chip_sum_%d" % i) for i, (s, t) in enumerate(zip(chip_sums, others))]
        reduced.extend(_sibling_gather(fulls))
        return dx

    loss_sum, grad_x, small = _local_grads(
        x, p[0], loss_target, wseg, w_br16, w_out16, w_ple16, b_gate_full, conv_w_full, conv_b, dt_bias, a_log,
        d_skip, ssm_norm_w, ln_g, ln_b, rel_bias, finish_dx)
    loss = lax.psum(loss_sum, ("x", "y", "c"))
    big = reduced
    g_w_in = _window_to_shard(big[0], chip)
    g_w_branch, g_w_out, g_w_ple = big[1], big[2], big[3]
    parts = _allgather8(_pack_rows([small[n] for n in SMALL_ORDER]), "allgather_small_grads")
    small_sum = _sum_rows([parts[i] for i in range(8)], F32, "small_grad_sum")
    sg = dict(zip(SMALL_ORDER, _unpack_rows(small_sum, [SMALL_FULL_SHAPES[n] for n in SMALL_ORDER])))
    sg["b_gate"] = lax.dynamic_slice_in_dim(sg["b_gate"], chip * 256, 256, axis=1)
    sg["conv_w"] = lax.dynamic_slice_in_dim(sg["conv_w"], chip * 768, 768, axis=1)
    del dev

    upd = {}
    upd["w_in"] = [jnp.transpose(t) for t in _adamw(w_in_t, g_w_in, jnp.transpose(m_w_in[0]),
                                                      jnp.transpose(v_w_in[0]), "adamw_w_in")]
    upd["w_branch"] = _adamw(w_branch[0], g_w_branch, m_w_branch[0], v_w_branch[0], "adamw_w_branch")
    upd["w_out"] = _adamw(w_out[0], g_w_out, m_w_out[0], v_w_out[0], "adamw_w_out")
    upd["w_ple"] = _adamw(w_ple[0], g_w_ple, m_w_ple[0], v_w_ple[0], "adamw_w_ple")
    small_w = dict(b_gate=b_gate, conv_w=conv_w, conv_b=conv_b, dt_bias=dt_bias, a_log=a_log, d_skip=d_skip,
                   ssm_norm_w=ssm_norm_w, ln_g=ln_g, ln_b=ln_b, rel_bias=rel_bias)
    small_m = dict(b_gate=m_b_gate, conv_w=m_conv_w, conv_b=m_conv_b, dt_bias=m_dt_bias, a_log=m_a_log,
                   d_skip=m_d_skip, ssm_norm_w=m_ssm_norm_w, ln_g=m_ln_g, ln_b=m_ln_b, rel_bias=m_rel_bias)
    small_v = dict(b_gate=v_b_gate, conv_w=v_conv_w, conv_b=v_conv_b, dt_bias=v_dt_bias, a_log=v_a_log,
                   d_skip=v_d_skip, ssm_norm_w=v_ssm_norm_w, ln_g=v_ln_g, ln_b=v_ln_b, rel_bias=v_rel_bias)
    shapes = [small_w[n].shape for n in SMALL_ORDER]
    s_delta, s_m, s_v = _adamw(_pack_rows([small_w[n] for n in SMALL_ORDER]), _pack_rows([sg[n] for n in SMALL_ORDER]),
                               _pack_rows([small_m[n] for n in SMALL_ORDER]), _pack_rows([small_v[n] for n in SMALL_ORDER]),
                               "adamw_small")
    for i, n in enumerate(SMALL_ORDER):
        upd[n] = tuple(_unpack_rows(t, shapes)[i] for t in (s_delta, s_m, s_v))
        sg[n] = sg[n].reshape(small_w[n].shape)

    order = ("w_in", "b_gate", "conv_w", "conv_b", "dt_bias", "a_log", "d_skip", "ssm_norm_w", "w_branch", "w_out",
             "w_ple", "ln_g", "ln_b", "rel_bias")
    grads = dict(sg, w_in=jnp.transpose(g_w_in)[None],w_branch=g_w_branch[None], w_out=g_w_out[None], w_ple=g_w_ple[None])
    lead = lambda n, t: t[None] if n in ("w_in", "w_branch", "w_out", "w_ple") else t
    return (loss, grad_x, *[grads[n] for n in order], *[lead(n, upd[n][0]) for n in order],
            *[lead(n, upd[n][1]) for n in order], *[lead(n, upd[n][2]) for n in order])
```

```python
import functools
import math

import numpy as np
import jax
import jax.numpy as jnp
from jax import lax
from jax.experimental import pallas as pl
from jax.experimental.pallas import tpu as pltpu

F32, BF16 = jnp.float32, jnp.bfloat16

D_MODEL = 1024
HEAD_DIM = 64
GROUP_HEADS = 12
ATT_OUT = GROUP_HEADS * HEAD_DIM
PATTERNS = ((128, 1), (512, 4), (2048, 16))
BAND = 128
NUM_BUCKETS = 32
MAX_DISTANCE = 2048
D_INNER = 2048
SSM_HEADS = 32
SSM_GROUPS = 4
GROUP_SSM_HEADS = SSM_HEADS // SSM_GROUPS
D_STATE = 128
CHUNK = 128
PLE_DIM = 256
ALPHA = 2.0 ** 0.25
LN_EPS = 1e-5
RMS_EPS = 1e-5
ADAM_LR, ADAM_B1, ADAM_B2, ADAM_EPS, ADAM_WD, ADAM_STEP = 0.001, 0.9, 0.999, 1e-08, 0.01, 10
NEG = -1e30

QKV_W = 3 * ATT_OUT
IN_COLS = 15904
SHARD_COLS = IN_COLS // 4
DT_COL = 12800
ROW_TILE = 16
WIN_ROWS = 4000


def _win_offset(k):
    return (k * SHARD_COLS) % ROW_TILE


def _win_start(k):
    return k * SHARD_COLS - _win_offset(k)

VMEM_LIMIT_BYTES = 56 * 1024 * 1024
LANE = 128
MESH = pl.DeviceIdType.MESH
NT = (((1,), (1,)), ((), ()))
TN = (((0,), (0,)), ((), ()))


def _pcall(body, **kw):
    return pl.pallas_call(body, **kw)


def _params(*sem):
    return pltpu.CompilerParams(dimension_semantics=sem, vmem_limit_bytes=VMEM_LIMIT_BYTES)


def _sigmoid(v):
    return jax.nn.sigmoid(v)


MM_TM = 512


def _permute(t, d):
    nb, seq, ch = t.shape
    return t if d == 1 else t.reshape(nb, seq // d, d, ch).transpose(0, 2, 1, 3).reshape(nb, seq, ch)


def _unpermute(t, d):
    nb, seq, ch = t.shape
    return t if d == 1 else t.reshape(nb, d, seq // d, ch).transpose(0, 2, 1, 3).reshape(nb, seq, ch)


def _tok_spec(tm, width):
    return pl.BlockSpec((None, tm, width), lambda b, i: (b, i, 0))


def _whole(arr, single_buffer=False):
    mode = dict(pipeline_mode=pl.Buffered(1)) if single_buffer else {}
    return pl.BlockSpec(arr.shape, lambda b, i: (0,) * arr.ndim, **mode)


def _proj(a3, ws, out_dtype, name, w_rows_are_outputs=False, tm=MM_TM):
    nb, seq, kdim = a3.shape
    nw = len(ws)
    widths = [w.shape[0] if w_rows_are_outputs else w.shape[1] for w in ws]

    def body(*refs):
        a = refs[0][...].astype(BF16)
        for w_ref, o_ref in zip(refs[1:1 + nw], refs[1 + nw:]):
            if w_rows_are_outputs:
                v = lax.dot_general(a, w_ref[...], NT, preferred_element_type=F32)
            else:
                v = jnp.dot(a, w_ref[...], preferred_element_type=F32)
            o_ref[...] = v.astype(out_dtype)

    return _pcall(
        body, name=name, grid=(nb, seq // tm),
        in_specs=[_tok_spec(tm, kdim)] + [_whole(w) for w in ws],
        out_specs=[_tok_spec(tm, n) for n in widths],
        out_shape=[jax.ShapeDtypeStruct((nb, seq, n), out_dtype) for n in widths],
        compiler_params=_params("parallel", "parallel"),
    )(a3, *ws)


def _dx(dhs, ws, accs, name, w_rows_are_outputs=False, tm=MM_TM, scatter=None):
    nb, seq, _ = dhs[0].shape
    nd, nacc = len(dhs), len(accs)
    kout = ws[0].shape[1] if w_rows_are_outputs else ws[0].shape[0]
    sums = scatter or []
    ns = len(sums)
    chunks = [_row_chunks(s.shape[1], s.shape[2] * s.dtype.itemsize) for s in sums]
    n_sem = 3 * sum(len(ch) for ch in chunks)
    grid = (nb, seq // tm)

    def body(*refs):
        n_in = 2 * nd + nacc
        sum_refs, o_ref, got_refs = refs[n_in:n_in + ns], refs[n_in + ns], refs[n_in + ns + 1:n_in + 2 * ns + 1]

        def copies():
            send_sems, recv_sems = refs[-2], refs[-1]
            x, y, c = _place()
            out = []
            for a in range(ns):
                for s, m in chunks[a]:
                    for j, (cx, cy) in enumerate(_other_chips(x, y)):
                        k = len(out)
                        out.append(_remote(sum_refs[a].at[2 * cx + cy, pl.ds(s, m)], got_refs[a].at[j, pl.ds(s, m)],
                                           send_sems.at[k], recv_sems.at[k], (cx, cy, c)))
            return out

        if ns:
            @pl.when((pl.program_id(0) == 0) & (pl.program_id(1) == 0))
            def _():
                for cp in copies():
                    cp.start()

        v = None
        for dh_ref, w_ref in zip(refs[:nd], refs[nd:2 * nd]):
            dh = dh_ref[...].astype(BF16)
            if w_rows_are_outputs:
                t = jnp.dot(dh, w_ref[...], preferred_element_type=F32)
            else:
                t = lax.dot_general(dh, w_ref[...], NT, preferred_element_type=F32)
            v = t if v is None else v + t
        for a_ref in refs[2 * nd:n_in]:
            v = v + a_ref[...]
        o_ref[...] = v

        if ns:
            @pl.when((pl.program_id(0) == grid[0] - 1) & (pl.program_id(1) == grid[1] - 1))
            def _():
                for cp in copies():
                    cp.wait()

    out = _pcall(
        body, name=name, grid=grid,
        in_specs=[_tok_spec(tm, dh.shape[-1]) for dh in dhs] + [_whole(w, bool(ns)) for w in ws]
        + [_tok_spec(tm, kout)] * nacc + [ANY] * ns,
        out_specs=[_tok_spec(tm, kout)] + [ANY] * ns,
        out_shape=[jax.ShapeDtypeStruct((nb, seq, kout), F32)]
        + [jax.ShapeDtypeStruct((3,) + s.shape[1:], s.dtype) for s in sums],
        input_output_aliases={2 * nd: 0} if nacc else {},
        scratch_shapes=[pltpu.SemaphoreType.DMA((n_sem,)), pltpu.SemaphoreType.DMA((n_sem,))] if ns else [],
        compiler_params=pltpu.CompilerParams(
            dimension_semantics=("arbitrary", "arbitrary") if ns else ("parallel", "parallel"),
            vmem_limit_bytes=VMEM_LIMIT_BYTES, has_side_effects=bool(ns)),
    )(*dhs, *ws, *accs, *sums)
    return (out[0], list(out[1:])) if ns else out[0]


def _dw(a3, dhs, out_dtype, name, rows_are_outputs=False):
    nb, seq, kdim = a3.shape
    nd = len(dhs)
    grid = (nb, seq // MM_TM)
    shapes = [(dh.shape[-1], kdim) if rows_are_outputs else (kdim, dh.shape[-1]) for dh in dhs]

    def body(*refs):
        b, i = pl.program_id(0), pl.program_id(1)
        dh_refs, o_refs, acc_refs = refs[1:1 + nd], refs[1 + nd:1 + 2 * nd], refs[1 + 2 * nd:]

        @pl.when((b == 0) & (i == 0))
        def _():
            for acc_ref in acc_refs:
                acc_ref[...] = jnp.zeros_like(acc_ref)

        a = refs[0][...].astype(BF16)
        for dh_ref, acc_ref in zip(dh_refs, acc_refs):
            dh = dh_ref[...].astype(BF16)
            acc_ref[...] += lax.dot_general(*((dh, a) if rows_are_outputs else (a, dh)), TN,
                                            preferred_element_type=F32)

        @pl.when((b == grid[0] - 1) & (i == grid[1] - 1))
        def _():
            for o_ref, acc_ref in zip(o_refs, acc_refs):
                o_ref[...] = acc_ref[...].astype(out_dtype)

    return _pcall(
        body, name=name, grid=grid,
        in_specs=[_tok_spec(MM_TM, kdim)] + [_tok_spec(MM_TM, dh.shape[-1]) for dh in dhs],
        out_specs=[pl.BlockSpec(s, lambda b, i: (0, 0)) for s in shapes],
        out_shape=[jax.ShapeDtypeStruct(s, out_dtype) for s in shapes],
        scratch_shapes=[pltpu.VMEM(s, F32) for s in shapes],
        compiler_params=_params("arbitrary", "arbitrary"),
    )(a3, *dhs)


def _qkv_rows(g):
    return [(part * QKV_W + g * ATT_OUT + hp * LANE, LANE) for hp in range(ATT_OUT // LANE) for part in range(3)]


XBC_START = 3 * QKV_W + ATT_OUT + D_INNER
GROUP_CH = GROUP_SSM_HEADS * HEAD_DIM
XBC_GROUP = GROUP_CH + 2 * D_STATE
CONV_DIM = SSM_GROUPS * XBC_GROUP


def _xbc_ranges():
    out = []
    for g in range(SSM_GROUPS):
        out += [(g * GROUP_CH, GROUP_CH), (D_INNER + g * D_STATE, D_STATE),
                (D_INNER + SSM_GROUPS * D_STATE + g * D_STATE, D_STATE)]
    return out


def _xbc_group_order(t):
    return jnp.concatenate([t[..., s:s + n] for s, n in _xbc_ranges()], axis=-1)


def _xbc_reference_order(t):
    g = lambda off, n: [t[..., k * XBC_GROUP + off:k * XBC_GROUP + off + n] for k in range(SSM_GROUPS)]
    return jnp.concatenate(g(0, GROUP_CH) + g(GROUP_CH, D_STATE) + g(GROUP_CH + D_STATE, D_STATE), axis=-1)


def _segments():
    one = lambda name, start, rows: (name, [(start, rows)], max(rows, LANE))
    return [("qkv%d" % g, _qkv_rows(g), QKV_W) for g in range(3)] + [
        one("gatt", 3 * QKV_W, ATT_OUT), one("z", 3 * QKV_W + ATT_OUT, D_INNER),
        ("xbc", [(XBC_START + s, n) for s, n in _xbc_ranges()], CONV_DIM), one("dt", DT_COL, SSM_HEADS),
        one("gm", DT_COL + SSM_HEADS, 2 * D_MODEL), one("gp", DT_COL + SSM_HEADS + 2 * D_MODEL, D_MODEL)]


LAYOUT_TC = 256
NAT_GROUPS = ((("gatt", "z", "dt", "gp"), 512), (("xbc", "gm"), 256))
DW_GROUPS = (("gatt", "z", "dt", "gp"), ("xbc",), ("gm",))


def _assemble(win):
    segs = _segments()

    def body(win_ref, *outs):
        def pieces(start, rows):
            t, end = start, start + rows
            while t < end:
                k = min(t // SHARD_COLS, 3)
                shard_end = (k + 1) * SHARD_COLS
                if k < 3 and shard_end % ROW_TILE and t == shard_end - shard_end % ROW_TILE:
                    lo = t - _win_start(k)
                    yield win_ref[k, lo:lo + ROW_TILE, :] + win_ref[k + 1, 0:ROW_TILE, :]
                    t += ROW_TILE
                    continue
                upto = min(end, shard_end - shard_end % ROW_TILE if k < 3 else end)
                yield win_ref[k, t - _win_start(k):upto - _win_start(k), :]
                t = upto

        for (_, ranges, total), o_ref in zip(segs, outs):
            off = 0
            for start, rows in ranges:
                for part in pieces(start, rows):
                    o_ref[off:off + part.shape[0], :] = part
                    off += part.shape[0]
            if off < total:
                o_ref[off:total, :] = jnp.zeros((total - off, o_ref.shape[1]), BF16)

    outs = _pcall(
        body, name="assemble_w_in", grid=(D_MODEL // LAYOUT_TC,),
        in_specs=[pl.BlockSpec((4, WIN_ROWS, LAYOUT_TC), lambda i: (0, 0, i))],
        out_specs=[pl.BlockSpec((total, LAYOUT_TC), lambda i: (0, i)) for _, _, total in segs],
        out_shape=[jax.ShapeDtypeStruct((total, D_MODEL), BF16) for _, _, total in segs],
        compiler_params=_params("parallel"),
    )(win)
    return {name: o for (name, _, _), o in zip(segs, outs)}


def _pack(dsegs):
    segs = _segments()

    def body(*refs):
        ins, o_ref = refs[:-1], refs[-1]
        tail = IN_COLS - _win_start(3)
        o_ref[3, tail:, :] = jnp.zeros((WIN_ROWS - tail, o_ref.shape[2]), BF16)
        for (_, ranges, _), s_ref in zip(segs, ins):
            off = 0
            for start, rows in ranges:
                for k in range(4):
                    lo = _win_start(k)
                    a, b = max(start, lo), min(start + rows, lo + WIN_ROWS)
                    if a < b:
                        o_ref[k, a - lo:b - lo, :] = s_ref[off + a - start:off + b - start, :]
                off += rows

    return _pcall(
        body, name="pack_dw_in", grid=(D_MODEL // LAYOUT_TC,),
        in_specs=[pl.BlockSpec((total, LAYOUT_TC), lambda i: (0, i)) for _, _, total in segs],
        out_specs=pl.BlockSpec((4, WIN_ROWS, LAYOUT_TC), lambda i: (0, 0, i)),
        out_shape=jax.ShapeDtypeStruct((4, WIN_ROWS, D_MODEL), BF16),
        compiler_params=_params("parallel"),
    )(*[dsegs[name] for name, _, _ in segs])


def _shard_to_window(shard_t, k):
    def at(off):
        return lambda w: jnp.pad(w.astype(BF16), ((off, WIN_ROWS - SHARD_COLS - off), (0, 0)))

    return lax.cond(k % 2 == 1, at(_win_offset(1)), at(_win_offset(0)), shard_t)


def _window_to_shard(win, k):
    return lax.dynamic_slice(win, ((k % 2) * _win_offset(1), 0), (SHARD_COLS, D_MODEL))


def _bucket_maps():
    qi = np.arange(BAND)[:, None]
    kj = np.arange(2 * BAND)[None, :]
    delta = qi + BAND - kj
    maps = []
    for window, dil in PATTERNS:
        valid = (delta >= 0) & (delta <= window // dil)
        dist = np.maximum(delta, 0) * dil
        max_exact = NUM_BUCKETS // 2
        d_f = np.maximum(dist, 1).astype(np.float32)
        large = max_exact + (np.log(d_f / np.float32(max_exact)) / np.float32(math.log(MAX_DISTANCE / max_exact))
                             * np.float32(NUM_BUCKETS - max_exact)).astype(np.int32)
        large = np.minimum(large, NUM_BUCKETS - 1)
        bucket = np.where(dist < max_exact, dist, large)
        maps.append(np.where(valid, bucket, -1).astype(np.int32))
    return np.stack(maps)


def _bias_tables(rel_bias, bmaps):
    def body(rb_ref, bm_ref, o_ref):
        h = pl.program_id(0)
        bm = bm_ref[...]
        acc = jnp.full(bm.shape, NEG, F32)
        for b in range(NUM_BUCKETS):
            acc = jnp.where(bm == b, rb_ref[b, h], acc)
        o_ref[...] = acc

    return _pcall(
        body, name="bias_tables", grid=(3 * GROUP_HEADS,),
        in_specs=[pl.BlockSpec(memory_space=pltpu.SMEM),
                  pl.BlockSpec((None, BAND, 2 * BAND), lambda h: (h // GROUP_HEADS, 0, 0))],
        out_specs=pl.BlockSpec((None, BAND, 2 * BAND), lambda h: (h, 0, 0)),
        out_shape=jax.ShapeDtypeStruct((3 * GROUP_HEADS, BAND, 2 * BAND), F32),
        compiler_params=_params("parallel"),
    )(rel_bias, bmaps)


def _bias_grad(dbias, bmaps):
    def body(db_ref, bm_ref, o_ref):
        bm = bm_ref[...]
        db = db_ref[...]
        lane = lax.broadcasted_iota(jnp.int32, (1, LANE), 1)
        vec = jnp.zeros((1, LANE), F32)
        for b in range(NUM_BUCKETS):
            s = jnp.sum(jnp.where(bm == b, db, 0.0), keepdims=True)
            vec = jnp.where(lane == b, s, vec)
        o_ref[...] = vec

    return _pcall(
        body, name="bias_grad", grid=(3 * GROUP_HEADS,),
        in_specs=[pl.BlockSpec((None, BAND, 2 * BAND), lambda h: (h, 0, 0)),
                  pl.BlockSpec((None, BAND, 2 * BAND), lambda h: (h // GROUP_HEADS, 0, 0))],
        out_specs=pl.BlockSpec((None, 1, LANE), lambda h: (h, 0, 0)),
        out_shape=jax.ShapeDtypeStruct((3 * GROUP_HEADS, 1, LANE), F32),
        compiler_params=_params("parallel"),
    )(dbias, bmaps)


def _rows(n):
    if isinstance(n, int):
        return pl.ds(n * BAND, BAND)
    return pl.ds(pl.multiple_of(n * BAND, BAND), BAND)


def _for_blocks(blocks, nblk, per, carry):
    carry = blocks([0], carry, False)
    start = 1 + (nblk - 1) % per
    for n in range(1, start):
        carry = blocks([n], carry, True)
    trips = (nblk - start) // per
    if trips > 0:
        carry = lax.fori_loop(
            0, trips, lambda t, c: blocks([start + t * per + u for u in range(per)], c, True), carry)
    return carry


def _pairs_per_step(d):
    return {1: 1, 4: 6, 16: 6}[d]


def _head_cols(i, h, part):
    base = 3 * LANE * i + part * LANE + h * HEAD_DIM
    return slice(base, base + HEAD_DIM)


def _attn_fwd(qkv4, bias, d, name):
    nb, _, sub, _ = qkv4.shape
    nblk = sub // BAND
    scale = HEAD_DIM ** -0.5
    npair = ATT_OUT // LANE
    hps = _pairs_per_step(d)
    compact = d > 1

    def body(qkv_ref, bias_ref, o_ref, l_ref):
        def blocks(ns, carry, with_prev):
            chains = [(bi, i, h) for bi in range(len(ns)) for i in range(hps) for h in range(2)]
            scores = []
            for bi, i, h in chains:
                n = ns[bi]
                q = qkv_ref[_rows(n), _head_cols(i, h, 0)] * scale
                s_c = lax.dot_general(q, qkv_ref[_rows(n), _head_cols(i, h, 1)], NT,
                                      preferred_element_type=F32) + bias_ref[2 * i + h, :, BAND:]
                s_p = None
                if with_prev:
                    s_p = lax.dot_general(q, qkv_ref[_rows(n - 1), _head_cols(i, h, 1)], NT,
                                          preferred_element_type=F32) + bias_ref[2 * i + h, :, :BAND]
                scores.append((s_c, s_p))
            probs = []
            for s_c, s_p in scores:
                m = jnp.max(s_c, -1, keepdims=True)
                if with_prev:
                    m = jnp.maximum(m, jnp.max(s_p, -1, keepdims=True))
                e_c = jnp.exp(s_c - m)
                den = jnp.sum(e_c, -1, keepdims=True)
                e_p = None
                if with_prev:
                    e_p = jnp.exp(s_p - m)
                    den = den + jnp.sum(e_p, -1, keepdims=True)
                    e_p = e_p.astype(BF16)
                probs.append((e_c.astype(BF16), e_p, den, m))
            outs = {}
            for (bi, i, h), (e_c, e_p, den, m) in zip(chains, probs):
                n = ns[bi]
                acc = jnp.dot(e_c, qkv_ref[_rows(n), _head_cols(i, h, 2)], preferred_element_type=F32)
                if with_prev:
                    acc = acc + jnp.dot(e_p, qkv_ref[_rows(n - 1), _head_cols(i, h, 2)], preferred_element_type=F32)
                outs[(bi, i, h)] = (acc / den, jnp.broadcast_to(m + jnp.log(den), (BAND, HEAD_DIM)))
            lane = lax.broadcasted_iota(jnp.int32, (BAND, LANE), 1)
            for bi, n in enumerate(ns):
                per_head = jnp.zeros((BAND, LANE), F32)
                for i in range(hps):
                    o_ref[_rows(n), i * LANE:(i + 1) * LANE] = jnp.concatenate(
                        [outs[(bi, i, 0)][0], outs[(bi, i, 1)][0]], axis=1)
                    if compact:
                        for h in range(2):
                            per_head = jnp.where(lane == 2 * i + h, outs[(bi, i, h)][1][:, :1], per_head)
                    else:
                        l_ref[_rows(n), i * LANE:(i + 1) * LANE] = jnp.concatenate(
                            [outs[(bi, i, 0)][1], outs[(bi, i, 1)][1]], axis=1)
                if compact:
                    l_ref[_rows(n), :] = per_head
            return carry

        _for_blocks(blocks, nblk, 2 if hps == 1 else 1, 0)

    in_specs = [pl.BlockSpec((None, None, sub, 3 * LANE * hps), lambda hp, b, r: (b, r, 0, hp)),
                pl.BlockSpec((2 * hps, BAND, 2 * BAND), lambda hp, b, r: (hp, 0, 0))]
    if compact:
        return _pcall(
            body, name=name, grid=(1, nb, d), in_specs=in_specs,
            out_specs=[pl.BlockSpec((None, None, sub, ATT_OUT), lambda hp, b, r: (b, r, 0, 0)),
                       pl.BlockSpec((None, None, sub, LANE), lambda hp, b, r: (b, r, 0, 0))],
            out_shape=[jax.ShapeDtypeStruct((nb, d, sub, ATT_OUT), F32), jax.ShapeDtypeStruct((nb, d, sub, LANE), F32)],
            compiler_params=_params("parallel", "parallel", "parallel"),
        )(qkv4, bias)
    ospec = pl.BlockSpec((None, sub, hps * LANE), lambda hp, b, r: (b, 0, r * (npair // hps) + hp))
    return _pcall(
        body, name=name, grid=(npair // hps, nb, d), in_specs=in_specs, out_specs=[ospec, ospec],
        out_shape=[jax.ShapeDtypeStruct((nb, sub, d * ATT_OUT), F32)] * 2,
        compiler_params=_params("parallel", "parallel", "parallel"),
    )(qkv4, bias)


STAT_LSE_LANE = 16


def _attn_bwd(qkv4, bias, cotangent, d, name):
    nb, _, sub, _ = qkv4.shape
    nblk = sub // BAND
    scale = HEAD_DIM ** -0.5
    npair = ATT_OUT // LANE
    hps = _pairs_per_step(d)
    compact = d > 1

    def body(qkv_ref, bias_ref, *rest):
        do_ref, dqkv_ref, db_ref = rest[0], rest[-2], rest[-1]
        b, r = pl.program_id(1), pl.program_id(2)

        @pl.when((b == 0) & (r == 0))
        def _():
            db_ref[...] = jnp.zeros_like(db_ref)

        def blocks(ns, carry, with_prev):
            sides = (0, 1) if with_prev else (0,)
            chains = [(bi, i, h, sd) for bi in range(len(ns)) for i in range(hps) for h in range(2) for sd in sides]
            key_rows = lambda bi, sd: _rows(ns[bi] - sd)
            qs = {}
            for bi in range(len(ns)):
                for i in range(hps):
                    for h in range(2):
                        hl = slice(i * LANE + h * HEAD_DIM, i * LANE + (h + 1) * HEAD_DIM)
                        do = do_ref[_rows(ns[bi]), hl]
                        if compact:
                            st_ref, head = rest[1], 2 * i + h
                            ebar = st_ref[_rows(ns[bi]), head:head + 1]
                            lcol = st_ref[_rows(ns[bi]), STAT_LSE_LANE + head:STAT_LSE_LANE + head + 1]
                        else:
                            ebar = jnp.sum(do * rest[1][_rows(ns[bi]), hl], -1, keepdims=True)
                            lcol = rest[2][_rows(ns[bi]), i * LANE + h * HEAD_DIM:i * LANE + h * HEAD_DIM + 1]
                        q_scaled = qkv_ref[_rows(ns[bi]), _head_cols(i, h, 0)] * scale
                        qs[(bi, i, h)] = (q_scaled, do.astype(BF16), ebar, lcol)
            raw = []
            for bi, i, h, sd in chains:
                q, do16, _, _ = qs[(bi, i, h)]
                k = qkv_ref[key_rows(bi, sd), _head_cols(i, h, 1)]
                v = qkv_ref[key_rows(bi, sd), _head_cols(i, h, 2)]
                bias_blk = bias_ref[2 * i + h, :, :BAND] if sd else bias_ref[2 * i + h, :, BAND:]
                s = lax.dot_general(q, k, NT, preferred_element_type=F32) + bias_blk
                dp = lax.dot_general(do16, v, NT, preferred_element_type=F32)
                raw.append((s, dp))
            soft = []
            for (bi, i, h, sd), (s, dp) in zip(chains, raw):
                _, _, ebar, lcol = qs[(bi, i, h)]
                p = jnp.exp(s - lcol)
                ds = p * (dp - ebar)
                if sd:
                    db_ref[2 * i + h, :, :BAND] += ds
                else:
                    db_ref[2 * i + h, :, BAND:] += ds
                soft.append((p.astype(BF16), ds.astype(BF16)))
            grads = {}
            for (bi, i, h, sd), (p16, ds16) in zip(chains, soft):
                q, do16, _, _ = qs[(bi, i, h)]
                k = qkv_ref[key_rows(bi, sd), _head_cols(i, h, 1)]
                grads[(bi, i, h, sd)] = (
                    jnp.dot(ds16, k, preferred_element_type=F32),
                    lax.dot_general(ds16, q, TN, preferred_element_type=F32),
                    lax.dot_general(p16, do16, TN, preferred_element_type=F32))
            both = lambda bi, i, sd, which: jnp.concatenate(
                [grads[(bi, i, 0, sd)][which], grads[(bi, i, 1, sd)][which]], axis=1)
            carry = list(carry) if carry is not None else None
            for bi, n in enumerate(ns):
                for i in range(hps):
                    base = 3 * LANE * i
                    dq = both(bi, i, 0, 0)
                    if with_prev:
                        dq = dq + both(bi, i, 1, 0)
                        dqkv_ref[_rows(n - 1), base + LANE:base + 2 * LANE] = (
                            carry[2 * i] + both(bi, i, 1, 1)).astype(BF16)
                        dqkv_ref[_rows(n - 1), base + 2 * LANE:base + 3 * LANE] = (
                            carry[2 * i + 1] + both(bi, i, 1, 2)).astype(BF16)
                    dqkv_ref[_rows(n), base:base + LANE] = (dq * scale).astype(BF16)
                carry = [t for i in range(hps) for t in (both(bi, i, 0, 1), both(bi, i, 0, 2))]
            return tuple(carry)

        carry = _for_blocks(blocks, nblk, 2 if hps == 1 else 1, None)
        for i in range(hps):
            base = 3 * LANE * i
            dqkv_ref[_rows(nblk - 1), base + LANE:base + 2 * LANE] = carry[2 * i].astype(BF16)
            dqkv_ref[_rows(nblk - 1), base + 2 * LANE:base + 3 * LANE] = carry[2 * i + 1].astype(BF16)

    qspec = pl.BlockSpec((None, None, sub, 3 * LANE * hps), lambda hp, b, r: (b, r, 0, hp))
    bspec = pl.BlockSpec((2 * hps, BAND, 2 * BAND), lambda hp, b, r: (hp, 0, 0))
    if compact:
        cspecs = [pl.BlockSpec((None, None, sub, ATT_OUT), lambda hp, b, r: (b, r, 0, 0)),
                  pl.BlockSpec((None, None, sub, LANE), lambda hp, b, r: (b, r, 0, 0))]
    else:
        cspecs = [pl.BlockSpec((None, sub, hps * LANE), lambda hp, b, r: (b, 0, r * (npair // hps) + hp))] * 3
    return _pcall(
        body, name=name, grid=(npair // hps, nb, d),
        in_specs=[qspec, bspec] + cspecs, out_specs=[qspec, bspec],
        out_shape=[jax.ShapeDtypeStruct(qkv4.shape, BF16),
                   jax.ShapeDtypeStruct((GROUP_HEADS, BAND, 2 * BAND), F32)],
        compiler_params=_params("parallel", "arbitrary", "arbitrary"),
    )(qkv4, bias, *cotangent)


def _head_lanes(first_lane, one_channel):
    c = lax.broadcasted_iota(jnp.int32, (ATT_OUT, LANE), 0)
    lane = lax.broadcasted_iota(jnp.int32, (ATT_OUT, LANE), 1)
    hit = lane == first_lane + c // HEAD_DIM
    if one_channel:
        hit = hit & (c % HEAD_DIM == 0)
    return hit.astype(BF16)


def _exact_dot(v, m01, dims=None):
    parts = _split3(v)
    if dims is None:
        dot = lambda t: jnp.dot(t, m01, preferred_element_type=F32)
    else:
        dot = lambda t: lax.dot_general(t, m01, dims, preferred_element_type=F32)
    return (dot(parts[0]) + dot(parts[1])) + dot(parts[2])


def _combine_fwd(o0, l0, o1, l1, o2, l2, gatt):
    nb, seq, _ = gatt.shape
    tm = 512

    def body(o0_ref, l0_ref, o1_ref, l1_ref, o2_ref, l2_ref, g_ref, oa_ref, oatt_ref, lse_ref):
        spread = _head_lanes(0, False)
        l0v = l0_ref[...]
        l1v = _exact_dot(l1_ref[...], spread, NT)
        l2v = _exact_dot(l2_ref[...], spread, NT)
        m = jnp.maximum(jnp.maximum(l0v, l1v), l2v)
        tot = m + jnp.log(jnp.exp(l0v - m) + jnp.exp(l1v - m) + jnp.exp(l2v - m))
        o = (jnp.exp(l0v - tot) * o0_ref[...] + jnp.exp(l1v - tot) * o1_ref[...]
             + jnp.exp(l2v - tot) * o2_ref[...])
        g = g_ref[...]
        oa_ref[...] = (o * (g * _sigmoid(g))).astype(BF16)
        oatt_ref[...] = o
        lse_ref[...] = tot

    spec = pl.BlockSpec((None, tm, ATT_OUT), lambda b, i: (b, i, 0))
    lspec = pl.BlockSpec((None, tm, LANE), lambda b, i: (b, i, 0))
    return _pcall(
        body, name="attn_combine", grid=(nb, seq // tm),
        in_specs=[spec, spec, spec, lspec, spec, lspec, spec], out_specs=[spec] * 3,
        out_shape=[jax.ShapeDtypeStruct((nb, seq, ATT_OUT), BF16), jax.ShapeDtypeStruct((nb, seq, ATT_OUT), F32),
                   jax.ShapeDtypeStruct((nb, seq, ATT_OUT), F32)],
        compiler_params=_params("parallel", "parallel"),
    )(o0, l0, o1, l1, o2, l2, gatt)


def _combine_bwd(doa, gatt, o_att, lse):
    nb, seq, _ = gatt.shape
    tm = 512

    def body(doa_ref, g_ref, o_ref, l_ref, do_ref, do16_ref, st_ref, dg_ref):
        g = g_ref[...]
        sg = _sigmoid(g)
        do = doa_ref[...] * (g * sg)
        do_ref[...] = do
        do16_ref[...] = do.astype(BF16)
        st_ref[...] = (_exact_dot(do * o_ref[...], _head_lanes(0, False))
                       + _exact_dot(l_ref[...], _head_lanes(STAT_LSE_LANE, True)))
        dg_ref[...] = (doa_ref[...] * o_ref[...] * (sg * (1.0 + g * (1.0 - sg)))).astype(BF16)

    spec = pl.BlockSpec((None, tm, ATT_OUT), lambda b, i: (b, i, 0))
    lspec = pl.BlockSpec((None, tm, LANE), lambda b, i: (b, i, 0))
    return _pcall(
        body, name="attn_combine_bwd", grid=(nb, seq // tm), in_specs=[spec] * 4,
        out_specs=[spec, spec, lspec, spec],
        out_shape=[jax.ShapeDtypeStruct((nb, seq, ATT_OUT), F32), jax.ShapeDtypeStruct((nb, seq, ATT_OUT), BF16),
                   jax.ShapeDtypeStruct((nb, seq, LANE), F32), jax.ShapeDtypeStruct((nb, seq, ATT_OUT), BF16)],
        compiler_params=_params("parallel", "parallel"),
    )(doa, gatt, o_att, lse)


CONV_TM = 512
CONV_TC = 512


def _shift_down(cur, halo, k):
    rolled = pltpu.roll(cur, k, 0)
    hro = pltpu.roll(halo, k, 0)
    row = lax.broadcasted_iota(jnp.int32, hro.shape, 0)
    return jnp.concatenate([jnp.where(row < k, hro, rolled[:8]), rolled[8:]], axis=0)


def _shift_up(cur, halo, k):
    n = cur.shape[0]
    rolled = pltpu.roll(cur, n - k, 0)
    hro = pltpu.roll(halo, 8 - k, 0)
    row = lax.broadcasted_iota(jnp.int32, hro.shape, 0)
    return jnp.concatenate([rolled[:n - 8], jnp.where(row >= 8 - k, hro, rolled[n - 8:])], axis=0)


def _conv_pre(cur, halo, w_ref, b_ref):
    acc = cur * w_ref[3:4, :] + b_ref[...]
    for k in range(1, 4):
        acc = acc + _shift_down(cur, halo, k) * w_ref[3 - k:4 - k, :]
    return acc


def _conv_specs(seq):
    nblk = seq // CONV_TM
    cur = pl.BlockSpec((None, CONV_TM, CONV_TC), lambda cb, b, i: (b, i, cb))
    prev = pl.BlockSpec((None, 8, CONV_TC), lambda cb, b, i: (b, jnp.maximum(i * (CONV_TM // 8) - 1, 0), cb))
    nxt = pl.BlockSpec((None, 8, CONV_TC),
                       lambda cb, b, i: (b, jnp.minimum((i + 1) * (CONV_TM // 8), seq // 8 - 1), cb))
    wspec = pl.BlockSpec((4, CONV_TC), lambda cb, b, i: (0, cb))
    bspec = pl.BlockSpec((1, CONV_TC), lambda cb, b, i: (0, cb))
    return nblk, cur, prev, nxt, wspec, bspec


def _conv_fwd(xin, w4, bias, name):
    nb, seq, ch = xin.shape
    _, cur, prev, _, wspec, bspec = _conv_specs(seq)

    def body(x_ref, h_ref, w_ref, b_ref, o_ref):
        halo = jnp.where(pl.program_id(2) > 0, h_ref[...], 0.0)
        pre = _conv_pre(x_ref[...], halo, w_ref, b_ref)
        o_ref[...] = pre * _sigmoid(pre)

    return _pcall(
        body, name=name, grid=(ch // CONV_TC, nb, seq // CONV_TM),
        in_specs=[cur, prev, wspec, bspec], out_specs=cur,
        out_shape=jax.ShapeDtypeStruct(xin.shape, F32),
        compiler_params=_params("parallel", "parallel", "parallel"),
    )(xin, xin, w4, bias)


def _conv_bwd_pre(dact, xin, w4, bias, name):
    nb, seq, ch = xin.shape
    _, cur, prev, _, wspec, bspec = _conv_specs(seq)

    def body(da_ref, x_ref, h_ref, w_ref, b_ref, dp_ref, s_ref):
        b, i = pl.program_id(1), pl.program_id(2)

        @pl.when((b == 0) & (i == 0))
        def _():
            s_ref[...] = jnp.zeros_like(s_ref)

        halo = jnp.where(i > 0, h_ref[...], 0.0)
        x = x_ref[...]
        pre = _conv_pre(x, halo, w_ref, b_ref)
        sg = _sigmoid(pre)
        dpre = da_ref[...] * (sg * (1.0 + pre * (1.0 - sg)))
        dp_ref[...] = dpre
        s_ref[3:4, :] += jnp.sum(dpre * x, 0, keepdims=True)
        for k in range(1, 4):
            s_ref[3 - k:4 - k, :] += jnp.sum(dpre * _shift_down(x, halo, k), 0, keepdims=True)
        s_ref[4:5, :] += jnp.sum(dpre, 0, keepdims=True)

    return _pcall(
        body, name=name, grid=(ch // CONV_TC, nb, seq // CONV_TM),
        in_specs=[cur, cur, prev, wspec, bspec],
        out_specs=[cur, pl.BlockSpec((8, CONV_TC), lambda cb, b, i: (0, cb))],
        out_shape=[jax.ShapeDtypeStruct(xin.shape, F32), jax.ShapeDtypeStruct((8, ch), F32)],
        compiler_params=_params("parallel", "arbitrary", "arbitrary"),
    )(dact, xin, xin, w4, bias)


def _conv_bwd_x(dpre, w4, name):
    nb, seq, ch = dpre.shape
    nblk, cur, _, nxt, wspec, _ = _conv_specs(seq)

    def body(d_ref, n_ref, w_ref, o_ref):
        halo = jnp.where(pl.program_id(2) < nblk - 1, n_ref[...], 0.0)
        cur_v = d_ref[...]
        acc = cur_v * w_ref[3:4, :]
        for j in range(1, 4):
            acc = acc + _shift_up(cur_v, halo, j) * w_ref[3 - j:4 - j, :]
        o_ref[...] = acc.astype(BF16)

    return _pcall(
        body, name=name, grid=(ch // CONV_TC, nb, seq // CONV_TM),
        in_specs=[cur, nxt, wspec], out_specs=cur,
        out_shape=jax.ShapeDtypeStruct(dpre.shape, BF16),
        compiler_params=_params("parallel", "parallel", "parallel"),
    )(dpre, dpre, w4)


def _softplus_sig(dt_raw, dt_bias_row):
    nb, seq, _ = dt_raw.shape
    tm = 512

    def body(r_ref, b_ref, sp_ref, sg_ref):
        v = r_ref[...] + b_ref[...]
        sp_ref[...] = jnp.maximum(v, 0.0) + jnp.log1p(jnp.exp(-jnp.abs(v)))
        sg_ref[...] = _sigmoid(v)

    spec = pl.BlockSpec((None, tm, LANE), lambda b, i: (b, i, 0))
    return _pcall(
        body, name="dt_softplus", grid=(nb, seq // tm),
        in_specs=[spec, pl.BlockSpec((1, LANE), lambda b, i: (0, 0))], out_specs=[spec, spec],
        out_shape=[jax.ShapeDtypeStruct(dt_raw.shape, F32)] * 2,
        compiler_params=_params("parallel", "parallel"),
    )(dt_raw, dt_bias_row)


def _group_lanes(t):
    pads = [(0, 0)] * (t.ndim - 1) + [(0, LANE - GROUP_SSM_HEADS)]
    return jnp.stack([jnp.pad(t[..., GROUP_SSM_HEADS * g:GROUP_SSM_HEADS * (g + 1)], pads) for g in range(SSM_GROUPS)])


def _ungroup_lanes(t):
    return jnp.concatenate([t[g][..., :GROUP_SSM_HEADS] for g in range(SSM_GROUPS)], axis=-1)


def _decays(dt, al_ref):
    row = lax.broadcasted_iota(jnp.int32, (CHUNK, CHUNK), 0)
    col = lax.broadcasted_iota(jnp.int32, (CHUNK, CHUNK), 1)
    tril = (row >= col).astype(BF16)
    triu = (row <= col).astype(BF16)
    arow = -jnp.exp(al_ref[...])
    hi, mid, lo = _split3(dt * arow)
    down = lambda t: jnp.dot(tril, t, preferred_element_type=F32)
    across = lambda t: lax.dot_general(t, triu, TN, preferred_element_type=F32)
    acs = (down(hi) + down(mid)) + down(lo)
    acs_t = (across(hi) + across(mid)) + across(lo)
    return arow, acs, acs_t, row >= col, triu


def _ssd_specs(nb, seq):
    nc = seq // CHUNK
    hw = GROUP_SSM_HEADS * HEAD_DIM

    def mk(rev):
        cidx = (lambda c: nc - 1 - c) if rev else (lambda c: c)
        wide = pl.BlockSpec((None, CHUNK, hw), lambda g, b, c: (b, cidx(c), g))
        xbc = pl.BlockSpec((None, CHUNK, XBC_GROUP), lambda g, b, c: (b, cidx(c), g))
        lanes = pl.BlockSpec((None, None, CHUNK, LANE), lambda g, b, c: (g, b, cidx(c), 0))
        prev = pl.BlockSpec((None, None, None, D_STATE, hw), lambda g, b, c: (b, cidx(c), g, 0, 0))
        return wide, xbc, lanes, prev

    grow = pl.BlockSpec((None, 1, LANE), lambda g, b, c: (g, 0, 0))
    nwspec = pl.BlockSpec((1, hw), lambda g, b, c: (0, g))
    return nc, hw, mk, grow, nwspec


def _head_expand():
    hw = GROUP_SSM_HEADS * HEAD_DIM
    r = lax.broadcasted_iota(jnp.int32, (LANE, hw), 0)
    c = lax.broadcasted_iota(jnp.int32, (LANE, hw), 1)
    return ((c // HEAD_DIM) == r).astype(BF16)


def _split3(v):
    hi = v.astype(BF16)
    rest = v - hi.astype(F32)
    mid = rest.astype(BF16)
    return hi, mid, (rest - mid.astype(F32)).astype(BF16)


def _to_channels(v, e):
    hi, mid, lo = _split3(v)
    dot = lambda t: jnp.dot(t, e, preferred_element_type=F32)
    return (dot(hi) + dot(mid)) + dot(lo)


def _to_heads(w, e):
    hi, mid, lo = _split3(w)
    dot = lambda t: lax.dot_general(t, e, (((1,), (1,)), ((), ())), preferred_element_type=F32)
    return (dot(hi) + dot(mid)) + dot(lo)


def _row8(v):
    return jnp.broadcast_to(v, (8, v.shape[1]))


def _ssd_chunk_setup(dt, al_ref, ds_ref):
    arow, acs, acs_t, causal, triu = _decays(dt, al_ref)
    e = _head_expand()
    dtx = _to_channels(dt, e)
    acsx = _to_channels(acs, e)
    lastx = acsx[CHUNK - 1:CHUNK, :]
    dskx = _to_channels(_row8(ds_ref[...]), e)[0:1, :]
    return arow, acs, acs_t, causal, triu, e, dtx, acsx, lastx, dskx


def _ssd_fwd(xbc, dtg, z, alog_g, dskip_g, normw):
    nb, seq, _ = xbc.shape
    nc, hw, mk, grow, nwspec = _ssd_specs(nb, seq)
    wide, xbc_spec, lanes, prev = mk(False)
    tn = (((0,), (0,)), ((), ()))

    def body(xbc_ref, dt_ref, z_ref, al_ref, ds_ref, nw_ref, ys_ref, y_ref, sp_ref, st_ref):
        @pl.when(pl.program_id(2) == 0)
        def _():
            st_ref[...] = jnp.zeros_like(st_ref)

        dt = dt_ref[...]
        _, acs, acs_t, causal, _, _, dtx, acsx, lastx, dskx = _ssd_chunk_setup(dt, al_ref, ds_ref)
        bmat = xbc_ref[:, GROUP_CH:GROUP_CH + D_STATE].astype(BF16)
        cmat = xbc_ref[:, GROUP_CH + D_STATE:].astype(BF16)
        cb = lax.dot_general(cmat, bmat, (((1,), (1,)), ((), ())), preferred_element_type=F32)
        x = xbc_ref[:, :GROUP_CH]
        xdt = x * dtx
        xdt16 = xdt.astype(BF16)
        first_head = lax.broadcasted_iota(jnp.int32, (CHUNK, LANE), 1) < HEAD_DIM
        pairs = []
        for hp in range(GROUP_SSM_HEADS // 2):
            xp = xdt16[:, hp * LANE:(hp + 1) * LANE]
            two = []
            for j in (2 * hp, 2 * hp + 1):
                lmat = jnp.exp(jnp.where(causal, acs[:, j:j + 1] - acs_t[j:j + 1, :], -jnp.inf))
                two.append(jnp.dot((cb * lmat).astype(BF16), xp, preferred_element_type=F32))
            pairs.append(jnp.where(first_head, two[0], two[1]))
        yd = jnp.concatenate(pairs, axis=1)
        s_prev = st_ref[...]
        s16 = s_prev.astype(BF16)
        sp_ref[...] = s16
        yo = jnp.dot(cmat, s16, preferred_element_type=F32) * jnp.exp(acsx)
        sts = lax.dot_general(bmat, (xdt * jnp.exp(lastx - acsx)).astype(BF16), tn, preferred_element_type=F32)
        st_ref[...] = s_prev * jnp.exp(lastx) + sts
        y = yd + yo + dskx * x
        zz = z_ref[...]
        u = y * (zz * _sigmoid(zz))
        rn = lax.rsqrt(jnp.mean(u * u, -1, keepdims=True) + RMS_EPS)
        ys_ref[...] = (u * rn * nw_ref[...]).astype(BF16)
        y_ref[...] = y

    return _pcall(
        body, name="ssd_fwd", grid=(SSM_GROUPS, nb, nc),
        in_specs=[xbc_spec, lanes, wide, grow, grow, nwspec],
        out_specs=[wide, wide, prev],
        out_shape=[jax.ShapeDtypeStruct((nb, seq, D_INNER), BF16), jax.ShapeDtypeStruct((nb, seq, D_INNER), F32),
                   jax.ShapeDtypeStruct((nb, nc, SSM_GROUPS, D_STATE, hw), BF16)],
        scratch_shapes=[pltpu.VMEM((D_STATE, hw), F32)],
        compiler_params=_params("parallel", "parallel", "arbitrary"),
    )(xbc, dtg, z, alog_g, dskip_g, normw)


def _ssd_bwd(xbc, dtg, sgg, z, y, dys, sprev, alog_g, dskip_g, normw):
    nb, seq, _ = xbc.shape
    nc, hw, mk, grow, nwspec = _ssd_specs(nb, seq)
    wide, xbc_spec, lanes, prev = mk(True)
    nt = (((1,), (1,)), ((), ()))
    tn = (((0,), (0,)), ((), ()))

    def body(xbc_ref, dt_ref, sg_ref, z_ref, y_ref, dys_ref, sp_ref, al_ref, ds_ref, nw_ref,
             dxbc_ref, ddt_ref, dz_ref, small_ref, dnw_ref, g_ref):
        b, c = pl.program_id(1), pl.program_id(2)

        @pl.when((b == 0) & (c == 0))
        def _():
            small_ref[...] = jnp.zeros_like(small_ref)
            dnw_ref[...] = jnp.zeros_like(dnw_ref)

        @pl.when(c == 0)
        def _():
            g_ref[...] = jnp.zeros_like(g_ref)

        yv, zz, dys_v, nw = y_ref[...], z_ref[...], dys_ref[...], nw_ref[...]
        sz = _sigmoid(zz)
        silu = zz * sz
        u = yv * silu
        rn = lax.rsqrt(jnp.mean(u * u, -1, keepdims=True) + RMS_EPS)
        gn = dys_v * nw
        du = rn * gn - u * (rn * rn * rn) * jnp.mean(u * gn, -1, keepdims=True)
        dnw_ref[...] += jnp.sum(dys_v * u * rn, 0, keepdims=True)
        dy = du * silu
        dz_ref[...] = du * yv * (sz * (1.0 + zz * (1.0 - sz)))

        dt = dt_ref[...]
        arow, acs, acs_t, causal, triu, e, dtx, acsx, lastx, dskx = _ssd_chunk_setup(dt, al_ref, ds_ref)
        dfsx = jnp.exp(acsx)
        dtex = jnp.exp(lastx - acsx)
        bmat = xbc_ref[:, GROUP_CH:GROUP_CH + D_STATE].astype(BF16)
        cmat = xbc_ref[:, GROUP_CH + D_STATE:].astype(BF16)
        cb = lax.dot_general(cmat, bmat, nt, preferred_element_type=F32)
        x = xbc_ref[:, :GROUP_CH]
        xdt = x * dtx
        xdt16 = xdt.astype(BF16)
        xdte = xdt * dtex
        dy16 = dy.astype(BF16)
        dyd = dy * dfsx
        dyd16 = dyd.astype(BF16)
        s16 = sp_ref[...]
        g = g_ref[...]
        g16 = g.astype(BF16)
        cs = jnp.dot(cmat, s16, preferred_element_type=F32)
        dc_off = lax.dot_general(dyd16, s16, nt, preferred_element_type=F32)
        g_here = lax.dot_general(cmat, dyd16, tn, preferred_element_type=F32)
        bg = jnp.dot(bmat, g16, preferred_element_type=F32)
        db_st = lax.dot_general(xdte.astype(BF16), g16, nt, preferred_element_type=F32)
        ddte_w = bg * xdte
        dcd = _to_heads(_row8(jnp.sum(g * s16.astype(F32), 0, keepdims=True)), e)[0:1, :]
        lane = lax.broadcasted_iota(jnp.int32, (CHUNK, LANE), 1)
        first_head = lane < HEAD_DIM
        sub = lax.broadcasted_iota(jnp.int32, (CHUNK, LANE), 0)
        dacs = jnp.zeros((CHUNK, LANE), F32)
        colsums = jnp.zeros((CHUNK, LANE), F32)
        dcb = jnp.zeros((CHUNK, CHUNK), F32)
        pairs = []
        for hp in range(GROUP_SSM_HEADS // 2):
            xp = xdt16[:, hp * LANE:(hp + 1) * LANE]
            dyp = dy16[:, hp * LANE:(hp + 1) * LANE]
            two = []
            for idx, j in enumerate((2 * hp, 2 * hp + 1)):
                lmat = jnp.exp(jnp.where(causal, acs[:, j:j + 1] - acs_t[j:j + 1, :], -jnp.inf))
                mf = cb * lmat
                dy_h = jnp.where(first_head if idx == 0 else jnp.logical_not(first_head), dyp, jnp.zeros_like(dyp))
                dm = lax.dot_general(dy_h, xp, nt, preferred_element_type=F32)
                two.append(lax.dot_general(mf.astype(BF16), dyp, tn, preferred_element_type=F32))
                wmat = dm * mf
                dcb = dcb + dm * lmat
                dacs = jnp.where(lane == j, jnp.sum(wmat, -1, keepdims=True), dacs)
                colsums = jnp.where(sub == j, jnp.sum(wmat, 0, keepdims=True), colsums)
            pairs.append(jnp.where(first_head, two[0], two[1]))
        dxdt = bg * dtex + jnp.concatenate(pairs, axis=1)
        dacs = dacs - colsums.T + _to_heads(dyd * cs - ddte_w, e)
        cd_row = jnp.exp(acs[CHUNK - 1:CHUNK, :])
        tail = _to_heads(_row8(jnp.sum(ddte_w, 0, keepdims=True)), e)[0:1, :] + dcd * cd_row
        dacs = dacs + jnp.where(sub == CHUNK - 1, tail, 0.0)
        d_hi, d_mid, d_lo = _split3(dacs)
        up = lambda t: jnp.dot(triu, t, preferred_element_type=F32)
        da = (up(d_hi) + up(d_mid)) + up(d_lo)
        ddt_raw = (da * arow + _to_heads(dxdt * x, e)) * sg_ref[...]
        ddt_ref[...] = ddt_raw
        small_ref[0:1, :] += jnp.sum(da * dt, 0, keepdims=True) * arow
        small_ref[1:2, :] += _to_heads(_row8(jnp.sum(dy * x, 0, keepdims=True)), e)[0:1, :]
        small_ref[2:3, :] += jnp.sum(ddt_raw, 0, keepdims=True)
        dcb16 = dcb.astype(BF16)
        dxbc_ref[:, GROUP_CH + D_STATE:] = dc_off + jnp.dot(dcb16, bmat, preferred_element_type=F32)
        dxbc_ref[:, GROUP_CH:GROUP_CH + D_STATE] = db_st + lax.dot_general(dcb16, cmat, tn,
                                                                            preferred_element_type=F32)
        dxbc_ref[:, :GROUP_CH] = dxdt * dtx + dskx * dy
        g_ref[...] = g * jnp.exp(lastx) + g_here

    return _pcall(
        body, name="ssd_bwd", grid=(SSM_GROUPS, nb, nc),
        in_specs=[xbc_spec, lanes, lanes, wide, wide, wide, prev, grow, grow, nwspec],
        out_specs=[xbc_spec, lanes, wide,
                   pl.BlockSpec((None, 8, LANE), lambda g, b, c: (g, 0, 0)), nwspec],
        out_shape=[jax.ShapeDtypeStruct((nb, seq, CONV_DIM), F32),
                   jax.ShapeDtypeStruct((SSM_GROUPS, nb, seq, LANE), F32),
                   jax.ShapeDtypeStruct((nb, seq, D_INNER), F32),
                   jax.ShapeDtypeStruct((SSM_GROUPS, 8, LANE), F32),
                   jax.ShapeDtypeStruct((1, D_INNER), F32)],
        scratch_shapes=[pltpu.VMEM((D_STATE, hw), F32)],
        compiler_params=_params("parallel", "arbitrary", "arbitrary"),
    )(xbc, dtg, sgg, z, y, dys, sprev, alog_g, dskip_g, normw)


EW_TM = 256


def _merge_fwd(y_a, y_b, gm, bgate):
    nb, seq, _ = y_a.shape

    def body(a_ref, b_ref, ga_ref, gb_ref, bg_ref, o_ref):
        sa = _sigmoid(ga_ref[...] + bg_ref[0:1, :])
        sb = _sigmoid(gb_ref[...] + bg_ref[1:2, :])
        o_ref[...] = (sa * a_ref[...] + sb * b_ref[...]).astype(BF16)

    spec = pl.BlockSpec((None, EW_TM, D_MODEL), lambda b, i: (b, i, 0))
    spec1 = pl.BlockSpec((None, EW_TM, D_MODEL), lambda b, i: (b, i, 1))
    return _pcall(
        body, name="merge_fwd", grid=(nb, seq // EW_TM),
        in_specs=[spec, spec, spec, spec1, pl.BlockSpec((8, D_MODEL), lambda b, i: (0, 0))], out_specs=spec,
        out_shape=jax.ShapeDtypeStruct((nb, seq, D_MODEL), BF16),
        compiler_params=_params("parallel", "parallel"),
    )(y_a, y_b, gm, gm, bgate)


def _merge_bwd(dmerged, y_a, y_b, gm, bgate):
    nb, seq, _ = y_a.shape

    def body(dm_ref, a_ref, b_ref, ga_ref, gb_ref, bg_ref, dya_ref, dyb_ref, dg_ref, s_ref):
        @pl.when((pl.program_id(0) == 0) & (pl.program_id(1) == 0))
        def _():
            s_ref[...] = jnp.zeros_like(s_ref)

        dm = dm_ref[...]
        sa = _sigmoid(ga_ref[...] + bg_ref[0:1, :])
        sb = _sigmoid(gb_ref[...] + bg_ref[1:2, :])
        dya_ref[...] = (dm * sa).astype(BF16)
        dyb_ref[...] = (dm * sb).astype(BF16)
        dga = dm * a_ref[...] * (sa * (1.0 - sa))
        dgb = dm * b_ref[...] * (sb * (1.0 - sb))
        dg_ref[:, :D_MODEL] = dga.astype(BF16)
        dg_ref[:, D_MODEL:] = dgb.astype(BF16)
        s_ref[0:1, :] += jnp.sum(dga, 0, keepdims=True)
        s_ref[1:2, :] += jnp.sum(dgb, 0, keepdims=True)

    spec = pl.BlockSpec((None, EW_TM, D_MODEL), lambda b, i: (b, i, 0))
    spec1 = pl.BlockSpec((None, EW_TM, D_MODEL), lambda b, i: (b, i, 1))
    small = pl.BlockSpec((8, D_MODEL), lambda b, i: (0, 0))
    return _pcall(
        body, name="merge_bwd", grid=(nb, seq // EW_TM),
        in_specs=[spec, spec, spec, spec, spec1, small],
        out_specs=[spec, spec, pl.BlockSpec((None, EW_TM, 2 * D_MODEL), lambda b, i: (b, i, 0)), small],
        out_shape=[jax.ShapeDtypeStruct((nb, seq, D_MODEL), BF16), jax.ShapeDtypeStruct((nb, seq, D_MODEL), BF16),
                   jax.ShapeDtypeStruct((nb, seq, 2 * D_MODEL), BF16), jax.ShapeDtypeStruct((8, D_MODEL), F32)],
        compiler_params=_params("arbitrary", "arbitrary"),
    )(dmerged, y_a, y_b, gm, gm, bgate)


def _ln_loss(x, mix, gp, pw, target, bgate, ln_g, ln_b):
    nb, seq, _ = x.shape

    def body(x_ref, mix_ref, gp_ref, pw_ref, t_ref, bg_ref, g_ref, b_ref, dx_ref, dp_ref, dpw_ref, dgp_ref, s_ref):
        @pl.when((pl.program_id(0) == 0) & (pl.program_id(1) == 0))
        def _():
            s_ref[...] = jnp.zeros_like(s_ref)

        sp = _sigmoid(gp_ref[...] + bg_ref[2:3, :])
        pw = pw_ref[...]
        pre = ALPHA * x_ref[...] + mix_ref[...] + sp * pw
        mu = jnp.mean(pre, -1, keepdims=True)
        cen = pre - mu
        rstd = lax.rsqrt(jnp.mean(cen * cen, -1, keepdims=True) + LN_EPS)
        xhat = cen * rstd
        err = xhat * g_ref[...] + b_ref[...] - t_ref[...]
        dy = err * (1.0 / D_MODEL)
        dxh = dy * g_ref[...]
        dpre = rstd * (dxh - jnp.mean(dxh, -1, keepdims=True) - xhat * jnp.mean(dxh * xhat, -1, keepdims=True))
        dx_ref[...] = ALPHA * dpre
        dp_ref[...] = dpre.astype(BF16)
        dpw_ref[...] = (dpre * sp).astype(BF16)
        dgp = dpre * pw * (sp * (1.0 - sp))
        dgp_ref[...] = dgp.astype(BF16)
        s_ref[0:1, :] += jnp.sum(dy * xhat, 0, keepdims=True)
        s_ref[1:2, :] += jnp.sum(dy, 0, keepdims=True)
        s_ref[2:3, :] += jnp.sum(dgp, 0, keepdims=True)
        s_ref[3:4, :] += jnp.sum(err * err, 0, keepdims=True)

    spec = pl.BlockSpec((None, EW_TM, D_MODEL), lambda b, i: (b, i, 0))
    small = pl.BlockSpec((8, D_MODEL), lambda b, i: (0, 0))
    row = pl.BlockSpec((1, D_MODEL), lambda b, i: (0, 0))
    return _pcall(
        body, name="ln_loss", grid=(nb, seq // EW_TM),
        in_specs=[spec] * 5 + [small, row, row], out_specs=[spec] * 4 + [small],
        out_shape=[jax.ShapeDtypeStruct((nb, seq, D_MODEL), F32)] + [jax.ShapeDtypeStruct((nb, seq, D_MODEL), BF16)] * 3
        + [jax.ShapeDtypeStruct((8, D_MODEL), F32)],
        compiler_params=_params("arbitrary", "arbitrary"),
    )(x, mix, gp, pw, target, bgate, ln_g, ln_b)


def _adamw(w, g, m, v, name):
    rows, cols = w.shape
    tr = _row_tile(rows, cols, 8, 5 << 19)
    c1 = 1.0 - ADAM_B1 ** ADAM_STEP
    c2 = 1.0 - ADAM_B2 ** ADAM_STEP

    def body(w_ref, g_ref, m_ref, v_ref, d_ref, nm_ref, nv_ref):
        gv = g_ref[...]
        nm = ADAM_B1 * m_ref[...] + (1.0 - ADAM_B1) * gv
        nv = ADAM_B2 * v_ref[...] + (1.0 - ADAM_B2) * (gv * gv)
        d_ref[...] = -ADAM_LR * ((nm / c1) / (jnp.sqrt(nv / c2) + ADAM_EPS) + ADAM_WD * w_ref[...])
        nm_ref[...] = nm
        nv_ref[...] = nv

    spec = pl.BlockSpec((tr, cols), lambda i: (i, 0))
    return _pcall(
        body, name=name, grid=(rows // tr,), in_specs=[spec] * 4, out_specs=[spec] * 3,
        out_shape=[jax.ShapeDtypeStruct(w.shape, F32)] * 3, compiler_params=_params("parallel"),
    )(w, g, m, v)


def _sum_rows(parts, out_dtype, name):
    rows, cols = parts[0].shape
    tr = rows
    for cand in range(16, rows, 16):
        if rows % cand == 0 and cand * cols * 4 <= (1 << 20):
            tr = cand
    n = len(parts)

    def body(*refs):
        acc = refs[0][...].astype(F32)
        for r in refs[1:n]:
            acc = acc + r[...].astype(F32)
        refs[n][...] = acc.astype(out_dtype)

    spec = pl.BlockSpec((tr, cols), lambda i: (i, 0))
    return _pcall(
        body, name=name, grid=(rows // tr,), in_specs=[spec] * n, out_specs=spec,
        out_shape=jax.ShapeDtypeStruct((rows, cols), out_dtype), compiler_params=_params("parallel"),
    )(*parts)


def _place():
    return lax.axis_index("x"), lax.axis_index("y"), lax.axis_index("c")


def _other_chips(x, y):
    return [(1 - x, y), (x, 1 - y), (1 - x, 1 - y)]


def _remote(src, dst, send_sem, recv_sem, to):
    return pltpu.make_async_remote_copy(src_ref=src, dst_ref=dst, send_sem=send_sem, recv_sem=recv_sem,
                                        device_id=to, device_id_type=MESH)


ANY = pl.BlockSpec(memory_space=pl.ANY)
DMA_CHUNK_BYTES = 512 * 1024


def _row_chunks(rows, row_bytes):
    per = max(16, DMA_CHUNK_BYTES // row_bytes // 16 * 16)
    return [(s, min(per, rows - s)) for s in range(0, rows, per)]


def _row_tile(rows, cols, align, limit=1 << 21):
    best = None
    for cand in range(align, rows + 1, align):
        if rows % cand == 0 and cand * cols * 4 <= limit:
            best = cand
    return best or rows


def _allgather_pieces(pieces):
    n = len(pieces)
    halves = [_row_chunks(p.shape[0] // 2, p.shape[1] * p.dtype.itemsize) for p in pieces]
    entries = [(a, q, s, m, j) for a in range(n) for q, (s, m) in enumerate(halves[a]) for j in range(3)]
    slot = {(a, q, j): k for k, (a, q, _, _, j) in enumerate(entries)}
    n_ici = len(entries)

    def body(*refs):
        ins, outs = refs[:n], refs[n:2 * n]
        send_sems, recv_sems = refs[2 * n:]
        x, y, c = _place()
        me = 2 * x + y
        chips = _other_chips(x, y)

        def landed(a, s, m, j, core):
            half = ins[a].shape[0] // 2
            return outs[a].at[2 * chips[j][0] + chips[j][1], pl.ds(core * half + s, m)]

        sent = []
        for k, (a, q, s, m, j) in enumerate(entries):
            if j < 2:
                half = ins[a].shape[0] // 2
                cp = _remote(ins[a].at[pl.ds(c * half + s, m)], outs[a].at[me, pl.ds(c * half + s, m)],
                             send_sems.at[k], recv_sems.at[k], (*chips[j], c))
                cp.start()
                sent.append(cp)
        for k, (a, q, s, m, j) in enumerate(entries):
            if j < 2:
                blk = landed(a, s, m, j, c)
                _remote(blk, blk, send_sems.at[k], recv_sems.at[k], (*chips[j], c)).wait_recv()
                first = q < (len(halves[a]) + 1) // 2
                if (j == 0) == first:
                    on = slot[(a, q, 2)]
                    rl = _remote(blk, blk, send_sems.at[on], recv_sems.at[on], (*chips[1 - j], c))
                    rl.start()
                    sent.append(rl)
        for k, (a, q, s, m, j) in enumerate(entries):
            if j == 2:
                blk = landed(a, s, m, j, c)
                _remote(blk, blk, send_sems.at[k], recv_sems.at[k], (*chips[j], c)).wait_recv()
        for cp in sent:
            cp.wait_send()

    gathered = _pcall(
        body, name="allgather_weights", in_specs=[ANY] * n, out_specs=[ANY] * n,
        out_shape=[jax.ShapeDtypeStruct((4,) + p.shape, p.dtype) for p in pieces],
        scratch_shapes=[pltpu.SemaphoreType.DMA((n_ici,)), pltpu.SemaphoreType.DMA((n_ici,))],
        compiler_params=pltpu.CompilerParams(has_side_effects=True),
    )(*pieces)

    def share(*refs):
        outs = refs[n:2 * n]
        send_sems, recv_sems = refs[2 * n:]
        x, y, c = _place()
        sibling = (x, y, 1 - c)
        chips = _other_chips(x, y)

        def landed(a, s, m, j, core):
            half = outs[a].shape[1] // 2
            return outs[a].at[2 * chips[j][0] + chips[j][1], pl.ds(core * half + s, m)]

        work = []
        for k, (a, q, s, m, j) in enumerate(entries):
            mine = landed(a, s, m, j, c)
            cp = _remote(mine, mine, send_sems.at[k], recv_sems.at[k], sibling)
            cp.start()
            work.append(cp)
        for k, (a, q, s, m, j) in enumerate(entries):
            work[k].wait_send()
            theirs = landed(a, s, m, j, 1 - c)
            _remote(theirs, theirs, send_sems.at[k], recv_sems.at[k], sibling).wait_recv()

    shared = _pcall(
        share, name="allgather_sibling_share", in_specs=[ANY] * n, out_specs=[ANY] * n,
        out_shape=[jax.ShapeDtypeStruct(g.shape, g.dtype) for g in gathered],
        input_output_aliases={a: a for a in range(n)},
        scratch_shapes=[pltpu.SemaphoreType.DMA((n_ici,)), pltpu.SemaphoreType.DMA((n_ici,))],
        compiler_params=pltpu.CompilerParams(has_side_effects=True),
    )(*gathered)
    x, y, _ = _place()
    return [lax.dynamic_update_slice(g, p[None], (2 * x + y, 0, 0)) for g, p in zip(shared, pieces)]


def _sibling_exchange(grads):
    n = len(grads)
    chunks = [_row_chunks(g.shape[1] // 2, g.shape[2] * g.dtype.itemsize) for g in grads]
    n_sem = 4 * sum(len(ch) for ch in chunks)

    def body(*refs):
        ins, gots = refs[:n], refs[n:2 * n]
        send_sems, recv_sems = refs[2 * n:]
        x, y, c = _place()
        sibling = (x, y, 1 - c)
        work = []
        for a in range(n):
            half = ins[a].shape[1] // 2
            for piece in range(4):
                for s, m in chunks[a]:
                    k = len(work)
                    cp = _remote(ins[a].at[piece, pl.ds((1 - c) * half + s, m)], gots[a].at[piece, pl.ds(s, m)],
                                 send_sems.at[k], recv_sems.at[k], sibling)
                    cp.start()
                    work.append(cp)
        for cp in work:
            cp.wait()

    return _pcall(
        body, name="grad_sibling_exchange", in_specs=[ANY] * n, out_specs=[ANY] * n,
        out_shape=[jax.ShapeDtypeStruct((4, g.shape[1] // 2, g.shape[2]), g.dtype) for g in grads],
        scratch_shapes=[pltpu.SemaphoreType.DMA((n_sem,)), pltpu.SemaphoreType.DMA((n_sem,))],
        compiler_params=pltpu.CompilerParams(has_side_effects=True),
    )(*grads)


def _sibling_gather(fulls):
    n = len(fulls)
    chunks = [_row_chunks(f.shape[0] // 2, f.shape[1] * f.dtype.itemsize) for f in fulls]
    n_sem = sum(len(ch) for ch in chunks)

    def body(*refs):
        outs = refs[n:2 * n]
        send_sems, recv_sems = refs[2 * n:]
        x, y, c = _place()
        sibling = (x, y, 1 - c)
        work = []
        for a in range(n):
            h = outs[a].shape[0] // 2
            for s, m in chunks[a]:
                k = len(work)
                mine = outs[a].at[pl.ds(c * h + s, m)]
                cp = _remote(mine, mine, send_sems.at[k], recv_sems.at[k], sibling)
                cp.start()
                work.append((a, s, m, cp))
        for k, (a, s, m, cp) in enumerate(work):
            h = outs[a].shape[0] // 2
            cp.wait_send()
            theirs = outs[a].at[pl.ds((1 - c) * h + s, m)]
            _remote(theirs, theirs, send_sems.at[k], recv_sems.at[k], sibling).wait_recv()

    return _pcall(
        body, name="grad_sibling_gather", in_specs=[ANY] * n, out_specs=[ANY] * n,
        out_shape=[jax.ShapeDtypeStruct(f.shape, f.dtype) for f in fulls],
        input_output_aliases={a: a for a in range(n)},
        scratch_shapes=[pltpu.SemaphoreType.DMA((n_sem,)), pltpu.SemaphoreType.DMA((n_sem,))],
        compiler_params=pltpu.CompilerParams(has_side_effects=True),
    )(*fulls)


def _pair_sum(grad, got, place, name):
    _, rows, cols = grad.shape
    half = rows // 2
    tr = _row_tile(half, cols, 16)

    def body(p_ref, a_ref, b_ref, o_ref):
        o_ref[...] = (a_ref[...].astype(F32) + b_ref[...].astype(F32)).astype(BF16)

    return _pcall(
        body, name=name,
        grid_spec=pltpu.PrefetchScalarGridSpec(
            num_scalar_prefetch=1, grid=(4, half // tr),
            in_specs=[pl.BlockSpec((None, tr, cols), lambda k, i, p: (k, p[1] * (half // tr) + i, 0)),
                      pl.BlockSpec((None, tr, cols), lambda k, i, p: (k, i, 0))],
            out_specs=pl.BlockSpec((None, tr, cols), lambda k, i, p: (k, i, 0))),
        out_shape=jax.ShapeDtypeStruct((4, half, cols), BF16),
        compiler_params=_params("parallel", "parallel"),
    )(place, grad, got)


def _chip_sum(sums, got, place, name):
    _, h, cols = sums.shape
    tr = _row_tile(h, cols, 16)

    def body(p_ref, own_ref, g0, g1, g2, o_ref):
        o_ref[...] = ((own_ref[...].astype(F32) + g0[...].astype(F32)) + g1[...].astype(F32)) + g2[...].astype(F32)

    gspec = lambda j: pl.BlockSpec((None, tr, cols), lambda i, p: (j, i, 0))
    return _pcall(
        body, name=name,
        grid_spec=pltpu.PrefetchScalarGridSpec(
            num_scalar_prefetch=1, grid=(h // tr,),
            in_specs=[pl.BlockSpec((None, tr, cols), lambda i, p: (p[0], i, 0)), gspec(0), gspec(1), gspec(2)],
            out_specs=pl.BlockSpec((tr, cols), lambda i, p: (p[1] * (h // tr) + i, 0))),
        out_shape=jax.ShapeDtypeStruct((2 * h, cols), F32),
        compiler_params=_params("parallel"),
    )(place, sums, got, got, got)


def _allgather8(buf, name):
    rows = buf.shape[0]

    def body(in_ref, out_ref, send_sems, recv_sems):
        x, y, c = _place()
        me = 4 * x + 2 * y + c
        out_ref[me] = in_ref[...]
        work = []
        for rel in range(1, 8):
            fx, fy, fc = (rel >> 2) & 1, (rel >> 1) & 1, rel & 1
            to = (x ^ fx, y ^ fy, c ^ fc)
            cp = _remote(in_ref, out_ref.at[me], send_sems.at[rel - 1], recv_sems.at[rel - 1], to)
            cp.start()
            work.append((cp, 4 * to[0] + 2 * to[1] + to[2]))
        for rel, (cp, frm) in enumerate(work):
            cp.wait_send()
            blk = out_ref.at[frm]
            _remote(blk, blk, send_sems.at[rel], recv_sems.at[rel], (x, y, c)).wait_recv()

    return _pcall(
        body, name=name, in_specs=[pl.BlockSpec(memory_space=pltpu.VMEM)],
        out_specs=pl.BlockSpec(memory_space=pltpu.VMEM),
        out_shape=jax.ShapeDtypeStruct((8, rows, LANE), F32),
        scratch_shapes=[pltpu.SemaphoreType.DMA((7,)), pltpu.SemaphoreType.DMA((7,))],
        compiler_params=pltpu.CompilerParams(has_side_effects=True),
    )(buf)


def _pack_rows(arrs):
    parts = []
    for a in arrs:
        f = a.reshape(-1).astype(F32)
        parts.append(jnp.pad(f, (0, (-f.shape[0]) % LANE)))
    flat = jnp.concatenate(parts)
    rows = -(-flat.shape[0] // LANE)
    rows8 = -(-rows // 8) * 8
    return jnp.pad(flat, (0, rows8 * LANE - flat.shape[0])).reshape(rows8, LANE)


def _unpack_rows(buf, shapes):
    flat = buf.reshape(-1)
    outs, off = [], 0
    for s in shapes:
        n = int(np.prod(s))
        outs.append(flat[off:off + n].reshape(s))
        off += -(-n // LANE) * LANE
    return outs


def _local_grads(x, p, target, wseg, w_br16, w_out16, w_ple16, b_gate, conv_w, conv_b, dt_bias, a_log, d_skip,
                 ssm_norm_w, ln_g, ln_b, rel_bias, finish_dx):
    nb, seq, _ = x.shape
    bmaps = jnp.asarray(_bucket_maps())
    bias = _bias_tables(rel_bias, bmaps)
    bgate8 = jnp.pad(b_gate, ((0, 5), (0, 0)))
    dils = [d for _, d in PATTERNS]

    x16 = x.astype(BF16)
    p16 = p.astype(BF16)
    x16p = [_permute(x16, d) for d in dils]
    qkv = [_proj(x16p[g], [wseg["qkv%d" % g]], BF16, "proj_qkv%d" % g, True)[0].reshape(
        nb, dils[g], seq // dils[g], -1) for g in range(3)]
    nat = {}
    for gi, (group, tm) in enumerate(NAT_GROUPS):
        outs = _proj(x16, [wseg[s] for s in group], F32, "proj_nat%d" % gi, True, tm)
        nat.update(zip(group, outs))
    att = [_attn_fwd(qkv[g], bias[g * GROUP_HEADS:(g + 1) * GROUP_HEADS], dils[g], "attn_fwd%d" % g) for g in range(3)]
    natural = lambda t, g: _unpermute(t.reshape(nb, seq, t.shape[-1]), dils[g])
    oa, o_att, lse = _combine_fwd(att[0][0], att[0][1], natural(att[1][0], 1), natural(att[1][1], 1),
                                  natural(att[2][0], 2), natural(att[2][1], 2), nat["gatt"])

    conv_wg, conv_bg = _xbc_group_order(conv_w), _xbc_group_order(conv_b)
    act = _conv_fwd(nat["xbc"], conv_wg, conv_bg, "conv_fwd")
    dt_sp, dt_sg = _softplus_sig(nat["dt"], jnp.pad(dt_bias, ((0, 0), (0, LANE - SSM_HEADS))))
    dtg, sgg = _group_lanes(dt_sp), _group_lanes(dt_sg)
    alog_g, dskip_g = _group_lanes(a_log), _group_lanes(d_skip)
    y_ssm, y_all, sprev = _ssd_fwd(act, dtg, nat["z"], alog_g, dskip_g, ssm_norm_w)

    w_bra, w_brb = w_br16[:ATT_OUT], w_br16[ATT_OUT:]
    y_a, = _proj(oa, [w_bra], F32, "proj_ya")
    y_b, = _proj(y_ssm, [w_brb], F32, "proj_yb")
    merged = _merge_fwd(y_a, y_b, nat["gm"], bgate8)
    mix, = _proj(merged, [w_out16], F32, "proj_mix")
    pw, = _proj(p16, [w_ple16], F32, "proj_ple")

    dx, dpre16, dpw16, dgp16, ln_sums = _ln_loss(x, mix, nat["gp"], pw, target, bgate8, ln_g, ln_b)
    loss_sum = (0.5 / D_MODEL) * jnp.sum(ln_sums[3])
    dmerged = _dx([dpre16], [w_out16], [], "dx_merged")
    dya16, dyb16, dgm16, mg_sums = _merge_bwd(dmerged, y_a, y_b, nat["gm"], bgate8)
    doa = _dx([dya16], [w_bra], [], "dx_oa")
    dys = _dx([dyb16], [w_brb], [], "dx_yssm")
    g_w_out, = _dw(merged, [dpre16], BF16, "dw_out")
    g_w_br = jnp.concatenate([_dw(oa, [dya16], BF16, "dw_bra")[0], _dw(y_ssm, [dyb16], BF16, "dw_brb")[0]], axis=0)
    g_w_ple, = _dw(p16, [dpw16], BF16, "dw_ple")

    do_att, do16, stats, dgatt16 = _combine_bwd(doa, nat["gatt"], o_att, lse)
    dseg = {"gatt": dgatt16, "gm": dgm16, "gp": dgp16}
    dbias = []
    for g in range(3):
        own_order = lambda t: _permute(t, dils[g]).reshape(nb, dils[g], seq // dils[g], t.shape[-1])
        cotangent = (do_att, o_att, lse) if g == 0 else (own_order(do16), own_order(stats))
        dqkv, db = _attn_bwd(qkv[g], bias[g * GROUP_HEADS:(g + 1) * GROUP_HEADS], cotangent, dils[g],
                             "attn_bwd%d" % g)
        dseg["qkv%d" % g] = dqkv.reshape(nb, seq, -1)
        dbias.append(db)
    g_rel = _bias_grad(jnp.concatenate(dbias, axis=0), bmaps)[:, 0, :NUM_BUCKETS].T

    dact, ddtg, dz, ssd_small, g_normw = _ssd_bwd(
        act, dtg, sgg, nat["z"], y_all, dys, sprev, alog_g, dskip_g, ssm_norm_w)
    dseg["z"] = dz
    dseg["dt"] = jnp.pad(_ungroup_lanes(ddtg), ((0, 0), (0, 0), (0, LANE - SSM_HEADS)))
    dpre, conv_sums = _conv_bwd_pre(dact, nat["xbc"], conv_wg, conv_bg, "conv_bwd")
    dseg["xbc"] = _conv_bwd_x(dpre, conv_wg, "conv_bwd_x")
    csum = _xbc_reference_order(conv_sums)

    dx_perm = [_unpermute(_dx([dseg["qkv%d" % g]], [wseg["qkv%d" % g]], [], "dx_qkv%d" % g, True), dils[g])
               for g in (1, 2)]
    dwseg = {"qkv%d" % g: _dw(x16p[g], [dseg["qkv%d" % g]], BF16, "dw_qkv%d" % g, True)[0] for g in range(3)}
    for gi, group in enumerate(DW_GROUPS):
        dwseg.update(zip(group, _dw(x16, [dseg[s] for s in group], BF16, "dw_nat%d" % gi, True)))
    names = ["qkv0"] + [s for group, _ in NAT_GROUPS for s in group]
    dx = finish_dx([dseg[s] for s in names], [wseg[s] for s in names], [dx] + dx_perm, dwseg, g_w_br, g_w_out, g_w_ple)

    small = dict(
        b_gate=jnp.stack([mg_sums[0], mg_sums[1], ln_sums[2]]),
        conv_w=csum[0:4], conv_b=csum[4:5],
        dt_bias=_ungroup_lanes(ssd_small[:, 2:3, :]), a_log=_ungroup_lanes(ssd_small[:, 0:1, :]),
        d_skip=_ungroup_lanes(ssd_small[:, 1:2, :]), ssm_norm_w=g_normw,
        ln_g=ln_sums[0:1], ln_b=ln_sums[1:2], rel_bias=g_rel)
    return loss_sum, dx, small


DX_TM = 256
SMALL_ORDER = ("b_gate", "conv_w", "conv_b", "dt_bias", "a_log", "d_skip", "ssm_norm_w", "ln_g", "ln_b", "rel_bias")
SMALL_FULL_SHAPES = dict(b_gate=(3, 1024), conv_w=(4, 3072), conv_b=(1, 3072), dt_bias=(1, 32), a_log=(1, 32),
                         d_skip=(1, 32), ssm_norm_w=(1, 2048), ln_g=(1, 1024), ln_b=(1, 1024), rel_bias=(32, 36))


def kernel(x, p, w_in, b_gate, conv_w, conv_b, dt_bias, a_log, d_skip, ssm_norm_w, w_branch, w_out, w_ple, ln_g, ln_b, rel_bias, loss_target, m_w_in, m_b_gate, m_conv_w, m_conv_b, m_dt_bias, m_a_log, m_d_skip, m_ssm_norm_w, m_w_branch, m_w_out, m_w_ple, m_ln_g, m_ln_b, m_rel_bias, v_w_in, v_b_gate, v_conv_w, v_conv_b, v_dt_bias, v_a_log, v_d_skip, v_ssm_norm_w, v_w_branch, v_w_out, v_w_ple, v_ln_g, v_ln_b, v_rel_bias):
    cx, cy, cc = _place()
    chip = 2 * cx + cy
    dev = 4 * cx + 2 * cy + cc

    w_in_t = jnp.transpose(w_in[0])
    win16 = _shard_to_window(w_in_t, chip)
    g_win, g_br, g_out, g_ple = _allgather_pieces(
        [win16, w_branch[0].astype(BF16), w_out[0].astype(BF16), w_ple[0].astype(BF16)])
    wseg = _assemble(g_win)
    w_br16 = g_br.reshape(4 * 704, D_MODEL)
    w_out16 = g_out.reshape(D_MODEL, D_MODEL)
    w_ple16 = jnp.transpose(g_ple, (1, 0, 2)).reshape(PLE_DIM, D_MODEL)
    shards = _allgather8(_pack_rows([b_gate[0], conv_w[0]]), "allgather_small_params")
    per_chip = [_unpack_rows(shards[2 * k], [(3, 256), (4, 768)]) for k in range(4)]
    b_gate_full = jnp.concatenate([pc[0] for pc in per_chip], axis=1)
    conv_w_full = jnp.concatenate([pc[1] for pc in per_chip], axis=1)

    place = jnp.stack([chip, cc]).astype(jnp.int32)
    reduced = []

    def finish_dx(dhs, ws, accs, dwseg, d_br, d_out, d_ple):
        grads = [_pack(dwseg), d_br.reshape(4, 704, D_MODEL), d_out.reshape(4, 256, D_MODEL),
                 jnp.transpose(d_ple.reshape(PLE_DIM, 4, 256), (1, 0, 2))]
        got = _sibling_exchange(grads)
        chip_sums = [_pair_sum(g, t, place, "grad_pair_sum_%d" % i) for i, (g, t) in enumerate(zip(grads, got))]
        dx, others = _dx(dhs, ws, accs, "dx_w_in_and_grad_chip_scatter", True, DX_TM, chip_sums)
        fulls = [_chip_sum(s, t, place, "grad_chip_sum_%d" % i) for i, (s, t) in enumerate(zip(chip_sums, others))]
        reduced.extend(_sibling_gather(fulls))
        return dx

    loss_sum, grad_x, small = _local_grads(
        x, p[0], loss_target, wseg, w_br16, w_out16, w_ple16, b_gate_full, conv_w_full, conv_b, dt_bias, a_log,
        d_skip, ssm_norm_w, ln_g, ln_b, rel_bias, finish_dx)
    loss = lax.psum(loss_sum, ("x", "y", "c"))
    big = reduced
    g_w_in = _window_to_shard(big[0], chip)
    g_w_branch, g_w_out, g_w_ple = big[1], big[2], big[3]
    parts = _allgather8(_pack_rows([small[n] for n in SMALL_ORDER]), "allgather_small_grads")
    small_sum = _sum_rows([parts[i] for i in range(8)], F32, "small_grad_sum")
    sg = dict(zip(SMALL_ORDER, _unpack_rows(small_sum, [SMALL_FULL_SHAPES[n] for n in SMALL_ORDER])))
    sg["b_gate"] = lax.dynamic_slice_in_dim(sg["b_gate"], chip * 256, 256, axis=1)
    sg["conv_w"] = lax.dynamic_slice_in_dim(sg["conv_w"], chip * 768, 768, axis=1)
    del dev

    upd = {}
    upd["w_in"] = [jnp.transpose(t) for t in _adamw(w_in_t, g_w_in, jnp.transpose(m_w_in[0]),
                                                      jnp.transpose(v_w_in[0]), "adamw_w_in")]
    upd["w_branch"] = _adamw(w_branch[0], g_w_branch, m_w_branch[0], v_w_branch[0], "adamw_w_branch")
    upd["w_out"] = _adamw(w_out[0], g_w_out, m_w_out[0], v_w_out[0], "adamw_w_out")
    upd["w_ple"] = _adamw(w_ple[0], g_w_ple, m_w_ple[0], v_w_ple[0], "adamw_w_ple")
    small_w = dict(b_gate=b_gate, conv_w=conv_w, conv_b=conv_b, dt_bias=dt_bias, a_log=a_log, d_skip=d_skip,
                   ssm_norm_w=ssm_norm_w, ln_g=ln_g, ln_b=ln_b, rel_bias=rel_bias)
    small_m = dict(b_gate=m_b_gate, conv_w=m_conv_w, conv_b=m_conv_b, dt_bias=m_dt_bias, a_log=m_a_log,
                   d_skip=m_d_skip, ssm_norm_w=m_ssm_norm_w, ln_g=m_ln_g, ln_b=m_ln_b, rel_bias=m_rel_bias)
    small_v = dict(b_gate=v_b_gate, conv_w=v_conv_w, conv_b=v_conv_b, dt_bias=v_dt_bias, a_log=v_a_log,
                   d_skip=v_d_skip, ssm_norm_w=v_ssm_norm_w, ln_g=v_ln_g, ln_b=v_ln_b, rel_bias=v_rel_bias)
    shapes = [small_w[n].shape for n in SMALL_ORDER]
    s_delta, s_m, s_v = _adamw(_pack_rows([small_w[n] for n in SMALL_ORDER]), _pack_rows([sg[n] for n in SMALL_ORDER]),
                               _pack_rows([small_m[n] for n in SMALL_ORDER]), _pack_rows([small_v[n] for n in SMALL_ORDER]),
                               "adamw_small")
    for i, n in enumerate(SMALL_ORDER):
        upd[n] = tuple(_unpack_rows(t, shapes)[i] for t in (s_delta, s_m, s_v))
        sg[n] = sg[n].reshape(small_w[n].shape)

    order = ("w_in", "b_gate", "conv_w", "conv_b", "dt_bias", "a_log", "d_skip", "ssm_norm_w", "w_branch", "w_out",
             "w_ple", "ln_g", "ln_b", "rel_bias")
    grads = dict(sg, w_in=jnp.transpose(g_w_in)[None],w_branch=g_w_branch[None], w_out=g_w_out[None], w_ple=g_w_ple[None])
    lead = lambda n, t: t[None] if n in ("w_in", "w_branch", "w_out", "w_ple") else t
    return (loss, grad_x, *[grads[n] for n in order], *[lead(n, upd[n][0]) for n in order],
            *[lead(n, upd[n][1]) for n in order], *[lead(n, upd[n][2]) for n in order])
```

```python
import functools
import math

import numpy as np
import jax
import jax.numpy as jnp
from jax import lax
from jax.experimental import pallas as pl
from jax.experimental.pallas import tpu as pltpu

F32, BF16 = jnp.float32, jnp.bfloat16

D_MODEL = 1024
HEAD_DIM = 64
GROUP_HEADS = 12
ATT_OUT = GROUP_HEADS * HEAD_DIM
PATTERNS = ((128, 1), (512, 4), (2048, 16))
BAND = 128
NUM_BUCKETS = 32
MAX_DISTANCE = 2048
D_INNER = 2048
SSM_HEADS = 32
SSM_GROUPS = 4
GROUP_SSM_HEADS = SSM_HEADS // SSM_GROUPS
D_STATE = 128
CHUNK = 128
PLE_DIM = 256
ALPHA = 2.0 ** 0.25
LN_EPS = 1e-5
RMS_EPS = 1e-5
ADAM_LR, ADAM_B1, ADAM_B2, ADAM_EPS, ADAM_WD, ADAM_STEP = 0.001, 0.9, 0.999, 1e-08, 0.01, 10
NEG = -1e30

QKV_W = 3 * ATT_OUT
IN_COLS = 15904
SHARD_COLS = IN_COLS // 4
DT_COL = 12800
ROW_TILE = 16
WIN_ROWS = 4000


def _win_offset(k):
    return (k * SHARD_COLS) % ROW_TILE


def _win_start(k):
    return k * SHARD_COLS - _win_offset(k)

VMEM_LIMIT_BYTES = 56 * 1024 * 1024
LANE = 128
MESH = pl.DeviceIdType.MESH
NT = (((1,), (1,)), ((), ()))
TN = (((0,), (0,)), ((), ()))


def _pcall(body, **kw):
    return pl.pallas_call(body, **kw)


def _params(*sem):
    return pltpu.CompilerParams(dimension_semantics=sem, vmem_limit_bytes=VMEM_LIMIT_BYTES)


def _sigmoid(v):
    return jax.nn.sigmoid(v)


MM_TM = 512


def _permute(t, d):
    nb, seq, ch = t.shape
    return t if d == 1 else t.reshape(nb, seq // d, d, ch).transpose(0, 2, 1, 3).reshape(nb, seq, ch)


def _unpermute(t, d):
    nb, seq, ch = t.shape
    return t if d == 1 else t.reshape(nb, d, seq // d, ch).transpose(0, 2, 1, 3).reshape(nb, seq, ch)


def _tok_spec(tm, width):
    return pl.BlockSpec((None, tm, width), lambda b, i: (b, i, 0))


def _whole(arr, single_buffer=False):
    mode = dict(pipeline_mode=pl.Buffered(1)) if single_buffer else {}
    return pl.BlockSpec(arr.shape, lambda b, i: (0,) * arr.ndim, **mode)


def _proj(a3, ws, out_dtype, name, w_rows_are_outputs=False, tm=MM_TM):
    nb, seq, kdim = a3.shape
    nw = len(ws)
    widths = [w.shape[0] if w_rows_are_outputs else w.shape[1] for w in ws]

    def body(*refs):
        a = refs[0][...].astype(BF16)
        for w_ref, o_ref in zip(refs[1:1 + nw], refs[1 + nw:]):
            if w_rows_are_outputs:
                v = lax.dot_general(a, w_ref[...], NT, preferred_element_type=F32)
            else:
                v = jnp.dot(a, w_ref[...], preferred_element_type=F32)
            o_ref[...] = v.astype(out_dtype)

    return _pcall(
        body, name=name, grid=(nb, seq // tm),
        in_specs=[_tok_spec(tm, kdim)] + [_whole(w) for w in ws],
        out_specs=[_tok_spec(tm, n) for n in widths],
        out_shape=[jax.ShapeDtypeStruct((nb, seq, n), out_dtype) for n in widths],
        compiler_params=_params("parallel", "parallel"),
    )(a3, *ws)


def _dx(dhs, ws, accs, name, w_rows_are_outputs=False, tm=MM_TM, scatter=None):
    nb, seq, _ = dhs[0].shape
    nd, nacc = len(dhs), len(accs)
    kout = ws[0].shape[1] if w_rows_are_outputs else ws[0].shape[0]
    sums = scatter or []
    ns = len(sums)
    chunks = [_row_chunks(s.shape[1], s.shape[2] * s.dtype.itemsize) for s in sums]
    n_sem = 3 * sum(len(ch) for ch in chunks)
    grid = (nb, seq // tm)

    def body(*refs):
        n_in = 2 * nd + nacc
        sum_refs, o_ref, got_refs = refs[n_in:n_in + ns], refs[n_in + ns], refs[n_in + ns + 1:n_in + 2 * ns + 1]

        def copies():
            send_sems, recv_sems = refs[-2], refs[-1]
            x, y, c = _place()
            out = []
            for a in range(ns):
                for s, m in chunks[a]:
                    for j, (cx, cy) in enumerate(_other_chips(x, y)):
                        k = len(out)
                        out.append(_remote(sum_refs[a].at[2 * cx + cy, pl.ds(s, m)], got_refs[a].at[j, pl.ds(s, m)],
                                           send_sems.at[k], recv_sems.at[k], (cx, cy, c)))
            return out

        if ns:
            @pl.when((pl.program_id(0) == 0) & (pl.program_id(1) == 0))
            def _():
                for cp in copies():
                    cp.start()

        v = None
        for dh_ref, w_ref in zip(refs[:nd], refs[nd:2 * nd]):
            dh = dh_ref[...].astype(BF16)
            if w_rows_are_outputs:
                t = jnp.dot(dh, w_ref[...], preferred_element_type=F32)
            else:
                t = lax.dot_general(dh, w_ref[...], NT, preferred_element_type=F32)
            v = t if v is None else v + t
        for a_ref in refs[2 * nd:n_in]:
            v = v + a_ref[...]
        o_ref[...] = v

        if ns:
            @pl.when((pl.program_id(0) == grid[0] - 1) & (pl.program_id(1) == grid[1] - 1))
            def _():
                for cp in copies():
                    cp.wait()

    out = _pcall(
        body, name=name, grid=grid,
        in_specs=[_tok_spec(tm, dh.shape[-1]) for dh in dhs] + [_whole(w, bool(ns)) for w in ws]
        + [_tok_spec(tm, kout)] * nacc + [ANY] * ns,
        out_specs=[_tok_spec(tm, kout)] + [ANY] * ns,
        out_shape=[jax.ShapeDtypeStruct((nb, seq, kout), F32)]
        + [jax.ShapeDtypeStruct((3,) + s.shape[1:], s.dtype) for s in sums],
        input_output_aliases={2 * nd: 0} if nacc else {},
        scratch_shapes=[pltpu.SemaphoreType.DMA((n_sem,)), pltpu.SemaphoreType.DMA((n_sem,))] if ns else [],
        compiler_params=pltpu.CompilerParams(
            dimension_semantics=("arbitrary", "arbitrary") if ns else ("parallel", "parallel"),
            vmem_limit_bytes=VMEM_LIMIT_BYTES, has_side_effects=bool(ns)),
    )(*dhs, *ws, *accs, *sums)
    return (out[0], list(out[1:])) if ns else out[0]


def _dw(a3, dhs, out_dtype, name, rows_are_outputs=False):
    nb, seq, kdim = a3.shape
    nd = len(dhs)
    grid = (nb, seq // MM_TM)
    shapes = [(dh.shape[-1], kdim) if rows_are_outputs else (kdim, dh.shape[-1]) for dh in dhs]

    def body(*refs):
        b, i = pl.program_id(0), pl.program_id(1)
        dh_refs, o_refs, acc_refs = refs[1:1 + nd], refs[1 + nd:1 + 2 * nd], refs[1 + 2 * nd:]

        @pl.when((b == 0) & (i == 0))
        def _():
            for acc_ref in acc_refs:
                acc_ref[...] = jnp.zeros_like(acc_ref)

        a = refs[0][...].astype(BF16)
        for dh_ref, acc_ref in zip(dh_refs, acc_refs):
            dh = dh_ref[...].astype(BF16)
            acc_ref[...] += lax.dot_general(*((dh, a) if rows_are_outputs else (a, dh)), TN,
                                            preferred_element_type=F32)

        @pl.when((b == grid[0] - 1) & (i == grid[1] - 1))
        def _():
            for o_ref, acc_ref in zip(o_refs, acc_refs):
                o_ref[...] = acc_ref[...].astype(out_dtype)

    return _pcall(
        body, name=name, grid=grid,
        in_specs=[_tok_spec(MM_TM, kdim)] + [_tok_spec(MM_TM, dh.shape[-1]) for dh in dhs],
        out_specs=[pl.BlockSpec(s, lambda b, i: (0, 0)) for s in shapes],
        out_shape=[jax.ShapeDtypeStruct(s, out_dtype) for s in shapes],
        scratch_shapes=[pltpu.VMEM(s, F32) for s in shapes],
        compiler_params=_params("arbitrary", "arbitrary"),
    )(a3, *dhs)


def _qkv_rows(g):
    return [(part * QKV_W + g * ATT_OUT + hp * LANE, LANE) for hp in range(ATT_OUT // LANE) for part in range(3)]


XBC_START = 3 * QKV_W + ATT_OUT + D_INNER
GROUP_CH = GROUP_SSM_HEADS * HEAD_DIM
XBC_GROUP = GROUP_CH + 2 * D_STATE
CONV_DIM = SSM_GROUPS * XBC_GROUP


def _xbc_ranges():
    out = []
    for g in range(SSM_GROUPS):
        out += [(g * GROUP_CH, GROUP_CH), (D_INNER + g * D_STATE, D_STATE),
                (D_INNER + SSM_GROUPS * D_STATE + g * D_STATE, D_STATE)]
    return out


def _xbc_group_order(t):
    return jnp.concatenate([t[..., s:s + n] for s, n in _xbc_ranges()], axis=-1)


def _xbc_reference_order(t):
    g = lambda off, n: [t[..., k * XBC_GROUP + off:k * XBC_GROUP + off + n] for k in range(SSM_GROUPS)]
    return jnp.concatenate(g(0, GROUP_CH) + g(GROUP_CH, D_STATE) + g(GROUP_CH + D_STATE, D_STATE), axis=-1)


def _segments():
    one = lambda name, start, rows: (name, [(start, rows)], max(rows, LANE))
    return [("qkv%d" % g, _qkv_rows(g), QKV_W) for g in range(3)] + [
        one("gatt", 3 * QKV_W, ATT_OUT), one("z", 3 * QKV_W + ATT_OUT, D_INNER),
        ("xbc", [(XBC_START + s, n) for s, n in _xbc_ranges()], CONV_DIM), one("dt", DT_COL, SSM_HEADS),
        one("gm", DT_COL + SSM_HEADS, 2 * D_MODEL), one("gp", DT_COL + SSM_HEADS + 2 * D_MODEL, D_MODEL)]


LAYOUT_TC = 256
NAT_GROUPS = ((("gatt", "z", "dt", "gp"), 512), (("xbc", "gm"), 256))
DW_GROUPS = (("gatt", "z", "dt", "gp"), ("xbc",), ("gm",))


def _assemble(win):
    segs = _segments()

    def body(win_ref, *outs):
        def pieces(start, rows):
            t, end = start, start + rows
            while t < end:
                k = min(t // SHARD_COLS, 3)
                shard_end = (k + 1) * SHARD_COLS
                if k < 3 and shard_end % ROW_TILE and t == shard_end - shard_end % ROW_TILE:
                    lo = t - _win_start(k)
                    yield win_ref[k, lo:lo + ROW_TILE, :] + win_ref[k + 1, 0:ROW_TILE, :]
                    t += ROW_TILE
                    continue
                upto = min(end, shard_end - shard_end % ROW_TILE if k < 3 else end)
                yield win_ref[k, t - _win_start(k):upto - _win_start(k), :]
                t = upto

        for (_, ranges, total), o_ref in zip(segs, outs):
            off = 0
            for start, rows in ranges:
                for part in pieces(start, rows):
                    o_ref[off:off + part.shape[0], :] = part
                    off += part.shape[0]
            if off < total:
                o_ref[off:total, :] = jnp.zeros((total - off, o_ref.shape[1]), BF16)

    outs = _pcall(
        body, name="assemble_w_in", grid=(D_MODEL // LAYOUT_TC,),
        in_specs=[pl.BlockSpec((4, WIN_ROWS, LAYOUT_TC), lambda i: (0, 0, i))],
        out_specs=[pl.BlockSpec((total, LAYOUT_TC), lambda i: (0, i)) for _, _, total in segs],
        out_shape=[jax.ShapeDtypeStruct((total, D_MODEL), BF16) for _, _, total in segs],
        compiler_params=_params("parallel"),
    )(win)
    return {name: o for (name, _, _), o in zip(segs, outs)}


def _pack(dsegs):
    segs = _segments()

    def body(*refs):
        ins, o_ref = refs[:-1], refs[-1]
        tail = IN_COLS - _win_start(3)
        o_ref[3, tail:, :] = jnp.zeros((WIN_ROWS - tail, o_ref.shape[2]), BF16)
        for (_, ranges, _), s_ref in zip(segs, ins):
            off = 0
            for start, rows in ranges:
                for k in range(4):
                    lo = _win_start(k)
                    a, b = max(start, lo), min(start + rows, lo + WIN_ROWS)
                    if a < b:
                        o_ref[k, a - lo:b - lo, :] = s_ref[off + a - start:off + b - start, :]
                off += rows

    return _pcall(
        body, name="pack_dw_in", grid=(D_MODEL // LAYOUT_TC,),
        in_specs=[pl.BlockSpec((total, LAYOUT_TC), lambda i: (0, i)) for _, _, total in segs],
        out_specs=pl.BlockSpec((4, WIN_ROWS, LAYOUT_TC), lambda i: (0, 0, i)),
        out_shape=jax.ShapeDtypeStruct((4, WIN_ROWS, D_MODEL), BF16),
        compiler_params=_params("parallel"),
    )(*[dsegs[name] for name, _, _ in segs])


def _shard_to_window(shard_t, k):
    def at(off):
        return lambda w: jnp.pad(w.astype(BF16), ((off, WIN_ROWS - SHARD_COLS - off), (0, 0)))

    return lax.cond(k % 2 == 1, at(_win_offset(1)), at(_win_offset(0)), shard_t)


def _window_to_shard(win, k):
    return lax.dynamic_slice(win, ((k % 2) * _win_offset(1), 0), (SHARD_COLS, D_MODEL))


def _bucket_maps():
    qi = np.arange(BAND)[:, None]
    kj = np.arange(2 * BAND)[None, :]
    delta = qi + BAND - kj
    maps = []
    for window, dil in PATTERNS:
        valid = (delta >= 0) & (delta <= window // dil)
        dist = np.maximum(delta, 0) * dil
        max_exact = NUM_BUCKETS // 2
        d_f = np.maximum(dist, 1).astype(np.float32)
        large = max_exact + (np.log(d_f / np.float32(max_exact)) / np.float32(math.log(MAX_DISTANCE / max_exact))
                             * np.float32(NUM_BUCKETS - max_exact)).astype(np.int32)
        large = np.minimum(large, NUM_BUCKETS - 1)
        bucket = np.where(dist < max_exact, dist, large)
        maps.append(np.where(valid, bucket, -1).astype(np.int32))
    return np.stack(maps)


def _bias_tables(rel_bias, bmaps):
    def body(rb_ref, bm_ref, o_ref):
        h = pl.program_id(0)
        bm = bm_ref[...]
        acc = jnp.full(bm.shape, NEG, F32)
        for b in range(NUM_BUCKETS):
            acc = jnp.where(bm == b, rb_ref[b, h], acc)
        o_ref[...] = acc

    return _pcall(
        body, name="bias_tables", grid=(3 * GROUP_HEADS,),
        in_specs=[pl.BlockSpec(memory_space=pltpu.SMEM),
                  pl.BlockSpec((None, BAND, 2 * BAND), lambda h: (h // GROUP_HEADS, 0, 0))],
        out_specs=pl.BlockSpec((None, BAND, 2 * BAND), lambda h: (h, 0, 0)),
        out_shape=jax.ShapeDtypeStruct((3 * GROUP_HEADS, BAND, 2 * BAND), F32),
        compiler_params=_params("parallel"),
    )(rel_bias, bmaps)


def _bias_grad(dbias, bmaps):
    def body(db_ref, bm_ref, o_ref):
        bm = bm_ref[...]
        db = db_ref[...]
        lane = lax.broadcasted_iota(jnp.int32, (1, LANE), 1)
        vec = jnp.zeros((1, LANE), F32)
        for b in range(NUM_BUCKETS):
            s = jnp.sum(jnp.where(bm == b, db, 0.0), keepdims=True)
            vec = jnp.where(lane == b, s, vec)
        o_ref[...] = vec

    return _pcall(
        body, name="bias_grad", grid=(3 * GROUP_HEADS,),
        in_specs=[pl.BlockSpec((None, BAND, 2 * BAND), lambda h: (h, 0, 0)),
                  pl.BlockSpec((None, BAND, 2 * BAND), lambda h: (h // GROUP_HEADS, 0, 0))],
        out_specs=pl.BlockSpec((None, 1, LANE), lambda h: (h, 0, 0)),
        out_shape=jax.ShapeDtypeStruct((3 * GROUP_HEADS, 1, LANE), F32),
        compiler_params=_params("parallel"),
    )(dbias, bmaps)


def _rows(n):
    if isinstance(n, int):
        return pl.ds(n * BAND, BAND)
    return pl.ds(pl.multiple_of(n * BAND, BAND), BAND)


def _for_blocks(blocks, nblk, per, carry):
    carry = blocks([0], carry, False)
    start = 1 + (nblk - 1) % per
    for n in range(1, start):
        carry = blocks([n], carry, True)
    trips = (nblk - start) // per
    if trips > 0:
        carry = lax.fori_loop(
            0, trips, lambda t, c: blocks([start + t * per + u for u in range(per)], c, True), carry)
    return carry


def _pairs_per_step(d):
    return {1: 1, 4: 6, 16: 6}[d]


def _head_cols(i, h, part):
    base = 3 * LANE * i + part * LANE + h * HEAD_DIM
    return slice(base, base + HEAD_DIM)


def _attn_fwd(qkv4, bias, d, name):
    nb, _, sub, _ = qkv4.shape
    nblk = sub // BAND
    scale = HEAD_DIM ** -0.5
    npair = ATT_OUT // LANE
    hps = _pairs_per_step(d)
    compact = d > 1

    def body(qkv_ref, bias_ref, o_ref, l_ref):
        def blocks(ns, carry, with_prev):
            chains = [(bi, i, h) for bi in range(len(ns)) for i in range(hps) for h in range(2)]
            scores = []
            for bi, i, h in chains:
                n = ns[bi]
                q = qkv_ref[_rows(n), _head_cols(i, h, 0)] * scale
                s_c = lax.dot_general(q, qkv_ref[_rows(n), _head_cols(i, h, 1)], NT,
                                      preferred_element_type=F32) + bias_ref[2 * i + h, :, BAND:]
                s_p = None
                if with_prev:
                    s_p = lax.dot_general(q, qkv_ref[_rows(n - 1), _head_cols(i, h, 1)], NT,
                                          preferred_element_type=F32) + bias_ref[2 * i + h, :, :BAND]
                scores.append((s_c, s_p))
            probs = []
            for s_c, s_p in scores:
                m = jnp.max(s_c, -1, keepdims=True)
                if with_prev:
                    m = jnp.maximum(m, jnp.max(s_p, -1, keepdims=True))
                e_c = jnp.exp(s_c - m)
                den = jnp.sum(e_c, -1, keepdims=True)
                e_p = None
                if with_prev:
                    e_p = jnp.exp(s_p - m)
                    den = den + jnp.sum(e_p, -1, keepdims=True)
                    e_p = e_p.astype(BF16)
                probs.append((e_c.astype(BF16), e_p, den, m))
            outs = {}
            for (bi, i, h), (e_c, e_p, den, m) in zip(chains, probs):
                n = ns[bi]
                acc = jnp.dot(e_c, qkv_ref[_rows(n), _head_cols(i, h, 2)], preferred_element_type=F32)
                if with_prev:
                    acc = acc + jnp.dot(e_p, qkv_ref[_rows(n - 1), _head_cols(i, h, 2)], preferred_element_type=F32)
                outs[(bi, i, h)] = (acc / den, jnp.broadcast_to(m + jnp.log(den), (BAND, HEAD_DIM)))
            lane = lax.broadcasted_iota(jnp.int32, (BAND, LANE), 1)
            for bi, n in enumerate(ns):
                per_head = jnp.zeros((BAND, LANE), F32)
                for i in range(hps):
                    o_ref[_rows(n), i * LANE:(i + 1) * LANE] = jnp.concatenate(
                        [outs[(bi, i, 0)][0], outs[(bi, i, 1)][0]], axis=1)
                    if compact:
                        for h in range(2):
                            per_head = jnp.where(lane == 2 * i + h, outs[(bi, i, h)][1][:, :1], per_head)
                    else:
                        l_ref[_rows(n), i * LANE:(i + 1) * LANE] = jnp.concatenate(
                            [outs[(bi, i, 0)][1], outs[(bi, i, 1)][1]], axis=1)
                if compact:
                    l_ref[_rows(n), :] = per_head
            return carry

        _for_blocks(blocks, nblk, 2 if hps == 1 else 1, 0)

    in_specs = [pl.BlockSpec((None, None, sub, 3 * LANE * hps), lambda hp, b, r: (b, r, 0, hp)),
                pl.BlockSpec((2 * hps, BAND, 2 * BAND), lambda hp, b, r: (hp, 0, 0))]
    if compact:
        return _pcall(
            body, name=name, grid=(1, nb, d), in_specs=in_specs,
            out_specs=[pl.BlockSpec((None, None, sub, ATT_OUT), lambda hp, b, r: (b, r, 0, 0)),
                       pl.BlockSpec((None, None, sub, LANE), lambda hp, b, r: (b, r, 0, 0))],
            out_shape=[jax.ShapeDtypeStruct((nb, d, sub, ATT_OUT), F32), jax.ShapeDtypeStruct((nb, d, sub, LANE), F32)],
            compiler_params=_params("parallel", "parallel", "parallel"),
        )(qkv4, bias)
    ospec = pl.BlockSpec((None, sub, hps * LANE), lambda hp, b, r: (b, 0, r * (npair // hps) + hp))
    return _pcall(
        body, name=name, grid=(npair // hps, nb, d), in_specs=in_specs, out_specs=[ospec, ospec],
        out_shape=[jax.ShapeDtypeStruct((nb, sub, d * ATT_OUT), F32)] * 2,
        compiler_params=_params("parallel", "parallel", "parallel"),
    )(qkv4, bias)


STAT_LSE_LANE = 16


def _attn_bwd(qkv4, bias, cotangent, d, name):
    nb, _, sub, _ = qkv4.shape
    nblk = sub // BAND
    scale = HEAD_DIM ** -0.5
    npair = ATT_OUT // LANE
    hps = _pairs_per_step(d)
    compact = d > 1

    def body(qkv_ref, bias_ref, *rest):
        do_ref, dqkv_ref, db_ref = rest[0], rest[-2], rest[-1]
        b, r = pl.program_id(1), pl.program_id(2)

        @pl.when((b == 0) & (r == 0))
        def _():
            db_ref[...] = jnp.zeros_like(db_ref)

        def blocks(ns, carry, with_prev):
            sides = (0, 1) if with_prev else (0,)
            chains = [(bi, i, h, sd) for bi in range(len(ns)) for i in range(hps) for h in range(2) for sd in sides]
            key_rows = lambda bi, sd: _rows(ns[bi] - sd)
            qs = {}
            for bi in range(len(ns)):
                for i in range(hps):
                    for h in range(2):
                        hl = slice(i * LANE + h * HEAD_DIM, i * LANE + (h + 1) * HEAD_DIM)
                        do = do_ref[_rows(ns[bi]), hl]
                        if compact:
                            st_ref, head = rest[1], 2 * i + h
                            ebar = st_ref[_rows(ns[bi]), head:head + 1]
                            lcol = st_ref[_rows(ns[bi]), STAT_LSE_LANE + head:STAT_LSE_LANE + head + 1]
                        else:
                            ebar = jnp.sum(do * rest[1][_rows(ns[bi]), hl], -1, keepdims=True)
                            lcol = rest[2][_rows(ns[bi]), i * LANE + h * HEAD_DIM:i * LANE + h * HEAD_DIM + 1]
                        q_scaled = qkv_ref[_rows(ns[bi]), _head_cols(i, h, 0)] * scale
                        qs[(bi, i, h)] = (q_scaled, do.astype(BF16), ebar, lcol)
            raw = []
            for bi, i, h, sd in chains:
                q, do16, _, _ = qs[(bi, i, h)]
                k = qkv_ref[key_rows(bi, sd), _head_cols(i, h, 1)]
                v = qkv_ref[key_rows(bi, sd), _head_cols(i, h, 2)]
                bias_blk = bias_ref[2 * i + h, :, :BAND] if sd else bias_ref[2 * i + h, :, BAND:]
                s = lax.dot_general(q, k, NT, preferred_element_type=F32) + bias_blk
                dp = lax.dot_general(do16, v, NT, preferred_element_type=F32)
                raw.append((s, dp))
            soft = []
            for (bi, i, h, sd), (s, dp) in zip(chains, raw):
                _, _, ebar, lcol = qs[(bi, i, h)]
                p = jnp.exp(s - lcol)
                ds = p * (dp - ebar)
                if sd:
                    db_ref[2 * i + h, :, :BAND] += ds
                else:
                    db_ref[2 * i + h, :, BAND:] += ds
                soft.append((p.astype(BF16), ds.astype(BF16)))
            grads = {}
            for (bi, i, h, sd), (p16, ds16) in zip(chains, soft):
                q, do16, _, _ = qs[(bi, i, h)]
                k = qkv_ref[key_rows(bi, sd), _head_cols(i, h, 1)]
                grads[(bi, i, h, sd)] = (
                    jnp.dot(ds16, k, preferred_element_type=F32),
                    lax.dot_general(ds16, q, TN, preferred_element_type=F32),
                    lax.dot_general(p16, do16, TN, preferred_element_type=F32))
            both = lambda bi, i, sd, which: jnp.concatenate(
                [grads[(bi, i, 0, sd)][which], grads[(bi, i, 1, sd)][which]], axis=1)
            carry = list(carry) if carry is not None else None
            for bi, n in enumerate(ns):
                for i in range(hps):
                    base = 3 * LANE * i
                    dq = both(bi, i, 0, 0)
                    if with_prev:
                        dq = dq + both(bi, i, 1, 0)
                        dqkv_ref[_rows(n - 1), base + LANE:base + 2 * LANE] = (
                            carry[2 * i] + both(bi, i, 1, 1)).astype(BF16)
                        dqkv_ref[_rows(n - 1), base + 2 * LANE:base + 3 * LANE] = (
                            carry[2 * i + 1] + both(bi, i, 1, 2)).astype(BF16)
                    dqkv_ref[_rows(n), base:base + LANE] = (dq * scale).astype(BF16)
                carry = [t for i in range(hps) for t in (both(bi, i, 0, 1), both(bi, i, 0, 2))]
            return tuple(carry)

        carry = _for_blocks(blocks, nblk, 2 if hps == 1 else 1, None)
        for i in range(hps):
            base = 3 * LANE * i
            dqkv_ref[_rows(nblk - 1), base + LANE:base + 2 * LANE] = carry[2 * i].astype(BF16)
            dqkv_ref[_rows(nblk - 1), base + 2 * LANE:base + 3 * LANE] = carry[2 * i + 1].astype(BF16)

    qspec = pl.BlockSpec((None, None, sub, 3 * LANE * hps), lambda hp, b, r: (b, r, 0, hp))
    bspec = pl.BlockSpec((2 * hps, BAND, 2 * BAND), lambda hp, b, r: (hp, 0, 0))
    if compact:
        cspecs = [pl.BlockSpec((None, None, sub, ATT_OUT), lambda hp, b, r: (b, r, 0, 0)),
                  pl.BlockSpec((None, None, sub, LANE), lambda hp, b, r: (b, r, 0, 0))]
    else:
        cspecs = [pl.BlockSpec((None, sub, hps * LANE), lambda hp, b, r: (b, 0, r * (npair // hps) + hp))] * 3
    return _pcall(
        body, name=name, grid=(npair // hps, nb, d),
        in_specs=[qspec, bspec] + cspecs, out_specs=[qspec, bspec],
        out_shape=[jax.ShapeDtypeStruct(qkv4.shape, BF16),
                   jax.ShapeDtypeStruct((GROUP_HEADS, BAND, 2 * BAND), F32)],
        compiler_params=_params("parallel", "arbitrary", "arbitrary"),
    )(qkv4, bias, *cotangent)


def _head_lanes(first_lane, one_channel):
    c = lax.broadcasted_iota(jnp.int32, (ATT_OUT, LANE), 0)
    lane = lax.broadcasted_iota(jnp.int32, (ATT_OUT, LANE), 1)
    hit = lane == first_lane + c // HEAD_DIM
    if one_channel:
        hit = hit & (c % HEAD_DIM == 0)
    return hit.astype(BF16)


def _exact_dot(v, m01, dims=None):
    parts = _split3(v)
    if dims is None:
        dot = lambda t: jnp.dot(t, m01, preferred_element_type=F32)
    else:
        dot = lambda t: lax.dot_general(t, m01, dims, preferred_element_type=F32)
    return (dot(parts[0]) + dot(parts[1])) + dot(parts[2])


def _combine_fwd(o0, l0, o1, l1, o2, l2, gatt):
    nb, seq, _ = gatt.shape
    tm = 512

    def body(o0_ref, l0_ref, o1_ref, l1_ref, o2_ref, l2_ref, g_ref, oa_ref, oatt_ref, lse_ref):
        spread = _head_lanes(0, False)
        l0v = l0_ref[...]
        l1v = _exact_dot(l1_ref[...], spread, NT)
        l2v = _exact_dot(l2_ref[...], spread, NT)
        m = jnp.maximum(jnp.maximum(l0v, l1v), l2v)
        tot = m + jnp.log(jnp.exp(l0v - m) + jnp.exp(l1v - m) + jnp.exp(l2v - m))
        o = (jnp.exp(l0v - tot) * o0_ref[...] + jnp.exp(l1v - tot) * o1_ref[...]
             + jnp.exp(l2v - tot) * o2_ref[...])
        g = g_ref[...]
        oa_ref[...] = (o * (g * _sigmoid(g))).astype(BF16)
        oatt_ref[...] = o
        lse_ref[...] = tot

    spec = pl.BlockSpec((None, tm, ATT_OUT), lambda b, i: (b, i, 0))
    lspec = pl.BlockSpec((None, tm, LANE), lambda b, i: (b, i, 0))
    return _pcall(
        body, name="attn_combine", grid=(nb, seq // tm),
        in_specs=[spec, spec, spec, lspec, spec, lspec, spec], out_specs=[spec] * 3,
        out_shape=[jax.ShapeDtypeStruct((nb, seq, ATT_OUT), BF16), jax.ShapeDtypeStruct((nb, seq, ATT_OUT), F32),
                   jax.ShapeDtypeStruct((nb, seq, ATT_OUT), F32)],
        compiler_params=_params("parallel", "parallel"),
    )(o0, l0, o1, l1, o2, l2, gatt)


def _combine_bwd(doa, gatt, o_att, lse):
    nb, seq, _ = gatt.shape
    tm = 512

    def body(doa_ref, g_ref, o_ref, l_ref, do_ref, do16_ref, st_ref, dg_ref):
        g = g_ref[...]
        sg = _sigmoid(g)
        do = doa_ref[...] * (g * sg)
        do_ref[...] = do
        do16_ref[...] = do.astype(BF16)
        st_ref[...] = (_exact_dot(do * o_ref[...], _head_lanes(0, False))
                       + _exact_dot(l_ref[...], _head_lanes(STAT_LSE_LANE, True)))
        dg_ref[...] = (doa_ref[...] * o_ref[...] * (sg * (1.0 + g * (1.0 - sg)))).astype(BF16)

    spec = pl.BlockSpec((None, tm, ATT_OUT), lambda b, i: (b, i, 0))
    lspec = pl.BlockSpec((None, tm, LANE), lambda b, i: (b, i, 0))
    return _pcall(
        body, name="attn_combine_bwd", grid=(nb, seq // tm), in_specs=[spec] * 4,
        out_specs=[spec, spec, lspec, spec],
        out_shape=[jax.ShapeDtypeStruct((nb, seq, ATT_OUT), F32), jax.ShapeDtypeStruct((nb, seq, ATT_OUT), BF16),
                   jax.ShapeDtypeStruct((nb, seq, LANE), F32), jax.ShapeDtypeStruct((nb, seq, ATT_OUT), BF16)],
        compiler_params=_params("parallel", "parallel"),
    )(doa, gatt, o_att, lse)


CONV_TM = 512
CONV_TC = 512


def _shift_down(cur, halo, k):
    rolled = pltpu.roll(cur, k, 0)
    hro = pltpu.roll(halo, k, 0)
    row = lax.broadcasted_iota(jnp.int32, hro.shape, 0)
    return jnp.concatenate([jnp.where(row < k, hro, rolled[:8]), rolled[8:]], axis=0)


def _shift_up(cur, halo, k):
    n = cur.shape[0]
    rolled = pltpu.roll(cur, n - k, 0)
    hro = pltpu.roll(halo, 8 - k, 0)
    row = lax.broadcasted_iota(jnp.int32, hro.shape, 0)
    return jnp.concatenate([rolled[:n - 8], jnp.where(row >= 8 - k, hro, rolled[n - 8:])], axis=0)


def _conv_pre(cur, halo, w_ref, b_ref):
    acc = cur * w_ref[3:4, :] + b_ref[...]
    for k in range(1, 4):
        acc = acc + _shift_down(cur, halo, k) * w_ref[3 - k:4 - k, :]
    return acc


def _conv_specs(seq):
    nblk = seq // CONV_TM
    cur = pl.BlockSpec((None, CONV_TM, CONV_TC), lambda cb, b, i: (b, i, cb))
    prev = pl.BlockSpec((None, 8, CONV_TC), lambda cb, b, i: (b, jnp.maximum(i * (CONV_TM // 8) - 1, 0), cb))
    nxt = pl.BlockSpec((None, 8, CONV_TC),
                       lambda cb, b, i: (b, jnp.minimum((i + 1) * (CONV_TM // 8), seq // 8 - 1), cb))
    wspec = pl.BlockSpec((4, CONV_TC), lambda cb, b, i: (0, cb))
    bspec = pl.BlockSpec((1, CONV_TC), lambda cb, b, i: (0, cb))
    return nblk, cur, prev, nxt, wspec, bspec


def _conv_fwd(xin, w4, bias, name):
    nb, seq, ch = xin.shape
    _, cur, prev, _, wspec, bspec = _conv_specs(seq)

    def body(x_ref, h_ref, w_ref, b_ref, o_ref):
        halo = jnp.where(pl.program_id(2) > 0, h_ref[...], 0.0)
        pre = _conv_pre(x_ref[...], halo, w_ref, b_ref)
        o_ref[...] = pre * _sigmoid(pre)

    return _pcall(
        body, name=name, grid=(ch // CONV_TC, nb, seq // CONV_TM),
        in_specs=[cur, prev, wspec, bspec], out_specs=cur,
        out_shape=jax.ShapeDtypeStruct(xin.shape, F32),
        compiler_params=_params("parallel", "parallel", "parallel"),
    )(xin, xin, w4, bias)


def _conv_bwd_pre(dact, xin, w4, bias, name):
    nb, seq, ch = xin.shape
    _, cur, prev, _, wspec, bspec = _conv_specs(seq)

    def body(da_ref, x_ref, h_ref, w_ref, b_ref, dp_ref, s_ref):
        b, i = pl.program_id(1), pl.program_id(2)

        @pl.when((b == 0) & (i == 0))
        def _():
            s_ref[...] = jnp.zeros_like(s_ref)

        halo = jnp.where(i > 0, h_ref[...], 0.0)
        x = x_ref[...]
        pre = _conv_pre(x, halo, w_ref, b_ref)
        sg = _sigmoid(pre)
        dpre = da_ref[...] * (sg * (1.0 + pre * (1.0 - sg)))
        dp_ref[...] = dpre
        s_ref[3:4, :] += jnp.sum(dpre * x, 0, keepdims=True)
        for k in range(1, 4):
            s_ref[3 - k:4 - k, :] += jnp.sum(dpre * _shift_down(x, halo, k), 0, keepdims=True)
        s_ref[4:5, :] += jnp.sum(dpre, 0, keepdims=True)

    return _pcall(
        body, name=name, grid=(ch // CONV_TC, nb, seq // CONV_TM),
        in_specs=[cur, cur, prev, wspec, bspec],
        out_specs=[cur, pl.BlockSpec((8, CONV_TC), lambda cb, b, i: (0, cb))],
        out_shape=[jax.ShapeDtypeStruct(xin.shape, F32), jax.ShapeDtypeStruct((8, ch), F32)],
        compiler_params=_params("parallel", "arbitrary", "arbitrary"),
    )(dact, xin, xin, w4, bias)


def _conv_bwd_x(dpre, w4, name):
    nb, seq, ch = dpre.shape
    nblk, cur, _, nxt, wspec, _ = _conv_specs(seq)

    def body(d_ref, n_ref, w_ref, o_ref):
        halo = jnp.where(pl.program_id(2) < nblk - 1, n_ref[...], 0.0)
        cur_v = d_ref[...]
        acc = cur_v * w_ref[3:4, :]
        for j in range(1, 4):
            acc = acc + _shift_up(cur_v, halo, j) * w_ref[3 - j:4 - j, :]
        o_ref[...] = acc.astype(BF16)

    return _pcall(
        body, name=name, grid=(ch // CONV_TC, nb, seq // CONV_TM),
        in_specs=[cur, nxt, wspec], out_specs=cur,
        out_shape=jax.ShapeDtypeStruct(dpre.shape, BF16),
        compiler_params=_params("parallel", "parallel", "parallel"),
    )(dpre, dpre, w4)


def _softplus_sig(dt_raw, dt_bias_row):
    nb, seq, _ = dt_raw.shape
    tm = 512

    def body(r_ref, b_ref, sp_ref, sg_ref):
        v = r_ref[...] + b_ref[...]
        sp_ref[...] = jnp.maximum(v, 0.0) + jnp.log1p(jnp.exp(-jnp.abs(v)))
        sg_ref[...] = _sigmoid(v)

    spec = pl.BlockSpec((None, tm, LANE), lambda b, i: (b, i, 0))
    return _pcall(
        body, name="dt_softplus", grid=(nb, seq // tm),
        in_specs=[spec, pl.BlockSpec((1, LANE), lambda b, i: (0, 0))], out_specs=[spec, spec],
        out_shape=[jax.ShapeDtypeStruct(dt_raw.shape, F32)] * 2,
        compiler_params=_params("parallel", "parallel"),
    )(dt_raw, dt_bias_row)


def _group_lanes(t):
    pads = [(0, 0)] * (t.ndim - 1) + [(0, LANE - GROUP_SSM_HEADS)]
    return jnp.stack([jnp.pad(t[..., GROUP_SSM_HEADS * g:GROUP_SSM_HEADS * (g + 1)], pads) for g in range(SSM_GROUPS)])


def _ungroup_lanes(t):
    return jnp.concatenate([t[g][..., :GROUP_SSM_HEADS] for g in range(SSM_GROUPS)], axis=-1)


def _decays(dt, al_ref):
    row = lax.broadcasted_iota(jnp.int32, (CHUNK, CHUNK), 0)
    col = lax.broadcasted_iota(jnp.int32, (CHUNK, CHUNK), 1)
    tril = (row >= col).astype(BF16)
    triu = (row <= col).astype(BF16)
    arow = -jnp.exp(al_ref[...])
    hi, mid, lo = _split3(dt * arow)
    down = lambda t: jnp.dot(tril, t, preferred_element_type=F32)
    across = lambda t: lax.dot_general(t, triu, TN, preferred_element_type=F32)
    acs = (down(hi) + down(mid)) + down(lo)
    acs_t = (across(hi) + across(mid)) + across(lo)
    return arow, acs, acs_t, row >= col, triu


def _ssd_specs(nb, seq):
    nc = seq // CHUNK
    hw = GROUP_SSM_HEADS * HEAD_DIM

    def mk(rev):
        cidx = (lambda c: nc - 1 - c) if rev else (lambda c: c)
        wide = pl.BlockSpec((None, CHUNK, hw), lambda g, b, c: (b, cidx(c), g))
        xbc = pl.BlockSpec((None, CHUNK, XBC_GROUP), lambda g, b, c: (b, cidx(c), g))
        lanes = pl.BlockSpec((None, None, CHUNK, LANE), lambda g, b, c: (g, b, cidx(c), 0))
        prev = pl.BlockSpec((None, None, None, D_STATE, hw), lambda g, b, c: (b, cidx(c), g, 0, 0))
        return wide, xbc, lanes, prev

    grow = pl.BlockSpec((None, 1, LANE), lambda g, b, c: (g, 0, 0))
    nwspec = pl.BlockSpec((1, hw), lambda g, b, c: (0, g))
    return nc, hw, mk, grow, nwspec


def _head_expand():
    hw = GROUP_SSM_HEADS * HEAD_DIM
    r = lax.broadcasted_iota(jnp.int32, (LANE, hw), 0)
    c = lax.broadcasted_iota(jnp.int32, (LANE, hw), 1)
    return ((c // HEAD_DIM) == r).astype(BF16)


def _split3(v):
    hi = v.astype(BF16)
    rest = v - hi.astype(F32)
    mid = rest.astype(BF16)
    return hi, mid, (rest - mid.astype(F32)).astype(BF16)


def _to_channels(v, e):
    hi, mid, lo = _split3(v)
    dot = lambda t: jnp.dot(t, e, preferred_element_type=F32)
    return (dot(hi) + dot(mid)) + dot(lo)


def _to_heads(w, e):
    hi, mid, lo = _split3(w)
    dot = lambda t: lax.dot_general(t, e, (((1,), (1,)), ((), ())), preferred_element_type=F32)
    return (dot(hi) + dot(mid)) + dot(lo)


def _row8(v):
    return jnp.broadcast_to(v, (8, v.shape[1]))


def _ssd_chunk_setup(dt, al_ref, ds_ref):
    arow, acs, acs_t, causal, triu = _decays(dt, al_ref)
    e = _head_expand()
    dtx = _to_channels(dt, e)
    acsx = _to_channels(acs, e)
    lastx = acsx[CHUNK - 1:CHUNK, :]
    dskx = _to_channels(_row8(ds_ref[...]), e)[0:1, :]
    return arow, acs, acs_t, causal, triu, e, dtx, acsx, lastx, dskx


def _ssd_fwd(xbc, dtg, z, alog_g, dskip_g, normw):
    nb, seq, _ = xbc.shape
    nc, hw, mk, grow, nwspec = _ssd_specs(nb, seq)
    wide, xbc_spec, lanes, prev = mk(False)
    tn = (((0,), (0,)), ((), ()))

    def body(xbc_ref, dt_ref, z_ref, al_ref, ds_ref, nw_ref, ys_ref, y_ref, sp_ref, st_ref):
        @pl.when(pl.program_id(2) == 0)
        def _():
            st_ref[...] = jnp.zeros_like(st_ref)

        dt = dt_ref[...]
        _, acs, acs_t, causal, _, _, dtx, acsx, lastx, dskx = _ssd_chunk_setup(dt, al_ref, ds_ref)
        bmat = xbc_ref[:, GROUP_CH:GROUP_CH + D_STATE].astype(BF16)
        cmat = xbc_ref[:, GROUP_CH + D_STATE:].astype(BF16)
        cb = lax.dot_general(cmat, bmat, (((1,), (1,)), ((), ())), preferred_element_type=F32)
        x = xbc_ref[:, :GROUP_CH]
        xdt = x * dtx
        xdt16 = xdt.astype(BF16)
        first_head = lax.broadcasted_iota(jnp.int32, (CHUNK, LANE), 1) < HEAD_DIM
        pairs = []
        for hp in range(GROUP_SSM_HEADS // 2):
            xp = xdt16[:, hp * LANE:(hp + 1) * LANE]
            two = []
            for j in (2 * hp, 2 * hp + 1):
                lmat = jnp.exp(jnp.where(causal, acs[:, j:j + 1] - acs_t[j:j + 1, :], -jnp.inf))
                two.append(jnp.dot((cb * lmat).astype(BF16), xp, preferred_element_type=F32))
            pairs.append(jnp.where(first_head, two[0], two[1]))
        yd = jnp.concatenate(pairs, axis=1)
        s_prev = st_ref[...]
        s16 = s_prev.astype(BF16)
        sp_ref[...] = s16
        yo = jnp.dot(cmat, s16, preferred_element_type=F32) * jnp.exp(acsx)
        sts = lax.dot_general(bmat, (xdt * jnp.exp(lastx - acsx)).astype(BF16), tn, preferred_element_type=F32)
        st_ref[...] = s_prev * jnp.exp(lastx) + sts
        y = yd + yo + dskx * x
        zz = z_ref[...]
        u = y * (zz * _sigmoid(zz))
        rn = lax.rsqrt(jnp.mean(u * u, -1, keepdims=True) + RMS_EPS)
        ys_ref[...] = (u * rn * nw_ref[...]).astype(BF16)
        y_ref[...] = y

    return _pcall(
        body, name="ssd_fwd", grid=(SSM_GROUPS, nb, nc),
        in_specs=[xbc_spec, lanes, wide, grow, grow, nwspec],
        out_specs=[wide, wide, prev],
        out_shape=[jax.ShapeDtypeStruct((nb, seq, D_INNER), BF16), jax.ShapeDtypeStruct((nb, seq, D_INNER), F32),
                   jax.ShapeDtypeStruct((nb, nc, SSM_GROUPS, D_STATE, hw), BF16)],
        scratch_shapes=[pltpu.VMEM((D_STATE, hw), F32)],
        compiler_params=_params("parallel", "parallel", "arbitrary"),
    )(xbc, dtg, z, alog_g, dskip_g, normw)


def _ssd_bwd(xbc, dtg, sgg, z, y, dys, sprev, alog_g, dskip_g, normw):
    nb, seq, _ = xbc.shape
    nc, hw, mk, grow, nwspec = _ssd_specs(nb, seq)
    wide, xbc_spec, lanes, prev = mk(True)
    nt = (((1,), (1,)), ((), ()))
    tn = (((0,), (0,)), ((), ()))

    def body(xbc_ref, dt_ref, sg_ref, z_ref, y_ref, dys_ref, sp_ref, al_ref, ds_ref, nw_ref,
             dxbc_ref, ddt_ref, dz_ref, small_ref, dnw_ref, g_ref):
        b, c = pl.program_id(1), pl.program_id(2)

        @pl.when((b == 0) & (c == 0))
        def _():
            small_ref[...] = jnp.zeros_like(small_ref)
            dnw_ref[...] = jnp.zeros_like(dnw_ref)

        @pl.when(c == 0)
        def _():
            g_ref[...] = jnp.zeros_like(g_ref)

        yv, zz, dys_v, nw = y_ref[...], z_ref[...], dys_ref[...], nw_ref[...]
        sz = _sigmoid(zz)
        silu = zz * sz
        u = yv * silu
        rn = lax.rsqrt(jnp.mean(u * u, -1, keepdims=True) + RMS_EPS)
        gn = dys_v * nw
        du = rn * gn - u * (rn * rn * rn) * jnp.mean(u * gn, -1, keepdims=True)
        dnw_ref[...] += jnp.sum(dys_v * u * rn, 0, keepdims=True)
        dy = du * silu
        dz_ref[...] = du * yv * (sz * (1.0 + zz * (1.0 - sz)))

        dt = dt_ref[...]
        arow, acs, acs_t, causal, triu, e, dtx, acsx, lastx, dskx = _ssd_chunk_setup(dt, al_ref, ds_ref)
        dfsx = jnp.exp(acsx)
        dtex = jnp.exp(lastx - acsx)
        bmat = xbc_ref[:, GROUP_CH:GROUP_CH + D_STATE].astype(BF16)
        cmat = xbc_ref[:, GROUP_CH + D_STATE:].astype(BF16)
        cb = lax.dot_general(cmat, bmat, nt, preferred_element_type=F32)
        x = xbc_ref[:, :GROUP_CH]
        xdt = x * dtx
        xdt16 = xdt.astype(BF16)
        xdte = xdt * dtex
        dy16 = dy.astype(BF16)
        dyd = dy * dfsx
        dyd16 = dyd.astype(BF16)
        s16 = sp_ref[...]
        g = g_ref[...]
        g16 = g.astype(BF16)
        cs = jnp.dot(cmat, s16, preferred_element_type=F32)
        dc_off = lax.dot_general(dyd16, s16, nt, preferred_element_type=F32)
        g_here = lax.dot_general(cmat, dyd16, tn, preferred_element_type=F32)
        bg = jnp.dot(bmat, g16, preferred_element_type=F32)
        db_st = lax.dot_general(xdte.astype(BF16), g16, nt, preferred_element_type=F32)
        ddte_w = bg * xdte
        dcd = _to_heads(_row8(jnp.sum(g * s16.astype(F32), 0, keepdims=True)), e)[0:1, :]
        lane = lax.broadcasted_iota(jnp.int32, (CHUNK, LANE), 1)
        first_head = lane < HEAD_DIM
        sub = lax.broadcasted_iota(jnp.int32, (CHUNK, LANE), 0)
        dacs = jnp.zeros((CHUNK, LANE), F32)
        colsums = jnp.zeros((CHUNK, LANE), F32)
        dcb = jnp.zeros((CHUNK, CHUNK), F32)
        pairs = []
        for hp in range(GROUP_SSM_HEADS // 2):
            xp = xdt16[:, hp * LANE:(hp + 1) * LANE]
            dyp = dy16[:, hp * LANE:(hp + 1) * LANE]
            two = []
            for idx, j in enumerate((2 * hp, 2 * hp + 1)):
                lmat = jnp.exp(jnp.where(causal, acs[:, j:j + 1] - acs_t[j:j + 1, :], -jnp.inf))
                mf = cb * lmat
                dy_h = jnp.where(first_head if idx == 0 else jnp.logical_not(first_head), dyp, jnp.zeros_like(dyp))
                dm = lax.dot_general(dy_h, xp, nt, preferred_element_type=F32)
                two.append(lax.dot_general(mf.astype(BF16), dyp, tn, preferred_element_type=F32))
                wmat = dm * mf
                dcb = dcb + dm * lmat
                dacs = jnp.where(lane == j, jnp.sum(wmat, -1, keepdims=True), dacs)
                colsums = jnp.where(sub == j, jnp.sum(wmat, 0, keepdims=True), colsums)
            pairs.append(jnp.where(first_head, two[0], two[1]))
        dxdt = bg * dtex + jnp.concatenate(pairs, axis=1)
        dacs = dacs - colsums.T + _to_heads(dyd * cs - ddte_w, e)
        cd_row = jnp.exp(acs[CHUNK - 1:CHUNK, :])
        tail = _to_heads(_row8(jnp.sum(ddte_w, 0, keepdims=True)), e)[0:1, :] + dcd * cd_row
        dacs = dacs + jnp.where(sub == CHUNK - 1, tail, 0.0)
        d_hi, d_mid, d_lo = _split3(dacs)
        up = lambda t: jnp.dot(triu, t, preferred_element_type=F32)
        da = (up(d_hi) + up(d_mid)) + up(d_lo)
        ddt_raw = (da * arow + _to_heads(dxdt * x, e)) * sg_ref[...]
        ddt_ref[...] = ddt_raw
        small_ref[0:1, :] += jnp.sum(da * dt, 0, keepdims=True) * arow
        small_ref[1:2, :] += _to_heads(_row8(jnp.sum(dy * x, 0, keepdims=True)), e)[0:1, :]
        small_ref[2:3, :] += jnp.sum(ddt_raw, 0, keepdims=True)
        dcb16 = dcb.astype(BF16)
        dxbc_ref[:, GROUP_CH + D_STATE:] = dc_off + jnp.dot(dcb16, bmat, preferred_element_type=F32)
        dxbc_ref[:, GROUP_CH:GROUP_CH + D_STATE] = db_st + lax.dot_general(dcb16, cmat, tn,
                                                                            preferred_element_type=F32)
        dxbc_ref[:, :GROUP_CH] = dxdt * dtx + dskx * dy
        g_ref[...] = g * jnp.exp(lastx) + g_here

    return _pcall(
        body, name="ssd_bwd", grid=(SSM_GROUPS, nb, nc),
        in_specs=[xbc_spec, lanes, lanes, wide, wide, wide, prev, grow, grow, nwspec],
        out_specs=[xbc_spec, lanes, wide,
                   pl.BlockSpec((None, 8, LANE), lambda g, b, c: (g, 0, 0)), nwspec],
        out_shape=[jax.ShapeDtypeStruct((nb, seq, CONV_DIM), F32),
                   jax.ShapeDtypeStruct((SSM_GROUPS, nb, seq, LANE), F32),
                   jax.ShapeDtypeStruct((nb, seq, D_INNER), F32),
                   jax.ShapeDtypeStruct((SSM_GROUPS, 8, LANE), F32),
                   jax.ShapeDtypeStruct((1, D_INNER), F32)],
        scratch_shapes=[pltpu.VMEM((D_STATE, hw), F32)],
        compiler_params=_params("parallel", "arbitrary", "arbitrary"),
    )(xbc, dtg, sgg, z, y, dys, sprev, alog_g, dskip_g, normw)


EW_TM = 256


def _merge_fwd(y_a, y_b, gm, bgate):
    nb, seq, _ = y_a.shape

    def body(a_ref, b_ref, ga_ref, gb_ref, bg_ref, o_ref):
        sa = _sigmoid(ga_ref[...] + bg_ref[0:1, :])
        sb = _sigmoid(gb_ref[...] + bg_ref[1:2, :])
        o_ref[...] = (sa * a_ref[...] + sb * b_ref[...]).astype(BF16)

    spec = pl.BlockSpec((None, EW_TM, D_MODEL), lambda b, i: (b, i, 0))
    spec1 = pl.BlockSpec((None, EW_TM, D_MODEL), lambda b, i: (b, i, 1))
    return _pcall(
        body, name="merge_fwd", grid=(nb, seq // EW_TM),
        in_specs=[spec, spec, spec, spec1, pl.BlockSpec((8, D_MODEL), lambda b, i: (0, 0))], out_specs=spec,
        out_shape=jax.ShapeDtypeStruct((nb, seq, D_MODEL), BF16),
        compiler_params=_params("parallel", "parallel"),
    )(y_a, y_b, gm, gm, bgate)


def _merge_bwd(dmerged, y_a, y_b, gm, bgate):
    nb, seq, _ = y_a.shape

    def body(dm_ref, a_ref, b_ref, ga_ref, gb_ref, bg_ref, dya_ref, dyb_ref, dg_ref, s_ref):
        @pl.when((pl.program_id(0) == 0) & (pl.program_id(1) == 0))
        def _():
            s_ref[...] = jnp.zeros_like(s_ref)

        dm = dm_ref[...]
        sa = _sigmoid(ga_ref[...] + bg_ref[0:1, :])
        sb = _sigmoid(gb_ref[...] + bg_ref[1:2, :])
        dya_ref[...] = (dm * sa).astype(BF16)
        dyb_ref[...] = (dm * sb).astype(BF16)
        dga = dm * a_ref[...] * (sa * (1.0 - sa))
        dgb = dm * b_ref[...] * (sb * (1.0 - sb))
        dg_ref[:, :D_MODEL] = dga.astype(BF16)
        dg_ref[:, D_MODEL:] = dgb.astype(BF16)
        s_ref[0:1, :] += jnp.sum(dga, 0, keepdims=True)
        s_ref[1:2, :] += jnp.sum(dgb, 0, keepdims=True)

    spec = pl.BlockSpec((None, EW_TM, D_MODEL), lambda b, i: (b, i, 0))
    spec1 = pl.BlockSpec((None, EW_TM, D_MODEL), lambda b, i: (b, i, 1))
    small = pl.BlockSpec((8, D_MODEL), lambda b, i: (0, 0))
    return _pcall(
        body, name="merge_bwd", grid=(nb, seq // EW_TM),
        in_specs=[spec, spec, spec, spec, spec1, small],
        out_specs=[spec, spec, pl.BlockSpec((None, EW_TM, 2 * D_MODEL), lambda b, i: (b, i, 0)), small],
        out_shape=[jax.ShapeDtypeStruct((nb, seq, D_MODEL), BF16), jax.ShapeDtypeStruct((nb, seq, D_MODEL), BF16),
                   jax.ShapeDtypeStruct((nb, seq, 2 * D_MODEL), BF16), jax.ShapeDtypeStruct((8, D_MODEL), F32)],
        compiler_params=_params("arbitrary", "arbitrary"),
    )(dmerged, y_a, y_b, gm, gm, bgate)


def _ln_loss(x, mix, gp, pw, target, bgate, ln_g, ln_b):
    nb, seq, _ = x.shape

    def body(x_ref, mix_ref, gp_ref, pw_ref, t_ref, bg_ref, g_ref, b_ref, dx_ref, dp_ref, dpw_ref, dgp_ref, s_ref):
        @pl.when((pl.program_id(0) == 0) & (pl.program_id(1) == 0))
        def _():
            s_ref[...] = jnp.zeros_like(s_ref)

        sp = _sigmoid(gp_ref[...] + bg_ref[2:3, :])
        pw = pw_ref[...]
        pre = ALPHA * x_ref[...] + mix_ref[...] + sp * pw
        mu = jnp.mean(pre, -1, keepdims=True)
        cen = pre - mu
        rstd = lax.rsqrt(jnp.mean(cen * cen, -1, keepdims=True) + LN_EPS)
        xhat = cen * rstd
        err = xhat * g_ref[...] + b_ref[...] - t_ref[...]
        dy = err * (1.0 / D_MODEL)
        dxh = dy * g_ref[...]
        dpre = rstd * (dxh - jnp.mean(dxh, -1, keepdims=True) - xhat * jnp.mean(dxh * xhat, -1, keepdims=True))
        dx_ref[...] = ALPHA * dpre
        dp_ref[...] = dpre.astype(BF16)
        dpw_ref[...] = (dpre * sp).astype(BF16)
        dgp = dpre * pw * (sp * (1.0 - sp))
        dgp_ref[...] = dgp.astype(BF16)
        s_ref[0:1, :] += jnp.sum(dy * xhat, 0, keepdims=True)
        s_ref[1:2, :] += jnp.sum(dy, 0, keepdims=True)
        s_ref[2:3, :] += jnp.sum(dgp, 0, keepdims=True)
        s_ref[3:4, :] += jnp.sum(err * err, 0, keepdims=True)

    spec = pl.BlockSpec((None, EW_TM, D_MODEL), lambda b, i: (b, i, 0))
    small = pl.BlockSpec((8, D_MODEL), lambda b, i: (0, 0))
    row = pl.BlockSpec((1, D_MODEL), lambda b, i: (0, 0))
    return _pcall(
        body, name="ln_loss", grid=(nb, seq // EW_TM),
        in_specs=[spec] * 5 + [small, row, row], out_specs=[spec] * 4 + [small],
        out_shape=[jax.ShapeDtypeStruct((nb, seq, D_MODEL), F32)] + [jax.ShapeDtypeStruct((nb, seq, D_MODEL), BF16)] * 3
        + [jax.ShapeDtypeStruct((8, D_MODEL), F32)],
        compiler_params=_params("arbitrary", "arbitrary"),
    )(x, mix, gp, pw, target, bgate, ln_g, ln_b)


def _adamw(w, g, m, v, name):
    rows, cols = w.shape
    tr = _row_tile(rows, cols, 8, 5 << 19)
    c1 = 1.0 - ADAM_B1 ** ADAM_STEP
    c2 = 1.0 - ADAM_B2 ** ADAM_STEP

    def body(w_ref, g_ref, m_ref, v_ref, d_ref, nm_ref, nv_ref):
        gv = g_ref[...]
        nm = ADAM_B1 * m_ref[...] + (1.0 - ADAM_B1) * gv
        nv = ADAM_B2 * v_ref[...] + (1.0 - ADAM_B2) * (gv * gv)
        d_ref[...] = -ADAM_LR * ((nm / c1) / (jnp.sqrt(nv / c2) + ADAM_EPS) + ADAM_WD * w_ref[...])
        nm_ref[...] = nm
        nv_ref[...] = nv

    spec = pl.BlockSpec((tr, cols), lambda i: (i, 0))
    return _pcall(
        body, name=name, grid=(rows // tr,), in_specs=[spec] * 4, out_specs=[spec] * 3,
        out_shape=[jax.ShapeDtypeStruct(w.shape, F32)] * 3, compiler_params=_params("parallel"),
    )(w, g, m, v)


def _sum_rows(parts, out_dtype, name):
    rows, cols = parts[0].shape
    tr = rows
    for cand in range(16, rows, 16):
        if rows % cand == 0 and cand * cols * 4 <= (1 << 20):
            tr = cand
    n = len(parts)

    def body(*refs):
        acc = refs[0][...].astype(F32)
        for r in refs[1:n]:
            acc = acc + r[...].astype(F32)
        refs[n][...] = acc.astype(out_dtype)

    spec = pl.BlockSpec((tr, cols), lambda i: (i, 0))
    return _pcall(
        body, name=name, grid=(rows // tr,), in_specs=[spec] * n, out_specs=spec,
        out_shape=jax.ShapeDtypeStruct((rows, cols), out_dtype), compiler_params=_params("parallel"),
    )(*parts)


def _place():
    return lax.axis_index("x"), lax.axis_index("y"), lax.axis_index("c")


def _other_chips(x, y):
    return [(1 - x, y), (x, 1 - y), (1 - x, 1 - y)]


def _remote(src, dst, send_sem, recv_sem, to):
    return pltpu.make_async_remote_copy(src_ref=src, dst_ref=dst, send_sem=send_sem, recv_sem=recv_sem,
                                        device_id=to, device_id_type=MESH)


ANY = pl.BlockSpec(memory_space=pl.ANY)
DMA_CHUNK_BYTES = 512 * 1024


def _row_chunks(rows, row_bytes):
    per = max(16, DMA_CHUNK_BYTES // row_bytes // 16 * 16)
    return [(s, min(per, rows - s)) for s in range(0, rows, per)]


def _row_tile(rows, cols, align, limit=1 << 21):
    best = None
    for cand in range(align, rows + 1, align):
        if rows % cand == 0 and cand * cols * 4 <= limit:
            best = cand
    return best or rows


def _allgather_pieces(pieces):
    n = len(pieces)
    halves = [_row_chunks(p.shape[0] // 2, p.shape[1] * p.dtype.itemsize) for p in pieces]
    entries = [(a, q, s, m, j) for a in range(n) for q, (s, m) in enumerate(halves[a]) for j in range(3)]
    slot = {(a, q, j): k for k, (a, q, _, _, j) in enumerate(entries)}
    n_ici = len(entries)

    def body(*refs):
        ins, outs = refs[:n], refs[n:2 * n]
        send_sems, recv_sems = refs[2 * n:]
        x, y, c = _place()
        me = 2 * x + y
        sibling = (x, y, 1 - c)
        chips = _other_chips(x, y)

        def landed(a, s, m, j, core):
            half = ins[a].shape[0] // 2
            return outs[a].at[2 * chips[j][0] + chips[j][1], pl.ds(core * half + s, m)]

        sent = []
        for k, (a, q, s, m, j) in enumerate(entries):
            if j < 2:
                half = ins[a].shape[0] // 2
                cp = _remote(ins[a].at[pl.ds(c * half + s, m)], outs[a].at[me, pl.ds(c * half + s, m)],
                             send_sems.at[k], recv_sems.at[k], (*chips[j], c))
                cp.start()
                sent.append(cp)

        def pass_to_sibling(k, blk):
            fw = _remote(blk, blk, send_sems.at[n_ici + k], recv_sems.at[n_ici + k], sibling)
            fw.start()
            sent.append(fw)

        for k, (a, q, s, m, j) in enumerate(entries):
            if j < 2:
                blk = landed(a, s, m, j, c)
                _remote(blk, blk, send_sems.at[k], recv_sems.at[k], (*chips[j], c)).wait_recv()
                first = q < (len(halves[a]) + 1) // 2
                if (j == 0) == first:
                    on = slot[(a, q, 2)]
                    rl = _remote(blk, blk, send_sems.at[on], recv_sems.at[on], (*chips[1 - j], c))
                    rl.start()
                    sent.append(rl)
                pass_to_sibling(k, blk)
        for k, (a, q, s, m, j) in enumerate(entries):
            if j == 2:
                blk = landed(a, s, m, j, c)
                _remote(blk, blk, send_sems.at[k], recv_sems.at[k], (*chips[j], c)).wait_recv()
                pass_to_sibling(k, blk)
        for k, (a, q, s, m, j) in enumerate(entries):
            blk = landed(a, s, m, j, 1 - c)
            _remote(blk, blk, send_sems.at[n_ici + k], recv_sems.at[n_ici + k], sibling).wait_recv()
        for cp in sent:
            cp.wait_send()

    gathered = _pcall(
        body, name="allgather_weights", in_specs=[ANY] * n, out_specs=[ANY] * n,
        out_shape=[jax.ShapeDtypeStruct((4,) + p.shape, p.dtype) for p in pieces],
        scratch_shapes=[pltpu.SemaphoreType.DMA((2 * n_ici,)), pltpu.SemaphoreType.DMA((2 * n_ici,))],
        compiler_params=pltpu.CompilerParams(has_side_effects=True),
    )(*pieces)
    x, y, _ = _place()
    return [lax.dynamic_update_slice(g, p[None], (2 * x + y, 0, 0)) for g, p in zip(gathered, pieces)]


def _sibling_exchange(grads):
    n = len(grads)
    chunks = [_row_chunks(g.shape[1] // 2, g.shape[2] * g.dtype.itemsize) for g in grads]
    n_sem = 4 * sum(len(ch) for ch in chunks)

    def body(*refs):
        ins, gots = refs[:n], refs[n:2 * n]
        send_sems, recv_sems = refs[2 * n:]
        x, y, c = _place()
        sibling = (x, y, 1 - c)
        work = []
        for a in range(n):
            half = ins[a].shape[1] // 2
            for piece in range(4):
                for s, m in chunks[a]:
                    k = len(work)
                    cp = _remote(ins[a].at[piece, pl.ds((1 - c) * half + s, m)], gots[a].at[piece, pl.ds(s, m)],
                                 send_sems.at[k], recv_sems.at[k], sibling)
                    cp.start()
                    work.append(cp)
        for cp in work:
            cp.wait()

    return _pcall(
        body, name="grad_sibling_exchange", in_specs=[ANY] * n, out_specs=[ANY] * n,
        out_shape=[jax.ShapeDtypeStruct((4, g.shape[1] // 2, g.shape[2]), g.dtype) for g in grads],
        scratch_shapes=[pltpu.SemaphoreType.DMA((n_sem,)), pltpu.SemaphoreType.DMA((n_sem,))],
        compiler_params=pltpu.CompilerParams(has_side_effects=True),
    )(*grads)


def _sibling_gather(fulls):
    n = len(fulls)
    chunks = [_row_chunks(f.shape[0] // 2, f.shape[1] * f.dtype.itemsize) for f in fulls]
    n_sem = sum(len(ch) for ch in chunks)

    def body(*refs):
        outs = refs[n:2 * n]
        send_sems, recv_sems = refs[2 * n:]
        x, y, c = _place()
        sibling = (x, y, 1 - c)
        work = []
        for a in range(n):
            h = outs[a].shape[0] // 2
            for s, m in chunks[a]:
                k = len(work)
                mine = outs[a].at[pl.ds(c * h + s, m)]
                cp = _remote(mine, mine, send_sems.at[k], recv_sems.at[k], sibling)
                cp.start()
                work.append((a, s, m, cp))
        for k, (a, s, m, cp) in enumerate(work):
            h = outs[a].shape[0] // 2
            cp.wait_send()
            theirs = outs[a].at[pl.ds((1 - c) * h + s, m)]
            _remote(theirs, theirs, send_sems.at[k], recv_sems.at[k], sibling).wait_recv()

    return _pcall(
        body, name="grad_sibling_gather", in_specs=[ANY] * n, out_specs=[ANY] * n,
        out_shape=[jax.ShapeDtypeStruct(f.shape, f.dtype) for f in fulls],
        input_output_aliases={a: a for a in range(n)},
        scratch_shapes=[pltpu.SemaphoreType.DMA((n_sem,)), pltpu.SemaphoreType.DMA((n_sem,))],
        compiler_params=pltpu.CompilerParams(has_side_effects=True),
    )(*fulls)


def _pair_sum(grad, got, place, name):
    _, rows, cols = grad.shape
    half = rows // 2
    tr = _row_tile(half, cols, 16)

    def body(p_ref, a_ref, b_ref, o_ref):
        o_ref[...] = (a_ref[...].astype(F32) + b_ref[...].astype(F32)).astype(BF16)

    return _pcall(
        body, name=name,
        grid_spec=pltpu.PrefetchScalarGridSpec(
            num_scalar_prefetch=1, grid=(4, half // tr),
            in_specs=[pl.BlockSpec((None, tr, cols), lambda k, i, p: (k, p[1] * (half // tr) + i, 0)),
                      pl.BlockSpec((None, tr, cols), lambda k, i, p: (k, i, 0))],
            out_specs=pl.BlockSpec((None, tr, cols), lambda k, i, p: (k, i, 0))),
        out_shape=jax.ShapeDtypeStruct((4, half, cols), BF16),
        compiler_params=_params("parallel", "parallel"),
    )(place, grad, got)


def _chip_sum(sums, got, place, name):
    _, h, cols = sums.shape
    tr = _row_tile(h, cols, 16)

    def body(p_ref, own_ref, g0, g1, g2, o_ref):
        o_ref[...] = ((own_ref[...].astype(F32) + g0[...].astype(F32)) + g1[...].astype(F32)) + g2[...].astype(F32)

    gspec = lambda j: pl.BlockSpec((None, tr, cols), lambda i, p: (j, i, 0))
    return _pcall(
        body, name=name,
        grid_spec=pltpu.PrefetchScalarGridSpec(
            num_scalar_prefetch=1, grid=(h // tr,),
            in_specs=[pl.BlockSpec((None, tr, cols), lambda i, p: (p[0], i, 0)), gspec(0), gspec(1), gspec(2)],
            out_specs=pl.BlockSpec((tr, cols), lambda i, p: (p[1] * (h // tr) + i, 0))),
        out_shape=jax.ShapeDtypeStruct((2 * h, cols), F32),
        compiler_params=_params("parallel"),
    )(place, sums, got, got, got)


def _allgather8(buf, name):
    rows = buf.shape[0]

    def body(in_ref, out_ref, send_sems, recv_sems):
        x, y, c = _place()
        me = 4 * x + 2 * y + c
        out_ref[me] = in_ref[...]
        work = []
        for rel in range(1, 8):
            fx, fy, fc = (rel >> 2) & 1, (rel >> 1) & 1, rel & 1
            to = (x ^ fx, y ^ fy, c ^ fc)
            cp = _remote(in_ref, out_ref.at[me], send_sems.at[rel - 1], recv_sems.at[rel - 1], to)
            cp.start()
            work.append((cp, 4 * to[0] + 2 * to[1] + to[2]))
        for rel, (cp, frm) in enumerate(work):
            cp.wait_send()
            blk = out_ref.at[frm]
            _remote(blk, blk, send_sems.at[rel], recv_sems.at[rel], (x, y, c)).wait_recv()

    return _pcall(
        body, name=name, in_specs=[pl.BlockSpec(memory_space=pltpu.VMEM)],
        out_specs=pl.BlockSpec(memory_space=pltpu.VMEM),
        out_shape=jax.ShapeDtypeStruct((8, rows, LANE), F32),
        scratch_shapes=[pltpu.SemaphoreType.DMA((7,)), pltpu.SemaphoreType.DMA((7,))],
        compiler_params=pltpu.CompilerParams(has_side_effects=True),
    )(buf)


def _pack_rows(arrs):
    parts = []
    for a in arrs:
        f = a.reshape(-1).astype(F32)
        parts.append(jnp.pad(f, (0, (-f.shape[0]) % LANE)))
    flat = jnp.concatenate(parts)
    rows = -(-flat.shape[0] // LANE)
    rows8 = -(-rows // 8) * 8
    return jnp.pad(flat, (0, rows8 * LANE - flat.shape[0])).reshape(rows8, LANE)


def _unpack_rows(buf, shapes):
    flat = buf.reshape(-1)
    outs, off = [], 0
    for s in shapes:
        n = int(np.prod(s))
        outs.append(flat[off:off + n].reshape(s))
        off += -(-n // LANE) * LANE
    return outs


def _local_grads(x, p, target, wseg, w_br16, w_out16, w_ple16, b_gate, conv_w, conv_b, dt_bias, a_log, d_skip,
                 ssm_norm_w, ln_g, ln_b, rel_bias, finish_dx):
    nb, seq, _ = x.shape
    bmaps = jnp.asarray(_bucket_maps())
    bias = _bias_tables(rel_bias, bmaps)
    bgate8 = jnp.pad(b_gate, ((0, 5), (0, 0)))
    dils = [d for _, d in PATTERNS]

    x16 = x.astype(BF16)
    p16 = p.astype(BF16)
    x16p = [_permute(x16, d) for d in dils]
    qkv = [_proj(x16p[g], [wseg["qkv%d" % g]], BF16, "proj_qkv%d" % g, True)[0].reshape(
        nb, dils[g], seq // dils[g], -1) for g in range(3)]
    nat = {}
    for gi, (group, tm) in enumerate(NAT_GROUPS):
        outs = _proj(x16, [wseg[s] for s in group], F32, "proj_nat%d" % gi, True, tm)
        nat.update(zip(group, outs))
    att = [_attn_fwd(qkv[g], bias[g * GROUP_HEADS:(g + 1) * GROUP_HEADS], dils[g], "attn_fwd%d" % g) for g in range(3)]
    natural = lambda t, g: _unpermute(t.reshape(nb, seq, t.shape[-1]), dils[g])
    oa, o_att, lse = _combine_fwd(att[0][0], att[0][1], natural(att[1][0], 1), natural(att[1][1], 1),
                                  natural(att[2][0], 2), natural(att[2][1], 2), nat["gatt"])

    conv_wg, conv_bg = _xbc_group_order(conv_w), _xbc_group_order(conv_b)
    act = _conv_fwd(nat["xbc"], conv_wg, conv_bg, "conv_fwd")
    dt_sp, dt_sg = _softplus_sig(nat["dt"], jnp.pad(dt_bias, ((0, 0), (0, LANE - SSM_HEADS))))
    dtg, sgg = _group_lanes(dt_sp), _group_lanes(dt_sg)
    alog_g, dskip_g = _group_lanes(a_log), _group_lanes(d_skip)
    y_ssm, y_all, sprev = _ssd_fwd(act, dtg, nat["z"], alog_g, dskip_g, ssm_norm_w)

    w_bra, w_brb = w_br16[:ATT_OUT], w_br16[ATT_OUT:]
    y_a, = _proj(oa, [w_bra], F32, "proj_ya")
    y_b, = _proj(y_ssm, [w_brb], F32, "proj_yb")
    merged = _merge_fwd(y_a, y_b, nat["gm"], bgate8)
    mix, = _proj(merged, [w_out16], F32, "proj_mix")
    pw, = _proj(p16, [w_ple16], F32, "proj_ple")

    dx, dpre16, dpw16, dgp16, ln_sums = _ln_loss(x, mix, nat["gp"], pw, target, bgate8, ln_g, ln_b)
    loss_sum = (0.5 / D_MODEL) * jnp.sum(ln_sums[3])
    dmerged = _dx([dpre16], [w_out16], [], "dx_merged")
    dya16, dyb16, dgm16, mg_sums = _merge_bwd(dmerged, y_a, y_b, nat["gm"], bgate8)
    doa = _dx([dya16], [w_bra], [], "dx_oa")
    dys = _dx([dyb16], [w_brb], [], "dx_yssm")
    g_w_out, = _dw(merged, [dpre16], BF16, "dw_out")
    g_w_br = jnp.concatenate([_dw(oa, [dya16], BF16, "dw_bra")[0], _dw(y_ssm, [dyb16], BF16, "dw_brb")[0]], axis=0)
    g_w_ple, = _dw(p16, [dpw16], BF16, "dw_ple")

    do_att, do16, stats, dgatt16 = _combine_bwd(doa, nat["gatt"], o_att, lse)
    dseg = {"gatt": dgatt16, "gm": dgm16, "gp": dgp16}
    dbias = []
    for g in range(3):
        own_order = lambda t: _permute(t, dils[g]).reshape(nb, dils[g], seq // dils[g], t.shape[-1])
        cotangent = (do_att, o_att, lse) if g == 0 else (own_order(do16), own_order(stats))
        dqkv, db = _attn_bwd(qkv[g], bias[g * GROUP_HEADS:(g + 1) * GROUP_HEADS], cotangent, dils[g],
                             "attn_bwd%d" % g)
        dseg["qkv%d" % g] = dqkv.reshape(nb, seq, -1)
        dbias.append(db)
    g_rel = _bias_grad(jnp.concatenate(dbias, axis=0), bmaps)[:, 0, :NUM_BUCKETS].T

    dact, ddtg, dz, ssd_small, g_normw = _ssd_bwd(
        act, dtg, sgg, nat["z"], y_all, dys, sprev, alog_g, dskip_g, ssm_norm_w)
    dseg["z"] = dz
    dseg["dt"] = jnp.pad(_ungroup_lanes(ddtg), ((0, 0), (0, 0), (0, LANE - SSM_HEADS)))
    dpre, conv_sums = _conv_bwd_pre(dact, nat["xbc"], conv_wg, conv_bg, "conv_bwd")
    dseg["xbc"] = _conv_bwd_x(dpre, conv_wg, "conv_bwd_x")
    csum = _xbc_reference_order(conv_sums)

    dx_perm = [_unpermute(_dx([dseg["qkv%d" % g]], [wseg["qkv%d" % g]], [], "dx_qkv%d" % g, True), dils[g])
               for g in (1, 2)]
    dwseg = {"qkv%d" % g: _dw(x16p[g], [dseg["qkv%d" % g]], BF16, "dw_qkv%d" % g, True)[0] for g in range(3)}
    for gi, group in enumerate(DW_GROUPS):
        dwseg.update(zip(group, _dw(x16, [dseg[s] for s in group], BF16, "dw_nat%d" % gi, True)))
    names = ["qkv0"] + [s for group, _ in NAT_GROUPS for s in group]
    dx = finish_dx([dseg[s] for s in names], [wseg[s] for s in names], [dx] + dx_perm, dwseg, g_w_br, g_w_out, g_w_ple)

    small = dict(
        b_gate=jnp.stack([mg_sums[0], mg_sums[1], ln_sums[2]]),
        conv_w=csum[0:4], conv_b=csum[4:5],
        dt_bias=_ungroup_lanes(ssd_small[:, 2:3, :]), a_log=_ungroup_lanes(ssd_small[:, 0:1, :]),
        d_skip=_ungroup_lanes(ssd_small[:, 1:2, :]), ssm_norm_w=g_normw,
        ln_g=ln_sums[0:1], ln_b=ln_sums[1:2], rel_bias=g_rel)
    return loss_sum, dx, small


DX_TM = 256
SMALL_ORDER = ("b_gate", "conv_w", "conv_b", "dt_bias", "a_log", "d_skip", "ssm_norm_w", "ln_g", "ln_b", "rel_bias")
SMALL_FULL_SHAPES = dict(b_gate=(3, 1024), conv_w=(4, 3072), conv_b=(1, 3072), dt_bias=(1, 32), a_log=(1, 32),
                         d_skip=(1, 32), ssm_norm_w=(1, 2048), ln_g=(1, 1024), ln_b=(1, 1024), rel_bias=(32, 36))


def kernel(x, p, w_in, b_gate, conv_w, conv_b, dt_bias, a_log, d_skip, ssm_norm_w, w_branch, w_out, w_ple, ln_g, ln_b, rel_bias, loss_target, m_w_in, m_b_gate, m_conv_w, m_conv_b, m_dt_bias, m_a_log, m_d_skip, m_ssm_norm_w, m_w_branch, m_w_out, m_w_ple, m_ln_g, m_ln_b, m_rel_bias, v_w_in, v_b_gate, v_conv_w, v_conv_b, v_dt_bias, v_a_log, v_d_skip, v_ssm_norm_w, v_w_branch, v_w_out, v_w_ple, v_ln_g, v_ln_b, v_rel_bias):
    cx, cy, cc = _place()
    chip = 2 * cx + cy
    dev = 4 * cx + 2 * cy + cc

    w_in_t = jnp.transpose(w_in[0])
    win16 = _shard_to_window(w_in_t, chip)
    g_win, g_br, g_out, g_ple = _allgather_pieces(
        [win16, w_branch[0].astype(BF16), w_out[0].astype(BF16), w_ple[0].astype(BF16)])
    wseg = _assemble(g_win)
    w_br16 = g_br.reshape(4 * 704, D_MODEL)
    w_out16 = g_out.reshape(D_MODEL, D_MODEL)
    w_ple16 = jnp.transpose(g_ple, (1, 0, 2)).reshape(PLE_DIM, D_MODEL)
    shards = _allgather8(_pack_rows([b_gate[0], conv_w[0]]), "allgather_small_params")
    per_chip = [_unpack_rows(shards[2 * k], [(3, 256), (4, 768)]) for k in range(4)]
    b_gate_full = jnp.concatenate([pc[0] for pc in per_chip], axis=1)
    conv_w_full = jnp.concatenate([pc[1] for pc in per_chip], axis=1)

    place = jnp.stack([chip, cc]).astype(jnp.int32)
    reduced = []

    def finish_dx(dhs, ws, accs, dwseg, d_br, d_out, d_ple):
        grads = [_pack(dwseg), d_br.reshape(4, 704, D_MODEL), d_out.reshape(4, 256, D_MODEL),
                 jnp.transpose(d_ple.reshape(PLE_DIM, 4, 256), (1, 0, 2))]
        got = _sibling_exchange(grads)
        chip_sums = [_pair_sum(g, t, place, "grad_pair_sum_%d" % i) for i, (g, t) in enumerate(zip(grads, got))]
        dx, others = _dx(dhs, ws, accs, "dx_w_in_and_grad_chip_scatter", True, DX_TM, chip_sums)
        fulls = [_chip_sum(s, t, place, "grad_chip_sum_%d" % i) for i, (s, t) in enumerate(zip(chip_sums, others))]
        reduced.extend(_sibling_gather(fulls))
        return dx

    loss_sum, grad_x, small = _local_grads(
        x, p[0], loss_target, wseg, w_br16, w_out16, w_ple16, b_gate_full, conv_w_full, conv_b, dt_bias, a_log,
        d_skip, ssm_norm_w, ln_g, ln_b, rel_bias, finish_dx)
    loss = lax.psum(loss_sum, ("x", "y", "c"))
    big = reduced
    g_w_in = _window_to_shard(big[0], chip)
    g_w_branch, g_w_out, g_w_ple = big[1], big[2], big[3]
    parts = _allgather8(_pack_rows([small[n] for n in SMALL_ORDER]), "allgather_small_grads")
    small_sum = _sum_rows([parts[i] for i in range(8)], F32, "small_grad_sum")
    sg = dict(zip(SMALL_ORDER, _unpack_rows(small_sum, [SMALL_FULL_SHAPES[n] for n in SMALL_ORDER])))
    sg["b_gate"] = lax.dynamic_slice_in_dim(sg["b_gate"], chip * 256, 256, axis=1)
    sg["conv_w"] = lax.dynamic_slice_in_dim(sg["conv_w"], chip * 768, 768, axis=1)
    del dev

    upd = {}
    upd["w_in"] = [jnp.transpose(t) for t in _adamw(w_in_t, g_w_in, jnp.transpose(m_w_in[0]),
                                                      jnp.transpose(v_w_in[0]), "adamw_w_in")]
    upd["w_branch"] = _adamw(w_branch[0], g_w_branch, m_w_branch[0], v_w_branch[0], "adamw_w_branch")
    upd["w_out"] = _adamw(w_out[0], g_w_out, m_w_out[0], v_w_out[0], "adamw_w_out")
    upd["w_ple"] = _adamw(w_ple[0], g_w_ple, m_w_ple[0], v_w_ple[0], "adamw_w_ple")
    small_w = dict(b_gate=b_gate, conv_w=conv_w, conv_b=conv_b, dt_bias=dt_bias, a_log=a_log, d_skip=d_skip,
                   ssm_norm_w=ssm_norm_w, ln_g=ln_g, ln_b=ln_b, rel_bias=rel_bias)
    small_m = dict(b_gate=m_b_gate, conv_w=m_conv_w, conv_b=m_conv_b, dt_bias=m_dt_bias, a_log=m_a_log,
                   d_skip=m_d_skip, ssm_norm_w=m_ssm_norm_w, ln_g=m_ln_g, ln_b=m_ln_b, rel_bias=m_rel_bias)
    small_v = dict(b_gate=v_b_gate, conv_w=v_conv_w, conv_b=v_conv_b, dt_bias=v_dt_bias, a_log=v_a_log,
                   d_skip=v_d_skip, ssm_norm_w=v_ssm_norm_w, ln_g=v_ln_g, ln_b=v_ln_b, rel_bias=v_rel_bias)
    shapes = [small_w[n].shape for n in SMALL_ORDER]
    s_delta, s_m, s_v = _adamw(_pack_rows([small_w[n] for n in SMALL_ORDER]), _pack_rows([sg[n] for n in SMALL_ORDER]),
                               _pack_rows([small_m[n] for n in SMALL_ORDER]), _pack_rows([small_v[n] for n in SMALL_ORDER]),
                               "adamw_small")
    for i, n in enumerate(SMALL_ORDER):
        upd[n] = tuple(_unpack_rows(t, shapes)[i] for t in (s_delta, s_m, s_v))
        sg[n] = sg[n].reshape(small_w[n].shape)

    order = ("w_in", "b_gate", "conv_w", "conv_b", "dt_bias", "a_log", "d_skip", "ssm_norm_w", "w_branch", "w_out",
             "w_ple", "ln_g", "ln_b", "rel_bias")
    grads = dict(sg, w_in=jnp.transpose(g_w_in)[None],w_branch=g_w_branch[None], w_out=g_w_out[None], w_ple=g_w_ple[None])
    lead = lambda n, t: t[None] if n in ("w_in", "w_branch", "w_out", "w_ple") else t
    return (loss, grad_x, *[grads[n] for n in order], *[lead(n, upd[n][0]) for n in order],
            *[lead(n, upd[n][1]) for n in order], *[lead(n, upd[n][2]) for n in order])
```

```python
import functools
import math

import numpy as np
import jax
import jax.numpy as jnp
from jax import lax
from jax.experimental import pallas as pl
from jax.experimental.pallas import tpu as pltpu

F32, BF16 = jnp.float32, jnp.bfloat16

D_MODEL = 1024
HEAD_DIM = 64
GROUP_HEADS = 12
ATT_OUT = GROUP_HEADS * HEAD_DIM
PATTERNS = ((128, 1), (512, 4), (2048, 16))
BAND = 128
NUM_BUCKETS = 32
MAX_DISTANCE = 2048
D_INNER = 2048
SSM_HEADS = 32
SSM_GROUPS = 4
GROUP_SSM_HEADS = SSM_HEADS // SSM_GROUPS
D_STATE = 128
CHUNK = 128
PLE_DIM = 256
ALPHA = 2.0 ** 0.25
LN_EPS = 1e-5
RMS_EPS = 1e-5
ADAM_LR, ADAM_B1, ADAM_B2, ADAM_EPS, ADAM_WD, ADAM_STEP = 0.001, 0.9, 0.999, 1e-08, 0.01, 10
NEG = -1e30

QKV_W = 3 * ATT_OUT
IN_COLS = 15904
SHARD_COLS = IN_COLS // 4
DT_COL = 12800
ROW_TILE = 16
WIN_ROWS = 4000


def _win_offset(k):
    return (k * SHARD_COLS) % ROW_TILE


def _win_start(k):
    return k * SHARD_COLS - _win_offset(k)

VMEM_LIMIT_BYTES = 56 * 1024 * 1024
LANE = 128
MESH = pl.DeviceIdType.MESH
NT = (((1,), (1,)), ((), ()))
TN = (((0,), (0,)), ((), ()))


def _pcall(body, **kw):
    return pl.pallas_call(body, **kw)


def _params(*sem):
    return pltpu.CompilerParams(dimension_semantics=sem, vmem_limit_bytes=VMEM_LIMIT_BYTES)


def _sigmoid(v):
    return jax.nn.sigmoid(v)


MM_TM = 512


def _permute(t, d):
    nb, seq, ch = t.shape
    return t if d == 1 else t.reshape(nb, seq // d, d, ch).transpose(0, 2, 1, 3).reshape(nb, seq, ch)


def _unpermute(t, d):
    nb, seq, ch = t.shape
    return t if d == 1 else t.reshape(nb, d, seq // d, ch).transpose(0, 2, 1, 3).reshape(nb, seq, ch)


def _tok_spec(tm, width):
    return pl.BlockSpec((None, tm, width), lambda b, i: (b, i, 0))


def _whole(arr, single_buffer=False):
    mode = dict(pipeline_mode=pl.Buffered(1)) if single_buffer else {}
    return pl.BlockSpec(arr.shape, lambda b, i: (0,) * arr.ndim, **mode)


def _proj(a3, ws, out_dtype, name, w_rows_are_outputs=False, tm=MM_TM):
    nb, seq, kdim = a3.shape
    nw = len(ws)
    widths = [w.shape[0] if w_rows_are_outputs else w.shape[1] for w in ws]

    def body(*refs):
        a = refs[0][...].astype(BF16)
        for w_ref, o_ref in zip(refs[1:1 + nw], refs[1 + nw:]):
            if w_rows_are_outputs:
                v = lax.dot_general(a, w_ref[...], NT, preferred_element_type=F32)
            else:
                v = jnp.dot(a, w_ref[...], preferred_element_type=F32)
            o_ref[...] = v.astype(out_dtype)

    return _pcall(
        body, name=name, grid=(nb, seq // tm),
        in_specs=[_tok_spec(tm, kdim)] + [_whole(w) for w in ws],
        out_specs=[_tok_spec(tm, n) for n in widths],
        out_shape=[jax.ShapeDtypeStruct((nb, seq, n), out_dtype) for n in widths],
        compiler_params=_params("parallel", "parallel"),
    )(a3, *ws)


def _dx(dhs, ws, accs, name, w_rows_are_outputs=False, tm=MM_TM, scatter=None):
    nb, seq, _ = dhs[0].shape
    nd, nacc = len(dhs), len(accs)
    kout = ws[0].shape[1] if w_rows_are_outputs else ws[0].shape[0]
    sums = scatter or []
    ns = len(sums)
    chunks = [_row_chunks(s.shape[1], s.shape[2] * s.dtype.itemsize) for s in sums]
    n_sem = 3 * sum(len(ch) for ch in chunks)
    grid = (nb, seq // tm)

    def body(*refs):
        n_in = 2 * nd + nacc
        sum_refs, o_ref, got_refs = refs[n_in:n_in + ns], refs[n_in + ns], refs[n_in + ns + 1:n_in + 2 * ns + 1]

        def copies():
            send_sems, recv_sems = refs[-2], refs[-1]
            x, y, c = _place()
            out = []
            for a in range(ns):
                for s, m in chunks[a]:
                    for j, (cx, cy) in enumerate(_other_chips(x, y)):
                        k = len(out)
                        out.append(_remote(sum_refs[a].at[2 * cx + cy, pl.ds(s, m)], got_refs[a].at[j, pl.ds(s, m)],
                                           send_sems.at[k], recv_sems.at[k], (cx, cy, c)))
            return out

        if ns:
            @pl.when((pl.program_id(0) == 0) & (pl.program_id(1) == 0))
            def _():
                for cp in copies():
                    cp.start()

        v = None
        for dh_ref, w_ref in zip(refs[:nd], refs[nd:2 * nd]):
            dh = dh_ref[...].astype(BF16)
            if w_rows_are_outputs:
                t = jnp.dot(dh, w_ref[...], preferred_element_type=F32)
            else:
                t = lax.dot_general(dh, w_ref[...], NT, preferred_element_type=F32)
            v = t if v is None else v + t
        for a_ref in refs[2 * nd:n_in]:
            v = v + a_ref[...]
        o_ref[...] = v

        if ns:
            @pl.when((pl.program_id(0) == grid[0] - 1) & (pl.program_id(1) == grid[1] - 1))
            def _():
                for cp in copies():
                    cp.wait()

    out = _pcall(
        body, name=name, grid=grid,
        in_specs=[_tok_spec(tm, dh.shape[-1]) for dh in dhs] + [_whole(w, bool(ns)) for w in ws]
        + [_tok_spec(tm, kout)] * nacc + [ANY] * ns,
        out_specs=[_tok_spec(tm, kout)] + [ANY] * ns,
        out_shape=[jax.ShapeDtypeStruct((nb, seq, kout), F32)]
        + [jax.ShapeDtypeStruct((3,) + s.shape[1:], s.dtype) for s in sums],
        input_output_aliases={2 * nd: 0} if nacc else {},
        scratch_shapes=[pltpu.SemaphoreType.DMA((n_sem,)), pltpu.SemaphoreType.DMA((n_sem,))] if ns else [],
        compiler_params=pltpu.CompilerParams(
            dimension_semantics=("arbitrary", "arbitrary") if ns else ("parallel", "parallel"),
            vmem_limit_bytes=VMEM_LIMIT_BYTES, has_side_effects=bool(ns)),
    )(*dhs, *ws, *accs, *sums)
    return (out[0], list(out[1:])) if ns else out[0]


def _dw(a3, dhs, out_dtype, name, rows_are_outputs=False):
    nb, seq, kdim = a3.shape
    nd = len(dhs)
    grid = (nb, seq // MM_TM)
    shapes = [(dh.shape[-1], kdim) if rows_are_outputs else (kdim, dh.shape[-1]) for dh in dhs]

    def body(*refs):
        b, i = pl.program_id(0), pl.program_id(1)
        dh_refs, o_refs, acc_refs = refs[1:1 + nd], refs[1 + nd:1 + 2 * nd], refs[1 + 2 * nd:]

        @pl.when((b == 0) & (i == 0))
        def _():
            for acc_ref in acc_refs:
                acc_ref[...] = jnp.zeros_like(acc_ref)

        a = refs[0][...].astype(BF16)
        for dh_ref, acc_ref in zip(dh_refs, acc_refs):
            dh = dh_ref[...].astype(BF16)
            acc_ref[...] += lax.dot_general(*((dh, a) if rows_are_outputs else (a, dh)), TN,
                                            preferred_element_type=F32)

        @pl.when((b == grid[0] - 1) & (i == grid[1] - 1))
        def _():
            for o_ref, acc_ref in zip(o_refs, acc_refs):
                o_ref[...] = acc_ref[...].astype(out_dtype)

    return _pcall(
        body, name=name, grid=grid,
        in_specs=[_tok_spec(MM_TM, kdim)] + [_tok_spec(MM_TM, dh.shape[-1]) for dh in dhs],
        out_specs=[pl.BlockSpec(s, lambda b, i: (0, 0)) for s in shapes],
        out_shape=[jax.ShapeDtypeStruct(s, out_dtype) for s in shapes],
        scratch_shapes=[pltpu.VMEM(s, F32) for s in shapes],
        compiler_params=_params("arbitrary", "arbitrary"),
    )(a3, *dhs)


def _qkv_rows(g):
    return [(part * QKV_W + g * ATT_OUT + hp * LANE, LANE) for hp in range(ATT_OUT // LANE) for part in range(3)]


XBC_START = 3 * QKV_W + ATT_OUT + D_INNER
GROUP_CH = GROUP_SSM_HEADS * HEAD_DIM
XBC_GROUP = GROUP_CH + 2 * D_STATE
CONV_DIM = SSM_GROUPS * XBC_GROUP


def _xbc_ranges():
    out = []
    for g in range(SSM_GROUPS):
        out += [(g * GROUP_CH, GROUP_CH), (D_INNER + g * D_STATE, D_STATE),
                (D_INNER + SSM_GROUPS * D_STATE + g * D_STATE, D_STATE)]
    return out


def _xbc_group_order(t):
    return jnp.concatenate([t[..., s:s + n] for s, n in _xbc_ranges()], axis=-1)


def _xbc_reference_order(t):
    g = lambda off, n: [t[..., k * XBC_GROUP + off:k * XBC_GROUP + off + n] for k in range(SSM_GROUPS)]
    return jnp.concatenate(g(0, GROUP_CH) + g(GROUP_CH, D_STATE) + g(GROUP_CH + D_STATE, D_STATE), axis=-1)


def _segments():
    one = lambda name, start, rows: (name, [(start, rows)], max(rows, LANE))
    return [("qkv%d" % g, _qkv_rows(g), QKV_W) for g in range(3)] + [
        one("gatt", 3 * QKV_W, ATT_OUT), one("z", 3 * QKV_W + ATT_OUT, D_INNER),
        ("xbc", [(XBC_START + s, n) for s, n in _xbc_ranges()], CONV_DIM), one("dt", DT_COL, SSM_HEADS),
        one("gm", DT_COL + SSM_HEADS, 2 * D_MODEL), one("gp", DT_COL + SSM_HEADS + 2 * D_MODEL, D_MODEL)]


LAYOUT_TC = 256
NAT_GROUPS = ((("gatt", "z", "dt", "gp"), 512), (("xbc", "gm"), 256))
DW_GROUPS = (("gatt", "z", "dt", "gp"), ("xbc",), ("gm",))


def _assemble(win):
    segs = _segments()

    def body(win_ref, *outs):
        def pieces(start, rows):
            t, end = start, start + rows
            while t < end:
                k = min(t // SHARD_COLS, 3)
                shard_end = (k + 1) * SHARD_COLS
                if k < 3 and shard_end % ROW_TILE and t == shard_end - shard_end % ROW_TILE:
                    lo = t - _win_start(k)
                    yield win_ref[k, lo:lo + ROW_TILE, :] + win_ref[k + 1, 0:ROW_TILE, :]
                    t += ROW_TILE
                    continue
                upto = min(end, shard_end - shard_end % ROW_TILE if k < 3 else end)
                yield win_ref[k, t - _win_start(k):upto - _win_start(k), :]
                t = upto

        for (_, ranges, total), o_ref in zip(segs, outs):
            off = 0
            for start, rows in ranges:
                for part in pieces(start, rows):
                    o_ref[off:off + part.shape[0], :] = part
                    off += part.shape[0]
            if off < total:
                o_ref[off:total, :] = jnp.zeros((total - off, o_ref.shape[1]), BF16)

    outs = _pcall(
        body, name="assemble_w_in", grid=(D_MODEL // LAYOUT_TC,),
        in_specs=[pl.BlockSpec((4, WIN_ROWS, LAYOUT_TC), lambda i: (0, 0, i))],
        out_specs=[pl.BlockSpec((total, LAYOUT_TC), lambda i: (0, i)) for _, _, total in segs],
        out_shape=[jax.ShapeDtypeStruct((total, D_MODEL), BF16) for _, _, total in segs],
        compiler_params=_params("parallel"),
    )(win)
    return {name: o for (name, _, _), o in zip(segs, outs)}


def _pack(dsegs):
    segs = _segments()

    def body(*refs):
        ins, o_ref = refs[:-1], refs[-1]
        tail = IN_COLS - _win_start(3)
        o_ref[3, tail:, :] = jnp.zeros((WIN_ROWS - tail, o_ref.shape[2]), BF16)
        for (_, ranges, _), s_ref in zip(segs, ins):
            off = 0
            for start, rows in ranges:
                for k in range(4):
                    lo = _win_start(k)
                    a, b = max(start, lo), min(start + rows, lo + WIN_ROWS)
                    if a < b:
                        o_ref[k, a - lo:b - lo, :] = s_ref[off + a - start:off + b - start, :]
                off += rows

    return _pcall(
        body, name="pack_dw_in", grid=(D_MODEL // LAYOUT_TC,),
        in_specs=[pl.BlockSpec((total, LAYOUT_TC), lambda i: (0, i)) for _, _, total in segs],
        out_specs=pl.BlockSpec((4, WIN_ROWS, LAYOUT_TC), lambda i: (0, 0, i)),
        out_shape=jax.ShapeDtypeStruct((4, WIN_ROWS, D_MODEL), BF16),
        compiler_params=_params("parallel"),
    )(*[dsegs[name] for name, _, _ in segs])


def _shard_to_window(shard_t, k):
    def at(off):
        return lambda w: jnp.pad(w.astype(BF16), ((off, WIN_ROWS - SHARD_COLS - off), (0, 0)))

    return lax.cond(k % 2 == 1, at(_win_offset(1)), at(_win_offset(0)), shard_t)


def _window_to_shard(win, k):
    return lax.dynamic_slice(win, ((k % 2) * _win_offset(1), 0), (SHARD_COLS, D_MODEL))


def _bucket_maps():
    qi = np.arange(BAND)[:, None]
    kj = np.arange(2 * BAND)[None, :]
    delta = qi + BAND - kj
    maps = []
    for window, dil in PATTERNS:
        valid = (delta >= 0) & (delta <= window // dil)
        dist = np.maximum(delta, 0) * dil
        max_exact = NUM_BUCKETS // 2
        d_f = np.maximum(dist, 1).astype(np.float32)
        large = max_exact + (np.log(d_f / np.float32(max_exact)) / np.float32(math.log(MAX_DISTANCE / max_exact))
                             * np.float32(NUM_BUCKETS - max_exact)).astype(np.int32)
        large = np.minimum(large, NUM_BUCKETS - 1)
        bucket = np.where(dist < max_exact, dist, large)
        maps.append(np.where(valid, bucket, -1).astype(np.int32))
    return np.stack(maps)


def _bias_tables(rel_bias, bmaps):
    def body(rb_ref, bm_ref, o_ref):
        h = pl.program_id(0)
        bm = bm_ref[...]
        acc = jnp.full(bm.shape, NEG, F32)
        for b in range(NUM_BUCKETS):
            acc = jnp.where(bm == b, rb_ref[b, h], acc)
        o_ref[...] = acc

    return _pcall(
        body, name="bias_tables", grid=(3 * GROUP_HEADS,),
        in_specs=[pl.BlockSpec(memory_space=pltpu.SMEM),
                  pl.BlockSpec((None, BAND, 2 * BAND), lambda h: (h // GROUP_HEADS, 0, 0))],
        out_specs=pl.BlockSpec((None, BAND, 2 * BAND), lambda h: (h, 0, 0)),
        out_shape=jax.ShapeDtypeStruct((3 * GROUP_HEADS, BAND, 2 * BAND), F32),
        compiler_params=_params("parallel"),
    )(rel_bias, bmaps)


def _bias_grad(dbias, bmaps):
    def body(db_ref, bm_ref, o_ref):
        bm = bm_ref[...]
        db = db_ref[...]
        lane = lax.broadcasted_iota(jnp.int32, (1, LANE), 1)
        vec = jnp.zeros((1, LANE), F32)
        for b in range(NUM_BUCKETS):
            s = jnp.sum(jnp.where(bm == b, db, 0.0), keepdims=True)
            vec = jnp.where(lane == b, s, vec)
        o_ref[...] = vec

    return _pcall(
        body, name="bias_grad", grid=(3 * GROUP_HEADS,),
        in_specs=[pl.BlockSpec((None, BAND, 2 * BAND), lambda h: (h, 0, 0)),
                  pl.BlockSpec((None, BAND, 2 * BAND), lambda h: (h // GROUP_HEADS, 0, 0))],
        out_specs=pl.BlockSpec((None, 1, LANE), lambda h: (h, 0, 0)),
        out_shape=jax.ShapeDtypeStruct((3 * GROUP_HEADS, 1, LANE), F32),
        compiler_params=_params("parallel"),
    )(dbias, bmaps)


def _rows(n):
    if isinstance(n, int):
        return pl.ds(n * BAND, BAND)
    return pl.ds(pl.multiple_of(n * BAND, BAND), BAND)


def _for_blocks(blocks, nblk, per, carry):
    carry = blocks([0], carry, False)
    start = 1 + (nblk - 1) % per
    for n in range(1, start):
        carry = blocks([n], carry, True)
    trips = (nblk - start) // per
    if trips > 0:
        carry = lax.fori_loop(
            0, trips, lambda t, c: blocks([start + t * per + u for u in range(per)], c, True), carry)
    return carry


def _pairs_per_step(d):
    return {1: 1, 4: 6, 16: 6}[d]


def _attn_fwd(qkv4, bias, d, name):
    nb, _, sub, _ = qkv4.shape
    nblk = sub // BAND
    scale = HEAD_DIM ** -0.5
    npair = ATT_OUT // LANE
    hps = _pairs_per_step(d)
    compact = d > 1

    def body(qkv_ref, bias_ref, o_ref, l_ref):
        def blocks(ns, carry, with_prev):
            chains = [(bi, i, h) for bi in range(len(ns)) for i in range(hps) for h in range(2)]
            first_head = lax.broadcasted_iota(jnp.int32, (BAND, LANE), 1) < HEAD_DIM
            pair = lambda n, i, part: qkv_ref[_rows(n), (3 * i + part) * LANE:(3 * i + part + 1) * LANE]
            scores = []
            for bi, i, h in chains:
                n = ns[bi]
                qp = pair(n, i, 0) * scale
                q = jnp.where(first_head if h == 0 else jnp.logical_not(first_head), qp, jnp.zeros_like(qp))
                s_c = lax.dot_general(q, pair(n, i, 1), NT, preferred_element_type=F32) + bias_ref[2 * i + h, :, BAND:]
                s_p = None
                if with_prev:
                    s_p = lax.dot_general(q, pair(n - 1, i, 1), NT,
                                          preferred_element_type=F32) + bias_ref[2 * i + h, :, :BAND]
                scores.append((s_c, s_p))
            probs = []
            for s_c, s_p in scores:
                m = jnp.max(s_c, -1, keepdims=True)
                if with_prev:
                    m = jnp.maximum(m, jnp.max(s_p, -1, keepdims=True))
                e_c = jnp.exp(s_c - m)
                den = jnp.sum(e_c, -1, keepdims=True)
                e_p = None
                if with_prev:
                    e_p = jnp.exp(s_p - m)
                    den = den + jnp.sum(e_p, -1, keepdims=True)
                    e_p = e_p.astype(BF16)
                probs.append((e_c.astype(BF16), e_p, den, m))
            outs = {}
            for (bi, i, h), (e_c, e_p, den, m) in zip(chains, probs):
                n = ns[bi]
                acc = jnp.dot(e_c, pair(n, i, 2), preferred_element_type=F32)
                if with_prev:
                    acc = acc + jnp.dot(e_p, pair(n - 1, i, 2), preferred_element_type=F32)
                outs[(bi, i, h)] = (acc / den, m + jnp.log(den))
            lane = lax.broadcasted_iota(jnp.int32, (BAND, LANE), 1)
            for bi, n in enumerate(ns):
                per_head = jnp.zeros((BAND, LANE), F32)
                for i in range(hps):
                    o_ref[_rows(n), i * LANE:(i + 1) * LANE] = jnp.where(first_head, outs[(bi, i, 0)][0],
                                                                         outs[(bi, i, 1)][0])
                    if compact:
                        for h in range(2):
                            per_head = jnp.where(lane == 2 * i + h, outs[(bi, i, h)][1], per_head)
                    else:
                        l_ref[_rows(n), i * LANE:(i + 1) * LANE] = jnp.where(first_head, outs[(bi, i, 0)][1],
                                                                             outs[(bi, i, 1)][1])
                if compact:
                    l_ref[_rows(n), :] = per_head
            return carry

        _for_blocks(blocks, nblk, 2 if hps == 1 else 1, 0)

    in_specs = [pl.BlockSpec((None, None, sub, 3 * LANE * hps), lambda hp, b, r: (b, r, 0, hp)),
                pl.BlockSpec((2 * hps, BAND, 2 * BAND), lambda hp, b, r: (hp, 0, 0))]
    if compact:
        return _pcall(
            body, name=name, grid=(1, nb, d), in_specs=in_specs,
            out_specs=[pl.BlockSpec((None, None, sub, ATT_OUT), lambda hp, b, r: (b, r, 0, 0)),
                       pl.BlockSpec((None, None, sub, LANE), lambda hp, b, r: (b, r, 0, 0))],
            out_shape=[jax.ShapeDtypeStruct((nb, d, sub, ATT_OUT), F32), jax.ShapeDtypeStruct((nb, d, sub, LANE), F32)],
            compiler_params=_params("parallel", "parallel", "parallel"),
        )(qkv4, bias)
    ospec = pl.BlockSpec((None, sub, hps * LANE), lambda hp, b, r: (b, 0, r * (npair // hps) + hp))
    return _pcall(
        body, name=name, grid=(npair // hps, nb, d), in_specs=in_specs, out_specs=[ospec, ospec],
        out_shape=[jax.ShapeDtypeStruct((nb, sub, d * ATT_OUT), F32)] * 2,
        compiler_params=_params("parallel", "parallel", "parallel"),
    )(qkv4, bias)


STAT_LSE_LANE = 16


def _attn_bwd(qkv4, bias, cotangent, d, name):
    nb, _, sub, _ = qkv4.shape
    nblk = sub // BAND
    scale = HEAD_DIM ** -0.5
    npair = ATT_OUT // LANE
    hps = _pairs_per_step(d)
    compact = d > 1

    def body(qkv_ref, bias_ref, *rest):
        do_ref, dqkv_ref, db_ref = rest[0], rest[-2], rest[-1]
        b, r = pl.program_id(1), pl.program_id(2)

        @pl.when((b == 0) & (r == 0))
        def _():
            db_ref[...] = jnp.zeros_like(db_ref)

        def blocks(ns, carry, with_prev):
            sides = (0, 1) if with_prev else (0,)
            chains = [(bi, i, h, sd) for bi in range(len(ns)) for i in range(hps) for h in range(2) for sd in sides]
            first_head = lax.broadcasted_iota(jnp.int32, (BAND, LANE), 1) < HEAD_DIM
            own = lambda h, t: jnp.where(first_head if h == 0 else jnp.logical_not(first_head), t, jnp.zeros_like(t))
            pair = lambda rows, i, part: qkv_ref[rows, (3 * i + part) * LANE:(3 * i + part + 1) * LANE]
            key_rows = lambda bi, sd: _rows(ns[bi] - sd)
            qs = {}
            for bi in range(len(ns)):
                for i in range(hps):
                    q_pair = pair(_rows(ns[bi]), i, 0) * scale
                    do = do_ref[_rows(ns[bi]), i * LANE:(i + 1) * LANE]
                    do16 = do.astype(BF16)
                    for h in range(2):
                        if compact:
                            st_ref, head = rest[1], 2 * i + h
                            ebar = st_ref[_rows(ns[bi]), head:head + 1]
                            lcol = st_ref[_rows(ns[bi]), STAT_LSE_LANE + head:STAT_LSE_LANE + head + 1]
                        else:
                            ebar = jnp.sum(own(h, do * rest[1][_rows(ns[bi]), i * LANE:(i + 1) * LANE]), -1, keepdims=True)
                            lcol = rest[2][_rows(ns[bi]), i * LANE + h * HEAD_DIM:i * LANE + h * HEAD_DIM + 1]
                        qs[(bi, i, h)] = (own(h, q_pair), q_pair, own(h, do16), do16, ebar, lcol)
            raw = []
            for bi, i, h, sd in chains:
                q, _, do_h, _, _, _ = qs[(bi, i, h)]
                bias_blk = bias_ref[2 * i + h, :, :BAND] if sd else bias_ref[2 * i + h, :, BAND:]
                s = lax.dot_general(q, pair(key_rows(bi, sd), i, 1), NT, preferred_element_type=F32) + bias_blk
                dp = lax.dot_general(do_h, pair(key_rows(bi, sd), i, 2), NT, preferred_element_type=F32)
                raw.append((s, dp))
            soft = []
            for (bi, i, h, sd), (s, dp) in zip(chains, raw):
                ebar, lcol = qs[(bi, i, h)][4:]
                p = jnp.exp(s - lcol)
                ds = p * (dp - ebar)
                if sd:
                    db_ref[2 * i + h, :, :BAND] += ds
                else:
                    db_ref[2 * i + h, :, BAND:] += ds
                soft.append((p.astype(BF16), ds.astype(BF16)))
            grads = {}
            for (bi, i, h, sd), (p16, ds16) in zip(chains, soft):
                _, q_pair, _, do16 = qs[(bi, i, h)][:4]
                grads[(bi, i, h, sd)] = (
                    jnp.dot(ds16, pair(key_rows(bi, sd), i, 1), preferred_element_type=F32),
                    lax.dot_general(ds16, q_pair, TN, preferred_element_type=F32),
                    lax.dot_general(p16, do16, TN, preferred_element_type=F32))
            both = lambda bi, i, sd, which: jnp.where(first_head, grads[(bi, i, 0, sd)][which],
                                                      grads[(bi, i, 1, sd)][which])
            carry = list(carry) if carry is not None else None
            for bi, n in enumerate(ns):
                for i in range(hps):
                    base = 3 * LANE * i
                    dq = both(bi, i, 0, 0)
                    if with_prev:
                        dq = dq + both(bi, i, 1, 0)
                        dqkv_ref[_rows(n - 1), base + LANE:base + 2 * LANE] = (
                            carry[2 * i] + both(bi, i, 1, 1)).astype(BF16)
                        dqkv_ref[_rows(n - 1), base + 2 * LANE:base + 3 * LANE] = (
                            carry[2 * i + 1] + both(bi, i, 1, 2)).astype(BF16)
                    dqkv_ref[_rows(n), base:base + LANE] = (dq * scale).astype(BF16)
                carry = [t for i in range(hps) for t in (both(bi, i, 0, 1), both(bi, i, 0, 2))]
            return tuple(carry)

        carry = _for_blocks(blocks, nblk, 2 if hps == 1 else 1, None)
        for i in range(hps):
            base = 3 * LANE * i
            dqkv_ref[_rows(nblk - 1), base + LANE:base + 2 * LANE] = carry[2 * i].astype(BF16)
            dqkv_ref[_rows(nblk - 1), base + 2 * LANE:base + 3 * LANE] = carry[2 * i + 1].astype(BF16)

    qspec = pl.BlockSpec((None, None, sub, 3 * LANE * hps), lambda hp, b, r: (b, r, 0, hp))
    bspec = pl.BlockSpec((2 * hps, BAND, 2 * BAND), lambda hp, b, r: (hp, 0, 0))
    if compact:
        cspecs = [pl.BlockSpec((None, None, sub, ATT_OUT), lambda hp, b, r: (b, r, 0, 0)),
                  pl.BlockSpec((None, None, sub, LANE), lambda hp, b, r: (b, r, 0, 0))]
    else:
        cspecs = [pl.BlockSpec((None, sub, hps * LANE), lambda hp, b, r: (b, 0, r * (npair // hps) + hp))] * 3
    return _pcall(
        body, name=name, grid=(npair // hps, nb, d),
        in_specs=[qspec, bspec] + cspecs, out_specs=[qspec, bspec],
        out_shape=[jax.ShapeDtypeStruct(qkv4.shape, BF16),
                   jax.ShapeDtypeStruct((GROUP_HEADS, BAND, 2 * BAND), F32)],
        compiler_params=_params("parallel", "arbitrary", "arbitrary"),
    )(qkv4, bias, *cotangent)


def _head_lanes(first_lane, one_channel):
    c = lax.broadcasted_iota(jnp.int32, (ATT_OUT, LANE), 0)
    lane = lax.broadcasted_iota(jnp.int32, (ATT_OUT, LANE), 1)
    hit = lane == first_lane + c // HEAD_DIM
    if one_channel:
        hit = hit & (c % HEAD_DIM == 0)
    return hit.astype(BF16)


def _exact_dot(v, m01, dims=None):
    parts = _split3(v)
    if dims is None:
        dot = lambda t: jnp.dot(t, m01, preferred_element_type=F32)
    else:
        dot = lambda t: lax.dot_general(t, m01, dims, preferred_element_type=F32)
    return (dot(parts[0]) + dot(parts[1])) + dot(parts[2])


def _combine_fwd(o0, l0, o1, l1, o2, l2, gatt):
    nb, seq, _ = gatt.shape
    tm = 512

    def body(o0_ref, l0_ref, o1_ref, l1_ref, o2_ref, l2_ref, g_ref, oa_ref, oatt_ref, lse_ref):
        spread = _head_lanes(0, False)
        l0v = l0_ref[...]
        l1v = _exact_dot(l1_ref[...], spread, NT)
        l2v = _exact_dot(l2_ref[...], spread, NT)
        m = jnp.maximum(jnp.maximum(l0v, l1v), l2v)
        tot = m + jnp.log(jnp.exp(l0v - m) + jnp.exp(l1v - m) + jnp.exp(l2v - m))
        o = (jnp.exp(l0v - tot) * o0_ref[...] + jnp.exp(l1v - tot) * o1_ref[...]
             + jnp.exp(l2v - tot) * o2_ref[...])
        g = g_ref[...]
        oa_ref[...] = (o * (g * _sigmoid(g))).astype(BF16)
        oatt_ref[...] = o
        lse_ref[...] = tot

    spec = pl.BlockSpec((None, tm, ATT_OUT), lambda b, i: (b, i, 0))
    lspec = pl.BlockSpec((None, tm, LANE), lambda b, i: (b, i, 0))
    return _pcall(
        body, name="attn_combine", grid=(nb, seq // tm),
        in_specs=[spec, spec, spec, lspec, spec, lspec, spec], out_specs=[spec] * 3,
        out_shape=[jax.ShapeDtypeStruct((nb, seq, ATT_OUT), BF16), jax.ShapeDtypeStruct((nb, seq, ATT_OUT), F32),
                   jax.ShapeDtypeStruct((nb, seq, ATT_OUT), F32)],
        compiler_params=_params("parallel", "parallel"),
    )(o0, l0, o1, l1, o2, l2, gatt)


def _combine_bwd(doa, gatt, o_att, lse):
    nb, seq, _ = gatt.shape
    tm = 512

    def body(doa_ref, g_ref, o_ref, l_ref, do_ref, do16_ref, st_ref, dg_ref):
        g = g_ref[...]
        sg = _sigmoid(g)
        do = doa_ref[...] * (g * sg)
        do_ref[...] = do
        do16_ref[...] = do.astype(BF16)
        st_ref[...] = (_exact_dot(do * o_ref[...], _head_lanes(0, False))
                       + _exact_dot(l_ref[...], _head_lanes(STAT_LSE_LANE, True)))
        dg_ref[...] = (doa_ref[...] * o_ref[...] * (sg * (1.0 + g * (1.0 - sg)))).astype(BF16)

    spec = pl.BlockSpec((None, tm, ATT_OUT), lambda b, i: (b, i, 0))
    lspec = pl.BlockSpec((None, tm, LANE), lambda b, i: (b, i, 0))
    return _pcall(
        body, name="attn_combine_bwd", grid=(nb, seq // tm), in_specs=[spec] * 4,
        out_specs=[spec, spec, lspec, spec],
        out_shape=[jax.ShapeDtypeStruct((nb, seq, ATT_OUT), F32), jax.ShapeDtypeStruct((nb, seq, ATT_OUT), BF16),
                   jax.ShapeDtypeStruct((nb, seq, LANE), F32), jax.ShapeDtypeStruct((nb, seq, ATT_OUT), BF16)],
        compiler_params=_params("parallel", "parallel"),
    )(doa, gatt, o_att, lse)


CONV_TM = 512
CONV_TC = 512


def _shift_down(cur, halo, k):
    rolled = pltpu.roll(cur, k, 0)
    hro = pltpu.roll(halo, k, 0)
    row = lax.broadcasted_iota(jnp.int32, hro.shape, 0)
    return jnp.concatenate([jnp.where(row < k, hro, rolled[:8]), rolled[8:]], axis=0)


def _shift_up(cur, halo, k):
    n = cur.shape[0]
    rolled = pltpu.roll(cur, n - k, 0)
    hro = pltpu.roll(halo, 8 - k, 0)
    row = lax.broadcasted_iota(jnp.int32, hro.shape, 0)
    return jnp.concatenate([rolled[:n - 8], jnp.where(row >= 8 - k, hro, rolled[n - 8:])], axis=0)


def _conv_pre(cur, halo, w_ref, b_ref):
    acc = cur * w_ref[3:4, :] + b_ref[...]
    for k in range(1, 4):
        acc = acc + _shift_down(cur, halo, k) * w_ref[3 - k:4 - k, :]
    return acc


def _conv_specs(seq):
    nblk = seq // CONV_TM
    cur = pl.BlockSpec((None, CONV_TM, CONV_TC), lambda cb, b, i: (b, i, cb))
    prev = pl.BlockSpec((None, 8, CONV_TC), lambda cb, b, i: (b, jnp.maximum(i * (CONV_TM // 8) - 1, 0), cb))
    nxt = pl.BlockSpec((None, 8, CONV_TC),
                       lambda cb, b, i: (b, jnp.minimum((i + 1) * (CONV_TM // 8), seq // 8 - 1), cb))
    wspec = pl.BlockSpec((4, CONV_TC), lambda cb, b, i: (0, cb))
    bspec = pl.BlockSpec((1, CONV_TC), lambda cb, b, i: (0, cb))
    return nblk, cur, prev, nxt, wspec, bspec


def _conv_fwd(xin, w4, bias, name):
    nb, seq, ch = xin.shape
    _, cur, prev, _, wspec, bspec = _conv_specs(seq)

    def body(x_ref, h_ref, w_ref, b_ref, o_ref):
        halo = jnp.where(pl.program_id(2) > 0, h_ref[...], 0.0)
        pre = _conv_pre(x_ref[...], halo, w_ref, b_ref)
        o_ref[...] = pre * _sigmoid(pre)

    return _pcall(
        body, name=name, grid=(ch // CONV_TC, nb, seq // CONV_TM),
        in_specs=[cur, prev, wspec, bspec], out_specs=cur,
        out_shape=jax.ShapeDtypeStruct(xin.shape, F32),
        compiler_params=_params("parallel", "parallel", "parallel"),
    )(xin, xin, w4, bias)


def _conv_bwd_pre(dact, xin, w4, bias, name):
    nb, seq, ch = xin.shape
    _, cur, prev, _, wspec, bspec = _conv_specs(seq)

    def body(da_ref, x_ref, h_ref, w_ref, b_ref, dp_ref, s_ref):
        b, i = pl.program_id(1), pl.program_id(2)

        @pl.when((b == 0) & (i == 0))
        def _():
            s_ref[...] = jnp.zeros_like(s_ref)

        halo = jnp.where(i > 0, h_ref[...], 0.0)
        x = x_ref[...]
        pre = _conv_pre(x, halo, w_ref, b_ref)
        sg = _sigmoid(pre)
        dpre = da_ref[...] * (sg * (1.0 + pre * (1.0 - sg)))
        dp_ref[...] = dpre
        s_ref[3:4, :] += jnp.sum(dpre * x, 0, keepdims=True)
        for k in range(1, 4):
            s_ref[3 - k:4 - k, :] += jnp.sum(dpre * _shift_down(x, halo, k), 0, keepdims=True)
        s_ref[4:5, :] += jnp.sum(dpre, 0, keepdims=True)

    return _pcall(
        body, name=name, grid=(ch // CONV_TC, nb, seq // CONV_TM),
        in_specs=[cur, cur, prev, wspec, bspec],
        out_specs=[cur, pl.BlockSpec((8, CONV_TC), lambda cb, b, i: (0, cb))],
        out_shape=[jax.ShapeDtypeStruct(xin.shape, F32), jax.ShapeDtypeStruct((8, ch), F32)],
        compiler_params=_params("parallel", "arbitrary", "arbitrary"),
    )(dact, xin, xin, w4, bias)


def _conv_bwd_x(dpre, w4, name):
    nb, seq, ch = dpre.shape
    nblk, cur, _, nxt, wspec, _ = _conv_specs(seq)

    def body(d_ref, n_ref, w_ref, o_ref):
        halo = jnp.where(pl.program_id(2) < nblk - 1, n_ref[...], 0.0)
        cur_v = d_ref[...]
        acc = cur_v * w_ref[3:4, :]
        for j in range(1, 4):
            acc = acc + _shift_up(cur_v, halo, j) * w_ref[3 - j:4 - j, :]
        o_ref[...] = acc.astype(BF16)

    return _pcall(
        body, name=name, grid=(ch // CONV_TC, nb, seq // CONV_TM),
        in_specs=[cur, nxt, wspec], out_specs=cur,
        out_shape=jax.ShapeDtypeStruct(dpre.shape, BF16),
        compiler_params=_params("parallel", "parallel", "parallel"),
    )(dpre, dpre, w4)


def _softplus_sig(dt_raw, dt_bias_row):
    nb, seq, _ = dt_raw.shape
    tm = 512

    def body(r_ref, b_ref, sp_ref, sg_ref):
        v = r_ref[...] + b_ref[...]
        sp_ref[...] = jnp.maximum(v, 0.0) + jnp.log1p(jnp.exp(-jnp.abs(v)))
        sg_ref[...] = _sigmoid(v)

    spec = pl.BlockSpec((None, tm, LANE), lambda b, i: (b, i, 0))
    return _pcall(
        body, name="dt_softplus", grid=(nb, seq // tm),
        in_specs=[spec, pl.BlockSpec((1, LANE), lambda b, i: (0, 0))], out_specs=[spec, spec],
        out_shape=[jax.ShapeDtypeStruct(dt_raw.shape, F32)] * 2,
        compiler_params=_params("parallel", "parallel"),
    )(dt_raw, dt_bias_row)


def _group_lanes(t):
    pads = [(0, 0)] * (t.ndim - 1) + [(0, LANE - GROUP_SSM_HEADS)]
    return jnp.stack([jnp.pad(t[..., GROUP_SSM_HEADS * g:GROUP_SSM_HEADS * (g + 1)], pads) for g in range(SSM_GROUPS)])


def _ungroup_lanes(t):
    return jnp.concatenate([t[g][..., :GROUP_SSM_HEADS] for g in range(SSM_GROUPS)], axis=-1)


def _decays(dt, al_ref):
    row = lax.broadcasted_iota(jnp.int32, (CHUNK, CHUNK), 0)
    col = lax.broadcasted_iota(jnp.int32, (CHUNK, CHUNK), 1)
    tril = (row >= col).astype(BF16)
    triu = (row <= col).astype(BF16)
    arow = -jnp.exp(al_ref[...])
    hi, mid, lo = _split3(dt * arow)
    down = lambda t: jnp.dot(tril, t, preferred_element_type=F32)
    across = lambda t: lax.dot_general(t, triu, TN, preferred_element_type=F32)
    acs = (down(hi) + down(mid)) + down(lo)
    acs_t = (across(hi) + across(mid)) + across(lo)
    return arow, acs, acs_t, row >= col, triu


def _ssd_specs(nb, seq):
    nc = seq // CHUNK
    hw = GROUP_SSM_HEADS * HEAD_DIM

    def mk(rev):
        cidx = (lambda c: nc - 1 - c) if rev else (lambda c: c)
        wide = pl.BlockSpec((None, CHUNK, hw), lambda g, b, c: (b, cidx(c), g))
        xbc = pl.BlockSpec((None, CHUNK, XBC_GROUP), lambda g, b, c: (b, cidx(c), g))
        lanes = pl.BlockSpec((None, None, CHUNK, LANE), lambda g, b, c: (g, b, cidx(c), 0))
        prev = pl.BlockSpec((None, None, None, D_STATE, hw), lambda g, b, c: (b, cidx(c), g, 0, 0))
        return wide, xbc, lanes, prev

    grow = pl.BlockSpec((None, 1, LANE), lambda g, b, c: (g, 0, 0))
    nwspec = pl.BlockSpec((1, hw), lambda g, b, c: (0, g))
    return nc, hw, mk, grow, nwspec


def _head_expand():
    hw = GROUP_SSM_HEADS * HEAD_DIM
    r = lax.broadcasted_iota(jnp.int32, (LANE, hw), 0)
    c = lax.broadcasted_iota(jnp.int32, (LANE, hw), 1)
    return ((c // HEAD_DIM) == r).astype(BF16)


def _split3(v):
    hi = v.astype(BF16)
    rest = v - hi.astype(F32)
    mid = rest.astype(BF16)
    return hi, mid, (rest - mid.astype(F32)).astype(BF16)


def _to_channels(v, e):
    hi, mid, lo = _split3(v)
    dot = lambda t: jnp.dot(t, e, preferred_element_type=F32)
    return (dot(hi) + dot(mid)) + dot(lo)


def _to_heads(w, e):
    hi, mid, lo = _split3(w)
    dot = lambda t: lax.dot_general(t, e, (((1,), (1,)), ((), ())), preferred_element_type=F32)
    return (dot(hi) + dot(mid)) + dot(lo)


def _row8(v):
    return jnp.broadcast_to(v, (8, v.shape[1]))


def _ssd_chunk_setup(dt, al_ref, ds_ref):
    arow, acs, acs_t, causal, triu = _decays(dt, al_ref)
    e = _head_expand()
    dtx = _to_channels(dt, e)
    acsx = _to_channels(acs, e)
    lastx = acsx[CHUNK - 1:CHUNK, :]
    dskx = _to_channels(_row8(ds_ref[...]), e)[0:1, :]
    return arow, acs, acs_t, causal, triu, e, dtx, acsx, lastx, dskx


def _ssd_fwd(xbc, dtg, z, alog_g, dskip_g, normw):
    nb, seq, _ = xbc.shape
    nc, hw, mk, grow, nwspec = _ssd_specs(nb, seq)
    wide, xbc_spec, lanes, prev = mk(False)
    tn = (((0,), (0,)), ((), ()))

    def body(xbc_ref, dt_ref, z_ref, al_ref, ds_ref, nw_ref, ys_ref, y_ref, sp_ref, st_ref):
        @pl.when(pl.program_id(2) == 0)
        def _():
            st_ref[...] = jnp.zeros_like(st_ref)

        dt = dt_ref[...]
        _, acs, acs_t, causal, _, _, dtx, acsx, lastx, dskx = _ssd_chunk_setup(dt, al_ref, ds_ref)
        bmat = xbc_ref[:, GROUP_CH:GROUP_CH + D_STATE].astype(BF16)
        cmat = xbc_ref[:, GROUP_CH + D_STATE:].astype(BF16)
        cb = lax.dot_general(cmat, bmat, (((1,), (1,)), ((), ())), preferred_element_type=F32)
        x = xbc_ref[:, :GROUP_CH]
        xdt = x * dtx
        xdt16 = xdt.astype(BF16)
        first_head = lax.broadcasted_iota(jnp.int32, (CHUNK, LANE), 1) < HEAD_DIM
        pairs = []
        for hp in range(GROUP_SSM_HEADS // 2):
            xp = xdt16[:, hp * LANE:(hp + 1) * LANE]
            two = []
            for j in (2 * hp, 2 * hp + 1):
                lmat = jnp.exp(jnp.where(causal, acs[:, j:j + 1] - acs_t[j:j + 1, :], -jnp.inf))
                two.append(jnp.dot((cb * lmat).astype(BF16), xp, preferred_element_type=F32))
            pairs.append(jnp.where(first_head, two[0], two[1]))
        yd = jnp.concatenate(pairs, axis=1)
        s_prev = st_ref[...]
        s16 = s_prev.astype(BF16)
        sp_ref[...] = s16
        yo = jnp.dot(cmat, s16, preferred_element_type=F32) * jnp.exp(acsx)
        sts = lax.dot_general(bmat, (xdt * jnp.exp(lastx - acsx)).astype(BF16), tn, preferred_element_type=F32)
        st_ref[...] = s_prev * jnp.exp(lastx) + sts
        y = yd + yo + dskx * x
        zz = z_ref[...]
        u = y * (zz * _sigmoid(zz))
        rn = lax.rsqrt(jnp.mean(u * u, -1, keepdims=True) + RMS_EPS)
        ys_ref[...] = (u * rn * nw_ref[...]).astype(BF16)
        y_ref[...] = y

    return _pcall(
        body, name="ssd_fwd", grid=(SSM_GROUPS, nb, nc),
        in_specs=[xbc_spec, lanes, wide, grow, grow, nwspec],
        out_specs=[wide, wide, prev],
        out_shape=[jax.ShapeDtypeStruct((nb, seq, D_INNER), BF16), jax.ShapeDtypeStruct((nb, seq, D_INNER), F32),
                   jax.ShapeDtypeStruct((nb, nc, SSM_GROUPS, D_STATE, hw), BF16)],
        scratch_shapes=[pltpu.VMEM((D_STATE, hw), F32)],
        compiler_params=_params("parallel", "parallel", "arbitrary"),
    )(xbc, dtg, z, alog_g, dskip_g, normw)


def _ssd_bwd(xbc, dtg, sgg, z, y, dys, sprev, alog_g, dskip_g, normw):
    nb, seq, _ = xbc.shape
    nc, hw, mk, grow, nwspec = _ssd_specs(nb, seq)
    wide, xbc_spec, lanes, prev = mk(True)
    nt = (((1,), (1,)), ((), ()))
    tn = (((0,), (0,)), ((), ()))

    def body(xbc_ref, dt_ref, sg_ref, z_ref, y_ref, dys_ref, sp_ref, al_ref, ds_ref, nw_ref,
             dxbc_ref, ddt_ref, dz_ref, small_ref, dnw_ref, g_ref):
        b, c = pl.program_id(1), pl.program_id(2)

        @pl.when((b == 0) & (c == 0))
        def _():
            small_ref[...] = jnp.zeros_like(small_ref)
            dnw_ref[...] = jnp.zeros_like(dnw_ref)

        @pl.when(c == 0)
        def _():
            g_ref[...] = jnp.zeros_like(g_ref)

        yv, zz, dys_v, nw = y_ref[...], z_ref[...], dys_ref[...], nw_ref[...]
        sz = _sigmoid(zz)
        silu = zz * sz
        u = yv * silu
        rn = lax.rsqrt(jnp.mean(u * u, -1, keepdims=True) + RMS_EPS)
        gn = dys_v * nw
        du = rn * gn - u * (rn * rn * rn) * jnp.mean(u * gn, -1, keepdims=True)
        dnw_ref[...] += jnp.sum(dys_v * u * rn, 0, keepdims=True)
        dy = du * silu
        dz_ref[...] = du * yv * (sz * (1.0 + zz * (1.0 - sz)))

        dt = dt_ref[...]
        arow, acs, acs_t, causal, triu, e, dtx, acsx, lastx, dskx = _ssd_chunk_setup(dt, al_ref, ds_ref)
        dfsx = jnp.exp(acsx)
        dtex = jnp.exp(lastx - acsx)
        bmat = xbc_ref[:, GROUP_CH:GROUP_CH + D_STATE].astype(BF16)
        cmat = xbc_ref[:, GROUP_CH + D_STATE:].astype(BF16)
        cb = lax.dot_general(cmat, bmat, nt, preferred_element_type=F32)
        x = xbc_ref[:, :GROUP_CH]
        xdt = x * dtx
        xdt16 = xdt.astype(BF16)
        xdte = xdt * dtex
        dy16 = dy.astype(BF16)
        dyd = dy * dfsx
        dyd16 = dyd.astype(BF16)
        s16 = sp_ref[...]
        g = g_ref[...]
        g16 = g.astype(BF16)
        cs = jnp.dot(cmat, s16, preferred_element_type=F32)
        dc_off = lax.dot_general(dyd16, s16, nt, preferred_element_type=F32)
        g_here = lax.dot_general(cmat, dyd16, tn, preferred_element_type=F32)
        bg = jnp.dot(bmat, g16, preferred_element_type=F32)
        db_st = lax.dot_general(xdte.astype(BF16), g16, nt, preferred_element_type=F32)
        ddte_w = bg * xdte
        dcd = _to_heads(_row8(jnp.sum(g * s16.astype(F32), 0, keepdims=True)), e)[0:1, :]
        lane = lax.broadcasted_iota(jnp.int32, (CHUNK, LANE), 1)
        first_head = lane < HEAD_DIM
        sub = lax.broadcasted_iota(jnp.int32, (CHUNK, LANE), 0)
        dacs = jnp.zeros((CHUNK, LANE), F32)
        colsums = jnp.zeros((CHUNK, LANE), F32)
        dcb = jnp.zeros((CHUNK, CHUNK), F32)
        pairs = []
        for hp in range(GROUP_SSM_HEADS // 2):
            xp = xdt16[:, hp * LANE:(hp + 1) * LANE]
            dyp = dy16[:, hp * LANE:(hp + 1) * LANE]
            two = []
            for idx, j in enumerate((2 * hp, 2 * hp + 1)):
                lmat = jnp.exp(jnp.where(causal, acs[:, j:j + 1] - acs_t[j:j + 1, :], -jnp.inf))
                mf = cb * lmat
                dy_h = jnp.where(first_head if idx == 0 else jnp.logical_not(first_head), dyp, jnp.zeros_like(dyp))
                dm = lax.dot_general(dy_h, xp, nt, preferred_element_type=F32)
                two.append(lax.dot_general(mf.astype(BF16), dyp, tn, preferred_element_type=F32))
                wmat = dm * mf
                dcb = dcb + dm * lmat
                dacs = jnp.where(lane == j, jnp.sum(wmat, -1, keepdims=True), dacs)
                colsums = jnp.where(sub == j, jnp.sum(wmat, 0, keepdims=True), colsums)
            pairs.append(jnp.where(first_head, two[0], two[1]))
        dxdt = bg * dtex + jnp.concatenate(pairs, axis=1)
        dacs = dacs - colsums.T + _to_heads(dyd * cs - ddte_w, e)
        cd_row = jnp.exp(acs[CHUNK - 1:CHUNK, :])
        tail = _to_heads(_row8(jnp.sum(ddte_w, 0, keepdims=True)), e)[0:1, :] + dcd * cd_row
        dacs = dacs + jnp.where(sub == CHUNK - 1, tail, 0.0)
        d_hi, d_mid, d_lo = _split3(dacs)
        up = lambda t: jnp.dot(triu, t, preferred_element_type=F32)
        da = (up(d_hi) + up(d_mid)) + up(d_lo)
        ddt_raw = (da * arow + _to_heads(dxdt * x, e)) * sg_ref[...]
        ddt_ref[...] = ddt_raw
        small_ref[0:1, :] += jnp.sum(da * dt, 0, keepdims=True) * arow
        small_ref[1:2, :] += _to_heads(_row8(jnp.sum(dy * x, 0, keepdims=True)), e)[0:1, :]
        small_ref[2:3, :] += jnp.sum(ddt_raw, 0, keepdims=True)
        dcb16 = dcb.astype(BF16)
        dxbc_ref[:, GROUP_CH + D_STATE:] = dc_off + jnp.dot(dcb16, bmat, preferred_element_type=F32)
        dxbc_ref[:, GROUP_CH:GROUP_CH + D_STATE] = db_st + lax.dot_general(dcb16, cmat, tn,
                                                                            preferred_element_type=F32)
        dxbc_ref[:, :GROUP_CH] = dxdt * dtx + dskx * dy
        g_ref[...] = g * jnp.exp(lastx) + g_here

    return _pcall(
        body, name="ssd_bwd", grid=(SSM_GROUPS, nb, nc),
        in_specs=[xbc_spec, lanes, lanes, wide, wide, wide, prev, grow, grow, nwspec],
        out_specs=[xbc_spec, lanes, wide,
                   pl.BlockSpec((None, 8, LANE), lambda g, b, c: (g, 0, 0)), nwspec],
        out_shape=[jax.ShapeDtypeStruct((nb, seq, CONV_DIM), F32),
                   jax.ShapeDtypeStruct((SSM_GROUPS, nb, seq, LANE), F32),
                   jax.ShapeDtypeStruct((nb, seq, D_INNER), F32),
                   jax.ShapeDtypeStruct((SSM_GROUPS, 8, LANE), F32),
                   jax.ShapeDtypeStruct((1, D_INNER), F32)],
        scratch_shapes=[pltpu.VMEM((D_STATE, hw), F32)],
        compiler_params=_params("parallel", "arbitrary", "arbitrary"),
    )(xbc, dtg, sgg, z, y, dys, sprev, alog_g, dskip_g, normw)


EW_TM = 256


def _merge_fwd(y_a, y_b, gm, bgate):
    nb, seq, _ = y_a.shape

    def body(a_ref, b_ref, ga_ref, gb_ref, bg_ref, o_ref):
        sa = _sigmoid(ga_ref[...] + bg_ref[0:1, :])
        sb = _sigmoid(gb_ref[...] + bg_ref[1:2, :])
        o_ref[...] = (sa * a_ref[...] + sb * b_ref[...]).astype(BF16)

    spec = pl.BlockSpec((None, EW_TM, D_MODEL), lambda b, i: (b, i, 0))
    spec1 = pl.BlockSpec((None, EW_TM, D_MODEL), lambda b, i: (b, i, 1))
    return _pcall(
        body, name="merge_fwd", grid=(nb, seq // EW_TM),
        in_specs=[spec, spec, spec, spec1, pl.BlockSpec((8, D_MODEL), lambda b, i: (0, 0))], out_specs=spec,
        out_shape=jax.ShapeDtypeStruct((nb, seq, D_MODEL), BF16),
        compiler_params=_params("parallel", "parallel"),
    )(y_a, y_b, gm, gm, bgate)


def _merge_bwd(dmerged, y_a, y_b, gm, bgate):
    nb, seq, _ = y_a.shape

    def body(dm_ref, a_ref, b_ref, ga_ref, gb_ref, bg_ref, dya_ref, dyb_ref, dg_ref, s_ref):
        @pl.when((pl.program_id(0) == 0) & (pl.program_id(1) == 0))
        def _():
            s_ref[...] = jnp.zeros_like(s_ref)

        dm = dm_ref[...]
        sa = _sigmoid(ga_ref[...] + bg_ref[0:1, :])
        sb = _sigmoid(gb_ref[...] + bg_ref[1:2, :])
        dya_ref[...] = (dm * sa).astype(BF16)
        dyb_ref[...] = (dm * sb).astype(BF16)
        dga = dm * a_ref[...] * (sa * (1.0 - sa))
        dgb = dm * b_ref[...] * (sb * (1.0 - sb))
        dg_ref[:, :D_MODEL] = dga.astype(BF16)
        dg_ref[:, D_MODEL:] = dgb.astype(BF16)
        s_ref[0:1, :] += jnp.sum(dga, 0, keepdims=True)
        s_ref[1:2, :] += jnp.sum(dgb, 0, keepdims=True)

    spec = pl.BlockSpec((None, EW_TM, D_MODEL), lambda b, i: (b, i, 0))
    spec1 = pl.BlockSpec((None, EW_TM, D_MODEL), lambda b, i: (b, i, 1))
    small = pl.BlockSpec((8, D_MODEL), lambda b, i: (0, 0))
    return _pcall(
        body, name="merge_bwd", grid=(nb, seq // EW_TM),
        in_specs=[spec, spec, spec, spec, spec1, small],
        out_specs=[spec, spec, pl.BlockSpec((None, EW_TM, 2 * D_MODEL), lambda b, i: (b, i, 0)), small],
        out_shape=[jax.ShapeDtypeStruct((nb, seq, D_MODEL), BF16), jax.ShapeDtypeStruct((nb, seq, D_MODEL), BF16),
                   jax.ShapeDtypeStruct((nb, seq, 2 * D_MODEL), BF16), jax.ShapeDtypeStruct((8, D_MODEL), F32)],
        compiler_params=_params("arbitrary", "arbitrary"),
    )(dmerged, y_a, y_b, gm, gm, bgate)


def _ln_loss(x, mix, gp, pw, target, bgate, ln_g, ln_b):
    nb, seq, _ = x.shape

    def body(x_ref, mix_ref, gp_ref, pw_ref, t_ref, bg_ref, g_ref, b_ref, dx_ref, dp_ref, dpw_ref, dgp_ref, s_ref):
        @pl.when((pl.program_id(0) == 0) & (pl.program_id(1) == 0))
        def _():
            s_ref[...] = jnp.zeros_like(s_ref)

        sp = _sigmoid(gp_ref[...] + bg_ref[2:3, :])
        pw = pw_ref[...]
        pre = ALPHA * x_ref[...] + mix_ref[...] + sp * pw
        mu = jnp.mean(pre, -1, keepdims=True)
        cen = pre - mu
        rstd = lax.rsqrt(jnp.mean(cen * cen, -1, keepdims=True) + LN_EPS)
        xhat = cen * rstd
        err = xhat * g_ref[...] + b_ref[...] - t_ref[...]
        dy = err * (1.0 / D_MODEL)
        dxh = dy * g_ref[...]
        dpre = rstd * (dxh - jnp.mean(dxh, -1, keepdims=True) - xhat * jnp.mean(dxh * xhat, -1, keepdims=True))
        dx_ref[...] = ALPHA * dpre
        dp_ref[...] = dpre.astype(BF16)
        dpw_ref[...] = (dpre * sp).astype(BF16)
        dgp = dpre * pw * (sp * (1.0 - sp))
        dgp_ref[...] = dgp.astype(BF16)
        s_ref[0:1, :] += jnp.sum(dy * xhat, 0, keepdims=True)
        s_ref[1:2, :] += jnp.sum(dy, 0, keepdims=True)
        s_ref[2:3, :] += jnp.sum(dgp, 0, keepdims=True)
        s_ref[3:4, :] += jnp.sum(err * err, 0, keepdims=True)

    spec = pl.BlockSpec((None, EW_TM, D_MODEL), lambda b, i: (b, i, 0))
    small = pl.BlockSpec((8, D_MODEL), lambda b, i: (0, 0))
    row = pl.BlockSpec((1, D_MODEL), lambda b, i: (0, 0))
    return _pcall(
        body, name="ln_loss", grid=(nb, seq // EW_TM),
        in_specs=[spec] * 5 + [small, row, row], out_specs=[spec] * 4 + [small],
        out_shape=[jax.ShapeDtypeStruct((nb, seq, D_MODEL), F32)] + [jax.ShapeDtypeStruct((nb, seq, D_MODEL), BF16)] * 3
        + [jax.ShapeDtypeStruct((8, D_MODEL), F32)],
        compiler_params=_params("arbitrary", "arbitrary"),
    )(x, mix, gp, pw, target, bgate, ln_g, ln_b)


def _adamw(w, g, m, v, name):
    rows, cols = w.shape
    tr = _row_tile(rows, cols, 8, 5 << 19)
    c1 = 1.0 - ADAM_B1 ** ADAM_STEP
    c2 = 1.0 - ADAM_B2 ** ADAM_STEP

    def body(w_ref, g_ref, m_ref, v_ref, d_ref, nm_ref, nv_ref):
        gv = g_ref[...]
        nm = ADAM_B1 * m_ref[...] + (1.0 - ADAM_B1) * gv
        nv = ADAM_B2 * v_ref[...] + (1.0 - ADAM_B2) * (gv * gv)
        d_ref[...] = -ADAM_LR * ((nm / c1) / (jnp.sqrt(nv / c2) + ADAM_EPS) + ADAM_WD * w_ref[...])
        nm_ref[...] = nm
        nv_ref[...] = nv

    spec = pl.BlockSpec((tr, cols), lambda i: (i, 0))
    return _pcall(
        body, name=name, grid=(rows // tr,), in_specs=[spec] * 4, out_specs=[spec] * 3,
        out_shape=[jax.ShapeDtypeStruct(w.shape, F32)] * 3, compiler_params=_params("parallel"),
    )(w, g, m, v)


def _sum_rows(parts, out_dtype, name):
    rows, cols = parts[0].shape
    tr = rows
    for cand in range(16, rows, 16):
        if rows % cand == 0 and cand * cols * 4 <= (1 << 20):
            tr = cand
    n = len(parts)

    def body(*refs):
        acc = refs[0][...].astype(F32)
        for r in refs[1:n]:
            acc = acc + r[...].astype(F32)
        refs[n][...] = acc.astype(out_dtype)

    spec = pl.BlockSpec((tr, cols), lambda i: (i, 0))
    return _pcall(
        body, name=name, grid=(rows // tr,), in_specs=[spec] * n, out_specs=spec,
        out_shape=jax.ShapeDtypeStruct((rows, cols), out_dtype), compiler_params=_params("parallel"),
    )(*parts)


def _place():
    return lax.axis_index("x"), lax.axis_index("y"), lax.axis_index("c")


def _other_chips(x, y):
    return [(1 - x, y), (x, 1 - y), (1 - x, 1 - y)]


def _remote(src, dst, send_sem, recv_sem, to):
    return pltpu.make_async_remote_copy(src_ref=src, dst_ref=dst, send_sem=send_sem, recv_sem=recv_sem,
                                        device_id=to, device_id_type=MESH)


ANY = pl.BlockSpec(memory_space=pl.ANY)
DMA_CHUNK_BYTES = 512 * 1024


def _row_chunks(rows, row_bytes):
    per = max(16, DMA_CHUNK_BYTES // row_bytes // 16 * 16)
    return [(s, min(per, rows - s)) for s in range(0, rows, per)]


def _row_tile(rows, cols, align, limit=1 << 21):
    best = None
    for cand in range(align, rows + 1, align):
        if rows % cand == 0 and cand * cols * 4 <= limit:
            best = cand
    return best or rows


def _allgather_pieces(pieces):
    n = len(pieces)
    halves = [_row_chunks(p.shape[0] // 2, p.shape[1] * p.dtype.itemsize) for p in pieces]
    entries = [(a, q, s, m, j) for a in range(n) for q, (s, m) in enumerate(halves[a]) for j in range(3)]
    slot = {(a, q, j): k for k, (a, q, _, _, j) in enumerate(entries)}
    n_ici = len(entries)

    def body(*refs):
        ins, outs = refs[:n], refs[n:2 * n]
        send_sems, recv_sems = refs[2 * n:]
        x, y, c = _place()
        me = 2 * x + y
        sibling = (x, y, 1 - c)
        chips = _other_chips(x, y)

        def landed(a, s, m, j, core):
            half = ins[a].shape[0] // 2
            return outs[a].at[2 * chips[j][0] + chips[j][1], pl.ds(core * half + s, m)]

        sent = []
        for k, (a, q, s, m, j) in enumerate(entries):
            if j < 2:
                half = ins[a].shape[0] // 2
                cp = _remote(ins[a].at[pl.ds(c * half + s, m)], outs[a].at[me, pl.ds(c * half + s, m)],
                             send_sems.at[k], recv_sems.at[k], (*chips[j], c))
                cp.start()
                sent.append(cp)

        def pass_to_sibling(k, blk):
            fw = _remote(blk, blk, send_sems.at[n_ici + k], recv_sems.at[n_ici + k], sibling)
            fw.start()
            sent.append(fw)

        for k, (a, q, s, m, j) in enumerate(entries):
            if j < 2:
                blk = landed(a, s, m, j, c)
                _remote(blk, blk, send_sems.at[k], recv_sems.at[k], (*chips[j], c)).wait_recv()
                first = q < (len(halves[a]) + 1) // 2
                if (j == 0) == first:
                    on = slot[(a, q, 2)]
                    rl = _remote(blk, blk, send_sems.at[on], recv_sems.at[on], (*chips[1 - j], c))
                    rl.start()
                    sent.append(rl)
                pass_to_sibling(k, blk)
        for k, (a, q, s, m, j) in enumerate(entries):
            if j == 2:
                blk = landed(a, s, m, j, c)
                _remote(blk, blk, send_sems.at[k], recv_sems.at[k], (*chips[j], c)).wait_recv()
                pass_to_sibling(k, blk)
        for k, (a, q, s, m, j) in enumerate(entries):
            blk = landed(a, s, m, j, 1 - c)
            _remote(blk, blk, send_sems.at[n_ici + k], recv_sems.at[n_ici + k], sibling).wait_recv()
        for cp in sent:
            cp.wait_send()

    gathered = _pcall(
        body, name="allgather_weights", in_specs=[ANY] * n, out_specs=[ANY] * n,
        out_shape=[jax.ShapeDtypeStruct((4,) + p.shape, p.dtype) for p in pieces],
        scratch_shapes=[pltpu.SemaphoreType.DMA((2 * n_ici,)), pltpu.SemaphoreType.DMA((2 * n_ici,))],
        compiler_params=pltpu.CompilerParams(has_side_effects=True),
    )(*pieces)
    x, y, _ = _place()
    return [lax.dynamic_update_slice(g, p[None], (2 * x + y, 0, 0)) for g, p in zip(gathered, pieces)]


def _sibling_exchange(grads):
    n = len(grads)
    chunks = [_row_chunks(g.shape[1] // 2, g.shape[2] * g.dtype.itemsize) for g in grads]
    n_sem = 4 * sum(len(ch) for ch in chunks)

    def body(*refs):
        ins, gots = refs[:n], refs[n:2 * n]
        send_sems, recv_sems = refs[2 * n:]
        x, y, c = _place()
        sibling = (x, y, 1 - c)
        work = []
        for a in range(n):
            half = ins[a].shape[1] // 2
            for piece in range(4):
                for s, m in chunks[a]:
                    k = len(work)
                    cp = _remote(ins[a].at[piece, pl.ds((1 - c) * half + s, m)], gots[a].at[piece, pl.ds(s, m)],
                                 send_sems.at[k], recv_sems.at[k], sibling)
                    cp.start()
                    work.append(cp)
        for cp in work:
            cp.wait()

    return _pcall(
        body, name="grad_sibling_exchange", in_specs=[ANY] * n, out_specs=[ANY] * n,
        out_shape=[jax.ShapeDtypeStruct((4, g.shape[1] // 2, g.shape[2]), g.dtype) for g in grads],
        scratch_shapes=[pltpu.SemaphoreType.DMA((n_sem,)), pltpu.SemaphoreType.DMA((n_sem,))],
        compiler_params=pltpu.CompilerParams(has_side_effects=True),
    )(*grads)


def _sibling_gather(fulls):
    n = len(fulls)
    chunks = [_row_chunks(f.shape[0] // 2, f.shape[1] * f.dtype.itemsize) for f in fulls]
    n_sem = sum(len(ch) for ch in chunks)

    def body(*refs):
        outs = refs[n:2 * n]
        send_sems, recv_sems = refs[2 * n:]
        x, y, c = _place()
        sibling = (x, y, 1 - c)
        work = []
        for a in range(n):
            h = outs[a].shape[0] // 2
            for s, m in chunks[a]:
                k = len(work)
                mine = outs[a].at[pl.ds(c * h + s, m)]
                cp = _remote(mine, mine, send_sems.at[k], recv_sems.at[k], sibling)
                cp.start()
                work.append((a, s, m, cp))
        for k, (a, s, m, cp) in enumerate(work):
            h = outs[a].shape[0] // 2
            cp.wait_send()
            theirs = outs[a].at[pl.ds((1 - c) * h + s, m)]
            _remote(theirs, theirs, send_sems.at[k], recv_sems.at[k], sibling).wait_recv()

    return _pcall(
        body, name="grad_sibling_gather", in_specs=[ANY] * n, out_specs=[ANY] * n,
        out_shape=[jax.ShapeDtypeStruct(f.shape, f.dtype) for f in fulls],
        input_output_aliases={a: a for a in range(n)},
        scratch_shapes=[pltpu.SemaphoreType.DMA((n_sem,)), pltpu.SemaphoreType.DMA((n_sem,))],
        compiler_params=pltpu.CompilerParams(has_side_effects=True),
    )(*fulls)


def _pair_sum(grad, got, place, name):
    _, rows, cols = grad.shape
    half = rows // 2
    tr = _row_tile(half, cols, 16)

    def body(p_ref, a_ref, b_ref, o_ref):
        o_ref[...] = (a_ref[...].astype(F32) + b_ref[...].astype(F32)).astype(BF16)

    return _pcall(
        body, name=name,
        grid_spec=pltpu.PrefetchScalarGridSpec(
            num_scalar_prefetch=1, grid=(4, half // tr),
            in_specs=[pl.BlockSpec((None, tr, cols), lambda k, i, p: (k, p[1] * (half // tr) + i, 0)),
                      pl.BlockSpec((None, tr, cols), lambda k, i, p: (k, i, 0))],
            out_specs=pl.BlockSpec((None, tr, cols), lambda k, i, p: (k, i, 0))),
        out_shape=jax.ShapeDtypeStruct((4, half, cols), BF16),
        compiler_params=_params("parallel", "parallel"),
    )(place, grad, got)


def _chip_sum(sums, got, place, name):
    _, h, cols = sums.shape
    tr = _row_tile(h, cols, 16)

    def body(p_ref, own_ref, g0, g1, g2, o_ref):
        o_ref[...] = ((own_ref[...].astype(F32) + g0[...].astype(F32)) + g1[...].astype(F32)) + g2[...].astype(F32)

    gspec = lambda j: pl.BlockSpec((None, tr, cols), lambda i, p: (j, i, 0))
    return _pcall(
        body, name=name,
        grid_spec=pltpu.PrefetchScalarGridSpec(
            num_scalar_prefetch=1, grid=(h // tr,),
            in_specs=[pl.BlockSpec((None, tr, cols), lambda i, p: (p[0], i, 0)), gspec(0), gspec(1), gspec(2)],
            out_specs=pl.BlockSpec((tr, cols), lambda i, p: (p[1] * (h // tr) + i, 0))),
        out_shape=jax.ShapeDtypeStruct((2 * h, cols), F32),
        compiler_params=_params("parallel"),
    )(place, sums, got, got, got)


def _allgather8(buf, name):
    rows = buf.shape[0]

    def body(in_ref, out_ref, send_sems, recv_sems):
        x, y, c = _place()
        me = 4 * x + 2 * y + c
        out_ref[me] = in_ref[...]
        work = []
        for rel in range(1, 8):
            fx, fy, fc = (rel >> 2) & 1, (rel >> 1) & 1, rel & 1
            to = (x ^ fx, y ^ fy, c ^ fc)
            cp = _remote(in_ref, out_ref.at[me], send_sems.at[rel - 1], recv_sems.at[rel - 1], to)
            cp.start()
            work.append((cp, 4 * to[0] + 2 * to[1] + to[2]))
        for rel, (cp, frm) in enumerate(work):
            cp.wait_send()
            blk = out_ref.at[frm]
            _remote(blk, blk, send_sems.at[rel], recv_sems.at[rel], (x, y, c)).wait_recv()

    return _pcall(
        body, name=name, in_specs=[pl.BlockSpec(memory_space=pltpu.VMEM)],
        out_specs=pl.BlockSpec(memory_space=pltpu.VMEM),
        out_shape=jax.ShapeDtypeStruct((8, rows, LANE), F32),
        scratch_shapes=[pltpu.SemaphoreType.DMA((7,)), pltpu.SemaphoreType.DMA((7,))],
        compiler_params=pltpu.CompilerParams(has_side_effects=True),
    )(buf)


def _pack_rows(arrs):
    parts = []
    for a in arrs:
        f = a.reshape(-1).astype(F32)
        parts.append(jnp.pad(f, (0, (-f.shape[0]) % LANE)))
    flat = jnp.concatenate(parts)
    rows = -(-flat.shape[0] // LANE)
    rows8 = -(-rows // 8) * 8
    return jnp.pad(flat, (0, rows8 * LANE - flat.shape[0])).reshape(rows8, LANE)


def _unpack_rows(buf, shapes):
    flat = buf.reshape(-1)
    outs, off = [], 0
    for s in shapes:
        n = int(np.prod(s))
        outs.append(flat[off:off + n].reshape(s))
        off += -(-n // LANE) * LANE
    return outs


def _local_grads(x, p, target, wseg, w_br16, w_out16, w_ple16, b_gate, conv_w, conv_b, dt_bias, a_log, d_skip,
                 ssm_norm_w, ln_g, ln_b, rel_bias, finish_dx):
    nb, seq, _ = x.shape
    bmaps = jnp.asarray(_bucket_maps())
    bias = _bias_tables(rel_bias, bmaps)
    bgate8 = jnp.pad(b_gate, ((0, 5), (0, 0)))
    dils = [d for _, d in PATTERNS]

    x16 = x.astype(BF16)
    p16 = p.astype(BF16)
    x16p = [_permute(x16, d) for d in dils]
    qkv = [_proj(x16p[g], [wseg["qkv%d" % g]], BF16, "proj_qkv%d" % g, True)[0].reshape(
        nb, dils[g], seq // dils[g], -1) for g in range(3)]
    nat = {}
    for gi, (group, tm) in enumerate(NAT_GROUPS):
        outs = _proj(x16, [wseg[s] for s in group], F32, "proj_nat%d" % gi, True, tm)
        nat.update(zip(group, outs))
    att = [_attn_fwd(qkv[g], bias[g * GROUP_HEADS:(g + 1) * GROUP_HEADS], dils[g], "attn_fwd%d" % g) for g in range(3)]
    natural = lambda t, g: _unpermute(t.reshape(nb, seq, t.shape[-1]), dils[g])
    oa, o_att, lse = _combine_fwd(att[0][0], att[0][1], natural(att[1][0], 1), natural(att[1][1], 1),
                                  natural(att[2][0], 2), natural(att[2][1], 2), nat["gatt"])

    conv_wg, conv_bg = _xbc_group_order(conv_w), _xbc_group_order(conv_b)
    act = _conv_fwd(nat["xbc"], conv_wg, conv_bg, "conv_fwd")
    dt_sp, dt_sg = _softplus_sig(nat["dt"], jnp.pad(dt_bias, ((0, 0), (0, LANE - SSM_HEADS))))
    dtg, sgg = _group_lanes(dt_sp), _group_lanes(dt_sg)
    alog_g, dskip_g = _group_lanes(a_log), _group_lanes(d_skip)
    y_ssm, y_all, sprev = _ssd_fwd(act, dtg, nat["z"], alog_g, dskip_g, ssm_norm_w)

    w_bra, w_brb = w_br16[:ATT_OUT], w_br16[ATT_OUT:]
    y_a, = _proj(oa, [w_bra], F32, "proj_ya")
    y_b, = _proj(y_ssm, [w_brb], F32, "proj_yb")
    merged = _merge_fwd(y_a, y_b, nat["gm"], bgate8)
    mix, = _proj(merged, [w_out16], F32, "proj_mix")
    pw, = _proj(p16, [w_ple16], F32, "proj_ple")

    dx, dpre16, dpw16, dgp16, ln_sums = _ln_loss(x, mix, nat["gp"], pw, target, bgate8, ln_g, ln_b)
    loss_sum = (0.5 / D_MODEL) * jnp.sum(ln_sums[3])
    dmerged = _dx([dpre16], [w_out16], [], "dx_merged")
    dya16, dyb16, dgm16, mg_sums = _merge_bwd(dmerged, y_a, y_b, nat["gm"], bgate8)
    doa = _dx([dya16], [w_bra], [], "dx_oa")
    dys = _dx([dyb16], [w_brb], [], "dx_yssm")
    g_w_out, = _dw(merged, [dpre16], BF16, "dw_out")
    g_w_br = jnp.concatenate([_dw(oa, [dya16], BF16, "dw_bra")[0], _dw(y_ssm, [dyb16], BF16, "dw_brb")[0]], axis=0)
    g_w_ple, = _dw(p16, [dpw16], BF16, "dw_ple")

    do_att, do16, stats, dgatt16 = _combine_bwd(doa, nat["gatt"], o_att, lse)
    dseg = {"gatt": dgatt16, "gm": dgm16, "gp": dgp16}
    dbias = []
    for g in range(3):
        own_order = lambda t: _permute(t, dils[g]).reshape(nb, dils[g], seq // dils[g], t.shape[-1])
        cotangent = (do_att, o_att, lse) if g == 0 else (own_order(do16), own_order(stats))
        dqkv, db = _attn_bwd(qkv[g], bias[g * GROUP_HEADS:(g + 1) * GROUP_HEADS], cotangent, dils[g],
                             "attn_bwd%d" % g)
        dseg["qkv%d" % g] = dqkv.reshape(nb, seq, -1)
        dbias.append(db)
    g_rel = _bias_grad(jnp.concatenate(dbias, axis=0), bmaps)[:, 0, :NUM_BUCKETS].T

    dact, ddtg, dz, ssd_small, g_normw = _ssd_bwd(
        act, dtg, sgg, nat["z"], y_all, dys, sprev, alog_g, dskip_g, ssm_norm_w)
    dseg["z"] = dz
    dseg["dt"] = jnp.pad(_ungroup_lanes(ddtg), ((0, 0), (0, 0), (0, LANE - SSM_HEADS)))
    dpre, conv_sums = _conv_bwd_pre(dact, nat["xbc"], conv_wg, conv_bg, "conv_bwd")
    dseg["xbc"] = _conv_bwd_x(dpre, conv_wg, "conv_bwd_x")
    csum = _xbc_reference_order(conv_sums)

    dx_perm = [_unpermute(_dx([dseg["qkv%d" % g]], [wseg["qkv%d" % g]], [], "dx_qkv%d" % g, True), dils[g])
               for g in (1, 2)]
    dwseg = {"qkv%d" % g: _dw(x16p[g], [dseg["qkv%d" % g]], BF16, "dw_qkv%d" % g, True)[0] for g in range(3)}
    for gi, group in enumerate(DW_GROUPS):
        dwseg.update(zip(group, _dw(x16, [dseg[s] for s in group], BF16, "dw_nat%d" % gi, True)))
    names = ["qkv0"] + [s for group, _ in NAT_GROUPS for s in group]
    dx = finish_dx([dseg[s] for s in names], [wseg[s] for s in names], [dx] + dx_perm, dwseg, g_w_br, g_w_out, g_w_ple)

    small = dict(
        b_gate=jnp.stack([mg_sums[0], mg_sums[1], ln_sums[2]]),
        conv_w=csum[0:4], conv_b=csum[4:5],
        dt_bias=_ungroup_lanes(ssd_small[:, 2:3, :]), a_log=_ungroup_lanes(ssd_small[:, 0:1, :]),
        d_skip=_ungroup_lanes(ssd_small[:, 1:2, :]), ssm_norm_w=g_normw,
        ln_g=ln_sums[0:1], ln_b=ln_sums[1:2], rel_bias=g_rel)
    return loss_sum, dx, small


DX_TM = 256
SMALL_ORDER = ("b_gate", "conv_w", "conv_b", "dt_bias", "a_log", "d_skip", "ssm_norm_w", "ln_g", "ln_b", "rel_bias")
SMALL_FULL_SHAPES = dict(b_gate=(3, 1024), conv_w=(4, 3072), conv_b=(1, 3072), dt_bias=(1, 32), a_log=(1, 32),
                         d_skip=(1, 32), ssm_norm_w=(1, 2048), ln_g=(1, 1024), ln_b=(1, 1024), rel_bias=(32, 36))


def kernel(x, p, w_in, b_gate, conv_w, conv_b, dt_bias, a_log, d_skip, ssm_norm_w, w_branch, w_out, w_ple, ln_g, ln_b, rel_bias, loss_target, m_w_in, m_b_gate, m_conv_w, m_conv_b, m_dt_bias, m_a_log, m_d_skip, m_ssm_norm_w, m_w_branch, m_w_out, m_w_ple, m_ln_g, m_ln_b, m_rel_bias, v_w_in, v_b_gate, v_conv_w, v_conv_b, v_dt_bias, v_a_log, v_d_skip, v_ssm_norm_w, v_w_branch, v_w_out, v_w_ple, v_ln_g, v_ln_b, v_rel_bias):
    cx, cy, cc = _place()
    chip = 2 * cx + cy
    dev = 4 * cx + 2 * cy + cc

    w_in_t = jnp.transpose(w_in[0])
    win16 = _shard_to_window(w_in_t, chip)
    g_win, g_br, g_out, g_ple = _allgather_pieces(
        [win16, w_branch[0].astype(BF16), w_out[0].astype(BF16), w_ple[0].astype(BF16)])
    wseg = _assemble(g_win)
    w_br16 = g_br.reshape(4 * 704, D_MODEL)
    w_out16 = g_out.reshape(D_MODEL, D_MODEL)
    w_ple16 = jnp.transpose(g_ple, (1, 0, 2)).reshape(PLE_DIM, D_MODEL)
    shards = _allgather8(_pack_rows([b_gate[0], conv_w[0]]), "allgather_small_params")
    per_chip = [_unpack_rows(shards[2 * k], [(3, 256), (4, 768)]) for k in range(4)]
    b_gate_full = jnp.concatenate([pc[0] for pc in per_chip], axis=1)
    conv_w_full = jnp.concatenate([pc[1] for pc in per_chip], axis=1)

    place = jnp.stack([chip, cc]).astype(jnp.int32)
    reduced = []

    def finish_dx(dhs, ws, accs, dwseg, d_br, d_out, d_ple):
        grads = [_pack(dwseg), d_br.reshape(4, 704, D_MODEL), d_out.reshape(4, 256, D_MODEL),
                 jnp.transpose(d_ple.reshape(PLE_DIM, 4, 256), (1, 0, 2))]
        got = _sibling_exchange(grads)
        chip_sums = [_pair_sum(g, t, place, "grad_pair_sum_%d" % i) for i, (g, t) in enumerate(zip(grads, got))]
        dx, others = _dx(dhs, ws, accs, "dx_w_in_and_grad_chip_scatter", True, DX_TM, chip_sums)
        fulls = [_chip_sum(s, t, place, "grad_chip_sum_%d" % i) for i, (s, t) in enumerate(zip(chip_sums, others))]
        reduced.extend(_sibling_gather(fulls))
        return dx

    loss_sum, grad_x, small = _local_grads(
        x, p[0], loss_target, wseg, w_br16, w_out16, w_ple16, b_gate_full, conv_w_full, conv_b, dt_bias, a_log,
        d_skip, ssm_norm_w, ln_g, ln_b, rel_bias, finish_dx)
    loss = lax.psum(loss_sum, ("x", "y", "c"))
    big = reduced
    g_w_in = _window_to_shard(big[0], chip)
    g_w_branch, g_w_out, g_w_ple = big[1], big[2], big[3]
    parts = _allgather8(_pack_rows([small[n] for n in SMALL_ORDER]), "allgather_small_grads")
    small_sum = _sum_rows([parts[i] for i in range(8)], F32, "small_grad_sum")
    sg = dict(zip(SMALL_ORDER, _unpack_rows(small_sum, [SMALL_FULL_SHAPES[n] for n in SMALL_ORDER])))
    sg["b_gate"] = lax.dynamic_slice_in_dim(sg["b_gate"], chip * 256, 256, axis=1)
    sg["conv_w"] = lax.dynamic_slice_in_dim(sg["conv_w"], chip * 768, 768, axis=1)
    del dev

    upd = {}
    upd["w_in"] = [jnp.transpose(t) for t in _adamw(w_in_t, g_w_in, jnp.transpose(m_w_in[0]),
                                                      jnp.transpose(v_w_in[0]), "adamw_w_in")]
    upd["w_branch"] = _adamw(w_branch[0], g_w_branch, m_w_branch[0], v_w_branch[0], "adamw_w_branch")
    upd["w_out"] = _adamw(w_out[0], g_w_out, m_w_out[0], v_w_out[0], "adamw_w_out")
    upd["w_ple"] = _adamw(w_ple[0], g_w_ple, m_w_ple[0], v_w_ple[0], "adamw_w_ple")
    small_w = dict(b_gate=b_gate, conv_w=conv_w, conv_b=conv_b, dt_bias=dt_bias, a_log=a_log, d_skip=d_skip,
                   ssm_norm_w=ssm_norm_w, ln_g=ln_g, ln_b=ln_b, rel_bias=rel_bias)
    small_m = dict(b_gate=m_b_gate, conv_w=m_conv_w, conv_b=m_conv_b, dt_bias=m_dt_bias, a_log=m_a_log,
                   d_skip=m_d_skip, ssm_norm_w=m_ssm_norm_w, ln_g=m_ln_g, ln_b=m_ln_b, rel_bias=m_rel_bias)
    small_v = dict(b_gate=v_b_gate, conv_w=v_conv_w, conv_b=v_conv_b, dt_bias=v_dt_bias, a_log=v_a_log,
                   d_skip=v_d_skip, ssm_norm_w=v_ssm_norm_w, ln_g=v_ln_g, ln_b=v_ln_b, rel_bias=v_rel_bias)
    shapes = [small_w[n].shape for n in SMALL_ORDER]
    s_delta, s_m, s_v = _adamw(_pack_rows([small_w[n] for n in SMALL_ORDER]), _pack_rows([sg[n] for n in SMALL_ORDER]),
                               _pack_rows([small_m[n] for n in SMALL_ORDER]), _pack_rows([small_v[n] for n in SMALL_ORDER]),
                               "adamw_small")
    for i, n in enumerate(SMALL_ORDER):
        upd[n] = tuple(_unpack_rows(t, shapes)[i] for t in (s_delta, s_m, s_v))
        sg[n] = sg[n].reshape(small_w[n].shape)

    order = ("w_in", "b_gate", "conv_w", "conv_b", "dt_bias", "a_log", "d_skip", "ssm_norm_w", "w_branch", "w_out",
             "w_ple", "ln_g", "ln_b", "rel_bias")
    grads = dict(sg, w_in=jnp.transpose(g_w_in)[None],w_branch=g_w_branch[None], w_out=g_w_out[None], w_ple=g_w_ple[None])
    lead = lambda n, t: t[None] if n in ("w_in", "w_branch", "w_out", "w_ple") else t
    return (loss, grad_x, *[grads[n] for n in order], *[lead(n, upd[n][0]) for n in order],
            *[lead(n, upd[n][1]) for n in order], *[lead(n, upd[n][2]) for n in order])
```

```python
import functools
import math

import numpy as np
import jax
import jax.numpy as jnp
from jax import lax
from jax.experimental import pallas as pl
from jax.experimental.pallas import tpu as pltpu

F32, BF16 = jnp.float32, jnp.bfloat16

D_MODEL = 1024
HEAD_DIM = 64
GROUP_HEADS = 12
ATT_OUT = GROUP_HEADS * HEAD_DIM
PATTERNS = ((128, 1), (512, 4), (2048, 16))
BAND = 128
NUM_BUCKETS = 32
MAX_DISTANCE = 2048
D_INNER = 2048
SSM_HEADS = 32
SSM_GROUPS = 4
GROUP_SSM_HEADS = SSM_HEADS // SSM_GROUPS
D_STATE = 128
CHUNK = 128
PLE_DIM = 256
ALPHA = 2.0 ** 0.25
LN_EPS = 1e-5
RMS_EPS = 1e-5
ADAM_LR, ADAM_B1, ADAM_B2, ADAM_EPS, ADAM_WD, ADAM_STEP = 0.001, 0.9, 0.999, 1e-08, 0.01, 10
NEG = -1e30

QKV_W = 3 * ATT_OUT
IN_COLS = 15904
SHARD_COLS = IN_COLS // 4
DT_COL = 12800
ROW_TILE = 16
WIN_ROWS = 4000


def _win_offset(k):
    return (k * SHARD_COLS) % ROW_TILE


def _win_start(k):
    return k * SHARD_COLS - _win_offset(k)

VMEM_LIMIT_BYTES = 56 * 1024 * 1024
LANE = 128
MESH = pl.DeviceIdType.MESH
NT = (((1,), (1,)), ((), ()))
TN = (((0,), (0,)), ((), ()))


def _pcall(body, **kw):
    return pl.pallas_call(body, **kw)


def _params(*sem):
    return pltpu.CompilerParams(dimension_semantics=sem, vmem_limit_bytes=VMEM_LIMIT_BYTES)


def _sigmoid(v):
    return jax.nn.sigmoid(v)


MM_TM = 512


def _permute(t, d):
    nb, seq, ch = t.shape
    return t if d == 1 else t.reshape(nb, seq // d, d, ch).transpose(0, 2, 1, 3).reshape(nb, seq, ch)


def _unpermute(t, d):
    nb, seq, ch = t.shape
    return t if d == 1 else t.reshape(nb, d, seq // d, ch).transpose(0, 2, 1, 3).reshape(nb, seq, ch)


def _tok_spec(tm, width):
    return pl.BlockSpec((None, tm, width), lambda b, i: (b, i, 0))


def _whole(arr, single_buffer=False):
    mode = dict(pipeline_mode=pl.Buffered(1)) if single_buffer else {}
    return pl.BlockSpec(arr.shape, lambda b, i: (0,) * arr.ndim, **mode)


def _proj(a3, ws, out_dtype, name, w_rows_are_outputs=False, tm=MM_TM):
    nb, seq, kdim = a3.shape
    nw = len(ws)
    widths = [w.shape[0] if w_rows_are_outputs else w.shape[1] for w in ws]

    def body(*refs):
        a = refs[0][...].astype(BF16)
        for w_ref, o_ref in zip(refs[1:1 + nw], refs[1 + nw:]):
            if w_rows_are_outputs:
                v = lax.dot_general(a, w_ref[...], NT, preferred_element_type=F32)
            else:
                v = jnp.dot(a, w_ref[...], preferred_element_type=F32)
            o_ref[...] = v.astype(out_dtype)

    return _pcall(
        body, name=name, grid=(nb, seq // tm),
        in_specs=[_tok_spec(tm, kdim)] + [_whole(w) for w in ws],
        out_specs=[_tok_spec(tm, n) for n in widths],
        out_shape=[jax.ShapeDtypeStruct((nb, seq, n), out_dtype) for n in widths],
        compiler_params=_params("parallel", "parallel"),
    )(a3, *ws)


def _dx(dhs, ws, accs, name, w_rows_are_outputs=False, tm=MM_TM, scatter=None):
    nb, seq, _ = dhs[0].shape
    nd, nacc = len(dhs), len(accs)
    kout = ws[0].shape[1] if w_rows_are_outputs else ws[0].shape[0]
    sums = scatter or []
    ns = len(sums)
    chunks = [_row_chunks(s.shape[1], s.shape[2] * s.dtype.itemsize) for s in sums]
    n_sem = 3 * sum(len(ch) for ch in chunks)
    grid = (nb, seq // tm)

    def body(*refs):
        n_in = 2 * nd + nacc
        sum_refs, o_ref, got_refs = refs[n_in:n_in + ns], refs[n_in + ns], refs[n_in + ns + 1:n_in + 2 * ns + 1]

        def copies():
            send_sems, recv_sems = refs[-2], refs[-1]
            x, y, c = _place()
            out = []
            for a in range(ns):
                for s, m in chunks[a]:
                    for j, (cx, cy) in enumerate(_other_chips(x, y)):
                        k = len(out)
                        out.append(_remote(sum_refs[a].at[2 * cx + cy, pl.ds(s, m)], got_refs[a].at[j, pl.ds(s, m)],
                                           send_sems.at[k], recv_sems.at[k], (cx, cy, c)))
            return out

        if ns:
            @pl.when((pl.program_id(0) == 0) & (pl.program_id(1) == 0))
            def _():
                for cp in copies():
                    cp.start()

        v = None
        for dh_ref, w_ref in zip(refs[:nd], refs[nd:2 * nd]):
            dh = dh_ref[...].astype(BF16)
            if w_rows_are_outputs:
                t = jnp.dot(dh, w_ref[...], preferred_element_type=F32)
            else:
                t = lax.dot_general(dh, w_ref[...], NT, preferred_element_type=F32)
            v = t if v is None else v + t
        for a_ref in refs[2 * nd:n_in]:
            v = v + a_ref[...]
        o_ref[...] = v

        if ns:
            @pl.when((pl.program_id(0) == grid[0] - 1) & (pl.program_id(1) == grid[1] - 1))
            def _():
                for cp in copies():
                    cp.wait()

    out = _pcall(
        body, name=name, grid=grid,
        in_specs=[_tok_spec(tm, dh.shape[-1]) for dh in dhs] + [_whole(w, bool(ns)) for w in ws]
        + [_tok_spec(tm, kout)] * nacc + [ANY] * ns,
        out_specs=[_tok_spec(tm, kout)] + [ANY] * ns,
        out_shape=[jax.ShapeDtypeStruct((nb, seq, kout), F32)]
        + [jax.ShapeDtypeStruct((3,) + s.shape[1:], s.dtype) for s in sums],
        input_output_aliases={2 * nd: 0} if nacc else {},
        scratch_shapes=[pltpu.SemaphoreType.DMA((n_sem,)), pltpu.SemaphoreType.DMA((n_sem,))] if ns else [],
        compiler_params=pltpu.CompilerParams(
            dimension_semantics=("arbitrary", "arbitrary") if ns else ("parallel", "parallel"),
            vmem_limit_bytes=VMEM_LIMIT_BYTES, has_side_effects=bool(ns)),
    )(*dhs, *ws, *accs, *sums)
    return (out[0], list(out[1:])) if ns else out[0]


def _dw(a3, dhs, out_dtype, name, rows_are_outputs=False):
    nb, seq, kdim = a3.shape
    nd = len(dhs)
    grid = (nb, seq // MM_TM)
    shapes = [(dh.shape[-1], kdim) if rows_are_outputs else (kdim, dh.shape[-1]) for dh in dhs]

    def body(*refs):
        b, i = pl.program_id(0), pl.program_id(1)
        dh_refs, o_refs, acc_refs = refs[1:1 + nd], refs[1 + nd:1 + 2 * nd], refs[1 + 2 * nd:]

        @pl.when((b == 0) & (i == 0))
        def _():
            for acc_ref in acc_refs:
                acc_ref[...] = jnp.zeros_like(acc_ref)

        a = refs[0][...].astype(BF16)
        for dh_ref, acc_ref in zip(dh_refs, acc_refs):
            dh = dh_ref[...].astype(BF16)
            acc_ref[...] += lax.dot_general(*((dh, a) if rows_are_outputs else (a, dh)), TN,
                                            preferred_element_type=F32)

        @pl.when((b == grid[0] - 1) & (i == grid[1] - 1))
        def _():
            for o_ref, acc_ref in zip(o_refs, acc_refs):
                o_ref[...] = acc_ref[...].astype(out_dtype)

    return _pcall(
        body, name=name, grid=grid,
        in_specs=[_tok_spec(MM_TM, kdim)] + [_tok_spec(MM_TM, dh.shape[-1]) for dh in dhs],
        out_specs=[pl.BlockSpec(s, lambda b, i: (0, 0)) for s in shapes],
        out_shape=[jax.ShapeDtypeStruct(s, out_dtype) for s in shapes],
        scratch_shapes=[pltpu.VMEM(s, F32) for s in shapes],
        compiler_params=_params("arbitrary", "arbitrary"),
    )(a3, *dhs)


def _qkv_rows(g):
    return [(part * QKV_W + g * ATT_OUT + hp * LANE, LANE) for hp in range(ATT_OUT // LANE) for part in range(3)]


XBC_START = 3 * QKV_W + ATT_OUT + D_INNER
GROUP_CH = GROUP_SSM_HEADS * HEAD_DIM
XBC_GROUP = GROUP_CH + 2 * D_STATE
CONV_DIM = SSM_GROUPS * XBC_GROUP


def _xbc_ranges():
    out = []
    for g in range(SSM_GROUPS):
        out += [(g * GROUP_CH, GROUP_CH), (D_INNER + g * D_STATE, D_STATE),
                (D_INNER + SSM_GROUPS * D_STATE + g * D_STATE, D_STATE)]
    return out


def _xbc_group_order(t):
    return jnp.concatenate([t[..., s:s + n] for s, n in _xbc_ranges()], axis=-1)


def _xbc_reference_order(t):
    g = lambda off, n: [t[..., k * XBC_GROUP + off:k * XBC_GROUP + off + n] for k in range(SSM_GROUPS)]
    return jnp.concatenate(g(0, GROUP_CH) + g(GROUP_CH, D_STATE) + g(GROUP_CH + D_STATE, D_STATE), axis=-1)


def _segments():
    one = lambda name, start, rows: (name, [(start, rows)], max(rows, LANE))
    return [("qkv%d" % g, _qkv_rows(g), QKV_W) for g in range(3)] + [
        one("gatt", 3 * QKV_W, ATT_OUT), one("z", 3 * QKV_W + ATT_OUT, D_INNER),
        ("xbc", [(XBC_START + s, n) for s, n in _xbc_ranges()], CONV_DIM), one("dt", DT_COL, SSM_HEADS),
        one("gm", DT_COL + SSM_HEADS, 2 * D_MODEL), one("gp", DT_COL + SSM_HEADS + 2 * D_MODEL, D_MODEL)]


LAYOUT_TC = 256
NAT_GROUPS = ((("gatt", "z", "dt", "gp"), 512), (("xbc", "gm"), 256))
DW_GROUPS = (("gatt", "z", "dt", "gp"), ("xbc",), ("gm",))


def _assemble(win):
    segs = _segments()

    def body(win_ref, *outs):
        def pieces(start, rows):
            t, end = start, start + rows
            while t < end:
                k = min(t // SHARD_COLS, 3)
                shard_end = (k + 1) * SHARD_COLS
                if k < 3 and shard_end % ROW_TILE and t == shard_end - shard_end % ROW_TILE:
                    lo = t - _win_start(k)
                    yield win_ref[k, lo:lo + ROW_TILE, :] + win_ref[k + 1, 0:ROW_TILE, :]
                    t += ROW_TILE
                    continue
                upto = min(end, shard_end - shard_end % ROW_TILE if k < 3 else end)
                yield win_ref[k, t - _win_start(k):upto - _win_start(k), :]
                t = upto

        for (_, ranges, total), o_ref in zip(segs, outs):
            off = 0
            for start, rows in ranges:
                for part in pieces(start, rows):
                    o_ref[off:off + part.shape[0], :] = part
                    off += part.shape[0]
            if off < total:
                o_ref[off:total, :] = jnp.zeros((total - off, o_ref.shape[1]), BF16)

    outs = _pcall(
        body, name="assemble_w_in", grid=(D_MODEL // LAYOUT_TC,),
        in_specs=[pl.BlockSpec((4, WIN_ROWS, LAYOUT_TC), lambda i: (0, 0, i))],
        out_specs=[pl.BlockSpec((total, LAYOUT_TC), lambda i: (0, i)) for _, _, total in segs],
        out_shape=[jax.ShapeDtypeStruct((total, D_MODEL), BF16) for _, _, total in segs],
        compiler_params=_params("parallel"),
    )(win)
    return {name: o for (name, _, _), o in zip(segs, outs)}


def _pack(dsegs):
    segs = _segments()

    def body(*refs):
        ins, o_ref = refs[:-1], refs[-1]
        tail = IN_COLS - _win_start(3)
        o_ref[3, tail:, :] = jnp.zeros((WIN_ROWS - tail, o_ref.shape[2]), BF16)
        for (_, ranges, _), s_ref in zip(segs, ins):
            off = 0
            for start, rows in ranges:
                for k in range(4):
                    lo = _win_start(k)
                    a, b = max(start, lo), min(start + rows, lo + WIN_ROWS)
                    if a < b:
                        o_ref[k, a - lo:b - lo, :] = s_ref[off + a - start:off + b - start, :]
                off += rows

    return _pcall(
        body, name="pack_dw_in", grid=(D_MODEL // LAYOUT_TC,),
        in_specs=[pl.BlockSpec((total, LAYOUT_TC), lambda i: (0, i)) for _, _, total in segs],
        out_specs=pl.BlockSpec((4, WIN_ROWS, LAYOUT_TC), lambda i: (0, 0, i)),
        out_shape=jax.ShapeDtypeStruct((4, WIN_ROWS, D_MODEL), BF16),
        compiler_params=_params("parallel"),
    )(*[dsegs[name] for name, _, _ in segs])


def _shard_to_window(shard_t, k):
    def at(off):
        return lambda w: jnp.pad(w.astype(BF16), ((off, WIN_ROWS - SHARD_COLS - off), (0, 0)))

    return lax.cond(k % 2 == 1, at(_win_offset(1)), at(_win_offset(0)), shard_t)


def _window_to_shard(win, k):
    return lax.dynamic_slice(win, ((k % 2) * _win_offset(1), 0), (SHARD_COLS, D_MODEL))


def _bucket_maps():
    qi = np.arange(BAND)[:, None]
    kj = np.arange(2 * BAND)[None, :]
    delta = qi + BAND - kj
    maps = []
    for window, dil in PATTERNS:
        valid = (delta >= 0) & (delta <= window // dil)
        dist = np.maximum(delta, 0) * dil
        max_exact = NUM_BUCKETS // 2
        d_f = np.maximum(dist, 1).astype(np.float32)
        large = max_exact + (np.log(d_f / np.float32(max_exact)) / np.float32(math.log(MAX_DISTANCE / max_exact))
                             * np.float32(NUM_BUCKETS - max_exact)).astype(np.int32)
        large = np.minimum(large, NUM_BUCKETS - 1)
        bucket = np.where(dist < max_exact, dist, large)
        maps.append(np.where(valid, bucket, -1).astype(np.int32))
    return np.stack(maps)


def _bias_tables(rel_bias, bmaps):
    def body(rb_ref, bm_ref, o_ref):
        h = pl.program_id(0)
        bm = bm_ref[...]
        acc = jnp.full(bm.shape, NEG, F32)
        for b in range(NUM_BUCKETS):
            acc = jnp.where(bm == b, rb_ref[b, h], acc)
        o_ref[...] = acc

    return _pcall(
        body, name="bias_tables", grid=(3 * GROUP_HEADS,),
        in_specs=[pl.BlockSpec(memory_space=pltpu.SMEM),
                  pl.BlockSpec((None, BAND, 2 * BAND), lambda h: (h // GROUP_HEADS, 0, 0))],
        out_specs=pl.BlockSpec((None, BAND, 2 * BAND), lambda h: (h, 0, 0)),
        out_shape=jax.ShapeDtypeStruct((3 * GROUP_HEADS, BAND, 2 * BAND), F32),
        compiler_params=_params("parallel"),
    )(rel_bias, bmaps)


def _bias_grad(dbias, bmaps):
    def body(db_ref, bm_ref, o_ref):
        bm = bm_ref[...]
        db = db_ref[...]
        lane = lax.broadcasted_iota(jnp.int32, (1, LANE), 1)
        vec = jnp.zeros((1, LANE), F32)
        for b in range(NUM_BUCKETS):
            s = jnp.sum(jnp.where(bm == b, db, 0.0), keepdims=True)
            vec = jnp.where(lane == b, s, vec)
        o_ref[...] = vec

    return _pcall(
        body, name="bias_grad", grid=(3 * GROUP_HEADS,),
        in_specs=[pl.BlockSpec((None, BAND, 2 * BAND), lambda h: (h, 0, 0)),
                  pl.BlockSpec((None, BAND, 2 * BAND), lambda h: (h // GROUP_HEADS, 0, 0))],
        out_specs=pl.BlockSpec((None, 1, LANE), lambda h: (h, 0, 0)),
        out_shape=jax.ShapeDtypeStruct((3 * GROUP_HEADS, 1, LANE), F32),
        compiler_params=_params("parallel"),
    )(dbias, bmaps)


def _rows(n):
    if isinstance(n, int):
        return pl.ds(n * BAND, BAND)
    return pl.ds(pl.multiple_of(n * BAND, BAND), BAND)


def _for_blocks(blocks, nblk, per, carry):
    carry = blocks([0], carry, False)
    start = 1 + (nblk - 1) % per
    for n in range(1, start):
        carry = blocks([n], carry, True)
    trips = (nblk - start) // per
    if trips > 0:
        carry = lax.fori_loop(
            0, trips, lambda t, c: blocks([start + t * per + u for u in range(per)], c, True), carry)
    return carry


def _pairs_per_step(d):
    return {1: 1, 4: 6, 16: 6}[d]


def _attn_fwd(qkv4, bias, d, name):
    nb, _, sub, _ = qkv4.shape
    nblk = sub // BAND
    scale = HEAD_DIM ** -0.5
    npair = ATT_OUT // LANE
    hps = _pairs_per_step(d)
    compact = d > 1

    def body(qkv_ref, bias_ref, o_ref, l_ref):
        def blocks(ns, carry, with_prev):
            chains = [(bi, i, h) for bi in range(len(ns)) for i in range(hps) for h in range(2)]
            first_head = lax.broadcasted_iota(jnp.int32, (BAND, LANE), 1) < HEAD_DIM
            pair = lambda n, i, part: qkv_ref[_rows(n), (3 * i + part) * LANE:(3 * i + part + 1) * LANE]
            scores = []
            for bi, i, h in chains:
                n = ns[bi]
                qp = pair(n, i, 0) * scale
                q = jnp.where(first_head if h == 0 else jnp.logical_not(first_head), qp, jnp.zeros_like(qp))
                s_c = lax.dot_general(q, pair(n, i, 1), NT, preferred_element_type=F32) + bias_ref[2 * i + h, :, BAND:]
                s_p = None
                if with_prev:
                    s_p = lax.dot_general(q, pair(n - 1, i, 1), NT,
                                          preferred_element_type=F32) + bias_ref[2 * i + h, :, :BAND]
                scores.append((s_c, s_p))
            probs = []
            for s_c, s_p in scores:
                m = jnp.max(s_c, -1, keepdims=True)
                if with_prev:
                    m = jnp.maximum(m, jnp.max(s_p, -1, keepdims=True))
                e_c = jnp.exp(s_c - m)
                den = jnp.sum(e_c, -1, keepdims=True)
                e_p = None
                if with_prev:
                    e_p = jnp.exp(s_p - m)
                    den = den + jnp.sum(e_p, -1, keepdims=True)
                    e_p = e_p.astype(BF16)
                probs.append((e_c.astype(BF16), e_p, den, m))
            outs = {}
            for (bi, i, h), (e_c, e_p, den, m) in zip(chains, probs):
                n = ns[bi]
                acc = jnp.dot(e_c, pair(n, i, 2), preferred_element_type=F32)
                if with_prev:
                    acc = acc + jnp.dot(e_p, pair(n - 1, i, 2), preferred_element_type=F32)
                outs[(bi, i, h)] = (acc / den, m + jnp.log(den))
            lane = lax.broadcasted_iota(jnp.int32, (BAND, LANE), 1)
            for bi, n in enumerate(ns):
                per_head = jnp.zeros((BAND, LANE), F32)
                for i in range(hps):
                    o_ref[_rows(n), i * LANE:(i + 1) * LANE] = jnp.where(first_head, outs[(bi, i, 0)][0],
                                                                         outs[(bi, i, 1)][0])
                    if compact:
                        for h in range(2):
                            per_head = jnp.where(lane == 2 * i + h, outs[(bi, i, h)][1], per_head)
                    else:
                        l_ref[_rows(n), i * LANE:(i + 1) * LANE] = jnp.where(first_head, outs[(bi, i, 0)][1],
                                                                             outs[(bi, i, 1)][1])
                if compact:
                    l_ref[_rows(n), :] = per_head
            return carry

        _for_blocks(blocks, nblk, 2 if hps == 1 else 1, 0)

    in_specs = [pl.BlockSpec((None, None, sub, 3 * LANE * hps), lambda hp, b, r: (b, r, 0, hp)),
                pl.BlockSpec((2 * hps, BAND, 2 * BAND), lambda hp, b, r: (hp, 0, 0))]
    if compact:
        return _pcall(
            body, name=name, grid=(1, nb, d), in_specs=in_specs,
            out_specs=[pl.BlockSpec((None, None, sub, ATT_OUT), lambda hp, b, r: (b, r, 0, 0)),
                       pl.BlockSpec((None, None, sub, LANE), lambda hp, b, r: (b, r, 0, 0))],
            out_shape=[jax.ShapeDtypeStruct((nb, d, sub, ATT_OUT), F32), jax.ShapeDtypeStruct((nb, d, sub, LANE), F32)],
            compiler_params=_params("parallel", "parallel", "parallel"),
        )(qkv4, bias)
    ospec = pl.BlockSpec((None, sub, hps * LANE), lambda hp, b, r: (b, 0, r * (npair // hps) + hp))
    return _pcall(
        body, name=name, grid=(npair // hps, nb, d), in_specs=in_specs, out_specs=[ospec, ospec],
        out_shape=[jax.ShapeDtypeStruct((nb, sub, d * ATT_OUT), F32)] * 2,
        compiler_params=_params("parallel", "parallel", "parallel"),
    )(qkv4, bias)


STAT_LSE_LANE = 16


def _attn_bwd(qkv4, bias, cotangent, d, name):
    nb, _, sub, _ = qkv4.shape
    nblk = sub // BAND
    scale = HEAD_DIM ** -0.5
    npair = ATT_OUT // LANE
    hps = _pairs_per_step(d)
    compact = d > 1

    def body(qkv_ref, bias_ref, *rest):
        do_ref, dqkv_ref, db_ref = rest[0], rest[-2], rest[-1]
        b, r = pl.program_id(1), pl.program_id(2)

        @pl.when((b == 0) & (r == 0))
        def _():
            db_ref[...] = jnp.zeros_like(db_ref)

        def blocks(ns, carry, with_prev):
            sides = (0, 1) if with_prev else (0,)
            chains = [(bi, i, h, sd) for bi in range(len(ns)) for i in range(hps) for h in range(2) for sd in sides]
            first_head = lax.broadcasted_iota(jnp.int32, (BAND, LANE), 1) < HEAD_DIM
            own = lambda h, t: jnp.where(first_head if h == 0 else jnp.logical_not(first_head), t, jnp.zeros_like(t))
            pair = lambda rows, i, part: qkv_ref[rows, (3 * i + part) * LANE:(3 * i + part + 1) * LANE]
            key_rows = lambda bi, sd: _rows(ns[bi] - sd)
            qs = {}
            for bi in range(len(ns)):
                for i in range(hps):
                    q_pair = pair(_rows(ns[bi]), i, 0) * scale
                    do = do_ref[_rows(ns[bi]), i * LANE:(i + 1) * LANE]
                    do16 = do.astype(BF16)
                    for h in range(2):
                        if compact:
                            st_ref, head = rest[1], 2 * i + h
                            ebar = st_ref[_rows(ns[bi]), head:head + 1]
                            lcol = st_ref[_rows(ns[bi]), STAT_LSE_LANE + head:STAT_LSE_LANE + head + 1]
                        else:
                            ebar = jnp.sum(own(h, do * rest[1][_rows(ns[bi]), i * LANE:(i + 1) * LANE]), -1, keepdims=True)
                            lcol = rest[2][_rows(ns[bi]), i * LANE + h * HEAD_DIM:i * LANE + h * HEAD_DIM + 1]
                        qs[(bi, i, h)] = (own(h, q_pair), q_pair, own(h, do16), do16, ebar, lcol)
            raw = []
            for bi, i, h, sd in chains:
                q, _, do_h, _, _, _ = qs[(bi, i, h)]
                bias_blk = bias_ref[2 * i + h, :, :BAND] if sd else bias_ref[2 * i + h, :, BAND:]
                s = lax.dot_general(q, pair(key_rows(bi, sd), i, 1), NT, preferred_element_type=F32) + bias_blk
                dp = lax.dot_general(do_h, pair(key_rows(bi, sd), i, 2), NT, preferred_element_type=F32)
                raw.append((s, dp))
            soft = []
            for (bi, i, h, sd), (s, dp) in zip(chains, raw):
                ebar, lcol = qs[(bi, i, h)][4:]
                p = jnp.exp(s - lcol)
                ds = p * (dp - ebar)
                if sd:
                    db_ref[2 * i + h, :, :BAND] += ds
                else:
                    db_ref[2 * i + h, :, BAND:] += ds
                soft.append((p.astype(BF16), ds.astype(BF16)))
            grads = {}
            for (bi, i, h, sd), (p16, ds16) in zip(chains, soft):
                _, q_pair, _, do16 = qs[(bi, i, h)][:4]
                grads[(bi, i, h, sd)] = (
                    jnp.dot(ds16, pair(key_rows(bi, sd), i, 1), preferred_element_type=F32),
                    lax.dot_general(ds16, q_pair, TN, preferred_element_type=F32),
                    lax.dot_general(p16, do16, TN, preferred_element_type=F32))
            both = lambda bi, i, sd, which: jnp.where(first_head, grads[(bi, i, 0, sd)][which],
                                                      grads[(bi, i, 1, sd)][which])
            carry = list(carry) if carry is not None else None
            for bi, n in enumerate(ns):
                for i in range(hps):
                    base = 3 * LANE * i
                    dq = both(bi, i, 0, 0)
                    if with_prev:
                        dq = dq + both(bi, i, 1, 0)
                        dqkv_ref[_rows(n - 1), base + LANE:base + 2 * LANE] = (
                            carry[2 * i] + both(bi, i, 1, 1)).astype(BF16)
                        dqkv_ref[_rows(n - 1), base + 2 * LANE:base + 3 * LANE] = (
                            carry[2 * i + 1] + both(bi, i, 1, 2)).astype(BF16)
                    dqkv_ref[_rows(n), base:base + LANE] = (dq * scale).astype(BF16)
                carry = [t for i in range(hps) for t in (both(bi, i, 0, 1), both(bi, i, 0, 2))]
            return tuple(carry)

        carry = _for_blocks(blocks, nblk, 2 if hps == 1 else 1, None)
        for i in range(hps):
            base = 3 * LANE * i
            dqkv_ref[_rows(nblk - 1), base + LANE:base + 2 * LANE] = carry[2 * i].astype(BF16)
            dqkv_ref[_rows(nblk - 1), base + 2 * LANE:base + 3 * LANE] = carry[2 * i + 1].astype(BF16)

    qspec = pl.BlockSpec((None, None, sub, 3 * LANE * hps), lambda hp, b, r: (b, r, 0, hp))
    bspec = pl.BlockSpec((2 * hps, BAND, 2 * BAND), lambda hp, b, r: (hp, 0, 0))
    if compact:
        cspecs = [pl.BlockSpec((None, None, sub, ATT_OUT), lambda hp, b, r: (b, r, 0, 0)),
                  pl.BlockSpec((None, None, sub, LANE), lambda hp, b, r: (b, r, 0, 0))]
    else:
        cspecs = [pl.BlockSpec((None, sub, hps * LANE), lambda hp, b, r: (b, 0, r * (npair // hps) + hp))] * 3
    return _pcall(
        body, name=name, grid=(npair // hps, nb, d),
        in_specs=[qspec, bspec] + cspecs, out_specs=[qspec, bspec],
        out_shape=[jax.ShapeDtypeStruct(qkv4.shape, BF16),
                   jax.ShapeDtypeStruct((GROUP_HEADS, BAND, 2 * BAND), F32)],
        compiler_params=_params("parallel", "arbitrary", "arbitrary"),
    )(qkv4, bias, *cotangent)


def _head_lanes(first_lane, one_channel):
    c = lax.broadcasted_iota(jnp.int32, (ATT_OUT, LANE), 0)
    lane = lax.broadcasted_iota(jnp.int32, (ATT_OUT, LANE), 1)
    hit = lane == first_lane + c // HEAD_DIM
    if one_channel:
        hit = hit & (c % HEAD_DIM == 0)
    return hit.astype(BF16)


def _exact_dot(v, m01, dims=None):
    parts = _split3(v)
    if dims is None:
        dot = lambda t: jnp.dot(t, m01, preferred_element_type=F32)
    else:
        dot = lambda t: lax.dot_general(t, m01, dims, preferred_element_type=F32)
    return (dot(parts[0]) + dot(parts[1])) + dot(parts[2])


def _combine_fwd(o0, l0, o1, l1, o2, l2, gatt):
    nb, seq, _ = gatt.shape
    tm = 512

    def body(o0_ref, l0_ref, o1_ref, l1_ref, o2_ref, l2_ref, g_ref, oa_ref, oatt_ref, lse_ref):
        spread = _head_lanes(0, False)
        l0v = l0_ref[...]
        l1v = _exact_dot(l1_ref[...], spread, NT)
        l2v = _exact_dot(l2_ref[...], spread, NT)
        m = jnp.maximum(jnp.maximum(l0v, l1v), l2v)
        tot = m + jnp.log(jnp.exp(l0v - m) + jnp.exp(l1v - m) + jnp.exp(l2v - m))
        o = (jnp.exp(l0v - tot) * o0_ref[...] + jnp.exp(l1v - tot) * o1_ref[...]
             + jnp.exp(l2v - tot) * o2_ref[...])
        g = g_ref[...]
        oa_ref[...] = (o * (g * _sigmoid(g))).astype(BF16)
        oatt_ref[...] = o
        lse_ref[...] = tot

    spec = pl.BlockSpec((None, tm, ATT_OUT), lambda b, i: (b, i, 0))
    lspec = pl.BlockSpec((None, tm, LANE), lambda b, i: (b, i, 0))
    return _pcall(
        body, name="attn_combine", grid=(nb, seq // tm),
        in_specs=[spec, spec, spec, lspec, spec, lspec, spec], out_specs=[spec] * 3,
        out_shape=[jax.ShapeDtypeStruct((nb, seq, ATT_OUT), BF16), jax.ShapeDtypeStruct((nb, seq, ATT_OUT), F32),
                   jax.ShapeDtypeStruct((nb, seq, ATT_OUT), F32)],
        compiler_params=_params("parallel", "parallel"),
    )(o0, l0, o1, l1, o2, l2, gatt)


def _combine_bwd(doa, gatt, o_att, lse):
    nb, seq, _ = gatt.shape
    tm = 512

    def body(doa_ref, g_ref, o_ref, l_ref, do_ref, do16_ref, st_ref, dg_ref):
        g = g_ref[...]
        sg = _sigmoid(g)
        do = doa_ref[...] * (g * sg)
        do_ref[...] = do
        do16_ref[...] = do.astype(BF16)
        st_ref[...] = (_exact_dot(do * o_ref[...], _head_lanes(0, False))
                       + _exact_dot(l_ref[...], _head_lanes(STAT_LSE_LANE, True)))
        dg_ref[...] = (doa_ref[...] * o_ref[...] * (sg * (1.0 + g * (1.0 - sg)))).astype(BF16)

    spec = pl.BlockSpec((None, tm, ATT_OUT), lambda b, i: (b, i, 0))
    lspec = pl.BlockSpec((None, tm, LANE), lambda b, i: (b, i, 0))
    return _pcall(
        body, name="attn_combine_bwd", grid=(nb, seq // tm), in_specs=[spec] * 4,
        out_specs=[spec, spec, lspec, spec],
        out_shape=[jax.ShapeDtypeStruct((nb, seq, ATT_OUT), F32), jax.ShapeDtypeStruct((nb, seq, ATT_OUT), BF16),
                   jax.ShapeDtypeStruct((nb, seq, LANE), F32), jax.ShapeDtypeStruct((nb, seq, ATT_OUT), BF16)],
        compiler_params=_params("parallel", "parallel"),
    )(doa, gatt, o_att, lse)


CONV_TM = 512
CONV_TC = 512


def _shift_down(cur, halo, k):
    rolled = pltpu.roll(cur, k, 0)
    hro = pltpu.roll(halo, k, 0)
    row = lax.broadcasted_iota(jnp.int32, hro.shape, 0)
    return jnp.concatenate([jnp.where(row < k, hro, rolled[:8]), rolled[8:]], axis=0)


def _shift_up(cur, halo, k):
    n = cur.shape[0]
    rolled = pltpu.roll(cur, n - k, 0)
    hro = pltpu.roll(halo, 8 - k, 0)
    row = lax.broadcasted_iota(jnp.int32, hro.shape, 0)
    return jnp.concatenate([rolled[:n - 8], jnp.where(row >= 8 - k, hro, rolled[n - 8:])], axis=0)


def _conv_pre(cur, halo, w_ref, b_ref):
    acc = cur * w_ref[3:4, :] + b_ref[...]
    for k in range(1, 4):
        acc = acc + _shift_down(cur, halo, k) * w_ref[3 - k:4 - k, :]
    return acc


def _conv_specs(seq):
    nblk = seq // CONV_TM
    cur = pl.BlockSpec((None, CONV_TM, CONV_TC), lambda cb, b, i: (b, i, cb))
    prev = pl.BlockSpec((None, 8, CONV_TC), lambda cb, b, i: (b, jnp.maximum(i * (CONV_TM // 8) - 1, 0), cb))
    nxt = pl.BlockSpec((None, 8, CONV_TC),
                       lambda cb, b, i: (b, jnp.minimum((i + 1) * (CONV_TM // 8), seq // 8 - 1), cb))
    wspec = pl.BlockSpec((4, CONV_TC), lambda cb, b, i: (0, cb))
    bspec = pl.BlockSpec((1, CONV_TC), lambda cb, b, i: (0, cb))
    return nblk, cur, prev, nxt, wspec, bspec


def _conv_fwd(xin, w4, bias, name):
    nb, seq, ch = xin.shape
    _, cur, prev, _, wspec, bspec = _conv_specs(seq)

    def body(x_ref, h_ref, w_ref, b_ref, o_ref):
        halo = jnp.where(pl.program_id(2) > 0, h_ref[...], 0.0)
        pre = _conv_pre(x_ref[...], halo, w_ref, b_ref)
        o_ref[...] = pre * _sigmoid(pre)

    return _pcall(
        body, name=name, grid=(ch // CONV_TC, nb, seq // CONV_TM),
        in_specs=[cur, prev, wspec, bspec], out_specs=cur,
        out_shape=jax.ShapeDtypeStruct(xin.shape, F32),
        compiler_params=_params("parallel", "parallel", "parallel"),
    )(xin, xin, w4, bias)


def _conv_bwd_pre(dact, xin, w4, bias, name):
    nb, seq, ch = xin.shape
    _, cur, prev, _, wspec, bspec = _conv_specs(seq)

    def body(da_ref, x_ref, h_ref, w_ref, b_ref, dp_ref, s_ref):
        b, i = pl.program_id(1), pl.program_id(2)

        @pl.when((b == 0) & (i == 0))
        def _():
            s_ref[...] = jnp.zeros_like(s_ref)

        halo = jnp.where(i > 0, h_ref[...], 0.0)
        x = x_ref[...]
        pre = _conv_pre(x, halo, w_ref, b_ref)
        sg = _sigmoid(pre)
        dpre = da_ref[...] * (sg * (1.0 + pre * (1.0 - sg)))
        dp_ref[...] = dpre
        s_ref[3:4, :] += jnp.sum(dpre * x, 0, keepdims=True)
        for k in range(1, 4):
            s_ref[3 - k:4 - k, :] += jnp.sum(dpre * _shift_down(x, halo, k), 0, keepdims=True)
        s_ref[4:5, :] += jnp.sum(dpre, 0, keepdims=True)

    return _pcall(
        body, name=name, grid=(ch // CONV_TC, nb, seq // CONV_TM),
        in_specs=[cur, cur, prev, wspec, bspec],
        out_specs=[cur, pl.BlockSpec((8, CONV_TC), lambda cb, b, i: (0, cb))],
        out_shape=[jax.ShapeDtypeStruct(xin.shape, F32), jax.ShapeDtypeStruct((8, ch), F32)],
        compiler_params=_params("parallel", "arbitrary", "arbitrary"),
    )(dact, xin, xin, w4, bias)


def _conv_bwd_x(dpre, w4, name):
    nb, seq, ch = dpre.shape
    nblk, cur, _, nxt, wspec, _ = _conv_specs(seq)

    def body(d_ref, n_ref, w_ref, o_ref):
        halo = jnp.where(pl.program_id(2) < nblk - 1, n_ref[...], 0.0)
        cur_v = d_ref[...]
        acc = cur_v * w_ref[3:4, :]
        for j in range(1, 4):
            acc = acc + _shift_up(cur_v, halo, j) * w_ref[3 - j:4 - j, :]
        o_ref[...] = acc.astype(BF16)

    return _pcall(
        body, name=name, grid=(ch // CONV_TC, nb, seq // CONV_TM),
        in_specs=[cur, nxt, wspec], out_specs=cur,
        out_shape=jax.ShapeDtypeStruct(dpre.shape, BF16),
        compiler_params=_params("parallel", "parallel", "parallel"),
    )(dpre, dpre, w4)


def _softplus_sig(dt_raw, dt_bias_row):
    nb, seq, _ = dt_raw.shape
    tm = 512

    def body(r_ref, b_ref, sp_ref, sg_ref):
        v = r_ref[...] + b_ref[...]
        sp_ref[...] = jnp.maximum(v, 0.0) + jnp.log1p(jnp.exp(-jnp.abs(v)))
        sg_ref[...] = _sigmoid(v)

    spec = pl.BlockSpec((None, tm, LANE), lambda b, i: (b, i, 0))
    return _pcall(
        body, name="dt_softplus", grid=(nb, seq // tm),
        in_specs=[spec, pl.BlockSpec((1, LANE), lambda b, i: (0, 0))], out_specs=[spec, spec],
        out_shape=[jax.ShapeDtypeStruct(dt_raw.shape, F32)] * 2,
        compiler_params=_params("parallel", "parallel"),
    )(dt_raw, dt_bias_row)


def _group_lanes(t):
    pads = [(0, 0)] * (t.ndim - 1) + [(0, LANE - GROUP_SSM_HEADS)]
    return jnp.stack([jnp.pad(t[..., GROUP_SSM_HEADS * g:GROUP_SSM_HEADS * (g + 1)], pads) for g in range(SSM_GROUPS)])


def _ungroup_lanes(t):
    return jnp.concatenate([t[g][..., :GROUP_SSM_HEADS] for g in range(SSM_GROUPS)], axis=-1)


def _decays(dt, al_ref):
    row = lax.broadcasted_iota(jnp.int32, (CHUNK, CHUNK), 0)
    col = lax.broadcasted_iota(jnp.int32, (CHUNK, CHUNK), 1)
    tril = (row >= col).astype(BF16)
    triu = (row <= col).astype(BF16)
    arow = -jnp.exp(al_ref[...])
    hi, mid, lo = _split3(dt * arow)
    down = lambda t: jnp.dot(tril, t, preferred_element_type=F32)
    across = lambda t: lax.dot_general(t, triu, TN, preferred_element_type=F32)
    acs = (down(hi) + down(mid)) + down(lo)
    acs_t = (across(hi) + across(mid)) + across(lo)
    return arow, acs, acs_t, row >= col, triu


STEP_CHUNKS = 2


def _ssd_specs(nb, seq):
    nc = seq // CHUNK
    hw = GROUP_SSM_HEADS * HEAD_DIM
    rows, steps = STEP_CHUNKS * CHUNK, nc // STEP_CHUNKS

    def mk(rev):
        cidx = (lambda c: steps - 1 - c) if rev else (lambda c: c)
        wide = pl.BlockSpec((None, rows, hw), lambda g, b, c: (b, cidx(c), g))
        xbc = pl.BlockSpec((None, rows, XBC_GROUP), lambda g, b, c: (b, cidx(c), g))
        lanes = pl.BlockSpec((None, None, rows, LANE), lambda g, b, c: (g, b, cidx(c), 0))
        prev = pl.BlockSpec((None, STEP_CHUNKS, None, D_STATE, hw), lambda g, b, c: (b, cidx(c), g, 0, 0))
        return wide, xbc, lanes, prev

    grow = pl.BlockSpec((None, 1, LANE), lambda g, b, c: (g, 0, 0))
    nwspec = pl.BlockSpec((1, hw), lambda g, b, c: (0, g))
    return nc, steps, hw, mk, grow, nwspec


def _head_expand():
    hw = GROUP_SSM_HEADS * HEAD_DIM
    r = lax.broadcasted_iota(jnp.int32, (LANE, hw), 0)
    c = lax.broadcasted_iota(jnp.int32, (LANE, hw), 1)
    return ((c // HEAD_DIM) == r).astype(BF16)


def _split3(v):
    hi = v.astype(BF16)
    rest = v - hi.astype(F32)
    mid = rest.astype(BF16)
    return hi, mid, (rest - mid.astype(F32)).astype(BF16)


def _to_channels(v, e):
    hi, mid, lo = _split3(v)
    dot = lambda t: jnp.dot(t, e, preferred_element_type=F32)
    return (dot(hi) + dot(mid)) + dot(lo)


def _to_heads(w, e):
    hi, mid, lo = _split3(w)
    dot = lambda t: lax.dot_general(t, e, (((1,), (1,)), ((), ())), preferred_element_type=F32)
    return (dot(hi) + dot(mid)) + dot(lo)


def _row8(v):
    return jnp.broadcast_to(v, (8, v.shape[1]))


def _ssd_chunk_setup(dt, al_ref, ds_ref):
    arow, acs, acs_t, causal, triu = _decays(dt, al_ref)
    e = _head_expand()
    dtx = _to_channels(dt, e)
    acsx = _to_channels(acs, e)
    lastx = acsx[CHUNK - 1:CHUNK, :]
    dskx = _to_channels(_row8(ds_ref[...]), e)[0:1, :]
    return arow, acs, acs_t, causal, triu, e, dtx, acsx, lastx, dskx


def _ssd_fwd(xbc, dtg, z, alog_g, dskip_g, normw):
    nb, seq, _ = xbc.shape
    nc, steps, hw, mk, grow, nwspec = _ssd_specs(nb, seq)
    wide, xbc_spec, lanes, prev = mk(False)
    tn = (((0,), (0,)), ((), ()))

    def body(xbc_ref, dt_ref, z_ref, al_ref, ds_ref, nw_ref, ys_ref, y_ref, sp_ref, st_ref):
        @pl.when(pl.program_id(2) == 0)
        def _():
            st_ref[...] = jnp.zeros_like(st_ref)

        for ci in range(STEP_CHUNKS):
            chunk(ci, xbc_ref, dt_ref, z_ref, al_ref, ds_ref, nw_ref, ys_ref, y_ref, sp_ref, st_ref)

    def chunk(ci, xbc_ref, dt_ref, z_ref, al_ref, ds_ref, nw_ref, ys_ref, y_ref, sp_ref, st_ref):
        rows = slice(ci * CHUNK, (ci + 1) * CHUNK)
        dt = dt_ref[rows, :]
        _, acs, acs_t, causal, _, _, dtx, acsx, lastx, dskx = _ssd_chunk_setup(dt, al_ref, ds_ref)
        bmat = xbc_ref[rows, GROUP_CH:GROUP_CH + D_STATE].astype(BF16)
        cmat = xbc_ref[rows, GROUP_CH + D_STATE:].astype(BF16)
        cb = lax.dot_general(cmat, bmat, (((1,), (1,)), ((), ())), preferred_element_type=F32)
        x = xbc_ref[rows, :GROUP_CH]
        xdt = x * dtx
        xdt16 = xdt.astype(BF16)
        first_head = lax.broadcasted_iota(jnp.int32, (CHUNK, LANE), 1) < HEAD_DIM
        pairs = []
        for hp in range(GROUP_SSM_HEADS // 2):
            xp = xdt16[:, hp * LANE:(hp + 1) * LANE]
            two = []
            for j in (2 * hp, 2 * hp + 1):
                lmat = jnp.exp(jnp.where(causal, acs[:, j:j + 1] - acs_t[j:j + 1, :], -jnp.inf))
                two.append(jnp.dot((cb * lmat).astype(BF16), xp, preferred_element_type=F32))
            pairs.append(jnp.where(first_head, two[0], two[1]))
        yd = jnp.concatenate(pairs, axis=1)
        s_prev = st_ref[...]
        s16 = s_prev.astype(BF16)
        sp_ref[ci] = s16
        yo = jnp.dot(cmat, s16, preferred_element_type=F32) * jnp.exp(acsx)
        sts = lax.dot_general(bmat, (xdt * jnp.exp(lastx - acsx)).astype(BF16), tn, preferred_element_type=F32)
        st_ref[...] = s_prev * jnp.exp(lastx) + sts
        y = yd + yo + dskx * x
        zz = z_ref[rows, :]
        u = y * (zz * _sigmoid(zz))
        rn = lax.rsqrt(jnp.mean(u * u, -1, keepdims=True) + RMS_EPS)
        ys_ref[rows, :] = (u * rn * nw_ref[...]).astype(BF16)
        y_ref[rows, :] = y

    return _pcall(
        body, name="ssd_fwd", grid=(SSM_GROUPS, nb, steps),
        in_specs=[xbc_spec, lanes, wide, grow, grow, nwspec],
        out_specs=[wide, wide, prev],
        out_shape=[jax.ShapeDtypeStruct((nb, seq, D_INNER), BF16), jax.ShapeDtypeStruct((nb, seq, D_INNER), F32),
                   jax.ShapeDtypeStruct((nb, nc, SSM_GROUPS, D_STATE, hw), BF16)],
        scratch_shapes=[pltpu.VMEM((D_STATE, hw), F32)],
        compiler_params=_params("parallel", "parallel", "arbitrary"),
    )(xbc, dtg, z, alog_g, dskip_g, normw)


def _ssd_bwd(xbc, dtg, sgg, z, y, dys, sprev, alog_g, dskip_g, normw):
    nb, seq, _ = xbc.shape
    nc, steps, hw, mk, grow, nwspec = _ssd_specs(nb, seq)
    wide, xbc_spec, lanes, prev = mk(True)
    nt = (((1,), (1,)), ((), ()))
    tn = (((0,), (0,)), ((), ()))

    def body(xbc_ref, dt_ref, sg_ref, z_ref, y_ref, dys_ref, sp_ref, al_ref, ds_ref, nw_ref,
             dxbc_ref, ddt_ref, dz_ref, small_ref, dnw_ref, g_ref):
        b, c = pl.program_id(1), pl.program_id(2)

        @pl.when((b == 0) & (c == 0))
        def _():
            small_ref[...] = jnp.zeros_like(small_ref)
            dnw_ref[...] = jnp.zeros_like(dnw_ref)

        @pl.when(c == 0)
        def _():
            g_ref[...] = jnp.zeros_like(g_ref)

        for ci in reversed(range(STEP_CHUNKS)):
            chunk(ci, xbc_ref, dt_ref, sg_ref, z_ref, y_ref, dys_ref, sp_ref, al_ref, ds_ref, nw_ref,
                  dxbc_ref, ddt_ref, dz_ref, small_ref, dnw_ref, g_ref)

    def chunk(ci, xbc_ref, dt_ref, sg_ref, z_ref, y_ref, dys_ref, sp_ref, al_ref, ds_ref, nw_ref,
              dxbc_ref, ddt_ref, dz_ref, small_ref, dnw_ref, g_ref):
        rows = slice(ci * CHUNK, (ci + 1) * CHUNK)
        yv, zz, dys_v, nw = y_ref[rows, :], z_ref[rows, :], dys_ref[rows, :], nw_ref[...]
        sz = _sigmoid(zz)
        silu = zz * sz
        u = yv * silu
        rn = lax.rsqrt(jnp.mean(u * u, -1, keepdims=True) + RMS_EPS)
        gn = dys_v * nw
        du = rn * gn - u * (rn * rn * rn) * jnp.mean(u * gn, -1, keepdims=True)
        dnw_ref[...] += jnp.sum(dys_v * u * rn, 0, keepdims=True)
        dy = du * silu
        dz_ref[rows, :] = du * yv * (sz * (1.0 + zz * (1.0 - sz)))

        dt = dt_ref[rows, :]
        arow, acs, acs_t, causal, triu, e, dtx, acsx, lastx, dskx = _ssd_chunk_setup(dt, al_ref, ds_ref)
        dfsx = jnp.exp(acsx)
        dtex = jnp.exp(lastx - acsx)
        bmat = xbc_ref[rows, GROUP_CH:GROUP_CH + D_STATE].astype(BF16)
        cmat = xbc_ref[rows, GROUP_CH + D_STATE:].astype(BF16)
        cb = lax.dot_general(cmat, bmat, nt, preferred_element_type=F32)
        x = xbc_ref[rows, :GROUP_CH]
        xdt = x * dtx
        xdt16 = xdt.astype(BF16)
        xdte = xdt * dtex
        dy16 = dy.astype(BF16)
        dyd = dy * dfsx
        dyd16 = dyd.astype(BF16)
        s16 = sp_ref[ci]
        g = g_ref[...]
        g16 = g.astype(BF16)
        cs = jnp.dot(cmat, s16, preferred_element_type=F32)
        dc_off = lax.dot_general(dyd16, s16, nt, preferred_element_type=F32)
        g_here = lax.dot_general(cmat, dyd16, tn, preferred_element_type=F32)
        bg = jnp.dot(bmat, g16, preferred_element_type=F32)
        db_st = lax.dot_general(xdte.astype(BF16), g16, nt, preferred_element_type=F32)
        ddte_w = bg * xdte
        dcd = _to_heads(_row8(jnp.sum(g * s16.astype(F32), 0, keepdims=True)), e)[0:1, :]
        lane = lax.broadcasted_iota(jnp.int32, (CHUNK, LANE), 1)
        first_head = lane < HEAD_DIM
        sub = lax.broadcasted_iota(jnp.int32, (CHUNK, LANE), 0)
        dacs = jnp.zeros((CHUNK, LANE), F32)
        colsums = jnp.zeros((CHUNK, LANE), F32)
        dcb = jnp.zeros((CHUNK, CHUNK), F32)
        pairs = []
        for hp in range(GROUP_SSM_HEADS // 2):
            xp = xdt16[:, hp * LANE:(hp + 1) * LANE]
            dyp = dy16[:, hp * LANE:(hp + 1) * LANE]
            two = []
            for idx, j in enumerate((2 * hp, 2 * hp + 1)):
                lmat = jnp.exp(jnp.where(causal, acs[:, j:j + 1] - acs_t[j:j + 1, :], -jnp.inf))
                mf = cb * lmat
                dy_h = jnp.where(first_head if idx == 0 else jnp.logical_not(first_head), dyp, jnp.zeros_like(dyp))
                dm = lax.dot_general(dy_h, xp, nt, preferred_element_type=F32)
                two.append(lax.dot_general(mf.astype(BF16), dyp, tn, preferred_element_type=F32))
                wmat = dm * mf
                dcb = dcb + dm * lmat
                dacs = jnp.where(lane == j, jnp.sum(wmat, -1, keepdims=True), dacs)
                colsums = jnp.where(sub == j, jnp.sum(wmat, 0, keepdims=True), colsums)
            pairs.append(jnp.where(first_head, two[0], two[1]))
        dxdt = bg * dtex + jnp.concatenate(pairs, axis=1)
        dacs = dacs - colsums.T + _to_heads(dyd * cs - ddte_w, e)
        cd_row = jnp.exp(acs[CHUNK - 1:CHUNK, :])
        tail = _to_heads(_row8(jnp.sum(ddte_w, 0, keepdims=True)), e)[0:1, :] + dcd * cd_row
        dacs = dacs + jnp.where(sub == CHUNK - 1, tail, 0.0)
        d_hi, d_mid, d_lo = _split3(dacs)
        up = lambda t: jnp.dot(triu, t, preferred_element_type=F32)
        da = (up(d_hi) + up(d_mid)) + up(d_lo)
        ddt_raw = (da * arow + _to_heads(dxdt * x, e)) * sg_ref[rows, :]
        ddt_ref[rows, :] = ddt_raw
        small_ref[0:1, :] += jnp.sum(da * dt, 0, keepdims=True) * arow
        small_ref[1:2, :] += _to_heads(_row8(jnp.sum(dy * x, 0, keepdims=True)), e)[0:1, :]
        small_ref[2:3, :] += jnp.sum(ddt_raw, 0, keepdims=True)
        dcb16 = dcb.astype(BF16)
        dxbc_ref[rows, GROUP_CH + D_STATE:] = dc_off + jnp.dot(dcb16, bmat, preferred_element_type=F32)
        dxbc_ref[rows, GROUP_CH:GROUP_CH + D_STATE] = db_st + lax.dot_general(dcb16, cmat, tn,
                                                                               preferred_element_type=F32)
        dxbc_ref[rows, :GROUP_CH] = dxdt * dtx + dskx * dy
        g_ref[...] = g * jnp.exp(lastx) + g_here

    return _pcall(
        body, name="ssd_bwd", grid=(SSM_GROUPS, nb, steps),
        in_specs=[xbc_spec, lanes, lanes, wide, wide, wide, prev, grow, grow, nwspec],
        out_specs=[xbc_spec, lanes, wide,
                   pl.BlockSpec((None, 8, LANE), lambda g, b, c: (g, 0, 0)), nwspec],
        out_shape=[jax.ShapeDtypeStruct((nb, seq, CONV_DIM), F32),
                   jax.ShapeDtypeStruct((SSM_GROUPS, nb, seq, LANE), F32),
                   jax.ShapeDtypeStruct((nb, seq, D_INNER), F32),
                   jax.ShapeDtypeStruct((SSM_GROUPS, 8, LANE), F32),
                   jax.ShapeDtypeStruct((1, D_INNER), F32)],
        scratch_shapes=[pltpu.VMEM((D_STATE, hw), F32)],
        compiler_params=_params("parallel", "arbitrary", "arbitrary"),
    )(xbc, dtg, sgg, z, y, dys, sprev, alog_g, dskip_g, normw)


EW_TM = 256


def _merge_fwd(y_a, y_b, gm, bgate):
    nb, seq, _ = y_a.shape

    def body(a_ref, b_ref, ga_ref, gb_ref, bg_ref, o_ref):
        sa = _sigmoid(ga_ref[...] + bg_ref[0:1, :])
        sb = _sigmoid(gb_ref[...] + bg_ref[1:2, :])
        o_ref[...] = (sa * a_ref[...] + sb * b_ref[...]).astype(BF16)

    spec = pl.BlockSpec((None, EW_TM, D_MODEL), lambda b, i: (b, i, 0))
    spec1 = pl.BlockSpec((None, EW_TM, D_MODEL), lambda b, i: (b, i, 1))
    return _pcall(
        body, name="merge_fwd", grid=(nb, seq // EW_TM),
        in_specs=[spec, spec, spec, spec1, pl.BlockSpec((8, D_MODEL), lambda b, i: (0, 0))], out_specs=spec,
        out_shape=jax.ShapeDtypeStruct((nb, seq, D_MODEL), BF16),
        compiler_params=_params("parallel", "parallel"),
    )(y_a, y_b, gm, gm, bgate)


def _merge_bwd(dmerged, y_a, y_b, gm, bgate):
    nb, seq, _ = y_a.shape

    def body(dm_ref, a_ref, b_ref, ga_ref, gb_ref, bg_ref, dya_ref, dyb_ref, dg_ref, s_ref):
        @pl.when((pl.program_id(0) == 0) & (pl.program_id(1) == 0))
        def _():
            s_ref[...] = jnp.zeros_like(s_ref)

        dm = dm_ref[...]
        sa = _sigmoid(ga_ref[...] + bg_ref[0:1, :])
        sb = _sigmoid(gb_ref[...] + bg_ref[1:2, :])
        dya_ref[...] = (dm * sa).astype(BF16)
        dyb_ref[...] = (dm * sb).astype(BF16)
        dga = dm * a_ref[...] * (sa * (1.0 - sa))
        dgb = dm * b_ref[...] * (sb * (1.0 - sb))
        dg_ref[:, :D_MODEL] = dga.astype(BF16)
        dg_ref[:, D_MODEL:] = dgb.astype(BF16)
        s_ref[0:1, :] += jnp.sum(dga, 0, keepdims=True)
        s_ref[1:2, :] += jnp.sum(dgb, 0, keepdims=True)

    spec = pl.BlockSpec((None, EW_TM, D_MODEL), lambda b, i: (b, i, 0))
    spec1 = pl.BlockSpec((None, EW_TM, D_MODEL), lambda b, i: (b, i, 1))
    small = pl.BlockSpec((8, D_MODEL), lambda b, i: (0, 0))
    return _pcall(
        body, name="merge_bwd", grid=(nb, seq // EW_TM),
        in_specs=[spec, spec, spec, spec, spec1, small],
        out_specs=[spec, spec, pl.BlockSpec((None, EW_TM, 2 * D_MODEL), lambda b, i: (b, i, 0)), small],
        out_shape=[jax.ShapeDtypeStruct((nb, seq, D_MODEL), BF16), jax.ShapeDtypeStruct((nb, seq, D_MODEL), BF16),
                   jax.ShapeDtypeStruct((nb, seq, 2 * D_MODEL), BF16), jax.ShapeDtypeStruct((8, D_MODEL), F32)],
        compiler_params=_params("arbitrary", "arbitrary"),
    )(dmerged, y_a, y_b, gm, gm, bgate)


def _ln_loss(x, mix, gp, pw, target, bgate, ln_g, ln_b):
    nb, seq, _ = x.shape

    def body(x_ref, mix_ref, gp_ref, pw_ref, t_ref, bg_ref, g_ref, b_ref, dx_ref, dp_ref, dpw_ref, dgp_ref, s_ref):
        @pl.when((pl.program_id(0) == 0) & (pl.program_id(1) == 0))
        def _():
            s_ref[...] = jnp.zeros_like(s_ref)

        sp = _sigmoid(gp_ref[...] + bg_ref[2:3, :])
        pw = pw_ref[...]
        pre = ALPHA * x_ref[...] + mix_ref[...] + sp * pw
        mu = jnp.mean(pre, -1, keepdims=True)
        cen = pre - mu
        rstd = lax.rsqrt(jnp.mean(cen * cen, -1, keepdims=True) + LN_EPS)
        xhat = cen * rstd
        err = xhat * g_ref[...] + b_ref[...] - t_ref[...]
        dy = err * (1.0 / D_MODEL)
        dxh = dy * g_ref[...]
        dpre = rstd * (dxh - jnp.mean(dxh, -1, keepdims=True) - xhat * jnp.mean(dxh * xhat, -1, keepdims=True))
        dx_ref[...] = ALPHA * dpre
        dp_ref[...] = dpre.astype(BF16)
        dpw_ref[...] = (dpre * sp).astype(BF16)
        dgp = dpre * pw * (sp * (1.0 - sp))
        dgp_ref[...] = dgp.astype(BF16)
        s_ref[0:1, :] += jnp.sum(dy * xhat, 0, keepdims=True)
        s_ref[1:2, :] += jnp.sum(dy, 0, keepdims=True)
        s_ref[2:3, :] += jnp.sum(dgp, 0, keepdims=True)
        s_ref[3:4, :] += jnp.sum(err * err, 0, keepdims=True)

    spec = pl.BlockSpec((None, EW_TM, D_MODEL), lambda b, i: (b, i, 0))
    small = pl.BlockSpec((8, D_MODEL), lambda b, i: (0, 0))
    row = pl.BlockSpec((1, D_MODEL), lambda b, i: (0, 0))
    return _pcall(
        body, name="ln_loss", grid=(nb, seq // EW_TM),
        in_specs=[spec] * 5 + [small, row, row], out_specs=[spec] * 4 + [small],
        out_shape=[jax.ShapeDtypeStruct((nb, seq, D_MODEL), F32)] + [jax.ShapeDtypeStruct((nb, seq, D_MODEL), BF16)] * 3
        + [jax.ShapeDtypeStruct((8, D_MODEL), F32)],
        compiler_params=_params("arbitrary", "arbitrary"),
    )(x, mix, gp, pw, target, bgate, ln_g, ln_b)


def _adamw(w, g, m, v, name):
    rows, cols = w.shape
    tr = _row_tile(rows, cols, 8, 5 << 19)
    c1 = 1.0 - ADAM_B1 ** ADAM_STEP
    c2 = 1.0 - ADAM_B2 ** ADAM_STEP

    def body(w_ref, g_ref, m_ref, v_ref, d_ref, nm_ref, nv_ref):
        gv = g_ref[...]
        nm = ADAM_B1 * m_ref[...] + (1.0 - ADAM_B1) * gv
        nv = ADAM_B2 * v_ref[...] + (1.0 - ADAM_B2) * (gv * gv)
        d_ref[...] = -ADAM_LR * ((nm / c1) / (jnp.sqrt(nv / c2) + ADAM_EPS) + ADAM_WD * w_ref[...])
        nm_ref[...] = nm
        nv_ref[...] = nv

    spec = pl.BlockSpec((tr, cols), lambda i: (i, 0))
    return _pcall(
        body, name=name, grid=(rows // tr,), in_specs=[spec] * 4, out_specs=[spec] * 3,
        out_shape=[jax.ShapeDtypeStruct(w.shape, F32)] * 3, compiler_params=_params("parallel"),
    )(w, g, m, v)


def _sum_rows(parts, out_dtype, name):
    rows, cols = parts[0].shape
    tr = rows
    for cand in range(16, rows, 16):
        if rows % cand == 0 and cand * cols * 4 <= (1 << 20):
            tr = cand
    n = len(parts)

    def body(*refs):
        acc = refs[0][...].astype(F32)
        for r in refs[1:n]:
            acc = acc + r[...].astype(F32)
        refs[n][...] = acc.astype(out_dtype)

    spec = pl.BlockSpec((tr, cols), lambda i: (i, 0))
    return _pcall(
        body, name=name, grid=(rows // tr,), in_specs=[spec] * n, out_specs=spec,
        out_shape=jax.ShapeDtypeStruct((rows, cols), out_dtype), compiler_params=_params("parallel"),
    )(*parts)


def _place():
    return lax.axis_index("x"), lax.axis_index("y"), lax.axis_index("c")


def _other_chips(x, y):
    return [(1 - x, y), (x, 1 - y), (1 - x, 1 - y)]


def _remote(src, dst, send_sem, recv_sem, to):
    return pltpu.make_async_remote_copy(src_ref=src, dst_ref=dst, send_sem=send_sem, recv_sem=recv_sem,
                                        device_id=to, device_id_type=MESH)


ANY = pl.BlockSpec(memory_space=pl.ANY)
DMA_CHUNK_BYTES = 512 * 1024


def _row_chunks(rows, row_bytes):
    per = max(16, DMA_CHUNK_BYTES // row_bytes // 16 * 16)
    return [(s, min(per, rows - s)) for s in range(0, rows, per)]


def _row_tile(rows, cols, align, limit=1 << 21):
    best = None
    for cand in range(align, rows + 1, align):
        if rows % cand == 0 and cand * cols * 4 <= limit:
            best = cand
    return best or rows


def _allgather_pieces(pieces):
    n = len(pieces)
    halves = [_row_chunks(p.shape[0] // 2, p.shape[1] * p.dtype.itemsize) for p in pieces]
    entries = [(a, q, s, m, j) for a in range(n) for q, (s, m) in enumerate(halves[a]) for j in range(3)]
    slot = {(a, q, j): k for k, (a, q, _, _, j) in enumerate(entries)}
    n_ici = len(entries)

    def body(*refs):
        ins, outs = refs[:n], refs[n:2 * n]
        send_sems, recv_sems = refs[2 * n:]
        x, y, c = _place()
        me = 2 * x + y
        sibling = (x, y, 1 - c)
        chips = _other_chips(x, y)

        def landed(a, s, m, j, core):
            half = ins[a].shape[0] // 2
            return outs[a].at[2 * chips[j][0] + chips[j][1], pl.ds(core * half + s, m)]

        sent = []
        for k, (a, q, s, m, j) in enumerate(entries):
            if j < 2:
                half = ins[a].shape[0] // 2
                cp = _remote(ins[a].at[pl.ds(c * half + s, m)], outs[a].at[me, pl.ds(c * half + s, m)],
                             send_sems.at[k], recv_sems.at[k], (*chips[j], c))
                cp.start()
                sent.append(cp)

        def pass_to_sibling(k, blk):
            fw = _remote(blk, blk, send_sems.at[n_ici + k], recv_sems.at[n_ici + k], sibling)
            fw.start()
            sent.append(fw)

        for k, (a, q, s, m, j) in enumerate(entries):
            if j < 2:
                blk = landed(a, s, m, j, c)
                _remote(blk, blk, send_sems.at[k], recv_sems.at[k], (*chips[j], c)).wait_recv()
                first = q < (len(halves[a]) + 1) // 2
                if (j == 0) == first:
                    on = slot[(a, q, 2)]
                    rl = _remote(blk, blk, send_sems.at[on], recv_sems.at[on], (*chips[1 - j], c))
                    rl.start()
                    sent.append(rl)
                pass_to_sibling(k, blk)
        for k, (a, q, s, m, j) in enumerate(entries):
            if j == 2:
                blk = landed(a, s, m, j, c)
                _remote(blk, blk, send_sems.at[k], recv_sems.at[k], (*chips[j], c)).wait_recv()
                pass_to_sibling(k, blk)
        for k, (a, q, s, m, j) in enumerate(entries):
            blk = landed(a, s, m, j, 1 - c)
            _remote(blk, blk, send_sems.at[n_ici + k], recv_sems.at[n_ici + k], sibling).wait_recv()
        for cp in sent:
            cp.wait_send()

    gathered = _pcall(
        body, name="allgather_weights", in_specs=[ANY] * n, out_specs=[ANY] * n,
        out_shape=[jax.ShapeDtypeStruct((4,) + p.shape, p.dtype) for p in pieces],
        scratch_shapes=[pltpu.SemaphoreType.DMA((2 * n_ici,)), pltpu.SemaphoreType.DMA((2 * n_ici,))],
        compiler_params=pltpu.CompilerParams(has_side_effects=True),
    )(*pieces)
    x, y, _ = _place()
    return [lax.dynamic_update_slice(g, p[None], (2 * x + y, 0, 0)) for g, p in zip(gathered, pieces)]


def _sibling_exchange(grads):
    n = len(grads)
    chunks = [_row_chunks(g.shape[1] // 2, g.shape[2] * g.dtype.itemsize) for g in grads]
    n_sem = 4 * sum(len(ch) for ch in chunks)

    def body(*refs):
        ins, gots = refs[:n], refs[n:2 * n]
        send_sems, recv_sems = refs[2 * n:]
        x, y, c = _place()
        sibling = (x, y, 1 - c)
        work = []
        for a in range(n):
            half = ins[a].shape[1] // 2
            for piece in range(4):
                for s, m in chunks[a]:
                    k = len(work)
                    cp = _remote(ins[a].at[piece, pl.ds((1 - c) * half + s, m)], gots[a].at[piece, pl.ds(s, m)],
                                 send_sems.at[k], recv_sems.at[k], sibling)
                    cp.start()
                    work.append(cp)
        for cp in work:
            cp.wait()

    return _pcall(
        body, name="grad_sibling_exchange", in_specs=[ANY] * n, out_specs=[ANY] * n,
        out_shape=[jax.ShapeDtypeStruct((4, g.shape[1] // 2, g.shape[2]), g.dtype) for g in grads],
        scratch_shapes=[pltpu.SemaphoreType.DMA((n_sem,)), pltpu.SemaphoreType.DMA((n_sem,))],
        compiler_params=pltpu.CompilerParams(has_side_effects=True),
    )(*grads)


def _sibling_gather(fulls):
    n = len(fulls)
    chunks = [_row_chunks(f.shape[0] // 2, f.shape[1] * f.dtype.itemsize) for f in fulls]
    n_sem = sum(len(ch) for ch in chunks)

    def body(*refs):
        outs = refs[n:2 * n]
        send_sems, recv_sems = refs[2 * n:]
        x, y, c = _place()
        sibling = (x, y, 1 - c)
        work = []
        for a in range(n):
            h = outs[a].shape[0] // 2
            for s, m in chunks[a]:
                k = len(work)
                mine = outs[a].at[pl.ds(c * h + s, m)]
                cp = _remote(mine, mine, send_sems.at[k], recv_sems.at[k], sibling)
                cp.start()
                work.append((a, s, m, cp))
        for k, (a, s, m, cp) in enumerate(work):
            h = outs[a].shape[0] // 2
            cp.wait_send()
            theirs = outs[a].at[pl.ds((1 - c) * h + s, m)]
            _remote(theirs, theirs, send_sems.at[k], recv_sems.at[k], sibling).wait_recv()

    return _pcall(
        body, name="grad_sibling_gather", in_specs=[ANY] * n, out_specs=[ANY] * n,
        out_shape=[jax.ShapeDtypeStruct(f.shape, f.dtype) for f in fulls],
        input_output_aliases={a: a for a in range(n)},
        scratch_shapes=[pltpu.SemaphoreType.DMA((n_sem,)), pltpu.SemaphoreType.DMA((n_sem,))],
        compiler_params=pltpu.CompilerParams(has_side_effects=True),
    )(*fulls)


def _pair_sum(grad, got, place, name):
    _, rows, cols = grad.shape
    half = rows // 2
    tr = _row_tile(half, cols, 16)

    def body(p_ref, a_ref, b_ref, o_ref):
        o_ref[...] = (a_ref[...].astype(F32) + b_ref[...].astype(F32)).astype(BF16)

    return _pcall(
        body, name=name,
        grid_spec=pltpu.PrefetchScalarGridSpec(
            num_scalar_prefetch=1, grid=(4, half // tr),
            in_specs=[pl.BlockSpec((None, tr, cols), lambda k, i, p: (k, p[1] * (half // tr) + i, 0)),
                      pl.BlockSpec((None, tr, cols), lambda k, i, p: (k, i, 0))],
            out_specs=pl.BlockSpec((None, tr, cols), lambda k, i, p: (k, i, 0))),
        out_shape=jax.ShapeDtypeStruct((4, half, cols), BF16),
        compiler_params=_params("parallel", "parallel"),
    )(place, grad, got)


def _chip_sum(sums, got, place, name):
    _, h, cols = sums.shape
    tr = _row_tile(h, cols, 16)

    def body(p_ref, own_ref, g0, g1, g2, o_ref):
        o_ref[...] = ((own_ref[...].astype(F32) + g0[...].astype(F32)) + g1[...].astype(F32)) + g2[...].astype(F32)

    gspec = lambda j: pl.BlockSpec((None, tr, cols), lambda i, p: (j, i, 0))
    return _pcall(
        body, name=name,
        grid_spec=pltpu.PrefetchScalarGridSpec(
            num_scalar_prefetch=1, grid=(h // tr,),
            in_specs=[pl.BlockSpec((None, tr, cols), lambda i, p: (p[0], i, 0)), gspec(0), gspec(1), gspec(2)],
            out_specs=pl.BlockSpec((tr, cols), lambda i, p: (p[1] * (h // tr) + i, 0))),
        out_shape=jax.ShapeDtypeStruct((2 * h, cols), F32),
        compiler_params=_params("parallel"),
    )(place, sums, got, got, got)


def _allgather8(buf, name):
    rows = buf.shape[0]

    def body(in_ref, out_ref, send_sems, recv_sems):
        x, y, c = _place()
        me = 4 * x + 2 * y + c
        out_ref[me] = in_ref[...]
        work = []
        for rel in range(1, 8):
            fx, fy, fc = (rel >> 2) & 1, (rel >> 1) & 1, rel & 1
            to = (x ^ fx, y ^ fy, c ^ fc)
            cp = _remote(in_ref, out_ref.at[me], send_sems.at[rel - 1], recv_sems.at[rel - 1], to)
            cp.start()
            work.append((cp, 4 * to[0] + 2 * to[1] + to[2]))
        for rel, (cp, frm) in enumerate(work):
            cp.wait_send()
            blk = out_ref.at[frm]
            _remote(blk, blk, send_sems.at[rel], recv_sems.at[rel], (x, y, c)).wait_recv()

    return _pcall(
        body, name=name, in_specs=[pl.BlockSpec(memory_space=pltpu.VMEM)],
        out_specs=pl.BlockSpec(memory_space=pltpu.VMEM),
        out_shape=jax.ShapeDtypeStruct((8, rows, LANE), F32),
        scratch_shapes=[pltpu.SemaphoreType.DMA((7,)), pltpu.SemaphoreType.DMA((7,))],
        compiler_params=pltpu.CompilerParams(has_side_effects=True),
    )(buf)


def _pack_rows(arrs):
    parts = []
    for a in arrs:
        f = a.reshape(-1).astype(F32)
        parts.append(jnp.pad(f, (0, (-f.shape[0]) % LANE)))
    flat = jnp.concatenate(parts)
    rows = -(-flat.shape[0] // LANE)
    rows8 = -(-rows // 8) * 8
    return jnp.pad(flat, (0, rows8 * LANE - flat.shape[0])).reshape(rows8, LANE)


def _unpack_rows(buf, shapes):
    flat = buf.reshape(-1)
    outs, off = [], 0
    for s in shapes:
        n = int(np.prod(s))
        outs.append(flat[off:off + n].reshape(s))
        off += -(-n // LANE) * LANE
    return outs


def _local_grads(x, p, target, wseg, w_br16, w_out16, w_ple16, b_gate, conv_w, conv_b, dt_bias, a_log, d_skip,
                 ssm_norm_w, ln_g, ln_b, rel_bias, finish_dx):
    nb, seq, _ = x.shape
    bmaps = jnp.asarray(_bucket_maps())
    bias = _bias_tables(rel_bias, bmaps)
    bgate8 = jnp.pad(b_gate, ((0, 5), (0, 0)))
    dils = [d for _, d in PATTERNS]

    x16 = x.astype(BF16)
    p16 = p.astype(BF16)
    x16p = [_permute(x16, d) for d in dils]
    qkv = [_proj(x16p[g], [wseg["qkv%d" % g]], BF16, "proj_qkv%d" % g, True)[0].reshape(
        nb, dils[g], seq // dils[g], -1) for g in range(3)]
    nat = {}
    for gi, (group, tm) in enumerate(NAT_GROUPS):
        outs = _proj(x16, [wseg[s] for s in group], F32, "proj_nat%d" % gi, True, tm)
        nat.update(zip(group, outs))
    att = [_attn_fwd(qkv[g], bias[g * GROUP_HEADS:(g + 1) * GROUP_HEADS], dils[g], "attn_fwd%d" % g) for g in range(3)]
    natural = lambda t, g: _unpermute(t.reshape(nb, seq, t.shape[-1]), dils[g])
    oa, o_att, lse = _combine_fwd(att[0][0], att[0][1], natural(att[1][0], 1), natural(att[1][1], 1),
                                  natural(att[2][0], 2), natural(att[2][1], 2), nat["gatt"])

    conv_wg, conv_bg = _xbc_group_order(conv_w), _xbc_group_order(conv_b)
    act = _conv_fwd(nat["xbc"], conv_wg, conv_bg, "conv_fwd")
    dt_sp, dt_sg = _softplus_sig(nat["dt"], jnp.pad(dt_bias, ((0, 0), (0, LANE - SSM_HEADS))))
    dtg, sgg = _group_lanes(dt_sp), _group_lanes(dt_sg)
    alog_g, dskip_g = _group_lanes(a_log), _group_lanes(d_skip)
    y_ssm, y_all, sprev = _ssd_fwd(act, dtg, nat["z"], alog_g, dskip_g, ssm_norm_w)

    w_bra, w_brb = w_br16[:ATT_OUT], w_br16[ATT_OUT:]
    y_a, = _proj(oa, [w_bra], F32, "proj_ya")
    y_b, = _proj(y_ssm, [w_brb], F32, "proj_yb")
    merged = _merge_fwd(y_a, y_b, nat["gm"], bgate8)
    mix, = _proj(merged, [w_out16], F32, "proj_mix")
    pw, = _proj(p16, [w_ple16], F32, "proj_ple")

    dx, dpre16, dpw16, dgp16, ln_sums = _ln_loss(x, mix, nat["gp"], pw, target, bgate8, ln_g, ln_b)
    loss_sum = (0.5 / D_MODEL) * jnp.sum(ln_sums[3])
    dmerged = _dx([dpre16], [w_out16], [], "dx_merged")
    dya16, dyb16, dgm16, mg_sums = _merge_bwd(dmerged, y_a, y_b, nat["gm"], bgate8)
    doa = _dx([dya16], [w_bra], [], "dx_oa")
    dys = _dx([dyb16], [w_brb], [], "dx_yssm")
    g_w_out, = _dw(merged, [dpre16], BF16, "dw_out")
    g_w_br = jnp.concatenate([_dw(oa, [dya16], BF16, "dw_bra")[0], _dw(y_ssm, [dyb16], BF16, "dw_brb")[0]], axis=0)
    g_w_ple, = _dw(p16, [dpw16], BF16, "dw_ple")

    do_att, do16, stats, dgatt16 = _combine_bwd(doa, nat["gatt"], o_att, lse)
    dseg = {"gatt": dgatt16, "gm": dgm16, "gp": dgp16}
    dbias = []
    for g in range(3):
        own_order = lambda t: _permute(t, dils[g]).reshape(nb, dils[g], seq // dils[g], t.shape[-1])
        cotangent = (do_att, o_att, lse) if g == 0 else (own_order(do16), own_order(stats))
        dqkv, db = _attn_bwd(qkv[g], bias[g * GROUP_HEADS:(g + 1) * GROUP_HEADS], cotangent, dils[g],
                             "attn_bwd%d" % g)
        dseg["qkv%d" % g] = dqkv.reshape(nb, seq, -1)
        dbias.append(db)
    g_rel = _bias_grad(jnp.concatenate(dbias, axis=0), bmaps)[:, 0, :NUM_BUCKETS].T

    dact, ddtg, dz, ssd_small, g_normw = _ssd_bwd(
        act, dtg, sgg, nat["z"], y_all, dys, sprev, alog_g, dskip_g, ssm_norm_w)
    dseg["z"] = dz
    dseg["dt"] = jnp.pad(_ungroup_lanes(ddtg), ((0, 0), (0, 0), (0, LANE - SSM_HEADS)))
    dpre, conv_sums = _conv_bwd_pre(dact, nat["xbc"], conv_wg, conv_bg, "conv_bwd")
    dseg["xbc"] = _conv_bwd_x(dpre, conv_wg, "conv_bwd_x")
    csum = _xbc_reference_order(conv_sums)

    dx_perm = [_unpermute(_dx([dseg["qkv%d" % g]], [wseg["qkv%d" % g]], [], "dx_qkv%d" % g, True), dils[g])
               for g in (1, 2)]
    dwseg = {"qkv%d" % g: _dw(x16p[g], [dseg["qkv%d" % g]], BF16, "dw_qkv%d" % g, True)[0] for g in range(3)}
    for gi, group in enumerate(DW_GROUPS):
        dwseg.update(zip(group, _dw(x16, [dseg[s] for s in group], BF16, "dw_nat%d" % gi, True)))
    names = ["qkv0"] + [s for group, _ in NAT_GROUPS for s in group]
    dx = finish_dx([dseg[s] for s in names], [wseg[s] for s in names], [dx] + dx_perm, dwseg, g_w_br, g_w_out, g_w_ple)

    small = dict(
        b_gate=jnp.stack([mg_sums[0], mg_sums[1], ln_sums[2]]),
        conv_w=csum[0:4], conv_b=csum[4:5],
        dt_bias=_ungroup_lanes(ssd_small[:, 2:3, :]), a_log=_ungroup_lanes(ssd_small[:, 0:1, :]),
        d_skip=_ungroup_lanes(ssd_small[:, 1:2, :]), ssm_norm_w=g_normw,
        ln_g=ln_sums[0:1], ln_b=ln_sums[1:2], rel_bias=g_rel)
    return loss_sum, dx, small


DX_TM = 256
SMALL_ORDER = ("b_gate", "conv_w", "conv_b", "dt_bias", "a_log", "d_skip", "ssm_norm_w", "ln_g", "ln_b", "rel_bias")
SMALL_FULL_SHAPES = dict(b_gate=(3, 1024), conv_w=(4, 3072), conv_b=(1, 3072), dt_bias=(1, 32), a_log=(1, 32),
                         d_skip=(1, 32), ssm_norm_w=(1, 2048), ln_g=(1, 1024), ln_b=(1, 1024), rel_bias=(32, 36))


def kernel(x, p, w_in, b_gate, conv_w, conv_b, dt_bias, a_log, d_skip, ssm_norm_w, w_branch, w_out, w_ple, ln_g, ln_b, rel_bias, loss_target, m_w_in, m_b_gate, m_conv_w, m_conv_b, m_dt_bias, m_a_log, m_d_skip, m_ssm_norm_w, m_w_branch, m_w_out, m_w_ple, m_ln_g, m_ln_b, m_rel_bias, v_w_in, v_b_gate, v_conv_w, v_conv_b, v_dt_bias, v_a_log, v_d_skip, v_ssm_norm_w, v_w_branch, v_w_out, v_w_ple, v_ln_g, v_ln_b, v_rel_bias):
    cx, cy, cc = _place()
    chip = 2 * cx + cy
    dev = 4 * cx + 2 * cy + cc

    w_in_t = jnp.transpose(w_in[0])
    win16 = _shard_to_window(w_in_t, chip)
    g_win, g_br, g_out, g_ple = _allgather_pieces(
        [win16, w_branch[0].astype(BF16), w_out[0].astype(BF16), w_ple[0].astype(BF16)])
    wseg = _assemble(g_win)
    w_br16 = g_br.reshape(4 * 704, D_MODEL)
    w_out16 = g_out.reshape(D_MODEL, D_MODEL)
    w_ple16 = jnp.transpose(g_ple, (1, 0, 2)).reshape(PLE_DIM, D_MODEL)
    shards = _allgather8(_pack_rows([b_gate[0], conv_w[0]]), "allgather_small_params")
    per_chip = [_unpack_rows(shards[2 * k], [(3, 256), (4, 768)]) for k in range(4)]
    b_gate_full = jnp.concatenate([pc[0] for pc in per_chip], axis=1)
    conv_w_full = jnp.concatenate([pc[1] for pc in per_chip], axis=1)

    place = jnp.stack([chip, cc]).astype(jnp.int32)
    reduced = []

    def finish_dx(dhs, ws, accs, dwseg, d_br, d_out, d_ple):
        grads = [_pack(dwseg), d_br.reshape(4, 704, D_MODEL), d_out.reshape(4, 256, D_MODEL),
                 jnp.transpose(d_ple.reshape(PLE_DIM, 4, 256), (1, 0, 2))]
        got = _sibling_exchange(grads)
        chip_sums = [_pair_sum(g, t, place, "grad_pair_sum_%d" % i) for i, (g, t) in enumerate(zip(grads, got))]
        dx, others = _dx(dhs, ws, accs, "dx_w_in_and_grad_chip_scatter", True, DX_TM, chip_sums)
        fulls = [_chip_sum(s, t, place, "grad_chip_sum_%d" % i) for i, (s, t) in enumerate(zip(chip_sums, others))]
        reduced.extend(_sibling_gather(fulls))
        return dx

    loss_sum, grad_x, small = _local_grads(
        x, p[0], loss_target, wseg, w_br16, w_out16, w_ple16, b_gate_full, conv_w_full, conv_b, dt_bias, a_log,
        d_skip, ssm_norm_w, ln_g, ln_b, rel_bias, finish_dx)
    big = reduced
    g_w_in = _window_to_shard(big[0], chip)
    g_w_branch, g_w_out, g_w_ple = big[1], big[2], big[3]
    parts = _allgather8(_pack_rows([small[n] for n in SMALL_ORDER] + [loss_sum.reshape(1, 1)]),
                        "allgather_small_grads")
    small_sum = _sum_rows([parts[i] for i in range(8)], F32, "small_grad_sum")
    *reduced_small, loss = _unpack_rows(small_sum, [SMALL_FULL_SHAPES[n] for n in SMALL_ORDER] + [(1, 1)])
    loss = loss.reshape(())
    sg = dict(zip(SMALL_ORDER, reduced_small))
    sg["b_gate"] = lax.dynamic_slice_in_dim(sg["b_gate"], chip * 256, 256, axis=1)
    sg["conv_w"] = lax.dynamic_slice_in_dim(sg["conv_w"], chip * 768, 768, axis=1)
    del dev

    upd = {}
    upd["w_in"] = [jnp.transpose(t) for t in _adamw(w_in_t, g_w_in, jnp.transpose(m_w_in[0]),
                                                      jnp.transpose(v_w_in[0]), "adamw_w_in")]
    upd["w_branch"] = _adamw(w_branch[0], g_w_branch, m_w_branch[0], v_w_branch[0], "adamw_w_branch")
    upd["w_out"] = _adamw(w_out[0], g_w_out, m_w_out[0], v_w_out[0], "adamw_w_out")
    upd["w_ple"] = _adamw(w_ple[0], g_w_ple, m_w_ple[0], v_w_ple[0], "adamw_w_ple")
    small_w = dict(b_gate=b_gate, conv_w=conv_w, conv_b=conv_b, dt_bias=dt_bias, a_log=a_log, d_skip=d_skip,
                   ssm_norm_w=ssm_norm_w, ln_g=ln_g, ln_b=ln_b, rel_bias=rel_bias)
    small_m = dict(b_gate=m_b_gate, conv_w=m_conv_w, conv_b=m_conv_b, dt_bias=m_dt_bias, a_log=m_a_log,
                   d_skip=m_d_skip, ssm_norm_w=m_ssm_norm_w, ln_g=m_ln_g, ln_b=m_ln_b, rel_bias=m_rel_bias)
    small_v = dict(b_gate=v_b_gate, conv_w=v_conv_w, conv_b=v_conv_b, dt_bias=v_dt_bias, a_log=v_a_log,
                   d_skip=v_d_skip, ssm_norm_w=v_ssm_norm_w, ln_g=v_ln_g, ln_b=v_ln_b, rel_bias=v_rel_bias)
    shapes = [small_w[n].shape for n in SMALL_ORDER]
    s_delta, s_m, s_v = _adamw(_pack_rows([small_w[n] for n in SMALL_ORDER]), _pack_rows([sg[n] for n in SMALL_ORDER]),
                               _pack_rows([small_m[n] for n in SMALL_ORDER]), _pack_rows([small_v[n] for n in SMALL_ORDER]),
                               "adamw_small")
    for i, n in enumerate(SMALL_ORDER):
        upd[n] = tuple(_unpack_rows(t, shapes)[i] for t in (s_delta, s_m, s_v))
        sg[n] = sg[n].reshape(small_w[n].shape)

    order = ("w_in", "b_gate", "conv_w", "conv_b", "dt_bias", "a_log", "d_skip", "ssm_norm_w", "w_branch", "w_out",
             "w_ple", "ln_g", "ln_b", "rel_bias")
    grads = dict(sg, w_in=jnp.transpose(g_w_in)[None],w_branch=g_w_branch[None], w_out=g_w_out[None], w_ple=g_w_ple[None])
    lead = lambda n, t: t[None] if n in ("w_in", "w_branch", "w_out", "w_ple") else t
    return (loss, grad_x, *[grads[n] for n in order], *[lead(n, upd[n][0]) for n in order],
            *[lead(n, upd[n][1]) for n in order], *[lead(n, upd[n][2]) for n in order])
```

```python
import functools
import math

import numpy as np
import jax
import jax.numpy as jnp
from jax import lax
from jax.experimental import pallas as pl
from jax.experimental.pallas import tpu as pltpu

F32, BF16 = jnp.float32, jnp.bfloat16

D_MODEL = 1024
HEAD_DIM = 64
GROUP_HEADS = 12
ATT_OUT = GROUP_HEADS * HEAD_DIM
PATTERNS = ((128, 1), (512, 4), (2048, 16))
BAND = 128
NUM_BUCKETS = 32
MAX_DISTANCE = 2048
D_INNER = 2048
SSM_HEADS = 32
SSM_GROUPS = 4
GROUP_SSM_HEADS = SSM_HEADS // SSM_GROUPS
D_STATE = 128
CHUNK = 128
PLE_DIM = 256
ALPHA = 2.0 ** 0.25
LN_EPS = 1e-5
RMS_EPS = 1e-5
ADAM_LR, ADAM_B1, ADAM_B2, ADAM_EPS, ADAM_WD, ADAM_STEP = 0.001, 0.9, 0.999, 1e-08, 0.01, 10
NEG = -1e30

QKV_W = 3 * ATT_OUT
IN_COLS = 15904
SHARD_COLS = IN_COLS // 4
DT_COL = 12800
ROW_TILE = 16
WIN_ROWS = 4000


def _win_offset(k):
    return (k * SHARD_COLS) % ROW_TILE


def _win_start(k):
    return k * SHARD_COLS - _win_offset(k)

VMEM_LIMIT_BYTES = 56 * 1024 * 1024
LANE = 128
MESH = pl.DeviceIdType.MESH
NT = (((1,), (1,)), ((), ()))
TN = (((0,), (0,)), ((), ()))


def _pcall(body, **kw):
    return pl.pallas_call(body, **kw)


def _params(*sem):
    return pltpu.CompilerParams(dimension_semantics=sem, vmem_limit_bytes=VMEM_LIMIT_BYTES)


def _sigmoid(v):
    return jax.nn.sigmoid(v)


MM_TM = 512


def _permute(t, d):
    nb, seq, ch = t.shape
    return t if d == 1 else t.reshape(nb, seq // d, d, ch).transpose(0, 2, 1, 3).reshape(nb, seq, ch)


def _unpermute(t, d):
    nb, seq, ch = t.shape
    return t if d == 1 else t.reshape(nb, d, seq // d, ch).transpose(0, 2, 1, 3).reshape(nb, seq, ch)


def _tok_spec(tm, width):
    return pl.BlockSpec((None, tm, width), lambda b, i: (b, i, 0))


def _whole(arr, single_buffer=False):
    mode = dict(pipeline_mode=pl.Buffered(1)) if single_buffer else {}
    return pl.BlockSpec(arr.shape, lambda b, i: (0,) * arr.ndim, **mode)


def _proj(a3, ws, out_dtype, name, w_rows_are_outputs=False, tm=MM_TM):
    nb, seq, kdim = a3.shape
    nw = len(ws)
    widths = [w.shape[0] if w_rows_are_outputs else w.shape[1] for w in ws]

    def body(*refs):
        a = refs[0][...].astype(BF16)
        for w_ref, o_ref in zip(refs[1:1 + nw], refs[1 + nw:]):
            if w_rows_are_outputs:
                v = lax.dot_general(a, w_ref[...], NT, preferred_element_type=F32)
            else:
                v = jnp.dot(a, w_ref[...], preferred_element_type=F32)
            o_ref[...] = v.astype(out_dtype)

    return _pcall(
        body, name=name, grid=(nb, seq // tm),
        in_specs=[_tok_spec(tm, kdim)] + [_whole(w) for w in ws],
        out_specs=[_tok_spec(tm, n) for n in widths],
        out_shape=[jax.ShapeDtypeStruct((nb, seq, n), out_dtype) for n in widths],
        compiler_params=_params("parallel", "parallel"),
    )(a3, *ws)


def _dx(dhs, ws, accs, name, w_rows_are_outputs=False, tm=MM_TM, scatter=None):
    nb, seq, _ = dhs[0].shape
    nd, nacc = len(dhs), len(accs)
    kout = ws[0].shape[1] if w_rows_are_outputs else ws[0].shape[0]
    sums = scatter or []
    ns = len(sums)
    chunks = [_row_chunks(s.shape[1], s.shape[2] * s.dtype.itemsize) for s in sums]
    n_sem = 3 * sum(len(ch) for ch in chunks)
    grid = (nb, seq // tm)

    def body(*refs):
        n_in = 2 * nd + nacc
        sum_refs, o_ref, got_refs = refs[n_in:n_in + ns], refs[n_in + ns], refs[n_in + ns + 1:n_in + 2 * ns + 1]

        def copies():
            send_sems, recv_sems = refs[-2], refs[-1]
            x, y, c = _place()
            out = []
            for a in range(ns):
                for s, m in chunks[a]:
                    for j, (cx, cy) in enumerate(_other_chips(x, y)):
                        k = len(out)
                        out.append(_remote(sum_refs[a].at[2 * cx + cy, pl.ds(s, m)], got_refs[a].at[j, pl.ds(s, m)],
                                           send_sems.at[k], recv_sems.at[k], (cx, cy, c)))
            return out

        if ns:
            @pl.when((pl.program_id(0) == 0) & (pl.program_id(1) == 0))
            def _():
                for cp in copies():
                    cp.start()

        v = None
        for dh_ref, w_ref in zip(refs[:nd], refs[nd:2 * nd]):
            dh = dh_ref[...].astype(BF16)
            if w_rows_are_outputs:
                t = jnp.dot(dh, w_ref[...], preferred_element_type=F32)
            else:
                t = lax.dot_general(dh, w_ref[...], NT, preferred_element_type=F32)
            v = t if v is None else v + t
        for a_ref in refs[2 * nd:n_in]:
            v = v + a_ref[...]
        o_ref[...] = v

        if ns:
            @pl.when((pl.program_id(0) == grid[0] - 1) & (pl.program_id(1) == grid[1] - 1))
            def _():
                for cp in copies():
                    cp.wait()

    out = _pcall(
        body, name=name, grid=grid,
        in_specs=[_tok_spec(tm, dh.shape[-1]) for dh in dhs] + [_whole(w, bool(ns)) for w in ws]
        + [_tok_spec(tm, kout)] * nacc + [ANY] * ns,
        out_specs=[_tok_spec(tm, kout)] + [ANY] * ns,
        out_shape=[jax.ShapeDtypeStruct((nb, seq, kout), F32)]
        + [jax.ShapeDtypeStruct((3,) + s.shape[1:], s.dtype) for s in sums],
        input_output_aliases={2 * nd: 0} if nacc else {},
        scratch_shapes=[pltpu.SemaphoreType.DMA((n_sem,)), pltpu.SemaphoreType.DMA((n_sem,))] if ns else [],
        compiler_params=pltpu.CompilerParams(
            dimension_semantics=("arbitrary", "arbitrary") if ns else ("parallel", "parallel"),
            vmem_limit_bytes=VMEM_LIMIT_BYTES, has_side_effects=bool(ns)),
    )(*dhs, *ws, *accs, *sums)
    return (out[0], list(out[1:])) if ns else out[0]


def _dw(a3, dhs, out_dtype, name, rows_are_outputs=False):
    nb, seq, kdim = a3.shape
    nd = len(dhs)
    grid = (nb, seq // MM_TM)
    shapes = [(dh.shape[-1], kdim) if rows_are_outputs else (kdim, dh.shape[-1]) for dh in dhs]

    def body(*refs):
        b, i = pl.program_id(0), pl.program_id(1)
        dh_refs, o_refs, acc_refs = refs[1:1 + nd], refs[1 + nd:1 + 2 * nd], refs[1 + 2 * nd:]

        @pl.when((b == 0) & (i == 0))
        def _():
            for acc_ref in acc_refs:
                acc_ref[...] = jnp.zeros_like(acc_ref)

        a = refs[0][...].astype(BF16)
        for dh_ref, acc_ref in zip(dh_refs, acc_refs):
            dh = dh_ref[...].astype(BF16)
            acc_ref[...] += lax.dot_general(*((dh, a) if rows_are_outputs else (a, dh)), TN,
                                            preferred_element_type=F32)

        @pl.when((b == grid[0] - 1) & (i == grid[1] - 1))
        def _():
            for o_ref, acc_ref in zip(o_refs, acc_refs):
                o_ref[...] = acc_ref[...].astype(out_dtype)

    return _pcall(
        body, name=name, grid=grid,
        in_specs=[_tok_spec(MM_TM, kdim)] + [_tok_spec(MM_TM, dh.shape[-1]) for dh in dhs],
        out_specs=[pl.BlockSpec(s, lambda b, i: (0, 0)) for s in shapes],
        out_shape=[jax.ShapeDtypeStruct(s, out_dtype) for s in shapes],
        scratch_shapes=[pltpu.VMEM(s, F32) for s in shapes],
        compiler_params=_params("arbitrary", "arbitrary"),
    )(a3, *dhs)


def _qkv_rows(g):
    return [(part * QKV_W + g * ATT_OUT + hp * LANE, LANE) for hp in range(ATT_OUT // LANE) for part in range(3)]


XBC_START = 3 * QKV_W + ATT_OUT + D_INNER
GROUP_CH = GROUP_SSM_HEADS * HEAD_DIM
XBC_GROUP = GROUP_CH + 2 * D_STATE
CONV_DIM = SSM_GROUPS * XBC_GROUP


def _xbc_ranges():
    out = []
    for g in range(SSM_GROUPS):
        out += [(g * GROUP_CH, GROUP_CH), (D_INNER + g * D_STATE, D_STATE),
                (D_INNER + SSM_GROUPS * D_STATE + g * D_STATE, D_STATE)]
    return out


def _xbc_group_order(t):
    return jnp.concatenate([t[..., s:s + n] for s, n in _xbc_ranges()], axis=-1)


def _xbc_reference_order(t):
    g = lambda off, n: [t[..., k * XBC_GROUP + off:k * XBC_GROUP + off + n] for k in range(SSM_GROUPS)]
    return jnp.concatenate(g(0, GROUP_CH) + g(GROUP_CH, D_STATE) + g(GROUP_CH + D_STATE, D_STATE), axis=-1)


def _segments():
    one = lambda name, start, rows: (name, [(start, rows)], max(rows, LANE))
    return [("qkv%d" % g, _qkv_rows(g), QKV_W) for g in range(3)] + [
        one("gatt", 3 * QKV_W, ATT_OUT), one("z", 3 * QKV_W + ATT_OUT, D_INNER),
        ("xbc", [(XBC_START + s, n) for s, n in _xbc_ranges()], CONV_DIM), one("dt", DT_COL, SSM_HEADS),
        one("gm", DT_COL + SSM_HEADS, 2 * D_MODEL), one("gp", DT_COL + SSM_HEADS + 2 * D_MODEL, D_MODEL)]


LAYOUT_TC = 256
NAT_GROUPS = ((("gatt", "z", "dt", "gp"), 512), (("xbc", "gm"), 256))
DW_GROUPS = (("gatt", "z", "dt", "gp"), ("xbc",), ("gm",))


def _assemble(win):
    segs = _segments()

    def body(win_ref, *outs):
        def pieces(start, rows):
            t, end = start, start + rows
            while t < end:
                k = min(t // SHARD_COLS, 3)
                shard_end = (k + 1) * SHARD_COLS
                if k < 3 and shard_end % ROW_TILE and t == shard_end - shard_end % ROW_TILE:
                    lo = t - _win_start(k)
                    yield win_ref[k, lo:lo + ROW_TILE, :] + win_ref[k + 1, 0:ROW_TILE, :]
                    t += ROW_TILE
                    continue
                upto = min(end, shard_end - shard_end % ROW_TILE if k < 3 else end)
                yield win_ref[k, t - _win_start(k):upto - _win_start(k), :]
                t = upto

        for (_, ranges, total), o_ref in zip(segs, outs):
            off = 0
            for start, rows in ranges:
                for part in pieces(start, rows):
                    o_ref[off:off + part.shape[0], :] = part
                    off += part.shape[0]
            if off < total:
                o_ref[off:total, :] = jnp.zeros((total - off, o_ref.shape[1]), BF16)

    outs = _pcall(
        body, name="assemble_w_in", grid=(D_MODEL // LAYOUT_TC,),
        in_specs=[pl.BlockSpec((4, WIN_ROWS, LAYOUT_TC), lambda i: (0, 0, i))],
        out_specs=[pl.BlockSpec((total, LAYOUT_TC), lambda i: (0, i)) for _, _, total in segs],
        out_shape=[jax.ShapeDtypeStruct((total, D_MODEL), BF16) for _, _, total in segs],
        compiler_params=_params("parallel"),
    )(win)
    return {name: o for (name, _, _), o in zip(segs, outs)}


def _pack(dsegs):
    segs = _segments()

    def body(*refs):
        ins, o_ref = refs[:-1], refs[-1]
        tail = IN_COLS - _win_start(3)
        o_ref[3, tail:, :] = jnp.zeros((WIN_ROWS - tail, o_ref.shape[2]), BF16)
        for (_, ranges, _), s_ref in zip(segs, ins):
            off = 0
            for start, rows in ranges:
                for k in range(4):
                    lo = _win_start(k)
                    a, b = max(start, lo), min(start + rows, lo + WIN_ROWS)
                    if a < b:
                        o_ref[k, a - lo:b - lo, :] = s_ref[off + a - start:off + b - start, :]
                off += rows

    return _pcall(
        body, name="pack_dw_in", grid=(D_MODEL // LAYOUT_TC,),
        in_specs=[pl.BlockSpec((total, LAYOUT_TC), lambda i: (0, i)) for _, _, total in segs],
        out_specs=pl.BlockSpec((4, WIN_ROWS, LAYOUT_TC), lambda i: (0, 0, i)),
        out_shape=jax.ShapeDtypeStruct((4, WIN_ROWS, D_MODEL), BF16),
        compiler_params=_params("parallel"),
    )(*[dsegs[name] for name, _, _ in segs])


def _shard_to_window(shard_t, k):
    def at(off):
        return lambda w: jnp.pad(w.astype(BF16), ((off, WIN_ROWS - SHARD_COLS - off), (0, 0)))

    return lax.cond(k % 2 == 1, at(_win_offset(1)), at(_win_offset(0)), shard_t)


def _window_to_shard(win, k):
    return lax.dynamic_slice(win, ((k % 2) * _win_offset(1), 0), (SHARD_COLS, D_MODEL))


def _bucket_maps():
    qi = np.arange(BAND)[:, None]
    kj = np.arange(2 * BAND)[None, :]
    delta = qi + BAND - kj
    maps = []
    for window, dil in PATTERNS:
        valid = (delta >= 0) & (delta <= window // dil)
        dist = np.maximum(delta, 0) * dil
        max_exact = NUM_BUCKETS // 2
        d_f = np.maximum(dist, 1).astype(np.float32)
        large = max_exact + (np.log(d_f / np.float32(max_exact)) / np.float32(math.log(MAX_DISTANCE / max_exact))
                             * np.float32(NUM_BUCKETS - max_exact)).astype(np.int32)
        large = np.minimum(large, NUM_BUCKETS - 1)
        bucket = np.where(dist < max_exact, dist, large)
        maps.append(np.where(valid, bucket, -1).astype(np.int32))
    return np.stack(maps)


def _bias_tables(rel_bias, bmaps):
    def body(rb_ref, bm_ref, o_ref):
        h = pl.program_id(0)
        bm = bm_ref[...]
        acc = jnp.full(bm.shape, NEG, F32)
        for b in range(NUM_BUCKETS):
            acc = jnp.where(bm == b, rb_ref[b, h], acc)
        o_ref[...] = acc

    return _pcall(
        body, name="bias_tables", grid=(3 * GROUP_HEADS,),
        in_specs=[pl.BlockSpec(memory_space=pltpu.SMEM),
                  pl.BlockSpec((None, BAND, 2 * BAND), lambda h: (h // GROUP_HEADS, 0, 0))],
        out_specs=pl.BlockSpec((None, BAND, 2 * BAND), lambda h: (h, 0, 0)),
        out_shape=jax.ShapeDtypeStruct((3 * GROUP_HEADS, BAND, 2 * BAND), F32),
        compiler_params=_params("parallel"),
    )(rel_bias, bmaps)


def _bias_grad(dbias, bmaps):
    def body(db_ref, bm_ref, o_ref):
        bm = bm_ref[...]
        db = db_ref[...]
        lane = lax.broadcasted_iota(jnp.int32, (1, LANE), 1)
        vec = jnp.zeros((1, LANE), F32)
        for b in range(NUM_BUCKETS):
            s = jnp.sum(jnp.where(bm == b, db, 0.0), keepdims=True)
            vec = jnp.where(lane == b, s, vec)
        o_ref[...] = vec

    return _pcall(
        body, name="bias_grad", grid=(3 * GROUP_HEADS,),
        in_specs=[pl.BlockSpec((None, BAND, 2 * BAND), lambda h: (h, 0, 0)),
                  pl.BlockSpec((None, BAND, 2 * BAND), lambda h: (h // GROUP_HEADS, 0, 0))],
        out_specs=pl.BlockSpec((None, 1, LANE), lambda h: (h, 0, 0)),
        out_shape=jax.ShapeDtypeStruct((3 * GROUP_HEADS, 1, LANE), F32),
        compiler_params=_params("parallel"),
    )(dbias, bmaps)


def _rows(n):
    if isinstance(n, int):
        return pl.ds(n * BAND, BAND)
    return pl.ds(pl.multiple_of(n * BAND, BAND), BAND)


def _for_blocks(blocks, nblk, per, carry):
    carry = blocks([0], carry, False)
    start = 1 + (nblk - 1) % per
    for n in range(1, start):
        carry = blocks([n], carry, True)
    trips = (nblk - start) // per
    if trips > 0:
        carry = lax.fori_loop(
            0, trips, lambda t, c: blocks([start + t * per + u for u in range(per)], c, True), carry)
    return carry


def _pairs_per_step(d):
    return {1: 3, 4: 6, 16: 6}[d]


def _attn_fwd(qkv4, bias, d, name):
    nb, _, sub, _ = qkv4.shape
    nblk = sub // BAND
    scale = HEAD_DIM ** -0.5
    npair = ATT_OUT // LANE
    hps = _pairs_per_step(d)
    compact = d > 1

    def body(qkv_ref, bias_ref, o_ref, l_ref):
        def blocks(ns, carry, with_prev):
            chains = [(bi, i, h) for bi in range(len(ns)) for i in range(hps) for h in range(2)]
            first_head = lax.broadcasted_iota(jnp.int32, (BAND, LANE), 1) < HEAD_DIM
            pair = lambda n, i, part: qkv_ref[_rows(n), (3 * i + part) * LANE:(3 * i + part + 1) * LANE]
            scores = []
            for bi, i, h in chains:
                n = ns[bi]
                qp = pair(n, i, 0) * scale
                q = jnp.where(first_head if h == 0 else jnp.logical_not(first_head), qp, jnp.zeros_like(qp))
                s_c = lax.dot_general(q, pair(n, i, 1), NT, preferred_element_type=F32) + bias_ref[2 * i + h, :, BAND:]
                s_p = None
                if with_prev:
                    s_p = lax.dot_general(q, pair(n - 1, i, 1), NT,
                                          preferred_element_type=F32) + bias_ref[2 * i + h, :, :BAND]
                scores.append((s_c, s_p))
            probs = []
            for s_c, s_p in scores:
                m = jnp.max(s_c, -1, keepdims=True)
                if with_prev:
                    m = jnp.maximum(m, jnp.max(s_p, -1, keepdims=True))
                e_c = jnp.exp(s_c - m)
                den = jnp.sum(e_c, -1, keepdims=True)
                e_p = None
                if with_prev:
                    e_p = jnp.exp(s_p - m)
                    den = den + jnp.sum(e_p, -1, keepdims=True)
                    e_p = e_p.astype(BF16)
                probs.append((e_c.astype(BF16), e_p, den, m))
            outs = {}
            for (bi, i, h), (e_c, e_p, den, m) in zip(chains, probs):
                n = ns[bi]
                acc = jnp.dot(e_c, pair(n, i, 2), preferred_element_type=F32)
                if with_prev:
                    acc = acc + jnp.dot(e_p, pair(n - 1, i, 2), preferred_element_type=F32)
                outs[(bi, i, h)] = (acc / den, m + jnp.log(den))
            lane = lax.broadcasted_iota(jnp.int32, (BAND, LANE), 1)
            for bi, n in enumerate(ns):
                per_head = jnp.zeros((BAND, LANE), F32)
                for i in range(hps):
                    o_ref[_rows(n), i * LANE:(i + 1) * LANE] = jnp.where(first_head, outs[(bi, i, 0)][0],
                                                                         outs[(bi, i, 1)][0])
                    if compact:
                        for h in range(2):
                            per_head = jnp.where(lane == 2 * i + h, outs[(bi, i, h)][1], per_head)
                    else:
                        l_ref[_rows(n), i * LANE:(i + 1) * LANE] = jnp.where(first_head, outs[(bi, i, 0)][1],
                                                                             outs[(bi, i, 1)][1])
                if compact:
                    l_ref[_rows(n), :] = per_head
            return carry

        _for_blocks(blocks, nblk, 2 if hps == 1 else 1, 0)

    in_specs = [pl.BlockSpec((None, None, sub, 3 * LANE * hps), lambda hp, b, r: (b, r, 0, hp)),
                pl.BlockSpec((2 * hps, BAND, 2 * BAND), lambda hp, b, r: (hp, 0, 0))]
    if compact:
        return _pcall(
            body, name=name, grid=(1, nb, d), in_specs=in_specs,
            out_specs=[pl.BlockSpec((None, None, sub, ATT_OUT), lambda hp, b, r: (b, r, 0, 0)),
                       pl.BlockSpec((None, None, sub, LANE), lambda hp, b, r: (b, r, 0, 0))],
            out_shape=[jax.ShapeDtypeStruct((nb, d, sub, ATT_OUT), F32), jax.ShapeDtypeStruct((nb, d, sub, LANE), F32)],
            compiler_params=_params("parallel", "parallel", "parallel"),
        )(qkv4, bias)
    ospec = pl.BlockSpec((None, sub, hps * LANE), lambda hp, b, r: (b, 0, r * (npair // hps) + hp))
    return _pcall(
        body, name=name, grid=(npair // hps, nb, d), in_specs=in_specs, out_specs=[ospec, ospec],
        out_shape=[jax.ShapeDtypeStruct((nb, sub, d * ATT_OUT), F32)] * 2,
        compiler_params=_params("parallel", "parallel", "parallel"),
    )(qkv4, bias)


STAT_LSE_LANE = 16


def _attn_bwd(qkv4, bias, cotangent, d, name):
    nb, _, sub, _ = qkv4.shape
    nblk = sub // BAND
    scale = HEAD_DIM ** -0.5
    npair = ATT_OUT // LANE
    hps = _pairs_per_step(d)
    compact = d > 1

    def body(qkv_ref, bias_ref, *rest):
        do_ref, dqkv_ref, db_ref = rest[0], rest[-2], rest[-1]
        b, r = pl.program_id(1), pl.program_id(2)

        @pl.when((b == 0) & (r == 0))
        def _():
            db_ref[...] = jnp.zeros_like(db_ref)

        def blocks(ns, carry, with_prev):
            sides = (0, 1) if with_prev else (0,)
            chains = [(bi, i, h, sd) for bi in range(len(ns)) for i in range(hps) for h in range(2) for sd in sides]
            first_head = lax.broadcasted_iota(jnp.int32, (BAND, LANE), 1) < HEAD_DIM
            own = lambda h, t: jnp.where(first_head if h == 0 else jnp.logical_not(first_head), t, jnp.zeros_like(t))
            pair = lambda rows, i, part: qkv_ref[rows, (3 * i + part) * LANE:(3 * i + part + 1) * LANE]
            key_rows = lambda bi, sd: _rows(ns[bi] - sd)
            qs = {}
            for bi in range(len(ns)):
                for i in range(hps):
                    q_pair = pair(_rows(ns[bi]), i, 0) * scale
                    do = do_ref[_rows(ns[bi]), i * LANE:(i + 1) * LANE]
                    do16 = do.astype(BF16)
                    for h in range(2):
                        if compact:
                            st_ref, head = rest[1], 2 * i + h
                            ebar = st_ref[_rows(ns[bi]), head:head + 1]
                            lcol = st_ref[_rows(ns[bi]), STAT_LSE_LANE + head:STAT_LSE_LANE + head + 1]
                        else:
                            ebar = jnp.sum(own(h, do * rest[1][_rows(ns[bi]), i * LANE:(i + 1) * LANE]), -1, keepdims=True)
                            lcol = rest[2][_rows(ns[bi]), i * LANE + h * HEAD_DIM:i * LANE + h * HEAD_DIM + 1]
                        qs[(bi, i, h)] = (own(h, q_pair), q_pair, own(h, do16), do16, ebar, lcol)
            raw = []
            for bi, i, h, sd in chains:
                q, _, do_h, _, _, _ = qs[(bi, i, h)]
                bias_blk = bias_ref[2 * i + h, :, :BAND] if sd else bias_ref[2 * i + h, :, BAND:]
                s = lax.dot_general(q, pair(key_rows(bi, sd), i, 1), NT, preferred_element_type=F32) + bias_blk
                dp = lax.dot_general(do_h, pair(key_rows(bi, sd), i, 2), NT, preferred_element_type=F32)
                raw.append((s, dp))
            soft = []
            for (bi, i, h, sd), (s, dp) in zip(chains, raw):
                ebar, lcol = qs[(bi, i, h)][4:]
                p = jnp.exp(s - lcol)
                ds = p * (dp - ebar)
                if sd:
                    db_ref[2 * i + h, :, :BAND] += ds
                else:
                    db_ref[2 * i + h, :, BAND:] += ds
                soft.append((p.astype(BF16), ds.astype(BF16)))
            grads = {}
            for (bi, i, h, sd), (p16, ds16) in zip(chains, soft):
                _, q_pair, _, do16 = qs[(bi, i, h)][:4]
                grads[(bi, i, h, sd)] = (
                    jnp.dot(ds16, pair(key_rows(bi, sd), i, 1), preferred_element_type=F32),
                    lax.dot_general(ds16, q_pair, TN, preferred_element_type=F32),
                    lax.dot_general(p16, do16, TN, preferred_element_type=F32))
            both = lambda bi, i, sd, which: jnp.where(first_head, grads[(bi, i, 0, sd)][which],
                                                      grads[(bi, i, 1, sd)][which])
            carry = list(carry) if carry is not None else None
            for bi, n in enumerate(ns):
                for i in range(hps):
                    base = 3 * LANE * i
                    dq = both(bi, i, 0, 0)
                    if with_prev:
                        dq = dq + both(bi, i, 1, 0)
                        dqkv_ref[_rows(n - 1), base + LANE:base + 2 * LANE] = (
                            carry[2 * i] + both(bi, i, 1, 1)).astype(BF16)
                        dqkv_ref[_rows(n - 1), base + 2 * LANE:base + 3 * LANE] = (
                            carry[2 * i + 1] + both(bi, i, 1, 2)).astype(BF16)
                    dqkv_ref[_rows(n), base:base + LANE] = (dq * scale).astype(BF16)
                carry = [t for i in range(hps) for t in (both(bi, i, 0, 1), both(bi, i, 0, 2))]
            return tuple(carry)

        carry = _for_blocks(blocks, nblk, 2 if hps == 1 else 1, None)
        for i in range(hps):
            base = 3 * LANE * i
            dqkv_ref[_rows(nblk - 1), base + LANE:base + 2 * LANE] = carry[2 * i].astype(BF16)
            dqkv_ref[_rows(nblk - 1), base + 2 * LANE:base + 3 * LANE] = carry[2 * i + 1].astype(BF16)

    qspec = pl.BlockSpec((None, None, sub, 3 * LANE * hps), lambda hp, b, r: (b, r, 0, hp))
    bspec = pl.BlockSpec((2 * hps, BAND, 2 * BAND), lambda hp, b, r: (hp, 0, 0))
    if compact:
        cspecs = [pl.BlockSpec((None, None, sub, ATT_OUT), lambda hp, b, r: (b, r, 0, 0)),
                  pl.BlockSpec((None, None, sub, LANE), lambda hp, b, r: (b, r, 0, 0))]
    else:
        cspecs = [pl.BlockSpec((None, sub, hps * LANE), lambda hp, b, r: (b, 0, r * (npair // hps) + hp))] * 3
    return _pcall(
        body, name=name, grid=(npair // hps, nb, d),
        in_specs=[qspec, bspec] + cspecs, out_specs=[qspec, bspec],
        out_shape=[jax.ShapeDtypeStruct(qkv4.shape, BF16),
                   jax.ShapeDtypeStruct((GROUP_HEADS, BAND, 2 * BAND), F32)],
        compiler_params=_params("parallel", "arbitrary", "arbitrary"),
    )(qkv4, bias, *cotangent)


def _head_lanes(first_lane, one_channel):
    c = lax.broadcasted_iota(jnp.int32, (ATT_OUT, LANE), 0)
    lane = lax.broadcasted_iota(jnp.int32, (ATT_OUT, LANE), 1)
    hit = lane == first_lane + c // HEAD_DIM
    if one_channel:
        hit = hit & (c % HEAD_DIM == 0)
    return hit.astype(BF16)


def _exact_dot(v, m01, dims=None):
    parts = _split3(v)
    if dims is None:
        dot = lambda t: jnp.dot(t, m01, preferred_element_type=F32)
    else:
        dot = lambda t: lax.dot_general(t, m01, dims, preferred_element_type=F32)
    return (dot(parts[0]) + dot(parts[1])) + dot(parts[2])


def _combine_fwd(o0, l0, o1, l1, o2, l2, gatt):
    nb, seq, _ = gatt.shape
    tm = 512

    def body(o0_ref, l0_ref, o1_ref, l1_ref, o2_ref, l2_ref, g_ref, oa_ref, oatt_ref, lse_ref):
        spread = _head_lanes(0, False)
        l0v = l0_ref[...]
        l1v = _exact_dot(l1_ref[...], spread, NT)
        l2v = _exact_dot(l2_ref[...], spread, NT)
        m = jnp.maximum(jnp.maximum(l0v, l1v), l2v)
        tot = m + jnp.log(jnp.exp(l0v - m) + jnp.exp(l1v - m) + jnp.exp(l2v - m))
        o = (jnp.exp(l0v - tot) * o0_ref[...] + jnp.exp(l1v - tot) * o1_ref[...]
             + jnp.exp(l2v - tot) * o2_ref[...])
        g = g_ref[...]
        oa_ref[...] = (o * (g * _sigmoid(g))).astype(BF16)
        oatt_ref[...] = o
        lse_ref[...] = tot

    spec = pl.BlockSpec((None, tm, ATT_OUT), lambda b, i: (b, i, 0))
    lspec = pl.BlockSpec((None, tm, LANE), lambda b, i: (b, i, 0))
    return _pcall(
        body, name="attn_combine", grid=(nb, seq // tm),
        in_specs=[spec, spec, spec, lspec, spec, lspec, spec], out_specs=[spec] * 3,
        out_shape=[jax.ShapeDtypeStruct((nb, seq, ATT_OUT), BF16), jax.ShapeDtypeStruct((nb, seq, ATT_OUT), F32),
                   jax.ShapeDtypeStruct((nb, seq, ATT_OUT), F32)],
        compiler_params=_params("parallel", "parallel"),
    )(o0, l0, o1, l1, o2, l2, gatt)


def _combine_bwd(doa, gatt, o_att, lse):
    nb, seq, _ = gatt.shape
    tm = 512

    def body(doa_ref, g_ref, o_ref, l_ref, do_ref, do16_ref, st_ref, dg_ref):
        g = g_ref[...]
        sg = _sigmoid(g)
        do = doa_ref[...] * (g * sg)
        do_ref[...] = do
        do16_ref[...] = do.astype(BF16)
        st_ref[...] = (_exact_dot(do * o_ref[...], _head_lanes(0, False))
                       + _exact_dot(l_ref[...], _head_lanes(STAT_LSE_LANE, True)))
        dg_ref[...] = (doa_ref[...] * o_ref[...] * (sg * (1.0 + g * (1.0 - sg)))).astype(BF16)

    spec = pl.BlockSpec((None, tm, ATT_OUT), lambda b, i: (b, i, 0))
    lspec = pl.BlockSpec((None, tm, LANE), lambda b, i: (b, i, 0))
    return _pcall(
        body, name="attn_combine_bwd", grid=(nb, seq // tm), in_specs=[spec] * 4,
        out_specs=[spec, spec, lspec, spec],
        out_shape=[jax.ShapeDtypeStruct((nb, seq, ATT_OUT), F32), jax.ShapeDtypeStruct((nb, seq, ATT_OUT), BF16),
                   jax.ShapeDtypeStruct((nb, seq, LANE), F32), jax.ShapeDtypeStruct((nb, seq, ATT_OUT), BF16)],
        compiler_params=_params("parallel", "parallel"),
    )(doa, gatt, o_att, lse)


CONV_TM = 512
CONV_TC = 1024


def _shift_down(cur, halo, k):
    rolled = pltpu.roll(cur, k, 0)
    hro = pltpu.roll(halo, k, 0)
    row = lax.broadcasted_iota(jnp.int32, hro.shape, 0)
    return jnp.concatenate([jnp.where(row < k, hro, rolled[:8]), rolled[8:]], axis=0)


def _shift_up(cur, halo, k):
    n = cur.shape[0]
    rolled = pltpu.roll(cur, n - k, 0)
    hro = pltpu.roll(halo, 8 - k, 0)
    row = lax.broadcasted_iota(jnp.int32, hro.shape, 0)
    return jnp.concatenate([rolled[:n - 8], jnp.where(row >= 8 - k, hro, rolled[n - 8:])], axis=0)


def _conv_pre(cur, halo, w_ref, b_ref):
    acc = cur * w_ref[3:4, :] + b_ref[...]
    for k in range(1, 4):
        acc = acc + _shift_down(cur, halo, k) * w_ref[3 - k:4 - k, :]
    return acc


def _conv_specs(seq):
    nblk = seq // CONV_TM
    cur = pl.BlockSpec((None, CONV_TM, CONV_TC), lambda cb, b, i: (b, i, cb))
    prev = pl.BlockSpec((None, 8, CONV_TC), lambda cb, b, i: (b, jnp.maximum(i * (CONV_TM // 8) - 1, 0), cb))
    nxt = pl.BlockSpec((None, 8, CONV_TC),
                       lambda cb, b, i: (b, jnp.minimum((i + 1) * (CONV_TM // 8), seq // 8 - 1), cb))
    wspec = pl.BlockSpec((4, CONV_TC), lambda cb, b, i: (0, cb))
    bspec = pl.BlockSpec((1, CONV_TC), lambda cb, b, i: (0, cb))
    return nblk, cur, prev, nxt, wspec, bspec


def _conv_fwd(xin, w4, bias, name):
    nb, seq, ch = xin.shape
    _, cur, prev, _, wspec, bspec = _conv_specs(seq)

    def body(x_ref, h_ref, w_ref, b_ref, o_ref):
        halo = jnp.where(pl.program_id(2) > 0, h_ref[...], 0.0)
        pre = _conv_pre(x_ref[...], halo, w_ref, b_ref)
        o_ref[...] = pre * _sigmoid(pre)

    return _pcall(
        body, name=name, grid=(ch // CONV_TC, nb, seq // CONV_TM),
        in_specs=[cur, prev, wspec, bspec], out_specs=cur,
        out_shape=jax.ShapeDtypeStruct(xin.shape, F32),
        compiler_params=_params("parallel", "parallel", "parallel"),
    )(xin, xin, w4, bias)


def _conv_bwd_pre(dact, xin, w4, bias, name):
    nb, seq, ch = xin.shape
    _, cur, prev, _, wspec, bspec = _conv_specs(seq)

    def body(da_ref, x_ref, h_ref, w_ref, b_ref, dp_ref, s_ref):
        b, i = pl.program_id(1), pl.program_id(2)

        @pl.when((b == 0) & (i == 0))
        def _():
            s_ref[...] = jnp.zeros_like(s_ref)

        halo = jnp.where(i > 0, h_ref[...], 0.0)
        x = x_ref[...]
        pre = _conv_pre(x, halo, w_ref, b_ref)
        sg = _sigmoid(pre)
        dpre = da_ref[...] * (sg * (1.0 + pre * (1.0 - sg)))
        dp_ref[...] = dpre
        s_ref[3:4, :] += jnp.sum(dpre * x, 0, keepdims=True)
        for k in range(1, 4):
            s_ref[3 - k:4 - k, :] += jnp.sum(dpre * _shift_down(x, halo, k), 0, keepdims=True)
        s_ref[4:5, :] += jnp.sum(dpre, 0, keepdims=True)

    return _pcall(
        body, name=name, grid=(ch // CONV_TC, nb, seq // CONV_TM),
        in_specs=[cur, cur, prev, wspec, bspec],
        out_specs=[cur, pl.BlockSpec((8, CONV_TC), lambda cb, b, i: (0, cb))],
        out_shape=[jax.ShapeDtypeStruct(xin.shape, F32), jax.ShapeDtypeStruct((8, ch), F32)],
        compiler_params=_params("parallel", "arbitrary", "arbitrary"),
    )(dact, xin, xin, w4, bias)


def _conv_bwd_x(dpre, w4, name):
    nb, seq, ch = dpre.shape
    nblk, cur, _, nxt, wspec, _ = _conv_specs(seq)

    def body(d_ref, n_ref, w_ref, o_ref):
        halo = jnp.where(pl.program_id(2) < nblk - 1, n_ref[...], 0.0)
        cur_v = d_ref[...]
        acc = cur_v * w_ref[3:4, :]
        for j in range(1, 4):
            acc = acc + _shift_up(cur_v, halo, j) * w_ref[3 - j:4 - j, :]
        o_ref[...] = acc.astype(BF16)

    return _pcall(
        body, name=name, grid=(ch // CONV_TC, nb, seq // CONV_TM),
        in_specs=[cur, nxt, wspec], out_specs=cur,
        out_shape=jax.ShapeDtypeStruct(dpre.shape, BF16),
        compiler_params=_params("parallel", "parallel", "parallel"),
    )(dpre, dpre, w4)


def _softplus_sig(dt_raw, dt_bias_row):
    nb, seq, _ = dt_raw.shape
    tm = 512

    def body(r_ref, b_ref, sp_ref, sg_ref):
        v = r_ref[...] + b_ref[...]
        sp_ref[...] = jnp.maximum(v, 0.0) + jnp.log1p(jnp.exp(-jnp.abs(v)))
        sg_ref[...] = _sigmoid(v)

    spec = pl.BlockSpec((None, tm, LANE), lambda b, i: (b, i, 0))
    return _pcall(
        body, name="dt_softplus", grid=(nb, seq // tm),
        in_specs=[spec, pl.BlockSpec((1, LANE), lambda b, i: (0, 0))], out_specs=[spec, spec],
        out_shape=[jax.ShapeDtypeStruct(dt_raw.shape, F32)] * 2,
        compiler_params=_params("parallel", "parallel"),
    )(dt_raw, dt_bias_row)


def _group_lanes(t):
    pads = [(0, 0)] * (t.ndim - 1) + [(0, LANE - GROUP_SSM_HEADS)]
    return jnp.stack([jnp.pad(t[..., GROUP_SSM_HEADS * g:GROUP_SSM_HEADS * (g + 1)], pads) for g in range(SSM_GROUPS)])


def _ungroup_lanes(t):
    return jnp.concatenate([t[g][..., :GROUP_SSM_HEADS] for g in range(SSM_GROUPS)], axis=-1)


def _decays(dt, al_ref):
    row = lax.broadcasted_iota(jnp.int32, (CHUNK, CHUNK), 0)
    col = lax.broadcasted_iota(jnp.int32, (CHUNK, CHUNK), 1)
    tril = (row >= col).astype(BF16)
    triu = (row <= col).astype(BF16)
    arow = -jnp.exp(al_ref[...])
    hi, mid, lo = _split3(dt * arow)
    down = lambda t: jnp.dot(tril, t, preferred_element_type=F32)
    across = lambda t: lax.dot_general(t, triu, TN, preferred_element_type=F32)
    acs = (down(hi) + down(mid)) + down(lo)
    acs_t = (across(hi) + across(mid)) + across(lo)
    return arow, acs, acs_t, row >= col, triu


STEP_CHUNKS = 4


def _ssd_specs(nb, seq):
    nc = seq // CHUNK
    hw = GROUP_SSM_HEADS * HEAD_DIM
    rows, steps = STEP_CHUNKS * CHUNK, nc // STEP_CHUNKS

    def mk(rev):
        cidx = (lambda c: steps - 1 - c) if rev else (lambda c: c)
        wide = pl.BlockSpec((None, rows, hw), lambda g, b, c: (b, cidx(c), g))
        xbc = pl.BlockSpec((None, rows, XBC_GROUP), lambda g, b, c: (b, cidx(c), g))
        lanes = pl.BlockSpec((None, None, rows, LANE), lambda g, b, c: (g, b, cidx(c), 0))
        prev = pl.BlockSpec((None, STEP_CHUNKS, None, D_STATE, hw), lambda g, b, c: (b, cidx(c), g, 0, 0))
        return wide, xbc, lanes, prev

    grow = pl.BlockSpec((None, 1, LANE), lambda g, b, c: (g, 0, 0))
    nwspec = pl.BlockSpec((1, hw), lambda g, b, c: (0, g))
    return nc, steps, hw, mk, grow, nwspec


def _head_expand():
    hw = GROUP_SSM_HEADS * HEAD_DIM
    r = lax.broadcasted_iota(jnp.int32, (LANE, hw), 0)
    c = lax.broadcasted_iota(jnp.int32, (LANE, hw), 1)
    return ((c // HEAD_DIM) == r).astype(BF16)


def _split3(v):
    hi = v.astype(BF16)
    rest = v - hi.astype(F32)
    mid = rest.astype(BF16)
    return hi, mid, (rest - mid.astype(F32)).astype(BF16)


def _to_channels(v, e):
    hi, mid, lo = _split3(v)
    dot = lambda t: jnp.dot(t, e, preferred_element_type=F32)
    return (dot(hi) + dot(mid)) + dot(lo)


def _to_heads(w, e):
    hi, mid, lo = _split3(w)
    dot = lambda t: lax.dot_general(t, e, (((1,), (1,)), ((), ())), preferred_element_type=F32)
    return (dot(hi) + dot(mid)) + dot(lo)


def _row8(v):
    return jnp.broadcast_to(v, (8, v.shape[1]))


def _ssd_chunk_setup(dt, al_ref, ds_ref):
    arow, acs, acs_t, causal, triu = _decays(dt, al_ref)
    e = _head_expand()
    dtx = _to_channels(dt, e)
    acsx = _to_channels(acs, e)
    lastx = acsx[CHUNK - 1:CHUNK, :]
    dskx = _to_channels(_row8(ds_ref[...]), e)[0:1, :]
    return arow, acs, acs_t, causal, triu, e, dtx, acsx, lastx, dskx


def _ssd_fwd(xbc, dtg, z, alog_g, dskip_g, normw):
    nb, seq, _ = xbc.shape
    nc, steps, hw, mk, grow, nwspec = _ssd_specs(nb, seq)
    wide, xbc_spec, lanes, prev = mk(False)
    tn = (((0,), (0,)), ((), ()))

    def body(xbc_ref, dt_ref, z_ref, al_ref, ds_ref, nw_ref, ys_ref, y_ref, sp_ref, st_ref):
        @pl.when(pl.program_id(2) == 0)
        def _():
            st_ref[...] = jnp.zeros_like(st_ref)

        for ci in range(STEP_CHUNKS):
            chunk(ci, xbc_ref, dt_ref, z_ref, al_ref, ds_ref, nw_ref, ys_ref, y_ref, sp_ref, st_ref)

    def chunk(ci, xbc_ref, dt_ref, z_ref, al_ref, ds_ref, nw_ref, ys_ref, y_ref, sp_ref, st_ref):
        rows = slice(ci * CHUNK, (ci + 1) * CHUNK)
        dt = dt_ref[rows, :]
        _, acs, acs_t, causal, _, _, dtx, acsx, lastx, dskx = _ssd_chunk_setup(dt, al_ref, ds_ref)
        bmat = xbc_ref[rows, GROUP_CH:GROUP_CH + D_STATE].astype(BF16)
        cmat = xbc_ref[rows, GROUP_CH + D_STATE:].astype(BF16)
        cb = lax.dot_general(cmat, bmat, (((1,), (1,)), ((), ())), preferred_element_type=F32)
        x = xbc_ref[rows, :GROUP_CH]
        xdt = x * dtx
        xdt16 = xdt.astype(BF16)
        first_head = lax.broadcasted_iota(jnp.int32, (CHUNK, LANE), 1) < HEAD_DIM
        pairs = []
        for hp in range(GROUP_SSM_HEADS // 2):
            xp = xdt16[:, hp * LANE:(hp + 1) * LANE]
            two = []
            for j in (2 * hp, 2 * hp + 1):
                lmat = jnp.exp(jnp.where(causal, acs[:, j:j + 1] - acs_t[j:j + 1, :], -jnp.inf))
                two.append(jnp.dot((cb * lmat).astype(BF16), xp, preferred_element_type=F32))
            pairs.append(jnp.where(first_head, two[0], two[1]))
        yd = jnp.concatenate(pairs, axis=1)
        s_prev = st_ref[...]
        s16 = s_prev.astype(BF16)
        sp_ref[ci] = s16
        yo = jnp.dot(cmat, s16, preferred_element_type=F32) * jnp.exp(acsx)
        sts = lax.dot_general(bmat, (xdt * jnp.exp(lastx - acsx)).astype(BF16), tn, preferred_element_type=F32)
        st_ref[...] = s_prev * jnp.exp(lastx) + sts
        y = yd + yo + dskx * x
        zz = z_ref[rows, :]
        u = y * (zz * _sigmoid(zz))
        rn = lax.rsqrt(jnp.mean(u * u, -1, keepdims=True) + RMS_EPS)
        ys_ref[rows, :] = (u * rn * nw_ref[...]).astype(BF16)
        y_ref[rows, :] = y

    return _pcall(
        body, name="ssd_fwd", grid=(SSM_GROUPS, nb, steps),
        in_specs=[xbc_spec, lanes, wide, grow, grow, nwspec],
        out_specs=[wide, wide, prev],
        out_shape=[jax.ShapeDtypeStruct((nb, seq, D_INNER), BF16), jax.ShapeDtypeStruct((nb, seq, D_INNER), F32),
                   jax.ShapeDtypeStruct((nb, nc, SSM_GROUPS, D_STATE, hw), BF16)],
        scratch_shapes=[pltpu.VMEM((D_STATE, hw), F32)],
        compiler_params=_params("parallel", "parallel", "arbitrary"),
    )(xbc, dtg, z, alog_g, dskip_g, normw)


def _ssd_bwd(xbc, dtg, sgg, z, y, dys, sprev, alog_g, dskip_g, normw):
    nb, seq, _ = xbc.shape
    nc, steps, hw, mk, grow, nwspec = _ssd_specs(nb, seq)
    wide, xbc_spec, lanes, prev = mk(True)
    nt = (((1,), (1,)), ((), ()))
    tn = (((0,), (0,)), ((), ()))

    def body(xbc_ref, dt_ref, sg_ref, z_ref, y_ref, dys_ref, sp_ref, al_ref, ds_ref, nw_ref,
             dxbc_ref, ddt_ref, dz_ref, small_ref, dnw_ref, g_ref):
        b, c = pl.program_id(1), pl.program_id(2)

        @pl.when((b == 0) & (c == 0))
        def _():
            small_ref[...] = jnp.zeros_like(small_ref)
            dnw_ref[...] = jnp.zeros_like(dnw_ref)

        @pl.when(c == 0)
        def _():
            g_ref[...] = jnp.zeros_like(g_ref)

        for ci in reversed(range(STEP_CHUNKS)):
            chunk(ci, xbc_ref, dt_ref, sg_ref, z_ref, y_ref, dys_ref, sp_ref, al_ref, ds_ref, nw_ref,
                  dxbc_ref, ddt_ref, dz_ref, small_ref, dnw_ref, g_ref)

    def chunk(ci, xbc_ref, dt_ref, sg_ref, z_ref, y_ref, dys_ref, sp_ref, al_ref, ds_ref, nw_ref,
              dxbc_ref, ddt_ref, dz_ref, small_ref, dnw_ref, g_ref):
        rows = slice(ci * CHUNK, (ci + 1) * CHUNK)
        yv, zz, dys_v, nw = y_ref[rows, :], z_ref[rows, :], dys_ref[rows, :], nw_ref[...]
        sz = _sigmoid(zz)
        silu = zz * sz
        u = yv * silu
        rn = lax.rsqrt(jnp.mean(u * u, -1, keepdims=True) + RMS_EPS)
        gn = dys_v * nw
        du = rn * gn - u * (rn * rn * rn) * jnp.mean(u * gn, -1, keepdims=True)
        dnw_ref[...] += jnp.sum(dys_v * u * rn, 0, keepdims=True)
        dy = du * silu
        dz_ref[rows, :] = du * yv * (sz * (1.0 + zz * (1.0 - sz)))

        dt = dt_ref[rows, :]
        arow, acs, acs_t, causal, triu, e, dtx, acsx, lastx, dskx = _ssd_chunk_setup(dt, al_ref, ds_ref)
        dfsx = jnp.exp(acsx)
        dtex = jnp.exp(lastx - acsx)
        bmat = xbc_ref[rows, GROUP_CH:GROUP_CH + D_STATE].astype(BF16)
        cmat = xbc_ref[rows, GROUP_CH + D_STATE:].astype(BF16)
        cb = lax.dot_general(cmat, bmat, nt, preferred_element_type=F32)
        x = xbc_ref[rows, :GROUP_CH]
        xdt = x * dtx
        xdt16 = xdt.astype(BF16)
        xdte = xdt * dtex
        dy16 = dy.astype(BF16)
        dyd = dy * dfsx
        dyd16 = dyd.astype(BF16)
        s16 = sp_ref[ci]
        g = g_ref[...]
        g16 = g.astype(BF16)
        cs = jnp.dot(cmat, s16, preferred_element_type=F32)
        dc_off = lax.dot_general(dyd16, s16, nt, preferred_element_type=F32)
        g_here = lax.dot_general(cmat, dyd16, tn, preferred_element_type=F32)
        bg = jnp.dot(bmat, g16, preferred_element_type=F32)
        db_st = lax.dot_general(xdte.astype(BF16), g16, nt, preferred_element_type=F32)
        ddte_w = bg * xdte
        dcd = _to_heads(_row8(jnp.sum(g * s16.astype(F32), 0, keepdims=True)), e)[0:1, :]
        lane = lax.broadcasted_iota(jnp.int32, (CHUNK, LANE), 1)
        first_head = lane < HEAD_DIM
        sub = lax.broadcasted_iota(jnp.int32, (CHUNK, LANE), 0)
        dacs = jnp.zeros((CHUNK, LANE), F32)
        colsums = jnp.zeros((CHUNK, LANE), F32)
        dcb = jnp.zeros((CHUNK, CHUNK), F32)
        pairs = []
        for hp in range(GROUP_SSM_HEADS // 2):
            xp = xdt16[:, hp * LANE:(hp + 1) * LANE]
            dyp = dy16[:, hp * LANE:(hp + 1) * LANE]
            two = []
            for idx, j in enumerate((2 * hp, 2 * hp + 1)):
                lmat = jnp.exp(jnp.where(causal, acs[:, j:j + 1] - acs_t[j:j + 1, :], -jnp.inf))
                mf = cb * lmat
                dy_h = jnp.where(first_head if idx == 0 else jnp.logical_not(first_head), dyp, jnp.zeros_like(dyp))
                dm = lax.dot_general(dy_h, xp, nt, preferred_element_type=F32)
                two.append(lax.dot_general(mf.astype(BF16), dyp, tn, preferred_element_type=F32))
                wmat = dm * mf
                dcb = dcb + dm * lmat
                dacs = jnp.where(lane == j, jnp.sum(wmat, -1, keepdims=True), dacs)
                colsums = jnp.where(sub == j, jnp.sum(wmat, 0, keepdims=True), colsums)
            pairs.append(jnp.where(first_head, two[0], two[1]))
        dxdt = bg * dtex + jnp.concatenate(pairs, axis=1)
        dacs = dacs - colsums.T + _to_heads(dyd * cs - ddte_w, e)
        cd_row = jnp.exp(acs[CHUNK - 1:CHUNK, :])
        tail = _to_heads(_row8(jnp.sum(ddte_w, 0, keepdims=True)), e)[0:1, :] + dcd * cd_row
        dacs = dacs + jnp.where(sub == CHUNK - 1, tail, 0.0)
        d_hi, d_mid, d_lo = _split3(dacs)
        up = lambda t: jnp.dot(triu, t, preferred_element_type=F32)
        da = (up(d_hi) + up(d_mid)) + up(d_lo)
        ddt_raw = (da * arow + _to_heads(dxdt * x, e)) * sg_ref[rows, :]
        ddt_ref[rows, :] = ddt_raw
        small_ref[0:1, :] += jnp.sum(da * dt, 0, keepdims=True) * arow
        small_ref[1:2, :] += _to_heads(_row8(jnp.sum(dy * x, 0, keepdims=True)), e)[0:1, :]
        small_ref[2:3, :] += jnp.sum(ddt_raw, 0, keepdims=True)
        dcb16 = dcb.astype(BF16)
        dxbc_ref[rows, GROUP_CH + D_STATE:] = dc_off + jnp.dot(dcb16, bmat, preferred_element_type=F32)
        dxbc_ref[rows, GROUP_CH:GROUP_CH + D_STATE] = db_st + lax.dot_general(dcb16, cmat, tn,
                                                                               preferred_element_type=F32)
        dxbc_ref[rows, :GROUP_CH] = dxdt * dtx + dskx * dy
        g_ref[...] = g * jnp.exp(lastx) + g_here

    return _pcall(
        body, name="ssd_bwd", grid=(SSM_GROUPS, nb, steps),
        in_specs=[xbc_spec, lanes, lanes, wide, wide, wide, prev, grow, grow, nwspec],
        out_specs=[xbc_spec, lanes, wide,
                   pl.BlockSpec((None, 8, LANE), lambda g, b, c: (g, 0, 0)), nwspec],
        out_shape=[jax.ShapeDtypeStruct((nb, seq, CONV_DIM), F32),
                   jax.ShapeDtypeStruct((SSM_GROUPS, nb, seq, LANE), F32),
                   jax.ShapeDtypeStruct((nb, seq, D_INNER), F32),
                   jax.ShapeDtypeStruct((SSM_GROUPS, 8, LANE), F32),
                   jax.ShapeDtypeStruct((1, D_INNER), F32)],
        scratch_shapes=[pltpu.VMEM((D_STATE, hw), F32)],
        compiler_params=_params("parallel", "arbitrary", "arbitrary"),
    )(xbc, dtg, sgg, z, y, dys, sprev, alog_g, dskip_g, normw)


EW_TM = 256


def _merge_fwd(y_a, y_b, gm, bgate):
    nb, seq, _ = y_a.shape

    def body(a_ref, b_ref, ga_ref, gb_ref, bg_ref, o_ref):
        sa = _sigmoid(ga_ref[...] + bg_ref[0:1, :])
        sb = _sigmoid(gb_ref[...] + bg_ref[1:2, :])
        o_ref[...] = (sa * a_ref[...] + sb * b_ref[...]).astype(BF16)

    spec = pl.BlockSpec((None, EW_TM, D_MODEL), lambda b, i: (b, i, 0))
    spec1 = pl.BlockSpec((None, EW_TM, D_MODEL), lambda b, i: (b, i, 1))
    return _pcall(
        body, name="merge_fwd", grid=(nb, seq // EW_TM),
        in_specs=[spec, spec, spec, spec1, pl.BlockSpec((8, D_MODEL), lambda b, i: (0, 0))], out_specs=spec,
        out_shape=jax.ShapeDtypeStruct((nb, seq, D_MODEL), BF16),
        compiler_params=_params("parallel", "parallel"),
    )(y_a, y_b, gm, gm, bgate)


def _merge_bwd(dmerged, y_a, y_b, gm, bgate):
    nb, seq, _ = y_a.shape

    def body(dm_ref, a_ref, b_ref, ga_ref, gb_ref, bg_ref, dya_ref, dyb_ref, dg_ref, s_ref):
        @pl.when((pl.program_id(0) == 0) & (pl.program_id(1) == 0))
        def _():
            s_ref[...] = jnp.zeros_like(s_ref)

        dm = dm_ref[...]
        sa = _sigmoid(ga_ref[...] + bg_ref[0:1, :])
        sb = _sigmoid(gb_ref[...] + bg_ref[1:2, :])
        dya_ref[...] = (dm * sa).astype(BF16)
        dyb_ref[...] = (dm * sb).astype(BF16)
        dga = dm * a_ref[...] * (sa * (1.0 - sa))
        dgb = dm * b_ref[...] * (sb * (1.0 - sb))
        dg_ref[:, :D_MODEL] = dga.astype(BF16)
        dg_ref[:, D_MODEL:] = dgb.astype(BF16)
        s_ref[0:1, :] += jnp.sum(dga, 0, keepdims=True)
        s_ref[1:2, :] += jnp.sum(dgb, 0, keepdims=True)

    spec = pl.BlockSpec((None, EW_TM, D_MODEL), lambda b, i: (b, i, 0))
    spec1 = pl.BlockSpec((None, EW_TM, D_MODEL), lambda b, i: (b, i, 1))
    small = pl.BlockSpec((8, D_MODEL), lambda b, i: (0, 0))
    return _pcall(
        body, name="merge_bwd", grid=(nb, seq // EW_TM),
        in_specs=[spec, spec, spec, spec, spec1, small],
        out_specs=[spec, spec, pl.BlockSpec((None, EW_TM, 2 * D_MODEL), lambda b, i: (b, i, 0)), small],
        out_shape=[jax.ShapeDtypeStruct((nb, seq, D_MODEL), BF16), jax.ShapeDtypeStruct((nb, seq, D_MODEL), BF16),
                   jax.ShapeDtypeStruct((nb, seq, 2 * D_MODEL), BF16), jax.ShapeDtypeStruct((8, D_MODEL), F32)],
        compiler_params=_params("arbitrary", "arbitrary"),
    )(dmerged, y_a, y_b, gm, gm, bgate)


def _ln_loss(x, mix, gp, pw, target, bgate, ln_g, ln_b):
    nb, seq, _ = x.shape

    def body(x_ref, mix_ref, gp_ref, pw_ref, t_ref, bg_ref, g_ref, b_ref, dx_ref, dp_ref, dpw_ref, dgp_ref, s_ref):
        @pl.when((pl.program_id(0) == 0) & (pl.program_id(1) == 0))
        def _():
            s_ref[...] = jnp.zeros_like(s_ref)

        sp = _sigmoid(gp_ref[...] + bg_ref[2:3, :])
        pw = pw_ref[...]
        pre = ALPHA * x_ref[...] + mix_ref[...] + sp * pw
        mu = jnp.mean(pre, -1, keepdims=True)
        cen = pre - mu
        rstd = lax.rsqrt(jnp.mean(cen * cen, -1, keepdims=True) + LN_EPS)
        xhat = cen * rstd
        err = xhat * g_ref[...] + b_ref[...] - t_ref[...]
        dy = err * (1.0 / D_MODEL)
        dxh = dy * g_ref[...]
        dpre = rstd * (dxh - jnp.mean(dxh, -1, keepdims=True) - xhat * jnp.mean(dxh * xhat, -1, keepdims=True))
        dx_ref[...] = ALPHA * dpre
        dp_ref[...] = dpre.astype(BF16)
        dpw_ref[...] = (dpre * sp).astype(BF16)
        dgp = dpre * pw * (sp * (1.0 - sp))
        dgp_ref[...] = dgp.astype(BF16)
        s_ref[0:1, :] += jnp.sum(dy * xhat, 0, keepdims=True)
        s_ref[1:2, :] += jnp.sum(dy, 0, keepdims=True)
        s_ref[2:3, :] += jnp.sum(dgp, 0, keepdims=True)
        s_ref[3:4, :] += jnp.sum(err * err, 0, keepdims=True)

    spec = pl.BlockSpec((None, EW_TM, D_MODEL), lambda b, i: (b, i, 0))
    small = pl.BlockSpec((8, D_MODEL), lambda b, i: (0, 0))
    row = pl.BlockSpec((1, D_MODEL), lambda b, i: (0, 0))
    return _pcall(
        body, name="ln_loss", grid=(nb, seq // EW_TM),
        in_specs=[spec] * 5 + [small, row, row], out_specs=[spec] * 4 + [small],
        out_shape=[jax.ShapeDtypeStruct((nb, seq, D_MODEL), F32)] + [jax.ShapeDtypeStruct((nb, seq, D_MODEL), BF16)] * 3
        + [jax.ShapeDtypeStruct((8, D_MODEL), F32)],
        compiler_params=_params("arbitrary", "arbitrary"),
    )(x, mix, gp, pw, target, bgate, ln_g, ln_b)


def _adamw(w, g, m, v, name):
    rows, cols = w.shape
    tr = _row_tile(rows, cols, 8, 5 << 19)
    c1 = 1.0 - ADAM_B1 ** ADAM_STEP
    c2 = 1.0 - ADAM_B2 ** ADAM_STEP

    def body(w_ref, g_ref, m_ref, v_ref, d_ref, nm_ref, nv_ref):
        gv = g_ref[...]
        nm = ADAM_B1 * m_ref[...] + (1.0 - ADAM_B1) * gv
        nv = ADAM_B2 * v_ref[...] + (1.0 - ADAM_B2) * (gv * gv)
        d_ref[...] = -ADAM_LR * ((nm / c1) / (jnp.sqrt(nv / c2) + ADAM_EPS) + ADAM_WD * w_ref[...])
        nm_ref[...] = nm
        nv_ref[...] = nv

    spec = pl.BlockSpec((tr, cols), lambda i: (i, 0))
    return _pcall(
        body, name=name, grid=(rows // tr,), in_specs=[spec] * 4, out_specs=[spec] * 3,
        out_shape=[jax.ShapeDtypeStruct(w.shape, F32)] * 3, compiler_params=_params("parallel"),
    )(w, g, m, v)


def _sum_rows(parts, out_dtype, name):
    rows, cols = parts[0].shape
    tr = rows
    for cand in range(16, rows, 16):
        if rows % cand == 0 and cand * cols * 4 <= (1 << 20):
            tr = cand
    n = len(parts)

    def body(*refs):
        acc = refs[0][...].astype(F32)
        for r in refs[1:n]:
            acc = acc + r[...].astype(F32)
        refs[n][...] = acc.astype(out_dtype)

    spec = pl.BlockSpec((tr, cols), lambda i: (i, 0))
    return _pcall(
        body, name=name, grid=(rows // tr,), in_specs=[spec] * n, out_specs=spec,
        out_shape=jax.ShapeDtypeStruct((rows, cols), out_dtype), compiler_params=_params("parallel"),
    )(*parts)


def _place():
    return lax.axis_index("x"), lax.axis_index("y"), lax.axis_index("c")


def _other_chips(x, y):
    return [(1 - x, y), (x, 1 - y), (1 - x, 1 - y)]


def _remote(src, dst, send_sem, recv_sem, to):
    return pltpu.make_async_remote_copy(src_ref=src, dst_ref=dst, send_sem=send_sem, recv_sem=recv_sem,
                                        device_id=to, device_id_type=MESH)


ANY = pl.BlockSpec(memory_space=pl.ANY)
DMA_CHUNK_BYTES = 512 * 1024


def _row_chunks(rows, row_bytes):
    per = max(16, DMA_CHUNK_BYTES // row_bytes // 16 * 16)
    return [(s, min(per, rows - s)) for s in range(0, rows, per)]


def _row_tile(rows, cols, align, limit=1 << 21):
    best = None
    for cand in range(align, rows + 1, align):
        if rows % cand == 0 and cand * cols * 4 <= limit:
            best = cand
    return best or rows


def _allgather_pieces(pieces):
    n = len(pieces)
    halves = [_row_chunks(p.shape[0] // 2, p.shape[1] * p.dtype.itemsize) for p in pieces]
    entries = [(a, q, s, m, j) for a in range(n) for q, (s, m) in enumerate(halves[a]) for j in range(3)]
    slot = {(a, q, j): k for k, (a, q, _, _, j) in enumerate(entries)}
    n_ici = len(entries)

    def body(*refs):
        ins, outs = refs[:n], refs[n:2 * n]
        send_sems, recv_sems = refs[2 * n:]
        x, y, c = _place()
        me = 2 * x + y
        sibling = (x, y, 1 - c)
        chips = _other_chips(x, y)

        def landed(a, s, m, j, core):
            half = ins[a].shape[0] // 2
            return outs[a].at[2 * chips[j][0] + chips[j][1], pl.ds(core * half + s, m)]

        sent = []
        for k, (a, q, s, m, j) in enumerate(entries):
            if j < 2:
                half = ins[a].shape[0] // 2
                cp = _remote(ins[a].at[pl.ds(c * half + s, m)], outs[a].at[me, pl.ds(c * half + s, m)],
                             send_sems.at[k], recv_sems.at[k], (*chips[j], c))
                cp.start()
                sent.append(cp)

        def pass_to_sibling(k, blk):
            fw = _remote(blk, blk, send_sems.at[n_ici + k], recv_sems.at[n_ici + k], sibling)
            fw.start()
            sent.append(fw)

        for k, (a, q, s, m, j) in enumerate(entries):
            if j < 2:
                blk = landed(a, s, m, j, c)
                _remote(blk, blk, send_sems.at[k], recv_sems.at[k], (*chips[j], c)).wait_recv()
                first = q < (len(halves[a]) + 1) // 2
                if (j == 0) == first:
                    on = slot[(a, q, 2)]
                    rl = _remote(blk, blk, send_sems.at[on], recv_sems.at[on], (*chips[1 - j], c))
                    rl.start()
                    sent.append(rl)
                pass_to_sibling(k, blk)
        for k, (a, q, s, m, j) in enumerate(entries):
            if j == 2:
                blk = landed(a, s, m, j, c)
                _remote(blk, blk, send_sems.at[k], recv_sems.at[k], (*chips[j], c)).wait_recv()
                pass_to_sibling(k, blk)
        for k, (a, q, s, m, j) in enumerate(entries):
            blk = landed(a, s, m, j, 1 - c)
            _remote(blk, blk, send_sems.at[n_ici + k], recv_sems.at[n_ici + k], sibling).wait_recv()
        for cp in sent:
            cp.wait_send()

    gathered = _pcall(
        body, name="allgather_weights", in_specs=[ANY] * n, out_specs=[ANY] * n,
        out_shape=[jax.ShapeDtypeStruct((4,) + p.shape, p.dtype) for p in pieces],
        scratch_shapes=[pltpu.SemaphoreType.DMA((2 * n_ici,)), pltpu.SemaphoreType.DMA((2 * n_ici,))],
        compiler_params=pltpu.CompilerParams(has_side_effects=True),
    )(*pieces)
    x, y, _ = _place()
    return [lax.dynamic_update_slice(g, p[None], (2 * x + y, 0, 0)) for g, p in zip(gathered, pieces)]


def _sibling_exchange(grads):
    n = len(grads)
    chunks = [_row_chunks(g.shape[1] // 2, g.shape[2] * g.dtype.itemsize) for g in grads]
    n_sem = 4 * sum(len(ch) for ch in chunks)

    def body(*refs):
        ins, gots = refs[:n], refs[n:2 * n]
        send_sems, recv_sems = refs[2 * n:]
        x, y, c = _place()
        sibling = (x, y, 1 - c)
        work = []
        for a in range(n):
            half = ins[a].shape[1] // 2
            for piece in range(4):
                for s, m in chunks[a]:
                    k = len(work)
                    cp = _remote(ins[a].at[piece, pl.ds((1 - c) * half + s, m)], gots[a].at[piece, pl.ds(s, m)],
                                 send_sems.at[k], recv_sems.at[k], sibling)
                    cp.start()
                    work.append(cp)
        for cp in work:
            cp.wait()

    return _pcall(
        body, name="grad_sibling_exchange", in_specs=[ANY] * n, out_specs=[ANY] * n,
        out_shape=[jax.ShapeDtypeStruct((4, g.shape[1] // 2, g.shape[2]), g.dtype) for g in grads],
        scratch_shapes=[pltpu.SemaphoreType.DMA((n_sem,)), pltpu.SemaphoreType.DMA((n_sem,))],
        compiler_params=pltpu.CompilerParams(has_side_effects=True),
    )(*grads)


def _sibling_gather(fulls):
    n = len(fulls)
    chunks = [_row_chunks(f.shape[0] // 2, f.shape[1] * f.dtype.itemsize) for f in fulls]
    n_sem = sum(len(ch) for ch in chunks)

    def body(*refs):
        outs = refs[n:2 * n]
        send_sems, recv_sems = refs[2 * n:]
        x, y, c = _place()
        sibling = (x, y, 1 - c)
        work = []
        for a in range(n):
            h = outs[a].shape[0] // 2
            for s, m in chunks[a]:
                k = len(work)
                mine = outs[a].at[pl.ds(c * h + s, m)]
                cp = _remote(mine, mine, send_sems.at[k], recv_sems.at[k], sibling)
                cp.start()
                work.append((a, s, m, cp))
        for k, (a, s, m, cp) in enumerate(work):
            h = outs[a].shape[0] // 2
            cp.wait_send()
            theirs = outs[a].at[pl.ds((1 - c) * h + s, m)]
            _remote(theirs, theirs, send_sems.at[k], recv_sems.at[k], sibling).wait_recv()

    return _pcall(
        body, name="grad_sibling_gather", in_specs=[ANY] * n, out_specs=[ANY] * n,
        out_shape=[jax.ShapeDtypeStruct(f.shape, f.dtype) for f in fulls],
        input_output_aliases={a: a for a in range(n)},
        scratch_shapes=[pltpu.SemaphoreType.DMA((n_sem,)), pltpu.SemaphoreType.DMA((n_sem,))],
        compiler_params=pltpu.CompilerParams(has_side_effects=True),
    )(*fulls)


def _pair_sum(grad, got, place, name):
    _, rows, cols = grad.shape
    half = rows // 2
    tr = _row_tile(half, cols, 16)

    def body(p_ref, a_ref, b_ref, o_ref):
        o_ref[...] = (a_ref[...].astype(F32) + b_ref[...].astype(F32)).astype(BF16)

    return _pcall(
        body, name=name,
        grid_spec=pltpu.PrefetchScalarGridSpec(
            num_scalar_prefetch=1, grid=(4, half // tr),
            in_specs=[pl.BlockSpec((None, tr, cols), lambda k, i, p: (k, p[1] * (half // tr) + i, 0)),
                      pl.BlockSpec((None, tr, cols), lambda k, i, p: (k, i, 0))],
            out_specs=pl.BlockSpec((None, tr, cols), lambda k, i, p: (k, i, 0))),
        out_shape=jax.ShapeDtypeStruct((4, half, cols), BF16),
        compiler_params=_params("parallel", "parallel"),
    )(place, grad, got)


def _chip_sum(sums, got, place, name):
    _, h, cols = sums.shape
    tr = _row_tile(h, cols, 16)

    def body(p_ref, own_ref, g0, g1, g2, o_ref):
        o_ref[...] = ((own_ref[...].astype(F32) + g0[...].astype(F32)) + g1[...].astype(F32)) + g2[...].astype(F32)

    gspec = lambda j: pl.BlockSpec((None, tr, cols), lambda i, p: (j, i, 0))
    return _pcall(
        body, name=name,
        grid_spec=pltpu.PrefetchScalarGridSpec(
            num_scalar_prefetch=1, grid=(h // tr,),
            in_specs=[pl.BlockSpec((None, tr, cols), lambda i, p: (p[0], i, 0)), gspec(0), gspec(1), gspec(2)],
            out_specs=pl.BlockSpec((tr, cols), lambda i, p: (p[1] * (h // tr) + i, 0))),
        out_shape=jax.ShapeDtypeStruct((2 * h, cols), F32),
        compiler_params=_params("parallel"),
    )(place, sums, got, got, got)


def _allgather8(buf, name):
    rows = buf.shape[0]

    def body(in_ref, out_ref, send_sems, recv_sems):
        x, y, c = _place()
        me = 4 * x + 2 * y + c
        out_ref[me] = in_ref[...]
        work = []
        for rel in range(1, 8):
            fx, fy, fc = (rel >> 2) & 1, (rel >> 1) & 1, rel & 1
            to = (x ^ fx, y ^ fy, c ^ fc)
            cp = _remote(in_ref, out_ref.at[me], send_sems.at[rel - 1], recv_sems.at[rel - 1], to)
            cp.start()
            work.append((cp, 4 * to[0] + 2 * to[1] + to[2]))
        for rel, (cp, frm) in enumerate(work):
            cp.wait_send()
            blk = out_ref.at[frm]
            _remote(blk, blk, send_sems.at[rel], recv_sems.at[rel], (x, y, c)).wait_recv()

    return _pcall(
        body, name=name, in_specs=[pl.BlockSpec(memory_space=pltpu.VMEM)],
        out_specs=pl.BlockSpec(memory_space=pltpu.VMEM),
        out_shape=jax.ShapeDtypeStruct((8, rows, LANE), F32),
        scratch_shapes=[pltpu.SemaphoreType.DMA((7,)), pltpu.SemaphoreType.DMA((7,))],
        compiler_params=pltpu.CompilerParams(has_side_effects=True),
    )(buf)


def _pack_rows(arrs):
    parts = []
    for a in arrs:
        f = a.reshape(-1).astype(F32)
        parts.append(jnp.pad(f, (0, (-f.shape[0]) % LANE)))
    flat = jnp.concatenate(parts)
    rows = -(-flat.shape[0] // LANE)
    rows8 = -(-rows // 8) * 8
    return jnp.pad(flat, (0, rows8 * LANE - flat.shape[0])).reshape(rows8, LANE)


def _unpack_rows(buf, shapes):
    flat = buf.reshape(-1)
    outs, off = [], 0
    for s in shapes:
        n = int(np.prod(s))
        outs.append(flat[off:off + n].reshape(s))
        off += -(-n // LANE) * LANE
    return outs


def _local_grads(x, p, target, wseg, w_br16, w_out16, w_ple16, b_gate, conv_w, conv_b, dt_bias, a_log, d_skip,
                 ssm_norm_w, ln_g, ln_b, rel_bias, finish_dx):
    nb, seq, _ = x.shape
    bmaps = jnp.asarray(_bucket_maps())
    bias = _bias_tables(rel_bias, bmaps)
    bgate8 = jnp.pad(b_gate, ((0, 5), (0, 0)))
    dils = [d for _, d in PATTERNS]

    x16 = x.astype(BF16)
    p16 = p.astype(BF16)
    x16p = [_permute(x16, d) for d in dils]
    qkv = [_proj(x16p[g], [wseg["qkv%d" % g]], BF16, "proj_qkv%d" % g, True)[0].reshape(
        nb, dils[g], seq // dils[g], -1) for g in range(3)]
    nat = {}
    for gi, (group, tm) in enumerate(NAT_GROUPS):
        outs = _proj(x16, [wseg[s] for s in group], F32, "proj_nat%d" % gi, True, tm)
        nat.update(zip(group, outs))
    att = [_attn_fwd(qkv[g], bias[g * GROUP_HEADS:(g + 1) * GROUP_HEADS], dils[g], "attn_fwd%d" % g) for g in range(3)]
    natural = lambda t, g: _unpermute(t.reshape(nb, seq, t.shape[-1]), dils[g])
    oa, o_att, lse = _combine_fwd(att[0][0], att[0][1], natural(att[1][0], 1), natural(att[1][1], 1),
                                  natural(att[2][0], 2), natural(att[2][1], 2), nat["gatt"])

    conv_wg, conv_bg = _xbc_group_order(conv_w), _xbc_group_order(conv_b)
    act = _conv_fwd(nat["xbc"], conv_wg, conv_bg, "conv_fwd")
    dt_sp, dt_sg = _softplus_sig(nat["dt"], jnp.pad(dt_bias, ((0, 0), (0, LANE - SSM_HEADS))))
    dtg, sgg = _group_lanes(dt_sp), _group_lanes(dt_sg)
    alog_g, dskip_g = _group_lanes(a_log), _group_lanes(d_skip)
    y_ssm, y_all, sprev = _ssd_fwd(act, dtg, nat["z"], alog_g, dskip_g, ssm_norm_w)

    w_bra, w_brb = w_br16[:ATT_OUT], w_br16[ATT_OUT:]
    y_a, = _proj(oa, [w_bra], F32, "proj_ya")
    y_b, = _proj(y_ssm, [w_brb], F32, "proj_yb")
    merged = _merge_fwd(y_a, y_b, nat["gm"], bgate8)
    mix, = _proj(merged, [w_out16], F32, "proj_mix")
    pw, = _proj(p16, [w_ple16], F32, "proj_ple")

    dx, dpre16, dpw16, dgp16, ln_sums = _ln_loss(x, mix, nat["gp"], pw, target, bgate8, ln_g, ln_b)
    loss_sum = (0.5 / D_MODEL) * jnp.sum(ln_sums[3])
    dmerged = _dx([dpre16], [w_out16], [], "dx_merged")
    dya16, dyb16, dgm16, mg_sums = _merge_bwd(dmerged, y_a, y_b, nat["gm"], bgate8)
    doa = _dx([dya16], [w_bra], [], "dx_oa")
    dys = _dx([dyb16], [w_brb], [], "dx_yssm")
    g_w_out, = _dw(merged, [dpre16], BF16, "dw_out")
    g_w_br = jnp.concatenate([_dw(oa, [dya16], BF16, "dw_bra")[0], _dw(y_ssm, [dyb16], BF16, "dw_brb")[0]], axis=0)
    g_w_ple, = _dw(p16, [dpw16], BF16, "dw_ple")

    do_att, do16, stats, dgatt16 = _combine_bwd(doa, nat["gatt"], o_att, lse)
    dseg = {"gatt": dgatt16, "gm": dgm16, "gp": dgp16}
    dbias = []
    for g in range(3):
        own_order = lambda t: _permute(t, dils[g]).reshape(nb, dils[g], seq // dils[g], t.shape[-1])
        cotangent = (do_att, o_att, lse) if g == 0 else (own_order(do16), own_order(stats))
        dqkv, db = _attn_bwd(qkv[g], bias[g * GROUP_HEADS:(g + 1) * GROUP_HEADS], cotangent, dils[g],
                             "attn_bwd%d" % g)
        dseg["qkv%d" % g] = dqkv.reshape(nb, seq, -1)
        dbias.append(db)
    g_rel = _bias_grad(jnp.concatenate(dbias, axis=0), bmaps)[:, 0, :NUM_BUCKETS].T

    dact, ddtg, dz, ssd_small, g_normw = _ssd_bwd(
        act, dtg, sgg, nat["z"], y_all, dys, sprev, alog_g, dskip_g, ssm_norm_w)
    dseg["z"] = dz
    dseg["dt"] = jnp.pad(_ungroup_lanes(ddtg), ((0, 0), (0, 0), (0, LANE - SSM_HEADS)))
    dpre, conv_sums = _conv_bwd_pre(dact, nat["xbc"], conv_wg, conv_bg, "conv_bwd")
    dseg["xbc"] = _conv_bwd_x(dpre, conv_wg, "conv_bwd_x")
    csum = _xbc_reference_order(conv_sums)

    dx_perm = [_unpermute(_dx([dseg["qkv%d" % g]], [wseg["qkv%d" % g]], [], "dx_qkv%d" % g, True), dils[g])
               for g in (1, 2)]
    dwseg = {"qkv%d" % g: _dw(x16p[g], [dseg["qkv%d" % g]], BF16, "dw_qkv%d" % g, True)[0] for g in range(3)}
    for gi, group in enumerate(DW_GROUPS):
        dwseg.update(zip(group, _dw(x16, [dseg[s] for s in group], BF16, "dw_nat%d" % gi, True)))
    names = ["qkv0"] + [s for group, _ in NAT_GROUPS for s in group]
    dx = finish_dx([dseg[s] for s in names], [wseg[s] for s in names], [dx] + dx_perm, dwseg, g_w_br, g_w_out, g_w_ple)

    small = dict(
        b_gate=jnp.stack([mg_sums[0], mg_sums[1], ln_sums[2]]),
        conv_w=csum[0:4], conv_b=csum[4:5],
        dt_bias=_ungroup_lanes(ssd_small[:, 2:3, :]), a_log=_ungroup_lanes(ssd_small[:, 0:1, :]),
        d_skip=_ungroup_lanes(ssd_small[:, 1:2, :]), ssm_norm_w=g_normw,
        ln_g=ln_sums[0:1], ln_b=ln_sums[1:2], rel_bias=g_rel)
    return loss_sum, dx, small


DX_TM = 256
SMALL_ORDER = ("b_gate", "conv_w", "conv_b", "dt_bias", "a_log", "d_skip", "ssm_norm_w", "ln_g", "ln_b", "rel_bias")
SMALL_FULL_SHAPES = dict(b_gate=(3, 1024), conv_w=(4, 3072), conv_b=(1, 3072), dt_bias=(1, 32), a_log=(1, 32),
                         d_skip=(1, 32), ssm_norm_w=(1, 2048), ln_g=(1, 1024), ln_b=(1, 1024), rel_bias=(32, 36))


def kernel(x, p, w_in, b_gate, conv_w, conv_b, dt_bias, a_log, d_skip, ssm_norm_w, w_branch, w_out, w_ple, ln_g, ln_b, rel_bias, loss_target, m_w_in, m_b_gate, m_conv_w, m_conv_b, m_dt_bias, m_a_log, m_d_skip, m_ssm_norm_w, m_w_branch, m_w_out, m_w_ple, m_ln_g, m_ln_b, m_rel_bias, v_w_in, v_b_gate, v_conv_w, v_conv_b, v_dt_bias, v_a_log, v_d_skip, v_ssm_norm_w, v_w_branch, v_w_out, v_w_ple, v_ln_g, v_ln_b, v_rel_bias):
    cx, cy, cc = _place()
    chip = 2 * cx + cy
    dev = 4 * cx + 2 * cy + cc

    w_in_t = jnp.transpose(w_in[0])
    win16 = _shard_to_window(w_in_t, chip)
    g_win, g_br, g_out, g_ple = _allgather_pieces(
        [win16, w_branch[0].astype(BF16), w_out[0].astype(BF16), w_ple[0].astype(BF16)])
    wseg = _assemble(g_win)
    w_br16 = g_br.reshape(4 * 704, D_MODEL)
    w_out16 = g_out.reshape(D_MODEL, D_MODEL)
    w_ple16 = jnp.transpose(g_ple, (1, 0, 2)).reshape(PLE_DIM, D_MODEL)
    shards = _allgather8(_pack_rows([b_gate[0], conv_w[0]]), "allgather_small_params")
    per_chip = [_unpack_rows(shards[2 * k], [(3, 256), (4, 768)]) for k in range(4)]
    b_gate_full = jnp.concatenate([pc[0] for pc in per_chip], axis=1)
    conv_w_full = jnp.concatenate([pc[1] for pc in per_chip], axis=1)

    place = jnp.stack([chip, cc]).astype(jnp.int32)
    reduced = []

    def finish_dx(dhs, ws, accs, dwseg, d_br, d_out, d_ple):
        grads = [_pack(dwseg), d_br.reshape(4, 704, D_MODEL), d_out.reshape(4, 256, D_MODEL),
                 jnp.transpose(d_ple.reshape(PLE_DIM, 4, 256), (1, 0, 2))]
        got = _sibling_exchange(grads)
        chip_sums = [_pair_sum(g, t, place, "grad_pair_sum_%d" % i) for i, (g, t) in enumerate(zip(grads, got))]
        dx, others = _dx(dhs, ws, accs, "dx_w_in_and_grad_chip_scatter", True, DX_TM, chip_sums)
        fulls = [_chip_sum(s, t, place, "grad_chip_sum_%d" % i) for i, (s, t) in enumerate(zip(chip_sums, others))]
        reduced.extend(_sibling_gather(fulls))
        return dx

    loss_sum, grad_x, small = _local_grads(
        x, p[0], loss_target, wseg, w_br16, w_out16, w_ple16, b_gate_full, conv_w_full, conv_b, dt_bias, a_log,
        d_skip, ssm_norm_w, ln_g, ln_b, rel_bias, finish_dx)
    big = reduced
    g_w_in = _window_to_shard(big[0], chip)
    g_w_branch, g_w_out, g_w_ple = big[1], big[2], big[3]
    parts = _allgather8(_pack_rows([small[n] for n in SMALL_ORDER] + [loss_sum.reshape(1, 1)]),
                        "allgather_small_grads")
    small_sum = _sum_rows([parts[i] for i in range(8)], F32, "small_grad_sum")
    *reduced_small, loss = _unpack_rows(small_sum, [SMALL_FULL_SHAPES[n] for n in SMALL_ORDER] + [(1, 1)])
    loss = loss.reshape(())
    sg = dict(zip(SMALL_ORDER, reduced_small))
    sg["b_gate"] = lax.dynamic_slice_in_dim(sg["b_gate"], chip * 256, 256, axis=1)
    sg["conv_w"] = lax.dynamic_slice_in_dim(sg["conv_w"], chip * 768, 768, axis=1)
    del dev

    upd = {}
    upd["w_in"] = [jnp.transpose(t) for t in _adamw(w_in_t, g_w_in, jnp.transpose(m_w_in[0]),
                                                      jnp.transpose(v_w_in[0]), "adamw_w_in")]
    upd["w_branch"] = _adamw(w_branch[0], g_w_branch, m_w_branch[0], v_w_branch[0], "adamw_w_branch")
    upd["w_out"] = _adamw(w_out[0], g_w_out, m_w_out[0], v_w_out[0], "adamw_w_out")
    upd["w_ple"] = _adamw(w_ple[0], g_w_ple, m_w_ple[0], v_w_ple[0], "adamw_w_ple")
    small_w = dict(b_gate=b_gate, conv_w=conv_w, conv_b=conv_b, dt_bias=dt_bias, a_log=a_log, d_skip=d_skip,
                   ssm_norm_w=ssm_norm_w, ln_g=ln_g, ln_b=ln_b, rel_bias=rel_bias)
    small_m = dict(b_gate=m_b_gate, conv_w=m_conv_w, conv_b=m_conv_b, dt_bias=m_dt_bias, a_log=m_a_log,
                   d_skip=m_d_skip, ssm_norm_w=m_ssm_norm_w, ln_g=m_ln_g, ln_b=m_ln_b, rel_bias=m_rel_bias)
    small_v = dict(b_gate=v_b_gate, conv_w=v_conv_w, conv_b=v_conv_b, dt_bias=v_dt_bias, a_log=v_a_log,
                   d_skip=v_d_skip, ssm_norm_w=v_ssm_norm_w, ln_g=v_ln_g, ln_b=v_ln_b, rel_bias=v_rel_bias)
    shapes = [small_w[n].shape for n in SMALL_ORDER]
    s_delta, s_m, s_v = _adamw(_pack_rows([small_w[n] for n in SMALL_ORDER]), _pack_rows([sg[n] for n in SMALL_ORDER]),
                               _pack_rows([small_m[n] for n in SMALL_ORDER]), _pack_rows([small_v[n] for n in SMALL_ORDER]),
                               "adamw_small")
    for i, n in enumerate(SMALL_ORDER):
        upd[n] = tuple(_unpack_rows(t, shapes)[i] for t in (s_delta, s_m, s_v))
        sg[n] = sg[n].reshape(small_w[n].shape)

    order = ("w_in", "b_gate", "conv_w", "conv_b", "dt_bias", "a_log", "d_skip", "ssm_norm_w", "w_branch", "w_out",
             "w_ple", "ln_g", "ln_b", "rel_bias")
    grads = dict(sg, w_in=jnp.transpose(g_w_in)[None],w_branch=g_w_branch[None], w_out=g_w_out[None], w_ple=g_w_ple[None])
    lead = lambda n, t: t[None] if n in ("w_in", "w_branch", "w_out", "w_ple") else t
    return (loss, grad_x, *[grads[n] for n in order], *[lead(n, upd[n][0]) for n in order],
            *[lead(n, upd[n][1]) for n in order], *[lead(n, upd[n][2]) for n in order])
```

```python
import functools
import math

import numpy as np
import jax
import jax.numpy as jnp
from jax import lax
from jax.experimental import pallas as pl
from jax.experimental.pallas import tpu as pltpu

F32, BF16 = jnp.float32, jnp.bfloat16

D_MODEL = 1024
HEAD_DIM = 64
GROUP_HEADS = 12
ATT_OUT = GROUP_HEADS * HEAD_DIM
PATTERNS = ((128, 1), (512, 4), (2048, 16))
BAND = 128
NUM_BUCKETS = 32
MAX_DISTANCE = 2048
D_INNER = 2048
SSM_HEADS = 32
SSM_GROUPS = 4
GROUP_SSM_HEADS = SSM_HEADS // SSM_GROUPS
D_STATE = 128
CHUNK = 128
PLE_DIM = 256
ALPHA = 2.0 ** 0.25
LN_EPS = 1e-5
RMS_EPS = 1e-5
ADAM_LR, ADAM_B1, ADAM_B2, ADAM_EPS, ADAM_WD, ADAM_STEP = 0.001, 0.9, 0.999, 1e-08, 0.01, 10
NEG = -1e30

QKV_W = 3 * ATT_OUT
IN_COLS = 15904
SHARD_COLS = IN_COLS // 4
DT_COL = 12800
ROW_TILE = 16
WIN_ROWS = 4000


def _win_offset(k):
    return (k * SHARD_COLS) % ROW_TILE


def _win_start(k):
    return k * SHARD_COLS - _win_offset(k)

VMEM_LIMIT_BYTES = 56 * 1024 * 1024
LANE = 128
MESH = pl.DeviceIdType.MESH
NT = (((1,), (1,)), ((), ()))
TN = (((0,), (0,)), ((), ()))


def _pcall(body, **kw):
    return pl.pallas_call(body, **kw)


def _params(*sem):
    return pltpu.CompilerParams(dimension_semantics=sem, vmem_limit_bytes=VMEM_LIMIT_BYTES)


def _sigmoid(v):
    return jax.nn.sigmoid(v)


MM_TM = 512


def _permute(t, d):
    nb, seq, ch = t.shape
    return t if d == 1 else t.reshape(nb, seq // d, d, ch).transpose(0, 2, 1, 3).reshape(nb, seq, ch)


def _unpermute(t, d):
    nb, seq, ch = t.shape
    return t if d == 1 else t.reshape(nb, d, seq // d, ch).transpose(0, 2, 1, 3).reshape(nb, seq, ch)


def _tok_spec(tm, width):
    return pl.BlockSpec((None, tm, width), lambda b, i: (b, i, 0))


def _whole(arr, single_buffer=False):
    mode = dict(pipeline_mode=pl.Buffered(1)) if single_buffer else {}
    return pl.BlockSpec(arr.shape, lambda b, i: (0,) * arr.ndim, **mode)


def _proj(a3, ws, out_dtype, name, w_rows_are_outputs=False, tm=MM_TM):
    nb, seq, kdim = a3.shape
    nw = len(ws)
    widths = [w.shape[0] if w_rows_are_outputs else w.shape[1] for w in ws]

    def body(*refs):
        a = refs[0][...].astype(BF16)
        for w_ref, o_ref in zip(refs[1:1 + nw], refs[1 + nw:]):
            if w_rows_are_outputs:
                v = lax.dot_general(a, w_ref[...], NT, preferred_element_type=F32)
            else:
                v = jnp.dot(a, w_ref[...], preferred_element_type=F32)
            o_ref[...] = v.astype(out_dtype)

    return _pcall(
        body, name=name, grid=(nb, seq // tm),
        in_specs=[_tok_spec(tm, kdim)] + [_whole(w) for w in ws],
        out_specs=[_tok_spec(tm, n) for n in widths],
        out_shape=[jax.ShapeDtypeStruct((nb, seq, n), out_dtype) for n in widths],
        compiler_params=_params("parallel", "parallel"),
    )(a3, *ws)


def _dx(dhs, ws, accs, name, w_rows_are_outputs=False, tm=MM_TM, scatter=None):
    nb, seq, _ = dhs[0].shape
    nd, nacc = len(dhs), len(accs)
    kout = ws[0].shape[1] if w_rows_are_outputs else ws[0].shape[0]
    sums = scatter or []
    ns = len(sums)
    chunks = [_row_chunks(s.shape[1], s.shape[2] * s.dtype.itemsize) for s in sums]
    n_sem = 3 * sum(len(ch) for ch in chunks)
    grid = (nb, seq // tm)

    def body(*refs):
        n_in = 2 * nd + nacc
        sum_refs, o_ref, got_refs = refs[n_in:n_in + ns], refs[n_in + ns], refs[n_in + ns + 1:n_in + 2 * ns + 1]

        def copies():
            send_sems, recv_sems = refs[-2], refs[-1]
            x, y, c = _place()
            out = []
            for a in range(ns):
                for s, m in chunks[a]:
                    for j, (cx, cy) in enumerate(_other_chips(x, y)):
                        k = len(out)
                        out.append(_remote(sum_refs[a].at[2 * cx + cy, pl.ds(s, m)], got_refs[a].at[j, pl.ds(s, m)],
                                           send_sems.at[k], recv_sems.at[k], (cx, cy, c)))
            return out

        if ns:
            @pl.when((pl.program_id(0) == 0) & (pl.program_id(1) == 0))
            def _():
                for cp in copies():
                    cp.start()

        v = None
        for dh_ref, w_ref in zip(refs[:nd], refs[nd:2 * nd]):
            dh = dh_ref[...].astype(BF16)
            if w_rows_are_outputs:
                t = jnp.dot(dh, w_ref[...], preferred_element_type=F32)
            else:
                t = lax.dot_general(dh, w_ref[...], NT, preferred_element_type=F32)
            v = t if v is None else v + t
        for a_ref in refs[2 * nd:n_in]:
            v = v + a_ref[...]
        o_ref[...] = v

        if ns:
            @pl.when((pl.program_id(0) == grid[0] - 1) & (pl.program_id(1) == grid[1] - 1))
            def _():
                for cp in copies():
                    cp.wait()

    out = _pcall(
        body, name=name, grid=grid,
        in_specs=[_tok_spec(tm, dh.shape[-1]) for dh in dhs] + [_whole(w, bool(ns)) for w in ws]
        + [_tok_spec(tm, kout)] * nacc + [ANY] * ns,
        out_specs=[_tok_spec(tm, kout)] + [ANY] * ns,
        out_shape=[jax.ShapeDtypeStruct((nb, seq, kout), F32)]
        + [jax.ShapeDtypeStruct((3,) + s.shape[1:], s.dtype) for s in sums],
        input_output_aliases={2 * nd: 0} if nacc else {},
        scratch_shapes=[pltpu.SemaphoreType.DMA((n_sem,)), pltpu.SemaphoreType.DMA((n_sem,))] if ns else [],
        compiler_params=pltpu.CompilerParams(
            dimension_semantics=("arbitrary", "arbitrary") if ns else ("parallel", "parallel"),
            vmem_limit_bytes=VMEM_LIMIT_BYTES, has_side_effects=bool(ns)),
    )(*dhs, *ws, *accs, *sums)
    return (out[0], list(out[1:])) if ns else out[0]


def _dw(a3, dhs, out_dtype, name, rows_are_outputs=False):
    nb, seq, kdim = a3.shape
    nd = len(dhs)
    grid = (nb, seq // MM_TM)
    shapes = [(dh.shape[-1], kdim) if rows_are_outputs else (kdim, dh.shape[-1]) for dh in dhs]

    def body(*refs):
        b, i = pl.program_id(0), pl.program_id(1)
        dh_refs, o_refs, acc_refs = refs[1:1 + nd], refs[1 + nd:1 + 2 * nd], refs[1 + 2 * nd:]

        @pl.when((b == 0) & (i == 0))
        def _():
            for acc_ref in acc_refs:
                acc_ref[...] = jnp.zeros_like(acc_ref)

        a = refs[0][...].astype(BF16)
        for dh_ref, acc_ref in zip(dh_refs, acc_refs):
            dh = dh_ref[...].astype(BF16)
            acc_ref[...] += lax.dot_general(*((dh, a) if rows_are_outputs else (a, dh)), TN,
                                            preferred_element_type=F32)

        @pl.when((b == grid[0] - 1) & (i == grid[1] - 1))
        def _():
            for o_ref, acc_ref in zip(o_refs, acc_refs):
                o_ref[...] = acc_ref[...].astype(out_dtype)

    return _pcall(
        body, name=name, grid=grid,
        in_specs=[_tok_spec(MM_TM, kdim)] + [_tok_spec(MM_TM, dh.shape[-1]) for dh in dhs],
        out_specs=[pl.BlockSpec(s, lambda b, i: (0, 0)) for s in shapes],
        out_shape=[jax.ShapeDtypeStruct(s, out_dtype) for s in shapes],
        scratch_shapes=[pltpu.VMEM(s, F32) for s in shapes],
        compiler_params=_params("arbitrary", "arbitrary"),
    )(a3, *dhs)


def _qkv_rows(g):
    return [(part * QKV_W + g * ATT_OUT + hp * LANE, LANE) for hp in range(ATT_OUT // LANE) for part in range(3)]


XBC_START = 3 * QKV_W + ATT_OUT + D_INNER
GROUP_CH = GROUP_SSM_HEADS * HEAD_DIM
XBC_GROUP = GROUP_CH + 2 * D_STATE
CONV_DIM = SSM_GROUPS * XBC_GROUP


def _xbc_ranges():
    out = []
    for g in range(SSM_GROUPS):
        out += [(g * GROUP_CH, GROUP_CH), (D_INNER + g * D_STATE, D_STATE),
                (D_INNER + SSM_GROUPS * D_STATE + g * D_STATE, D_STATE)]
    return out


def _xbc_group_order(t):
    return jnp.concatenate([t[..., s:s + n] for s, n in _xbc_ranges()], axis=-1)


def _xbc_reference_order(t):
    g = lambda off, n: [t[..., k * XBC_GROUP + off:k * XBC_GROUP + off + n] for k in range(SSM_GROUPS)]
    return jnp.concatenate(g(0, GROUP_CH) + g(GROUP_CH, D_STATE) + g(GROUP_CH + D_STATE, D_STATE), axis=-1)


def _segments():
    one = lambda name, start, rows: (name, [(start, rows)], max(rows, LANE))
    return [("qkv%d" % g, _qkv_rows(g), QKV_W) for g in range(3)] + [
        one("gatt", 3 * QKV_W, ATT_OUT), one("z", 3 * QKV_W + ATT_OUT, D_INNER),
        ("xbc", [(XBC_START + s, n) for s, n in _xbc_ranges()], CONV_DIM), one("dt", DT_COL, SSM_HEADS),
        one("gm", DT_COL + SSM_HEADS, 2 * D_MODEL), one("gp", DT_COL + SSM_HEADS + 2 * D_MODEL, D_MODEL)]


LAYOUT_TC = 256
NAT_GROUPS = ((("gatt", "z", "dt", "gp"), 512), (("xbc", "gm"), 256))
DW_GROUPS = (("gatt", "z", "dt", "gp"), ("xbc",), ("gm",))


def _assemble(win):
    segs = _segments()

    def body(win_ref, *outs):
        def pieces(start, rows):
            t, end = start, start + rows
            while t < end:
                k = min(t // SHARD_COLS, 3)
                shard_end = (k + 1) * SHARD_COLS
                if k < 3 and shard_end % ROW_TILE and t == shard_end - shard_end % ROW_TILE:
                    lo = t - _win_start(k)
                    yield win_ref[k, lo:lo + ROW_TILE, :] + win_ref[k + 1, 0:ROW_TILE, :]
                    t += ROW_TILE
                    continue
                upto = min(end, shard_end - shard_end % ROW_TILE if k < 3 else end)
                yield win_ref[k, t - _win_start(k):upto - _win_start(k), :]
                t = upto

        for (_, ranges, total), o_ref in zip(segs, outs):
            off = 0
            for start, rows in ranges:
                for part in pieces(start, rows):
                    o_ref[off:off + part.shape[0], :] = part
                    off += part.shape[0]
            if off < total:
                o_ref[off:total, :] = jnp.zeros((total - off, o_ref.shape[1]), BF16)

    outs = _pcall(
        body, name="assemble_w_in", grid=(D_MODEL // LAYOUT_TC,),
        in_specs=[pl.BlockSpec((4, WIN_ROWS, LAYOUT_TC), lambda i: (0, 0, i))],
        out_specs=[pl.BlockSpec((total, LAYOUT_TC), lambda i: (0, i)) for _, _, total in segs],
        out_shape=[jax.ShapeDtypeStruct((total, D_MODEL), BF16) for _, _, total in segs],
        compiler_params=_params("parallel"),
    )(win)
    return {name: o for (name, _, _), o in zip(segs, outs)}


def _pack(dsegs):
    segs = _segments()

    def body(*refs):
        ins, o_ref = refs[:-1], refs[-1]
        tail = IN_COLS - _win_start(3)
        o_ref[3, tail:, :] = jnp.zeros((WIN_ROWS - tail, o_ref.shape[2]), BF16)
        for (_, ranges, _), s_ref in zip(segs, ins):
            off = 0
            for start, rows in ranges:
                for k in range(4):
                    lo = _win_start(k)
                    a, b = max(start, lo), min(start + rows, lo + WIN_ROWS)
                    if a < b:
                        o_ref[k, a - lo:b - lo, :] = s_ref[off + a - start:off + b - start, :]
                off += rows

    return _pcall(
        body, name="pack_dw_in", grid=(D_MODEL // LAYOUT_TC,),
        in_specs=[pl.BlockSpec((total, LAYOUT_TC), lambda i: (0, i)) for _, _, total in segs],
        out_specs=pl.BlockSpec((4, WIN_ROWS, LAYOUT_TC), lambda i: (0, 0, i)),
        out_shape=jax.ShapeDtypeStruct((4, WIN_ROWS, D_MODEL), BF16),
        compiler_params=_params("parallel"),
    )(*[dsegs[name] for name, _, _ in segs])


def _shard_to_window(shard_t, k):
    def at(off):
        return lambda w: jnp.pad(w.astype(BF16), ((off, WIN_ROWS - SHARD_COLS - off), (0, 0)))

    return lax.cond(k % 2 == 1, at(_win_offset(1)), at(_win_offset(0)), shard_t)


def _window_to_shard(win, k):
    return lax.dynamic_slice(win, ((k % 2) * _win_offset(1), 0), (SHARD_COLS, D_MODEL))


def _bucket_maps():
    qi = np.arange(BAND)[:, None]
    kj = np.arange(2 * BAND)[None, :]
    delta = qi + BAND - kj
    maps = []
    for window, dil in PATTERNS:
        valid = (delta >= 0) & (delta <= window // dil)
        dist = np.maximum(delta, 0) * dil
        max_exact = NUM_BUCKETS // 2
        d_f = np.maximum(dist, 1).astype(np.float32)
        large = max_exact + (np.log(d_f / np.float32(max_exact)) / np.float32(math.log(MAX_DISTANCE / max_exact))
                             * np.float32(NUM_BUCKETS - max_exact)).astype(np.int32)
        large = np.minimum(large, NUM_BUCKETS - 1)
        bucket = np.where(dist < max_exact, dist, large)
        maps.append(np.where(valid, bucket, -1).astype(np.int32))
    return np.stack(maps)


def _bias_tables(rel_bias, bmaps):
    def body(rb_ref, bm_ref, o_ref):
        h = pl.program_id(0)
        bm = bm_ref[...]
        acc = jnp.full(bm.shape, NEG, F32)
        for b in range(NUM_BUCKETS):
            acc = jnp.where(bm == b, rb_ref[b, h], acc)
        o_ref[...] = acc

    return _pcall(
        body, name="bias_tables", grid=(3 * GROUP_HEADS,),
        in_specs=[pl.BlockSpec(memory_space=pltpu.SMEM),
                  pl.BlockSpec((None, BAND, 2 * BAND), lambda h: (h // GROUP_HEADS, 0, 0))],
        out_specs=pl.BlockSpec((None, BAND, 2 * BAND), lambda h: (h, 0, 0)),
        out_shape=jax.ShapeDtypeStruct((3 * GROUP_HEADS, BAND, 2 * BAND), F32),
        compiler_params=_params("parallel"),
    )(rel_bias, bmaps)


def _bias_grad(dbias, bmaps):
    def body(db_ref, bm_ref, o_ref):
        bm = bm_ref[...]
        db = db_ref[...]
        lane = lax.broadcasted_iota(jnp.int32, (1, LANE), 1)
        vec = jnp.zeros((1, LANE), F32)
        for b in range(NUM_BUCKETS):
            s = jnp.sum(jnp.where(bm == b, db, 0.0), keepdims=True)
            vec = jnp.where(lane == b, s, vec)
        o_ref[...] = vec

    return _pcall(
        body, name="bias_grad", grid=(3 * GROUP_HEADS,),
        in_specs=[pl.BlockSpec((None, BAND, 2 * BAND), lambda h: (h, 0, 0)),
                  pl.BlockSpec((None, BAND, 2 * BAND), lambda h: (h // GROUP_HEADS, 0, 0))],
        out_specs=pl.BlockSpec((None, 1, LANE), lambda h: (h, 0, 0)),
        out_shape=jax.ShapeDtypeStruct((3 * GROUP_HEADS, 1, LANE), F32),
        compiler_params=_params("parallel"),
    )(dbias, bmaps)


def _rows(n):
    if isinstance(n, int):
        return pl.ds(n * BAND, BAND)
    return pl.ds(pl.multiple_of(n * BAND, BAND), BAND)


def _for_blocks(blocks, nblk, per, carry):
    carry = blocks([0], carry, False)
    start = 1 + (nblk - 1) % per
    for n in range(1, start):
        carry = blocks([n], carry, True)
    trips = (nblk - start) // per
    if trips > 0:
        carry = lax.fori_loop(
            0, trips, lambda t, c: blocks([start + t * per + u for u in range(per)], c, True), carry)
    return carry


def _pairs_per_step(d):
    return {1: 3, 4: 6, 16: 6}[d]


def _attn_fwd(qkv4, bias, d, name):
    nb, _, sub, _ = qkv4.shape
    nblk = sub // BAND
    scale = HEAD_DIM ** -0.5
    npair = ATT_OUT // LANE
    hps = _pairs_per_step(d)
    compact = d > 1

    def body(qkv_ref, bias_ref, o_ref, l_ref):
        def blocks(ns, carry, with_prev):
            chains = [(bi, i, h) for bi in range(len(ns)) for i in range(hps) for h in range(2)]
            first_head = lax.broadcasted_iota(jnp.int32, (BAND, LANE), 1) < HEAD_DIM
            pair = lambda n, i, part: qkv_ref[_rows(n), (3 * i + part) * LANE:(3 * i + part + 1) * LANE]
            scores = []
            for bi, i, h in chains:
                n = ns[bi]
                qp = pair(n, i, 0) * scale
                q = jnp.where(first_head if h == 0 else jnp.logical_not(first_head), qp, jnp.zeros_like(qp))
                s_c = lax.dot_general(q, pair(n, i, 1), NT, preferred_element_type=F32) + bias_ref[2 * i + h, :, BAND:]
                s_p = None
                if with_prev:
                    s_p = lax.dot_general(q, pair(n - 1, i, 1), NT,
                                          preferred_element_type=F32) + bias_ref[2 * i + h, :, :BAND]
                scores.append((s_c, s_p))
            probs = []
            for s_c, s_p in scores:
                m = jnp.max(s_c, -1, keepdims=True)
                if with_prev:
                    m = jnp.maximum(m, jnp.max(s_p, -1, keepdims=True))
                e_c = jnp.exp(s_c - m)
                den = jnp.sum(e_c, -1, keepdims=True)
                e_p = None
                if with_prev:
                    e_p = jnp.exp(s_p - m)
                    den = den + jnp.sum(e_p, -1, keepdims=True)
                    e_p = e_p.astype(BF16)
                probs.append((e_c.astype(BF16), e_p, den, m))
            outs = {}
            for (bi, i, h), (e_c, e_p, den, m) in zip(chains, probs):
                n = ns[bi]
                acc = jnp.dot(e_c, pair(n, i, 2), preferred_element_type=F32)
                if with_prev:
                    acc = acc + jnp.dot(e_p, pair(n - 1, i, 2), preferred_element_type=F32)
                outs[(bi, i, h)] = (acc / den, m + jnp.log(den))
            lane = lax.broadcasted_iota(jnp.int32, (BAND, LANE), 1)
            for bi, n in enumerate(ns):
                per_head = jnp.zeros((BAND, LANE), F32)
                for i in range(hps):
                    o_ref[_rows(n), i * LANE:(i + 1) * LANE] = jnp.where(first_head, outs[(bi, i, 0)][0],
                                                                         outs[(bi, i, 1)][0])
                    if compact:
                        for h in range(2):
                            per_head = jnp.where(lane == 2 * i + h, outs[(bi, i, h)][1], per_head)
                    else:
                        l_ref[_rows(n), i * LANE:(i + 1) * LANE] = jnp.where(first_head, outs[(bi, i, 0)][1],
                                                                             outs[(bi, i, 1)][1])
                if compact:
                    l_ref[_rows(n), :] = per_head
            return carry

        _for_blocks(blocks, nblk, 2 if hps == 1 else 1, 0)

    in_specs = [pl.BlockSpec((None, None, sub, 3 * LANE * hps), lambda hp, b, r: (b, r, 0, hp)),
                pl.BlockSpec((2 * hps, BAND, 2 * BAND), lambda hp, b, r: (hp, 0, 0))]
    if compact:
        return _pcall(
            body, name=name, grid=(1, nb, d), in_specs=in_specs,
            out_specs=[pl.BlockSpec((None, None, sub, ATT_OUT), lambda hp, b, r: (b, r, 0, 0)),
                       pl.BlockSpec((None, None, sub, LANE), lambda hp, b, r: (b, r, 0, 0))],
            out_shape=[jax.ShapeDtypeStruct((nb, d, sub, ATT_OUT), F32), jax.ShapeDtypeStruct((nb, d, sub, LANE), F32)],
            compiler_params=_params("parallel", "parallel", "parallel"),
        )(qkv4, bias)
    ospec = pl.BlockSpec((None, sub, hps * LANE), lambda hp, b, r: (b, 0, r * (npair // hps) + hp))
    return _pcall(
        body, name=name, grid=(npair // hps, nb, d), in_specs=in_specs, out_specs=[ospec, ospec],
        out_shape=[jax.ShapeDtypeStruct((nb, sub, d * ATT_OUT), F32)] * 2,
        compiler_params=_params("parallel", "parallel", "parallel"),
    )(qkv4, bias)


STAT_LSE_LANE = 16


def _attn_bwd(qkv4, bias, cotangent, d, name):
    nb, _, sub, _ = qkv4.shape
    nblk = sub // BAND
    scale = HEAD_DIM ** -0.5
    npair = ATT_OUT // LANE
    hps = _pairs_per_step(d)
    compact = d > 1

    def body(qkv_ref, bias_ref, *rest):
        do_ref, dqkv_ref, db_ref = rest[0], rest[-2], rest[-1]
        b, r = pl.program_id(1), pl.program_id(2)

        @pl.when((b == 0) & (r == 0))
        def _():
            db_ref[...] = jnp.zeros_like(db_ref)

        def blocks(ns, carry, with_prev):
            sides = (0, 1) if with_prev else (0,)
            chains = [(bi, i, h, sd) for bi in range(len(ns)) for i in range(hps) for h in range(2) for sd in sides]
            first_head = lax.broadcasted_iota(jnp.int32, (BAND, LANE), 1) < HEAD_DIM
            own = lambda h, t: jnp.where(first_head if h == 0 else jnp.logical_not(first_head), t, jnp.zeros_like(t))
            pair = lambda rows, i, part: qkv_ref[rows, (3 * i + part) * LANE:(3 * i + part + 1) * LANE]
            key_rows = lambda bi, sd: _rows(ns[bi] - sd)
            qs = {}
            for bi in range(len(ns)):
                for i in range(hps):
                    q_pair = pair(_rows(ns[bi]), i, 0) * scale
                    do = do_ref[_rows(ns[bi]), i * LANE:(i + 1) * LANE]
                    do16 = do.astype(BF16)
                    for h in range(2):
                        if compact:
                            st_ref, head = rest[1], 2 * i + h
                            ebar = st_ref[_rows(ns[bi]), head:head + 1]
                            lcol = st_ref[_rows(ns[bi]), STAT_LSE_LANE + head:STAT_LSE_LANE + head + 1]
                        else:
                            ebar = jnp.sum(own(h, do * rest[1][_rows(ns[bi]), i * LANE:(i + 1) * LANE]), -1, keepdims=True)
                            lcol = rest[2][_rows(ns[bi]), i * LANE + h * HEAD_DIM:i * LANE + h * HEAD_DIM + 1]
                        qs[(bi, i, h)] = (own(h, q_pair), q_pair, own(h, do16), do16, ebar, lcol)
            raw = []
            for bi, i, h, sd in chains:
                q, _, do_h, _, _, _ = qs[(bi, i, h)]
                bias_blk = bias_ref[2 * i + h, :, :BAND] if sd else bias_ref[2 * i + h, :, BAND:]
                s = lax.dot_general(q, pair(key_rows(bi, sd), i, 1), NT, preferred_element_type=F32) + bias_blk
                dp = lax.dot_general(do_h, pair(key_rows(bi, sd), i, 2), NT, preferred_element_type=F32)
                raw.append((s, dp))
            soft = []
            for (bi, i, h, sd), (s, dp) in zip(chains, raw):
                ebar, lcol = qs[(bi, i, h)][4:]
                p = jnp.exp(s - lcol)
                ds = p * (dp - ebar)
                if sd:
                    db_ref[2 * i + h, :, :BAND] += ds
                else:
                    db_ref[2 * i + h, :, BAND:] += ds
                soft.append((p.astype(BF16), ds.astype(BF16)))
            grads = {}
            for (bi, i, h, sd), (p16, ds16) in zip(chains, soft):
                _, q_pair, _, do16 = qs[(bi, i, h)][:4]
                grads[(bi, i, h, sd)] = (
                    jnp.dot(ds16, pair(key_rows(bi, sd), i, 1), preferred_element_type=F32),
                    lax.dot_general(ds16, q_pair, TN, preferred_element_type=F32),
                    lax.dot_general(p16, do16, TN, preferred_element_type=F32))
            both = lambda bi, i, sd, which: jnp.where(first_head, grads[(bi, i, 0, sd)][which],
                                                      grads[(bi, i, 1, sd)][which])
            carry = list(carry) if carry is not None else None
            for bi, n in enumerate(ns):
                for i in range(hps):
                    base = 3 * LANE * i
                    dq = both(bi, i, 0, 0)
                    if with_prev:
                        dq = dq + both(bi, i, 1, 0)
                        dqkv_ref[_rows(n - 1), base + LANE:base + 2 * LANE] = (
                            carry[2 * i] + both(bi, i, 1, 1)).astype(BF16)
                        dqkv_ref[_rows(n - 1), base + 2 * LANE:base + 3 * LANE] = (
                            carry[2 * i + 1] + both(bi, i, 1, 2)).astype(BF16)
                    dqkv_ref[_rows(n), base:base + LANE] = (dq * scale).astype(BF16)
                carry = [t for i in range(hps) for t in (both(bi, i, 0, 1), both(bi, i, 0, 2))]
            return tuple(carry)

        carry = _for_blocks(blocks, nblk, 2 if hps == 1 else 1, None)
        for i in range(hps):
            base = 3 * LANE * i
            dqkv_ref[_rows(nblk - 1), base + LANE:base + 2 * LANE] = carry[2 * i].astype(BF16)
            dqkv_ref[_rows(nblk - 1), base + 2 * LANE:base + 3 * LANE] = carry[2 * i + 1].astype(BF16)

    qspec = pl.BlockSpec((None, None, sub, 3 * LANE * hps), lambda hp, b, r: (b, r, 0, hp))
    bspec = pl.BlockSpec((2 * hps, BAND, 2 * BAND), lambda hp, b, r: (hp, 0, 0))
    if compact:
        cspecs = [pl.BlockSpec((None, None, sub, ATT_OUT), lambda hp, b, r: (b, r, 0, 0)),
                  pl.BlockSpec((None, None, sub, LANE), lambda hp, b, r: (b, r, 0, 0))]
    else:
        cspecs = [pl.BlockSpec((None, sub, hps * LANE), lambda hp, b, r: (b, 0, r * (npair // hps) + hp))] * 3
    return _pcall(
        body, name=name, grid=(npair // hps, nb, d),
        in_specs=[qspec, bspec] + cspecs, out_specs=[qspec, bspec],
        out_shape=[jax.ShapeDtypeStruct(qkv4.shape, BF16),
                   jax.ShapeDtypeStruct((GROUP_HEADS, BAND, 2 * BAND), F32)],
        compiler_params=_params("parallel", "arbitrary", "arbitrary"),
    )(qkv4, bias, *cotangent)


def _head_lanes(first_lane, one_channel):
    c = lax.broadcasted_iota(jnp.int32, (ATT_OUT, LANE), 0)
    lane = lax.broadcasted_iota(jnp.int32, (ATT_OUT, LANE), 1)
    hit = lane == first_lane + c // HEAD_DIM
    if one_channel:
        hit = hit & (c % HEAD_DIM == 0)
    return hit.astype(BF16)


def _exact_dot(v, m01, dims=None):
    parts = _split3(v)
    if dims is None:
        dot = lambda t: jnp.dot(t, m01, preferred_element_type=F32)
    else:
        dot = lambda t: lax.dot_general(t, m01, dims, preferred_element_type=F32)
    return (dot(parts[0]) + dot(parts[1])) + dot(parts[2])


def _combine_fwd(o0, l0, o1, l1, o2, l2, gatt):
    nb, seq, _ = gatt.shape
    tm = 512

    def body(o0_ref, l0_ref, o1_ref, l1_ref, o2_ref, l2_ref, g_ref, oa_ref, oatt_ref, lse_ref):
        spread = _head_lanes(0, False)
        l0v = l0_ref[...]
        l1v = _exact_dot(l1_ref[...], spread, NT)
        l2v = _exact_dot(l2_ref[...], spread, NT)
        m = jnp.maximum(jnp.maximum(l0v, l1v), l2v)
        tot = m + jnp.log(jnp.exp(l0v - m) + jnp.exp(l1v - m) + jnp.exp(l2v - m))
        o = (jnp.exp(l0v - tot) * o0_ref[...] + jnp.exp(l1v - tot) * o1_ref[...]
             + jnp.exp(l2v - tot) * o2_ref[...])
        g = g_ref[...]
        oa_ref[...] = (o * (g * _sigmoid(g))).astype(BF16)
        oatt_ref[...] = o
        lse_ref[...] = tot

    spec = pl.BlockSpec((None, tm, ATT_OUT), lambda b, i: (b, i, 0))
    lspec = pl.BlockSpec((None, tm, LANE), lambda b, i: (b, i, 0))
    return _pcall(
        body, name="attn_combine", grid=(nb, seq // tm),
        in_specs=[spec, spec, spec, lspec, spec, lspec, spec], out_specs=[spec] * 3,
        out_shape=[jax.ShapeDtypeStruct((nb, seq, ATT_OUT), BF16), jax.ShapeDtypeStruct((nb, seq, ATT_OUT), F32),
                   jax.ShapeDtypeStruct((nb, seq, ATT_OUT), F32)],
        compiler_params=_params("parallel", "parallel"),
    )(o0, l0, o1, l1, o2, l2, gatt)


def _combine_bwd(doa, gatt, o_att, lse):
    nb, seq, _ = gatt.shape
    tm = 512

    def body(doa_ref, g_ref, o_ref, l_ref, do_ref, do16_ref, st_ref, dg_ref):
        g = g_ref[...]
        sg = _sigmoid(g)
        do = doa_ref[...] * (g * sg)
        do_ref[...] = do
        do16_ref[...] = do.astype(BF16)
        st_ref[...] = (_exact_dot(do * o_ref[...], _head_lanes(0, False))
                       + _exact_dot(l_ref[...], _head_lanes(STAT_LSE_LANE, True)))
        dg_ref[...] = (doa_ref[...] * o_ref[...] * (sg * (1.0 + g * (1.0 - sg)))).astype(BF16)

    spec = pl.BlockSpec((None, tm, ATT_OUT), lambda b, i: (b, i, 0))
    lspec = pl.BlockSpec((None, tm, LANE), lambda b, i: (b, i, 0))
    return _pcall(
        body, name="attn_combine_bwd", grid=(nb, seq // tm), in_specs=[spec] * 4,
        out_specs=[spec, spec, lspec, spec],
        out_shape=[jax.ShapeDtypeStruct((nb, seq, ATT_OUT), F32), jax.ShapeDtypeStruct((nb, seq, ATT_OUT), BF16),
                   jax.ShapeDtypeStruct((nb, seq, LANE), F32), jax.ShapeDtypeStruct((nb, seq, ATT_OUT), BF16)],
        compiler_params=_params("parallel", "parallel"),
    )(doa, gatt, o_att, lse)


CONV_TM = 1024
CONV_TC = 1024


def _shift_down(cur, halo, k):
    rolled = pltpu.roll(cur, k, 0)
    hro = pltpu.roll(halo, k, 0)
    row = lax.broadcasted_iota(jnp.int32, hro.shape, 0)
    return jnp.concatenate([jnp.where(row < k, hro, rolled[:8]), rolled[8:]], axis=0)


def _shift_up(cur, halo, k):
    n = cur.shape[0]
    rolled = pltpu.roll(cur, n - k, 0)
    hro = pltpu.roll(halo, 8 - k, 0)
    row = lax.broadcasted_iota(jnp.int32, hro.shape, 0)
    return jnp.concatenate([rolled[:n - 8], jnp.where(row >= 8 - k, hro, rolled[n - 8:])], axis=0)


def _conv_pre(cur, halo, w_ref, b_ref):
    acc = cur * w_ref[3:4, :] + b_ref[...]
    for k in range(1, 4):
        acc = acc + _shift_down(cur, halo, k) * w_ref[3 - k:4 - k, :]
    return acc


def _conv_specs(seq):
    nblk = seq // CONV_TM
    cur = pl.BlockSpec((None, CONV_TM, CONV_TC), lambda cb, b, i: (b, i, cb))
    prev = pl.BlockSpec((None, 8, CONV_TC), lambda cb, b, i: (b, jnp.maximum(i * (CONV_TM // 8) - 1, 0), cb))
    nxt = pl.BlockSpec((None, 8, CONV_TC),
                       lambda cb, b, i: (b, jnp.minimum((i + 1) * (CONV_TM // 8), seq // 8 - 1), cb))
    wspec = pl.BlockSpec((4, CONV_TC), lambda cb, b, i: (0, cb))
    bspec = pl.BlockSpec((1, CONV_TC), lambda cb, b, i: (0, cb))
    return nblk, cur, prev, nxt, wspec, bspec


def _conv_fwd(xin, w4, bias, name):
    nb, seq, ch = xin.shape
    _, cur, prev, _, wspec, bspec = _conv_specs(seq)

    def body(x_ref, h_ref, w_ref, b_ref, o_ref):
        halo = jnp.where(pl.program_id(2) > 0, h_ref[...], 0.0)
        pre = _conv_pre(x_ref[...], halo, w_ref, b_ref)
        o_ref[...] = pre * _sigmoid(pre)

    return _pcall(
        body, name=name, grid=(ch // CONV_TC, nb, seq // CONV_TM),
        in_specs=[cur, prev, wspec, bspec], out_specs=cur,
        out_shape=jax.ShapeDtypeStruct(xin.shape, F32),
        compiler_params=_params("parallel", "parallel", "parallel"),
    )(xin, xin, w4, bias)


def _conv_bwd_pre(dact, xin, w4, bias, name):
    nb, seq, ch = xin.shape
    _, cur, prev, _, wspec, bspec = _conv_specs(seq)

    def body(da_ref, x_ref, h_ref, w_ref, b_ref, dp_ref, s_ref):
        b, i = pl.program_id(1), pl.program_id(2)

        @pl.when((b == 0) & (i == 0))
        def _():
            s_ref[...] = jnp.zeros_like(s_ref)

        halo = jnp.where(i > 0, h_ref[...], 0.0)
        x = x_ref[...]
        pre = _conv_pre(x, halo, w_ref, b_ref)
        sg = _sigmoid(pre)
        dpre = da_ref[...] * (sg * (1.0 + pre * (1.0 - sg)))
        dp_ref[...] = dpre
        s_ref[3:4, :] += jnp.sum(dpre * x, 0, keepdims=True)
        for k in range(1, 4):
            s_ref[3 - k:4 - k, :] += jnp.sum(dpre * _shift_down(x, halo, k), 0, keepdims=True)
        s_ref[4:5, :] += jnp.sum(dpre, 0, keepdims=True)

    return _pcall(
        body, name=name, grid=(ch // CONV_TC, nb, seq // CONV_TM),
        in_specs=[cur, cur, prev, wspec, bspec],
        out_specs=[cur, pl.BlockSpec((8, CONV_TC), lambda cb, b, i: (0, cb))],
        out_shape=[jax.ShapeDtypeStruct(xin.shape, F32), jax.ShapeDtypeStruct((8, ch), F32)],
        compiler_params=_params("parallel", "arbitrary", "arbitrary"),
    )(dact, xin, xin, w4, bias)


def _conv_bwd_x(dpre, w4, name):
    nb, seq, ch = dpre.shape
    nblk, cur, _, nxt, wspec, _ = _conv_specs(seq)

    def body(d_ref, n_ref, w_ref, o_ref):
        halo = jnp.where(pl.program_id(2) < nblk - 1, n_ref[...], 0.0)
        cur_v = d_ref[...]
        acc = cur_v * w_ref[3:4, :]
        for j in range(1, 4):
            acc = acc + _shift_up(cur_v, halo, j) * w_ref[3 - j:4 - j, :]
        o_ref[...] = acc.astype(BF16)

    return _pcall(
        body, name=name, grid=(ch // CONV_TC, nb, seq // CONV_TM),
        in_specs=[cur, nxt, wspec], out_specs=cur,
        out_shape=jax.ShapeDtypeStruct(dpre.shape, BF16),
        compiler_params=_params("parallel", "parallel", "parallel"),
    )(dpre, dpre, w4)


def _softplus_sig(dt_raw, dt_bias_row):
    nb, seq, _ = dt_raw.shape
    tm = 512

    def body(r_ref, b_ref, sp_ref, sg_ref):
        v = r_ref[...] + b_ref[...]
        sp_ref[...] = jnp.maximum(v, 0.0) + jnp.log1p(jnp.exp(-jnp.abs(v)))
        sg_ref[...] = _sigmoid(v)

    spec = pl.BlockSpec((None, tm, LANE), lambda b, i: (b, i, 0))
    return _pcall(
        body, name="dt_softplus", grid=(nb, seq // tm),
        in_specs=[spec, pl.BlockSpec((1, LANE), lambda b, i: (0, 0))], out_specs=[spec, spec],
        out_shape=[jax.ShapeDtypeStruct(dt_raw.shape, F32)] * 2,
        compiler_params=_params("parallel", "parallel"),
    )(dt_raw, dt_bias_row)


def _group_lanes(t):
    pads = [(0, 0)] * (t.ndim - 1) + [(0, LANE - GROUP_SSM_HEADS)]
    return jnp.stack([jnp.pad(t[..., GROUP_SSM_HEADS * g:GROUP_SSM_HEADS * (g + 1)], pads) for g in range(SSM_GROUPS)])


def _ungroup_lanes(t):
    return jnp.concatenate([t[g][..., :GROUP_SSM_HEADS] for g in range(SSM_GROUPS)], axis=-1)


def _decays(dt, al_ref):
    row = lax.broadcasted_iota(jnp.int32, (CHUNK, CHUNK), 0)
    col = lax.broadcasted_iota(jnp.int32, (CHUNK, CHUNK), 1)
    tril = (row >= col).astype(BF16)
    triu = (row <= col).astype(BF16)
    arow = -jnp.exp(al_ref[...])
    hi, mid, lo = _split3(dt * arow)
    down = lambda t: jnp.dot(tril, t, preferred_element_type=F32)
    across = lambda t: lax.dot_general(t, triu, TN, preferred_element_type=F32)
    acs = (down(hi) + down(mid)) + down(lo)
    acs_t = (across(hi) + across(mid)) + across(lo)
    return arow, acs, acs_t, row >= col, triu


STEP_CHUNKS = 8


def _ssd_specs(nb, seq):
    nc = seq // CHUNK
    hw = GROUP_SSM_HEADS * HEAD_DIM
    rows, steps = STEP_CHUNKS * CHUNK, nc // STEP_CHUNKS

    def mk(rev):
        cidx = (lambda c: steps - 1 - c) if rev else (lambda c: c)
        wide = pl.BlockSpec((None, rows, hw), lambda g, b, c: (b, cidx(c), g))
        xbc = pl.BlockSpec((None, rows, XBC_GROUP), lambda g, b, c: (b, cidx(c), g))
        lanes = pl.BlockSpec((None, None, rows, LANE), lambda g, b, c: (g, b, cidx(c), 0))
        prev = pl.BlockSpec((None, STEP_CHUNKS, None, D_STATE, hw), lambda g, b, c: (b, cidx(c), g, 0, 0))
        return wide, xbc, lanes, prev

    grow = pl.BlockSpec((None, 1, LANE), lambda g, b, c: (g, 0, 0))
    nwspec = pl.BlockSpec((1, hw), lambda g, b, c: (0, g))
    return nc, steps, hw, mk, grow, nwspec


def _head_expand():
    hw = GROUP_SSM_HEADS * HEAD_DIM
    r = lax.broadcasted_iota(jnp.int32, (LANE, hw), 0)
    c = lax.broadcasted_iota(jnp.int32, (LANE, hw), 1)
    return ((c // HEAD_DIM) == r).astype(BF16)


def _split3(v):
    hi = v.astype(BF16)
    rest = v - hi.astype(F32)
    mid = rest.astype(BF16)
    return hi, mid, (rest - mid.astype(F32)).astype(BF16)


def _to_channels(v, e):
    hi, mid, lo = _split3(v)
    dot = lambda t: jnp.dot(t, e, preferred_element_type=F32)
    return (dot(hi) + dot(mid)) + dot(lo)


def _to_heads(w, e):
    hi, mid, lo = _split3(w)
    dot = lambda t: lax.dot_general(t, e, (((1,), (1,)), ((), ())), preferred_element_type=F32)
    return (dot(hi) + dot(mid)) + dot(lo)


def _row8(v):
    return jnp.broadcast_to(v, (8, v.shape[1]))


def _ssd_chunk_setup(dt, al_ref, ds_ref):
    arow, acs, acs_t, causal, triu = _decays(dt, al_ref)
    e = _head_expand()
    dtx = _to_channels(dt, e)
    acsx = _to_channels(acs, e)
    lastx = acsx[CHUNK - 1:CHUNK, :]
    dskx = _to_channels(_row8(ds_ref[...]), e)[0:1, :]
    return arow, acs, acs_t, causal, triu, e, dtx, acsx, lastx, dskx


def _ssd_fwd(xbc, dtg, z, alog_g, dskip_g, normw):
    nb, seq, _ = xbc.shape
    nc, steps, hw, mk, grow, nwspec = _ssd_specs(nb, seq)
    wide, xbc_spec, lanes, prev = mk(False)
    tn = (((0,), (0,)), ((), ()))

    def body(xbc_ref, dt_ref, z_ref, al_ref, ds_ref, nw_ref, ys_ref, y_ref, sp_ref, st_ref):
        @pl.when(pl.program_id(2) == 0)
        def _():
            st_ref[...] = jnp.zeros_like(st_ref)

        for ci in range(STEP_CHUNKS):
            chunk(ci, xbc_ref, dt_ref, z_ref, al_ref, ds_ref, nw_ref, ys_ref, y_ref, sp_ref, st_ref)

    def chunk(ci, xbc_ref, dt_ref, z_ref, al_ref, ds_ref, nw_ref, ys_ref, y_ref, sp_ref, st_ref):
        rows = slice(ci * CHUNK, (ci + 1) * CHUNK)
        dt = dt_ref[rows, :]
        _, acs, acs_t, causal, _, _, dtx, acsx, lastx, dskx = _ssd_chunk_setup(dt, al_ref, ds_ref)
        bmat = xbc_ref[rows, GROUP_CH:GROUP_CH + D_STATE].astype(BF16)
        cmat = xbc_ref[rows, GROUP_CH + D_STATE:].astype(BF16)
        cb = lax.dot_general(cmat, bmat, (((1,), (1,)), ((), ())), preferred_element_type=F32)
        x = xbc_ref[rows, :GROUP_CH]
        xdt = x * dtx
        xdt16 = xdt.astype(BF16)
        first_head = lax.broadcasted_iota(jnp.int32, (CHUNK, LANE), 1) < HEAD_DIM
        pairs = []
        for hp in range(GROUP_SSM_HEADS // 2):
            xp = xdt16[:, hp * LANE:(hp + 1) * LANE]
            two = []
            for j in (2 * hp, 2 * hp + 1):
                lmat = jnp.exp(jnp.where(causal, acs[:, j:j + 1] - acs_t[j:j + 1, :], -jnp.inf))
                two.append(jnp.dot((cb * lmat).astype(BF16), xp, preferred_element_type=F32))
            pairs.append(jnp.where(first_head, two[0], two[1]))
        yd = jnp.concatenate(pairs, axis=1)
        s_prev = st_ref[...]
        s16 = s_prev.astype(BF16)
        sp_ref[ci] = s16
        yo = jnp.dot(cmat, s16, preferred_element_type=F32) * jnp.exp(acsx)
        sts = lax.dot_general(bmat, (xdt * jnp.exp(lastx - acsx)).astype(BF16), tn, preferred_element_type=F32)
        st_ref[...] = s_prev * jnp.exp(lastx) + sts
        y = yd + yo + dskx * x
        zz = z_ref[rows, :]
        u = y * (zz * _sigmoid(zz))
        rn = lax.rsqrt(jnp.mean(u * u, -1, keepdims=True) + RMS_EPS)
        ys_ref[rows, :] = (u * rn * nw_ref[...]).astype(BF16)
        y_ref[rows, :] = y

    return _pcall(
        body, name="ssd_fwd", grid=(SSM_GROUPS, nb, steps),
        in_specs=[xbc_spec, lanes, wide, grow, grow, nwspec],
        out_specs=[wide, wide, prev],
        out_shape=[jax.ShapeDtypeStruct((nb, seq, D_INNER), BF16), jax.ShapeDtypeStruct((nb, seq, D_INNER), F32),
                   jax.ShapeDtypeStruct((nb, nc, SSM_GROUPS, D_STATE, hw), BF16)],
        scratch_shapes=[pltpu.VMEM((D_STATE, hw), F32)],
        compiler_params=_params("parallel", "parallel", "arbitrary"),
    )(xbc, dtg, z, alog_g, dskip_g, normw)


def _ssd_bwd(xbc, dtg, sgg, z, y, dys, sprev, alog_g, dskip_g, normw):
    nb, seq, _ = xbc.shape
    nc, steps, hw, mk, grow, nwspec = _ssd_specs(nb, seq)
    wide, xbc_spec, lanes, prev = mk(True)
    nt = (((1,), (1,)), ((), ()))
    tn = (((0,), (0,)), ((), ()))

    def body(xbc_ref, dt_ref, sg_ref, z_ref, y_ref, dys_ref, sp_ref, al_ref, ds_ref, nw_ref,
             dxbc_ref, ddt_ref, dz_ref, small_ref, dnw_ref, g_ref):
        b, c = pl.program_id(1), pl.program_id(2)

        @pl.when((b == 0) & (c == 0))
        def _():
            small_ref[...] = jnp.zeros_like(small_ref)
            dnw_ref[...] = jnp.zeros_like(dnw_ref)

        @pl.when(c == 0)
        def _():
            g_ref[...] = jnp.zeros_like(g_ref)

        for ci in reversed(range(STEP_CHUNKS)):
            chunk(ci, xbc_ref, dt_ref, sg_ref, z_ref, y_ref, dys_ref, sp_ref, al_ref, ds_ref, nw_ref,
                  dxbc_ref, ddt_ref, dz_ref, small_ref, dnw_ref, g_ref)

    def chunk(ci, xbc_ref, dt_ref, sg_ref, z_ref, y_ref, dys_ref, sp_ref, al_ref, ds_ref, nw_ref,
              dxbc_ref, ddt_ref, dz_ref, small_ref, dnw_ref, g_ref):
        rows = slice(ci * CHUNK, (ci + 1) * CHUNK)
        yv, zz, dys_v, nw = y_ref[rows, :], z_ref[rows, :], dys_ref[rows, :], nw_ref[...]
        sz = _sigmoid(zz)
        silu = zz * sz
        u = yv * silu
        rn = lax.rsqrt(jnp.mean(u * u, -1, keepdims=True) + RMS_EPS)
        gn = dys_v * nw
        du = rn * gn - u * (rn * rn * rn) * jnp.mean(u * gn, -1, keepdims=True)
        dnw_ref[...] += jnp.sum(dys_v * u * rn, 0, keepdims=True)
        dy = du * silu
        dz_ref[rows, :] = du * yv * (sz * (1.0 + zz * (1.0 - sz)))

        dt = dt_ref[rows, :]
        arow, acs, acs_t, causal, triu, e, dtx, acsx, lastx, dskx = _ssd_chunk_setup(dt, al_ref, ds_ref)
        dfsx = jnp.exp(acsx)
        dtex = jnp.exp(lastx - acsx)
        bmat = xbc_ref[rows, GROUP_CH:GROUP_CH + D_STATE].astype(BF16)
        cmat = xbc_ref[rows, GROUP_CH + D_STATE:].astype(BF16)
        cb = lax.dot_general(cmat, bmat, nt, preferred_element_type=F32)
        x = xbc_ref[rows, :GROUP_CH]
        xdt = x * dtx
        xdt16 = xdt.astype(BF16)
        xdte = xdt * dtex
        dy16 = dy.astype(BF16)
        dyd = dy * dfsx
        dyd16 = dyd.astype(BF16)
        s16 = sp_ref[ci]
        g = g_ref[...]
        g16 = g.astype(BF16)
        cs = jnp.dot(cmat, s16, preferred_element_type=F32)
        dc_off = lax.dot_general(dyd16, s16, nt, preferred_element_type=F32)
        g_here = lax.dot_general(cmat, dyd16, tn, preferred_element_type=F32)
        bg = jnp.dot(bmat, g16, preferred_element_type=F32)
        db_st = lax.dot_general(xdte.astype(BF16), g16, nt, preferred_element_type=F32)
        ddte_w = bg * xdte
        dcd = _to_heads(_row8(jnp.sum(g * s16.astype(F32), 0, keepdims=True)), e)[0:1, :]
        lane = lax.broadcasted_iota(jnp.int32, (CHUNK, LANE), 1)
        first_head = lane < HEAD_DIM
        sub = lax.broadcasted_iota(jnp.int32, (CHUNK, LANE), 0)
        dacs = jnp.zeros((CHUNK, LANE), F32)
        colsums = jnp.zeros((CHUNK, LANE), F32)
        dcb = jnp.zeros((CHUNK, CHUNK), F32)
        pairs = []
        for hp in range(GROUP_SSM_HEADS // 2):
            xp = xdt16[:, hp * LANE:(hp + 1) * LANE]
            dyp = dy16[:, hp * LANE:(hp + 1) * LANE]
            two = []
            for idx, j in enumerate((2 * hp, 2 * hp + 1)):
                lmat = jnp.exp(jnp.where(causal, acs[:, j:j + 1] - acs_t[j:j + 1, :], -jnp.inf))
                mf = cb * lmat
                dy_h = jnp.where(first_head if idx == 0 else jnp.logical_not(first_head), dyp, jnp.zeros_like(dyp))
                dm = lax.dot_general(dy_h, xp, nt, preferred_element_type=F32)
                two.append(lax.dot_general(mf.astype(BF16), dyp, tn, preferred_element_type=F32))
                wmat = dm * mf
                dcb = dcb + dm * lmat
                dacs = jnp.where(lane == j, jnp.sum(wmat, -1, keepdims=True), dacs)
                colsums = jnp.where(sub == j, jnp.sum(wmat, 0, keepdims=True), colsums)
            pairs.append(jnp.where(first_head, two[0], two[1]))
        dxdt = bg * dtex + jnp.concatenate(pairs, axis=1)
        dacs = dacs - colsums.T + _to_heads(dyd * cs - ddte_w, e)
        cd_row = jnp.exp(acs[CHUNK - 1:CHUNK, :])
        tail = _to_heads(_row8(jnp.sum(ddte_w, 0, keepdims=True)), e)[0:1, :] + dcd * cd_row
        dacs = dacs + jnp.where(sub == CHUNK - 1, tail, 0.0)
        d_hi, d_mid, d_lo = _split3(dacs)
        up = lambda t: jnp.dot(triu, t, preferred_element_type=F32)
        da = (up(d_hi) + up(d_mid)) + up(d_lo)
        ddt_raw = (da * arow + _to_heads(dxdt * x, e)) * sg_ref[rows, :]
        ddt_ref[rows, :] = ddt_raw
        small_ref[0:1, :] += jnp.sum(da * dt, 0, keepdims=True) * arow
        small_ref[1:2, :] += _to_heads(_row8(jnp.sum(dy * x, 0, keepdims=True)), e)[0:1, :]
        small_ref[2:3, :] += jnp.sum(ddt_raw, 0, keepdims=True)
        dcb16 = dcb.astype(BF16)
        dxbc_ref[rows, GROUP_CH + D_STATE:] = dc_off + jnp.dot(dcb16, bmat, preferred_element_type=F32)
        dxbc_ref[rows, GROUP_CH:GROUP_CH + D_STATE] = db_st + lax.dot_general(dcb16, cmat, tn,
                                                                               preferred_element_type=F32)
        dxbc_ref[rows, :GROUP_CH] = dxdt * dtx + dskx * dy
        g_ref[...] = g * jnp.exp(lastx) + g_here

    return _pcall(
        body, name="ssd_bwd", grid=(SSM_GROUPS, nb, steps),
        in_specs=[xbc_spec, lanes, lanes, wide, wide, wide, prev, grow, grow, nwspec],
        out_specs=[xbc_spec, lanes, wide,
                   pl.BlockSpec((None, 8, LANE), lambda g, b, c: (g, 0, 0)), nwspec],
        out_shape=[jax.ShapeDtypeStruct((nb, seq, CONV_DIM), F32),
                   jax.ShapeDtypeStruct((SSM_GROUPS, nb, seq, LANE), F32),
                   jax.ShapeDtypeStruct((nb, seq, D_INNER), F32),
                   jax.ShapeDtypeStruct((SSM_GROUPS, 8, LANE), F32),
                   jax.ShapeDtypeStruct((1, D_INNER), F32)],
        scratch_shapes=[pltpu.VMEM((D_STATE, hw), F32)],
        compiler_params=_params("parallel", "arbitrary", "arbitrary"),
    )(xbc, dtg, sgg, z, y, dys, sprev, alog_g, dskip_g, normw)


EW_TM = 256


def _merge_fwd(y_a, y_b, gm, bgate):
    nb, seq, _ = y_a.shape

    def body(a_ref, b_ref, ga_ref, gb_ref, bg_ref, o_ref):
        sa = _sigmoid(ga_ref[...] + bg_ref[0:1, :])
        sb = _sigmoid(gb_ref[...] + bg_ref[1:2, :])
        o_ref[...] = (sa * a_ref[...] + sb * b_ref[...]).astype(BF16)

    spec = pl.BlockSpec((None, EW_TM, D_MODEL), lambda b, i: (b, i, 0))
    spec1 = pl.BlockSpec((None, EW_TM, D_MODEL), lambda b, i: (b, i, 1))
    return _pcall(
        body, name="merge_fwd", grid=(nb, seq // EW_TM),
        in_specs=[spec, spec, spec, spec1, pl.BlockSpec((8, D_MODEL), lambda b, i: (0, 0))], out_specs=spec,
        out_shape=jax.ShapeDtypeStruct((nb, seq, D_MODEL), BF16),
        compiler_params=_params("parallel", "parallel"),
    )(y_a, y_b, gm, gm, bgate)


def _merge_bwd(dmerged, y_a, y_b, gm, bgate):
    nb, seq, _ = y_a.shape

    def body(dm_ref, a_ref, b_ref, ga_ref, gb_ref, bg_ref, dya_ref, dyb_ref, dg_ref, s_ref):
        @pl.when((pl.program_id(0) == 0) & (pl.program_id(1) == 0))
        def _():
            s_ref[...] = jnp.zeros_like(s_ref)

        dm = dm_ref[...]
        sa = _sigmoid(ga_ref[...] + bg_ref[0:1, :])
        sb = _sigmoid(gb_ref[...] + bg_ref[1:2, :])
        dya_ref[...] = (dm * sa).astype(BF16)
        dyb_ref[...] = (dm * sb).astype(BF16)
        dga = dm * a_ref[...] * (sa * (1.0 - sa))
        dgb = dm * b_ref[...] * (sb * (1.0 - sb))
        dg_ref[:, :D_MODEL] = dga.astype(BF16)
        dg_ref[:, D_MODEL:] = dgb.astype(BF16)
        s_ref[0:1, :] += jnp.sum(dga, 0, keepdims=True)
        s_ref[1:2, :] += jnp.sum(dgb, 0, keepdims=True)

    spec = pl.BlockSpec((None, EW_TM, D_MODEL), lambda b, i: (b, i, 0))
    spec1 = pl.BlockSpec((None, EW_TM, D_MODEL), lambda b, i: (b, i, 1))
    small = pl.BlockSpec((8, D_MODEL), lambda b, i: (0, 0))
    return _pcall(
        body, name="merge_bwd", grid=(nb, seq // EW_TM),
        in_specs=[spec, spec, spec, spec, spec1, small],
        out_specs=[spec, spec, pl.BlockSpec((None, EW_TM, 2 * D_MODEL), lambda b, i: (b, i, 0)), small],
        out_shape=[jax.ShapeDtypeStruct((nb, seq, D_MODEL), BF16), jax.ShapeDtypeStruct((nb, seq, D_MODEL), BF16),
                   jax.ShapeDtypeStruct((nb, seq, 2 * D_MODEL), BF16), jax.ShapeDtypeStruct((8, D_MODEL), F32)],
        compiler_params=_params("arbitrary", "arbitrary"),
    )(dmerged, y_a, y_b, gm, gm, bgate)


def _ln_loss(x, mix, gp, pw, target, bgate, ln_g, ln_b):
    nb, seq, _ = x.shape

    def body(x_ref, mix_ref, gp_ref, pw_ref, t_ref, bg_ref, g_ref, b_ref, dx_ref, dp_ref, dpw_ref, dgp_ref, s_ref):
        @pl.when((pl.program_id(0) == 0) & (pl.program_id(1) == 0))
        def _():
            s_ref[...] = jnp.zeros_like(s_ref)

        sp = _sigmoid(gp_ref[...] + bg_ref[2:3, :])
        pw = pw_ref[...]
        pre = ALPHA * x_ref[...] + mix_ref[...] + sp * pw
        mu = jnp.mean(pre, -1, keepdims=True)
        cen = pre - mu
        rstd = lax.rsqrt(jnp.mean(cen * cen, -1, keepdims=True) + LN_EPS)
        xhat = cen * rstd
        err = xhat * g_ref[...] + b_ref[...] - t_ref[...]
        dy = err * (1.0 / D_MODEL)
        dxh = dy * g_ref[...]
        dpre = rstd * (dxh - jnp.mean(dxh, -1, keepdims=True) - xhat * jnp.mean(dxh * xhat, -1, keepdims=True))
        dx_ref[...] = ALPHA * dpre
        dp_ref[...] = dpre.astype(BF16)
        dpw_ref[...] = (dpre * sp).astype(BF16)
        dgp = dpre * pw * (sp * (1.0 - sp))
        dgp_ref[...] = dgp.astype(BF16)
        s_ref[0:1, :] += jnp.sum(dy * xhat, 0, keepdims=True)
        s_ref[1:2, :] += jnp.sum(dy, 0, keepdims=True)
        s_ref[2:3, :] += jnp.sum(dgp, 0, keepdims=True)
        s_ref[3:4, :] += jnp.sum(err * err, 0, keepdims=True)

    spec = pl.BlockSpec((None, EW_TM, D_MODEL), lambda b, i: (b, i, 0))
    small = pl.BlockSpec((8, D_MODEL), lambda b, i: (0, 0))
    row = pl.BlockSpec((1, D_MODEL), lambda b, i: (0, 0))
    return _pcall(
        body, name="ln_loss", grid=(nb, seq // EW_TM),
        in_specs=[spec] * 5 + [small, row, row], out_specs=[spec] * 4 + [small],
        out_shape=[jax.ShapeDtypeStruct((nb, seq, D_MODEL), F32)] + [jax.ShapeDtypeStruct((nb, seq, D_MODEL), BF16)] * 3
        + [jax.ShapeDtypeStruct((8, D_MODEL), F32)],
        compiler_params=_params("arbitrary", "arbitrary"),
    )(x, mix, gp, pw, target, bgate, ln_g, ln_b)


def _adamw(w, g, m, v, name):
    rows, cols = w.shape
    tr = _row_tile(rows, cols, 8, 5 << 19)
    c1 = 1.0 - ADAM_B1 ** ADAM_STEP
    c2 = 1.0 - ADAM_B2 ** ADAM_STEP

    def body(w_ref, g_ref, m_ref, v_ref, d_ref, nm_ref, nv_ref):
        gv = g_ref[...]
        nm = ADAM_B1 * m_ref[...] + (1.0 - ADAM_B1) * gv
        nv = ADAM_B2 * v_ref[...] + (1.0 - ADAM_B2) * (gv * gv)
        d_ref[...] = -ADAM_LR * ((nm / c1) / (jnp.sqrt(nv / c2) + ADAM_EPS) + ADAM_WD * w_ref[...])
        nm_ref[...] = nm
        nv_ref[...] = nv

    spec = pl.BlockSpec((tr, cols), lambda i: (i, 0))
    return _pcall(
        body, name=name, grid=(rows // tr,), in_specs=[spec] * 4, out_specs=[spec] * 3,
        out_shape=[jax.ShapeDtypeStruct(w.shape, F32)] * 3, compiler_params=_params("parallel"),
    )(w, g, m, v)


def _sum_rows(parts, out_dtype, name):
    rows, cols = parts[0].shape
    tr = rows
    for cand in range(16, rows, 16):
        if rows % cand == 0 and cand * cols * 4 <= (1 << 20):
            tr = cand
    n = len(parts)

    def body(*refs):
        acc = refs[0][...].astype(F32)
        for r in refs[1:n]:
            acc = acc + r[...].astype(F32)
        refs[n][...] = acc.astype(out_dtype)

    spec = pl.BlockSpec((tr, cols), lambda i: (i, 0))
    return _pcall(
        body, name=name, grid=(rows // tr,), in_specs=[spec] * n, out_specs=spec,
        out_shape=jax.ShapeDtypeStruct((rows, cols), out_dtype), compiler_params=_params("parallel"),
    )(*parts)


def _place():
    return lax.axis_index("x"), lax.axis_index("y"), lax.axis_index("c")


def _other_chips(x, y):
    return [(1 - x, y), (x, 1 - y), (1 - x, 1 - y)]


def _remote(src, dst, send_sem, recv_sem, to):
    return pltpu.make_async_remote_copy(src_ref=src, dst_ref=dst, send_sem=send_sem, recv_sem=recv_sem,
                                        device_id=to, device_id_type=MESH)


ANY = pl.BlockSpec(memory_space=pl.ANY)
DMA_CHUNK_BYTES = 512 * 1024


def _row_chunks(rows, row_bytes):
    per = max(16, DMA_CHUNK_BYTES // row_bytes // 16 * 16)
    return [(s, min(per, rows - s)) for s in range(0, rows, per)]


def _row_tile(rows, cols, align, limit=1 << 21):
    best = None
    for cand in range(align, rows + 1, align):
        if rows % cand == 0 and cand * cols * 4 <= limit:
            best = cand
    return best or rows


def _allgather_pieces(pieces):
    n = len(pieces)
    halves = [_row_chunks(p.shape[0] // 2, p.shape[1] * p.dtype.itemsize) for p in pieces]
    entries = [(a, q, s, m, j) for a in range(n) for q, (s, m) in enumerate(halves[a]) for j in range(3)]
    slot = {(a, q, j): k for k, (a, q, _, _, j) in enumerate(entries)}
    n_ici = len(entries)

    def body(*refs):
        ins, outs = refs[:n], refs[n:2 * n]
        send_sems, recv_sems = refs[2 * n:]
        x, y, c = _place()
        me = 2 * x + y
        sibling = (x, y, 1 - c)
        chips = _other_chips(x, y)

        def landed(a, s, m, j, core):
            half = ins[a].shape[0] // 2
            return outs[a].at[2 * chips[j][0] + chips[j][1], pl.ds(core * half + s, m)]

        sent = []
        for k, (a, q, s, m, j) in enumerate(entries):
            if j < 2:
                half = ins[a].shape[0] // 2
                cp = _remote(ins[a].at[pl.ds(c * half + s, m)], outs[a].at[me, pl.ds(c * half + s, m)],
                             send_sems.at[k], recv_sems.at[k], (*chips[j], c))
                cp.start()
                sent.append(cp)

        def pass_to_sibling(k, blk):
            fw = _remote(blk, blk, send_sems.at[n_ici + k], recv_sems.at[n_ici + k], sibling)
            fw.start()
            sent.append(fw)

        for k, (a, q, s, m, j) in enumerate(entries):
            if j < 2:
                blk = landed(a, s, m, j, c)
                _remote(blk, blk, send_sems.at[k], recv_sems.at[k], (*chips[j], c)).wait_recv()
                first = q < (len(halves[a]) + 1) // 2
                if (j == 0) == first:
                    on = slot[(a, q, 2)]
                    rl = _remote(blk, blk, send_sems.at[on], recv_sems.at[on], (*chips[1 - j], c))
                    rl.start()
                    sent.append(rl)
                pass_to_sibling(k, blk)
        for k, (a, q, s, m, j) in enumerate(entries):
            if j == 2:
                blk = landed(a, s, m, j, c)
                _remote(blk, blk, send_sems.at[k], recv_sems.at[k], (*chips[j], c)).wait_recv()
                pass_to_sibling(k, blk)
        for k, (a, q, s, m, j) in enumerate(entries):
            blk = landed(a, s, m, j, 1 - c)
            _remote(blk, blk, send_sems.at[n_ici + k], recv_sems.at[n_ici + k], sibling).wait_recv()
        for cp in sent:
            cp.wait_send()

    gathered = _pcall(
        body, name="allgather_weights", in_specs=[ANY] * n, out_specs=[ANY] * n,
        out_shape=[jax.ShapeDtypeStruct((4,) + p.shape, p.dtype) for p in pieces],
        scratch_shapes=[pltpu.SemaphoreType.DMA((2 * n_ici,)), pltpu.SemaphoreType.DMA((2 * n_ici,))],
        compiler_params=pltpu.CompilerParams(has_side_effects=True),
    )(*pieces)
    x, y, _ = _place()
    return [lax.dynamic_update_slice(g, p[None], (2 * x + y, 0, 0)) for g, p in zip(gathered, pieces)]


def _sibling_exchange(grads):
    n = len(grads)
    chunks = [_row_chunks(g.shape[1] // 2, g.shape[2] * g.dtype.itemsize) for g in grads]
    n_sem = 4 * sum(len(ch) for ch in chunks)

    def body(*refs):
        ins, gots = refs[:n], refs[n:2 * n]
        send_sems, recv_sems = refs[2 * n:]
        x, y, c = _place()
        sibling = (x, y, 1 - c)
        work = []
        for a in range(n):
            half = ins[a].shape[1] // 2
            for piece in range(4):
                for s, m in chunks[a]:
                    k = len(work)
                    cp = _remote(ins[a].at[piece, pl.ds((1 - c) * half + s, m)], gots[a].at[piece, pl.ds(s, m)],
                                 send_sems.at[k], recv_sems.at[k], sibling)
                    cp.start()
                    work.append(cp)
        for cp in work:
            cp.wait()

    return _pcall(
        body, name="grad_sibling_exchange", in_specs=[ANY] * n, out_specs=[ANY] * n,
        out_shape=[jax.ShapeDtypeStruct((4, g.shape[1] // 2, g.shape[2]), g.dtype) for g in grads],
        scratch_shapes=[pltpu.SemaphoreType.DMA((n_sem,)), pltpu.SemaphoreType.DMA((n_sem,))],
        compiler_params=pltpu.CompilerParams(has_side_effects=True),
    )(*grads)


def _sibling_gather(fulls):
    n = len(fulls)
    chunks = [_row_chunks(f.shape[0] // 2, f.shape[1] * f.dtype.itemsize) for f in fulls]
    n_sem = sum(len(ch) for ch in chunks)

    def body(*refs):
        outs = refs[n:2 * n]
        send_sems, recv_sems = refs[2 * n:]
        x, y, c = _place()
        sibling = (x, y, 1 - c)
        work = []
        for a in range(n):
            h = outs[a].shape[0] // 2
            for s, m in chunks[a]:
                k = len(work)
                mine = outs[a].at[pl.ds(c * h + s, m)]
                cp = _remote(mine, mine, send_sems.at[k], recv_sems.at[k], sibling)
                cp.start()
                work.append((a, s, m, cp))
        for k, (a, s, m, cp) in enumerate(work):
            h = outs[a].shape[0] // 2
            cp.wait_send()
            theirs = outs[a].at[pl.ds((1 - c) * h + s, m)]
            _remote(theirs, theirs, send_sems.at[k], recv_sems.at[k], sibling).wait_recv()

    return _pcall(
        body, name="grad_sibling_gather", in_specs=[ANY] * n, out_specs=[ANY] * n,
        out_shape=[jax.ShapeDtypeStruct(f.shape, f.dtype) for f in fulls],
        input_output_aliases={a: a for a in range(n)},
        scratch_shapes=[pltpu.SemaphoreType.DMA((n_sem,)), pltpu.SemaphoreType.DMA((n_sem,))],
        compiler_params=pltpu.CompilerParams(has_side_effects=True),
    )(*fulls)


def _pair_sum(grad, got, place, name):
    _, rows, cols = grad.shape
    half = rows // 2
    tr = _row_tile(half, cols, 16)

    def body(p_ref, a_ref, b_ref, o_ref):
        o_ref[...] = (a_ref[...].astype(F32) + b_ref[...].astype(F32)).astype(BF16)

    return _pcall(
        body, name=name,
        grid_spec=pltpu.PrefetchScalarGridSpec(
            num_scalar_prefetch=1, grid=(4, half // tr),
            in_specs=[pl.BlockSpec((None, tr, cols), lambda k, i, p: (k, p[1] * (half // tr) + i, 0)),
                      pl.BlockSpec((None, tr, cols), lambda k, i, p: (k, i, 0))],
            out_specs=pl.BlockSpec((None, tr, cols), lambda k, i, p: (k, i, 0))),
        out_shape=jax.ShapeDtypeStruct((4, half, cols), BF16),
        compiler_params=_params("parallel", "parallel"),
    )(place, grad, got)


def _chip_sum(sums, got, place, name):
    _, h, cols = sums.shape
    tr = _row_tile(h, cols, 16)

    def body(p_ref, own_ref, g0, g1, g2, o_ref):
        o_ref[...] = ((own_ref[...].astype(F32) + g0[...].astype(F32)) + g1[...].astype(F32)) + g2[...].astype(F32)

    gspec = lambda j: pl.BlockSpec((None, tr, cols), lambda i, p: (j, i, 0))
    return _pcall(
        body, name=name,
        grid_spec=pltpu.PrefetchScalarGridSpec(
            num_scalar_prefetch=1, grid=(h // tr,),
            in_specs=[pl.BlockSpec((None, tr, cols), lambda i, p: (p[0], i, 0)), gspec(0), gspec(1), gspec(2)],
            out_specs=pl.BlockSpec((tr, cols), lambda i, p: (p[1] * (h // tr) + i, 0))),
        out_shape=jax.ShapeDtypeStruct((2 * h, cols), F32),
        compiler_params=_params("parallel"),
    )(place, sums, got, got, got)


def _allgather8(buf, name):
    rows = buf.shape[0]

    def body(in_ref, out_ref, send_sems, recv_sems):
        x, y, c = _place()
        me = 4 * x + 2 * y + c
        out_ref[me] = in_ref[...]
        work = []
        for rel in range(1, 8):
            fx, fy, fc = (rel >> 2) & 1, (rel >> 1) & 1, rel & 1
            to = (x ^ fx, y ^ fy, c ^ fc)
            cp = _remote(in_ref, out_ref.at[me], send_sems.at[rel - 1], recv_sems.at[rel - 1], to)
            cp.start()
            work.append((cp, 4 * to[0] + 2 * to[1] + to[2]))
        for rel, (cp, frm) in enumerate(work):
            cp.wait_send()
            blk = out_ref.at[frm]
            _remote(blk, blk, send_sems.at[rel], recv_sems.at[rel], (x, y, c)).wait_recv()

    return _pcall(
        body, name=name, in_specs=[pl.BlockSpec(memory_space=pltpu.VMEM)],
        out_specs=pl.BlockSpec(memory_space=pltpu.VMEM),
        out_shape=jax.ShapeDtypeStruct((8, rows, LANE), F32),
        scratch_shapes=[pltpu.SemaphoreType.DMA((7,)), pltpu.SemaphoreType.DMA((7,))],
        compiler_params=pltpu.CompilerParams(has_side_effects=True),
    )(buf)


def _pack_rows(arrs):
    parts = []
    for a in arrs:
        f = a.reshape(-1).astype(F32)
        parts.append(jnp.pad(f, (0, (-f.shape[0]) % LANE)))
    flat = jnp.concatenate(parts)
    rows = -(-flat.shape[0] // LANE)
    rows8 = -(-rows // 8) * 8
    return jnp.pad(flat, (0, rows8 * LANE - flat.shape[0])).reshape(rows8, LANE)


def _unpack_rows(buf, shapes):
    flat = buf.reshape(-1)
    outs, off = [], 0
    for s in shapes:
        n = int(np.prod(s))
        outs.append(flat[off:off + n].reshape(s))
        off += -(-n // LANE) * LANE
    return outs


def _local_grads(x, p, target, wseg, w_br16, w_out16, w_ple16, b_gate, conv_w, conv_b, dt_bias, a_log, d_skip,
                 ssm_norm_w, ln_g, ln_b, rel_bias, finish_dx):
    nb, seq, _ = x.shape
    bmaps = jnp.asarray(_bucket_maps())
    bias = _bias_tables(rel_bias, bmaps)
    bgate8 = jnp.pad(b_gate, ((0, 5), (0, 0)))
    dils = [d for _, d in PATTERNS]

    x16 = x.astype(BF16)
    p16 = p.astype(BF16)
    x16p = [_permute(x16, d) for d in dils]
    qkv = [_proj(x16p[g], [wseg["qkv%d" % g]], BF16, "proj_qkv%d" % g, True)[0].reshape(
        nb, dils[g], seq // dils[g], -1) for g in range(3)]
    nat = {}
    for gi, (group, tm) in enumerate(NAT_GROUPS):
        outs = _proj(x16, [wseg[s] for s in group], F32, "proj_nat%d" % gi, True, tm)
        nat.update(zip(group, outs))
    att = [_attn_fwd(qkv[g], bias[g * GROUP_HEADS:(g + 1) * GROUP_HEADS], dils[g], "attn_fwd%d" % g) for g in range(3)]
    natural = lambda t, g: _unpermute(t.reshape(nb, seq, t.shape[-1]), dils[g])
    oa, o_att, lse = _combine_fwd(att[0][0], att[0][1], natural(att[1][0], 1), natural(att[1][1], 1),
                                  natural(att[2][0], 2), natural(att[2][1], 2), nat["gatt"])

    conv_wg, conv_bg = _xbc_group_order(conv_w), _xbc_group_order(conv_b)
    act = _conv_fwd(nat["xbc"], conv_wg, conv_bg, "conv_fwd")
    dt_sp, dt_sg = _softplus_sig(nat["dt"], jnp.pad(dt_bias, ((0, 0), (0, LANE - SSM_HEADS))))
    dtg, sgg = _group_lanes(dt_sp), _group_lanes(dt_sg)
    alog_g, dskip_g = _group_lanes(a_log), _group_lanes(d_skip)
    y_ssm, y_all, sprev = _ssd_fwd(act, dtg, nat["z"], alog_g, dskip_g, ssm_norm_w)

    w_bra, w_brb = w_br16[:ATT_OUT], w_br16[ATT_OUT:]
    y_a, = _proj(oa, [w_bra], F32, "proj_ya")
    y_b, = _proj(y_ssm, [w_brb], F32, "proj_yb")
    merged = _merge_fwd(y_a, y_b, nat["gm"], bgate8)
    mix, = _proj(merged, [w_out16], F32, "proj_mix")
    pw, = _proj(p16, [w_ple16], F32, "proj_ple")

    dx, dpre16, dpw16, dgp16, ln_sums = _ln_loss(x, mix, nat["gp"], pw, target, bgate8, ln_g, ln_b)
    loss_sum = (0.5 / D_MODEL) * jnp.sum(ln_sums[3])
    dmerged = _dx([dpre16], [w_out16], [], "dx_merged")
    dya16, dyb16, dgm16, mg_sums = _merge_bwd(dmerged, y_a, y_b, nat["gm"], bgate8)
    doa = _dx([dya16], [w_bra], [], "dx_oa")
    dys = _dx([dyb16], [w_brb], [], "dx_yssm")
    g_w_out, = _dw(merged, [dpre16], BF16, "dw_out")
    g_w_br = jnp.concatenate([_dw(oa, [dya16], BF16, "dw_bra")[0], _dw(y_ssm, [dyb16], BF16, "dw_brb")[0]], axis=0)
    g_w_ple, = _dw(p16, [dpw16], BF16, "dw_ple")

    do_att, do16, stats, dgatt16 = _combine_bwd(doa, nat["gatt"], o_att, lse)
    dseg = {"gatt": dgatt16, "gm": dgm16, "gp": dgp16}
    dbias = []
    for g in range(3):
        own_order = lambda t: _permute(t, dils[g]).reshape(nb, dils[g], seq // dils[g], t.shape[-1])
        cotangent = (do_att, o_att, lse) if g == 0 else (own_order(do16), own_order(stats))
        dqkv, db = _attn_bwd(qkv[g], bias[g * GROUP_HEADS:(g + 1) * GROUP_HEADS], cotangent, dils[g],
                             "attn_bwd%d" % g)
        dseg["qkv%d" % g] = dqkv.reshape(nb, seq, -1)
        dbias.append(db)
    g_rel = _bias_grad(jnp.concatenate(dbias, axis=0), bmaps)[:, 0, :NUM_BUCKETS].T

    dact, ddtg, dz, ssd_small, g_normw = _ssd_bwd(
        act, dtg, sgg, nat["z"], y_all, dys, sprev, alog_g, dskip_g, ssm_norm_w)
    dseg["z"] = dz
    dseg["dt"] = jnp.pad(_ungroup_lanes(ddtg), ((0, 0), (0, 0), (0, LANE - SSM_HEADS)))
    dpre, conv_sums = _conv_bwd_pre(dact, nat["xbc"], conv_wg, conv_bg, "conv_bwd")
    dseg["xbc"] = _conv_bwd_x(dpre, conv_wg, "conv_bwd_x")
    csum = _xbc_reference_order(conv_sums)

    dx_perm = [_unpermute(_dx([dseg["qkv%d" % g]], [wseg["qkv%d" % g]], [], "dx_qkv%d" % g, True), dils[g])
               for g in (1, 2)]
    dwseg = {"qkv%d" % g: _dw(x16p[g], [dseg["qkv%d" % g]], BF16, "dw_qkv%d" % g, True)[0] for g in range(3)}
    for gi, group in enumerate(DW_GROUPS):
        dwseg.update(zip(group, _dw(x16, [dseg[s] for s in group], BF16, "dw_nat%d" % gi, True)))
    names = ["qkv0"] + [s for group, _ in NAT_GROUPS for s in group]
    dx = finish_dx([dseg[s] for s in names], [wseg[s] for s in names], [dx] + dx_perm, dwseg, g_w_br, g_w_out, g_w_ple)

    small = dict(
        b_gate=jnp.stack([mg_sums[0], mg_sums[1], ln_sums[2]]),
        conv_w=csum[0:4], conv_b=csum[4:5],
        dt_bias=_ungroup_lanes(ssd_small[:, 2:3, :]), a_log=_ungroup_lanes(ssd_small[:, 0:1, :]),
        d_skip=_ungroup_lanes(ssd_small[:, 1:2, :]), ssm_norm_w=g_normw,
        ln_g=ln_sums[0:1], ln_b=ln_sums[1:2], rel_bias=g_rel)
    return loss_sum, dx, small


DX_TM = 256
SMALL_ORDER = ("b_gate", "conv_w", "conv_b", "dt_bias", "a_log", "d_skip", "ssm_norm_w", "ln_g", "ln_b", "rel_bias")
SMALL_FULL_SHAPES = dict(b_gate=(3, 1024), conv_w=(4, 3072), conv_b=(1, 3072), dt_bias=(1, 32), a_log=(1, 32),
                         d_skip=(1, 32), ssm_norm_w=(1, 2048), ln_g=(1, 1024), ln_b=(1, 1024), rel_bias=(32, 36))


def kernel(x, p, w_in, b_gate, conv_w, conv_b, dt_bias, a_log, d_skip, ssm_norm_w, w_branch, w_out, w_ple, ln_g, ln_b, rel_bias, loss_target, m_w_in, m_b_gate, m_conv_w, m_conv_b, m_dt_bias, m_a_log, m_d_skip, m_ssm_norm_w, m_w_branch, m_w_out, m_w_ple, m_ln_g, m_ln_b, m_rel_bias, v_w_in, v_b_gate, v_conv_w, v_conv_b, v_dt_bias, v_a_log, v_d_skip, v_ssm_norm_w, v_w_branch, v_w_out, v_w_ple, v_ln_g, v_ln_b, v_rel_bias):
    cx, cy, cc = _place()
    chip = 2 * cx + cy
    dev = 4 * cx + 2 * cy + cc

    w_in_t = jnp.transpose(w_in[0])
    win16 = _shard_to_window(w_in_t, chip)
    g_win, g_br, g_out, g_ple = _allgather_pieces(
        [win16, w_branch[0].astype(BF16), w_out[0].astype(BF16), w_ple[0].astype(BF16)])
    wseg = _assemble(g_win)
    w_br16 = g_br.reshape(4 * 704, D_MODEL)
    w_out16 = g_out.reshape(D_MODEL, D_MODEL)
    w_ple16 = jnp.transpose(g_ple, (1, 0, 2)).reshape(PLE_DIM, D_MODEL)
    shards = _allgather8(_pack_rows([b_gate[0], conv_w[0]]), "allgather_small_params")
    per_chip = [_unpack_rows(shards[2 * k], [(3, 256), (4, 768)]) for k in range(4)]
    b_gate_full = jnp.concatenate([pc[0] for pc in per_chip], axis=1)
    conv_w_full = jnp.concatenate([pc[1] for pc in per_chip], axis=1)

    place = jnp.stack([chip, cc]).astype(jnp.int32)
    reduced = []

    def finish_dx(dhs, ws, accs, dwseg, d_br, d_out, d_ple):
        grads = [_pack(dwseg), d_br.reshape(4, 704, D_MODEL), d_out.reshape(4, 256, D_MODEL),
                 jnp.transpose(d_ple.reshape(PLE_DIM, 4, 256), (1, 0, 2))]
        got = _sibling_exchange(grads)
        chip_sums = [_pair_sum(g, t, place, "grad_pair_sum_%d" % i) for i, (g, t) in enumerate(zip(grads, got))]
        dx, others = _dx(dhs, ws, accs, "dx_w_in_and_grad_chip_scatter", True, DX_TM, chip_sums)
        fulls = [_chip_sum(s, t, place, "grad_chip_sum_%d" % i) for i, (s, t) in enumerate(zip(chip_sums, others))]
        reduced.extend(_sibling_gather(fulls))
        return dx

    loss_sum, grad_x, small = _local_grads(
        x, p[0], loss_target, wseg, w_br16, w_out16, w_ple16, b_gate_full, conv_w_full, conv_b, dt_bias, a_log,
        d_skip, ssm_norm_w, ln_g, ln_b, rel_bias, finish_dx)
    big = reduced
    g_w_in = _window_to_shard(big[0], chip)
    g_w_branch, g_w_out, g_w_ple = big[1], big[2], big[3]
    parts = _allgather8(_pack_rows([small[n] for n in SMALL_ORDER] + [loss_sum.reshape(1, 1)]),
                        "allgather_small_grads")
    small_sum = _sum_rows([parts[i] for i in range(8)], F32, "small_grad_sum")
    *reduced_small, loss = _unpack_rows(small_sum, [SMALL_FULL_SHAPES[n] for n in SMALL_ORDER] + [(1, 1)])
    loss = loss.reshape(())
    sg = dict(zip(SMALL_ORDER, reduced_small))
    sg["b_gate"] = lax.dynamic_slice_in_dim(sg["b_gate"], chip * 256, 256, axis=1)
    sg["conv_w"] = lax.dynamic_slice_in_dim(sg["conv_w"], chip * 768, 768, axis=1)
    del dev

    upd = {}
    upd["w_in"] = [jnp.transpose(t) for t in _adamw(w_in_t, g_w_in, jnp.transpose(m_w_in[0]),
                                                      jnp.transpose(v_w_in[0]), "adamw_w_in")]
    upd["w_branch"] = _adamw(w_branch[0], g_w_branch, m_w_branch[0], v_w_branch[0], "adamw_w_branch")
    upd["w_out"] = _adamw(w_out[0], g_w_out, m_w_out[0], v_w_out[0], "adamw_w_out")
    upd["w_ple"] = _adamw(w_ple[0], g_w_ple, m_w_ple[0], v_w_ple[0], "adamw_w_ple")
    small_w = dict(b_gate=b_gate, conv_w=conv_w, conv_b=conv_b, dt_bias=dt_bias, a_log=a_log, d_skip=d_skip,
                   ssm_norm_w=ssm_norm_w, ln_g=ln_g, ln_b=ln_b, rel_bias=rel_bias)
    small_m = dict(b_gate=m_b_gate, conv_w=m_conv_w, conv_b=m_conv_b, dt_bias=m_dt_bias, a_log=m_a_log,
                   d_skip=m_d_skip, ssm_norm_w=m_ssm_norm_w, ln_g=m_ln_g, ln_b=m_ln_b, rel_bias=m_rel_bias)
    small_v = dict(b_gate=v_b_gate, conv_w=v_conv_w, conv_b=v_conv_b, dt_bias=v_dt_bias, a_log=v_a_log,
                   d_skip=v_d_skip, ssm_norm_w=v_ssm_norm_w, ln_g=v_ln_g, ln_b=v_ln_b, rel_bias=v_rel_bias)
    shapes = [small_w[n].shape for n in SMALL_ORDER]
    s_delta, s_m, s_v = _adamw(_pack_rows([small_w[n] for n in SMALL_ORDER]), _pack_rows([sg[n] for n in SMALL_ORDER]),
                               _pack_rows([small_m[n] for n in SMALL_ORDER]), _pack_rows([small_v[n] for n in SMALL_ORDER]),
                               "adamw_small")
    for i, n in enumerate(SMALL_ORDER):
        upd[n] = tuple(_unpack_rows(t, shapes)[i] for t in (s_delta, s_m, s_v))
        sg[n] = sg[n].reshape(small_w[n].shape)

    order = ("w_in", "b_gate", "conv_w", "conv_b", "dt_bias", "a_log", "d_skip", "ssm_norm_w", "w_branch", "w_out",
             "w_ple", "ln_g", "ln_b", "rel_bias")
    grads = dict(sg, w_in=jnp.transpose(g_w_in)[None],w_branch=g_w_branch[None], w_out=g_w_out[None], w_ple=g_w_ple[None])
    lead = lambda n, t: t[None] if n in ("w_in", "w_branch", "w_out", "w_ple") else t
    return (loss, grad_x, *[grads[n] for n in order], *[lead(n, upd[n][0]) for n in order],
            *[lead(n, upd[n][1]) for n in order], *[lead(n, upd[n][2]) for n in order])
```

```python
import functools
import math

import numpy as np
import jax
import jax.numpy as jnp
from jax import lax
from jax.experimental import pallas as pl
from jax.experimental.pallas import tpu as pltpu

F32, BF16 = jnp.float32, jnp.bfloat16

D_MODEL = 1024
HEAD_DIM = 64
GROUP_HEADS = 12
ATT_OUT = GROUP_HEADS * HEAD_DIM
PATTERNS = ((128, 1), (512, 4), (2048, 16))
BAND = 128
NUM_BUCKETS = 32
MAX_DISTANCE = 2048
D_INNER = 2048
SSM_HEADS = 32
SSM_GROUPS = 4
GROUP_SSM_HEADS = SSM_HEADS // SSM_GROUPS
D_STATE = 128
CHUNK = 128
PLE_DIM = 256
ALPHA = 2.0 ** 0.25
LN_EPS = 1e-5
RMS_EPS = 1e-5
ADAM_LR, ADAM_B1, ADAM_B2, ADAM_EPS, ADAM_WD, ADAM_STEP = 0.001, 0.9, 0.999, 1e-08, 0.01, 10
NEG = -1e30

QKV_W = 3 * ATT_OUT
IN_COLS = 15904
SHARD_COLS = IN_COLS // 4
DT_COL = 12800
ROW_TILE = 16
WIN_ROWS = 4000


def _win_offset(k):
    return (k * SHARD_COLS) % ROW_TILE


def _win_start(k):
    return k * SHARD_COLS - _win_offset(k)

VMEM_LIMIT_BYTES = 56 * 1024 * 1024
LANE = 128
MESH = pl.DeviceIdType.MESH
NT = (((1,), (1,)), ((), ()))
TN = (((0,), (0,)), ((), ()))


def _pcall(body, **kw):
    return pl.pallas_call(body, **kw)


def _params(*sem):
    return pltpu.CompilerParams(dimension_semantics=sem, vmem_limit_bytes=VMEM_LIMIT_BYTES)


def _sigmoid(v):
    return jax.nn.sigmoid(v)


MM_TM = 512


def _tok_spec(tm, width):
    return pl.BlockSpec((None, tm, width), lambda b, i: (b, i, 0))


def _whole(arr, single_buffer=False):
    mode = dict(pipeline_mode=pl.Buffered(1)) if single_buffer else {}
    return pl.BlockSpec(arr.shape, lambda b, i: (0,) * arr.ndim, **mode)


def _proj(a3, ws, out_dtype, name, w_rows_are_outputs=False, tm=MM_TM):
    nb, seq, kdim = a3.shape
    nw = len(ws)
    widths = [w.shape[0] if w_rows_are_outputs else w.shape[1] for w in ws]

    def body(*refs):
        a = refs[0][...].astype(BF16)
        for w_ref, o_ref in zip(refs[1:1 + nw], refs[1 + nw:]):
            if w_rows_are_outputs:
                v = lax.dot_general(a, w_ref[...], NT, preferred_element_type=F32)
            else:
                v = jnp.dot(a, w_ref[...], preferred_element_type=F32)
            o_ref[...] = v.astype(out_dtype)

    return _pcall(
        body, name=name, grid=(nb, seq // tm),
        in_specs=[_tok_spec(tm, kdim)] + [_whole(w) for w in ws],
        out_specs=[_tok_spec(tm, n) for n in widths],
        out_shape=[jax.ShapeDtypeStruct((nb, seq, n), out_dtype) for n in widths],
        compiler_params=_params("parallel", "parallel"),
    )(a3, *ws)


def _dx(dhs, ws, accs, name, w_rows_are_outputs=False, tm=MM_TM, scatter=None, own_order_accs=()):
    nb, seq, _ = dhs[0].shape
    nd, nacc, npa = len(dhs), len(accs), len(own_order_accs)
    kout = ws[0].shape[1] if w_rows_are_outputs else ws[0].shape[0]
    sums = scatter or []
    ns = len(sums)
    chunks = [_row_chunks(s.shape[1], s.shape[2] * s.dtype.itemsize) for s in sums]
    n_sem = 3 * sum(len(ch) for ch in chunks)
    grid = (nb, seq // tm)
    ntile = kout // LANE if npa else 0

    def body(*refs):
        n_in = 2 * nd + nacc + npa
        sum_refs, o_ref, got_refs = refs[n_in:n_in + ns], refs[n_in + ns], refs[n_in + ns + 1:n_in + 2 * ns + 1]
        tile_refs = refs[n_in + 2 * ns + 1:n_in + 2 * ns + 1 + ntile]

        def copies():
            send_sems, recv_sems = refs[-2], refs[-1]
            x, y, c = _place()
            out = []
            for a in range(ns):
                for s, m in chunks[a]:
                    for j, (cx, cy) in enumerate(_other_chips(x, y)):
                        k = len(out)
                        out.append(_remote(sum_refs[a].at[2 * cx + cy, pl.ds(s, m)], got_refs[a].at[j, pl.ds(s, m)],
                                           send_sems.at[k], recv_sems.at[k], (cx, cy, c)))
            return out

        if ns:
            @pl.when((pl.program_id(0) == 0) & (pl.program_id(1) == 0))
            def _():
                for cp in copies():
                    cp.start()

        v = None
        for dh_ref, w_ref in zip(refs[:nd], refs[nd:2 * nd]):
            dh = dh_ref[...].astype(BF16)
            if w_rows_are_outputs:
                t = jnp.dot(dh, w_ref[...], preferred_element_type=F32)
            else:
                t = lax.dot_general(dh, w_ref[...], NT, preferred_element_type=F32)
            v = t if v is None else v + t
        for a_ref in refs[2 * nd:2 * nd + nacc]:
            v = v + a_ref[...]
        for p_ref in refs[2 * nd + nacc:n_in]:
            v = v + _natural_rows(p_ref, tile_refs)
        o_ref[...] = v

        if ns:
            @pl.when((pl.program_id(0) == grid[0] - 1) & (pl.program_id(1) == grid[1] - 1))
            def _():
                for cp in copies():
                    cp.wait()

    out = _pcall(
        body, name=name, grid=grid,
        in_specs=[_tok_spec(tm, dh.shape[-1]) for dh in dhs] + [_whole(w, bool(ns)) for w in ws]
        + [_tok_spec(tm, kout)] * nacc
        + [pl.BlockSpec((None, p.shape[1], tm // p.shape[1], kout), lambda b, i: (b, 0, i, 0)) for p in own_order_accs]
        + [ANY] * ns,
        out_specs=[_tok_spec(tm, kout)] + [ANY] * ns,
        out_shape=[jax.ShapeDtypeStruct((nb, seq, kout), F32)]
        + [jax.ShapeDtypeStruct((3,) + s.shape[1:], s.dtype) for s in sums],
        input_output_aliases={2 * nd: 0} if nacc else {},
        scratch_shapes=[pltpu.VMEM((tm, LANE), F32)] * ntile
        + ([pltpu.SemaphoreType.DMA((n_sem,)), pltpu.SemaphoreType.DMA((n_sem,))] if ns else []),
        compiler_params=pltpu.CompilerParams(
            dimension_semantics=("arbitrary", "arbitrary") if ns else ("parallel", "parallel"),
            vmem_limit_bytes=VMEM_LIMIT_BYTES, has_side_effects=bool(ns)),
    )(*dhs, *ws, *accs, *own_order_accs, *sums)
    return (out[0], list(out[1:])) if ns else out[0]


def _dw(a3, dhs, out_dtype, name, rows_are_outputs=False):
    nb, seq, kdim = a3.shape
    nd = len(dhs)
    grid = (nb, seq // MM_TM)
    shapes = [(dh.shape[-1], kdim) if rows_are_outputs else (kdim, dh.shape[-1]) for dh in dhs]

    def body(*refs):
        b, i = pl.program_id(0), pl.program_id(1)
        dh_refs, o_refs, acc_refs = refs[1:1 + nd], refs[1 + nd:1 + 2 * nd], refs[1 + 2 * nd:]

        @pl.when((b == 0) & (i == 0))
        def _():
            for acc_ref in acc_refs:
                acc_ref[...] = jnp.zeros_like(acc_ref)

        a = refs[0][...].astype(BF16)
        for dh_ref, acc_ref in zip(dh_refs, acc_refs):
            dh = dh_ref[...].astype(BF16)
            acc_ref[...] += lax.dot_general(*((dh, a) if rows_are_outputs else (a, dh)), TN,
                                            preferred_element_type=F32)

        @pl.when((b == grid[0] - 1) & (i == grid[1] - 1))
        def _():
            for o_ref, acc_ref in zip(o_refs, acc_refs):
                o_ref[...] = acc_ref[...].astype(out_dtype)

    return _pcall(
        body, name=name, grid=grid,
        in_specs=[_tok_spec(MM_TM, kdim)] + [_tok_spec(MM_TM, dh.shape[-1]) for dh in dhs],
        out_specs=[pl.BlockSpec(s, lambda b, i: (0, 0)) for s in shapes],
        out_shape=[jax.ShapeDtypeStruct(s, out_dtype) for s in shapes],
        scratch_shapes=[pltpu.VMEM(s, F32) for s in shapes],
        compiler_params=_params("arbitrary", "arbitrary"),
    )(a3, *dhs)


def _qkv_rows(g):
    return [(part * QKV_W + g * ATT_OUT + hp * LANE, LANE) for hp in range(ATT_OUT // LANE) for part in range(3)]


XBC_START = 3 * QKV_W + ATT_OUT + D_INNER
GROUP_CH = GROUP_SSM_HEADS * HEAD_DIM
XBC_GROUP = GROUP_CH + 2 * D_STATE
CONV_DIM = SSM_GROUPS * XBC_GROUP


def _xbc_ranges():
    out = []
    for g in range(SSM_GROUPS):
        out += [(g * GROUP_CH, GROUP_CH), (D_INNER + g * D_STATE, D_STATE),
                (D_INNER + SSM_GROUPS * D_STATE + g * D_STATE, D_STATE)]
    return out


def _xbc_group_order(t):
    return jnp.concatenate([t[..., s:s + n] for s, n in _xbc_ranges()], axis=-1)


def _xbc_reference_order(t):
    g = lambda off, n: [t[..., k * XBC_GROUP + off:k * XBC_GROUP + off + n] for k in range(SSM_GROUPS)]
    return jnp.concatenate(g(0, GROUP_CH) + g(GROUP_CH, D_STATE) + g(GROUP_CH + D_STATE, D_STATE), axis=-1)


def _segments():
    one = lambda name, start, rows: (name, [(start, rows)], max(rows, LANE))
    return [("qkv%d" % g, _qkv_rows(g), QKV_W) for g in range(3)] + [
        one("gatt", 3 * QKV_W, ATT_OUT), one("z", 3 * QKV_W + ATT_OUT, D_INNER),
        ("xbc", [(XBC_START + s, n) for s, n in _xbc_ranges()], CONV_DIM), one("dt", DT_COL, SSM_HEADS),
        one("gm", DT_COL + SSM_HEADS, 2 * D_MODEL), one("gp", DT_COL + SSM_HEADS + 2 * D_MODEL, D_MODEL)]


LAYOUT_TC = 256
NAT_GROUPS = ((("gatt", "z", "dt", "gp"), 512), (("xbc", "gm"), 256))
DW_GROUPS = (("gatt", "z", "dt", "gp"), ("xbc",), ("gm",))


def _assemble(win):
    segs = _segments()

    def body(win_ref, *outs):
        def pieces(start, rows):
            t, end = start, start + rows
            while t < end:
                k = min(t // SHARD_COLS, 3)
                shard_end = (k + 1) * SHARD_COLS
                if k < 3 and shard_end % ROW_TILE and t == shard_end - shard_end % ROW_TILE:
                    lo = t - _win_start(k)
                    yield win_ref[k, lo:lo + ROW_TILE, :] + win_ref[k + 1, 0:ROW_TILE, :]
                    t += ROW_TILE
                    continue
                upto = min(end, shard_end - shard_end % ROW_TILE if k < 3 else end)
                yield win_ref[k, t - _win_start(k):upto - _win_start(k), :]
                t = upto

        for (_, ranges, total), o_ref in zip(segs, outs):
            off = 0
            for start, rows in ranges:
                for part in pieces(start, rows):
                    o_ref[off:off + part.shape[0], :] = part
                    off += part.shape[0]
            if off < total:
                o_ref[off:total, :] = jnp.zeros((total - off, o_ref.shape[1]), BF16)

    outs = _pcall(
        body, name="assemble_w_in", grid=(D_MODEL // LAYOUT_TC,),
        in_specs=[pl.BlockSpec((4, WIN_ROWS, LAYOUT_TC), lambda i: (0, 0, i))],
        out_specs=[pl.BlockSpec((total, LAYOUT_TC), lambda i: (0, i)) for _, _, total in segs],
        out_shape=[jax.ShapeDtypeStruct((total, D_MODEL), BF16) for _, _, total in segs],
        compiler_params=_params("parallel"),
    )(win)
    return {name: o for (name, _, _), o in zip(segs, outs)}


def _pack(dsegs):
    segs = _segments()

    def body(*refs):
        ins, o_ref = refs[:-1], refs[-1]
        tail = IN_COLS - _win_start(3)
        o_ref[3, tail:, :] = jnp.zeros((WIN_ROWS - tail, o_ref.shape[2]), BF16)
        for (_, ranges, _), s_ref in zip(segs, ins):
            off = 0
            for start, rows in ranges:
                for k in range(4):
                    lo = _win_start(k)
                    a, b = max(start, lo), min(start + rows, lo + WIN_ROWS)
                    if a < b:
                        o_ref[k, a - lo:b - lo, :] = s_ref[off + a - start:off + b - start, :]
                off += rows

    return _pcall(
        body, name="pack_dw_in", grid=(D_MODEL // LAYOUT_TC,),
        in_specs=[pl.BlockSpec((total, LAYOUT_TC), lambda i: (0, i)) for _, _, total in segs],
        out_specs=pl.BlockSpec((4, WIN_ROWS, LAYOUT_TC), lambda i: (0, 0, i)),
        out_shape=jax.ShapeDtypeStruct((4, WIN_ROWS, D_MODEL), BF16),
        compiler_params=_params("parallel"),
    )(*[dsegs[name] for name, _, _ in segs])


def _shard_to_window(shard_t, k):
    def at(off):
        return lambda w: jnp.pad(w.astype(BF16), ((off, WIN_ROWS - SHARD_COLS - off), (0, 0)))

    return lax.cond(k % 2 == 1, at(_win_offset(1)), at(_win_offset(0)), shard_t)


def _window_to_shard(win, k):
    return lax.dynamic_slice(win, ((k % 2) * _win_offset(1), 0), (SHARD_COLS, D_MODEL))


def _bucket_maps():
    qi = np.arange(BAND)[:, None]
    kj = np.arange(2 * BAND)[None, :]
    delta = qi + BAND - kj
    maps = []
    for window, dil in PATTERNS:
        valid = (delta >= 0) & (delta <= window // dil)
        dist = np.maximum(delta, 0) * dil
        max_exact = NUM_BUCKETS // 2
        d_f = np.maximum(dist, 1).astype(np.float32)
        large = max_exact + (np.log(d_f / np.float32(max_exact)) / np.float32(math.log(MAX_DISTANCE / max_exact))
                             * np.float32(NUM_BUCKETS - max_exact)).astype(np.int32)
        large = np.minimum(large, NUM_BUCKETS - 1)
        bucket = np.where(dist < max_exact, dist, large)
        maps.append(np.where(valid, bucket, -1).astype(np.int32))
    return np.stack(maps)


def _bias_tables(rel_bias, bmaps):
    def body(rb_ref, bm_ref, o_ref):
        h = pl.program_id(0)
        bm = bm_ref[...]
        acc = jnp.full(bm.shape, NEG, F32)
        for b in range(NUM_BUCKETS):
            acc = jnp.where(bm == b, rb_ref[b, h], acc)
        o_ref[...] = acc

    return _pcall(
        body, name="bias_tables", grid=(3 * GROUP_HEADS,),
        in_specs=[pl.BlockSpec(memory_space=pltpu.SMEM),
                  pl.BlockSpec((None, BAND, 2 * BAND), lambda h: (h // GROUP_HEADS, 0, 0))],
        out_specs=pl.BlockSpec((None, BAND, 2 * BAND), lambda h: (h, 0, 0)),
        out_shape=jax.ShapeDtypeStruct((3 * GROUP_HEADS, BAND, 2 * BAND), F32),
        compiler_params=_params("parallel"),
    )(rel_bias, bmaps)


def _bias_grad(dbias, bmaps):
    def body(db_ref, bm_ref, o_ref):
        bm = bm_ref[...]
        db = db_ref[...]
        lane = lax.broadcasted_iota(jnp.int32, (1, LANE), 1)
        vec = jnp.zeros((1, LANE), F32)
        for b in range(NUM_BUCKETS):
            s = jnp.sum(jnp.where(bm == b, db, 0.0), keepdims=True)
            vec = jnp.where(lane == b, s, vec)
        o_ref[...] = vec

    return _pcall(
        body, name="bias_grad", grid=(3 * GROUP_HEADS,),
        in_specs=[pl.BlockSpec((None, BAND, 2 * BAND), lambda h: (h, 0, 0)),
                  pl.BlockSpec((None, BAND, 2 * BAND), lambda h: (h // GROUP_HEADS, 0, 0))],
        out_specs=pl.BlockSpec((None, 1, LANE), lambda h: (h, 0, 0)),
        out_shape=jax.ShapeDtypeStruct((3 * GROUP_HEADS, 1, LANE), F32),
        compiler_params=_params("parallel"),
    )(dbias, bmaps)


def _rows(n):
    if isinstance(n, int):
        return pl.ds(n * BAND, BAND)
    return pl.ds(pl.multiple_of(n * BAND, BAND), BAND)


def _for_blocks(blocks, nblk, per, carry):
    carry = blocks([0], carry, False)
    start = 1 + (nblk - 1) % per
    for n in range(1, start):
        carry = blocks([n], carry, True)
    trips = (nblk - start) // per
    if trips > 0:
        carry = lax.fori_loop(
            0, trips, lambda t, c: blocks([start + t * per + u for u in range(per)], c, True), carry)
    return carry


def _pairs_per_step(d):
    return {1: 3, 4: 6, 16: 6}[d]


def _attn_fwd(qkv4, bias, d, name):
    nb, _, sub, _ = qkv4.shape
    nblk = sub // BAND
    scale = HEAD_DIM ** -0.5
    npair = ATT_OUT // LANE
    hps = _pairs_per_step(d)
    compact = d > 1

    def body(qkv_ref, bias_ref, o_ref, l_ref):
        def blocks(ns, carry, with_prev):
            chains = [(bi, i, h) for bi in range(len(ns)) for i in range(hps) for h in range(2)]
            first_head = lax.broadcasted_iota(jnp.int32, (BAND, LANE), 1) < HEAD_DIM
            pair = lambda n, i, part: qkv_ref[_rows(n), (3 * i + part) * LANE:(3 * i + part + 1) * LANE]
            scores = []
            for bi, i, h in chains:
                n = ns[bi]
                qp = pair(n, i, 0) * scale
                q = jnp.where(first_head if h == 0 else jnp.logical_not(first_head), qp, jnp.zeros_like(qp))
                s_c = lax.dot_general(q, pair(n, i, 1), NT, preferred_element_type=F32) + bias_ref[2 * i + h, :, BAND:]
                s_p = None
                if with_prev:
                    s_p = lax.dot_general(q, pair(n - 1, i, 1), NT,
                                          preferred_element_type=F32) + bias_ref[2 * i + h, :, :BAND]
                scores.append((s_c, s_p))
            probs = []
            for s_c, s_p in scores:
                m = jnp.max(s_c, -1, keepdims=True)
                if with_prev:
                    m = jnp.maximum(m, jnp.max(s_p, -1, keepdims=True))
                e_c = jnp.exp(s_c - m)
                den = jnp.sum(e_c, -1, keepdims=True)
                e_p = None
                if with_prev:
                    e_p = jnp.exp(s_p - m)
                    den = den + jnp.sum(e_p, -1, keepdims=True)
                    e_p = e_p.astype(BF16)
                probs.append((e_c.astype(BF16), e_p, den, m))
            outs = {}
            for (bi, i, h), (e_c, e_p, den, m) in zip(chains, probs):
                n = ns[bi]
                acc = jnp.dot(e_c, pair(n, i, 2), preferred_element_type=F32)
                if with_prev:
                    acc = acc + jnp.dot(e_p, pair(n - 1, i, 2), preferred_element_type=F32)
                outs[(bi, i, h)] = (acc / den, m + jnp.log(den))
            lane = lax.broadcasted_iota(jnp.int32, (BAND, LANE), 1)
            for bi, n in enumerate(ns):
                per_head = jnp.zeros((BAND, LANE), F32)
                for i in range(hps):
                    o_ref[_rows(n), i * LANE:(i + 1) * LANE] = jnp.where(first_head, outs[(bi, i, 0)][0],
                                                                         outs[(bi, i, 1)][0])
                    if compact:
                        for h in range(2):
                            per_head = jnp.where(lane == 2 * i + h, outs[(bi, i, h)][1], per_head)
                    else:
                        l_ref[_rows(n), i * LANE:(i + 1) * LANE] = jnp.where(first_head, outs[(bi, i, 0)][1],
                                                                             outs[(bi, i, 1)][1])
                if compact:
                    l_ref[_rows(n), :] = per_head
            return carry

        _for_blocks(blocks, nblk, 2 if hps == 1 else 1, 0)

    in_specs = [pl.BlockSpec((None, None, sub, 3 * LANE * hps), lambda hp, b, r: (b, r, 0, hp)),
                pl.BlockSpec((2 * hps, BAND, 2 * BAND), lambda hp, b, r: (hp, 0, 0))]
    if compact:
        return _pcall(
            body, name=name, grid=(1, nb, d), in_specs=in_specs,
            out_specs=[pl.BlockSpec((None, None, sub, ATT_OUT), lambda hp, b, r: (b, r, 0, 0)),
                       pl.BlockSpec((None, None, sub, LANE), lambda hp, b, r: (b, r, 0, 0))],
            out_shape=[jax.ShapeDtypeStruct((nb, d, sub, ATT_OUT), F32), jax.ShapeDtypeStruct((nb, d, sub, LANE), F32)],
            compiler_params=_params("parallel", "parallel", "parallel"),
        )(qkv4, bias)
    ospec = pl.BlockSpec((None, sub, hps * LANE), lambda hp, b, r: (b, 0, r * (npair // hps) + hp))
    return _pcall(
        body, name=name, grid=(npair // hps, nb, d), in_specs=in_specs, out_specs=[ospec, ospec],
        out_shape=[jax.ShapeDtypeStruct((nb, sub, d * ATT_OUT), F32)] * 2,
        compiler_params=_params("parallel", "parallel", "parallel"),
    )(qkv4, bias)


STAT_LSE_LANE = 16


def _attn_bwd(qkv4, bias, cotangent, d, name):
    nb, _, sub, _ = qkv4.shape
    nblk = sub // BAND
    scale = HEAD_DIM ** -0.5
    npair = ATT_OUT // LANE
    hps = _pairs_per_step(d)
    compact = d > 1

    def body(qkv_ref, bias_ref, *rest):
        do_ref, dqkv_ref, db_ref = rest[0], rest[-2], rest[-1]
        b, r = pl.program_id(1), pl.program_id(2)

        @pl.when((b == 0) & (r == 0))
        def _():
            db_ref[...] = jnp.zeros_like(db_ref)

        def blocks(ns, carry, with_prev):
            sides = (0, 1) if with_prev else (0,)
            chains = [(bi, i, h, sd) for bi in range(len(ns)) for i in range(hps) for h in range(2) for sd in sides]
            first_head = lax.broadcasted_iota(jnp.int32, (BAND, LANE), 1) < HEAD_DIM
            own = lambda h, t: jnp.where(first_head if h == 0 else jnp.logical_not(first_head), t, jnp.zeros_like(t))
            pair = lambda rows, i, part: qkv_ref[rows, (3 * i + part) * LANE:(3 * i + part + 1) * LANE]
            key_rows = lambda bi, sd: _rows(ns[bi] - sd)
            qs = {}
            for bi in range(len(ns)):
                for i in range(hps):
                    q_pair = pair(_rows(ns[bi]), i, 0) * scale
                    do = do_ref[_rows(ns[bi]), i * LANE:(i + 1) * LANE]
                    do16 = do.astype(BF16)
                    for h in range(2):
                        if compact:
                            st_ref, head = rest[1], 2 * i + h
                            ebar = st_ref[_rows(ns[bi]), head:head + 1]
                            lcol = st_ref[_rows(ns[bi]), STAT_LSE_LANE + head:STAT_LSE_LANE + head + 1]
                        else:
                            ebar = jnp.sum(own(h, do * rest[1][_rows(ns[bi]), i * LANE:(i + 1) * LANE]), -1, keepdims=True)
                            lcol = rest[2][_rows(ns[bi]), i * LANE + h * HEAD_DIM:i * LANE + h * HEAD_DIM + 1]
                        qs[(bi, i, h)] = (own(h, q_pair), q_pair, own(h, do16), do16, ebar, lcol)
            raw = []
            for bi, i, h, sd in chains:
                q, _, do_h, _, _, _ = qs[(bi, i, h)]
                bias_blk = bias_ref[2 * i + h, :, :BAND] if sd else bias_ref[2 * i + h, :, BAND:]
                s = lax.dot_general(q, pair(key_rows(bi, sd), i, 1), NT, preferred_element_type=F32) + bias_blk
                dp = lax.dot_general(do_h, pair(key_rows(bi, sd), i, 2), NT, preferred_element_type=F32)
                raw.append((s, dp))
            soft = []
            for (bi, i, h, sd), (s, dp) in zip(chains, raw):
                ebar, lcol = qs[(bi, i, h)][4:]
                p = jnp.exp(s - lcol)
                ds = p * (dp - ebar)
                if sd:
                    db_ref[2 * i + h, :, :BAND] += ds
                else:
                    db_ref[2 * i + h, :, BAND:] += ds
                soft.append((p.astype(BF16), ds.astype(BF16)))
            grads = {}
            for (bi, i, h, sd), (p16, ds16) in zip(chains, soft):
                _, q_pair, _, do16 = qs[(bi, i, h)][:4]
                grads[(bi, i, h, sd)] = (
                    jnp.dot(ds16, pair(key_rows(bi, sd), i, 1), preferred_element_type=F32),
                    lax.dot_general(ds16, q_pair, TN, preferred_element_type=F32),
                    lax.dot_general(p16, do16, TN, preferred_element_type=F32))
            both = lambda bi, i, sd, which: jnp.where(first_head, grads[(bi, i, 0, sd)][which],
                                                      grads[(bi, i, 1, sd)][which])
            carry = list(carry) if carry is not None else None
            for bi, n in enumerate(ns):
                for i in range(hps):
                    base = 3 * LANE * i
                    dq = both(bi, i, 0, 0)
                    if with_prev:
                        dq = dq + both(bi, i, 1, 0)
                        dqkv_ref[_rows(n - 1), base + LANE:base + 2 * LANE] = (
                            carry[2 * i] + both(bi, i, 1, 1)).astype(BF16)
                        dqkv_ref[_rows(n - 1), base + 2 * LANE:base + 3 * LANE] = (
                            carry[2 * i + 1] + both(bi, i, 1, 2)).astype(BF16)
                    dqkv_ref[_rows(n), base:base + LANE] = (dq * scale).astype(BF16)
                carry = [t for i in range(hps) for t in (both(bi, i, 0, 1), both(bi, i, 0, 2))]
            return tuple(carry)

        carry = _for_blocks(blocks, nblk, 2 if hps == 1 else 1, None)
        for i in range(hps):
            base = 3 * LANE * i
            dqkv_ref[_rows(nblk - 1), base + LANE:base + 2 * LANE] = carry[2 * i].astype(BF16)
            dqkv_ref[_rows(nblk - 1), base + 2 * LANE:base + 3 * LANE] = carry[2 * i + 1].astype(BF16)

    qspec = pl.BlockSpec((None, None, sub, 3 * LANE * hps), lambda hp, b, r: (b, r, 0, hp))
    bspec = pl.BlockSpec((2 * hps, BAND, 2 * BAND), lambda hp, b, r: (hp, 0, 0))
    if compact:
        cspecs = [pl.BlockSpec((None, None, sub, ATT_OUT), lambda hp, b, r: (b, r, 0, 0)),
                  pl.BlockSpec((None, None, sub, LANE), lambda hp, b, r: (b, r, 0, 0))]
    else:
        cspecs = [pl.BlockSpec((None, sub, hps * LANE), lambda hp, b, r: (b, 0, r * (npair // hps) + hp))] * 3
    return _pcall(
        body, name=name, grid=(npair // hps, nb, d),
        in_specs=[qspec, bspec] + cspecs, out_specs=[qspec, bspec],
        out_shape=[jax.ShapeDtypeStruct(qkv4.shape, BF16),
                   jax.ShapeDtypeStruct((GROUP_HEADS, BAND, 2 * BAND), F32)],
        compiler_params=_params("parallel", "arbitrary", "arbitrary"),
    )(qkv4, bias, *cotangent)


def _head_lanes(first_lane, one_channel):
    c = lax.broadcasted_iota(jnp.int32, (ATT_OUT, LANE), 0)
    lane = lax.broadcasted_iota(jnp.int32, (ATT_OUT, LANE), 1)
    hit = lane == first_lane + c // HEAD_DIM
    if one_channel:
        hit = hit & (c % HEAD_DIM == 0)
    return hit.astype(BF16)


def _exact_dot(v, m01, dims=None):
    parts = _split3(v)
    if dims is None:
        dot = lambda t: jnp.dot(t, m01, preferred_element_type=F32)
    else:
        dot = lambda t: lax.dot_general(t, m01, dims, preferred_element_type=F32)
    return (dot(parts[0]) + dot(parts[1])) + dot(parts[2])


def _store_own_order(value, tile_refs, out_ref):
    d, per, width = out_ref.shape
    for j in range(width // LANE):
        tile_refs[j][...] = value[:, j * LANE:(j + 1) * LANE]
    for r in range(d):
        rows = pl.ds(r, per, stride=d)
        for j in range(width // LANE):
            out_ref[r, :, j * LANE:(j + 1) * LANE] = tile_refs[j][rows, :].astype(out_ref.dtype)


def _token_orders(x, dilations):
    nb, seq, kdim = x.shape
    tm = 512

    def body(x_ref, nat_ref, *rest):
        outs, tile_refs = rest[:len(dilations)], rest[len(dilations):]
        xv = x_ref[...]
        nat_ref[...] = xv.astype(BF16)
        for o_ref in outs:
            _store_own_order(xv, tile_refs, o_ref)

    outs = _pcall(
        body, name="token_orders", grid=(nb, seq // tm), in_specs=[_tok_spec(tm, kdim)],
        out_specs=[_tok_spec(tm, kdim)]
        + [pl.BlockSpec((None, d, tm // d, kdim), lambda b, i: (b, 0, i, 0)) for d in dilations],
        out_shape=[jax.ShapeDtypeStruct((nb, seq, kdim), BF16)]
        + [jax.ShapeDtypeStruct((nb, d, seq // d, kdim), BF16) for d in dilations],
        scratch_shapes=[pltpu.VMEM((tm, LANE), F32)] * (kdim // LANE),
        compiler_params=_params("parallel", "parallel"),
    )(x)
    return [outs[0]] + [o.reshape(nb, seq, kdim) for o in outs[1:]]


def _natural_rows(p_ref, tile_refs):
    d, per, width = p_ref.shape
    for r in range(d):
        rows = pl.ds(r, per, stride=d)
        for j in range(width // LANE):
            tile_refs[j][rows, :] = p_ref[r, :, j * LANE:(j + 1) * LANE]
    return jnp.concatenate([tile_refs[j][...] for j in range(width // LANE)], axis=1)


def _combine_fwd(o0, l0, dilated, gatt):
    nb, seq, _ = gatt.shape
    tm = 512
    ntile = ATT_OUT // LANE

    def body(o0_ref, l0_ref, o1_ref, l1_ref, o2_ref, l2_ref, g_ref, oa_ref, oatt_ref, lse_ref, *tile_refs):
        spread = _head_lanes(0, False)
        l0v = l0_ref[...]
        l1v = _exact_dot(_natural_rows(l1_ref, tile_refs), spread, NT)
        l2v = _exact_dot(_natural_rows(l2_ref, tile_refs), spread, NT)
        m = jnp.maximum(jnp.maximum(l0v, l1v), l2v)
        tot = m + jnp.log(jnp.exp(l0v - m) + jnp.exp(l1v - m) + jnp.exp(l2v - m))
        o = jnp.exp(l0v - tot) * o0_ref[...]
        o = o + jnp.exp(l1v - tot) * _natural_rows(o1_ref, tile_refs)
        o = o + jnp.exp(l2v - tot) * _natural_rows(o2_ref, tile_refs)
        g = g_ref[...]
        oa_ref[...] = (o * (g * _sigmoid(g))).astype(BF16)
        oatt_ref[...] = o
        lse_ref[...] = tot

    spec = pl.BlockSpec((None, tm, ATT_OUT), lambda b, i: (b, i, 0))
    own = lambda t: pl.BlockSpec((None, t.shape[1], tm // t.shape[1], t.shape[3]), lambda b, i: (b, 0, i, 0))
    (o1, l1), (o2, l2) = dilated
    return _pcall(
        body, name="attn_combine", grid=(nb, seq // tm),
        in_specs=[spec, spec, own(o1), own(l1), own(o2), own(l2), spec], out_specs=[spec] * 3,
        out_shape=[jax.ShapeDtypeStruct((nb, seq, ATT_OUT), BF16), jax.ShapeDtypeStruct((nb, seq, ATT_OUT), F32),
                   jax.ShapeDtypeStruct((nb, seq, ATT_OUT), F32)],
        scratch_shapes=[pltpu.VMEM((tm, LANE), F32)] * ntile,
        compiler_params=_params("parallel", "parallel"),
    )(o0, l0, o1, l1, o2, l2, gatt)


def _combine_bwd(doa, gatt, o_att, lse, dilations):
    nb, seq, _ = gatt.shape
    tm = 512

    def body(doa_ref, g_ref, o_ref, l_ref, do_ref, dg_ref, *rest):
        ntile = ATT_OUT // LANE
        outs, tile_refs = rest[:-ntile], rest[-ntile:]
        g = g_ref[...]
        sg = _sigmoid(g)
        do = doa_ref[...] * (g * sg)
        do_ref[...] = do
        stats = (_exact_dot(do * o_ref[...], _head_lanes(0, False))
                 + _exact_dot(l_ref[...], _head_lanes(STAT_LSE_LANE, True)))
        dg_ref[...] = (doa_ref[...] * o_ref[...] * (sg * (1.0 + g * (1.0 - sg)))).astype(BF16)
        for k in range(len(dilations)):
            _store_own_order(do, tile_refs, outs[2 * k])
            _store_own_order(stats, tile_refs, outs[2 * k + 1])

    spec = pl.BlockSpec((None, tm, ATT_OUT), lambda b, i: (b, i, 0))
    own = lambda d, width: pl.BlockSpec((None, d, tm // d, width), lambda b, i: (b, 0, i, 0))
    outs = _pcall(
        body, name="attn_combine_bwd", grid=(nb, seq // tm), in_specs=[spec] * 4,
        out_specs=[spec, spec] + [own(d, w) for d in dilations for w in (ATT_OUT, LANE)],
        out_shape=[jax.ShapeDtypeStruct((nb, seq, ATT_OUT), F32), jax.ShapeDtypeStruct((nb, seq, ATT_OUT), BF16)]
        + [jax.ShapeDtypeStruct((nb, d, seq // d, w), t) for d in dilations for w, t in ((ATT_OUT, BF16), (LANE, F32))],
        scratch_shapes=[pltpu.VMEM((tm, LANE), F32)] * (ATT_OUT // LANE),
        compiler_params=_params("parallel", "parallel"),
    )(doa, gatt, o_att, lse)
    return outs[0], outs[1], outs[2:]


CONV_TM = 1024
CONV_TC = 1024


def _shift_down(cur, halo, k):
    rolled = pltpu.roll(cur, k, 0)
    hro = pltpu.roll(halo, k, 0)
    row = lax.broadcasted_iota(jnp.int32, hro.shape, 0)
    return jnp.concatenate([jnp.where(row < k, hro, rolled[:8]), rolled[8:]], axis=0)


def _shift_up(cur, halo, k):
    n = cur.shape[0]
    rolled = pltpu.roll(cur, n - k, 0)
    hro = pltpu.roll(halo, 8 - k, 0)
    row = lax.broadcasted_iota(jnp.int32, hro.shape, 0)
    return jnp.concatenate([rolled[:n - 8], jnp.where(row >= 8 - k, hro, rolled[n - 8:])], axis=0)


def _conv_pre(cur, halo, w_ref, b_ref):
    acc = cur * w_ref[3:4, :] + b_ref[...]
    for k in range(1, 4):
        acc = acc + _shift_down(cur, halo, k) * w_ref[3 - k:4 - k, :]
    return acc


def _conv_specs(seq):
    nblk = seq // CONV_TM
    cur = pl.BlockSpec((None, CONV_TM, CONV_TC), lambda cb, b, i: (b, i, cb))
    prev = pl.BlockSpec((None, 8, CONV_TC), lambda cb, b, i: (b, jnp.maximum(i * (CONV_TM // 8) - 1, 0), cb))
    nxt = pl.BlockSpec((None, 8, CONV_TC),
                       lambda cb, b, i: (b, jnp.minimum((i + 1) * (CONV_TM // 8), seq // 8 - 1), cb))
    wspec = pl.BlockSpec((4, CONV_TC), lambda cb, b, i: (0, cb))
    bspec = pl.BlockSpec((1, CONV_TC), lambda cb, b, i: (0, cb))
    return nblk, cur, prev, nxt, wspec, bspec


def _conv_fwd(xin, w4, bias, name):
    nb, seq, ch = xin.shape
    _, cur, prev, _, wspec, bspec = _conv_specs(seq)

    def body(x_ref, h_ref, w_ref, b_ref, o_ref):
        halo = jnp.where(pl.program_id(2) > 0, h_ref[...], 0.0)
        pre = _conv_pre(x_ref[...], halo, w_ref, b_ref)
        o_ref[...] = pre * _sigmoid(pre)

    return _pcall(
        body, name=name, grid=(ch // CONV_TC, nb, seq // CONV_TM),
        in_specs=[cur, prev, wspec, bspec], out_specs=cur,
        out_shape=jax.ShapeDtypeStruct(xin.shape, F32),
        compiler_params=_params("parallel", "parallel", "parallel"),
    )(xin, xin, w4, bias)


def _conv_bwd_pre(dact, xin, w4, bias, name):
    nb, seq, ch = xin.shape
    _, cur, prev, _, wspec, bspec = _conv_specs(seq)

    def body(da_ref, x_ref, h_ref, w_ref, b_ref, dp_ref, s_ref):
        b, i = pl.program_id(1), pl.program_id(2)

        @pl.when((b == 0) & (i == 0))
        def _():
            s_ref[...] = jnp.zeros_like(s_ref)

        halo = jnp.where(i > 0, h_ref[...], 0.0)
        x = x_ref[...]
        pre = _conv_pre(x, halo, w_ref, b_ref)
        sg = _sigmoid(pre)
        dpre = da_ref[...] * (sg * (1.0 + pre * (1.0 - sg)))
        dp_ref[...] = dpre
        s_ref[3:4, :] += jnp.sum(dpre * x, 0, keepdims=True)
        for k in range(1, 4):
            s_ref[3 - k:4 - k, :] += jnp.sum(dpre * _shift_down(x, halo, k), 0, keepdims=True)
        s_ref[4:5, :] += jnp.sum(dpre, 0, keepdims=True)

    return _pcall(
        body, name=name, grid=(ch // CONV_TC, nb, seq // CONV_TM),
        in_specs=[cur, cur, prev, wspec, bspec],
        out_specs=[cur, pl.BlockSpec((8, CONV_TC), lambda cb, b, i: (0, cb))],
        out_shape=[jax.ShapeDtypeStruct(xin.shape, F32), jax.ShapeDtypeStruct((8, ch), F32)],
        compiler_params=_params("parallel", "arbitrary", "arbitrary"),
    )(dact, xin, xin, w4, bias)


def _conv_bwd_x(dpre, w4, name):
    nb, seq, ch = dpre.shape
    nblk, cur, _, nxt, wspec, _ = _conv_specs(seq)

    def body(d_ref, n_ref, w_ref, o_ref):
        halo = jnp.where(pl.program_id(2) < nblk - 1, n_ref[...], 0.0)
        cur_v = d_ref[...]
        acc = cur_v * w_ref[3:4, :]
        for j in range(1, 4):
            acc = acc + _shift_up(cur_v, halo, j) * w_ref[3 - j:4 - j, :]
        o_ref[...] = acc.astype(BF16)

    return _pcall(
        body, name=name, grid=(ch // CONV_TC, nb, seq // CONV_TM),
        in_specs=[cur, nxt, wspec], out_specs=cur,
        out_shape=jax.ShapeDtypeStruct(dpre.shape, BF16),
        compiler_params=_params("parallel", "parallel", "parallel"),
    )(dpre, dpre, w4)


def _softplus_sig(dt_raw, dt_bias_row):
    nb, seq, _ = dt_raw.shape
    tm = 512

    def body(r_ref, b_ref, sp_ref, sg_ref):
        v = r_ref[...] + b_ref[...]
        sp_ref[...] = jnp.maximum(v, 0.0) + jnp.log1p(jnp.exp(-jnp.abs(v)))
        sg_ref[...] = _sigmoid(v)

    spec = pl.BlockSpec((None, tm, LANE), lambda b, i: (b, i, 0))
    return _pcall(
        body, name="dt_softplus", grid=(nb, seq // tm),
        in_specs=[spec, pl.BlockSpec((1, LANE), lambda b, i: (0, 0))], out_specs=[spec, spec],
        out_shape=[jax.ShapeDtypeStruct(dt_raw.shape, F32)] * 2,
        compiler_params=_params("parallel", "parallel"),
    )(dt_raw, dt_bias_row)


def _group_lanes(t):
    pads = [(0, 0)] * (t.ndim - 1) + [(0, LANE - GROUP_SSM_HEADS)]
    return jnp.stack([jnp.pad(t[..., GROUP_SSM_HEADS * g:GROUP_SSM_HEADS * (g + 1)], pads) for g in range(SSM_GROUPS)])


def _ungroup_lanes(t):
    return jnp.concatenate([t[g][..., :GROUP_SSM_HEADS] for g in range(SSM_GROUPS)], axis=-1)


def _decays(dt, al_ref):
    row = lax.broadcasted_iota(jnp.int32, (CHUNK, CHUNK), 0)
    col = lax.broadcasted_iota(jnp.int32, (CHUNK, CHUNK), 1)
    tril = (row >= col).astype(BF16)
    triu = (row <= col).astype(BF16)
    arow = -jnp.exp(al_ref[...])
    hi, mid, lo = _split3(dt * arow)
    down = lambda t: jnp.dot(tril, t, preferred_element_type=F32)
    across = lambda t: lax.dot_general(t, triu, TN, preferred_element_type=F32)
    acs = (down(hi) + down(mid)) + down(lo)
    acs_t = (across(hi) + across(mid)) + across(lo)
    return arow, acs, acs_t, row >= col, triu


STEP_CHUNKS = 8


def _ssd_specs(nb, seq):
    nc = seq // CHUNK
    hw = GROUP_SSM_HEADS * HEAD_DIM
    rows, steps = STEP_CHUNKS * CHUNK, nc // STEP_CHUNKS

    def mk(rev):
        cidx = (lambda c: steps - 1 - c) if rev else (lambda c: c)
        wide = pl.BlockSpec((None, rows, hw), lambda g, b, c: (b, cidx(c), g))
        xbc = pl.BlockSpec((None, rows, XBC_GROUP), lambda g, b, c: (b, cidx(c), g))
        lanes = pl.BlockSpec((None, None, rows, LANE), lambda g, b, c: (g, b, cidx(c), 0))
        prev = pl.BlockSpec((None, STEP_CHUNKS, None, D_STATE, hw), lambda g, b, c: (b, cidx(c), g, 0, 0))
        return wide, xbc, lanes, prev

    grow = pl.BlockSpec((None, 1, LANE), lambda g, b, c: (g, 0, 0))
    nwspec = pl.BlockSpec((1, hw), lambda g, b, c: (0, g))
    return nc, steps, hw, mk, grow, nwspec


def _head_expand():
    hw = GROUP_SSM_HEADS * HEAD_DIM
    r = lax.broadcasted_iota(jnp.int32, (LANE, hw), 0)
    c = lax.broadcasted_iota(jnp.int32, (LANE, hw), 1)
    return ((c // HEAD_DIM) == r).astype(BF16)


def _split3(v):
    hi = v.astype(BF16)
    rest = v - hi.astype(F32)
    mid = rest.astype(BF16)
    return hi, mid, (rest - mid.astype(F32)).astype(BF16)


def _to_channels(v, e):
    hi, mid, lo = _split3(v)
    dot = lambda t: jnp.dot(t, e, preferred_element_type=F32)
    return (dot(hi) + dot(mid)) + dot(lo)


def _to_heads(w, e):
    hi, mid, lo = _split3(w)
    dot = lambda t: lax.dot_general(t, e, (((1,), (1,)), ((), ())), preferred_element_type=F32)
    return (dot(hi) + dot(mid)) + dot(lo)


def _row8(v):
    return jnp.broadcast_to(v, (8, v.shape[1]))


def _ssd_chunk_setup(dt, al_ref, ds_ref):
    arow, acs, acs_t, causal, triu = _decays(dt, al_ref)
    e = _head_expand()
    dtx = _to_channels(dt, e)
    acsx = _to_channels(acs, e)
    lastx = acsx[CHUNK - 1:CHUNK, :]
    dskx = _to_channels(_row8(ds_ref[...]), e)[0:1, :]
    return arow, acs, acs_t, causal, triu, e, dtx, acsx, lastx, dskx


def _ssd_fwd(xbc, dtg, z, alog_g, dskip_g, normw):
    nb, seq, _ = xbc.shape
    nc, steps, hw, mk, grow, nwspec = _ssd_specs(nb, seq)
    wide, xbc_spec, lanes, prev = mk(False)
    tn = (((0,), (0,)), ((), ()))

    def body(xbc_ref, dt_ref, z_ref, al_ref, ds_ref, nw_ref, ys_ref, y_ref, sp_ref, st_ref):
        @pl.when(pl.program_id(2) == 0)
        def _():
            st_ref[...] = jnp.zeros_like(st_ref)

        for ci in range(STEP_CHUNKS):
            chunk(ci, xbc_ref, dt_ref, z_ref, al_ref, ds_ref, nw_ref, ys_ref, y_ref, sp_ref, st_ref)

    def chunk(ci, xbc_ref, dt_ref, z_ref, al_ref, ds_ref, nw_ref, ys_ref, y_ref, sp_ref, st_ref):
        rows = slice(ci * CHUNK, (ci + 1) * CHUNK)
        dt = dt_ref[rows, :]
        _, acs, acs_t, causal, _, _, dtx, acsx, lastx, dskx = _ssd_chunk_setup(dt, al_ref, ds_ref)
        bmat = xbc_ref[rows, GROUP_CH:GROUP_CH + D_STATE].astype(BF16)
        cmat = xbc_ref[rows, GROUP_CH + D_STATE:].astype(BF16)
        cb = lax.dot_general(cmat, bmat, (((1,), (1,)), ((), ())), preferred_element_type=F32)
        x = xbc_ref[rows, :GROUP_CH]
        xdt = x * dtx
        xdt16 = xdt.astype(BF16)
        first_head = lax.broadcasted_iota(jnp.int32, (CHUNK, LANE), 1) < HEAD_DIM
        pairs = []
        for hp in range(GROUP_SSM_HEADS // 2):
            xp = xdt16[:, hp * LANE:(hp + 1) * LANE]
            two = []
            for j in (2 * hp, 2 * hp + 1):
                lmat = jnp.exp(jnp.where(causal, acs[:, j:j + 1] - acs_t[j:j + 1, :], -jnp.inf))
                two.append(jnp.dot((cb * lmat).astype(BF16), xp, preferred_element_type=F32))
            pairs.append(jnp.where(first_head, two[0], two[1]))
        yd = jnp.concatenate(pairs, axis=1)
        s_prev = st_ref[...]
        s16 = s_prev.astype(BF16)
        sp_ref[ci] = s16
        yo = jnp.dot(cmat, s16, preferred_element_type=F32) * jnp.exp(acsx)
        sts = lax.dot_general(bmat, (xdt * jnp.exp(lastx - acsx)).astype(BF16), tn, preferred_element_type=F32)
        st_ref[...] = s_prev * jnp.exp(lastx) + sts
        y = yd + yo + dskx * x
        zz = z_ref[rows, :]
        u = y * (zz * _sigmoid(zz))
        rn = lax.rsqrt(jnp.mean(u * u, -1, keepdims=True) + RMS_EPS)
        ys_ref[rows, :] = (u * rn * nw_ref[...]).astype(BF16)
        y_ref[rows, :] = y

    return _pcall(
        body, name="ssd_fwd", grid=(SSM_GROUPS, nb, steps),
        in_specs=[xbc_spec, lanes, wide, grow, grow, nwspec],
        out_specs=[wide, wide, prev],
        out_shape=[jax.ShapeDtypeStruct((nb, seq, D_INNER), BF16), jax.ShapeDtypeStruct((nb, seq, D_INNER), F32),
                   jax.ShapeDtypeStruct((nb, nc, SSM_GROUPS, D_STATE, hw), BF16)],
        scratch_shapes=[pltpu.VMEM((D_STATE, hw), F32)],
        compiler_params=_params("parallel", "parallel", "arbitrary"),
    )(xbc, dtg, z, alog_g, dskip_g, normw)


def _ssd_bwd(xbc, dtg, sgg, z, y, dys, sprev, alog_g, dskip_g, normw):
    nb, seq, _ = xbc.shape
    nc, steps, hw, mk, grow, nwspec = _ssd_specs(nb, seq)
    wide, xbc_spec, lanes, prev = mk(True)
    nt = (((1,), (1,)), ((), ()))
    tn = (((0,), (0,)), ((), ()))

    def body(xbc_ref, dt_ref, sg_ref, z_ref, y_ref, dys_ref, sp_ref, al_ref, ds_ref, nw_ref,
             dxbc_ref, ddt_ref, dz_ref, small_ref, dnw_ref, g_ref):
        b, c = pl.program_id(1), pl.program_id(2)

        @pl.when((b == 0) & (c == 0))
        def _():
            small_ref[...] = jnp.zeros_like(small_ref)
            dnw_ref[...] = jnp.zeros_like(dnw_ref)

        @pl.when(c == 0)
        def _():
            g_ref[...] = jnp.zeros_like(g_ref)

        for ci in reversed(range(STEP_CHUNKS)):
            chunk(ci, xbc_ref, dt_ref, sg_ref, z_ref, y_ref, dys_ref, sp_ref, al_ref, ds_ref, nw_ref,
                  dxbc_ref, ddt_ref, dz_ref, small_ref, dnw_ref, g_ref)

    def chunk(ci, xbc_ref, dt_ref, sg_ref, z_ref, y_ref, dys_ref, sp_ref, al_ref, ds_ref, nw_ref,
              dxbc_ref, ddt_ref, dz_ref, small_ref, dnw_ref, g_ref):
        rows = slice(ci * CHUNK, (ci + 1) * CHUNK)
        yv, zz, dys_v, nw = y_ref[rows, :], z_ref[rows, :], dys_ref[rows, :], nw_ref[...]
        sz = _sigmoid(zz)
        silu = zz * sz
        u = yv * silu
        rn = lax.rsqrt(jnp.mean(u * u, -1, keepdims=True) + RMS_EPS)
        gn = dys_v * nw
        du = rn * gn - u * (rn * rn * rn) * jnp.mean(u * gn, -1, keepdims=True)
        dnw_ref[...] += jnp.sum(dys_v * u * rn, 0, keepdims=True)
        dy = du * silu
        dz_ref[rows, :] = du * yv * (sz * (1.0 + zz * (1.0 - sz)))

        dt = dt_ref[rows, :]
        arow, acs, acs_t, causal, triu, e, dtx, acsx, lastx, dskx = _ssd_chunk_setup(dt, al_ref, ds_ref)
        dfsx = jnp.exp(acsx)
        dtex = jnp.exp(lastx - acsx)
        bmat = xbc_ref[rows, GROUP_CH:GROUP_CH + D_STATE].astype(BF16)
        cmat = xbc_ref[rows, GROUP_CH + D_STATE:].astype(BF16)
        cb = lax.dot_general(cmat, bmat, nt, preferred_element_type=F32)
        x = xbc_ref[rows, :GROUP_CH]
        xdt = x * dtx
        xdt16 = xdt.astype(BF16)
        xdte = xdt * dtex
        dy16 = dy.astype(BF16)
        dyd = dy * dfsx
        dyd16 = dyd.astype(BF16)
        s16 = sp_ref[ci]
        g = g_ref[...]
        g16 = g.astype(BF16)
        cs = jnp.dot(cmat, s16, preferred_element_type=F32)
        dc_off = lax.dot_general(dyd16, s16, nt, preferred_element_type=F32)
        g_here = lax.dot_general(cmat, dyd16, tn, preferred_element_type=F32)
        bg = jnp.dot(bmat, g16, preferred_element_type=F32)
        db_st = lax.dot_general(xdte.astype(BF16), g16, nt, preferred_element_type=F32)
        ddte_w = bg * xdte
        dcd = _to_heads(_row8(jnp.sum(g * s16.astype(F32), 0, keepdims=True)), e)[0:1, :]
        lane = lax.broadcasted_iota(jnp.int32, (CHUNK, LANE), 1)
        first_head = lane < HEAD_DIM
        sub = lax.broadcasted_iota(jnp.int32, (CHUNK, LANE), 0)
        dacs = jnp.zeros((CHUNK, LANE), F32)
        colsums = jnp.zeros((CHUNK, LANE), F32)
        dcb = jnp.zeros((CHUNK, CHUNK), F32)
        pairs = []
        for hp in range(GROUP_SSM_HEADS // 2):
            xp = xdt16[:, hp * LANE:(hp + 1) * LANE]
            dyp = dy16[:, hp * LANE:(hp + 1) * LANE]
            two = []
            for idx, j in enumerate((2 * hp, 2 * hp + 1)):
                lmat = jnp.exp(jnp.where(causal, acs[:, j:j + 1] - acs_t[j:j + 1, :], -jnp.inf))
                mf = cb * lmat
                dy_h = jnp.where(first_head if idx == 0 else jnp.logical_not(first_head), dyp, jnp.zeros_like(dyp))
                dm = lax.dot_general(dy_h, xp, nt, preferred_element_type=F32)
                two.append(lax.dot_general(mf.astype(BF16), dyp, tn, preferred_element_type=F32))
                wmat = dm * mf
                dcb = dcb + dm * lmat
                dacs = jnp.where(lane == j, jnp.sum(wmat, -1, keepdims=True), dacs)
                colsums = jnp.where(sub == j, jnp.sum(wmat, 0, keepdims=True), colsums)
            pairs.append(jnp.where(first_head, two[0], two[1]))
        dxdt = bg * dtex + jnp.concatenate(pairs, axis=1)
        dacs = dacs - colsums.T + _to_heads(dyd * cs - ddte_w, e)
        cd_row = jnp.exp(acs[CHUNK - 1:CHUNK, :])
        tail = _to_heads(_row8(jnp.sum(ddte_w, 0, keepdims=True)), e)[0:1, :] + dcd * cd_row
        dacs = dacs + jnp.where(sub == CHUNK - 1, tail, 0.0)
        d_hi, d_mid, d_lo = _split3(dacs)
        up = lambda t: jnp.dot(triu, t, preferred_element_type=F32)
        da = (up(d_hi) + up(d_mid)) + up(d_lo)
        ddt_raw = (da * arow + _to_heads(dxdt * x, e)) * sg_ref[rows, :]
        ddt_ref[rows, :] = ddt_raw
        small_ref[0:1, :] += jnp.sum(da * dt, 0, keepdims=True) * arow
        small_ref[1:2, :] += _to_heads(_row8(jnp.sum(dy * x, 0, keepdims=True)), e)[0:1, :]
        small_ref[2:3, :] += jnp.sum(ddt_raw, 0, keepdims=True)
        dcb16 = dcb.astype(BF16)
        dxbc_ref[rows, GROUP_CH + D_STATE:] = dc_off + jnp.dot(dcb16, bmat, preferred_element_type=F32)
        dxbc_ref[rows, GROUP_CH:GROUP_CH + D_STATE] = db_st + lax.dot_general(dcb16, cmat, tn,
                                                                               preferred_element_type=F32)
        dxbc_ref[rows, :GROUP_CH] = dxdt * dtx + dskx * dy
        g_ref[...] = g * jnp.exp(lastx) + g_here

    return _pcall(
        body, name="ssd_bwd", grid=(SSM_GROUPS, nb, steps),
        in_specs=[xbc_spec, lanes, lanes, wide, wide, wide, prev, grow, grow, nwspec],
        out_specs=[xbc_spec, lanes, wide,
                   pl.BlockSpec((None, 8, LANE), lambda g, b, c: (g, 0, 0)), nwspec],
        out_shape=[jax.ShapeDtypeStruct((nb, seq, CONV_DIM), F32),
                   jax.ShapeDtypeStruct((SSM_GROUPS, nb, seq, LANE), F32),
                   jax.ShapeDtypeStruct((nb, seq, D_INNER), F32),
                   jax.ShapeDtypeStruct((SSM_GROUPS, 8, LANE), F32),
                   jax.ShapeDtypeStruct((1, D_INNER), F32)],
        scratch_shapes=[pltpu.VMEM((D_STATE, hw), F32)],
        compiler_params=_params("parallel", "arbitrary", "arbitrary"),
    )(xbc, dtg, sgg, z, y, dys, sprev, alog_g, dskip_g, normw)


EW_TM = 256


def _merge_fwd(y_a, y_b, gm, bgate):
    nb, seq, _ = y_a.shape

    def body(a_ref, b_ref, ga_ref, gb_ref, bg_ref, o_ref):
        sa = _sigmoid(ga_ref[...] + bg_ref[0:1, :])
        sb = _sigmoid(gb_ref[...] + bg_ref[1:2, :])
        o_ref[...] = (sa * a_ref[...] + sb * b_ref[...]).astype(BF16)

    spec = pl.BlockSpec((None, EW_TM, D_MODEL), lambda b, i: (b, i, 0))
    spec1 = pl.BlockSpec((None, EW_TM, D_MODEL), lambda b, i: (b, i, 1))
    return _pcall(
        body, name="merge_fwd", grid=(nb, seq // EW_TM),
        in_specs=[spec, spec, spec, spec1, pl.BlockSpec((8, D_MODEL), lambda b, i: (0, 0))], out_specs=spec,
        out_shape=jax.ShapeDtypeStruct((nb, seq, D_MODEL), BF16),
        compiler_params=_params("parallel", "parallel"),
    )(y_a, y_b, gm, gm, bgate)


def _merge_bwd(dmerged, y_a, y_b, gm, bgate):
    nb, seq, _ = y_a.shape

    def body(dm_ref, a_ref, b_ref, ga_ref, gb_ref, bg_ref, dya_ref, dyb_ref, dg_ref, s_ref):
        @pl.when((pl.program_id(0) == 0) & (pl.program_id(1) == 0))
        def _():
            s_ref[...] = jnp.zeros_like(s_ref)

        dm = dm_ref[...]
        sa = _sigmoid(ga_ref[...] + bg_ref[0:1, :])
        sb = _sigmoid(gb_ref[...] + bg_ref[1:2, :])
        dya_ref[...] = (dm * sa).astype(BF16)
        dyb_ref[...] = (dm * sb).astype(BF16)
        dga = dm * a_ref[...] * (sa * (1.0 - sa))
        dgb = dm * b_ref[...] * (sb * (1.0 - sb))
        dg_ref[:, :D_MODEL] = dga.astype(BF16)
        dg_ref[:, D_MODEL:] = dgb.astype(BF16)
        s_ref[0:1, :] += jnp.sum(dga, 0, keepdims=True)
        s_ref[1:2, :] += jnp.sum(dgb, 0, keepdims=True)

    spec = pl.BlockSpec((None, EW_TM, D_MODEL), lambda b, i: (b, i, 0))
    spec1 = pl.BlockSpec((None, EW_TM, D_MODEL), lambda b, i: (b, i, 1))
    small = pl.BlockSpec((8, D_MODEL), lambda b, i: (0, 0))
    return _pcall(
        body, name="merge_bwd", grid=(nb, seq // EW_TM),
        in_specs=[spec, spec, spec, spec, spec1, small],
        out_specs=[spec, spec, pl.BlockSpec((None, EW_TM, 2 * D_MODEL), lambda b, i: (b, i, 0)), small],
        out_shape=[jax.ShapeDtypeStruct((nb, seq, D_MODEL), BF16), jax.ShapeDtypeStruct((nb, seq, D_MODEL), BF16),
                   jax.ShapeDtypeStruct((nb, seq, 2 * D_MODEL), BF16), jax.ShapeDtypeStruct((8, D_MODEL), F32)],
        compiler_params=_params("arbitrary", "arbitrary"),
    )(dmerged, y_a, y_b, gm, gm, bgate)


def _ln_loss(x, mix, gp, pw, target, bgate, ln_g, ln_b):
    nb, seq, _ = x.shape

    def body(x_ref, mix_ref, gp_ref, pw_ref, t_ref, bg_ref, g_ref, b_ref, dx_ref, dp_ref, dpw_ref, dgp_ref, s_ref):
        @pl.when((pl.program_id(0) == 0) & (pl.program_id(1) == 0))
        def _():
            s_ref[...] = jnp.zeros_like(s_ref)

        sp = _sigmoid(gp_ref[...] + bg_ref[2:3, :])
        pw = pw_ref[...]
        pre = ALPHA * x_ref[...] + mix_ref[...] + sp * pw
        mu = jnp.mean(pre, -1, keepdims=True)
        cen = pre - mu
        rstd = lax.rsqrt(jnp.mean(cen * cen, -1, keepdims=True) + LN_EPS)
        xhat = cen * rstd
        err = xhat * g_ref[...] + b_ref[...] - t_ref[...]
        dy = err * (1.0 / D_MODEL)
        dxh = dy * g_ref[...]
        dpre = rstd * (dxh - jnp.mean(dxh, -1, keepdims=True) - xhat * jnp.mean(dxh * xhat, -1, keepdims=True))
        dx_ref[...] = ALPHA * dpre
        dp_ref[...] = dpre.astype(BF16)
        dpw_ref[...] = (dpre * sp).astype(BF16)
        dgp = dpre * pw * (sp * (1.0 - sp))
        dgp_ref[...] = dgp.astype(BF16)
        s_ref[0:1, :] += jnp.sum(dy * xhat, 0, keepdims=True)
        s_ref[1:2, :] += jnp.sum(dy, 0, keepdims=True)
        s_ref[2:3, :] += jnp.sum(dgp, 0, keepdims=True)
        s_ref[3:4, :] += jnp.sum(err * err, 0, keepdims=True)

    spec = pl.BlockSpec((None, EW_TM, D_MODEL), lambda b, i: (b, i, 0))
    small = pl.BlockSpec((8, D_MODEL), lambda b, i: (0, 0))
    row = pl.BlockSpec((1, D_MODEL), lambda b, i: (0, 0))
    return _pcall(
        body, name="ln_loss", grid=(nb, seq // EW_TM),
        in_specs=[spec] * 5 + [small, row, row], out_specs=[spec] * 4 + [small],
        out_shape=[jax.ShapeDtypeStruct((nb, seq, D_MODEL), F32)] + [jax.ShapeDtypeStruct((nb, seq, D_MODEL), BF16)] * 3
        + [jax.ShapeDtypeStruct((8, D_MODEL), F32)],
        compiler_params=_params("arbitrary", "arbitrary"),
    )(x, mix, gp, pw, target, bgate, ln_g, ln_b)


def _adamw(w, g, m, v, name):
    rows, cols = w.shape
    tr = _row_tile(rows, cols, 8, 5 << 19)
    c1 = 1.0 - ADAM_B1 ** ADAM_STEP
    c2 = 1.0 - ADAM_B2 ** ADAM_STEP

    def body(w_ref, g_ref, m_ref, v_ref, d_ref, nm_ref, nv_ref):
        gv = g_ref[...]
        nm = ADAM_B1 * m_ref[...] + (1.0 - ADAM_B1) * gv
        nv = ADAM_B2 * v_ref[...] + (1.0 - ADAM_B2) * (gv * gv)
        d_ref[...] = -ADAM_LR * ((nm / c1) / (jnp.sqrt(nv / c2) + ADAM_EPS) + ADAM_WD * w_ref[...])
        nm_ref[...] = nm
        nv_ref[...] = nv

    spec = pl.BlockSpec((tr, cols), lambda i: (i, 0))
    return _pcall(
        body, name=name, grid=(rows // tr,), in_specs=[spec] * 4, out_specs=[spec] * 3,
        out_shape=[jax.ShapeDtypeStruct(w.shape, F32)] * 3, compiler_params=_params("parallel"),
    )(w, g, m, v)


def _sum_rows(parts, out_dtype, name):
    rows, cols = parts[0].shape
    tr = rows
    for cand in range(16, rows, 16):
        if rows % cand == 0 and cand * cols * 4 <= (1 << 20):
            tr = cand
    n = len(parts)

    def body(*refs):
        acc = refs[0][...].astype(F32)
        for r in refs[1:n]:
            acc = acc + r[...].astype(F32)
        refs[n][...] = acc.astype(out_dtype)

    spec = pl.BlockSpec((tr, cols), lambda i: (i, 0))
    return _pcall(
        body, name=name, grid=(rows // tr,), in_specs=[spec] * n, out_specs=spec,
        out_shape=jax.ShapeDtypeStruct((rows, cols), out_dtype), compiler_params=_params("parallel"),
    )(*parts)


def _place():
    return lax.axis_index("x"), lax.axis_index("y"), lax.axis_index("c")


def _other_chips(x, y):
    return [(1 - x, y), (x, 1 - y), (1 - x, 1 - y)]


def _remote(src, dst, send_sem, recv_sem, to):
    return pltpu.make_async_remote_copy(src_ref=src, dst_ref=dst, send_sem=send_sem, recv_sem=recv_sem,
                                        device_id=to, device_id_type=MESH)


ANY = pl.BlockSpec(memory_space=pl.ANY)
DMA_CHUNK_BYTES = 512 * 1024


def _row_chunks(rows, row_bytes):
    per = max(16, DMA_CHUNK_BYTES // row_bytes // 16 * 16)
    return [(s, min(per, rows - s)) for s in range(0, rows, per)]


def _row_tile(rows, cols, align, limit=1 << 21):
    best = None
    for cand in range(align, rows + 1, align):
        if rows % cand == 0 and cand * cols * 4 <= limit:
            best = cand
    return best or rows


def _allgather_pieces(pieces):
    n = len(pieces)
    halves = [_row_chunks(p.shape[0] // 2, p.shape[1] * p.dtype.itemsize) for p in pieces]
    entries = [(a, q, s, m, j) for a in range(n) for q, (s, m) in enumerate(halves[a]) for j in range(3)]
    slot = {(a, q, j): k for k, (a, q, _, _, j) in enumerate(entries)}
    n_ici = len(entries)

    def body(*refs):
        ins, outs = refs[:n], refs[n:2 * n]
        send_sems, recv_sems = refs[2 * n:]
        x, y, c = _place()
        me = 2 * x + y
        sibling = (x, y, 1 - c)
        chips = _other_chips(x, y)

        def landed(a, s, m, j, core):
            half = ins[a].shape[0] // 2
            return outs[a].at[2 * chips[j][0] + chips[j][1], pl.ds(core * half + s, m)]

        sent = []
        for k, (a, q, s, m, j) in enumerate(entries):
            if j < 2:
                half = ins[a].shape[0] // 2
                cp = _remote(ins[a].at[pl.ds(c * half + s, m)], outs[a].at[me, pl.ds(c * half + s, m)],
                             send_sems.at[k], recv_sems.at[k], (*chips[j], c))
                cp.start()
                sent.append(cp)

        def pass_to_sibling(k, blk):
            fw = _remote(blk, blk, send_sems.at[n_ici + k], recv_sems.at[n_ici + k], sibling)
            fw.start()
            sent.append(fw)

        for k, (a, q, s, m, j) in enumerate(entries):
            if j < 2:
                blk = landed(a, s, m, j, c)
                _remote(blk, blk, send_sems.at[k], recv_sems.at[k], (*chips[j], c)).wait_recv()
                first = q < (len(halves[a]) + 1) // 2
                if (j == 0) == first:
                    on = slot[(a, q, 2)]
                    rl = _remote(blk, blk, send_sems.at[on], recv_sems.at[on], (*chips[1 - j], c))
                    rl.start()
                    sent.append(rl)
                pass_to_sibling(k, blk)
        for k, (a, q, s, m, j) in enumerate(entries):
            if j == 2:
                blk = landed(a, s, m, j, c)
                _remote(blk, blk, send_sems.at[k], recv_sems.at[k], (*chips[j], c)).wait_recv()
                pass_to_sibling(k, blk)
        for k, (a, q, s, m, j) in enumerate(entries):
            blk = landed(a, s, m, j, 1 - c)
            _remote(blk, blk, send_sems.at[n_ici + k], recv_sems.at[n_ici + k], sibling).wait_recv()
        for cp in sent:
            cp.wait_send()

    gathered = _pcall(
        body, name="allgather_weights", in_specs=[ANY] * n, out_specs=[ANY] * n,
        out_shape=[jax.ShapeDtypeStruct((4,) + p.shape, p.dtype) for p in pieces],
        scratch_shapes=[pltpu.SemaphoreType.DMA((2 * n_ici,)), pltpu.SemaphoreType.DMA((2 * n_ici,))],
        compiler_params=pltpu.CompilerParams(has_side_effects=True),
    )(*pieces)
    x, y, _ = _place()
    return [lax.dynamic_update_slice(g, p[None], (2 * x + y, 0, 0)) for g, p in zip(gathered, pieces)]


def _sibling_exchange(grads):
    n = len(grads)
    chunks = [_row_chunks(g.shape[1] // 2, g.shape[2] * g.dtype.itemsize) for g in grads]
    n_sem = 4 * sum(len(ch) for ch in chunks)

    def body(*refs):
        ins, gots = refs[:n], refs[n:2 * n]
        send_sems, recv_sems = refs[2 * n:]
        x, y, c = _place()
        sibling = (x, y, 1 - c)
        work = []
        for a in range(n):
            half = ins[a].shape[1] // 2
            for piece in range(4):
                for s, m in chunks[a]:
                    k = len(work)
                    cp = _remote(ins[a].at[piece, pl.ds((1 - c) * half + s, m)], gots[a].at[piece, pl.ds(s, m)],
                                 send_sems.at[k], recv_sems.at[k], sibling)
                    cp.start()
                    work.append(cp)
        for cp in work:
            cp.wait()

    return _pcall(
        body, name="grad_sibling_exchange", in_specs=[ANY] * n, out_specs=[ANY] * n,
        out_shape=[jax.ShapeDtypeStruct((4, g.shape[1] // 2, g.shape[2]), g.dtype) for g in grads],
        scratch_shapes=[pltpu.SemaphoreType.DMA((n_sem,)), pltpu.SemaphoreType.DMA((n_sem,))],
        compiler_params=pltpu.CompilerParams(has_side_effects=True),
    )(*grads)


def _sibling_gather(fulls):
    n = len(fulls)
    chunks = [_row_chunks(f.shape[0] // 2, f.shape[1] * f.dtype.itemsize) for f in fulls]
    n_sem = sum(len(ch) for ch in chunks)

    def body(*refs):
        outs = refs[n:2 * n]
        send_sems, recv_sems = refs[2 * n:]
        x, y, c = _place()
        sibling = (x, y, 1 - c)
        work = []
        for a in range(n):
            h = outs[a].shape[0] // 2
            for s, m in chunks[a]:
                k = len(work)
                mine = outs[a].at[pl.ds(c * h + s, m)]
                cp = _remote(mine, mine, send_sems.at[k], recv_sems.at[k], sibling)
                cp.start()
                work.append((a, s, m, cp))
        for k, (a, s, m, cp) in enumerate(work):
            h = outs[a].shape[0] // 2
            cp.wait_send()
            theirs = outs[a].at[pl.ds((1 - c) * h + s, m)]
            _remote(theirs, theirs, send_sems.at[k], recv_sems.at[k], sibling).wait_recv()

    return _pcall(
        body, name="grad_sibling_gather", in_specs=[ANY] * n, out_specs=[ANY] * n,
        out_shape=[jax.ShapeDtypeStruct(f.shape, f.dtype) for f in fulls],
        input_output_aliases={a: a for a in range(n)},
        scratch_shapes=[pltpu.SemaphoreType.DMA((n_sem,)), pltpu.SemaphoreType.DMA((n_sem,))],
        compiler_params=pltpu.CompilerParams(has_side_effects=True),
    )(*fulls)


def _pair_sum(grad, got, place, name):
    _, rows, cols = grad.shape
    half = rows // 2
    tr = _row_tile(half, cols, 16)

    def body(p_ref, a_ref, b_ref, o_ref):
        o_ref[...] = (a_ref[...].astype(F32) + b_ref[...].astype(F32)).astype(BF16)

    return _pcall(
        body, name=name,
        grid_spec=pltpu.PrefetchScalarGridSpec(
            num_scalar_prefetch=1, grid=(4, half // tr),
            in_specs=[pl.BlockSpec((None, tr, cols), lambda k, i, p: (k, p[1] * (half // tr) + i, 0)),
                      pl.BlockSpec((None, tr, cols), lambda k, i, p: (k, i, 0))],
            out_specs=pl.BlockSpec((None, tr, cols), lambda k, i, p: (k, i, 0))),
        out_shape=jax.ShapeDtypeStruct((4, half, cols), BF16),
        compiler_params=_params("parallel", "parallel"),
    )(place, grad, got)


def _chip_sum(sums, got, place, name):
    _, h, cols = sums.shape
    tr = _row_tile(h, cols, 16)

    def body(p_ref, own_ref, g0, g1, g2, o_ref):
        o_ref[...] = ((own_ref[...].astype(F32) + g0[...].astype(F32)) + g1[...].astype(F32)) + g2[...].astype(F32)

    gspec = lambda j: pl.BlockSpec((None, tr, cols), lambda i, p: (j, i, 0))
    return _pcall(
        body, name=name,
        grid_spec=pltpu.PrefetchScalarGridSpec(
            num_scalar_prefetch=1, grid=(h // tr,),
            in_specs=[pl.BlockSpec((None, tr, cols), lambda i, p: (p[0], i, 0)), gspec(0), gspec(1), gspec(2)],
            out_specs=pl.BlockSpec((tr, cols), lambda i, p: (p[1] * (h // tr) + i, 0))),
        out_shape=jax.ShapeDtypeStruct((2 * h, cols), F32),
        compiler_params=_params("parallel"),
    )(place, sums, got, got, got)


def _allgather8(buf, name):
    rows = buf.shape[0]

    def body(in_ref, out_ref, send_sems, recv_sems):
        x, y, c = _place()
        me = 4 * x + 2 * y + c
        out_ref[me] = in_ref[...]
        work = []
        for rel in range(1, 8):
            fx, fy, fc = (rel >> 2) & 1, (rel >> 1) & 1, rel & 1
            to = (x ^ fx, y ^ fy, c ^ fc)
            cp = _remote(in_ref, out_ref.at[me], send_sems.at[rel - 1], recv_sems.at[rel - 1], to)
            cp.start()
            work.append((cp, 4 * to[0] + 2 * to[1] + to[2]))
        for rel, (cp, frm) in enumerate(work):
            cp.wait_send()
            blk = out_ref.at[frm]
            _remote(blk, blk, send_sems.at[rel], recv_sems.at[rel], (x, y, c)).wait_recv()

    return _pcall(
        body, name=name, in_specs=[pl.BlockSpec(memory_space=pltpu.VMEM)],
        out_specs=pl.BlockSpec(memory_space=pltpu.VMEM),
        out_shape=jax.ShapeDtypeStruct((8, rows, LANE), F32),
        scratch_shapes=[pltpu.SemaphoreType.DMA((7,)), pltpu.SemaphoreType.DMA((7,))],
        compiler_params=pltpu.CompilerParams(has_side_effects=True),
    )(buf)


def _pack_rows(arrs):
    parts = []
    for a in arrs:
        f = a.reshape(-1).astype(F32)
        parts.append(jnp.pad(f, (0, (-f.shape[0]) % LANE)))
    flat = jnp.concatenate(parts)
    rows = -(-flat.shape[0] // LANE)
    rows8 = -(-rows // 8) * 8
    return jnp.pad(flat, (0, rows8 * LANE - flat.shape[0])).reshape(rows8, LANE)


def _unpack_rows(buf, shapes):
    flat = buf.reshape(-1)
    outs, off = [], 0
    for s in shapes:
        n = int(np.prod(s))
        outs.append(flat[off:off + n].reshape(s))
        off += -(-n // LANE) * LANE
    return outs


def _local_grads(x, p, target, wseg, w_br16, w_out16, w_ple16, b_gate, conv_w, conv_b, dt_bias, a_log, d_skip,
                 ssm_norm_w, ln_g, ln_b, rel_bias, finish_dx):
    nb, seq, _ = x.shape
    bmaps = jnp.asarray(_bucket_maps())
    bias = _bias_tables(rel_bias, bmaps)
    bgate8 = jnp.pad(b_gate, ((0, 5), (0, 0)))
    dils = [d for _, d in PATTERNS]

    x16p = _token_orders(x, dils[1:])
    x16 = x16p[0]
    p16 = p.astype(BF16)
    qkv = [_proj(x16p[g], [wseg["qkv%d" % g]], BF16, "proj_qkv%d" % g, True)[0].reshape(
        nb, dils[g], seq // dils[g], -1) for g in range(3)]
    nat = {}
    for gi, (group, tm) in enumerate(NAT_GROUPS):
        outs = _proj(x16, [wseg[s] for s in group], F32, "proj_nat%d" % gi, True, tm)
        nat.update(zip(group, outs))
    att = [_attn_fwd(qkv[g], bias[g * GROUP_HEADS:(g + 1) * GROUP_HEADS], dils[g], "attn_fwd%d" % g) for g in range(3)]
    oa, o_att, lse = _combine_fwd(att[0][0], att[0][1], att[1:], nat["gatt"])

    conv_wg, conv_bg = _xbc_group_order(conv_w), _xbc_group_order(conv_b)
    act = _conv_fwd(nat["xbc"], conv_wg, conv_bg, "conv_fwd")
    dt_sp, dt_sg = _softplus_sig(nat["dt"], jnp.pad(dt_bias, ((0, 0), (0, LANE - SSM_HEADS))))
    dtg, sgg = _group_lanes(dt_sp), _group_lanes(dt_sg)
    alog_g, dskip_g = _group_lanes(a_log), _group_lanes(d_skip)
    y_ssm, y_all, sprev = _ssd_fwd(act, dtg, nat["z"], alog_g, dskip_g, ssm_norm_w)

    w_bra, w_brb = w_br16[:ATT_OUT], w_br16[ATT_OUT:]
    y_a, = _proj(oa, [w_bra], F32, "proj_ya")
    y_b, = _proj(y_ssm, [w_brb], F32, "proj_yb")
    merged = _merge_fwd(y_a, y_b, nat["gm"], bgate8)
    mix, = _proj(merged, [w_out16], F32, "proj_mix")
    pw, = _proj(p16, [w_ple16], F32, "proj_ple")

    dx, dpre16, dpw16, dgp16, ln_sums = _ln_loss(x, mix, nat["gp"], pw, target, bgate8, ln_g, ln_b)
    loss_sum = (0.5 / D_MODEL) * jnp.sum(ln_sums[3])
    dmerged = _dx([dpre16], [w_out16], [], "dx_merged")
    dya16, dyb16, dgm16, mg_sums = _merge_bwd(dmerged, y_a, y_b, nat["gm"], bgate8)
    doa = _dx([dya16], [w_bra], [], "dx_oa")
    dys = _dx([dyb16], [w_brb], [], "dx_yssm")
    g_w_out, = _dw(merged, [dpre16], BF16, "dw_out")
    g_w_br = jnp.concatenate([_dw(oa, [dya16], BF16, "dw_bra")[0], _dw(y_ssm, [dyb16], BF16, "dw_brb")[0]], axis=0)
    g_w_ple, = _dw(p16, [dpw16], BF16, "dw_ple")

    do_att, dgatt16, own_order = _combine_bwd(doa, nat["gatt"], o_att, lse, dils[1:])
    dseg = {"gatt": dgatt16, "gm": dgm16, "gp": dgp16}
    dbias = []
    for g in range(3):
        cotangent = (do_att, o_att, lse) if g == 0 else (own_order[2 * g - 2], own_order[2 * g - 1])
        dqkv, db = _attn_bwd(qkv[g], bias[g * GROUP_HEADS:(g + 1) * GROUP_HEADS], cotangent, dils[g],
                             "attn_bwd%d" % g)
        dseg["qkv%d" % g] = dqkv.reshape(nb, seq, -1)
        dbias.append(db)
    g_rel = _bias_grad(jnp.concatenate(dbias, axis=0), bmaps)[:, 0, :NUM_BUCKETS].T

    dact, ddtg, dz, ssd_small, g_normw = _ssd_bwd(
        act, dtg, sgg, nat["z"], y_all, dys, sprev, alog_g, dskip_g, ssm_norm_w)
    dseg["z"] = dz
    dseg["dt"] = jnp.pad(_ungroup_lanes(ddtg), ((0, 0), (0, 0), (0, LANE - SSM_HEADS)))
    dpre, conv_sums = _conv_bwd_pre(dact, nat["xbc"], conv_wg, conv_bg, "conv_bwd")
    dseg["xbc"] = _conv_bwd_x(dpre, conv_wg, "conv_bwd_x")
    csum = _xbc_reference_order(conv_sums)

    dx_own = [_dx([dseg["qkv%d" % g]], [wseg["qkv%d" % g]], [], "dx_qkv%d" % g, True).reshape(
        nb, dils[g], seq // dils[g], D_MODEL) for g in (1, 2)]
    dwseg = {"qkv%d" % g: _dw(x16p[g], [dseg["qkv%d" % g]], BF16, "dw_qkv%d" % g, True)[0] for g in range(3)}
    for gi, group in enumerate(DW_GROUPS):
        dwseg.update(zip(group, _dw(x16, [dseg[s] for s in group], BF16, "dw_nat%d" % gi, True)))
    names = ["qkv0"] + [s for group, _ in NAT_GROUPS for s in group]
    dx = finish_dx([dseg[s] for s in names], [wseg[s] for s in names], [dx], dx_own, dwseg, g_w_br, g_w_out, g_w_ple)

    small = dict(
        b_gate=jnp.stack([mg_sums[0], mg_sums[1], ln_sums[2]]),
        conv_w=csum[0:4], conv_b=csum[4:5],
        dt_bias=_ungroup_lanes(ssd_small[:, 2:3, :]), a_log=_ungroup_lanes(ssd_small[:, 0:1, :]),
        d_skip=_ungroup_lanes(ssd_small[:, 1:2, :]), ssm_norm_w=g_normw,
        ln_g=ln_sums[0:1], ln_b=ln_sums[1:2], rel_bias=g_rel)
    return loss_sum, dx, small


DX_TM = 256
SMALL_ORDER = ("b_gate", "conv_w", "conv_b", "dt_bias", "a_log", "d_skip", "ssm_norm_w", "ln_g", "ln_b", "rel_bias")
SMALL_FULL_SHAPES = dict(b_gate=(3, 1024), conv_w=(4, 3072), conv_b=(1, 3072), dt_bias=(1, 32), a_log=(1, 32),
                         d_skip=(1, 32), ssm_norm_w=(1, 2048), ln_g=(1, 1024), ln_b=(1, 1024), rel_bias=(32, 36))


def kernel(x, p, w_in, b_gate, conv_w, conv_b, dt_bias, a_log, d_skip, ssm_norm_w, w_branch, w_out, w_ple, ln_g, ln_b, rel_bias, loss_target, m_w_in, m_b_gate, m_conv_w, m_conv_b, m_dt_bias, m_a_log, m_d_skip, m_ssm_norm_w, m_w_branch, m_w_out, m_w_ple, m_ln_g, m_ln_b, m_rel_bias, v_w_in, v_b_gate, v_conv_w, v_conv_b, v_dt_bias, v_a_log, v_d_skip, v_ssm_norm_w, v_w_branch, v_w_out, v_w_ple, v_ln_g, v_ln_b, v_rel_bias):
    cx, cy, cc = _place()
    chip = 2 * cx + cy
    dev = 4 * cx + 2 * cy + cc

    w_in_t = jnp.transpose(w_in[0])
    win16 = _shard_to_window(w_in_t, chip)
    g_win, g_br, g_out, g_ple = _allgather_pieces(
        [win16, w_branch[0].astype(BF16), w_out[0].astype(BF16), w_ple[0].astype(BF16)])
    wseg = _assemble(g_win)
    w_br16 = g_br.reshape(4 * 704, D_MODEL)
    w_out16 = g_out.reshape(D_MODEL, D_MODEL)
    w_ple16 = jnp.transpose(g_ple, (1, 0, 2)).reshape(PLE_DIM, D_MODEL)
    shards = _allgather8(_pack_rows([b_gate[0], conv_w[0]]), "allgather_small_params")
    per_chip = [_unpack_rows(shards[2 * k], [(3, 256), (4, 768)]) for k in range(4)]
    b_gate_full = jnp.concatenate([pc[0] for pc in per_chip], axis=1)
    conv_w_full = jnp.concatenate([pc[1] for pc in per_chip], axis=1)

    place = jnp.stack([chip, cc]).astype(jnp.int32)
    reduced = []

    def finish_dx(dhs, ws, accs, own_order_accs, dwseg, d_br, d_out, d_ple):
        grads = [_pack(dwseg), d_br.reshape(4, 704, D_MODEL), d_out.reshape(4, 256, D_MODEL),
                 jnp.transpose(d_ple.reshape(PLE_DIM, 4, 256), (1, 0, 2))]
        got = _sibling_exchange(grads)
        chip_sums = [_pair_sum(g, t, place, "grad_pair_sum_%d" % i) for i, (g, t) in enumerate(zip(grads, got))]
        dx, others = _dx(dhs, ws, accs, "dx_w_in_and_grad_chip_scatter", True, DX_TM, chip_sums, own_order_accs)
        fulls = [_chip_sum(s, t, place, "grad_chip_sum_%d" % i) for i, (s, t) in enumerate(zip(chip_sums, others))]
        reduced.extend(_sibling_gather(fulls))
        return dx

    loss_sum, grad_x, small = _local_grads(
        x, p[0], loss_target, wseg, w_br16, w_out16, w_ple16, b_gate_full, conv_w_full, conv_b, dt_bias, a_log,
        d_skip, ssm_norm_w, ln_g, ln_b, rel_bias, finish_dx)
    big = reduced
    g_w_in = _window_to_shard(big[0], chip)
    g_w_branch, g_w_out, g_w_ple = big[1], big[2], big[3]
    parts = _allgather8(_pack_rows([small[n] for n in SMALL_ORDER] + [loss_sum.reshape(1, 1)]),
                        "allgather_small_grads")
    small_sum = _sum_rows([parts[i] for i in range(8)], F32, "small_grad_sum")
    *reduced_small, loss = _unpack_rows(small_sum, [SMALL_FULL_SHAPES[n] for n in SMALL_ORDER] + [(1, 1)])
    loss = loss.reshape(())
    sg = dict(zip(SMALL_ORDER, reduced_small))
    sg["b_gate"] = lax.dynamic_slice_in_dim(sg["b_gate"], chip * 256, 256, axis=1)
    sg["conv_w"] = lax.dynamic_slice_in_dim(sg["conv_w"], chip * 768, 768, axis=1)
    del dev

    upd = {}
    upd["w_in"] = [jnp.transpose(t) for t in _adamw(w_in_t, g_w_in, jnp.transpose(m_w_in[0]),
                                                      jnp.transpose(v_w_in[0]), "adamw_w_in")]
    upd["w_branch"] = _adamw(w_branch[0], g_w_branch, m_w_branch[0], v_w_branch[0], "adamw_w_branch")
    upd["w_out"] = _adamw(w_out[0], g_w_out, m_w_out[0], v_w_out[0], "adamw_w_out")
    upd["w_ple"] = _adamw(w_ple[0], g_w_ple, m_w_ple[0], v_w_ple[0], "adamw_w_ple")
    small_w = dict(b_gate=b_gate, conv_w=conv_w, conv_b=conv_b, dt_bias=dt_bias, a_log=a_log, d_skip=d_skip,
                   ssm_norm_w=ssm_norm_w, ln_g=ln_g, ln_b=ln_b, rel_bias=rel_bias)
    small_m = dict(b_gate=m_b_gate, conv_w=m_conv_w, conv_b=m_conv_b, dt_bias=m_dt_bias, a_log=m_a_log,
                   d_skip=m_d_skip, ssm_norm_w=m_ssm_norm_w, ln_g=m_ln_g, ln_b=m_ln_b, rel_bias=m_rel_bias)
    small_v = dict(b_gate=v_b_gate, conv_w=v_conv_w, conv_b=v_conv_b, dt_bias=v_dt_bias, a_log=v_a_log,
                   d_skip=v_d_skip, ssm_norm_w=v_ssm_norm_w, ln_g=v_ln_g, ln_b=v_ln_b, rel_bias=v_rel_bias)
    shapes = [small_w[n].shape for n in SMALL_ORDER]
    s_delta, s_m, s_v = _adamw(_pack_rows([small_w[n] for n in SMALL_ORDER]), _pack_rows([sg[n] for n in SMALL_ORDER]),
                               _pack_rows([small_m[n] for n in SMALL_ORDER]), _pack_rows([small_v[n] for n in SMALL_ORDER]),
                               "adamw_small")
    for i, n in enumerate(SMALL_ORDER):
        upd[n] = tuple(_unpack_rows(t, shapes)[i] for t in (s_delta, s_m, s_v))
        sg[n] = sg[n].reshape(small_w[n].shape)

    order = ("w_in", "b_gate", "conv_w", "conv_b", "dt_bias", "a_log", "d_skip", "ssm_norm_w", "w_branch", "w_out",
             "w_ple", "ln_g", "ln_b", "rel_bias")
    grads = dict(sg, w_in=jnp.transpose(g_w_in)[None],w_branch=g_w_branch[None], w_out=g_w_out[None], w_ple=g_w_ple[None])
    lead = lambda n, t: t[None] if n in ("w_in", "w_branch", "w_out", "w_ple") else t
    return (loss, grad_x, *[grads[n] for n in order], *[lead(n, upd[n][0]) for n in order],
            *[lead(n, upd[n][1]) for n in order], *[lead(n, upd[n][2]) for n in order])
```

```python
import functools
import math

import numpy as np
import jax
import jax.numpy as jnp
from jax import lax
from jax.experimental import pallas as pl
from jax.experimental.pallas import tpu as pltpu

F32, BF16 = jnp.float32, jnp.bfloat16

D_MODEL = 1024
HEAD_DIM = 64
GROUP_HEADS = 12
ATT_OUT = GROUP_HEADS * HEAD_DIM
PATTERNS = ((128, 1), (512, 4), (2048, 16))
BAND = 128
NUM_BUCKETS = 32
MAX_DISTANCE = 2048
D_INNER = 2048
SSM_HEADS = 32
SSM_GROUPS = 4
GROUP_SSM_HEADS = SSM_HEADS // SSM_GROUPS
D_STATE = 128
CHUNK = 128
PLE_DIM = 256
ALPHA = 2.0 ** 0.25
LN_EPS = 1e-5
RMS_EPS = 1e-5
ADAM_LR, ADAM_B1, ADAM_B2, ADAM_EPS, ADAM_WD, ADAM_STEP = 0.001, 0.9, 0.999, 1e-08, 0.01, 10
NEG = -1e30

QKV_W = 3 * ATT_OUT
IN_COLS = 15904
SHARD_COLS = IN_COLS // 4
DT_COL = 12800
ROW_TILE = 16
WIN_ROWS = 4000


def _win_offset(k):
    return (k * SHARD_COLS) % ROW_TILE


def _win_start(k):
    return k * SHARD_COLS - _win_offset(k)

VMEM_LIMIT_BYTES = 56 * 1024 * 1024
LANE = 128
MESH = pl.DeviceIdType.MESH
NT = (((1,), (1,)), ((), ()))
TN = (((0,), (0,)), ((), ()))


def _pcall(body, **kw):
    return pl.pallas_call(body, **kw)


def _params(*sem):
    return pltpu.CompilerParams(dimension_semantics=sem, vmem_limit_bytes=VMEM_LIMIT_BYTES)


def _sigmoid(v):
    return jax.nn.sigmoid(v)


MM_TM = 512


def _tok_spec(tm, width):
    return pl.BlockSpec((None, tm, width), lambda b, i: (b, i, 0))


def _whole(arr, single_buffer=False):
    mode = dict(pipeline_mode=pl.Buffered(1)) if single_buffer else {}
    return pl.BlockSpec(arr.shape, lambda b, i: (0,) * arr.ndim, **mode)


def _proj(a3, ws, out_dtype, name, w_rows_are_outputs=False, tm=MM_TM):
    nb, seq, kdim = a3.shape
    nw = len(ws)
    widths = [w.shape[0] if w_rows_are_outputs else w.shape[1] for w in ws]

    def body(*refs):
        a = refs[0][...].astype(BF16)
        for w_ref, o_ref in zip(refs[1:1 + nw], refs[1 + nw:]):
            if w_rows_are_outputs:
                v = lax.dot_general(a, w_ref[...], NT, preferred_element_type=F32)
            else:
                v = jnp.dot(a, w_ref[...], preferred_element_type=F32)
            o_ref[...] = v.astype(out_dtype)

    return _pcall(
        body, name=name, grid=(nb, seq // tm),
        in_specs=[_tok_spec(tm, kdim)] + [_whole(w) for w in ws],
        out_specs=[_tok_spec(tm, n) for n in widths],
        out_shape=[jax.ShapeDtypeStruct((nb, seq, n), out_dtype) for n in widths],
        compiler_params=_params("parallel", "parallel"),
    )(a3, *ws)


def _dx(dhs, ws, accs, name, w_rows_are_outputs=False, tm=MM_TM, scatter=None, own_order_accs=()):
    nb, seq, _ = dhs[0].shape
    nd, nacc, npa = len(dhs), len(accs), len(own_order_accs)
    kout = ws[0].shape[1] if w_rows_are_outputs else ws[0].shape[0]
    sums = scatter or []
    ns = len(sums)
    chunks = [_row_chunks(s.shape[1], s.shape[2] * s.dtype.itemsize) for s in sums]
    n_sem = 3 * sum(len(ch) for ch in chunks)
    grid = (nb, seq // tm)
    ntile = kout // LANE if npa else 0

    def body(*refs):
        n_in = 2 * nd + nacc + npa
        sum_refs, o_ref, got_refs = refs[n_in:n_in + ns], refs[n_in + ns], refs[n_in + ns + 1:n_in + 2 * ns + 1]
        tile_refs = refs[n_in + 2 * ns + 1:n_in + 2 * ns + 1 + ntile]

        def copies():
            send_sems, recv_sems = refs[-2], refs[-1]
            x, y, c = _place()
            out = []
            for a in range(ns):
                for s, m in chunks[a]:
                    for j, (cx, cy) in enumerate(_other_chips(x, y)):
                        k = len(out)
                        out.append(_remote(sum_refs[a].at[2 * cx + cy, pl.ds(s, m)], got_refs[a].at[j, pl.ds(s, m)],
                                           send_sems.at[k], recv_sems.at[k], (cx, cy, c)))
            return out

        if ns:
            @pl.when((pl.program_id(0) == 0) & (pl.program_id(1) == 0))
            def _():
                for cp in copies():
                    cp.start()

        v = None
        for dh_ref, w_ref in zip(refs[:nd], refs[nd:2 * nd]):
            dh = dh_ref[...].astype(BF16)
            if w_rows_are_outputs:
                t = jnp.dot(dh, w_ref[...], preferred_element_type=F32)
            else:
                t = lax.dot_general(dh, w_ref[...], NT, preferred_element_type=F32)
            v = t if v is None else v + t
        for a_ref in refs[2 * nd:2 * nd + nacc]:
            v = v + a_ref[...]
        for p_ref in refs[2 * nd + nacc:n_in]:
            v = v + _natural_rows(p_ref, tile_refs)
        o_ref[...] = v

        if ns:
            @pl.when((pl.program_id(0) == grid[0] - 1) & (pl.program_id(1) == grid[1] - 1))
            def _():
                for cp in copies():
                    cp.wait()

    out = _pcall(
        body, name=name, grid=grid,
        in_specs=[_tok_spec(tm, dh.shape[-1]) for dh in dhs] + [_whole(w, bool(ns)) for w in ws]
        + [_tok_spec(tm, kout)] * nacc
        + [pl.BlockSpec((None, p.shape[1], tm // p.shape[1], kout), lambda b, i: (b, 0, i, 0)) for p in own_order_accs]
        + [ANY] * ns,
        out_specs=[_tok_spec(tm, kout)] + [ANY] * ns,
        out_shape=[jax.ShapeDtypeStruct((nb, seq, kout), F32)]
        + [jax.ShapeDtypeStruct((3,) + s.shape[1:], s.dtype) for s in sums],
        input_output_aliases={2 * nd: 0} if nacc else {},
        scratch_shapes=[pltpu.VMEM((tm, LANE), F32)] * ntile
        + ([pltpu.SemaphoreType.DMA((n_sem,)), pltpu.SemaphoreType.DMA((n_sem,))] if ns else []),
        compiler_params=pltpu.CompilerParams(
            dimension_semantics=("arbitrary", "arbitrary") if ns else ("parallel", "parallel"),
            vmem_limit_bytes=VMEM_LIMIT_BYTES, has_side_effects=bool(ns)),
    )(*dhs, *ws, *accs, *own_order_accs, *sums)
    return (out[0], list(out[1:])) if ns else out[0]


DW_TN = 1024


def _dw(a3, dh3, out_dtype, name, rows_are_outputs=False):
    nb, seq, kdim = a3.shape
    n = dh3.shape[-1]
    tn = max(t for t in range(LANE, min(n, DW_TN) + 1, LANE) if n % t == 0)
    grid = (n // tn, nb)
    tile = (tn, kdim) if rows_are_outputs else (kdim, tn)

    def body(a_ref, dh_ref, o_ref, acc_ref):
        b = pl.program_id(1)
        a, dh = a_ref[...].astype(BF16), dh_ref[...].astype(BF16)
        part = lax.dot_general(*((dh, a) if rows_are_outputs else (a, dh)), TN, preferred_element_type=F32)

        @pl.when(b == 0)
        def _():
            acc_ref[...] = part

        @pl.when(b > 0)
        def _():
            acc_ref[...] += part

        @pl.when(b == grid[1] - 1)
        def _():
            o_ref[...] = acc_ref[...].astype(out_dtype)

    return _pcall(
        body, name=name, grid=grid,
        in_specs=[pl.BlockSpec((None, seq, kdim), lambda j, b: (b, 0, 0)),
                  pl.BlockSpec((None, seq, tn), lambda j, b: (b, 0, j))],
        out_specs=pl.BlockSpec(tile, (lambda j, b: (j, 0)) if rows_are_outputs else (lambda j, b: (0, j))),
        out_shape=jax.ShapeDtypeStruct((n, kdim) if rows_are_outputs else (kdim, n), out_dtype),
        scratch_shapes=[pltpu.VMEM(tile, F32)],
        compiler_params=_params("parallel", "arbitrary"),
    )(a3, dh3)


def _qkv_rows(g):
    return [(part * QKV_W + g * ATT_OUT + hp * LANE, LANE) for hp in range(ATT_OUT // LANE) for part in range(3)]


XBC_START = 3 * QKV_W + ATT_OUT + D_INNER
GROUP_CH = GROUP_SSM_HEADS * HEAD_DIM
XBC_GROUP = GROUP_CH + 2 * D_STATE
CONV_DIM = SSM_GROUPS * XBC_GROUP


def _xbc_ranges():
    out = []
    for g in range(SSM_GROUPS):
        out += [(g * GROUP_CH, GROUP_CH), (D_INNER + g * D_STATE, D_STATE),
                (D_INNER + SSM_GROUPS * D_STATE + g * D_STATE, D_STATE)]
    return out


def _xbc_group_order(t):
    return jnp.concatenate([t[..., s:s + n] for s, n in _xbc_ranges()], axis=-1)


def _xbc_reference_order(t):
    g = lambda off, n: [t[..., k * XBC_GROUP + off:k * XBC_GROUP + off + n] for k in range(SSM_GROUPS)]
    return jnp.concatenate(g(0, GROUP_CH) + g(GROUP_CH, D_STATE) + g(GROUP_CH + D_STATE, D_STATE), axis=-1)


def _segments():
    one = lambda name, start, rows: (name, [(start, rows)], max(rows, LANE))
    return [("qkv%d" % g, _qkv_rows(g), QKV_W) for g in range(3)] + [
        one("gatt", 3 * QKV_W, ATT_OUT), one("z", 3 * QKV_W + ATT_OUT, D_INNER),
        ("xbc", [(XBC_START + s, n) for s, n in _xbc_ranges()], CONV_DIM), one("dt", DT_COL, SSM_HEADS),
        one("gm", DT_COL + SSM_HEADS, 2 * D_MODEL), one("gp", DT_COL + SSM_HEADS + 2 * D_MODEL, D_MODEL)]


LAYOUT_TC = 256
NAT_GROUPS = ((("gatt", "z", "dt", "gp"), 512), (("xbc", "gm"), 256))


def _assemble(win):
    segs = _segments()

    def body(win_ref, *outs):
        def pieces(start, rows):
            t, end = start, start + rows
            while t < end:
                k = min(t // SHARD_COLS, 3)
                shard_end = (k + 1) * SHARD_COLS
                if k < 3 and shard_end % ROW_TILE and t == shard_end - shard_end % ROW_TILE:
                    lo = t - _win_start(k)
                    yield win_ref[k, lo:lo + ROW_TILE, :] + win_ref[k + 1, 0:ROW_TILE, :]
                    t += ROW_TILE
                    continue
                upto = min(end, shard_end - shard_end % ROW_TILE if k < 3 else end)
                yield win_ref[k, t - _win_start(k):upto - _win_start(k), :]
                t = upto

        for (_, ranges, total), o_ref in zip(segs, outs):
            off = 0
            for start, rows in ranges:
                for part in pieces(start, rows):
                    o_ref[off:off + part.shape[0], :] = part
                    off += part.shape[0]
            if off < total:
                o_ref[off:total, :] = jnp.zeros((total - off, o_ref.shape[1]), BF16)

    outs = _pcall(
        body, name="assemble_w_in", grid=(D_MODEL // LAYOUT_TC,),
        in_specs=[pl.BlockSpec((4, WIN_ROWS, LAYOUT_TC), lambda i: (0, 0, i))],
        out_specs=[pl.BlockSpec((total, LAYOUT_TC), lambda i: (0, i)) for _, _, total in segs],
        out_shape=[jax.ShapeDtypeStruct((total, D_MODEL), BF16) for _, _, total in segs],
        compiler_params=_params("parallel"),
    )(win)
    return {name: o for (name, _, _), o in zip(segs, outs)}


def _pack(dsegs):
    segs = _segments()

    def body(*refs):
        ins, o_ref = refs[:-1], refs[-1]
        tail = IN_COLS - _win_start(3)
        o_ref[3, tail:, :] = jnp.zeros((WIN_ROWS - tail, o_ref.shape[2]), BF16)
        for (_, ranges, _), s_ref in zip(segs, ins):
            off = 0
            for start, rows in ranges:
                for k in range(4):
                    lo = _win_start(k)
                    a, b = max(start, lo), min(start + rows, lo + WIN_ROWS)
                    if a < b:
                        o_ref[k, a - lo:b - lo, :] = s_ref[off + a - start:off + b - start, :]
                off += rows

    return _pcall(
        body, name="pack_dw_in", grid=(D_MODEL // LAYOUT_TC,),
        in_specs=[pl.BlockSpec((total, LAYOUT_TC), lambda i: (0, i)) for _, _, total in segs],
        out_specs=pl.BlockSpec((4, WIN_ROWS, LAYOUT_TC), lambda i: (0, 0, i)),
        out_shape=jax.ShapeDtypeStruct((4, WIN_ROWS, D_MODEL), BF16),
        compiler_params=_params("parallel"),
    )(*[dsegs[name] for name, _, _ in segs])


def _shard_to_window(shard_t, k):
    def at(off):
        return lambda w: jnp.pad(w.astype(BF16), ((off, WIN_ROWS - SHARD_COLS - off), (0, 0)))

    return lax.cond(k % 2 == 1, at(_win_offset(1)), at(_win_offset(0)), shard_t)


def _window_to_shard(win, k):
    return lax.dynamic_slice(win, ((k % 2) * _win_offset(1), 0), (SHARD_COLS, D_MODEL))


def _bucket_maps():
    qi = np.arange(BAND)[:, None]
    kj = np.arange(2 * BAND)[None, :]
    delta = qi + BAND - kj
    maps = []
    for window, dil in PATTERNS:
        valid = (delta >= 0) & (delta <= window // dil)
        dist = np.maximum(delta, 0) * dil
        max_exact = NUM_BUCKETS // 2
        d_f = np.maximum(dist, 1).astype(np.float32)
        large = max_exact + (np.log(d_f / np.float32(max_exact)) / np.float32(math.log(MAX_DISTANCE / max_exact))
                             * np.float32(NUM_BUCKETS - max_exact)).astype(np.int32)
        large = np.minimum(large, NUM_BUCKETS - 1)
        bucket = np.where(dist < max_exact, dist, large)
        maps.append(np.where(valid, bucket, -1).astype(np.int32))
    return np.stack(maps)


def _bias_tables(rel_bias, bmaps):
    def body(rb_ref, bm_ref, o_ref):
        h = pl.program_id(0)
        bm = bm_ref[...]
        acc = jnp.full(bm.shape, NEG, F32)
        for b in range(NUM_BUCKETS):
            acc = jnp.where(bm == b, rb_ref[b, h], acc)
        o_ref[...] = acc

    return _pcall(
        body, name="bias_tables", grid=(3 * GROUP_HEADS,),
        in_specs=[pl.BlockSpec(memory_space=pltpu.SMEM),
                  pl.BlockSpec((None, BAND, 2 * BAND), lambda h: (h // GROUP_HEADS, 0, 0))],
        out_specs=pl.BlockSpec((None, BAND, 2 * BAND), lambda h: (h, 0, 0)),
        out_shape=jax.ShapeDtypeStruct((3 * GROUP_HEADS, BAND, 2 * BAND), F32),
        compiler_params=_params("parallel"),
    )(rel_bias, bmaps)


def _bias_grad(dbias, bmaps):
    def body(db_ref, bm_ref, o_ref):
        bm = bm_ref[...]
        db = db_ref[...]
        lane = lax.broadcasted_iota(jnp.int32, (1, LANE), 1)
        vec = jnp.zeros((1, LANE), F32)
        for b in range(NUM_BUCKETS):
            s = jnp.sum(jnp.where(bm == b, db, 0.0), keepdims=True)
            vec = jnp.where(lane == b, s, vec)
        o_ref[...] = vec

    return _pcall(
        body, name="bias_grad", grid=(3 * GROUP_HEADS,),
        in_specs=[pl.BlockSpec((None, BAND, 2 * BAND), lambda h: (h, 0, 0)),
                  pl.BlockSpec((None, BAND, 2 * BAND), lambda h: (h // GROUP_HEADS, 0, 0))],
        out_specs=pl.BlockSpec((None, 1, LANE), lambda h: (h, 0, 0)),
        out_shape=jax.ShapeDtypeStruct((3 * GROUP_HEADS, 1, LANE), F32),
        compiler_params=_params("parallel"),
    )(dbias, bmaps)


def _rows(n):
    if isinstance(n, int):
        return pl.ds(n * BAND, BAND)
    return pl.ds(pl.multiple_of(n * BAND, BAND), BAND)


def _for_blocks(blocks, nblk, per, carry):
    carry = blocks([0], carry, False)
    start = 1 + (nblk - 1) % per
    for n in range(1, start):
        carry = blocks([n], carry, True)
    trips = (nblk - start) // per
    if trips > 0:
        carry = lax.fori_loop(
            0, trips, lambda t, c: blocks([start + t * per + u for u in range(per)], c, True), carry)
    return carry


def _pairs_per_step(d):
    return {1: 3, 4: 6, 16: 6}[d]


def _attn_fwd(qkv4, bias, d, name):
    nb, _, sub, _ = qkv4.shape
    nblk = sub // BAND
    scale = HEAD_DIM ** -0.5
    npair = ATT_OUT // LANE
    hps = _pairs_per_step(d)
    compact = d > 1

    def body(qkv_ref, bias_ref, o_ref, l_ref):
        def blocks(ns, carry, with_prev):
            chains = [(bi, i, h) for bi in range(len(ns)) for i in range(hps) for h in range(2)]
            first_head = lax.broadcasted_iota(jnp.int32, (BAND, LANE), 1) < HEAD_DIM
            pair = lambda n, i, part: qkv_ref[_rows(n), (3 * i + part) * LANE:(3 * i + part + 1) * LANE]
            scores = []
            for bi, i, h in chains:
                n = ns[bi]
                qp = pair(n, i, 0) * scale
                q = jnp.where(first_head if h == 0 else jnp.logical_not(first_head), qp, jnp.zeros_like(qp))
                s_c = lax.dot_general(q, pair(n, i, 1), NT, preferred_element_type=F32) + bias_ref[2 * i + h, :, BAND:]
                s_p = None
                if with_prev:
                    s_p = lax.dot_general(q, pair(n - 1, i, 1), NT,
                                          preferred_element_type=F32) + bias_ref[2 * i + h, :, :BAND]
                scores.append((s_c, s_p))
            probs = []
            for s_c, s_p in scores:
                m = jnp.max(s_c, -1, keepdims=True)
                if with_prev:
                    m = jnp.maximum(m, jnp.max(s_p, -1, keepdims=True))
                e_c = jnp.exp(s_c - m)
                den = jnp.sum(e_c, -1, keepdims=True)
                e_p = None
                if with_prev:
                    e_p = jnp.exp(s_p - m)
                    den = den + jnp.sum(e_p, -1, keepdims=True)
                    e_p = e_p.astype(BF16)
                probs.append((e_c.astype(BF16), e_p, den, m))
            outs = {}
            for (bi, i, h), (e_c, e_p, den, m) in zip(chains, probs):
                n = ns[bi]
                acc = jnp.dot(e_c, pair(n, i, 2), preferred_element_type=F32)
                if with_prev:
                    acc = acc + jnp.dot(e_p, pair(n - 1, i, 2), preferred_element_type=F32)
                outs[(bi, i, h)] = (acc / den, m + jnp.log(den))
            lane = lax.broadcasted_iota(jnp.int32, (BAND, LANE), 1)
            for bi, n in enumerate(ns):
                per_head = jnp.zeros((BAND, LANE), F32)
                for i in range(hps):
                    o_ref[_rows(n), i * LANE:(i + 1) * LANE] = jnp.where(first_head, outs[(bi, i, 0)][0],
                                                                         outs[(bi, i, 1)][0])
                    if compact:
                        for h in range(2):
                            per_head = jnp.where(lane == 2 * i + h, outs[(bi, i, h)][1], per_head)
                    else:
                        l_ref[_rows(n), i * LANE:(i + 1) * LANE] = jnp.where(first_head, outs[(bi, i, 0)][1],
                                                                             outs[(bi, i, 1)][1])
                if compact:
                    l_ref[_rows(n), :] = per_head
            return carry

        _for_blocks(blocks, nblk, 2 if hps == 1 else 1, 0)

    in_specs = [pl.BlockSpec((None, None, sub, 3 * LANE * hps), lambda hp, b, r: (b, r, 0, hp)),
                pl.BlockSpec((2 * hps, BAND, 2 * BAND), lambda hp, b, r: (hp, 0, 0))]
    if compact:
        return _pcall(
            body, name=name, grid=(1, nb, d), in_specs=in_specs,
            out_specs=[pl.BlockSpec((None, None, sub, ATT_OUT), lambda hp, b, r: (b, r, 0, 0)),
                       pl.BlockSpec((None, None, sub, LANE), lambda hp, b, r: (b, r, 0, 0))],
            out_shape=[jax.ShapeDtypeStruct((nb, d, sub, ATT_OUT), F32), jax.ShapeDtypeStruct((nb, d, sub, LANE), F32)],
            compiler_params=_params("parallel", "parallel", "parallel"),
        )(qkv4, bias)
    ospec = pl.BlockSpec((None, sub, hps * LANE), lambda hp, b, r: (b, 0, r * (npair // hps) + hp))
    return _pcall(
        body, name=name, grid=(npair // hps, nb, d), in_specs=in_specs, out_specs=[ospec, ospec],
        out_shape=[jax.ShapeDtypeStruct((nb, sub, d * ATT_OUT), F32)] * 2,
        compiler_params=_params("parallel", "parallel", "parallel"),
    )(qkv4, bias)


STAT_LSE_LANE = 16


def _attn_bwd(qkv4, bias, cotangent, d, name):
    nb, _, sub, _ = qkv4.shape
    nblk = sub // BAND
    scale = HEAD_DIM ** -0.5
    npair = ATT_OUT // LANE
    hps = _pairs_per_step(d)
    compact = d > 1

    def body(qkv_ref, bias_ref, *rest):
        do_ref, dqkv_ref, db_ref = rest[0], rest[-2], rest[-1]
        b, r = pl.program_id(1), pl.program_id(2)

        @pl.when((b == 0) & (r == 0))
        def _():
            db_ref[...] = jnp.zeros_like(db_ref)

        def blocks(ns, carry, with_prev):
            sides = (0, 1) if with_prev else (0,)
            chains = [(bi, i, h, sd) for bi in range(len(ns)) for i in range(hps) for h in range(2) for sd in sides]
            first_head = lax.broadcasted_iota(jnp.int32, (BAND, LANE), 1) < HEAD_DIM
            own = lambda h, t: jnp.where(first_head if h == 0 else jnp.logical_not(first_head), t, jnp.zeros_like(t))
            pair = lambda rows, i, part: qkv_ref[rows, (3 * i + part) * LANE:(3 * i + part + 1) * LANE]
            key_rows = lambda bi, sd: _rows(ns[bi] - sd)
            qs = {}
            for bi in range(len(ns)):
                for i in range(hps):
                    q_pair = pair(_rows(ns[bi]), i, 0) * scale
                    do = do_ref[_rows(ns[bi]), i * LANE:(i + 1) * LANE]
                    do16 = do.astype(BF16)
                    for h in range(2):
                        if compact:
                            st_ref, head = rest[1], 2 * i + h
                            ebar = st_ref[_rows(ns[bi]), head:head + 1]
                            lcol = st_ref[_rows(ns[bi]), STAT_LSE_LANE + head:STAT_LSE_LANE + head + 1]
                        else:
                            ebar = jnp.sum(own(h, do * rest[1][_rows(ns[bi]), i * LANE:(i + 1) * LANE]), -1, keepdims=True)
                            lcol = rest[2][_rows(ns[bi]), i * LANE + h * HEAD_DIM:i * LANE + h * HEAD_DIM + 1]
                        qs[(bi, i, h)] = (own(h, q_pair), q_pair, own(h, do16), do16, ebar, lcol)
            raw = []
            for bi, i, h, sd in chains:
                q, _, do_h, _, _, _ = qs[(bi, i, h)]
                bias_blk = bias_ref[2 * i + h, :, :BAND] if sd else bias_ref[2 * i + h, :, BAND:]
                s = lax.dot_general(q, pair(key_rows(bi, sd), i, 1), NT, preferred_element_type=F32) + bias_blk
                dp = lax.dot_general(do_h, pair(key_rows(bi, sd), i, 2), NT, preferred_element_type=F32)
                raw.append((s, dp))
            soft = []
            for (bi, i, h, sd), (s, dp) in zip(chains, raw):
                ebar, lcol = qs[(bi, i, h)][4:]
                p = jnp.exp(s - lcol)
                ds = p * (dp - ebar)
                if sd:
                    db_ref[2 * i + h, :, :BAND] += ds
                else:
                    db_ref[2 * i + h, :, BAND:] += ds
                soft.append((p.astype(BF16), ds.astype(BF16)))
            grads = {}
            for (bi, i, h, sd), (p16, ds16) in zip(chains, soft):
                _, q_pair, _, do16 = qs[(bi, i, h)][:4]
                grads[(bi, i, h, sd)] = (
                    jnp.dot(ds16, pair(key_rows(bi, sd), i, 1), preferred_element_type=F32),
                    lax.dot_general(ds16, q_pair, TN, preferred_element_type=F32),
                    lax.dot_general(p16, do16, TN, preferred_element_type=F32))
            both = lambda bi, i, sd, which: jnp.where(first_head, grads[(bi, i, 0, sd)][which],
                                                      grads[(bi, i, 1, sd)][which])
            carry = list(carry) if carry is not None else None
            for bi, n in enumerate(ns):
                for i in range(hps):
                    base = 3 * LANE * i
                    dq = both(bi, i, 0, 0)
                    if with_prev:
                        dq = dq + both(bi, i, 1, 0)
                        dqkv_ref[_rows(n - 1), base + LANE:base + 2 * LANE] = (
                            carry[2 * i] + both(bi, i, 1, 1)).astype(BF16)
                        dqkv_ref[_rows(n - 1), base + 2 * LANE:base + 3 * LANE] = (
                            carry[2 * i + 1] + both(bi, i, 1, 2)).astype(BF16)
                    dqkv_ref[_rows(n), base:base + LANE] = (dq * scale).astype(BF16)
                carry = [t for i in range(hps) for t in (both(bi, i, 0, 1), both(bi, i, 0, 2))]
            return tuple(carry)

        carry = _for_blocks(blocks, nblk, 2 if hps == 1 else 1, None)
        for i in range(hps):
            base = 3 * LANE * i
            dqkv_ref[_rows(nblk - 1), base + LANE:base + 2 * LANE] = carry[2 * i].astype(BF16)
            dqkv_ref[_rows(nblk - 1), base + 2 * LANE:base + 3 * LANE] = carry[2 * i + 1].astype(BF16)

    qspec = pl.BlockSpec((None, None, sub, 3 * LANE * hps), lambda hp, b, r: (b, r, 0, hp))
    bspec = pl.BlockSpec((2 * hps, BAND, 2 * BAND), lambda hp, b, r: (hp, 0, 0))
    if compact:
        cspecs = [pl.BlockSpec((None, None, sub, ATT_OUT), lambda hp, b, r: (b, r, 0, 0)),
                  pl.BlockSpec((None, None, sub, LANE), lambda hp, b, r: (b, r, 0, 0))]
    else:
        cspecs = [pl.BlockSpec((None, sub, hps * LANE), lambda hp, b, r: (b, 0, r * (npair // hps) + hp))] * 3
    return _pcall(
        body, name=name, grid=(npair // hps, nb, d),
        in_specs=[qspec, bspec] + cspecs, out_specs=[qspec, bspec],
        out_shape=[jax.ShapeDtypeStruct(qkv4.shape, BF16),
                   jax.ShapeDtypeStruct((GROUP_HEADS, BAND, 2 * BAND), F32)],
        compiler_params=_params("parallel", "arbitrary", "arbitrary"),
    )(qkv4, bias, *cotangent)


def _head_lanes(first_lane, one_channel):
    c = lax.broadcasted_iota(jnp.int32, (ATT_OUT, LANE), 0)
    lane = lax.broadcasted_iota(jnp.int32, (ATT_OUT, LANE), 1)
    hit = lane == first_lane + c // HEAD_DIM
    if one_channel:
        hit = hit & (c % HEAD_DIM == 0)
    return hit.astype(BF16)


def _exact_dot(v, m01, dims=None):
    parts = _split3(v)
    if dims is None:
        dot = lambda t: jnp.dot(t, m01, preferred_element_type=F32)
    else:
        dot = lambda t: lax.dot_general(t, m01, dims, preferred_element_type=F32)
    return (dot(parts[0]) + dot(parts[1])) + dot(parts[2])


def _store_own_order(value, tile_refs, out_ref):
    d, per, width = out_ref.shape
    for j in range(width // LANE):
        tile_refs[j][...] = value[:, j * LANE:(j + 1) * LANE]
    for r in range(d):
        rows = pl.ds(r, per, stride=d)
        for j in range(width // LANE):
            out_ref[r, :, j * LANE:(j + 1) * LANE] = tile_refs[j][rows, :].astype(out_ref.dtype)


def _token_orders(x, dilations):
    nb, seq, kdim = x.shape
    tm = 512

    def body(x_ref, nat_ref, *rest):
        outs, tile_refs = rest[:len(dilations)], rest[len(dilations):]
        xv = x_ref[...]
        nat_ref[...] = xv.astype(BF16)
        for o_ref in outs:
            _store_own_order(xv, tile_refs, o_ref)

    outs = _pcall(
        body, name="token_orders", grid=(nb, seq // tm), in_specs=[_tok_spec(tm, kdim)],
        out_specs=[_tok_spec(tm, kdim)]
        + [pl.BlockSpec((None, d, tm // d, kdim), lambda b, i: (b, 0, i, 0)) for d in dilations],
        out_shape=[jax.ShapeDtypeStruct((nb, seq, kdim), BF16)]
        + [jax.ShapeDtypeStruct((nb, d, seq // d, kdim), BF16) for d in dilations],
        scratch_shapes=[pltpu.VMEM((tm, LANE), F32)] * (kdim // LANE),
        compiler_params=_params("parallel", "parallel"),
    )(x)
    return [outs[0]] + [o.reshape(nb, seq, kdim) for o in outs[1:]]


def _natural_rows(p_ref, tile_refs):
    d, per, width = p_ref.shape
    for r in range(d):
        rows = pl.ds(r, per, stride=d)
        for j in range(width // LANE):
            tile_refs[j][rows, :] = p_ref[r, :, j * LANE:(j + 1) * LANE]
    return jnp.concatenate([tile_refs[j][...] for j in range(width // LANE)], axis=1)


def _combine_fwd(o0, l0, dilated, gatt):
    nb, seq, _ = gatt.shape
    tm = 512
    ntile = ATT_OUT // LANE

    def body(o0_ref, l0_ref, o1_ref, l1_ref, o2_ref, l2_ref, g_ref, oa_ref, oatt_ref, lse_ref, *tile_refs):
        spread = _head_lanes(0, False)
        l0v = l0_ref[...]
        l1v = _exact_dot(_natural_rows(l1_ref, tile_refs), spread, NT)
        l2v = _exact_dot(_natural_rows(l2_ref, tile_refs), spread, NT)
        m = jnp.maximum(jnp.maximum(l0v, l1v), l2v)
        tot = m + jnp.log(jnp.exp(l0v - m) + jnp.exp(l1v - m) + jnp.exp(l2v - m))
        o = jnp.exp(l0v - tot) * o0_ref[...]
        o = o + jnp.exp(l1v - tot) * _natural_rows(o1_ref, tile_refs)
        o = o + jnp.exp(l2v - tot) * _natural_rows(o2_ref, tile_refs)
        g = g_ref[...]
        oa_ref[...] = (o * (g * _sigmoid(g))).astype(BF16)
        oatt_ref[...] = o
        lse_ref[...] = tot

    spec = pl.BlockSpec((None, tm, ATT_OUT), lambda b, i: (b, i, 0))
    own = lambda t: pl.BlockSpec((None, t.shape[1], tm // t.shape[1], t.shape[3]), lambda b, i: (b, 0, i, 0))
    (o1, l1), (o2, l2) = dilated
    return _pcall(
        body, name="attn_combine", grid=(nb, seq // tm),
        in_specs=[spec, spec, own(o1), own(l1), own(o2), own(l2), spec], out_specs=[spec] * 3,
        out_shape=[jax.ShapeDtypeStruct((nb, seq, ATT_OUT), BF16), jax.ShapeDtypeStruct((nb, seq, ATT_OUT), F32),
                   jax.ShapeDtypeStruct((nb, seq, ATT_OUT), F32)],
        scratch_shapes=[pltpu.VMEM((tm, LANE), F32)] * ntile,
        compiler_params=_params("parallel", "parallel"),
    )(o0, l0, o1, l1, o2, l2, gatt)


def _combine_bwd(doa, gatt, o_att, lse, dilations):
    nb, seq, _ = gatt.shape
    tm = 512

    def body(doa_ref, g_ref, o_ref, l_ref, do_ref, dg_ref, *rest):
        ntile = ATT_OUT // LANE
        outs, tile_refs = rest[:-ntile], rest[-ntile:]
        g = g_ref[...]
        sg = _sigmoid(g)
        do = doa_ref[...] * (g * sg)
        do_ref[...] = do
        stats = (_exact_dot(do * o_ref[...], _head_lanes(0, False))
                 + _exact_dot(l_ref[...], _head_lanes(STAT_LSE_LANE, True)))
        dg_ref[...] = (doa_ref[...] * o_ref[...] * (sg * (1.0 + g * (1.0 - sg)))).astype(BF16)
        for k in range(len(dilations)):
            _store_own_order(do, tile_refs, outs[2 * k])
            _store_own_order(stats, tile_refs, outs[2 * k + 1])

    spec = pl.BlockSpec((None, tm, ATT_OUT), lambda b, i: (b, i, 0))
    own = lambda d, width: pl.BlockSpec((None, d, tm // d, width), lambda b, i: (b, 0, i, 0))
    outs = _pcall(
        body, name="attn_combine_bwd", grid=(nb, seq // tm), in_specs=[spec] * 4,
        out_specs=[spec, spec] + [own(d, w) for d in dilations for w in (ATT_OUT, LANE)],
        out_shape=[jax.ShapeDtypeStruct((nb, seq, ATT_OUT), F32), jax.ShapeDtypeStruct((nb, seq, ATT_OUT), BF16)]
        + [jax.ShapeDtypeStruct((nb, d, seq // d, w), t) for d in dilations for w, t in ((ATT_OUT, BF16), (LANE, F32))],
        scratch_shapes=[pltpu.VMEM((tm, LANE), F32)] * (ATT_OUT // LANE),
        compiler_params=_params("parallel", "parallel"),
    )(doa, gatt, o_att, lse)
    return outs[0], outs[1], outs[2:]


CONV_TM = 1024
CONV_TC = 1024


def _shift_down(cur, halo, k):
    rolled = pltpu.roll(cur, k, 0)
    hro = pltpu.roll(halo, k, 0)
    row = lax.broadcasted_iota(jnp.int32, hro.shape, 0)
    return jnp.concatenate([jnp.where(row < k, hro, rolled[:8]), rolled[8:]], axis=0)


def _shift_up(cur, halo, k):
    n = cur.shape[0]
    rolled = pltpu.roll(cur, n - k, 0)
    hro = pltpu.roll(halo, 8 - k, 0)
    row = lax.broadcasted_iota(jnp.int32, hro.shape, 0)
    return jnp.concatenate([rolled[:n - 8], jnp.where(row >= 8 - k, hro, rolled[n - 8:])], axis=0)


def _conv_pre(cur, halo, w_ref, b_ref):
    acc = cur * w_ref[3:4, :] + b_ref[...]
    for k in range(1, 4):
        acc = acc + _shift_down(cur, halo, k) * w_ref[3 - k:4 - k, :]
    return acc


def _conv_specs(seq):
    nblk = seq // CONV_TM
    cur = pl.BlockSpec((None, CONV_TM, CONV_TC), lambda cb, b, i: (b, i, cb))
    prev = pl.BlockSpec((None, 8, CONV_TC), lambda cb, b, i: (b, jnp.maximum(i * (CONV_TM // 8) - 1, 0), cb))
    nxt = pl.BlockSpec((None, 8, CONV_TC),
                       lambda cb, b, i: (b, jnp.minimum((i + 1) * (CONV_TM // 8), seq // 8 - 1), cb))
    wspec = pl.BlockSpec((4, CONV_TC), lambda cb, b, i: (0, cb))
    bspec = pl.BlockSpec((1, CONV_TC), lambda cb, b, i: (0, cb))
    return nblk, cur, prev, nxt, wspec, bspec


def _conv_fwd(xin, w4, bias, name):
    nb, seq, ch = xin.shape
    _, cur, prev, _, wspec, bspec = _conv_specs(seq)

    def body(x_ref, h_ref, w_ref, b_ref, o_ref):
        halo = jnp.where(pl.program_id(2) > 0, h_ref[...], 0.0)
        pre = _conv_pre(x_ref[...], halo, w_ref, b_ref)
        o_ref[...] = pre * _sigmoid(pre)

    return _pcall(
        body, name=name, grid=(ch // CONV_TC, nb, seq // CONV_TM),
        in_specs=[cur, prev, wspec, bspec], out_specs=cur,
        out_shape=jax.ShapeDtypeStruct(xin.shape, F32),
        compiler_params=_params("parallel", "parallel", "parallel"),
    )(xin, xin, w4, bias)


def _conv_bwd_pre(dact, xin, w4, bias, name):
    nb, seq, ch = xin.shape
    _, cur, prev, _, wspec, bspec = _conv_specs(seq)

    def body(da_ref, x_ref, h_ref, w_ref, b_ref, dp_ref, s_ref):
        b, i = pl.program_id(1), pl.program_id(2)

        @pl.when((b == 0) & (i == 0))
        def _():
            s_ref[...] = jnp.zeros_like(s_ref)

        halo = jnp.where(i > 0, h_ref[...], 0.0)
        x = x_ref[...]
        pre = _conv_pre(x, halo, w_ref, b_ref)
        sg = _sigmoid(pre)
        dpre = da_ref[...] * (sg * (1.0 + pre * (1.0 - sg)))
        dp_ref[...] = dpre
        s_ref[3:4, :] += jnp.sum(dpre * x, 0, keepdims=True)
        for k in range(1, 4):
            s_ref[3 - k:4 - k, :] += jnp.sum(dpre * _shift_down(x, halo, k), 0, keepdims=True)
        s_ref[4:5, :] += jnp.sum(dpre, 0, keepdims=True)

    return _pcall(
        body, name=name, grid=(ch // CONV_TC, nb, seq // CONV_TM),
        in_specs=[cur, cur, prev, wspec, bspec],
        out_specs=[cur, pl.BlockSpec((8, CONV_TC), lambda cb, b, i: (0, cb))],
        out_shape=[jax.ShapeDtypeStruct(xin.shape, F32), jax.ShapeDtypeStruct((8, ch), F32)],
        compiler_params=_params("parallel", "arbitrary", "arbitrary"),
    )(dact, xin, xin, w4, bias)


def _conv_bwd_x(dpre, w4, name):
    nb, seq, ch = dpre.shape
    nblk, cur, _, nxt, wspec, _ = _conv_specs(seq)

    def body(d_ref, n_ref, w_ref, o_ref):
        halo = jnp.where(pl.program_id(2) < nblk - 1, n_ref[...], 0.0)
        cur_v = d_ref[...]
        acc = cur_v * w_ref[3:4, :]
        for j in range(1, 4):
            acc = acc + _shift_up(cur_v, halo, j) * w_ref[3 - j:4 - j, :]
        o_ref[...] = acc.astype(BF16)

    return _pcall(
        body, name=name, grid=(ch // CONV_TC, nb, seq // CONV_TM),
        in_specs=[cur, nxt, wspec], out_specs=cur,
        out_shape=jax.ShapeDtypeStruct(dpre.shape, BF16),
        compiler_params=_params("parallel", "parallel", "parallel"),
    )(dpre, dpre, w4)


def _softplus_sig(dt_raw, dt_bias_row):
    nb, seq, _ = dt_raw.shape
    tm = 512

    def body(r_ref, b_ref, sp_ref, sg_ref):
        v = r_ref[...] + b_ref[...]
        sp_ref[...] = jnp.maximum(v, 0.0) + jnp.log1p(jnp.exp(-jnp.abs(v)))
        sg_ref[...] = _sigmoid(v)

    spec = pl.BlockSpec((None, tm, LANE), lambda b, i: (b, i, 0))
    return _pcall(
        body, name="dt_softplus", grid=(nb, seq // tm),
        in_specs=[spec, pl.BlockSpec((1, LANE), lambda b, i: (0, 0))], out_specs=[spec, spec],
        out_shape=[jax.ShapeDtypeStruct(dt_raw.shape, F32)] * 2,
        compiler_params=_params("parallel", "parallel"),
    )(dt_raw, dt_bias_row)


def _group_lanes(t):
    pads = [(0, 0)] * (t.ndim - 1) + [(0, LANE - GROUP_SSM_HEADS)]
    return jnp.stack([jnp.pad(t[..., GROUP_SSM_HEADS * g:GROUP_SSM_HEADS * (g + 1)], pads) for g in range(SSM_GROUPS)])


def _ungroup_lanes(t):
    return jnp.concatenate([t[g][..., :GROUP_SSM_HEADS] for g in range(SSM_GROUPS)], axis=-1)


def _decays(dt, al_ref):
    row = lax.broadcasted_iota(jnp.int32, (CHUNK, CHUNK), 0)
    col = lax.broadcasted_iota(jnp.int32, (CHUNK, CHUNK), 1)
    tril = (row >= col).astype(BF16)
    triu = (row <= col).astype(BF16)
    arow = -jnp.exp(al_ref[...])
    hi, mid, lo = _split3(dt * arow)
    down = lambda t: jnp.dot(tril, t, preferred_element_type=F32)
    across = lambda t: lax.dot_general(t, triu, TN, preferred_element_type=F32)
    acs = (down(hi) + down(mid)) + down(lo)
    acs_t = (across(hi) + across(mid)) + across(lo)
    return arow, acs, acs_t, row >= col, triu


STEP_CHUNKS = 8


def _ssd_specs(nb, seq):
    nc = seq // CHUNK
    hw = GROUP_SSM_HEADS * HEAD_DIM
    rows, steps = STEP_CHUNKS * CHUNK, nc // STEP_CHUNKS

    def mk(rev):
        cidx = (lambda c: steps - 1 - c) if rev else (lambda c: c)
        wide = pl.BlockSpec((None, rows, hw), lambda g, b, c: (b, cidx(c), g))
        xbc = pl.BlockSpec((None, rows, XBC_GROUP), lambda g, b, c: (b, cidx(c), g))
        lanes = pl.BlockSpec((None, None, rows, LANE), lambda g, b, c: (g, b, cidx(c), 0))
        prev = pl.BlockSpec((None, STEP_CHUNKS, None, D_STATE, hw), lambda g, b, c: (b, cidx(c), g, 0, 0))
        return wide, xbc, lanes, prev

    grow = pl.BlockSpec((None, 1, LANE), lambda g, b, c: (g, 0, 0))
    nwspec = pl.BlockSpec((1, hw), lambda g, b, c: (0, g))
    return nc, steps, hw, mk, grow, nwspec


def _head_expand():
    hw = GROUP_SSM_HEADS * HEAD_DIM
    r = lax.broadcasted_iota(jnp.int32, (LANE, hw), 0)
    c = lax.broadcasted_iota(jnp.int32, (LANE, hw), 1)
    return ((c // HEAD_DIM) == r).astype(BF16)


def _split3(v):
    hi = v.astype(BF16)
    rest = v - hi.astype(F32)
    mid = rest.astype(BF16)
    return hi, mid, (rest - mid.astype(F32)).astype(BF16)


def _to_channels(v, e):
    hi, mid, lo = _split3(v)
    dot = lambda t: jnp.dot(t, e, preferred_element_type=F32)
    return (dot(hi) + dot(mid)) + dot(lo)


def _to_heads(w, e):
    hi, mid, lo = _split3(w)
    dot = lambda t: lax.dot_general(t, e, (((1,), (1,)), ((), ())), preferred_element_type=F32)
    return (dot(hi) + dot(mid)) + dot(lo)


def _row8(v):
    return jnp.broadcast_to(v, (8, v.shape[1]))


def _ssd_chunk_setup(dt, al_ref, ds_ref):
    arow, acs, acs_t, causal, triu = _decays(dt, al_ref)
    e = _head_expand()
    dtx = _to_channels(dt, e)
    acsx = _to_channels(acs, e)
    lastx = acsx[CHUNK - 1:CHUNK, :]
    dskx = _to_channels(_row8(ds_ref[...]), e)[0:1, :]
    return arow, acs, acs_t, causal, triu, e, dtx, acsx, lastx, dskx


def _ssd_fwd(xbc, dtg, z, alog_g, dskip_g, normw):
    nb, seq, _ = xbc.shape
    nc, steps, hw, mk, grow, nwspec = _ssd_specs(nb, seq)
    wide, xbc_spec, lanes, prev = mk(False)
    tn = (((0,), (0,)), ((), ()))

    def body(xbc_ref, dt_ref, z_ref, al_ref, ds_ref, nw_ref, ys_ref, y_ref, sp_ref, st_ref):
        @pl.when(pl.program_id(2) == 0)
        def _():
            st_ref[...] = jnp.zeros_like(st_ref)

        for ci in range(STEP_CHUNKS):
            chunk(ci, xbc_ref, dt_ref, z_ref, al_ref, ds_ref, nw_ref, ys_ref, y_ref, sp_ref, st_ref)

    def chunk(ci, xbc_ref, dt_ref, z_ref, al_ref, ds_ref, nw_ref, ys_ref, y_ref, sp_ref, st_ref):
        rows = slice(ci * CHUNK, (ci + 1) * CHUNK)
        dt = dt_ref[rows, :]
        _, acs, acs_t, causal, _, _, dtx, acsx, lastx, dskx = _ssd_chunk_setup(dt, al_ref, ds_ref)
        bmat = xbc_ref[rows, GROUP_CH:GROUP_CH + D_STATE].astype(BF16)
        cmat = xbc_ref[rows, GROUP_CH + D_STATE:].astype(BF16)
        cb = lax.dot_general(cmat, bmat, (((1,), (1,)), ((), ())), preferred_element_type=F32)
        x = xbc_ref[rows, :GROUP_CH]
        xdt = x * dtx
        xdt16 = xdt.astype(BF16)
        first_head = lax.broadcasted_iota(jnp.int32, (CHUNK, LANE), 1) < HEAD_DIM
        pairs = []
        for hp in range(GROUP_SSM_HEADS // 2):
            xp = xdt16[:, hp * LANE:(hp + 1) * LANE]
            two = []
            for j in (2 * hp, 2 * hp + 1):
                lmat = jnp.exp(jnp.where(causal, acs[:, j:j + 1] - acs_t[j:j + 1, :], -jnp.inf))
                two.append(jnp.dot((cb * lmat).astype(BF16), xp, preferred_element_type=F32))
            pairs.append(jnp.where(first_head, two[0], two[1]))
        yd = jnp.concatenate(pairs, axis=1)
        s_prev = st_ref[...]
        s16 = s_prev.astype(BF16)
        sp_ref[ci] = s16
        yo = jnp.dot(cmat, s16, preferred_element_type=F32) * jnp.exp(acsx)
        sts = lax.dot_general(bmat, (xdt * jnp.exp(lastx - acsx)).astype(BF16), tn, preferred_element_type=F32)
        st_ref[...] = s_prev * jnp.exp(lastx) + sts
        y = yd + yo + dskx * x
        zz = z_ref[rows, :]
        u = y * (zz * _sigmoid(zz))
        rn = lax.rsqrt(jnp.mean(u * u, -1, keepdims=True) + RMS_EPS)
        ys_ref[rows, :] = (u * rn * nw_ref[...]).astype(BF16)
        y_ref[rows, :] = y

    return _pcall(
        body, name="ssd_fwd", grid=(SSM_GROUPS, nb, steps),
        in_specs=[xbc_spec, lanes, wide, grow, grow, nwspec],
        out_specs=[wide, wide, prev],
        out_shape=[jax.ShapeDtypeStruct((nb, seq, D_INNER), BF16), jax.ShapeDtypeStruct((nb, seq, D_INNER), F32),
                   jax.ShapeDtypeStruct((nb, nc, SSM_GROUPS, D_STATE, hw), BF16)],
        scratch_shapes=[pltpu.VMEM((D_STATE, hw), F32)],
        compiler_params=_params("parallel", "parallel", "arbitrary"),
    )(xbc, dtg, z, alog_g, dskip_g, normw)


def _ssd_bwd(xbc, dtg, sgg, z, y, dys, sprev, alog_g, dskip_g, normw):
    nb, seq, _ = xbc.shape
    nc, steps, hw, mk, grow, nwspec = _ssd_specs(nb, seq)
    wide, xbc_spec, lanes, prev = mk(True)
    nt = (((1,), (1,)), ((), ()))
    tn = (((0,), (0,)), ((), ()))

    def body(xbc_ref, dt_ref, sg_ref, z_ref, y_ref, dys_ref, sp_ref, al_ref, ds_ref, nw_ref,
             dxbc_ref, ddt_ref, dz_ref, small_ref, dnw_ref, g_ref):
        b, c = pl.program_id(1), pl.program_id(2)

        @pl.when((b == 0) & (c == 0))
        def _():
            small_ref[...] = jnp.zeros_like(small_ref)
            dnw_ref[...] = jnp.zeros_like(dnw_ref)

        @pl.when(c == 0)
        def _():
            g_ref[...] = jnp.zeros_like(g_ref)

        for ci in reversed(range(STEP_CHUNKS)):
            chunk(ci, xbc_ref, dt_ref, sg_ref, z_ref, y_ref, dys_ref, sp_ref, al_ref, ds_ref, nw_ref,
                  dxbc_ref, ddt_ref, dz_ref, small_ref, dnw_ref, g_ref)

    def chunk(ci, xbc_ref, dt_ref, sg_ref, z_ref, y_ref, dys_ref, sp_ref, al_ref, ds_ref, nw_ref,
              dxbc_ref, ddt_ref, dz_ref, small_ref, dnw_ref, g_ref):
        rows = slice(ci * CHUNK, (ci + 1) * CHUNK)
        yv, zz, dys_v, nw = y_ref[rows, :], z_ref[rows, :], dys_ref[rows, :], nw_ref[...]
        sz = _sigmoid(zz)
        silu = zz * sz
        u = yv * silu
        rn = lax.rsqrt(jnp.mean(u * u, -1, keepdims=True) + RMS_EPS)
        gn = dys_v * nw
        du = rn * gn - u * (rn * rn * rn) * jnp.mean(u * gn, -1, keepdims=True)
        dnw_ref[...] += jnp.sum(dys_v * u * rn, 0, keepdims=True)
        dy = du * silu
        dz_ref[rows, :] = du * yv * (sz * (1.0 + zz * (1.0 - sz)))

        dt = dt_ref[rows, :]
        arow, acs, acs_t, causal, triu, e, dtx, acsx, lastx, dskx = _ssd_chunk_setup(dt, al_ref, ds_ref)
        dfsx = jnp.exp(acsx)
        dtex = jnp.exp(lastx - acsx)
        bmat = xbc_ref[rows, GROUP_CH:GROUP_CH + D_STATE].astype(BF16)
        cmat = xbc_ref[rows, GROUP_CH + D_STATE:].astype(BF16)
        cb = lax.dot_general(cmat, bmat, nt, preferred_element_type=F32)
        x = xbc_ref[rows, :GROUP_CH]
        xdt = x * dtx
        xdt16 = xdt.astype(BF16)
        xdte = xdt * dtex
        dy16 = dy.astype(BF16)
        dyd = dy * dfsx
        dyd16 = dyd.astype(BF16)
        s16 = sp_ref[ci]
        g = g_ref[...]
        g16 = g.astype(BF16)
        cs = jnp.dot(cmat, s16, preferred_element_type=F32)
        dc_off = lax.dot_general(dyd16, s16, nt, preferred_element_type=F32)
        g_here = lax.dot_general(cmat, dyd16, tn, preferred_element_type=F32)
        bg = jnp.dot(bmat, g16, preferred_element_type=F32)
        db_st = lax.dot_general(xdte.astype(BF16), g16, nt, preferred_element_type=F32)
        ddte_w = bg * xdte
        dcd = _to_heads(_row8(jnp.sum(g * s16.astype(F32), 0, keepdims=True)), e)[0:1, :]
        lane = lax.broadcasted_iota(jnp.int32, (CHUNK, LANE), 1)
        first_head = lane < HEAD_DIM
        sub = lax.broadcasted_iota(jnp.int32, (CHUNK, LANE), 0)
        dacs = jnp.zeros((CHUNK, LANE), F32)
        colsums = jnp.zeros((CHUNK, LANE), F32)
        dcb = jnp.zeros((CHUNK, CHUNK), F32)
        pairs = []
        for hp in range(GROUP_SSM_HEADS // 2):
            xp = xdt16[:, hp * LANE:(hp + 1) * LANE]
            dyp = dy16[:, hp * LANE:(hp + 1) * LANE]
            two = []
            for idx, j in enumerate((2 * hp, 2 * hp + 1)):
                lmat = jnp.exp(jnp.where(causal, acs[:, j:j + 1] - acs_t[j:j + 1, :], -jnp.inf))
                mf = cb * lmat
                dy_h = jnp.where(first_head if idx == 0 else jnp.logical_not(first_head), dyp, jnp.zeros_like(dyp))
                dm = lax.dot_general(dy_h, xp, nt, preferred_element_type=F32)
                two.append(lax.dot_general(mf.astype(BF16), dyp, tn, preferred_element_type=F32))
                wmat = dm * mf
                dcb = dcb + dm * lmat
                dacs = jnp.where(lane == j, jnp.sum(wmat, -1, keepdims=True), dacs)
                colsums = jnp.where(sub == j, jnp.sum(wmat, 0, keepdims=True), colsums)
            pairs.append(jnp.where(first_head, two[0], two[1]))
        dxdt = bg * dtex + jnp.concatenate(pairs, axis=1)
        dacs = dacs - colsums.T + _to_heads(dyd * cs - ddte_w, e)
        cd_row = jnp.exp(acs[CHUNK - 1:CHUNK, :])
        tail = _to_heads(_row8(jnp.sum(ddte_w, 0, keepdims=True)), e)[0:1, :] + dcd * cd_row
        dacs = dacs + jnp.where(sub == CHUNK - 1, tail, 0.0)
        d_hi, d_mid, d_lo = _split3(dacs)
        up = lambda t: jnp.dot(triu, t, preferred_element_type=F32)
        da = (up(d_hi) + up(d_mid)) + up(d_lo)
        ddt_raw = (da * arow + _to_heads(dxdt * x, e)) * sg_ref[rows, :]
        ddt_ref[rows, :] = ddt_raw
        small_ref[0:1, :] += jnp.sum(da * dt, 0, keepdims=True) * arow
        small_ref[1:2, :] += _to_heads(_row8(jnp.sum(dy * x, 0, keepdims=True)), e)[0:1, :]
        small_ref[2:3, :] += jnp.sum(ddt_raw, 0, keepdims=True)
        dcb16 = dcb.astype(BF16)
        dxbc_ref[rows, GROUP_CH + D_STATE:] = dc_off + jnp.dot(dcb16, bmat, preferred_element_type=F32)
        dxbc_ref[rows, GROUP_CH:GROUP_CH + D_STATE] = db_st + lax.dot_general(dcb16, cmat, tn,
                                                                               preferred_element_type=F32)
        dxbc_ref[rows, :GROUP_CH] = dxdt * dtx + dskx * dy
        g_ref[...] = g * jnp.exp(lastx) + g_here

    return _pcall(
        body, name="ssd_bwd", grid=(SSM_GROUPS, nb, steps),
        in_specs=[xbc_spec, lanes, lanes, wide, wide, wide, prev, grow, grow, nwspec],
        out_specs=[xbc_spec, lanes, wide,
                   pl.BlockSpec((None, 8, LANE), lambda g, b, c: (g, 0, 0)), nwspec],
        out_shape=[jax.ShapeDtypeStruct((nb, seq, CONV_DIM), F32),
                   jax.ShapeDtypeStruct((SSM_GROUPS, nb, seq, LANE), F32),
                   jax.ShapeDtypeStruct((nb, seq, D_INNER), F32),
                   jax.ShapeDtypeStruct((SSM_GROUPS, 8, LANE), F32),
                   jax.ShapeDtypeStruct((1, D_INNER), F32)],
        scratch_shapes=[pltpu.VMEM((D_STATE, hw), F32)],
        compiler_params=_params("parallel", "arbitrary", "arbitrary"),
    )(xbc, dtg, sgg, z, y, dys, sprev, alog_g, dskip_g, normw)


EW_TM = 256


def _merge_fwd(y_a, y_b, gm, bgate):
    nb, seq, _ = y_a.shape

    def body(a_ref, b_ref, ga_ref, gb_ref, bg_ref, o_ref):
        sa = _sigmoid(ga_ref[...] + bg_ref[0:1, :])
        sb = _sigmoid(gb_ref[...] + bg_ref[1:2, :])
        o_ref[...] = (sa * a_ref[...] + sb * b_ref[...]).astype(BF16)

    spec = pl.BlockSpec((None, EW_TM, D_MODEL), lambda b, i: (b, i, 0))
    spec1 = pl.BlockSpec((None, EW_TM, D_MODEL), lambda b, i: (b, i, 1))
    return _pcall(
        body, name="merge_fwd", grid=(nb, seq // EW_TM),
        in_specs=[spec, spec, spec, spec1, pl.BlockSpec((8, D_MODEL), lambda b, i: (0, 0))], out_specs=spec,
        out_shape=jax.ShapeDtypeStruct((nb, seq, D_MODEL), BF16),
        compiler_params=_params("parallel", "parallel"),
    )(y_a, y_b, gm, gm, bgate)


def _merge_bwd(dmerged, y_a, y_b, gm, bgate):
    nb, seq, _ = y_a.shape

    def body(dm_ref, a_ref, b_ref, ga_ref, gb_ref, bg_ref, dya_ref, dyb_ref, dg_ref, s_ref):
        @pl.when((pl.program_id(0) == 0) & (pl.program_id(1) == 0))
        def _():
            s_ref[...] = jnp.zeros_like(s_ref)

        dm = dm_ref[...]
        sa = _sigmoid(ga_ref[...] + bg_ref[0:1, :])
        sb = _sigmoid(gb_ref[...] + bg_ref[1:2, :])
        dya_ref[...] = (dm * sa).astype(BF16)
        dyb_ref[...] = (dm * sb).astype(BF16)
        dga = dm * a_ref[...] * (sa * (1.0 - sa))
        dgb = dm * b_ref[...] * (sb * (1.0 - sb))
        dg_ref[:, :D_MODEL] = dga.astype(BF16)
        dg_ref[:, D_MODEL:] = dgb.astype(BF16)
        s_ref[0:1, :] += jnp.sum(dga, 0, keepdims=True)
        s_ref[1:2, :] += jnp.sum(dgb, 0, keepdims=True)

    spec = pl.BlockSpec((None, EW_TM, D_MODEL), lambda b, i: (b, i, 0))
    spec1 = pl.BlockSpec((None, EW_TM, D_MODEL), lambda b, i: (b, i, 1))
    small = pl.BlockSpec((8, D_MODEL), lambda b, i: (0, 0))
    return _pcall(
        body, name="merge_bwd", grid=(nb, seq // EW_TM),
        in_specs=[spec, spec, spec, spec, spec1, small],
        out_specs=[spec, spec, pl.BlockSpec((None, EW_TM, 2 * D_MODEL), lambda b, i: (b, i, 0)), small],
        out_shape=[jax.ShapeDtypeStruct((nb, seq, D_MODEL), BF16), jax.ShapeDtypeStruct((nb, seq, D_MODEL), BF16),
                   jax.ShapeDtypeStruct((nb, seq, 2 * D_MODEL), BF16), jax.ShapeDtypeStruct((8, D_MODEL), F32)],
        compiler_params=_params("arbitrary", "arbitrary"),
    )(dmerged, y_a, y_b, gm, gm, bgate)


def _ln_loss(x, mix, gp, pw, target, bgate, ln_g, ln_b):
    nb, seq, _ = x.shape

    def body(x_ref, mix_ref, gp_ref, pw_ref, t_ref, bg_ref, g_ref, b_ref, dx_ref, dp_ref, dpw_ref, dgp_ref, s_ref):
        @pl.when((pl.program_id(0) == 0) & (pl.program_id(1) == 0))
        def _():
            s_ref[...] = jnp.zeros_like(s_ref)

        sp = _sigmoid(gp_ref[...] + bg_ref[2:3, :])
        pw = pw_ref[...]
        pre = ALPHA * x_ref[...] + mix_ref[...] + sp * pw
        mu = jnp.mean(pre, -1, keepdims=True)
        cen = pre - mu
        rstd = lax.rsqrt(jnp.mean(cen * cen, -1, keepdims=True) + LN_EPS)
        xhat = cen * rstd
        err = xhat * g_ref[...] + b_ref[...] - t_ref[...]
        dy = err * (1.0 / D_MODEL)
        dxh = dy * g_ref[...]
        dpre = rstd * (dxh - jnp.mean(dxh, -1, keepdims=True) - xhat * jnp.mean(dxh * xhat, -1, keepdims=True))
        dx_ref[...] = ALPHA * dpre
        dp_ref[...] = dpre.astype(BF16)
        dpw_ref[...] = (dpre * sp).astype(BF16)
        dgp = dpre * pw * (sp * (1.0 - sp))
        dgp_ref[...] = dgp.astype(BF16)
        s_ref[0:1, :] += jnp.sum(dy * xhat, 0, keepdims=True)
        s_ref[1:2, :] += jnp.sum(dy, 0, keepdims=True)
        s_ref[2:3, :] += jnp.sum(dgp, 0, keepdims=True)
        s_ref[3:4, :] += jnp.sum(err * err, 0, keepdims=True)

    spec = pl.BlockSpec((None, EW_TM, D_MODEL), lambda b, i: (b, i, 0))
    small = pl.BlockSpec((8, D_MODEL), lambda b, i: (0, 0))
    row = pl.BlockSpec((1, D_MODEL), lambda b, i: (0, 0))
    return _pcall(
        body, name="ln_loss", grid=(nb, seq // EW_TM),
        in_specs=[spec] * 5 + [small, row, row], out_specs=[spec] * 4 + [small],
        out_shape=[jax.ShapeDtypeStruct((nb, seq, D_MODEL), F32)] + [jax.ShapeDtypeStruct((nb, seq, D_MODEL), BF16)] * 3
        + [jax.ShapeDtypeStruct((8, D_MODEL), F32)],
        compiler_params=_params("arbitrary", "arbitrary"),
    )(x, mix, gp, pw, target, bgate, ln_g, ln_b)


def _adamw(w, g, m, v, name):
    rows, cols = w.shape
    tr = _row_tile(rows, cols, 8, 5 << 19)
    c1 = 1.0 - ADAM_B1 ** ADAM_STEP
    c2 = 1.0 - ADAM_B2 ** ADAM_STEP

    def body(w_ref, g_ref, m_ref, v_ref, d_ref, nm_ref, nv_ref):
        gv = g_ref[...]
        nm = ADAM_B1 * m_ref[...] + (1.0 - ADAM_B1) * gv
        nv = ADAM_B2 * v_ref[...] + (1.0 - ADAM_B2) * (gv * gv)
        d_ref[...] = -ADAM_LR * ((nm / c1) / (jnp.sqrt(nv / c2) + ADAM_EPS) + ADAM_WD * w_ref[...])
        nm_ref[...] = nm
        nv_ref[...] = nv

    spec = pl.BlockSpec((tr, cols), lambda i: (i, 0))
    return _pcall(
        body, name=name, grid=(rows // tr,), in_specs=[spec] * 4, out_specs=[spec] * 3,
        out_shape=[jax.ShapeDtypeStruct(w.shape, F32)] * 3, compiler_params=_params("parallel"),
    )(w, g, m, v)


def _sum_rows(parts, out_dtype, name):
    rows, cols = parts[0].shape
    tr = rows
    for cand in range(16, rows, 16):
        if rows % cand == 0 and cand * cols * 4 <= (1 << 20):
            tr = cand
    n = len(parts)

    def body(*refs):
        acc = refs[0][...].astype(F32)
        for r in refs[1:n]:
            acc = acc + r[...].astype(F32)
        refs[n][...] = acc.astype(out_dtype)

    spec = pl.BlockSpec((tr, cols), lambda i: (i, 0))
    return _pcall(
        body, name=name, grid=(rows // tr,), in_specs=[spec] * n, out_specs=spec,
        out_shape=jax.ShapeDtypeStruct((rows, cols), out_dtype), compiler_params=_params("parallel"),
    )(*parts)


def _place():
    return lax.axis_index("x"), lax.axis_index("y"), lax.axis_index("c")


def _other_chips(x, y):
    return [(1 - x, y), (x, 1 - y), (1 - x, 1 - y)]


def _remote(src, dst, send_sem, recv_sem, to):
    return pltpu.make_async_remote_copy(src_ref=src, dst_ref=dst, send_sem=send_sem, recv_sem=recv_sem,
                                        device_id=to, device_id_type=MESH)


ANY = pl.BlockSpec(memory_space=pl.ANY)
DMA_CHUNK_BYTES = 512 * 1024


def _row_chunks(rows, row_bytes):
    per = max(16, DMA_CHUNK_BYTES // row_bytes // 16 * 16)
    return [(s, min(per, rows - s)) for s in range(0, rows, per)]


def _row_tile(rows, cols, align, limit=1 << 21):
    best = None
    for cand in range(align, rows + 1, align):
        if rows % cand == 0 and cand * cols * 4 <= limit:
            best = cand
    return best or rows


def _allgather_pieces(pieces):
    n = len(pieces)
    halves = [_row_chunks(p.shape[0] // 2, p.shape[1] * p.dtype.itemsize) for p in pieces]
    entries = [(a, q, s, m, j) for a in range(n) for q, (s, m) in enumerate(halves[a]) for j in range(3)]
    slot = {(a, q, j): k for k, (a, q, _, _, j) in enumerate(entries)}
    n_ici = len(entries)

    def body(*refs):
        ins, outs = refs[:n], refs[n:2 * n]
        send_sems, recv_sems = refs[2 * n:]
        x, y, c = _place()
        me = 2 * x + y
        sibling = (x, y, 1 - c)
        chips = _other_chips(x, y)

        def landed(a, s, m, j, core):
            half = ins[a].shape[0] // 2
            return outs[a].at[2 * chips[j][0] + chips[j][1], pl.ds(core * half + s, m)]

        sent = []
        for k, (a, q, s, m, j) in enumerate(entries):
            if j < 2:
                half = ins[a].shape[0] // 2
                cp = _remote(ins[a].at[pl.ds(c * half + s, m)], outs[a].at[me, pl.ds(c * half + s, m)],
                             send_sems.at[k], recv_sems.at[k], (*chips[j], c))
                cp.start()
                sent.append(cp)

        def pass_to_sibling(k, blk):
            fw = _remote(blk, blk, send_sems.at[n_ici + k], recv_sems.at[n_ici + k], sibling)
            fw.start()
            sent.append(fw)

        for k, (a, q, s, m, j) in enumerate(entries):
            if j < 2:
                blk = landed(a, s, m, j, c)
                _remote(blk, blk, send_sems.at[k], recv_sems.at[k], (*chips[j], c)).wait_recv()
                first = q < (len(halves[a]) + 1) // 2
                if (j == 0) == first:
                    on = slot[(a, q, 2)]
                    rl = _remote(blk, blk, send_sems.at[on], recv_sems.at[on], (*chips[1 - j], c))
                    rl.start()
                    sent.append(rl)
                pass_to_sibling(k, blk)
        for k, (a, q, s, m, j) in enumerate(entries):
            if j == 2:
                blk = landed(a, s, m, j, c)
                _remote(blk, blk, send_sems.at[k], recv_sems.at[k], (*chips[j], c)).wait_recv()
                pass_to_sibling(k, blk)
        for k, (a, q, s, m, j) in enumerate(entries):
            blk = landed(a, s, m, j, 1 - c)
            _remote(blk, blk, send_sems.at[n_ici + k], recv_sems.at[n_ici + k], sibling).wait_recv()
        for cp in sent:
            cp.wait_send()

    gathered = _pcall(
        body, name="allgather_weights", in_specs=[ANY] * n, out_specs=[ANY] * n,
        out_shape=[jax.ShapeDtypeStruct((4,) + p.shape, p.dtype) for p in pieces],
        scratch_shapes=[pltpu.SemaphoreType.DMA((2 * n_ici,)), pltpu.SemaphoreType.DMA((2 * n_ici,))],
        compiler_params=pltpu.CompilerParams(has_side_effects=True),
    )(*pieces)
    x, y, _ = _place()
    return [lax.dynamic_update_slice(g, p[None], (2 * x + y, 0, 0)) for g, p in zip(gathered, pieces)]


def _sibling_exchange(grads):
    n = len(grads)
    chunks = [_row_chunks(g.shape[1] // 2, g.shape[2] * g.dtype.itemsize) for g in grads]
    n_sem = 4 * sum(len(ch) for ch in chunks)

    def body(*refs):
        ins, gots = refs[:n], refs[n:2 * n]
        send_sems, recv_sems = refs[2 * n:]
        x, y, c = _place()
        sibling = (x, y, 1 - c)
        work = []
        for a in range(n):
            half = ins[a].shape[1] // 2
            for piece in range(4):
                for s, m in chunks[a]:
                    k = len(work)
                    cp = _remote(ins[a].at[piece, pl.ds((1 - c) * half + s, m)], gots[a].at[piece, pl.ds(s, m)],
                                 send_sems.at[k], recv_sems.at[k], sibling)
                    cp.start()
                    work.append(cp)
        for cp in work:
            cp.wait()

    return _pcall(
        body, name="grad_sibling_exchange", in_specs=[ANY] * n, out_specs=[ANY] * n,
        out_shape=[jax.ShapeDtypeStruct((4, g.shape[1] // 2, g.shape[2]), g.dtype) for g in grads],
        scratch_shapes=[pltpu.SemaphoreType.DMA((n_sem,)), pltpu.SemaphoreType.DMA((n_sem,))],
        compiler_params=pltpu.CompilerParams(has_side_effects=True),
    )(*grads)


def _sibling_gather(fulls):
    n = len(fulls)
    chunks = [_row_chunks(f.shape[0] // 2, f.shape[1] * f.dtype.itemsize) for f in fulls]
    n_sem = sum(len(ch) for ch in chunks)

    def body(*refs):
        outs = refs[n:2 * n]
        send_sems, recv_sems = refs[2 * n:]
        x, y, c = _place()
        sibling = (x, y, 1 - c)
        work = []
        for a in range(n):
            h = outs[a].shape[0] // 2
            for s, m in chunks[a]:
                k = len(work)
                mine = outs[a].at[pl.ds(c * h + s, m)]
                cp = _remote(mine, mine, send_sems.at[k], recv_sems.at[k], sibling)
                cp.start()
                work.append((a, s, m, cp))
        for k, (a, s, m, cp) in enumerate(work):
            h = outs[a].shape[0] // 2
            cp.wait_send()
            theirs = outs[a].at[pl.ds((1 - c) * h + s, m)]
            _remote(theirs, theirs, send_sems.at[k], recv_sems.at[k], sibling).wait_recv()

    return _pcall(
        body, name="grad_sibling_gather", in_specs=[ANY] * n, out_specs=[ANY] * n,
        out_shape=[jax.ShapeDtypeStruct(f.shape, f.dtype) for f in fulls],
        input_output_aliases={a: a for a in range(n)},
        scratch_shapes=[pltpu.SemaphoreType.DMA((n_sem,)), pltpu.SemaphoreType.DMA((n_sem,))],
        compiler_params=pltpu.CompilerParams(has_side_effects=True),
    )(*fulls)


def _pair_sum(grad, got, place, name):
    _, rows, cols = grad.shape
    half = rows // 2
    tr = _row_tile(half, cols, 16)

    def body(p_ref, a_ref, b_ref, o_ref):
        o_ref[...] = (a_ref[...].astype(F32) + b_ref[...].astype(F32)).astype(BF16)

    return _pcall(
        body, name=name,
        grid_spec=pltpu.PrefetchScalarGridSpec(
            num_scalar_prefetch=1, grid=(4, half // tr),
            in_specs=[pl.BlockSpec((None, tr, cols), lambda k, i, p: (k, p[1] * (half // tr) + i, 0)),
                      pl.BlockSpec((None, tr, cols), lambda k, i, p: (k, i, 0))],
            out_specs=pl.BlockSpec((None, tr, cols), lambda k, i, p: (k, i, 0))),
        out_shape=jax.ShapeDtypeStruct((4, half, cols), BF16),
        compiler_params=_params("parallel", "parallel"),
    )(place, grad, got)


def _chip_sum(sums, got, place, name):
    _, h, cols = sums.shape
    tr = _row_tile(h, cols, 16)

    def body(p_ref, own_ref, g0, g1, g2, o_ref):
        o_ref[...] = ((own_ref[...].astype(F32) + g0[...].astype(F32)) + g1[...].astype(F32)) + g2[...].astype(F32)

    gspec = lambda j: pl.BlockSpec((None, tr, cols), lambda i, p: (j, i, 0))
    return _pcall(
        body, name=name,
        grid_spec=pltpu.PrefetchScalarGridSpec(
            num_scalar_prefetch=1, grid=(h // tr,),
            in_specs=[pl.BlockSpec((None, tr, cols), lambda i, p: (p[0], i, 0)), gspec(0), gspec(1), gspec(2)],
            out_specs=pl.BlockSpec((tr, cols), lambda i, p: (p[1] * (h // tr) + i, 0))),
        out_shape=jax.ShapeDtypeStruct((2 * h, cols), F32),
        compiler_params=_params("parallel"),
    )(place, sums, got, got, got)


def _allgather8(buf, name):
    rows = buf.shape[0]

    def body(in_ref, out_ref, send_sems, recv_sems):
        x, y, c = _place()
        me = 4 * x + 2 * y + c
        out_ref[me] = in_ref[...]
        work = []
        for rel in range(1, 8):
            fx, fy, fc = (rel >> 2) & 1, (rel >> 1) & 1, rel & 1
            to = (x ^ fx, y ^ fy, c ^ fc)
            cp = _remote(in_ref, out_ref.at[me], send_sems.at[rel - 1], recv_sems.at[rel - 1], to)
            cp.start()
            work.append((cp, 4 * to[0] + 2 * to[1] + to[2]))
        for rel, (cp, frm) in enumerate(work):
            cp.wait_send()
            blk = out_ref.at[frm]
            _remote(blk, blk, send_sems.at[rel], recv_sems.at[rel], (x, y, c)).wait_recv()

    return _pcall(
        body, name=name, in_specs=[pl.BlockSpec(memory_space=pltpu.VMEM)],
        out_specs=pl.BlockSpec(memory_space=pltpu.VMEM),
        out_shape=jax.ShapeDtypeStruct((8, rows, LANE), F32),
        scratch_shapes=[pltpu.SemaphoreType.DMA((7,)), pltpu.SemaphoreType.DMA((7,))],
        compiler_params=pltpu.CompilerParams(has_side_effects=True),
    )(buf)


def _pack_rows(arrs):
    parts = []
    for a in arrs:
        f = a.reshape(-1).astype(F32)
        parts.append(jnp.pad(f, (0, (-f.shape[0]) % LANE)))
    flat = jnp.concatenate(parts)
    rows = -(-flat.shape[0] // LANE)
    rows8 = -(-rows // 8) * 8
    return jnp.pad(flat, (0, rows8 * LANE - flat.shape[0])).reshape(rows8, LANE)


def _unpack_rows(buf, shapes):
    flat = buf.reshape(-1)
    outs, off = [], 0
    for s in shapes:
        n = int(np.prod(s))
        outs.append(flat[off:off + n].reshape(s))
        off += -(-n // LANE) * LANE
    return outs


def _local_grads(x, p, target, wseg, w_br16, w_out16, w_ple16, b_gate, conv_w, conv_b, dt_bias, a_log, d_skip,
                 ssm_norm_w, ln_g, ln_b, rel_bias, finish_dx):
    nb, seq, _ = x.shape
    bmaps = jnp.asarray(_bucket_maps())
    bias = _bias_tables(rel_bias, bmaps)
    bgate8 = jnp.pad(b_gate, ((0, 5), (0, 0)))
    dils = [d for _, d in PATTERNS]

    x16p = _token_orders(x, dils[1:])
    x16 = x16p[0]
    p16 = p.astype(BF16)
    qkv = [_proj(x16p[g], [wseg["qkv%d" % g]], BF16, "proj_qkv%d" % g, True)[0].reshape(
        nb, dils[g], seq // dils[g], -1) for g in range(3)]
    nat = {}
    for gi, (group, tm) in enumerate(NAT_GROUPS):
        outs = _proj(x16, [wseg[s] for s in group], F32, "proj_nat%d" % gi, True, tm)
        nat.update(zip(group, outs))
    att = [_attn_fwd(qkv[g], bias[g * GROUP_HEADS:(g + 1) * GROUP_HEADS], dils[g], "attn_fwd%d" % g) for g in range(3)]
    oa, o_att, lse = _combine_fwd(att[0][0], att[0][1], att[1:], nat["gatt"])

    conv_wg, conv_bg = _xbc_group_order(conv_w), _xbc_group_order(conv_b)
    act = _conv_fwd(nat["xbc"], conv_wg, conv_bg, "conv_fwd")
    dt_sp, dt_sg = _softplus_sig(nat["dt"], jnp.pad(dt_bias, ((0, 0), (0, LANE - SSM_HEADS))))
    dtg, sgg = _group_lanes(dt_sp), _group_lanes(dt_sg)
    alog_g, dskip_g = _group_lanes(a_log), _group_lanes(d_skip)
    y_ssm, y_all, sprev = _ssd_fwd(act, dtg, nat["z"], alog_g, dskip_g, ssm_norm_w)

    w_bra, w_brb = w_br16[:ATT_OUT], w_br16[ATT_OUT:]
    y_a, = _proj(oa, [w_bra], F32, "proj_ya")
    y_b, = _proj(y_ssm, [w_brb], F32, "proj_yb")
    merged = _merge_fwd(y_a, y_b, nat["gm"], bgate8)
    mix, = _proj(merged, [w_out16], F32, "proj_mix")
    pw, = _proj(p16, [w_ple16], F32, "proj_ple")

    dx, dpre16, dpw16, dgp16, ln_sums = _ln_loss(x, mix, nat["gp"], pw, target, bgate8, ln_g, ln_b)
    loss_sum = (0.5 / D_MODEL) * jnp.sum(ln_sums[3])
    dmerged = _dx([dpre16], [w_out16], [], "dx_merged")
    dya16, dyb16, dgm16, mg_sums = _merge_bwd(dmerged, y_a, y_b, nat["gm"], bgate8)
    doa = _dx([dya16], [w_bra], [], "dx_oa")
    dys = _dx([dyb16], [w_brb], [], "dx_yssm")
    g_w_out = _dw(merged, dpre16, BF16, "dw_out")
    g_w_br = jnp.concatenate([_dw(oa, dya16, BF16, "dw_bra"), _dw(y_ssm, dyb16, BF16, "dw_brb")], axis=0)
    g_w_ple = _dw(p16, dpw16, BF16, "dw_ple")

    do_att, dgatt16, own_order = _combine_bwd(doa, nat["gatt"], o_att, lse, dils[1:])
    dseg = {"gatt": dgatt16, "gm": dgm16, "gp": dgp16}
    dbias = []
    for g in range(3):
        cotangent = (do_att, o_att, lse) if g == 0 else (own_order[2 * g - 2], own_order[2 * g - 1])
        dqkv, db = _attn_bwd(qkv[g], bias[g * GROUP_HEADS:(g + 1) * GROUP_HEADS], cotangent, dils[g],
                             "attn_bwd%d" % g)
        dseg["qkv%d" % g] = dqkv.reshape(nb, seq, -1)
        dbias.append(db)
    g_rel = _bias_grad(jnp.concatenate(dbias, axis=0), bmaps)[:, 0, :NUM_BUCKETS].T

    dact, ddtg, dz, ssd_small, g_normw = _ssd_bwd(
        act, dtg, sgg, nat["z"], y_all, dys, sprev, alog_g, dskip_g, ssm_norm_w)
    dseg["z"] = dz
    dseg["dt"] = jnp.pad(_ungroup_lanes(ddtg), ((0, 0), (0, 0), (0, LANE - SSM_HEADS)))
    dpre, conv_sums = _conv_bwd_pre(dact, nat["xbc"], conv_wg, conv_bg, "conv_bwd")
    dseg["xbc"] = _conv_bwd_x(dpre, conv_wg, "conv_bwd_x")
    csum = _xbc_reference_order(conv_sums)

    dx_own = [_dx([dseg["qkv%d" % g]], [wseg["qkv%d" % g]], [], "dx_qkv%d" % g, True).reshape(
        nb, dils[g], seq // dils[g], D_MODEL) for g in (1, 2)]
    dwseg = {"qkv%d" % g: _dw(x16p[g], dseg["qkv%d" % g], BF16, "dw_qkv%d" % g, True) for g in range(3)}
    for group, _ in NAT_GROUPS:
        dwseg.update({s: _dw(x16, dseg[s], BF16, "dw_" + s, True) for s in group})
    names = ["qkv0"] + [s for group, _ in NAT_GROUPS for s in group]
    dx = finish_dx([dseg[s] for s in names], [wseg[s] for s in names], [dx], dx_own, dwseg, g_w_br, g_w_out, g_w_ple)

    small = dict(
        b_gate=jnp.stack([mg_sums[0], mg_sums[1], ln_sums[2]]),
        conv_w=csum[0:4], conv_b=csum[4:5],
        dt_bias=_ungroup_lanes(ssd_small[:, 2:3, :]), a_log=_ungroup_lanes(ssd_small[:, 0:1, :]),
        d_skip=_ungroup_lanes(ssd_small[:, 1:2, :]), ssm_norm_w=g_normw,
        ln_g=ln_sums[0:1], ln_b=ln_sums[1:2], rel_bias=g_rel)
    return loss_sum, dx, small


DX_TM = 256
SMALL_ORDER = ("b_gate", "conv_w", "conv_b", "dt_bias", "a_log", "d_skip", "ssm_norm_w", "ln_g", "ln_b", "rel_bias")
SMALL_FULL_SHAPES = dict(b_gate=(3, 1024), conv_w=(4, 3072), conv_b=(1, 3072), dt_bias=(1, 32), a_log=(1, 32),
                         d_skip=(1, 32), ssm_norm_w=(1, 2048), ln_g=(1, 1024), ln_b=(1, 1024), rel_bias=(32, 36))


def kernel(x, p, w_in, b_gate, conv_w, conv_b, dt_bias, a_log, d_skip, ssm_norm_w, w_branch, w_out, w_ple, ln_g, ln_b, rel_bias, loss_target, m_w_in, m_b_gate, m_conv_w, m_conv_b, m_dt_bias, m_a_log, m_d_skip, m_ssm_norm_w, m_w_branch, m_w_out, m_w_ple, m_ln_g, m_ln_b, m_rel_bias, v_w_in, v_b_gate, v_conv_w, v_conv_b, v_dt_bias, v_a_log, v_d_skip, v_ssm_norm_w, v_w_branch, v_w_out, v_w_ple, v_ln_g, v_ln_b, v_rel_bias):
    cx, cy, cc = _place()
    chip = 2 * cx + cy
    dev = 4 * cx + 2 * cy + cc

    w_in_t = jnp.transpose(w_in[0])
    win16 = _shard_to_window(w_in_t, chip)
    g_win, g_br, g_out, g_ple = _allgather_pieces(
        [win16, w_branch[0].astype(BF16), w_out[0].astype(BF16), w_ple[0].astype(BF16)])
    wseg = _assemble(g_win)
    w_br16 = g_br.reshape(4 * 704, D_MODEL)
    w_out16 = g_out.reshape(D_MODEL, D_MODEL)
    w_ple16 = jnp.transpose(g_ple, (1, 0, 2)).reshape(PLE_DIM, D_MODEL)
    shards = _allgather8(_pack_rows([b_gate[0], conv_w[0]]), "allgather_small_params")
    per_chip = [_unpack_rows(shards[2 * k], [(3, 256), (4, 768)]) for k in range(4)]
    b_gate_full = jnp.concatenate([pc[0] for pc in per_chip], axis=1)
    conv_w_full = jnp.concatenate([pc[1] for pc in per_chip], axis=1)

    place = jnp.stack([chip, cc]).astype(jnp.int32)
    reduced = []

    def finish_dx(dhs, ws, accs, own_order_accs, dwseg, d_br, d_out, d_ple):
        grads = [_pack(dwseg), d_br.reshape(4, 704, D_MODEL), d_out.reshape(4, 256, D_MODEL),
                 jnp.transpose(d_ple.reshape(PLE_DIM, 4, 256), (1, 0, 2))]
        got = _sibling_exchange(grads)
        chip_sums = [_pair_sum(g, t, place, "grad_pair_sum_%d" % i) for i, (g, t) in enumerate(zip(grads, got))]
        dx, others = _dx(dhs, ws, accs, "dx_w_in_and_grad_chip_scatter", True, DX_TM, chip_sums, own_order_accs)
        fulls = [_chip_sum(s, t, place, "grad_chip_sum_%d" % i) for i, (s, t) in enumerate(zip(chip_sums, others))]
        reduced.extend(_sibling_gather(fulls))
        return dx

    loss_sum, grad_x, small = _local_grads(
        x, p[0], loss_target, wseg, w_br16, w_out16, w_ple16, b_gate_full, conv_w_full, conv_b, dt_bias, a_log,
        d_skip, ssm_norm_w, ln_g, ln_b, rel_bias, finish_dx)
    big = reduced
    g_w_in = _window_to_shard(big[0], chip)
    g_w_branch, g_w_out, g_w_ple = big[1], big[2], big[3]
    parts = _allgather8(_pack_rows([small[n] for n in SMALL_ORDER] + [loss_sum.reshape(1, 1)]),
                        "allgather_small_grads")
    small_sum = _sum_rows([parts[i] for i in range(8)], F32, "small_grad_sum")
    *reduced_small, loss = _unpack_rows(small_sum, [SMALL_FULL_SHAPES[n] for n in SMALL_ORDER] + [(1, 1)])
    loss = loss.reshape(())
    sg = dict(zip(SMALL_ORDER, reduced_small))
    sg["b_gate"] = lax.dynamic_slice_in_dim(sg["b_gate"], chip * 256, 256, axis=1)
    sg["conv_w"] = lax.dynamic_slice_in_dim(sg["conv_w"], chip * 768, 768, axis=1)
    del dev

    upd = {}
    upd["w_in"] = [jnp.transpose(t) for t in _adamw(w_in_t, g_w_in, jnp.transpose(m_w_in[0]),
                                                      jnp.transpose(v_w_in[0]), "adamw_w_in")]
    upd["w_branch"] = _adamw(w_branch[0], g_w_branch, m_w_branch[0], v_w_branch[0], "adamw_w_branch")
    upd["w_out"] = _adamw(w_out[0], g_w_out, m_w_out[0], v_w_out[0], "adamw_w_out")
    upd["w_ple"] = _adamw(w_ple[0], g_w_ple, m_w_ple[0], v_w_ple[0], "adamw_w_ple")
    small_w = dict(b_gate=b_gate, conv_w=conv_w, conv_b=conv_b, dt_bias=dt_bias, a_log=a_log, d_skip=d_skip,
                   ssm_norm_w=ssm_norm_w, ln_g=ln_g, ln_b=ln_b, rel_bias=rel_bias)
    small_m = dict(b_gate=m_b_gate, conv_w=m_conv_w, conv_b=m_conv_b, dt_bias=m_dt_bias, a_log=m_a_log,
                   d_skip=m_d_skip, ssm_norm_w=m_ssm_norm_w, ln_g=m_ln_g, ln_b=m_ln_b, rel_bias=m_rel_bias)
    small_v = dict(b_gate=v_b_gate, conv_w=v_conv_w, conv_b=v_conv_b, dt_bias=v_dt_bias, a_log=v_a_log,
                   d_skip=v_d_skip, ssm_norm_w=v_ssm_norm_w, ln_g=v_ln_g, ln_b=v_ln_b, rel_bias=v_rel_bias)
    shapes = [small_w[n].shape for n in SMALL_ORDER]
    s_delta, s_m, s_v = _adamw(_pack_rows([small_w[n] for n in SMALL_ORDER]), _pack_rows([sg[n] for n in SMALL_ORDER]),
                               _pack_rows([small_m[n] for n in SMALL_ORDER]), _pack_rows([small_v[n] for n in SMALL_ORDER]),
                               "adamw_small")
    for i, n in enumerate(SMALL_ORDER):
        upd[n] = tuple(_unpack_rows(t, shapes)[i] for t in (s_delta, s_m, s_v))
        sg[n] = sg[n].reshape(small_w[n].shape)

    order = ("w_in", "b_gate", "conv_w", "conv_b", "dt_bias", "a_log", "d_skip", "ssm_norm_w", "w_branch", "w_out",
             "w_ple", "ln_g", "ln_b", "rel_bias")
    grads = dict(sg, w_in=jnp.transpose(g_w_in)[None],w_branch=g_w_branch[None], w_out=g_w_out[None], w_ple=g_w_ple[None])
    lead = lambda n, t: t[None] if n in ("w_in", "w_branch", "w_out", "w_ple") else t
    return (loss, grad_x, *[grads[n] for n in order], *[lead(n, upd[n][0]) for n in order],
            *[lead(n, upd[n][1]) for n in order], *[lead(n, upd[n][2]) for n in order])
```

```python
import functools
import math

import numpy as np
import jax
import jax.numpy as jnp
from jax import lax
from jax.experimental import pallas as pl
from jax.experimental.pallas import tpu as pltpu

F32, BF16 = jnp.float32, jnp.bfloat16

D_MODEL = 1024
HEAD_DIM = 64
GROUP_HEADS = 12
ATT_OUT = GROUP_HEADS * HEAD_DIM
PATTERNS = ((128, 1), (512, 4), (2048, 16))
BAND = 128
NUM_BUCKETS = 32
MAX_DISTANCE = 2048
D_INNER = 2048
SSM_HEADS = 32
SSM_GROUPS = 4
GROUP_SSM_HEADS = SSM_HEADS // SSM_GROUPS
D_STATE = 128
CHUNK = 128
PLE_DIM = 256
ALPHA = 2.0 ** 0.25
LN_EPS = 1e-5
RMS_EPS = 1e-5
ADAM_LR, ADAM_B1, ADAM_B2, ADAM_EPS, ADAM_WD, ADAM_STEP = 0.001, 0.9, 0.999, 1e-08, 0.01, 10
NEG = -1e30

QKV_W = 3 * ATT_OUT
IN_COLS = 15904
SHARD_COLS = IN_COLS // 4
DT_COL = 12800
ROW_TILE = 16
WIN_ROWS = 4000


def _win_offset(k):
    return (k * SHARD_COLS) % ROW_TILE


def _win_start(k):
    return k * SHARD_COLS - _win_offset(k)

VMEM_LIMIT_BYTES = 56 * 1024 * 1024
LANE = 128
MESH = pl.DeviceIdType.MESH
NT = (((1,), (1,)), ((), ()))
TN = (((0,), (0,)), ((), ()))


def _pcall(body, **kw):
    return pl.pallas_call(body, **kw)


def _params(*sem):
    return pltpu.CompilerParams(dimension_semantics=sem, vmem_limit_bytes=VMEM_LIMIT_BYTES)


def _sigmoid(v):
    return jax.nn.sigmoid(v)


MM_TM = 512


def _tok_spec(tm, width):
    return pl.BlockSpec((None, tm, width), lambda b, i: (b, i, 0))


def _whole(arr, single_buffer=False):
    mode = dict(pipeline_mode=pl.Buffered(1)) if single_buffer else {}
    return pl.BlockSpec(arr.shape, lambda b, i: (0,) * arr.ndim, **mode)


def _proj(a3, ws, out_dtype, name, w_rows_are_outputs=False, tm=MM_TM):
    nb, seq, kdim = a3.shape
    nw = len(ws)
    widths = [w.shape[0] if w_rows_are_outputs else w.shape[1] for w in ws]

    def body(*refs):
        a = refs[0][...].astype(BF16)
        for w_ref, o_ref in zip(refs[1:1 + nw], refs[1 + nw:]):
            if w_rows_are_outputs:
                v = lax.dot_general(a, w_ref[...], NT, preferred_element_type=F32)
            else:
                v = jnp.dot(a, w_ref[...], preferred_element_type=F32)
            o_ref[...] = v.astype(out_dtype)

    return _pcall(
        body, name=name, grid=(nb, seq // tm),
        in_specs=[_tok_spec(tm, kdim)] + [_whole(w) for w in ws],
        out_specs=[_tok_spec(tm, n) for n in widths],
        out_shape=[jax.ShapeDtypeStruct((nb, seq, n), out_dtype) for n in widths],
        compiler_params=_params("parallel", "parallel"),
    )(a3, *ws)


def _dx(dhs, ws, accs, name, w_rows_are_outputs=False, tm=MM_TM, scatter=None, own_order_accs=()):
    nb, seq, _ = dhs[0].shape
    nd, nacc, npa = len(dhs), len(accs), len(own_order_accs)
    kout = ws[0].shape[1] if w_rows_are_outputs else ws[0].shape[0]
    sums = scatter or []
    ns = len(sums)
    chunks = [_row_chunks(s.shape[1], s.shape[2] * s.dtype.itemsize) for s in sums]
    n_sem = 3 * sum(len(ch) for ch in chunks)
    grid = (nb, seq // tm)
    ntile = kout // LANE if npa else 0

    def body(*refs):
        n_in = 2 * nd + nacc + npa
        sum_refs, o_ref, got_refs = refs[n_in:n_in + ns], refs[n_in + ns], refs[n_in + ns + 1:n_in + 2 * ns + 1]
        tile_refs = refs[n_in + 2 * ns + 1:n_in + 2 * ns + 1 + ntile]

        def copies():
            send_sems, recv_sems = refs[-2], refs[-1]
            x, y, c = _place()
            out = []
            for a in range(ns):
                for s, m in chunks[a]:
                    for j, (cx, cy) in enumerate(_other_chips(x, y)):
                        k = len(out)
                        out.append(_remote(sum_refs[a].at[2 * cx + cy, pl.ds(s, m)], got_refs[a].at[j, pl.ds(s, m)],
                                           send_sems.at[k], recv_sems.at[k], (cx, cy, c)))
            return out

        if ns:
            @pl.when((pl.program_id(0) == 0) & (pl.program_id(1) == 0))
            def _():
                for cp in copies():
                    cp.start()

        v = None
        for dh_ref, w_ref in zip(refs[:nd], refs[nd:2 * nd]):
            dh = dh_ref[...].astype(BF16)
            if w_rows_are_outputs:
                t = jnp.dot(dh, w_ref[...], preferred_element_type=F32)
            else:
                t = lax.dot_general(dh, w_ref[...], NT, preferred_element_type=F32)
            v = t if v is None else v + t
        for a_ref in refs[2 * nd:2 * nd + nacc]:
            v = v + a_ref[...]
        for p_ref in refs[2 * nd + nacc:n_in]:
            v = v + _natural_rows(p_ref, tile_refs)
        o_ref[...] = v

        if ns:
            @pl.when((pl.program_id(0) == grid[0] - 1) & (pl.program_id(1) == grid[1] - 1))
            def _():
                for cp in copies():
                    cp.wait()

    out = _pcall(
        body, name=name, grid=grid,
        in_specs=[_tok_spec(tm, dh.shape[-1]) for dh in dhs] + [_whole(w, bool(ns)) for w in ws]
        + [_tok_spec(tm, kout)] * nacc
        + [pl.BlockSpec((None, p.shape[1], tm // p.shape[1], kout), lambda b, i: (b, 0, i, 0)) for p in own_order_accs]
        + [ANY] * ns,
        out_specs=[_tok_spec(tm, kout)] + [ANY] * ns,
        out_shape=[jax.ShapeDtypeStruct((nb, seq, kout), F32)]
        + [jax.ShapeDtypeStruct((3,) + s.shape[1:], s.dtype) for s in sums],
        input_output_aliases={2 * nd: 0} if nacc else {},
        scratch_shapes=[pltpu.VMEM((tm, LANE), F32)] * ntile
        + ([pltpu.SemaphoreType.DMA((n_sem,)), pltpu.SemaphoreType.DMA((n_sem,))] if ns else []),
        compiler_params=pltpu.CompilerParams(
            dimension_semantics=("arbitrary", "arbitrary") if ns else ("parallel", "parallel"),
            vmem_limit_bytes=VMEM_LIMIT_BYTES, has_side_effects=bool(ns)),
    )(*dhs, *ws, *accs, *own_order_accs, *sums)
    return (out[0], list(out[1:])) if ns else out[0]


def _dw(a3, dhs, out_dtype, name, rows_are_outputs=False):
    nb, seq, kdim = a3.shape
    nd = len(dhs)
    grid = (nb, seq // MM_TM)
    shapes = [(dh.shape[-1], kdim) if rows_are_outputs else (kdim, dh.shape[-1]) for dh in dhs]

    def body(*refs):
        b, i = pl.program_id(0), pl.program_id(1)
        dh_refs, o_refs, acc_refs = refs[1:1 + nd], refs[1 + nd:1 + 2 * nd], refs[1 + 2 * nd:]

        @pl.when((b == 0) & (i == 0))
        def _():
            for acc_ref in acc_refs:
                acc_ref[...] = jnp.zeros_like(acc_ref)

        a = refs[0][...].astype(BF16)
        for dh_ref, acc_ref in zip(dh_refs, acc_refs):
            dh = dh_ref[...].astype(BF16)
            acc_ref[...] += lax.dot_general(*((dh, a) if rows_are_outputs else (a, dh)), TN,
                                            preferred_element_type=F32)

        @pl.when((b == grid[0] - 1) & (i == grid[1] - 1))
        def _():
            for o_ref, acc_ref in zip(o_refs, acc_refs):
                o_ref[...] = acc_ref[...].astype(out_dtype)

    return _pcall(
        body, name=name, grid=grid,
        in_specs=[_tok_spec(MM_TM, kdim)] + [_tok_spec(MM_TM, dh.shape[-1]) for dh in dhs],
        out_specs=[pl.BlockSpec(s, lambda b, i: (0, 0)) for s in shapes],
        out_shape=[jax.ShapeDtypeStruct(s, out_dtype) for s in shapes],
        scratch_shapes=[pltpu.VMEM(s, F32) for s in shapes],
        compiler_params=_params("arbitrary", "arbitrary"),
    )(a3, *dhs)


def _qkv_rows(g):
    return [(part * QKV_W + g * ATT_OUT + hp * LANE, LANE) for hp in range(ATT_OUT // LANE) for part in range(3)]


XBC_START = 3 * QKV_W + ATT_OUT + D_INNER
GROUP_CH = GROUP_SSM_HEADS * HEAD_DIM
XBC_GROUP = GROUP_CH + 2 * D_STATE
CONV_DIM = SSM_GROUPS * XBC_GROUP


def _xbc_ranges():
    out = []
    for g in range(SSM_GROUPS):
        out += [(g * GROUP_CH, GROUP_CH), (D_INNER + g * D_STATE, D_STATE),
                (D_INNER + SSM_GROUPS * D_STATE + g * D_STATE, D_STATE)]
    return out


def _xbc_group_order(t):
    return jnp.concatenate([t[..., s:s + n] for s, n in _xbc_ranges()], axis=-1)


def _xbc_reference_order(t):
    g = lambda off, n: [t[..., k * XBC_GROUP + off:k * XBC_GROUP + off + n] for k in range(SSM_GROUPS)]
    return jnp.concatenate(g(0, GROUP_CH) + g(GROUP_CH, D_STATE) + g(GROUP_CH + D_STATE, D_STATE), axis=-1)


def _segments():
    one = lambda name, start, rows: (name, [(start, rows)], max(rows, LANE))
    return [("qkv%d" % g, _qkv_rows(g), QKV_W) for g in range(3)] + [
        one("gatt", 3 * QKV_W, ATT_OUT), one("z", 3 * QKV_W + ATT_OUT, D_INNER),
        ("xbc", [(XBC_START + s, n) for s, n in _xbc_ranges()], CONV_DIM), one("dt", DT_COL, SSM_HEADS),
        one("gm", DT_COL + SSM_HEADS, 2 * D_MODEL), one("gp", DT_COL + SSM_HEADS + 2 * D_MODEL, D_MODEL)]


LAYOUT_TC = 256
NAT_GROUPS = ((("gatt", "z", "dt", "gp"), 512), (("xbc", "gm"), 256))
DW_GROUPS = (("gatt", "z", "dt", "gp"), ("xbc",), ("gm",))


def _assemble(win):
    segs = _segments()

    def body(win_ref, *outs):
        def pieces(start, rows):
            t, end = start, start + rows
            while t < end:
                k = min(t // SHARD_COLS, 3)
                shard_end = (k + 1) * SHARD_COLS
                if k < 3 and shard_end % ROW_TILE and t == shard_end - shard_end % ROW_TILE:
                    lo = t - _win_start(k)
                    yield win_ref[k, lo:lo + ROW_TILE, :] + win_ref[k + 1, 0:ROW_TILE, :]
                    t += ROW_TILE
                    continue
                upto = min(end, shard_end - shard_end % ROW_TILE if k < 3 else end)
                yield win_ref[k, t - _win_start(k):upto - _win_start(k), :]
                t = upto

        for (_, ranges, total), o_ref in zip(segs, outs):
            off = 0
            for start, rows in ranges:
                for part in pieces(start, rows):
                    o_ref[off:off + part.shape[0], :] = part
                    off += part.shape[0]
            if off < total:
                o_ref[off:total, :] = jnp.zeros((total - off, o_ref.shape[1]), BF16)

    outs = _pcall(
        body, name="assemble_w_in", grid=(D_MODEL // LAYOUT_TC,),
        in_specs=[pl.BlockSpec((4, WIN_ROWS, LAYOUT_TC), lambda i: (0, 0, i))],
        out_specs=[pl.BlockSpec((total, LAYOUT_TC), lambda i: (0, i)) for _, _, total in segs],
        out_shape=[jax.ShapeDtypeStruct((total, D_MODEL), BF16) for _, _, total in segs],
        compiler_params=_params("parallel"),
    )(win)
    return {name: o for (name, _, _), o in zip(segs, outs)}


def _pack(dsegs):
    segs = _segments()

    def body(*refs):
        ins, o_ref = refs[:-1], refs[-1]
        tail = IN_COLS - _win_start(3)
        o_ref[3, tail:, :] = jnp.zeros((WIN_ROWS - tail, o_ref.shape[2]), BF16)
        for (_, ranges, _), s_ref in zip(segs, ins):
            off = 0
            for start, rows in ranges:
                for k in range(4):
                    lo = _win_start(k)
                    a, b = max(start, lo), min(start + rows, lo + WIN_ROWS)
                    if a < b:
                        o_ref[k, a - lo:b - lo, :] = s_ref[off + a - start:off + b - start, :]
                off += rows

    return _pcall(
        body, name="pack_dw_in", grid=(D_MODEL // LAYOUT_TC,),
        in_specs=[pl.BlockSpec((total, LAYOUT_TC), lambda i: (0, i)) for _, _, total in segs],
        out_specs=pl.BlockSpec((4, WIN_ROWS, LAYOUT_TC), lambda i: (0, 0, i)),
        out_shape=jax.ShapeDtypeStruct((4, WIN_ROWS, D_MODEL), BF16),
        compiler_params=_params("parallel"),
    )(*[dsegs[name] for name, _, _ in segs])


def _shard_to_window(shard_t, k):
    def at(off):
        return lambda w: jnp.pad(w.astype(BF16), ((off, WIN_ROWS - SHARD_COLS - off), (0, 0)))

    return lax.cond(k % 2 == 1, at(_win_offset(1)), at(_win_offset(0)), shard_t)


def _window_to_shard(win, k):
    return lax.dynamic_slice(win, ((k % 2) * _win_offset(1), 0), (SHARD_COLS, D_MODEL))


def _bucket_maps():
    qi = np.arange(8)[:, None]
    kj = np.arange(2 * BAND)[None, :]
    delta = qi + BAND - kj
    maps = []
    for window, dil in PATTERNS:
        valid = (delta >= 0) & (delta <= window // dil)
        dist = np.maximum(delta, 0) * dil
        max_exact = NUM_BUCKETS // 2
        d_f = np.maximum(dist, 1).astype(np.float32)
        large = max_exact + (np.log(d_f / np.float32(max_exact)) / np.float32(math.log(MAX_DISTANCE / max_exact))
                             * np.float32(NUM_BUCKETS - max_exact)).astype(np.int32)
        large = np.minimum(large, NUM_BUCKETS - 1)
        bucket = np.where(dist < max_exact, dist, large)
        maps.append(np.where(valid, bucket, -1).astype(np.int32))
    return np.stack(maps)


def _bias_tables(rel_bias, bmaps):
    def body(rb_ref, bm_ref, o_ref):
        h = pl.program_id(0)
        bm = bm_ref[...]
        acc = jnp.full(bm.shape, NEG, F32)
        for b in range(NUM_BUCKETS):
            acc = jnp.where(bm == b, rb_ref[b, h], acc)
        for a in range(BAND // 8):
            o_ref[8 * a:8 * a + 8, :] = acc if a == 0 else pltpu.roll(acc, 8 * a, 1)

    return _pcall(
        body, name="bias_tables", grid=(3 * GROUP_HEADS,),
        in_specs=[pl.BlockSpec(memory_space=pltpu.SMEM),
                  pl.BlockSpec((None, 8, 2 * BAND), lambda h: (h // GROUP_HEADS, 0, 0))],
        out_specs=pl.BlockSpec((None, BAND, 2 * BAND), lambda h: (h, 0, 0)),
        out_shape=jax.ShapeDtypeStruct((3 * GROUP_HEADS, BAND, 2 * BAND), F32),
        compiler_params=_params("parallel"),
    )(rel_bias, bmaps)


def _bias_grad(dbias, bmaps):
    def body(db_ref, bm_ref, o_ref):
        bm = bm_ref[...]
        db = db_ref[0:8, :]
        for a in range(1, BAND // 8):
            db = db + pltpu.roll(db_ref[8 * a:8 * a + 8, :], 2 * BAND - 8 * a, 1)
        lane = lax.broadcasted_iota(jnp.int32, (1, LANE), 1)
        vec = jnp.zeros((1, LANE), F32)
        for b in range(NUM_BUCKETS):
            s = jnp.sum(jnp.where(bm == b, db, 0.0), keepdims=True)
            vec = jnp.where(lane == b, s, vec)
        o_ref[...] = vec

    return _pcall(
        body, name="bias_grad", grid=(3 * GROUP_HEADS,),
        in_specs=[pl.BlockSpec((None, BAND, 2 * BAND), lambda h: (h, 0, 0)),
                  pl.BlockSpec((None, 8, 2 * BAND), lambda h: (h // GROUP_HEADS, 0, 0))],
        out_specs=pl.BlockSpec((None, 1, LANE), lambda h: (h, 0, 0)),
        out_shape=jax.ShapeDtypeStruct((3 * GROUP_HEADS, 1, LANE), F32),
        compiler_params=_params("parallel"),
    )(dbias, bmaps)


def _rows(n):
    if isinstance(n, int):
        return pl.ds(n * BAND, BAND)
    return pl.ds(pl.multiple_of(n * BAND, BAND), BAND)


def _for_blocks(blocks, nblk, per, carry):
    carry = blocks([0], carry, False)
    start = 1 + (nblk - 1) % per
    for n in range(1, start):
        carry = blocks([n], carry, True)
    trips = (nblk - start) // per
    if trips > 0:
        carry = lax.fori_loop(
            0, trips, lambda t, c: blocks([start + t * per + u for u in range(per)], c, True), carry)
    return carry


def _pairs_per_step(d):
    return {1: 3, 4: 6, 16: 6}[d]


def _attn_fwd(qkv4, bias, d, name):
    nb, _, sub, _ = qkv4.shape
    nblk = sub // BAND
    scale = HEAD_DIM ** -0.5
    npair = ATT_OUT // LANE
    hps = _pairs_per_step(d)
    compact = d > 1

    def body(qkv_ref, bias_ref, o_ref, l_ref):
        def blocks(ns, carry, with_prev):
            chains = [(bi, i, h) for bi in range(len(ns)) for i in range(hps) for h in range(2)]
            first_head = lax.broadcasted_iota(jnp.int32, (BAND, LANE), 1) < HEAD_DIM
            pair = lambda n, i, part: qkv_ref[_rows(n), (3 * i + part) * LANE:(3 * i + part + 1) * LANE]
            scores = []
            for bi, i, h in chains:
                n = ns[bi]
                qp = pair(n, i, 0) * scale
                q = jnp.where(first_head if h == 0 else jnp.logical_not(first_head), qp, jnp.zeros_like(qp))
                s_c = lax.dot_general(q, pair(n, i, 1), NT, preferred_element_type=F32) + bias_ref[2 * i + h, :, BAND:]
                s_p = None
                if with_prev:
                    s_p = lax.dot_general(q, pair(n - 1, i, 1), NT,
                                          preferred_element_type=F32) + bias_ref[2 * i + h, :, :BAND]
                scores.append((s_c, s_p))
            probs = []
            for s_c, s_p in scores:
                m = jnp.max(s_c, -1, keepdims=True)
                if with_prev:
                    m = jnp.maximum(m, jnp.max(s_p, -1, keepdims=True))
                e_c = jnp.exp(s_c - m)
                den = jnp.sum(e_c, -1, keepdims=True)
                e_p = None
                if with_prev:
                    e_p = jnp.exp(s_p - m)
                    den = den + jnp.sum(e_p, -1, keepdims=True)
                    e_p = e_p.astype(BF16)
                probs.append((e_c.astype(BF16), e_p, den, m))
            outs = {}
            for (bi, i, h), (e_c, e_p, den, m) in zip(chains, probs):
                n = ns[bi]
                acc = jnp.dot(e_c, pair(n, i, 2), preferred_element_type=F32)
                if with_prev:
                    acc = acc + jnp.dot(e_p, pair(n - 1, i, 2), preferred_element_type=F32)
                outs[(bi, i, h)] = (acc / den, m + jnp.log(den))
            lane = lax.broadcasted_iota(jnp.int32, (BAND, LANE), 1)
            for bi, n in enumerate(ns):
                per_head = jnp.zeros((BAND, LANE), F32)
                for i in range(hps):
                    o_ref[_rows(n), i * LANE:(i + 1) * LANE] = jnp.where(first_head, outs[(bi, i, 0)][0],
                                                                         outs[(bi, i, 1)][0])
                    if compact:
                        for h in range(2):
                            per_head = jnp.where(lane == 2 * i + h, outs[(bi, i, h)][1], per_head)
                    else:
                        l_ref[_rows(n), i * LANE:(i + 1) * LANE] = jnp.where(first_head, outs[(bi, i, 0)][1],
                                                                             outs[(bi, i, 1)][1])
                if compact:
                    l_ref[_rows(n), :] = per_head
            return carry

        _for_blocks(blocks, nblk, 2 if hps == 1 else 1, 0)

    in_specs = [pl.BlockSpec((None, None, sub, 3 * LANE * hps), lambda hp, b, r: (b, r, 0, hp)),
                pl.BlockSpec((2 * hps, BAND, 2 * BAND), lambda hp, b, r: (hp, 0, 0))]
    if compact:
        return _pcall(
            body, name=name, grid=(1, nb, d), in_specs=in_specs,
            out_specs=[pl.BlockSpec((None, None, sub, ATT_OUT), lambda hp, b, r: (b, r, 0, 0)),
                       pl.BlockSpec((None, None, sub, LANE), lambda hp, b, r: (b, r, 0, 0))],
            out_shape=[jax.ShapeDtypeStruct((nb, d, sub, ATT_OUT), F32), jax.ShapeDtypeStruct((nb, d, sub, LANE), F32)],
            compiler_params=_params("parallel", "parallel", "parallel"),
        )(qkv4, bias)
    ospec = pl.BlockSpec((None, sub, hps * LANE), lambda hp, b, r: (b, 0, r * (npair // hps) + hp))
    return _pcall(
        body, name=name, grid=(npair // hps, nb, d), in_specs=in_specs, out_specs=[ospec, ospec],
        out_shape=[jax.ShapeDtypeStruct((nb, sub, d * ATT_OUT), F32)] * 2,
        compiler_params=_params("parallel", "parallel", "parallel"),
    )(qkv4, bias)


STAT_LSE_LANE = 16


def _attn_bwd(qkv4, bias, cotangent, d, name):
    nb, _, sub, _ = qkv4.shape
    nblk = sub // BAND
    scale = HEAD_DIM ** -0.5
    npair = ATT_OUT // LANE
    hps = _pairs_per_step(d)
    compact = d > 1

    def body(qkv_ref, bias_ref, *rest):
        do_ref, dqkv_ref, db_ref = rest[0], rest[-2], rest[-1]
        b, r = pl.program_id(1), pl.program_id(2)

        @pl.when((b == 0) & (r == 0))
        def _():
            db_ref[...] = jnp.zeros_like(db_ref)

        def blocks(ns, carry, with_prev):
            sides = (0, 1) if with_prev else (0,)
            chains = [(bi, i, h, sd) for bi in range(len(ns)) for i in range(hps) for h in range(2) for sd in sides]
            first_head = lax.broadcasted_iota(jnp.int32, (BAND, LANE), 1) < HEAD_DIM
            own = lambda h, t: jnp.where(first_head if h == 0 else jnp.logical_not(first_head), t, jnp.zeros_like(t))
            pair = lambda rows, i, part: qkv_ref[rows, (3 * i + part) * LANE:(3 * i + part + 1) * LANE]
            key_rows = lambda bi, sd: _rows(ns[bi] - sd)
            qs = {}
            for bi in range(len(ns)):
                for i in range(hps):
                    q_pair = pair(_rows(ns[bi]), i, 0) * scale
                    do = do_ref[_rows(ns[bi]), i * LANE:(i + 1) * LANE]
                    do16 = do.astype(BF16)
                    for h in range(2):
                        if compact:
                            st_ref, head = rest[1], 2 * i + h
                            ebar = st_ref[_rows(ns[bi]), head:head + 1]
                            lcol = st_ref[_rows(ns[bi]), STAT_LSE_LANE + head:STAT_LSE_LANE + head + 1]
                        else:
                            ebar = jnp.sum(own(h, do * rest[1][_rows(ns[bi]), i * LANE:(i + 1) * LANE]), -1, keepdims=True)
                            lcol = rest[2][_rows(ns[bi]), i * LANE + h * HEAD_DIM:i * LANE + h * HEAD_DIM + 1]
                        qs[(bi, i, h)] = (own(h, q_pair), q_pair, own(h, do16), do16, ebar, lcol)
            raw = []
            for bi, i, h, sd in chains:
                q, _, do_h, _, _, _ = qs[(bi, i, h)]
                bias_blk = bias_ref[2 * i + h, :, :BAND] if sd else bias_ref[2 * i + h, :, BAND:]
                s = lax.dot_general(q, pair(key_rows(bi, sd), i, 1), NT, preferred_element_type=F32) + bias_blk
                dp = lax.dot_general(do_h, pair(key_rows(bi, sd), i, 2), NT, preferred_element_type=F32)
                raw.append((s, dp))
            soft = []
            for (bi, i, h, sd), (s, dp) in zip(chains, raw):
                ebar, lcol = qs[(bi, i, h)][4:]
                p = jnp.exp(s - lcol)
                ds = p * (dp - ebar)
                if sd:
                    db_ref[2 * i + h, :, :BAND] += ds
                else:
                    db_ref[2 * i + h, :, BAND:] += ds
                soft.append((p.astype(BF16), ds.astype(BF16)))
            grads = {}
            for (bi, i, h, sd), (p16, ds16) in zip(chains, soft):
                _, q_pair, _, do16 = qs[(bi, i, h)][:4]
                grads[(bi, i, h, sd)] = (
                    jnp.dot(ds16, pair(key_rows(bi, sd), i, 1), preferred_element_type=F32),
                    lax.dot_general(ds16, q_pair, TN, preferred_element_type=F32),
                    lax.dot_general(p16, do16, TN, preferred_element_type=F32))
            both = lambda bi, i, sd, which: jnp.where(first_head, grads[(bi, i, 0, sd)][which],
                                                      grads[(bi, i, 1, sd)][which])
            carry = list(carry) if carry is not None else None
            for bi, n in enumerate(ns):
                for i in range(hps):
                    base = 3 * LANE * i
                    dq = both(bi, i, 0, 0)
                    if with_prev:
                        dq = dq + both(bi, i, 1, 0)
                        dqkv_ref[_rows(n - 1), base + LANE:base + 2 * LANE] = (
                            carry[2 * i] + both(bi, i, 1, 1)).astype(BF16)
                        dqkv_ref[_rows(n - 1), base + 2 * LANE:base + 3 * LANE] = (
                            carry[2 * i + 1] + both(bi, i, 1, 2)).astype(BF16)
                    dqkv_ref[_rows(n), base:base + LANE] = (dq * scale).astype(BF16)
                carry = [t for i in range(hps) for t in (both(bi, i, 0, 1), both(bi, i, 0, 2))]
            return tuple(carry)

        carry = _for_blocks(blocks, nblk, 2 if hps == 1 else 1, None)
        for i in range(hps):
            base = 3 * LANE * i
            dqkv_ref[_rows(nblk - 1), base + LANE:base + 2 * LANE] = carry[2 * i].astype(BF16)
            dqkv_ref[_rows(nblk - 1), base + 2 * LANE:base + 3 * LANE] = carry[2 * i + 1].astype(BF16)

    qspec = pl.BlockSpec((None, None, sub, 3 * LANE * hps), lambda hp, b, r: (b, r, 0, hp))
    bspec = pl.BlockSpec((2 * hps, BAND, 2 * BAND), lambda hp, b, r: (hp, 0, 0))
    if compact:
        cspecs = [pl.BlockSpec((None, None, sub, ATT_OUT), lambda hp, b, r: (b, r, 0, 0)),
                  pl.BlockSpec((None, None, sub, LANE), lambda hp, b, r: (b, r, 0, 0))]
    else:
        cspecs = [pl.BlockSpec((None, sub, hps * LANE), lambda hp, b, r: (b, 0, r * (npair // hps) + hp))] * 3
    return _pcall(
        body, name=name, grid=(npair // hps, nb, d),
        in_specs=[qspec, bspec] + cspecs, out_specs=[qspec, bspec],
        out_shape=[jax.ShapeDtypeStruct(qkv4.shape, BF16),
                   jax.ShapeDtypeStruct((GROUP_HEADS, BAND, 2 * BAND), F32)],
        compiler_params=_params("parallel", "arbitrary", "arbitrary"),
    )(qkv4, bias, *cotangent)


def _head_lanes(first_lane, one_channel):
    c = lax.broadcasted_iota(jnp.int32, (ATT_OUT, LANE), 0)
    lane = lax.broadcasted_iota(jnp.int32, (ATT_OUT, LANE), 1)
    hit = lane == first_lane + c // HEAD_DIM
    if one_channel:
        hit = hit & (c % HEAD_DIM == 0)
    return hit.astype(BF16)


def _exact_dot(v, m01, dims=None):
    parts = _split3(v)
    if dims is None:
        dot = lambda t: jnp.dot(t, m01, preferred_element_type=F32)
    else:
        dot = lambda t: lax.dot_general(t, m01, dims, preferred_element_type=F32)
    return (dot(parts[0]) + dot(parts[1])) + dot(parts[2])


def _store_own_order(value, tile_refs, out_ref):
    d, per, width = out_ref.shape
    for j in range(width // LANE):
        tile_refs[j][...] = value[:, j * LANE:(j + 1) * LANE]
    for r in range(d):
        rows = pl.ds(r, per, stride=d)
        for j in range(width // LANE):
            out_ref[r, :, j * LANE:(j + 1) * LANE] = tile_refs[j][rows, :].astype(out_ref.dtype)


def _token_orders(x, dilations):
    nb, seq, kdim = x.shape
    tm = 512

    def body(x_ref, nat_ref, *rest):
        outs, tile_refs = rest[:len(dilations)], rest[len(dilations):]
        xv = x_ref[...]
        nat_ref[...] = xv.astype(BF16)
        for o_ref in outs:
            _store_own_order(xv, tile_refs, o_ref)

    outs = _pcall(
        body, name="token_orders", grid=(nb, seq // tm), in_specs=[_tok_spec(tm, kdim)],
        out_specs=[_tok_spec(tm, kdim)]
        + [pl.BlockSpec((None, d, tm // d, kdim), lambda b, i: (b, 0, i, 0)) for d in dilations],
        out_shape=[jax.ShapeDtypeStruct((nb, seq, kdim), BF16)]
        + [jax.ShapeDtypeStruct((nb, d, seq // d, kdim), BF16) for d in dilations],
        scratch_shapes=[pltpu.VMEM((tm, LANE), F32)] * (kdim // LANE),
        compiler_params=_params("parallel", "parallel"),
    )(x)
    return [outs[0]] + [o.reshape(nb, seq, kdim) for o in outs[1:]]


def _natural_rows(p_ref, tile_refs):
    d, per, width = p_ref.shape
    for r in range(d):
        rows = pl.ds(r, per, stride=d)
        for j in range(width // LANE):
            tile_refs[j][rows, :] = p_ref[r, :, j * LANE:(j + 1) * LANE]
    return jnp.concatenate([tile_refs[j][...] for j in range(width // LANE)], axis=1)


def _combine_fwd(o0, l0, dilated, gatt):
    nb, seq, _ = gatt.shape
    tm = 512
    ntile = ATT_OUT // LANE

    def body(o0_ref, l0_ref, o1_ref, l1_ref, o2_ref, l2_ref, g_ref, oa_ref, oatt_ref, lse_ref, *tile_refs):
        spread = _head_lanes(0, False)
        l0v = l0_ref[...]
        l1v = _exact_dot(_natural_rows(l1_ref, tile_refs), spread, NT)
        l2v = _exact_dot(_natural_rows(l2_ref, tile_refs), spread, NT)
        m = jnp.maximum(jnp.maximum(l0v, l1v), l2v)
        tot = m + jnp.log(jnp.exp(l0v - m) + jnp.exp(l1v - m) + jnp.exp(l2v - m))
        o = jnp.exp(l0v - tot) * o0_ref[...]
        o = o + jnp.exp(l1v - tot) * _natural_rows(o1_ref, tile_refs)
        o = o + jnp.exp(l2v - tot) * _natural_rows(o2_ref, tile_refs)
        g = g_ref[...]
        oa_ref[...] = (o * (g * _sigmoid(g))).astype(BF16)
        oatt_ref[...] = o
        lse_ref[...] = tot

    spec = pl.BlockSpec((None, tm, ATT_OUT), lambda b, i: (b, i, 0))
    own = lambda t: pl.BlockSpec((None, t.shape[1], tm // t.shape[1], t.shape[3]), lambda b, i: (b, 0, i, 0))
    (o1, l1), (o2, l2) = dilated
    return _pcall(
        body, name="attn_combine", grid=(nb, seq // tm),
        in_specs=[spec, spec, own(o1), own(l1), own(o2), own(l2), spec], out_specs=[spec] * 3,
        out_shape=[jax.ShapeDtypeStruct((nb, seq, ATT_OUT), BF16), jax.ShapeDtypeStruct((nb, seq, ATT_OUT), F32),
                   jax.ShapeDtypeStruct((nb, seq, ATT_OUT), F32)],
        scratch_shapes=[pltpu.VMEM((tm, LANE), F32)] * ntile,
        compiler_params=_params("parallel", "parallel"),
    )(o0, l0, o1, l1, o2, l2, gatt)


def _combine_bwd(doa, gatt, o_att, lse, dilations):
    nb, seq, _ = gatt.shape
    tm = 512

    def body(doa_ref, g_ref, o_ref, l_ref, do_ref, dg_ref, *rest):
        ntile = ATT_OUT // LANE
        outs, tile_refs = rest[:-ntile], rest[-ntile:]
        g = g_ref[...]
        sg = _sigmoid(g)
        do = doa_ref[...] * (g * sg)
        do_ref[...] = do
        stats = (_exact_dot(do * o_ref[...], _head_lanes(0, False))
                 + _exact_dot(l_ref[...], _head_lanes(STAT_LSE_LANE, True)))
        dg_ref[...] = (doa_ref[...] * o_ref[...] * (sg * (1.0 + g * (1.0 - sg)))).astype(BF16)
        for k in range(len(dilations)):
            _store_own_order(do, tile_refs, outs[2 * k])
            _store_own_order(stats, tile_refs, outs[2 * k + 1])

    spec = pl.BlockSpec((None, tm, ATT_OUT), lambda b, i: (b, i, 0))
    own = lambda d, width: pl.BlockSpec((None, d, tm // d, width), lambda b, i: (b, 0, i, 0))
    outs = _pcall(
        body, name="attn_combine_bwd", grid=(nb, seq // tm), in_specs=[spec] * 4,
        out_specs=[spec, spec] + [own(d, w) for d in dilations for w in (ATT_OUT, LANE)],
        out_shape=[jax.ShapeDtypeStruct((nb, seq, ATT_OUT), F32), jax.ShapeDtypeStruct((nb, seq, ATT_OUT), BF16)]
        + [jax.ShapeDtypeStruct((nb, d, seq // d, w), t) for d in dilations for w, t in ((ATT_OUT, BF16), (LANE, F32))],
        scratch_shapes=[pltpu.VMEM((tm, LANE), F32)] * (ATT_OUT // LANE),
        compiler_params=_params("parallel", "parallel"),
    )(doa, gatt, o_att, lse)
    return outs[0], outs[1], outs[2:]


CONV_TM = 1024
CONV_TC = 1024


def _shift_down(cur, halo, k):
    rolled = pltpu.roll(cur, k, 0)
    hro = pltpu.roll(halo, k, 0)
    row = lax.broadcasted_iota(jnp.int32, hro.shape, 0)
    first = jnp.where(row < k, hro, rolled[:8])
    return first if cur.shape[0] == 8 else jnp.concatenate([first, rolled[8:]], axis=0)


def _shift_up(cur, halo, k):
    n = cur.shape[0]
    rolled = pltpu.roll(cur, n - k, 0)
    hro = pltpu.roll(halo, 8 - k, 0)
    row = lax.broadcasted_iota(jnp.int32, hro.shape, 0)
    return jnp.concatenate([rolled[:n - 8], jnp.where(row >= 8 - k, hro, rolled[n - 8:])], axis=0)


def _conv_pre(cur, halo, w_ref, b_ref):
    acc = cur * w_ref[3:4, :] + b_ref[...]
    for k in range(1, 4):
        acc = acc + _shift_down(cur, halo, k) * w_ref[3 - k:4 - k, :]
    return acc


def _conv_specs(seq):
    nblk = seq // CONV_TM
    cur = pl.BlockSpec((None, CONV_TM, CONV_TC), lambda cb, b, i: (b, i, cb))
    prev = pl.BlockSpec((None, 8, CONV_TC), lambda cb, b, i: (b, jnp.maximum(i * (CONV_TM // 8) - 1, 0), cb))
    nxt = pl.BlockSpec((None, 8, CONV_TC),
                       lambda cb, b, i: (b, jnp.minimum((i + 1) * (CONV_TM // 8), seq // 8 - 1), cb))
    wspec = pl.BlockSpec((4, CONV_TC), lambda cb, b, i: (0, cb))
    bspec = pl.BlockSpec((1, CONV_TC), lambda cb, b, i: (0, cb))
    return nblk, cur, prev, nxt, wspec, bspec


def _conv_fwd(xin, w4, bias, name):
    nb, seq, ch = xin.shape
    _, cur, prev, _, wspec, bspec = _conv_specs(seq)

    def body(x_ref, h_ref, w_ref, b_ref, o_ref):
        halo = jnp.where(pl.program_id(2) > 0, h_ref[...], 0.0)
        pre = _conv_pre(x_ref[...], halo, w_ref, b_ref)
        o_ref[...] = pre * _sigmoid(pre)

    return _pcall(
        body, name=name, grid=(ch // CONV_TC, nb, seq // CONV_TM),
        in_specs=[cur, prev, wspec, bspec], out_specs=cur,
        out_shape=jax.ShapeDtypeStruct(xin.shape, F32),
        compiler_params=_params("parallel", "parallel", "parallel"),
    )(xin, xin, w4, bias)


def _conv_bwd(dact, xin, w4, bias, name):
    nb, seq, ch = xin.shape
    nblk, cur, prev, nxt, wspec, bspec = _conv_specs(seq)

    def dsilu(pre):
        sg = _sigmoid(pre)
        return sg * (1.0 + pre * (1.0 - sg))

    def body(da_ref, dn_ref, x_ref, xp_ref, xn_ref, w_ref, b_ref, dx_ref, s_ref):
        b, i = pl.program_id(1), pl.program_id(2)

        @pl.when((b == 0) & (i == 0))
        def _():
            s_ref[...] = jnp.zeros_like(s_ref)

        halo = jnp.where(i > 0, xp_ref[...], 0.0)
        x = x_ref[...]
        dpre = da_ref[...] * dsilu(_conv_pre(x, halo, w_ref, b_ref))
        s_ref[3:4, :] += jnp.sum(dpre * x, 0, keepdims=True)
        for k in range(1, 4):
            s_ref[3 - k:4 - k, :] += jnp.sum(dpre * _shift_down(x, halo, k), 0, keepdims=True)
        s_ref[4:5, :] += jnp.sum(dpre, 0, keepdims=True)
        dpre_next = dn_ref[...] * dsilu(_conv_pre(xn_ref[...], x[CONV_TM - 8:], w_ref, b_ref))
        dpre_next = jnp.where(i < nblk - 1, dpre_next, 0.0)
        acc = dpre * w_ref[3:4, :]
        for j in range(1, 4):
            acc = acc + _shift_up(dpre, dpre_next, j) * w_ref[3 - j:4 - j, :]
        dx_ref[...] = acc.astype(BF16)

    return _pcall(
        body, name=name, grid=(ch // CONV_TC, nb, seq // CONV_TM),
        in_specs=[cur, nxt, cur, prev, nxt, wspec, bspec],
        out_specs=[cur, pl.BlockSpec((8, CONV_TC), lambda cb, b, i: (0, cb))],
        out_shape=[jax.ShapeDtypeStruct(xin.shape, BF16), jax.ShapeDtypeStruct((8, ch), F32)],
        compiler_params=_params("parallel", "arbitrary", "arbitrary"),
    )(dact, dact, xin, xin, xin, w4, bias)


def _softplus_sig(dt_raw, dt_bias_row):
    nb, seq, _ = dt_raw.shape
    tm = 512

    def body(r_ref, b_ref, sp_ref, sg_ref):
        v = r_ref[...] + b_ref[...]
        sp_ref[...] = jnp.maximum(v, 0.0) + jnp.log1p(jnp.exp(-jnp.abs(v)))
        sg_ref[...] = _sigmoid(v)

    spec = pl.BlockSpec((None, tm, LANE), lambda b, i: (b, i, 0))
    return _pcall(
        body, name="dt_softplus", grid=(nb, seq // tm),
        in_specs=[spec, pl.BlockSpec((1, LANE), lambda b, i: (0, 0))], out_specs=[spec, spec],
        out_shape=[jax.ShapeDtypeStruct(dt_raw.shape, F32)] * 2,
        compiler_params=_params("parallel", "parallel"),
    )(dt_raw, dt_bias_row)


def _group_lanes(t):
    pads = [(0, 0)] * (t.ndim - 1) + [(0, LANE - GROUP_SSM_HEADS)]
    return jnp.stack([jnp.pad(t[..., GROUP_SSM_HEADS * g:GROUP_SSM_HEADS * (g + 1)], pads) for g in range(SSM_GROUPS)])


def _ungroup_lanes(t):
    return jnp.concatenate([t[g][..., :GROUP_SSM_HEADS] for g in range(SSM_GROUPS)], axis=-1)


def _decays(dt, al_ref):
    row = lax.broadcasted_iota(jnp.int32, (CHUNK, CHUNK), 0)
    col = lax.broadcasted_iota(jnp.int32, (CHUNK, CHUNK), 1)
    tril = (row >= col).astype(BF16)
    triu = (row <= col).astype(BF16)
    arow = -jnp.exp(al_ref[...])
    hi, mid, lo = _split3(dt * arow)
    down = lambda t: jnp.dot(tril, t, preferred_element_type=F32)
    across = lambda t: lax.dot_general(t, triu, TN, preferred_element_type=F32)
    acs = (down(hi) + down(mid)) + down(lo)
    acs_t = (across(hi) + across(mid)) + across(lo)
    return arow, acs, acs_t, row >= col, triu


STEP_CHUNKS = 8


def _ssd_specs(nb, seq):
    nc = seq // CHUNK
    hw = GROUP_SSM_HEADS * HEAD_DIM
    rows, steps = STEP_CHUNKS * CHUNK, nc // STEP_CHUNKS

    def mk(rev):
        cidx = (lambda c: steps - 1 - c) if rev else (lambda c: c)
        wide = pl.BlockSpec((None, rows, hw), lambda g, b, c: (b, cidx(c), g))
        xbc = pl.BlockSpec((None, rows, XBC_GROUP), lambda g, b, c: (b, cidx(c), g))
        lanes = pl.BlockSpec((None, None, rows, LANE), lambda g, b, c: (g, b, cidx(c), 0))
        prev = pl.BlockSpec((None, STEP_CHUNKS, None, D_STATE, hw), lambda g, b, c: (b, cidx(c), g, 0, 0))
        return wide, xbc, lanes, prev

    grow = pl.BlockSpec((None, 1, LANE), lambda g, b, c: (g, 0, 0))
    nwspec = pl.BlockSpec((1, hw), lambda g, b, c: (0, g))
    return nc, steps, hw, mk, grow, nwspec


def _head_expand():
    hw = GROUP_SSM_HEADS * HEAD_DIM
    r = lax.broadcasted_iota(jnp.int32, (LANE, hw), 0)
    c = lax.broadcasted_iota(jnp.int32, (LANE, hw), 1)
    return ((c // HEAD_DIM) == r).astype(BF16)


def _split3(v):
    hi = v.astype(BF16)
    rest = v - hi.astype(F32)
    mid = rest.astype(BF16)
    return hi, mid, (rest - mid.astype(F32)).astype(BF16)


def _to_channels(v, e):
    hi, mid, lo = _split3(v)
    dot = lambda t: jnp.dot(t, e, preferred_element_type=F32)
    return (dot(hi) + dot(mid)) + dot(lo)


def _to_heads(w, e):
    hi, mid, lo = _split3(w)
    dot = lambda t: lax.dot_general(t, e, (((1,), (1,)), ((), ())), preferred_element_type=F32)
    return (dot(hi) + dot(mid)) + dot(lo)


def _row8(v):
    return jnp.broadcast_to(v, (8, v.shape[1]))


def _ssd_chunk_setup(dt, al_ref, ds_ref):
    arow, acs, acs_t, causal, triu = _decays(dt, al_ref)
    e = _head_expand()
    dtx = _to_channels(dt, e)
    acsx = _to_channels(acs, e)
    lastx = acsx[CHUNK - 1:CHUNK, :]
    dskx = _to_channels(_row8(ds_ref[...]), e)[0:1, :]
    return arow, acs, acs_t, causal, triu, e, dtx, acsx, lastx, dskx


def _ssd_fwd(xbc, dtg, z, alog_g, dskip_g, normw):
    nb, seq, _ = xbc.shape
    nc, steps, hw, mk, grow, nwspec = _ssd_specs(nb, seq)
    wide, xbc_spec, lanes, prev = mk(False)
    tn = (((0,), (0,)), ((), ()))

    def body(xbc_ref, dt_ref, z_ref, al_ref, ds_ref, nw_ref, ys_ref, y_ref, sp_ref, st_ref):
        @pl.when(pl.program_id(2) == 0)
        def _():
            st_ref[...] = jnp.zeros_like(st_ref)

        for ci in range(STEP_CHUNKS):
            chunk(ci, xbc_ref, dt_ref, z_ref, al_ref, ds_ref, nw_ref, ys_ref, y_ref, sp_ref, st_ref)

    def chunk(ci, xbc_ref, dt_ref, z_ref, al_ref, ds_ref, nw_ref, ys_ref, y_ref, sp_ref, st_ref):
        rows = slice(ci * CHUNK, (ci + 1) * CHUNK)
        dt = dt_ref[rows, :]
        _, acs, acs_t, causal, _, _, dtx, acsx, lastx, dskx = _ssd_chunk_setup(dt, al_ref, ds_ref)
        bmat = xbc_ref[rows, GROUP_CH:GROUP_CH + D_STATE].astype(BF16)
        cmat = xbc_ref[rows, GROUP_CH + D_STATE:].astype(BF16)
        cb = lax.dot_general(cmat, bmat, (((1,), (1,)), ((), ())), preferred_element_type=F32)
        x = xbc_ref[rows, :GROUP_CH]
        xdt = x * dtx
        xdt16 = xdt.astype(BF16)
        first_head = lax.broadcasted_iota(jnp.int32, (CHUNK, LANE), 1) < HEAD_DIM
        pairs = []
        for hp in range(GROUP_SSM_HEADS // 2):
            xp = xdt16[:, hp * LANE:(hp + 1) * LANE]
            two = []
            for j in (2 * hp, 2 * hp + 1):
                lmat = jnp.exp(jnp.where(causal, acs[:, j:j + 1] - acs_t[j:j + 1, :], -jnp.inf))
                two.append(jnp.dot((cb * lmat).astype(BF16), xp, preferred_element_type=F32))
            pairs.append(jnp.where(first_head, two[0], two[1]))
        yd = jnp.concatenate(pairs, axis=1)
        s_prev = st_ref[...]
        s16 = s_prev.astype(BF16)
        sp_ref[ci] = s16
        yo = jnp.dot(cmat, s16, preferred_element_type=F32) * jnp.exp(acsx)
        sts = lax.dot_general(bmat, (xdt * jnp.exp(lastx - acsx)).astype(BF16), tn, preferred_element_type=F32)
        st_ref[...] = s_prev * jnp.exp(lastx) + sts
        y = yd + yo + dskx * x
        zz = z_ref[rows, :]
        u = y * (zz * _sigmoid(zz))
        rn = lax.rsqrt(jnp.mean(u * u, -1, keepdims=True) + RMS_EPS)
        ys_ref[rows, :] = (u * rn * nw_ref[...]).astype(BF16)
        y_ref[rows, :] = y

    return _pcall(
        body, name="ssd_fwd", grid=(SSM_GROUPS, nb, steps),
        in_specs=[xbc_spec, lanes, wide, grow, grow, nwspec],
        out_specs=[wide, wide, prev],
        out_shape=[jax.ShapeDtypeStruct((nb, seq, D_INNER), BF16), jax.ShapeDtypeStruct((nb, seq, D_INNER), F32),
                   jax.ShapeDtypeStruct((nb, nc, SSM_GROUPS, D_STATE, hw), BF16)],
        scratch_shapes=[pltpu.VMEM((D_STATE, hw), F32)],
        compiler_params=_params("parallel", "parallel", "arbitrary"),
    )(xbc, dtg, z, alog_g, dskip_g, normw)


def _ssd_bwd(xbc, dtg, sgg, z, y, dys, sprev, alog_g, dskip_g, normw):
    nb, seq, _ = xbc.shape
    nc, steps, hw, mk, grow, nwspec = _ssd_specs(nb, seq)
    wide, xbc_spec, lanes, prev = mk(True)
    nt = (((1,), (1,)), ((), ()))
    tn = (((0,), (0,)), ((), ()))

    def body(xbc_ref, dt_ref, sg_ref, z_ref, y_ref, dys_ref, sp_ref, al_ref, ds_ref, nw_ref,
             dxbc_ref, ddt_ref, dz_ref, small_ref, dnw_ref, g_ref):
        b, c = pl.program_id(1), pl.program_id(2)

        @pl.when((b == 0) & (c == 0))
        def _():
            small_ref[...] = jnp.zeros_like(small_ref)
            dnw_ref[...] = jnp.zeros_like(dnw_ref)

        @pl.when(c == 0)
        def _():
            g_ref[...] = jnp.zeros_like(g_ref)

        for ci in reversed(range(STEP_CHUNKS)):
            chunk(ci, xbc_ref, dt_ref, sg_ref, z_ref, y_ref, dys_ref, sp_ref, al_ref, ds_ref, nw_ref,
                  dxbc_ref, ddt_ref, dz_ref, small_ref, dnw_ref, g_ref)

    def chunk(ci, xbc_ref, dt_ref, sg_ref, z_ref, y_ref, dys_ref, sp_ref, al_ref, ds_ref, nw_ref,
              dxbc_ref, ddt_ref, dz_ref, small_ref, dnw_ref, g_ref):
        rows = slice(ci * CHUNK, (ci + 1) * CHUNK)
        yv, zz, dys_v, nw = y_ref[rows, :], z_ref[rows, :], dys_ref[rows, :], nw_ref[...]
        sz = _sigmoid(zz)
        silu = zz * sz
        u = yv * silu
        rn = lax.rsqrt(jnp.mean(u * u, -1, keepdims=True) + RMS_EPS)
        gn = dys_v * nw
        du = rn * gn - u * (rn * rn * rn) * jnp.mean(u * gn, -1, keepdims=True)
        dnw_ref[...] += jnp.sum(dys_v * u * rn, 0, keepdims=True)
        dy = du * silu
        dz_ref[rows, :] = du * yv * (sz * (1.0 + zz * (1.0 - sz)))

        dt = dt_ref[rows, :]
        arow, acs, acs_t, causal, triu, e, dtx, acsx, lastx, dskx = _ssd_chunk_setup(dt, al_ref, ds_ref)
        dfsx = jnp.exp(acsx)
        dtex = jnp.exp(lastx - acsx)
        bmat = xbc_ref[rows, GROUP_CH:GROUP_CH + D_STATE].astype(BF16)
        cmat = xbc_ref[rows, GROUP_CH + D_STATE:].astype(BF16)
        cb = lax.dot_general(cmat, bmat, nt, preferred_element_type=F32)
        x = xbc_ref[rows, :GROUP_CH]
        xdt = x * dtx
        xdt16 = xdt.astype(BF16)
        xdte = xdt * dtex
        dy16 = dy.astype(BF16)
        dyd = dy * dfsx
        dyd16 = dyd.astype(BF16)
        s16 = sp_ref[ci]
        g = g_ref[...]
        g16 = g.astype(BF16)
        cs = jnp.dot(cmat, s16, preferred_element_type=F32)
        dc_off = lax.dot_general(dyd16, s16, nt, preferred_element_type=F32)
        g_here = lax.dot_general(cmat, dyd16, tn, preferred_element_type=F32)
        bg = jnp.dot(bmat, g16, preferred_element_type=F32)
        db_st = lax.dot_general(xdte.astype(BF16), g16, nt, preferred_element_type=F32)
        ddte_w = bg * xdte
        dcd = _to_heads(_row8(jnp.sum(g * s16.astype(F32), 0, keepdims=True)), e)[0:1, :]
        lane = lax.broadcasted_iota(jnp.int32, (CHUNK, LANE), 1)
        first_head = lane < HEAD_DIM
        sub = lax.broadcasted_iota(jnp.int32, (CHUNK, LANE), 0)
        dacs = jnp.zeros((CHUNK, LANE), F32)
        colsums = jnp.zeros((CHUNK, LANE), F32)
        dcb = jnp.zeros((CHUNK, CHUNK), F32)
        pairs = []
        for hp in range(GROUP_SSM_HEADS // 2):
            xp = xdt16[:, hp * LANE:(hp + 1) * LANE]
            dyp = dy16[:, hp * LANE:(hp + 1) * LANE]
            two = []
            for idx, j in enumerate((2 * hp, 2 * hp + 1)):
                lmat = jnp.exp(jnp.where(causal, acs[:, j:j + 1] - acs_t[j:j + 1, :], -jnp.inf))
                mf = cb * lmat
                dy_h = jnp.where(first_head if idx == 0 else jnp.logical_not(first_head), dyp, jnp.zeros_like(dyp))
                dm = lax.dot_general(dy_h, xp, nt, preferred_element_type=F32)
                two.append(lax.dot_general(mf.astype(BF16), dyp, tn, preferred_element_type=F32))
                wmat = dm * mf
                dcb = dcb + dm * lmat
                dacs = jnp.where(lane == j, jnp.sum(wmat, -1, keepdims=True), dacs)
                colsums = jnp.where(sub == j, jnp.sum(wmat, 0, keepdims=True), colsums)
            pairs.append(jnp.where(first_head, two[0], two[1]))
        dxdt = bg * dtex + jnp.concatenate(pairs, axis=1)
        dacs = dacs - colsums.T + _to_heads(dyd * cs - ddte_w, e)
        cd_row = jnp.exp(acs[CHUNK - 1:CHUNK, :])
        tail = _to_heads(_row8(jnp.sum(ddte_w, 0, keepdims=True)), e)[0:1, :] + dcd * cd_row
        dacs = dacs + jnp.where(sub == CHUNK - 1, tail, 0.0)
        d_hi, d_mid, d_lo = _split3(dacs)
        up = lambda t: jnp.dot(triu, t, preferred_element_type=F32)
        da = (up(d_hi) + up(d_mid)) + up(d_lo)
        ddt_raw = (da * arow + _to_heads(dxdt * x, e)) * sg_ref[rows, :]
        ddt_ref[rows, :] = ddt_raw
        small_ref[0:1, :] += jnp.sum(da * dt, 0, keepdims=True) * arow
        small_ref[1:2, :] += _to_heads(_row8(jnp.sum(dy * x, 0, keepdims=True)), e)[0:1, :]
        small_ref[2:3, :] += jnp.sum(ddt_raw, 0, keepdims=True)
        dcb16 = dcb.astype(BF16)
        dxbc_ref[rows, GROUP_CH + D_STATE:] = dc_off + jnp.dot(dcb16, bmat, preferred_element_type=F32)
        dxbc_ref[rows, GROUP_CH:GROUP_CH + D_STATE] = db_st + lax.dot_general(dcb16, cmat, tn,
                                                                               preferred_element_type=F32)
        dxbc_ref[rows, :GROUP_CH] = dxdt * dtx + dskx * dy
        g_ref[...] = g * jnp.exp(lastx) + g_here

    return _pcall(
        body, name="ssd_bwd", grid=(SSM_GROUPS, nb, steps),
        in_specs=[xbc_spec, lanes, lanes, wide, wide, wide, prev, grow, grow, nwspec],
        out_specs=[xbc_spec, lanes, wide,
                   pl.BlockSpec((None, 8, LANE), lambda g, b, c: (g, 0, 0)), nwspec],
        out_shape=[jax.ShapeDtypeStruct((nb, seq, CONV_DIM), F32),
                   jax.ShapeDtypeStruct((SSM_GROUPS, nb, seq, LANE), F32),
                   jax.ShapeDtypeStruct((nb, seq, D_INNER), F32),
                   jax.ShapeDtypeStruct((SSM_GROUPS, 8, LANE), F32),
                   jax.ShapeDtypeStruct((1, D_INNER), F32)],
        scratch_shapes=[pltpu.VMEM((D_STATE, hw), F32)],
        compiler_params=_params("parallel", "arbitrary", "arbitrary"),
    )(xbc, dtg, sgg, z, y, dys, sprev, alog_g, dskip_g, normw)


EW_TM = 256


def _merge_fwd(y_a, y_b, gm, bgate):
    nb, seq, _ = y_a.shape

    def body(a_ref, b_ref, ga_ref, gb_ref, bg_ref, o_ref):
        sa = _sigmoid(ga_ref[...] + bg_ref[0:1, :])
        sb = _sigmoid(gb_ref[...] + bg_ref[1:2, :])
        o_ref[...] = (sa * a_ref[...] + sb * b_ref[...]).astype(BF16)

    spec = pl.BlockSpec((None, EW_TM, D_MODEL), lambda b, i: (b, i, 0))
    spec1 = pl.BlockSpec((None, EW_TM, D_MODEL), lambda b, i: (b, i, 1))
    return _pcall(
        body, name="merge_fwd", grid=(nb, seq // EW_TM),
        in_specs=[spec, spec, spec, spec1, pl.BlockSpec((8, D_MODEL), lambda b, i: (0, 0))], out_specs=spec,
        out_shape=jax.ShapeDtypeStruct((nb, seq, D_MODEL), BF16),
        compiler_params=_params("parallel", "parallel"),
    )(y_a, y_b, gm, gm, bgate)


def _merge_bwd(dmerged, y_a, y_b, gm, bgate):
    nb, seq, _ = y_a.shape

    def body(dm_ref, a_ref, b_ref, ga_ref, gb_ref, bg_ref, dya_ref, dyb_ref, dg_ref, s_ref):
        @pl.when((pl.program_id(0) == 0) & (pl.program_id(1) == 0))
        def _():
            s_ref[...] = jnp.zeros_like(s_ref)

        dm = dm_ref[...]
        sa = _sigmoid(ga_ref[...] + bg_ref[0:1, :])
        sb = _sigmoid(gb_ref[...] + bg_ref[1:2, :])
        dya_ref[...] = (dm * sa).astype(BF16)
        dyb_ref[...] = (dm * sb).astype(BF16)
        dga = dm * a_ref[...] * (sa * (1.0 - sa))
        dgb = dm * b_ref[...] * (sb * (1.0 - sb))
        dg_ref[:, :D_MODEL] = dga.astype(BF16)
        dg_ref[:, D_MODEL:] = dgb.astype(BF16)
        s_ref[0:1, :] += jnp.sum(dga, 0, keepdims=True)
        s_ref[1:2, :] += jnp.sum(dgb, 0, keepdims=True)

    spec = pl.BlockSpec((None, EW_TM, D_MODEL), lambda b, i: (b, i, 0))
    spec1 = pl.BlockSpec((None, EW_TM, D_MODEL), lambda b, i: (b, i, 1))
    small = pl.BlockSpec((8, D_MODEL), lambda b, i: (0, 0))
    return _pcall(
        body, name="merge_bwd", grid=(nb, seq // EW_TM),
        in_specs=[spec, spec, spec, spec, spec1, small],
        out_specs=[spec, spec, pl.BlockSpec((None, EW_TM, 2 * D_MODEL), lambda b, i: (b, i, 0)), small],
        out_shape=[jax.ShapeDtypeStruct((nb, seq, D_MODEL), BF16), jax.ShapeDtypeStruct((nb, seq, D_MODEL), BF16),
                   jax.ShapeDtypeStruct((nb, seq, 2 * D_MODEL), BF16), jax.ShapeDtypeStruct((8, D_MODEL), F32)],
        compiler_params=_params("arbitrary", "arbitrary"),
    )(dmerged, y_a, y_b, gm, gm, bgate)


def _ln_loss(x, mix, gp, pw, target, bgate, ln_g, ln_b):
    nb, seq, _ = x.shape

    def body(x_ref, mix_ref, gp_ref, pw_ref, t_ref, bg_ref, g_ref, b_ref, dx_ref, dp_ref, dpw_ref, dgp_ref, s_ref):
        @pl.when((pl.program_id(0) == 0) & (pl.program_id(1) == 0))
        def _():
            s_ref[...] = jnp.zeros_like(s_ref)

        sp = _sigmoid(gp_ref[...] + bg_ref[2:3, :])
        pw = pw_ref[...]
        pre = ALPHA * x_ref[...] + mix_ref[...] + sp * pw
        mu = jnp.mean(pre, -1, keepdims=True)
        cen = pre - mu
        rstd = lax.rsqrt(jnp.mean(cen * cen, -1, keepdims=True) + LN_EPS)
        xhat = cen * rstd
        err = xhat * g_ref[...] + b_ref[...] - t_ref[...]
        dy = err * (1.0 / D_MODEL)
        dxh = dy * g_ref[...]
        dpre = rstd * (dxh - jnp.mean(dxh, -1, keepdims=True) - xhat * jnp.mean(dxh * xhat, -1, keepdims=True))
        dx_ref[...] = ALPHA * dpre
        dp_ref[...] = dpre.astype(BF16)
        dpw_ref[...] = (dpre * sp).astype(BF16)
        dgp = dpre * pw * (sp * (1.0 - sp))
        dgp_ref[...] = dgp.astype(BF16)
        s_ref[0:1, :] += jnp.sum(dy * xhat, 0, keepdims=True)
        s_ref[1:2, :] += jnp.sum(dy, 0, keepdims=True)
        s_ref[2:3, :] += jnp.sum(dgp, 0, keepdims=True)
        s_ref[3:4, :] += jnp.sum(err * err, 0, keepdims=True)

    spec = pl.BlockSpec((None, EW_TM, D_MODEL), lambda b, i: (b, i, 0))
    small = pl.BlockSpec((8, D_MODEL), lambda b, i: (0, 0))
    row = pl.BlockSpec((1, D_MODEL), lambda b, i: (0, 0))
    return _pcall(
        body, name="ln_loss", grid=(nb, seq // EW_TM),
        in_specs=[spec] * 5 + [small, row, row], out_specs=[spec] * 4 + [small],
        out_shape=[jax.ShapeDtypeStruct((nb, seq, D_MODEL), F32)] + [jax.ShapeDtypeStruct((nb, seq, D_MODEL), BF16)] * 3
        + [jax.ShapeDtypeStruct((8, D_MODEL), F32)],
        compiler_params=_params("arbitrary", "arbitrary"),
    )(x, mix, gp, pw, target, bgate, ln_g, ln_b)


def _adamw(w, g, m, v, name):
    rows, cols = w.shape
    tr = _row_tile(rows, cols, 8, 5 << 19)
    c1 = 1.0 - ADAM_B1 ** ADAM_STEP
    c2 = 1.0 - ADAM_B2 ** ADAM_STEP

    def body(w_ref, g_ref, m_ref, v_ref, d_ref, nm_ref, nv_ref):
        gv = g_ref[...]
        nm = ADAM_B1 * m_ref[...] + (1.0 - ADAM_B1) * gv
        nv = ADAM_B2 * v_ref[...] + (1.0 - ADAM_B2) * (gv * gv)
        d_ref[...] = -ADAM_LR * ((nm / c1) / (jnp.sqrt(nv / c2) + ADAM_EPS) + ADAM_WD * w_ref[...])
        nm_ref[...] = nm
        nv_ref[...] = nv

    spec = pl.BlockSpec((tr, cols), lambda i: (i, 0))
    return _pcall(
        body, name=name, grid=(rows // tr,), in_specs=[spec] * 4, out_specs=[spec] * 3,
        out_shape=[jax.ShapeDtypeStruct(w.shape, F32)] * 3, compiler_params=_params("parallel"),
    )(w, g, m, v)


def _sum_rows(parts, out_dtype, name):
    rows, cols = parts[0].shape
    tr = rows
    for cand in range(16, rows, 16):
        if rows % cand == 0 and cand * cols * 4 <= (1 << 20):
            tr = cand
    n = len(parts)

    def body(*refs):
        acc = refs[0][...].astype(F32)
        for r in refs[1:n]:
            acc = acc + r[...].astype(F32)
        refs[n][...] = acc.astype(out_dtype)

    spec = pl.BlockSpec((tr, cols), lambda i: (i, 0))
    return _pcall(
        body, name=name, grid=(rows // tr,), in_specs=[spec] * n, out_specs=spec,
        out_shape=jax.ShapeDtypeStruct((rows, cols), out_dtype), compiler_params=_params("parallel"),
    )(*parts)


def _place():
    return lax.axis_index("x"), lax.axis_index("y"), lax.axis_index("c")


def _other_chips(x, y):
    return [(1 - x, y), (x, 1 - y), (1 - x, 1 - y)]


def _remote(src, dst, send_sem, recv_sem, to):
    return pltpu.make_async_remote_copy(src_ref=src, dst_ref=dst, send_sem=send_sem, recv_sem=recv_sem,
                                        device_id=to, device_id_type=MESH)


ANY = pl.BlockSpec(memory_space=pl.ANY)
DMA_CHUNK_BYTES = 512 * 1024


def _row_chunks(rows, row_bytes):
    per = max(16, DMA_CHUNK_BYTES // row_bytes // 16 * 16)
    return [(s, min(per, rows - s)) for s in range(0, rows, per)]


def _row_tile(rows, cols, align, limit=1 << 21):
    best = None
    for cand in range(align, rows + 1, align):
        if rows % cand == 0 and cand * cols * 4 <= limit:
            best = cand
    return best or rows


def _allgather_pieces(pieces):
    n = len(pieces)
    halves = [_row_chunks(p.shape[0] // 2, p.shape[1] * p.dtype.itemsize) for p in pieces]
    entries = [(a, q, s, m, j) for a in range(n) for q, (s, m) in enumerate(halves[a]) for j in range(3)]
    slot = {(a, q, j): k for k, (a, q, _, _, j) in enumerate(entries)}
    n_ici = len(entries)

    def body(*refs):
        ins, outs = refs[:n], refs[n:2 * n]
        send_sems, recv_sems = refs[2 * n:]
        x, y, c = _place()
        me = 2 * x + y
        sibling = (x, y, 1 - c)
        chips = _other_chips(x, y)

        def landed(a, s, m, j, core):
            half = ins[a].shape[0] // 2
            return outs[a].at[2 * chips[j][0] + chips[j][1], pl.ds(core * half + s, m)]

        sent = []
        for k, (a, q, s, m, j) in enumerate(entries):
            if j < 2:
                half = ins[a].shape[0] // 2
                cp = _remote(ins[a].at[pl.ds(c * half + s, m)], outs[a].at[me, pl.ds(c * half + s, m)],
                             send_sems.at[k], recv_sems.at[k], (*chips[j], c))
                cp.start()
                sent.append(cp)

        def pass_to_sibling(k, blk):
            fw = _remote(blk, blk, send_sems.at[n_ici + k], recv_sems.at[n_ici + k], sibling)
            fw.start()
            sent.append(fw)

        for k, (a, q, s, m, j) in enumerate(entries):
            if j < 2:
                blk = landed(a, s, m, j, c)
                _remote(blk, blk, send_sems.at[k], recv_sems.at[k], (*chips[j], c)).wait_recv()
                first = q < (len(halves[a]) + 1) // 2
                if (j == 0) == first:
                    on = slot[(a, q, 2)]
                    rl = _remote(blk, blk, send_sems.at[on], recv_sems.at[on], (*chips[1 - j], c))
                    rl.start()
                    sent.append(rl)
                pass_to_sibling(k, blk)
        for k, (a, q, s, m, j) in enumerate(entries):
            if j == 2:
                blk = landed(a, s, m, j, c)
                _remote(blk, blk, send_sems.at[k], recv_sems.at[k], (*chips[j], c)).wait_recv()
                pass_to_sibling(k, blk)
        for k, (a, q, s, m, j) in enumerate(entries):
            blk = landed(a, s, m, j, 1 - c)
            _remote(blk, blk, send_sems.at[n_ici + k], recv_sems.at[n_ici + k], sibling).wait_recv()
        for cp in sent:
            cp.wait_send()

    gathered = _pcall(
        body, name="allgather_weights", in_specs=[ANY] * n, out_specs=[ANY] * n,
        out_shape=[jax.ShapeDtypeStruct((4,) + p.shape, p.dtype) for p in pieces],
        scratch_shapes=[pltpu.SemaphoreType.DMA((2 * n_ici,)), pltpu.SemaphoreType.DMA((2 * n_ici,))],
        compiler_params=pltpu.CompilerParams(has_side_effects=True),
    )(*pieces)
    x, y, _ = _place()
    return [lax.dynamic_update_slice(g, p[None], (2 * x + y, 0, 0)) for g, p in zip(gathered, pieces)]


def _sibling_exchange(grads):
    n = len(grads)
    chunks = [_row_chunks(g.shape[1] // 2, g.shape[2] * g.dtype.itemsize) for g in grads]
    n_sem = 4 * sum(len(ch) for ch in chunks)

    def body(*refs):
        ins, gots = refs[:n], refs[n:2 * n]
        send_sems, recv_sems = refs[2 * n:]
        x, y, c = _place()
        sibling = (x, y, 1 - c)
        work = []
        for a in range(n):
            half = ins[a].shape[1] // 2
            for piece in range(4):
                for s, m in chunks[a]:
                    k = len(work)
                    cp = _remote(ins[a].at[piece, pl.ds((1 - c) * half + s, m)], gots[a].at[piece, pl.ds(s, m)],
                                 send_sems.at[k], recv_sems.at[k], sibling)
                    cp.start()
                    work.append(cp)
        for cp in work:
            cp.wait()

    return _pcall(
        body, name="grad_sibling_exchange", in_specs=[ANY] * n, out_specs=[ANY] * n,
        out_shape=[jax.ShapeDtypeStruct((4, g.shape[1] // 2, g.shape[2]), g.dtype) for g in grads],
        scratch_shapes=[pltpu.SemaphoreType.DMA((n_sem,)), pltpu.SemaphoreType.DMA((n_sem,))],
        compiler_params=pltpu.CompilerParams(has_side_effects=True),
    )(*grads)


def _sibling_gather(fulls):
    n = len(fulls)
    chunks = [_row_chunks(f.shape[0] // 2, f.shape[1] * f.dtype.itemsize) for f in fulls]
    n_sem = sum(len(ch) for ch in chunks)

    def body(*refs):
        outs = refs[n:2 * n]
        send_sems, recv_sems = refs[2 * n:]
        x, y, c = _place()
        sibling = (x, y, 1 - c)
        work = []
        for a in range(n):
            h = outs[a].shape[0] // 2
            for s, m in chunks[a]:
                k = len(work)
                mine = outs[a].at[pl.ds(c * h + s, m)]
                cp = _remote(mine, mine, send_sems.at[k], recv_sems.at[k], sibling)
                cp.start()
                work.append((a, s, m, cp))
        for k, (a, s, m, cp) in enumerate(work):
            h = outs[a].shape[0] // 2
            cp.wait_send()
            theirs = outs[a].at[pl.ds((1 - c) * h + s, m)]
            _remote(theirs, theirs, send_sems.at[k], recv_sems.at[k], sibling).wait_recv()

    return _pcall(
        body, name="grad_sibling_gather", in_specs=[ANY] * n, out_specs=[ANY] * n,
        out_shape=[jax.ShapeDtypeStruct(f.shape, f.dtype) for f in fulls],
        input_output_aliases={a: a for a in range(n)},
        scratch_shapes=[pltpu.SemaphoreType.DMA((n_sem,)), pltpu.SemaphoreType.DMA((n_sem,))],
        compiler_params=pltpu.CompilerParams(has_side_effects=True),
    )(*fulls)


def _pair_sum(grad, got, place, name):
    _, rows, cols = grad.shape
    half = rows // 2
    tr = _row_tile(half, cols, 16)

    def body(p_ref, a_ref, b_ref, o_ref):
        o_ref[...] = (a_ref[...].astype(F32) + b_ref[...].astype(F32)).astype(BF16)

    return _pcall(
        body, name=name,
        grid_spec=pltpu.PrefetchScalarGridSpec(
            num_scalar_prefetch=1, grid=(4, half // tr),
            in_specs=[pl.BlockSpec((None, tr, cols), lambda k, i, p: (k, p[1] * (half // tr) + i, 0)),
                      pl.BlockSpec((None, tr, cols), lambda k, i, p: (k, i, 0))],
            out_specs=pl.BlockSpec((None, tr, cols), lambda k, i, p: (k, i, 0))),
        out_shape=jax.ShapeDtypeStruct((4, half, cols), BF16),
        compiler_params=_params("parallel", "parallel"),
    )(place, grad, got)


def _chip_sum(sums, got, place, name):
    _, h, cols = sums.shape
    tr = _row_tile(h, cols, 16)

    def body(p_ref, own_ref, g0, g1, g2, o_ref):
        o_ref[...] = ((own_ref[...].astype(F32) + g0[...].astype(F32)) + g1[...].astype(F32)) + g2[...].astype(F32)

    gspec = lambda j: pl.BlockSpec((None, tr, cols), lambda i, p: (j, i, 0))
    return _pcall(
        body, name=name,
        grid_spec=pltpu.PrefetchScalarGridSpec(
            num_scalar_prefetch=1, grid=(h // tr,),
            in_specs=[pl.BlockSpec((None, tr, cols), lambda i, p: (p[0], i, 0)), gspec(0), gspec(1), gspec(2)],
            out_specs=pl.BlockSpec((tr, cols), lambda i, p: (p[1] * (h // tr) + i, 0))),
        out_shape=jax.ShapeDtypeStruct((2 * h, cols), F32),
        compiler_params=_params("parallel"),
    )(place, sums, got, got, got)


def _allgather8(buf, name):
    rows = buf.shape[0]

    def body(in_ref, out_ref, send_sems, recv_sems):
        x, y, c = _place()
        me = 4 * x + 2 * y + c
        out_ref[me] = in_ref[...]
        work = []
        for rel in range(1, 8):
            fx, fy, fc = (rel >> 2) & 1, (rel >> 1) & 1, rel & 1
            to = (x ^ fx, y ^ fy, c ^ fc)
            cp = _remote(in_ref, out_ref.at[me], send_sems.at[rel - 1], recv_sems.at[rel - 1], to)
            cp.start()
            work.append((cp, 4 * to[0] + 2 * to[1] + to[2]))
        for rel, (cp, frm) in enumerate(work):
            cp.wait_send()
            blk = out_ref.at[frm]
            _remote(blk, blk, send_sems.at[rel], recv_sems.at[rel], (x, y, c)).wait_recv()

    return _pcall(
        body, name=name, in_specs=[pl.BlockSpec(memory_space=pltpu.VMEM)],
        out_specs=pl.BlockSpec(memory_space=pltpu.VMEM),
        out_shape=jax.ShapeDtypeStruct((8, rows, LANE), F32),
        scratch_shapes=[pltpu.SemaphoreType.DMA((7,)), pltpu.SemaphoreType.DMA((7,))],
        compiler_params=pltpu.CompilerParams(has_side_effects=True),
    )(buf)


def _pack_rows(arrs):
    parts = []
    for a in arrs:
        f = a.reshape(-1).astype(F32)
        parts.append(jnp.pad(f, (0, (-f.shape[0]) % LANE)))
    flat = jnp.concatenate(parts)
    rows = -(-flat.shape[0] // LANE)
    rows8 = -(-rows // 8) * 8
    return jnp.pad(flat, (0, rows8 * LANE - flat.shape[0])).reshape(rows8, LANE)


def _unpack_rows(buf, shapes):
    flat = buf.reshape(-1)
    outs, off = [], 0
    for s in shapes:
        n = int(np.prod(s))
        outs.append(flat[off:off + n].reshape(s))
        off += -(-n // LANE) * LANE
    return outs


def _local_grads(x, p, target, wseg, w_br16, w_out16, w_ple16, b_gate, conv_w, conv_b, dt_bias, a_log, d_skip,
                 ssm_norm_w, ln_g, ln_b, rel_bias, finish_dx):
    nb, seq, _ = x.shape
    bmaps = jnp.asarray(_bucket_maps())
    bias = _bias_tables(rel_bias, bmaps)
    bgate8 = jnp.pad(b_gate, ((0, 5), (0, 0)))
    dils = [d for _, d in PATTERNS]

    x16p = _token_orders(x, dils[1:])
    x16 = x16p[0]
    p16 = p.astype(BF16)
    qkv = [_proj(x16p[g], [wseg["qkv%d" % g]], BF16, "proj_qkv%d" % g, True)[0].reshape(
        nb, dils[g], seq // dils[g], -1) for g in range(3)]
    nat = {}
    for gi, (group, tm) in enumerate(NAT_GROUPS):
        outs = _proj(x16, [wseg[s] for s in group], F32, "proj_nat%d" % gi, True, tm)
        nat.update(zip(group, outs))
    att = [_attn_fwd(qkv[g], bias[g * GROUP_HEADS:(g + 1) * GROUP_HEADS], dils[g], "attn_fwd%d" % g) for g in range(3)]
    oa, o_att, lse = _combine_fwd(att[0][0], att[0][1], att[1:], nat["gatt"])

    conv_wg, conv_bg = _xbc_group_order(conv_w), _xbc_group_order(conv_b)
    act = _conv_fwd(nat["xbc"], conv_wg, conv_bg, "conv_fwd")
    dt_sp, dt_sg = _softplus_sig(nat["dt"], jnp.pad(dt_bias, ((0, 0), (0, LANE - SSM_HEADS))))
    dtg, sgg = _group_lanes(dt_sp), _group_lanes(dt_sg)
    alog_g, dskip_g = _group_lanes(a_log), _group_lanes(d_skip)
    y_ssm, y_all, sprev = _ssd_fwd(act, dtg, nat["z"], alog_g, dskip_g, ssm_norm_w)

    w_bra, w_brb = w_br16[:ATT_OUT], w_br16[ATT_OUT:]
    y_a, = _proj(oa, [w_bra], F32, "proj_ya")
    y_b, = _proj(y_ssm, [w_brb], F32, "proj_yb")
    merged = _merge_fwd(y_a, y_b, nat["gm"], bgate8)
    mix, = _proj(merged, [w_out16], F32, "proj_mix")
    pw, = _proj(p16, [w_ple16], F32, "proj_ple")

    dx, dpre16, dpw16, dgp16, ln_sums = _ln_loss(x, mix, nat["gp"], pw, target, bgate8, ln_g, ln_b)
    loss_sum = (0.5 / D_MODEL) * jnp.sum(ln_sums[3])
    dmerged = _dx([dpre16], [w_out16], [], "dx_merged")
    dya16, dyb16, dgm16, mg_sums = _merge_bwd(dmerged, y_a, y_b, nat["gm"], bgate8)
    doa = _dx([dya16], [w_bra], [], "dx_oa")
    dys = _dx([dyb16], [w_brb], [], "dx_yssm")
    g_w_out, = _dw(merged, [dpre16], BF16, "dw_out")
    g_w_br = jnp.concatenate([_dw(oa, [dya16], BF16, "dw_bra")[0], _dw(y_ssm, [dyb16], BF16, "dw_brb")[0]], axis=0)
    g_w_ple, = _dw(p16, [dpw16], BF16, "dw_ple")

    do_att, dgatt16, own_order = _combine_bwd(doa, nat["gatt"], o_att, lse, dils[1:])
    dseg = {"gatt": dgatt16, "gm": dgm16, "gp": dgp16}
    dbias = []
    for g in range(3):
        cotangent = (do_att, o_att, lse) if g == 0 else (own_order[2 * g - 2], own_order[2 * g - 1])
        dqkv, db = _attn_bwd(qkv[g], bias[g * GROUP_HEADS:(g + 1) * GROUP_HEADS], cotangent, dils[g],
                             "attn_bwd%d" % g)
        dseg["qkv%d" % g] = dqkv.reshape(nb, seq, -1)
        dbias.append(db)
    g_rel = _bias_grad(jnp.concatenate(dbias, axis=0), bmaps)[:, 0, :NUM_BUCKETS].T

    dact, ddtg, dz, ssd_small, g_normw = _ssd_bwd(
        act, dtg, sgg, nat["z"], y_all, dys, sprev, alog_g, dskip_g, ssm_norm_w)
    dseg["z"] = dz
    dseg["dt"] = jnp.pad(_ungroup_lanes(ddtg), ((0, 0), (0, 0), (0, LANE - SSM_HEADS)))
    dseg["xbc"], conv_sums = _conv_bwd(dact, nat["xbc"], conv_wg, conv_bg, "conv_bwd")
    csum = _xbc_reference_order(conv_sums)

    dx_own = [_dx([dseg["qkv%d" % g]], [wseg["qkv%d" % g]], [], "dx_qkv%d" % g, True).reshape(
        nb, dils[g], seq // dils[g], D_MODEL) for g in (1, 2)]
    dwseg = {"qkv%d" % g: _dw(x16p[g], [dseg["qkv%d" % g]], BF16, "dw_qkv%d" % g, True)[0] for g in range(3)}
    for gi, group in enumerate(DW_GROUPS):
        dwseg.update(zip(group, _dw(x16, [dseg[s] for s in group], BF16, "dw_nat%d" % gi, True)))
    names = ["qkv0"] + [s for group, _ in NAT_GROUPS for s in group]
    dx = finish_dx([dseg[s] for s in names], [wseg[s] for s in names], [dx], dx_own, dwseg, g_w_br, g_w_out, g_w_ple)

    small = dict(
        b_gate=jnp.stack([mg_sums[0], mg_sums[1], ln_sums[2]]),
        conv_w=csum[0:4], conv_b=csum[4:5],
        dt_bias=_ungroup_lanes(ssd_small[:, 2:3, :]), a_log=_ungroup_lanes(ssd_small[:, 0:1, :]),
        d_skip=_ungroup_lanes(ssd_small[:, 1:2, :]), ssm_norm_w=g_normw,
        ln_g=ln_sums[0:1], ln_b=ln_sums[1:2], rel_bias=g_rel)
    return loss_sum, dx, small


DX_TM = 256
SMALL_ORDER = ("b_gate", "conv_w", "conv_b", "dt_bias", "a_log", "d_skip", "ssm_norm_w", "ln_g", "ln_b", "rel_bias")
SMALL_FULL_SHAPES = dict(b_gate=(3, 1024), conv_w=(4, 3072), conv_b=(1, 3072), dt_bias=(1, 32), a_log=(1, 32),
                         d_skip=(1, 32), ssm_norm_w=(1, 2048), ln_g=(1, 1024), ln_b=(1, 1024), rel_bias=(32, 36))


def kernel(x, p, w_in, b_gate, conv_w, conv_b, dt_bias, a_log, d_skip, ssm_norm_w, w_branch, w_out, w_ple, ln_g, ln_b, rel_bias, loss_target, m_w_in, m_b_gate, m_conv_w, m_conv_b, m_dt_bias, m_a_log, m_d_skip, m_ssm_norm_w, m_w_branch, m_w_out, m_w_ple, m_ln_g, m_ln_b, m_rel_bias, v_w_in, v_b_gate, v_conv_w, v_conv_b, v_dt_bias, v_a_log, v_d_skip, v_ssm_norm_w, v_w_branch, v_w_out, v_w_ple, v_ln_g, v_ln_b, v_rel_bias):
    cx, cy, cc = _place()
    chip = 2 * cx + cy
    dev = 4 * cx + 2 * cy + cc

    w_in_t = jnp.transpose(w_in[0])
    win16 = _shard_to_window(w_in_t, chip)
    g_win, g_br, g_out, g_ple = _allgather_pieces(
        [win16, w_branch[0].astype(BF16), w_out[0].astype(BF16), w_ple[0].astype(BF16)])
    wseg = _assemble(g_win)
    w_br16 = g_br.reshape(4 * 704, D_MODEL)
    w_out16 = g_out.reshape(D_MODEL, D_MODEL)
    w_ple16 = jnp.transpose(g_ple, (1, 0, 2)).reshape(PLE_DIM, D_MODEL)
    shards = _allgather8(_pack_rows([b_gate[0], conv_w[0]]), "allgather_small_params")
    per_chip = [_unpack_rows(shards[2 * k], [(3, 256), (4, 768)]) for k in range(4)]
    b_gate_full = jnp.concatenate([pc[0] for pc in per_chip], axis=1)
    conv_w_full = jnp.concatenate([pc[1] for pc in per_chip], axis=1)

    place = jnp.stack([chip, cc]).astype(jnp.int32)
    reduced = []

    def finish_dx(dhs, ws, accs, own_order_accs, dwseg, d_br, d_out, d_ple):
        grads = [_pack(dwseg), d_br.reshape(4, 704, D_MODEL), d_out.reshape(4, 256, D_MODEL),
                 jnp.transpose(d_ple.reshape(PLE_DIM, 4, 256), (1, 0, 2))]
        got = _sibling_exchange(grads)
        chip_sums = [_pair_sum(g, t, place, "grad_pair_sum_%d" % i) for i, (g, t) in enumerate(zip(grads, got))]
        dx, others = _dx(dhs, ws, accs, "dx_w_in_and_grad_chip_scatter", True, DX_TM, chip_sums, own_order_accs)
        fulls = [_chip_sum(s, t, place, "grad_chip_sum_%d" % i) for i, (s, t) in enumerate(zip(chip_sums, others))]
        reduced.extend(_sibling_gather(fulls))
        return dx

    loss_sum, grad_x, small = _local_grads(
        x, p[0], loss_target, wseg, w_br16, w_out16, w_ple16, b_gate_full, conv_w_full, conv_b, dt_bias, a_log,
        d_skip, ssm_norm_w, ln_g, ln_b, rel_bias, finish_dx)
    big = reduced
    g_w_in = _window_to_shard(big[0], chip)
    g_w_branch, g_w_out, g_w_ple = big[1], big[2], big[3]
    parts = _allgather8(_pack_rows([small[n] for n in SMALL_ORDER] + [loss_sum.reshape(1, 1)]),
                        "allgather_small_grads")
    small_sum = _sum_rows([parts[i] for i in range(8)], F32, "small_grad_sum")
    *reduced_small, loss = _unpack_rows(small_sum, [SMALL_FULL_SHAPES[n] for n in SMALL_ORDER] + [(1, 1)])
    loss = loss.reshape(())
    sg = dict(zip(SMALL_ORDER, reduced_small))
    sg["b_gate"] = lax.dynamic_slice_in_dim(sg["b_gate"], chip * 256, 256, axis=1)
    sg["conv_w"] = lax.dynamic_slice_in_dim(sg["conv_w"], chip * 768, 768, axis=1)
    del dev

    upd = {}
    upd["w_in"] = [jnp.transpose(t) for t in _adamw(w_in_t, g_w_in, jnp.transpose(m_w_in[0]),
                                                      jnp.transpose(v_w_in[0]), "adamw_w_in")]
    upd["w_branch"] = _adamw(w_branch[0], g_w_branch, m_w_branch[0], v_w_branch[0], "adamw_w_branch")
    upd["w_out"] = _adamw(w_out[0], g_w_out, m_w_out[0], v_w_out[0], "adamw_w_out")
    upd["w_ple"] = _adamw(w_ple[0], g_w_ple, m_w_ple[0], v_w_ple[0], "adamw_w_ple")
    small_w = dict(b_gate=b_gate, conv_w=conv_w, conv_b=conv_b, dt_bias=dt_bias, a_log=a_log, d_skip=d_skip,
                   ssm_norm_w=ssm_norm_w, ln_g=ln_g, ln_b=ln_b, rel_bias=rel_bias)
    small_m = dict(b_gate=m_b_gate, conv_w=m_conv_w, conv_b=m_conv_b, dt_bias=m_dt_bias, a_log=m_a_log,
                   d_skip=m_d_skip, ssm_norm_w=m_ssm_norm_w, ln_g=m_ln_g, ln_b=m_ln_b, rel_bias=m_rel_bias)
    small_v = dict(b_gate=v_b_gate, conv_w=v_conv_w, conv_b=v_conv_b, dt_bias=v_dt_bias, a_log=v_a_log,
                   d_skip=v_d_skip, ssm_norm_w=v_ssm_norm_w, ln_g=v_ln_g, ln_b=v_ln_b, rel_bias=v_rel_bias)
    shapes = [small_w[n].shape for n in SMALL_ORDER]
    s_delta, s_m, s_v = _adamw(_pack_rows([small_w[n] for n in SMALL_ORDER]), _pack_rows([sg[n] for n in SMALL_ORDER]),
                               _pack_rows([small_m[n] for n in SMALL_ORDER]), _pack_rows([small_v[n] for n in SMALL_ORDER]),
                               "adamw_small")
    for i, n in enumerate(SMALL_ORDER):
        upd[n] = tuple(_unpack_rows(t, shapes)[i] for t in (s_delta, s_m, s_v))
        sg[n] = sg[n].reshape(small_w[n].shape)

    order = ("w_in", "b_gate", "conv_w", "conv_b", "dt_bias", "a_log", "d_skip", "ssm_norm_w", "w_branch", "w_out",
             "w_ple", "ln_g", "ln_b", "rel_bias")
    grads = dict(sg, w_in=jnp.transpose(g_w_in)[None],w_branch=g_w_branch[None], w_out=g_w_out[None], w_ple=g_w_ple[None])
    lead = lambda n, t: t[None] if n in ("w_in", "w_branch", "w_out", "w_ple") else t
    return (loss, grad_x, *[grads[n] for n in order], *[lead(n, upd[n][0]) for n in order],
            *[lead(n, upd[n][1]) for n in order], *[lead(n, upd[n][2]) for n in order])
```

```python
import functools
import math

import numpy as np
import jax
import jax.numpy as jnp
from jax import lax
from jax.experimental import pallas as pl
from jax.experimental.pallas import tpu as pltpu

F32, BF16 = jnp.float32, jnp.bfloat16

D_MODEL = 1024
HEAD_DIM = 64
GROUP_HEADS = 12
ATT_OUT = GROUP_HEADS * HEAD_DIM
PATTERNS = ((128, 1), (512, 4), (2048, 16))
BAND = 128
NUM_BUCKETS = 32
MAX_DISTANCE = 2048
D_INNER = 2048
SSM_HEADS = 32
SSM_GROUPS = 4
GROUP_SSM_HEADS = SSM_HEADS // SSM_GROUPS
D_STATE = 128
CHUNK = 128
PLE_DIM = 256
ALPHA = 2.0 ** 0.25
LN_EPS = 1e-5
RMS_EPS = 1e-5
ADAM_LR, ADAM_B1, ADAM_B2, ADAM_EPS, ADAM_WD, ADAM_STEP = 0.001, 0.9, 0.999, 1e-08, 0.01, 10
NEG = -1e30

QKV_W = 3 * ATT_OUT
IN_COLS = 15904
SHARD_COLS = IN_COLS // 4
DT_COL = 12800
ROW_TILE = 16
WIN_ROWS = 4000


def _win_offset(k):
    return (k * SHARD_COLS) % ROW_TILE


def _win_start(k):
    return k * SHARD_COLS - _win_offset(k)

VMEM_LIMIT_BYTES = 56 * 1024 * 1024
LANE = 128
MESH = pl.DeviceIdType.MESH
NT = (((1,), (1,)), ((), ()))
TN = (((0,), (0,)), ((), ()))


def _pcall(body, **kw):
    return pl.pallas_call(body, **kw)


def _params(*sem):
    return pltpu.CompilerParams(dimension_semantics=sem, vmem_limit_bytes=VMEM_LIMIT_BYTES)


def _sigmoid(v):
    return jax.nn.sigmoid(v)


MM_TM = 512


def _tok_spec(tm, width):
    return pl.BlockSpec((None, tm, width), lambda b, i: (b, i, 0))


def _whole(arr, single_buffer=False):
    mode = dict(pipeline_mode=pl.Buffered(1)) if single_buffer else {}
    return pl.BlockSpec(arr.shape, lambda b, i: (0,) * arr.ndim, **mode)


def _proj(a3, ws, out_dtype, name, w_rows_are_outputs=False, tm=MM_TM):
    nb, seq, kdim = a3.shape
    nw = len(ws)
    widths = [w.shape[0] if w_rows_are_outputs else w.shape[1] for w in ws]

    def body(*refs):
        a = refs[0][...].astype(BF16)
        for w_ref, o_ref in zip(refs[1:1 + nw], refs[1 + nw:]):
            if w_rows_are_outputs:
                v = lax.dot_general(a, w_ref[...], NT, preferred_element_type=F32)
            else:
                v = jnp.dot(a, w_ref[...], preferred_element_type=F32)
            o_ref[...] = v.astype(out_dtype)

    return _pcall(
        body, name=name, grid=(nb, seq // tm),
        in_specs=[_tok_spec(tm, kdim)] + [_whole(w) for w in ws],
        out_specs=[_tok_spec(tm, n) for n in widths],
        out_shape=[jax.ShapeDtypeStruct((nb, seq, n), out_dtype) for n in widths],
        compiler_params=_params("parallel", "parallel"),
    )(a3, *ws)


def _dx(dhs, ws, accs, name, w_rows_are_outputs=False, tm=MM_TM, scatter=None, own_order_accs=()):
    nb, seq, _ = dhs[0].shape
    nd, nacc, npa = len(dhs), len(accs), len(own_order_accs)
    kout = ws[0].shape[1] if w_rows_are_outputs else ws[0].shape[0]
    sums = scatter or []
    ns = len(sums)
    chunks = [_row_chunks(s.shape[1], s.shape[2] * s.dtype.itemsize) for s in sums]
    n_sem = 3 * sum(len(ch) for ch in chunks)
    grid = (nb, seq // tm)
    ntile = kout // LANE if npa else 0

    def body(*refs):
        n_in = 2 * nd + nacc + npa
        sum_refs, o_ref, got_refs = refs[n_in:n_in + ns], refs[n_in + ns], refs[n_in + ns + 1:n_in + 2 * ns + 1]
        tile_refs = refs[n_in + 2 * ns + 1:n_in + 2 * ns + 1 + ntile]

        def copies():
            send_sems, recv_sems = refs[-2], refs[-1]
            x, y, c = _place()
            out = []
            for a in range(ns):
                for s, m in chunks[a]:
                    for j, (cx, cy) in enumerate(_other_chips(x, y)):
                        k = len(out)
                        out.append(_remote(sum_refs[a].at[2 * cx + cy, pl.ds(s, m)], got_refs[a].at[j, pl.ds(s, m)],
                                           send_sems.at[k], recv_sems.at[k], (cx, cy, c)))
            return out

        if ns:
            @pl.when((pl.program_id(0) == 0) & (pl.program_id(1) == 0))
            def _():
                for cp in copies():
                    cp.start()

        v = None
        for dh_ref, w_ref in zip(refs[:nd], refs[nd:2 * nd]):
            dh = dh_ref[...].astype(BF16)
            if w_rows_are_outputs:
                t = jnp.dot(dh, w_ref[...], preferred_element_type=F32)
            else:
                t = lax.dot_general(dh, w_ref[...], NT, preferred_element_type=F32)
            v = t if v is None else v + t
        for a_ref in refs[2 * nd:2 * nd + nacc]:
            v = v + a_ref[...]
        for p_ref in refs[2 * nd + nacc:n_in]:
            v = v + _natural_rows(p_ref, tile_refs)
        o_ref[...] = v

        if ns:
            @pl.when((pl.program_id(0) == grid[0] - 1) & (pl.program_id(1) == grid[1] - 1))
            def _():
                for cp in copies():
                    cp.wait()

    out = _pcall(
        body, name=name, grid=grid,
        in_specs=[_tok_spec(tm, dh.shape[-1]) for dh in dhs] + [_whole(w, bool(ns)) for w in ws]
        + [_tok_spec(tm, kout)] * nacc
        + [pl.BlockSpec((None, p.shape[1], tm // p.shape[1], kout), lambda b, i: (b, 0, i, 0)) for p in own_order_accs]
        + [ANY] * ns,
        out_specs=[_tok_spec(tm, kout)] + [ANY] * ns,
        out_shape=[jax.ShapeDtypeStruct((nb, seq, kout), F32)]
        + [jax.ShapeDtypeStruct((3,) + s.shape[1:], s.dtype) for s in sums],
        input_output_aliases={2 * nd: 0} if nacc else {},
        scratch_shapes=[pltpu.VMEM((tm, LANE), F32)] * ntile
        + ([pltpu.SemaphoreType.DMA((n_sem,)), pltpu.SemaphoreType.DMA((n_sem,))] if ns else []),
        compiler_params=pltpu.CompilerParams(
            dimension_semantics=("arbitrary", "arbitrary") if ns else ("parallel", "parallel"),
            vmem_limit_bytes=VMEM_LIMIT_BYTES, has_side_effects=bool(ns)),
    )(*dhs, *ws, *accs, *own_order_accs, *sums)
    return (out[0], list(out[1:])) if ns else out[0]


def _dw(a3, dhs, out_dtype, name, rows_are_outputs=False):
    nb, seq, kdim = a3.shape
    nd = len(dhs)
    grid = (nb, seq // MM_TM)
    shapes = [(dh.shape[-1], kdim) if rows_are_outputs else (kdim, dh.shape[-1]) for dh in dhs]

    def body(*refs):
        b, i = pl.program_id(0), pl.program_id(1)
        dh_refs, o_refs, acc_refs = refs[1:1 + nd], refs[1 + nd:1 + 2 * nd], refs[1 + 2 * nd:]

        @pl.when((b == 0) & (i == 0))
        def _():
            for acc_ref in acc_refs:
                acc_ref[...] = jnp.zeros_like(acc_ref)

        a = refs[0][...].astype(BF16)
        for dh_ref, acc_ref in zip(dh_refs, acc_refs):
            dh = dh_ref[...].astype(BF16)
            acc_ref[...] += lax.dot_general(*((dh, a) if rows_are_outputs else (a, dh)), TN,
                                            preferred_element_type=F32)

        @pl.when((b == grid[0] - 1) & (i == grid[1] - 1))
        def _():
            for o_ref, acc_ref in zip(o_refs, acc_refs):
                o_ref[...] = acc_ref[...].astype(out_dtype)

    return _pcall(
        body, name=name, grid=grid,
        in_specs=[_tok_spec(MM_TM, kdim)] + [_tok_spec(MM_TM, dh.shape[-1]) for dh in dhs],
        out_specs=[pl.BlockSpec(s, lambda b, i: (0, 0)) for s in shapes],
        out_shape=[jax.ShapeDtypeStruct(s, out_dtype) for s in shapes],
        scratch_shapes=[pltpu.VMEM(s, F32) for s in shapes],
        compiler_params=_params("arbitrary", "arbitrary"),
    )(a3, *dhs)


def _qkv_rows(g):
    return [(part * QKV_W + g * ATT_OUT + hp * LANE, LANE) for hp in range(ATT_OUT // LANE) for part in range(3)]


XBC_START = 3 * QKV_W + ATT_OUT + D_INNER
GROUP_CH = GROUP_SSM_HEADS * HEAD_DIM
XBC_GROUP = GROUP_CH + 2 * D_STATE
CONV_DIM = SSM_GROUPS * XBC_GROUP


def _xbc_ranges():
    out = []
    for g in range(SSM_GROUPS):
        out += [(g * GROUP_CH, GROUP_CH), (D_INNER + g * D_STATE, D_STATE),
                (D_INNER + SSM_GROUPS * D_STATE + g * D_STATE, D_STATE)]
    return out


def _xbc_group_order(t):
    return jnp.concatenate([t[..., s:s + n] for s, n in _xbc_ranges()], axis=-1)


def _xbc_reference_order(t):
    g = lambda off, n: [t[..., k * XBC_GROUP + off:k * XBC_GROUP + off + n] for k in range(SSM_GROUPS)]
    return jnp.concatenate(g(0, GROUP_CH) + g(GROUP_CH, D_STATE) + g(GROUP_CH + D_STATE, D_STATE), axis=-1)


def _segments():
    one = lambda name, start, rows: (name, [(start, rows)], max(rows, LANE))
    return [("qkv%d" % g, _qkv_rows(g), QKV_W) for g in range(3)] + [
        one("gatt", 3 * QKV_W, ATT_OUT), one("z", 3 * QKV_W + ATT_OUT, D_INNER),
        ("xbc", [(XBC_START + s, n) for s, n in _xbc_ranges()], CONV_DIM), one("dt", DT_COL, SSM_HEADS),
        one("gm", DT_COL + SSM_HEADS, 2 * D_MODEL), one("gp", DT_COL + SSM_HEADS + 2 * D_MODEL, D_MODEL)]


LAYOUT_TC = 256
NAT_GROUPS = ((("gatt", "z", "dt", "gp"), 512), (("xbc", "gm"), 256))
DW_GROUPS = (("gatt", "z", "dt", "gp"), ("xbc",), ("gm",))


def _assemble(win):
    segs = _segments()

    def body(win_ref, *outs):
        def pieces(start, rows):
            t, end = start, start + rows
            while t < end:
                k = min(t // SHARD_COLS, 3)
                shard_end = (k + 1) * SHARD_COLS
                if k < 3 and shard_end % ROW_TILE and t == shard_end - shard_end % ROW_TILE:
                    lo = t - _win_start(k)
                    yield win_ref[k, lo:lo + ROW_TILE, :] + win_ref[k + 1, 0:ROW_TILE, :]
                    t += ROW_TILE
                    continue
                upto = min(end, shard_end - shard_end % ROW_TILE if k < 3 else end)
                yield win_ref[k, t - _win_start(k):upto - _win_start(k), :]
                t = upto

        for (_, ranges, total), o_ref in zip(segs, outs):
            off = 0
            for start, rows in ranges:
                for part in pieces(start, rows):
                    o_ref[off:off + part.shape[0], :] = part
                    off += part.shape[0]
            if off < total:
                o_ref[off:total, :] = jnp.zeros((total - off, o_ref.shape[1]), BF16)

    outs = _pcall(
        body, name="assemble_w_in", grid=(D_MODEL // LAYOUT_TC,),
        in_specs=[pl.BlockSpec((4, WIN_ROWS, LAYOUT_TC), lambda i: (0, 0, i))],
        out_specs=[pl.BlockSpec((total, LAYOUT_TC), lambda i: (0, i)) for _, _, total in segs],
        out_shape=[jax.ShapeDtypeStruct((total, D_MODEL), BF16) for _, _, total in segs],
        compiler_params=_params("parallel"),
    )(win)
    return {name: o for (name, _, _), o in zip(segs, outs)}


def _pack(dsegs):
    segs = _segments()

    def body(*refs):
        ins, o_ref = refs[:-1], refs[-1]
        tail = IN_COLS - _win_start(3)
        o_ref[3, tail:, :] = jnp.zeros((WIN_ROWS - tail, o_ref.shape[2]), BF16)
        for (_, ranges, _), s_ref in zip(segs, ins):
            off = 0
            for start, rows in ranges:
                for k in range(4):
                    lo = _win_start(k)
                    a, b = max(start, lo), min(start + rows, lo + WIN_ROWS)
                    if a < b:
                        o_ref[k, a - lo:b - lo, :] = s_ref[off + a - start:off + b - start, :]
                off += rows

    return _pcall(
        body, name="pack_dw_in", grid=(D_MODEL // LAYOUT_TC,),
        in_specs=[pl.BlockSpec((total, LAYOUT_TC), lambda i: (0, i)) for _, _, total in segs],
        out_specs=pl.BlockSpec((4, WIN_ROWS, LAYOUT_TC), lambda i: (0, 0, i)),
        out_shape=jax.ShapeDtypeStruct((4, WIN_ROWS, D_MODEL), BF16),
        compiler_params=_params("parallel"),
    )(*[dsegs[name] for name, _, _ in segs])


def _shard_to_window(shard_t, k):
    def at(off):
        return lambda w: jnp.pad(w.astype(BF16), ((off, WIN_ROWS - SHARD_COLS - off), (0, 0)))

    return lax.cond(k % 2 == 1, at(_win_offset(1)), at(_win_offset(0)), shard_t)


def _window_to_shard(win, k):
    return lax.dynamic_slice(win, ((k % 2) * _win_offset(1), 0), (SHARD_COLS, D_MODEL))


def _bucket_maps():
    qi = np.arange(8)[:, None]
    kj = np.arange(2 * BAND)[None, :]
    delta = qi + BAND - kj
    maps = []
    for window, dil in PATTERNS:
        valid = (delta >= 0) & (delta <= window // dil)
        dist = np.maximum(delta, 0) * dil
        max_exact = NUM_BUCKETS // 2
        d_f = np.maximum(dist, 1).astype(np.float32)
        large = max_exact + (np.log(d_f / np.float32(max_exact)) / np.float32(math.log(MAX_DISTANCE / max_exact))
                             * np.float32(NUM_BUCKETS - max_exact)).astype(np.int32)
        large = np.minimum(large, NUM_BUCKETS - 1)
        bucket = np.where(dist < max_exact, dist, large)
        maps.append(np.where(valid, bucket, -1).astype(np.int32))
    return np.stack(maps)


def _bias_tables(rel_bias, bmaps):
    def body(rb_ref, bm_ref, o_ref):
        g = pl.program_id(0)
        bm = bm_ref[...]
        for hh in range(GROUP_HEADS):
            acc = jnp.full(bm.shape, NEG, F32)
            for b in range(NUM_BUCKETS):
                acc = jnp.where(bm == b, rb_ref[b, g * GROUP_HEADS + hh], acc)
            for a in range(BAND // 8):
                o_ref[hh, 8 * a:8 * a + 8, :] = acc if a == 0 else pltpu.roll(acc, 8 * a, 1)

    return _pcall(
        body, name="bias_tables", grid=(3,),
        in_specs=[pl.BlockSpec(memory_space=pltpu.SMEM),
                  pl.BlockSpec((None, 8, 2 * BAND), lambda g: (g, 0, 0))],
        out_specs=pl.BlockSpec((GROUP_HEADS, BAND, 2 * BAND), lambda g: (g, 0, 0)),
        out_shape=jax.ShapeDtypeStruct((3 * GROUP_HEADS, BAND, 2 * BAND), F32),
        compiler_params=_params("parallel"),
    )(rel_bias, bmaps)


def _bias_grad(dbias, bmaps):
    def body(db_ref, bm_ref, o_ref):
        bm = bm_ref[...]
        lane = lax.broadcasted_iota(jnp.int32, (1, LANE), 1)
        for hh in range(GROUP_HEADS):
            db = db_ref[hh, 0:8, :]
            for a in range(1, BAND // 8):
                db = db + pltpu.roll(db_ref[hh, 8 * a:8 * a + 8, :], 2 * BAND - 8 * a, 1)
            vec = jnp.zeros((1, LANE), F32)
            for b in range(NUM_BUCKETS):
                s = jnp.sum(jnp.where(bm == b, db, 0.0), keepdims=True)
                vec = jnp.where(lane == b, s, vec)
            o_ref[hh] = vec

    return _pcall(
        body, name="bias_grad", grid=(3,),
        in_specs=[pl.BlockSpec((GROUP_HEADS, BAND, 2 * BAND), lambda g: (g, 0, 0)),
                  pl.BlockSpec((None, 8, 2 * BAND), lambda g: (g, 0, 0))],
        out_specs=pl.BlockSpec((GROUP_HEADS, 1, LANE), lambda g: (g, 0, 0)),
        out_shape=jax.ShapeDtypeStruct((3 * GROUP_HEADS, 1, LANE), F32),
        compiler_params=_params("parallel"),
    )(dbias, bmaps)


def _rows(n):
    if isinstance(n, int):
        return pl.ds(n * BAND, BAND)
    return pl.ds(pl.multiple_of(n * BAND, BAND), BAND)


def _for_blocks(blocks, nblk, per, carry):
    carry = blocks([0], carry, False)
    start = 1 + (nblk - 1) % per
    for n in range(1, start):
        carry = blocks([n], carry, True)
    trips = (nblk - start) // per
    if trips > 0:
        carry = lax.fori_loop(
            0, trips, lambda t, c: blocks([start + t * per + u for u in range(per)], c, True), carry)
    return carry


def _pairs_per_step(d):
    return {1: 3, 4: 6, 16: 6}[d]


def _attn_fwd(qkv4, bias, d, name):
    nb, _, sub, _ = qkv4.shape
    nblk = sub // BAND
    scale = HEAD_DIM ** -0.5
    npair = ATT_OUT // LANE
    hps = _pairs_per_step(d)
    compact = d > 1

    def body(qkv_ref, bias_ref, o_ref, l_ref):
        def blocks(ns, carry, with_prev):
            chains = [(bi, i, h) for bi in range(len(ns)) for i in range(hps) for h in range(2)]
            first_head = lax.broadcasted_iota(jnp.int32, (BAND, LANE), 1) < HEAD_DIM
            pair = lambda n, i, part: qkv_ref[_rows(n), (3 * i + part) * LANE:(3 * i + part + 1) * LANE]
            scores = []
            for bi, i, h in chains:
                n = ns[bi]
                qp = pair(n, i, 0) * scale
                q = jnp.where(first_head if h == 0 else jnp.logical_not(first_head), qp, jnp.zeros_like(qp))
                s_c = lax.dot_general(q, pair(n, i, 1), NT, preferred_element_type=F32) + bias_ref[2 * i + h, :, BAND:]
                s_p = None
                if with_prev:
                    s_p = lax.dot_general(q, pair(n - 1, i, 1), NT,
                                          preferred_element_type=F32) + bias_ref[2 * i + h, :, :BAND]
                scores.append((s_c, s_p))
            probs = []
            for s_c, s_p in scores:
                m = jnp.max(s_c, -1, keepdims=True)
                if with_prev:
                    m = jnp.maximum(m, jnp.max(s_p, -1, keepdims=True))
                e_c = jnp.exp(s_c - m)
                den = jnp.sum(e_c, -1, keepdims=True)
                e_p = None
                if with_prev:
                    e_p = jnp.exp(s_p - m)
                    den = den + jnp.sum(e_p, -1, keepdims=True)
                    e_p = e_p.astype(BF16)
                probs.append((e_c.astype(BF16), e_p, den, m))
            outs = {}
            for (bi, i, h), (e_c, e_p, den, m) in zip(chains, probs):
                n = ns[bi]
                acc = jnp.dot(e_c, pair(n, i, 2), preferred_element_type=F32)
                if with_prev:
                    acc = acc + jnp.dot(e_p, pair(n - 1, i, 2), preferred_element_type=F32)
                outs[(bi, i, h)] = (acc / den, m + jnp.log(den))
            lane = lax.broadcasted_iota(jnp.int32, (BAND, LANE), 1)
            for bi, n in enumerate(ns):
                per_head = jnp.zeros((BAND, LANE), F32)
                for i in range(hps):
                    o_ref[_rows(n), i * LANE:(i + 1) * LANE] = jnp.where(first_head, outs[(bi, i, 0)][0],
                                                                         outs[(bi, i, 1)][0])
                    if compact:
                        for h in range(2):
                            per_head = jnp.where(lane == 2 * i + h, outs[(bi, i, h)][1], per_head)
                    else:
                        l_ref[_rows(n), i * LANE:(i + 1) * LANE] = jnp.where(first_head, outs[(bi, i, 0)][1],
                                                                             outs[(bi, i, 1)][1])
                if compact:
                    l_ref[_rows(n), :] = per_head
            return carry

        _for_blocks(blocks, nblk, 2 if hps == 1 else 1, 0)

    in_specs = [pl.BlockSpec((None, None, sub, 3 * LANE * hps), lambda hp, b, r: (b, r, 0, hp)),
                pl.BlockSpec((2 * hps, BAND, 2 * BAND), lambda hp, b, r: (hp, 0, 0))]
    if compact:
        return _pcall(
            body, name=name, grid=(1, nb, d), in_specs=in_specs,
            out_specs=[pl.BlockSpec((None, None, sub, ATT_OUT), lambda hp, b, r: (b, r, 0, 0)),
                       pl.BlockSpec((None, None, sub, LANE), lambda hp, b, r: (b, r, 0, 0))],
            out_shape=[jax.ShapeDtypeStruct((nb, d, sub, ATT_OUT), F32), jax.ShapeDtypeStruct((nb, d, sub, LANE), F32)],
            compiler_params=_params("parallel", "parallel", "parallel"),
        )(qkv4, bias)
    ospec = pl.BlockSpec((None, sub, hps * LANE), lambda hp, b, r: (b, 0, r * (npair // hps) + hp))
    return _pcall(
        body, name=name, grid=(npair // hps, nb, d), in_specs=in_specs, out_specs=[ospec, ospec],
        out_shape=[jax.ShapeDtypeStruct((nb, sub, d * ATT_OUT), F32)] * 2,
        compiler_params=_params("parallel", "parallel", "parallel"),
    )(qkv4, bias)


STAT_LSE_LANE = 16


def _attn_bwd(qkv4, bias, cotangent, d, name):
    nb, _, sub, _ = qkv4.shape
    nblk = sub // BAND
    scale = HEAD_DIM ** -0.5
    npair = ATT_OUT // LANE
    hps = _pairs_per_step(d)
    compact = d > 1

    def body(qkv_ref, bias_ref, *rest):
        do_ref, dqkv_ref, db_ref = rest[0], rest[-2], rest[-1]
        b, r = pl.program_id(1), pl.program_id(2)

        @pl.when((b == 0) & (r == 0))
        def _():
            db_ref[...] = jnp.zeros_like(db_ref)

        def blocks(ns, carry, with_prev):
            sides = (0, 1) if with_prev else (0,)
            chains = [(bi, i, h, sd) for bi in range(len(ns)) for i in range(hps) for h in range(2) for sd in sides]
            first_head = lax.broadcasted_iota(jnp.int32, (BAND, LANE), 1) < HEAD_DIM
            own = lambda h, t: jnp.where(first_head if h == 0 else jnp.logical_not(first_head), t, jnp.zeros_like(t))
            pair = lambda rows, i, part: qkv_ref[rows, (3 * i + part) * LANE:(3 * i + part + 1) * LANE]
            key_rows = lambda bi, sd: _rows(ns[bi] - sd)
            qs = {}
            for bi in range(len(ns)):
                for i in range(hps):
                    q_pair = pair(_rows(ns[bi]), i, 0) * scale
                    do = do_ref[_rows(ns[bi]), i * LANE:(i + 1) * LANE]
                    do16 = do.astype(BF16)
                    for h in range(2):
                        if compact:
                            st_ref, head = rest[1], 2 * i + h
                            ebar = st_ref[_rows(ns[bi]), head:head + 1]
                            lcol = st_ref[_rows(ns[bi]), STAT_LSE_LANE + head:STAT_LSE_LANE + head + 1]
                        else:
                            ebar = jnp.sum(own(h, do * rest[1][_rows(ns[bi]), i * LANE:(i + 1) * LANE]), -1, keepdims=True)
                            lcol = rest[2][_rows(ns[bi]), i * LANE + h * HEAD_DIM:i * LANE + h * HEAD_DIM + 1]
                        qs[(bi, i, h)] = (own(h, q_pair), q_pair, own(h, do16), do16, ebar, lcol)
            raw = []
            for bi, i, h, sd in chains:
                q, _, do_h, _, _, _ = qs[(bi, i, h)]
                bias_blk = bias_ref[2 * i + h, :, :BAND] if sd else bias_ref[2 * i + h, :, BAND:]
                s = lax.dot_general(q, pair(key_rows(bi, sd), i, 1), NT, preferred_element_type=F32) + bias_blk
                dp = lax.dot_general(do_h, pair(key_rows(bi, sd), i, 2), NT, preferred_element_type=F32)
                raw.append((s, dp))
            soft = []
            for (bi, i, h, sd), (s, dp) in zip(chains, raw):
                ebar, lcol = qs[(bi, i, h)][4:]
                p = jnp.exp(s - lcol)
                ds = p * (dp - ebar)
                if sd:
                    db_ref[2 * i + h, :, :BAND] += ds
                else:
                    db_ref[2 * i + h, :, BAND:] += ds
                soft.append((p.astype(BF16), ds.astype(BF16)))
            grads = {}
            for (bi, i, h, sd), (p16, ds16) in zip(chains, soft):
                _, q_pair, _, do16 = qs[(bi, i, h)][:4]
                grads[(bi, i, h, sd)] = (
                    jnp.dot(ds16, pair(key_rows(bi, sd), i, 1), preferred_element_type=F32),
                    lax.dot_general(ds16, q_pair, TN, preferred_element_type=F32),
                    lax.dot_general(p16, do16, TN, preferred_element_type=F32))
            both = lambda bi, i, sd, which: jnp.where(first_head, grads[(bi, i, 0, sd)][which],
                                                      grads[(bi, i, 1, sd)][which])
            carry = list(carry) if carry is not None else None
            for bi, n in enumerate(ns):
                for i in range(hps):
                    base = 3 * LANE * i
                    dq = both(bi, i, 0, 0)
                    if with_prev:
                        dq = dq + both(bi, i, 1, 0)
                        dqkv_ref[_rows(n - 1), base + LANE:base + 2 * LANE] = (
                            carry[2 * i] + both(bi, i, 1, 1)).astype(BF16)
                        dqkv_ref[_rows(n - 1), base + 2 * LANE:base + 3 * LANE] = (
                            carry[2 * i + 1] + both(bi, i, 1, 2)).astype(BF16)
                    dqkv_ref[_rows(n), base:base + LANE] = (dq * scale).astype(BF16)
                carry = [t for i in range(hps) for t in (both(bi, i, 0, 1), both(bi, i, 0, 2))]
            return tuple(carry)

        carry = _for_blocks(blocks, nblk, 2 if hps == 1 else 1, None)
        for i in range(hps):
            base = 3 * LANE * i
            dqkv_ref[_rows(nblk - 1), base + LANE:base + 2 * LANE] = carry[2 * i].astype(BF16)
            dqkv_ref[_rows(nblk - 1), base + 2 * LANE:base + 3 * LANE] = carry[2 * i + 1].astype(BF16)

    qspec = pl.BlockSpec((None, None, sub, 3 * LANE * hps), lambda hp, b, r: (b, r, 0, hp))
    bspec = pl.BlockSpec((2 * hps, BAND, 2 * BAND), lambda hp, b, r: (hp, 0, 0))
    if compact:
        cspecs = [pl.BlockSpec((None, None, sub, ATT_OUT), lambda hp, b, r: (b, r, 0, 0)),
                  pl.BlockSpec((None, None, sub, LANE), lambda hp, b, r: (b, r, 0, 0))]
    else:
        cspecs = [pl.BlockSpec((None, sub, hps * LANE), lambda hp, b, r: (b, 0, r * (npair // hps) + hp))] * 3
    return _pcall(
        body, name=name, grid=(npair // hps, nb, d),
        in_specs=[qspec, bspec] + cspecs, out_specs=[qspec, bspec],
        out_shape=[jax.ShapeDtypeStruct(qkv4.shape, BF16),
                   jax.ShapeDtypeStruct((GROUP_HEADS, BAND, 2 * BAND), F32)],
        compiler_params=_params("parallel", "arbitrary", "arbitrary"),
    )(qkv4, bias, *cotangent)


def _head_lanes(first_lane, one_channel):
    c = lax.broadcasted_iota(jnp.int32, (ATT_OUT, LANE), 0)
    lane = lax.broadcasted_iota(jnp.int32, (ATT_OUT, LANE), 1)
    hit = lane == first_lane + c // HEAD_DIM
    if one_channel:
        hit = hit & (c % HEAD_DIM == 0)
    return hit.astype(BF16)


def _exact_dot(v, m01, dims=None):
    parts = _split3(v)
    if dims is None:
        dot = lambda t: jnp.dot(t, m01, preferred_element_type=F32)
    else:
        dot = lambda t: lax.dot_general(t, m01, dims, preferred_element_type=F32)
    return (dot(parts[0]) + dot(parts[1])) + dot(parts[2])


def _store_own_order(value, tile_refs, out_ref):
    d, per, width = out_ref.shape
    for j in range(width // LANE):
        tile_refs[j][...] = value[:, j * LANE:(j + 1) * LANE]
    for r in range(d):
        rows = pl.ds(r, per, stride=d)
        for j in range(width // LANE):
            out_ref[r, :, j * LANE:(j + 1) * LANE] = tile_refs[j][rows, :].astype(out_ref.dtype)


def _token_orders(x, dilations):
    nb, seq, kdim = x.shape
    tm = 512

    def body(x_ref, nat_ref, *rest):
        outs, tile_refs = rest[:len(dilations)], rest[len(dilations):]
        xv = x_ref[...]
        nat_ref[...] = xv.astype(BF16)
        for o_ref in outs:
            _store_own_order(xv, tile_refs, o_ref)

    outs = _pcall(
        body, name="token_orders", grid=(nb, seq // tm), in_specs=[_tok_spec(tm, kdim)],
        out_specs=[_tok_spec(tm, kdim)]
        + [pl.BlockSpec((None, d, tm // d, kdim), lambda b, i: (b, 0, i, 0)) for d in dilations],
        out_shape=[jax.ShapeDtypeStruct((nb, seq, kdim), BF16)]
        + [jax.ShapeDtypeStruct((nb, d, seq // d, kdim), BF16) for d in dilations],
        scratch_shapes=[pltpu.VMEM((tm, LANE), F32)] * (kdim // LANE),
        compiler_params=_params("parallel", "parallel"),
    )(x)
    return [outs[0]] + [o.reshape(nb, seq, kdim) for o in outs[1:]]


def _natural_rows(p_ref, tile_refs):
    d, per, width = p_ref.shape
    for r in range(d):
        rows = pl.ds(r, per, stride=d)
        for j in range(width // LANE):
            tile_refs[j][rows, :] = p_ref[r, :, j * LANE:(j + 1) * LANE]
    return jnp.concatenate([tile_refs[j][...] for j in range(width // LANE)], axis=1)


def _combine_fwd(o0, l0, dilated, gatt):
    nb, seq, _ = gatt.shape
    tm = 512
    ntile = ATT_OUT // LANE

    def body(o0_ref, l0_ref, o1_ref, l1_ref, o2_ref, l2_ref, g_ref, oa_ref, oatt_ref, lse_ref, *tile_refs):
        spread = _head_lanes(0, False)
        l0v = l0_ref[...]
        l1v = _exact_dot(_natural_rows(l1_ref, tile_refs), spread, NT)
        l2v = _exact_dot(_natural_rows(l2_ref, tile_refs), spread, NT)
        m = jnp.maximum(jnp.maximum(l0v, l1v), l2v)
        tot = m + jnp.log(jnp.exp(l0v - m) + jnp.exp(l1v - m) + jnp.exp(l2v - m))
        o = jnp.exp(l0v - tot) * o0_ref[...]
        o = o + jnp.exp(l1v - tot) * _natural_rows(o1_ref, tile_refs)
        o = o + jnp.exp(l2v - tot) * _natural_rows(o2_ref, tile_refs)
        g = g_ref[...]
        oa_ref[...] = (o * (g * _sigmoid(g))).astype(BF16)
        oatt_ref[...] = o
        lse_ref[...] = tot

    spec = pl.BlockSpec((None, tm, ATT_OUT), lambda b, i: (b, i, 0))
    own = lambda t: pl.BlockSpec((None, t.shape[1], tm // t.shape[1], t.shape[3]), lambda b, i: (b, 0, i, 0))
    (o1, l1), (o2, l2) = dilated
    return _pcall(
        body, name="attn_combine", grid=(nb, seq // tm),
        in_specs=[spec, spec, own(o1), own(l1), own(o2), own(l2), spec], out_specs=[spec] * 3,
        out_shape=[jax.ShapeDtypeStruct((nb, seq, ATT_OUT), BF16), jax.ShapeDtypeStruct((nb, seq, ATT_OUT), F32),
                   jax.ShapeDtypeStruct((nb, seq, ATT_OUT), F32)],
        scratch_shapes=[pltpu.VMEM((tm, LANE), F32)] * ntile,
        compiler_params=_params("parallel", "parallel"),
    )(o0, l0, o1, l1, o2, l2, gatt)


def _combine_bwd(doa, gatt, o_att, lse, dilations):
    nb, seq, _ = gatt.shape
    tm = 512

    def body(doa_ref, g_ref, o_ref, l_ref, do_ref, dg_ref, *rest):
        ntile = ATT_OUT // LANE
        outs, tile_refs = rest[:-ntile], rest[-ntile:]
        g = g_ref[...]
        sg = _sigmoid(g)
        do = doa_ref[...] * (g * sg)
        do_ref[...] = do
        stats = (_exact_dot(do * o_ref[...], _head_lanes(0, False))
                 + _exact_dot(l_ref[...], _head_lanes(STAT_LSE_LANE, True)))
        dg_ref[...] = (doa_ref[...] * o_ref[...] * (sg * (1.0 + g * (1.0 - sg)))).astype(BF16)
        for k in range(len(dilations)):
            _store_own_order(do, tile_refs, outs[2 * k])
            _store_own_order(stats, tile_refs, outs[2 * k + 1])

    spec = pl.BlockSpec((None, tm, ATT_OUT), lambda b, i: (b, i, 0))
    own = lambda d, width: pl.BlockSpec((None, d, tm // d, width), lambda b, i: (b, 0, i, 0))
    outs = _pcall(
        body, name="attn_combine_bwd", grid=(nb, seq // tm), in_specs=[spec] * 4,
        out_specs=[spec, spec] + [own(d, w) for d in dilations for w in (ATT_OUT, LANE)],
        out_shape=[jax.ShapeDtypeStruct((nb, seq, ATT_OUT), F32), jax.ShapeDtypeStruct((nb, seq, ATT_OUT), BF16)]
        + [jax.ShapeDtypeStruct((nb, d, seq // d, w), t) for d in dilations for w, t in ((ATT_OUT, BF16), (LANE, F32))],
        scratch_shapes=[pltpu.VMEM((tm, LANE), F32)] * (ATT_OUT // LANE),
        compiler_params=_params("parallel", "parallel"),
    )(doa, gatt, o_att, lse)
    return outs[0], outs[1], outs[2:]


CONV_TM = 1024
CONV_TC = 1024


def _shift_down(cur, halo, k):
    rolled = pltpu.roll(cur, k, 0)
    hro = pltpu.roll(halo, k, 0)
    row = lax.broadcasted_iota(jnp.int32, hro.shape, 0)
    return jnp.concatenate([jnp.where(row < k, hro, rolled[:8]), rolled[8:]], axis=0)


def _shift_up(cur, halo, k):
    n = cur.shape[0]
    rolled = pltpu.roll(cur, n - k, 0)
    hro = pltpu.roll(halo, 8 - k, 0)
    row = lax.broadcasted_iota(jnp.int32, hro.shape, 0)
    return jnp.concatenate([rolled[:n - 8], jnp.where(row >= 8 - k, hro, rolled[n - 8:])], axis=0)


def _conv_pre(cur, halo, w_ref, b_ref):
    acc = cur * w_ref[3:4, :] + b_ref[...]
    for k in range(1, 4):
        acc = acc + _shift_down(cur, halo, k) * w_ref[3 - k:4 - k, :]
    return acc


def _conv_specs(seq):
    nblk = seq // CONV_TM
    cur = pl.BlockSpec((None, CONV_TM, CONV_TC), lambda cb, b, i: (b, i, cb))
    prev = pl.BlockSpec((None, 8, CONV_TC), lambda cb, b, i: (b, jnp.maximum(i * (CONV_TM // 8) - 1, 0), cb))
    nxt = pl.BlockSpec((None, 8, CONV_TC),
                       lambda cb, b, i: (b, jnp.minimum((i + 1) * (CONV_TM // 8), seq // 8 - 1), cb))
    wspec = pl.BlockSpec((4, CONV_TC), lambda cb, b, i: (0, cb))
    bspec = pl.BlockSpec((1, CONV_TC), lambda cb, b, i: (0, cb))
    return nblk, cur, prev, nxt, wspec, bspec


def _conv_fwd(xin, w4, bias, name):
    nb, seq, ch = xin.shape
    _, cur, prev, _, wspec, bspec = _conv_specs(seq)

    def body(x_ref, h_ref, w_ref, b_ref, o_ref):
        halo = jnp.where(pl.program_id(2) > 0, h_ref[...], 0.0)
        pre = _conv_pre(x_ref[...], halo, w_ref, b_ref)
        o_ref[...] = pre * _sigmoid(pre)

    return _pcall(
        body, name=name, grid=(ch // CONV_TC, nb, seq // CONV_TM),
        in_specs=[cur, prev, wspec, bspec], out_specs=cur,
        out_shape=jax.ShapeDtypeStruct(xin.shape, F32),
        compiler_params=_params("parallel", "parallel", "parallel"),
    )(xin, xin, w4, bias)


def _conv_bwd_pre(dact, xin, w4, bias, name):
    nb, seq, ch = xin.shape
    _, cur, prev, _, wspec, bspec = _conv_specs(seq)

    def body(da_ref, x_ref, h_ref, w_ref, b_ref, dp_ref, s_ref):
        b, i = pl.program_id(1), pl.program_id(2)

        @pl.when((b == 0) & (i == 0))
        def _():
            s_ref[...] = jnp.zeros_like(s_ref)

        halo = jnp.where(i > 0, h_ref[...], 0.0)
        x = x_ref[...]
        pre = _conv_pre(x, halo, w_ref, b_ref)
        sg = _sigmoid(pre)
        dpre = da_ref[...] * (sg * (1.0 + pre * (1.0 - sg)))
        dp_ref[...] = dpre
        s_ref[3:4, :] += jnp.sum(dpre * x, 0, keepdims=True)
        for k in range(1, 4):
            s_ref[3 - k:4 - k, :] += jnp.sum(dpre * _shift_down(x, halo, k), 0, keepdims=True)
        s_ref[4:5, :] += jnp.sum(dpre, 0, keepdims=True)

    return _pcall(
        body, name=name, grid=(ch // CONV_TC, nb, seq // CONV_TM),
        in_specs=[cur, cur, prev, wspec, bspec],
        out_specs=[cur, pl.BlockSpec((8, CONV_TC), lambda cb, b, i: (0, cb))],
        out_shape=[jax.ShapeDtypeStruct(xin.shape, F32), jax.ShapeDtypeStruct((8, ch), F32)],
        compiler_params=_params("parallel", "arbitrary", "arbitrary"),
    )(dact, xin, xin, w4, bias)


def _conv_bwd_x(dpre, w4, name):
    nb, seq, ch = dpre.shape
    nblk, cur, _, nxt, wspec, _ = _conv_specs(seq)

    def body(d_ref, n_ref, w_ref, o_ref):
        halo = jnp.where(pl.program_id(2) < nblk - 1, n_ref[...], 0.0)
        cur_v = d_ref[...]
        acc = cur_v * w_ref[3:4, :]
        for j in range(1, 4):
            acc = acc + _shift_up(cur_v, halo, j) * w_ref[3 - j:4 - j, :]
        o_ref[...] = acc.astype(BF16)

    return _pcall(
        body, name=name, grid=(ch // CONV_TC, nb, seq // CONV_TM),
        in_specs=[cur, nxt, wspec], out_specs=cur,
        out_shape=jax.ShapeDtypeStruct(dpre.shape, BF16),
        compiler_params=_params("parallel", "parallel", "parallel"),
    )(dpre, dpre, w4)


def _softplus_sig(dt_raw, dt_bias_row):
    nb, seq, _ = dt_raw.shape
    tm = 512

    def body(r_ref, b_ref, sp_ref, sg_ref):
        v = r_ref[...] + b_ref[...]
        sp_ref[...] = jnp.maximum(v, 0.0) + jnp.log1p(jnp.exp(-jnp.abs(v)))
        sg_ref[...] = _sigmoid(v)

    spec = pl.BlockSpec((None, tm, LANE), lambda b, i: (b, i, 0))
    return _pcall(
        body, name="dt_softplus", grid=(nb, seq // tm),
        in_specs=[spec, pl.BlockSpec((1, LANE), lambda b, i: (0, 0))], out_specs=[spec, spec],
        out_shape=[jax.ShapeDtypeStruct(dt_raw.shape, F32)] * 2,
        compiler_params=_params("parallel", "parallel"),
    )(dt_raw, dt_bias_row)


def _group_lanes(t):
    pads = [(0, 0)] * (t.ndim - 1) + [(0, LANE - GROUP_SSM_HEADS)]
    return jnp.stack([jnp.pad(t[..., GROUP_SSM_HEADS * g:GROUP_SSM_HEADS * (g + 1)], pads) for g in range(SSM_GROUPS)])


def _ungroup_lanes(t):
    return jnp.concatenate([t[g][..., :GROUP_SSM_HEADS] for g in range(SSM_GROUPS)], axis=-1)


def _decays(dt, al_ref):
    row = lax.broadcasted_iota(jnp.int32, (CHUNK, CHUNK), 0)
    col = lax.broadcasted_iota(jnp.int32, (CHUNK, CHUNK), 1)
    tril = (row >= col).astype(BF16)
    triu = (row <= col).astype(BF16)
    arow = -jnp.exp(al_ref[...])
    hi, mid, lo = _split3(dt * arow)
    down = lambda t: jnp.dot(tril, t, preferred_element_type=F32)
    across = lambda t: lax.dot_general(t, triu, TN, preferred_element_type=F32)
    acs = (down(hi) + down(mid)) + down(lo)
    acs_t = (across(hi) + across(mid)) + across(lo)
    return arow, acs, acs_t, row >= col, triu


STEP_CHUNKS = 8


def _ssd_specs(nb, seq):
    nc = seq // CHUNK
    hw = GROUP_SSM_HEADS * HEAD_DIM
    rows, steps = STEP_CHUNKS * CHUNK, nc // STEP_CHUNKS

    def mk(rev):
        cidx = (lambda c: steps - 1 - c) if rev else (lambda c: c)
        wide = pl.BlockSpec((None, rows, hw), lambda g, b, c: (b, cidx(c), g))
        xbc = pl.BlockSpec((None, rows, XBC_GROUP), lambda g, b, c: (b, cidx(c), g))
        lanes = pl.BlockSpec((None, None, rows, LANE), lambda g, b, c: (g, b, cidx(c), 0))
        prev = pl.BlockSpec((None, STEP_CHUNKS, None, D_STATE, hw), lambda g, b, c: (b, cidx(c), g, 0, 0))
        return wide, xbc, lanes, prev

    grow = pl.BlockSpec((None, 1, LANE), lambda g, b, c: (g, 0, 0))
    nwspec = pl.BlockSpec((1, hw), lambda g, b, c: (0, g))
    return nc, steps, hw, mk, grow, nwspec


def _head_expand():
    hw = GROUP_SSM_HEADS * HEAD_DIM
    r = lax.broadcasted_iota(jnp.int32, (LANE, hw), 0)
    c = lax.broadcasted_iota(jnp.int32, (LANE, hw), 1)
    return ((c // HEAD_DIM) == r).astype(BF16)


def _split3(v):
    hi = v.astype(BF16)
    rest = v - hi.astype(F32)
    mid = rest.astype(BF16)
    return hi, mid, (rest - mid.astype(F32)).astype(BF16)


def _to_channels(v, e):
    hi, mid, lo = _split3(v)
    dot = lambda t: jnp.dot(t, e, preferred_element_type=F32)
    return (dot(hi) + dot(mid)) + dot(lo)


def _to_heads(w, e):
    hi, mid, lo = _split3(w)
    dot = lambda t: lax.dot_general(t, e, (((1,), (1,)), ((), ())), preferred_element_type=F32)
    return (dot(hi) + dot(mid)) + dot(lo)


def _row8(v):
    return jnp.broadcast_to(v, (8, v.shape[1]))


def _ssd_chunk_setup(dt, al_ref, ds_ref):
    arow, acs, acs_t, causal, triu = _decays(dt, al_ref)
    e = _head_expand()
    dtx = _to_channels(dt, e)
    acsx = _to_channels(acs, e)
    lastx = acsx[CHUNK - 1:CHUNK, :]
    dskx = _to_channels(_row8(ds_ref[...]), e)[0:1, :]
    return arow, acs, acs_t, causal, triu, e, dtx, acsx, lastx, dskx


def _ssd_fwd(xbc, dtg, z, alog_g, dskip_g, normw):
    nb, seq, _ = xbc.shape
    nc, steps, hw, mk, grow, nwspec = _ssd_specs(nb, seq)
    wide, xbc_spec, lanes, prev = mk(False)
    tn = (((0,), (0,)), ((), ()))

    def body(xbc_ref, dt_ref, z_ref, al_ref, ds_ref, nw_ref, ys_ref, y_ref, sp_ref, st_ref):
        @pl.when(pl.program_id(2) == 0)
        def _():
            st_ref[...] = jnp.zeros_like(st_ref)

        for ci in range(STEP_CHUNKS):
            chunk(ci, xbc_ref, dt_ref, z_ref, al_ref, ds_ref, nw_ref, ys_ref, y_ref, sp_ref, st_ref)

    def chunk(ci, xbc_ref, dt_ref, z_ref, al_ref, ds_ref, nw_ref, ys_ref, y_ref, sp_ref, st_ref):
        rows = slice(ci * CHUNK, (ci + 1) * CHUNK)
        dt = dt_ref[rows, :]
        _, acs, acs_t, causal, _, _, dtx, acsx, lastx, dskx = _ssd_chunk_setup(dt, al_ref, ds_ref)
        bmat = xbc_ref[rows, GROUP_CH:GROUP_CH + D_STATE].astype(BF16)
        cmat = xbc_ref[rows, GROUP_CH + D_STATE:].astype(BF16)
        cb = lax.dot_general(cmat, bmat, (((1,), (1,)), ((), ())), preferred_element_type=F32)
        x = xbc_ref[rows, :GROUP_CH]
        xdt = x * dtx
        xdt16 = xdt.astype(BF16)
        first_head = lax.broadcasted_iota(jnp.int32, (CHUNK, LANE), 1) < HEAD_DIM
        pairs = []
        for hp in range(GROUP_SSM_HEADS // 2):
            xp = xdt16[:, hp * LANE:(hp + 1) * LANE]
            two = []
            for j in (2 * hp, 2 * hp + 1):
                lmat = jnp.exp(jnp.where(causal, acs[:, j:j + 1] - acs_t[j:j + 1, :], -jnp.inf))
                two.append(jnp.dot((cb * lmat).astype(BF16), xp, preferred_element_type=F32))
            pairs.append(jnp.where(first_head, two[0], two[1]))
        yd = jnp.concatenate(pairs, axis=1)
        s_prev = st_ref[...]
        s16 = s_prev.astype(BF16)
        sp_ref[ci] = s16
        yo = jnp.dot(cmat, s16, preferred_element_type=F32) * jnp.exp(acsx)
        sts = lax.dot_general(bmat, (xdt * jnp.exp(lastx - acsx)).astype(BF16), tn, preferred_element_type=F32)
        st_ref[...] = s_prev * jnp.exp(lastx) + sts
        y = yd + yo + dskx * x
        zz = z_ref[rows, :]
        u = y * (zz * _sigmoid(zz))
        rn = lax.rsqrt(jnp.mean(u * u, -1, keepdims=True) + RMS_EPS)
        ys_ref[rows, :] = (u * rn * nw_ref[...]).astype(BF16)
        y_ref[rows, :] = y

    return _pcall(
        body, name="ssd_fwd", grid=(SSM_GROUPS, nb, steps),
        in_specs=[xbc_spec, lanes, wide, grow, grow, nwspec],
        out_specs=[wide, wide, prev],
        out_shape=[jax.ShapeDtypeStruct((nb, seq, D_INNER), BF16), jax.ShapeDtypeStruct((nb, seq, D_INNER), F32),
                   jax.ShapeDtypeStruct((nb, nc, SSM_GROUPS, D_STATE, hw), BF16)],
        scratch_shapes=[pltpu.VMEM((D_STATE, hw), F32)],
        compiler_params=_params("parallel", "parallel", "arbitrary"),
    )(xbc, dtg, z, alog_g, dskip_g, normw)


def _ssd_bwd(xbc, dtg, sgg, z, y, dys, sprev, alog_g, dskip_g, normw):
    nb, seq, _ = xbc.shape
    nc, steps, hw, mk, grow, nwspec = _ssd_specs(nb, seq)
    wide, xbc_spec, lanes, prev = mk(True)
    nt = (((1,), (1,)), ((), ()))
    tn = (((0,), (0,)), ((), ()))

    def body(xbc_ref, dt_ref, sg_ref, z_ref, y_ref, dys_ref, sp_ref, al_ref, ds_ref, nw_ref,
             dxbc_ref, ddt_ref, dz_ref, small_ref, dnw_ref, g_ref):
        b, c = pl.program_id(1), pl.program_id(2)

        @pl.when((b == 0) & (c == 0))
        def _():
            small_ref[...] = jnp.zeros_like(small_ref)
            dnw_ref[...] = jnp.zeros_like(dnw_ref)

        @pl.when(c == 0)
        def _():
            g_ref[...] = jnp.zeros_like(g_ref)

        for ci in reversed(range(STEP_CHUNKS)):
            chunk(ci, xbc_ref, dt_ref, sg_ref, z_ref, y_ref, dys_ref, sp_ref, al_ref, ds_ref, nw_ref,
                  dxbc_ref, ddt_ref, dz_ref, small_ref, dnw_ref, g_ref)

    def chunk(ci, xbc_ref, dt_ref, sg_ref, z_ref, y_ref, dys_ref, sp_ref, al_ref, ds_ref, nw_ref,
              dxbc_ref, ddt_ref, dz_ref, small_ref, dnw_ref, g_ref):
        rows = slice(ci * CHUNK, (ci + 1) * CHUNK)
        yv, zz, dys_v, nw = y_ref[rows, :], z_ref[rows, :], dys_ref[rows, :], nw_ref[...]
        sz = _sigmoid(zz)
        silu = zz * sz
        u = yv * silu
        rn = lax.rsqrt(jnp.mean(u * u, -1, keepdims=True) + RMS_EPS)
        gn = dys_v * nw
        du = rn * gn - u * (rn * rn * rn) * jnp.mean(u * gn, -1, keepdims=True)
        dnw_ref[...] += jnp.sum(dys_v * u * rn, 0, keepdims=True)
        dy = du * silu
        dz_ref[rows, :] = du * yv * (sz * (1.0 + zz * (1.0 - sz)))

        dt = dt_ref[rows, :]
        arow, acs, acs_t, causal, triu, e, dtx, acsx, lastx, dskx = _ssd_chunk_setup(dt, al_ref, ds_ref)
        dfsx = jnp.exp(acsx)
        dtex = jnp.exp(lastx - acsx)
        bmat = xbc_ref[rows, GROUP_CH:GROUP_CH + D_STATE].astype(BF16)
        cmat = xbc_ref[rows, GROUP_CH + D_STATE:].astype(BF16)
        cb = lax.dot_general(cmat, bmat, nt, preferred_element_type=F32)
        x = xbc_ref[rows, :GROUP_CH]
        xdt = x * dtx
        xdt16 = xdt.astype(BF16)
        xdte = xdt * dtex
        dy16 = dy.astype(BF16)
        dyd = dy * dfsx
        dyd16 = dyd.astype(BF16)
        s16 = sp_ref[ci]
        g = g_ref[...]
        g16 = g.astype(BF16)
        cs = jnp.dot(cmat, s16, preferred_element_type=F32)
        dc_off = lax.dot_general(dyd16, s16, nt, preferred_element_type=F32)
        g_here = lax.dot_general(cmat, dyd16, tn, preferred_element_type=F32)
        bg = jnp.dot(bmat, g16, preferred_element_type=F32)
        db_st = lax.dot_general(xdte.astype(BF16), g16, nt, preferred_element_type=F32)
        ddte_w = bg * xdte
        dcd = _to_heads(_row8(jnp.sum(g * s16.astype(F32), 0, keepdims=True)), e)[0:1, :]
        lane = lax.broadcasted_iota(jnp.int32, (CHUNK, LANE), 1)
        first_head = lane < HEAD_DIM
        sub = lax.broadcasted_iota(jnp.int32, (CHUNK, LANE), 0)
        dacs = jnp.zeros((CHUNK, LANE), F32)
        colsums = jnp.zeros((CHUNK, LANE), F32)
        dcb = jnp.zeros((CHUNK, CHUNK), F32)
        pairs = []
        for hp in range(GROUP_SSM_HEADS // 2):
            xp = xdt16[:, hp * LANE:(hp + 1) * LANE]
            dyp = dy16[:, hp * LANE:(hp + 1) * LANE]
            two = []
            for idx, j in enumerate((2 * hp, 2 * hp + 1)):
                lmat = jnp.exp(jnp.where(causal, acs[:, j:j + 1] - acs_t[j:j + 1, :], -jnp.inf))
                mf = cb * lmat
                dy_h = jnp.where(first_head if idx == 0 else jnp.logical_not(first_head), dyp, jnp.zeros_like(dyp))
                dm = lax.dot_general(dy_h, xp, nt, preferred_element_type=F32)
                two.append(lax.dot_general(mf.astype(BF16), dyp, tn, preferred_element_type=F32))
                wmat = dm * mf
                dcb = dcb + dm * lmat
                dacs = jnp.where(lane == j, jnp.sum(wmat, -1, keepdims=True), dacs)
                colsums = jnp.where(sub == j, jnp.sum(wmat, 0, keepdims=True), colsums)
            pairs.append(jnp.where(first_head, two[0], two[1]))
        dxdt = bg * dtex + jnp.concatenate(pairs, axis=1)
        dacs = dacs - colsums.T + _to_heads(dyd * cs - ddte_w, e)
        cd_row = jnp.exp(acs[CHUNK - 1:CHUNK, :])
        tail = _to_heads(_row8(jnp.sum(ddte_w, 0, keepdims=True)), e)[0:1, :] + dcd * cd_row
        dacs = dacs + jnp.where(sub == CHUNK - 1, tail, 0.0)
        d_hi, d_mid, d_lo = _split3(dacs)
        up = lambda t: jnp.dot(triu, t, preferred_element_type=F32)
        da = (up(d_hi) + up(d_mid)) + up(d_lo)
        ddt_raw = (da * arow + _to_heads(dxdt * x, e)) * sg_ref[rows, :]
        ddt_ref[rows, :] = ddt_raw
        small_ref[0:1, :] += jnp.sum(da * dt, 0, keepdims=True) * arow
        small_ref[1:2, :] += _to_heads(_row8(jnp.sum(dy * x, 0, keepdims=True)), e)[0:1, :]
        small_ref[2:3, :] += jnp.sum(ddt_raw, 0, keepdims=True)
        dcb16 = dcb.astype(BF16)
        dxbc_ref[rows, GROUP_CH + D_STATE:] = dc_off + jnp.dot(dcb16, bmat, preferred_element_type=F32)
        dxbc_ref[rows, GROUP_CH:GROUP_CH + D_STATE] = db_st + lax.dot_general(dcb16, cmat, tn,
                                                                               preferred_element_type=F32)
        dxbc_ref[rows, :GROUP_CH] = dxdt * dtx + dskx * dy
        g_ref[...] = g * jnp.exp(lastx) + g_here

    return _pcall(
        body, name="ssd_bwd", grid=(SSM_GROUPS, nb, steps),
        in_specs=[xbc_spec, lanes, lanes, wide, wide, wide, prev, grow, grow, nwspec],
        out_specs=[xbc_spec, lanes, wide,
                   pl.BlockSpec((None, 8, LANE), lambda g, b, c: (g, 0, 0)), nwspec],
        out_shape=[jax.ShapeDtypeStruct((nb, seq, CONV_DIM), F32),
                   jax.ShapeDtypeStruct((SSM_GROUPS, nb, seq, LANE), F32),
                   jax.ShapeDtypeStruct((nb, seq, D_INNER), F32),
                   jax.ShapeDtypeStruct((SSM_GROUPS, 8, LANE), F32),
                   jax.ShapeDtypeStruct((1, D_INNER), F32)],
        scratch_shapes=[pltpu.VMEM((D_STATE, hw), F32)],
        compiler_params=_params("parallel", "arbitrary", "arbitrary"),
    )(xbc, dtg, sgg, z, y, dys, sprev, alog_g, dskip_g, normw)


EW_TM = 256


def _merge_fwd(y_a, y_b, gm, bgate):
    nb, seq, _ = y_a.shape

    def body(a_ref, b_ref, ga_ref, gb_ref, bg_ref, o_ref):
        sa = _sigmoid(ga_ref[...] + bg_ref[0:1, :])
        sb = _sigmoid(gb_ref[...] + bg_ref[1:2, :])
        o_ref[...] = (sa * a_ref[...] + sb * b_ref[...]).astype(BF16)

    spec = pl.BlockSpec((None, EW_TM, D_MODEL), lambda b, i: (b, i, 0))
    spec1 = pl.BlockSpec((None, EW_TM, D_MODEL), lambda b, i: (b, i, 1))
    return _pcall(
        body, name="merge_fwd", grid=(nb, seq // EW_TM),
        in_specs=[spec, spec, spec, spec1, pl.BlockSpec((8, D_MODEL), lambda b, i: (0, 0))], out_specs=spec,
        out_shape=jax.ShapeDtypeStruct((nb, seq, D_MODEL), BF16),
        compiler_params=_params("parallel", "parallel"),
    )(y_a, y_b, gm, gm, bgate)


def _merge_bwd(dmerged, y_a, y_b, gm, bgate):
    nb, seq, _ = y_a.shape

    def body(dm_ref, a_ref, b_ref, ga_ref, gb_ref, bg_ref, dya_ref, dyb_ref, dg_ref, s_ref):
        @pl.when((pl.program_id(0) == 0) & (pl.program_id(1) == 0))
        def _():
            s_ref[...] = jnp.zeros_like(s_ref)

        dm = dm_ref[...]
        sa = _sigmoid(ga_ref[...] + bg_ref[0:1, :])
        sb = _sigmoid(gb_ref[...] + bg_ref[1:2, :])
        dya_ref[...] = (dm * sa).astype(BF16)
        dyb_ref[...] = (dm * sb).astype(BF16)
        dga = dm * a_ref[...] * (sa * (1.0 - sa))
        dgb = dm * b_ref[...] * (sb * (1.0 - sb))
        dg_ref[:, :D_MODEL] = dga.astype(BF16)
        dg_ref[:, D_MODEL:] = dgb.astype(BF16)
        s_ref[0:1, :] += jnp.sum(dga, 0, keepdims=True)
        s_ref[1:2, :] += jnp.sum(dgb, 0, keepdims=True)

    spec = pl.BlockSpec((None, EW_TM, D_MODEL), lambda b, i: (b, i, 0))
    spec1 = pl.BlockSpec((None, EW_TM, D_MODEL), lambda b, i: (b, i, 1))
    small = pl.BlockSpec((8, D_MODEL), lambda b, i: (0, 0))
    return _pcall(
        body, name="merge_bwd", grid=(nb, seq // EW_TM),
        in_specs=[spec, spec, spec, spec, spec1, small],
        out_specs=[spec, spec, pl.BlockSpec((None, EW_TM, 2 * D_MODEL), lambda b, i: (b, i, 0)), small],
        out_shape=[jax.ShapeDtypeStruct((nb, seq, D_MODEL), BF16), jax.ShapeDtypeStruct((nb, seq, D_MODEL), BF16),
                   jax.ShapeDtypeStruct((nb, seq, 2 * D_MODEL), BF16), jax.ShapeDtypeStruct((8, D_MODEL), F32)],
        compiler_params=_params("arbitrary", "arbitrary"),
    )(dmerged, y_a, y_b, gm, gm, bgate)


def _ln_loss(x, mix, gp, pw, target, bgate, ln_g, ln_b):
    nb, seq, _ = x.shape

    def body(x_ref, mix_ref, gp_ref, pw_ref, t_ref, bg_ref, g_ref, b_ref, dx_ref, dp_ref, dpw_ref, dgp_ref, s_ref):
        @pl.when((pl.program_id(0) == 0) & (pl.program_id(1) == 0))
        def _():
            s_ref[...] = jnp.zeros_like(s_ref)

        sp = _sigmoid(gp_ref[...] + bg_ref[2:3, :])
        pw = pw_ref[...]
        pre = ALPHA * x_ref[...] + mix_ref[...] + sp * pw
        mu = jnp.mean(pre, -1, keepdims=True)
        cen = pre - mu
        rstd = lax.rsqrt(jnp.mean(cen * cen, -1, keepdims=True) + LN_EPS)
        xhat = cen * rstd
        err = xhat * g_ref[...] + b_ref[...] - t_ref[...]
        dy = err * (1.0 / D_MODEL)
        dxh = dy * g_ref[...]
        dpre = rstd * (dxh - jnp.mean(dxh, -1, keepdims=True) - xhat * jnp.mean(dxh * xhat, -1, keepdims=True))
        dx_ref[...] = ALPHA * dpre
        dp_ref[...] = dpre.astype(BF16)
        dpw_ref[...] = (dpre * sp).astype(BF16)
        dgp = dpre * pw * (sp * (1.0 - sp))
        dgp_ref[...] = dgp.astype(BF16)
        s_ref[0:1, :] += jnp.sum(dy * xhat, 0, keepdims=True)
        s_ref[1:2, :] += jnp.sum(dy, 0, keepdims=True)
        s_ref[2:3, :] += jnp.sum(dgp, 0, keepdims=True)
        s_ref[3:4, :] += jnp.sum(err * err, 0, keepdims=True)

    spec = pl.BlockSpec((None, EW_TM, D_MODEL), lambda b, i: (b, i, 0))
    small = pl.BlockSpec((8, D_MODEL), lambda b, i: (0, 0))
    row = pl.BlockSpec((1, D_MODEL), lambda b, i: (0, 0))
    return _pcall(
        body, name="ln_loss", grid=(nb, seq // EW_TM),
        in_specs=[spec] * 5 + [small, row, row], out_specs=[spec] * 4 + [small],
        out_shape=[jax.ShapeDtypeStruct((nb, seq, D_MODEL), F32)] + [jax.ShapeDtypeStruct((nb, seq, D_MODEL), BF16)] * 3
        + [jax.ShapeDtypeStruct((8, D_MODEL), F32)],
        compiler_params=_params("arbitrary", "arbitrary"),
    )(x, mix, gp, pw, target, bgate, ln_g, ln_b)


def _adamw(w, g, m, v, name):
    rows, cols = w.shape
    tr = _row_tile(rows, cols, 8, 5 << 19)
    c1 = 1.0 - ADAM_B1 ** ADAM_STEP
    c2 = 1.0 - ADAM_B2 ** ADAM_STEP

    def body(w_ref, g_ref, m_ref, v_ref, d_ref, nm_ref, nv_ref):
        gv = g_ref[...]
        nm = ADAM_B1 * m_ref[...] + (1.0 - ADAM_B1) * gv
        nv = ADAM_B2 * v_ref[...] + (1.0 - ADAM_B2) * (gv * gv)
        d_ref[...] = -ADAM_LR * ((nm / c1) / (jnp.sqrt(nv / c2) + ADAM_EPS) + ADAM_WD * w_ref[...])
        nm_ref[...] = nm
        nv_ref[...] = nv

    spec = pl.BlockSpec((tr, cols), lambda i: (i, 0))
    return _pcall(
        body, name=name, grid=(rows // tr,), in_specs=[spec] * 4, out_specs=[spec] * 3,
        out_shape=[jax.ShapeDtypeStruct(w.shape, F32)] * 3, compiler_params=_params("parallel"),
    )(w, g, m, v)


def _sum_rows(parts, out_dtype, name):
    rows, cols = parts[0].shape
    tr = rows
    for cand in range(16, rows, 16):
        if rows % cand == 0 and cand * cols * 4 <= (1 << 20):
            tr = cand
    n = len(parts)

    def body(*refs):
        acc = refs[0][...].astype(F32)
        for r in refs[1:n]:
            acc = acc + r[...].astype(F32)
        refs[n][...] = acc.astype(out_dtype)

    spec = pl.BlockSpec((tr, cols), lambda i: (i, 0))
    return _pcall(
        body, name=name, grid=(rows // tr,), in_specs=[spec] * n, out_specs=spec,
        out_shape=jax.ShapeDtypeStruct((rows, cols), out_dtype), compiler_params=_params("parallel"),
    )(*parts)


def _place():
    return lax.axis_index("x"), lax.axis_index("y"), lax.axis_index("c")


def _other_chips(x, y):
    return [(1 - x, y), (x, 1 - y), (1 - x, 1 - y)]


def _remote(src, dst, send_sem, recv_sem, to):
    return pltpu.make_async_remote_copy(src_ref=src, dst_ref=dst, send_sem=send_sem, recv_sem=recv_sem,
                                        device_id=to, device_id_type=MESH)


ANY = pl.BlockSpec(memory_space=pl.ANY)
DMA_CHUNK_BYTES = 512 * 1024


def _row_chunks(rows, row_bytes):
    per = max(16, DMA_CHUNK_BYTES // row_bytes // 16 * 16)
    return [(s, min(per, rows - s)) for s in range(0, rows, per)]


def _row_tile(rows, cols, align, limit=1 << 21):
    best = None
    for cand in range(align, rows + 1, align):
        if rows % cand == 0 and cand * cols * 4 <= limit:
            best = cand
    return best or rows


def _allgather_pieces(pieces):
    n = len(pieces)
    halves = [_row_chunks(p.shape[0] // 2, p.shape[1] * p.dtype.itemsize) for p in pieces]
    entries = [(a, q, s, m, j) for a in range(n) for q, (s, m) in enumerate(halves[a]) for j in range(3)]
    slot = {(a, q, j): k for k, (a, q, _, _, j) in enumerate(entries)}
    n_ici = len(entries)

    def body(*refs):
        ins, outs = refs[:n], refs[n:2 * n]
        send_sems, recv_sems = refs[2 * n:]
        x, y, c = _place()
        me = 2 * x + y
        sibling = (x, y, 1 - c)
        chips = _other_chips(x, y)

        def landed(a, s, m, j, core):
            half = ins[a].shape[0] // 2
            return outs[a].at[2 * chips[j][0] + chips[j][1], pl.ds(core * half + s, m)]

        sent = []
        for k, (a, q, s, m, j) in enumerate(entries):
            if j < 2:
                half = ins[a].shape[0] // 2
                cp = _remote(ins[a].at[pl.ds(c * half + s, m)], outs[a].at[me, pl.ds(c * half + s, m)],
                             send_sems.at[k], recv_sems.at[k], (*chips[j], c))
                cp.start()
                sent.append(cp)

        def pass_to_sibling(k, blk):
            fw = _remote(blk, blk, send_sems.at[n_ici + k], recv_sems.at[n_ici + k], sibling)
            fw.start()
            sent.append(fw)

        for k, (a, q, s, m, j) in enumerate(entries):
            if j < 2:
                blk = landed(a, s, m, j, c)
                _remote(blk, blk, send_sems.at[k], recv_sems.at[k], (*chips[j], c)).wait_recv()
                first = q < (len(halves[a]) + 1) // 2
                if (j == 0) == first:
                    on = slot[(a, q, 2)]
                    rl = _remote(blk, blk, send_sems.at[on], recv_sems.at[on], (*chips[1 - j], c))
                    rl.start()
                    sent.append(rl)
                pass_to_sibling(k, blk)
        for k, (a, q, s, m, j) in enumerate(entries):
            if j == 2:
                blk = landed(a, s, m, j, c)
                _remote(blk, blk, send_sems.at[k], recv_sems.at[k], (*chips[j], c)).wait_recv()
                pass_to_sibling(k, blk)
        for k, (a, q, s, m, j) in enumerate(entries):
            blk = landed(a, s, m, j, 1 - c)
            _remote(blk, blk, send_sems.at[n_ici + k], recv_sems.at[n_ici + k], sibling).wait_recv()
        for cp in sent:
            cp.wait_send()

    gathered = _pcall(
        body, name="allgather_weights", in_specs=[ANY] * n, out_specs=[ANY] * n,
        out_shape=[jax.ShapeDtypeStruct((4,) + p.shape, p.dtype) for p in pieces],
        scratch_shapes=[pltpu.SemaphoreType.DMA((2 * n_ici,)), pltpu.SemaphoreType.DMA((2 * n_ici,))],
        compiler_params=pltpu.CompilerParams(has_side_effects=True),
    )(*pieces)
    x, y, _ = _place()
    return [lax.dynamic_update_slice(g, p[None], (2 * x + y, 0, 0)) for g, p in zip(gathered, pieces)]


def _sibling_exchange(grads):
    n = len(grads)
    chunks = [_row_chunks(g.shape[1] // 2, g.shape[2] * g.dtype.itemsize) for g in grads]
    n_sem = 4 * sum(len(ch) for ch in chunks)

    def body(*refs):
        ins, gots = refs[:n], refs[n:2 * n]
        send_sems, recv_sems = refs[2 * n:]
        x, y, c = _place()
        sibling = (x, y, 1 - c)
        work = []
        for a in range(n):
            half = ins[a].shape[1] // 2
            for piece in range(4):
                for s, m in chunks[a]:
                    k = len(work)
                    cp = _remote(ins[a].at[piece, pl.ds((1 - c) * half + s, m)], gots[a].at[piece, pl.ds(s, m)],
                                 send_sems.at[k], recv_sems.at[k], sibling)
                    cp.start()
                    work.append(cp)
        for cp in work:
            cp.wait()

    return _pcall(
        body, name="grad_sibling_exchange", in_specs=[ANY] * n, out_specs=[ANY] * n,
        out_shape=[jax.ShapeDtypeStruct((4, g.shape[1] // 2, g.shape[2]), g.dtype) for g in grads],
        scratch_shapes=[pltpu.SemaphoreType.DMA((n_sem,)), pltpu.SemaphoreType.DMA((n_sem,))],
        compiler_params=pltpu.CompilerParams(has_side_effects=True),
    )(*grads)


def _sibling_gather(fulls):
    n = len(fulls)
    chunks = [_row_chunks(f.shape[0] // 2, f.shape[1] * f.dtype.itemsize) for f in fulls]
    n_sem = sum(len(ch) for ch in chunks)

    def body(*refs):
        outs = refs[n:2 * n]
        send_sems, recv_sems = refs[2 * n:]
        x, y, c = _place()
        sibling = (x, y, 1 - c)
        work = []
        for a in range(n):
            h = outs[a].shape[0] // 2
            for s, m in chunks[a]:
                k = len(work)
                mine = outs[a].at[pl.ds(c * h + s, m)]
                cp = _remote(mine, mine, send_sems.at[k], recv_sems.at[k], sibling)
                cp.start()
                work.append((a, s, m, cp))
        for k, (a, s, m, cp) in enumerate(work):
            h = outs[a].shape[0] // 2
            cp.wait_send()
            theirs = outs[a].at[pl.ds((1 - c) * h + s, m)]
            _remote(theirs, theirs, send_sems.at[k], recv_sems.at[k], sibling).wait_recv()

    return _pcall(
        body, name="grad_sibling_gather", in_specs=[ANY] * n, out_specs=[ANY] * n,
        out_shape=[jax.ShapeDtypeStruct(f.shape, f.dtype) for f in fulls],
        input_output_aliases={a: a for a in range(n)},
        scratch_shapes=[pltpu.SemaphoreType.DMA((n_sem,)), pltpu.SemaphoreType.DMA((n_sem,))],
        compiler_params=pltpu.CompilerParams(has_side_effects=True),
    )(*fulls)


def _pair_sum(grad, got, place, name):
    _, rows, cols = grad.shape
    half = rows // 2
    tr = _row_tile(half, cols, 16)

    def body(p_ref, a_ref, b_ref, o_ref):
        o_ref[...] = (a_ref[...].astype(F32) + b_ref[...].astype(F32)).astype(BF16)

    return _pcall(
        body, name=name,
        grid_spec=pltpu.PrefetchScalarGridSpec(
            num_scalar_prefetch=1, grid=(4, half // tr),
            in_specs=[pl.BlockSpec((None, tr, cols), lambda k, i, p: (k, p[1] * (half // tr) + i, 0)),
                      pl.BlockSpec((None, tr, cols), lambda k, i, p: (k, i, 0))],
            out_specs=pl.BlockSpec((None, tr, cols), lambda k, i, p: (k, i, 0))),
        out_shape=jax.ShapeDtypeStruct((4, half, cols), BF16),
        compiler_params=_params("parallel", "parallel"),
    )(place, grad, got)


def _chip_sum(sums, got, place, name):
    _, h, cols = sums.shape
    tr = _row_tile(h, cols, 16)

    def body(p_ref, own_ref, g0, g1, g2, o_ref):
        o_ref[...] = ((own_ref[...].astype(F32) + g0[...].astype(F32)) + g1[...].astype(F32)) + g2[...].astype(F32)

    gspec = lambda j: pl.BlockSpec((None, tr, cols), lambda i, p: (j, i, 0))
    return _pcall(
        body, name=name,
        grid_spec=pltpu.PrefetchScalarGridSpec(
            num_scalar_prefetch=1, grid=(h // tr,),
            in_specs=[pl.BlockSpec((None, tr, cols), lambda i, p: (p[0], i, 0)), gspec(0), gspec(1), gspec(2)],
            out_specs=pl.BlockSpec((tr, cols), lambda i, p: (p[1] * (h // tr) + i, 0))),
        out_shape=jax.ShapeDtypeStruct((2 * h, cols), F32),
        compiler_params=_params("parallel"),
    )(place, sums, got, got, got)


def _allgather8(buf, name):
    rows = buf.shape[0]

    def body(in_ref, out_ref, send_sems, recv_sems):
        x, y, c = _place()
        me = 4 * x + 2 * y + c
        out_ref[me] = in_ref[...]
        work = []
        for rel in range(1, 8):
            fx, fy, fc = (rel >> 2) & 1, (rel >> 1) & 1, rel & 1
            to = (x ^ fx, y ^ fy, c ^ fc)
            cp = _remote(in_ref, out_ref.at[me], send_sems.at[rel - 1], recv_sems.at[rel - 1], to)
            cp.start()
            work.append((cp, 4 * to[0] + 2 * to[1] + to[2]))
        for rel, (cp, frm) in enumerate(work):
            cp.wait_send()
            blk = out_ref.at[frm]
            _remote(blk, blk, send_sems.at[rel], recv_sems.at[rel], (x, y, c)).wait_recv()

    return _pcall(
        body, name=name, in_specs=[pl.BlockSpec(memory_space=pltpu.VMEM)],
        out_specs=pl.BlockSpec(memory_space=pltpu.VMEM),
        out_shape=jax.ShapeDtypeStruct((8, rows, LANE), F32),
        scratch_shapes=[pltpu.SemaphoreType.DMA((7,)), pltpu.SemaphoreType.DMA((7,))],
        compiler_params=pltpu.CompilerParams(has_side_effects=True),
    )(buf)


def _pack_rows(arrs):
    parts = []
    for a in arrs:
        f = a.reshape(-1).astype(F32)
        parts.append(jnp.pad(f, (0, (-f.shape[0]) % LANE)))
    flat = jnp.concatenate(parts)
    rows = -(-flat.shape[0] // LANE)
    rows8 = -(-rows // 8) * 8
    return jnp.pad(flat, (0, rows8 * LANE - flat.shape[0])).reshape(rows8, LANE)


def _unpack_rows(buf, shapes):
    flat = buf.reshape(-1)
    outs, off = [], 0
    for s in shapes:
        n = int(np.prod(s))
        outs.append(flat[off:off + n].reshape(s))
        off += -(-n // LANE) * LANE
    return outs


def _local_grads(x, p, target, wseg, w_br16, w_out16, w_ple16, b_gate, conv_w, conv_b, dt_bias, a_log, d_skip,
                 ssm_norm_w, ln_g, ln_b, rel_bias, finish_dx):
    nb, seq, _ = x.shape
    bmaps = jnp.asarray(_bucket_maps())
    bias = _bias_tables(rel_bias, bmaps)
    bgate8 = jnp.pad(b_gate, ((0, 5), (0, 0)))
    dils = [d for _, d in PATTERNS]

    x16p = _token_orders(x, dils[1:])
    x16 = x16p[0]
    p16 = p.astype(BF16)
    qkv = [_proj(x16p[g], [wseg["qkv%d" % g]], BF16, "proj_qkv%d" % g, True)[0].reshape(
        nb, dils[g], seq // dils[g], -1) for g in range(3)]
    nat = {}
    for gi, (group, tm) in enumerate(NAT_GROUPS):
        outs = _proj(x16, [wseg[s] for s in group], F32, "proj_nat%d" % gi, True, tm)
        nat.update(zip(group, outs))
    att = [_attn_fwd(qkv[g], bias[g * GROUP_HEADS:(g + 1) * GROUP_HEADS], dils[g], "attn_fwd%d" % g) for g in range(3)]
    oa, o_att, lse = _combine_fwd(att[0][0], att[0][1], att[1:], nat["gatt"])

    conv_wg, conv_bg = _xbc_group_order(conv_w), _xbc_group_order(conv_b)
    act = _conv_fwd(nat["xbc"], conv_wg, conv_bg, "conv_fwd")
    dt_sp, dt_sg = _softplus_sig(nat["dt"], jnp.pad(dt_bias, ((0, 0), (0, LANE - SSM_HEADS))))
    dtg, sgg = _group_lanes(dt_sp), _group_lanes(dt_sg)
    alog_g, dskip_g = _group_lanes(a_log), _group_lanes(d_skip)
    y_ssm, y_all, sprev = _ssd_fwd(act, dtg, nat["z"], alog_g, dskip_g, ssm_norm_w)

    w_bra, w_brb = w_br16[:ATT_OUT], w_br16[ATT_OUT:]
    y_a, = _proj(oa, [w_bra], F32, "proj_ya")
    y_b, = _proj(y_ssm, [w_brb], F32, "proj_yb")
    merged = _merge_fwd(y_a, y_b, nat["gm"], bgate8)
    mix, = _proj(merged, [w_out16], F32, "proj_mix")
    pw, = _proj(p16, [w_ple16], F32, "proj_ple")

    dx, dpre16, dpw16, dgp16, ln_sums = _ln_loss(x, mix, nat["gp"], pw, target, bgate8, ln_g, ln_b)
    loss_sum = (0.5 / D_MODEL) * jnp.sum(ln_sums[3])
    dmerged = _dx([dpre16], [w_out16], [], "dx_merged")
    dya16, dyb16, dgm16, mg_sums = _merge_bwd(dmerged, y_a, y_b, nat["gm"], bgate8)
    doa = _dx([dya16], [w_bra], [], "dx_oa")
    dys = _dx([dyb16], [w_brb], [], "dx_yssm")
    g_w_out, = _dw(merged, [dpre16], BF16, "dw_out")
    g_w_br = jnp.concatenate([_dw(oa, [dya16], BF16, "dw_bra")[0], _dw(y_ssm, [dyb16], BF16, "dw_brb")[0]], axis=0)
    g_w_ple, = _dw(p16, [dpw16], BF16, "dw_ple")

    do_att, dgatt16, own_order = _combine_bwd(doa, nat["gatt"], o_att, lse, dils[1:])
    dseg = {"gatt": dgatt16, "gm": dgm16, "gp": dgp16}
    dbias = []
    for g in range(3):
        cotangent = (do_att, o_att, lse) if g == 0 else (own_order[2 * g - 2], own_order[2 * g - 1])
        dqkv, db = _attn_bwd(qkv[g], bias[g * GROUP_HEADS:(g + 1) * GROUP_HEADS], cotangent, dils[g],
                             "attn_bwd%d" % g)
        dseg["qkv%d" % g] = dqkv.reshape(nb, seq, -1)
        dbias.append(db)
    g_rel = _bias_grad(jnp.concatenate(dbias, axis=0), bmaps)[:, 0, :NUM_BUCKETS].T

    dact, ddtg, dz, ssd_small, g_normw = _ssd_bwd(
        act, dtg, sgg, nat["z"], y_all, dys, sprev, alog_g, dskip_g, ssm_norm_w)
    dseg["z"] = dz
    dseg["dt"] = jnp.pad(_ungroup_lanes(ddtg), ((0, 0), (0, 0), (0, LANE - SSM_HEADS)))
    dpre, conv_sums = _conv_bwd_pre(dact, nat["xbc"], conv_wg, conv_bg, "conv_bwd")
    dseg["xbc"] = _conv_bwd_x(dpre, conv_wg, "conv_bwd_x")
    csum = _xbc_reference_order(conv_sums)

    dx_own = [_dx([dseg["qkv%d" % g]], [wseg["qkv%d" % g]], [], "dx_qkv%d" % g, True).reshape(
        nb, dils[g], seq // dils[g], D_MODEL) for g in (1, 2)]
    dwseg = {"qkv%d" % g: _dw(x16p[g], [dseg["qkv%d" % g]], BF16, "dw_qkv%d" % g, True)[0] for g in range(3)}
    for gi, group in enumerate(DW_GROUPS):
        dwseg.update(zip(group, _dw(x16, [dseg[s] for s in group], BF16, "dw_nat%d" % gi, True)))
    names = ["qkv0"] + [s for group, _ in NAT_GROUPS for s in group]
    dx = finish_dx([dseg[s] for s in names], [wseg[s] for s in names], [dx], dx_own, dwseg, g_w_br, g_w_out, g_w_ple)

    small = dict(
        b_gate=jnp.stack([mg_sums[0], mg_sums[1], ln_sums[2]]),
        conv_w=csum[0:4], conv_b=csum[4:5],
        dt_bias=_ungroup_lanes(ssd_small[:, 2:3, :]), a_log=_ungroup_lanes(ssd_small[:, 0:1, :]),
        d_skip=_ungroup_lanes(ssd_small[:, 1:2, :]), ssm_norm_w=g_normw,
        ln_g=ln_sums[0:1], ln_b=ln_sums[1:2], rel_bias=g_rel)
    return loss_sum, dx, small


DX_TM = 256
SMALL_ORDER = ("b_gate", "conv_w", "conv_b", "dt_bias", "a_log", "d_skip", "ssm_norm_w", "ln_g", "ln_b", "rel_bias")
SMALL_FULL_SHAPES = dict(b_gate=(3, 1024), conv_w=(4, 3072), conv_b=(1, 3072), dt_bias=(1, 32), a_log=(1, 32),
                         d_skip=(1, 32), ssm_norm_w=(1, 2048), ln_g=(1, 1024), ln_b=(1, 1024), rel_bias=(32, 36))


def kernel(x, p, w_in, b_gate, conv_w, conv_b, dt_bias, a_log, d_skip, ssm_norm_w, w_branch, w_out, w_ple, ln_g, ln_b, rel_bias, loss_target, m_w_in, m_b_gate, m_conv_w, m_conv_b, m_dt_bias, m_a_log, m_d_skip, m_ssm_norm_w, m_w_branch, m_w_out, m_w_ple, m_ln_g, m_ln_b, m_rel_bias, v_w_in, v_b_gate, v_conv_w, v_conv_b, v_dt_bias, v_a_log, v_d_skip, v_ssm_norm_w, v_w_branch, v_w_out, v_w_ple, v_ln_g, v_ln_b, v_rel_bias):
    cx, cy, cc = _place()
    chip = 2 * cx + cy
    dev = 4 * cx + 2 * cy + cc

    w_in_t = jnp.transpose(w_in[0])
    win16 = _shard_to_window(w_in_t, chip)
    g_win, g_br, g_out, g_ple = _allgather_pieces(
        [win16, w_branch[0].astype(BF16), w_out[0].astype(BF16), w_ple[0].astype(BF16)])
    wseg = _assemble(g_win)
    w_br16 = g_br.reshape(4 * 704, D_MODEL)
    w_out16 = g_out.reshape(D_MODEL, D_MODEL)
    w_ple16 = jnp.transpose(g_ple, (1, 0, 2)).reshape(PLE_DIM, D_MODEL)
    shards = _allgather8(_pack_rows([b_gate[0], conv_w[0]]), "allgather_small_params")
    per_chip = [_unpack_rows(shards[2 * k], [(3, 256), (4, 768)]) for k in range(4)]
    b_gate_full = jnp.concatenate([pc[0] for pc in per_chip], axis=1)
    conv_w_full = jnp.concatenate([pc[1] for pc in per_chip], axis=1)

    place = jnp.stack([chip, cc]).astype(jnp.int32)
    reduced = []

    def finish_dx(dhs, ws, accs, own_order_accs, dwseg, d_br, d_out, d_ple):
        grads = [_pack(dwseg), d_br.reshape(4, 704, D_MODEL), d_out.reshape(4, 256, D_MODEL),
                 jnp.transpose(d_ple.reshape(PLE_DIM, 4, 256), (1, 0, 2))]
        got = _sibling_exchange(grads)
        chip_sums = [_pair_sum(g, t, place, "grad_pair_sum_%d" % i) for i, (g, t) in enumerate(zip(grads, got))]
        dx, others = _dx(dhs, ws, accs, "dx_w_in_and_grad_chip_scatter", True, DX_TM, chip_sums, own_order_accs)
        fulls = [_chip_sum(s, t, place, "grad_chip_sum_%d" % i) for i, (s, t) in enumerate(zip(chip_sums, others))]
        reduced.extend(_sibling_gather(fulls))
        return dx

    loss_sum, grad_x, small = _local_grads(
        x, p[0], loss_target, wseg, w_br16, w_out16, w_ple16, b_gate_full, conv_w_full, conv_b, dt_bias, a_log,
        d_skip, ssm_norm_w, ln_g, ln_b, rel_bias, finish_dx)
    big = reduced
    g_w_in = _window_to_shard(big[0], chip)
    g_w_branch, g_w_out, g_w_ple = big[1], big[2], big[3]
    parts = _allgather8(_pack_rows([small[n] for n in SMALL_ORDER] + [loss_sum.reshape(1, 1)]),
                        "allgather_small_grads")
    small_sum = _sum_rows([parts[i] for i in range(8)], F32, "small_grad_sum")
    *reduced_small, loss = _unpack_rows(small_sum, [SMALL_FULL_SHAPES[n] for n in SMALL_ORDER] + [(1, 1)])
    loss = loss.reshape(())
    sg = dict(zip(SMALL_ORDER, reduced_small))
    sg["b_gate"] = lax.dynamic_slice_in_dim(sg["b_gate"], chip * 256, 256, axis=1)
    sg["conv_w"] = lax.dynamic_slice_in_dim(sg["conv_w"], chip * 768, 768, axis=1)
    del dev

    upd = {}
    upd["w_in"] = [jnp.transpose(t) for t in _adamw(w_in_t, g_w_in, jnp.transpose(m_w_in[0]),
                                                      jnp.transpose(v_w_in[0]), "adamw_w_in")]
    upd["w_branch"] = _adamw(w_branch[0], g_w_branch, m_w_branch[0], v_w_branch[0], "adamw_w_branch")
    upd["w_out"] = _adamw(w_out[0], g_w_out, m_w_out[0], v_w_out[0], "adamw_w_out")
    upd["w_ple"] = _adamw(w_ple[0], g_w_ple, m_w_ple[0], v_w_ple[0], "adamw_w_ple")
    small_w = dict(b_gate=b_gate, conv_w=conv_w, conv_b=conv_b, dt_bias=dt_bias, a_log=a_log, d_skip=d_skip,
                   ssm_norm_w=ssm_norm_w, ln_g=ln_g, ln_b=ln_b, rel_bias=rel_bias)
    small_m = dict(b_gate=m_b_gate, conv_w=m_conv_w, conv_b=m_conv_b, dt_bias=m_dt_bias, a_log=m_a_log,
                   d_skip=m_d_skip, ssm_norm_w=m_ssm_norm_w, ln_g=m_ln_g, ln_b=m_ln_b, rel_bias=m_rel_bias)
    small_v = dict(b_gate=v_b_gate, conv_w=v_conv_w, conv_b=v_conv_b, dt_bias=v_dt_bias, a_log=v_a_log,
                   d_skip=v_d_skip, ssm_norm_w=v_ssm_norm_w, ln_g=v_ln_g, ln_b=v_ln_b, rel_bias=v_rel_bias)
    shapes = [small_w[n].shape for n in SMALL_ORDER]
    s_delta, s_m, s_v = _adamw(_pack_rows([small_w[n] for n in SMALL_ORDER]), _pack_rows([sg[n] for n in SMALL_ORDER]),
                               _pack_rows([small_m[n] for n in SMALL_ORDER]), _pack_rows([small_v[n] for n in SMALL_ORDER]),
                               "adamw_small")
    for i, n in enumerate(SMALL_ORDER):
        upd[n] = tuple(_unpack_rows(t, shapes)[i] for t in (s_delta, s_m, s_v))
        sg[n] = sg[n].reshape(small_w[n].shape)

    order = ("w_in", "b_gate", "conv_w", "conv_b", "dt_bias", "a_log", "d_skip", "ssm_norm_w", "w_branch", "w_out",
             "w_ple", "ln_g", "ln_b", "rel_bias")
    grads = dict(sg, w_in=jnp.transpose(g_w_in)[None],w_branch=g_w_branch[None], w_out=g_w_out[None], w_ple=g_w_ple[None])
    lead = lambda n, t: t[None] if n in ("w_in", "w_branch", "w_out", "w_ple") else t
    return (loss, grad_x, *[grads[n] for n in order], *[lead(n, upd[n][0]) for n in order],
            *[lead(n, upd[n][1]) for n in order], *[lead(n, upd[n][2]) for n in order])
```

```python
import functools
import math

import numpy as np
import jax
import jax.numpy as jnp
from jax import lax
from jax.experimental import pallas as pl
from jax.experimental.pallas import tpu as pltpu

F32, BF16 = jnp.float32, jnp.bfloat16

D_MODEL = 1024
HEAD_DIM = 64
GROUP_HEADS = 12
ATT_OUT = GROUP_HEADS * HEAD_DIM
PATTERNS = ((128, 1), (512, 4), (2048, 16))
BAND = 128
NUM_BUCKETS = 32
MAX_DISTANCE = 2048
D_INNER = 2048
SSM_HEADS = 32
SSM_GROUPS = 4
GROUP_SSM_HEADS = SSM_HEADS // SSM_GROUPS
D_STATE = 128
CHUNK = 128
PLE_DIM = 256
ALPHA = 2.0 ** 0.25
LN_EPS = 1e-5
RMS_EPS = 1e-5
ADAM_LR, ADAM_B1, ADAM_B2, ADAM_EPS, ADAM_WD, ADAM_STEP = 0.001, 0.9, 0.999, 1e-08, 0.01, 10
NEG = -1e30

QKV_W = 3 * ATT_OUT
IN_COLS = 15904
SHARD_COLS = IN_COLS // 4
DT_COL = 12800
ROW_TILE = 16
WIN_ROWS = 4000


def _win_offset(k):
    return (k * SHARD_COLS) % ROW_TILE


def _win_start(k):
    return k * SHARD_COLS - _win_offset(k)

VMEM_LIMIT_BYTES = 56 * 1024 * 1024
LANE = 128
MESH = pl.DeviceIdType.MESH
NT = (((1,), (1,)), ((), ()))
TN = (((0,), (0,)), ((), ()))


def _pcall(body, **kw):
    return pl.pallas_call(body, **kw)


def _params(*sem):
    return pltpu.CompilerParams(dimension_semantics=sem, vmem_limit_bytes=VMEM_LIMIT_BYTES)


def _sigmoid(v):
    return jax.nn.sigmoid(v)


MM_TM = 512


def _tok_spec(tm, width):
    return pl.BlockSpec((None, tm, width), lambda b, i: (b, i, 0))


def _whole(arr, single_buffer=False):
    mode = dict(pipeline_mode=pl.Buffered(1)) if single_buffer else {}
    return pl.BlockSpec(arr.shape, lambda b, i: (0,) * arr.ndim, **mode)


def _proj(a3, ws, out_dtype, name, w_rows_are_outputs=False, tm=MM_TM):
    nb, seq, kdim = a3.shape
    nw = len(ws)
    widths = [w.shape[0] if w_rows_are_outputs else w.shape[1] for w in ws]

    def body(*refs):
        a = refs[0][...].astype(BF16)
        for w_ref, o_ref in zip(refs[1:1 + nw], refs[1 + nw:]):
            if w_rows_are_outputs:
                v = lax.dot_general(a, w_ref[...], NT, preferred_element_type=F32)
            else:
                v = jnp.dot(a, w_ref[...], preferred_element_type=F32)
            o_ref[...] = v.astype(out_dtype)

    return _pcall(
        body, name=name, grid=(nb, seq // tm),
        in_specs=[_tok_spec(tm, kdim)] + [_whole(w, True) for w in ws],
        out_specs=[_tok_spec(tm, n) for n in widths],
        out_shape=[jax.ShapeDtypeStruct((nb, seq, n), out_dtype) for n in widths],
        compiler_params=_params("parallel", "parallel"),
    )(a3, *ws)


def _dx(dhs, ws, accs, name, w_rows_are_outputs=False, tm=MM_TM, scatter=None, own_order_accs=()):
    nb, seq, _ = dhs[0].shape
    nd, nacc, npa = len(dhs), len(accs), len(own_order_accs)
    kout = ws[0].shape[1] if w_rows_are_outputs else ws[0].shape[0]
    sums = scatter or []
    ns = len(sums)
    chunks = [_row_chunks(s.shape[1], s.shape[2] * s.dtype.itemsize) for s in sums]
    n_sem = 3 * sum(len(ch) for ch in chunks)
    grid = (nb, seq // tm)
    ntile = kout // LANE if npa else 0

    def body(*refs):
        n_in = 2 * nd + nacc + npa
        sum_refs, o_ref, got_refs = refs[n_in:n_in + ns], refs[n_in + ns], refs[n_in + ns + 1:n_in + 2 * ns + 1]
        tile_refs = refs[n_in + 2 * ns + 1:n_in + 2 * ns + 1 + ntile]

        def copies():
            send_sems, recv_sems = refs[-2], refs[-1]
            x, y, c = _place()
            out = []
            for a in range(ns):
                for s, m in chunks[a]:
                    for j, (cx, cy) in enumerate(_other_chips(x, y)):
                        k = len(out)
                        out.append(_remote(sum_refs[a].at[2 * cx + cy, pl.ds(s, m)], got_refs[a].at[j, pl.ds(s, m)],
                                           send_sems.at[k], recv_sems.at[k], (cx, cy, c)))
            return out

        if ns:
            @pl.when((pl.program_id(0) == 0) & (pl.program_id(1) == 0))
            def _():
                for cp in copies():
                    cp.start()

        v = None
        for dh_ref, w_ref in zip(refs[:nd], refs[nd:2 * nd]):
            dh = dh_ref[...].astype(BF16)
            if w_rows_are_outputs:
                t = jnp.dot(dh, w_ref[...], preferred_element_type=F32)
            else:
                t = lax.dot_general(dh, w_ref[...], NT, preferred_element_type=F32)
            v = t if v is None else v + t
        for a_ref in refs[2 * nd:2 * nd + nacc]:
            v = v + a_ref[...]
        for p_ref in refs[2 * nd + nacc:n_in]:
            v = v + _natural_rows(p_ref, tile_refs)
        o_ref[...] = v

        if ns:
            @pl.when((pl.program_id(0) == grid[0] - 1) & (pl.program_id(1) == grid[1] - 1))
            def _():
                for cp in copies():
                    cp.wait()

    out = _pcall(
        body, name=name, grid=grid,
        in_specs=[_tok_spec(tm, dh.shape[-1]) for dh in dhs] + [_whole(w, True) for w in ws]
        + [_tok_spec(tm, kout)] * nacc
        + [pl.BlockSpec((None, p.shape[1], tm // p.shape[1], kout), lambda b, i: (b, 0, i, 0)) for p in own_order_accs]
        + [ANY] * ns,
        out_specs=[_tok_spec(tm, kout)] + [ANY] * ns,
        out_shape=[jax.ShapeDtypeStruct((nb, seq, kout), F32)]
        + [jax.ShapeDtypeStruct((3,) + s.shape[1:], s.dtype) for s in sums],
        input_output_aliases={2 * nd: 0} if nacc else {},
        scratch_shapes=[pltpu.VMEM((tm, LANE), F32)] * ntile
        + ([pltpu.SemaphoreType.DMA((n_sem,)), pltpu.SemaphoreType.DMA((n_sem,))] if ns else []),
        compiler_params=pltpu.CompilerParams(
            dimension_semantics=("arbitrary", "arbitrary") if ns else ("parallel", "parallel"),
            vmem_limit_bytes=VMEM_LIMIT_BYTES, has_side_effects=bool(ns)),
    )(*dhs, *ws, *accs, *own_order_accs, *sums)
    return (out[0], list(out[1:])) if ns else out[0]


def _dw(a3, dhs, out_dtype, name, rows_are_outputs=False):
    nb, seq, kdim = a3.shape
    nd = len(dhs)
    grid = (nb, seq // MM_TM)
    shapes = [(dh.shape[-1], kdim) if rows_are_outputs else (kdim, dh.shape[-1]) for dh in dhs]

    def body(*refs):
        b, i = pl.program_id(0), pl.program_id(1)
        dh_refs, o_refs, acc_refs = refs[1:1 + nd], refs[1 + nd:1 + 2 * nd], refs[1 + 2 * nd:]

        @pl.when((b == 0) & (i == 0))
        def _():
            for acc_ref in acc_refs:
                acc_ref[...] = jnp.zeros_like(acc_ref)

        a = refs[0][...].astype(BF16)
        for dh_ref, acc_ref in zip(dh_refs, acc_refs):
            dh = dh_ref[...].astype(BF16)
            acc_ref[...] += lax.dot_general(*((dh, a) if rows_are_outputs else (a, dh)), TN,
                                            preferred_element_type=F32)

        @pl.when((b == grid[0] - 1) & (i == grid[1] - 1))
        def _():
            for o_ref, acc_ref in zip(o_refs, acc_refs):
                o_ref[...] = acc_ref[...].astype(out_dtype)

    return _pcall(
        body, name=name, grid=grid,
        in_specs=[_tok_spec(MM_TM, kdim)] + [_tok_spec(MM_TM, dh.shape[-1]) for dh in dhs],
        out_specs=[pl.BlockSpec(s, lambda b, i: (0, 0)) for s in shapes],
        out_shape=[jax.ShapeDtypeStruct(s, out_dtype) for s in shapes],
        scratch_shapes=[pltpu.VMEM(s, F32) for s in shapes],
        compiler_params=_params("arbitrary", "arbitrary"),
    )(a3, *dhs)


def _qkv_rows(g):
    return [(part * QKV_W + g * ATT_OUT + hp * LANE, LANE) for hp in range(ATT_OUT // LANE) for part in range(3)]


XBC_START = 3 * QKV_W + ATT_OUT + D_INNER
GROUP_CH = GROUP_SSM_HEADS * HEAD_DIM
XBC_GROUP = GROUP_CH + 2 * D_STATE
CONV_DIM = SSM_GROUPS * XBC_GROUP


def _xbc_ranges():
    out = []
    for g in range(SSM_GROUPS):
        out += [(g * GROUP_CH, GROUP_CH), (D_INNER + g * D_STATE, D_STATE),
                (D_INNER + SSM_GROUPS * D_STATE + g * D_STATE, D_STATE)]
    return out


def _xbc_group_order(t):
    return jnp.concatenate([t[..., s:s + n] for s, n in _xbc_ranges()], axis=-1)


def _xbc_reference_order(t):
    g = lambda off, n: [t[..., k * XBC_GROUP + off:k * XBC_GROUP + off + n] for k in range(SSM_GROUPS)]
    return jnp.concatenate(g(0, GROUP_CH) + g(GROUP_CH, D_STATE) + g(GROUP_CH + D_STATE, D_STATE), axis=-1)


def _segments():
    one = lambda name, start, rows: (name, [(start, rows)], max(rows, LANE))
    return [("qkv%d" % g, _qkv_rows(g), QKV_W) for g in range(3)] + [
        one("gatt", 3 * QKV_W, ATT_OUT), one("z", 3 * QKV_W + ATT_OUT, D_INNER),
        ("xbc", [(XBC_START + s, n) for s, n in _xbc_ranges()], CONV_DIM), one("dt", DT_COL, SSM_HEADS),
        one("gm", DT_COL + SSM_HEADS, 2 * D_MODEL), one("gp", DT_COL + SSM_HEADS + 2 * D_MODEL, D_MODEL)]


LAYOUT_TC = 256
NAT_GROUPS = ((("gatt", "z", "dt", "gp"), 512), (("xbc", "gm"), 512))
DW_GROUPS = (("gatt", "z", "dt", "gp"), ("xbc",), ("gm",))


def _assemble(win):
    segs = _segments()

    def body(win_ref, *outs):
        def pieces(start, rows):
            t, end = start, start + rows
            while t < end:
                k = min(t // SHARD_COLS, 3)
                shard_end = (k + 1) * SHARD_COLS
                if k < 3 and shard_end % ROW_TILE and t == shard_end - shard_end % ROW_TILE:
                    lo = t - _win_start(k)
                    yield win_ref[k, lo:lo + ROW_TILE, :] + win_ref[k + 1, 0:ROW_TILE, :]
                    t += ROW_TILE
                    continue
                upto = min(end, shard_end - shard_end % ROW_TILE if k < 3 else end)
                yield win_ref[k, t - _win_start(k):upto - _win_start(k), :]
                t = upto

        for (_, ranges, total), o_ref in zip(segs, outs):
            off = 0
            for start, rows in ranges:
                for part in pieces(start, rows):
                    o_ref[off:off + part.shape[0], :] = part
                    off += part.shape[0]
            if off < total:
                o_ref[off:total, :] = jnp.zeros((total - off, o_ref.shape[1]), BF16)

    outs = _pcall(
        body, name="assemble_w_in", grid=(D_MODEL // LAYOUT_TC,),
        in_specs=[pl.BlockSpec((4, WIN_ROWS, LAYOUT_TC), lambda i: (0, 0, i))],
        out_specs=[pl.BlockSpec((total, LAYOUT_TC), lambda i: (0, i)) for _, _, total in segs],
        out_shape=[jax.ShapeDtypeStruct((total, D_MODEL), BF16) for _, _, total in segs],
        compiler_params=_params("parallel"),
    )(win)
    return {name: o for (name, _, _), o in zip(segs, outs)}


def _pack(dsegs):
    segs = _segments()

    def body(*refs):
        ins, o_ref = refs[:-1], refs[-1]
        tail = IN_COLS - _win_start(3)
        o_ref[3, tail:, :] = jnp.zeros((WIN_ROWS - tail, o_ref.shape[2]), BF16)
        for (_, ranges, _), s_ref in zip(segs, ins):
            off = 0
            for start, rows in ranges:
                for k in range(4):
                    lo = _win_start(k)
                    a, b = max(start, lo), min(start + rows, lo + WIN_ROWS)
                    if a < b:
                        o_ref[k, a - lo:b - lo, :] = s_ref[off + a - start:off + b - start, :]
                off += rows

    return _pcall(
        body, name="pack_dw_in", grid=(D_MODEL // LAYOUT_TC,),
        in_specs=[pl.BlockSpec((total, LAYOUT_TC), lambda i: (0, i)) for _, _, total in segs],
        out_specs=pl.BlockSpec((4, WIN_ROWS, LAYOUT_TC), lambda i: (0, 0, i)),
        out_shape=jax.ShapeDtypeStruct((4, WIN_ROWS, D_MODEL), BF16),
        compiler_params=_params("parallel"),
    )(*[dsegs[name] for name, _, _ in segs])


def _shard_to_window(shard_t, k):
    def at(off):
        return lambda w: jnp.pad(w.astype(BF16), ((off, WIN_ROWS - SHARD_COLS - off), (0, 0)))

    return lax.cond(k % 2 == 1, at(_win_offset(1)), at(_win_offset(0)), shard_t)


def _window_to_shard(win, k):
    return lax.dynamic_slice(win, ((k % 2) * _win_offset(1), 0), (SHARD_COLS, D_MODEL))


def _bucket_maps():
    qi = np.arange(8)[:, None]
    kj = np.arange(2 * BAND)[None, :]
    delta = qi + BAND - kj
    maps = []
    for window, dil in PATTERNS:
        valid = (delta >= 0) & (delta <= window // dil)
        dist = np.maximum(delta, 0) * dil
        max_exact = NUM_BUCKETS // 2
        d_f = np.maximum(dist, 1).astype(np.float32)
        large = max_exact + (np.log(d_f / np.float32(max_exact)) / np.float32(math.log(MAX_DISTANCE / max_exact))
                             * np.float32(NUM_BUCKETS - max_exact)).astype(np.int32)
        large = np.minimum(large, NUM_BUCKETS - 1)
        bucket = np.where(dist < max_exact, dist, large)
        maps.append(np.where(valid, bucket, -1).astype(np.int32))
    return np.stack(maps)


def _bias_tables(rel_bias, bmaps):
    def body(rb_ref, bm_ref, o_ref):
        g = pl.program_id(0)
        bm = bm_ref[...]
        for hh in range(GROUP_HEADS):
            acc = jnp.full(bm.shape, NEG, F32)
            for b in range(NUM_BUCKETS):
                acc = jnp.where(bm == b, rb_ref[b, g * GROUP_HEADS + hh], acc)
            for a in range(BAND // 8):
                o_ref[hh, 8 * a:8 * a + 8, :] = acc if a == 0 else pltpu.roll(acc, 8 * a, 1)

    return _pcall(
        body, name="bias_tables", grid=(3,),
        in_specs=[pl.BlockSpec(memory_space=pltpu.SMEM),
                  pl.BlockSpec((None, 8, 2 * BAND), lambda g: (g, 0, 0))],
        out_specs=pl.BlockSpec((GROUP_HEADS, BAND, 2 * BAND), lambda g: (g, 0, 0)),
        out_shape=jax.ShapeDtypeStruct((3 * GROUP_HEADS, BAND, 2 * BAND), F32),
        compiler_params=_params("parallel"),
    )(rel_bias, bmaps)


def _bias_grad(dbias, bmaps):
    def body(db_ref, bm_ref, o_ref):
        bm = bm_ref[...]
        lane = lax.broadcasted_iota(jnp.int32, (1, LANE), 1)
        for hh in range(GROUP_HEADS):
            db = db_ref[hh, 0:8, :]
            for a in range(1, BAND // 8):
                db = db + pltpu.roll(db_ref[hh, 8 * a:8 * a + 8, :], 2 * BAND - 8 * a, 1)
            vec = jnp.zeros((1, LANE), F32)
            for b in range(NUM_BUCKETS):
                s = jnp.sum(jnp.where(bm == b, db, 0.0), keepdims=True)
                vec = jnp.where(lane == b, s, vec)
            o_ref[hh] = vec

    return _pcall(
        body, name="bias_grad", grid=(3,),
        in_specs=[pl.BlockSpec((GROUP_HEADS, BAND, 2 * BAND), lambda g: (g, 0, 0)),
                  pl.BlockSpec((None, 8, 2 * BAND), lambda g: (g, 0, 0))],
        out_specs=pl.BlockSpec((GROUP_HEADS, 1, LANE), lambda g: (g, 0, 0)),
        out_shape=jax.ShapeDtypeStruct((3 * GROUP_HEADS, 1, LANE), F32),
        compiler_params=_params("parallel"),
    )(dbias, bmaps)


def _rows(n):
    if isinstance(n, int):
        return pl.ds(n * BAND, BAND)
    return pl.ds(pl.multiple_of(n * BAND, BAND), BAND)


def _for_blocks(blocks, nblk, per, carry):
    carry = blocks([0], carry, False)
    start = 1 + (nblk - 1) % per
    for n in range(1, start):
        carry = blocks([n], carry, True)
    trips = (nblk - start) // per
    if trips > 0:
        carry = lax.fori_loop(
            0, trips, lambda t, c: blocks([start + t * per + u for u in range(per)], c, True), carry)
    return carry


def _pairs_per_step(d):
    return {1: 3, 4: 6, 16: 6}[d]


def _bias_spec(group, hps):
    first = group * GROUP_HEADS // (2 * hps)
    return pl.BlockSpec((2 * hps, BAND, 2 * BAND), lambda hp, b, r: (first + hp, 0, 0))


def _attn_fwd(qkv4, bias, group, d, name):
    nb, _, sub, _ = qkv4.shape
    nblk = sub // BAND
    scale = HEAD_DIM ** -0.5
    npair = ATT_OUT // LANE
    hps = _pairs_per_step(d)
    compact = d > 1

    def body(qkv_ref, bias_ref, o_ref, l_ref):
        def blocks(ns, carry, with_prev):
            chains = [(bi, i, h) for bi in range(len(ns)) for i in range(hps) for h in range(2)]
            first_head = lax.broadcasted_iota(jnp.int32, (BAND, LANE), 1) < HEAD_DIM
            pair = lambda n, i, part: qkv_ref[_rows(n), (3 * i + part) * LANE:(3 * i + part + 1) * LANE]
            scores = []
            for bi, i, h in chains:
                n = ns[bi]
                qp = pair(n, i, 0) * scale
                q = jnp.where(first_head if h == 0 else jnp.logical_not(first_head), qp, jnp.zeros_like(qp))
                s_c = lax.dot_general(q, pair(n, i, 1), NT, preferred_element_type=F32) + bias_ref[2 * i + h, :, BAND:]
                s_p = None
                if with_prev:
                    s_p = lax.dot_general(q, pair(n - 1, i, 1), NT,
                                          preferred_element_type=F32) + bias_ref[2 * i + h, :, :BAND]
                scores.append((s_c, s_p))
            probs = []
            for s_c, s_p in scores:
                m = jnp.max(s_c, -1, keepdims=True)
                if with_prev:
                    m = jnp.maximum(m, jnp.max(s_p, -1, keepdims=True))
                e_c = jnp.exp(s_c - m)
                den = jnp.sum(e_c, -1, keepdims=True)
                e_p = None
                if with_prev:
                    e_p = jnp.exp(s_p - m)
                    den = den + jnp.sum(e_p, -1, keepdims=True)
                    e_p = e_p.astype(BF16)
                probs.append((e_c.astype(BF16), e_p, den, m))
            outs = {}
            for (bi, i, h), (e_c, e_p, den, m) in zip(chains, probs):
                n = ns[bi]
                acc = jnp.dot(e_c, pair(n, i, 2), preferred_element_type=F32)
                if with_prev:
                    acc = acc + jnp.dot(e_p, pair(n - 1, i, 2), preferred_element_type=F32)
                outs[(bi, i, h)] = (acc / den, m + jnp.log(den))
            lane = lax.broadcasted_iota(jnp.int32, (BAND, LANE), 1)
            for bi, n in enumerate(ns):
                per_head = jnp.zeros((BAND, LANE), F32)
                for i in range(hps):
                    o_ref[_rows(n), i * LANE:(i + 1) * LANE] = jnp.where(first_head, outs[(bi, i, 0)][0],
                                                                         outs[(bi, i, 1)][0])
                    if compact:
                        for h in range(2):
                            per_head = jnp.where(lane == 2 * i + h, outs[(bi, i, h)][1], per_head)
                    else:
                        l_ref[_rows(n), i * LANE:(i + 1) * LANE] = jnp.where(first_head, outs[(bi, i, 0)][1],
                                                                             outs[(bi, i, 1)][1])
                if compact:
                    l_ref[_rows(n), :] = per_head
            return carry

        _for_blocks(blocks, nblk, 2 if hps == 1 else 1, 0)

    in_specs = [pl.BlockSpec((None, None, sub, 3 * LANE * hps), lambda hp, b, r: (b, r, 0, hp)),
                _bias_spec(group, hps)]
    if compact:
        return _pcall(
            body, name=name, grid=(1, nb, d), in_specs=in_specs,
            out_specs=[pl.BlockSpec((None, None, sub, ATT_OUT), lambda hp, b, r: (b, r, 0, 0)),
                       pl.BlockSpec((None, None, sub, LANE), lambda hp, b, r: (b, r, 0, 0))],
            out_shape=[jax.ShapeDtypeStruct((nb, d, sub, ATT_OUT), F32), jax.ShapeDtypeStruct((nb, d, sub, LANE), F32)],
            compiler_params=_params("parallel", "parallel", "parallel"),
        )(qkv4, bias)
    ospec = pl.BlockSpec((None, sub, hps * LANE), lambda hp, b, r: (b, 0, r * (npair // hps) + hp))
    return _pcall(
        body, name=name, grid=(npair // hps, nb, d), in_specs=in_specs, out_specs=[ospec, ospec],
        out_shape=[jax.ShapeDtypeStruct((nb, sub, d * ATT_OUT), F32)] * 2,
        compiler_params=_params("parallel", "parallel", "parallel"),
    )(qkv4, bias)


STAT_LSE_LANE = 16


def _attn_bwd(qkv4, bias, group, cotangent, d, name):
    nb, _, sub, _ = qkv4.shape
    nblk = sub // BAND
    scale = HEAD_DIM ** -0.5
    npair = ATT_OUT // LANE
    hps = _pairs_per_step(d)
    compact = d > 1

    def body(qkv_ref, bias_ref, *rest):
        do_ref, dqkv_ref, db_ref = rest[0], rest[-2], rest[-1]
        b, r = pl.program_id(1), pl.program_id(2)

        @pl.when((b == 0) & (r == 0))
        def _():
            db_ref[...] = jnp.zeros_like(db_ref)

        def blocks(ns, carry, with_prev):
            sides = (0, 1) if with_prev else (0,)
            chains = [(bi, i, h, sd) for bi in range(len(ns)) for i in range(hps) for h in range(2) for sd in sides]
            first_head = lax.broadcasted_iota(jnp.int32, (BAND, LANE), 1) < HEAD_DIM
            own = lambda h, t: jnp.where(first_head if h == 0 else jnp.logical_not(first_head), t, jnp.zeros_like(t))
            pair = lambda rows, i, part: qkv_ref[rows, (3 * i + part) * LANE:(3 * i + part + 1) * LANE]
            key_rows = lambda bi, sd: _rows(ns[bi] - sd)
            qs = {}
            for bi in range(len(ns)):
                for i in range(hps):
                    q_pair = pair(_rows(ns[bi]), i, 0) * scale
                    do = do_ref[_rows(ns[bi]), i * LANE:(i + 1) * LANE]
                    do16 = do.astype(BF16)
                    for h in range(2):
                        if compact:
                            st_ref, head = rest[1], 2 * i + h
                            ebar = st_ref[_rows(ns[bi]), head:head + 1]
                            lcol = st_ref[_rows(ns[bi]), STAT_LSE_LANE + head:STAT_LSE_LANE + head + 1]
                        else:
                            ebar = jnp.sum(own(h, do * rest[1][_rows(ns[bi]), i * LANE:(i + 1) * LANE]), -1, keepdims=True)
                            lcol = rest[2][_rows(ns[bi]), i * LANE + h * HEAD_DIM:i * LANE + h * HEAD_DIM + 1]
                        qs[(bi, i, h)] = (own(h, q_pair), q_pair, own(h, do16), do16, ebar, lcol)
            raw = []
            for bi, i, h, sd in chains:
                q, _, do_h, _, _, _ = qs[(bi, i, h)]
                bias_blk = bias_ref[2 * i + h, :, :BAND] if sd else bias_ref[2 * i + h, :, BAND:]
                s = lax.dot_general(q, pair(key_rows(bi, sd), i, 1), NT, preferred_element_type=F32) + bias_blk
                dp = lax.dot_general(do_h, pair(key_rows(bi, sd), i, 2), NT, preferred_element_type=F32)
                raw.append((s, dp))
            soft = []
            for (bi, i, h, sd), (s, dp) in zip(chains, raw):
                ebar, lcol = qs[(bi, i, h)][4:]
                p = jnp.exp(s - lcol)
                ds = p * (dp - ebar)
                if sd:
                    db_ref[2 * i + h, :, :BAND] += ds
                else:
                    db_ref[2 * i + h, :, BAND:] += ds
                soft.append((p.astype(BF16), ds.astype(BF16)))
            grads = {}
            for (bi, i, h, sd), (p16, ds16) in zip(chains, soft):
                _, q_pair, _, do16 = qs[(bi, i, h)][:4]
                grads[(bi, i, h, sd)] = (
                    jnp.dot(ds16, pair(key_rows(bi, sd), i, 1), preferred_element_type=F32),
                    lax.dot_general(ds16, q_pair, TN, preferred_element_type=F32),
                    lax.dot_general(p16, do16, TN, preferred_element_type=F32))
            both = lambda bi, i, sd, which: jnp.where(first_head, grads[(bi, i, 0, sd)][which],
                                                      grads[(bi, i, 1, sd)][which])
            carry = list(carry) if carry is not None else None
            for bi, n in enumerate(ns):
                for i in range(hps):
                    base = 3 * LANE * i
                    dq = both(bi, i, 0, 0)
                    if with_prev:
                        dq = dq + both(bi, i, 1, 0)
                        dqkv_ref[_rows(n - 1), base + LANE:base + 2 * LANE] = (
                            carry[2 * i] + both(bi, i, 1, 1)).astype(BF16)
                        dqkv_ref[_rows(n - 1), base + 2 * LANE:base + 3 * LANE] = (
                            carry[2 * i + 1] + both(bi, i, 1, 2)).astype(BF16)
                    dqkv_ref[_rows(n), base:base + LANE] = (dq * scale).astype(BF16)
                carry = [t for i in range(hps) for t in (both(bi, i, 0, 1), both(bi, i, 0, 2))]
            return tuple(carry)

        carry = _for_blocks(blocks, nblk, 2 if hps == 1 else 1, None)
        for i in range(hps):
            base = 3 * LANE * i
            dqkv_ref[_rows(nblk - 1), base + LANE:base + 2 * LANE] = carry[2 * i].astype(BF16)
            dqkv_ref[_rows(nblk - 1), base + 2 * LANE:base + 3 * LANE] = carry[2 * i + 1].astype(BF16)

    qspec = pl.BlockSpec((None, None, sub, 3 * LANE * hps), lambda hp, b, r: (b, r, 0, hp))
    bspec = pl.BlockSpec((2 * hps, BAND, 2 * BAND), lambda hp, b, r: (hp, 0, 0))
    if compact:
        cspecs = [pl.BlockSpec((None, None, sub, ATT_OUT), lambda hp, b, r: (b, r, 0, 0)),
                  pl.BlockSpec((None, None, sub, LANE), lambda hp, b, r: (b, r, 0, 0))]
    else:
        cspecs = [pl.BlockSpec((None, sub, hps * LANE), lambda hp, b, r: (b, 0, r * (npair // hps) + hp))] * 3
    return _pcall(
        body, name=name, grid=(npair // hps, nb, d),
        in_specs=[qspec, _bias_spec(group, hps)] + cspecs, out_specs=[qspec, bspec],
        out_shape=[jax.ShapeDtypeStruct(qkv4.shape, BF16),
                   jax.ShapeDtypeStruct((GROUP_HEADS, BAND, 2 * BAND), F32)],
        compiler_params=_params("parallel", "arbitrary", "arbitrary"),
    )(qkv4, bias, *cotangent)


def _head_lanes(first_lane, one_channel):
    c = lax.broadcasted_iota(jnp.int32, (ATT_OUT, LANE), 0)
    lane = lax.broadcasted_iota(jnp.int32, (ATT_OUT, LANE), 1)
    hit = lane == first_lane + c // HEAD_DIM
    if one_channel:
        hit = hit & (c % HEAD_DIM == 0)
    return hit.astype(BF16)


def _exact_dot(v, m01, dims=None):
    parts = _split3(v)
    if dims is None:
        dot = lambda t: jnp.dot(t, m01, preferred_element_type=F32)
    else:
        dot = lambda t: lax.dot_general(t, m01, dims, preferred_element_type=F32)
    return (dot(parts[0]) + dot(parts[1])) + dot(parts[2])


def _store_own_order(value, tile_refs, out_ref):
    d, per, width = out_ref.shape
    for j in range(width // LANE):
        tile_refs[j][...] = value[:, j * LANE:(j + 1) * LANE]
    for r in range(d):
        rows = pl.ds(r, per, stride=d)
        for j in range(width // LANE):
            out_ref[r, :, j * LANE:(j + 1) * LANE] = tile_refs[j][rows, :].astype(out_ref.dtype)


def _token_orders(x, dilations):
    nb, seq, kdim = x.shape
    tm = 512

    def body(x_ref, nat_ref, *rest):
        outs, tile_refs = rest[:len(dilations)], rest[len(dilations):]
        xv = x_ref[...]
        nat_ref[...] = xv.astype(BF16)
        for o_ref in outs:
            _store_own_order(xv, tile_refs, o_ref)

    outs = _pcall(
        body, name="token_orders", grid=(nb, seq // tm), in_specs=[_tok_spec(tm, kdim)],
        out_specs=[_tok_spec(tm, kdim)]
        + [pl.BlockSpec((None, d, tm // d, kdim), lambda b, i: (b, 0, i, 0)) for d in dilations],
        out_shape=[jax.ShapeDtypeStruct((nb, seq, kdim), BF16)]
        + [jax.ShapeDtypeStruct((nb, d, seq // d, kdim), BF16) for d in dilations],
        scratch_shapes=[pltpu.VMEM((tm, LANE), F32)] * (kdim // LANE),
        compiler_params=_params("parallel", "parallel"),
    )(x)
    return [outs[0]] + [o.reshape(nb, seq, kdim) for o in outs[1:]]


def _natural_rows(p_ref, tile_refs):
    d, per, width = p_ref.shape
    for r in range(d):
        rows = pl.ds(r, per, stride=d)
        for j in range(width // LANE):
            tile_refs[j][rows, :] = p_ref[r, :, j * LANE:(j + 1) * LANE]
    return jnp.concatenate([tile_refs[j][...] for j in range(width // LANE)], axis=1)


def _combine_fwd(o0, l0, dilated, gatt):
    nb, seq, _ = gatt.shape
    tm = 512
    ntile = ATT_OUT // LANE

    def body(o0_ref, l0_ref, o1_ref, l1_ref, o2_ref, l2_ref, g_ref, oa_ref, oatt_ref, lse_ref, *tile_refs):
        spread = _head_lanes(0, False)
        l0v = l0_ref[...]
        l1v = _exact_dot(_natural_rows(l1_ref, tile_refs), spread, NT)
        l2v = _exact_dot(_natural_rows(l2_ref, tile_refs), spread, NT)
        m = jnp.maximum(jnp.maximum(l0v, l1v), l2v)
        tot = m + jnp.log(jnp.exp(l0v - m) + jnp.exp(l1v - m) + jnp.exp(l2v - m))
        o = jnp.exp(l0v - tot) * o0_ref[...]
        o = o + jnp.exp(l1v - tot) * _natural_rows(o1_ref, tile_refs)
        o = o + jnp.exp(l2v - tot) * _natural_rows(o2_ref, tile_refs)
        g = g_ref[...]
        oa_ref[...] = (o * (g * _sigmoid(g))).astype(BF16)
        oatt_ref[...] = o
        lse_ref[...] = tot

    spec = pl.BlockSpec((None, tm, ATT_OUT), lambda b, i: (b, i, 0))
    own = lambda t: pl.BlockSpec((None, t.shape[1], tm // t.shape[1], t.shape[3]), lambda b, i: (b, 0, i, 0))
    (o1, l1), (o2, l2) = dilated
    return _pcall(
        body, name="attn_combine", grid=(nb, seq // tm),
        in_specs=[spec, spec, own(o1), own(l1), own(o2), own(l2), spec], out_specs=[spec] * 3,
        out_shape=[jax.ShapeDtypeStruct((nb, seq, ATT_OUT), BF16), jax.ShapeDtypeStruct((nb, seq, ATT_OUT), F32),
                   jax.ShapeDtypeStruct((nb, seq, ATT_OUT), F32)],
        scratch_shapes=[pltpu.VMEM((tm, LANE), F32)] * ntile,
        compiler_params=_params("parallel", "parallel"),
    )(o0, l0, o1, l1, o2, l2, gatt)


def _combine_bwd(doa, gatt, o_att, lse, dilations):
    nb, seq, _ = gatt.shape
    tm = 512

    def body(doa_ref, g_ref, o_ref, l_ref, do_ref, dg_ref, *rest):
        ntile = ATT_OUT // LANE
        outs, tile_refs = rest[:-ntile], rest[-ntile:]
        g = g_ref[...]
        sg = _sigmoid(g)
        do = doa_ref[...] * (g * sg)
        do_ref[...] = do
        stats = (_exact_dot(do * o_ref[...], _head_lanes(0, False))
                 + _exact_dot(l_ref[...], _head_lanes(STAT_LSE_LANE, True)))
        dg_ref[...] = (doa_ref[...] * o_ref[...] * (sg * (1.0 + g * (1.0 - sg)))).astype(BF16)
        for k in range(len(dilations)):
            _store_own_order(do, tile_refs, outs[2 * k])
            _store_own_order(stats, tile_refs, outs[2 * k + 1])

    spec = pl.BlockSpec((None, tm, ATT_OUT), lambda b, i: (b, i, 0))
    own = lambda d, width: pl.BlockSpec((None, d, tm // d, width), lambda b, i: (b, 0, i, 0))
    outs = _pcall(
        body, name="attn_combine_bwd", grid=(nb, seq // tm), in_specs=[spec] * 4,
        out_specs=[spec, spec] + [own(d, w) for d in dilations for w in (ATT_OUT, LANE)],
        out_shape=[jax.ShapeDtypeStruct((nb, seq, ATT_OUT), F32), jax.ShapeDtypeStruct((nb, seq, ATT_OUT), BF16)]
        + [jax.ShapeDtypeStruct((nb, d, seq // d, w), t) for d in dilations for w, t in ((ATT_OUT, BF16), (LANE, F32))],
        scratch_shapes=[pltpu.VMEM((tm, LANE), F32)] * (ATT_OUT // LANE),
        compiler_params=_params("parallel", "parallel"),
    )(doa, gatt, o_att, lse)
    return outs[0], outs[1], outs[2:]


CONV_TM = 1024
CONV_TC = 1024


def _shift_down(cur, halo, k):
    rolled = pltpu.roll(cur, k, 0)
    hro = pltpu.roll(halo, k, 0)
    row = lax.broadcasted_iota(jnp.int32, hro.shape, 0)
    return jnp.concatenate([jnp.where(row < k, hro, rolled[:8]), rolled[8:]], axis=0)


def _shift_up(cur, halo, k):
    n = cur.shape[0]
    rolled = pltpu.roll(cur, n - k, 0)
    hro = pltpu.roll(halo, 8 - k, 0)
    row = lax.broadcasted_iota(jnp.int32, hro.shape, 0)
    return jnp.concatenate([rolled[:n - 8], jnp.where(row >= 8 - k, hro, rolled[n - 8:])], axis=0)


def _conv_pre(cur, halo, w_ref, b_ref):
    acc = cur * w_ref[3:4, :] + b_ref[...]
    for k in range(1, 4):
        acc = acc + _shift_down(cur, halo, k) * w_ref[3 - k:4 - k, :]
    return acc


def _conv_specs(seq):
    nblk = seq // CONV_TM
    cur = pl.BlockSpec((None, CONV_TM, CONV_TC), lambda cb, b, i: (b, i, cb))
    prev = pl.BlockSpec((None, 8, CONV_TC), lambda cb, b, i: (b, jnp.maximum(i * (CONV_TM // 8) - 1, 0), cb))
    nxt = pl.BlockSpec((None, 8, CONV_TC),
                       lambda cb, b, i: (b, jnp.minimum((i + 1) * (CONV_TM // 8), seq // 8 - 1), cb))
    wspec = pl.BlockSpec((4, CONV_TC), lambda cb, b, i: (0, cb))
    bspec = pl.BlockSpec((1, CONV_TC), lambda cb, b, i: (0, cb))
    return nblk, cur, prev, nxt, wspec, bspec


def _conv_fwd(xin, w4, bias, name):
    nb, seq, ch = xin.shape
    _, cur, prev, _, wspec, bspec = _conv_specs(seq)

    def body(x_ref, h_ref, w_ref, b_ref, o_ref):
        halo = jnp.where(pl.program_id(2) > 0, h_ref[...], 0.0)
        pre = _conv_pre(x_ref[...], halo, w_ref, b_ref)
        o_ref[...] = pre * _sigmoid(pre)

    return _pcall(
        body, name=name, grid=(ch // CONV_TC, nb, seq // CONV_TM),
        in_specs=[cur, prev, wspec, bspec], out_specs=cur,
        out_shape=jax.ShapeDtypeStruct(xin.shape, F32),
        compiler_params=_params("parallel", "parallel", "parallel"),
    )(xin, xin, w4, bias)


def _conv_bwd_pre(dact, xin, w4, bias, name):
    nb, seq, ch = xin.shape
    _, cur, prev, _, wspec, bspec = _conv_specs(seq)

    def body(da_ref, x_ref, h_ref, w_ref, b_ref, dp_ref, s_ref):
        b, i = pl.program_id(1), pl.program_id(2)

        @pl.when((b == 0) & (i == 0))
        def _():
            s_ref[...] = jnp.zeros_like(s_ref)

        halo = jnp.where(i > 0, h_ref[...], 0.0)
        x = x_ref[...]
        pre = _conv_pre(x, halo, w_ref, b_ref)
        sg = _sigmoid(pre)
        dpre = da_ref[...] * (sg * (1.0 + pre * (1.0 - sg)))
        dp_ref[...] = dpre
        s_ref[3:4, :] += jnp.sum(dpre * x, 0, keepdims=True)
        for k in range(1, 4):
            s_ref[3 - k:4 - k, :] += jnp.sum(dpre * _shift_down(x, halo, k), 0, keepdims=True)
        s_ref[4:5, :] += jnp.sum(dpre, 0, keepdims=True)

    return _pcall(
        body, name=name, grid=(ch // CONV_TC, nb, seq // CONV_TM),
        in_specs=[cur, cur, prev, wspec, bspec],
        out_specs=[cur, pl.BlockSpec((8, CONV_TC), lambda cb, b, i: (0, cb))],
        out_shape=[jax.ShapeDtypeStruct(xin.shape, F32), jax.ShapeDtypeStruct((8, ch), F32)],
        compiler_params=_params("parallel", "arbitrary", "arbitrary"),
    )(dact, xin, xin, w4, bias)


def _conv_bwd_x(dpre, w4, name):
    nb, seq, ch = dpre.shape
    nblk, cur, _, nxt, wspec, _ = _conv_specs(seq)

    def body(d_ref, n_ref, w_ref, o_ref):
        halo = jnp.where(pl.program_id(2) < nblk - 1, n_ref[...], 0.0)
        cur_v = d_ref[...]
        acc = cur_v * w_ref[3:4, :]
        for j in range(1, 4):
            acc = acc + _shift_up(cur_v, halo, j) * w_ref[3 - j:4 - j, :]
        o_ref[...] = acc.astype(BF16)

    return _pcall(
        body, name=name, grid=(ch // CONV_TC, nb, seq // CONV_TM),
        in_specs=[cur, nxt, wspec], out_specs=cur,
        out_shape=jax.ShapeDtypeStruct(dpre.shape, BF16),
        compiler_params=_params("parallel", "parallel", "parallel"),
    )(dpre, dpre, w4)


def _softplus_sig(dt_raw, dt_bias_row):
    nb, seq, _ = dt_raw.shape
    tm = 512

    def body(r_ref, b_ref, sp_ref, sg_ref):
        v = r_ref[...] + b_ref[...]
        sp_ref[...] = jnp.maximum(v, 0.0) + jnp.log1p(jnp.exp(-jnp.abs(v)))
        sg_ref[...] = _sigmoid(v)

    spec = pl.BlockSpec((None, tm, LANE), lambda b, i: (b, i, 0))
    return _pcall(
        body, name="dt_softplus", grid=(nb, seq // tm),
        in_specs=[spec, pl.BlockSpec((1, LANE), lambda b, i: (0, 0))], out_specs=[spec, spec],
        out_shape=[jax.ShapeDtypeStruct(dt_raw.shape, F32)] * 2,
        compiler_params=_params("parallel", "parallel"),
    )(dt_raw, dt_bias_row)


def _group_lanes(t):
    pads = [(0, 0)] * (t.ndim - 1) + [(0, LANE - GROUP_SSM_HEADS)]
    return jnp.stack([jnp.pad(t[..., GROUP_SSM_HEADS * g:GROUP_SSM_HEADS * (g + 1)], pads) for g in range(SSM_GROUPS)])


def _ungroup_lanes(t):
    return jnp.concatenate([t[g][..., :GROUP_SSM_HEADS] for g in range(SSM_GROUPS)], axis=-1)


def _decays(dt, al_ref):
    row = lax.broadcasted_iota(jnp.int32, (CHUNK, CHUNK), 0)
    col = lax.broadcasted_iota(jnp.int32, (CHUNK, CHUNK), 1)
    tril = (row >= col).astype(BF16)
    triu = (row <= col).astype(BF16)
    arow = -jnp.exp(al_ref[...])
    hi, mid, lo = _split3(dt * arow)
    down = lambda t: jnp.dot(tril, t, preferred_element_type=F32)
    across = lambda t: lax.dot_general(t, triu, TN, preferred_element_type=F32)
    acs = (down(hi) + down(mid)) + down(lo)
    acs_t = (across(hi) + across(mid)) + across(lo)
    return arow, acs, acs_t, row >= col, triu


STEP_CHUNKS = 8


def _ssd_specs(nb, seq):
    nc = seq // CHUNK
    hw = GROUP_SSM_HEADS * HEAD_DIM
    rows, steps = STEP_CHUNKS * CHUNK, nc // STEP_CHUNKS

    def mk(rev):
        cidx = (lambda c: steps - 1 - c) if rev else (lambda c: c)
        wide = pl.BlockSpec((None, rows, hw), lambda g, b, c: (b, cidx(c), g))
        xbc = pl.BlockSpec((None, rows, XBC_GROUP), lambda g, b, c: (b, cidx(c), g))
        lanes = pl.BlockSpec((None, None, rows, LANE), lambda g, b, c: (g, b, cidx(c), 0))
        prev = pl.BlockSpec((None, STEP_CHUNKS, None, D_STATE, hw), lambda g, b, c: (b, cidx(c), g, 0, 0))
        return wide, xbc, lanes, prev

    grow = pl.BlockSpec((None, 1, LANE), lambda g, b, c: (g, 0, 0))
    nwspec = pl.BlockSpec((1, hw), lambda g, b, c: (0, g))
    return nc, steps, hw, mk, grow, nwspec


def _head_expand():
    hw = GROUP_SSM_HEADS * HEAD_DIM
    r = lax.broadcasted_iota(jnp.int32, (LANE, hw), 0)
    c = lax.broadcasted_iota(jnp.int32, (LANE, hw), 1)
    return ((c // HEAD_DIM) == r).astype(BF16)


def _split3(v):
    hi = v.astype(BF16)
    rest = v - hi.astype(F32)
    mid = rest.astype(BF16)
    return hi, mid, (rest - mid.astype(F32)).astype(BF16)


def _to_channels(v, e):
    hi, mid, lo = _split3(v)
    dot = lambda t: jnp.dot(t, e, preferred_element_type=F32)
    return (dot(hi) + dot(mid)) + dot(lo)


def _to_heads(w, e):
    hi, mid, lo = _split3(w)
    dot = lambda t: lax.dot_general(t, e, (((1,), (1,)), ((), ())), preferred_element_type=F32)
    return (dot(hi) + dot(mid)) + dot(lo)


def _row8(v):
    return jnp.broadcast_to(v, (8, v.shape[1]))


def _ssd_chunk_setup(dt, al_ref, ds_ref):
    arow, acs, acs_t, causal, triu = _decays(dt, al_ref)
    e = _head_expand()
    dtx = _to_channels(dt, e)
    acsx = _to_channels(acs, e)
    lastx = acsx[CHUNK - 1:CHUNK, :]
    dskx = _to_channels(_row8(ds_ref[...]), e)[0:1, :]
    return arow, acs, acs_t, causal, triu, e, dtx, acsx, lastx, dskx


def _ssd_fwd(xbc, dtg, z, alog_g, dskip_g, normw):
    nb, seq, _ = xbc.shape
    nc, steps, hw, mk, grow, nwspec = _ssd_specs(nb, seq)
    wide, xbc_spec, lanes, prev = mk(False)
    tn = (((0,), (0,)), ((), ()))

    def body(xbc_ref, dt_ref, z_ref, al_ref, ds_ref, nw_ref, ys_ref, y_ref, sp_ref, st_ref):
        @pl.when(pl.program_id(2) == 0)
        def _():
            st_ref[...] = jnp.zeros_like(st_ref)

        for ci in range(STEP_CHUNKS):
            chunk(ci, xbc_ref, dt_ref, z_ref, al_ref, ds_ref, nw_ref, ys_ref, y_ref, sp_ref, st_ref)

    def chunk(ci, xbc_ref, dt_ref, z_ref, al_ref, ds_ref, nw_ref, ys_ref, y_ref, sp_ref, st_ref):
        rows = slice(ci * CHUNK, (ci + 1) * CHUNK)
        dt = dt_ref[rows, :]
        _, acs, acs_t, causal, _, _, dtx, acsx, lastx, dskx = _ssd_chunk_setup(dt, al_ref, ds_ref)
        bmat = xbc_ref[rows, GROUP_CH:GROUP_CH + D_STATE].astype(BF16)
        cmat = xbc_ref[rows, GROUP_CH + D_STATE:].astype(BF16)
        cb = lax.dot_general(cmat, bmat, (((1,), (1,)), ((), ())), preferred_element_type=F32)
        x = xbc_ref[rows, :GROUP_CH]
        xdt = x * dtx
        xdt16 = xdt.astype(BF16)
        first_head = lax.broadcasted_iota(jnp.int32, (CHUNK, LANE), 1) < HEAD_DIM
        pairs = []
        for hp in range(GROUP_SSM_HEADS // 2):
            xp = xdt16[:, hp * LANE:(hp + 1) * LANE]
            two = []
            for j in (2 * hp, 2 * hp + 1):
                lmat = jnp.exp(jnp.where(causal, acs[:, j:j + 1] - acs_t[j:j + 1, :], -jnp.inf))
                two.append(jnp.dot((cb * lmat).astype(BF16), xp, preferred_element_type=F32))
            pairs.append(jnp.where(first_head, two[0], two[1]))
        yd = jnp.concatenate(pairs, axis=1)
        s_prev = st_ref[...]
        s16 = s_prev.astype(BF16)
        sp_ref[ci] = s16
        yo = jnp.dot(cmat, s16, preferred_element_type=F32) * jnp.exp(acsx)
        sts = lax.dot_general(bmat, (xdt * jnp.exp(lastx - acsx)).astype(BF16), tn, preferred_element_type=F32)
        st_ref[...] = s_prev * jnp.exp(lastx) + sts
        y = yd + yo + dskx * x
        zz = z_ref[rows, :]
        u = y * (zz * _sigmoid(zz))
        rn = lax.rsqrt(jnp.mean(u * u, -1, keepdims=True) + RMS_EPS)
        ys_ref[rows, :] = (u * rn * nw_ref[...]).astype(BF16)
        y_ref[rows, :] = y

    return _pcall(
        body, name="ssd_fwd", grid=(SSM_GROUPS, nb, steps),
        in_specs=[xbc_spec, lanes, wide, grow, grow, nwspec],
        out_specs=[wide, wide, prev],
        out_shape=[jax.ShapeDtypeStruct((nb, seq, D_INNER), BF16), jax.ShapeDtypeStruct((nb, seq, D_INNER), F32),
                   jax.ShapeDtypeStruct((nb, nc, SSM_GROUPS, D_STATE, hw), BF16)],
        scratch_shapes=[pltpu.VMEM((D_STATE, hw), F32)],
        compiler_params=_params("parallel", "parallel", "arbitrary"),
    )(xbc, dtg, z, alog_g, dskip_g, normw)


def _ssd_bwd(xbc, dtg, sgg, z, y, dys, sprev, alog_g, dskip_g, normw):
    nb, seq, _ = xbc.shape
    nc, steps, hw, mk, grow, nwspec = _ssd_specs(nb, seq)
    wide, xbc_spec, lanes, prev = mk(True)
    nt = (((1,), (1,)), ((), ()))
    tn = (((0,), (0,)), ((), ()))

    def body(xbc_ref, dt_ref, sg_ref, z_ref, y_ref, dys_ref, sp_ref, al_ref, ds_ref, nw_ref,
             dxbc_ref, ddt_ref, dz_ref, small_ref, dnw_ref, g_ref):
        b, c = pl.program_id(1), pl.program_id(2)

        @pl.when((b == 0) & (c == 0))
        def _():
            small_ref[...] = jnp.zeros_like(small_ref)
            dnw_ref[...] = jnp.zeros_like(dnw_ref)

        @pl.when(c == 0)
        def _():
            g_ref[...] = jnp.zeros_like(g_ref)

        for ci in reversed(range(STEP_CHUNKS)):
            chunk(ci, xbc_ref, dt_ref, sg_ref, z_ref, y_ref, dys_ref, sp_ref, al_ref, ds_ref, nw_ref,
                  dxbc_ref, ddt_ref, dz_ref, small_ref, dnw_ref, g_ref)

    def chunk(ci, xbc_ref, dt_ref, sg_ref, z_ref, y_ref, dys_ref, sp_ref, al_ref, ds_ref, nw_ref,
              dxbc_ref, ddt_ref, dz_ref, small_ref, dnw_ref, g_ref):
        rows = slice(ci * CHUNK, (ci + 1) * CHUNK)
        yv, zz, dys_v, nw = y_ref[rows, :], z_ref[rows, :], dys_ref[rows, :], nw_ref[...]
        sz = _sigmoid(zz)
        silu = zz * sz
        u = yv * silu
        rn = lax.rsqrt(jnp.mean(u * u, -1, keepdims=True) + RMS_EPS)
        gn = dys_v * nw
        du = rn * gn - u * (rn * rn * rn) * jnp.mean(u * gn, -1, keepdims=True)
        dnw_ref[...] += jnp.sum(dys_v * u * rn, 0, keepdims=True)
        dy = du * silu
        dz_ref[rows, :] = du * yv * (sz * (1.0 + zz * (1.0 - sz)))

        dt = dt_ref[rows, :]
        arow, acs, acs_t, causal, triu, e, dtx, acsx, lastx, dskx = _ssd_chunk_setup(dt, al_ref, ds_ref)
        dfsx = jnp.exp(acsx)
        dtex = jnp.exp(lastx - acsx)
        bmat = xbc_ref[rows, GROUP_CH:GROUP_CH + D_STATE].astype(BF16)
        cmat = xbc_ref[rows, GROUP_CH + D_STATE:].astype(BF16)
        cb = lax.dot_general(cmat, bmat, nt, preferred_element_type=F32)
        x = xbc_ref[rows, :GROUP_CH]
        xdt = x * dtx
        xdt16 = xdt.astype(BF16)
        xdte = xdt * dtex
        dy16 = dy.astype(BF16)
        dyd = dy * dfsx
        dyd16 = dyd.astype(BF16)
        s16 = sp_ref[ci]
        g = g_ref[...]
        g16 = g.astype(BF16)
        cs = jnp.dot(cmat, s16, preferred_element_type=F32)
        dc_off = lax.dot_general(dyd16, s16, nt, preferred_element_type=F32)
        g_here = lax.dot_general(cmat, dyd16, tn, preferred_element_type=F32)
        bg = jnp.dot(bmat, g16, preferred_element_type=F32)
        db_st = lax.dot_general(xdte.astype(BF16), g16, nt, preferred_element_type=F32)
        ddte_w = bg * xdte
        dcd = _to_heads(_row8(jnp.sum(g * s16.astype(F32), 0, keepdims=True)), e)[0:1, :]
        lane = lax.broadcasted_iota(jnp.int32, (CHUNK, LANE), 1)
        first_head = lane < HEAD_DIM
        sub = lax.broadcasted_iota(jnp.int32, (CHUNK, LANE), 0)
        dacs = jnp.zeros((CHUNK, LANE), F32)
        colsums = jnp.zeros((CHUNK, LANE), F32)
        dcb = jnp.zeros((CHUNK, CHUNK), F32)
        pairs = []
        for hp in range(GROUP_SSM_HEADS // 2):
            xp = xdt16[:, hp * LANE:(hp + 1) * LANE]
            dyp = dy16[:, hp * LANE:(hp + 1) * LANE]
            two = []
            for idx, j in enumerate((2 * hp, 2 * hp + 1)):
                lmat = jnp.exp(jnp.where(causal, acs[:, j:j + 1] - acs_t[j:j + 1, :], -jnp.inf))
                mf = cb * lmat
                dy_h = jnp.where(first_head if idx == 0 else jnp.logical_not(first_head), dyp, jnp.zeros_like(dyp))
                dm = lax.dot_general(dy_h, xp, nt, preferred_element_type=F32)
                two.append(lax.dot_general(mf.astype(BF16), dyp, tn, preferred_element_type=F32))
                wmat = dm * mf
                dcb = dcb + dm * lmat
                dacs = jnp.where(lane == j, jnp.sum(wmat, -1, keepdims=True), dacs)
                colsums = jnp.where(sub == j, jnp.sum(wmat, 0, keepdims=True), colsums)
            pairs.append(jnp.where(first_head, two[0], two[1]))
        dxdt = bg * dtex + jnp.concatenate(pairs, axis=1)
        dacs = dacs - colsums.T + _to_heads(dyd * cs - ddte_w, e)
        cd_row = jnp.exp(acs[CHUNK - 1:CHUNK, :])
        tail = _to_heads(_row8(jnp.sum(ddte_w, 0, keepdims=True)), e)[0:1, :] + dcd * cd_row
        dacs = dacs + jnp.where(sub == CHUNK - 1, tail, 0.0)
        d_hi, d_mid, d_lo = _split3(dacs)
        up = lambda t: jnp.dot(triu, t, preferred_element_type=F32)
        da = (up(d_hi) + up(d_mid)) + up(d_lo)
        ddt_raw = (da * arow + _to_heads(dxdt * x, e)) * sg_ref[rows, :]
        ddt_ref[rows, :] = ddt_raw
        small_ref[0:1, :] += jnp.sum(da * dt, 0, keepdims=True) * arow
        small_ref[1:2, :] += _to_heads(_row8(jnp.sum(dy * x, 0, keepdims=True)), e)[0:1, :]
        small_ref[2:3, :] += jnp.sum(ddt_raw, 0, keepdims=True)
        dcb16 = dcb.astype(BF16)
        dxbc_ref[rows, GROUP_CH + D_STATE:] = dc_off + jnp.dot(dcb16, bmat, preferred_element_type=F32)
        dxbc_ref[rows, GROUP_CH:GROUP_CH + D_STATE] = db_st + lax.dot_general(dcb16, cmat, tn,
                                                                               preferred_element_type=F32)
        dxbc_ref[rows, :GROUP_CH] = dxdt * dtx + dskx * dy
        g_ref[...] = g * jnp.exp(lastx) + g_here

    return _pcall(
        body, name="ssd_bwd", grid=(SSM_GROUPS, nb, steps),
        in_specs=[xbc_spec, lanes, lanes, wide, wide, wide, prev, grow, grow, nwspec],
        out_specs=[xbc_spec, lanes, wide,
                   pl.BlockSpec((None, 8, LANE), lambda g, b, c: (g, 0, 0)), nwspec],
        out_shape=[jax.ShapeDtypeStruct((nb, seq, CONV_DIM), F32),
                   jax.ShapeDtypeStruct((SSM_GROUPS, nb, seq, LANE), F32),
                   jax.ShapeDtypeStruct((nb, seq, D_INNER), F32),
                   jax.ShapeDtypeStruct((SSM_GROUPS, 8, LANE), F32),
                   jax.ShapeDtypeStruct((1, D_INNER), F32)],
        scratch_shapes=[pltpu.VMEM((D_STATE, hw), F32)],
        compiler_params=_params("parallel", "arbitrary", "arbitrary"),
    )(xbc, dtg, sgg, z, y, dys, sprev, alog_g, dskip_g, normw)


EW_TM = 256


def _merge_fwd(y_a, y_b, gm, bgate):
    nb, seq, _ = y_a.shape

    def body(a_ref, b_ref, ga_ref, gb_ref, bg_ref, o_ref):
        sa = _sigmoid(ga_ref[...] + bg_ref[0:1, :])
        sb = _sigmoid(gb_ref[...] + bg_ref[1:2, :])
        o_ref[...] = (sa * a_ref[...] + sb * b_ref[...]).astype(BF16)

    spec = pl.BlockSpec((None, EW_TM, D_MODEL), lambda b, i: (b, i, 0))
    spec1 = pl.BlockSpec((None, EW_TM, D_MODEL), lambda b, i: (b, i, 1))
    return _pcall(
        body, name="merge_fwd", grid=(nb, seq // EW_TM),
        in_specs=[spec, spec, spec, spec1, pl.BlockSpec((8, D_MODEL), lambda b, i: (0, 0))], out_specs=spec,
        out_shape=jax.ShapeDtypeStruct((nb, seq, D_MODEL), BF16),
        compiler_params=_params("parallel", "parallel"),
    )(y_a, y_b, gm, gm, bgate)


def _merge_bwd(dmerged, y_a, y_b, gm, bgate):
    nb, seq, _ = y_a.shape

    def body(dm_ref, a_ref, b_ref, ga_ref, gb_ref, bg_ref, dya_ref, dyb_ref, dg_ref, s_ref):
        @pl.when((pl.program_id(0) == 0) & (pl.program_id(1) == 0))
        def _():
            s_ref[...] = jnp.zeros_like(s_ref)

        dm = dm_ref[...]
        sa = _sigmoid(ga_ref[...] + bg_ref[0:1, :])
        sb = _sigmoid(gb_ref[...] + bg_ref[1:2, :])
        dya_ref[...] = (dm * sa).astype(BF16)
        dyb_ref[...] = (dm * sb).astype(BF16)
        dga = dm * a_ref[...] * (sa * (1.0 - sa))
        dgb = dm * b_ref[...] * (sb * (1.0 - sb))
        dg_ref[:, :D_MODEL] = dga.astype(BF16)
        dg_ref[:, D_MODEL:] = dgb.astype(BF16)
        s_ref[0:1, :] += jnp.sum(dga, 0, keepdims=True)
        s_ref[1:2, :] += jnp.sum(dgb, 0, keepdims=True)

    spec = pl.BlockSpec((None, EW_TM, D_MODEL), lambda b, i: (b, i, 0))
    spec1 = pl.BlockSpec((None, EW_TM, D_MODEL), lambda b, i: (b, i, 1))
    small = pl.BlockSpec((8, D_MODEL), lambda b, i: (0, 0))
    return _pcall(
        body, name="merge_bwd", grid=(nb, seq // EW_TM),
        in_specs=[spec, spec, spec, spec, spec1, small],
        out_specs=[spec, spec, pl.BlockSpec((None, EW_TM, 2 * D_MODEL), lambda b, i: (b, i, 0)), small],
        out_shape=[jax.ShapeDtypeStruct((nb, seq, D_MODEL), BF16), jax.ShapeDtypeStruct((nb, seq, D_MODEL), BF16),
                   jax.ShapeDtypeStruct((nb, seq, 2 * D_MODEL), BF16), jax.ShapeDtypeStruct((8, D_MODEL), F32)],
        compiler_params=_params("arbitrary", "arbitrary"),
    )(dmerged, y_a, y_b, gm, gm, bgate)


def _ln_loss(x, mix, gp, pw, target, bgate, ln_g, ln_b):
    nb, seq, _ = x.shape

    def body(x_ref, mix_ref, gp_ref, pw_ref, t_ref, bg_ref, g_ref, b_ref, dx_ref, dp_ref, dpw_ref, dgp_ref, s_ref):
        @pl.when((pl.program_id(0) == 0) & (pl.program_id(1) == 0))
        def _():
            s_ref[...] = jnp.zeros_like(s_ref)

        sp = _sigmoid(gp_ref[...] + bg_ref[2:3, :])
        pw = pw_ref[...]
        pre = ALPHA * x_ref[...] + mix_ref[...] + sp * pw
        mu = jnp.mean(pre, -1, keepdims=True)
        cen = pre - mu
        rstd = lax.rsqrt(jnp.mean(cen * cen, -1, keepdims=True) + LN_EPS)
        xhat = cen * rstd
        err = xhat * g_ref[...] + b_ref[...] - t_ref[...]
        dy = err * (1.0 / D_MODEL)
        dxh = dy * g_ref[...]
        dpre = rstd * (dxh - jnp.mean(dxh, -1, keepdims=True) - xhat * jnp.mean(dxh * xhat, -1, keepdims=True))
        dx_ref[...] = ALPHA * dpre
        dp_ref[...] = dpre.astype(BF16)
        dpw_ref[...] = (dpre * sp).astype(BF16)
        dgp = dpre * pw * (sp * (1.0 - sp))
        dgp_ref[...] = dgp.astype(BF16)
        s_ref[0:1, :] += jnp.sum(dy * xhat, 0, keepdims=True)
        s_ref[1:2, :] += jnp.sum(dy, 0, keepdims=True)
        s_ref[2:3, :] += jnp.sum(dgp, 0, keepdims=True)
        s_ref[3:4, :] += jnp.sum(err * err, 0, keepdims=True)

    spec = pl.BlockSpec((None, EW_TM, D_MODEL), lambda b, i: (b, i, 0))
    small = pl.BlockSpec((8, D_MODEL), lambda b, i: (0, 0))
    row = pl.BlockSpec((1, D_MODEL), lambda b, i: (0, 0))
    return _pcall(
        body, name="ln_loss", grid=(nb, seq // EW_TM),
        in_specs=[spec] * 5 + [small, row, row], out_specs=[spec] * 4 + [small],
        out_shape=[jax.ShapeDtypeStruct((nb, seq, D_MODEL), F32)] + [jax.ShapeDtypeStruct((nb, seq, D_MODEL), BF16)] * 3
        + [jax.ShapeDtypeStruct((8, D_MODEL), F32)],
        compiler_params=_params("arbitrary", "arbitrary"),
    )(x, mix, gp, pw, target, bgate, ln_g, ln_b)


def _adamw(w, g, m, v, name):
    rows, cols = w.shape
    tr = _row_tile(rows, cols, 8, 5 << 19)
    c1 = 1.0 - ADAM_B1 ** ADAM_STEP
    c2 = 1.0 - ADAM_B2 ** ADAM_STEP

    def body(w_ref, g_ref, m_ref, v_ref, d_ref, nm_ref, nv_ref):
        gv = g_ref[...]
        nm = ADAM_B1 * m_ref[...] + (1.0 - ADAM_B1) * gv
        nv = ADAM_B2 * v_ref[...] + (1.0 - ADAM_B2) * (gv * gv)
        d_ref[...] = -ADAM_LR * ((nm / c1) / (jnp.sqrt(nv / c2) + ADAM_EPS) + ADAM_WD * w_ref[...])
        nm_ref[...] = nm
        nv_ref[...] = nv

    spec = pl.BlockSpec((tr, cols), lambda i: (i, 0))
    return _pcall(
        body, name=name, grid=(rows // tr,), in_specs=[spec] * 4, out_specs=[spec] * 3,
        out_shape=[jax.ShapeDtypeStruct(w.shape, F32)] * 3, compiler_params=_params("parallel"),
    )(w, g, m, v)


def _sum_rows(parts, out_dtype, name):
    rows, cols = parts[0].shape
    tr = rows
    for cand in range(16, rows, 16):
        if rows % cand == 0 and cand * cols * 4 <= (1 << 20):
            tr = cand
    n = len(parts)

    def body(*refs):
        acc = refs[0][...].astype(F32)
        for r in refs[1:n]:
            acc = acc + r[...].astype(F32)
        refs[n][...] = acc.astype(out_dtype)

    spec = pl.BlockSpec((tr, cols), lambda i: (i, 0))
    return _pcall(
        body, name=name, grid=(rows // tr,), in_specs=[spec] * n, out_specs=spec,
        out_shape=jax.ShapeDtypeStruct((rows, cols), out_dtype), compiler_params=_params("parallel"),
    )(*parts)


def _place():
    return lax.axis_index("x"), lax.axis_index("y"), lax.axis_index("c")


def _other_chips(x, y):
    return [(1 - x, y), (x, 1 - y), (1 - x, 1 - y)]


def _remote(src, dst, send_sem, recv_sem, to):
    return pltpu.make_async_remote_copy(src_ref=src, dst_ref=dst, send_sem=send_sem, recv_sem=recv_sem,
                                        device_id=to, device_id_type=MESH)


ANY = pl.BlockSpec(memory_space=pl.ANY)
DMA_CHUNK_BYTES = 512 * 1024


def _row_chunks(rows, row_bytes):
    per = max(16, DMA_CHUNK_BYTES // row_bytes // 16 * 16)
    return [(s, min(per, rows - s)) for s in range(0, rows, per)]


def _row_tile(rows, cols, align, limit=1 << 21):
    best = None
    for cand in range(align, rows + 1, align):
        if rows % cand == 0 and cand * cols * 4 <= limit:
            best = cand
    return best or rows


def _allgather_pieces(pieces):
    n = len(pieces)
    halves = [_row_chunks(p.shape[0] // 2, p.shape[1] * p.dtype.itemsize) for p in pieces]
    entries = [(a, q, s, m, j) for a in range(n) for q, (s, m) in enumerate(halves[a]) for j in range(3)]
    slot = {(a, q, j): k for k, (a, q, _, _, j) in enumerate(entries)}
    n_ici = len(entries)

    def body(*refs):
        ins, outs = refs[:n], refs[n:2 * n]
        send_sems, recv_sems = refs[2 * n:]
        x, y, c = _place()
        me = 2 * x + y
        sibling = (x, y, 1 - c)
        chips = _other_chips(x, y)

        def landed(a, s, m, j, core):
            half = ins[a].shape[0] // 2
            return outs[a].at[2 * chips[j][0] + chips[j][1], pl.ds(core * half + s, m)]

        sent = []
        for k, (a, q, s, m, j) in enumerate(entries):
            if j < 2:
                half = ins[a].shape[0] // 2
                cp = _remote(ins[a].at[pl.ds(c * half + s, m)], outs[a].at[me, pl.ds(c * half + s, m)],
                             send_sems.at[k], recv_sems.at[k], (*chips[j], c))
                cp.start()
                sent.append(cp)

        def pass_to_sibling(k, blk):
            fw = _remote(blk, blk, send_sems.at[n_ici + k], recv_sems.at[n_ici + k], sibling)
            fw.start()
            sent.append(fw)

        for k, (a, q, s, m, j) in enumerate(entries):
            if j < 2:
                blk = landed(a, s, m, j, c)
                _remote(blk, blk, send_sems.at[k], recv_sems.at[k], (*chips[j], c)).wait_recv()
                first = q < (len(halves[a]) + 1) // 2
                if (j == 0) == first:
                    on = slot[(a, q, 2)]
                    rl = _remote(blk, blk, send_sems.at[on], recv_sems.at[on], (*chips[1 - j], c))
                    rl.start()
                    sent.append(rl)
                pass_to_sibling(k, blk)
        for k, (a, q, s, m, j) in enumerate(entries):
            if j == 2:
                blk = landed(a, s, m, j, c)
                _remote(blk, blk, send_sems.at[k], recv_sems.at[k], (*chips[j], c)).wait_recv()
                pass_to_sibling(k, blk)
        for k, (a, q, s, m, j) in enumerate(entries):
            blk = landed(a, s, m, j, 1 - c)
            _remote(blk, blk, send_sems.at[n_ici + k], recv_sems.at[n_ici + k], sibling).wait_recv()
        for cp in sent:
            cp.wait_send()

    gathered = _pcall(
        body, name="allgather_weights", in_specs=[ANY] * n, out_specs=[ANY] * n,
        out_shape=[jax.ShapeDtypeStruct((4,) + p.shape, p.dtype) for p in pieces],
        scratch_shapes=[pltpu.SemaphoreType.DMA((2 * n_ici,)), pltpu.SemaphoreType.DMA((2 * n_ici,))],
        compiler_params=pltpu.CompilerParams(has_side_effects=True),
    )(*pieces)
    x, y, _ = _place()
    return [lax.dynamic_update_slice(g, p[None], (2 * x + y, 0, 0)) for g, p in zip(gathered, pieces)]


def _sibling_exchange(grads):
    n = len(grads)
    chunks = [_row_chunks(g.shape[1] // 2, g.shape[2] * g.dtype.itemsize) for g in grads]
    n_sem = 4 * sum(len(ch) for ch in chunks)

    def body(*refs):
        ins, gots = refs[:n], refs[n:2 * n]
        send_sems, recv_sems = refs[2 * n:]
        x, y, c = _place()
        sibling = (x, y, 1 - c)
        work = []
        for a in range(n):
            half = ins[a].shape[1] // 2
            for piece in range(4):
                for s, m in chunks[a]:
                    k = len(work)
                    cp = _remote(ins[a].at[piece, pl.ds((1 - c) * half + s, m)], gots[a].at[piece, pl.ds(s, m)],
                                 send_sems.at[k], recv_sems.at[k], sibling)
                    cp.start()
                    work.append(cp)
        for cp in work:
            cp.wait()

    return _pcall(
        body, name="grad_sibling_exchange", in_specs=[ANY] * n, out_specs=[ANY] * n,
        out_shape=[jax.ShapeDtypeStruct((4, g.shape[1] // 2, g.shape[2]), g.dtype) for g in grads],
        scratch_shapes=[pltpu.SemaphoreType.DMA((n_sem,)), pltpu.SemaphoreType.DMA((n_sem,))],
        compiler_params=pltpu.CompilerParams(has_side_effects=True),
    )(*grads)


def _sibling_gather(fulls):
    n = len(fulls)
    chunks = [_row_chunks(f.shape[0] // 2, f.shape[1] * f.dtype.itemsize) for f in fulls]
    n_sem = sum(len(ch) for ch in chunks)

    def body(*refs):
        outs = refs[n:2 * n]
        send_sems, recv_sems = refs[2 * n:]
        x, y, c = _place()
        sibling = (x, y, 1 - c)
        work = []
        for a in range(n):
            h = outs[a].shape[0] // 2
            for s, m in chunks[a]:
                k = len(work)
                mine = outs[a].at[pl.ds(c * h + s, m)]
                cp = _remote(mine, mine, send_sems.at[k], recv_sems.at[k], sibling)
                cp.start()
                work.append((a, s, m, cp))
        for k, (a, s, m, cp) in enumerate(work):
            h = outs[a].shape[0] // 2
            cp.wait_send()
            theirs = outs[a].at[pl.ds((1 - c) * h + s, m)]
            _remote(theirs, theirs, send_sems.at[k], recv_sems.at[k], sibling).wait_recv()

    return _pcall(
        body, name="grad_sibling_gather", in_specs=[ANY] * n, out_specs=[ANY] * n,
        out_shape=[jax.ShapeDtypeStruct(f.shape, f.dtype) for f in fulls],
        input_output_aliases={a: a for a in range(n)},
        scratch_shapes=[pltpu.SemaphoreType.DMA((n_sem,)), pltpu.SemaphoreType.DMA((n_sem,))],
        compiler_params=pltpu.CompilerParams(has_side_effects=True),
    )(*fulls)


def _pair_sum(grad, got, place, name):
    _, rows, cols = grad.shape
    half = rows // 2
    tr = _row_tile(half, cols, 16)

    def body(p_ref, a_ref, b_ref, o_ref):
        o_ref[...] = (a_ref[...].astype(F32) + b_ref[...].astype(F32)).astype(BF16)

    return _pcall(
        body, name=name,
        grid_spec=pltpu.PrefetchScalarGridSpec(
            num_scalar_prefetch=1, grid=(4, half // tr),
            in_specs=[pl.BlockSpec((None, tr, cols), lambda k, i, p: (k, p[1] * (half // tr) + i, 0)),
                      pl.BlockSpec((None, tr, cols), lambda k, i, p: (k, i, 0))],
            out_specs=pl.BlockSpec((None, tr, cols), lambda k, i, p: (k, i, 0))),
        out_shape=jax.ShapeDtypeStruct((4, half, cols), BF16),
        compiler_params=_params("parallel", "parallel"),
    )(place, grad, got)


def _chip_sum(sums, got, place, name):
    _, h, cols = sums.shape
    tr = _row_tile(h, cols, 16)

    def body(p_ref, own_ref, g0, g1, g2, o_ref):
        o_ref[...] = ((own_ref[...].astype(F32) + g0[...].astype(F32)) + g1[...].astype(F32)) + g2[...].astype(F32)

    gspec = lambda j: pl.BlockSpec((None, tr, cols), lambda i, p: (j, i, 0))
    return _pcall(
        body, name=name,
        grid_spec=pltpu.PrefetchScalarGridSpec(
            num_scalar_prefetch=1, grid=(h // tr,),
            in_specs=[pl.BlockSpec((None, tr, cols), lambda i, p: (p[0], i, 0)), gspec(0), gspec(1), gspec(2)],
            out_specs=pl.BlockSpec((tr, cols), lambda i, p: (p[1] * (h // tr) + i, 0))),
        out_shape=jax.ShapeDtypeStruct((2 * h, cols), F32),
        compiler_params=_params("parallel"),
    )(place, sums, got, got, got)


def _allgather8(buf, name):
    rows = buf.shape[0]

    def body(in_ref, out_ref, send_sems, recv_sems):
        x, y, c = _place()
        me = 4 * x + 2 * y + c
        out_ref[me] = in_ref[...]
        work = []
        for rel in range(1, 8):
            fx, fy, fc = (rel >> 2) & 1, (rel >> 1) & 1, rel & 1
            to = (x ^ fx, y ^ fy, c ^ fc)
            cp = _remote(in_ref, out_ref.at[me], send_sems.at[rel - 1], recv_sems.at[rel - 1], to)
            cp.start()
            work.append((cp, 4 * to[0] + 2 * to[1] + to[2]))
        for rel, (cp, frm) in enumerate(work):
            cp.wait_send()
            blk = out_ref.at[frm]
            _remote(blk, blk, send_sems.at[rel], recv_sems.at[rel], (x, y, c)).wait_recv()

    return _pcall(
        body, name=name, in_specs=[pl.BlockSpec(memory_space=pltpu.VMEM)],
        out_specs=pl.BlockSpec(memory_space=pltpu.VMEM),
        out_shape=jax.ShapeDtypeStruct((8, rows, LANE), F32),
        scratch_shapes=[pltpu.SemaphoreType.DMA((7,)), pltpu.SemaphoreType.DMA((7,))],
        compiler_params=pltpu.CompilerParams(has_side_effects=True),
    )(buf)


def _pack_rows(arrs):
    parts = []
    for a in arrs:
        f = a.reshape(-1).astype(F32)
        parts.append(jnp.pad(f, (0, (-f.shape[0]) % LANE)))
    flat = jnp.concatenate(parts)
    rows = -(-flat.shape[0] // LANE)
    rows8 = -(-rows // 8) * 8
    return jnp.pad(flat, (0, rows8 * LANE - flat.shape[0])).reshape(rows8, LANE)


def _unpack_rows(buf, shapes):
    flat = buf.reshape(-1)
    outs, off = [], 0
    for s in shapes:
        n = int(np.prod(s))
        outs.append(flat[off:off + n].reshape(s))
        off += -(-n // LANE) * LANE
    return outs


def _local_grads(x, p, target, wseg, w_br16, w_out16, w_ple16, b_gate, conv_w, conv_b, dt_bias, a_log, d_skip,
                 ssm_norm_w, ln_g, ln_b, rel_bias, finish_dx):
    nb, seq, _ = x.shape
    bmaps = jnp.asarray(_bucket_maps())
    bias = _bias_tables(rel_bias, bmaps)
    bgate8 = jnp.pad(b_gate, ((0, 5), (0, 0)))
    dils = [d for _, d in PATTERNS]

    x16p = _token_orders(x, dils[1:])
    x16 = x16p[0]
    p16 = p.astype(BF16)
    qkv = [_proj(x16p[g], [wseg["qkv%d" % g]], BF16, "proj_qkv%d" % g, True, 2 * MM_TM)[0].reshape(
        nb, dils[g], seq // dils[g], -1) for g in range(3)]
    nat = {}
    for gi, (group, tm) in enumerate(NAT_GROUPS):
        outs = _proj(x16, [wseg[s] for s in group], F32, "proj_nat%d" % gi, True, tm)
        nat.update(zip(group, outs))
    att = [_attn_fwd(qkv[g], bias, g, dils[g], "attn_fwd%d" % g) for g in range(3)]
    oa, o_att, lse = _combine_fwd(att[0][0], att[0][1], att[1:], nat["gatt"])

    conv_wg, conv_bg = _xbc_group_order(conv_w), _xbc_group_order(conv_b)
    act = _conv_fwd(nat["xbc"], conv_wg, conv_bg, "conv_fwd")
    dt_sp, dt_sg = _softplus_sig(nat["dt"], jnp.pad(dt_bias, ((0, 0), (0, LANE - SSM_HEADS))))
    dtg, sgg = _group_lanes(dt_sp), _group_lanes(dt_sg)
    alog_g, dskip_g = _group_lanes(a_log), _group_lanes(d_skip)
    y_ssm, y_all, sprev = _ssd_fwd(act, dtg, nat["z"], alog_g, dskip_g, ssm_norm_w)

    w_bra, w_brb = w_br16[:ATT_OUT], w_br16[ATT_OUT:]
    y_a, = _proj(oa, [w_bra], F32, "proj_ya")
    y_b, = _proj(y_ssm, [w_brb], F32, "proj_yb")
    merged = _merge_fwd(y_a, y_b, nat["gm"], bgate8)
    mix, = _proj(merged, [w_out16], F32, "proj_mix")
    pw, = _proj(p16, [w_ple16], F32, "proj_ple")

    dx, dpre16, dpw16, dgp16, ln_sums = _ln_loss(x, mix, nat["gp"], pw, target, bgate8, ln_g, ln_b)
    loss_sum = (0.5 / D_MODEL) * jnp.sum(ln_sums[3])
    dmerged = _dx([dpre16], [w_out16], [], "dx_merged")
    dya16, dyb16, dgm16, mg_sums = _merge_bwd(dmerged, y_a, y_b, nat["gm"], bgate8)
    doa = _dx([dya16], [w_bra], [], "dx_oa")
    dys = _dx([dyb16], [w_brb], [], "dx_yssm")
    g_w_out, = _dw(merged, [dpre16], BF16, "dw_out")
    g_w_br = jnp.concatenate([_dw(oa, [dya16], BF16, "dw_bra")[0], _dw(y_ssm, [dyb16], BF16, "dw_brb")[0]], axis=0)
    g_w_ple, = _dw(p16, [dpw16], BF16, "dw_ple")

    do_att, dgatt16, own_order = _combine_bwd(doa, nat["gatt"], o_att, lse, dils[1:])
    dseg = {"gatt": dgatt16, "gm": dgm16, "gp": dgp16}
    dbias = []
    for g in range(3):
        cotangent = (do_att, o_att, lse) if g == 0 else (own_order[2 * g - 2], own_order[2 * g - 1])
        dqkv, db = _attn_bwd(qkv[g], bias, g, cotangent, dils[g],
                             "attn_bwd%d" % g)
        dseg["qkv%d" % g] = dqkv.reshape(nb, seq, -1)
        dbias.append(db)
    g_rel = _bias_grad(jnp.concatenate(dbias, axis=0), bmaps)[:, 0, :NUM_BUCKETS].T

    dact, ddtg, dz, ssd_small, g_normw = _ssd_bwd(
        act, dtg, sgg, nat["z"], y_all, dys, sprev, alog_g, dskip_g, ssm_norm_w)
    dseg["z"] = dz
    dseg["dt"] = jnp.pad(_ungroup_lanes(ddtg), ((0, 0), (0, 0), (0, LANE - SSM_HEADS)))
    dpre, conv_sums = _conv_bwd_pre(dact, nat["xbc"], conv_wg, conv_bg, "conv_bwd")
    dseg["xbc"] = _conv_bwd_x(dpre, conv_wg, "conv_bwd_x")
    csum = _xbc_reference_order(conv_sums)

    dx_own = [_dx([dseg["qkv%d" % g]], [wseg["qkv%d" % g]], [], "dx_qkv%d" % g, True).reshape(
        nb, dils[g], seq // dils[g], D_MODEL) for g in (1, 2)]
    dwseg = {"qkv%d" % g: _dw(x16p[g], [dseg["qkv%d" % g]], BF16, "dw_qkv%d" % g, True)[0] for g in range(3)}
    for gi, group in enumerate(DW_GROUPS):
        dwseg.update(zip(group, _dw(x16, [dseg[s] for s in group], BF16, "dw_nat%d" % gi, True)))
    names = ["qkv0"] + [s for group, _ in NAT_GROUPS for s in group]
    dx = finish_dx([dseg[s] for s in names], [wseg[s] for s in names], [dx], dx_own, dwseg, g_w_br, g_w_out, g_w_ple)

    small = dict(
        b_gate=jnp.stack([mg_sums[0], mg_sums[1], ln_sums[2]]),
        conv_w=csum[0:4], conv_b=csum[4:5],
        dt_bias=_ungroup_lanes(ssd_small[:, 2:3, :]), a_log=_ungroup_lanes(ssd_small[:, 0:1, :]),
        d_skip=_ungroup_lanes(ssd_small[:, 1:2, :]), ssm_norm_w=g_normw,
        ln_g=ln_sums[0:1], ln_b=ln_sums[1:2], rel_bias=g_rel)
    return loss_sum, dx, small


DX_TM = 256
SMALL_ORDER = ("b_gate", "conv_w", "conv_b", "dt_bias", "a_log", "d_skip", "ssm_norm_w", "ln_g", "ln_b", "rel_bias")
SMALL_FULL_SHAPES = dict(b_gate=(3, 1024), conv_w=(4, 3072), conv_b=(1, 3072), dt_bias=(1, 32), a_log=(1, 32),
                         d_skip=(1, 32), ssm_norm_w=(1, 2048), ln_g=(1, 1024), ln_b=(1, 1024), rel_bias=(32, 36))


def kernel(x, p, w_in, b_gate, conv_w, conv_b, dt_bias, a_log, d_skip, ssm_norm_w, w_branch, w_out, w_ple, ln_g, ln_b, rel_bias, loss_target, m_w_in, m_b_gate, m_conv_w, m_conv_b, m_dt_bias, m_a_log, m_d_skip, m_ssm_norm_w, m_w_branch, m_w_out, m_w_ple, m_ln_g, m_ln_b, m_rel_bias, v_w_in, v_b_gate, v_conv_w, v_conv_b, v_dt_bias, v_a_log, v_d_skip, v_ssm_norm_w, v_w_branch, v_w_out, v_w_ple, v_ln_g, v_ln_b, v_rel_bias):
    cx, cy, cc = _place()
    chip = 2 * cx + cy
    dev = 4 * cx + 2 * cy + cc

    w_in_t = jnp.transpose(w_in[0])
    win16 = _shard_to_window(w_in_t, chip)
    g_win, g_br, g_out, g_ple = _allgather_pieces(
        [win16, w_branch[0].astype(BF16), w_out[0].astype(BF16), w_ple[0].astype(BF16)])
    wseg = _assemble(g_win)
    w_br16 = g_br.reshape(4 * 704, D_MODEL)
    w_out16 = g_out.reshape(D_MODEL, D_MODEL)
    w_ple16 = jnp.transpose(g_ple, (1, 0, 2)).reshape(PLE_DIM, D_MODEL)
    shards = _allgather8(_pack_rows([b_gate[0], conv_w[0]]), "allgather_small_params")
    per_chip = [_unpack_rows(shards[2 * k], [(3, 256), (4, 768)]) for k in range(4)]
    b_gate_full = jnp.concatenate([pc[0] for pc in per_chip], axis=1)
    conv_w_full = jnp.concatenate([pc[1] for pc in per_chip], axis=1)

    place = jnp.stack([chip, cc]).astype(jnp.int32)
    reduced = []

    def finish_dx(dhs, ws, accs, own_order_accs, dwseg, d_br, d_out, d_ple):
        grads = [_pack(dwseg), d_br.reshape(4, 704, D_MODEL), d_out.reshape(4, 256, D_MODEL),
                 jnp.transpose(d_ple.reshape(PLE_DIM, 4, 256), (1, 0, 2))]
        got = _sibling_exchange(grads)
        chip_sums = [_pair_sum(g, t, place, "grad_pair_sum_%d" % i) for i, (g, t) in enumerate(zip(grads, got))]
        dx, others = _dx(dhs, ws, accs, "dx_w_in_and_grad_chip_scatter", True, DX_TM, chip_sums, own_order_accs)
        fulls = [_chip_sum(s, t, place, "grad_chip_sum_%d" % i) for i, (s, t) in enumerate(zip(chip_sums, others))]
        reduced.extend(_sibling_gather(fulls))
        return dx

    loss_sum, grad_x, small = _local_grads(
        x, p[0], loss_target, wseg, w_br16, w_out16, w_ple16, b_gate_full, conv_w_full, conv_b, dt_bias, a_log,
        d_skip, ssm_norm_w, ln_g, ln_b, rel_bias, finish_dx)
    big = reduced
    g_w_in = _window_to_shard(big[0], chip)
    g_w_branch, g_w_out, g_w_ple = big[1], big[2], big[3]
    parts = _allgather8(_pack_rows([small[n] for n in SMALL_ORDER] + [loss_sum.reshape(1, 1)]),
                        "allgather_small_grads")
    small_sum = _sum_rows([parts[i] for i in range(8)], F32, "small_grad_sum")
    *reduced_small, loss = _unpack_rows(small_sum, [SMALL_FULL_SHAPES[n] for n in SMALL_ORDER] + [(1, 1)])
    loss = loss.reshape(())
    sg = dict(zip(SMALL_ORDER, reduced_small))
    sg["b_gate"] = lax.dynamic_slice_in_dim(sg["b_gate"], chip * 256, 256, axis=1)
    sg["conv_w"] = lax.dynamic_slice_in_dim(sg["conv_w"], chip * 768, 768, axis=1)
    del dev

    upd = {}
    upd["w_in"] = [jnp.transpose(t) for t in _adamw(w_in_t, g_w_in, jnp.transpose(m_w_in[0]),
                                                      jnp.transpose(v_w_in[0]), "adamw_w_in")]
    upd["w_branch"] = _adamw(w_branch[0], g_w_branch, m_w_branch[0], v_w_branch[0], "adamw_w_branch")
    upd["w_out"] = _adamw(w_out[0], g_w_out, m_w_out[0], v_w_out[0], "adamw_w_out")
    upd["w_ple"] = _adamw(w_ple[0], g_w_ple, m_w_ple[0], v_w_ple[0], "adamw_w_ple")
    small_w = dict(b_gate=b_gate, conv_w=conv_w, conv_b=conv_b, dt_bias=dt_bias, a_log=a_log, d_skip=d_skip,
                   ssm_norm_w=ssm_norm_w, ln_g=ln_g, ln_b=ln_b, rel_bias=rel_bias)
    small_m = dict(b_gate=m_b_gate, conv_w=m_conv_w, conv_b=m_conv_b, dt_bias=m_dt_bias, a_log=m_a_log,
                   d_skip=m_d_skip, ssm_norm_w=m_ssm_norm_w, ln_g=m_ln_g, ln_b=m_ln_b, rel_bias=m_rel_bias)
    small_v = dict(b_gate=v_b_gate, conv_w=v_conv_w, conv_b=v_conv_b, dt_bias=v_dt_bias, a_log=v_a_log,
                   d_skip=v_d_skip, ssm_norm_w=v_ssm_norm_w, ln_g=v_ln_g, ln_b=v_ln_b, rel_bias=v_rel_bias)
    shapes = [small_w[n].shape for n in SMALL_ORDER]
    s_delta, s_m, s_v = _adamw(_pack_rows([small_w[n] for n in SMALL_ORDER]), _pack_rows([sg[n] for n in SMALL_ORDER]),
                               _pack_rows([small_m[n] for n in SMALL_ORDER]), _pack_rows([small_v[n] for n in SMALL_ORDER]),
                               "adamw_small")
    for i, n in enumerate(SMALL_ORDER):
        upd[n] = tuple(_unpack_rows(t, shapes)[i] for t in (s_delta, s_m, s_v))
        sg[n] = sg[n].reshape(small_w[n].shape)

    order = ("w_in", "b_gate", "conv_w", "conv_b", "dt_bias", "a_log", "d_skip", "ssm_norm_w", "w_branch", "w_out",
             "w_ple", "ln_g", "ln_b", "rel_bias")
    grads = dict(sg, w_in=jnp.transpose(g_w_in)[None],w_branch=g_w_branch[None], w_out=g_w_out[None], w_ple=g_w_ple[None])
    lead = lambda n, t: t[None] if n in ("w_in", "w_branch", "w_out", "w_ple") else t
    return (loss, grad_x, *[grads[n] for n in order], *[lead(n, upd[n][0]) for n in order],
            *[lead(n, upd[n][1]) for n in order], *[lead(n, upd[n][2]) for n in order])
```

```python
import math

import numpy as np
import jax
import jax.numpy as jnp
from jax import lax
from jax.experimental import pallas as pl
from jax.experimental.pallas import tpu as pltpu

F32, BF16 = jnp.float32, jnp.bfloat16

D_MODEL = 1024
HEAD_DIM = 64
GROUP_HEADS = 12
ATT_OUT = GROUP_HEADS * HEAD_DIM
PATTERNS = ((128, 1), (512, 4), (2048, 16))
BAND = 128
NUM_BUCKETS = 32
MAX_DISTANCE = 2048
D_INNER = 2048
SSM_HEADS = 32
SSM_GROUPS = 4
GROUP_SSM_HEADS = SSM_HEADS // SSM_GROUPS
D_STATE = 128
CHUNK = 128
PLE_DIM = 256
ALPHA = 2.0 ** 0.25
LN_EPS = 1e-5
RMS_EPS = 1e-5
ADAM_LR, ADAM_B1, ADAM_B2, ADAM_EPS, ADAM_WD, ADAM_STEP = 0.001, 0.9, 0.999, 1e-08, 0.01, 10
NEG = -1e30

QKV_W = 3 * ATT_OUT
IN_COLS = 15904
SHARD_COLS = IN_COLS // 4
DT_COL = 12800
ROW_TILE = 16
WIN_ROWS = 4000


def _win_offset(k):
    return (k * SHARD_COLS) % ROW_TILE


def _win_start(k):
    return k * SHARD_COLS - _win_offset(k)

VMEM_LIMIT_BYTES = 56 * 1024 * 1024
LANE = 128
MESH = pl.DeviceIdType.MESH
NT = (((1,), (1,)), ((), ()))
TN = (((0,), (0,)), ((), ()))


def _pcall(body, **kw):
    return pl.pallas_call(body, **kw)


def _params(*sem):
    return pltpu.CompilerParams(dimension_semantics=sem, vmem_limit_bytes=VMEM_LIMIT_BYTES)


def _sigmoid(v):
    return jax.nn.sigmoid(v)


MM_TM = 512


def _tok_spec(tm, width):
    return pl.BlockSpec((None, tm, width), lambda b, i: (b, i, 0))


def _whole(arr, single_buffer=False):
    mode = dict(pipeline_mode=pl.Buffered(1)) if single_buffer else {}
    return pl.BlockSpec(arr.shape, lambda b, i: (0,) * arr.ndim, **mode)


def _proj(a3, ws, out_dtype, name, w_rows_are_outputs=False, tm=MM_TM):
    nb, seq, kdim = a3.shape
    nw = len(ws)
    widths = [w.shape[0] if w_rows_are_outputs else w.shape[1] for w in ws]

    def body(*refs):
        a = refs[0][...].astype(BF16)
        for w_ref, o_ref in zip(refs[1:1 + nw], refs[1 + nw:]):
            if w_rows_are_outputs:
                v = lax.dot_general(a, w_ref[...], NT, preferred_element_type=F32)
            else:
                v = jnp.dot(a, w_ref[...], preferred_element_type=F32)
            o_ref[...] = v.astype(out_dtype)

    return _pcall(
        body, name=name, grid=(nb, seq // tm),
        in_specs=[_tok_spec(tm, kdim)] + [_whole(w, True) for w in ws],
        out_specs=[_tok_spec(tm, n) for n in widths],
        out_shape=[jax.ShapeDtypeStruct((nb, seq, n), out_dtype) for n in widths],
        compiler_params=_params("parallel", "parallel"),
    )(a3, *ws)


def _dx(dhs, ws, accs, name, w_rows_are_outputs=False, tm=MM_TM, scatter=None, own_order_accs=()):
    nb, seq, _ = dhs[0].shape
    nd, nacc, npa = len(dhs), len(accs), len(own_order_accs)
    kout = ws[0].shape[1] if w_rows_are_outputs else ws[0].shape[0]
    sums = scatter or []
    ns = len(sums)
    chunks = [_row_chunks(s.shape[1], s.shape[2] * s.dtype.itemsize, ICI_CHUNK_BYTES) for s in sums]
    n_sem = 3 * sum(len(ch) for ch in chunks)
    grid = (nb, seq // tm)
    ntile = kout // LANE if npa else 0

    def body(*refs):
        n_in = 2 * nd + nacc + npa
        sum_refs, o_ref, got_refs = refs[n_in:n_in + ns], refs[n_in + ns], refs[n_in + ns + 1:n_in + 2 * ns + 1]
        tile_refs = refs[n_in + 2 * ns + 1:n_in + 2 * ns + 1 + ntile]

        def copies():
            send_sems, recv_sems = refs[-2], refs[-1]
            x, y, c = _place()
            out = []
            for a in range(ns):
                for s, m in chunks[a]:
                    for j, (cx, cy) in enumerate(_other_chips(x, y)):
                        k = len(out)
                        out.append(_remote(sum_refs[a].at[2 * cx + cy, pl.ds(s, m)], got_refs[a].at[j, pl.ds(s, m)],
                                           send_sems.at[k], recv_sems.at[k], (cx, cy, c)))
            return out

        if ns:
            @pl.when((pl.program_id(0) == 0) & (pl.program_id(1) == 0))
            def _():
                for cp in copies():
                    cp.start()

        v = None
        for dh_ref, w_ref in zip(refs[:nd], refs[nd:2 * nd]):
            dh = dh_ref[...].astype(BF16)
            if w_rows_are_outputs:
                t = jnp.dot(dh, w_ref[...], preferred_element_type=F32)
            else:
                t = lax.dot_general(dh, w_ref[...], NT, preferred_element_type=F32)
            v = t if v is None else v + t
        for a_ref in refs[2 * nd:2 * nd + nacc]:
            v = v + a_ref[...]
        for p_ref in refs[2 * nd + nacc:n_in]:
            v = v + _natural_rows(p_ref, tile_refs)
        o_ref[...] = v

        if ns:
            @pl.when((pl.program_id(0) == grid[0] - 1) & (pl.program_id(1) == grid[1] - 1))
            def _():
                for cp in copies():
                    cp.wait()

    out = _pcall(
        body, name=name, grid=grid,
        in_specs=[_tok_spec(tm, dh.shape[-1]) for dh in dhs] + [_whole(w, True) for w in ws]
        + [_tok_spec(tm, kout)] * nacc
        + [pl.BlockSpec((None, p.shape[1], tm // p.shape[1], kout), lambda b, i: (b, 0, i, 0)) for p in own_order_accs]
        + [ANY] * ns,
        out_specs=[_tok_spec(tm, kout)] + [ANY] * ns,
        out_shape=[jax.ShapeDtypeStruct((nb, seq, kout), F32)]
        + [jax.ShapeDtypeStruct((3,) + s.shape[1:], s.dtype) for s in sums],
        input_output_aliases={2 * nd: 0} if nacc else {},
        scratch_shapes=[pltpu.VMEM((tm, LANE), F32)] * ntile
        + ([pltpu.SemaphoreType.DMA((n_sem,)), pltpu.SemaphoreType.DMA((n_sem,))] if ns else []),
        compiler_params=pltpu.CompilerParams(
            dimension_semantics=("arbitrary", "arbitrary") if ns else ("parallel", "parallel"),
            vmem_limit_bytes=VMEM_LIMIT_BYTES, has_side_effects=bool(ns)),
    )(*dhs, *ws, *accs, *own_order_accs, *sums)
    return (out[0], list(out[1:])) if ns else out[0]


def _dw(a3, dhs, out_dtype, name, rows_are_outputs=False):
    nb, seq, kdim = a3.shape
    nd = len(dhs)
    grid = (nb, seq // MM_TM)
    shapes = [(dh.shape[-1], kdim) if rows_are_outputs else (kdim, dh.shape[-1]) for dh in dhs]

    def body(*refs):
        b, i = pl.program_id(0), pl.program_id(1)
        dh_refs, o_refs, acc_refs = refs[1:1 + nd], refs[1 + nd:1 + 2 * nd], refs[1 + 2 * nd:]

        @pl.when((b == 0) & (i == 0))
        def _():
            for acc_ref in acc_refs:
                acc_ref[...] = jnp.zeros_like(acc_ref)

        a = refs[0][...].astype(BF16)
        for dh_ref, acc_ref in zip(dh_refs, acc_refs):
            dh = dh_ref[...].astype(BF16)
            acc_ref[...] += lax.dot_general(*((dh, a) if rows_are_outputs else (a, dh)), TN,
                                            preferred_element_type=F32)

        @pl.when((b == grid[0] - 1) & (i == grid[1] - 1))
        def _():
            for o_ref, acc_ref in zip(o_refs, acc_refs):
                o_ref[...] = acc_ref[...].astype(out_dtype)

    return _pcall(
        body, name=name, grid=grid,
        in_specs=[_tok_spec(MM_TM, kdim)] + [_tok_spec(MM_TM, dh.shape[-1]) for dh in dhs],
        out_specs=[pl.BlockSpec(s, lambda b, i: (0, 0)) for s in shapes],
        out_shape=[jax.ShapeDtypeStruct(s, out_dtype) for s in shapes],
        scratch_shapes=[pltpu.VMEM(s, F32) for s in shapes],
        compiler_params=_params("arbitrary", "arbitrary"),
    )(a3, *dhs)


def _qkv_rows(g):
    return [(part * QKV_W + g * ATT_OUT + hp * LANE, LANE) for hp in range(ATT_OUT // LANE) for part in range(3)]


XBC_START = 3 * QKV_W + ATT_OUT + D_INNER
GROUP_CH = GROUP_SSM_HEADS * HEAD_DIM
XBC_GROUP = GROUP_CH + 2 * D_STATE
CONV_DIM = SSM_GROUPS * XBC_GROUP


def _xbc_ranges():
    out = []
    for g in range(SSM_GROUPS):
        out += [(g * GROUP_CH, GROUP_CH), (D_INNER + g * D_STATE, D_STATE),
                (D_INNER + SSM_GROUPS * D_STATE + g * D_STATE, D_STATE)]
    return out


def _xbc_group_order(t):
    return jnp.concatenate([t[..., s:s + n] for s, n in _xbc_ranges()], axis=-1)


def _xbc_reference_order(t):
    g = lambda off, n: [t[..., k * XBC_GROUP + off:k * XBC_GROUP + off + n] for k in range(SSM_GROUPS)]
    return jnp.concatenate(g(0, GROUP_CH) + g(GROUP_CH, D_STATE) + g(GROUP_CH + D_STATE, D_STATE), axis=-1)


def _segments():
    one = lambda name, start, rows: (name, [(start, rows)], max(rows, LANE))
    return [("qkv%d" % g, _qkv_rows(g), QKV_W) for g in range(3)] + [
        one("gatt", 3 * QKV_W, ATT_OUT), one("z", 3 * QKV_W + ATT_OUT, D_INNER),
        ("xbc", [(XBC_START + s, n) for s, n in _xbc_ranges()], CONV_DIM), one("dt", DT_COL, SSM_HEADS),
        one("gm", DT_COL + SSM_HEADS, 2 * D_MODEL), one("gp", DT_COL + SSM_HEADS + 2 * D_MODEL, D_MODEL)]


LAYOUT_TC = 256
NAT_GROUPS = ((("gatt", "z", "dt", "gp"), 512), (("xbc", "gm"), 512))
DW_GROUPS = (("gatt", "z", "dt", "gp"), ("xbc",), ("gm",))


def _assemble(win):
    segs = _segments()

    def body(win_ref, *outs):
        def pieces(start, rows):
            t, end = start, start + rows
            while t < end:
                k = min(t // SHARD_COLS, 3)
                shard_end = (k + 1) * SHARD_COLS
                if k < 3 and shard_end % ROW_TILE and t == shard_end - shard_end % ROW_TILE:
                    lo = t - _win_start(k)
                    yield win_ref[k, lo:lo + ROW_TILE, :] + win_ref[k + 1, 0:ROW_TILE, :]
                    t += ROW_TILE
                    continue
                upto = min(end, shard_end - shard_end % ROW_TILE if k < 3 else end)
                yield win_ref[k, t - _win_start(k):upto - _win_start(k), :]
                t = upto

        for (_, ranges, total), o_ref in zip(segs, outs):
            off = 0
            for start, rows in ranges:
                for part in pieces(start, rows):
                    o_ref[off:off + part.shape[0], :] = part
                    off += part.shape[0]
            if off < total:
                o_ref[off:total, :] = jnp.zeros((total - off, o_ref.shape[1]), BF16)

    outs = _pcall(
        body, name="assemble_w_in", grid=(D_MODEL // LAYOUT_TC,),
        in_specs=[pl.BlockSpec((4, WIN_ROWS, LAYOUT_TC), lambda i: (0, 0, i))],
        out_specs=[pl.BlockSpec((total, LAYOUT_TC), lambda i: (0, i)) for _, _, total in segs],
        out_shape=[jax.ShapeDtypeStruct((total, D_MODEL), BF16) for _, _, total in segs],
        compiler_params=_params("parallel"),
    )(win)
    return {name: o for (name, _, _), o in zip(segs, outs)}


def _pack(dsegs):
    segs = _segments()

    def body(*refs):
        ins, o_ref = refs[:-1], refs[-1]
        tail = IN_COLS - _win_start(3)
        o_ref[3, tail:, :] = jnp.zeros((WIN_ROWS - tail, o_ref.shape[2]), BF16)
        for (_, ranges, _), s_ref in zip(segs, ins):
            off = 0
            for start, rows in ranges:
                for k in range(4):
                    lo = _win_start(k)
                    a, b = max(start, lo), min(start + rows, lo + WIN_ROWS)
                    if a < b:
                        o_ref[k, a - lo:b - lo, :] = s_ref[off + a - start:off + b - start, :]
                off += rows

    return _pcall(
        body, name="pack_dw_in", grid=(D_MODEL // LAYOUT_TC,),
        in_specs=[pl.BlockSpec((total, LAYOUT_TC), lambda i: (0, i)) for _, _, total in segs],
        out_specs=pl.BlockSpec((4, WIN_ROWS, LAYOUT_TC), lambda i: (0, 0, i)),
        out_shape=jax.ShapeDtypeStruct((4, WIN_ROWS, D_MODEL), BF16),
        compiler_params=_params("parallel"),
    )(*[dsegs[name] for name, _, _ in segs])


def _shard_to_window(shard_t, k):
    def at(off):
        return lambda w: jnp.pad(w.astype(BF16), ((off, WIN_ROWS - SHARD_COLS - off), (0, 0)))

    return lax.cond(k % 2 == 1, at(_win_offset(1)), at(_win_offset(0)), shard_t)


def _window_to_shard(win, k):
    return lax.dynamic_slice(win, ((k % 2) * _win_offset(1), 0), (SHARD_COLS, D_MODEL))


def _bucket_maps():
    qi = np.arange(8)[:, None]
    kj = np.arange(2 * BAND)[None, :]
    delta = qi + BAND - kj
    maps = []
    for window, dil in PATTERNS:
        valid = (delta >= 0) & (delta <= window // dil)
        dist = np.maximum(delta, 0) * dil
        max_exact = NUM_BUCKETS // 2
        d_f = np.maximum(dist, 1).astype(np.float32)
        large = max_exact + (np.log(d_f / np.float32(max_exact)) / np.float32(math.log(MAX_DISTANCE / max_exact))
                             * np.float32(NUM_BUCKETS - max_exact)).astype(np.int32)
        large = np.minimum(large, NUM_BUCKETS - 1)
        bucket = np.where(dist < max_exact, dist, large)
        maps.append(np.where(valid, bucket, -1).astype(np.int32))
    return np.stack(maps)


def _bias_tables(rel_bias, bmaps):
    def body(rb_ref, bm_ref, o_ref):
        g = pl.program_id(0)
        bm = bm_ref[...]
        for hh in range(GROUP_HEADS):
            acc = jnp.full(bm.shape, NEG, F32)
            for b in range(NUM_BUCKETS):
                acc = jnp.where(bm == b, rb_ref[b, g * GROUP_HEADS + hh], acc)
            for a in range(BAND // 8):
                o_ref[hh, 8 * a:8 * a + 8, :] = acc if a == 0 else pltpu.roll(acc, 8 * a, 1)

    return _pcall(
        body, name="bias_tables", grid=(3,),
        in_specs=[pl.BlockSpec(memory_space=pltpu.SMEM),
                  pl.BlockSpec((None, 8, 2 * BAND), lambda g: (g, 0, 0))],
        out_specs=pl.BlockSpec((GROUP_HEADS, BAND, 2 * BAND), lambda g: (g, 0, 0)),
        out_shape=jax.ShapeDtypeStruct((3 * GROUP_HEADS, BAND, 2 * BAND), F32),
        compiler_params=_params("parallel"),
    )(rel_bias, bmaps)


def _bias_grad(dbias, bmaps):
    def body(db_ref, bm_ref, o_ref):
        bm = bm_ref[...]
        lane = lax.broadcasted_iota(jnp.int32, (1, LANE), 1)
        for hh in range(GROUP_HEADS):
            db = db_ref[hh, 0:8, :]
            for a in range(1, BAND // 8):
                db = db + pltpu.roll(db_ref[hh, 8 * a:8 * a + 8, :], 2 * BAND - 8 * a, 1)
            vec = jnp.zeros((1, LANE), F32)
            for b in range(NUM_BUCKETS):
                s = jnp.sum(jnp.where(bm == b, db, 0.0), keepdims=True)
                vec = jnp.where(lane == b, s, vec)
            o_ref[hh] = vec

    return _pcall(
        body, name="bias_grad", grid=(3,),
        in_specs=[pl.BlockSpec((GROUP_HEADS, BAND, 2 * BAND), lambda g: (g, 0, 0)),
                  pl.BlockSpec((None, 8, 2 * BAND), lambda g: (g, 0, 0))],
        out_specs=pl.BlockSpec((GROUP_HEADS, 1, LANE), lambda g: (g, 0, 0)),
        out_shape=jax.ShapeDtypeStruct((3 * GROUP_HEADS, 1, LANE), F32),
        compiler_params=_params("parallel"),
    )(dbias, bmaps)


def _rows(n):
    if isinstance(n, int):
        return pl.ds(n * BAND, BAND)
    return pl.ds(pl.multiple_of(n * BAND, BAND), BAND)


def _for_blocks(blocks, nblk, per, carry):
    carry = blocks([0], carry, False)
    start = 1 + (nblk - 1) % per
    for n in range(1, start):
        carry = blocks([n], carry, True)
    trips = (nblk - start) // per
    if trips > 0:
        carry = lax.fori_loop(
            0, trips, lambda t, c: blocks([start + t * per + u for u in range(per)], c, True), carry)
    return carry


def _pairs_per_step(d):
    return {1: 3, 4: 6, 16: 6}[d]


def _bias_spec(group, hps):
    first = group * GROUP_HEADS // (2 * hps)
    return pl.BlockSpec((2 * hps, BAND, 2 * BAND), lambda hp, b, r: (first + hp, 0, 0))


def _attn_fwd(qkv4, bias, group, d, name):
    nb, _, sub, _ = qkv4.shape
    nblk = sub // BAND
    scale = HEAD_DIM ** -0.5
    npair = ATT_OUT // LANE
    hps = _pairs_per_step(d)
    compact = d > 1

    def body(qkv_ref, bias_ref, o_ref, l_ref):
        def blocks(ns, carry, with_prev):
            chains = [(bi, i, h) for bi in range(len(ns)) for i in range(hps) for h in range(2)]
            first_head = lax.broadcasted_iota(jnp.int32, (BAND, LANE), 1) < HEAD_DIM
            pair = lambda n, i, part: qkv_ref[_rows(n), (3 * i + part) * LANE:(3 * i + part + 1) * LANE]
            scores = []
            for bi, i, h in chains:
                n = ns[bi]
                qp = pair(n, i, 0) * scale
                q = jnp.where(first_head if h == 0 else jnp.logical_not(first_head), qp, jnp.zeros_like(qp))
                s_c = lax.dot_general(q, pair(n, i, 1), NT, preferred_element_type=F32) + bias_ref[2 * i + h, :, BAND:]
                s_p = None
                if with_prev:
                    s_p = lax.dot_general(q, pair(n - 1, i, 1), NT,
                                          preferred_element_type=F32) + bias_ref[2 * i + h, :, :BAND]
                scores.append((s_c, s_p))
            probs = []
            for s_c, s_p in scores:
                m = jnp.max(s_c, -1, keepdims=True)
                if with_prev:
                    m = jnp.maximum(m, jnp.max(s_p, -1, keepdims=True))
                e_c = jnp.exp(s_c - m)
                den = jnp.sum(e_c, -1, keepdims=True)
                e_p = None
                if with_prev:
                    e_p = jnp.exp(s_p - m)
                    den = den + jnp.sum(e_p, -1, keepdims=True)
                    e_p = e_p.astype(BF16)
                probs.append((e_c.astype(BF16), e_p, den, m))
            outs = {}
            for (bi, i, h), (e_c, e_p, den, m) in zip(chains, probs):
                n = ns[bi]
                acc = jnp.dot(e_c, pair(n, i, 2), preferred_element_type=F32)
                if with_prev:
                    acc = acc + jnp.dot(e_p, pair(n - 1, i, 2), preferred_element_type=F32)
                outs[(bi, i, h)] = (acc / den, m + jnp.log(den))
            lane = lax.broadcasted_iota(jnp.int32, (BAND, LANE), 1)
            for bi, n in enumerate(ns):
                per_head = jnp.zeros((BAND, LANE), F32)
                for i in range(hps):
                    o_ref[_rows(n), i * LANE:(i + 1) * LANE] = jnp.where(first_head, outs[(bi, i, 0)][0],
                                                                         outs[(bi, i, 1)][0])
                    if compact:
                        for h in range(2):
                            per_head = jnp.where(lane == 2 * i + h, outs[(bi, i, h)][1], per_head)
                    else:
                        l_ref[_rows(n), i * LANE:(i + 1) * LANE] = jnp.where(first_head, outs[(bi, i, 0)][1],
                                                                             outs[(bi, i, 1)][1])
                if compact:
                    l_ref[_rows(n), :] = per_head
            return carry

        _for_blocks(blocks, nblk, 2 if hps == 1 else 1, 0)

    in_specs = [pl.BlockSpec((None, None, sub, 3 * LANE * hps), lambda hp, b, r: (b, r, 0, hp)),
                _bias_spec(group, hps)]
    if compact:
        return _pcall(
            body, name=name, grid=(1, nb, d), in_specs=in_specs,
            out_specs=[pl.BlockSpec((None, None, sub, ATT_OUT), lambda hp, b, r: (b, r, 0, 0)),
                       pl.BlockSpec((None, None, sub, LANE), lambda hp, b, r: (b, r, 0, 0))],
            out_shape=[jax.ShapeDtypeStruct((nb, d, sub, ATT_OUT), F32), jax.ShapeDtypeStruct((nb, d, sub, LANE), F32)],
            compiler_params=_params("parallel", "parallel", "parallel"),
        )(qkv4, bias)
    ospec = pl.BlockSpec((None, sub, hps * LANE), lambda hp, b, r: (b, 0, r * (npair // hps) + hp))
    return _pcall(
        body, name=name, grid=(npair // hps, nb, d), in_specs=in_specs, out_specs=[ospec, ospec],
        out_shape=[jax.ShapeDtypeStruct((nb, sub, d * ATT_OUT), F32)] * 2,
        compiler_params=_params("parallel", "parallel", "parallel"),
    )(qkv4, bias)


STAT_LSE_LANE = 16


def _attn_bwd(qkv4, bias, group, cotangent, d, name):
    nb, _, sub, _ = qkv4.shape
    nblk = sub // BAND
    scale = HEAD_DIM ** -0.5
    npair = ATT_OUT // LANE
    hps = _pairs_per_step(d)
    compact = d > 1

    def body(qkv_ref, bias_ref, *rest):
        do_ref, dqkv_ref, db_ref = rest[0], rest[-2], rest[-1]
        b, r = pl.program_id(1), pl.program_id(2)

        @pl.when((b == 0) & (r == 0))
        def _():
            db_ref[...] = jnp.zeros_like(db_ref)

        def blocks(ns, carry, with_prev):
            sides = (0, 1) if with_prev else (0,)
            chains = [(bi, i, h, sd) for bi in range(len(ns)) for i in range(hps) for h in range(2) for sd in sides]
            first_head = lax.broadcasted_iota(jnp.int32, (BAND, LANE), 1) < HEAD_DIM
            own = lambda h, t: jnp.where(first_head if h == 0 else jnp.logical_not(first_head), t, jnp.zeros_like(t))
            pair = lambda rows, i, part: qkv_ref[rows, (3 * i + part) * LANE:(3 * i + part + 1) * LANE]
            key_rows = lambda bi, sd: _rows(ns[bi] - sd)
            qs = {}
            for bi in range(len(ns)):
                for i in range(hps):
                    q_pair = pair(_rows(ns[bi]), i, 0) * scale
                    do = do_ref[_rows(ns[bi]), i * LANE:(i + 1) * LANE]
                    do16 = do.astype(BF16)
                    for h in range(2):
                        if compact:
                            st_ref, head = rest[1], 2 * i + h
                            ebar = st_ref[_rows(ns[bi]), head:head + 1]
                            lcol = st_ref[_rows(ns[bi]), STAT_LSE_LANE + head:STAT_LSE_LANE + head + 1]
                        else:
                            ebar = jnp.sum(own(h, do * rest[1][_rows(ns[bi]), i * LANE:(i + 1) * LANE]), -1, keepdims=True)
                            lcol = rest[2][_rows(ns[bi]), i * LANE + h * HEAD_DIM:i * LANE + h * HEAD_DIM + 1]
                        qs[(bi, i, h)] = (own(h, q_pair), q_pair, own(h, do16), do16, ebar, lcol)
            raw = []
            for bi, i, h, sd in chains:
                q, _, do_h, _, _, _ = qs[(bi, i, h)]
                bias_blk = bias_ref[2 * i + h, :, :BAND] if sd else bias_ref[2 * i + h, :, BAND:]
                s = lax.dot_general(q, pair(key_rows(bi, sd), i, 1), NT, preferred_element_type=F32) + bias_blk
                dp = lax.dot_general(do_h, pair(key_rows(bi, sd), i, 2), NT, preferred_element_type=F32)
                raw.append((s, dp))
            soft = []
            for (bi, i, h, sd), (s, dp) in zip(chains, raw):
                ebar, lcol = qs[(bi, i, h)][4:]
                p = jnp.exp(s - lcol)
                ds = p * (dp - ebar)
                if sd:
                    db_ref[2 * i + h, :, :BAND] += ds
                else:
                    db_ref[2 * i + h, :, BAND:] += ds
                soft.append((p.astype(BF16), ds.astype(BF16)))
            grads = {}
            for (bi, i, h, sd), (p16, ds16) in zip(chains, soft):
                _, q_pair, _, do16 = qs[(bi, i, h)][:4]
                grads[(bi, i, h, sd)] = (
                    jnp.dot(ds16, pair(key_rows(bi, sd), i, 1), preferred_element_type=F32),
                    lax.dot_general(ds16, q_pair, TN, preferred_element_type=F32),
                    lax.dot_general(p16, do16, TN, preferred_element_type=F32))
            both = lambda bi, i, sd, which: jnp.where(first_head, grads[(bi, i, 0, sd)][which],
                                                      grads[(bi, i, 1, sd)][which])
            carry = list(carry) if carry is not None else None
            for bi, n in enumerate(ns):
                for i in range(hps):
                    base = 3 * LANE * i
                    dq = both(bi, i, 0, 0)
                    if with_prev:
                        dq = dq + both(bi, i, 1, 0)
                        dqkv_ref[_rows(n - 1), base + LANE:base + 2 * LANE] = (
                            carry[2 * i] + both(bi, i, 1, 1)).astype(BF16)
                        dqkv_ref[_rows(n - 1), base + 2 * LANE:base + 3 * LANE] = (
                            carry[2 * i + 1] + both(bi, i, 1, 2)).astype(BF16)
                    dqkv_ref[_rows(n), base:base + LANE] = (dq * scale).astype(BF16)
                carry = [t for i in range(hps) for t in (both(bi, i, 0, 1), both(bi, i, 0, 2))]
            return tuple(carry)

        carry = _for_blocks(blocks, nblk, 2 if hps == 1 else 1, None)
        for i in range(hps):
            base = 3 * LANE * i
            dqkv_ref[_rows(nblk - 1), base + LANE:base + 2 * LANE] = carry[2 * i].astype(BF16)
            dqkv_ref[_rows(nblk - 1), base + 2 * LANE:base + 3 * LANE] = carry[2 * i + 1].astype(BF16)

    qspec = pl.BlockSpec((None, None, sub, 3 * LANE * hps), lambda hp, b, r: (b, r, 0, hp))
    bspec = pl.BlockSpec((2 * hps, BAND, 2 * BAND), lambda hp, b, r: (hp, 0, 0))
    if compact:
        cspecs = [pl.BlockSpec((None, None, sub, ATT_OUT), lambda hp, b, r: (b, r, 0, 0)),
                  pl.BlockSpec((None, None, sub, LANE), lambda hp, b, r: (b, r, 0, 0))]
    else:
        cspecs = [pl.BlockSpec((None, sub, hps * LANE), lambda hp, b, r: (b, 0, r * (npair // hps) + hp))] * 3
    return _pcall(
        body, name=name, grid=(npair // hps, nb, d),
        in_specs=[qspec, _bias_spec(group, hps)] + cspecs, out_specs=[qspec, bspec],
        out_shape=[jax.ShapeDtypeStruct(qkv4.shape, BF16),
                   jax.ShapeDtypeStruct((GROUP_HEADS, BAND, 2 * BAND), F32)],
        compiler_params=_params("parallel", "arbitrary", "arbitrary"),
    )(qkv4, bias, *cotangent)


def _head_lanes(first_lane, one_channel):
    c = lax.broadcasted_iota(jnp.int32, (ATT_OUT, LANE), 0)
    lane = lax.broadcasted_iota(jnp.int32, (ATT_OUT, LANE), 1)
    hit = lane == first_lane + c // HEAD_DIM
    if one_channel:
        hit = hit & (c % HEAD_DIM == 0)
    return hit.astype(BF16)


def _exact_dot(v, m01, dims=None):
    parts = _split3(v)
    if dims is None:
        dot = lambda t: jnp.dot(t, m01, preferred_element_type=F32)
    else:
        dot = lambda t: lax.dot_general(t, m01, dims, preferred_element_type=F32)
    return (dot(parts[0]) + dot(parts[1])) + dot(parts[2])


def _store_own_order(value, tile_refs, out_ref):
    d, per, width = out_ref.shape
    for j in range(width // LANE):
        tile_refs[j][...] = value[:, j * LANE:(j + 1) * LANE]
    for r in range(d):
        rows = pl.ds(r, per, stride=d)
        for j in range(width // LANE):
            out_ref[r, :, j * LANE:(j + 1) * LANE] = tile_refs[j][rows, :].astype(out_ref.dtype)


def _token_orders(x, dilations):
    nb, seq, kdim = x.shape
    tm = 512

    def body(x_ref, nat_ref, *rest):
        outs, tile_refs = rest[:len(dilations)], rest[len(dilations):]
        xv = x_ref[...]
        nat_ref[...] = xv.astype(BF16)
        for o_ref in outs:
            _store_own_order(xv, tile_refs, o_ref)

    outs = _pcall(
        body, name="token_orders", grid=(nb, seq // tm), in_specs=[_tok_spec(tm, kdim)],
        out_specs=[_tok_spec(tm, kdim)]
        + [pl.BlockSpec((None, d, tm // d, kdim), lambda b, i: (b, 0, i, 0)) for d in dilations],
        out_shape=[jax.ShapeDtypeStruct((nb, seq, kdim), BF16)]
        + [jax.ShapeDtypeStruct((nb, d, seq // d, kdim), BF16) for d in dilations],
        scratch_shapes=[pltpu.VMEM((tm, LANE), F32)] * (kdim // LANE),
        compiler_params=_params("parallel", "parallel"),
    )(x)
    return [outs[0]] + [o.reshape(nb, seq, kdim) for o in outs[1:]]


def _natural_rows(p_ref, tile_refs):
    d, per, width = p_ref.shape
    for r in range(d):
        rows = pl.ds(r, per, stride=d)
        for j in range(width // LANE):
            tile_refs[j][rows, :] = p_ref[r, :, j * LANE:(j + 1) * LANE]
    return jnp.concatenate([tile_refs[j][...] for j in range(width // LANE)], axis=1)


def _combine_fwd(o0, l0, dilated, gatt):
    nb, seq, _ = gatt.shape
    tm = 512
    ntile = ATT_OUT // LANE

    def body(o0_ref, l0_ref, o1_ref, l1_ref, o2_ref, l2_ref, g_ref, oa_ref, oatt_ref, lse_ref, *tile_refs):
        spread = _head_lanes(0, False)
        l0v = l0_ref[...]
        l1v = _exact_dot(_natural_rows(l1_ref, tile_refs), spread, NT)
        l2v = _exact_dot(_natural_rows(l2_ref, tile_refs), spread, NT)
        m = jnp.maximum(jnp.maximum(l0v, l1v), l2v)
        tot = m + jnp.log(jnp.exp(l0v - m) + jnp.exp(l1v - m) + jnp.exp(l2v - m))
        o = jnp.exp(l0v - tot) * o0_ref[...]
        o = o + jnp.exp(l1v - tot) * _natural_rows(o1_ref, tile_refs)
        o = o + jnp.exp(l2v - tot) * _natural_rows(o2_ref, tile_refs)
        g = g_ref[...]
        oa_ref[...] = (o * (g * _sigmoid(g))).astype(BF16)
        oatt_ref[...] = o
        lse_ref[...] = tot

    spec = pl.BlockSpec((None, tm, ATT_OUT), lambda b, i: (b, i, 0))
    own = lambda t: pl.BlockSpec((None, t.shape[1], tm // t.shape[1], t.shape[3]), lambda b, i: (b, 0, i, 0))
    (o1, l1), (o2, l2) = dilated
    return _pcall(
        body, name="attn_combine", grid=(nb, seq // tm),
        in_specs=[spec, spec, own(o1), own(l1), own(o2), own(l2), spec], out_specs=[spec] * 3,
        out_shape=[jax.ShapeDtypeStruct((nb, seq, ATT_OUT), BF16), jax.ShapeDtypeStruct((nb, seq, ATT_OUT), F32),
                   jax.ShapeDtypeStruct((nb, seq, ATT_OUT), F32)],
        scratch_shapes=[pltpu.VMEM((tm, LANE), F32)] * ntile,
        compiler_params=_params("parallel", "parallel"),
    )(o0, l0, o1, l1, o2, l2, gatt)


def _combine_bwd(doa, gatt, o_att, lse, dilations):
    nb, seq, _ = gatt.shape
    tm = 512

    def body(doa_ref, g_ref, o_ref, l_ref, do_ref, dg_ref, *rest):
        ntile = ATT_OUT // LANE
        outs, tile_refs = rest[:-ntile], rest[-ntile:]
        g = g_ref[...]
        sg = _sigmoid(g)
        do = doa_ref[...] * (g * sg)
        do_ref[...] = do
        stats = (_exact_dot(do * o_ref[...], _head_lanes(0, False))
                 + _exact_dot(l_ref[...], _head_lanes(STAT_LSE_LANE, True)))
        dg_ref[...] = (doa_ref[...] * o_ref[...] * (sg * (1.0 + g * (1.0 - sg)))).astype(BF16)
        for k in range(len(dilations)):
            _store_own_order(do, tile_refs, outs[2 * k])
            _store_own_order(stats, tile_refs, outs[2 * k + 1])

    spec = pl.BlockSpec((None, tm, ATT_OUT), lambda b, i: (b, i, 0))
    own = lambda d, width: pl.BlockSpec((None, d, tm // d, width), lambda b, i: (b, 0, i, 0))
    outs = _pcall(
        body, name="attn_combine_bwd", grid=(nb, seq // tm), in_specs=[spec] * 4,
        out_specs=[spec, spec] + [own(d, w) for d in dilations for w in (ATT_OUT, LANE)],
        out_shape=[jax.ShapeDtypeStruct((nb, seq, ATT_OUT), F32), jax.ShapeDtypeStruct((nb, seq, ATT_OUT), BF16)]
        + [jax.ShapeDtypeStruct((nb, d, seq // d, w), t) for d in dilations for w, t in ((ATT_OUT, BF16), (LANE, F32))],
        scratch_shapes=[pltpu.VMEM((tm, LANE), F32)] * (ATT_OUT // LANE),
        compiler_params=_params("parallel", "parallel"),
    )(doa, gatt, o_att, lse)
    return outs[0], outs[1], outs[2:]


CONV_TM = 1024
CONV_TC = 1024


def _shift_down(cur, halo, k):
    rolled = pltpu.roll(cur, k, 0)
    hro = pltpu.roll(halo, k, 0)
    row = lax.broadcasted_iota(jnp.int32, hro.shape, 0)
    return jnp.concatenate([jnp.where(row < k, hro, rolled[:8]), rolled[8:]], axis=0)


def _shift_up(cur, halo, k):
    n = cur.shape[0]
    rolled = pltpu.roll(cur, n - k, 0)
    hro = pltpu.roll(halo, 8 - k, 0)
    row = lax.broadcasted_iota(jnp.int32, hro.shape, 0)
    return jnp.concatenate([rolled[:n - 8], jnp.where(row >= 8 - k, hro, rolled[n - 8:])], axis=0)


def _conv_pre(cur, halo, w_ref, b_ref):
    acc = cur * w_ref[3:4, :] + b_ref[...]
    for k in range(1, 4):
        acc = acc + _shift_down(cur, halo, k) * w_ref[3 - k:4 - k, :]
    return acc


def _conv_specs(seq):
    nblk = seq // CONV_TM
    cur = pl.BlockSpec((None, CONV_TM, CONV_TC), lambda cb, b, i: (b, i, cb))
    prev = pl.BlockSpec((None, 8, CONV_TC), lambda cb, b, i: (b, jnp.maximum(i * (CONV_TM // 8) - 1, 0), cb))
    nxt = pl.BlockSpec((None, 8, CONV_TC),
                       lambda cb, b, i: (b, jnp.minimum((i + 1) * (CONV_TM // 8), seq // 8 - 1), cb))
    wspec = pl.BlockSpec((4, CONV_TC), lambda cb, b, i: (0, cb))
    bspec = pl.BlockSpec((1, CONV_TC), lambda cb, b, i: (0, cb))
    return nblk, cur, prev, nxt, wspec, bspec


def _conv_fwd(xin, w4, bias, name):
    nb, seq, ch = xin.shape
    _, cur, prev, _, wspec, bspec = _conv_specs(seq)

    def body(x_ref, h_ref, w_ref, b_ref, o_ref):
        halo = jnp.where(pl.program_id(2) > 0, h_ref[...], 0.0)
        pre = _conv_pre(x_ref[...], halo, w_ref, b_ref)
        o_ref[...] = pre * _sigmoid(pre)

    return _pcall(
        body, name=name, grid=(ch // CONV_TC, nb, seq // CONV_TM),
        in_specs=[cur, prev, wspec, bspec], out_specs=cur,
        out_shape=jax.ShapeDtypeStruct(xin.shape, F32),
        compiler_params=_params("parallel", "parallel", "parallel"),
    )(xin, xin, w4, bias)


def _conv_bwd_pre(dact, xin, w4, bias, name):
    nb, seq, ch = xin.shape
    _, cur, prev, _, wspec, bspec = _conv_specs(seq)

    def body(da_ref, x_ref, h_ref, w_ref, b_ref, dp_ref, s_ref):
        b, i = pl.program_id(1), pl.program_id(2)

        @pl.when((b == 0) & (i == 0))
        def _():
            s_ref[...] = jnp.zeros_like(s_ref)

        halo = jnp.where(i > 0, h_ref[...], 0.0)
        x = x_ref[...]
        pre = _conv_pre(x, halo, w_ref, b_ref)
        sg = _sigmoid(pre)
        dpre = da_ref[...] * (sg * (1.0 + pre * (1.0 - sg)))
        dp_ref[...] = dpre
        s_ref[3:4, :] += jnp.sum(dpre * x, 0, keepdims=True)
        for k in range(1, 4):
            s_ref[3 - k:4 - k, :] += jnp.sum(dpre * _shift_down(x, halo, k), 0, keepdims=True)
        s_ref[4:5, :] += jnp.sum(dpre, 0, keepdims=True)

    return _pcall(
        body, name=name, grid=(ch // CONV_TC, nb, seq // CONV_TM),
        in_specs=[cur, cur, prev, wspec, bspec],
        out_specs=[cur, pl.BlockSpec((8, CONV_TC), lambda cb, b, i: (0, cb))],
        out_shape=[jax.ShapeDtypeStruct(xin.shape, F32), jax.ShapeDtypeStruct((8, ch), F32)],
        compiler_params=_params("parallel", "arbitrary", "arbitrary"),
    )(dact, xin, xin, w4, bias)


def _conv_bwd_x(dpre, w4, name):
    nb, seq, ch = dpre.shape
    nblk, cur, _, nxt, wspec, _ = _conv_specs(seq)

    def body(d_ref, n_ref, w_ref, o_ref):
        halo = jnp.where(pl.program_id(2) < nblk - 1, n_ref[...], 0.0)
        cur_v = d_ref[...]
        acc = cur_v * w_ref[3:4, :]
        for j in range(1, 4):
            acc = acc + _shift_up(cur_v, halo, j) * w_ref[3 - j:4 - j, :]
        o_ref[...] = acc.astype(BF16)

    return _pcall(
        body, name=name, grid=(ch // CONV_TC, nb, seq // CONV_TM),
        in_specs=[cur, nxt, wspec], out_specs=cur,
        out_shape=jax.ShapeDtypeStruct(dpre.shape, BF16),
        compiler_params=_params("parallel", "parallel", "parallel"),
    )(dpre, dpre, w4)


def _softplus_sig(dt_raw, dt_bias_row):
    nb, seq, _ = dt_raw.shape
    tm = 512

    def body(r_ref, b_ref, sp_ref, sg_ref):
        v = r_ref[...] + b_ref[...]
        sp_ref[...] = jnp.maximum(v, 0.0) + jnp.log1p(jnp.exp(-jnp.abs(v)))
        sg_ref[...] = _sigmoid(v)

    spec = pl.BlockSpec((None, tm, LANE), lambda b, i: (b, i, 0))
    return _pcall(
        body, name="dt_softplus", grid=(nb, seq // tm),
        in_specs=[spec, pl.BlockSpec((1, LANE), lambda b, i: (0, 0))], out_specs=[spec, spec],
        out_shape=[jax.ShapeDtypeStruct(dt_raw.shape, F32)] * 2,
        compiler_params=_params("parallel", "parallel"),
    )(dt_raw, dt_bias_row)


def _group_lanes(t):
    pads = [(0, 0)] * (t.ndim - 1) + [(0, LANE - GROUP_SSM_HEADS)]
    return jnp.stack([jnp.pad(t[..., GROUP_SSM_HEADS * g:GROUP_SSM_HEADS * (g + 1)], pads) for g in range(SSM_GROUPS)])


def _ungroup_lanes(t):
    return jnp.concatenate([t[g][..., :GROUP_SSM_HEADS] for g in range(SSM_GROUPS)], axis=-1)


def _decays(dt, al_ref):
    row = lax.broadcasted_iota(jnp.int32, (CHUNK, CHUNK), 0)
    col = lax.broadcasted_iota(jnp.int32, (CHUNK, CHUNK), 1)
    tril = (row >= col).astype(BF16)
    triu = (row <= col).astype(BF16)
    arow = -jnp.exp(al_ref[...])
    hi, mid, lo = _split3(dt * arow)
    down = lambda t: jnp.dot(tril, t, preferred_element_type=F32)
    across = lambda t: lax.dot_general(t, triu, TN, preferred_element_type=F32)
    acs = (down(hi) + down(mid)) + down(lo)
    acs_t = (across(hi) + across(mid)) + across(lo)
    return arow, acs, acs_t, row >= col, triu


STEP_CHUNKS = 8


def _ssd_specs(nb, seq):
    nc = seq // CHUNK
    hw = GROUP_SSM_HEADS * HEAD_DIM
    rows, steps = STEP_CHUNKS * CHUNK, nc // STEP_CHUNKS

    def mk(rev):
        cidx = (lambda c: steps - 1 - c) if rev else (lambda c: c)
        wide = pl.BlockSpec((None, rows, hw), lambda g, b, c: (b, cidx(c), g))
        xbc = pl.BlockSpec((None, rows, XBC_GROUP), lambda g, b, c: (b, cidx(c), g))
        lanes = pl.BlockSpec((None, None, rows, LANE), lambda g, b, c: (g, b, cidx(c), 0))
        prev = pl.BlockSpec((None, STEP_CHUNKS, None, D_STATE, hw), lambda g, b, c: (b, cidx(c), g, 0, 0))
        return wide, xbc, lanes, prev

    grow = pl.BlockSpec((None, 1, LANE), lambda g, b, c: (g, 0, 0))
    nwspec = pl.BlockSpec((1, hw), lambda g, b, c: (0, g))
    return nc, steps, hw, mk, grow, nwspec


def _head_expand():
    hw = GROUP_SSM_HEADS * HEAD_DIM
    r = lax.broadcasted_iota(jnp.int32, (LANE, hw), 0)
    c = lax.broadcasted_iota(jnp.int32, (LANE, hw), 1)
    return ((c // HEAD_DIM) == r).astype(BF16)


def _split3(v):
    hi = v.astype(BF16)
    rest = v - hi.astype(F32)
    mid = rest.astype(BF16)
    return hi, mid, (rest - mid.astype(F32)).astype(BF16)


def _to_channels(v, e):
    hi, mid, lo = _split3(v)
    dot = lambda t: jnp.dot(t, e, preferred_element_type=F32)
    return (dot(hi) + dot(mid)) + dot(lo)


def _to_heads(w, e):
    hi, mid, lo = _split3(w)
    dot = lambda t: lax.dot_general(t, e, (((1,), (1,)), ((), ())), preferred_element_type=F32)
    return (dot(hi) + dot(mid)) + dot(lo)


def _row8(v):
    return jnp.broadcast_to(v, (8, v.shape[1]))


def _ssd_chunk_setup(dt, al_ref, ds_ref):
    arow, acs, acs_t, causal, triu = _decays(dt, al_ref)
    e = _head_expand()
    dtx = _to_channels(dt, e)
    acsx = _to_channels(acs, e)
    lastx = acsx[CHUNK - 1:CHUNK, :]
    dskx = _to_channels(_row8(ds_ref[...]), e)[0:1, :]
    return arow, acs, acs_t, causal, triu, e, dtx, acsx, lastx, dskx


def _ssd_fwd(xbc, dtg, z, alog_g, dskip_g, normw):
    nb, seq, _ = xbc.shape
    nc, steps, hw, mk, grow, nwspec = _ssd_specs(nb, seq)
    wide, xbc_spec, lanes, prev = mk(False)
    tn = (((0,), (0,)), ((), ()))

    def body(xbc_ref, dt_ref, z_ref, al_ref, ds_ref, nw_ref, ys_ref, y_ref, sp_ref, st_ref):
        @pl.when(pl.program_id(2) == 0)
        def _():
            st_ref[...] = jnp.zeros_like(st_ref)

        for ci in range(STEP_CHUNKS):
            chunk(ci, xbc_ref, dt_ref, z_ref, al_ref, ds_ref, nw_ref, ys_ref, y_ref, sp_ref, st_ref)

    def chunk(ci, xbc_ref, dt_ref, z_ref, al_ref, ds_ref, nw_ref, ys_ref, y_ref, sp_ref, st_ref):
        rows = slice(ci * CHUNK, (ci + 1) * CHUNK)
        dt = dt_ref[rows, :]
        _, acs, acs_t, causal, _, _, dtx, acsx, lastx, dskx = _ssd_chunk_setup(dt, al_ref, ds_ref)
        bmat = xbc_ref[rows, GROUP_CH:GROUP_CH + D_STATE].astype(BF16)
        cmat = xbc_ref[rows, GROUP_CH + D_STATE:].astype(BF16)
        cb = lax.dot_general(cmat, bmat, (((1,), (1,)), ((), ())), preferred_element_type=F32)
        x = xbc_ref[rows, :GROUP_CH]
        xdt = x * dtx
        xdt16 = xdt.astype(BF16)
        first_head = lax.broadcasted_iota(jnp.int32, (CHUNK, LANE), 1) < HEAD_DIM
        pairs = []
        for hp in range(GROUP_SSM_HEADS // 2):
            xp = xdt16[:, hp * LANE:(hp + 1) * LANE]
            two = []
            for j in (2 * hp, 2 * hp + 1):
                lmat = jnp.exp(jnp.where(causal, acs[:, j:j + 1] - acs_t[j:j + 1, :], -jnp.inf))
                two.append(jnp.dot((cb * lmat).astype(BF16), xp, preferred_element_type=F32))
            pairs.append(jnp.where(first_head, two[0], two[1]))
        yd = jnp.concatenate(pairs, axis=1)
        s_prev = st_ref[...]
        s16 = s_prev.astype(BF16)
        sp_ref[ci] = s16
        yo = jnp.dot(cmat, s16, preferred_element_type=F32) * jnp.exp(acsx)
        sts = lax.dot_general(bmat, (xdt * jnp.exp(lastx - acsx)).astype(BF16), tn, preferred_element_type=F32)
        st_ref[...] = s_prev * jnp.exp(lastx) + sts
        y = yd + yo + dskx * x
        zz = z_ref[rows, :]
        u = y * (zz * _sigmoid(zz))
        rn = lax.rsqrt(jnp.mean(u * u, -1, keepdims=True) + RMS_EPS)
        ys_ref[rows, :] = (u * rn * nw_ref[...]).astype(BF16)
        y_ref[rows, :] = y

    return _pcall(
        body, name="ssd_fwd", grid=(SSM_GROUPS, nb, steps),
        in_specs=[xbc_spec, lanes, wide, grow, grow, nwspec],
        out_specs=[wide, wide, prev],
        out_shape=[jax.ShapeDtypeStruct((nb, seq, D_INNER), BF16), jax.ShapeDtypeStruct((nb, seq, D_INNER), F32),
                   jax.ShapeDtypeStruct((nb, nc, SSM_GROUPS, D_STATE, hw), BF16)],
        scratch_shapes=[pltpu.VMEM((D_STATE, hw), F32)],
        compiler_params=_params("parallel", "parallel", "arbitrary"),
    )(xbc, dtg, z, alog_g, dskip_g, normw)


def _ssd_bwd(xbc, dtg, sgg, z, y, dys, sprev, alog_g, dskip_g, normw):
    nb, seq, _ = xbc.shape
    nc, steps, hw, mk, grow, nwspec = _ssd_specs(nb, seq)
    wide, xbc_spec, lanes, prev = mk(True)
    nt = (((1,), (1,)), ((), ()))
    tn = (((0,), (0,)), ((), ()))

    def body(xbc_ref, dt_ref, sg_ref, z_ref, y_ref, dys_ref, sp_ref, al_ref, ds_ref, nw_ref,
             dxbc_ref, ddt_ref, dz_ref, small_ref, dnw_ref, g_ref):
        b, c = pl.program_id(1), pl.program_id(2)

        @pl.when((b == 0) & (c == 0))
        def _():
            small_ref[...] = jnp.zeros_like(small_ref)
            dnw_ref[...] = jnp.zeros_like(dnw_ref)

        @pl.when(c == 0)
        def _():
            g_ref[...] = jnp.zeros_like(g_ref)

        for ci in reversed(range(STEP_CHUNKS)):
            chunk(ci, xbc_ref, dt_ref, sg_ref, z_ref, y_ref, dys_ref, sp_ref, al_ref, ds_ref, nw_ref,
                  dxbc_ref, ddt_ref, dz_ref, small_ref, dnw_ref, g_ref)

    def chunk(ci, xbc_ref, dt_ref, sg_ref, z_ref, y_ref, dys_ref, sp_ref, al_ref, ds_ref, nw_ref,
              dxbc_ref, ddt_ref, dz_ref, small_ref, dnw_ref, g_ref):
        rows = slice(ci * CHUNK, (ci + 1) * CHUNK)
        yv, zz, dys_v, nw = y_ref[rows, :], z_ref[rows, :], dys_ref[rows, :], nw_ref[...]
        sz = _sigmoid(zz)
        silu = zz * sz
        u = yv * silu
        rn = lax.rsqrt(jnp.mean(u * u, -1, keepdims=True) + RMS_EPS)
        gn = dys_v * nw
        du = rn * gn - u * (rn * rn * rn) * jnp.mean(u * gn, -1, keepdims=True)
        dnw_ref[...] += jnp.sum(dys_v * u * rn, 0, keepdims=True)
        dy = du * silu
        dz_ref[rows, :] = du * yv * (sz * (1.0 + zz * (1.0 - sz)))

        dt = dt_ref[rows, :]
        arow, acs, acs_t, causal, triu, e, dtx, acsx, lastx, dskx = _ssd_chunk_setup(dt, al_ref, ds_ref)
        dfsx = jnp.exp(acsx)
        dtex = jnp.exp(lastx - acsx)
        bmat = xbc_ref[rows, GROUP_CH:GROUP_CH + D_STATE].astype(BF16)
        cmat = xbc_ref[rows, GROUP_CH + D_STATE:].astype(BF16)
        cb = lax.dot_general(cmat, bmat, nt, preferred_element_type=F32)
        x = xbc_ref[rows, :GROUP_CH]
        xdt = x * dtx
        xdt16 = xdt.astype(BF16)
        xdte = xdt * dtex
        dy16 = dy.astype(BF16)
        dyd = dy * dfsx
        dyd16 = dyd.astype(BF16)
        s16 = sp_ref[ci]
        g = g_ref[...]
        g16 = g.astype(BF16)
        cs = jnp.dot(cmat, s16, preferred_element_type=F32)
        dc_off = lax.dot_general(dyd16, s16, nt, preferred_element_type=F32)
        g_here = lax.dot_general(cmat, dyd16, tn, preferred_element_type=F32)
        bg = jnp.dot(bmat, g16, preferred_element_type=F32)
        db_st = lax.dot_general(xdte.astype(BF16), g16, nt, preferred_element_type=F32)
        ddte_w = bg * xdte
        dcd = _to_heads(_row8(jnp.sum(g * s16.astype(F32), 0, keepdims=True)), e)[0:1, :]
        lane = lax.broadcasted_iota(jnp.int32, (CHUNK, LANE), 1)
        first_head = lane < HEAD_DIM
        sub = lax.broadcasted_iota(jnp.int32, (CHUNK, LANE), 0)
        dacs = jnp.zeros((CHUNK, LANE), F32)
        colsums = jnp.zeros((CHUNK, LANE), F32)
        dcb = jnp.zeros((CHUNK, CHUNK), F32)
        pairs = []
        for hp in range(GROUP_SSM_HEADS // 2):
            xp = xdt16[:, hp * LANE:(hp + 1) * LANE]
            dyp = dy16[:, hp * LANE:(hp + 1) * LANE]
            two = []
            for idx, j in enumerate((2 * hp, 2 * hp + 1)):
                lmat = jnp.exp(jnp.where(causal, acs[:, j:j + 1] - acs_t[j:j + 1, :], -jnp.inf))
                mf = cb * lmat
                dy_h = jnp.where(first_head if idx == 0 else jnp.logical_not(first_head), dyp, jnp.zeros_like(dyp))
                dm = lax.dot_general(dy_h, xp, nt, preferred_element_type=F32)
                two.append(lax.dot_general(mf.astype(BF16), dyp, tn, preferred_element_type=F32))
                wmat = dm * mf
                dcb = dcb + dm * lmat
                dacs = jnp.where(lane == j, jnp.sum(wmat, -1, keepdims=True), dacs)
                colsums = jnp.where(sub == j, jnp.sum(wmat, 0, keepdims=True), colsums)
            pairs.append(jnp.where(first_head, two[0], two[1]))
        dxdt = bg * dtex + jnp.concatenate(pairs, axis=1)
        dacs = dacs - colsums.T + _to_heads(dyd * cs - ddte_w, e)
        cd_row = jnp.exp(acs[CHUNK - 1:CHUNK, :])
        tail = _to_heads(_row8(jnp.sum(ddte_w, 0, keepdims=True)), e)[0:1, :] + dcd * cd_row
        dacs = dacs + jnp.where(sub == CHUNK - 1, tail, 0.0)
        d_hi, d_mid, d_lo = _split3(dacs)
        up = lambda t: jnp.dot(triu, t, preferred_element_type=F32)
        da = (up(d_hi) + up(d_mid)) + up(d_lo)
        ddt_raw = (da * arow + _to_heads(dxdt * x, e)) * sg_ref[rows, :]
        ddt_ref[rows, :] = ddt_raw
        small_ref[0:1, :] += jnp.sum(da * dt, 0, keepdims=True) * arow
        small_ref[1:2, :] += _to_heads(_row8(jnp.sum(dy * x, 0, keepdims=True)), e)[0:1, :]
        small_ref[2:3, :] += jnp.sum(ddt_raw, 0, keepdims=True)
        dcb16 = dcb.astype(BF16)
        dxbc_ref[rows, GROUP_CH + D_STATE:] = dc_off + jnp.dot(dcb16, bmat, preferred_element_type=F32)
        dxbc_ref[rows, GROUP_CH:GROUP_CH + D_STATE] = db_st + lax.dot_general(dcb16, cmat, tn,
                                                                               preferred_element_type=F32)
        dxbc_ref[rows, :GROUP_CH] = dxdt * dtx + dskx * dy
        g_ref[...] = g * jnp.exp(lastx) + g_here

    return _pcall(
        body, name="ssd_bwd", grid=(SSM_GROUPS, nb, steps),
        in_specs=[xbc_spec, lanes, lanes, wide, wide, wide, prev, grow, grow, nwspec],
        out_specs=[xbc_spec, lanes, wide,
                   pl.BlockSpec((None, 8, LANE), lambda g, b, c: (g, 0, 0)), nwspec],
        out_shape=[jax.ShapeDtypeStruct((nb, seq, CONV_DIM), F32),
                   jax.ShapeDtypeStruct((SSM_GROUPS, nb, seq, LANE), F32),
                   jax.ShapeDtypeStruct((nb, seq, D_INNER), F32),
                   jax.ShapeDtypeStruct((SSM_GROUPS, 8, LANE), F32),
                   jax.ShapeDtypeStruct((1, D_INNER), F32)],
        scratch_shapes=[pltpu.VMEM((D_STATE, hw), F32)],
        compiler_params=_params("parallel", "arbitrary", "arbitrary"),
    )(xbc, dtg, sgg, z, y, dys, sprev, alog_g, dskip_g, normw)


EW_TM = 256


def _merge_fwd(y_a, y_b, gm, bgate):
    nb, seq, _ = y_a.shape

    def body(a_ref, b_ref, ga_ref, gb_ref, bg_ref, o_ref):
        sa = _sigmoid(ga_ref[...] + bg_ref[0:1, :])
        sb = _sigmoid(gb_ref[...] + bg_ref[1:2, :])
        o_ref[...] = (sa * a_ref[...] + sb * b_ref[...]).astype(BF16)

    spec = pl.BlockSpec((None, EW_TM, D_MODEL), lambda b, i: (b, i, 0))
    spec1 = pl.BlockSpec((None, EW_TM, D_MODEL), lambda b, i: (b, i, 1))
    return _pcall(
        body, name="merge_fwd", grid=(nb, seq // EW_TM),
        in_specs=[spec, spec, spec, spec1, pl.BlockSpec((8, D_MODEL), lambda b, i: (0, 0))], out_specs=spec,
        out_shape=jax.ShapeDtypeStruct((nb, seq, D_MODEL), BF16),
        compiler_params=_params("parallel", "parallel"),
    )(y_a, y_b, gm, gm, bgate)


def _merge_bwd(dmerged, y_a, y_b, gm, bgate):
    nb, seq, _ = y_a.shape

    def body(dm_ref, a_ref, b_ref, ga_ref, gb_ref, bg_ref, dya_ref, dyb_ref, dg_ref, s_ref):
        @pl.when((pl.program_id(0) == 0) & (pl.program_id(1) == 0))
        def _():
            s_ref[...] = jnp.zeros_like(s_ref)

        dm = dm_ref[...]
        sa = _sigmoid(ga_ref[...] + bg_ref[0:1, :])
        sb = _sigmoid(gb_ref[...] + bg_ref[1:2, :])
        dya_ref[...] = (dm * sa).astype(BF16)
        dyb_ref[...] = (dm * sb).astype(BF16)
        dga = dm * a_ref[...] * (sa * (1.0 - sa))
        dgb = dm * b_ref[...] * (sb * (1.0 - sb))
        dg_ref[:, :D_MODEL] = dga.astype(BF16)
        dg_ref[:, D_MODEL:] = dgb.astype(BF16)
        s_ref[0:1, :] += jnp.sum(dga, 0, keepdims=True)
        s_ref[1:2, :] += jnp.sum(dgb, 0, keepdims=True)

    spec = pl.BlockSpec((None, EW_TM, D_MODEL), lambda b, i: (b, i, 0))
    spec1 = pl.BlockSpec((None, EW_TM, D_MODEL), lambda b, i: (b, i, 1))
    small = pl.BlockSpec((8, D_MODEL), lambda b, i: (0, 0))
    return _pcall(
        body, name="merge_bwd", grid=(nb, seq // EW_TM),
        in_specs=[spec, spec, spec, spec, spec1, small],
        out_specs=[spec, spec, pl.BlockSpec((None, EW_TM, 2 * D_MODEL), lambda b, i: (b, i, 0)), small],
        out_shape=[jax.ShapeDtypeStruct((nb, seq, D_MODEL), BF16), jax.ShapeDtypeStruct((nb, seq, D_MODEL), BF16),
                   jax.ShapeDtypeStruct((nb, seq, 2 * D_MODEL), BF16), jax.ShapeDtypeStruct((8, D_MODEL), F32)],
        compiler_params=_params("arbitrary", "arbitrary"),
    )(dmerged, y_a, y_b, gm, gm, bgate)


def _ln_loss(x, mix, gp, pw, target, bgate, ln_g, ln_b):
    nb, seq, _ = x.shape

    def body(x_ref, mix_ref, gp_ref, pw_ref, t_ref, bg_ref, g_ref, b_ref, dx_ref, dp_ref, dpw_ref, dgp_ref, s_ref):
        @pl.when((pl.program_id(0) == 0) & (pl.program_id(1) == 0))
        def _():
            s_ref[...] = jnp.zeros_like(s_ref)

        sp = _sigmoid(gp_ref[...] + bg_ref[2:3, :])
        pw = pw_ref[...]
        pre = ALPHA * x_ref[...] + mix_ref[...] + sp * pw
        mu = jnp.mean(pre, -1, keepdims=True)
        cen = pre - mu
        rstd = lax.rsqrt(jnp.mean(cen * cen, -1, keepdims=True) + LN_EPS)
        xhat = cen * rstd
        err = xhat * g_ref[...] + b_ref[...] - t_ref[...]
        dy = err * (1.0 / D_MODEL)
        dxh = dy * g_ref[...]
        dpre = rstd * (dxh - jnp.mean(dxh, -1, keepdims=True) - xhat * jnp.mean(dxh * xhat, -1, keepdims=True))
        dx_ref[...] = ALPHA * dpre
        dp_ref[...] = dpre.astype(BF16)
        dpw_ref[...] = (dpre * sp).astype(BF16)
        dgp = dpre * pw * (sp * (1.0 - sp))
        dgp_ref[...] = dgp.astype(BF16)
        s_ref[0:1, :] += jnp.sum(dy * xhat, 0, keepdims=True)
        s_ref[1:2, :] += jnp.sum(dy, 0, keepdims=True)
        s_ref[2:3, :] += jnp.sum(dgp, 0, keepdims=True)
        s_ref[3:4, :] += jnp.sum(err * err, 0, keepdims=True)

    spec = pl.BlockSpec((None, EW_TM, D_MODEL), lambda b, i: (b, i, 0))
    small = pl.BlockSpec((8, D_MODEL), lambda b, i: (0, 0))
    row = pl.BlockSpec((1, D_MODEL), lambda b, i: (0, 0))
    return _pcall(
        body, name="ln_loss", grid=(nb, seq // EW_TM),
        in_specs=[spec] * 5 + [small, row, row], out_specs=[spec] * 4 + [small],
        out_shape=[jax.ShapeDtypeStruct((nb, seq, D_MODEL), F32)] + [jax.ShapeDtypeStruct((nb, seq, D_MODEL), BF16)] * 3
        + [jax.ShapeDtypeStruct((8, D_MODEL), F32)],
        compiler_params=_params("arbitrary", "arbitrary"),
    )(x, mix, gp, pw, target, bgate, ln_g, ln_b)


def _adamw(w, g, m, v, name):
    rows, cols = w.shape
    tr = _row_tile(rows, cols, 8, 5 << 19)
    c1 = 1.0 - ADAM_B1 ** ADAM_STEP
    c2 = 1.0 - ADAM_B2 ** ADAM_STEP

    def body(w_ref, g_ref, m_ref, v_ref, d_ref, nm_ref, nv_ref):
        gv = g_ref[...]
        nm = ADAM_B1 * m_ref[...] + (1.0 - ADAM_B1) * gv
        nv = ADAM_B2 * v_ref[...] + (1.0 - ADAM_B2) * (gv * gv)
        d_ref[...] = -ADAM_LR * ((nm / c1) / (jnp.sqrt(nv / c2) + ADAM_EPS) + ADAM_WD * w_ref[...])
        nm_ref[...] = nm
        nv_ref[...] = nv

    spec = pl.BlockSpec((tr, cols), lambda i: (i, 0))
    return _pcall(
        body, name=name, grid=(rows // tr,), in_specs=[spec] * 4, out_specs=[spec] * 3,
        out_shape=[jax.ShapeDtypeStruct(w.shape, F32)] * 3, compiler_params=_params("parallel"),
    )(w, g, m, v)


def _sum_rows(parts, out_dtype, name):
    rows, cols = parts[0].shape
    tr = rows
    for cand in range(16, rows, 16):
        if rows % cand == 0 and cand * cols * 4 <= (1 << 20):
            tr = cand
    n = len(parts)

    def body(*refs):
        acc = refs[0][...].astype(F32)
        for r in refs[1:n]:
            acc = acc + r[...].astype(F32)
        refs[n][...] = acc.astype(out_dtype)

    spec = pl.BlockSpec((tr, cols), lambda i: (i, 0))
    return _pcall(
        body, name=name, grid=(rows // tr,), in_specs=[spec] * n, out_specs=spec,
        out_shape=jax.ShapeDtypeStruct((rows, cols), out_dtype), compiler_params=_params("parallel"),
    )(*parts)


def _place():
    return lax.axis_index("x"), lax.axis_index("y"), lax.axis_index("c")


def _other_chips(x, y):
    return [(1 - x, y), (x, 1 - y), (1 - x, 1 - y)]


def _remote(src, dst, send_sem, recv_sem, to):
    return pltpu.make_async_remote_copy(src_ref=src, dst_ref=dst, send_sem=send_sem, recv_sem=recv_sem,
                                        device_id=to, device_id_type=MESH)


ANY = pl.BlockSpec(memory_space=pl.ANY)
D2D_CHUNK_BYTES = 512 * 1024
ICI_CHUNK_BYTES = 2 * 1024 * 1024


def _row_chunks(rows, row_bytes, chunk_bytes=D2D_CHUNK_BYTES):
    per = max(16, chunk_bytes // row_bytes // 16 * 16)
    return [(s, min(per, rows - s)) for s in range(0, rows, per)]


def _row_tile(rows, cols, align, limit=1 << 21):
    best = None
    for cand in range(align, rows + 1, align):
        if rows % cand == 0 and cand * cols * 4 <= limit:
            best = cand
    return best or rows


def _allgather_pieces(pieces):
    n = len(pieces)
    halves = [_row_chunks(p.shape[0] // 2, p.shape[1] * p.dtype.itemsize, ICI_CHUNK_BYTES) for p in pieces]
    entries = [(a, q, s, m, j) for a in range(n) for q, (s, m) in enumerate(halves[a]) for j in range(3)]
    slot = {(a, q, j): k for k, (a, q, _, _, j) in enumerate(entries)}
    n_ici = len(entries)

    def body(*refs):
        ins, outs = refs[:n], refs[n:2 * n]
        send_sems, recv_sems = refs[2 * n:]
        x, y, c = _place()
        me = 2 * x + y
        sibling = (x, y, 1 - c)
        chips = _other_chips(x, y)

        def landed(a, s, m, j, core):
            half = ins[a].shape[0] // 2
            return outs[a].at[2 * chips[j][0] + chips[j][1], pl.ds(core * half + s, m)]

        sent = []
        for k, (a, q, s, m, j) in enumerate(entries):
            if j < 2:
                half = ins[a].shape[0] // 2
                cp = _remote(ins[a].at[pl.ds(c * half + s, m)], outs[a].at[me, pl.ds(c * half + s, m)],
                             send_sems.at[k], recv_sems.at[k], (*chips[j], c))
                cp.start()
                sent.append(cp)

        def pass_to_sibling(k, blk):
            fw = _remote(blk, blk, send_sems.at[n_ici + k], recv_sems.at[n_ici + k], sibling)
            fw.start()
            sent.append(fw)

        for k, (a, q, s, m, j) in enumerate(entries):
            if j < 2:
                blk = landed(a, s, m, j, c)
                _remote(blk, blk, send_sems.at[k], recv_sems.at[k], (*chips[j], c)).wait_recv()
                first = q < (len(halves[a]) + 1) // 2
                if (j == 0) == first:
                    on = slot[(a, q, 2)]
                    rl = _remote(blk, blk, send_sems.at[on], recv_sems.at[on], (*chips[1 - j], c))
                    rl.start()
                    sent.append(rl)
                pass_to_sibling(k, blk)
        for k, (a, q, s, m, j) in enumerate(entries):
            if j == 2:
                blk = landed(a, s, m, j, c)
                _remote(blk, blk, send_sems.at[k], recv_sems.at[k], (*chips[j], c)).wait_recv()
                pass_to_sibling(k, blk)
        for k, (a, q, s, m, j) in enumerate(entries):
            blk = landed(a, s, m, j, 1 - c)
            _remote(blk, blk, send_sems.at[n_ici + k], recv_sems.at[n_ici + k], sibling).wait_recv()
        for cp in sent:
            cp.wait_send()

    gathered = _pcall(
        body, name="allgather_weights", in_specs=[ANY] * n, out_specs=[ANY] * n,
        out_shape=[jax.ShapeDtypeStruct((4,) + p.shape, p.dtype) for p in pieces],
        scratch_shapes=[pltpu.SemaphoreType.DMA((2 * n_ici,)), pltpu.SemaphoreType.DMA((2 * n_ici,))],
        compiler_params=pltpu.CompilerParams(has_side_effects=True),
    )(*pieces)
    x, y, _ = _place()
    return [lax.dynamic_update_slice(g, p[None], (2 * x + y, 0, 0)) for g, p in zip(gathered, pieces)]


def _sibling_exchange(grads):
    n = len(grads)
    chunks = [_row_chunks(g.shape[1] // 2, g.shape[2] * g.dtype.itemsize) for g in grads]
    n_sem = 4 * sum(len(ch) for ch in chunks)

    def body(*refs):
        ins, gots = refs[:n], refs[n:2 * n]
        send_sems, recv_sems = refs[2 * n:]
        x, y, c = _place()
        sibling = (x, y, 1 - c)
        work = []
        for a in range(n):
            half = ins[a].shape[1] // 2
            for piece in range(4):
                for s, m in chunks[a]:
                    k = len(work)
                    cp = _remote(ins[a].at[piece, pl.ds((1 - c) * half + s, m)], gots[a].at[piece, pl.ds(s, m)],
                                 send_sems.at[k], recv_sems.at[k], sibling)
                    cp.start()
                    work.append(cp)
        for cp in work:
            cp.wait()

    return _pcall(
        body, name="grad_sibling_exchange", in_specs=[ANY] * n, out_specs=[ANY] * n,
        out_shape=[jax.ShapeDtypeStruct((4, g.shape[1] // 2, g.shape[2]), g.dtype) for g in grads],
        scratch_shapes=[pltpu.SemaphoreType.DMA((n_sem,)), pltpu.SemaphoreType.DMA((n_sem,))],
        compiler_params=pltpu.CompilerParams(has_side_effects=True),
    )(*grads)


def _sibling_gather(fulls):
    n = len(fulls)
    chunks = [_row_chunks(f.shape[0] // 2, f.shape[1] * f.dtype.itemsize) for f in fulls]
    n_sem = sum(len(ch) for ch in chunks)

    def body(*refs):
        outs = refs[n:2 * n]
        send_sems, recv_sems = refs[2 * n:]
        x, y, c = _place()
        sibling = (x, y, 1 - c)
        work = []
        for a in range(n):
            h = outs[a].shape[0] // 2
            for s, m in chunks[a]:
                k = len(work)
                mine = outs[a].at[pl.ds(c * h + s, m)]
                cp = _remote(mine, mine, send_sems.at[k], recv_sems.at[k], sibling)
                cp.start()
                work.append((a, s, m, cp))
        for k, (a, s, m, cp) in enumerate(work):
            h = outs[a].shape[0] // 2
            cp.wait_send()
            theirs = outs[a].at[pl.ds((1 - c) * h + s, m)]
            _remote(theirs, theirs, send_sems.at[k], recv_sems.at[k], sibling).wait_recv()

    return _pcall(
        body, name="grad_sibling_gather", in_specs=[ANY] * n, out_specs=[ANY] * n,
        out_shape=[jax.ShapeDtypeStruct(f.shape, f.dtype) for f in fulls],
        input_output_aliases={a: a for a in range(n)},
        scratch_shapes=[pltpu.SemaphoreType.DMA((n_sem,)), pltpu.SemaphoreType.DMA((n_sem,))],
        compiler_params=pltpu.CompilerParams(has_side_effects=True),
    )(*fulls)


def _pair_sum(grad, got, place, name):
    _, rows, cols = grad.shape
    half = rows // 2
    tr = _row_tile(half, cols, 16)

    def body(p_ref, a_ref, b_ref, o_ref):
        o_ref[...] = (a_ref[...].astype(F32) + b_ref[...].astype(F32)).astype(BF16)

    return _pcall(
        body, name=name,
        grid_spec=pltpu.PrefetchScalarGridSpec(
            num_scalar_prefetch=1, grid=(4, half // tr),
            in_specs=[pl.BlockSpec((None, tr, cols), lambda k, i, p: (k, p[1] * (half // tr) + i, 0)),
                      pl.BlockSpec((None, tr, cols), lambda k, i, p: (k, i, 0))],
            out_specs=pl.BlockSpec((None, tr, cols), lambda k, i, p: (k, i, 0))),
        out_shape=jax.ShapeDtypeStruct((4, half, cols), BF16),
        compiler_params=_params("parallel", "parallel"),
    )(place, grad, got)


def _chip_sum(sums, got, place, name):
    _, h, cols = sums.shape
    tr = _row_tile(h, cols, 16)

    def body(p_ref, own_ref, g0, g1, g2, o_ref):
        o_ref[...] = ((own_ref[...].astype(F32) + g0[...].astype(F32)) + g1[...].astype(F32)) + g2[...].astype(F32)

    gspec = lambda j: pl.BlockSpec((None, tr, cols), lambda i, p: (j, i, 0))
    return _pcall(
        body, name=name,
        grid_spec=pltpu.PrefetchScalarGridSpec(
            num_scalar_prefetch=1, grid=(h // tr,),
            in_specs=[pl.BlockSpec((None, tr, cols), lambda i, p: (p[0], i, 0)), gspec(0), gspec(1), gspec(2)],
            out_specs=pl.BlockSpec((tr, cols), lambda i, p: (p[1] * (h // tr) + i, 0))),
        out_shape=jax.ShapeDtypeStruct((2 * h, cols), F32),
        compiler_params=_params("parallel"),
    )(place, sums, got, got, got)


def _allgather8(buf, name):
    rows = buf.shape[0]

    def body(in_ref, out_ref, send_sems, recv_sems):
        x, y, c = _place()
        me = 4 * x + 2 * y + c
        out_ref[me] = in_ref[...]
        work = []
        for rel in range(1, 8):
            fx, fy, fc = (rel >> 2) & 1, (rel >> 1) & 1, rel & 1
            to = (x ^ fx, y ^ fy, c ^ fc)
            cp = _remote(in_ref, out_ref.at[me], send_sems.at[rel - 1], recv_sems.at[rel - 1], to)
            cp.start()
            work.append((cp, 4 * to[0] + 2 * to[1] + to[2]))
        for rel, (cp, frm) in enumerate(work):
            cp.wait_send()
            blk = out_ref.at[frm]
            _remote(blk, blk, send_sems.at[rel], recv_sems.at[rel], (x, y, c)).wait_recv()

    return _pcall(
        body, name=name, in_specs=[pl.BlockSpec(memory_space=pltpu.VMEM)],
        out_specs=pl.BlockSpec(memory_space=pltpu.VMEM),
        out_shape=jax.ShapeDtypeStruct((8, rows, LANE), F32),
        scratch_shapes=[pltpu.SemaphoreType.DMA((7,)), pltpu.SemaphoreType.DMA((7,))],
        compiler_params=pltpu.CompilerParams(has_side_effects=True),
    )(buf)


def _pack_rows(arrs):
    parts = []
    for a in arrs:
        f = a.reshape(-1).astype(F32)
        parts.append(jnp.pad(f, (0, (-f.shape[0]) % LANE)))
    flat = jnp.concatenate(parts)
    rows = -(-flat.shape[0] // LANE)
    rows8 = -(-rows // 8) * 8
    return jnp.pad(flat, (0, rows8 * LANE - flat.shape[0])).reshape(rows8, LANE)


def _unpack_rows(buf, shapes):
    flat = buf.reshape(-1)
    outs, off = [], 0
    for s in shapes:
        n = int(np.prod(s))
        outs.append(flat[off:off + n].reshape(s))
        off += -(-n // LANE) * LANE
    return outs


def _local_grads(x, p, target, wseg, w_br16, w_out16, w_ple16, b_gate, conv_w, conv_b, dt_bias, a_log, d_skip,
                 ssm_norm_w, ln_g, ln_b, rel_bias, finish_dx):
    nb, seq, _ = x.shape
    bmaps = jnp.asarray(_bucket_maps())
    bias = _bias_tables(rel_bias, bmaps)
    bgate8 = jnp.pad(b_gate, ((0, 5), (0, 0)))
    dils = [d for _, d in PATTERNS]

    x16p = _token_orders(x, dils[1:])
    x16 = x16p[0]
    p16 = p.astype(BF16)
    qkv = [_proj(x16p[g], [wseg["qkv%d" % g]], BF16, "proj_qkv%d" % g, True, 2 * MM_TM)[0].reshape(
        nb, dils[g], seq // dils[g], -1) for g in range(3)]
    nat = {}
    for gi, (group, tm) in enumerate(NAT_GROUPS):
        outs = _proj(x16, [wseg[s] for s in group], F32, "proj_nat%d" % gi, True, tm)
        nat.update(zip(group, outs))
    att = [_attn_fwd(qkv[g], bias, g, dils[g], "attn_fwd%d" % g) for g in range(3)]
    oa, o_att, lse = _combine_fwd(att[0][0], att[0][1], att[1:], nat["gatt"])

    conv_wg, conv_bg = _xbc_group_order(conv_w), _xbc_group_order(conv_b)
    act = _conv_fwd(nat["xbc"], conv_wg, conv_bg, "conv_fwd")
    dt_sp, dt_sg = _softplus_sig(nat["dt"], jnp.pad(dt_bias, ((0, 0), (0, LANE - SSM_HEADS))))
    dtg, sgg = _group_lanes(dt_sp), _group_lanes(dt_sg)
    alog_g, dskip_g = _group_lanes(a_log), _group_lanes(d_skip)
    y_ssm, y_all, sprev = _ssd_fwd(act, dtg, nat["z"], alog_g, dskip_g, ssm_norm_w)

    w_bra, w_brb = w_br16[:ATT_OUT], w_br16[ATT_OUT:]
    y_a, = _proj(oa, [w_bra], F32, "proj_ya")
    y_b, = _proj(y_ssm, [w_brb], F32, "proj_yb")
    merged = _merge_fwd(y_a, y_b, nat["gm"], bgate8)
    mix, = _proj(merged, [w_out16], F32, "proj_mix")
    pw, = _proj(p16, [w_ple16], F32, "proj_ple")

    dx, dpre16, dpw16, dgp16, ln_sums = _ln_loss(x, mix, nat["gp"], pw, target, bgate8, ln_g, ln_b)
    loss_sum = (0.5 / D_MODEL) * jnp.sum(ln_sums[3])
    dmerged = _dx([dpre16], [w_out16], [], "dx_merged")
    dya16, dyb16, dgm16, mg_sums = _merge_bwd(dmerged, y_a, y_b, nat["gm"], bgate8)
    doa = _dx([dya16], [w_bra], [], "dx_oa")
    dys = _dx([dyb16], [w_brb], [], "dx_yssm")
    g_w_out, = _dw(merged, [dpre16], BF16, "dw_out")
    g_w_br = jnp.concatenate([_dw(oa, [dya16], BF16, "dw_bra")[0], _dw(y_ssm, [dyb16], BF16, "dw_brb")[0]], axis=0)
    g_w_ple, = _dw(p16, [dpw16], BF16, "dw_ple")

    do_att, dgatt16, own_order = _combine_bwd(doa, nat["gatt"], o_att, lse, dils[1:])
    dseg = {"gatt": dgatt16, "gm": dgm16, "gp": dgp16}
    dbias = []
    for g in range(3):
        cotangent = (do_att, o_att, lse) if g == 0 else (own_order[2 * g - 2], own_order[2 * g - 1])
        dqkv, db = _attn_bwd(qkv[g], bias, g, cotangent, dils[g],
                             "attn_bwd%d" % g)
        dseg["qkv%d" % g] = dqkv.reshape(nb, seq, -1)
        dbias.append(db)
    g_rel = _bias_grad(jnp.concatenate(dbias, axis=0), bmaps)[:, 0, :NUM_BUCKETS].T

    dact, ddtg, dz, ssd_small, g_normw = _ssd_bwd(
        act, dtg, sgg, nat["z"], y_all, dys, sprev, alog_g, dskip_g, ssm_norm_w)
    dseg["z"] = dz
    dseg["dt"] = jnp.pad(_ungroup_lanes(ddtg), ((0, 0), (0, 0), (0, LANE - SSM_HEADS)))
    dpre, conv_sums = _conv_bwd_pre(dact, nat["xbc"], conv_wg, conv_bg, "conv_bwd")
    dseg["xbc"] = _conv_bwd_x(dpre, conv_wg, "conv_bwd_x")
    csum = _xbc_reference_order(conv_sums)

    dx_own = [_dx([dseg["qkv%d" % g]], [wseg["qkv%d" % g]], [], "dx_qkv%d" % g, True).reshape(
        nb, dils[g], seq // dils[g], D_MODEL) for g in (1, 2)]
    dwseg = {"qkv%d" % g: _dw(x16p[g], [dseg["qkv%d" % g]], BF16, "dw_qkv%d" % g, True)[0] for g in range(3)}
    for gi, group in enumerate(DW_GROUPS):
        dwseg.update(zip(group, _dw(x16, [dseg[s] for s in group], BF16, "dw_nat%d" % gi, True)))
    names = ["qkv0"] + [s for group, _ in NAT_GROUPS for s in group]
    dx = finish_dx([dseg[s] for s in names], [wseg[s] for s in names], [dx], dx_own, dwseg, g_w_br, g_w_out, g_w_ple)

    small = dict(
        b_gate=jnp.stack([mg_sums[0], mg_sums[1], ln_sums[2]]),
        conv_w=csum[0:4], conv_b=csum[4:5],
        dt_bias=_ungroup_lanes(ssd_small[:, 2:3, :]), a_log=_ungroup_lanes(ssd_small[:, 0:1, :]),
        d_skip=_ungroup_lanes(ssd_small[:, 1:2, :]), ssm_norm_w=g_normw,
        ln_g=ln_sums[0:1], ln_b=ln_sums[1:2], rel_bias=g_rel)
    return loss_sum, dx, small


DX_TM = 256
SMALL_ORDER = ("b_gate", "conv_w", "conv_b", "dt_bias", "a_log", "d_skip", "ssm_norm_w", "ln_g", "ln_b", "rel_bias")
SMALL_FULL_SHAPES = dict(b_gate=(3, 1024), conv_w=(4, 3072), conv_b=(1, 3072), dt_bias=(1, 32), a_log=(1, 32),
                         d_skip=(1, 32), ssm_norm_w=(1, 2048), ln_g=(1, 1024), ln_b=(1, 1024), rel_bias=(32, 36))


def kernel(x, p, w_in, b_gate, conv_w, conv_b, dt_bias, a_log, d_skip, ssm_norm_w, w_branch, w_out, w_ple, ln_g, ln_b, rel_bias, loss_target, m_w_in, m_b_gate, m_conv_w, m_conv_b, m_dt_bias, m_a_log, m_d_skip, m_ssm_norm_w, m_w_branch, m_w_out, m_w_ple, m_ln_g, m_ln_b, m_rel_bias, v_w_in, v_b_gate, v_conv_w, v_conv_b, v_dt_bias, v_a_log, v_d_skip, v_ssm_norm_w, v_w_branch, v_w_out, v_w_ple, v_ln_g, v_ln_b, v_rel_bias):
    cx, cy, cc = _place()
    chip = 2 * cx + cy
    dev = 4 * cx + 2 * cy + cc

    w_in_t = jnp.transpose(w_in[0])
    win16 = _shard_to_window(w_in_t, chip)
    g_win, g_br, g_out, g_ple = _allgather_pieces(
        [win16, w_branch[0].astype(BF16), w_out[0].astype(BF16), w_ple[0].astype(BF16)])
    wseg = _assemble(g_win)
    w_br16 = g_br.reshape(4 * 704, D_MODEL)
    w_out16 = g_out.reshape(D_MODEL, D_MODEL)
    w_ple16 = jnp.transpose(g_ple, (1, 0, 2)).reshape(PLE_DIM, D_MODEL)
    shards = _allgather8(_pack_rows([b_gate[0], conv_w[0]]), "allgather_small_params")
    per_chip = [_unpack_rows(shards[2 * k], [(3, 256), (4, 768)]) for k in range(4)]
    b_gate_full = jnp.concatenate([pc[0] for pc in per_chip], axis=1)
    conv_w_full = jnp.concatenate([pc[1] for pc in per_chip], axis=1)

    place = jnp.stack([chip, cc]).astype(jnp.int32)
    reduced = []

    def finish_dx(dhs, ws, accs, own_order_accs, dwseg, d_br, d_out, d_ple):
        grads = [_pack(dwseg), d_br.reshape(4, 704, D_MODEL), d_out.reshape(4, 256, D_MODEL),
                 jnp.transpose(d_ple.reshape(PLE_DIM, 4, 256), (1, 0, 2))]
        got = _sibling_exchange(grads)
        chip_sums = [_pair_sum(g, t, place, "grad_pair_sum_%d" % i) for i, (g, t) in enumerate(zip(grads, got))]
        dx, others = _dx(dhs, ws, accs, "dx_w_in_and_grad_chip_scatter", True, DX_TM, chip_sums, own_order_accs)
        fulls = [_chip_sum(s, t, place, "grad_chip_sum_%d" % i) for i, (s, t) in enumerate(zip(chip_sums, others))]
        reduced.extend(_sibling_gather(fulls))
        return dx

    loss_sum, grad_x, small = _local_grads(
        x, p[0], loss_target, wseg, w_br16, w_out16, w_ple16, b_gate_full, conv_w_full, conv_b, dt_bias, a_log,
        d_skip, ssm_norm_w, ln_g, ln_b, rel_bias, finish_dx)
    big = reduced
    g_w_in = _window_to_shard(big[0], chip)
    g_w_branch, g_w_out, g_w_ple = big[1], big[2], big[3]
    parts = _allgather8(_pack_rows([small[n] for n in SMALL_ORDER] + [loss_sum.reshape(1, 1)]),
                        "allgather_small_grads")
    small_sum = _sum_rows([parts[i] for i in range(8)], F32, "small_grad_sum")
    *reduced_small, loss = _unpack_rows(small_sum, [SMALL_FULL_SHAPES[n] for n in SMALL_ORDER] + [(1, 1)])
    loss = loss.reshape(())
    sg = dict(zip(SMALL_ORDER, reduced_small))
    sg["b_gate"] = lax.dynamic_slice_in_dim(sg["b_gate"], chip * 256, 256, axis=1)
    sg["conv_w"] = lax.dynamic_slice_in_dim(sg["conv_w"], chip * 768, 768, axis=1)
    del dev

    upd = {}
    upd["w_in"] = [jnp.transpose(t) for t in _adamw(w_in_t, g_w_in, jnp.transpose(m_w_in[0]),
                                                      jnp.transpose(v_w_in[0]), "adamw_w_in")]
    upd["w_branch"] = _adamw(w_branch[0], g_w_branch, m_w_branch[0], v_w_branch[0], "adamw_w_branch")
    upd["w_out"] = _adamw(w_out[0], g_w_out, m_w_out[0], v_w_out[0], "adamw_w_out")
    upd["w_ple"] = _adamw(w_ple[0], g_w_ple, m_w_ple[0], v_w_ple[0], "adamw_w_ple")
    small_w = dict(b_gate=b_gate, conv_w=conv_w, conv_b=conv_b, dt_bias=dt_bias, a_log=a_log, d_skip=d_skip,
                   ssm_norm_w=ssm_norm_w, ln_g=ln_g, ln_b=ln_b, rel_bias=rel_bias)
    small_m = dict(b_gate=m_b_gate, conv_w=m_conv_w, conv_b=m_conv_b, dt_bias=m_dt_bias, a_log=m_a_log,
                   d_skip=m_d_skip, ssm_norm_w=m_ssm_norm_w, ln_g=m_ln_g, ln_b=m_ln_b, rel_bias=m_rel_bias)
    small_v = dict(b_gate=v_b_gate, conv_w=v_conv_w, conv_b=v_conv_b, dt_bias=v_dt_bias, a_log=v_a_log,
                   d_skip=v_d_skip, ssm_norm_w=v_ssm_norm_w, ln_g=v_ln_g, ln_b=v_ln_b, rel_bias=v_rel_bias)
    shapes = [small_w[n].shape for n in SMALL_ORDER]
    s_delta, s_m, s_v = _adamw(_pack_rows([small_w[n] for n in SMALL_ORDER]), _pack_rows([sg[n] for n in SMALL_ORDER]),
                               _pack_rows([small_m[n] for n in SMALL_ORDER]), _pack_rows([small_v[n] for n in SMALL_ORDER]),
                               "adamw_small")
    for i, n in enumerate(SMALL_ORDER):
        upd[n] = tuple(_unpack_rows(t, shapes)[i] for t in (s_delta, s_m, s_v))
        sg[n] = sg[n].reshape(small_w[n].shape)

    order = ("w_in", "b_gate", "conv_w", "conv_b", "dt_bias", "a_log", "d_skip", "ssm_norm_w", "w_branch", "w_out",
             "w_ple", "ln_g", "ln_b", "rel_bias")
    grads = dict(sg, w_in=jnp.transpose(g_w_in)[None],w_branch=g_w_branch[None], w_out=g_w_out[None], w_ple=g_w_ple[None])
    lead = lambda n, t: t[None] if n in ("w_in", "w_branch", "w_out", "w_ple") else t
    return (loss, grad_x, *[grads[n] for n in order], *[lead(n, upd[n][0]) for n in order],
            *[lead(n, upd[n][1]) for n in order], *[lead(n, upd[n][2]) for n in order])
```

```python
import math

import numpy as np
import jax
import jax.numpy as jnp
from jax import lax
from jax.experimental import pallas as pl
from jax.experimental.pallas import tpu as pltpu

F32, BF16 = jnp.float32, jnp.bfloat16

D_MODEL = 1024
HEAD_DIM = 64
GROUP_HEADS = 12
ATT_OUT = GROUP_HEADS * HEAD_DIM
PATTERNS = ((128, 1), (512, 4), (2048, 16))
BAND = 128
NUM_BUCKETS = 32
MAX_DISTANCE = 2048
D_INNER = 2048
SSM_HEADS = 32
SSM_GROUPS = 4
GROUP_SSM_HEADS = SSM_HEADS // SSM_GROUPS
D_STATE = 128
CHUNK = 128
PLE_DIM = 256
ALPHA = 2.0 ** 0.25
LN_EPS = 1e-5
RMS_EPS = 1e-5
ADAM_LR, ADAM_B1, ADAM_B2, ADAM_EPS, ADAM_WD, ADAM_STEP = 0.001, 0.9, 0.999, 1e-08, 0.01, 10
NEG = -1e30

QKV_W = 3 * ATT_OUT
IN_COLS = 15904
SHARD_COLS = IN_COLS // 4
DT_COL = 12800
ROW_TILE = 16
WIN_ROWS = 4000


def _win_offset(k):
    return (k * SHARD_COLS) % ROW_TILE


def _win_start(k):
    return k * SHARD_COLS - _win_offset(k)

VMEM_LIMIT_BYTES = 56 * 1024 * 1024
LANE = 128
MESH = pl.DeviceIdType.MESH
NT = (((1,), (1,)), ((), ()))
TN = (((0,), (0,)), ((), ()))


def _pcall(body, **kw):
    return pl.pallas_call(body, **kw)


def _params(*sem):
    return pltpu.CompilerParams(dimension_semantics=sem, vmem_limit_bytes=VMEM_LIMIT_BYTES)


def _sigmoid(v):
    return jax.nn.sigmoid(v)


MM_TM = 512


def _tok_spec(tm, width):
    return pl.BlockSpec((None, tm, width), lambda b, i: (b, i, 0))


def _whole(arr, single_buffer=False):
    mode = dict(pipeline_mode=pl.Buffered(1)) if single_buffer else {}
    return pl.BlockSpec(arr.shape, lambda b, i: (0,) * arr.ndim, **mode)


def _proj(a3, ws, out_dtype, name, w_rows_are_outputs=False, tm=MM_TM):
    nb, seq, kdim = a3.shape
    nw = len(ws)
    widths = [w.shape[0] if w_rows_are_outputs else w.shape[1] for w in ws]

    def body(*refs):
        a = refs[0][...].astype(BF16)
        for w_ref, o_ref in zip(refs[1:1 + nw], refs[1 + nw:]):
            if w_rows_are_outputs:
                v = lax.dot_general(a, w_ref[...], NT, preferred_element_type=F32)
            else:
                v = jnp.dot(a, w_ref[...], preferred_element_type=F32)
            o_ref[...] = v.astype(out_dtype)

    return _pcall(
        body, name=name, grid=(nb, seq // tm),
        in_specs=[_tok_spec(tm, kdim)] + [_whole(w, True) for w in ws],
        out_specs=[_tok_spec(tm, n) for n in widths],
        out_shape=[jax.ShapeDtypeStruct((nb, seq, n), out_dtype) for n in widths],
        compiler_params=_params("parallel", "parallel"),
    )(a3, *ws)


def _dx(dhs, ws, accs, name, w_rows_are_outputs=False, tm=MM_TM, scatter=None, own_order_accs=()):
    nb, seq, _ = dhs[0].shape
    nd, nacc, npa = len(dhs), len(accs), len(own_order_accs)
    kout = ws[0].shape[1] if w_rows_are_outputs else ws[0].shape[0]
    sums = scatter or []
    ns = len(sums)
    chunks = [_row_chunks(s.shape[1], s.shape[2] * s.dtype.itemsize, ICI_CHUNK_BYTES) for s in sums]
    n_sem = 3 * sum(len(ch) for ch in chunks)
    grid = (nb, seq // tm)
    ntile = kout // LANE if npa else 0

    def body(*refs):
        n_in = 2 * nd + nacc + npa
        sum_refs, o_ref, got_refs = refs[n_in:n_in + ns], refs[n_in + ns], refs[n_in + ns + 1:n_in + 2 * ns + 1]
        tile_refs = refs[n_in + 2 * ns + 1:n_in + 2 * ns + 1 + ntile]

        def copies():
            send_sems, recv_sems = refs[-2], refs[-1]
            x, y, c = _place()
            out = []
            for a in range(ns):
                for s, m in chunks[a]:
                    for j, (cx, cy) in enumerate(_other_chips(x, y)):
                        k = len(out)
                        out.append(_remote(sum_refs[a].at[2 * cx + cy, pl.ds(s, m)], got_refs[a].at[j, pl.ds(s, m)],
                                           send_sems.at[k], recv_sems.at[k], (cx, cy, c)))
            return out

        if ns:
            @pl.when((pl.program_id(0) == 0) & (pl.program_id(1) == 0))
            def _():
                for cp in copies():
                    cp.start()

        v = None
        for dh_ref, w_ref in zip(refs[:nd], refs[nd:2 * nd]):
            dh = dh_ref[...].astype(BF16)
            if w_rows_are_outputs:
                t = jnp.dot(dh, w_ref[...], preferred_element_type=F32)
            else:
                t = lax.dot_general(dh, w_ref[...], NT, preferred_element_type=F32)
            v = t if v is None else v + t
        for a_ref in refs[2 * nd:2 * nd + nacc]:
            v = v + a_ref[...]
        for p_ref in refs[2 * nd + nacc:n_in]:
            v = v + _natural_rows(p_ref, tile_refs)
        o_ref[...] = v

        if ns:
            @pl.when((pl.program_id(0) == grid[0] - 1) & (pl.program_id(1) == grid[1] - 1))
            def _():
                for cp in copies():
                    cp.wait()

    out = _pcall(
        body, name=name, grid=grid,
        in_specs=[_tok_spec(tm, dh.shape[-1]) for dh in dhs] + [_whole(w, True) for w in ws]
        + [_tok_spec(tm, kout)] * nacc
        + [pl.BlockSpec((None, p.shape[1], tm // p.shape[1], kout), lambda b, i: (b, 0, i, 0)) for p in own_order_accs]
        + [ANY] * ns,
        out_specs=[_tok_spec(tm, kout)] + [ANY] * ns,
        out_shape=[jax.ShapeDtypeStruct((nb, seq, kout), F32)]
        + [jax.ShapeDtypeStruct((3,) + s.shape[1:], s.dtype) for s in sums],
        input_output_aliases={2 * nd: 0} if nacc else {},
        scratch_shapes=[pltpu.VMEM((tm, LANE), F32)] * ntile
        + ([pltpu.SemaphoreType.DMA((n_sem,)), pltpu.SemaphoreType.DMA((n_sem,))] if ns else []),
        compiler_params=pltpu.CompilerParams(
            dimension_semantics=("arbitrary", "arbitrary") if ns else ("parallel", "parallel"),
            vmem_limit_bytes=VMEM_LIMIT_BYTES, has_side_effects=bool(ns)),
    )(*dhs, *ws, *accs, *own_order_accs, *sums)
    return (out[0], list(out[1:])) if ns else out[0]


def _dw(a3, dhs, out_dtype, name, rows_are_outputs=False):
    nb, seq, kdim = a3.shape
    nd = len(dhs)
    grid = (nb, seq // MM_TM)
    shapes = [(dh.shape[-1], kdim) if rows_are_outputs else (kdim, dh.shape[-1]) for dh in dhs]

    def body(*refs):
        b, i = pl.program_id(0), pl.program_id(1)
        dh_refs, o_refs, acc_refs = refs[1:1 + nd], refs[1 + nd:1 + 2 * nd], refs[1 + 2 * nd:]

        @pl.when((b == 0) & (i == 0))
        def _():
            for acc_ref in acc_refs:
                acc_ref[...] = jnp.zeros_like(acc_ref)

        a = refs[0][...].astype(BF16)
        for dh_ref, acc_ref in zip(dh_refs, acc_refs):
            dh = dh_ref[...].astype(BF16)
            acc_ref[...] += lax.dot_general(*((dh, a) if rows_are_outputs else (a, dh)), TN,
                                            preferred_element_type=F32)

        @pl.when((b == grid[0] - 1) & (i == grid[1] - 1))
        def _():
            for o_ref, acc_ref in zip(o_refs, acc_refs):
                o_ref[...] = acc_ref[...].astype(out_dtype)

    return _pcall(
        body, name=name, grid=grid,
        in_specs=[_tok_spec(MM_TM, kdim)] + [_tok_spec(MM_TM, dh.shape[-1]) for dh in dhs],
        out_specs=[pl.BlockSpec(s, lambda b, i: (0, 0)) for s in shapes],
        out_shape=[jax.ShapeDtypeStruct(s, out_dtype) for s in shapes],
        scratch_shapes=[pltpu.VMEM(s, F32) for s in shapes],
        compiler_params=_params("arbitrary", "arbitrary"),
    )(a3, *dhs)


def _qkv_rows(g):
    return [(part * QKV_W + g * ATT_OUT + hp * LANE, LANE) for hp in range(ATT_OUT // LANE) for part in range(3)]


XBC_START = 3 * QKV_W + ATT_OUT + D_INNER
GROUP_CH = GROUP_SSM_HEADS * HEAD_DIM
XBC_GROUP = GROUP_CH + 2 * D_STATE
CONV_DIM = SSM_GROUPS * XBC_GROUP


def _xbc_ranges():
    out = []
    for g in range(SSM_GROUPS):
        out += [(g * GROUP_CH, GROUP_CH), (D_INNER + g * D_STATE, D_STATE),
                (D_INNER + SSM_GROUPS * D_STATE + g * D_STATE, D_STATE)]
    return out


def _xbc_group_order(t):
    return jnp.concatenate([t[..., s:s + n] for s, n in _xbc_ranges()], axis=-1)


def _xbc_reference_order(t):
    g = lambda off, n: [t[..., k * XBC_GROUP + off:k * XBC_GROUP + off + n] for k in range(SSM_GROUPS)]
    return jnp.concatenate(g(0, GROUP_CH) + g(GROUP_CH, D_STATE) + g(GROUP_CH + D_STATE, D_STATE), axis=-1)


def _segments():
    one = lambda name, start, rows: (name, [(start, rows)], max(rows, LANE))
    return [("qkv%d" % g, _qkv_rows(g), QKV_W) for g in range(3)] + [
        one("gatt", 3 * QKV_W, ATT_OUT), one("z", 3 * QKV_W + ATT_OUT, D_INNER),
        ("xbc", [(XBC_START + s, n) for s, n in _xbc_ranges()], CONV_DIM), one("dt", DT_COL, SSM_HEADS),
        one("gm", DT_COL + SSM_HEADS, 2 * D_MODEL), one("gp", DT_COL + SSM_HEADS + 2 * D_MODEL, D_MODEL)]


LAYOUT_TC = 256
NAT_GROUPS = ((("gatt", "z", "dt", "gp"), 512), (("xbc", "gm"), 512))
DW_GROUPS = (("gatt", "z", "dt", "gp"), ("xbc",), ("gm",))


def _assemble(win):
    segs = _segments()

    def body(win_ref, *outs):
        def pieces(start, rows):
            t, end = start, start + rows
            while t < end:
                k = min(t // SHARD_COLS, 3)
                shard_end = (k + 1) * SHARD_COLS
                if k < 3 and shard_end % ROW_TILE and t == shard_end - shard_end % ROW_TILE:
                    lo = t - _win_start(k)
                    yield win_ref[k, lo:lo + ROW_TILE, :] + win_ref[k + 1, 0:ROW_TILE, :]
                    t += ROW_TILE
                    continue
                upto = min(end, shard_end - shard_end % ROW_TILE if k < 3 else end)
                yield win_ref[k, t - _win_start(k):upto - _win_start(k), :]
                t = upto

        for (_, ranges, total), o_ref in zip(segs, outs):
            off = 0
            for start, rows in ranges:
                for part in pieces(start, rows):
                    o_ref[off:off + part.shape[0], :] = part
                    off += part.shape[0]
            if off < total:
                o_ref[off:total, :] = jnp.zeros((total - off, o_ref.shape[1]), BF16)

    outs = _pcall(
        body, name="assemble_w_in", grid=(D_MODEL // LAYOUT_TC,),
        in_specs=[pl.BlockSpec((4, WIN_ROWS, LAYOUT_TC), lambda i: (0, 0, i))],
        out_specs=[pl.BlockSpec((total, LAYOUT_TC), lambda i: (0, i)) for _, _, total in segs],
        out_shape=[jax.ShapeDtypeStruct((total, D_MODEL), BF16) for _, _, total in segs],
        compiler_params=_params("parallel"),
    )(win)
    return {name: o for (name, _, _), o in zip(segs, outs)}


def _pack(dsegs):
    segs = _segments()

    def body(*refs):
        ins, o_ref = refs[:-1], refs[-1]
        tail = IN_COLS - _win_start(3)
        o_ref[3, tail:, :] = jnp.zeros((WIN_ROWS - tail, o_ref.shape[2]), BF16)
        for (_, ranges, _), s_ref in zip(segs, ins):
            off = 0
            for start, rows in ranges:
                for k in range(4):
                    lo = _win_start(k)
                    a, b = max(start, lo), min(start + rows, lo + WIN_ROWS)
                    if a < b:
                        o_ref[k, a - lo:b - lo, :] = s_ref[off + a - start:off + b - start, :]
                off += rows

    return _pcall(
        body, name="pack_dw_in", grid=(D_MODEL // LAYOUT_TC,),
        in_specs=[pl.BlockSpec((total, LAYOUT_TC), lambda i: (0, i)) for _, _, total in segs],
        out_specs=pl.BlockSpec((4, WIN_ROWS, LAYOUT_TC), lambda i: (0, 0, i)),
        out_shape=jax.ShapeDtypeStruct((4, WIN_ROWS, D_MODEL), BF16),
        compiler_params=_params("parallel"),
    )(*[dsegs[name] for name, _, _ in segs])


def _shard_to_window(shard_t, k):
    def at(off):
        return lambda w: jnp.pad(w.astype(BF16), ((off, WIN_ROWS - SHARD_COLS - off), (0, 0)))

    return lax.cond(k % 2 == 1, at(_win_offset(1)), at(_win_offset(0)), shard_t)


def _window_to_shard(win, k):
    return lax.dynamic_slice(win, ((k % 2) * _win_offset(1), 0), (SHARD_COLS, D_MODEL))


def _bucket_maps():
    qi = np.arange(8)[:, None]
    kj = np.arange(2 * BAND)[None, :]
    delta = qi + BAND - kj
    maps = []
    for window, dil in PATTERNS:
        valid = (delta >= 0) & (delta <= window // dil)
        dist = np.maximum(delta, 0) * dil
        max_exact = NUM_BUCKETS // 2
        d_f = np.maximum(dist, 1).astype(np.float32)
        large = max_exact + (np.log(d_f / np.float32(max_exact)) / np.float32(math.log(MAX_DISTANCE / max_exact))
                             * np.float32(NUM_BUCKETS - max_exact)).astype(np.int32)
        large = np.minimum(large, NUM_BUCKETS - 1)
        bucket = np.where(dist < max_exact, dist, large)
        maps.append(np.where(valid, bucket, -1).astype(np.int32))
    return np.stack(maps)


def _bias_tables(rel_bias, bmaps):
    def body(rb_ref, bm_ref, o_ref):
        g = pl.program_id(0)
        bm = bm_ref[...]
        for hh in range(GROUP_HEADS):
            acc = jnp.full(bm.shape, NEG, F32)
            for b in range(NUM_BUCKETS):
                acc = jnp.where(bm == b, rb_ref[b, g * GROUP_HEADS + hh], acc)
            for a in range(BAND // 8):
                o_ref[hh, 8 * a:8 * a + 8, :] = acc if a == 0 else pltpu.roll(acc, 8 * a, 1)

    return _pcall(
        body, name="bias_tables", grid=(3,),
        in_specs=[pl.BlockSpec(memory_space=pltpu.SMEM),
                  pl.BlockSpec((None, 8, 2 * BAND), lambda g: (g, 0, 0))],
        out_specs=pl.BlockSpec((GROUP_HEADS, BAND, 2 * BAND), lambda g: (g, 0, 0)),
        out_shape=jax.ShapeDtypeStruct((3 * GROUP_HEADS, BAND, 2 * BAND), F32),
        compiler_params=_params("parallel"),
    )(rel_bias, bmaps)


def _bias_grad(dbias, bmaps):
    def body(db_ref, bm_ref, o_ref):
        bm = bm_ref[...]
        lane = lax.broadcasted_iota(jnp.int32, (1, LANE), 1)
        for hh in range(GROUP_HEADS):
            db = db_ref[hh, 0:8, :]
            for a in range(1, BAND // 8):
                db = db + pltpu.roll(db_ref[hh, 8 * a:8 * a + 8, :], 2 * BAND - 8 * a, 1)
            vec = jnp.zeros((1, LANE), F32)
            for b in range(NUM_BUCKETS):
                s = jnp.sum(jnp.where(bm == b, db, 0.0), keepdims=True)
                vec = jnp.where(lane == b, s, vec)
            o_ref[hh] = vec

    return _pcall(
        body, name="bias_grad", grid=(3,),
        in_specs=[pl.BlockSpec((GROUP_HEADS, BAND, 2 * BAND), lambda g: (g, 0, 0)),
                  pl.BlockSpec((None, 8, 2 * BAND), lambda g: (g, 0, 0))],
        out_specs=pl.BlockSpec((GROUP_HEADS, 1, LANE), lambda g: (g, 0, 0)),
        out_shape=jax.ShapeDtypeStruct((3 * GROUP_HEADS, 1, LANE), F32),
        compiler_params=_params("parallel"),
    )(dbias, bmaps)


def _rows(n):
    if isinstance(n, int):
        return pl.ds(n * BAND, BAND)
    return pl.ds(pl.multiple_of(n * BAND, BAND), BAND)


def _for_blocks(blocks, nblk, per, carry):
    carry = blocks([0], carry, False)
    start = 1 + (nblk - 1) % per
    for n in range(1, start):
        carry = blocks([n], carry, True)
    trips = (nblk - start) // per
    if trips > 0:
        carry = lax.fori_loop(
            0, trips, lambda t, c: blocks([start + t * per + u for u in range(per)], c, True), carry)
    return carry


def _pairs_per_step(d):
    return {1: 3, 4: 6, 16: 6}[d]


def _bias_spec(group, hps):
    first = group * GROUP_HEADS // (2 * hps)
    return pl.BlockSpec((2 * hps, BAND, 2 * BAND), lambda hp, b, r: (first + hp, 0, 0))


def _attn_fwd(qkv4, bias, group, d, name):
    nb, _, sub, _ = qkv4.shape
    nblk = sub // BAND
    scale = HEAD_DIM ** -0.5
    npair = ATT_OUT // LANE
    hps = _pairs_per_step(d)
    compact = d > 1

    def body(qkv_ref, bias_ref, o_ref, l_ref):
        def blocks(ns, carry, with_prev):
            chains = [(bi, i, h) for bi in range(len(ns)) for i in range(hps) for h in range(2)]
            first_head = lax.broadcasted_iota(jnp.int32, (BAND, LANE), 1) < HEAD_DIM
            pair = lambda n, i, part: qkv_ref[_rows(n), (3 * i + part) * LANE:(3 * i + part + 1) * LANE]
            scores = []
            for bi, i, h in chains:
                n = ns[bi]
                qp = pair(n, i, 0) * scale
                q = jnp.where(first_head if h == 0 else jnp.logical_not(first_head), qp, jnp.zeros_like(qp))
                s_c = lax.dot_general(q, pair(n, i, 1), NT, preferred_element_type=F32) + bias_ref[2 * i + h, :, BAND:]
                s_p = None
                if with_prev:
                    s_p = lax.dot_general(q, pair(n - 1, i, 1), NT,
                                          preferred_element_type=F32) + bias_ref[2 * i + h, :, :BAND]
                scores.append((s_c, s_p))
            probs = []
            for s_c, s_p in scores:
                m = jnp.max(s_c, -1, keepdims=True)
                if with_prev:
                    m = jnp.maximum(m, jnp.max(s_p, -1, keepdims=True))
                e_c = jnp.exp(s_c - m)
                den = jnp.sum(e_c, -1, keepdims=True)
                e_p = None
                if with_prev:
                    e_p = jnp.exp(s_p - m)
                    den = den + jnp.sum(e_p, -1, keepdims=True)
                    e_p = e_p.astype(BF16)
                probs.append((e_c.astype(BF16), e_p, den, m))
            outs = {}
            for (bi, i, h), (e_c, e_p, den, m) in zip(chains, probs):
                n = ns[bi]
                acc = jnp.dot(e_c, pair(n, i, 2), preferred_element_type=F32)
                if with_prev:
                    acc = acc + jnp.dot(e_p, pair(n - 1, i, 2), preferred_element_type=F32)
                outs[(bi, i, h)] = (acc / den, m + jnp.log(den))
            lane = lax.broadcasted_iota(jnp.int32, (BAND, LANE), 1)
            for bi, n in enumerate(ns):
                per_head = jnp.zeros((BAND, LANE), F32)
                for i in range(hps):
                    o_ref[_rows(n), i * LANE:(i + 1) * LANE] = jnp.where(first_head, outs[(bi, i, 0)][0],
                                                                         outs[(bi, i, 1)][0])
                    if compact:
                        for h in range(2):
                            per_head = jnp.where(lane == 2 * i + h, outs[(bi, i, h)][1], per_head)
                    else:
                        l_ref[_rows(n), i * LANE:(i + 1) * LANE] = jnp.where(first_head, outs[(bi, i, 0)][1],
                                                                             outs[(bi, i, 1)][1])
                if compact:
                    l_ref[_rows(n), :] = per_head
            return carry

        _for_blocks(blocks, nblk, 2 if hps == 1 else 1, 0)

    in_specs = [pl.BlockSpec((None, None, sub, 3 * LANE * hps), lambda hp, b, r: (b, r, 0, hp)),
                _bias_spec(group, hps)]
    if compact:
        return _pcall(
            body, name=name, grid=(1, nb, d), in_specs=in_specs,
            out_specs=[pl.BlockSpec((None, None, sub, ATT_OUT), lambda hp, b, r: (b, r, 0, 0)),
                       pl.BlockSpec((None, None, sub, LANE), lambda hp, b, r: (b, r, 0, 0))],
            out_shape=[jax.ShapeDtypeStruct((nb, d, sub, ATT_OUT), F32), jax.ShapeDtypeStruct((nb, d, sub, LANE), F32)],
            compiler_params=_params("parallel", "parallel", "parallel"),
        )(qkv4, bias)
    ospec = pl.BlockSpec((None, sub, hps * LANE), lambda hp, b, r: (b, 0, r * (npair // hps) + hp))
    return _pcall(
        body, name=name, grid=(npair // hps, nb, d), in_specs=in_specs, out_specs=[ospec, ospec],
        out_shape=[jax.ShapeDtypeStruct((nb, sub, d * ATT_OUT), F32)] * 2,
        compiler_params=_params("parallel", "parallel", "parallel"),
    )(qkv4, bias)


STAT_LSE_LANE = 16


def _attn_bwd(qkv4, bias, group, cotangent, d, name):
    nb, _, sub, _ = qkv4.shape
    nblk = sub // BAND
    scale = HEAD_DIM ** -0.5
    npair = ATT_OUT // LANE
    hps = _pairs_per_step(d)
    compact = d > 1

    def body(qkv_ref, bias_ref, *rest):
        do_ref, dqkv_ref, db_ref = rest[0], rest[-2], rest[-1]
        b, r = pl.program_id(1), pl.program_id(2)

        @pl.when((b == 0) & (r == 0))
        def _():
            db_ref[...] = jnp.zeros_like(db_ref)

        def blocks(ns, carry, with_prev):
            sides = (0, 1) if with_prev else (0,)
            chains = [(bi, i, h, sd) for bi in range(len(ns)) for i in range(hps) for h in range(2) for sd in sides]
            first_head = lax.broadcasted_iota(jnp.int32, (BAND, LANE), 1) < HEAD_DIM
            own = lambda h, t: jnp.where(first_head if h == 0 else jnp.logical_not(first_head), t, jnp.zeros_like(t))
            pair = lambda rows, i, part: qkv_ref[rows, (3 * i + part) * LANE:(3 * i + part + 1) * LANE]
            key_rows = lambda bi, sd: _rows(ns[bi] - sd)
            qs = {}
            for bi in range(len(ns)):
                for i in range(hps):
                    q_pair = pair(_rows(ns[bi]), i, 0) * scale
                    do = do_ref[_rows(ns[bi]), i * LANE:(i + 1) * LANE]
                    do16 = do.astype(BF16)
                    for h in range(2):
                        if compact:
                            st_ref, head = rest[1], 2 * i + h
                            ebar = st_ref[_rows(ns[bi]), head:head + 1]
                            lcol = st_ref[_rows(ns[bi]), STAT_LSE_LANE + head:STAT_LSE_LANE + head + 1]
                        else:
                            ebar = jnp.sum(own(h, do * rest[1][_rows(ns[bi]), i * LANE:(i + 1) * LANE]), -1, keepdims=True)
                            lcol = rest[2][_rows(ns[bi]), i * LANE + h * HEAD_DIM:i * LANE + h * HEAD_DIM + 1]
                        qs[(bi, i, h)] = (own(h, q_pair), q_pair, own(h, do16), do16, ebar, lcol)
            raw = []
            for bi, i, h, sd in chains:
                q, _, do_h, _, _, _ = qs[(bi, i, h)]
                bias_blk = bias_ref[2 * i + h, :, :BAND] if sd else bias_ref[2 * i + h, :, BAND:]
                s = lax.dot_general(q, pair(key_rows(bi, sd), i, 1), NT, preferred_element_type=F32) + bias_blk
                dp = lax.dot_general(do_h, pair(key_rows(bi, sd), i, 2), NT, preferred_element_type=F32)
                raw.append((s, dp))
            soft = []
            for (bi, i, h, sd), (s, dp) in zip(chains, raw):
                ebar, lcol = qs[(bi, i, h)][4:]
                p = jnp.exp(s - lcol)
                ds = p * (dp - ebar)
                if sd:
                    db_ref[2 * i + h, :, :BAND] += ds
                else:
                    db_ref[2 * i + h, :, BAND:] += ds
                soft.append((p.astype(BF16), ds.astype(BF16)))
            grads = {}
            for (bi, i, h, sd), (p16, ds16) in zip(chains, soft):
                _, q_pair, _, do16 = qs[(bi, i, h)][:4]
                grads[(bi, i, h, sd)] = (
                    jnp.dot(ds16, pair(key_rows(bi, sd), i, 1), preferred_element_type=F32),
                    lax.dot_general(ds16, q_pair, TN, preferred_element_type=F32),
                    lax.dot_general(p16, do16, TN, preferred_element_type=F32))
            both = lambda bi, i, sd, which: jnp.where(first_head, grads[(bi, i, 0, sd)][which],
                                                      grads[(bi, i, 1, sd)][which])
            carry = list(carry) if carry is not None else None
            for bi, n in enumerate(ns):
                for i in range(hps):
                    base = 3 * LANE * i
                    dq = both(bi, i, 0, 0)
                    if with_prev:
                        dq = dq + both(bi, i, 1, 0)
                        dqkv_ref[_rows(n - 1), base + LANE:base + 2 * LANE] = (
                            carry[2 * i] + both(bi, i, 1, 1)).astype(BF16)
                        dqkv_ref[_rows(n - 1), base + 2 * LANE:base + 3 * LANE] = (
                            carry[2 * i + 1] + both(bi, i, 1, 2)).astype(BF16)
                    dqkv_ref[_rows(n), base:base + LANE] = (dq * scale).astype(BF16)
                carry = [t for i in range(hps) for t in (both(bi, i, 0, 1), both(bi, i, 0, 2))]
            return tuple(carry)

        carry = _for_blocks(blocks, nblk, 2 if hps == 1 else 1, None)
        for i in range(hps):
            base = 3 * LANE * i
            dqkv_ref[_rows(nblk - 1), base + LANE:base + 2 * LANE] = carry[2 * i].astype(BF16)
            dqkv_ref[_rows(nblk - 1), base + 2 * LANE:base + 3 * LANE] = carry[2 * i + 1].astype(BF16)

    qspec = pl.BlockSpec((None, None, sub, 3 * LANE * hps), lambda hp, b, r: (b, r, 0, hp))
    bspec = pl.BlockSpec((2 * hps, BAND, 2 * BAND), lambda hp, b, r: (hp, 0, 0))
    if compact:
        cspecs = [pl.BlockSpec((None, None, sub, ATT_OUT), lambda hp, b, r: (b, r, 0, 0)),
                  pl.BlockSpec((None, None, sub, LANE), lambda hp, b, r: (b, r, 0, 0))]
    else:
        cspecs = [pl.BlockSpec((None, sub, hps * LANE), lambda hp, b, r: (b, 0, r * (npair // hps) + hp))] * 3
    return _pcall(
        body, name=name, grid=(npair // hps, nb, d),
        in_specs=[qspec, _bias_spec(group, hps)] + cspecs, out_specs=[qspec, bspec],
        out_shape=[jax.ShapeDtypeStruct(qkv4.shape, BF16),
                   jax.ShapeDtypeStruct((GROUP_HEADS, BAND, 2 * BAND), F32)],
        compiler_params=_params("parallel", "arbitrary", "arbitrary"),
    )(qkv4, bias, *cotangent)


def _head_lanes(first_lane, one_channel):
    c = lax.broadcasted_iota(jnp.int32, (ATT_OUT, LANE), 0)
    lane = lax.broadcasted_iota(jnp.int32, (ATT_OUT, LANE), 1)
    hit = lane == first_lane + c // HEAD_DIM
    if one_channel:
        hit = hit & (c % HEAD_DIM == 0)
    return hit.astype(BF16)


def _exact_dot(v, m01, dims=None):
    parts = _split3(v)
    if dims is None:
        dot = lambda t: jnp.dot(t, m01, preferred_element_type=F32)
    else:
        dot = lambda t: lax.dot_general(t, m01, dims, preferred_element_type=F32)
    return (dot(parts[0]) + dot(parts[1])) + dot(parts[2])


def _store_own_order(value, tile_refs, out_ref):
    d, per, width = out_ref.shape
    for j in range(width // LANE):
        tile_refs[j][...] = value[:, j * LANE:(j + 1) * LANE]
    for r in range(d):
        rows = pl.ds(r, per, stride=d)
        for j in range(width // LANE):
            out_ref[r, :, j * LANE:(j + 1) * LANE] = tile_refs[j][rows, :].astype(out_ref.dtype)


def _token_orders(x, dilations):
    nb, seq, kdim = x.shape
    tm = 512

    def body(x_ref, nat_ref, *rest):
        outs, tile_refs = rest[:len(dilations)], rest[len(dilations):]
        xv = x_ref[...]
        nat_ref[...] = xv.astype(BF16)
        for o_ref in outs:
            _store_own_order(xv, tile_refs, o_ref)

    outs = _pcall(
        body, name="token_orders", grid=(nb, seq // tm), in_specs=[_tok_spec(tm, kdim)],
        out_specs=[_tok_spec(tm, kdim)]
        + [pl.BlockSpec((None, d, tm // d, kdim), lambda b, i: (b, 0, i, 0)) for d in dilations],
        out_shape=[jax.ShapeDtypeStruct((nb, seq, kdim), BF16)]
        + [jax.ShapeDtypeStruct((nb, d, seq // d, kdim), BF16) for d in dilations],
        scratch_shapes=[pltpu.VMEM((tm, LANE), F32)] * (kdim // LANE),
        compiler_params=_params("parallel", "parallel"),
    )(x)
    return [outs[0]] + [o.reshape(nb, seq, kdim) for o in outs[1:]]


def _natural_rows(p_ref, tile_refs):
    d, per, width = p_ref.shape
    for r in range(d):
        rows = pl.ds(r, per, stride=d)
        for j in range(width // LANE):
            tile_refs[j][rows, :] = p_ref[r, :, j * LANE:(j + 1) * LANE]
    return jnp.concatenate([tile_refs[j][...] for j in range(width // LANE)], axis=1)


def _combine_fwd(o0, l0, dilated, gatt):
    nb, seq, _ = gatt.shape
    tm = 512
    ntile = ATT_OUT // LANE

    def body(o0_ref, l0_ref, o1_ref, l1_ref, o2_ref, l2_ref, g_ref, oa_ref, oatt_ref, lse_ref, *tile_refs):
        spread = _head_lanes(0, False)
        l0v = l0_ref[...]
        l1v = _exact_dot(_natural_rows(l1_ref, tile_refs), spread, NT)
        l2v = _exact_dot(_natural_rows(l2_ref, tile_refs), spread, NT)
        m = jnp.maximum(jnp.maximum(l0v, l1v), l2v)
        tot = m + jnp.log(jnp.exp(l0v - m) + jnp.exp(l1v - m) + jnp.exp(l2v - m))
        o = jnp.exp(l0v - tot) * o0_ref[...]
        o = o + jnp.exp(l1v - tot) * _natural_rows(o1_ref, tile_refs)
        o = o + jnp.exp(l2v - tot) * _natural_rows(o2_ref, tile_refs)
        g = g_ref[...]
        oa_ref[...] = (o * (g * _sigmoid(g))).astype(BF16)
        oatt_ref[...] = o
        lse_ref[...] = tot

    spec = pl.BlockSpec((None, tm, ATT_OUT), lambda b, i: (b, i, 0))
    own = lambda t: pl.BlockSpec((None, t.shape[1], tm // t.shape[1], t.shape[3]), lambda b, i: (b, 0, i, 0))
    (o1, l1), (o2, l2) = dilated
    return _pcall(
        body, name="attn_combine", grid=(nb, seq // tm),
        in_specs=[spec, spec, own(o1), own(l1), own(o2), own(l2), spec], out_specs=[spec] * 3,
        out_shape=[jax.ShapeDtypeStruct((nb, seq, ATT_OUT), BF16), jax.ShapeDtypeStruct((nb, seq, ATT_OUT), F32),
                   jax.ShapeDtypeStruct((nb, seq, ATT_OUT), F32)],
        scratch_shapes=[pltpu.VMEM((tm, LANE), F32)] * ntile,
        compiler_params=_params("parallel", "parallel"),
    )(o0, l0, o1, l1, o2, l2, gatt)


def _combine_bwd(doa, gatt, o_att, lse, dilations):
    nb, seq, _ = gatt.shape
    tm = 512

    def body(doa_ref, g_ref, o_ref, l_ref, do_ref, dg_ref, *rest):
        ntile = ATT_OUT // LANE
        outs, tile_refs = rest[:-ntile], rest[-ntile:]
        g = g_ref[...]
        sg = _sigmoid(g)
        do = doa_ref[...] * (g * sg)
        do_ref[...] = do
        stats = (_exact_dot(do * o_ref[...], _head_lanes(0, False))
                 + _exact_dot(l_ref[...], _head_lanes(STAT_LSE_LANE, True)))
        dg_ref[...] = (doa_ref[...] * o_ref[...] * (sg * (1.0 + g * (1.0 - sg)))).astype(BF16)
        for k in range(len(dilations)):
            _store_own_order(do, tile_refs, outs[2 * k])
            _store_own_order(stats, tile_refs, outs[2 * k + 1])

    spec = pl.BlockSpec((None, tm, ATT_OUT), lambda b, i: (b, i, 0))
    own = lambda d, width: pl.BlockSpec((None, d, tm // d, width), lambda b, i: (b, 0, i, 0))
    outs = _pcall(
        body, name="attn_combine_bwd", grid=(nb, seq // tm), in_specs=[spec] * 4,
        out_specs=[spec, spec] + [own(d, w) for d in dilations for w in (ATT_OUT, LANE)],
        out_shape=[jax.ShapeDtypeStruct((nb, seq, ATT_OUT), F32), jax.ShapeDtypeStruct((nb, seq, ATT_OUT), BF16)]
        + [jax.ShapeDtypeStruct((nb, d, seq // d, w), t) for d in dilations for w, t in ((ATT_OUT, BF16), (LANE, F32))],
        scratch_shapes=[pltpu.VMEM((tm, LANE), F32)] * (ATT_OUT // LANE),
        compiler_params=_params("parallel", "parallel"),
    )(doa, gatt, o_att, lse)
    return outs[0], outs[1], outs[2:]


CONV_TM = 1024
CONV_TC = 1024


def _shift_down(cur, halo, k):
    rolled = pltpu.roll(cur, k, 0)
    hro = pltpu.roll(halo, k, 0)
    row = lax.broadcasted_iota(jnp.int32, hro.shape, 0)
    return jnp.concatenate([jnp.where(row < k, hro, rolled[:8]), rolled[8:]], axis=0)


def _shift_up(cur, halo, k):
    n = cur.shape[0]
    rolled = pltpu.roll(cur, n - k, 0)
    hro = pltpu.roll(halo, 8 - k, 0)
    row = lax.broadcasted_iota(jnp.int32, hro.shape, 0)
    return jnp.concatenate([rolled[:n - 8], jnp.where(row >= 8 - k, hro, rolled[n - 8:])], axis=0)


def _conv_pre(cur, halo, w_ref, b_ref):
    acc = cur * w_ref[3:4, :] + b_ref[...]
    for k in range(1, 4):
        acc = acc + _shift_down(cur, halo, k) * w_ref[3 - k:4 - k, :]
    return acc


def _conv_specs(seq):
    nblk = seq // CONV_TM
    cur = pl.BlockSpec((None, CONV_TM, CONV_TC), lambda cb, b, i: (b, i, cb))
    prev = pl.BlockSpec((None, 8, CONV_TC), lambda cb, b, i: (b, jnp.maximum(i * (CONV_TM // 8) - 1, 0), cb))
    nxt = pl.BlockSpec((None, 8, CONV_TC),
                       lambda cb, b, i: (b, jnp.minimum((i + 1) * (CONV_TM // 8), seq // 8 - 1), cb))
    wspec = pl.BlockSpec((4, CONV_TC), lambda cb, b, i: (0, cb))
    bspec = pl.BlockSpec((1, CONV_TC), lambda cb, b, i: (0, cb))
    return nblk, cur, prev, nxt, wspec, bspec


def _conv_fwd(xin, w4, bias, name):
    nb, seq, ch = xin.shape
    _, cur, prev, _, wspec, bspec = _conv_specs(seq)

    def body(x_ref, h_ref, w_ref, b_ref, o_ref):
        halo = jnp.where(pl.program_id(2) > 0, h_ref[...], 0.0)
        pre = _conv_pre(x_ref[...], halo, w_ref, b_ref)
        o_ref[...] = pre * _sigmoid(pre)

    return _pcall(
        body, name=name, grid=(ch // CONV_TC, nb, seq // CONV_TM),
        in_specs=[cur, prev, wspec, bspec], out_specs=cur,
        out_shape=jax.ShapeDtypeStruct(xin.shape, F32),
        compiler_params=_params("parallel", "parallel", "parallel"),
    )(xin, xin, w4, bias)


def _conv_bwd_pre(dact, xin, w4, bias, name):
    nb, seq, ch = xin.shape
    _, cur, prev, _, wspec, bspec = _conv_specs(seq)

    def body(da_ref, x_ref, h_ref, w_ref, b_ref, dp_ref, s_ref):
        b, i = pl.program_id(1), pl.program_id(2)

        @pl.when((b == 0) & (i == 0))
        def _():
            s_ref[...] = jnp.zeros_like(s_ref)

        halo = jnp.where(i > 0, h_ref[...], 0.0)
        x = x_ref[...]
        pre = _conv_pre(x, halo, w_ref, b_ref)
        sg = _sigmoid(pre)
        dpre = da_ref[...] * (sg * (1.0 + pre * (1.0 - sg)))
        dp_ref[...] = dpre
        s_ref[3:4, :] += jnp.sum(dpre * x, 0, keepdims=True)
        for k in range(1, 4):
            s_ref[3 - k:4 - k, :] += jnp.sum(dpre * _shift_down(x, halo, k), 0, keepdims=True)
        s_ref[4:5, :] += jnp.sum(dpre, 0, keepdims=True)

    return _pcall(
        body, name=name, grid=(ch // CONV_TC, nb, seq // CONV_TM),
        in_specs=[cur, cur, prev, wspec, bspec],
        out_specs=[cur, pl.BlockSpec((8, CONV_TC), lambda cb, b, i: (0, cb))],
        out_shape=[jax.ShapeDtypeStruct(xin.shape, F32), jax.ShapeDtypeStruct((8, ch), F32)],
        compiler_params=_params("parallel", "arbitrary", "arbitrary"),
    )(dact, xin, xin, w4, bias)


def _conv_bwd_x(dpre, w4, name):
    nb, seq, ch = dpre.shape
    nblk, cur, _, nxt, wspec, _ = _conv_specs(seq)

    def body(d_ref, n_ref, w_ref, o_ref):
        halo = jnp.where(pl.program_id(2) < nblk - 1, n_ref[...], 0.0)
        cur_v = d_ref[...]
        acc = cur_v * w_ref[3:4, :]
        for j in range(1, 4):
            acc = acc + _shift_up(cur_v, halo, j) * w_ref[3 - j:4 - j, :]
        o_ref[...] = acc.astype(BF16)

    return _pcall(
        body, name=name, grid=(ch // CONV_TC, nb, seq // CONV_TM),
        in_specs=[cur, nxt, wspec], out_specs=cur,
        out_shape=jax.ShapeDtypeStruct(dpre.shape, BF16),
        compiler_params=_params("parallel", "parallel", "parallel"),
    )(dpre, dpre, w4)


def _softplus_sig(dt_raw, dt_bias_row):
    nb, seq, _ = dt_raw.shape
    tm = 512

    def body(r_ref, b_ref, sp_ref, sg_ref):
        v = r_ref[...] + b_ref[...]
        sp_ref[...] = jnp.maximum(v, 0.0) + jnp.log1p(jnp.exp(-jnp.abs(v)))
        sg_ref[...] = _sigmoid(v)

    spec = pl.BlockSpec((None, tm, LANE), lambda b, i: (b, i, 0))
    return _pcall(
        body, name="dt_softplus", grid=(nb, seq // tm),
        in_specs=[spec, pl.BlockSpec((1, LANE), lambda b, i: (0, 0))], out_specs=[spec, spec],
        out_shape=[jax.ShapeDtypeStruct(dt_raw.shape, F32)] * 2,
        compiler_params=_params("parallel", "parallel"),
    )(dt_raw, dt_bias_row)


def _group_lanes(t):
    pads = [(0, 0)] * (t.ndim - 1) + [(0, LANE - GROUP_SSM_HEADS)]
    return jnp.stack([jnp.pad(t[..., GROUP_SSM_HEADS * g:GROUP_SSM_HEADS * (g + 1)], pads) for g in range(SSM_GROUPS)])


def _ungroup_lanes(t):
    return jnp.concatenate([t[g][..., :GROUP_SSM_HEADS] for g in range(SSM_GROUPS)], axis=-1)


def _decays(dt, al_ref):
    row = lax.broadcasted_iota(jnp.int32, (CHUNK, CHUNK), 0)
    col = lax.broadcasted_iota(jnp.int32, (CHUNK, CHUNK), 1)
    tril = (row >= col).astype(BF16)
    triu = (row <= col).astype(BF16)
    arow = -jnp.exp(al_ref[...])
    hi, mid, lo = _split3(dt * arow)
    down = lambda t: jnp.dot(tril, t, preferred_element_type=F32)
    across = lambda t: lax.dot_general(t, triu, TN, preferred_element_type=F32)
    acs = (down(hi) + down(mid)) + down(lo)
    acs_t = (across(hi) + across(mid)) + across(lo)
    return arow, acs, acs_t, row >= col, triu


STEP_CHUNKS = 8


def _ssd_specs(nb, seq):
    nc = seq // CHUNK
    hw = GROUP_SSM_HEADS * HEAD_DIM
    rows, steps = STEP_CHUNKS * CHUNK, nc // STEP_CHUNKS

    def mk(rev):
        cidx = (lambda c: steps - 1 - c) if rev else (lambda c: c)
        wide = pl.BlockSpec((None, rows, hw), lambda g, b, c: (b, cidx(c), g))
        xbc = pl.BlockSpec((None, rows, XBC_GROUP), lambda g, b, c: (b, cidx(c), g))
        lanes = pl.BlockSpec((None, None, rows, LANE), lambda g, b, c: (g, b, cidx(c), 0))
        prev = pl.BlockSpec((None, STEP_CHUNKS, None, D_STATE, hw), lambda g, b, c: (b, cidx(c), g, 0, 0))
        return wide, xbc, lanes, prev

    grow = pl.BlockSpec((None, 1, LANE), lambda g, b, c: (g, 0, 0))
    nwspec = pl.BlockSpec((1, hw), lambda g, b, c: (0, g))
    return nc, steps, hw, mk, grow, nwspec


def _head_expand():
    hw = GROUP_SSM_HEADS * HEAD_DIM
    r = lax.broadcasted_iota(jnp.int32, (LANE, hw), 0)
    c = lax.broadcasted_iota(jnp.int32, (LANE, hw), 1)
    return ((c // HEAD_DIM) == r).astype(BF16)


def _split3(v):
    hi = v.astype(BF16)
    rest = v - hi.astype(F32)
    mid = rest.astype(BF16)
    return hi, mid, (rest - mid.astype(F32)).astype(BF16)


def _to_channels(v, e):
    hi, mid, lo = _split3(v)
    dot = lambda t: jnp.dot(t, e, preferred_element_type=F32)
    return (dot(hi) + dot(mid)) + dot(lo)


def _to_heads(w, e):
    hi, mid, lo = _split3(w)
    dot = lambda t: lax.dot_general(t, e, (((1,), (1,)), ((), ())), preferred_element_type=F32)
    return (dot(hi) + dot(mid)) + dot(lo)


def _row8(v):
    return jnp.broadcast_to(v, (8, v.shape[1]))


def _ssd_chunk_setup(dt, al_ref, ds_ref):
    arow, acs, acs_t, causal, triu = _decays(dt, al_ref)
    e = _head_expand()
    dtx = _to_channels(dt, e)
    acsx = _to_channels(acs, e)
    lastx = acsx[CHUNK - 1:CHUNK, :]
    dskx = _to_channels(_row8(ds_ref[...]), e)[0:1, :]
    return arow, acs, acs_t, causal, triu, e, dtx, acsx, lastx, dskx


def _ssd_fwd(xbc, dtg, z, alog_g, dskip_g, normw):
    nb, seq, _ = xbc.shape
    nc, steps, hw, mk, grow, nwspec = _ssd_specs(nb, seq)
    wide, xbc_spec, lanes, prev = mk(False)
    tn = (((0,), (0,)), ((), ()))

    def body(xbc_ref, dt_ref, z_ref, al_ref, ds_ref, nw_ref, ys_ref, y_ref, sp_ref, st_ref):
        @pl.when(pl.program_id(2) == 0)
        def _():
            st_ref[...] = jnp.zeros_like(st_ref)

        for ci in range(STEP_CHUNKS):
            chunk(ci, xbc_ref, dt_ref, z_ref, al_ref, ds_ref, nw_ref, ys_ref, y_ref, sp_ref, st_ref)

    def chunk(ci, xbc_ref, dt_ref, z_ref, al_ref, ds_ref, nw_ref, ys_ref, y_ref, sp_ref, st_ref):
        rows = slice(ci * CHUNK, (ci + 1) * CHUNK)
        dt = dt_ref[rows, :]
        _, acs, acs_t, causal, _, _, dtx, acsx, lastx, dskx = _ssd_chunk_setup(dt, al_ref, ds_ref)
        bmat = xbc_ref[rows, GROUP_CH:GROUP_CH + D_STATE].astype(BF16)
        cmat = xbc_ref[rows, GROUP_CH + D_STATE:].astype(BF16)
        cb = lax.dot_general(cmat, bmat, (((1,), (1,)), ((), ())), preferred_element_type=F32)
        x = xbc_ref[rows, :GROUP_CH]
        xdt = x * dtx
        xdt16 = xdt.astype(BF16)
        first_head = lax.broadcasted_iota(jnp.int32, (CHUNK, LANE), 1) < HEAD_DIM
        pairs = []
        for hp in range(GROUP_SSM_HEADS // 2):
            xp = xdt16[:, hp * LANE:(hp + 1) * LANE]
            two = []
            for j in (2 * hp, 2 * hp + 1):
                lmat = jnp.exp(jnp.where(causal, acs[:, j:j + 1] - acs_t[j:j + 1, :], -jnp.inf))
                two.append(jnp.dot((cb * lmat).astype(BF16), xp, preferred_element_type=F32))
            pairs.append(jnp.where(first_head, two[0], two[1]))
        yd = jnp.concatenate(pairs, axis=1)
        s_prev = st_ref[...]
        s16 = s_prev.astype(BF16)
        sp_ref[ci] = s16
        yo = jnp.dot(cmat, s16, preferred_element_type=F32) * jnp.exp(acsx)
        sts = lax.dot_general(bmat, (xdt * jnp.exp(lastx - acsx)).astype(BF16), tn, preferred_element_type=F32)
        st_ref[...] = s_prev * jnp.exp(lastx) + sts
        y = yd + yo + dskx * x
        zz = z_ref[rows, :]
        u = y * (zz * _sigmoid(zz))
        rn = lax.rsqrt(jnp.mean(u * u, -1, keepdims=True) + RMS_EPS)
        ys_ref[rows, :] = (u * rn * nw_ref[...]).astype(BF16)
        y_ref[rows, :] = y

    return _pcall(
        body, name="ssd_fwd", grid=(SSM_GROUPS, nb, steps),
        in_specs=[xbc_spec, lanes, wide, grow, grow, nwspec],
        out_specs=[wide, wide, prev],
        out_shape=[jax.ShapeDtypeStruct((nb, seq, D_INNER), BF16), jax.ShapeDtypeStruct((nb, seq, D_INNER), F32),
                   jax.ShapeDtypeStruct((nb, nc, SSM_GROUPS, D_STATE, hw), BF16)],
        scratch_shapes=[pltpu.VMEM((D_STATE, hw), F32)],
        compiler_params=_params("parallel", "parallel", "arbitrary"),
    )(xbc, dtg, z, alog_g, dskip_g, normw)


def _ssd_bwd(xbc, dtg, sgg, z, y, dys, sprev, alog_g, dskip_g, normw):
    nb, seq, _ = xbc.shape
    nc, steps, hw, mk, grow, nwspec = _ssd_specs(nb, seq)
    wide, xbc_spec, lanes, prev = mk(True)
    nt = (((1,), (1,)), ((), ()))
    tn = (((0,), (0,)), ((), ()))

    def body(xbc_ref, dt_ref, sg_ref, z_ref, y_ref, dys_ref, sp_ref, al_ref, ds_ref, nw_ref,
             dxbc_ref, ddt_ref, dz_ref, small_ref, dnw_ref, g_ref):
        b, c = pl.program_id(1), pl.program_id(2)

        @pl.when((b == 0) & (c == 0))
        def _():
            small_ref[...] = jnp.zeros_like(small_ref)
            dnw_ref[...] = jnp.zeros_like(dnw_ref)

        @pl.when(c == 0)
        def _():
            g_ref[...] = jnp.zeros_like(g_ref)

        for ci in reversed(range(STEP_CHUNKS)):
            chunk(ci, xbc_ref, dt_ref, sg_ref, z_ref, y_ref, dys_ref, sp_ref, al_ref, ds_ref, nw_ref,
                  dxbc_ref, ddt_ref, dz_ref, small_ref, dnw_ref, g_ref)

    def chunk(ci, xbc_ref, dt_ref, sg_ref, z_ref, y_ref, dys_ref, sp_ref, al_ref, ds_ref, nw_ref,
              dxbc_ref, ddt_ref, dz_ref, small_ref, dnw_ref, g_ref):
        rows = slice(ci * CHUNK, (ci + 1) * CHUNK)
        yv, zz, dys_v, nw = y_ref[rows, :], z_ref[rows, :], dys_ref[rows, :], nw_ref[...]
        sz = _sigmoid(zz)
        silu = zz * sz
        u = yv * silu
        rn = lax.rsqrt(jnp.mean(u * u, -1, keepdims=True) + RMS_EPS)
        gn = dys_v * nw
        du = rn * gn - u * (rn * rn * rn) * jnp.mean(u * gn, -1, keepdims=True)
        dnw_ref[...] += jnp.sum(dys_v * u * rn, 0, keepdims=True)
        dy = du * silu
        dz_ref[rows, :] = du * yv * (sz * (1.0 + zz * (1.0 - sz)))

        dt = dt_ref[rows, :]
        arow, acs, acs_t, causal, triu, e, dtx, acsx, lastx, dskx = _ssd_chunk_setup(dt, al_ref, ds_ref)
        dfsx = jnp.exp(acsx)
        dtex = jnp.exp(lastx - acsx)
        bmat = xbc_ref[rows, GROUP_CH:GROUP_CH + D_STATE].astype(BF16)
        cmat = xbc_ref[rows, GROUP_CH + D_STATE:].astype(BF16)
        cb = lax.dot_general(cmat, bmat, nt, preferred_element_type=F32)
        x = xbc_ref[rows, :GROUP_CH]
        xdt = x * dtx
        xdt16 = xdt.astype(BF16)
        xdte = xdt * dtex
        dy16 = dy.astype(BF16)
        dyd = dy * dfsx
        dyd16 = dyd.astype(BF16)
        s16 = sp_ref[ci]
        g = g_ref[...]
        g16 = g.astype(BF16)
        cs = jnp.dot(cmat, s16, preferred_element_type=F32)
        dc_off = lax.dot_general(dyd16, s16, nt, preferred_element_type=F32)
        g_here = lax.dot_general(cmat, dyd16, tn, preferred_element_type=F32)
        bg = jnp.dot(bmat, g16, preferred_element_type=F32)
        db_st = lax.dot_general(xdte.astype(BF16), g16, nt, preferred_element_type=F32)
        ddte_w = bg * xdte
        dcd = _to_heads(_row8(jnp.sum(g * s16.astype(F32), 0, keepdims=True)), e)[0:1, :]
        lane = lax.broadcasted_iota(jnp.int32, (CHUNK, LANE), 1)
        first_head = lane < HEAD_DIM
        sub = lax.broadcasted_iota(jnp.int32, (CHUNK, LANE), 0)
        dacs = jnp.zeros((CHUNK, LANE), F32)
        colsums = jnp.zeros((CHUNK, LANE), F32)
        dcb = jnp.zeros((CHUNK, CHUNK), F32)
        pairs = []
        for hp in range(GROUP_SSM_HEADS // 2):
            xp = xdt16[:, hp * LANE:(hp + 1) * LANE]
            dyp = dy16[:, hp * LANE:(hp + 1) * LANE]
            two = []
            for idx, j in enumerate((2 * hp, 2 * hp + 1)):
                lmat = jnp.exp(jnp.where(causal, acs[:, j:j + 1] - acs_t[j:j + 1, :], -jnp.inf))
                mf = cb * lmat
                dy_h = jnp.where(first_head if idx == 0 else jnp.logical_not(first_head), dyp, jnp.zeros_like(dyp))
                dm = lax.dot_general(dy_h, xp, nt, preferred_element_type=F32)
                two.append(lax.dot_general(mf.astype(BF16), dyp, tn, preferred_element_type=F32))
                wmat = dm * mf
                dcb = dcb + dm * lmat
                dacs = jnp.where(lane == j, jnp.sum(wmat, -1, keepdims=True), dacs)
                colsums = jnp.where(sub == j, jnp.sum(wmat, 0, keepdims=True), colsums)
            pairs.append(jnp.where(first_head, two[0], two[1]))
        dxdt = bg * dtex + jnp.concatenate(pairs, axis=1)
        dacs = dacs - colsums.T + _to_heads(dyd * cs - ddte_w, e)
        cd_row = jnp.exp(acs[CHUNK - 1:CHUNK, :])
        tail = _to_heads(_row8(jnp.sum(ddte_w, 0, keepdims=True)), e)[0:1, :] + dcd * cd_row
        dacs = dacs + jnp.where(sub == CHUNK - 1, tail, 0.0)
        d_hi, d_mid, d_lo = _split3(dacs)
        up = lambda t: jnp.dot(triu, t, preferred_element_type=F32)
        da = (up(d_hi) + up(d_mid)) + up(d_lo)
        ddt_raw = (da * arow + _to_heads(dxdt * x, e)) * sg_ref[rows, :]
        ddt_ref[rows, :] = ddt_raw
        small_ref[0:1, :] += jnp.sum(da * dt, 0, keepdims=True) * arow
        small_ref[1:2, :] += _to_heads(_row8(jnp.sum(dy * x, 0, keepdims=True)), e)[0:1, :]
        small_ref[2:3, :] += jnp.sum(ddt_raw, 0, keepdims=True)
        dcb16 = dcb.astype(BF16)
        dxbc_ref[rows, GROUP_CH + D_STATE:] = dc_off + jnp.dot(dcb16, bmat, preferred_element_type=F32)
        dxbc_ref[rows, GROUP_CH:GROUP_CH + D_STATE] = db_st + lax.dot_general(dcb16, cmat, tn,
                                                                               preferred_element_type=F32)
        dxbc_ref[rows, :GROUP_CH] = dxdt * dtx + dskx * dy
        g_ref[...] = g * jnp.exp(lastx) + g_here

    return _pcall(
        body, name="ssd_bwd", grid=(SSM_GROUPS, nb, steps),
        in_specs=[xbc_spec, lanes, lanes, wide, wide, wide, prev, grow, grow, nwspec],
        out_specs=[xbc_spec, lanes, wide,
                   pl.BlockSpec((None, 8, LANE), lambda g, b, c: (g, 0, 0)), nwspec],
        out_shape=[jax.ShapeDtypeStruct((nb, seq, CONV_DIM), F32),
                   jax.ShapeDtypeStruct((SSM_GROUPS, nb, seq, LANE), F32),
                   jax.ShapeDtypeStruct((nb, seq, D_INNER), F32),
                   jax.ShapeDtypeStruct((SSM_GROUPS, 8, LANE), F32),
                   jax.ShapeDtypeStruct((1, D_INNER), F32)],
        scratch_shapes=[pltpu.VMEM((D_STATE, hw), F32)],
        compiler_params=_params("parallel", "arbitrary", "arbitrary"),
    )(xbc, dtg, sgg, z, y, dys, sprev, alog_g, dskip_g, normw)


EW_TM = 256


def _merge_fwd(y_a, y_b, gm, bgate):
    nb, seq, _ = y_a.shape

    def body(a_ref, b_ref, ga_ref, gb_ref, bg_ref, o_ref):
        sa = _sigmoid(ga_ref[...] + bg_ref[0:1, :])
        sb = _sigmoid(gb_ref[...] + bg_ref[1:2, :])
        o_ref[...] = (sa * a_ref[...] + sb * b_ref[...]).astype(BF16)

    spec = pl.BlockSpec((None, EW_TM, D_MODEL), lambda b, i: (b, i, 0))
    spec1 = pl.BlockSpec((None, EW_TM, D_MODEL), lambda b, i: (b, i, 1))
    return _pcall(
        body, name="merge_fwd", grid=(nb, seq // EW_TM),
        in_specs=[spec, spec, spec, spec1, pl.BlockSpec((8, D_MODEL), lambda b, i: (0, 0))], out_specs=spec,
        out_shape=jax.ShapeDtypeStruct((nb, seq, D_MODEL), BF16),
        compiler_params=_params("parallel", "parallel"),
    )(y_a, y_b, gm, gm, bgate)


def _merge_bwd(dpre16, w_out16, y_a, y_b, gm, bgate):
    nb, seq, _ = y_a.shape

    def body(dp_ref, w_ref, a_ref, b_ref, ga_ref, gb_ref, bg_ref, dya_ref, dyb_ref, dg_ref, s_ref):
        @pl.when((pl.program_id(0) == 0) & (pl.program_id(1) == 0))
        def _():
            s_ref[...] = jnp.zeros_like(s_ref)

        dm = lax.dot_general(dp_ref[...], w_ref[...], NT, preferred_element_type=F32)
        sa = _sigmoid(ga_ref[...] + bg_ref[0:1, :])
        sb = _sigmoid(gb_ref[...] + bg_ref[1:2, :])
        dya_ref[...] = (dm * sa).astype(BF16)
        dyb_ref[...] = (dm * sb).astype(BF16)
        dga = dm * a_ref[...] * (sa * (1.0 - sa))
        dgb = dm * b_ref[...] * (sb * (1.0 - sb))
        dg_ref[:, :D_MODEL] = dga.astype(BF16)
        dg_ref[:, D_MODEL:] = dgb.astype(BF16)
        s_ref[0:1, :] += jnp.sum(dga, 0, keepdims=True)
        s_ref[1:2, :] += jnp.sum(dgb, 0, keepdims=True)

    spec = pl.BlockSpec((None, EW_TM, D_MODEL), lambda b, i: (b, i, 0))
    spec1 = pl.BlockSpec((None, EW_TM, D_MODEL), lambda b, i: (b, i, 1))
    small = pl.BlockSpec((8, D_MODEL), lambda b, i: (0, 0))
    return _pcall(
        body, name="merge_bwd", grid=(nb, seq // EW_TM),
        in_specs=[spec, _whole(w_out16, True), spec, spec, spec, spec1, small],
        out_specs=[spec, spec, pl.BlockSpec((None, EW_TM, 2 * D_MODEL), lambda b, i: (b, i, 0)), small],
        out_shape=[jax.ShapeDtypeStruct((nb, seq, D_MODEL), BF16), jax.ShapeDtypeStruct((nb, seq, D_MODEL), BF16),
                   jax.ShapeDtypeStruct((nb, seq, 2 * D_MODEL), BF16), jax.ShapeDtypeStruct((8, D_MODEL), F32)],
        compiler_params=_params("arbitrary", "arbitrary"),
    )(dpre16, w_out16, y_a, y_b, gm, gm, bgate)


def _ln_loss(x, merged16, w_out16, gp, p16, w_ple16, target, bgate, ln_g, ln_b):
    nb, seq, _ = x.shape

    def body(x_ref, m_ref, wo_ref, gp_ref, p_ref, wp_ref, t_ref, bg_ref, g_ref, b_ref,
             dx_ref, dp_ref, dpw_ref, dgp_ref, s_ref):
        @pl.when((pl.program_id(0) == 0) & (pl.program_id(1) == 0))
        def _():
            s_ref[...] = jnp.zeros_like(s_ref)

        sp = _sigmoid(gp_ref[...] + bg_ref[2:3, :])
        pw = jnp.dot(p_ref[...], wp_ref[...], preferred_element_type=F32)
        mix = jnp.dot(m_ref[...], wo_ref[...], preferred_element_type=F32)
        pre = ALPHA * x_ref[...] + mix + sp * pw
        mu = jnp.mean(pre, -1, keepdims=True)
        cen = pre - mu
        rstd = lax.rsqrt(jnp.mean(cen * cen, -1, keepdims=True) + LN_EPS)
        xhat = cen * rstd
        err = xhat * g_ref[...] + b_ref[...] - t_ref[...]
        dy = err * (1.0 / D_MODEL)
        dxh = dy * g_ref[...]
        dpre = rstd * (dxh - jnp.mean(dxh, -1, keepdims=True) - xhat * jnp.mean(dxh * xhat, -1, keepdims=True))
        dx_ref[...] = ALPHA * dpre
        dp_ref[...] = dpre.astype(BF16)
        dpw_ref[...] = (dpre * sp).astype(BF16)
        dgp = dpre * pw * (sp * (1.0 - sp))
        dgp_ref[...] = dgp.astype(BF16)
        s_ref[0:1, :] += jnp.sum(dy * xhat, 0, keepdims=True)
        s_ref[1:2, :] += jnp.sum(dy, 0, keepdims=True)
        s_ref[2:3, :] += jnp.sum(dgp, 0, keepdims=True)
        s_ref[3:4, :] += jnp.sum(err * err, 0, keepdims=True)

    spec = pl.BlockSpec((None, EW_TM, D_MODEL), lambda b, i: (b, i, 0))
    small = pl.BlockSpec((8, D_MODEL), lambda b, i: (0, 0))
    row = pl.BlockSpec((1, D_MODEL), lambda b, i: (0, 0))
    return _pcall(
        body, name="ln_loss", grid=(nb, seq // EW_TM),
        in_specs=[spec, spec, _whole(w_out16, True), spec, pl.BlockSpec((None, EW_TM, PLE_DIM), lambda b, i: (b, i, 0)),
                  _whole(w_ple16, True), spec, small, row, row],
        out_specs=[spec] * 4 + [small],
        out_shape=[jax.ShapeDtypeStruct((nb, seq, D_MODEL), F32)] + [jax.ShapeDtypeStruct((nb, seq, D_MODEL), BF16)] * 3
        + [jax.ShapeDtypeStruct((8, D_MODEL), F32)],
        compiler_params=_params("arbitrary", "arbitrary"),
    )(x, merged16, w_out16, gp, p16, w_ple16, target, bgate, ln_g, ln_b)


def _adamw(w, g, m, v, name):
    rows, cols = w.shape
    tr = _row_tile(rows, cols, 8, 5 << 19)
    c1 = 1.0 - ADAM_B1 ** ADAM_STEP
    c2 = 1.0 - ADAM_B2 ** ADAM_STEP

    def body(w_ref, g_ref, m_ref, v_ref, d_ref, nm_ref, nv_ref):
        gv = g_ref[...]
        nm = ADAM_B1 * m_ref[...] + (1.0 - ADAM_B1) * gv
        nv = ADAM_B2 * v_ref[...] + (1.0 - ADAM_B2) * (gv * gv)
        d_ref[...] = -ADAM_LR * ((nm / c1) / (jnp.sqrt(nv / c2) + ADAM_EPS) + ADAM_WD * w_ref[...])
        nm_ref[...] = nm
        nv_ref[...] = nv

    spec = pl.BlockSpec((tr, cols), lambda i: (i, 0))
    return _pcall(
        body, name=name, grid=(rows // tr,), in_specs=[spec] * 4, out_specs=[spec] * 3,
        out_shape=[jax.ShapeDtypeStruct(w.shape, F32)] * 3, compiler_params=_params("parallel"),
    )(w, g, m, v)


def _sum_rows(parts, out_dtype, name):
    rows, cols = parts[0].shape
    tr = rows
    for cand in range(16, rows, 16):
        if rows % cand == 0 and cand * cols * 4 <= (1 << 20):
            tr = cand
    n = len(parts)

    def body(*refs):
        acc = refs[0][...].astype(F32)
        for r in refs[1:n]:
            acc = acc + r[...].astype(F32)
        refs[n][...] = acc.astype(out_dtype)

    spec = pl.BlockSpec((tr, cols), lambda i: (i, 0))
    return _pcall(
        body, name=name, grid=(rows // tr,), in_specs=[spec] * n, out_specs=spec,
        out_shape=jax.ShapeDtypeStruct((rows, cols), out_dtype), compiler_params=_params("parallel"),
    )(*parts)


def _place():
    return lax.axis_index("x"), lax.axis_index("y"), lax.axis_index("c")


def _other_chips(x, y):
    return [(1 - x, y), (x, 1 - y), (1 - x, 1 - y)]


def _remote(src, dst, send_sem, recv_sem, to):
    return pltpu.make_async_remote_copy(src_ref=src, dst_ref=dst, send_sem=send_sem, recv_sem=recv_sem,
                                        device_id=to, device_id_type=MESH)


ANY = pl.BlockSpec(memory_space=pl.ANY)
D2D_CHUNK_BYTES = 512 * 1024
ICI_CHUNK_BYTES = 2 * 1024 * 1024


def _row_chunks(rows, row_bytes, chunk_bytes=D2D_CHUNK_BYTES):
    per = max(16, chunk_bytes // row_bytes // 16 * 16)
    return [(s, min(per, rows - s)) for s in range(0, rows, per)]


def _row_tile(rows, cols, align, limit=1 << 21):
    best = None
    for cand in range(align, rows + 1, align):
        if rows % cand == 0 and cand * cols * 4 <= limit:
            best = cand
    return best or rows


def _allgather_pieces(pieces):
    n = len(pieces)
    halves = [_row_chunks(p.shape[0] // 2, p.shape[1] * p.dtype.itemsize, ICI_CHUNK_BYTES) for p in pieces]
    entries = [(a, q, s, m, j) for a in range(n) for q, (s, m) in enumerate(halves[a]) for j in range(3)]
    slot = {(a, q, j): k for k, (a, q, _, _, j) in enumerate(entries)}
    n_ici = len(entries)

    def body(*refs):
        ins, outs = refs[:n], refs[n:2 * n]
        send_sems, recv_sems = refs[2 * n:]
        x, y, c = _place()
        me = 2 * x + y
        sibling = (x, y, 1 - c)
        chips = _other_chips(x, y)

        def landed(a, s, m, j, core):
            half = ins[a].shape[0] // 2
            return outs[a].at[2 * chips[j][0] + chips[j][1], pl.ds(core * half + s, m)]

        sent = []
        for k, (a, q, s, m, j) in enumerate(entries):
            if j < 2:
                half = ins[a].shape[0] // 2
                cp = _remote(ins[a].at[pl.ds(c * half + s, m)], outs[a].at[me, pl.ds(c * half + s, m)],
                             send_sems.at[k], recv_sems.at[k], (*chips[j], c))
                cp.start()
                sent.append(cp)

        def pass_to_sibling(k, blk):
            fw = _remote(blk, blk, send_sems.at[n_ici + k], recv_sems.at[n_ici + k], sibling)
            fw.start()
            sent.append(fw)

        for k, (a, q, s, m, j) in enumerate(entries):
            if j < 2:
                blk = landed(a, s, m, j, c)
                _remote(blk, blk, send_sems.at[k], recv_sems.at[k], (*chips[j], c)).wait_recv()
                first = q < (len(halves[a]) + 1) // 2
                if (j == 0) == first:
                    on = slot[(a, q, 2)]
                    rl = _remote(blk, blk, send_sems.at[on], recv_sems.at[on], (*chips[1 - j], c))
                    rl.start()
                    sent.append(rl)
                pass_to_sibling(k, blk)
        for k, (a, q, s, m, j) in enumerate(entries):
            if j == 2:
                blk = landed(a, s, m, j, c)
                _remote(blk, blk, send_sems.at[k], recv_sems.at[k], (*chips[j], c)).wait_recv()
                pass_to_sibling(k, blk)
        for k, (a, q, s, m, j) in enumerate(entries):
            blk = landed(a, s, m, j, 1 - c)
            _remote(blk, blk, send_sems.at[n_ici + k], recv_sems.at[n_ici + k], sibling).wait_recv()
        for cp in sent:
            cp.wait_send()

    gathered = _pcall(
        body, name="allgather_weights", in_specs=[ANY] * n, out_specs=[ANY] * n,
        out_shape=[jax.ShapeDtypeStruct((4,) + p.shape, p.dtype) for p in pieces],
        scratch_shapes=[pltpu.SemaphoreType.DMA((2 * n_ici,)), pltpu.SemaphoreType.DMA((2 * n_ici,))],
        compiler_params=pltpu.CompilerParams(has_side_effects=True),
    )(*pieces)
    x, y, _ = _place()
    return [lax.dynamic_update_slice(g, p[None], (2 * x + y, 0, 0)) for g, p in zip(gathered, pieces)]


def _sibling_exchange(grads):
    n = len(grads)
    chunks = [_row_chunks(g.shape[1] // 2, g.shape[2] * g.dtype.itemsize) for g in grads]
    n_sem = 4 * sum(len(ch) for ch in chunks)

    def body(*refs):
        ins, gots = refs[:n], refs[n:2 * n]
        send_sems, recv_sems = refs[2 * n:]
        x, y, c = _place()
        sibling = (x, y, 1 - c)
        work = []
        for a in range(n):
            half = ins[a].shape[1] // 2
            for piece in range(4):
                for s, m in chunks[a]:
                    k = len(work)
                    cp = _remote(ins[a].at[piece, pl.ds((1 - c) * half + s, m)], gots[a].at[piece, pl.ds(s, m)],
                                 send_sems.at[k], recv_sems.at[k], sibling)
                    cp.start()
                    work.append(cp)
        for cp in work:
            cp.wait()

    return _pcall(
        body, name="grad_sibling_exchange", in_specs=[ANY] * n, out_specs=[ANY] * n,
        out_shape=[jax.ShapeDtypeStruct((4, g.shape[1] // 2, g.shape[2]), g.dtype) for g in grads],
        scratch_shapes=[pltpu.SemaphoreType.DMA((n_sem,)), pltpu.SemaphoreType.DMA((n_sem,))],
        compiler_params=pltpu.CompilerParams(has_side_effects=True),
    )(*grads)


def _sibling_gather(fulls):
    n = len(fulls)
    chunks = [_row_chunks(f.shape[0] // 2, f.shape[1] * f.dtype.itemsize) for f in fulls]
    n_sem = sum(len(ch) for ch in chunks)

    def body(*refs):
        outs = refs[n:2 * n]
        send_sems, recv_sems = refs[2 * n:]
        x, y, c = _place()
        sibling = (x, y, 1 - c)
        work = []
        for a in range(n):
            h = outs[a].shape[0] // 2
            for s, m in chunks[a]:
                k = len(work)
                mine = outs[a].at[pl.ds(c * h + s, m)]
                cp = _remote(mine, mine, send_sems.at[k], recv_sems.at[k], sibling)
                cp.start()
                work.append((a, s, m, cp))
        for k, (a, s, m, cp) in enumerate(work):
            h = outs[a].shape[0] // 2
            cp.wait_send()
            theirs = outs[a].at[pl.ds((1 - c) * h + s, m)]
            _remote(theirs, theirs, send_sems.at[k], recv_sems.at[k], sibling).wait_recv()

    return _pcall(
        body, name="grad_sibling_gather", in_specs=[ANY] * n, out_specs=[ANY] * n,
        out_shape=[jax.ShapeDtypeStruct(f.shape, f.dtype) for f in fulls],
        input_output_aliases={a: a for a in range(n)},
        scratch_shapes=[pltpu.SemaphoreType.DMA((n_sem,)), pltpu.SemaphoreType.DMA((n_sem,))],
        compiler_params=pltpu.CompilerParams(has_side_effects=True),
    )(*fulls)


def _pair_sum(grad, got, place, name):
    _, rows, cols = grad.shape
    half = rows // 2
    tr = _row_tile(half, cols, 16)

    def body(p_ref, a_ref, b_ref, o_ref):
        o_ref[...] = (a_ref[...].astype(F32) + b_ref[...].astype(F32)).astype(BF16)

    return _pcall(
        body, name=name,
        grid_spec=pltpu.PrefetchScalarGridSpec(
            num_scalar_prefetch=1, grid=(4, half // tr),
            in_specs=[pl.BlockSpec((None, tr, cols), lambda k, i, p: (k, p[1] * (half // tr) + i, 0)),
                      pl.BlockSpec((None, tr, cols), lambda k, i, p: (k, i, 0))],
            out_specs=pl.BlockSpec((None, tr, cols), lambda k, i, p: (k, i, 0))),
        out_shape=jax.ShapeDtypeStruct((4, half, cols), BF16),
        compiler_params=_params("parallel", "parallel"),
    )(place, grad, got)


def _chip_sum(sums, got, place, name):
    _, h, cols = sums.shape
    tr = _row_tile(h, cols, 16)

    def body(p_ref, own_ref, g0, g1, g2, o_ref):
        o_ref[...] = ((own_ref[...].astype(F32) + g0[...].astype(F32)) + g1[...].astype(F32)) + g2[...].astype(F32)

    gspec = lambda j: pl.BlockSpec((None, tr, cols), lambda i, p: (j, i, 0))
    return _pcall(
        body, name=name,
        grid_spec=pltpu.PrefetchScalarGridSpec(
            num_scalar_prefetch=1, grid=(h // tr,),
            in_specs=[pl.BlockSpec((None, tr, cols), lambda i, p: (p[0], i, 0)), gspec(0), gspec(1), gspec(2)],
            out_specs=pl.BlockSpec((tr, cols), lambda i, p: (p[1] * (h // tr) + i, 0))),
        out_shape=jax.ShapeDtypeStruct((2 * h, cols), F32),
        compiler_params=_params("parallel"),
    )(place, sums, got, got, got)


def _allgather8(buf, name):
    rows = buf.shape[0]

    def body(in_ref, out_ref, send_sems, recv_sems):
        x, y, c = _place()
        me = 4 * x + 2 * y + c
        out_ref[me] = in_ref[...]
        work = []
        for rel in range(1, 8):
            fx, fy, fc = (rel >> 2) & 1, (rel >> 1) & 1, rel & 1
            to = (x ^ fx, y ^ fy, c ^ fc)
            cp = _remote(in_ref, out_ref.at[me], send_sems.at[rel - 1], recv_sems.at[rel - 1], to)
            cp.start()
            work.append((cp, 4 * to[0] + 2 * to[1] + to[2]))
        for rel, (cp, frm) in enumerate(work):
            cp.wait_send()
            blk = out_ref.at[frm]
            _remote(blk, blk, send_sems.at[rel], recv_sems.at[rel], (x, y, c)).wait_recv()

    return _pcall(
        body, name=name, in_specs=[pl.BlockSpec(memory_space=pltpu.VMEM)],
        out_specs=pl.BlockSpec(memory_space=pltpu.VMEM),
        out_shape=jax.ShapeDtypeStruct((8, rows, LANE), F32),
        scratch_shapes=[pltpu.SemaphoreType.DMA((7,)), pltpu.SemaphoreType.DMA((7,))],
        compiler_params=pltpu.CompilerParams(has_side_effects=True),
    )(buf)


def _pack_rows(arrs):
    parts = []
    for a in arrs:
        f = a.reshape(-1).astype(F32)
        parts.append(jnp.pad(f, (0, (-f.shape[0]) % LANE)))
    flat = jnp.concatenate(parts)
    rows = -(-flat.shape[0] // LANE)
    rows8 = -(-rows // 8) * 8
    return jnp.pad(flat, (0, rows8 * LANE - flat.shape[0])).reshape(rows8, LANE)


def _unpack_rows(buf, shapes):
    flat = buf.reshape(-1)
    outs, off = [], 0
    for s in shapes:
        n = int(np.prod(s))
        outs.append(flat[off:off + n].reshape(s))
        off += -(-n // LANE) * LANE
    return outs


def _local_grads(x, p, target, wseg, w_br16, w_out16, w_ple16, b_gate, conv_w, conv_b, dt_bias, a_log, d_skip,
                 ssm_norm_w, ln_g, ln_b, rel_bias, finish_dx):
    nb, seq, _ = x.shape
    bmaps = jnp.asarray(_bucket_maps())
    bias = _bias_tables(rel_bias, bmaps)
    bgate8 = jnp.pad(b_gate, ((0, 5), (0, 0)))
    dils = [d for _, d in PATTERNS]

    x16p = _token_orders(x, dils[1:])
    x16 = x16p[0]
    p16 = p.astype(BF16)
    qkv = [_proj(x16p[g], [wseg["qkv%d" % g]], BF16, "proj_qkv%d" % g, True, 2 * MM_TM)[0].reshape(
        nb, dils[g], seq // dils[g], -1) for g in range(3)]
    nat = {}
    for gi, (group, tm) in enumerate(NAT_GROUPS):
        outs = _proj(x16, [wseg[s] for s in group], F32, "proj_nat%d" % gi, True, tm)
        nat.update(zip(group, outs))
    att = [_attn_fwd(qkv[g], bias, g, dils[g], "attn_fwd%d" % g) for g in range(3)]
    oa, o_att, lse = _combine_fwd(att[0][0], att[0][1], att[1:], nat["gatt"])

    conv_wg, conv_bg = _xbc_group_order(conv_w), _xbc_group_order(conv_b)
    act = _conv_fwd(nat["xbc"], conv_wg, conv_bg, "conv_fwd")
    dt_sp, dt_sg = _softplus_sig(nat["dt"], jnp.pad(dt_bias, ((0, 0), (0, LANE - SSM_HEADS))))
    dtg, sgg = _group_lanes(dt_sp), _group_lanes(dt_sg)
    alog_g, dskip_g = _group_lanes(a_log), _group_lanes(d_skip)
    y_ssm, y_all, sprev = _ssd_fwd(act, dtg, nat["z"], alog_g, dskip_g, ssm_norm_w)

    w_bra, w_brb = w_br16[:ATT_OUT], w_br16[ATT_OUT:]
    y_a, = _proj(oa, [w_bra], F32, "proj_ya")
    y_b, = _proj(y_ssm, [w_brb], F32, "proj_yb")
    merged = _merge_fwd(y_a, y_b, nat["gm"], bgate8)

    dx, dpre16, dpw16, dgp16, ln_sums = _ln_loss(x, merged, w_out16, nat["gp"], p16, w_ple16, target, bgate8,
                                                 ln_g, ln_b)
    loss_sum = (0.5 / D_MODEL) * jnp.sum(ln_sums[3])
    dya16, dyb16, dgm16, mg_sums = _merge_bwd(dpre16, w_out16, y_a, y_b, nat["gm"], bgate8)
    doa = _dx([dya16], [w_bra], [], "dx_oa")
    dys = _dx([dyb16], [w_brb], [], "dx_yssm")
    g_w_out, = _dw(merged, [dpre16], BF16, "dw_out")
    g_w_br = jnp.concatenate([_dw(oa, [dya16], BF16, "dw_bra")[0], _dw(y_ssm, [dyb16], BF16, "dw_brb")[0]], axis=0)
    g_w_ple, = _dw(p16, [dpw16], BF16, "dw_ple")

    do_att, dgatt16, own_order = _combine_bwd(doa, nat["gatt"], o_att, lse, dils[1:])
    dseg = {"gatt": dgatt16, "gm": dgm16, "gp": dgp16}
    dbias = []
    for g in range(3):
        cotangent = (do_att, o_att, lse) if g == 0 else (own_order[2 * g - 2], own_order[2 * g - 1])
        dqkv, db = _attn_bwd(qkv[g], bias, g, cotangent, dils[g],
                             "attn_bwd%d" % g)
        dseg["qkv%d" % g] = dqkv.reshape(nb, seq, -1)
        dbias.append(db)
    g_rel = _bias_grad(jnp.concatenate(dbias, axis=0), bmaps)[:, 0, :NUM_BUCKETS].T

    dact, ddtg, dz, ssd_small, g_normw = _ssd_bwd(
        act, dtg, sgg, nat["z"], y_all, dys, sprev, alog_g, dskip_g, ssm_norm_w)
    dseg["z"] = dz
    dseg["dt"] = jnp.pad(_ungroup_lanes(ddtg), ((0, 0), (0, 0), (0, LANE - SSM_HEADS)))
    dpre, conv_sums = _conv_bwd_pre(dact, nat["xbc"], conv_wg, conv_bg, "conv_bwd")
    dseg["xbc"] = _conv_bwd_x(dpre, conv_wg, "conv_bwd_x")
    csum = _xbc_reference_order(conv_sums)

    dx_own = [_dx([dseg["qkv%d" % g]], [wseg["qkv%d" % g]], [], "dx_qkv%d" % g, True).reshape(
        nb, dils[g], seq // dils[g], D_MODEL) for g in (1, 2)]
    dwseg = {"qkv%d" % g: _dw(x16p[g], [dseg["qkv%d" % g]], BF16, "dw_qkv%d" % g, True)[0] for g in range(3)}
    for gi, group in enumerate(DW_GROUPS):
        dwseg.update(zip(group, _dw(x16, [dseg[s] for s in group], BF16, "dw_nat%d" % gi, True)))
    names = ["qkv0"] + [s for group, _ in NAT_GROUPS for s in group]
    dx = finish_dx([dseg[s] for s in names], [wseg[s] for s in names], [dx], dx_own, dwseg, g_w_br, g_w_out, g_w_ple)

    small = dict(
        b_gate=jnp.stack([mg_sums[0], mg_sums[1], ln_sums[2]]),
        conv_w=csum[0:4], conv_b=csum[4:5],
        dt_bias=_ungroup_lanes(ssd_small[:, 2:3, :]), a_log=_ungroup_lanes(ssd_small[:, 0:1, :]),
        d_skip=_ungroup_lanes(ssd_small[:, 1:2, :]), ssm_norm_w=g_normw,
        ln_g=ln_sums[0:1], ln_b=ln_sums[1:2], rel_bias=g_rel)
    return loss_sum, dx, small


DX_TM = 256
SMALL_ORDER = ("b_gate", "conv_w", "conv_b", "dt_bias", "a_log", "d_skip", "ssm_norm_w", "ln_g", "ln_b", "rel_bias")
SMALL_FULL_SHAPES = dict(b_gate=(3, 1024), conv_w=(4, 3072), conv_b=(1, 3072), dt_bias=(1, 32), a_log=(1, 32),
                         d_skip=(1, 32), ssm_norm_w=(1, 2048), ln_g=(1, 1024), ln_b=(1, 1024), rel_bias=(32, 36))


def kernel(x, p, w_in, b_gate, conv_w, conv_b, dt_bias, a_log, d_skip, ssm_norm_w, w_branch, w_out, w_ple, ln_g, ln_b, rel_bias, loss_target, m_w_in, m_b_gate, m_conv_w, m_conv_b, m_dt_bias, m_a_log, m_d_skip, m_ssm_norm_w, m_w_branch, m_w_out, m_w_ple, m_ln_g, m_ln_b, m_rel_bias, v_w_in, v_b_gate, v_conv_w, v_conv_b, v_dt_bias, v_a_log, v_d_skip, v_ssm_norm_w, v_w_branch, v_w_out, v_w_ple, v_ln_g, v_ln_b, v_rel_bias):
    cx, cy, cc = _place()
    chip = 2 * cx + cy
    dev = 4 * cx + 2 * cy + cc

    w_in_t = jnp.transpose(w_in[0])
    win16 = _shard_to_window(w_in_t, chip)
    g_win, g_br, g_out, g_ple = _allgather_pieces(
        [win16, w_branch[0].astype(BF16), w_out[0].astype(BF16), w_ple[0].astype(BF16)])
    wseg = _assemble(g_win)
    w_br16 = g_br.reshape(4 * 704, D_MODEL)
    w_out16 = g_out.reshape(D_MODEL, D_MODEL)
    w_ple16 = jnp.transpose(g_ple, (1, 0, 2)).reshape(PLE_DIM, D_MODEL)
    shards = _allgather8(_pack_rows([b_gate[0], conv_w[0]]), "allgather_small_params")
    per_chip = [_unpack_rows(shards[2 * k], [(3, 256), (4, 768)]) for k in range(4)]
    b_gate_full = jnp.concatenate([pc[0] for pc in per_chip], axis=1)
    conv_w_full = jnp.concatenate([pc[1] for pc in per_chip], axis=1)

    place = jnp.stack([chip, cc]).astype(jnp.int32)
    reduced = []

    def finish_dx(dhs, ws, accs, own_order_accs, dwseg, d_br, d_out, d_ple):
        grads = [_pack(dwseg), d_br.reshape(4, 704, D_MODEL), d_out.reshape(4, 256, D_MODEL),
                 jnp.transpose(d_ple.reshape(PLE_DIM, 4, 256), (1, 0, 2))]
        got = _sibling_exchange(grads)
        chip_sums = [_pair_sum(g, t, place, "grad_pair_sum_%d" % i) for i, (g, t) in enumerate(zip(grads, got))]
        dx, others = _dx(dhs, ws, accs, "dx_w_in_and_grad_chip_scatter", True, DX_TM, chip_sums, own_order_accs)
        fulls = [_chip_sum(s, t, place, "grad_chip_sum_%d" % i) for i, (s, t) in enumerate(zip(chip_sums, others))]
        reduced.extend(_sibling_gather(fulls))
        return dx

    loss_sum, grad_x, small = _local_grads(
        x, p[0], loss_target, wseg, w_br16, w_out16, w_ple16, b_gate_full, conv_w_full, conv_b, dt_bias, a_log,
        d_skip, ssm_norm_w, ln_g, ln_b, rel_bias, finish_dx)
    big = reduced
    g_w_in = _window_to_shard(big[0], chip)
    g_w_branch, g_w_out, g_w_ple = big[1], big[2], big[3]
    parts = _allgather8(_pack_rows([small[n] for n in SMALL_ORDER] + [loss_sum.reshape(1, 1)]),
                        "allgather_small_grads")
    small_sum = _sum_rows([parts[i] for i in range(8)], F32, "small_grad_sum")
    *reduced_small, loss = _unpack_rows(small_sum, [SMALL_FULL_SHAPES[n] for n in SMALL_ORDER] + [(1, 1)])
    loss = loss.reshape(())
    sg = dict(zip(SMALL_ORDER, reduced_small))
    sg["b_gate"] = lax.dynamic_slice_in_dim(sg["b_gate"], chip * 256, 256, axis=1)
    sg["conv_w"] = lax.dynamic_slice_in_dim(sg["conv_w"], chip * 768, 768, axis=1)
    del dev

    upd = {}
    upd["w_in"] = [jnp.transpose(t) for t in _adamw(w_in_t, g_w_in, jnp.transpose(m_w_in[0]),
                                                      jnp.transpose(v_w_in[0]), "adamw_w_in")]
    upd["w_branch"] = _adamw(w_branch[0], g_w_branch, m_w_branch[0], v_w_branch[0], "adamw_w_branch")
    upd["w_out"] = _adamw(w_out[0], g_w_out, m_w_out[0], v_w_out[0], "adamw_w_out")
    upd["w_ple"] = _adamw(w_ple[0], g_w_ple, m_w_ple[0], v_w_ple[0], "adamw_w_ple")
    small_w = dict(b_gate=b_gate, conv_w=conv_w, conv_b=conv_b, dt_bias=dt_bias, a_log=a_log, d_skip=d_skip,
                   ssm_norm_w=ssm_norm_w, ln_g=ln_g, ln_b=ln_b, rel_bias=rel_bias)
    small_m = dict(b_gate=m_b_gate, conv_w=m_conv_w, conv_b=m_conv_b, dt_bias=m_dt_bias, a_log=m_a_log,
                   d_skip=m_d_skip, ssm_norm_w=m_ssm_norm_w, ln_g=m_ln_g, ln_b=m_ln_b, rel_bias=m_rel_bias)
    small_v = dict(b_gate=v_b_gate, conv_w=v_conv_w, conv_b=v_conv_b, dt_bias=v_dt_bias, a_log=v_a_log,
                   d_skip=v_d_skip, ssm_norm_w=v_ssm_norm_w, ln_g=v_ln_g, ln_b=v_ln_b, rel_bias=v_rel_bias)
    shapes = [small_w[n].shape for n in SMALL_ORDER]
    s_delta, s_m, s_v = _adamw(_pack_rows([small_w[n] for n in SMALL_ORDER]), _pack_rows([sg[n] for n in SMALL_ORDER]),
                               _pack_rows([small_m[n] for n in SMALL_ORDER]), _pack_rows([small_v[n] for n in SMALL_ORDER]),
                               "adamw_small")
    for i, n in enumerate(SMALL_ORDER):
        upd[n] = tuple(_unpack_rows(t, shapes)[i] for t in (s_delta, s_m, s_v))
        sg[n] = sg[n].reshape(small_w[n].shape)

    order = ("w_in", "b_gate", "conv_w", "conv_b", "dt_bias", "a_log", "d_skip", "ssm_norm_w", "w_branch", "w_out",
             "w_ple", "ln_g", "ln_b", "rel_bias")
    grads = dict(sg, w_in=jnp.transpose(g_w_in)[None],w_branch=g_w_branch[None], w_out=g_w_out[None], w_ple=g_w_ple[None])
    lead = lambda n, t: t[None] if n in ("w_in", "w_branch", "w_out", "w_ple") else t
    return (loss, grad_x, *[grads[n] for n in order], *[lead(n, upd[n][0]) for n in order],
            *[lead(n, upd[n][1]) for n in order], *[lead(n, upd[n][2]) for n in order])
```

```python
import math

import numpy as np
import jax
import jax.numpy as jnp
from jax import lax
from jax.experimental import pallas as pl
from jax.experimental.pallas import tpu as pltpu

F32, BF16 = jnp.float32, jnp.bfloat16

D_MODEL = 1024
HEAD_DIM = 64
GROUP_HEADS = 12
ATT_OUT = GROUP_HEADS * HEAD_DIM
PATTERNS = ((128, 1), (512, 4), (2048, 16))
BAND = 128
NUM_BUCKETS = 32
MAX_DISTANCE = 2048
D_INNER = 2048
SSM_HEADS = 32
SSM_GROUPS = 4
GROUP_SSM_HEADS = SSM_HEADS // SSM_GROUPS
D_STATE = 128
CHUNK = 128
PLE_DIM = 256
ALPHA = 2.0 ** 0.25
LN_EPS = 1e-5
RMS_EPS = 1e-5
ADAM_LR, ADAM_B1, ADAM_B2, ADAM_EPS, ADAM_WD, ADAM_STEP = 0.001, 0.9, 0.999, 1e-08, 0.01, 10
NEG = -1e30

QKV_W = 3 * ATT_OUT
IN_COLS = 15904
SHARD_COLS = IN_COLS // 4
DT_COL = 12800
ROW_TILE = 16
WIN_ROWS = 4000


def _win_offset(k):
    return (k * SHARD_COLS) % ROW_TILE


def _win_start(k):
    return k * SHARD_COLS - _win_offset(k)

VMEM_LIMIT_BYTES = 56 * 1024 * 1024
LANE = 128
MESH = pl.DeviceIdType.MESH
NT = (((1,), (1,)), ((), ()))
TN = (((0,), (0,)), ((), ()))


def _pcall(body, **kw):
    return pl.pallas_call(body, **kw)


def _params(*sem):
    return pltpu.CompilerParams(dimension_semantics=sem, vmem_limit_bytes=VMEM_LIMIT_BYTES)


def _sigmoid(v):
    return jax.nn.sigmoid(v)


MM_TM = 512


def _tok_spec(tm, width):
    return pl.BlockSpec((None, tm, width), lambda b, i: (b, i, 0))


def _whole(arr, single_buffer=False):
    mode = dict(pipeline_mode=pl.Buffered(1)) if single_buffer else {}
    return pl.BlockSpec(arr.shape, lambda b, i: (0,) * arr.ndim, **mode)


def _proj(a3, ws, out_dtype, name, w_rows_are_outputs=False, tm=MM_TM):
    nb, seq, kdim = a3.shape
    nw = len(ws)
    widths = [w.shape[0] if w_rows_are_outputs else w.shape[1] for w in ws]

    def body(*refs):
        a = refs[0][...].astype(BF16)
        for w_ref, o_ref in zip(refs[1:1 + nw], refs[1 + nw:]):
            if w_rows_are_outputs:
                v = lax.dot_general(a, w_ref[...], NT, preferred_element_type=F32)
            else:
                v = jnp.dot(a, w_ref[...], preferred_element_type=F32)
            o_ref[...] = v.astype(out_dtype)

    return _pcall(
        body, name=name, grid=(nb, seq // tm),
        in_specs=[_tok_spec(tm, kdim)] + [_whole(w, True) for w in ws],
        out_specs=[_tok_spec(tm, n) for n in widths],
        out_shape=[jax.ShapeDtypeStruct((nb, seq, n), out_dtype) for n in widths],
        compiler_params=_params("parallel", "parallel"),
    )(a3, *ws)


def _dx(dhs, ws, accs, name, w_rows_are_outputs=False, tm=MM_TM, scatter=None, own_order_accs=()):
    nb, seq, _ = dhs[0].shape
    nd, nacc, npa = len(dhs), len(accs), len(own_order_accs)
    kout = ws[0].shape[1] if w_rows_are_outputs else ws[0].shape[0]
    sums = scatter or []
    ns = len(sums)
    chunks = [_row_chunks(s.shape[1], s.shape[2] * s.dtype.itemsize, ICI_CHUNK_BYTES) for s in sums]
    n_sem = 3 * sum(len(ch) for ch in chunks)
    grid = (nb, seq // tm)
    ntile = kout // LANE if npa else 0

    def body(*refs):
        n_in = 2 * nd + nacc + npa
        sum_refs, o_ref, got_refs = refs[n_in:n_in + ns], refs[n_in + ns], refs[n_in + ns + 1:n_in + 2 * ns + 1]
        tile_refs = refs[n_in + 2 * ns + 1:n_in + 2 * ns + 1 + ntile]

        def copies():
            send_sems, recv_sems = refs[-2], refs[-1]
            x, y, c = _place()
            out = []
            for a in range(ns):
                for s, m in chunks[a]:
                    for j, (cx, cy) in enumerate(_other_chips(x, y)):
                        k = len(out)
                        out.append(_remote(sum_refs[a].at[2 * cx + cy, pl.ds(s, m)], got_refs[a].at[j, pl.ds(s, m)],
                                           send_sems.at[k], recv_sems.at[k], (cx, cy, c)))
            return out

        if ns:
            @pl.when((pl.program_id(0) == 0) & (pl.program_id(1) == 0))
            def _():
                for cp in copies():
                    cp.start()

        v = None
        for dh_ref, w_ref in zip(refs[:nd], refs[nd:2 * nd]):
            dh = dh_ref[...].astype(BF16)
            if w_rows_are_outputs:
                t = jnp.dot(dh, w_ref[...], preferred_element_type=F32)
            else:
                t = lax.dot_general(dh, w_ref[...], NT, preferred_element_type=F32)
            v = t if v is None else v + t
        for a_ref in refs[2 * nd:2 * nd + nacc]:
            v = v + a_ref[...]
        for p_ref in refs[2 * nd + nacc:n_in]:
            v = v + _natural_rows(p_ref, tile_refs)
        o_ref[...] = v

        if ns:
            @pl.when((pl.program_id(0) == grid[0] - 1) & (pl.program_id(1) == grid[1] - 1))
            def _():
                for cp in copies():
                    cp.wait()

    out = _pcall(
        body, name=name, grid=grid,
        in_specs=[_tok_spec(tm, dh.shape[-1]) for dh in dhs] + [_whole(w, True) for w in ws]
        + [_tok_spec(tm, kout)] * nacc
        + [pl.BlockSpec((None, p.shape[1], tm // p.shape[1], kout), lambda b, i: (b, 0, i, 0)) for p in own_order_accs]
        + [ANY] * ns,
        out_specs=[_tok_spec(tm, kout)] + [ANY] * ns,
        out_shape=[jax.ShapeDtypeStruct((nb, seq, kout), F32)]
        + [jax.ShapeDtypeStruct((3,) + s.shape[1:], s.dtype) for s in sums],
        input_output_aliases={2 * nd: 0} if nacc else {},
        scratch_shapes=[pltpu.VMEM((tm, LANE), F32)] * ntile
        + ([pltpu.SemaphoreType.DMA((n_sem,)), pltpu.SemaphoreType.DMA((n_sem,))] if ns else []),
        compiler_params=pltpu.CompilerParams(
            dimension_semantics=("arbitrary", "arbitrary") if ns else ("parallel", "parallel"),
            vmem_limit_bytes=VMEM_LIMIT_BYTES, has_side_effects=bool(ns)),
    )(*dhs, *ws, *accs, *own_order_accs, *sums)
    return (out[0], list(out[1:])) if ns else out[0]


def _dw(a3, dhs, out_dtype, name, rows_are_outputs=False):
    nb, seq, kdim = a3.shape
    nd = len(dhs)
    grid = (nb, seq // MM_TM)
    shapes = [(dh.shape[-1], kdim) if rows_are_outputs else (kdim, dh.shape[-1]) for dh in dhs]

    def body(*refs):
        b, i = pl.program_id(0), pl.program_id(1)
        dh_refs, o_refs, acc_refs = refs[1:1 + nd], refs[1 + nd:1 + 2 * nd], refs[1 + 2 * nd:]

        @pl.when((b == 0) & (i == 0))
        def _():
            for acc_ref in acc_refs:
                acc_ref[...] = jnp.zeros_like(acc_ref)

        a = refs[0][...].astype(BF16)
        for dh_ref, acc_ref in zip(dh_refs, acc_refs):
            dh = dh_ref[...].astype(BF16)
            acc_ref[...] += lax.dot_general(*((dh, a) if rows_are_outputs else (a, dh)), TN,
                                            preferred_element_type=F32)

        @pl.when((b == grid[0] - 1) & (i == grid[1] - 1))
        def _():
            for o_ref, acc_ref in zip(o_refs, acc_refs):
                o_ref[...] = acc_ref[...].astype(out_dtype)

    return _pcall(
        body, name=name, grid=grid,
        in_specs=[_tok_spec(MM_TM, kdim)] + [_tok_spec(MM_TM, dh.shape[-1]) for dh in dhs],
        out_specs=[pl.BlockSpec(s, lambda b, i: (0, 0)) for s in shapes],
        out_shape=[jax.ShapeDtypeStruct(s, out_dtype) for s in shapes],
        scratch_shapes=[pltpu.VMEM(s, F32) for s in shapes],
        compiler_params=_params("arbitrary", "arbitrary"),
    )(a3, *dhs)


def _qkv_rows(g):
    return [(part * QKV_W + g * ATT_OUT + hp * LANE, LANE) for hp in range(ATT_OUT // LANE) for part in range(3)]


XBC_START = 3 * QKV_W + ATT_OUT + D_INNER
GROUP_CH = GROUP_SSM_HEADS * HEAD_DIM
XBC_GROUP = GROUP_CH + 2 * D_STATE
CONV_DIM = SSM_GROUPS * XBC_GROUP


def _xbc_ranges():
    out = []
    for g in range(SSM_GROUPS):
        out += [(g * GROUP_CH, GROUP_CH), (D_INNER + g * D_STATE, D_STATE),
                (D_INNER + SSM_GROUPS * D_STATE + g * D_STATE, D_STATE)]
    return out


def _xbc_group_order(t):
    return jnp.concatenate([t[..., s:s + n] for s, n in _xbc_ranges()], axis=-1)


def _xbc_reference_order(t):
    g = lambda off, n: [t[..., k * XBC_GROUP + off:k * XBC_GROUP + off + n] for k in range(SSM_GROUPS)]
    return jnp.concatenate(g(0, GROUP_CH) + g(GROUP_CH, D_STATE) + g(GROUP_CH + D_STATE, D_STATE), axis=-1)


def _segments():
    one = lambda name, start, rows: (name, [(start, rows)], max(rows, LANE))
    return [("qkv%d" % g, _qkv_rows(g), QKV_W) for g in range(3)] + [
        one("gatt", 3 * QKV_W, ATT_OUT), one("z", 3 * QKV_W + ATT_OUT, D_INNER),
        ("xbc", [(XBC_START + s, n) for s, n in _xbc_ranges()], CONV_DIM), one("dt", DT_COL, SSM_HEADS),
        one("gm", DT_COL + SSM_HEADS, 2 * D_MODEL), one("gp", DT_COL + SSM_HEADS + 2 * D_MODEL, D_MODEL)]


LAYOUT_TC = 256
NAT_GROUPS = ((("gatt", "z", "dt", "gp"), 512), (("xbc", "gm"), 512))
DW_GROUPS = (("gatt", "z", "dt", "gp"), ("xbc",), ("gm",))


def _assemble(win):
    segs = _segments()

    def body(win_ref, *outs):
        def pieces(start, rows):
            t, end = start, start + rows
            while t < end:
                k = min(t // SHARD_COLS, 3)
                shard_end = (k + 1) * SHARD_COLS
                if k < 3 and shard_end % ROW_TILE and t == shard_end - shard_end % ROW_TILE:
                    lo = t - _win_start(k)
                    yield win_ref[k, lo:lo + ROW_TILE, :] + win_ref[k + 1, 0:ROW_TILE, :]
                    t += ROW_TILE
                    continue
                upto = min(end, shard_end - shard_end % ROW_TILE if k < 3 else end)
                yield win_ref[k, t - _win_start(k):upto - _win_start(k), :]
                t = upto

        for (_, ranges, total), o_ref in zip(segs, outs):
            off = 0
            for start, rows in ranges:
                for part in pieces(start, rows):
                    o_ref[off:off + part.shape[0], :] = part
                    off += part.shape[0]
            if off < total:
                o_ref[off:total, :] = jnp.zeros((total - off, o_ref.shape[1]), BF16)

    outs = _pcall(
        body, name="assemble_w_in", grid=(D_MODEL // LAYOUT_TC,),
        in_specs=[pl.BlockSpec((4, WIN_ROWS, LAYOUT_TC), lambda i: (0, 0, i))],
        out_specs=[pl.BlockSpec((total, LAYOUT_TC), lambda i: (0, i)) for _, _, total in segs],
        out_shape=[jax.ShapeDtypeStruct((total, D_MODEL), BF16) for _, _, total in segs],
        compiler_params=_params("parallel"),
    )(win)
    return {name: o for (name, _, _), o in zip(segs, outs)}


def _pack(dsegs):
    segs = _segments()

    def body(*refs):
        ins, o_ref = refs[:-1], refs[-1]
        tail = IN_COLS - _win_start(3)
        o_ref[3, tail:, :] = jnp.zeros((WIN_ROWS - tail, o_ref.shape[2]), BF16)
        for (_, ranges, _), s_ref in zip(segs, ins):
            off = 0
            for start, rows in ranges:
                for k in range(4):
                    lo = _win_start(k)
                    a, b = max(start, lo), min(start + rows, lo + WIN_ROWS)
                    if a < b:
                        o_ref[k, a - lo:b - lo, :] = s_ref[off + a - start:off + b - start, :]
                off += rows

    return _pcall(
        body, name="pack_dw_in", grid=(D_MODEL // LAYOUT_TC,),
        in_specs=[pl.BlockSpec((total, LAYOUT_TC), lambda i: (0, i)) for _, _, total in segs],
        out_specs=pl.BlockSpec((4, WIN_ROWS, LAYOUT_TC), lambda i: (0, 0, i)),
        out_shape=jax.ShapeDtypeStruct((4, WIN_ROWS, D_MODEL), BF16),
        compiler_params=_params("parallel"),
    )(*[dsegs[name] for name, _, _ in segs])


def _shard_to_window(shard_t, k):
    def at(off):
        return lambda w: jnp.pad(w.astype(BF16), ((off, WIN_ROWS - SHARD_COLS - off), (0, 0)))

    return lax.cond(k % 2 == 1, at(_win_offset(1)), at(_win_offset(0)), shard_t)


def _window_to_shard(win, k):
    return lax.dynamic_slice(win, ((k % 2) * _win_offset(1), 0), (SHARD_COLS, D_MODEL))


def _bucket_maps():
    qi = np.arange(8)[:, None]
    kj = np.arange(2 * BAND)[None, :]
    delta = qi + BAND - kj
    maps = []
    for window, dil in PATTERNS:
        valid = (delta >= 0) & (delta <= window // dil)
        dist = np.maximum(delta, 0) * dil
        max_exact = NUM_BUCKETS // 2
        d_f = np.maximum(dist, 1).astype(np.float32)
        large = max_exact + (np.log(d_f / np.float32(max_exact)) / np.float32(math.log(MAX_DISTANCE / max_exact))
                             * np.float32(NUM_BUCKETS - max_exact)).astype(np.int32)
        large = np.minimum(large, NUM_BUCKETS - 1)
        bucket = np.where(dist < max_exact, dist, large)
        maps.append(np.where(valid, bucket, -1).astype(np.int32))
    return np.stack(maps)


def _bias_tables(rel_bias, bmaps):
    def body(rb_ref, bm_ref, o_ref):
        g = pl.program_id(0)
        bm = bm_ref[...]
        for hh in range(GROUP_HEADS):
            acc = jnp.full(bm.shape, NEG, F32)
            for b in range(NUM_BUCKETS):
                acc = jnp.where(bm == b, rb_ref[b, g * GROUP_HEADS + hh], acc)
            for a in range(BAND // 8):
                o_ref[hh, 8 * a:8 * a + 8, :] = acc if a == 0 else pltpu.roll(acc, 8 * a, 1)

    return _pcall(
        body, name="bias_tables", grid=(3,),
        in_specs=[pl.BlockSpec(memory_space=pltpu.SMEM),
                  pl.BlockSpec((None, 8, 2 * BAND), lambda g: (g, 0, 0))],
        out_specs=pl.BlockSpec((GROUP_HEADS, BAND, 2 * BAND), lambda g: (g, 0, 0)),
        out_shape=jax.ShapeDtypeStruct((3 * GROUP_HEADS, BAND, 2 * BAND), F32),
        compiler_params=_params("parallel"),
    )(rel_bias, bmaps)


def _bias_grad(dbias, bmaps):
    def body(db_ref, bm_ref, o_ref):
        bm = bm_ref[...]
        lane = lax.broadcasted_iota(jnp.int32, (1, LANE), 1)
        for hh in range(GROUP_HEADS):
            db = db_ref[hh, 0:8, :]
            for a in range(1, BAND // 8):
                db = db + pltpu.roll(db_ref[hh, 8 * a:8 * a + 8, :], 2 * BAND - 8 * a, 1)
            vec = jnp.zeros((1, LANE), F32)
            for b in range(NUM_BUCKETS):
                s = jnp.sum(jnp.where(bm == b, db, 0.0), keepdims=True)
                vec = jnp.where(lane == b, s, vec)
            o_ref[hh] = vec

    return _pcall(
        body, name="bias_grad", grid=(3,),
        in_specs=[pl.BlockSpec((GROUP_HEADS, BAND, 2 * BAND), lambda g: (g, 0, 0)),
                  pl.BlockSpec((None, 8, 2 * BAND), lambda g: (g, 0, 0))],
        out_specs=pl.BlockSpec((GROUP_HEADS, 1, LANE), lambda g: (g, 0, 0)),
        out_shape=jax.ShapeDtypeStruct((3 * GROUP_HEADS, 1, LANE), F32),
        compiler_params=_params("parallel"),
    )(dbias, bmaps)


def _rows(n):
    if isinstance(n, int):
        return pl.ds(n * BAND, BAND)
    return pl.ds(pl.multiple_of(n * BAND, BAND), BAND)


def _for_blocks(blocks, nblk, per, carry):
    carry = blocks([0], carry, False)
    start = 1 + (nblk - 1) % per
    for n in range(1, start):
        carry = blocks([n], carry, True)
    trips = (nblk - start) // per
    if trips > 0:
        carry = lax.fori_loop(
            0, trips, lambda t, c: blocks([start + t * per + u for u in range(per)], c, True), carry)
    return carry


def _pairs_per_step(d):
    return {1: 3, 4: 6, 16: 6}[d]


def _bias_spec(group, hps):
    first = group * GROUP_HEADS // (2 * hps)
    return pl.BlockSpec((2 * hps, BAND, 2 * BAND), lambda hp, b, r: (first + hp, 0, 0))


def _attn_fwd(qkv4, bias, group, d, name):
    nb, _, sub, _ = qkv4.shape
    nblk = sub // BAND
    scale = HEAD_DIM ** -0.5
    npair = ATT_OUT // LANE
    hps = _pairs_per_step(d)
    compact = d > 1

    def body(qkv_ref, bias_ref, o_ref, l_ref):
        def blocks(ns, carry, with_prev):
            chains = [(bi, i, h) for bi in range(len(ns)) for i in range(hps) for h in range(2)]
            first_head = lax.broadcasted_iota(jnp.int32, (BAND, LANE), 1) < HEAD_DIM
            pair = lambda n, i, part: qkv_ref[_rows(n), (3 * i + part) * LANE:(3 * i + part + 1) * LANE]
            scores = []
            for bi, i, h in chains:
                n = ns[bi]
                qp = pair(n, i, 0) * scale
                q = jnp.where(first_head if h == 0 else jnp.logical_not(first_head), qp, jnp.zeros_like(qp))
                s_c = lax.dot_general(q, pair(n, i, 1), NT, preferred_element_type=F32) + bias_ref[2 * i + h, :, BAND:]
                s_p = None
                if with_prev:
                    s_p = lax.dot_general(q, pair(n - 1, i, 1), NT,
                                          preferred_element_type=F32) + bias_ref[2 * i + h, :, :BAND]
                scores.append((s_c, s_p))
            probs = []
            for s_c, s_p in scores:
                m = jnp.max(s_c, -1, keepdims=True)
                if with_prev:
                    m = jnp.maximum(m, jnp.max(s_p, -1, keepdims=True))
                e_c = jnp.exp(s_c - m)
                den = jnp.sum(e_c, -1, keepdims=True)
                e_p = None
                if with_prev:
                    e_p = jnp.exp(s_p - m)
                    den = den + jnp.sum(e_p, -1, keepdims=True)
                    e_p = e_p.astype(BF16)
                probs.append((e_c.astype(BF16), e_p, den, m))
            outs = {}
            for (bi, i, h), (e_c, e_p, den, m) in zip(chains, probs):
                n = ns[bi]
                acc = jnp.dot(e_c, pair(n, i, 2), preferred_element_type=F32)
                if with_prev:
                    acc = acc + jnp.dot(e_p, pair(n - 1, i, 2), preferred_element_type=F32)
                outs[(bi, i, h)] = (acc / den, m + jnp.log(den))
            lane = lax.broadcasted_iota(jnp.int32, (BAND, LANE), 1)
            for bi, n in enumerate(ns):
                per_head = jnp.zeros((BAND, LANE), F32)
                for i in range(hps):
                    o_ref[_rows(n), i * LANE:(i + 1) * LANE] = jnp.where(first_head, outs[(bi, i, 0)][0],
                                                                         outs[(bi, i, 1)][0])
                    if compact:
                        for h in range(2):
                            per_head = jnp.where(lane == 2 * i + h, outs[(bi, i, h)][1], per_head)
                    else:
                        l_ref[_rows(n), i * LANE:(i + 1) * LANE] = jnp.where(first_head, outs[(bi, i, 0)][1],
                                                                             outs[(bi, i, 1)][1])
                if compact:
                    l_ref[_rows(n), :] = per_head
            return carry

        _for_blocks(blocks, nblk, 2 if hps == 1 else 1, 0)

    in_specs = [pl.BlockSpec((None, None, sub, 3 * LANE * hps), lambda hp, b, r: (b, r, 0, hp)),
                _bias_spec(group, hps)]
    if compact:
        return _pcall(
            body, name=name, grid=(1, nb, d), in_specs=in_specs,
            out_specs=[pl.BlockSpec((None, None, sub, ATT_OUT), lambda hp, b, r: (b, r, 0, 0)),
                       pl.BlockSpec((None, None, sub, LANE), lambda hp, b, r: (b, r, 0, 0))],
            out_shape=[jax.ShapeDtypeStruct((nb, d, sub, ATT_OUT), F32), jax.ShapeDtypeStruct((nb, d, sub, LANE), F32)],
            compiler_params=_params("parallel", "parallel", "parallel"),
        )(qkv4, bias)
    ospec = pl.BlockSpec((None, sub, hps * LANE), lambda hp, b, r: (b, 0, r * (npair // hps) + hp))
    return _pcall(
        body, name=name, grid=(npair // hps, nb, d), in_specs=in_specs, out_specs=[ospec, ospec],
        out_shape=[jax.ShapeDtypeStruct((nb, sub, d * ATT_OUT), F32)] * 2,
        compiler_params=_params("parallel", "parallel", "parallel"),
    )(qkv4, bias)


STAT_LSE_LANE = 16


def _attn_bwd(qkv4, bias, group, cotangent, d, name):
    nb, _, sub, _ = qkv4.shape
    nblk = sub // BAND
    scale = HEAD_DIM ** -0.5
    npair = ATT_OUT // LANE
    hps = _pairs_per_step(d)
    compact = d > 1

    def body(qkv_ref, bias_ref, *rest):
        do_ref, dqkv_ref, db_ref = rest[0], rest[-2], rest[-1]
        b, r = pl.program_id(1), pl.program_id(2)

        @pl.when((b == 0) & (r == 0))
        def _():
            db_ref[...] = jnp.zeros_like(db_ref)

        def blocks(ns, carry, with_prev):
            sides = (0, 1) if with_prev else (0,)
            chains = [(bi, i, h, sd) for bi in range(len(ns)) for i in range(hps) for h in range(2) for sd in sides]
            first_head = lax.broadcasted_iota(jnp.int32, (BAND, LANE), 1) < HEAD_DIM
            own = lambda h, t: jnp.where(first_head if h == 0 else jnp.logical_not(first_head), t, jnp.zeros_like(t))
            pair = lambda rows, i, part: qkv_ref[rows, (3 * i + part) * LANE:(3 * i + part + 1) * LANE]
            key_rows = lambda bi, sd: _rows(ns[bi] - sd)
            qs = {}
            for bi in range(len(ns)):
                for i in range(hps):
                    q_pair = pair(_rows(ns[bi]), i, 0) * scale
                    do = do_ref[_rows(ns[bi]), i * LANE:(i + 1) * LANE]
                    do16 = do.astype(BF16)
                    for h in range(2):
                        if compact:
                            st_ref, head = rest[1], 2 * i + h
                            ebar = st_ref[_rows(ns[bi]), head:head + 1]
                            lcol = st_ref[_rows(ns[bi]), STAT_LSE_LANE + head:STAT_LSE_LANE + head + 1]
                        else:
                            ebar = jnp.sum(own(h, do * rest[1][_rows(ns[bi]), i * LANE:(i + 1) * LANE]), -1, keepdims=True)
                            lcol = rest[2][_rows(ns[bi]), i * LANE + h * HEAD_DIM:i * LANE + h * HEAD_DIM + 1]
                        qs[(bi, i, h)] = (own(h, q_pair), q_pair, own(h, do16), do16, ebar, lcol)
            raw = []
            for bi, i, h, sd in chains:
                q, _, do_h, _, _, _ = qs[(bi, i, h)]
                bias_blk = bias_ref[2 * i + h, :, :BAND] if sd else bias_ref[2 * i + h, :, BAND:]
                s = lax.dot_general(q, pair(key_rows(bi, sd), i, 1), NT, preferred_element_type=F32) + bias_blk
                dp = lax.dot_general(do_h, pair(key_rows(bi, sd), i, 2), NT, preferred_element_type=F32)
                raw.append((s, dp))
            soft = []
            for (bi, i, h, sd), (s, dp) in zip(chains, raw):
                ebar, lcol = qs[(bi, i, h)][4:]
                p = jnp.exp(s - lcol)
                ds = p * (dp - ebar)
                if sd:
                    db_ref[2 * i + h, :, :BAND] += ds
                else:
                    db_ref[2 * i + h, :, BAND:] += ds
                soft.append((p.astype(BF16), ds.astype(BF16)))
            grads = {}
            for (bi, i, h, sd), (p16, ds16) in zip(chains, soft):
                _, q_pair, _, do16 = qs[(bi, i, h)][:4]
                grads[(bi, i, h, sd)] = (
                    jnp.dot(ds16, pair(key_rows(bi, sd), i, 1), preferred_element_type=F32),
                    lax.dot_general(ds16, q_pair, TN, preferred_element_type=F32),
                    lax.dot_general(p16, do16, TN, preferred_element_type=F32))
            both = lambda bi, i, sd, which: jnp.where(first_head, grads[(bi, i, 0, sd)][which],
                                                      grads[(bi, i, 1, sd)][which])
            carry = list(carry) if carry is not None else None
            for bi, n in enumerate(ns):
                for i in range(hps):
                    base = 3 * LANE * i
                    dq = both(bi, i, 0, 0)
                    if with_prev:
                        dq = dq + both(bi, i, 1, 0)
                        dqkv_ref[_rows(n - 1), base + LANE:base + 2 * LANE] = (
                            carry[2 * i] + both(bi, i, 1, 1)).astype(BF16)
                        dqkv_ref[_rows(n - 1), base + 2 * LANE:base + 3 * LANE] = (
                            carry[2 * i + 1] + both(bi, i, 1, 2)).astype(BF16)
                    dqkv_ref[_rows(n), base:base + LANE] = (dq * scale).astype(BF16)
                carry = [t for i in range(hps) for t in (both(bi, i, 0, 1), both(bi, i, 0, 2))]
            return tuple(carry)

        carry = _for_blocks(blocks, nblk, 2 if hps == 1 else 1, None)
        for i in range(hps):
            base = 3 * LANE * i
            dqkv_ref[_rows(nblk - 1), base + LANE:base + 2 * LANE] = carry[2 * i].astype(BF16)
            dqkv_ref[_rows(nblk - 1), base + 2 * LANE:base + 3 * LANE] = carry[2 * i + 1].astype(BF16)

    qspec = pl.BlockSpec((None, None, sub, 3 * LANE * hps), lambda hp, b, r: (b, r, 0, hp))
    bspec = pl.BlockSpec((2 * hps, BAND, 2 * BAND), lambda hp, b, r: (hp, 0, 0))
    if compact:
        cspecs = [pl.BlockSpec((None, None, sub, ATT_OUT), lambda hp, b, r: (b, r, 0, 0)),
                  pl.BlockSpec((None, None, sub, LANE), lambda hp, b, r: (b, r, 0, 0))]
    else:
        cspecs = [pl.BlockSpec((None, sub, hps * LANE), lambda hp, b, r: (b, 0, r * (npair // hps) + hp))] * 3
    return _pcall(
        body, name=name, grid=(npair // hps, nb, d),
        in_specs=[qspec, _bias_spec(group, hps)] + cspecs, out_specs=[qspec, bspec],
        out_shape=[jax.ShapeDtypeStruct(qkv4.shape, BF16),
                   jax.ShapeDtypeStruct((GROUP_HEADS, BAND, 2 * BAND), F32)],
        compiler_params=_params("parallel", "arbitrary", "arbitrary"),
    )(qkv4, bias, *cotangent)


def _head_lanes(first_lane, one_channel):
    c = lax.broadcasted_iota(jnp.int32, (ATT_OUT, LANE), 0)
    lane = lax.broadcasted_iota(jnp.int32, (ATT_OUT, LANE), 1)
    hit = lane == first_lane + c // HEAD_DIM
    if one_channel:
        hit = hit & (c % HEAD_DIM == 0)
    return hit.astype(BF16)


def _exact_dot(v, m01, dims=None):
    parts = _split3(v)
    if dims is None:
        dot = lambda t: jnp.dot(t, m01, preferred_element_type=F32)
    else:
        dot = lambda t: lax.dot_general(t, m01, dims, preferred_element_type=F32)
    return (dot(parts[0]) + dot(parts[1])) + dot(parts[2])


def _store_own_order(value, tile_refs, out_ref):
    d, per, width = out_ref.shape
    for j in range(width // LANE):
        tile_refs[j][...] = value[:, j * LANE:(j + 1) * LANE]
    for r in range(d):
        rows = pl.ds(r, per, stride=d)
        for j in range(width // LANE):
            out_ref[r, :, j * LANE:(j + 1) * LANE] = tile_refs[j][rows, :].astype(out_ref.dtype)


def _token_orders(x, dilations):
    nb, seq, kdim = x.shape
    tm = 512

    def body(x_ref, nat_ref, *rest):
        outs, tile_refs = rest[:len(dilations)], rest[len(dilations):]
        xv = x_ref[...]
        nat_ref[...] = xv.astype(BF16)
        for o_ref in outs:
            _store_own_order(xv, tile_refs, o_ref)

    outs = _pcall(
        body, name="token_orders", grid=(nb, seq // tm), in_specs=[_tok_spec(tm, kdim)],
        out_specs=[_tok_spec(tm, kdim)]
        + [pl.BlockSpec((None, d, tm // d, kdim), lambda b, i: (b, 0, i, 0)) for d in dilations],
        out_shape=[jax.ShapeDtypeStruct((nb, seq, kdim), BF16)]
        + [jax.ShapeDtypeStruct((nb, d, seq // d, kdim), BF16) for d in dilations],
        scratch_shapes=[pltpu.VMEM((tm, LANE), F32)] * (kdim // LANE),
        compiler_params=_params("parallel", "parallel"),
    )(x)
    return [outs[0]] + [o.reshape(nb, seq, kdim) for o in outs[1:]]


def _natural_rows(p_ref, tile_refs):
    d, per, width = p_ref.shape
    for r in range(d):
        rows = pl.ds(r, per, stride=d)
        for j in range(width // LANE):
            tile_refs[j][rows, :] = p_ref[r, :, j * LANE:(j + 1) * LANE]
    return jnp.concatenate([tile_refs[j][...] for j in range(width // LANE)], axis=1)


def _combine_fwd(o0, l0, dilated, gatt):
    nb, seq, _ = gatt.shape
    tm = 512
    ntile = ATT_OUT // LANE

    def body(o0_ref, l0_ref, o1_ref, l1_ref, o2_ref, l2_ref, g_ref, oa_ref, oatt_ref, lse_ref, *tile_refs):
        spread = _head_lanes(0, False)
        l0v = l0_ref[...]
        l1v = _exact_dot(_natural_rows(l1_ref, tile_refs), spread, NT)
        l2v = _exact_dot(_natural_rows(l2_ref, tile_refs), spread, NT)
        m = jnp.maximum(jnp.maximum(l0v, l1v), l2v)
        tot = m + jnp.log(jnp.exp(l0v - m) + jnp.exp(l1v - m) + jnp.exp(l2v - m))
        o = jnp.exp(l0v - tot) * o0_ref[...]
        o = o + jnp.exp(l1v - tot) * _natural_rows(o1_ref, tile_refs)
        o = o + jnp.exp(l2v - tot) * _natural_rows(o2_ref, tile_refs)
        g = g_ref[...]
        oa_ref[...] = (o * (g * _sigmoid(g))).astype(BF16)
        oatt_ref[...] = o
        lse_ref[...] = tot

    spec = pl.BlockSpec((None, tm, ATT_OUT), lambda b, i: (b, i, 0))
    own = lambda t: pl.BlockSpec((None, t.shape[1], tm // t.shape[1], t.shape[3]), lambda b, i: (b, 0, i, 0))
    (o1, l1), (o2, l2) = dilated
    return _pcall(
        body, name="attn_combine", grid=(nb, seq // tm),
        in_specs=[spec, spec, own(o1), own(l1), own(o2), own(l2), spec], out_specs=[spec] * 3,
        out_shape=[jax.ShapeDtypeStruct((nb, seq, ATT_OUT), BF16), jax.ShapeDtypeStruct((nb, seq, ATT_OUT), F32),
                   jax.ShapeDtypeStruct((nb, seq, ATT_OUT), F32)],
        scratch_shapes=[pltpu.VMEM((tm, LANE), F32)] * ntile,
        compiler_params=_params("parallel", "parallel"),
    )(o0, l0, o1, l1, o2, l2, gatt)


def _combine_bwd(dya16, w_bra, gatt, o_att, lse, dilations):
    nb, seq, _ = gatt.shape
    tm = 512

    def body(dya_ref, w_ref, g_ref, o_ref, l_ref, do_ref, dg_ref, *rest):
        ntile = ATT_OUT // LANE
        outs, tile_refs = rest[:-ntile], rest[-ntile:]
        doa = lax.dot_general(dya_ref[...], w_ref[...], NT, preferred_element_type=F32)
        g = g_ref[...]
        sg = _sigmoid(g)
        do = doa * (g * sg)
        do_ref[...] = do
        stats = (_exact_dot(do * o_ref[...], _head_lanes(0, False))
                 + _exact_dot(l_ref[...], _head_lanes(STAT_LSE_LANE, True)))
        dg_ref[...] = (doa * o_ref[...] * (sg * (1.0 + g * (1.0 - sg)))).astype(BF16)
        for k in range(len(dilations)):
            _store_own_order(do, tile_refs, outs[2 * k])
            _store_own_order(stats, tile_refs, outs[2 * k + 1])

    spec = pl.BlockSpec((None, tm, ATT_OUT), lambda b, i: (b, i, 0))
    own = lambda d, width: pl.BlockSpec((None, d, tm // d, width), lambda b, i: (b, 0, i, 0))
    outs = _pcall(
        body, name="attn_combine_bwd", grid=(nb, seq // tm),
        in_specs=[_tok_spec(tm, D_MODEL), _whole(w_bra, True)] + [spec] * 3,
        out_specs=[spec, spec] + [own(d, w) for d in dilations for w in (ATT_OUT, LANE)],
        out_shape=[jax.ShapeDtypeStruct((nb, seq, ATT_OUT), F32), jax.ShapeDtypeStruct((nb, seq, ATT_OUT), BF16)]
        + [jax.ShapeDtypeStruct((nb, d, seq // d, w), t) for d in dilations for w, t in ((ATT_OUT, BF16), (LANE, F32))],
        scratch_shapes=[pltpu.VMEM((tm, LANE), F32)] * (ATT_OUT // LANE),
        compiler_params=_params("parallel", "parallel"),
    )(dya16, w_bra, gatt, o_att, lse)
    return outs[0], outs[1], outs[2:]


CONV_TM = 1024
CONV_TC = 1024


def _shift_down(cur, halo, k):
    rolled = pltpu.roll(cur, k, 0)
    hro = pltpu.roll(halo, k, 0)
    row = lax.broadcasted_iota(jnp.int32, hro.shape, 0)
    return jnp.concatenate([jnp.where(row < k, hro, rolled[:8]), rolled[8:]], axis=0)


def _shift_up(cur, halo, k):
    n = cur.shape[0]
    rolled = pltpu.roll(cur, n - k, 0)
    hro = pltpu.roll(halo, 8 - k, 0)
    row = lax.broadcasted_iota(jnp.int32, hro.shape, 0)
    return jnp.concatenate([rolled[:n - 8], jnp.where(row >= 8 - k, hro, rolled[n - 8:])], axis=0)


def _conv_pre(cur, halo, w_ref, b_ref):
    acc = cur * w_ref[3:4, :] + b_ref[...]
    for k in range(1, 4):
        acc = acc + _shift_down(cur, halo, k) * w_ref[3 - k:4 - k, :]
    return acc


def _conv_specs(seq):
    nblk = seq // CONV_TM
    cur = pl.BlockSpec((None, CONV_TM, CONV_TC), lambda cb, b, i: (b, i, cb))
    prev = pl.BlockSpec((None, 8, CONV_TC), lambda cb, b, i: (b, jnp.maximum(i * (CONV_TM // 8) - 1, 0), cb))
    nxt = pl.BlockSpec((None, 8, CONV_TC),
                       lambda cb, b, i: (b, jnp.minimum((i + 1) * (CONV_TM // 8), seq // 8 - 1), cb))
    wspec = pl.BlockSpec((4, CONV_TC), lambda cb, b, i: (0, cb))
    bspec = pl.BlockSpec((1, CONV_TC), lambda cb, b, i: (0, cb))
    return nblk, cur, prev, nxt, wspec, bspec


def _conv_fwd(xin, w4, bias, name):
    nb, seq, ch = xin.shape
    _, cur, prev, _, wspec, bspec = _conv_specs(seq)

    def body(x_ref, h_ref, w_ref, b_ref, o_ref):
        halo = jnp.where(pl.program_id(2) > 0, h_ref[...], 0.0)
        pre = _conv_pre(x_ref[...], halo, w_ref, b_ref)
        o_ref[...] = pre * _sigmoid(pre)

    return _pcall(
        body, name=name, grid=(ch // CONV_TC, nb, seq // CONV_TM),
        in_specs=[cur, prev, wspec, bspec], out_specs=cur,
        out_shape=jax.ShapeDtypeStruct(xin.shape, F32),
        compiler_params=_params("parallel", "parallel", "parallel"),
    )(xin, xin, w4, bias)


def _conv_bwd_pre(dact, xin, w4, bias, name):
    nb, seq, ch = xin.shape
    _, cur, prev, _, wspec, bspec = _conv_specs(seq)

    def body(da_ref, x_ref, h_ref, w_ref, b_ref, dp_ref, s_ref):
        b, i = pl.program_id(1), pl.program_id(2)

        @pl.when((b == 0) & (i == 0))
        def _():
            s_ref[...] = jnp.zeros_like(s_ref)

        halo = jnp.where(i > 0, h_ref[...], 0.0)
        x = x_ref[...]
        pre = _conv_pre(x, halo, w_ref, b_ref)
        sg = _sigmoid(pre)
        dpre = da_ref[...] * (sg * (1.0 + pre * (1.0 - sg)))
        dp_ref[...] = dpre
        s_ref[3:4, :] += jnp.sum(dpre * x, 0, keepdims=True)
        for k in range(1, 4):
            s_ref[3 - k:4 - k, :] += jnp.sum(dpre * _shift_down(x, halo, k), 0, keepdims=True)
        s_ref[4:5, :] += jnp.sum(dpre, 0, keepdims=True)

    return _pcall(
        body, name=name, grid=(ch // CONV_TC, nb, seq // CONV_TM),
        in_specs=[cur, cur, prev, wspec, bspec],
        out_specs=[cur, pl.BlockSpec((8, CONV_TC), lambda cb, b, i: (0, cb))],
        out_shape=[jax.ShapeDtypeStruct(xin.shape, F32), jax.ShapeDtypeStruct((8, ch), F32)],
        compiler_params=_params("parallel", "arbitrary", "arbitrary"),
    )(dact, xin, xin, w4, bias)


def _conv_bwd_x(dpre, w4, name):
    nb, seq, ch = dpre.shape
    nblk, cur, _, nxt, wspec, _ = _conv_specs(seq)

    def body(d_ref, n_ref, w_ref, o_ref):
        halo = jnp.where(pl.program_id(2) < nblk - 1, n_ref[...], 0.0)
        cur_v = d_ref[...]
        acc = cur_v * w_ref[3:4, :]
        for j in range(1, 4):
            acc = acc + _shift_up(cur_v, halo, j) * w_ref[3 - j:4 - j, :]
        o_ref[...] = acc.astype(BF16)

    return _pcall(
        body, name=name, grid=(ch // CONV_TC, nb, seq // CONV_TM),
        in_specs=[cur, nxt, wspec], out_specs=cur,
        out_shape=jax.ShapeDtypeStruct(dpre.shape, BF16),
        compiler_params=_params("parallel", "parallel", "parallel"),
    )(dpre, dpre, w4)


def _softplus_sig(dt_raw, dt_bias_row):
    nb, seq, _ = dt_raw.shape
    tm = 512

    def body(r_ref, b_ref, sp_ref, sg_ref):
        v = r_ref[...] + b_ref[...]
        sp_ref[...] = jnp.maximum(v, 0.0) + jnp.log1p(jnp.exp(-jnp.abs(v)))
        sg_ref[...] = _sigmoid(v)

    spec = pl.BlockSpec((None, tm, LANE), lambda b, i: (b, i, 0))
    return _pcall(
        body, name="dt_softplus", grid=(nb, seq // tm),
        in_specs=[spec, pl.BlockSpec((1, LANE), lambda b, i: (0, 0))], out_specs=[spec, spec],
        out_shape=[jax.ShapeDtypeStruct(dt_raw.shape, F32)] * 2,
        compiler_params=_params("parallel", "parallel"),
    )(dt_raw, dt_bias_row)


def _group_lanes(t):
    pads = [(0, 0)] * (t.ndim - 1) + [(0, LANE - GROUP_SSM_HEADS)]
    return jnp.stack([jnp.pad(t[..., GROUP_SSM_HEADS * g:GROUP_SSM_HEADS * (g + 1)], pads) for g in range(SSM_GROUPS)])


def _ungroup_lanes(t):
    return jnp.concatenate([t[g][..., :GROUP_SSM_HEADS] for g in range(SSM_GROUPS)], axis=-1)


def _decays(dt, al_ref):
    row = lax.broadcasted_iota(jnp.int32, (CHUNK, CHUNK), 0)
    col = lax.broadcasted_iota(jnp.int32, (CHUNK, CHUNK), 1)
    tril = (row >= col).astype(BF16)
    triu = (row <= col).astype(BF16)
    arow = -jnp.exp(al_ref[...])
    hi, mid, lo = _split3(dt * arow)
    down = lambda t: jnp.dot(tril, t, preferred_element_type=F32)
    across = lambda t: lax.dot_general(t, triu, TN, preferred_element_type=F32)
    acs = (down(hi) + down(mid)) + down(lo)
    acs_t = (across(hi) + across(mid)) + across(lo)
    return arow, acs, acs_t, row >= col, triu


STEP_CHUNKS = 8


def _ssd_specs(nb, seq):
    nc = seq // CHUNK
    hw = GROUP_SSM_HEADS * HEAD_DIM
    rows, steps = STEP_CHUNKS * CHUNK, nc // STEP_CHUNKS

    def mk(rev):
        cidx = (lambda c: steps - 1 - c) if rev else (lambda c: c)
        wide = pl.BlockSpec((None, rows, hw), lambda g, b, c: (b, cidx(c), g))
        xbc = pl.BlockSpec((None, rows, XBC_GROUP), lambda g, b, c: (b, cidx(c), g))
        lanes = pl.BlockSpec((None, None, rows, LANE), lambda g, b, c: (g, b, cidx(c), 0))
        prev = pl.BlockSpec((None, STEP_CHUNKS, None, D_STATE, hw), lambda g, b, c: (b, cidx(c), g, 0, 0))
        return wide, xbc, lanes, prev

    grow = pl.BlockSpec((None, 1, LANE), lambda g, b, c: (g, 0, 0))
    nwspec = pl.BlockSpec((1, hw), lambda g, b, c: (0, g))
    return nc, steps, hw, mk, grow, nwspec


def _head_expand():
    hw = GROUP_SSM_HEADS * HEAD_DIM
    r = lax.broadcasted_iota(jnp.int32, (LANE, hw), 0)
    c = lax.broadcasted_iota(jnp.int32, (LANE, hw), 1)
    return ((c // HEAD_DIM) == r).astype(BF16)


def _split3(v):
    hi = v.astype(BF16)
    rest = v - hi.astype(F32)
    mid = rest.astype(BF16)
    return hi, mid, (rest - mid.astype(F32)).astype(BF16)


def _to_channels(v, e):
    hi, mid, lo = _split3(v)
    dot = lambda t: jnp.dot(t, e, preferred_element_type=F32)
    return (dot(hi) + dot(mid)) + dot(lo)


def _to_heads(w, e):
    hi, mid, lo = _split3(w)
    dot = lambda t: lax.dot_general(t, e, (((1,), (1,)), ((), ())), preferred_element_type=F32)
    return (dot(hi) + dot(mid)) + dot(lo)


def _row8(v):
    return jnp.broadcast_to(v, (8, v.shape[1]))


def _ssd_chunk_setup(dt, al_ref, ds_ref):
    arow, acs, acs_t, causal, triu = _decays(dt, al_ref)
    e = _head_expand()
    dtx = _to_channels(dt, e)
    acsx = _to_channels(acs, e)
    lastx = acsx[CHUNK - 1:CHUNK, :]
    dskx = _to_channels(_row8(ds_ref[...]), e)[0:1, :]
    return arow, acs, acs_t, causal, triu, e, dtx, acsx, lastx, dskx


def _ssd_fwd(xbc, dtg, z, alog_g, dskip_g, normw):
    nb, seq, _ = xbc.shape
    nc, steps, hw, mk, grow, nwspec = _ssd_specs(nb, seq)
    wide, xbc_spec, lanes, prev = mk(False)
    tn = (((0,), (0,)), ((), ()))

    def body(xbc_ref, dt_ref, z_ref, al_ref, ds_ref, nw_ref, ys_ref, y_ref, sp_ref, st_ref):
        @pl.when(pl.program_id(2) == 0)
        def _():
            st_ref[...] = jnp.zeros_like(st_ref)

        for ci in range(STEP_CHUNKS):
            chunk(ci, xbc_ref, dt_ref, z_ref, al_ref, ds_ref, nw_ref, ys_ref, y_ref, sp_ref, st_ref)

    def chunk(ci, xbc_ref, dt_ref, z_ref, al_ref, ds_ref, nw_ref, ys_ref, y_ref, sp_ref, st_ref):
        rows = slice(ci * CHUNK, (ci + 1) * CHUNK)
        dt = dt_ref[rows, :]
        _, acs, acs_t, causal, _, _, dtx, acsx, lastx, dskx = _ssd_chunk_setup(dt, al_ref, ds_ref)
        bmat = xbc_ref[rows, GROUP_CH:GROUP_CH + D_STATE].astype(BF16)
        cmat = xbc_ref[rows, GROUP_CH + D_STATE:].astype(BF16)
        cb = lax.dot_general(cmat, bmat, (((1,), (1,)), ((), ())), preferred_element_type=F32)
        x = xbc_ref[rows, :GROUP_CH]
        xdt = x * dtx
        xdt16 = xdt.astype(BF16)
        first_head = lax.broadcasted_iota(jnp.int32, (CHUNK, LANE), 1) < HEAD_DIM
        pairs = []
        for hp in range(GROUP_SSM_HEADS // 2):
            xp = xdt16[:, hp * LANE:(hp + 1) * LANE]
            two = []
            for j in (2 * hp, 2 * hp + 1):
                lmat = jnp.exp(jnp.where(causal, acs[:, j:j + 1] - acs_t[j:j + 1, :], -jnp.inf))
                two.append(jnp.dot((cb * lmat).astype(BF16), xp, preferred_element_type=F32))
            pairs.append(jnp.where(first_head, two[0], two[1]))
        yd = jnp.concatenate(pairs, axis=1)
        s_prev = st_ref[...]
        s16 = s_prev.astype(BF16)
        sp_ref[ci] = s16
        yo = jnp.dot(cmat, s16, preferred_element_type=F32) * jnp.exp(acsx)
        sts = lax.dot_general(bmat, (xdt * jnp.exp(lastx - acsx)).astype(BF16), tn, preferred_element_type=F32)
        st_ref[...] = s_prev * jnp.exp(lastx) + sts
        y = yd + yo + dskx * x
        zz = z_ref[rows, :]
        u = y * (zz * _sigmoid(zz))
        rn = lax.rsqrt(jnp.mean(u * u, -1, keepdims=True) + RMS_EPS)
        ys_ref[rows, :] = (u * rn * nw_ref[...]).astype(BF16)
        y_ref[rows, :] = y

    return _pcall(
        body, name="ssd_fwd", grid=(SSM_GROUPS, nb, steps),
        in_specs=[xbc_spec, lanes, wide, grow, grow, nwspec],
        out_specs=[wide, wide, prev],
        out_shape=[jax.ShapeDtypeStruct((nb, seq, D_INNER), BF16), jax.ShapeDtypeStruct((nb, seq, D_INNER), F32),
                   jax.ShapeDtypeStruct((nb, nc, SSM_GROUPS, D_STATE, hw), BF16)],
        scratch_shapes=[pltpu.VMEM((D_STATE, hw), F32)],
        compiler_params=_params("parallel", "parallel", "arbitrary"),
    )(xbc, dtg, z, alog_g, dskip_g, normw)


def _ssd_bwd(xbc, dtg, sgg, z, y, dys, sprev, alog_g, dskip_g, normw):
    nb, seq, _ = xbc.shape
    nc, steps, hw, mk, grow, nwspec = _ssd_specs(nb, seq)
    wide, xbc_spec, lanes, prev = mk(True)
    nt = (((1,), (1,)), ((), ()))
    tn = (((0,), (0,)), ((), ()))

    def body(xbc_ref, dt_ref, sg_ref, z_ref, y_ref, dys_ref, sp_ref, al_ref, ds_ref, nw_ref,
             dxbc_ref, ddt_ref, dz_ref, small_ref, dnw_ref, g_ref):
        b, c = pl.program_id(1), pl.program_id(2)

        @pl.when((b == 0) & (c == 0))
        def _():
            small_ref[...] = jnp.zeros_like(small_ref)
            dnw_ref[...] = jnp.zeros_like(dnw_ref)

        @pl.when(c == 0)
        def _():
            g_ref[...] = jnp.zeros_like(g_ref)

        for ci in reversed(range(STEP_CHUNKS)):
            chunk(ci, xbc_ref, dt_ref, sg_ref, z_ref, y_ref, dys_ref, sp_ref, al_ref, ds_ref, nw_ref,
                  dxbc_ref, ddt_ref, dz_ref, small_ref, dnw_ref, g_ref)

    def chunk(ci, xbc_ref, dt_ref, sg_ref, z_ref, y_ref, dys_ref, sp_ref, al_ref, ds_ref, nw_ref,
              dxbc_ref, ddt_ref, dz_ref, small_ref, dnw_ref, g_ref):
        rows = slice(ci * CHUNK, (ci + 1) * CHUNK)
        yv, zz, dys_v, nw = y_ref[rows, :], z_ref[rows, :], dys_ref[rows, :], nw_ref[...]
        sz = _sigmoid(zz)
        silu = zz * sz
        u = yv * silu
        rn = lax.rsqrt(jnp.mean(u * u, -1, keepdims=True) + RMS_EPS)
        gn = dys_v * nw
        du = rn * gn - u * (rn * rn * rn) * jnp.mean(u * gn, -1, keepdims=True)
        dnw_ref[...] += jnp.sum(dys_v * u * rn, 0, keepdims=True)
        dy = du * silu
        dz_ref[rows, :] = du * yv * (sz * (1.0 + zz * (1.0 - sz)))

        dt = dt_ref[rows, :]
        arow, acs, acs_t, causal, triu, e, dtx, acsx, lastx, dskx = _ssd_chunk_setup(dt, al_ref, ds_ref)
        dfsx = jnp.exp(acsx)
        dtex = jnp.exp(lastx - acsx)
        bmat = xbc_ref[rows, GROUP_CH:GROUP_CH + D_STATE].astype(BF16)
        cmat = xbc_ref[rows, GROUP_CH + D_STATE:].astype(BF16)
        cb = lax.dot_general(cmat, bmat, nt, preferred_element_type=F32)
        x = xbc_ref[rows, :GROUP_CH]
        xdt = x * dtx
        xdt16 = xdt.astype(BF16)
        xdte = xdt * dtex
        dy16 = dy.astype(BF16)
        dyd = dy * dfsx
        dyd16 = dyd.astype(BF16)
        s16 = sp_ref[ci]
        g = g_ref[...]
        g16 = g.astype(BF16)
        cs = jnp.dot(cmat, s16, preferred_element_type=F32)
        dc_off = lax.dot_general(dyd16, s16, nt, preferred_element_type=F32)
        g_here = lax.dot_general(cmat, dyd16, tn, preferred_element_type=F32)
        bg = jnp.dot(bmat, g16, preferred_element_type=F32)
        db_st = lax.dot_general(xdte.astype(BF16), g16, nt, preferred_element_type=F32)
        ddte_w = bg * xdte
        dcd = _to_heads(_row8(jnp.sum(g * s16.astype(F32), 0, keepdims=True)), e)[0:1, :]
        lane = lax.broadcasted_iota(jnp.int32, (CHUNK, LANE), 1)
        first_head = lane < HEAD_DIM
        sub = lax.broadcasted_iota(jnp.int32, (CHUNK, LANE), 0)
        dacs = jnp.zeros((CHUNK, LANE), F32)
        colsums = jnp.zeros((CHUNK, LANE), F32)
        dcb = jnp.zeros((CHUNK, CHUNK), F32)
        pairs = []
        for hp in range(GROUP_SSM_HEADS // 2):
            xp = xdt16[:, hp * LANE:(hp + 1) * LANE]
            dyp = dy16[:, hp * LANE:(hp + 1) * LANE]
            two = []
            for idx, j in enumerate((2 * hp, 2 * hp + 1)):
                lmat = jnp.exp(jnp.where(causal, acs[:, j:j + 1] - acs_t[j:j + 1, :], -jnp.inf))
                mf = cb * lmat
                dy_h = jnp.where(first_head if idx == 0 else jnp.logical_not(first_head), dyp, jnp.zeros_like(dyp))
                dm = lax.dot_general(dy_h, xp, nt, preferred_element_type=F32)
                two.append(lax.dot_general(mf.astype(BF16), dyp, tn, preferred_element_type=F32))
                wmat = dm * mf
                dcb = dcb + dm * lmat
                dacs = jnp.where(lane == j, jnp.sum(wmat, -1, keepdims=True), dacs)
                colsums = jnp.where(sub == j, jnp.sum(wmat, 0, keepdims=True), colsums)
            pairs.append(jnp.where(first_head, two[0], two[1]))
        dxdt = bg * dtex + jnp.concatenate(pairs, axis=1)
        dacs = dacs - colsums.T + _to_heads(dyd * cs - ddte_w, e)
        cd_row = jnp.exp(acs[CHUNK - 1:CHUNK, :])
        tail = _to_heads(_row8(jnp.sum(ddte_w, 0, keepdims=True)), e)[0:1, :] + dcd * cd_row
        dacs = dacs + jnp.where(sub == CHUNK - 1, tail, 0.0)
        d_hi, d_mid, d_lo = _split3(dacs)
        up = lambda t: jnp.dot(triu, t, preferred_element_type=F32)
        da = (up(d_hi) + up(d_mid)) + up(d_lo)
        ddt_raw = (da * arow + _to_heads(dxdt * x, e)) * sg_ref[rows, :]
        ddt_ref[rows, :] = ddt_raw
        small_ref[0:1, :] += jnp.sum(da * dt, 0, keepdims=True) * arow
        small_ref[1:2, :] += _to_heads(_row8(jnp.sum(dy * x, 0, keepdims=True)), e)[0:1, :]
        small_ref[2:3, :] += jnp.sum(ddt_raw, 0, keepdims=True)
        dcb16 = dcb.astype(BF16)
        dxbc_ref[rows, GROUP_CH + D_STATE:] = dc_off + jnp.dot(dcb16, bmat, preferred_element_type=F32)
        dxbc_ref[rows, GROUP_CH:GROUP_CH + D_STATE] = db_st + lax.dot_general(dcb16, cmat, tn,
                                                                               preferred_element_type=F32)
        dxbc_ref[rows, :GROUP_CH] = dxdt * dtx + dskx * dy
        g_ref[...] = g * jnp.exp(lastx) + g_here

    return _pcall(
        body, name="ssd_bwd", grid=(SSM_GROUPS, nb, steps),
        in_specs=[xbc_spec, lanes, lanes, wide, wide, wide, prev, grow, grow, nwspec],
        out_specs=[xbc_spec, lanes, wide,
                   pl.BlockSpec((None, 8, LANE), lambda g, b, c: (g, 0, 0)), nwspec],
        out_shape=[jax.ShapeDtypeStruct((nb, seq, CONV_DIM), F32),
                   jax.ShapeDtypeStruct((SSM_GROUPS, nb, seq, LANE), F32),
                   jax.ShapeDtypeStruct((nb, seq, D_INNER), F32),
                   jax.ShapeDtypeStruct((SSM_GROUPS, 8, LANE), F32),
                   jax.ShapeDtypeStruct((1, D_INNER), F32)],
        scratch_shapes=[pltpu.VMEM((D_STATE, hw), F32)],
        compiler_params=_params("parallel", "arbitrary", "arbitrary"),
    )(xbc, dtg, sgg, z, y, dys, sprev, alog_g, dskip_g, normw)


EW_TM = 256


def _merge_fwd(oa16, y_ssm16, w_bra, w_brb, gm, bgate):
    nb, seq, _ = oa16.shape

    def body(oa_ref, ys_ref, wa_ref, wb_ref, ga_ref, gb_ref, bg_ref, a_ref, b_ref, o_ref):
        y_a = jnp.dot(oa_ref[...], wa_ref[...], preferred_element_type=F32)
        y_b = jnp.dot(ys_ref[...], wb_ref[...], preferred_element_type=F32)
        a_ref[...] = y_a
        b_ref[...] = y_b
        sa = _sigmoid(ga_ref[...] + bg_ref[0:1, :])
        sb = _sigmoid(gb_ref[...] + bg_ref[1:2, :])
        o_ref[...] = (sa * y_a + sb * y_b).astype(BF16)

    spec = pl.BlockSpec((None, EW_TM, D_MODEL), lambda b, i: (b, i, 0))
    spec1 = pl.BlockSpec((None, EW_TM, D_MODEL), lambda b, i: (b, i, 1))
    return _pcall(
        body, name="merge_fwd", grid=(nb, seq // EW_TM),
        in_specs=[_tok_spec(EW_TM, ATT_OUT), _tok_spec(EW_TM, D_INNER), _whole(w_bra, True), _whole(w_brb, True),
                  spec, spec1, pl.BlockSpec((8, D_MODEL), lambda b, i: (0, 0))],
        out_specs=[spec] * 3,
        out_shape=[jax.ShapeDtypeStruct((nb, seq, D_MODEL), F32)] * 2 + [jax.ShapeDtypeStruct((nb, seq, D_MODEL), BF16)],
        compiler_params=_params("parallel", "parallel"),
    )(oa16, y_ssm16, w_bra, w_brb, gm, gm, bgate)


def _merge_bwd(dpre16, w_out16, y_a, y_b, gm, bgate):
    nb, seq, _ = y_a.shape

    def body(dp_ref, w_ref, a_ref, b_ref, ga_ref, gb_ref, bg_ref, dya_ref, dyb_ref, dg_ref, s_ref):
        @pl.when((pl.program_id(0) == 0) & (pl.program_id(1) == 0))
        def _():
            s_ref[...] = jnp.zeros_like(s_ref)

        dm = lax.dot_general(dp_ref[...], w_ref[...], NT, preferred_element_type=F32)
        sa = _sigmoid(ga_ref[...] + bg_ref[0:1, :])
        sb = _sigmoid(gb_ref[...] + bg_ref[1:2, :])
        dya_ref[...] = (dm * sa).astype(BF16)
        dyb_ref[...] = (dm * sb).astype(BF16)
        dga = dm * a_ref[...] * (sa * (1.0 - sa))
        dgb = dm * b_ref[...] * (sb * (1.0 - sb))
        dg_ref[:, :D_MODEL] = dga.astype(BF16)
        dg_ref[:, D_MODEL:] = dgb.astype(BF16)
        s_ref[0:1, :] += jnp.sum(dga, 0, keepdims=True)
        s_ref[1:2, :] += jnp.sum(dgb, 0, keepdims=True)

    spec = pl.BlockSpec((None, EW_TM, D_MODEL), lambda b, i: (b, i, 0))
    spec1 = pl.BlockSpec((None, EW_TM, D_MODEL), lambda b, i: (b, i, 1))
    small = pl.BlockSpec((8, D_MODEL), lambda b, i: (0, 0))
    return _pcall(
        body, name="merge_bwd", grid=(nb, seq // EW_TM),
        in_specs=[spec, _whole(w_out16, True), spec, spec, spec, spec1, small],
        out_specs=[spec, spec, pl.BlockSpec((None, EW_TM, 2 * D_MODEL), lambda b, i: (b, i, 0)), small],
        out_shape=[jax.ShapeDtypeStruct((nb, seq, D_MODEL), BF16), jax.ShapeDtypeStruct((nb, seq, D_MODEL), BF16),
                   jax.ShapeDtypeStruct((nb, seq, 2 * D_MODEL), BF16), jax.ShapeDtypeStruct((8, D_MODEL), F32)],
        compiler_params=_params("arbitrary", "arbitrary"),
    )(dpre16, w_out16, y_a, y_b, gm, gm, bgate)


def _ln_loss(x, merged16, w_out16, gp, p16, w_ple16, target, bgate, ln_g, ln_b):
    nb, seq, _ = x.shape

    def body(x_ref, m_ref, wo_ref, gp_ref, p_ref, wp_ref, t_ref, bg_ref, g_ref, b_ref,
             dx_ref, dp_ref, dpw_ref, dgp_ref, s_ref):
        @pl.when((pl.program_id(0) == 0) & (pl.program_id(1) == 0))
        def _():
            s_ref[...] = jnp.zeros_like(s_ref)

        sp = _sigmoid(gp_ref[...] + bg_ref[2:3, :])
        pw = jnp.dot(p_ref[...], wp_ref[...], preferred_element_type=F32)
        mix = jnp.dot(m_ref[...], wo_ref[...], preferred_element_type=F32)
        pre = ALPHA * x_ref[...] + mix + sp * pw
        mu = jnp.mean(pre, -1, keepdims=True)
        cen = pre - mu
        rstd = lax.rsqrt(jnp.mean(cen * cen, -1, keepdims=True) + LN_EPS)
        xhat = cen * rstd
        err = xhat * g_ref[...] + b_ref[...] - t_ref[...]
        dy = err * (1.0 / D_MODEL)
        dxh = dy * g_ref[...]
        dpre = rstd * (dxh - jnp.mean(dxh, -1, keepdims=True) - xhat * jnp.mean(dxh * xhat, -1, keepdims=True))
        dx_ref[...] = ALPHA * dpre
        dp_ref[...] = dpre.astype(BF16)
        dpw_ref[...] = (dpre * sp).astype(BF16)
        dgp = dpre * pw * (sp * (1.0 - sp))
        dgp_ref[...] = dgp.astype(BF16)
        s_ref[0:1, :] += jnp.sum(dy * xhat, 0, keepdims=True)
        s_ref[1:2, :] += jnp.sum(dy, 0, keepdims=True)
        s_ref[2:3, :] += jnp.sum(dgp, 0, keepdims=True)
        s_ref[3:4, :] += jnp.sum(err * err, 0, keepdims=True)

    spec = pl.BlockSpec((None, EW_TM, D_MODEL), lambda b, i: (b, i, 0))
    small = pl.BlockSpec((8, D_MODEL), lambda b, i: (0, 0))
    row = pl.BlockSpec((1, D_MODEL), lambda b, i: (0, 0))
    return _pcall(
        body, name="ln_loss", grid=(nb, seq // EW_TM),
        in_specs=[spec, spec, _whole(w_out16, True), spec, pl.BlockSpec((None, EW_TM, PLE_DIM), lambda b, i: (b, i, 0)),
                  _whole(w_ple16, True), spec, small, row, row],
        out_specs=[spec] * 4 + [small],
        out_shape=[jax.ShapeDtypeStruct((nb, seq, D_MODEL), F32)] + [jax.ShapeDtypeStruct((nb, seq, D_MODEL), BF16)] * 3
        + [jax.ShapeDtypeStruct((8, D_MODEL), F32)],
        compiler_params=_params("arbitrary", "arbitrary"),
    )(x, merged16, w_out16, gp, p16, w_ple16, target, bgate, ln_g, ln_b)


def _adamw(w, g, m, v, name):
    rows, cols = w.shape
    tr = _row_tile(rows, cols, 8, 5 << 19)
    c1 = 1.0 - ADAM_B1 ** ADAM_STEP
    c2 = 1.0 - ADAM_B2 ** ADAM_STEP

    def body(w_ref, g_ref, m_ref, v_ref, d_ref, nm_ref, nv_ref):
        gv = g_ref[...]
        nm = ADAM_B1 * m_ref[...] + (1.0 - ADAM_B1) * gv
        nv = ADAM_B2 * v_ref[...] + (1.0 - ADAM_B2) * (gv * gv)
        d_ref[...] = -ADAM_LR * ((nm / c1) / (jnp.sqrt(nv / c2) + ADAM_EPS) + ADAM_WD * w_ref[...])
        nm_ref[...] = nm
        nv_ref[...] = nv

    spec = pl.BlockSpec((tr, cols), lambda i: (i, 0))
    return _pcall(
        body, name=name, grid=(rows // tr,), in_specs=[spec] * 4, out_specs=[spec] * 3,
        out_shape=[jax.ShapeDtypeStruct(w.shape, F32)] * 3, compiler_params=_params("parallel"),
    )(w, g, m, v)


def _sum_rows(parts, out_dtype, name):
    rows, cols = parts[0].shape
    tr = rows
    for cand in range(16, rows, 16):
        if rows % cand == 0 and cand * cols * 4 <= (1 << 20):
            tr = cand
    n = len(parts)

    def body(*refs):
        acc = refs[0][...].astype(F32)
        for r in refs[1:n]:
            acc = acc + r[...].astype(F32)
        refs[n][...] = acc.astype(out_dtype)

    spec = pl.BlockSpec((tr, cols), lambda i: (i, 0))
    return _pcall(
        body, name=name, grid=(rows // tr,), in_specs=[spec] * n, out_specs=spec,
        out_shape=jax.ShapeDtypeStruct((rows, cols), out_dtype), compiler_params=_params("parallel"),
    )(*parts)


def _place():
    return lax.axis_index("x"), lax.axis_index("y"), lax.axis_index("c")


def _other_chips(x, y):
    return [(1 - x, y), (x, 1 - y), (1 - x, 1 - y)]


def _remote(src, dst, send_sem, recv_sem, to):
    return pltpu.make_async_remote_copy(src_ref=src, dst_ref=dst, send_sem=send_sem, recv_sem=recv_sem,
                                        device_id=to, device_id_type=MESH)


ANY = pl.BlockSpec(memory_space=pl.ANY)
D2D_CHUNK_BYTES = 512 * 1024
ICI_CHUNK_BYTES = 2 * 1024 * 1024


def _row_chunks(rows, row_bytes, chunk_bytes=D2D_CHUNK_BYTES):
    per = max(16, chunk_bytes // row_bytes // 16 * 16)
    return [(s, min(per, rows - s)) for s in range(0, rows, per)]


def _row_tile(rows, cols, align, limit=1 << 21):
    best = None
    for cand in range(align, rows + 1, align):
        if rows % cand == 0 and cand * cols * 4 <= limit:
            best = cand
    return best or rows


def _allgather_pieces(pieces):
    n = len(pieces)
    halves = [_row_chunks(p.shape[0] // 2, p.shape[1] * p.dtype.itemsize, ICI_CHUNK_BYTES) for p in pieces]
    entries = [(a, q, s, m, j) for a in range(n) for q, (s, m) in enumerate(halves[a]) for j in range(3)]
    slot = {(a, q, j): k for k, (a, q, _, _, j) in enumerate(entries)}
    n_ici = len(entries)

    def body(*refs):
        ins, outs = refs[:n], refs[n:2 * n]
        send_sems, recv_sems = refs[2 * n:]
        x, y, c = _place()
        me = 2 * x + y
        sibling = (x, y, 1 - c)
        chips = _other_chips(x, y)

        def landed(a, s, m, j, core):
            half = ins[a].shape[0] // 2
            return outs[a].at[2 * chips[j][0] + chips[j][1], pl.ds(core * half + s, m)]

        sent = []
        for k, (a, q, s, m, j) in enumerate(entries):
            if j < 2:
                half = ins[a].shape[0] // 2
                cp = _remote(ins[a].at[pl.ds(c * half + s, m)], outs[a].at[me, pl.ds(c * half + s, m)],
                             send_sems.at[k], recv_sems.at[k], (*chips[j], c))
                cp.start()
                sent.append(cp)

        def pass_to_sibling(k, blk):
            fw = _remote(blk, blk, send_sems.at[n_ici + k], recv_sems.at[n_ici + k], sibling)
            fw.start()
            sent.append(fw)

        for k, (a, q, s, m, j) in enumerate(entries):
            if j < 2:
                blk = landed(a, s, m, j, c)
                _remote(blk, blk, send_sems.at[k], recv_sems.at[k], (*chips[j], c)).wait_recv()
                first = q < (len(halves[a]) + 1) // 2
                if (j == 0) == first:
                    on = slot[(a, q, 2)]
                    rl = _remote(blk, blk, send_sems.at[on], recv_sems.at[on], (*chips[1 - j], c))
                    rl.start()
                    sent.append(rl)
                pass_to_sibling(k, blk)
        for k, (a, q, s, m, j) in enumerate(entries):
            if j == 2:
                blk = landed(a, s, m, j, c)
                _remote(blk, blk, send_sems.at[k], recv_sems.at[k], (*chips[j], c)).wait_recv()
                pass_to_sibling(k, blk)
        for k, (a, q, s, m, j) in enumerate(entries):
            blk = landed(a, s, m, j, 1 - c)
            _remote(blk, blk, send_sems.at[n_ici + k], recv_sems.at[n_ici + k], sibling).wait_recv()
        for cp in sent:
            cp.wait_send()

    gathered = _pcall(
        body, name="allgather_weights", in_specs=[ANY] * n, out_specs=[ANY] * n,
        out_shape=[jax.ShapeDtypeStruct((4,) + p.shape, p.dtype) for p in pieces],
        scratch_shapes=[pltpu.SemaphoreType.DMA((2 * n_ici,)), pltpu.SemaphoreType.DMA((2 * n_ici,))],
        compiler_params=pltpu.CompilerParams(has_side_effects=True),
    )(*pieces)
    x, y, _ = _place()
    return [lax.dynamic_update_slice(g, p[None], (2 * x + y, 0, 0)) for g, p in zip(gathered, pieces)]


def _sibling_exchange(grads):
    n = len(grads)
    chunks = [_row_chunks(g.shape[1] // 2, g.shape[2] * g.dtype.itemsize) for g in grads]
    n_sem = 4 * sum(len(ch) for ch in chunks)

    def body(*refs):
        ins, gots = refs[:n], refs[n:2 * n]
        send_sems, recv_sems = refs[2 * n:]
        x, y, c = _place()
        sibling = (x, y, 1 - c)
        work = []
        for a in range(n):
            half = ins[a].shape[1] // 2
            for piece in range(4):
                for s, m in chunks[a]:
                    k = len(work)
                    cp = _remote(ins[a].at[piece, pl.ds((1 - c) * half + s, m)], gots[a].at[piece, pl.ds(s, m)],
                                 send_sems.at[k], recv_sems.at[k], sibling)
                    cp.start()
                    work.append(cp)
        for cp in work:
            cp.wait()

    return _pcall(
        body, name="grad_sibling_exchange", in_specs=[ANY] * n, out_specs=[ANY] * n,
        out_shape=[jax.ShapeDtypeStruct((4, g.shape[1] // 2, g.shape[2]), g.dtype) for g in grads],
        scratch_shapes=[pltpu.SemaphoreType.DMA((n_sem,)), pltpu.SemaphoreType.DMA((n_sem,))],
        compiler_params=pltpu.CompilerParams(has_side_effects=True),
    )(*grads)


def _sibling_gather(fulls):
    n = len(fulls)
    chunks = [_row_chunks(f.shape[0] // 2, f.shape[1] * f.dtype.itemsize) for f in fulls]
    n_sem = sum(len(ch) for ch in chunks)

    def body(*refs):
        outs = refs[n:2 * n]
        send_sems, recv_sems = refs[2 * n:]
        x, y, c = _place()
        sibling = (x, y, 1 - c)
        work = []
        for a in range(n):
            h = outs[a].shape[0] // 2
            for s, m in chunks[a]:
                k = len(work)
                mine = outs[a].at[pl.ds(c * h + s, m)]
                cp = _remote(mine, mine, send_sems.at[k], recv_sems.at[k], sibling)
                cp.start()
                work.append((a, s, m, cp))
        for k, (a, s, m, cp) in enumerate(work):
            h = outs[a].shape[0] // 2
            cp.wait_send()
            theirs = outs[a].at[pl.ds((1 - c) * h + s, m)]
            _remote(theirs, theirs, send_sems.at[k], recv_sems.at[k], sibling).wait_recv()

    return _pcall(
        body, name="grad_sibling_gather", in_specs=[ANY] * n, out_specs=[ANY] * n,
        out_shape=[jax.ShapeDtypeStruct(f.shape, f.dtype) for f in fulls],
        input_output_aliases={a: a for a in range(n)},
        scratch_shapes=[pltpu.SemaphoreType.DMA((n_sem,)), pltpu.SemaphoreType.DMA((n_sem,))],
        compiler_params=pltpu.CompilerParams(has_side_effects=True),
    )(*fulls)


def _pair_sum(grad, got, place, name):
    _, rows, cols = grad.shape
    half = rows // 2
    tr = _row_tile(half, cols, 16)

    def body(p_ref, a_ref, b_ref, o_ref):
        o_ref[...] = (a_ref[...].astype(F32) + b_ref[...].astype(F32)).astype(BF16)

    return _pcall(
        body, name=name,
        grid_spec=pltpu.PrefetchScalarGridSpec(
            num_scalar_prefetch=1, grid=(4, half // tr),
            in_specs=[pl.BlockSpec((None, tr, cols), lambda k, i, p: (k, p[1] * (half // tr) + i, 0)),
                      pl.BlockSpec((None, tr, cols), lambda k, i, p: (k, i, 0))],
            out_specs=pl.BlockSpec((None, tr, cols), lambda k, i, p: (k, i, 0))),
        out_shape=jax.ShapeDtypeStruct((4, half, cols), BF16),
        compiler_params=_params("parallel", "parallel"),
    )(place, grad, got)


def _chip_sum(sums, got, place, name):
    _, h, cols = sums.shape
    tr = _row_tile(h, cols, 16)

    def body(p_ref, own_ref, g0, g1, g2, o_ref):
        o_ref[...] = ((own_ref[...].astype(F32) + g0[...].astype(F32)) + g1[...].astype(F32)) + g2[...].astype(F32)

    gspec = lambda j: pl.BlockSpec((None, tr, cols), lambda i, p: (j, i, 0))
    return _pcall(
        body, name=name,
        grid_spec=pltpu.PrefetchScalarGridSpec(
            num_scalar_prefetch=1, grid=(h // tr,),
            in_specs=[pl.BlockSpec((None, tr, cols), lambda i, p: (p[0], i, 0)), gspec(0), gspec(1), gspec(2)],
            out_specs=pl.BlockSpec((tr, cols), lambda i, p: (p[1] * (h // tr) + i, 0))),
        out_shape=jax.ShapeDtypeStruct((2 * h, cols), F32),
        compiler_params=_params("parallel"),
    )(place, sums, got, got, got)


def _allgather8(buf, name):
    rows = buf.shape[0]

    def body(in_ref, out_ref, send_sems, recv_sems):
        x, y, c = _place()
        me = 4 * x + 2 * y + c
        out_ref[me] = in_ref[...]
        work = []
        for rel in range(1, 8):
            fx, fy, fc = (rel >> 2) & 1, (rel >> 1) & 1, rel & 1
            to = (x ^ fx, y ^ fy, c ^ fc)
            cp = _remote(in_ref, out_ref.at[me], send_sems.at[rel - 1], recv_sems.at[rel - 1], to)
            cp.start()
            work.append((cp, 4 * to[0] + 2 * to[1] + to[2]))
        for rel, (cp, frm) in enumerate(work):
            cp.wait_send()
            blk = out_ref.at[frm]
            _remote(blk, blk, send_sems.at[rel], recv_sems.at[rel], (x, y, c)).wait_recv()

    return _pcall(
        body, name=name, in_specs=[pl.BlockSpec(memory_space=pltpu.VMEM)],
        out_specs=pl.BlockSpec(memory_space=pltpu.VMEM),
        out_shape=jax.ShapeDtypeStruct((8, rows, LANE), F32),
        scratch_shapes=[pltpu.SemaphoreType.DMA((7,)), pltpu.SemaphoreType.DMA((7,))],
        compiler_params=pltpu.CompilerParams(has_side_effects=True),
    )(buf)


def _pack_rows(arrs):
    parts = []
    for a in arrs:
        f = a.reshape(-1).astype(F32)
        parts.append(jnp.pad(f, (0, (-f.shape[0]) % LANE)))
    flat = jnp.concatenate(parts)
    rows = -(-flat.shape[0] // LANE)
    rows8 = -(-rows // 8) * 8
    return jnp.pad(flat, (0, rows8 * LANE - flat.shape[0])).reshape(rows8, LANE)


def _unpack_rows(buf, shapes):
    flat = buf.reshape(-1)
    outs, off = [], 0
    for s in shapes:
        n = int(np.prod(s))
        outs.append(flat[off:off + n].reshape(s))
        off += -(-n // LANE) * LANE
    return outs


def _local_grads(x, p, target, wseg, w_br16, w_out16, w_ple16, b_gate, conv_w, conv_b, dt_bias, a_log, d_skip,
                 ssm_norm_w, ln_g, ln_b, rel_bias, finish_dx):
    nb, seq, _ = x.shape
    bmaps = jnp.asarray(_bucket_maps())
    bias = _bias_tables(rel_bias, bmaps)
    bgate8 = jnp.pad(b_gate, ((0, 5), (0, 0)))
    dils = [d for _, d in PATTERNS]

    x16p = _token_orders(x, dils[1:])
    x16 = x16p[0]
    p16 = p.astype(BF16)
    qkv = [_proj(x16p[g], [wseg["qkv%d" % g]], BF16, "proj_qkv%d" % g, True, 2 * MM_TM)[0].reshape(
        nb, dils[g], seq // dils[g], -1) for g in range(3)]
    nat = {}
    for gi, (group, tm) in enumerate(NAT_GROUPS):
        outs = _proj(x16, [wseg[s] for s in group], F32, "proj_nat%d" % gi, True, tm)
        nat.update(zip(group, outs))
    att = [_attn_fwd(qkv[g], bias, g, dils[g], "attn_fwd%d" % g) for g in range(3)]
    oa, o_att, lse = _combine_fwd(att[0][0], att[0][1], att[1:], nat["gatt"])

    conv_wg, conv_bg = _xbc_group_order(conv_w), _xbc_group_order(conv_b)
    act = _conv_fwd(nat["xbc"], conv_wg, conv_bg, "conv_fwd")
    dt_sp, dt_sg = _softplus_sig(nat["dt"], jnp.pad(dt_bias, ((0, 0), (0, LANE - SSM_HEADS))))
    dtg, sgg = _group_lanes(dt_sp), _group_lanes(dt_sg)
    alog_g, dskip_g = _group_lanes(a_log), _group_lanes(d_skip)
    y_ssm, y_all, sprev = _ssd_fwd(act, dtg, nat["z"], alog_g, dskip_g, ssm_norm_w)

    w_bra, w_brb = w_br16[:ATT_OUT], w_br16[ATT_OUT:]
    y_a, y_b, merged = _merge_fwd(oa, y_ssm, w_bra, w_brb, nat["gm"], bgate8)

    dx, dpre16, dpw16, dgp16, ln_sums = _ln_loss(x, merged, w_out16, nat["gp"], p16, w_ple16, target, bgate8,
                                                 ln_g, ln_b)
    loss_sum = (0.5 / D_MODEL) * jnp.sum(ln_sums[3])
    dya16, dyb16, dgm16, mg_sums = _merge_bwd(dpre16, w_out16, y_a, y_b, nat["gm"], bgate8)
    dys = _dx([dyb16], [w_brb], [], "dx_yssm")
    g_w_out, = _dw(merged, [dpre16], BF16, "dw_out")
    g_w_br = jnp.concatenate([_dw(oa, [dya16], BF16, "dw_bra")[0], _dw(y_ssm, [dyb16], BF16, "dw_brb")[0]], axis=0)
    g_w_ple, = _dw(p16, [dpw16], BF16, "dw_ple")

    do_att, dgatt16, own_order = _combine_bwd(dya16, w_bra, nat["gatt"], o_att, lse, dils[1:])
    dseg = {"gatt": dgatt16, "gm": dgm16, "gp": dgp16}
    dbias = []
    for g in range(3):
        cotangent = (do_att, o_att, lse) if g == 0 else (own_order[2 * g - 2], own_order[2 * g - 1])
        dqkv, db = _attn_bwd(qkv[g], bias, g, cotangent, dils[g],
                             "attn_bwd%d" % g)
        dseg["qkv%d" % g] = dqkv.reshape(nb, seq, -1)
        dbias.append(db)
    g_rel = _bias_grad(jnp.concatenate(dbias, axis=0), bmaps)[:, 0, :NUM_BUCKETS].T

    dact, ddtg, dz, ssd_small, g_normw = _ssd_bwd(
        act, dtg, sgg, nat["z"], y_all, dys, sprev, alog_g, dskip_g, ssm_norm_w)
    dseg["z"] = dz
    dseg["dt"] = jnp.pad(_ungroup_lanes(ddtg), ((0, 0), (0, 0), (0, LANE - SSM_HEADS)))
    dpre, conv_sums = _conv_bwd_pre(dact, nat["xbc"], conv_wg, conv_bg, "conv_bwd")
    dseg["xbc"] = _conv_bwd_x(dpre, conv_wg, "conv_bwd_x")
    csum = _xbc_reference_order(conv_sums)

    dx_own = [_dx([dseg["qkv%d" % g]], [wseg["qkv%d" % g]], [], "dx_qkv%d" % g, True).reshape(
        nb, dils[g], seq // dils[g], D_MODEL) for g in (1, 2)]
    dwseg = {"qkv%d" % g: _dw(x16p[g], [dseg["qkv%d" % g]], BF16, "dw_qkv%d" % g, True)[0] for g in range(3)}
    for gi, group in enumerate(DW_GROUPS):
        dwseg.update(zip(group, _dw(x16, [dseg[s] for s in group], BF16, "dw_nat%d" % gi, True)))
    names = ["qkv0"] + [s for group, _ in NAT_GROUPS for s in group]
    dx = finish_dx([dseg[s] for s in names], [wseg[s] for s in names], [dx], dx_own, dwseg, g_w_br, g_w_out, g_w_ple)

    small = dict(
        b_gate=jnp.stack([mg_sums[0], mg_sums[1], ln_sums[2]]),
        conv_w=csum[0:4], conv_b=csum[4:5],
        dt_bias=_ungroup_lanes(ssd_small[:, 2:3, :]), a_log=_ungroup_lanes(ssd_small[:, 0:1, :]),
        d_skip=_ungroup_lanes(ssd_small[:, 1:2, :]), ssm_norm_w=g_normw,
        ln_g=ln_sums[0:1], ln_b=ln_sums[1:2], rel_bias=g_rel)
    return loss_sum, dx, small


DX_TM = 256
SMALL_ORDER = ("b_gate", "conv_w", "conv_b", "dt_bias", "a_log", "d_skip", "ssm_norm_w", "ln_g", "ln_b", "rel_bias")
SMALL_FULL_SHAPES = dict(b_gate=(3, 1024), conv_w=(4, 3072), conv_b=(1, 3072), dt_bias=(1, 32), a_log=(1, 32),
                         d_skip=(1, 32), ssm_norm_w=(1, 2048), ln_g=(1, 1024), ln_b=(1, 1024), rel_bias=(32, 36))


def kernel(x, p, w_in, b_gate, conv_w, conv_b, dt_bias, a_log, d_skip, ssm_norm_w, w_branch, w_out, w_ple, ln_g, ln_b, rel_bias, loss_target, m_w_in, m_b_gate, m_conv_w, m_conv_b, m_dt_bias, m_a_log, m_d_skip, m_ssm_norm_w, m_w_branch, m_w_out, m_w_ple, m_ln_g, m_ln_b, m_rel_bias, v_w_in, v_b_gate, v_conv_w, v_conv_b, v_dt_bias, v_a_log, v_d_skip, v_ssm_norm_w, v_w_branch, v_w_out, v_w_ple, v_ln_g, v_ln_b, v_rel_bias):
    cx, cy, cc = _place()
    chip = 2 * cx + cy
    dev = 4 * cx + 2 * cy + cc

    w_in_t = jnp.transpose(w_in[0])
    win16 = _shard_to_window(w_in_t, chip)
    g_win, g_br, g_out, g_ple = _allgather_pieces(
        [win16, w_branch[0].astype(BF16), w_out[0].astype(BF16), w_ple[0].astype(BF16)])
    wseg = _assemble(g_win)
    w_br16 = g_br.reshape(4 * 704, D_MODEL)
    w_out16 = g_out.reshape(D_MODEL, D_MODEL)
    w_ple16 = jnp.transpose(g_ple, (1, 0, 2)).reshape(PLE_DIM, D_MODEL)
    shards = _allgather8(_pack_rows([b_gate[0], conv_w[0]]), "allgather_small_params")
    per_chip = [_unpack_rows(shards[2 * k], [(3, 256), (4, 768)]) for k in range(4)]
    b_gate_full = jnp.concatenate([pc[0] for pc in per_chip], axis=1)
    conv_w_full = jnp.concatenate([pc[1] for pc in per_chip], axis=1)

    place = jnp.stack([chip, cc]).astype(jnp.int32)
    reduced = []

    def finish_dx(dhs, ws, accs, own_order_accs, dwseg, d_br, d_out, d_ple):
        grads = [_pack(dwseg), d_br.reshape(4, 704, D_MODEL), d_out.reshape(4, 256, D_MODEL),
                 jnp.transpose(d_ple.reshape(PLE_DIM, 4, 256), (1, 0, 2))]
        got = _sibling_exchange(grads)
        chip_sums = [_pair_sum(g, t, place, "grad_pair_sum_%d" % i) for i, (g, t) in enumerate(zip(grads, got))]
        dx, others = _dx(dhs, ws, accs, "dx_w_in_and_grad_chip_scatter", True, DX_TM, chip_sums, own_order_accs)
        fulls = [_chip_sum(s, t, place, "grad_chip_sum_%d" % i) for i, (s, t) in enumerate(zip(chip_sums, others))]
        reduced.extend(_sibling_gather(fulls))
        return dx

    loss_sum, grad_x, small = _local_grads(
        x, p[0], loss_target, wseg, w_br16, w_out16, w_ple16, b_gate_full, conv_w_full, conv_b, dt_bias, a_log,
        d_skip, ssm_norm_w, ln_g, ln_b, rel_bias, finish_dx)
    big = reduced
    g_w_in = _window_to_shard(big[0], chip)
    g_w_branch, g_w_out, g_w_ple = big[1], big[2], big[3]
    parts = _allgather8(_pack_rows([small[n] for n in SMALL_ORDER] + [loss_sum.reshape(1, 1)]),
                        "allgather_small_grads")
    small_sum = _sum_rows([parts[i] for i in range(8)], F32, "small_grad_sum")
    *reduced_small, loss = _unpack_rows(small_sum, [SMALL_FULL_SHAPES[n] for n in SMALL_ORDER] + [(1, 1)])
    loss = loss.reshape(())
    sg = dict(zip(SMALL_ORDER, reduced_small))
    sg["b_gate"] = lax.dynamic_slice_in_dim(sg["b_gate"], chip * 256, 256, axis=1)
    sg["conv_w"] = lax.dynamic_slice_in_dim(sg["conv_w"], chip * 768, 768, axis=1)
    del dev

    upd = {}
    upd["w_in"] = [jnp.transpose(t) for t in _adamw(w_in_t, g_w_in, jnp.transpose(m_w_in[0]),
                                                      jnp.transpose(v_w_in[0]), "adamw_w_in")]
    upd["w_branch"] = _adamw(w_branch[0], g_w_branch, m_w_branch[0], v_w_branch[0], "adamw_w_branch")
    upd["w_out"] = _adamw(w_out[0], g_w_out, m_w_out[0], v_w_out[0], "adamw_w_out")
    upd["w_ple"] = _adamw(w_ple[0], g_w_ple, m_w_ple[0], v_w_ple[0], "adamw_w_ple")
    small_w = dict(b_gate=b_gate, conv_w=conv_w, conv_b=conv_b, dt_bias=dt_bias, a_log=a_log, d_skip=d_skip,
                   ssm_norm_w=ssm_norm_w, ln_g=ln_g, ln_b=ln_b, rel_bias=rel_bias)
    small_m = dict(b_gate=m_b_gate, conv_w=m_conv_w, conv_b=m_conv_b, dt_bias=m_dt_bias, a_log=m_a_log,
                   d_skip=m_d_skip, ssm_norm_w=m_ssm_norm_w, ln_g=m_ln_g, ln_b=m_ln_b, rel_bias=m_rel_bias)
    small_v = dict(b_gate=v_b_gate, conv_w=v_conv_w, conv_b=v_conv_b, dt_bias=v_dt_bias, a_log=v_a_log,
                   d_skip=v_d_skip, ssm_norm_w=v_ssm_norm_w, ln_g=v_ln_g, ln_b=v_ln_b, rel_bias=v_rel_bias)
    shapes = [small_w[n].shape for n in SMALL_ORDER]
    s_delta, s_m, s_v = _adamw(_pack_rows([small_w[n] for n in SMALL_ORDER]), _pack_rows([sg[n] for n in SMALL_ORDER]),
                               _pack_rows([small_m[n] for n in SMALL_ORDER]), _pack_rows([small_v[n] for n in SMALL_ORDER]),
                               "adamw_small")
    for i, n in enumerate(SMALL_ORDER):
        upd[n] = tuple(_unpack_rows(t, shapes)[i] for t in (s_delta, s_m, s_v))
        sg[n] = sg[n].reshape(small_w[n].shape)

    order = ("w_in", "b_gate", "conv_w", "conv_b", "dt_bias", "a_log", "d_skip", "ssm_norm_w", "w_branch", "w_out",
             "w_ple", "ln_g", "ln_b", "rel_bias")
    grads = dict(sg, w_in=jnp.transpose(g_w_in)[None],w_branch=g_w_branch[None], w_out=g_w_out[None], w_ple=g_w_ple[None])
    lead = lambda n, t: t[None] if n in ("w_in", "w_branch", "w_out", "w_ple") else t
    return (loss, grad_x, *[grads[n] for n in order], *[lead(n, upd[n][0]) for n in order],
            *[lead(n, upd[n][1]) for n in order], *[lead(n, upd[n][2]) for n in order])
```

```python
import math

import numpy as np
import jax
import jax.numpy as jnp
from jax import lax
from jax.experimental import pallas as pl
from jax.experimental.pallas import tpu as pltpu

F32, BF16 = jnp.float32, jnp.bfloat16

D_MODEL = 1024
HEAD_DIM = 64
GROUP_HEADS = 12
ATT_OUT = GROUP_HEADS * HEAD_DIM
PATTERNS = ((128, 1), (512, 4), (2048, 16))
BAND = 128
NUM_BUCKETS = 32
MAX_DISTANCE = 2048
D_INNER = 2048
SSM_HEADS = 32
SSM_GROUPS = 4
GROUP_SSM_HEADS = SSM_HEADS // SSM_GROUPS
D_STATE = 128
CHUNK = 128
PLE_DIM = 256
ALPHA = 2.0 ** 0.25
LN_EPS = 1e-5
RMS_EPS = 1e-5
ADAM_LR, ADAM_B1, ADAM_B2, ADAM_EPS, ADAM_WD, ADAM_STEP = 0.001, 0.9, 0.999, 1e-08, 0.01, 10
NEG = -1e30

QKV_W = 3 * ATT_OUT
IN_COLS = 15904
SHARD_COLS = IN_COLS // 4
DT_COL = 12800
ROW_TILE = 16
WIN_ROWS = 4000


def _win_offset(k):
    return (k * SHARD_COLS) % ROW_TILE


def _win_start(k):
    return k * SHARD_COLS - _win_offset(k)

VMEM_LIMIT_BYTES = 56 * 1024 * 1024
LANE = 128
MESH = pl.DeviceIdType.MESH
NT = (((1,), (1,)), ((), ()))
TN = (((0,), (0,)), ((), ()))


def _pcall(body, **kw):
    return pl.pallas_call(body, **kw)


def _params(*sem):
    return pltpu.CompilerParams(dimension_semantics=sem, vmem_limit_bytes=VMEM_LIMIT_BYTES)


def _sigmoid(v):
    return jax.nn.sigmoid(v)


MM_TM = 512


def _tok_spec(tm, width):
    return pl.BlockSpec((None, tm, width), lambda b, i: (b, i, 0))


def _whole(arr, single_buffer=False):
    mode = dict(pipeline_mode=pl.Buffered(1)) if single_buffer else {}
    return pl.BlockSpec(arr.shape, lambda b, i: (0,) * arr.ndim, **mode)


def _proj(a3, ws, out_dtype, name, w_rows_are_outputs=False, tm=MM_TM):
    nb, seq, kdim = a3.shape
    nw = len(ws)
    widths = [w.shape[0] if w_rows_are_outputs else w.shape[1] for w in ws]

    def body(*refs):
        a = refs[0][...].astype(BF16)
        for w_ref, o_ref in zip(refs[1:1 + nw], refs[1 + nw:]):
            if w_rows_are_outputs:
                v = lax.dot_general(a, w_ref[...], NT, preferred_element_type=F32)
            else:
                v = jnp.dot(a, w_ref[...], preferred_element_type=F32)
            o_ref[...] = v.astype(out_dtype)

    return _pcall(
        body, name=name, grid=(nb, seq // tm),
        in_specs=[_tok_spec(tm, kdim)] + [_whole(w, True) for w in ws],
        out_specs=[_tok_spec(tm, n) for n in widths],
        out_shape=[jax.ShapeDtypeStruct((nb, seq, n), out_dtype) for n in widths],
        compiler_params=_params("parallel", "parallel"),
    )(a3, *ws)


def _dx(dhs, ws, accs, name, w_rows_are_outputs=False, tm=MM_TM, scatter=None, own_order_accs=()):
    nb, seq, _ = dhs[0].shape
    nd, nacc, npa = len(dhs), len(accs), len(own_order_accs)
    kout = ws[0].shape[1] if w_rows_are_outputs else ws[0].shape[0]
    sums = scatter or []
    ns = len(sums)
    chunks = [_row_chunks(s.shape[1], s.shape[2] * s.dtype.itemsize, ICI_CHUNK_BYTES) for s in sums]
    n_sem = 3 * sum(len(ch) for ch in chunks)
    grid = (nb, seq // tm)
    ntile = kout // LANE if npa else 0

    def body(*refs):
        n_in = 2 * nd + nacc + npa
        sum_refs, o_ref, got_refs = refs[n_in:n_in + ns], refs[n_in + ns], refs[n_in + ns + 1:n_in + 2 * ns + 1]
        tile_refs = refs[n_in + 2 * ns + 1:n_in + 2 * ns + 1 + ntile]

        def copies():
            send_sems, recv_sems = refs[-2], refs[-1]
            x, y, c = _place()
            out = []
            for a in range(ns):
                for s, m in chunks[a]:
                    for j, (cx, cy) in enumerate(_other_chips(x, y)):
                        k = len(out)
                        out.append(_remote(sum_refs[a].at[2 * cx + cy, pl.ds(s, m)], got_refs[a].at[j, pl.ds(s, m)],
                                           send_sems.at[k], recv_sems.at[k], (cx, cy, c)))
            return out

        if ns:
            @pl.when((pl.program_id(0) == 0) & (pl.program_id(1) == 0))
            def _():
                for cp in copies():
                    cp.start()

        v = None
        for dh_ref, w_ref in zip(refs[:nd], refs[nd:2 * nd]):
            dh = dh_ref[...].astype(BF16)
            if w_rows_are_outputs:
                t = jnp.dot(dh, w_ref[...], preferred_element_type=F32)
            else:
                t = lax.dot_general(dh, w_ref[...], NT, preferred_element_type=F32)
            v = t if v is None else v + t
        for a_ref in refs[2 * nd:2 * nd + nacc]:
            v = v + a_ref[...]
        for p_ref in refs[2 * nd + nacc:n_in]:
            v = v + _natural_rows(p_ref, tile_refs)
        o_ref[...] = v

        if ns:
            @pl.when((pl.program_id(0) == grid[0] - 1) & (pl.program_id(1) == grid[1] - 1))
            def _():
                for cp in copies():
                    cp.wait()

    out = _pcall(
        body, name=name, grid=grid,
        in_specs=[_tok_spec(tm, dh.shape[-1]) for dh in dhs] + [_whole(w, True) for w in ws]
        + [_tok_spec(tm, kout)] * nacc
        + [pl.BlockSpec((None, p.shape[1], tm // p.shape[1], kout), lambda b, i: (b, 0, i, 0)) for p in own_order_accs]
        + [ANY] * ns,
        out_specs=[_tok_spec(tm, kout)] + [ANY] * ns,
        out_shape=[jax.ShapeDtypeStruct((nb, seq, kout), F32)]
        + [jax.ShapeDtypeStruct((3,) + s.shape[1:], s.dtype) for s in sums],
        input_output_aliases={2 * nd: 0} if nacc else {},
        scratch_shapes=[pltpu.VMEM((tm, LANE), F32)] * ntile
        + ([pltpu.SemaphoreType.DMA((n_sem,)), pltpu.SemaphoreType.DMA((n_sem,))] if ns else []),
        compiler_params=pltpu.CompilerParams(
            dimension_semantics=("arbitrary", "arbitrary") if ns else ("parallel", "parallel"),
            vmem_limit_bytes=VMEM_LIMIT_BYTES, has_side_effects=bool(ns)),
    )(*dhs, *ws, *accs, *own_order_accs, *sums)
    return (out[0], list(out[1:])) if ns else out[0]


def _dw(a3, dhs, out_dtype, name, rows_are_outputs=False):
    nb, seq, kdim = a3.shape
    nd = len(dhs)
    grid = (nb, seq // MM_TM)
    shapes = [(dh.shape[-1], kdim) if rows_are_outputs else (kdim, dh.shape[-1]) for dh in dhs]

    def body(*refs):
        b, i = pl.program_id(0), pl.program_id(1)
        dh_refs, o_refs, acc_refs = refs[1:1 + nd], refs[1 + nd:1 + 2 * nd], refs[1 + 2 * nd:]

        @pl.when((b == 0) & (i == 0))
        def _():
            for acc_ref in acc_refs:
                acc_ref[...] = jnp.zeros_like(acc_ref)

        a = refs[0][...].astype(BF16)
        for dh_ref, acc_ref in zip(dh_refs, acc_refs):
            dh = dh_ref[...].astype(BF16)
            acc_ref[...] += lax.dot_general(*((dh, a) if rows_are_outputs else (a, dh)), TN,
                                            preferred_element_type=F32)

        @pl.when((b == grid[0] - 1) & (i == grid[1] - 1))
        def _():
            for o_ref, acc_ref in zip(o_refs, acc_refs):
                o_ref[...] = acc_ref[...].astype(out_dtype)

    return _pcall(
        body, name=name, grid=grid,
        in_specs=[_tok_spec(MM_TM, kdim)] + [_tok_spec(MM_TM, dh.shape[-1]) for dh in dhs],
        out_specs=[pl.BlockSpec(s, lambda b, i: (0, 0)) for s in shapes],
        out_shape=[jax.ShapeDtypeStruct(s, out_dtype) for s in shapes],
        scratch_shapes=[pltpu.VMEM(s, F32) for s in shapes],
        compiler_params=_params("arbitrary", "arbitrary"),
    )(a3, *dhs)


def _qkv_rows(g):
    return [(part * QKV_W + g * ATT_OUT + hp * LANE, LANE) for hp in range(ATT_OUT // LANE) for part in range(3)]


XBC_START = 3 * QKV_W + ATT_OUT + D_INNER
GROUP_CH = GROUP_SSM_HEADS * HEAD_DIM
XBC_GROUP = GROUP_CH + 2 * D_STATE
CONV_DIM = SSM_GROUPS * XBC_GROUP


def _xbc_ranges():
    out = []
    for g in range(SSM_GROUPS):
        out += [(g * GROUP_CH, GROUP_CH), (D_INNER + g * D_STATE, D_STATE),
                (D_INNER + SSM_GROUPS * D_STATE + g * D_STATE, D_STATE)]
    return out


def _xbc_group_order(t):
    return jnp.concatenate([t[..., s:s + n] for s, n in _xbc_ranges()], axis=-1)


def _xbc_reference_order(t):
    g = lambda off, n: [t[..., k * XBC_GROUP + off:k * XBC_GROUP + off + n] for k in range(SSM_GROUPS)]
    return jnp.concatenate(g(0, GROUP_CH) + g(GROUP_CH, D_STATE) + g(GROUP_CH + D_STATE, D_STATE), axis=-1)


def _segments():
    one = lambda name, start, rows: (name, [(start, rows)], max(rows, LANE))
    return [("qkv%d" % g, _qkv_rows(g), QKV_W) for g in range(3)] + [
        one("gatt", 3 * QKV_W, ATT_OUT), one("z", 3 * QKV_W + ATT_OUT, D_INNER),
        ("xbc", [(XBC_START + s, n) for s, n in _xbc_ranges()], CONV_DIM), one("dt", DT_COL, SSM_HEADS),
        one("gm", DT_COL + SSM_HEADS, 2 * D_MODEL), one("gp", DT_COL + SSM_HEADS + 2 * D_MODEL, D_MODEL)]


LAYOUT_TC = 256
NAT_GROUPS = ((("gatt", "z", "dt", "gp"), 512), (("xbc", "gm"), 512))
DW_GROUPS = (("gatt", "z", "dt", "gp"), ("xbc",), ("gm",))


def _assemble(win):
    segs = _segments()

    def body(win_ref, *outs):
        def pieces(start, rows):
            t, end = start, start + rows
            while t < end:
                k = min(t // SHARD_COLS, 3)
                shard_end = (k + 1) * SHARD_COLS
                if k < 3 and shard_end % ROW_TILE and t == shard_end - shard_end % ROW_TILE:
                    lo = t - _win_start(k)
                    yield win_ref[k, lo:lo + ROW_TILE, :] + win_ref[k + 1, 0:ROW_TILE, :]
                    t += ROW_TILE
                    continue
                upto = min(end, shard_end - shard_end % ROW_TILE if k < 3 else end)
                yield win_ref[k, t - _win_start(k):upto - _win_start(k), :]
                t = upto

        for (_, ranges, total), o_ref in zip(segs, outs):
            off = 0
            for start, rows in ranges:
                for part in pieces(start, rows):
                    o_ref[off:off + part.shape[0], :] = part
                    off += part.shape[0]
            if off < total:
                o_ref[off:total, :] = jnp.zeros((total - off, o_ref.shape[1]), BF16)

    outs = _pcall(
        body, name="assemble_w_in", grid=(D_MODEL // LAYOUT_TC,),
        in_specs=[pl.BlockSpec((4, WIN_ROWS, LAYOUT_TC), lambda i: (0, 0, i))],
        out_specs=[pl.BlockSpec((total, LAYOUT_TC), lambda i: (0, i)) for _, _, total in segs],
        out_shape=[jax.ShapeDtypeStruct((total, D_MODEL), BF16) for _, _, total in segs],
        compiler_params=_params("parallel"),
    )(win)
    return {name: o for (name, _, _), o in zip(segs, outs)}


def _pack(dsegs):
    segs = _segments()

    def body(*refs):
        ins, o_ref = refs[:-1], refs[-1]
        tail = IN_COLS - _win_start(3)
        o_ref[3, tail:, :] = jnp.zeros((WIN_ROWS - tail, o_ref.shape[2]), BF16)
        for (_, ranges, _), s_ref in zip(segs, ins):
            off = 0
            for start, rows in ranges:
                for k in range(4):
                    lo = _win_start(k)
                    a, b = max(start, lo), min(start + rows, lo + WIN_ROWS)
                    if a < b:
                        o_ref[k, a - lo:b - lo, :] = s_ref[off + a - start:off + b - start, :]
                off += rows

    return _pcall(
        body, name="pack_dw_in", grid=(D_MODEL // LAYOUT_TC,),
        in_specs=[pl.BlockSpec((total, LAYOUT_TC), lambda i: (0, i)) for _, _, total in segs],
        out_specs=pl.BlockSpec((4, WIN_ROWS, LAYOUT_TC), lambda i: (0, 0, i)),
        out_shape=jax.ShapeDtypeStruct((4, WIN_ROWS, D_MODEL), BF16),
        compiler_params=_params("parallel"),
    )(*[dsegs[name] for name, _, _ in segs])


def _shard_to_window(shard_t, k):
    def at(off):
        return lambda w: jnp.pad(w.astype(BF16), ((off, WIN_ROWS - SHARD_COLS - off), (0, 0)))

    return lax.cond(k % 2 == 1, at(_win_offset(1)), at(_win_offset(0)), shard_t)


def _window_to_shard(win, k):
    return lax.dynamic_slice(win, ((k % 2) * _win_offset(1), 0), (SHARD_COLS, D_MODEL))


def _bucket_maps():
    qi = np.arange(8)[:, None]
    kj = np.arange(2 * BAND)[None, :]
    delta = qi + BAND - kj
    maps = []
    for window, dil in PATTERNS:
        valid = (delta >= 0) & (delta <= window // dil)
        dist = np.maximum(delta, 0) * dil
        max_exact = NUM_BUCKETS // 2
        d_f = np.maximum(dist, 1).astype(np.float32)
        large = max_exact + (np.log(d_f / np.float32(max_exact)) / np.float32(math.log(MAX_DISTANCE / max_exact))
                             * np.float32(NUM_BUCKETS - max_exact)).astype(np.int32)
        large = np.minimum(large, NUM_BUCKETS - 1)
        bucket = np.where(dist < max_exact, dist, large)
        maps.append(np.where(valid, bucket, -1).astype(np.int32))
    return np.stack(maps)


def _bias_tables(rel_bias, bmaps):
    def body(rb_ref, bm_ref, o_ref):
        g = pl.program_id(0)
        bm = bm_ref[...]
        for hh in range(GROUP_HEADS):
            acc = jnp.full(bm.shape, NEG, F32)
            for b in range(NUM_BUCKETS):
                acc = jnp.where(bm == b, rb_ref[b, g * GROUP_HEADS + hh], acc)
            for a in range(BAND // 8):
                o_ref[hh, 8 * a:8 * a + 8, :] = acc if a == 0 else pltpu.roll(acc, 8 * a, 1)

    return _pcall(
        body, name="bias_tables", grid=(3,),
        in_specs=[pl.BlockSpec(memory_space=pltpu.SMEM),
                  pl.BlockSpec((None, 8, 2 * BAND), lambda g: (g, 0, 0))],
        out_specs=pl.BlockSpec((GROUP_HEADS, BAND, 2 * BAND), lambda g: (g, 0, 0)),
        out_shape=jax.ShapeDtypeStruct((3 * GROUP_HEADS, BAND, 2 * BAND), F32),
        compiler_params=_params("parallel"),
    )(rel_bias, bmaps)


def _bias_grad(dbias, bmaps):
    def body(db_ref, bm_ref, o_ref):
        bm = bm_ref[...]
        lane = lax.broadcasted_iota(jnp.int32, (1, LANE), 1)
        for hh in range(GROUP_HEADS):
            db = db_ref[hh, 0:8, :]
            for a in range(1, BAND // 8):
                db = db + pltpu.roll(db_ref[hh, 8 * a:8 * a + 8, :], 2 * BAND - 8 * a, 1)
            vec = jnp.zeros((1, LANE), F32)
            for b in range(NUM_BUCKETS):
                s = jnp.sum(jnp.where(bm == b, db, 0.0), keepdims=True)
                vec = jnp.where(lane == b, s, vec)
            o_ref[hh] = vec

    return _pcall(
        body, name="bias_grad", grid=(3,),
        in_specs=[pl.BlockSpec((GROUP_HEADS, BAND, 2 * BAND), lambda g: (g, 0, 0)),
                  pl.BlockSpec((None, 8, 2 * BAND), lambda g: (g, 0, 0))],
        out_specs=pl.BlockSpec((GROUP_HEADS, 1, LANE), lambda g: (g, 0, 0)),
        out_shape=jax.ShapeDtypeStruct((3 * GROUP_HEADS, 1, LANE), F32),
        compiler_params=_params("parallel"),
    )(dbias, bmaps)


def _rows(n):
    if isinstance(n, int):
        return pl.ds(n * BAND, BAND)
    return pl.ds(pl.multiple_of(n * BAND, BAND), BAND)


def _for_blocks(blocks, nblk, per, carry):
    carry = blocks([0], carry, False)
    start = 1 + (nblk - 1) % per
    for n in range(1, start):
        carry = blocks([n], carry, True)
    trips = (nblk - start) // per
    if trips > 0:
        carry = lax.fori_loop(
            0, trips, lambda t, c: blocks([start + t * per + u for u in range(per)], c, True), carry)
    return carry


def _pairs_per_step(d):
    return {1: 3, 4: 6, 16: 6}[d]


def _bias_spec(group, hps):
    first = group * GROUP_HEADS // (2 * hps)
    return pl.BlockSpec((2 * hps, BAND, 2 * BAND), lambda hp, b, r: (first + hp, 0, 0))


def _attn_fwd(qkv4, bias, group, d, name):
    nb, _, sub, _ = qkv4.shape
    nblk = sub // BAND
    scale = HEAD_DIM ** -0.5
    npair = ATT_OUT // LANE
    hps = _pairs_per_step(d)
    compact = d > 1

    def body(qkv_ref, bias_ref, o_ref, l_ref):
        def blocks(ns, carry, with_prev):
            chains = [(bi, i, h) for bi in range(len(ns)) for i in range(hps) for h in range(2)]
            first_head = lax.broadcasted_iota(jnp.int32, (BAND, LANE), 1) < HEAD_DIM
            pair = lambda n, i, part: qkv_ref[_rows(n), (3 * i + part) * LANE:(3 * i + part + 1) * LANE]
            scores = []
            for bi, i, h in chains:
                n = ns[bi]
                qp = pair(n, i, 0) * scale
                q = jnp.where(first_head if h == 0 else jnp.logical_not(first_head), qp, jnp.zeros_like(qp))
                s_c = lax.dot_general(q, pair(n, i, 1), NT, preferred_element_type=F32) + bias_ref[2 * i + h, :, BAND:]
                s_p = None
                if with_prev:
                    s_p = lax.dot_general(q, pair(n - 1, i, 1), NT,
                                          preferred_element_type=F32) + bias_ref[2 * i + h, :, :BAND]
                scores.append((s_c, s_p))
            probs = []
            for s_c, s_p in scores:
                m = jnp.max(s_c, -1, keepdims=True)
                if with_prev:
                    m = jnp.maximum(m, jnp.max(s_p, -1, keepdims=True))
                e_c = jnp.exp(s_c - m)
                den = jnp.sum(e_c, -1, keepdims=True)
                e_p = None
                if with_prev:
                    e_p = jnp.exp(s_p - m)
                    den = den + jnp.sum(e_p, -1, keepdims=True)
                    e_p = e_p.astype(BF16)
                probs.append((e_c.astype(BF16), e_p, den, m))
            outs = {}
            for (bi, i, h), (e_c, e_p, den, m) in zip(chains, probs):
                n = ns[bi]
                acc = jnp.dot(e_c, pair(n, i, 2), preferred_element_type=F32)
                if with_prev:
                    acc = acc + jnp.dot(e_p, pair(n - 1, i, 2), preferred_element_type=F32)
                outs[(bi, i, h)] = (acc / den, m + jnp.log(den))
            lane = lax.broadcasted_iota(jnp.int32, (BAND, LANE), 1)
            for bi, n in enumerate(ns):
                per_head = jnp.zeros((BAND, LANE), F32)
                for i in range(hps):
                    o_ref[_rows(n), i * LANE:(i + 1) * LANE] = jnp.where(first_head, outs[(bi, i, 0)][0],
                                                                         outs[(bi, i, 1)][0])
                    if compact:
                        for h in range(2):
                            per_head = jnp.where(lane == 2 * i + h, outs[(bi, i, h)][1], per_head)
                    else:
                        l_ref[_rows(n), i * LANE:(i + 1) * LANE] = jnp.where(first_head, outs[(bi, i, 0)][1],
                                                                             outs[(bi, i, 1)][1])
                if compact:
                    l_ref[_rows(n), :] = per_head
            return carry

        _for_blocks(blocks, nblk, 2 if hps == 1 else 1, 0)

    in_specs = [pl.BlockSpec((None, None, sub, 3 * LANE * hps), lambda hp, b, r: (b, r, 0, hp)),
                _bias_spec(group, hps)]
    if compact:
        return _pcall(
            body, name=name, grid=(1, nb, d), in_specs=in_specs,
            out_specs=[pl.BlockSpec((None, None, sub, ATT_OUT), lambda hp, b, r: (b, r, 0, 0)),
                       pl.BlockSpec((None, None, sub, LANE), lambda hp, b, r: (b, r, 0, 0))],
            out_shape=[jax.ShapeDtypeStruct((nb, d, sub, ATT_OUT), F32), jax.ShapeDtypeStruct((nb, d, sub, LANE), F32)],
            compiler_params=_params("parallel", "parallel", "parallel"),
        )(qkv4, bias)
    ospec = pl.BlockSpec((None, sub, hps * LANE), lambda hp, b, r: (b, 0, r * (npair // hps) + hp))
    return _pcall(
        body, name=name, grid=(npair // hps, nb, d), in_specs=in_specs, out_specs=[ospec, ospec],
        out_shape=[jax.ShapeDtypeStruct((nb, sub, d * ATT_OUT), F32)] * 2,
        compiler_params=_params("parallel", "parallel", "parallel"),
    )(qkv4, bias)


STAT_LSE_LANE = 16


def _attn_bwd(qkv4, bias, group, cotangent, d, name):
    nb, _, sub, _ = qkv4.shape
    nblk = sub // BAND
    scale = HEAD_DIM ** -0.5
    npair = ATT_OUT // LANE
    hps = _pairs_per_step(d)
    compact = d > 1

    def body(qkv_ref, bias_ref, *rest):
        do_ref, dqkv_ref, db_ref = rest[0], rest[-2], rest[-1]
        b, r = pl.program_id(1), pl.program_id(2)

        @pl.when((b == 0) & (r == 0))
        def _():
            db_ref[...] = jnp.zeros_like(db_ref)

        def blocks(ns, carry, with_prev):
            sides = (0, 1) if with_prev else (0,)
            chains = [(bi, i, h, sd) for bi in range(len(ns)) for i in range(hps) for h in range(2) for sd in sides]
            first_head = lax.broadcasted_iota(jnp.int32, (BAND, LANE), 1) < HEAD_DIM
            own = lambda h, t: jnp.where(first_head if h == 0 else jnp.logical_not(first_head), t, jnp.zeros_like(t))
            pair = lambda rows, i, part: qkv_ref[rows, (3 * i + part) * LANE:(3 * i + part + 1) * LANE]
            key_rows = lambda bi, sd: _rows(ns[bi] - sd)
            qs = {}
            for bi in range(len(ns)):
                for i in range(hps):
                    q_pair = pair(_rows(ns[bi]), i, 0) * scale
                    do = do_ref[_rows(ns[bi]), i * LANE:(i + 1) * LANE]
                    do16 = do.astype(BF16)
                    for h in range(2):
                        if compact:
                            st_ref, head = rest[1], 2 * i + h
                            ebar = st_ref[_rows(ns[bi]), head:head + 1]
                            lcol = st_ref[_rows(ns[bi]), STAT_LSE_LANE + head:STAT_LSE_LANE + head + 1]
                        else:
                            ebar = jnp.sum(own(h, do * rest[1][_rows(ns[bi]), i * LANE:(i + 1) * LANE]), -1, keepdims=True)
                            lcol = rest[2][_rows(ns[bi]), i * LANE + h * HEAD_DIM:i * LANE + h * HEAD_DIM + 1]
                        qs[(bi, i, h)] = (own(h, q_pair), q_pair, own(h, do16), do16, ebar, lcol)
            raw = []
            for bi, i, h, sd in chains:
                q, _, do_h, _, _, _ = qs[(bi, i, h)]
                bias_blk = bias_ref[2 * i + h, :, :BAND] if sd else bias_ref[2 * i + h, :, BAND:]
                s = lax.dot_general(q, pair(key_rows(bi, sd), i, 1), NT, preferred_element_type=F32) + bias_blk
                dp = lax.dot_general(do_h, pair(key_rows(bi, sd), i, 2), NT, preferred_element_type=F32)
                raw.append((s, dp))
            soft = []
            for (bi, i, h, sd), (s, dp) in zip(chains, raw):
                ebar, lcol = qs[(bi, i, h)][4:]
                p = jnp.exp(s - lcol)
                ds = p * (dp - ebar)
                if sd:
                    db_ref[2 * i + h, :, :BAND] += ds
                else:
                    db_ref[2 * i + h, :, BAND:] += ds
                soft.append((p.astype(BF16), ds.astype(BF16)))
            grads = {}
            for (bi, i, h, sd), (p16, ds16) in zip(chains, soft):
                _, q_pair, _, do16 = qs[(bi, i, h)][:4]
                grads[(bi, i, h, sd)] = (
                    jnp.dot(ds16, pair(key_rows(bi, sd), i, 1), preferred_element_type=F32),
                    lax.dot_general(ds16, q_pair, TN, preferred_element_type=F32),
                    lax.dot_general(p16, do16, TN, preferred_element_type=F32))
            both = lambda bi, i, sd, which: jnp.where(first_head, grads[(bi, i, 0, sd)][which],
                                                      grads[(bi, i, 1, sd)][which])
            carry = list(carry) if carry is not None else None
            for bi, n in enumerate(ns):
                for i in range(hps):
                    base = 3 * LANE * i
                    dq = both(bi, i, 0, 0)
                    if with_prev:
                        dq = dq + both(bi, i, 1, 0)
                        dqkv_ref[_rows(n - 1), base + LANE:base + 2 * LANE] = (
                            carry[2 * i] + both(bi, i, 1, 1)).astype(BF16)
                        dqkv_ref[_rows(n - 1), base + 2 * LANE:base + 3 * LANE] = (
                            carry[2 * i + 1] + both(bi, i, 1, 2)).astype(BF16)
                    dqkv_ref[_rows(n), base:base + LANE] = (dq * scale).astype(BF16)
                carry = [t for i in range(hps) for t in (both(bi, i, 0, 1), both(bi, i, 0, 2))]
            return tuple(carry)

        carry = _for_blocks(blocks, nblk, 2 if hps == 1 else 1, None)
        for i in range(hps):
            base = 3 * LANE * i
            dqkv_ref[_rows(nblk - 1), base + LANE:base + 2 * LANE] = carry[2 * i].astype(BF16)
            dqkv_ref[_rows(nblk - 1), base + 2 * LANE:base + 3 * LANE] = carry[2 * i + 1].astype(BF16)

    qspec = pl.BlockSpec((None, None, sub, 3 * LANE * hps), lambda hp, b, r: (b, r, 0, hp))
    bspec = pl.BlockSpec((2 * hps, BAND, 2 * BAND), lambda hp, b, r: (hp, 0, 0))
    if compact:
        cspecs = [pl.BlockSpec((None, None, sub, ATT_OUT), lambda hp, b, r: (b, r, 0, 0)),
                  pl.BlockSpec((None, None, sub, LANE), lambda hp, b, r: (b, r, 0, 0))]
    else:
        cspecs = [pl.BlockSpec((None, sub, hps * LANE), lambda hp, b, r: (b, 0, r * (npair // hps) + hp))] * 3
    return _pcall(
        body, name=name, grid=(npair // hps, nb, d),
        in_specs=[qspec, _bias_spec(group, hps)] + cspecs, out_specs=[qspec, bspec],
        out_shape=[jax.ShapeDtypeStruct(qkv4.shape, BF16),
                   jax.ShapeDtypeStruct((GROUP_HEADS, BAND, 2 * BAND), F32)],
        compiler_params=_params("parallel", "arbitrary", "arbitrary"),
    )(qkv4, bias, *cotangent)


def _head_lanes(first_lane, one_channel):
    c = lax.broadcasted_iota(jnp.int32, (ATT_OUT, LANE), 0)
    lane = lax.broadcasted_iota(jnp.int32, (ATT_OUT, LANE), 1)
    hit = lane == first_lane + c // HEAD_DIM
    if one_channel:
        hit = hit & (c % HEAD_DIM == 0)
    return hit.astype(BF16)


def _exact_dot(v, m01, dims=None):
    parts = _split3(v)
    if dims is None:
        dot = lambda t: jnp.dot(t, m01, preferred_element_type=F32)
    else:
        dot = lambda t: lax.dot_general(t, m01, dims, preferred_element_type=F32)
    return (dot(parts[0]) + dot(parts[1])) + dot(parts[2])


def _store_own_order(value, tile_refs, out_ref):
    d, per, width = out_ref.shape
    for j in range(width // LANE):
        tile_refs[j][...] = value[:, j * LANE:(j + 1) * LANE]
    for r in range(d):
        rows = pl.ds(r, per, stride=d)
        for j in range(width // LANE):
            out_ref[r, :, j * LANE:(j + 1) * LANE] = tile_refs[j][rows, :].astype(out_ref.dtype)


def _token_orders(x, dilations):
    nb, seq, kdim = x.shape
    tm = 512

    def body(x_ref, nat_ref, *rest):
        outs, tile_refs = rest[:len(dilations)], rest[len(dilations):]
        xv = x_ref[...]
        nat_ref[...] = xv.astype(BF16)
        for o_ref in outs:
            _store_own_order(xv, tile_refs, o_ref)

    outs = _pcall(
        body, name="token_orders", grid=(nb, seq // tm), in_specs=[_tok_spec(tm, kdim)],
        out_specs=[_tok_spec(tm, kdim)]
        + [pl.BlockSpec((None, d, tm // d, kdim), lambda b, i: (b, 0, i, 0)) for d in dilations],
        out_shape=[jax.ShapeDtypeStruct((nb, seq, kdim), BF16)]
        + [jax.ShapeDtypeStruct((nb, d, seq // d, kdim), BF16) for d in dilations],
        scratch_shapes=[pltpu.VMEM((tm, LANE), F32)] * (kdim // LANE),
        compiler_params=_params("parallel", "parallel"),
    )(x)
    return [outs[0]] + [o.reshape(nb, seq, kdim) for o in outs[1:]]


def _natural_rows(p_ref, tile_refs):
    d, per, width = p_ref.shape
    for r in range(d):
        rows = pl.ds(r, per, stride=d)
        for j in range(width // LANE):
            tile_refs[j][rows, :] = p_ref[r, :, j * LANE:(j + 1) * LANE]
    return jnp.concatenate([tile_refs[j][...] for j in range(width // LANE)], axis=1)


def _combine_fwd(o0, l0, dilated, gatt):
    nb, seq, _ = gatt.shape
    tm = 512
    ntile = ATT_OUT // LANE

    def body(o0_ref, l0_ref, o1_ref, l1_ref, o2_ref, l2_ref, g_ref, oa_ref, oatt_ref, lse_ref, *tile_refs):
        spread = _head_lanes(0, False)
        l0v = l0_ref[...]
        l1v = _exact_dot(_natural_rows(l1_ref, tile_refs), spread, NT)
        l2v = _exact_dot(_natural_rows(l2_ref, tile_refs), spread, NT)
        m = jnp.maximum(jnp.maximum(l0v, l1v), l2v)
        tot = m + jnp.log(jnp.exp(l0v - m) + jnp.exp(l1v - m) + jnp.exp(l2v - m))
        o = jnp.exp(l0v - tot) * o0_ref[...]
        o = o + jnp.exp(l1v - tot) * _natural_rows(o1_ref, tile_refs)
        o = o + jnp.exp(l2v - tot) * _natural_rows(o2_ref, tile_refs)
        g = g_ref[...]
        oa_ref[...] = (o * (g * _sigmoid(g))).astype(BF16)
        oatt_ref[...] = o
        lse_ref[...] = tot

    spec = pl.BlockSpec((None, tm, ATT_OUT), lambda b, i: (b, i, 0))
    own = lambda t: pl.BlockSpec((None, t.shape[1], tm // t.shape[1], t.shape[3]), lambda b, i: (b, 0, i, 0))
    (o1, l1), (o2, l2) = dilated
    return _pcall(
        body, name="attn_combine", grid=(nb, seq // tm),
        in_specs=[spec, spec, own(o1), own(l1), own(o2), own(l2), spec], out_specs=[spec] * 3,
        out_shape=[jax.ShapeDtypeStruct((nb, seq, ATT_OUT), BF16), jax.ShapeDtypeStruct((nb, seq, ATT_OUT), F32),
                   jax.ShapeDtypeStruct((nb, seq, ATT_OUT), F32)],
        scratch_shapes=[pltpu.VMEM((tm, LANE), F32)] * ntile,
        compiler_params=_params("parallel", "parallel"),
    )(o0, l0, o1, l1, o2, l2, gatt)


def _combine_bwd(dya16, w_bra, gatt, o_att, lse, dilations):
    nb, seq, _ = gatt.shape
    tm = 512

    def body(dya_ref, w_ref, g_ref, o_ref, l_ref, do_ref, dg_ref, *rest):
        ntile = ATT_OUT // LANE
        outs, tile_refs = rest[:-ntile], rest[-ntile:]
        doa = lax.dot_general(dya_ref[...], w_ref[...], NT, preferred_element_type=F32)
        g = g_ref[...]
        sg = _sigmoid(g)
        do = doa * (g * sg)
        do_ref[...] = do
        stats = (_exact_dot(do * o_ref[...], _head_lanes(0, False))
                 + _exact_dot(l_ref[...], _head_lanes(STAT_LSE_LANE, True)))
        dg_ref[...] = (doa * o_ref[...] * (sg * (1.0 + g * (1.0 - sg)))).astype(BF16)
        for k in range(len(dilations)):
            _store_own_order(do, tile_refs, outs[2 * k])
            _store_own_order(stats, tile_refs, outs[2 * k + 1])

    spec = pl.BlockSpec((None, tm, ATT_OUT), lambda b, i: (b, i, 0))
    own = lambda d, width: pl.BlockSpec((None, d, tm // d, width), lambda b, i: (b, 0, i, 0))
    outs = _pcall(
        body, name="attn_combine_bwd", grid=(nb, seq // tm),
        in_specs=[_tok_spec(tm, D_MODEL), _whole(w_bra, True)] + [spec] * 3,
        out_specs=[spec, spec] + [own(d, w) for d in dilations for w in (ATT_OUT, LANE)],
        out_shape=[jax.ShapeDtypeStruct((nb, seq, ATT_OUT), F32), jax.ShapeDtypeStruct((nb, seq, ATT_OUT), BF16)]
        + [jax.ShapeDtypeStruct((nb, d, seq // d, w), t) for d in dilations for w, t in ((ATT_OUT, BF16), (LANE, F32))],
        scratch_shapes=[pltpu.VMEM((tm, LANE), F32)] * (ATT_OUT // LANE),
        compiler_params=_params("parallel", "parallel"),
    )(dya16, w_bra, gatt, o_att, lse)
    return outs[0], outs[1], outs[2:]


CONV_TM = 1024
CONV_TC = 1024


def _shift_down(cur, halo, k):
    rolled = pltpu.roll(cur, k, 0)
    hro = pltpu.roll(halo, k, 0)
    row = lax.broadcasted_iota(jnp.int32, hro.shape, 0)
    return jnp.concatenate([jnp.where(row < k, hro, rolled[:8]), rolled[8:]], axis=0)


def _shift_up(cur, halo, k):
    n = cur.shape[0]
    rolled = pltpu.roll(cur, n - k, 0)
    hro = pltpu.roll(halo, 8 - k, 0)
    row = lax.broadcasted_iota(jnp.int32, hro.shape, 0)
    return jnp.concatenate([rolled[:n - 8], jnp.where(row >= 8 - k, hro, rolled[n - 8:])], axis=0)


def _conv_pre(cur, halo, w_ref, b_ref):
    acc = cur * w_ref[3:4, :] + b_ref[...]
    for k in range(1, 4):
        acc = acc + _shift_down(cur, halo, k) * w_ref[3 - k:4 - k, :]
    return acc


def _conv_specs(seq):
    nblk = seq // CONV_TM
    cur = pl.BlockSpec((None, CONV_TM, CONV_TC), lambda cb, b, i: (b, i, cb))
    prev = pl.BlockSpec((None, 8, CONV_TC), lambda cb, b, i: (b, jnp.maximum(i * (CONV_TM // 8) - 1, 0), cb))
    nxt = pl.BlockSpec((None, 8, CONV_TC),
                       lambda cb, b, i: (b, jnp.minimum((i + 1) * (CONV_TM // 8), seq // 8 - 1), cb))
    wspec = pl.BlockSpec((4, CONV_TC), lambda cb, b, i: (0, cb))
    bspec = pl.BlockSpec((1, CONV_TC), lambda cb, b, i: (0, cb))
    return nblk, cur, prev, nxt, wspec, bspec


def _conv_fwd(xin, w4, bias, name):
    nb, seq, ch = xin.shape
    _, cur, prev, _, wspec, bspec = _conv_specs(seq)

    def body(x_ref, h_ref, w_ref, b_ref, o_ref):
        halo = jnp.where(pl.program_id(2) > 0, h_ref[...], 0.0)
        pre = _conv_pre(x_ref[...], halo, w_ref, b_ref)
        o_ref[...] = pre * _sigmoid(pre)

    return _pcall(
        body, name=name, grid=(ch // CONV_TC, nb, seq // CONV_TM),
        in_specs=[cur, prev, wspec, bspec], out_specs=cur,
        out_shape=jax.ShapeDtypeStruct(xin.shape, F32),
        compiler_params=_params("parallel", "parallel", "parallel"),
    )(xin, xin, w4, bias)


def _conv_bwd_pre(dact, xin, w4, bias, name):
    nb, seq, ch = xin.shape
    _, cur, prev, _, wspec, bspec = _conv_specs(seq)

    def body(da_ref, x_ref, h_ref, w_ref, b_ref, dp_ref, s_ref):
        b, i = pl.program_id(1), pl.program_id(2)

        @pl.when((b == 0) & (i == 0))
        def _():
            s_ref[...] = jnp.zeros_like(s_ref)

        halo = jnp.where(i > 0, h_ref[...], 0.0)
        x = x_ref[...]
        pre = _conv_pre(x, halo, w_ref, b_ref)
        sg = _sigmoid(pre)
        dpre = da_ref[...] * (sg * (1.0 + pre * (1.0 - sg)))
        dp_ref[...] = dpre
        s_ref[3:4, :] += jnp.sum(dpre * x, 0, keepdims=True)
        for k in range(1, 4):
            s_ref[3 - k:4 - k, :] += jnp.sum(dpre * _shift_down(x, halo, k), 0, keepdims=True)
        s_ref[4:5, :] += jnp.sum(dpre, 0, keepdims=True)

    return _pcall(
        body, name=name, grid=(ch // CONV_TC, nb, seq // CONV_TM),
        in_specs=[cur, cur, prev, wspec, bspec],
        out_specs=[cur, pl.BlockSpec((8, CONV_TC), lambda cb, b, i: (0, cb))],
        out_shape=[jax.ShapeDtypeStruct(xin.shape, F32), jax.ShapeDtypeStruct((8, ch), F32)],
        compiler_params=_params("parallel", "arbitrary", "arbitrary"),
    )(dact, xin, xin, w4, bias)


def _conv_bwd_x(dpre, w4, name):
    nb, seq, ch = dpre.shape
    nblk, cur, _, nxt, wspec, _ = _conv_specs(seq)

    def body(d_ref, n_ref, w_ref, o_ref):
        halo = jnp.where(pl.program_id(2) < nblk - 1, n_ref[...], 0.0)
        cur_v = d_ref[...]
        acc = cur_v * w_ref[3:4, :]
        for j in range(1, 4):
            acc = acc + _shift_up(cur_v, halo, j) * w_ref[3 - j:4 - j, :]
        o_ref[...] = acc.astype(BF16)

    return _pcall(
        body, name=name, grid=(ch // CONV_TC, nb, seq // CONV_TM),
        in_specs=[cur, nxt, wspec], out_specs=cur,
        out_shape=jax.ShapeDtypeStruct(dpre.shape, BF16),
        compiler_params=_params("parallel", "parallel", "parallel"),
    )(dpre, dpre, w4)


def _step_sizes(raw, tb_ref):
    shift = (LANE - GROUP_SSM_HEADS * pl.program_id(0)) % LANE
    v = pltpu.roll(raw + tb_ref[...], shift, 1)
    own = lax.broadcasted_iota(jnp.int32, v.shape, 1) < GROUP_SSM_HEADS
    sp = jnp.maximum(v, 0.0) + jnp.log1p(jnp.exp(-jnp.abs(v)))
    return jnp.where(own, sp, 0.0), jnp.where(own, _sigmoid(v), 0.0)


def _group_lanes(t):
    pads = [(0, 0)] * (t.ndim - 1) + [(0, LANE - GROUP_SSM_HEADS)]
    return jnp.stack([jnp.pad(t[..., GROUP_SSM_HEADS * g:GROUP_SSM_HEADS * (g + 1)], pads) for g in range(SSM_GROUPS)])


def _ungroup_lanes(t):
    return jnp.concatenate([t[g][..., :GROUP_SSM_HEADS] for g in range(SSM_GROUPS)], axis=-1)


def _decays(dt, al_ref):
    row = lax.broadcasted_iota(jnp.int32, (CHUNK, CHUNK), 0)
    col = lax.broadcasted_iota(jnp.int32, (CHUNK, CHUNK), 1)
    tril = (row >= col).astype(BF16)
    triu = (row <= col).astype(BF16)
    arow = -jnp.exp(al_ref[...])
    hi, mid, lo = _split3(dt * arow)
    down = lambda t: jnp.dot(tril, t, preferred_element_type=F32)
    across = lambda t: lax.dot_general(t, triu, TN, preferred_element_type=F32)
    acs = (down(hi) + down(mid)) + down(lo)
    acs_t = (across(hi) + across(mid)) + across(lo)
    return arow, acs, acs_t, row >= col, triu


STEP_CHUNKS = 8


def _ssd_specs(nb, seq):
    nc = seq // CHUNK
    hw = GROUP_SSM_HEADS * HEAD_DIM
    rows, steps = STEP_CHUNKS * CHUNK, nc // STEP_CHUNKS

    def mk(rev):
        cidx = (lambda c: steps - 1 - c) if rev else (lambda c: c)
        wide = pl.BlockSpec((None, rows, hw), lambda g, b, c: (b, cidx(c), g))
        xbc = pl.BlockSpec((None, rows, XBC_GROUP), lambda g, b, c: (b, cidx(c), g))
        lanes = pl.BlockSpec((None, None, rows, LANE), lambda g, b, c: (g, b, cidx(c), 0))
        prev = pl.BlockSpec((None, STEP_CHUNKS, None, D_STATE, hw), lambda g, b, c: (b, cidx(c), g, 0, 0))
        raw = pl.BlockSpec((None, rows, LANE), lambda g, b, c: (b, cidx(c), 0))
        return wide, xbc, lanes, prev, raw

    grow = pl.BlockSpec((None, 1, LANE), lambda g, b, c: (g, 0, 0))
    nwspec = pl.BlockSpec((1, hw), lambda g, b, c: (0, g))
    tbspec = pl.BlockSpec((1, LANE), lambda g, b, c: (0, 0))
    return nc, steps, hw, mk, grow, nwspec, tbspec


def _head_expand():
    hw = GROUP_SSM_HEADS * HEAD_DIM
    r = lax.broadcasted_iota(jnp.int32, (LANE, hw), 0)
    c = lax.broadcasted_iota(jnp.int32, (LANE, hw), 1)
    return ((c // HEAD_DIM) == r).astype(BF16)


def _split3(v):
    hi = v.astype(BF16)
    rest = v - hi.astype(F32)
    mid = rest.astype(BF16)
    return hi, mid, (rest - mid.astype(F32)).astype(BF16)


def _to_channels(v, e):
    hi, mid, lo = _split3(v)
    dot = lambda t: jnp.dot(t, e, preferred_element_type=F32)
    return (dot(hi) + dot(mid)) + dot(lo)


def _to_heads(w, e):
    hi, mid, lo = _split3(w)
    dot = lambda t: lax.dot_general(t, e, (((1,), (1,)), ((), ())), preferred_element_type=F32)
    return (dot(hi) + dot(mid)) + dot(lo)


def _row8(v):
    return jnp.broadcast_to(v, (8, v.shape[1]))


def _ssd_chunk_setup(dt, al_ref, ds_ref):
    arow, acs, acs_t, causal, triu = _decays(dt, al_ref)
    e = _head_expand()
    dtx = _to_channels(dt, e)
    acsx = _to_channels(acs, e)
    lastx = acsx[CHUNK - 1:CHUNK, :]
    dskx = _to_channels(_row8(ds_ref[...]), e)[0:1, :]
    return arow, acs, acs_t, causal, triu, e, dtx, acsx, lastx, dskx


def _ssd_fwd(xbc, dt_raw, dt_bias_row, z, alog_g, dskip_g, normw):
    nb, seq, _ = xbc.shape
    nc, steps, hw, mk, grow, nwspec, tbspec = _ssd_specs(nb, seq)
    wide, xbc_spec, lanes, prev, raw = mk(False)
    tn = (((0,), (0,)), ((), ()))

    def body(xbc_ref, dt_ref, tb_ref, z_ref, al_ref, ds_ref, nw_ref, ys_ref, y_ref, sp_ref, st_ref):
        @pl.when(pl.program_id(2) == 0)
        def _():
            st_ref[...] = jnp.zeros_like(st_ref)

        for ci in range(STEP_CHUNKS):
            chunk(ci, xbc_ref, dt_ref, tb_ref, z_ref, al_ref, ds_ref, nw_ref, ys_ref, y_ref, sp_ref, st_ref)

    def chunk(ci, xbc_ref, dt_ref, tb_ref, z_ref, al_ref, ds_ref, nw_ref, ys_ref, y_ref, sp_ref, st_ref):
        rows = slice(ci * CHUNK, (ci + 1) * CHUNK)
        dt, _ = _step_sizes(dt_ref[rows, :], tb_ref)
        _, acs, acs_t, causal, _, _, dtx, acsx, lastx, dskx = _ssd_chunk_setup(dt, al_ref, ds_ref)
        bmat = xbc_ref[rows, GROUP_CH:GROUP_CH + D_STATE].astype(BF16)
        cmat = xbc_ref[rows, GROUP_CH + D_STATE:].astype(BF16)
        cb = lax.dot_general(cmat, bmat, (((1,), (1,)), ((), ())), preferred_element_type=F32)
        x = xbc_ref[rows, :GROUP_CH]
        xdt = x * dtx
        xdt16 = xdt.astype(BF16)
        first_head = lax.broadcasted_iota(jnp.int32, (CHUNK, LANE), 1) < HEAD_DIM
        pairs = []
        for hp in range(GROUP_SSM_HEADS // 2):
            xp = xdt16[:, hp * LANE:(hp + 1) * LANE]
            two = []
            for j in (2 * hp, 2 * hp + 1):
                lmat = jnp.exp(jnp.where(causal, acs[:, j:j + 1] - acs_t[j:j + 1, :], -jnp.inf))
                two.append(jnp.dot((cb * lmat).astype(BF16), xp, preferred_element_type=F32))
            pairs.append(jnp.where(first_head, two[0], two[1]))
        yd = jnp.concatenate(pairs, axis=1)
        s_prev = st_ref[...]
        s16 = s_prev.astype(BF16)
        sp_ref[ci] = s16
        yo = jnp.dot(cmat, s16, preferred_element_type=F32) * jnp.exp(acsx)
        sts = lax.dot_general(bmat, (xdt * jnp.exp(lastx - acsx)).astype(BF16), tn, preferred_element_type=F32)
        st_ref[...] = s_prev * jnp.exp(lastx) + sts
        y = yd + yo + dskx * x
        zz = z_ref[rows, :]
        u = y * (zz * _sigmoid(zz))
        rn = lax.rsqrt(jnp.mean(u * u, -1, keepdims=True) + RMS_EPS)
        ys_ref[rows, :] = (u * rn * nw_ref[...]).astype(BF16)
        y_ref[rows, :] = y

    return _pcall(
        body, name="ssd_fwd", grid=(SSM_GROUPS, nb, steps),
        in_specs=[xbc_spec, raw, tbspec, wide, grow, grow, nwspec],
        out_specs=[wide, wide, prev],
        out_shape=[jax.ShapeDtypeStruct((nb, seq, D_INNER), BF16), jax.ShapeDtypeStruct((nb, seq, D_INNER), F32),
                   jax.ShapeDtypeStruct((nb, nc, SSM_GROUPS, D_STATE, hw), BF16)],
        scratch_shapes=[pltpu.VMEM((D_STATE, hw), F32)],
        compiler_params=_params("parallel", "parallel", "arbitrary"),
    )(xbc, dt_raw, dt_bias_row, z, alog_g, dskip_g, normw)


def _ssd_bwd(xbc, dt_raw, dt_bias_row, z, y, dys, sprev, alog_g, dskip_g, normw):
    nb, seq, _ = xbc.shape
    nc, steps, hw, mk, grow, nwspec, tbspec = _ssd_specs(nb, seq)
    wide, xbc_spec, lanes, prev, raw = mk(True)
    nt = (((1,), (1,)), ((), ()))
    tn = (((0,), (0,)), ((), ()))

    def body(xbc_ref, dt_ref, tb_ref, z_ref, y_ref, dys_ref, sp_ref, al_ref, ds_ref, nw_ref,
             dxbc_ref, ddt_ref, dz_ref, small_ref, dnw_ref, g_ref):
        b, c = pl.program_id(1), pl.program_id(2)

        @pl.when((b == 0) & (c == 0))
        def _():
            small_ref[...] = jnp.zeros_like(small_ref)
            dnw_ref[...] = jnp.zeros_like(dnw_ref)

        @pl.when(c == 0)
        def _():
            g_ref[...] = jnp.zeros_like(g_ref)

        for ci in reversed(range(STEP_CHUNKS)):
            chunk(ci, xbc_ref, dt_ref, tb_ref, z_ref, y_ref, dys_ref, sp_ref, al_ref, ds_ref, nw_ref,
                  dxbc_ref, ddt_ref, dz_ref, small_ref, dnw_ref, g_ref)

    def chunk(ci, xbc_ref, dt_ref, tb_ref, z_ref, y_ref, dys_ref, sp_ref, al_ref, ds_ref, nw_ref,
              dxbc_ref, ddt_ref, dz_ref, small_ref, dnw_ref, g_ref):
        rows = slice(ci * CHUNK, (ci + 1) * CHUNK)
        yv, zz, dys_v, nw = y_ref[rows, :], z_ref[rows, :], dys_ref[rows, :], nw_ref[...]
        sz = _sigmoid(zz)
        silu = zz * sz
        u = yv * silu
        rn = lax.rsqrt(jnp.mean(u * u, -1, keepdims=True) + RMS_EPS)
        gn = dys_v * nw
        du = rn * gn - u * (rn * rn * rn) * jnp.mean(u * gn, -1, keepdims=True)
        dnw_ref[...] += jnp.sum(dys_v * u * rn, 0, keepdims=True)
        dy = du * silu
        dz_ref[rows, :] = du * yv * (sz * (1.0 + zz * (1.0 - sz)))

        dt, sg = _step_sizes(dt_ref[rows, :], tb_ref)
        arow, acs, acs_t, causal, triu, e, dtx, acsx, lastx, dskx = _ssd_chunk_setup(dt, al_ref, ds_ref)
        dfsx = jnp.exp(acsx)
        dtex = jnp.exp(lastx - acsx)
        bmat = xbc_ref[rows, GROUP_CH:GROUP_CH + D_STATE].astype(BF16)
        cmat = xbc_ref[rows, GROUP_CH + D_STATE:].astype(BF16)
        cb = lax.dot_general(cmat, bmat, nt, preferred_element_type=F32)
        x = xbc_ref[rows, :GROUP_CH]
        xdt = x * dtx
        xdt16 = xdt.astype(BF16)
        xdte = xdt * dtex
        dy16 = dy.astype(BF16)
        dyd = dy * dfsx
        dyd16 = dyd.astype(BF16)
        s16 = sp_ref[ci]
        g = g_ref[...]
        g16 = g.astype(BF16)
        cs = jnp.dot(cmat, s16, preferred_element_type=F32)
        dc_off = lax.dot_general(dyd16, s16, nt, preferred_element_type=F32)
        g_here = lax.dot_general(cmat, dyd16, tn, preferred_element_type=F32)
        bg = jnp.dot(bmat, g16, preferred_element_type=F32)
        db_st = lax.dot_general(xdte.astype(BF16), g16, nt, preferred_element_type=F32)
        ddte_w = bg * xdte
        dcd = _to_heads(_row8(jnp.sum(g * s16.astype(F32), 0, keepdims=True)), e)[0:1, :]
        lane = lax.broadcasted_iota(jnp.int32, (CHUNK, LANE), 1)
        first_head = lane < HEAD_DIM
        sub = lax.broadcasted_iota(jnp.int32, (CHUNK, LANE), 0)
        dacs = jnp.zeros((CHUNK, LANE), F32)
        colsums = jnp.zeros((CHUNK, LANE), F32)
        dcb = jnp.zeros((CHUNK, CHUNK), F32)
        pairs = []
        for hp in range(GROUP_SSM_HEADS // 2):
            xp = xdt16[:, hp * LANE:(hp + 1) * LANE]
            dyp = dy16[:, hp * LANE:(hp + 1) * LANE]
            two = []
            for idx, j in enumerate((2 * hp, 2 * hp + 1)):
                lmat = jnp.exp(jnp.where(causal, acs[:, j:j + 1] - acs_t[j:j + 1, :], -jnp.inf))
                mf = cb * lmat
                dy_h = jnp.where(first_head if idx == 0 else jnp.logical_not(first_head), dyp, jnp.zeros_like(dyp))
                dm = lax.dot_general(dy_h, xp, nt, preferred_element_type=F32)
                two.append(lax.dot_general(mf.astype(BF16), dyp, tn, preferred_element_type=F32))
                wmat = dm * mf
                dcb = dcb + dm * lmat
                dacs = jnp.where(lane == j, jnp.sum(wmat, -1, keepdims=True), dacs)
                colsums = jnp.where(sub == j, jnp.sum(wmat, 0, keepdims=True), colsums)
            pairs.append(jnp.where(first_head, two[0], two[1]))
        dxdt = bg * dtex + jnp.concatenate(pairs, axis=1)
        dacs = dacs - colsums.T + _to_heads(dyd * cs - ddte_w, e)
        cd_row = jnp.exp(acs[CHUNK - 1:CHUNK, :])
        tail = _to_heads(_row8(jnp.sum(ddte_w, 0, keepdims=True)), e)[0:1, :] + dcd * cd_row
        dacs = dacs + jnp.where(sub == CHUNK - 1, tail, 0.0)
        d_hi, d_mid, d_lo = _split3(dacs)
        up = lambda t: jnp.dot(triu, t, preferred_element_type=F32)
        da = (up(d_hi) + up(d_mid)) + up(d_lo)
        ddt_raw = (da * arow + _to_heads(dxdt * x, e)) * sg
        ddt_ref[rows, :] = ddt_raw
        small_ref[0:1, :] += jnp.sum(da * dt, 0, keepdims=True) * arow
        small_ref[1:2, :] += _to_heads(_row8(jnp.sum(dy * x, 0, keepdims=True)), e)[0:1, :]
        small_ref[2:3, :] += jnp.sum(ddt_raw, 0, keepdims=True)
        dcb16 = dcb.astype(BF16)
        dxbc_ref[rows, GROUP_CH + D_STATE:] = dc_off + jnp.dot(dcb16, bmat, preferred_element_type=F32)
        dxbc_ref[rows, GROUP_CH:GROUP_CH + D_STATE] = db_st + lax.dot_general(dcb16, cmat, tn,
                                                                               preferred_element_type=F32)
        dxbc_ref[rows, :GROUP_CH] = dxdt * dtx + dskx * dy
        g_ref[...] = g * jnp.exp(lastx) + g_here

    return _pcall(
        body, name="ssd_bwd", grid=(SSM_GROUPS, nb, steps),
        in_specs=[xbc_spec, raw, tbspec, wide, wide, wide, prev, grow, grow, nwspec],
        out_specs=[xbc_spec, lanes, wide,
                   pl.BlockSpec((None, 8, LANE), lambda g, b, c: (g, 0, 0)), nwspec],
        out_shape=[jax.ShapeDtypeStruct((nb, seq, CONV_DIM), F32),
                   jax.ShapeDtypeStruct((SSM_GROUPS, nb, seq, LANE), F32),
                   jax.ShapeDtypeStruct((nb, seq, D_INNER), F32),
                   jax.ShapeDtypeStruct((SSM_GROUPS, 8, LANE), F32),
                   jax.ShapeDtypeStruct((1, D_INNER), F32)],
        scratch_shapes=[pltpu.VMEM((D_STATE, hw), F32)],
        compiler_params=_params("parallel", "arbitrary", "arbitrary"),
    )(xbc, dt_raw, dt_bias_row, z, y, dys, sprev, alog_g, dskip_g, normw)


EW_TM = 256


def _merge_fwd(oa16, y_ssm16, w_bra, w_brb, gm, bgate):
    nb, seq, _ = oa16.shape

    def body(oa_ref, ys_ref, wa_ref, wb_ref, ga_ref, gb_ref, bg_ref, a_ref, b_ref, o_ref):
        y_a = jnp.dot(oa_ref[...], wa_ref[...], preferred_element_type=F32)
        y_b = jnp.dot(ys_ref[...], wb_ref[...], preferred_element_type=F32)
        a_ref[...] = y_a
        b_ref[...] = y_b
        sa = _sigmoid(ga_ref[...] + bg_ref[0:1, :])
        sb = _sigmoid(gb_ref[...] + bg_ref[1:2, :])
        o_ref[...] = (sa * y_a + sb * y_b).astype(BF16)

    spec = pl.BlockSpec((None, EW_TM, D_MODEL), lambda b, i: (b, i, 0))
    spec1 = pl.BlockSpec((None, EW_TM, D_MODEL), lambda b, i: (b, i, 1))
    return _pcall(
        body, name="merge_fwd", grid=(nb, seq // EW_TM),
        in_specs=[_tok_spec(EW_TM, ATT_OUT), _tok_spec(EW_TM, D_INNER), _whole(w_bra, True), _whole(w_brb, True),
                  spec, spec1, pl.BlockSpec((8, D_MODEL), lambda b, i: (0, 0))],
        out_specs=[spec] * 3,
        out_shape=[jax.ShapeDtypeStruct((nb, seq, D_MODEL), F32)] * 2 + [jax.ShapeDtypeStruct((nb, seq, D_MODEL), BF16)],
        compiler_params=_params("parallel", "parallel"),
    )(oa16, y_ssm16, w_bra, w_brb, gm, gm, bgate)


def _merge_bwd(dpre16, w_out16, y_a, y_b, gm, bgate):
    nb, seq, _ = y_a.shape

    def body(dp_ref, w_ref, a_ref, b_ref, ga_ref, gb_ref, bg_ref, dya_ref, dyb_ref, dg_ref, s_ref):
        @pl.when((pl.program_id(0) == 0) & (pl.program_id(1) == 0))
        def _():
            s_ref[...] = jnp.zeros_like(s_ref)

        dm = lax.dot_general(dp_ref[...], w_ref[...], NT, preferred_element_type=F32)
        sa = _sigmoid(ga_ref[...] + bg_ref[0:1, :])
        sb = _sigmoid(gb_ref[...] + bg_ref[1:2, :])
        dya_ref[...] = (dm * sa).astype(BF16)
        dyb_ref[...] = (dm * sb).astype(BF16)
        dga = dm * a_ref[...] * (sa * (1.0 - sa))
        dgb = dm * b_ref[...] * (sb * (1.0 - sb))
        dg_ref[:, :D_MODEL] = dga.astype(BF16)
        dg_ref[:, D_MODEL:] = dgb.astype(BF16)
        s_ref[0:1, :] += jnp.sum(dga, 0, keepdims=True)
        s_ref[1:2, :] += jnp.sum(dgb, 0, keepdims=True)

    spec = pl.BlockSpec((None, EW_TM, D_MODEL), lambda b, i: (b, i, 0))
    spec1 = pl.BlockSpec((None, EW_TM, D_MODEL), lambda b, i: (b, i, 1))
    small = pl.BlockSpec((8, D_MODEL), lambda b, i: (0, 0))
    return _pcall(
        body, name="merge_bwd", grid=(nb, seq // EW_TM),
        in_specs=[spec, _whole(w_out16, True), spec, spec, spec, spec1, small],
        out_specs=[spec, spec, pl.BlockSpec((None, EW_TM, 2 * D_MODEL), lambda b, i: (b, i, 0)), small],
        out_shape=[jax.ShapeDtypeStruct((nb, seq, D_MODEL), BF16), jax.ShapeDtypeStruct((nb, seq, D_MODEL), BF16),
                   jax.ShapeDtypeStruct((nb, seq, 2 * D_MODEL), BF16), jax.ShapeDtypeStruct((8, D_MODEL), F32)],
        compiler_params=_params("arbitrary", "arbitrary"),
    )(dpre16, w_out16, y_a, y_b, gm, gm, bgate)


def _ln_loss(x, merged16, w_out16, gp, p16, w_ple16, target, bgate, ln_g, ln_b):
    nb, seq, _ = x.shape

    def body(x_ref, m_ref, wo_ref, gp_ref, p_ref, wp_ref, t_ref, bg_ref, g_ref, b_ref,
             dx_ref, dp_ref, dpw_ref, dgp_ref, s_ref):
        @pl.when((pl.program_id(0) == 0) & (pl.program_id(1) == 0))
        def _():
            s_ref[...] = jnp.zeros_like(s_ref)

        sp = _sigmoid(gp_ref[...] + bg_ref[2:3, :])
        pw = jnp.dot(p_ref[...], wp_ref[...], preferred_element_type=F32)
        mix = jnp.dot(m_ref[...], wo_ref[...], preferred_element_type=F32)
        pre = ALPHA * x_ref[...] + mix + sp * pw
        mu = jnp.mean(pre, -1, keepdims=True)
        cen = pre - mu
        rstd = lax.rsqrt(jnp.mean(cen * cen, -1, keepdims=True) + LN_EPS)
        xhat = cen * rstd
        err = xhat * g_ref[...] + b_ref[...] - t_ref[...]
        dy = err * (1.0 / D_MODEL)
        dxh = dy * g_ref[...]
        dpre = rstd * (dxh - jnp.mean(dxh, -1, keepdims=True) - xhat * jnp.mean(dxh * xhat, -1, keepdims=True))
        dx_ref[...] = ALPHA * dpre
        dp_ref[...] = dpre.astype(BF16)
        dpw_ref[...] = (dpre * sp).astype(BF16)
        dgp = dpre * pw * (sp * (1.0 - sp))
        dgp_ref[...] = dgp.astype(BF16)
        s_ref[0:1, :] += jnp.sum(dy * xhat, 0, keepdims=True)
        s_ref[1:2, :] += jnp.sum(dy, 0, keepdims=True)
        s_ref[2:3, :] += jnp.sum(dgp, 0, keepdims=True)
        s_ref[3:4, :] += jnp.sum(err * err, 0, keepdims=True)

    spec = pl.BlockSpec((None, EW_TM, D_MODEL), lambda b, i: (b, i, 0))
    small = pl.BlockSpec((8, D_MODEL), lambda b, i: (0, 0))
    row = pl.BlockSpec((1, D_MODEL), lambda b, i: (0, 0))
    return _pcall(
        body, name="ln_loss", grid=(nb, seq // EW_TM),
        in_specs=[spec, spec, _whole(w_out16, True), spec, pl.BlockSpec((None, EW_TM, PLE_DIM), lambda b, i: (b, i, 0)),
                  _whole(w_ple16, True), spec, small, row, row],
        out_specs=[spec] * 4 + [small],
        out_shape=[jax.ShapeDtypeStruct((nb, seq, D_MODEL), F32)] + [jax.ShapeDtypeStruct((nb, seq, D_MODEL), BF16)] * 3
        + [jax.ShapeDtypeStruct((8, D_MODEL), F32)],
        compiler_params=_params("arbitrary", "arbitrary"),
    )(x, merged16, w_out16, gp, p16, w_ple16, target, bgate, ln_g, ln_b)


def _adamw(w, g, m, v, name):
    rows, cols = w.shape
    tr = _row_tile(rows, cols, 8, 5 << 19)
    c1 = 1.0 - ADAM_B1 ** ADAM_STEP
    c2 = 1.0 - ADAM_B2 ** ADAM_STEP

    def body(w_ref, g_ref, m_ref, v_ref, d_ref, nm_ref, nv_ref):
        gv = g_ref[...]
        nm = ADAM_B1 * m_ref[...] + (1.0 - ADAM_B1) * gv
        nv = ADAM_B2 * v_ref[...] + (1.0 - ADAM_B2) * (gv * gv)
        d_ref[...] = -ADAM_LR * ((nm / c1) / (jnp.sqrt(nv / c2) + ADAM_EPS) + ADAM_WD * w_ref[...])
        nm_ref[...] = nm
        nv_ref[...] = nv

    spec = pl.BlockSpec((tr, cols), lambda i: (i, 0))
    return _pcall(
        body, name=name, grid=(rows // tr,), in_specs=[spec] * 4, out_specs=[spec] * 3,
        out_shape=[jax.ShapeDtypeStruct(w.shape, F32)] * 3, compiler_params=_params("parallel"),
    )(w, g, m, v)


def _sum_rows(parts, out_dtype, name):
    rows, cols = parts[0].shape
    tr = rows
    for cand in range(16, rows, 16):
        if rows % cand == 0 and cand * cols * 4 <= (1 << 20):
            tr = cand
    n = len(parts)

    def body(*refs):
        acc = refs[0][...].astype(F32)
        for r in refs[1:n]:
            acc = acc + r[...].astype(F32)
        refs[n][...] = acc.astype(out_dtype)

    spec = pl.BlockSpec((tr, cols), lambda i: (i, 0))
    return _pcall(
        body, name=name, grid=(rows // tr,), in_specs=[spec] * n, out_specs=spec,
        out_shape=jax.ShapeDtypeStruct((rows, cols), out_dtype), compiler_params=_params("parallel"),
    )(*parts)


def _place():
    return lax.axis_index("x"), lax.axis_index("y"), lax.axis_index("c")


def _other_chips(x, y):
    return [(1 - x, y), (x, 1 - y), (1 - x, 1 - y)]


def _remote(src, dst, send_sem, recv_sem, to):
    return pltpu.make_async_remote_copy(src_ref=src, dst_ref=dst, send_sem=send_sem, recv_sem=recv_sem,
                                        device_id=to, device_id_type=MESH)


ANY = pl.BlockSpec(memory_space=pl.ANY)
D2D_CHUNK_BYTES = 512 * 1024
ICI_CHUNK_BYTES = 2 * 1024 * 1024


def _row_chunks(rows, row_bytes, chunk_bytes=D2D_CHUNK_BYTES):
    per = max(16, chunk_bytes // row_bytes // 16 * 16)
    return [(s, min(per, rows - s)) for s in range(0, rows, per)]


def _row_tile(rows, cols, align, limit=1 << 21):
    best = None
    for cand in range(align, rows + 1, align):
        if rows % cand == 0 and cand * cols * 4 <= limit:
            best = cand
    return best or rows


def _allgather_pieces(pieces):
    n = len(pieces)
    halves = [_row_chunks(p.shape[0] // 2, p.shape[1] * p.dtype.itemsize, ICI_CHUNK_BYTES) for p in pieces]
    entries = [(a, q, s, m, j) for a in range(n) for q, (s, m) in enumerate(halves[a]) for j in range(3)]
    slot = {(a, q, j): k for k, (a, q, _, _, j) in enumerate(entries)}
    n_ici = len(entries)

    def body(*refs):
        ins, outs = refs[:n], refs[n:2 * n]
        send_sems, recv_sems = refs[2 * n:]
        x, y, c = _place()
        me = 2 * x + y
        sibling = (x, y, 1 - c)
        chips = _other_chips(x, y)

        def landed(a, s, m, j, core):
            half = ins[a].shape[0] // 2
            return outs[a].at[2 * chips[j][0] + chips[j][1], pl.ds(core * half + s, m)]

        sent = []
        for k, (a, q, s, m, j) in enumerate(entries):
            if j < 2:
                half = ins[a].shape[0] // 2
                cp = _remote(ins[a].at[pl.ds(c * half + s, m)], outs[a].at[me, pl.ds(c * half + s, m)],
                             send_sems.at[k], recv_sems.at[k], (*chips[j], c))
                cp.start()
                sent.append(cp)

        def pass_to_sibling(k, blk):
            fw = _remote(blk, blk, send_sems.at[n_ici + k], recv_sems.at[n_ici + k], sibling)
            fw.start()
            sent.append(fw)

        for k, (a, q, s, m, j) in enumerate(entries):
            if j < 2:
                blk = landed(a, s, m, j, c)
                _remote(blk, blk, send_sems.at[k], recv_sems.at[k], (*chips[j], c)).wait_recv()
                first = q < (len(halves[a]) + 1) // 2
                if (j == 0) == first:
                    on = slot[(a, q, 2)]
                    rl = _remote(blk, blk, send_sems.at[on], recv_sems.at[on], (*chips[1 - j], c))
                    rl.start()
                    sent.append(rl)
                pass_to_sibling(k, blk)
        for k, (a, q, s, m, j) in enumerate(entries):
            if j == 2:
                blk = landed(a, s, m, j, c)
                _remote(blk, blk, send_sems.at[k], recv_sems.at[k], (*chips[j], c)).wait_recv()
                pass_to_sibling(k, blk)
        for k, (a, q, s, m, j) in enumerate(entries):
            blk = landed(a, s, m, j, 1 - c)
            _remote(blk, blk, send_sems.at[n_ici + k], recv_sems.at[n_ici + k], sibling).wait_recv()
        for cp in sent:
            cp.wait_send()

    gathered = _pcall(
        body, name="allgather_weights", in_specs=[ANY] * n, out_specs=[ANY] * n,
        out_shape=[jax.ShapeDtypeStruct((4,) + p.shape, p.dtype) for p in pieces],
        scratch_shapes=[pltpu.SemaphoreType.DMA((2 * n_ici,)), pltpu.SemaphoreType.DMA((2 * n_ici,))],
        compiler_params=pltpu.CompilerParams(has_side_effects=True),
    )(*pieces)
    x, y, _ = _place()
    return [lax.dynamic_update_slice(g, p[None], (2 * x + y, 0, 0)) for g, p in zip(gathered, pieces)]


def _sibling_exchange(grads):
    n = len(grads)
    chunks = [_row_chunks(g.shape[1] // 2, g.shape[2] * g.dtype.itemsize) for g in grads]
    n_sem = 4 * sum(len(ch) for ch in chunks)

    def body(*refs):
        ins, gots = refs[:n], refs[n:2 * n]
        send_sems, recv_sems = refs[2 * n:]
        x, y, c = _place()
        sibling = (x, y, 1 - c)
        work = []
        for a in range(n):
            half = ins[a].shape[1] // 2
            for piece in range(4):
                for s, m in chunks[a]:
                    k = len(work)
                    cp = _remote(ins[a].at[piece, pl.ds((1 - c) * half + s, m)], gots[a].at[piece, pl.ds(s, m)],
                                 send_sems.at[k], recv_sems.at[k], sibling)
                    cp.start()
                    work.append(cp)
        for cp in work:
            cp.wait()

    return _pcall(
        body, name="grad_sibling_exchange", in_specs=[ANY] * n, out_specs=[ANY] * n,
        out_shape=[jax.ShapeDtypeStruct((4, g.shape[1] // 2, g.shape[2]), g.dtype) for g in grads],
        scratch_shapes=[pltpu.SemaphoreType.DMA((n_sem,)), pltpu.SemaphoreType.DMA((n_sem,))],
        compiler_params=pltpu.CompilerParams(has_side_effects=True),
    )(*grads)


def _sibling_gather(fulls):
    n = len(fulls)
    chunks = [_row_chunks(f.shape[0] // 2, f.shape[1] * f.dtype.itemsize) for f in fulls]
    n_sem = sum(len(ch) for ch in chunks)

    def body(*refs):
        outs = refs[n:2 * n]
        send_sems, recv_sems = refs[2 * n:]
        x, y, c = _place()
        sibling = (x, y, 1 - c)
        work = []
        for a in range(n):
            h = outs[a].shape[0] // 2
            for s, m in chunks[a]:
                k = len(work)
                mine = outs[a].at[pl.ds(c * h + s, m)]
                cp = _remote(mine, mine, send_sems.at[k], recv_sems.at[k], sibling)
                cp.start()
                work.append((a, s, m, cp))
        for k, (a, s, m, cp) in enumerate(work):
            h = outs[a].shape[0] // 2
            cp.wait_send()
            theirs = outs[a].at[pl.ds((1 - c) * h + s, m)]
            _remote(theirs, theirs, send_sems.at[k], recv_sems.at[k], sibling).wait_recv()

    return _pcall(
        body, name="grad_sibling_gather", in_specs=[ANY] * n, out_specs=[ANY] * n,
        out_shape=[jax.ShapeDtypeStruct(f.shape, f.dtype) for f in fulls],
        input_output_aliases={a: a for a in range(n)},
        scratch_shapes=[pltpu.SemaphoreType.DMA((n_sem,)), pltpu.SemaphoreType.DMA((n_sem,))],
        compiler_params=pltpu.CompilerParams(has_side_effects=True),
    )(*fulls)


def _pair_sum(grad, got, place, name):
    _, rows, cols = grad.shape
    half = rows // 2
    tr = _row_tile(half, cols, 16)

    def body(p_ref, a_ref, b_ref, o_ref):
        o_ref[...] = (a_ref[...].astype(F32) + b_ref[...].astype(F32)).astype(BF16)

    return _pcall(
        body, name=name,
        grid_spec=pltpu.PrefetchScalarGridSpec(
            num_scalar_prefetch=1, grid=(4, half // tr),
            in_specs=[pl.BlockSpec((None, tr, cols), lambda k, i, p: (k, p[1] * (half // tr) + i, 0)),
                      pl.BlockSpec((None, tr, cols), lambda k, i, p: (k, i, 0))],
            out_specs=pl.BlockSpec((None, tr, cols), lambda k, i, p: (k, i, 0))),
        out_shape=jax.ShapeDtypeStruct((4, half, cols), BF16),
        compiler_params=_params("parallel", "parallel"),
    )(place, grad, got)


def _chip_sum(sums, got, place, name):
    _, h, cols = sums.shape
    tr = _row_tile(h, cols, 16)

    def body(p_ref, own_ref, g0, g1, g2, o_ref):
        o_ref[...] = ((own_ref[...].astype(F32) + g0[...].astype(F32)) + g1[...].astype(F32)) + g2[...].astype(F32)

    gspec = lambda j: pl.BlockSpec((None, tr, cols), lambda i, p: (j, i, 0))
    return _pcall(
        body, name=name,
        grid_spec=pltpu.PrefetchScalarGridSpec(
            num_scalar_prefetch=1, grid=(h // tr,),
            in_specs=[pl.BlockSpec((None, tr, cols), lambda i, p: (p[0], i, 0)), gspec(0), gspec(1), gspec(2)],
            out_specs=pl.BlockSpec((tr, cols), lambda i, p: (p[1] * (h // tr) + i, 0))),
        out_shape=jax.ShapeDtypeStruct((2 * h, cols), F32),
        compiler_params=_params("parallel"),
    )(place, sums, got, got, got)


def _allgather8(buf, name):
    rows = buf.shape[0]

    def body(in_ref, out_ref, send_sems, recv_sems):
        x, y, c = _place()
        me = 4 * x + 2 * y + c
        out_ref[me] = in_ref[...]
        work = []
        for rel in range(1, 8):
            fx, fy, fc = (rel >> 2) & 1, (rel >> 1) & 1, rel & 1
            to = (x ^ fx, y ^ fy, c ^ fc)
            cp = _remote(in_ref, out_ref.at[me], send_sems.at[rel - 1], recv_sems.at[rel - 1], to)
            cp.start()
            work.append((cp, 4 * to[0] + 2 * to[1] + to[2]))
        for rel, (cp, frm) in enumerate(work):
            cp.wait_send()
            blk = out_ref.at[frm]
            _remote(blk, blk, send_sems.at[rel], recv_sems.at[rel], (x, y, c)).wait_recv()

    return _pcall(
        body, name=name, in_specs=[pl.BlockSpec(memory_space=pltpu.VMEM)],
        out_specs=pl.BlockSpec(memory_space=pltpu.VMEM),
        out_shape=jax.ShapeDtypeStruct((8, rows, LANE), F32),
        scratch_shapes=[pltpu.SemaphoreType.DMA((7,)), pltpu.SemaphoreType.DMA((7,))],
        compiler_params=pltpu.CompilerParams(has_side_effects=True),
    )(buf)


def _pack_rows(arrs):
    parts = []
    for a in arrs:
        f = a.reshape(-1).astype(F32)
        parts.append(jnp.pad(f, (0, (-f.shape[0]) % LANE)))
    flat = jnp.concatenate(parts)
    rows = -(-flat.shape[0] // LANE)
    rows8 = -(-rows // 8) * 8
    return jnp.pad(flat, (0, rows8 * LANE - flat.shape[0])).reshape(rows8, LANE)


def _unpack_rows(buf, shapes):
    flat = buf.reshape(-1)
    outs, off = [], 0
    for s in shapes:
        n = int(np.prod(s))
        outs.append(flat[off:off + n].reshape(s))
        off += -(-n // LANE) * LANE
    return outs


def _local_grads(x, p, target, wseg, w_br16, w_out16, w_ple16, b_gate, conv_w, conv_b, dt_bias, a_log, d_skip,
                 ssm_norm_w, ln_g, ln_b, rel_bias, finish_dx):
    nb, seq, _ = x.shape
    bmaps = jnp.asarray(_bucket_maps())
    bias = _bias_tables(rel_bias, bmaps)
    bgate8 = jnp.pad(b_gate, ((0, 5), (0, 0)))
    dils = [d for _, d in PATTERNS]

    x16p = _token_orders(x, dils[1:])
    x16 = x16p[0]
    p16 = p.astype(BF16)
    qkv = [_proj(x16p[g], [wseg["qkv%d" % g]], BF16, "proj_qkv%d" % g, True, 2 * MM_TM)[0].reshape(
        nb, dils[g], seq // dils[g], -1) for g in range(3)]
    nat = {}
    for gi, (group, tm) in enumerate(NAT_GROUPS):
        outs = _proj(x16, [wseg[s] for s in group], F32, "proj_nat%d" % gi, True, tm)
        nat.update(zip(group, outs))
    att = [_attn_fwd(qkv[g], bias, g, dils[g], "attn_fwd%d" % g) for g in range(3)]
    oa, o_att, lse = _combine_fwd(att[0][0], att[0][1], att[1:], nat["gatt"])

    conv_wg, conv_bg = _xbc_group_order(conv_w), _xbc_group_order(conv_b)
    act = _conv_fwd(nat["xbc"], conv_wg, conv_bg, "conv_fwd")
    dt_bias_row = jnp.pad(dt_bias, ((0, 0), (0, LANE - SSM_HEADS)))
    alog_g, dskip_g = _group_lanes(a_log), _group_lanes(d_skip)
    y_ssm, y_all, sprev = _ssd_fwd(act, nat["dt"], dt_bias_row, nat["z"], alog_g, dskip_g, ssm_norm_w)

    w_bra, w_brb = w_br16[:ATT_OUT], w_br16[ATT_OUT:]
    y_a, y_b, merged = _merge_fwd(oa, y_ssm, w_bra, w_brb, nat["gm"], bgate8)

    dx, dpre16, dpw16, dgp16, ln_sums = _ln_loss(x, merged, w_out16, nat["gp"], p16, w_ple16, target, bgate8,
                                                 ln_g, ln_b)
    loss_sum = (0.5 / D_MODEL) * jnp.sum(ln_sums[3])
    dya16, dyb16, dgm16, mg_sums = _merge_bwd(dpre16, w_out16, y_a, y_b, nat["gm"], bgate8)
    dys = _dx([dyb16], [w_brb], [], "dx_yssm")
    g_w_out, = _dw(merged, [dpre16], BF16, "dw_out")
    g_w_br = jnp.concatenate([_dw(oa, [dya16], BF16, "dw_bra")[0], _dw(y_ssm, [dyb16], BF16, "dw_brb")[0]], axis=0)
    g_w_ple, = _dw(p16, [dpw16], BF16, "dw_ple")

    do_att, dgatt16, own_order = _combine_bwd(dya16, w_bra, nat["gatt"], o_att, lse, dils[1:])
    dseg = {"gatt": dgatt16, "gm": dgm16, "gp": dgp16}
    dbias = []
    for g in range(3):
        cotangent = (do_att, o_att, lse) if g == 0 else (own_order[2 * g - 2], own_order[2 * g - 1])
        dqkv, db = _attn_bwd(qkv[g], bias, g, cotangent, dils[g],
                             "attn_bwd%d" % g)
        dseg["qkv%d" % g] = dqkv.reshape(nb, seq, -1)
        dbias.append(db)
    g_rel = _bias_grad(jnp.concatenate(dbias, axis=0), bmaps)[:, 0, :NUM_BUCKETS].T

    dact, ddtg, dz, ssd_small, g_normw = _ssd_bwd(
        act, nat["dt"], dt_bias_row, nat["z"], y_all, dys, sprev, alog_g, dskip_g, ssm_norm_w)
    dseg["z"] = dz
    dseg["dt"] = jnp.pad(_ungroup_lanes(ddtg), ((0, 0), (0, 0), (0, LANE - SSM_HEADS)))
    dpre, conv_sums = _conv_bwd_pre(dact, nat["xbc"], conv_wg, conv_bg, "conv_bwd")
    dseg["xbc"] = _conv_bwd_x(dpre, conv_wg, "conv_bwd_x")
    csum = _xbc_reference_order(conv_sums)

    dx_own = [_dx([dseg["qkv%d" % g]], [wseg["qkv%d" % g]], [], "dx_qkv%d" % g, True).reshape(
        nb, dils[g], seq // dils[g], D_MODEL) for g in (1, 2)]
    dwseg = {"qkv%d" % g: _dw(x16p[g], [dseg["qkv%d" % g]], BF16, "dw_qkv%d" % g, True)[0] for g in range(3)}
    for gi, group in enumerate(DW_GROUPS):
        dwseg.update(zip(group, _dw(x16, [dseg[s] for s in group], BF16, "dw_nat%d" % gi, True)))
    names = ["qkv0"] + [s for group, _ in NAT_GROUPS for s in group]
    dx = finish_dx([dseg[s] for s in names], [wseg[s] for s in names], [dx], dx_own, dwseg, g_w_br, g_w_out, g_w_ple)

    small = dict(
        b_gate=jnp.stack([mg_sums[0], mg_sums[1], ln_sums[2]]),
        conv_w=csum[0:4], conv_b=csum[4:5],
        dt_bias=_ungroup_lanes(ssd_small[:, 2:3, :]), a_log=_ungroup_lanes(ssd_small[:, 0:1, :]),
        d_skip=_ungroup_lanes(ssd_small[:, 1:2, :]), ssm_norm_w=g_normw,
        ln_g=ln_sums[0:1], ln_b=ln_sums[1:2], rel_bias=g_rel)
    return loss_sum, dx, small


DX_TM = 256
SMALL_ORDER = ("b_gate", "conv_w", "conv_b", "dt_bias", "a_log", "d_skip", "ssm_norm_w", "ln_g", "ln_b", "rel_bias")
SMALL_FULL_SHAPES = dict(b_gate=(3, 1024), conv_w=(4, 3072), conv_b=(1, 3072), dt_bias=(1, 32), a_log=(1, 32),
                         d_skip=(1, 32), ssm_norm_w=(1, 2048), ln_g=(1, 1024), ln_b=(1, 1024), rel_bias=(32, 36))


def kernel(x, p, w_in, b_gate, conv_w, conv_b, dt_bias, a_log, d_skip, ssm_norm_w, w_branch, w_out, w_ple, ln_g, ln_b, rel_bias, loss_target, m_w_in, m_b_gate, m_conv_w, m_conv_b, m_dt_bias, m_a_log, m_d_skip, m_ssm_norm_w, m_w_branch, m_w_out, m_w_ple, m_ln_g, m_ln_b, m_rel_bias, v_w_in, v_b_gate, v_conv_w, v_conv_b, v_dt_bias, v_a_log, v_d_skip, v_ssm_norm_w, v_w_branch, v_w_out, v_w_ple, v_ln_g, v_ln_b, v_rel_bias):
    cx, cy, cc = _place()
    chip = 2 * cx + cy
    dev = 4 * cx + 2 * cy + cc

    w_in_t = jnp.transpose(w_in[0])
    win16 = _shard_to_window(w_in_t, chip)
    g_win, g_br, g_out, g_ple = _allgather_pieces(
        [win16, w_branch[0].astype(BF16), w_out[0].astype(BF16), w_ple[0].astype(BF16)])
    wseg = _assemble(g_win)
    w_br16 = g_br.reshape(4 * 704, D_MODEL)
    w_out16 = g_out.reshape(D_MODEL, D_MODEL)
    w_ple16 = jnp.transpose(g_ple, (1, 0, 2)).reshape(PLE_DIM, D_MODEL)
    shards = _allgather8(_pack_rows([b_gate[0], conv_w[0]]), "allgather_small_params")
    per_chip = [_unpack_rows(shards[2 * k], [(3, 256), (4, 768)]) for k in range(4)]
    b_gate_full = jnp.concatenate([pc[0] for pc in per_chip], axis=1)
    conv_w_full = jnp.concatenate([pc[1] for pc in per_chip], axis=1)

    place = jnp.stack([chip, cc]).astype(jnp.int32)
    reduced = []

    def finish_dx(dhs, ws, accs, own_order_accs, dwseg, d_br, d_out, d_ple):
        grads = [_pack(dwseg), d_br.reshape(4, 704, D_MODEL), d_out.reshape(4, 256, D_MODEL),
                 jnp.transpose(d_ple.reshape(PLE_DIM, 4, 256), (1, 0, 2))]
        got = _sibling_exchange(grads)
        chip_sums = [_pair_sum(g, t, place, "grad_pair_sum_%d" % i) for i, (g, t) in enumerate(zip(grads, got))]
        dx, others = _dx(dhs, ws, accs, "dx_w_in_and_grad_chip_scatter", True, DX_TM, chip_sums, own_order_accs)
        fulls = [_chip_sum(s, t, place, "grad_chip_sum_%d" % i) for i, (s, t) in enumerate(zip(chip_sums, others))]
        reduced.extend(_sibling_gather(fulls))
        return dx

    loss_sum, grad_x, small = _local_grads(
        x, p[0], loss_target, wseg, w_br16, w_out16, w_ple16, b_gate_full, conv_w_full, conv_b, dt_bias, a_log,
        d_skip, ssm_norm_w, ln_g, ln_b, rel_bias, finish_dx)
    big = reduced
    g_w_in = _window_to_shard(big[0], chip)
    g_w_branch, g_w_out, g_w_ple = big[1], big[2], big[3]
    parts = _allgather8(_pack_rows([small[n] for n in SMALL_ORDER] + [loss_sum.reshape(1, 1)]),
                        "allgather_small_grads")
    small_sum = _sum_rows([parts[i] for i in range(8)], F32, "small_grad_sum")
    *reduced_small, loss = _unpack_rows(small_sum, [SMALL_FULL_SHAPES[n] for n in SMALL_ORDER] + [(1, 1)])
    loss = loss.reshape(())
    sg = dict(zip(SMALL_ORDER, reduced_small))
    sg["b_gate"] = lax.dynamic_slice_in_dim(sg["b_gate"], chip * 256, 256, axis=1)
    sg["conv_w"] = lax.dynamic_slice_in_dim(sg["conv_w"], chip * 768, 768, axis=1)
    del dev

    upd = {}
    upd["w_in"] = [jnp.transpose(t) for t in _adamw(w_in_t, g_w_in, jnp.transpose(m_w_in[0]),
                                                      jnp.transpose(v_w_in[0]), "adamw_w_in")]
    upd["w_branch"] = _adamw(w_branch[0], g_w_branch, m_w_branch[0], v_w_branch[0], "adamw_w_branch")
    upd["w_out"] = _adamw(w_out[0], g_w_out, m_w_out[0], v_w_out[0], "adamw_w_out")
    upd["w_ple"] = _adamw(w_ple[0], g_w_ple, m_w_ple[0], v_w_ple[0], "adamw_w_ple")
    small_w = dict(b_gate=b_gate, conv_w=conv_w, conv_b=conv_b, dt_bias=dt_bias, a_log=a_log, d_skip=d_skip,
                   ssm_norm_w=ssm_norm_w, ln_g=ln_g, ln_b=ln_b, rel_bias=rel_bias)
    small_m = dict(b_gate=m_b_gate, conv_w=m_conv_w, conv_b=m_conv_b, dt_bias=m_dt_bias, a_log=m_a_log,
                   d_skip=m_d_skip, ssm_norm_w=m_ssm_norm_w, ln_g=m_ln_g, ln_b=m_ln_b, rel_bias=m_rel_bias)
    small_v = dict(b_gate=v_b_gate, conv_w=v_conv_w, conv_b=v_conv_b, dt_bias=v_dt_bias, a_log=v_a_log,
                   d_skip=v_d_skip, ssm_norm_w=v_ssm_norm_w, ln_g=v_ln_g, ln_b=v_ln_b, rel_bias=v_rel_bias)
    shapes = [small_w[n].shape for n in SMALL_ORDER]
    s_delta, s_m, s_v = _adamw(_pack_rows([small_w[n] for n in SMALL_ORDER]), _pack_rows([sg[n] for n in SMALL_ORDER]),
                               _pack_rows([small_m[n] for n in SMALL_ORDER]), _pack_rows([small_v[n] for n in SMALL_ORDER]),
                               "adamw_small")
    for i, n in enumerate(SMALL_ORDER):
        upd[n] = tuple(_unpack_rows(t, shapes)[i] for t in (s_delta, s_m, s_v))
        sg[n] = sg[n].reshape(small_w[n].shape)

    order = ("w_in", "b_gate", "conv_w", "conv_b", "dt_bias", "a_log", "d_skip", "ssm_norm_w", "w_branch", "w_out",
             "w_ple", "ln_g", "ln_b", "rel_bias")
    grads = dict(sg, w_in=jnp.transpose(g_w_in)[None],w_branch=g_w_branch[None], w_out=g_w_out[None], w_ple=g_w_ple[None])
    lead = lambda n, t: t[None] if n in ("w_in", "w_branch", "w_out", "w_ple") else t
    return (loss, grad_x, *[grads[n] for n in order], *[lead(n, upd[n][0]) for n in order],
            *[lead(n, upd[n][1]) for n in order], *[lead(n, upd[n][2]) for n in order])
```

```python
import math

import numpy as np
import jax
import jax.numpy as jnp
from jax import lax
from jax.experimental import pallas as pl
from jax.experimental.pallas import tpu as pltpu

F32, BF16 = jnp.float32, jnp.bfloat16

D_MODEL = 1024
HEAD_DIM = 64
GROUP_HEADS = 12
ATT_OUT = GROUP_HEADS * HEAD_DIM
PATTERNS = ((128, 1), (512, 4), (2048, 16))
BAND = 128
NUM_BUCKETS = 32
MAX_DISTANCE = 2048
D_INNER = 2048
SSM_HEADS = 32
SSM_GROUPS = 4
GROUP_SSM_HEADS = SSM_HEADS // SSM_GROUPS
D_STATE = 128
CHUNK = 128
PLE_DIM = 256
ALPHA = 2.0 ** 0.25
LN_EPS = 1e-5
RMS_EPS = 1e-5
ADAM_LR, ADAM_B1, ADAM_B2, ADAM_EPS, ADAM_WD, ADAM_STEP = 0.001, 0.9, 0.999, 1e-08, 0.01, 10
NEG = -1e30

QKV_W = 3 * ATT_OUT
IN_COLS = 15904
SHARD_COLS = IN_COLS // 4
DT_COL = 12800
ROW_TILE = 16
WIN_ROWS = 4000


def _win_offset(k):
    return (k * SHARD_COLS) % ROW_TILE


def _win_start(k):
    return k * SHARD_COLS - _win_offset(k)

VMEM_LIMIT_BYTES = 56 * 1024 * 1024
LANE = 128
MESH = pl.DeviceIdType.MESH
NT = (((1,), (1,)), ((), ()))
TN = (((0,), (0,)), ((), ()))


def _pcall(body, **kw):
    return pl.pallas_call(body, **kw)


def _params(*sem):
    return pltpu.CompilerParams(dimension_semantics=sem, vmem_limit_bytes=VMEM_LIMIT_BYTES)


def _sigmoid(v):
    return jax.nn.sigmoid(v)


MM_TM = 512


def _tok_spec(tm, width):
    return pl.BlockSpec((None, tm, width), lambda b, i: (b, i, 0))


def _whole(arr, single_buffer=False):
    mode = dict(pipeline_mode=pl.Buffered(1)) if single_buffer else {}
    return pl.BlockSpec(arr.shape, lambda b, i: (0,) * arr.ndim, **mode)


def _proj(a3, ws, out_dtype, name, w_rows_are_outputs=False, tm=MM_TM):
    nb, seq, kdim = a3.shape
    nw = len(ws)
    widths = [w.shape[0] if w_rows_are_outputs else w.shape[1] for w in ws]

    def body(*refs):
        a = refs[0][...].astype(BF16)
        for w_ref, o_ref in zip(refs[1:1 + nw], refs[1 + nw:]):
            if w_rows_are_outputs:
                v = lax.dot_general(a, w_ref[...], NT, preferred_element_type=F32)
            else:
                v = jnp.dot(a, w_ref[...], preferred_element_type=F32)
            o_ref[...] = v.astype(out_dtype)

    return _pcall(
        body, name=name, grid=(nb, seq // tm),
        in_specs=[_tok_spec(tm, kdim)] + [_whole(w, True) for w in ws],
        out_specs=[_tok_spec(tm, n) for n in widths],
        out_shape=[jax.ShapeDtypeStruct((nb, seq, n), out_dtype) for n in widths],
        compiler_params=_params("parallel", "parallel"),
    )(a3, *ws)


def _dx(dhs, ws, accs, name, w_rows_are_outputs=False, tm=MM_TM, scatter=None, own_order_accs=()):
    nb, seq, _ = dhs[0].shape
    nd, nacc, npa = len(dhs), len(accs), len(own_order_accs)
    kout = ws[0].shape[1] if w_rows_are_outputs else ws[0].shape[0]
    sums = scatter or []
    ns = len(sums)
    chunks = [_row_chunks(s.shape[1], s.shape[2] * s.dtype.itemsize, ICI_CHUNK_BYTES) for s in sums]
    n_sem = 3 * sum(len(ch) for ch in chunks)
    grid = (nb, seq // tm)
    ntile = kout // LANE if npa else 0

    def body(*refs):
        n_in = 2 * nd + nacc + npa
        sum_refs, o_ref, got_refs = refs[n_in:n_in + ns], refs[n_in + ns], refs[n_in + ns + 1:n_in + 2 * ns + 1]
        tile_refs = refs[n_in + 2 * ns + 1:n_in + 2 * ns + 1 + ntile]

        def copies():
            send_sems, recv_sems = refs[-2], refs[-1]
            x, y, c = _place()
            out = []
            for a in range(ns):
                for s, m in chunks[a]:
                    for j, (cx, cy) in enumerate(_other_chips(x, y)):
                        k = len(out)
                        out.append(_remote(sum_refs[a].at[2 * cx + cy, pl.ds(s, m)], got_refs[a].at[j, pl.ds(s, m)],
                                           send_sems.at[k], recv_sems.at[k], (cx, cy, c)))
            return out

        if ns:
            @pl.when((pl.program_id(0) == 0) & (pl.program_id(1) == 0))
            def _():
                for cp in copies():
                    cp.start()

        v = None
        for dh_ref, w_ref in zip(refs[:nd], refs[nd:2 * nd]):
            dh = dh_ref[...].astype(BF16)
            if w_rows_are_outputs:
                t = jnp.dot(dh, w_ref[...], preferred_element_type=F32)
            else:
                t = lax.dot_general(dh, w_ref[...], NT, preferred_element_type=F32)
            v = t if v is None else v + t
        for a_ref in refs[2 * nd:2 * nd + nacc]:
            v = v + a_ref[...]
        for p_ref in refs[2 * nd + nacc:n_in]:
            v = v + _natural_rows(p_ref, tile_refs)
        o_ref[...] = v

        if ns:
            @pl.when((pl.program_id(0) == grid[0] - 1) & (pl.program_id(1) == grid[1] - 1))
            def _():
                for cp in copies():
                    cp.wait()

    out = _pcall(
        body, name=name, grid=grid,
        in_specs=[_tok_spec(tm, dh.shape[-1]) for dh in dhs] + [_whole(w, True) for w in ws]
        + [_tok_spec(tm, kout)] * nacc
        + [pl.BlockSpec((None, p.shape[1], tm // p.shape[1], kout), lambda b, i: (b, 0, i, 0)) for p in own_order_accs]
        + [ANY] * ns,
        out_specs=[_tok_spec(tm, kout)] + [ANY] * ns,
        out_shape=[jax.ShapeDtypeStruct((nb, seq, kout), F32)]
        + [jax.ShapeDtypeStruct((3,) + s.shape[1:], s.dtype) for s in sums],
        input_output_aliases={2 * nd: 0} if nacc else {},
        scratch_shapes=[pltpu.VMEM((tm, LANE), F32)] * ntile
        + ([pltpu.SemaphoreType.DMA((n_sem,)), pltpu.SemaphoreType.DMA((n_sem,))] if ns else []),
        compiler_params=pltpu.CompilerParams(
            dimension_semantics=("arbitrary", "arbitrary") if ns else ("parallel", "parallel"),
            vmem_limit_bytes=VMEM_LIMIT_BYTES, has_side_effects=bool(ns)),
    )(*dhs, *ws, *accs, *own_order_accs, *sums)
    return (out[0], list(out[1:])) if ns else out[0]


def _dw(a3, dhs, out_dtype, name, rows_are_outputs=False):
    nb, seq, kdim = a3.shape
    nd = len(dhs)
    grid = (nb, seq // MM_TM)
    shapes = [(dh.shape[-1], kdim) if rows_are_outputs else (kdim, dh.shape[-1]) for dh in dhs]

    def body(*refs):
        b, i = pl.program_id(0), pl.program_id(1)
        dh_refs, o_refs, acc_refs = refs[1:1 + nd], refs[1 + nd:1 + 2 * nd], refs[1 + 2 * nd:]

        @pl.when((b == 0) & (i == 0))
        def _():
            for acc_ref in acc_refs:
                acc_ref[...] = jnp.zeros_like(acc_ref)

        a = refs[0][...].astype(BF16)
        for dh_ref, acc_ref in zip(dh_refs, acc_refs):
            dh = dh_ref[...].astype(BF16)
            acc_ref[...] += lax.dot_general(*((dh, a) if rows_are_outputs else (a, dh)), TN,
                                            preferred_element_type=F32)

        @pl.when((b == grid[0] - 1) & (i == grid[1] - 1))
        def _():
            for o_ref, acc_ref in zip(o_refs, acc_refs):
                o_ref[...] = acc_ref[...].astype(out_dtype)

    return _pcall(
        body, name=name, grid=grid,
        in_specs=[_tok_spec(MM_TM, kdim)] + [_tok_spec(MM_TM, dh.shape[-1]) for dh in dhs],
        out_specs=[pl.BlockSpec(s, lambda b, i: (0, 0)) for s in shapes],
        out_shape=[jax.ShapeDtypeStruct(s, out_dtype) for s in shapes],
        scratch_shapes=[pltpu.VMEM(s, F32) for s in shapes],
        compiler_params=_params("arbitrary", "arbitrary"),
    )(a3, *dhs)


def _qkv_rows(g):
    return [(part * QKV_W + g * ATT_OUT + hp * LANE, LANE) for hp in range(ATT_OUT // LANE) for part in range(3)]


XBC_START = 3 * QKV_W + ATT_OUT + D_INNER
GROUP_CH = GROUP_SSM_HEADS * HEAD_DIM
XBC_GROUP = GROUP_CH + 2 * D_STATE
CONV_DIM = SSM_GROUPS * XBC_GROUP


def _xbc_ranges():
    out = []
    for g in range(SSM_GROUPS):
        out += [(g * GROUP_CH, GROUP_CH), (D_INNER + g * D_STATE, D_STATE),
                (D_INNER + SSM_GROUPS * D_STATE + g * D_STATE, D_STATE)]
    return out


def _join_last(parts):
    widths = [t.shape[-1] for t in parts]
    total, lead = sum(widths), [(0, 0)] * (parts[0].ndim - 1)
    starts = np.cumsum([0] + widths)
    return sum(jnp.pad(t, lead + [(int(s), total - int(s) - w)]) for t, s, w in zip(parts, starts, widths))


def _xbc_group_order(t):
    return _join_last([t[..., s:s + n] for s, n in _xbc_ranges()])


def _xbc_reference_order(t):
    g = lambda off, n: [t[..., k * XBC_GROUP + off:k * XBC_GROUP + off + n] for k in range(SSM_GROUPS)]
    return _join_last(g(0, GROUP_CH) + g(GROUP_CH, D_STATE) + g(GROUP_CH + D_STATE, D_STATE))


def _segments():
    one = lambda name, start, rows: (name, [(start, rows)], max(rows, LANE))
    return [("qkv%d" % g, _qkv_rows(g), QKV_W) for g in range(3)] + [
        one("gatt", 3 * QKV_W, ATT_OUT), one("z", 3 * QKV_W + ATT_OUT, D_INNER),
        ("xbc", [(XBC_START + s, n) for s, n in _xbc_ranges()], CONV_DIM), one("dt", DT_COL, SSM_HEADS),
        one("gm", DT_COL + SSM_HEADS, 2 * D_MODEL), one("gp", DT_COL + SSM_HEADS + 2 * D_MODEL, D_MODEL)]


LAYOUT_TC = 256
NAT_GROUPS = ((("gatt", "z", "dt", "gp"), 512), (("xbc", "gm"), 512))
DW_GROUPS = (("gatt", "z", "dt", "gp"), ("xbc",), ("gm",))


def _assemble(win):
    segs = _segments()

    def body(win_ref, *outs):
        def pieces(start, rows):
            t, end = start, start + rows
            while t < end:
                k = min(t // SHARD_COLS, 3)
                shard_end = (k + 1) * SHARD_COLS
                if k < 3 and shard_end % ROW_TILE and t == shard_end - shard_end % ROW_TILE:
                    lo = t - _win_start(k)
                    yield win_ref[k, lo:lo + ROW_TILE, :] + win_ref[k + 1, 0:ROW_TILE, :]
                    t += ROW_TILE
                    continue
                upto = min(end, shard_end - shard_end % ROW_TILE if k < 3 else end)
                yield win_ref[k, t - _win_start(k):upto - _win_start(k), :]
                t = upto

        for (_, ranges, total), o_ref in zip(segs, outs):
            off = 0
            for start, rows in ranges:
                for part in pieces(start, rows):
                    o_ref[off:off + part.shape[0], :] = part
                    off += part.shape[0]
            if off < total:
                o_ref[off:total, :] = jnp.zeros((total - off, o_ref.shape[1]), BF16)

    outs = _pcall(
        body, name="assemble_w_in", grid=(D_MODEL // LAYOUT_TC,),
        in_specs=[pl.BlockSpec((4, WIN_ROWS, LAYOUT_TC), lambda i: (0, 0, i))],
        out_specs=[pl.BlockSpec((total, LAYOUT_TC), lambda i: (0, i)) for _, _, total in segs],
        out_shape=[jax.ShapeDtypeStruct((total, D_MODEL), BF16) for _, _, total in segs],
        compiler_params=_params("parallel"),
    )(win)
    return {name: o for (name, _, _), o in zip(segs, outs)}


def _pack(dsegs):
    segs = _segments()

    def body(*refs):
        ins, o_ref = refs[:-1], refs[-1]
        tail = IN_COLS - _win_start(3)
        o_ref[3, tail:, :] = jnp.zeros((WIN_ROWS - tail, o_ref.shape[2]), BF16)
        for (_, ranges, _), s_ref in zip(segs, ins):
            off = 0
            for start, rows in ranges:
                for k in range(4):
                    lo = _win_start(k)
                    a, b = max(start, lo), min(start + rows, lo + WIN_ROWS)
                    if a < b:
                        o_ref[k, a - lo:b - lo, :] = s_ref[off + a - start:off + b - start, :]
                off += rows

    return _pcall(
        body, name="pack_dw_in", grid=(D_MODEL // LAYOUT_TC,),
        in_specs=[pl.BlockSpec((total, LAYOUT_TC), lambda i: (0, i)) for _, _, total in segs],
        out_specs=pl.BlockSpec((4, WIN_ROWS, LAYOUT_TC), lambda i: (0, 0, i)),
        out_shape=jax.ShapeDtypeStruct((4, WIN_ROWS, D_MODEL), BF16),
        compiler_params=_params("parallel"),
    )(*[dsegs[name] for name, _, _ in segs])


def _shard_to_window(shard_t, k):
    def at(off):
        return lambda w: jnp.pad(w.astype(BF16), ((off, WIN_ROWS - SHARD_COLS - off), (0, 0)))

    return lax.cond(k % 2 == 1, at(_win_offset(1)), at(_win_offset(0)), shard_t)


def _window_to_shard(win, k):
    return lax.dynamic_slice(win, ((k % 2) * _win_offset(1), 0), (SHARD_COLS, D_MODEL))


def _bucket_maps():
    qi = np.arange(8)[:, None]
    kj = np.arange(2 * BAND)[None, :]
    delta = qi + BAND - kj
    maps = []
    for window, dil in PATTERNS:
        valid = (delta >= 0) & (delta <= window // dil)
        dist = np.maximum(delta, 0) * dil
        max_exact = NUM_BUCKETS // 2
        d_f = np.maximum(dist, 1).astype(np.float32)
        large = max_exact + (np.log(d_f / np.float32(max_exact)) / np.float32(math.log(MAX_DISTANCE / max_exact))
                             * np.float32(NUM_BUCKETS - max_exact)).astype(np.int32)
        large = np.minimum(large, NUM_BUCKETS - 1)
        bucket = np.where(dist < max_exact, dist, large)
        maps.append(np.where(valid, bucket, -1).astype(np.int32))
    return np.stack(maps)


def _bias_tables(rel_bias, bmaps):
    def body(rb_ref, bm_ref, o_ref):
        g = pl.program_id(0)
        bm = bm_ref[...]
        for hh in range(GROUP_HEADS):
            acc = jnp.full(bm.shape, NEG, F32)
            for b in range(NUM_BUCKETS):
                acc = jnp.where(bm == b, rb_ref[b, g * GROUP_HEADS + hh], acc)
            for a in range(BAND // 8):
                o_ref[hh, 8 * a:8 * a + 8, :] = acc if a == 0 else pltpu.roll(acc, 8 * a, 1)

    return _pcall(
        body, name="bias_tables", grid=(3,),
        in_specs=[pl.BlockSpec(memory_space=pltpu.SMEM),
                  pl.BlockSpec((None, 8, 2 * BAND), lambda g: (g, 0, 0))],
        out_specs=pl.BlockSpec((GROUP_HEADS, BAND, 2 * BAND), lambda g: (g, 0, 0)),
        out_shape=jax.ShapeDtypeStruct((3 * GROUP_HEADS, BAND, 2 * BAND), F32),
        compiler_params=_params("parallel"),
    )(rel_bias, bmaps)


def _bias_grad(dbias, bmaps):
    def body(db_ref, bm_ref, o_ref):
        bm = bm_ref[...]
        lane = lax.broadcasted_iota(jnp.int32, (1, LANE), 1)
        for hh in range(GROUP_HEADS):
            db = db_ref[hh, 0:8, :]
            for a in range(1, BAND // 8):
                db = db + pltpu.roll(db_ref[hh, 8 * a:8 * a + 8, :], 2 * BAND - 8 * a, 1)
            vec = jnp.zeros((1, LANE), F32)
            for b in range(NUM_BUCKETS):
                s = jnp.sum(jnp.where(bm == b, db, 0.0), keepdims=True)
                vec = jnp.where(lane == b, s, vec)
            o_ref[hh] = vec

    return _pcall(
        body, name="bias_grad", grid=(3,),
        in_specs=[pl.BlockSpec((GROUP_HEADS, BAND, 2 * BAND), lambda g: (g, 0, 0)),
                  pl.BlockSpec((None, 8, 2 * BAND), lambda g: (g, 0, 0))],
        out_specs=pl.BlockSpec((GROUP_HEADS, 1, LANE), lambda g: (g, 0, 0)),
        out_shape=jax.ShapeDtypeStruct((3 * GROUP_HEADS, 1, LANE), F32),
        compiler_params=_params("parallel"),
    )(dbias, bmaps)


def _rows(n):
    if isinstance(n, int):
        return pl.ds(n * BAND, BAND)
    return pl.ds(pl.multiple_of(n * BAND, BAND), BAND)


def _for_blocks(blocks, nblk, per, carry):
    carry = blocks([0], carry, False)
    start = 1 + (nblk - 1) % per
    for n in range(1, start):
        carry = blocks([n], carry, True)
    trips = (nblk - start) // per
    if trips > 0:
        carry = lax.fori_loop(
            0, trips, lambda t, c: blocks([start + t * per + u for u in range(per)], c, True), carry)
    return carry


def _pairs_per_step(d):
    return {1: 3, 4: 6, 16: 6}[d]


def _bias_spec(group, hps):
    first = group * GROUP_HEADS // (2 * hps)
    return pl.BlockSpec((2 * hps, BAND, 2 * BAND), lambda hp, b, r: (first + hp, 0, 0))


def _attn_fwd(qkv4, bias, group, d, name):
    nb, _, sub, _ = qkv4.shape
    nblk = sub // BAND
    scale = HEAD_DIM ** -0.5
    npair = ATT_OUT // LANE
    hps = _pairs_per_step(d)
    compact = d > 1

    def body(qkv_ref, bias_ref, o_ref, l_ref):
        def blocks(ns, carry, with_prev):
            chains = [(bi, i, h) for bi in range(len(ns)) for i in range(hps) for h in range(2)]
            first_head = lax.broadcasted_iota(jnp.int32, (BAND, LANE), 1) < HEAD_DIM
            pair = lambda n, i, part: qkv_ref[_rows(n), (3 * i + part) * LANE:(3 * i + part + 1) * LANE]
            scores = []
            for bi, i, h in chains:
                n = ns[bi]
                qp = pair(n, i, 0) * scale
                q = jnp.where(first_head if h == 0 else jnp.logical_not(first_head), qp, jnp.zeros_like(qp))
                s_c = lax.dot_general(q, pair(n, i, 1), NT, preferred_element_type=F32) + bias_ref[2 * i + h, :, BAND:]
                s_p = None
                if with_prev:
                    s_p = lax.dot_general(q, pair(n - 1, i, 1), NT,
                                          preferred_element_type=F32) + bias_ref[2 * i + h, :, :BAND]
                scores.append((s_c, s_p))
            probs = []
            for s_c, s_p in scores:
                m = jnp.max(s_c, -1, keepdims=True)
                if with_prev:
                    m = jnp.maximum(m, jnp.max(s_p, -1, keepdims=True))
                e_c = jnp.exp(s_c - m)
                den = jnp.sum(e_c, -1, keepdims=True)
                e_p = None
                if with_prev:
                    e_p = jnp.exp(s_p - m)
                    den = den + jnp.sum(e_p, -1, keepdims=True)
                    e_p = e_p.astype(BF16)
                probs.append((e_c.astype(BF16), e_p, den, m))
            outs = {}
            for (bi, i, h), (e_c, e_p, den, m) in zip(chains, probs):
                n = ns[bi]
                acc = jnp.dot(e_c, pair(n, i, 2), preferred_element_type=F32)
                if with_prev:
                    acc = acc + jnp.dot(e_p, pair(n - 1, i, 2), preferred_element_type=F32)
                outs[(bi, i, h)] = (acc / den, m + jnp.log(den))
            lane = lax.broadcasted_iota(jnp.int32, (BAND, LANE), 1)
            for bi, n in enumerate(ns):
                per_head = jnp.zeros((BAND, LANE), F32)
                for i in range(hps):
                    o_ref[_rows(n), i * LANE:(i + 1) * LANE] = jnp.where(first_head, outs[(bi, i, 0)][0],
                                                                         outs[(bi, i, 1)][0])
                    if compact:
                        for h in range(2):
                            per_head = jnp.where(lane == 2 * i + h, outs[(bi, i, h)][1], per_head)
                    else:
                        l_ref[_rows(n), i * LANE:(i + 1) * LANE] = jnp.where(first_head, outs[(bi, i, 0)][1],
                                                                             outs[(bi, i, 1)][1])
                if compact:
                    l_ref[_rows(n), :] = per_head
            return carry

        _for_blocks(blocks, nblk, 2 if hps == 1 else 1, 0)

    in_specs = [pl.BlockSpec((None, None, sub, 3 * LANE * hps), lambda hp, b, r: (b, r, 0, hp)),
                _bias_spec(group, hps)]
    if compact:
        return _pcall(
            body, name=name, grid=(1, nb, d), in_specs=in_specs,
            out_specs=[pl.BlockSpec((None, None, sub, ATT_OUT), lambda hp, b, r: (b, r, 0, 0)),
                       pl.BlockSpec((None, None, sub, LANE), lambda hp, b, r: (b, r, 0, 0))],
            out_shape=[jax.ShapeDtypeStruct((nb, d, sub, ATT_OUT), F32), jax.ShapeDtypeStruct((nb, d, sub, LANE), F32)],
            compiler_params=_params("parallel", "parallel", "parallel"),
        )(qkv4, bias)
    ospec = pl.BlockSpec((None, sub, hps * LANE), lambda hp, b, r: (b, 0, r * (npair // hps) + hp))
    return _pcall(
        body, name=name, grid=(npair // hps, nb, d), in_specs=in_specs, out_specs=[ospec, ospec],
        out_shape=[jax.ShapeDtypeStruct((nb, sub, d * ATT_OUT), F32)] * 2,
        compiler_params=_params("parallel", "parallel", "parallel"),
    )(qkv4, bias)


STAT_LSE_LANE = 16


def _attn_bwd(qkv4, bias, group, cotangent, d, name):
    nb, _, sub, _ = qkv4.shape
    nblk = sub // BAND
    scale = HEAD_DIM ** -0.5
    npair = ATT_OUT // LANE
    hps = _pairs_per_step(d)
    compact = d > 1

    def body(qkv_ref, bias_ref, *rest):
        do_ref, dqkv_ref, db_ref = rest[0], rest[-2], rest[-1]
        b, r = pl.program_id(1), pl.program_id(2)

        @pl.when((b == 0) & (r == 0))
        def _():
            db_ref[...] = jnp.zeros_like(db_ref)

        def blocks(ns, carry, with_prev):
            sides = (0, 1) if with_prev else (0,)
            chains = [(bi, i, h, sd) for bi in range(len(ns)) for i in range(hps) for h in range(2) for sd in sides]
            first_head = lax.broadcasted_iota(jnp.int32, (BAND, LANE), 1) < HEAD_DIM
            own = lambda h, t: jnp.where(first_head if h == 0 else jnp.logical_not(first_head), t, jnp.zeros_like(t))
            pair = lambda rows, i, part: qkv_ref[rows, (3 * i + part) * LANE:(3 * i + part + 1) * LANE]
            key_rows = lambda bi, sd: _rows(ns[bi] - sd)
            qs = {}
            for bi in range(len(ns)):
                for i in range(hps):
                    q_pair = pair(_rows(ns[bi]), i, 0) * scale
                    do = do_ref[_rows(ns[bi]), i * LANE:(i + 1) * LANE]
                    do16 = do.astype(BF16)
                    for h in range(2):
                        if compact:
                            st_ref, head = rest[1], 2 * i + h
                            ebar = st_ref[_rows(ns[bi]), head:head + 1]
                            lcol = st_ref[_rows(ns[bi]), STAT_LSE_LANE + head:STAT_LSE_LANE + head + 1]
                        else:
                            ebar = jnp.sum(own(h, do * rest[1][_rows(ns[bi]), i * LANE:(i + 1) * LANE]), -1, keepdims=True)
                            lcol = rest[2][_rows(ns[bi]), i * LANE + h * HEAD_DIM:i * LANE + h * HEAD_DIM + 1]
                        qs[(bi, i, h)] = (own(h, q_pair), q_pair, own(h, do16), do16, ebar, lcol)
            raw = []
            for bi, i, h, sd in chains:
                q, _, do_h, _, _, _ = qs[(bi, i, h)]
                bias_blk = bias_ref[2 * i + h, :, :BAND] if sd else bias_ref[2 * i + h, :, BAND:]
                s = lax.dot_general(q, pair(key_rows(bi, sd), i, 1), NT, preferred_element_type=F32) + bias_blk
                dp = lax.dot_general(do_h, pair(key_rows(bi, sd), i, 2), NT, preferred_element_type=F32)
                raw.append((s, dp))
            soft = []
            for (bi, i, h, sd), (s, dp) in zip(chains, raw):
                ebar, lcol = qs[(bi, i, h)][4:]
                p = jnp.exp(s - lcol)
                ds = p * (dp - ebar)
                if sd:
                    db_ref[2 * i + h, :, :BAND] += ds
                else:
                    db_ref[2 * i + h, :, BAND:] += ds
                soft.append((p.astype(BF16), ds.astype(BF16)))
            grads = {}
            for (bi, i, h, sd), (p16, ds16) in zip(chains, soft):
                _, q_pair, _, do16 = qs[(bi, i, h)][:4]
                grads[(bi, i, h, sd)] = (
                    jnp.dot(ds16, pair(key_rows(bi, sd), i, 1), preferred_element_type=F32),
                    lax.dot_general(ds16, q_pair, TN, preferred_element_type=F32),
                    lax.dot_general(p16, do16, TN, preferred_element_type=F32))
            both = lambda bi, i, sd, which: jnp.where(first_head, grads[(bi, i, 0, sd)][which],
                                                      grads[(bi, i, 1, sd)][which])
            carry = list(carry) if carry is not None else None
            for bi, n in enumerate(ns):
                for i in range(hps):
                    base = 3 * LANE * i
                    dq = both(bi, i, 0, 0)
                    if with_prev:
                        dq = dq + both(bi, i, 1, 0)
                        dqkv_ref[_rows(n - 1), base + LANE:base + 2 * LANE] = (
                            carry[2 * i] + both(bi, i, 1, 1)).astype(BF16)
                        dqkv_ref[_rows(n - 1), base + 2 * LANE:base + 3 * LANE] = (
                            carry[2 * i + 1] + both(bi, i, 1, 2)).astype(BF16)
                    dqkv_ref[_rows(n), base:base + LANE] = (dq * scale).astype(BF16)
                carry = [t for i in range(hps) for t in (both(bi, i, 0, 1), both(bi, i, 0, 2))]
            return tuple(carry)

        carry = _for_blocks(blocks, nblk, 2 if hps == 1 else 1, None)
        for i in range(hps):
            base = 3 * LANE * i
            dqkv_ref[_rows(nblk - 1), base + LANE:base + 2 * LANE] = carry[2 * i].astype(BF16)
            dqkv_ref[_rows(nblk - 1), base + 2 * LANE:base + 3 * LANE] = carry[2 * i + 1].astype(BF16)

    qspec = pl.BlockSpec((None, None, sub, 3 * LANE * hps), lambda hp, b, r: (b, r, 0, hp))
    bspec = pl.BlockSpec((2 * hps, BAND, 2 * BAND), lambda hp, b, r: (hp, 0, 0))
    if compact:
        cspecs = [pl.BlockSpec((None, None, sub, ATT_OUT), lambda hp, b, r: (b, r, 0, 0)),
                  pl.BlockSpec((None, None, sub, LANE), lambda hp, b, r: (b, r, 0, 0))]
    else:
        cspecs = [pl.BlockSpec((None, sub, hps * LANE), lambda hp, b, r: (b, 0, r * (npair // hps) + hp))] * 3
    return _pcall(
        body, name=name, grid=(npair // hps, nb, d),
        in_specs=[qspec, _bias_spec(group, hps)] + cspecs, out_specs=[qspec, bspec],
        out_shape=[jax.ShapeDtypeStruct(qkv4.shape, BF16),
                   jax.ShapeDtypeStruct((GROUP_HEADS, BAND, 2 * BAND), F32)],
        compiler_params=_params("parallel", "arbitrary", "arbitrary"),
    )(qkv4, bias, *cotangent)


def _head_lanes(first_lane, one_channel):
    c = lax.broadcasted_iota(jnp.int32, (ATT_OUT, LANE), 0)
    lane = lax.broadcasted_iota(jnp.int32, (ATT_OUT, LANE), 1)
    hit = lane == first_lane + c // HEAD_DIM
    if one_channel:
        hit = hit & (c % HEAD_DIM == 0)
    return hit.astype(BF16)


def _exact_dot(v, m01, dims=None):
    parts = _split3(v)
    if dims is None:
        dot = lambda t: jnp.dot(t, m01, preferred_element_type=F32)
    else:
        dot = lambda t: lax.dot_general(t, m01, dims, preferred_element_type=F32)
    return (dot(parts[0]) + dot(parts[1])) + dot(parts[2])


def _store_own_order(value, tile_refs, out_ref):
    d, per, width = out_ref.shape
    for j in range(width // LANE):
        tile_refs[j][...] = value[:, j * LANE:(j + 1) * LANE]
    for r in range(d):
        rows = pl.ds(r, per, stride=d)
        for j in range(width // LANE):
            out_ref[r, :, j * LANE:(j + 1) * LANE] = tile_refs[j][rows, :].astype(out_ref.dtype)


def _token_orders(x, dilations):
    nb, seq, kdim = x.shape
    tm = 512

    def body(x_ref, nat_ref, *rest):
        outs, tile_refs = rest[:len(dilations)], rest[len(dilations):]
        xv = x_ref[...]
        nat_ref[...] = xv.astype(BF16)
        for o_ref in outs:
            _store_own_order(xv, tile_refs, o_ref)

    outs = _pcall(
        body, name="token_orders", grid=(nb, seq // tm), in_specs=[_tok_spec(tm, kdim)],
        out_specs=[_tok_spec(tm, kdim)]
        + [pl.BlockSpec((None, d, tm // d, kdim), lambda b, i: (b, 0, i, 0)) for d in dilations],
        out_shape=[jax.ShapeDtypeStruct((nb, seq, kdim), BF16)]
        + [jax.ShapeDtypeStruct((nb, d, seq // d, kdim), BF16) for d in dilations],
        scratch_shapes=[pltpu.VMEM((tm, LANE), F32)] * (kdim // LANE),
        compiler_params=_params("parallel", "parallel"),
    )(x)
    return [outs[0]] + [o.reshape(nb, seq, kdim) for o in outs[1:]]


def _natural_rows(p_ref, tile_refs):
    d, per, width = p_ref.shape
    for r in range(d):
        rows = pl.ds(r, per, stride=d)
        for j in range(width // LANE):
            tile_refs[j][rows, :] = p_ref[r, :, j * LANE:(j + 1) * LANE]
    return jnp.concatenate([tile_refs[j][...] for j in range(width // LANE)], axis=1)


def _combine_fwd(o0, l0, dilated, gatt):
    nb, seq, _ = gatt.shape
    tm = 512
    ntile = ATT_OUT // LANE

    def body(o0_ref, l0_ref, o1_ref, l1_ref, o2_ref, l2_ref, g_ref, oa_ref, oatt_ref, lse_ref, *tile_refs):
        spread = _head_lanes(0, False)
        l0v = l0_ref[...]
        l1v = _exact_dot(_natural_rows(l1_ref, tile_refs), spread, NT)
        l2v = _exact_dot(_natural_rows(l2_ref, tile_refs), spread, NT)
        m = jnp.maximum(jnp.maximum(l0v, l1v), l2v)
        tot = m + jnp.log(jnp.exp(l0v - m) + jnp.exp(l1v - m) + jnp.exp(l2v - m))
        o = jnp.exp(l0v - tot) * o0_ref[...]
        o = o + jnp.exp(l1v - tot) * _natural_rows(o1_ref, tile_refs)
        o = o + jnp.exp(l2v - tot) * _natural_rows(o2_ref, tile_refs)
        g = g_ref[...]
        oa_ref[...] = (o * (g * _sigmoid(g))).astype(BF16)
        oatt_ref[...] = o
        lse_ref[...] = tot

    spec = pl.BlockSpec((None, tm, ATT_OUT), lambda b, i: (b, i, 0))
    own = lambda t: pl.BlockSpec((None, t.shape[1], tm // t.shape[1], t.shape[3]), lambda b, i: (b, 0, i, 0))
    (o1, l1), (o2, l2) = dilated
    return _pcall(
        body, name="attn_combine", grid=(nb, seq // tm),
        in_specs=[spec, spec, own(o1), own(l1), own(o2), own(l2), spec], out_specs=[spec] * 3,
        out_shape=[jax.ShapeDtypeStruct((nb, seq, ATT_OUT), BF16), jax.ShapeDtypeStruct((nb, seq, ATT_OUT), F32),
                   jax.ShapeDtypeStruct((nb, seq, ATT_OUT), F32)],
        scratch_shapes=[pltpu.VMEM((tm, LANE), F32)] * ntile,
        compiler_params=_params("parallel", "parallel"),
    )(o0, l0, o1, l1, o2, l2, gatt)


def _combine_bwd(dya16, w_bra, gatt, o_att, lse, dilations):
    nb, seq, _ = gatt.shape
    tm = 512

    def body(dya_ref, w_ref, g_ref, o_ref, l_ref, do_ref, dg_ref, *rest):
        ntile = ATT_OUT // LANE
        outs, tile_refs = rest[:-ntile], rest[-ntile:]
        doa = lax.dot_general(dya_ref[...], w_ref[...], NT, preferred_element_type=F32)
        g = g_ref[...]
        sg = _sigmoid(g)
        do = doa * (g * sg)
        do_ref[...] = do
        stats = (_exact_dot(do * o_ref[...], _head_lanes(0, False))
                 + _exact_dot(l_ref[...], _head_lanes(STAT_LSE_LANE, True)))
        dg_ref[...] = (doa * o_ref[...] * (sg * (1.0 + g * (1.0 - sg)))).astype(BF16)
        for k in range(len(dilations)):
            _store_own_order(do, tile_refs, outs[2 * k])
            _store_own_order(stats, tile_refs, outs[2 * k + 1])

    spec = pl.BlockSpec((None, tm, ATT_OUT), lambda b, i: (b, i, 0))
    own = lambda d, width: pl.BlockSpec((None, d, tm // d, width), lambda b, i: (b, 0, i, 0))
    outs = _pcall(
        body, name="attn_combine_bwd", grid=(nb, seq // tm),
        in_specs=[_tok_spec(tm, D_MODEL), _whole(w_bra, True)] + [spec] * 3,
        out_specs=[spec, spec] + [own(d, w) for d in dilations for w in (ATT_OUT, LANE)],
        out_shape=[jax.ShapeDtypeStruct((nb, seq, ATT_OUT), F32), jax.ShapeDtypeStruct((nb, seq, ATT_OUT), BF16)]
        + [jax.ShapeDtypeStruct((nb, d, seq // d, w), t) for d in dilations for w, t in ((ATT_OUT, BF16), (LANE, F32))],
        scratch_shapes=[pltpu.VMEM((tm, LANE), F32)] * (ATT_OUT // LANE),
        compiler_params=_params("parallel", "parallel"),
    )(dya16, w_bra, gatt, o_att, lse)
    return outs[0], outs[1], outs[2:]


CONV_TM = 1024
CONV_TC = 1024


def _shift_down(cur, halo, k):
    rolled = pltpu.roll(cur, k, 0)
    hro = pltpu.roll(halo, k, 0)
    row = lax.broadcasted_iota(jnp.int32, hro.shape, 0)
    return jnp.concatenate([jnp.where(row < k, hro, rolled[:8]), rolled[8:]], axis=0)


def _shift_up(cur, halo, k):
    n = cur.shape[0]
    rolled = pltpu.roll(cur, n - k, 0)
    hro = pltpu.roll(halo, 8 - k, 0)
    row = lax.broadcasted_iota(jnp.int32, hro.shape, 0)
    return jnp.concatenate([rolled[:n - 8], jnp.where(row >= 8 - k, hro, rolled[n - 8:])], axis=0)


def _conv_pre(cur, halo, w_ref, b_ref):
    acc = cur * w_ref[3:4, :] + b_ref[...]
    for k in range(1, 4):
        acc = acc + _shift_down(cur, halo, k) * w_ref[3 - k:4 - k, :]
    return acc


def _conv_specs(seq):
    nblk = seq // CONV_TM
    cur = pl.BlockSpec((None, CONV_TM, CONV_TC), lambda cb, b, i: (b, i, cb))
    prev = pl.BlockSpec((None, 8, CONV_TC), lambda cb, b, i: (b, jnp.maximum(i * (CONV_TM // 8) - 1, 0), cb))
    nxt = pl.BlockSpec((None, 8, CONV_TC),
                       lambda cb, b, i: (b, jnp.minimum((i + 1) * (CONV_TM // 8), seq // 8 - 1), cb))
    wspec = pl.BlockSpec((4, CONV_TC), lambda cb, b, i: (0, cb))
    bspec = pl.BlockSpec((1, CONV_TC), lambda cb, b, i: (0, cb))
    return nblk, cur, prev, nxt, wspec, bspec


def _conv_fwd(xin, w4, bias, name):
    nb, seq, ch = xin.shape
    _, cur, prev, _, wspec, bspec = _conv_specs(seq)

    def body(x_ref, h_ref, w_ref, b_ref, o_ref):
        halo = jnp.where(pl.program_id(2) > 0, h_ref[...], 0.0)
        pre = _conv_pre(x_ref[...], halo, w_ref, b_ref)
        o_ref[...] = pre * _sigmoid(pre)

    return _pcall(
        body, name=name, grid=(ch // CONV_TC, nb, seq // CONV_TM),
        in_specs=[cur, prev, wspec, bspec], out_specs=cur,
        out_shape=jax.ShapeDtypeStruct(xin.shape, F32),
        compiler_params=_params("parallel", "parallel", "parallel"),
    )(xin, xin, w4, bias)


def _conv_bwd_pre(dact, xin, w4, bias, name):
    nb, seq, ch = xin.shape
    _, cur, prev, _, wspec, bspec = _conv_specs(seq)

    def body(da_ref, x_ref, h_ref, w_ref, b_ref, dp_ref, s_ref):
        b, i = pl.program_id(1), pl.program_id(2)

        @pl.when((b == 0) & (i == 0))
        def _():
            s_ref[...] = jnp.zeros_like(s_ref)

        halo = jnp.where(i > 0, h_ref[...], 0.0)
        x = x_ref[...]
        pre = _conv_pre(x, halo, w_ref, b_ref)
        sg = _sigmoid(pre)
        dpre = da_ref[...] * (sg * (1.0 + pre * (1.0 - sg)))
        dp_ref[...] = dpre
        s_ref[3:4, :] += jnp.sum(dpre * x, 0, keepdims=True)
        for k in range(1, 4):
            s_ref[3 - k:4 - k, :] += jnp.sum(dpre * _shift_down(x, halo, k), 0, keepdims=True)
        s_ref[4:5, :] += jnp.sum(dpre, 0, keepdims=True)

    return _pcall(
        body, name=name, grid=(ch // CONV_TC, nb, seq // CONV_TM),
        in_specs=[cur, cur, prev, wspec, bspec],
        out_specs=[cur, pl.BlockSpec((8, CONV_TC), lambda cb, b, i: (0, cb))],
        out_shape=[jax.ShapeDtypeStruct(xin.shape, F32), jax.ShapeDtypeStruct((8, ch), F32)],
        compiler_params=_params("parallel", "arbitrary", "arbitrary"),
    )(dact, xin, xin, w4, bias)


def _conv_bwd_x(dpre, w4, name):
    nb, seq, ch = dpre.shape
    nblk, cur, _, nxt, wspec, _ = _conv_specs(seq)

    def body(d_ref, n_ref, w_ref, o_ref):
        halo = jnp.where(pl.program_id(2) < nblk - 1, n_ref[...], 0.0)
        cur_v = d_ref[...]
        acc = cur_v * w_ref[3:4, :]
        for j in range(1, 4):
            acc = acc + _shift_up(cur_v, halo, j) * w_ref[3 - j:4 - j, :]
        o_ref[...] = acc.astype(BF16)

    return _pcall(
        body, name=name, grid=(ch // CONV_TC, nb, seq // CONV_TM),
        in_specs=[cur, nxt, wspec], out_specs=cur,
        out_shape=jax.ShapeDtypeStruct(dpre.shape, BF16),
        compiler_params=_params("parallel", "parallel", "parallel"),
    )(dpre, dpre, w4)


def _step_sizes(raw, tb_ref):
    shift = (LANE - GROUP_SSM_HEADS * pl.program_id(0)) % LANE
    v = pltpu.roll(raw + tb_ref[...], shift, 1)
    own = lax.broadcasted_iota(jnp.int32, v.shape, 1) < GROUP_SSM_HEADS
    sp = jnp.maximum(v, 0.0) + jnp.log1p(jnp.exp(-jnp.abs(v)))
    return jnp.where(own, sp, 0.0), jnp.where(own, _sigmoid(v), 0.0)


def _group_lanes(t):
    pads = [(0, 0)] * (t.ndim - 1) + [(0, LANE - GROUP_SSM_HEADS)]
    return jnp.stack([jnp.pad(t[..., GROUP_SSM_HEADS * g:GROUP_SSM_HEADS * (g + 1)], pads) for g in range(SSM_GROUPS)])


def _ungroup_lanes(t):
    return jnp.concatenate([t[g][..., :GROUP_SSM_HEADS] for g in range(SSM_GROUPS)], axis=-1)


def _decays(dt, al_ref):
    row = lax.broadcasted_iota(jnp.int32, (CHUNK, CHUNK), 0)
    col = lax.broadcasted_iota(jnp.int32, (CHUNK, CHUNK), 1)
    tril = (row >= col).astype(BF16)
    triu = (row <= col).astype(BF16)
    arow = -jnp.exp(al_ref[...])
    hi, mid, lo = _split3(dt * arow)
    down = lambda t: jnp.dot(tril, t, preferred_element_type=F32)
    across = lambda t: lax.dot_general(t, triu, TN, preferred_element_type=F32)
    acs = (down(hi) + down(mid)) + down(lo)
    acs_t = (across(hi) + across(mid)) + across(lo)
    return arow, acs, acs_t, row >= col, triu


STEP_CHUNKS = 8


def _ssd_specs(nb, seq):
    nc = seq // CHUNK
    hw = GROUP_SSM_HEADS * HEAD_DIM
    rows, steps = STEP_CHUNKS * CHUNK, nc // STEP_CHUNKS

    def mk(rev):
        cidx = (lambda c: steps - 1 - c) if rev else (lambda c: c)
        wide = pl.BlockSpec((None, rows, hw), lambda g, b, c: (b, cidx(c), g))
        xbc = pl.BlockSpec((None, rows, XBC_GROUP), lambda g, b, c: (b, cidx(c), g))
        lanes = pl.BlockSpec((None, None, rows, LANE), lambda g, b, c: (g, b, cidx(c), 0))
        prev = pl.BlockSpec((None, STEP_CHUNKS, None, D_STATE, hw), lambda g, b, c: (b, cidx(c), g, 0, 0))
        raw = pl.BlockSpec((None, rows, LANE), lambda g, b, c: (b, cidx(c), 0))
        return wide, xbc, lanes, prev, raw

    grow = pl.BlockSpec((None, 1, LANE), lambda g, b, c: (g, 0, 0))
    nwspec = pl.BlockSpec((1, hw), lambda g, b, c: (0, g))
    tbspec = pl.BlockSpec((1, LANE), lambda g, b, c: (0, 0))
    return nc, steps, hw, mk, grow, nwspec, tbspec


def _head_expand():
    hw = GROUP_SSM_HEADS * HEAD_DIM
    r = lax.broadcasted_iota(jnp.int32, (LANE, hw), 0)
    c = lax.broadcasted_iota(jnp.int32, (LANE, hw), 1)
    return ((c // HEAD_DIM) == r).astype(BF16)


def _split3(v):
    hi = v.astype(BF16)
    rest = v - hi.astype(F32)
    mid = rest.astype(BF16)
    return hi, mid, (rest - mid.astype(F32)).astype(BF16)


def _to_channels(v, e):
    hi, mid, lo = _split3(v)
    dot = lambda t: jnp.dot(t, e, preferred_element_type=F32)
    return (dot(hi) + dot(mid)) + dot(lo)


def _to_heads(w, e):
    hi, mid, lo = _split3(w)
    dot = lambda t: lax.dot_general(t, e, (((1,), (1,)), ((), ())), preferred_element_type=F32)
    return (dot(hi) + dot(mid)) + dot(lo)


def _row8(v):
    return jnp.broadcast_to(v, (8, v.shape[1]))


def _ssd_chunk_setup(dt, al_ref, ds_ref):
    arow, acs, acs_t, causal, triu = _decays(dt, al_ref)
    e = _head_expand()
    dtx = _to_channels(dt, e)
    acsx = _to_channels(acs, e)
    lastx = acsx[CHUNK - 1:CHUNK, :]
    dskx = _to_channels(_row8(ds_ref[...]), e)[0:1, :]
    return arow, acs, acs_t, causal, triu, e, dtx, acsx, lastx, dskx


def _ssd_fwd(xbc, dt_raw, dt_bias_row, z, alog_g, dskip_g, normw):
    nb, seq, _ = xbc.shape
    nc, steps, hw, mk, grow, nwspec, tbspec = _ssd_specs(nb, seq)
    wide, xbc_spec, lanes, prev, raw = mk(False)
    tn = (((0,), (0,)), ((), ()))

    def body(xbc_ref, dt_ref, tb_ref, z_ref, al_ref, ds_ref, nw_ref, ys_ref, y_ref, sp_ref, st_ref):
        @pl.when(pl.program_id(2) == 0)
        def _():
            st_ref[...] = jnp.zeros_like(st_ref)

        for ci in range(STEP_CHUNKS):
            chunk(ci, xbc_ref, dt_ref, tb_ref, z_ref, al_ref, ds_ref, nw_ref, ys_ref, y_ref, sp_ref, st_ref)

    def chunk(ci, xbc_ref, dt_ref, tb_ref, z_ref, al_ref, ds_ref, nw_ref, ys_ref, y_ref, sp_ref, st_ref):
        rows = slice(ci * CHUNK, (ci + 1) * CHUNK)
        dt, _ = _step_sizes(dt_ref[rows, :], tb_ref)
        _, acs, acs_t, causal, _, _, dtx, acsx, lastx, dskx = _ssd_chunk_setup(dt, al_ref, ds_ref)
        bmat = xbc_ref[rows, GROUP_CH:GROUP_CH + D_STATE].astype(BF16)
        cmat = xbc_ref[rows, GROUP_CH + D_STATE:].astype(BF16)
        cb = lax.dot_general(cmat, bmat, (((1,), (1,)), ((), ())), preferred_element_type=F32)
        x = xbc_ref[rows, :GROUP_CH]
        xdt = x * dtx
        xdt16 = xdt.astype(BF16)
        first_head = lax.broadcasted_iota(jnp.int32, (CHUNK, LANE), 1) < HEAD_DIM
        pairs = []
        for hp in range(GROUP_SSM_HEADS // 2):
            xp = xdt16[:, hp * LANE:(hp + 1) * LANE]
            two = []
            for j in (2 * hp, 2 * hp + 1):
                lmat = jnp.exp(jnp.where(causal, acs[:, j:j + 1] - acs_t[j:j + 1, :], -jnp.inf))
                two.append(jnp.dot((cb * lmat).astype(BF16), xp, preferred_element_type=F32))
            pairs.append(jnp.where(first_head, two[0], two[1]))
        yd = jnp.concatenate(pairs, axis=1)
        s_prev = st_ref[...]
        s16 = s_prev.astype(BF16)
        sp_ref[ci] = s16
        yo = jnp.dot(cmat, s16, preferred_element_type=F32) * jnp.exp(acsx)
        sts = lax.dot_general(bmat, (xdt * jnp.exp(lastx - acsx)).astype(BF16), tn, preferred_element_type=F32)
        st_ref[...] = s_prev * jnp.exp(lastx) + sts
        y = yd + yo + dskx * x
        zz = z_ref[rows, :]
        u = y * (zz * _sigmoid(zz))
        rn = lax.rsqrt(jnp.mean(u * u, -1, keepdims=True) + RMS_EPS)
        ys_ref[rows, :] = (u * rn * nw_ref[...]).astype(BF16)
        y_ref[rows, :] = y

    return _pcall(
        body, name="ssd_fwd", grid=(SSM_GROUPS, nb, steps),
        in_specs=[xbc_spec, raw, tbspec, wide, grow, grow, nwspec],
        out_specs=[wide, wide, prev],
        out_shape=[jax.ShapeDtypeStruct((nb, seq, D_INNER), BF16), jax.ShapeDtypeStruct((nb, seq, D_INNER), F32),
                   jax.ShapeDtypeStruct((nb, nc, SSM_GROUPS, D_STATE, hw), BF16)],
        scratch_shapes=[pltpu.VMEM((D_STATE, hw), F32)],
        compiler_params=_params("parallel", "parallel", "arbitrary"),
    )(xbc, dt_raw, dt_bias_row, z, alog_g, dskip_g, normw)


def _ssd_bwd(xbc, dt_raw, dt_bias_row, z, y, dys, sprev, alog_g, dskip_g, normw):
    nb, seq, _ = xbc.shape
    nc, steps, hw, mk, grow, nwspec, tbspec = _ssd_specs(nb, seq)
    wide, xbc_spec, lanes, prev, raw = mk(True)
    nt = (((1,), (1,)), ((), ()))
    tn = (((0,), (0,)), ((), ()))

    def body(xbc_ref, dt_ref, tb_ref, z_ref, y_ref, dys_ref, sp_ref, al_ref, ds_ref, nw_ref,
             dxbc_ref, ddt_ref, dz_ref, small_ref, dnw_ref, g_ref):
        b, c = pl.program_id(1), pl.program_id(2)

        @pl.when((b == 0) & (c == 0))
        def _():
            small_ref[...] = jnp.zeros_like(small_ref)
            dnw_ref[...] = jnp.zeros_like(dnw_ref)

        @pl.when(c == 0)
        def _():
            g_ref[...] = jnp.zeros_like(g_ref)

        for ci in reversed(range(STEP_CHUNKS)):
            chunk(ci, xbc_ref, dt_ref, tb_ref, z_ref, y_ref, dys_ref, sp_ref, al_ref, ds_ref, nw_ref,
                  dxbc_ref, ddt_ref, dz_ref, small_ref, dnw_ref, g_ref)

    def chunk(ci, xbc_ref, dt_ref, tb_ref, z_ref, y_ref, dys_ref, sp_ref, al_ref, ds_ref, nw_ref,
              dxbc_ref, ddt_ref, dz_ref, small_ref, dnw_ref, g_ref):
        rows = slice(ci * CHUNK, (ci + 1) * CHUNK)
        yv, zz, dys_v, nw = y_ref[rows, :], z_ref[rows, :], dys_ref[rows, :], nw_ref[...]
        sz = _sigmoid(zz)
        silu = zz * sz
        u = yv * silu
        rn = lax.rsqrt(jnp.mean(u * u, -1, keepdims=True) + RMS_EPS)
        gn = dys_v * nw
        du = rn * gn - u * (rn * rn * rn) * jnp.mean(u * gn, -1, keepdims=True)
        dnw_ref[...] += jnp.sum(dys_v * u * rn, 0, keepdims=True)
        dy = du * silu
        dz_ref[rows, :] = du * yv * (sz * (1.0 + zz * (1.0 - sz)))

        dt, sg = _step_sizes(dt_ref[rows, :], tb_ref)
        arow, acs, acs_t, causal, triu, e, dtx, acsx, lastx, dskx = _ssd_chunk_setup(dt, al_ref, ds_ref)
        dfsx = jnp.exp(acsx)
        dtex = jnp.exp(lastx - acsx)
        bmat = xbc_ref[rows, GROUP_CH:GROUP_CH + D_STATE].astype(BF16)
        cmat = xbc_ref[rows, GROUP_CH + D_STATE:].astype(BF16)
        cb = lax.dot_general(cmat, bmat, nt, preferred_element_type=F32)
        x = xbc_ref[rows, :GROUP_CH]
        xdt = x * dtx
        xdt16 = xdt.astype(BF16)
        xdte = xdt * dtex
        dy16 = dy.astype(BF16)
        dyd = dy * dfsx
        dyd16 = dyd.astype(BF16)
        s16 = sp_ref[ci]
        g = g_ref[...]
        g16 = g.astype(BF16)
        cs = jnp.dot(cmat, s16, preferred_element_type=F32)
        dc_off = lax.dot_general(dyd16, s16, nt, preferred_element_type=F32)
        g_here = lax.dot_general(cmat, dyd16, tn, preferred_element_type=F32)
        bg = jnp.dot(bmat, g16, preferred_element_type=F32)
        db_st = lax.dot_general(xdte.astype(BF16), g16, nt, preferred_element_type=F32)
        ddte_w = bg * xdte
        dcd = _to_heads(_row8(jnp.sum(g * s16.astype(F32), 0, keepdims=True)), e)[0:1, :]
        lane = lax.broadcasted_iota(jnp.int32, (CHUNK, LANE), 1)
        first_head = lane < HEAD_DIM
        sub = lax.broadcasted_iota(jnp.int32, (CHUNK, LANE), 0)
        dacs = jnp.zeros((CHUNK, LANE), F32)
        colsums = jnp.zeros((CHUNK, LANE), F32)
        dcb = jnp.zeros((CHUNK, CHUNK), F32)
        pairs = []
        for hp in range(GROUP_SSM_HEADS // 2):
            xp = xdt16[:, hp * LANE:(hp + 1) * LANE]
            dyp = dy16[:, hp * LANE:(hp + 1) * LANE]
            two = []
            for idx, j in enumerate((2 * hp, 2 * hp + 1)):
                lmat = jnp.exp(jnp.where(causal, acs[:, j:j + 1] - acs_t[j:j + 1, :], -jnp.inf))
                mf = cb * lmat
                dy_h = jnp.where(first_head if idx == 0 else jnp.logical_not(first_head), dyp, jnp.zeros_like(dyp))
                dm = lax.dot_general(dy_h, xp, nt, preferred_element_type=F32)
                two.append(lax.dot_general(mf.astype(BF16), dyp, tn, preferred_element_type=F32))
                wmat = dm * mf
                dcb = dcb + dm * lmat
                dacs = jnp.where(lane == j, jnp.sum(wmat, -1, keepdims=True), dacs)
                colsums = jnp.where(sub == j, jnp.sum(wmat, 0, keepdims=True), colsums)
            pairs.append(jnp.where(first_head, two[0], two[1]))
        dxdt = bg * dtex + jnp.concatenate(pairs, axis=1)
        dacs = dacs - colsums.T + _to_heads(dyd * cs - ddte_w, e)
        cd_row = jnp.exp(acs[CHUNK - 1:CHUNK, :])
        tail = _to_heads(_row8(jnp.sum(ddte_w, 0, keepdims=True)), e)[0:1, :] + dcd * cd_row
        dacs = dacs + jnp.where(sub == CHUNK - 1, tail, 0.0)
        d_hi, d_mid, d_lo = _split3(dacs)
        up = lambda t: jnp.dot(triu, t, preferred_element_type=F32)
        da = (up(d_hi) + up(d_mid)) + up(d_lo)
        ddt_raw = (da * arow + _to_heads(dxdt * x, e)) * sg
        ddt_ref[rows, :] = ddt_raw
        small_ref[0:1, :] += jnp.sum(da * dt, 0, keepdims=True) * arow
        small_ref[1:2, :] += _to_heads(_row8(jnp.sum(dy * x, 0, keepdims=True)), e)[0:1, :]
        small_ref[2:3, :] += jnp.sum(ddt_raw, 0, keepdims=True)
        dcb16 = dcb.astype(BF16)
        dxbc_ref[rows, GROUP_CH + D_STATE:] = dc_off + jnp.dot(dcb16, bmat, preferred_element_type=F32)
        dxbc_ref[rows, GROUP_CH:GROUP_CH + D_STATE] = db_st + lax.dot_general(dcb16, cmat, tn,
                                                                               preferred_element_type=F32)
        dxbc_ref[rows, :GROUP_CH] = dxdt * dtx + dskx * dy
        g_ref[...] = g * jnp.exp(lastx) + g_here

    return _pcall(
        body, name="ssd_bwd", grid=(SSM_GROUPS, nb, steps),
        in_specs=[xbc_spec, raw, tbspec, wide, wide, wide, prev, grow, grow, nwspec],
        out_specs=[xbc_spec, lanes, wide,
                   pl.BlockSpec((None, 8, LANE), lambda g, b, c: (g, 0, 0)), nwspec],
        out_shape=[jax.ShapeDtypeStruct((nb, seq, CONV_DIM), F32),
                   jax.ShapeDtypeStruct((SSM_GROUPS, nb, seq, LANE), F32),
                   jax.ShapeDtypeStruct((nb, seq, D_INNER), F32),
                   jax.ShapeDtypeStruct((SSM_GROUPS, 8, LANE), F32),
                   jax.ShapeDtypeStruct((1, D_INNER), F32)],
        scratch_shapes=[pltpu.VMEM((D_STATE, hw), F32)],
        compiler_params=_params("parallel", "arbitrary", "arbitrary"),
    )(xbc, dt_raw, dt_bias_row, z, y, dys, sprev, alog_g, dskip_g, normw)


EW_TM = 256


def _merge_fwd(oa16, y_ssm16, w_bra, w_brb, gm, bgate):
    nb, seq, _ = oa16.shape

    def body(oa_ref, ys_ref, wa_ref, wb_ref, ga_ref, gb_ref, bg_ref, a_ref, b_ref, o_ref):
        y_a = jnp.dot(oa_ref[...], wa_ref[...], preferred_element_type=F32)
        y_b = jnp.dot(ys_ref[...], wb_ref[...], preferred_element_type=F32)
        a_ref[...] = y_a
        b_ref[...] = y_b
        sa = _sigmoid(ga_ref[...] + bg_ref[0:1, :])
        sb = _sigmoid(gb_ref[...] + bg_ref[1:2, :])
        o_ref[...] = (sa * y_a + sb * y_b).astype(BF16)

    spec = pl.BlockSpec((None, EW_TM, D_MODEL), lambda b, i: (b, i, 0))
    spec1 = pl.BlockSpec((None, EW_TM, D_MODEL), lambda b, i: (b, i, 1))
    return _pcall(
        body, name="merge_fwd", grid=(nb, seq // EW_TM),
        in_specs=[_tok_spec(EW_TM, ATT_OUT), _tok_spec(EW_TM, D_INNER), _whole(w_bra, True), _whole(w_brb, True),
                  spec, spec1, pl.BlockSpec((8, D_MODEL), lambda b, i: (0, 0))],
        out_specs=[spec] * 3,
        out_shape=[jax.ShapeDtypeStruct((nb, seq, D_MODEL), F32)] * 2 + [jax.ShapeDtypeStruct((nb, seq, D_MODEL), BF16)],
        compiler_params=_params("parallel", "parallel"),
    )(oa16, y_ssm16, w_bra, w_brb, gm, gm, bgate)


def _merge_bwd(dpre16, w_out16, y_a, y_b, gm, bgate):
    nb, seq, _ = y_a.shape

    def body(dp_ref, w_ref, a_ref, b_ref, ga_ref, gb_ref, bg_ref, dya_ref, dyb_ref, dg_ref, s_ref):
        @pl.when((pl.program_id(0) == 0) & (pl.program_id(1) == 0))
        def _():
            s_ref[...] = jnp.zeros_like(s_ref)

        dm = lax.dot_general(dp_ref[...], w_ref[...], NT, preferred_element_type=F32)
        sa = _sigmoid(ga_ref[...] + bg_ref[0:1, :])
        sb = _sigmoid(gb_ref[...] + bg_ref[1:2, :])
        dya_ref[...] = (dm * sa).astype(BF16)
        dyb_ref[...] = (dm * sb).astype(BF16)
        dga = dm * a_ref[...] * (sa * (1.0 - sa))
        dgb = dm * b_ref[...] * (sb * (1.0 - sb))
        dg_ref[:, :D_MODEL] = dga.astype(BF16)
        dg_ref[:, D_MODEL:] = dgb.astype(BF16)
        s_ref[0:1, :] += jnp.sum(dga, 0, keepdims=True)
        s_ref[1:2, :] += jnp.sum(dgb, 0, keepdims=True)

    spec = pl.BlockSpec((None, EW_TM, D_MODEL), lambda b, i: (b, i, 0))
    spec1 = pl.BlockSpec((None, EW_TM, D_MODEL), lambda b, i: (b, i, 1))
    small = pl.BlockSpec((8, D_MODEL), lambda b, i: (0, 0))
    return _pcall(
        body, name="merge_bwd", grid=(nb, seq // EW_TM),
        in_specs=[spec, _whole(w_out16, True), spec, spec, spec, spec1, small],
        out_specs=[spec, spec, pl.BlockSpec((None, EW_TM, 2 * D_MODEL), lambda b, i: (b, i, 0)), small],
        out_shape=[jax.ShapeDtypeStruct((nb, seq, D_MODEL), BF16), jax.ShapeDtypeStruct((nb, seq, D_MODEL), BF16),
                   jax.ShapeDtypeStruct((nb, seq, 2 * D_MODEL), BF16), jax.ShapeDtypeStruct((8, D_MODEL), F32)],
        compiler_params=_params("arbitrary", "arbitrary"),
    )(dpre16, w_out16, y_a, y_b, gm, gm, bgate)


def _ln_loss(x, merged16, w_out16, gp, p16, w_ple16, target, bgate, ln_g, ln_b):
    nb, seq, _ = x.shape

    def body(x_ref, m_ref, wo_ref, gp_ref, p_ref, wp_ref, t_ref, bg_ref, g_ref, b_ref,
             dx_ref, dp_ref, dpw_ref, dgp_ref, s_ref):
        @pl.when((pl.program_id(0) == 0) & (pl.program_id(1) == 0))
        def _():
            s_ref[...] = jnp.zeros_like(s_ref)

        sp = _sigmoid(gp_ref[...] + bg_ref[2:3, :])
        pw = jnp.dot(p_ref[...], wp_ref[...], preferred_element_type=F32)
        mix = jnp.dot(m_ref[...], wo_ref[...], preferred_element_type=F32)
        pre = ALPHA * x_ref[...] + mix + sp * pw
        mu = jnp.mean(pre, -1, keepdims=True)
        cen = pre - mu
        rstd = lax.rsqrt(jnp.mean(cen * cen, -1, keepdims=True) + LN_EPS)
        xhat = cen * rstd
        err = xhat * g_ref[...] + b_ref[...] - t_ref[...]
        dy = err * (1.0 / D_MODEL)
        dxh = dy * g_ref[...]
        dpre = rstd * (dxh - jnp.mean(dxh, -1, keepdims=True) - xhat * jnp.mean(dxh * xhat, -1, keepdims=True))
        dx_ref[...] = ALPHA * dpre
        dp_ref[...] = dpre.astype(BF16)
        dpw_ref[...] = (dpre * sp).astype(BF16)
        dgp = dpre * pw * (sp * (1.0 - sp))
        dgp_ref[...] = dgp.astype(BF16)
        s_ref[0:1, :] += jnp.sum(dy * xhat, 0, keepdims=True)
        s_ref[1:2, :] += jnp.sum(dy, 0, keepdims=True)
        s_ref[2:3, :] += jnp.sum(dgp, 0, keepdims=True)
        s_ref[3:4, :] += jnp.sum(err * err, 0, keepdims=True)

    spec = pl.BlockSpec((None, EW_TM, D_MODEL), lambda b, i: (b, i, 0))
    small = pl.BlockSpec((8, D_MODEL), lambda b, i: (0, 0))
    row = pl.BlockSpec((1, D_MODEL), lambda b, i: (0, 0))
    return _pcall(
        body, name="ln_loss", grid=(nb, seq // EW_TM),
        in_specs=[spec, spec, _whole(w_out16, True), spec, pl.BlockSpec((None, EW_TM, PLE_DIM), lambda b, i: (b, i, 0)),
                  _whole(w_ple16, True), spec, small, row, row],
        out_specs=[spec] * 4 + [small],
        out_shape=[jax.ShapeDtypeStruct((nb, seq, D_MODEL), F32)] + [jax.ShapeDtypeStruct((nb, seq, D_MODEL), BF16)] * 3
        + [jax.ShapeDtypeStruct((8, D_MODEL), F32)],
        compiler_params=_params("arbitrary", "arbitrary"),
    )(x, merged16, w_out16, gp, p16, w_ple16, target, bgate, ln_g, ln_b)


def _adamw(w, g, m, v, name):
    rows, cols = w.shape
    tr = _row_tile(rows, cols, 8, 5 << 19)
    c1 = 1.0 - ADAM_B1 ** ADAM_STEP
    c2 = 1.0 - ADAM_B2 ** ADAM_STEP

    def body(w_ref, g_ref, m_ref, v_ref, d_ref, nm_ref, nv_ref):
        gv = g_ref[...]
        nm = ADAM_B1 * m_ref[...] + (1.0 - ADAM_B1) * gv
        nv = ADAM_B2 * v_ref[...] + (1.0 - ADAM_B2) * (gv * gv)
        d_ref[...] = -ADAM_LR * ((nm / c1) / (jnp.sqrt(nv / c2) + ADAM_EPS) + ADAM_WD * w_ref[...])
        nm_ref[...] = nm
        nv_ref[...] = nv

    spec = pl.BlockSpec((tr, cols), lambda i: (i, 0))
    return _pcall(
        body, name=name, grid=(rows // tr,), in_specs=[spec] * 4, out_specs=[spec] * 3,
        out_shape=[jax.ShapeDtypeStruct(w.shape, F32)] * 3, compiler_params=_params("parallel"),
    )(w, g, m, v)


def _sum_rows(parts, out_dtype, name):
    rows, cols = parts[0].shape
    tr = rows
    for cand in range(16, rows, 16):
        if rows % cand == 0 and cand * cols * 4 <= (1 << 20):
            tr = cand
    n = len(parts)

    def body(*refs):
        acc = refs[0][...].astype(F32)
        for r in refs[1:n]:
            acc = acc + r[...].astype(F32)
        refs[n][...] = acc.astype(out_dtype)

    spec = pl.BlockSpec((tr, cols), lambda i: (i, 0))
    return _pcall(
        body, name=name, grid=(rows // tr,), in_specs=[spec] * n, out_specs=spec,
        out_shape=jax.ShapeDtypeStruct((rows, cols), out_dtype), compiler_params=_params("parallel"),
    )(*parts)


def _place():
    return lax.axis_index("x"), lax.axis_index("y"), lax.axis_index("c")


def _other_chips(x, y):
    return [(1 - x, y), (x, 1 - y), (1 - x, 1 - y)]


def _remote(src, dst, send_sem, recv_sem, to):
    return pltpu.make_async_remote_copy(src_ref=src, dst_ref=dst, send_sem=send_sem, recv_sem=recv_sem,
                                        device_id=to, device_id_type=MESH)


ANY = pl.BlockSpec(memory_space=pl.ANY)
D2D_CHUNK_BYTES = 512 * 1024
ICI_CHUNK_BYTES = 2 * 1024 * 1024


def _row_chunks(rows, row_bytes, chunk_bytes=D2D_CHUNK_BYTES):
    per = max(16, chunk_bytes // row_bytes // 16 * 16)
    return [(s, min(per, rows - s)) for s in range(0, rows, per)]


def _row_tile(rows, cols, align, limit=1 << 21):
    best = None
    for cand in range(align, rows + 1, align):
        if rows % cand == 0 and cand * cols * 4 <= limit:
            best = cand
    return best or rows


def _allgather_pieces(pieces):
    n = len(pieces)
    halves = [_row_chunks(p.shape[0] // 2, p.shape[1] * p.dtype.itemsize, ICI_CHUNK_BYTES) for p in pieces]
    entries = [(a, q, s, m, j) for a in range(n) for q, (s, m) in enumerate(halves[a]) for j in range(3)]
    slot = {(a, q, j): k for k, (a, q, _, _, j) in enumerate(entries)}
    n_ici = len(entries)

    def body(*refs):
        ins, outs = refs[:n], refs[n:2 * n]
        send_sems, recv_sems = refs[2 * n:]
        x, y, c = _place()
        me = 2 * x + y
        sibling = (x, y, 1 - c)
        chips = _other_chips(x, y)

        def landed(a, s, m, j, core):
            half = ins[a].shape[0] // 2
            return outs[a].at[2 * chips[j][0] + chips[j][1], pl.ds(core * half + s, m)]

        sent = []
        for k, (a, q, s, m, j) in enumerate(entries):
            if j < 2:
                half = ins[a].shape[0] // 2
                cp = _remote(ins[a].at[pl.ds(c * half + s, m)], outs[a].at[me, pl.ds(c * half + s, m)],
                             send_sems.at[k], recv_sems.at[k], (*chips[j], c))
                cp.start()
                sent.append(cp)

        def pass_to_sibling(k, blk):
            fw = _remote(blk, blk, send_sems.at[n_ici + k], recv_sems.at[n_ici + k], sibling)
            fw.start()
            sent.append(fw)

        for k, (a, q, s, m, j) in enumerate(entries):
            if j < 2:
                blk = landed(a, s, m, j, c)
                _remote(blk, blk, send_sems.at[k], recv_sems.at[k], (*chips[j], c)).wait_recv()
                first = q < (len(halves[a]) + 1) // 2
                if (j == 0) == first:
                    on = slot[(a, q, 2)]
                    rl = _remote(blk, blk, send_sems.at[on], recv_sems.at[on], (*chips[1 - j], c))
                    rl.start()
                    sent.append(rl)
                pass_to_sibling(k, blk)
        for k, (a, q, s, m, j) in enumerate(entries):
            if j == 2:
                blk = landed(a, s, m, j, c)
                _remote(blk, blk, send_sems.at[k], recv_sems.at[k], (*chips[j], c)).wait_recv()
                pass_to_sibling(k, blk)
        for k, (a, q, s, m, j) in enumerate(entries):
            blk = landed(a, s, m, j, 1 - c)
            _remote(blk, blk, send_sems.at[n_ici + k], recv_sems.at[n_ici + k], sibling).wait_recv()
        for cp in sent:
            cp.wait_send()

    gathered = _pcall(
        body, name="allgather_weights", in_specs=[ANY] * n, out_specs=[ANY] * n,
        out_shape=[jax.ShapeDtypeStruct((4,) + p.shape, p.dtype) for p in pieces],
        scratch_shapes=[pltpu.SemaphoreType.DMA((2 * n_ici,)), pltpu.SemaphoreType.DMA((2 * n_ici,))],
        compiler_params=pltpu.CompilerParams(has_side_effects=True),
    )(*pieces)
    x, y, _ = _place()
    return [lax.dynamic_update_slice(g, p[None], (2 * x + y, 0, 0)) for g, p in zip(gathered, pieces)]


def _sibling_exchange(grads):
    n = len(grads)
    chunks = [_row_chunks(g.shape[1] // 2, g.shape[2] * g.dtype.itemsize) for g in grads]
    n_sem = 4 * sum(len(ch) for ch in chunks)

    def body(*refs):
        ins, gots = refs[:n], refs[n:2 * n]
        send_sems, recv_sems = refs[2 * n:]
        x, y, c = _place()
        sibling = (x, y, 1 - c)
        work = []
        for a in range(n):
            half = ins[a].shape[1] // 2
            for piece in range(4):
                for s, m in chunks[a]:
                    k = len(work)
                    cp = _remote(ins[a].at[piece, pl.ds((1 - c) * half + s, m)], gots[a].at[piece, pl.ds(s, m)],
                                 send_sems.at[k], recv_sems.at[k], sibling)
                    cp.start()
                    work.append(cp)
        for cp in work:
            cp.wait()

    return _pcall(
        body, name="grad_sibling_exchange", in_specs=[ANY] * n, out_specs=[ANY] * n,
        out_shape=[jax.ShapeDtypeStruct((4, g.shape[1] // 2, g.shape[2]), g.dtype) for g in grads],
        scratch_shapes=[pltpu.SemaphoreType.DMA((n_sem,)), pltpu.SemaphoreType.DMA((n_sem,))],
        compiler_params=pltpu.CompilerParams(has_side_effects=True),
    )(*grads)


def _sibling_gather(fulls):
    n = len(fulls)
    chunks = [_row_chunks(f.shape[0] // 2, f.shape[1] * f.dtype.itemsize) for f in fulls]
    n_sem = sum(len(ch) for ch in chunks)

    def body(*refs):
        outs = refs[n:2 * n]
        send_sems, recv_sems = refs[2 * n:]
        x, y, c = _place()
        sibling = (x, y, 1 - c)
        work = []
        for a in range(n):
            h = outs[a].shape[0] // 2
            for s, m in chunks[a]:
                k = len(work)
                mine = outs[a].at[pl.ds(c * h + s, m)]
                cp = _remote(mine, mine, send_sems.at[k], recv_sems.at[k], sibling)
                cp.start()
                work.append((a, s, m, cp))
        for k, (a, s, m, cp) in enumerate(work):
            h = outs[a].shape[0] // 2
            cp.wait_send()
            theirs = outs[a].at[pl.ds((1 - c) * h + s, m)]
            _remote(theirs, theirs, send_sems.at[k], recv_sems.at[k], sibling).wait_recv()

    return _pcall(
        body, name="grad_sibling_gather", in_specs=[ANY] * n, out_specs=[ANY] * n,
        out_shape=[jax.ShapeDtypeStruct(f.shape, f.dtype) for f in fulls],
        input_output_aliases={a: a for a in range(n)},
        scratch_shapes=[pltpu.SemaphoreType.DMA((n_sem,)), pltpu.SemaphoreType.DMA((n_sem,))],
        compiler_params=pltpu.CompilerParams(has_side_effects=True),
    )(*fulls)


def _pair_sum(grad, got, place, name):
    _, rows, cols = grad.shape
    half = rows // 2
    tr = _row_tile(half, cols, 16)

    def body(p_ref, a_ref, b_ref, o_ref):
        o_ref[...] = (a_ref[...].astype(F32) + b_ref[...].astype(F32)).astype(BF16)

    return _pcall(
        body, name=name,
        grid_spec=pltpu.PrefetchScalarGridSpec(
            num_scalar_prefetch=1, grid=(4, half // tr),
            in_specs=[pl.BlockSpec((None, tr, cols), lambda k, i, p: (k, p[1] * (half // tr) + i, 0)),
                      pl.BlockSpec((None, tr, cols), lambda k, i, p: (k, i, 0))],
            out_specs=pl.BlockSpec((None, tr, cols), lambda k, i, p: (k, i, 0))),
        out_shape=jax.ShapeDtypeStruct((4, half, cols), BF16),
        compiler_params=_params("parallel", "parallel"),
    )(place, grad, got)


def _chip_sum(sums, got, place, name):
    _, h, cols = sums.shape
    tr = _row_tile(h, cols, 16)

    def body(p_ref, own_ref, g0, g1, g2, o_ref):
        o_ref[...] = ((own_ref[...].astype(F32) + g0[...].astype(F32)) + g1[...].astype(F32)) + g2[...].astype(F32)

    gspec = lambda j: pl.BlockSpec((None, tr, cols), lambda i, p: (j, i, 0))
    return _pcall(
        body, name=name,
        grid_spec=pltpu.PrefetchScalarGridSpec(
            num_scalar_prefetch=1, grid=(h // tr,),
            in_specs=[pl.BlockSpec((None, tr, cols), lambda i, p: (p[0], i, 0)), gspec(0), gspec(1), gspec(2)],
            out_specs=pl.BlockSpec((tr, cols), lambda i, p: (p[1] * (h // tr) + i, 0))),
        out_shape=jax.ShapeDtypeStruct((2 * h, cols), F32),
        compiler_params=_params("parallel"),
    )(place, sums, got, got, got)


def _allgather8(buf, name):
    rows = buf.shape[0]

    def body(in_ref, out_ref, send_sems, recv_sems):
        x, y, c = _place()
        me = 4 * x + 2 * y + c
        out_ref[me] = in_ref[...]
        work = []
        for rel in range(1, 8):
            fx, fy, fc = (rel >> 2) & 1, (rel >> 1) & 1, rel & 1
            to = (x ^ fx, y ^ fy, c ^ fc)
            cp = _remote(in_ref, out_ref.at[me], send_sems.at[rel - 1], recv_sems.at[rel - 1], to)
            cp.start()
            work.append((cp, 4 * to[0] + 2 * to[1] + to[2]))
        for rel, (cp, frm) in enumerate(work):
            cp.wait_send()
            blk = out_ref.at[frm]
            _remote(blk, blk, send_sems.at[rel], recv_sems.at[rel], (x, y, c)).wait_recv()

    return _pcall(
        body, name=name, in_specs=[pl.BlockSpec(memory_space=pltpu.VMEM)],
        out_specs=pl.BlockSpec(memory_space=pltpu.VMEM),
        out_shape=jax.ShapeDtypeStruct((8, rows, LANE), F32),
        scratch_shapes=[pltpu.SemaphoreType.DMA((7,)), pltpu.SemaphoreType.DMA((7,))],
        compiler_params=pltpu.CompilerParams(has_side_effects=True),
    )(buf)


def _pack_rows(arrs):
    flats = [a.reshape(-1).astype(F32) for a in arrs]
    starts = np.cumsum([0] + [-(-f.shape[0] // LANE) * LANE for f in flats])
    total = -(-int(starts[-1]) // (8 * LANE)) * 8 * LANE
    flat = sum(jnp.pad(f, (int(s), total - int(s) - f.shape[0])) for f, s in zip(flats, starts))
    return flat.reshape(total // LANE, LANE)


def _unpack_rows(buf, shapes):
    flat = buf.reshape(-1)
    outs, off = [], 0
    for s in shapes:
        n = int(np.prod(s))
        outs.append(flat[off:off + n].reshape(s))
        off += -(-n // LANE) * LANE
    return outs


def _local_grads(x, p, target, wseg, w_br16, w_out16, w_ple16, b_gate, conv_w, conv_b, dt_bias, a_log, d_skip,
                 ssm_norm_w, ln_g, ln_b, rel_bias, finish_dx):
    nb, seq, _ = x.shape
    bmaps = jnp.asarray(_bucket_maps())
    bias = _bias_tables(rel_bias, bmaps)
    bgate8 = jnp.pad(b_gate, ((0, 5), (0, 0)))
    dils = [d for _, d in PATTERNS]

    x16p = _token_orders(x, dils[1:])
    x16 = x16p[0]
    p16 = p.astype(BF16)
    qkv = [_proj(x16p[g], [wseg["qkv%d" % g]], BF16, "proj_qkv%d" % g, True, 2 * MM_TM)[0].reshape(
        nb, dils[g], seq // dils[g], -1) for g in range(3)]
    nat = {}
    for gi, (group, tm) in enumerate(NAT_GROUPS):
        outs = _proj(x16, [wseg[s] for s in group], F32, "proj_nat%d" % gi, True, tm)
        nat.update(zip(group, outs))
    att = [_attn_fwd(qkv[g], bias, g, dils[g], "attn_fwd%d" % g) for g in range(3)]
    oa, o_att, lse = _combine_fwd(att[0][0], att[0][1], att[1:], nat["gatt"])

    conv_wg, conv_bg = _xbc_group_order(conv_w), _xbc_group_order(conv_b)
    act = _conv_fwd(nat["xbc"], conv_wg, conv_bg, "conv_fwd")
    dt_bias_row = jnp.pad(dt_bias, ((0, 0), (0, LANE - SSM_HEADS)))
    alog_g, dskip_g = _group_lanes(a_log), _group_lanes(d_skip)
    y_ssm, y_all, sprev = _ssd_fwd(act, nat["dt"], dt_bias_row, nat["z"], alog_g, dskip_g, ssm_norm_w)

    w_bra, w_brb = w_br16[:ATT_OUT], w_br16[ATT_OUT:]
    y_a, y_b, merged = _merge_fwd(oa, y_ssm, w_bra, w_brb, nat["gm"], bgate8)

    dx, dpre16, dpw16, dgp16, ln_sums = _ln_loss(x, merged, w_out16, nat["gp"], p16, w_ple16, target, bgate8,
                                                 ln_g, ln_b)
    loss_sum = (0.5 / D_MODEL) * jnp.sum(ln_sums[3])
    dya16, dyb16, dgm16, mg_sums = _merge_bwd(dpre16, w_out16, y_a, y_b, nat["gm"], bgate8)
    dys = _dx([dyb16], [w_brb], [], "dx_yssm")
    g_w_out, = _dw(merged, [dpre16], BF16, "dw_out")
    g_w_br = jnp.concatenate([_dw(oa, [dya16], BF16, "dw_bra")[0], _dw(y_ssm, [dyb16], BF16, "dw_brb")[0]], axis=0)
    g_w_ple, = _dw(p16, [dpw16], BF16, "dw_ple")

    do_att, dgatt16, own_order = _combine_bwd(dya16, w_bra, nat["gatt"], o_att, lse, dils[1:])
    dseg = {"gatt": dgatt16, "gm": dgm16, "gp": dgp16}
    dbias = []
    for g in range(3):
        cotangent = (do_att, o_att, lse) if g == 0 else (own_order[2 * g - 2], own_order[2 * g - 1])
        dqkv, db = _attn_bwd(qkv[g], bias, g, cotangent, dils[g],
                             "attn_bwd%d" % g)
        dseg["qkv%d" % g] = dqkv.reshape(nb, seq, -1)
        dbias.append(db)
    g_rel = _bias_grad(jnp.concatenate(dbias, axis=0), bmaps)[:, 0, :NUM_BUCKETS].T

    dact, ddtg, dz, ssd_small, g_normw = _ssd_bwd(
        act, nat["dt"], dt_bias_row, nat["z"], y_all, dys, sprev, alog_g, dskip_g, ssm_norm_w)
    dseg["z"] = dz
    dseg["dt"] = jnp.pad(_ungroup_lanes(ddtg), ((0, 0), (0, 0), (0, LANE - SSM_HEADS)))
    dpre, conv_sums = _conv_bwd_pre(dact, nat["xbc"], conv_wg, conv_bg, "conv_bwd")
    dseg["xbc"] = _conv_bwd_x(dpre, conv_wg, "conv_bwd_x")
    csum = _xbc_reference_order(conv_sums)

    dx_own = [_dx([dseg["qkv%d" % g]], [wseg["qkv%d" % g]], [], "dx_qkv%d" % g, True).reshape(
        nb, dils[g], seq // dils[g], D_MODEL) for g in (1, 2)]
    dwseg = {"qkv%d" % g: _dw(x16p[g], [dseg["qkv%d" % g]], BF16, "dw_qkv%d" % g, True)[0] for g in range(3)}
    for gi, group in enumerate(DW_GROUPS):
        dwseg.update(zip(group, _dw(x16, [dseg[s] for s in group], BF16, "dw_nat%d" % gi, True)))
    names = ["qkv0"] + [s for group, _ in NAT_GROUPS for s in group]
    dx = finish_dx([dseg[s] for s in names], [wseg[s] for s in names], [dx], dx_own, dwseg, g_w_br, g_w_out, g_w_ple)

    small = dict(
        b_gate=jnp.stack([mg_sums[0], mg_sums[1], ln_sums[2]]),
        conv_w=csum[0:4], conv_b=csum[4:5],
        dt_bias=_ungroup_lanes(ssd_small[:, 2:3, :]), a_log=_ungroup_lanes(ssd_small[:, 0:1, :]),
        d_skip=_ungroup_lanes(ssd_small[:, 1:2, :]), ssm_norm_w=g_normw,
        ln_g=ln_sums[0:1], ln_b=ln_sums[1:2], rel_bias=g_rel)
    return loss_sum, dx, small


DX_TM = 256
SMALL_ORDER = ("b_gate", "conv_w", "conv_b", "dt_bias", "a_log", "d_skip", "ssm_norm_w", "ln_g", "ln_b", "rel_bias")
SMALL_FULL_SHAPES = dict(b_gate=(3, 1024), conv_w=(4, 3072), conv_b=(1, 3072), dt_bias=(1, 32), a_log=(1, 32),
                         d_skip=(1, 32), ssm_norm_w=(1, 2048), ln_g=(1, 1024), ln_b=(1, 1024), rel_bias=(32, 36))


def kernel(x, p, w_in, b_gate, conv_w, conv_b, dt_bias, a_log, d_skip, ssm_norm_w, w_branch, w_out, w_ple, ln_g, ln_b, rel_bias, loss_target, m_w_in, m_b_gate, m_conv_w, m_conv_b, m_dt_bias, m_a_log, m_d_skip, m_ssm_norm_w, m_w_branch, m_w_out, m_w_ple, m_ln_g, m_ln_b, m_rel_bias, v_w_in, v_b_gate, v_conv_w, v_conv_b, v_dt_bias, v_a_log, v_d_skip, v_ssm_norm_w, v_w_branch, v_w_out, v_w_ple, v_ln_g, v_ln_b, v_rel_bias):
    cx, cy, cc = _place()
    chip = 2 * cx + cy
    dev = 4 * cx + 2 * cy + cc

    w_in_t = jnp.transpose(w_in[0])
    win16 = _shard_to_window(w_in_t, chip)
    g_win, g_br, g_out, g_ple = _allgather_pieces(
        [win16, w_branch[0].astype(BF16), w_out[0].astype(BF16), w_ple[0].astype(BF16)])
    wseg = _assemble(g_win)
    w_br16 = g_br.reshape(4 * 704, D_MODEL)
    w_out16 = g_out.reshape(D_MODEL, D_MODEL)
    w_ple16 = jnp.transpose(g_ple, (1, 0, 2)).reshape(PLE_DIM, D_MODEL)
    shards = _allgather8(_pack_rows([b_gate[0], conv_w[0]]), "allgather_small_params")
    per_chip = [_unpack_rows(shards[2 * k], [(3, 256), (4, 768)]) for k in range(4)]
    b_gate_full = _join_last([pc[0] for pc in per_chip])
    conv_w_full = _join_last([pc[1] for pc in per_chip])

    place = jnp.stack([chip, cc]).astype(jnp.int32)
    reduced = []

    def finish_dx(dhs, ws, accs, own_order_accs, dwseg, d_br, d_out, d_ple):
        grads = [_pack(dwseg), d_br.reshape(4, 704, D_MODEL), d_out.reshape(4, 256, D_MODEL),
                 jnp.transpose(d_ple.reshape(PLE_DIM, 4, 256), (1, 0, 2))]
        got = _sibling_exchange(grads)
        chip_sums = [_pair_sum(g, t, place, "grad_pair_sum_%d" % i) for i, (g, t) in enumerate(zip(grads, got))]
        dx, others = _dx(dhs, ws, accs, "dx_w_in_and_grad_chip_scatter", True, DX_TM, chip_sums, own_order_accs)
        fulls = [_chip_sum(s, t, place, "grad_chip_sum_%d" % i) for i, (s, t) in enumerate(zip(chip_sums, others))]
        reduced.extend(_sibling_gather(fulls))
        return dx

    loss_sum, grad_x, small = _local_grads(
        x, p[0], loss_target, wseg, w_br16, w_out16, w_ple16, b_gate_full, conv_w_full, conv_b, dt_bias, a_log,
        d_skip, ssm_norm_w, ln_g, ln_b, rel_bias, finish_dx)
    big = reduced
    g_w_in = _window_to_shard(big[0], chip)
    g_w_branch, g_w_out, g_w_ple = big[1], big[2], big[3]
    parts = _allgather8(_pack_rows([small[n] for n in SMALL_ORDER] + [loss_sum.reshape(1, 1)]),
                        "allgather_small_grads")
    small_sum = _sum_rows([parts[i] for i in range(8)], F32, "small_grad_sum")
    *reduced_small, loss = _unpack_rows(small_sum, [SMALL_FULL_SHAPES[n] for n in SMALL_ORDER] + [(1, 1)])
    loss = loss.reshape(())
    sg = dict(zip(SMALL_ORDER, reduced_small))
    sg["b_gate"] = lax.dynamic_slice_in_dim(sg["b_gate"], chip * 256, 256, axis=1)
    sg["conv_w"] = lax.dynamic_slice_in_dim(sg["conv_w"], chip * 768, 768, axis=1)
    del dev

    upd = {}
    upd["w_in"] = [jnp.transpose(t) for t in _adamw(w_in_t, g_w_in, jnp.transpose(m_w_in[0]),
                                                      jnp.transpose(v_w_in[0]), "adamw_w_in")]
    upd["w_branch"] = _adamw(w_branch[0], g_w_branch, m_w_branch[0], v_w_branch[0], "adamw_w_branch")
    upd["w_out"] = _adamw(w_out[0], g_w_out, m_w_out[0], v_w_out[0], "adamw_w_out")
    upd["w_ple"] = _adamw(w_ple[0], g_w_ple, m_w_ple[0], v_w_ple[0], "adamw_w_ple")
    small_w = dict(b_gate=b_gate, conv_w=conv_w, conv_b=conv_b, dt_bias=dt_bias, a_log=a_log, d_skip=d_skip,
                   ssm_norm_w=ssm_norm_w, ln_g=ln_g, ln_b=ln_b, rel_bias=rel_bias)
    small_m = dict(b_gate=m_b_gate, conv_w=m_conv_w, conv_b=m_conv_b, dt_bias=m_dt_bias, a_log=m_a_log,
                   d_skip=m_d_skip, ssm_norm_w=m_ssm_norm_w, ln_g=m_ln_g, ln_b=m_ln_b, rel_bias=m_rel_bias)
    small_v = dict(b_gate=v_b_gate, conv_w=v_conv_w, conv_b=v_conv_b, dt_bias=v_dt_bias, a_log=v_a_log,
                   d_skip=v_d_skip, ssm_norm_w=v_ssm_norm_w, ln_g=v_ln_g, ln_b=v_ln_b, rel_bias=v_rel_bias)
    shapes = [small_w[n].shape for n in SMALL_ORDER]
    s_delta, s_m, s_v = _adamw(_pack_rows([small_w[n] for n in SMALL_ORDER]), _pack_rows([sg[n] for n in SMALL_ORDER]),
                               _pack_rows([small_m[n] for n in SMALL_ORDER]), _pack_rows([small_v[n] for n in SMALL_ORDER]),
                               "adamw_small")
    for i, n in enumerate(SMALL_ORDER):
        upd[n] = tuple(_unpack_rows(t, shapes)[i] for t in (s_delta, s_m, s_v))
        sg[n] = sg[n].reshape(small_w[n].shape)

    order = ("w_in", "b_gate", "conv_w", "conv_b", "dt_bias", "a_log", "d_skip", "ssm_norm_w", "w_branch", "w_out",
             "w_ple", "ln_g", "ln_b", "rel_bias")
    grads = dict(sg, w_in=jnp.transpose(g_w_in)[None],w_branch=g_w_branch[None], w_out=g_w_out[None], w_ple=g_w_ple[None])
    lead = lambda n, t: t[None] if n in ("w_in", "w_branch", "w_out", "w_ple") else t
    return (loss, grad_x, *[grads[n] for n in order], *[lead(n, upd[n][0]) for n in order],
            *[lead(n, upd[n][1]) for n in order], *[lead(n, upd[n][2]) for n in order])
```

```python
import math

import numpy as np
import jax
import jax.numpy as jnp
from jax import lax
from jax.experimental import pallas as pl
from jax.experimental.pallas import tpu as pltpu

F32, BF16 = jnp.float32, jnp.bfloat16

D_MODEL = 1024
HEAD_DIM = 64
GROUP_HEADS = 12
ATT_OUT = GROUP_HEADS * HEAD_DIM
PATTERNS = ((128, 1), (512, 4), (2048, 16))
BAND = 128
NUM_BUCKETS = 32
MAX_DISTANCE = 2048
D_INNER = 2048
SSM_HEADS = 32
SSM_GROUPS = 4
GROUP_SSM_HEADS = SSM_HEADS // SSM_GROUPS
D_STATE = 128
CHUNK = 128
PLE_DIM = 256
ALPHA = 2.0 ** 0.25
LN_EPS = 1e-5
RMS_EPS = 1e-5
ADAM_LR, ADAM_B1, ADAM_B2, ADAM_EPS, ADAM_WD, ADAM_STEP = 0.001, 0.9, 0.999, 1e-08, 0.01, 10
NEG = -1e30

QKV_W = 3 * ATT_OUT
IN_COLS = 15904
SHARD_COLS = IN_COLS // 4
DT_COL = 12800
ROW_TILE = 16
WIN_ROWS = 4000


def _win_offset(k):
    return (k * SHARD_COLS) % ROW_TILE


def _win_start(k):
    return k * SHARD_COLS - _win_offset(k)

VMEM_LIMIT_BYTES = 56 * 1024 * 1024
LANE = 128
MESH = pl.DeviceIdType.MESH
NT = (((1,), (1,)), ((), ()))
TN = (((0,), (0,)), ((), ()))


def _pcall(body, **kw):
    return pl.pallas_call(body, **kw)


def _params(*sem):
    return pltpu.CompilerParams(dimension_semantics=sem, vmem_limit_bytes=VMEM_LIMIT_BYTES)


def _sigmoid(v):
    return jax.nn.sigmoid(v)


MM_TM = 512


def _tok_spec(tm, width):
    return pl.BlockSpec((None, tm, width), lambda b, i: (b, i, 0))


def _whole(arr, single_buffer=False):
    mode = dict(pipeline_mode=pl.Buffered(1)) if single_buffer else {}
    return pl.BlockSpec(arr.shape, lambda b, i: (0,) * arr.ndim, **mode)


def _proj(a3, ws, out_dtype, name, w_rows_are_outputs=False, tm=MM_TM):
    nb, seq, kdim = a3.shape
    nw = len(ws)
    widths = [w.shape[0] if w_rows_are_outputs else w.shape[1] for w in ws]

    def body(*refs):
        a = refs[0][...].astype(BF16)
        for w_ref, o_ref in zip(refs[1:1 + nw], refs[1 + nw:]):
            if w_rows_are_outputs:
                v = lax.dot_general(a, w_ref[...], NT, preferred_element_type=F32)
            else:
                v = jnp.dot(a, w_ref[...], preferred_element_type=F32)
            o_ref[...] = v.astype(out_dtype)

    return _pcall(
        body, name=name, grid=(nb, seq // tm),
        in_specs=[_tok_spec(tm, kdim)] + [_whole(w, True) for w in ws],
        out_specs=[_tok_spec(tm, n) for n in widths],
        out_shape=[jax.ShapeDtypeStruct((nb, seq, n), out_dtype) for n in widths],
        compiler_params=_params("parallel", "parallel"),
    )(a3, *ws)


def _dx(dhs, ws, accs, name, w_rows_are_outputs=False, tm=MM_TM, scatter=None, own_order_accs=()):
    nb, seq, _ = dhs[0].shape
    nd, nacc, npa = len(dhs), len(accs), len(own_order_accs)
    kout = ws[0].shape[1] if w_rows_are_outputs else ws[0].shape[0]
    sums = scatter or []
    ns = len(sums)
    chunks = [_row_chunks(s.shape[1], s.shape[2] * s.dtype.itemsize, ICI_CHUNK_BYTES) for s in sums]
    n_sem = 3 * sum(len(ch) for ch in chunks)
    grid = (nb, seq // tm)
    ntile = kout // LANE if npa else 0

    def body(*refs):
        n_in = 2 * nd + nacc + npa
        sum_refs, o_ref, got_refs = refs[n_in:n_in + ns], refs[n_in + ns], refs[n_in + ns + 1:n_in + 2 * ns + 1]
        tile_refs = refs[n_in + 2 * ns + 1:n_in + 2 * ns + 1 + ntile]

        def copies():
            send_sems, recv_sems = refs[-2], refs[-1]
            x, y, c = _place()
            out = []
            for a in range(ns):
                for s, m in chunks[a]:
                    for j, (cx, cy) in enumerate(_other_chips(x, y)):
                        k = len(out)
                        out.append(_remote(sum_refs[a].at[2 * cx + cy, pl.ds(s, m)], got_refs[a].at[j, pl.ds(s, m)],
                                           send_sems.at[k], recv_sems.at[k], (cx, cy, c)))
            return out

        if ns:
            @pl.when((pl.program_id(0) == 0) & (pl.program_id(1) == 0))
            def _():
                for cp in copies():
                    cp.start()

        v = None
        for dh_ref, w_ref in zip(refs[:nd], refs[nd:2 * nd]):
            dh = dh_ref[...].astype(BF16)
            if w_rows_are_outputs:
                t = jnp.dot(dh, w_ref[...], preferred_element_type=F32)
            else:
                t = lax.dot_general(dh, w_ref[...], NT, preferred_element_type=F32)
            v = t if v is None else v + t
        for a_ref in refs[2 * nd:2 * nd + nacc]:
            v = v + a_ref[...]
        for p_ref in refs[2 * nd + nacc:n_in]:
            v = v + _natural_rows(p_ref, tile_refs)
        o_ref[...] = v

        if ns:
            @pl.when((pl.program_id(0) == grid[0] - 1) & (pl.program_id(1) == grid[1] - 1))
            def _():
                for cp in copies():
                    cp.wait()

    out = _pcall(
        body, name=name, grid=grid,
        in_specs=[_tok_spec(tm, dh.shape[-1]) for dh in dhs] + [_whole(w, True) for w in ws]
        + [_tok_spec(tm, kout)] * nacc
        + [pl.BlockSpec((None, p.shape[1], tm // p.shape[1], kout), lambda b, i: (b, 0, i, 0)) for p in own_order_accs]
        + [ANY] * ns,
        out_specs=[_tok_spec(tm, kout)] + [ANY] * ns,
        out_shape=[jax.ShapeDtypeStruct((nb, seq, kout), F32)]
        + [jax.ShapeDtypeStruct((3,) + s.shape[1:], s.dtype) for s in sums],
        input_output_aliases={2 * nd: 0} if nacc else {},
        scratch_shapes=[pltpu.VMEM((tm, LANE), F32)] * ntile
        + ([pltpu.SemaphoreType.DMA((n_sem,)), pltpu.SemaphoreType.DMA((n_sem,))] if ns else []),
        compiler_params=pltpu.CompilerParams(
            dimension_semantics=("arbitrary", "arbitrary") if ns else ("parallel", "parallel"),
            vmem_limit_bytes=VMEM_LIMIT_BYTES, has_side_effects=bool(ns)),
    )(*dhs, *ws, *accs, *own_order_accs, *sums)
    return (out[0], list(out[1:])) if ns else out[0]


def _dw(a3, dhs, out_dtype, name, rows_are_outputs=False):
    nb, seq, kdim = a3.shape
    nd = len(dhs)
    grid = (nb, seq // MM_TM)
    shapes = [(dh.shape[-1], kdim) if rows_are_outputs else (kdim, dh.shape[-1]) for dh in dhs]

    def body(*refs):
        b, i = pl.program_id(0), pl.program_id(1)
        dh_refs, o_refs, acc_refs = refs[1:1 + nd], refs[1 + nd:1 + 2 * nd], refs[1 + 2 * nd:]

        @pl.when((b == 0) & (i == 0))
        def _():
            for acc_ref in acc_refs:
                acc_ref[...] = jnp.zeros_like(acc_ref)

        a = refs[0][...].astype(BF16)
        for dh_ref, acc_ref in zip(dh_refs, acc_refs):
            dh = dh_ref[...].astype(BF16)
            acc_ref[...] += lax.dot_general(*((dh, a) if rows_are_outputs else (a, dh)), TN,
                                            preferred_element_type=F32)

        @pl.when((b == grid[0] - 1) & (i == grid[1] - 1))
        def _():
            for o_ref, acc_ref in zip(o_refs, acc_refs):
                o_ref[...] = acc_ref[...].astype(out_dtype)

    return _pcall(
        body, name=name, grid=grid,
        in_specs=[_tok_spec(MM_TM, kdim)] + [_tok_spec(MM_TM, dh.shape[-1]) for dh in dhs],
        out_specs=[pl.BlockSpec(s, lambda b, i: (0, 0)) for s in shapes],
        out_shape=[jax.ShapeDtypeStruct(s, out_dtype) for s in shapes],
        scratch_shapes=[pltpu.VMEM(s, F32) for s in shapes],
        compiler_params=_params("arbitrary", "arbitrary"),
    )(a3, *dhs)


def _qkv_rows(g):
    return [(part * QKV_W + g * ATT_OUT + hp * LANE, LANE) for hp in range(ATT_OUT // LANE) for part in range(3)]


XBC_START = 3 * QKV_W + ATT_OUT + D_INNER
GROUP_CH = GROUP_SSM_HEADS * HEAD_DIM
XBC_GROUP = GROUP_CH + 2 * D_STATE
CONV_DIM = SSM_GROUPS * XBC_GROUP


def _xbc_ranges():
    out = []
    for g in range(SSM_GROUPS):
        out += [(g * GROUP_CH, GROUP_CH), (D_INNER + g * D_STATE, D_STATE),
                (D_INNER + SSM_GROUPS * D_STATE + g * D_STATE, D_STATE)]
    return out


def _join_last(parts):
    widths = [t.shape[-1] for t in parts]
    total, lead = sum(widths), [(0, 0)] * (parts[0].ndim - 1)
    starts = np.cumsum([0] + widths)
    return sum(jnp.pad(t, lead + [(int(s), total - int(s) - w)]) for t, s, w in zip(parts, starts, widths))


def _xbc_group_order(t):
    return _join_last([t[..., s:s + n] for s, n in _xbc_ranges()])


def _xbc_reference_order(t):
    g = lambda off, n: [t[..., k * XBC_GROUP + off:k * XBC_GROUP + off + n] for k in range(SSM_GROUPS)]
    return _join_last(g(0, GROUP_CH) + g(GROUP_CH, D_STATE) + g(GROUP_CH + D_STATE, D_STATE))


def _segments():
    one = lambda name, start, rows: (name, [(start, rows)], max(rows, LANE))
    return [("qkv%d" % g, _qkv_rows(g), QKV_W) for g in range(3)] + [
        one("gatt", 3 * QKV_W, ATT_OUT), one("z", 3 * QKV_W + ATT_OUT, D_INNER),
        ("xbc", [(XBC_START + s, n) for s, n in _xbc_ranges()], CONV_DIM), one("dt", DT_COL, SSM_HEADS),
        one("gm", DT_COL + SSM_HEADS, 2 * D_MODEL), one("gp", DT_COL + SSM_HEADS + 2 * D_MODEL, D_MODEL)]


LAYOUT_TC = 256
NAT_GROUPS = ((("gatt", "z", "dt", "gp"), 512), (("xbc", "gm"), 512))
DW_GROUPS = (("gatt", "z", "dt", "gp"), ("xbc",), ("gm",))


def _assemble(win):
    segs = _segments()

    def body(win_ref, *outs):
        def pieces(start, rows):
            t, end = start, start + rows
            while t < end:
                k = min(t // SHARD_COLS, 3)
                shard_end = (k + 1) * SHARD_COLS
                if k < 3 and shard_end % ROW_TILE and t == shard_end - shard_end % ROW_TILE:
                    lo = t - _win_start(k)
                    yield win_ref[k, lo:lo + ROW_TILE, :] + win_ref[k + 1, 0:ROW_TILE, :]
                    t += ROW_TILE
                    continue
                upto = min(end, shard_end - shard_end % ROW_TILE if k < 3 else end)
                yield win_ref[k, t - _win_start(k):upto - _win_start(k), :]
                t = upto

        for (_, ranges, total), o_ref in zip(segs, outs):
            off = 0
            for start, rows in ranges:
                for part in pieces(start, rows):
                    o_ref[off:off + part.shape[0], :] = part
                    off += part.shape[0]
            if off < total:
                o_ref[off:total, :] = jnp.zeros((total - off, o_ref.shape[1]), BF16)

    outs = _pcall(
        body, name="assemble_w_in", grid=(D_MODEL // LAYOUT_TC,),
        in_specs=[pl.BlockSpec((4, WIN_ROWS, LAYOUT_TC), lambda i: (0, 0, i))],
        out_specs=[pl.BlockSpec((total, LAYOUT_TC), lambda i: (0, i)) for _, _, total in segs],
        out_shape=[jax.ShapeDtypeStruct((total, D_MODEL), BF16) for _, _, total in segs],
        compiler_params=_params("parallel"),
    )(win)
    return {name: o for (name, _, _), o in zip(segs, outs)}


def _pack(dsegs):
    segs = _segments()

    def body(*refs):
        ins, o_ref = refs[:-1], refs[-1]
        tail = IN_COLS - _win_start(3)
        o_ref[3, tail:, :] = jnp.zeros((WIN_ROWS - tail, o_ref.shape[2]), BF16)
        for (_, ranges, _), s_ref in zip(segs, ins):
            off = 0
            for start, rows in ranges:
                for k in range(4):
                    lo = _win_start(k)
                    a, b = max(start, lo), min(start + rows, lo + WIN_ROWS)
                    if a < b:
                        o_ref[k, a - lo:b - lo, :] = s_ref[off + a - start:off + b - start, :]
                off += rows

    return _pcall(
        body, name="pack_dw_in", grid=(D_MODEL // LAYOUT_TC,),
        in_specs=[pl.BlockSpec((total, LAYOUT_TC), lambda i: (0, i)) for _, _, total in segs],
        out_specs=pl.BlockSpec((4, WIN_ROWS, LAYOUT_TC), lambda i: (0, 0, i)),
        out_shape=jax.ShapeDtypeStruct((4, WIN_ROWS, D_MODEL), BF16),
        compiler_params=_params("parallel"),
    )(*[dsegs[name] for name, _, _ in segs])


def _shard_to_window(shard_t, k):
    def at(off):
        return lambda w: jnp.pad(w.astype(BF16), ((off, WIN_ROWS - SHARD_COLS - off), (0, 0)))

    return lax.cond(k % 2 == 1, at(_win_offset(1)), at(_win_offset(0)), shard_t)


def _window_to_shard(win, k):
    return lax.dynamic_slice(win, ((k % 2) * _win_offset(1), 0), (SHARD_COLS, D_MODEL))


def _bucket_maps():
    qi = np.arange(8)[:, None]
    kj = np.arange(2 * BAND)[None, :]
    delta = qi + BAND - kj
    maps = []
    for window, dil in PATTERNS:
        valid = (delta >= 0) & (delta <= window // dil)
        dist = np.maximum(delta, 0) * dil
        max_exact = NUM_BUCKETS // 2
        d_f = np.maximum(dist, 1).astype(np.float32)
        large = max_exact + (np.log(d_f / np.float32(max_exact)) / np.float32(math.log(MAX_DISTANCE / max_exact))
                             * np.float32(NUM_BUCKETS - max_exact)).astype(np.int32)
        large = np.minimum(large, NUM_BUCKETS - 1)
        bucket = np.where(dist < max_exact, dist, large)
        maps.append(np.where(valid, bucket, -1).astype(np.int32))
    return np.stack(maps)


def _bias_tables(rel_bias, bmaps):
    def body(rb_ref, bm_ref, o_ref):
        g = pl.program_id(0)
        bm = bm_ref[...]
        for hh in range(GROUP_HEADS):
            acc = jnp.full(bm.shape, NEG, F32)
            for b in range(NUM_BUCKETS):
                acc = jnp.where(bm == b, rb_ref[b, g * GROUP_HEADS + hh], acc)
            for a in range(BAND // 8):
                o_ref[hh, 8 * a:8 * a + 8, :] = acc if a == 0 else pltpu.roll(acc, 8 * a, 1)

    return _pcall(
        body, name="bias_tables", grid=(3,),
        in_specs=[pl.BlockSpec(memory_space=pltpu.SMEM),
                  pl.BlockSpec((None, 8, 2 * BAND), lambda g: (g, 0, 0))],
        out_specs=pl.BlockSpec((GROUP_HEADS, BAND, 2 * BAND), lambda g: (g, 0, 0)),
        out_shape=jax.ShapeDtypeStruct((3 * GROUP_HEADS, BAND, 2 * BAND), F32),
        compiler_params=_params("parallel"),
    )(rel_bias, bmaps)


def _bias_grad(dbias, bmaps):
    def body(db_ref, bm_ref, o_ref):
        bm = bm_ref[...]
        lane = lax.broadcasted_iota(jnp.int32, (1, LANE), 1)
        for hh in range(GROUP_HEADS):
            db = db_ref[hh, 0:8, :]
            for a in range(1, BAND // 8):
                db = db + pltpu.roll(db_ref[hh, 8 * a:8 * a + 8, :], 2 * BAND - 8 * a, 1)
            vec = jnp.zeros((1, LANE), F32)
            for b in range(NUM_BUCKETS):
                s = jnp.sum(jnp.where(bm == b, db, 0.0), keepdims=True)
                vec = jnp.where(lane == b, s, vec)
            o_ref[hh] = vec

    return _pcall(
        body, name="bias_grad", grid=(3,),
        in_specs=[pl.BlockSpec((GROUP_HEADS, BAND, 2 * BAND), lambda g: (g, 0, 0)),
                  pl.BlockSpec((None, 8, 2 * BAND), lambda g: (g, 0, 0))],
        out_specs=pl.BlockSpec((GROUP_HEADS, 1, LANE), lambda g: (g, 0, 0)),
        out_shape=jax.ShapeDtypeStruct((3 * GROUP_HEADS, 1, LANE), F32),
        compiler_params=_params("parallel"),
    )(dbias, bmaps)


def _rows(n):
    if isinstance(n, int):
        return pl.ds(n * BAND, BAND)
    return pl.ds(pl.multiple_of(n * BAND, BAND), BAND)


def _for_blocks(blocks, nblk, per, carry):
    carry = blocks([0], carry, False)
    start = 1 + (nblk - 1) % per
    for n in range(1, start):
        carry = blocks([n], carry, True)
    trips = (nblk - start) // per
    if trips > 0:
        carry = lax.fori_loop(
            0, trips, lambda t, c: blocks([start + t * per + u for u in range(per)], c, True), carry)
    return carry


def _pairs_per_step(d):
    return {1: 3, 4: 6, 16: 6}[d]


def _bias_spec(group, hps):
    first = group * GROUP_HEADS // (2 * hps)
    return pl.BlockSpec((2 * hps, BAND, 2 * BAND), lambda hp, b, r: (first + hp, 0, 0))


def _attn_fwd(qkv4, bias, group, d, name):
    nb, _, sub, _ = qkv4.shape
    nblk = sub // BAND
    scale = HEAD_DIM ** -0.5
    npair = ATT_OUT // LANE
    hps = _pairs_per_step(d)
    compact = d > 1

    def body(qkv_ref, bias_ref, o_ref, l_ref):
        def blocks(ns, carry, with_prev):
            chains = [(bi, i, h) for bi in range(len(ns)) for i in range(hps) for h in range(2)]
            first_head = lax.broadcasted_iota(jnp.int32, (BAND, LANE), 1) < HEAD_DIM
            pair = lambda n, i, part: qkv_ref[_rows(n), (3 * i + part) * LANE:(3 * i + part + 1) * LANE]
            scores = []
            for bi, i, h in chains:
                n = ns[bi]
                qp = pair(n, i, 0) * scale
                q = jnp.where(first_head if h == 0 else jnp.logical_not(first_head), qp, jnp.zeros_like(qp))
                s_c = lax.dot_general(q, pair(n, i, 1), NT, preferred_element_type=F32) + bias_ref[2 * i + h, :, BAND:]
                s_p = None
                if with_prev:
                    s_p = lax.dot_general(q, pair(n - 1, i, 1), NT,
                                          preferred_element_type=F32) + bias_ref[2 * i + h, :, :BAND]
                scores.append((s_c, s_p))
            probs = []
            for s_c, s_p in scores:
                m = jnp.max(s_c, -1, keepdims=True)
                if with_prev:
                    m = jnp.maximum(m, jnp.max(s_p, -1, keepdims=True))
                e_c = jnp.exp(s_c - m)
                den = jnp.sum(e_c, -1, keepdims=True)
                e_p = None
                if with_prev:
                    e_p = jnp.exp(s_p - m)
                    den = den + jnp.sum(e_p, -1, keepdims=True)
                    e_p = e_p.astype(BF16)
                probs.append((e_c.astype(BF16), e_p, den, m))
            outs = {}
            for (bi, i, h), (e_c, e_p, den, m) in zip(chains, probs):
                n = ns[bi]
                acc = jnp.dot(e_c, pair(n, i, 2), preferred_element_type=F32)
                if with_prev:
                    acc = acc + jnp.dot(e_p, pair(n - 1, i, 2), preferred_element_type=F32)
                outs[(bi, i, h)] = (acc / den, m + jnp.log(den))
            lane = lax.broadcasted_iota(jnp.int32, (BAND, LANE), 1)
            for bi, n in enumerate(ns):
                per_head = jnp.zeros((BAND, LANE), F32)
                for i in range(hps):
                    o_ref[_rows(n), i * LANE:(i + 1) * LANE] = jnp.where(first_head, outs[(bi, i, 0)][0],
                                                                         outs[(bi, i, 1)][0])
                    if compact:
                        for h in range(2):
                            per_head = jnp.where(lane == 2 * i + h, outs[(bi, i, h)][1], per_head)
                    else:
                        l_ref[_rows(n), i * LANE:(i + 1) * LANE] = jnp.where(first_head, outs[(bi, i, 0)][1],
                                                                             outs[(bi, i, 1)][1])
                if compact:
                    l_ref[_rows(n), :] = per_head
            return carry

        _for_blocks(blocks, nblk, 2 if hps == 1 else 1, 0)

    in_specs = [pl.BlockSpec((None, None, sub, 3 * LANE * hps), lambda hp, b, r: (b, r, 0, hp)),
                _bias_spec(group, hps)]
    if compact:
        return _pcall(
            body, name=name, grid=(1, nb, d), in_specs=in_specs,
            out_specs=[pl.BlockSpec((None, None, sub, ATT_OUT), lambda hp, b, r: (b, r, 0, 0)),
                       pl.BlockSpec((None, None, sub, LANE), lambda hp, b, r: (b, r, 0, 0))],
            out_shape=[jax.ShapeDtypeStruct((nb, d, sub, ATT_OUT), F32), jax.ShapeDtypeStruct((nb, d, sub, LANE), F32)],
            compiler_params=_params("parallel", "parallel", "parallel"),
        )(qkv4, bias)
    ospec = pl.BlockSpec((None, sub, hps * LANE), lambda hp, b, r: (b, 0, r * (npair // hps) + hp))
    return _pcall(
        body, name=name, grid=(npair // hps, nb, d), in_specs=in_specs, out_specs=[ospec, ospec],
        out_shape=[jax.ShapeDtypeStruct((nb, sub, d * ATT_OUT), F32)] * 2,
        compiler_params=_params("parallel", "parallel", "parallel"),
    )(qkv4, bias)


STAT_LSE_LANE = 16


def _attn_bwd(qkv4, bias, group, cotangent, d, name):
    nb, _, sub, _ = qkv4.shape
    nblk = sub // BAND
    scale = HEAD_DIM ** -0.5
    npair = ATT_OUT // LANE
    hps = _pairs_per_step(d)
    compact = d > 1

    def body(qkv_ref, bias_ref, *rest):
        do_ref, dqkv_ref, db_ref = rest[0], rest[-2], rest[-1]
        b, r = pl.program_id(1), pl.program_id(2)

        @pl.when((b == 0) & (r == 0))
        def _():
            db_ref[...] = jnp.zeros_like(db_ref)

        def blocks(ns, carry, with_prev):
            sides = (0, 1) if with_prev else (0,)
            chains = [(bi, i, h, sd) for bi in range(len(ns)) for i in range(hps) for h in range(2) for sd in sides]
            first_head = lax.broadcasted_iota(jnp.int32, (BAND, LANE), 1) < HEAD_DIM
            own = lambda h, t: jnp.where(first_head if h == 0 else jnp.logical_not(first_head), t, jnp.zeros_like(t))
            pair = lambda rows, i, part: qkv_ref[rows, (3 * i + part) * LANE:(3 * i + part + 1) * LANE]
            key_rows = lambda bi, sd: _rows(ns[bi] - sd)
            qs = {}
            for bi in range(len(ns)):
                for i in range(hps):
                    q_pair = pair(_rows(ns[bi]), i, 0) * scale
                    do = do_ref[_rows(ns[bi]), i * LANE:(i + 1) * LANE]
                    do16 = do.astype(BF16)
                    for h in range(2):
                        if compact:
                            st_ref, head = rest[1], 2 * i + h
                            ebar = st_ref[_rows(ns[bi]), head:head + 1]
                            lcol = st_ref[_rows(ns[bi]), STAT_LSE_LANE + head:STAT_LSE_LANE + head + 1]
                        else:
                            ebar = jnp.sum(own(h, do * rest[1][_rows(ns[bi]), i * LANE:(i + 1) * LANE]), -1, keepdims=True)
                            lcol = rest[2][_rows(ns[bi]), i * LANE + h * HEAD_DIM:i * LANE + h * HEAD_DIM + 1]
                        qs[(bi, i, h)] = (own(h, q_pair), q_pair, own(h, do16), do16, ebar, lcol)
            raw = []
            for bi, i, h, sd in chains:
                q, _, do_h, _, _, _ = qs[(bi, i, h)]
                bias_blk = bias_ref[2 * i + h, :, :BAND] if sd else bias_ref[2 * i + h, :, BAND:]
                s = lax.dot_general(q, pair(key_rows(bi, sd), i, 1), NT, preferred_element_type=F32) + bias_blk
                dp = lax.dot_general(do_h, pair(key_rows(bi, sd), i, 2), NT, preferred_element_type=F32)
                raw.append((s, dp))
            soft = []
            for (bi, i, h, sd), (s, dp) in zip(chains, raw):
                ebar, lcol = qs[(bi, i, h)][4:]
                p = jnp.exp(s - lcol)
                ds = p * (dp - ebar)
                if sd:
                    db_ref[2 * i + h, :, :BAND] += ds
                else:
                    db_ref[2 * i + h, :, BAND:] += ds
                soft.append((p.astype(BF16), ds.astype(BF16)))
            grads = {}
            for (bi, i, h, sd), (p16, ds16) in zip(chains, soft):
                _, q_pair, _, do16 = qs[(bi, i, h)][:4]
                grads[(bi, i, h, sd)] = (
                    jnp.dot(ds16, pair(key_rows(bi, sd), i, 1), preferred_element_type=F32),
                    lax.dot_general(ds16, q_pair, TN, preferred_element_type=F32),
                    lax.dot_general(p16, do16, TN, preferred_element_type=F32))
            both = lambda bi, i, sd, which: jnp.where(first_head, grads[(bi, i, 0, sd)][which],
                                                      grads[(bi, i, 1, sd)][which])
            carry = list(carry) if carry is not None else None
            for bi, n in enumerate(ns):
                for i in range(hps):
                    base = 3 * LANE * i
                    dq = both(bi, i, 0, 0)
                    if with_prev:
                        dq = dq + both(bi, i, 1, 0)
                        dqkv_ref[_rows(n - 1), base + LANE:base + 2 * LANE] = (
                            carry[2 * i] + both(bi, i, 1, 1)).astype(BF16)
                        dqkv_ref[_rows(n - 1), base + 2 * LANE:base + 3 * LANE] = (
                            carry[2 * i + 1] + both(bi, i, 1, 2)).astype(BF16)
                    dqkv_ref[_rows(n), base:base + LANE] = (dq * scale).astype(BF16)
                carry = [t for i in range(hps) for t in (both(bi, i, 0, 1), both(bi, i, 0, 2))]
            return tuple(carry)

        carry = _for_blocks(blocks, nblk, 2 if hps == 1 else 1, None)
        for i in range(hps):
            base = 3 * LANE * i
            dqkv_ref[_rows(nblk - 1), base + LANE:base + 2 * LANE] = carry[2 * i].astype(BF16)
            dqkv_ref[_rows(nblk - 1), base + 2 * LANE:base + 3 * LANE] = carry[2 * i + 1].astype(BF16)

    qspec = pl.BlockSpec((None, None, sub, 3 * LANE * hps), lambda hp, b, r: (b, r, 0, hp))
    bspec = pl.BlockSpec((2 * hps, BAND, 2 * BAND), lambda hp, b, r: (hp, 0, 0))
    if compact:
        cspecs = [pl.BlockSpec((None, None, sub, ATT_OUT), lambda hp, b, r: (b, r, 0, 0)),
                  pl.BlockSpec((None, None, sub, LANE), lambda hp, b, r: (b, r, 0, 0))]
    else:
        cspecs = [pl.BlockSpec((None, sub, hps * LANE), lambda hp, b, r: (b, 0, r * (npair // hps) + hp))] * 3
    return _pcall(
        body, name=name, grid=(npair // hps, nb, d),
        in_specs=[qspec, _bias_spec(group, hps)] + cspecs, out_specs=[qspec, bspec],
        out_shape=[jax.ShapeDtypeStruct(qkv4.shape, BF16),
                   jax.ShapeDtypeStruct((GROUP_HEADS, BAND, 2 * BAND), F32)],
        compiler_params=_params("parallel", "arbitrary", "arbitrary"),
    )(qkv4, bias, *cotangent)


def _head_lanes(first_lane, one_channel):
    c = lax.broadcasted_iota(jnp.int32, (ATT_OUT, LANE), 0)
    lane = lax.broadcasted_iota(jnp.int32, (ATT_OUT, LANE), 1)
    hit = lane == first_lane + c // HEAD_DIM
    if one_channel:
        hit = hit & (c % HEAD_DIM == 0)
    return hit.astype(BF16)


def _exact_dot(v, m01, dims=None):
    parts = _split3(v)
    if dims is None:
        dot = lambda t: jnp.dot(t, m01, preferred_element_type=F32)
    else:
        dot = lambda t: lax.dot_general(t, m01, dims, preferred_element_type=F32)
    return (dot(parts[0]) + dot(parts[1])) + dot(parts[2])


def _store_own_order(value, tile_refs, out_ref):
    d, per, width = out_ref.shape
    for j in range(width // LANE):
        tile_refs[j][...] = value[:, j * LANE:(j + 1) * LANE]
    for r in range(d):
        rows = pl.ds(r, per, stride=d)
        for j in range(width // LANE):
            out_ref[r, :, j * LANE:(j + 1) * LANE] = tile_refs[j][rows, :].astype(out_ref.dtype)


def _token_orders(x, dilations):
    nb, seq, kdim = x.shape
    tm = 512

    def body(x_ref, nat_ref, *rest):
        outs, tile_refs = rest[:len(dilations)], rest[len(dilations):]
        xv = x_ref[...]
        nat_ref[...] = xv.astype(BF16)
        for o_ref in outs:
            _store_own_order(xv, tile_refs, o_ref)

    outs = _pcall(
        body, name="token_orders", grid=(nb, seq // tm), in_specs=[_tok_spec(tm, kdim)],
        out_specs=[_tok_spec(tm, kdim)]
        + [pl.BlockSpec((None, d, tm // d, kdim), lambda b, i: (b, 0, i, 0)) for d in dilations],
        out_shape=[jax.ShapeDtypeStruct((nb, seq, kdim), BF16)]
        + [jax.ShapeDtypeStruct((nb, d, seq // d, kdim), BF16) for d in dilations],
        scratch_shapes=[pltpu.VMEM((tm, LANE), F32)] * (kdim // LANE),
        compiler_params=_params("parallel", "parallel"),
    )(x)
    return [outs[0]] + [o.reshape(nb, seq, kdim) for o in outs[1:]]


def _natural_rows(p_ref, tile_refs):
    d, per, width = p_ref.shape
    for r in range(d):
        rows = pl.ds(r, per, stride=d)
        for j in range(width // LANE):
            tile_refs[j][rows, :] = p_ref[r, :, j * LANE:(j + 1) * LANE]
    return jnp.concatenate([tile_refs[j][...] for j in range(width // LANE)], axis=1)


def _combine_fwd(o0, l0, dilated, gatt):
    nb, seq, _ = gatt.shape
    tm = 512
    ntile = ATT_OUT // LANE

    def body(o0_ref, l0_ref, o1_ref, l1_ref, o2_ref, l2_ref, g_ref, oa_ref, oatt_ref, lse_ref, *tile_refs):
        spread = _head_lanes(0, False)
        l0v = l0_ref[...]
        l1v = _exact_dot(_natural_rows(l1_ref, tile_refs), spread, NT)
        l2v = _exact_dot(_natural_rows(l2_ref, tile_refs), spread, NT)
        m = jnp.maximum(jnp.maximum(l0v, l1v), l2v)
        tot = m + jnp.log(jnp.exp(l0v - m) + jnp.exp(l1v - m) + jnp.exp(l2v - m))
        o = jnp.exp(l0v - tot) * o0_ref[...]
        o = o + jnp.exp(l1v - tot) * _natural_rows(o1_ref, tile_refs)
        o = o + jnp.exp(l2v - tot) * _natural_rows(o2_ref, tile_refs)
        g = g_ref[...]
        oa_ref[...] = (o * (g * _sigmoid(g))).astype(BF16)
        oatt_ref[...] = o
        lse_ref[...] = tot

    spec = pl.BlockSpec((None, tm, ATT_OUT), lambda b, i: (b, i, 0))
    own = lambda t: pl.BlockSpec((None, t.shape[1], tm // t.shape[1], t.shape[3]), lambda b, i: (b, 0, i, 0))
    (o1, l1), (o2, l2) = dilated
    return _pcall(
        body, name="attn_combine", grid=(nb, seq // tm),
        in_specs=[spec, spec, own(o1), own(l1), own(o2), own(l2), spec], out_specs=[spec] * 3,
        out_shape=[jax.ShapeDtypeStruct((nb, seq, ATT_OUT), BF16), jax.ShapeDtypeStruct((nb, seq, ATT_OUT), F32),
                   jax.ShapeDtypeStruct((nb, seq, ATT_OUT), F32)],
        scratch_shapes=[pltpu.VMEM((tm, LANE), F32)] * ntile,
        compiler_params=_params("parallel", "parallel"),
    )(o0, l0, o1, l1, o2, l2, gatt)


def _combine_bwd(dya16, w_bra, gatt, o_att, lse, dilations):
    nb, seq, _ = gatt.shape
    tm = 512

    def body(dya_ref, w_ref, g_ref, o_ref, l_ref, do_ref, dg_ref, *rest):
        ntile = ATT_OUT // LANE
        outs, tile_refs = rest[:-ntile], rest[-ntile:]
        doa = lax.dot_general(dya_ref[...], w_ref[...], NT, preferred_element_type=F32)
        g = g_ref[...]
        sg = _sigmoid(g)
        do = doa * (g * sg)
        do_ref[...] = do
        stats = (_exact_dot(do * o_ref[...], _head_lanes(0, False))
                 + _exact_dot(l_ref[...], _head_lanes(STAT_LSE_LANE, True)))
        dg_ref[...] = (doa * o_ref[...] * (sg * (1.0 + g * (1.0 - sg)))).astype(BF16)
        for k in range(len(dilations)):
            _store_own_order(do, tile_refs, outs[2 * k])
            _store_own_order(stats, tile_refs, outs[2 * k + 1])

    spec = pl.BlockSpec((None, tm, ATT_OUT), lambda b, i: (b, i, 0))
    own = lambda d, width: pl.BlockSpec((None, d, tm // d, width), lambda b, i: (b, 0, i, 0))
    outs = _pcall(
        body, name="attn_combine_bwd", grid=(nb, seq // tm),
        in_specs=[_tok_spec(tm, D_MODEL), _whole(w_bra, True)] + [spec] * 3,
        out_specs=[spec, spec] + [own(d, w) for d in dilations for w in (ATT_OUT, LANE)],
        out_shape=[jax.ShapeDtypeStruct((nb, seq, ATT_OUT), F32), jax.ShapeDtypeStruct((nb, seq, ATT_OUT), BF16)]
        + [jax.ShapeDtypeStruct((nb, d, seq // d, w), t) for d in dilations for w, t in ((ATT_OUT, BF16), (LANE, F32))],
        scratch_shapes=[pltpu.VMEM((tm, LANE), F32)] * (ATT_OUT // LANE),
        compiler_params=_params("parallel", "parallel"),
    )(dya16, w_bra, gatt, o_att, lse)
    return outs[0], outs[1], outs[2:]


CONV_TM = 1024
CONV_TC = 1024


def _shift_down(cur, halo, k):
    rolled = pltpu.roll(cur, k, 0)
    hro = pltpu.roll(halo, k, 0)
    row = lax.broadcasted_iota(jnp.int32, hro.shape, 0)
    return jnp.concatenate([jnp.where(row < k, hro, rolled[:8]), rolled[8:]], axis=0)


def _shift_up(cur, halo, k):
    n = cur.shape[0]
    rolled = pltpu.roll(cur, n - k, 0)
    hro = pltpu.roll(halo, 8 - k, 0)
    row = lax.broadcasted_iota(jnp.int32, hro.shape, 0)
    return jnp.concatenate([rolled[:n - 8], jnp.where(row >= 8 - k, hro, rolled[n - 8:])], axis=0)


def _conv_pre(cur, halo, w_ref, b_ref):
    acc = cur * w_ref[3:4, :] + b_ref[...]
    for k in range(1, 4):
        acc = acc + _shift_down(cur, halo, k) * w_ref[3 - k:4 - k, :]
    return acc


def _conv_specs(seq):
    nblk = seq // CONV_TM
    cur = pl.BlockSpec((None, CONV_TM, CONV_TC), lambda cb, b, i: (b, i, cb))
    prev = pl.BlockSpec((None, 8, CONV_TC), lambda cb, b, i: (b, jnp.maximum(i * (CONV_TM // 8) - 1, 0), cb))
    nxt = pl.BlockSpec((None, 8, CONV_TC),
                       lambda cb, b, i: (b, jnp.minimum((i + 1) * (CONV_TM // 8), seq // 8 - 1), cb))
    wspec = pl.BlockSpec((4, CONV_TC), lambda cb, b, i: (0, cb))
    bspec = pl.BlockSpec((1, CONV_TC), lambda cb, b, i: (0, cb))
    return nblk, cur, prev, nxt, wspec, bspec


def _conv_fwd(xin, w4, bias, name):
    nb, seq, ch = xin.shape
    _, cur, prev, _, wspec, bspec = _conv_specs(seq)

    def body(x_ref, h_ref, w_ref, b_ref, o_ref):
        halo = jnp.where(pl.program_id(2) > 0, h_ref[...], 0.0)
        pre = _conv_pre(x_ref[...], halo, w_ref, b_ref)
        o_ref[...] = pre * _sigmoid(pre)

    return _pcall(
        body, name=name, grid=(ch // CONV_TC, nb, seq // CONV_TM),
        in_specs=[cur, prev, wspec, bspec], out_specs=cur,
        out_shape=jax.ShapeDtypeStruct(xin.shape, F32),
        compiler_params=_params("parallel", "parallel", "parallel"),
    )(xin, xin, w4, bias)


def _conv_bwd_pre(dact, xin, w4, bias, name):
    nb, seq, ch = xin.shape
    _, cur, prev, _, wspec, bspec = _conv_specs(seq)

    def body(da_ref, x_ref, h_ref, w_ref, b_ref, dp_ref, s_ref):
        b, i = pl.program_id(1), pl.program_id(2)

        @pl.when((b == 0) & (i == 0))
        def _():
            s_ref[...] = jnp.zeros_like(s_ref)

        halo = jnp.where(i > 0, h_ref[...], 0.0)
        x = x_ref[...]
        pre = _conv_pre(x, halo, w_ref, b_ref)
        sg = _sigmoid(pre)
        dpre = da_ref[...] * (sg * (1.0 + pre * (1.0 - sg)))
        dp_ref[...] = dpre
        s_ref[3:4, :] += jnp.sum(dpre * x, 0, keepdims=True)
        for k in range(1, 4):
            s_ref[3 - k:4 - k, :] += jnp.sum(dpre * _shift_down(x, halo, k), 0, keepdims=True)
        s_ref[4:5, :] += jnp.sum(dpre, 0, keepdims=True)

    return _pcall(
        body, name=name, grid=(ch // CONV_TC, nb, seq // CONV_TM),
        in_specs=[cur, cur, prev, wspec, bspec],
        out_specs=[cur, pl.BlockSpec((8, CONV_TC), lambda cb, b, i: (0, cb))],
        out_shape=[jax.ShapeDtypeStruct(xin.shape, F32), jax.ShapeDtypeStruct((8, ch), F32)],
        compiler_params=_params("parallel", "arbitrary", "arbitrary"),
    )(dact, xin, xin, w4, bias)


def _conv_bwd_x(dpre, w4, name):
    nb, seq, ch = dpre.shape
    nblk, cur, _, nxt, wspec, _ = _conv_specs(seq)

    def body(d_ref, n_ref, w_ref, o_ref):
        halo = jnp.where(pl.program_id(2) < nblk - 1, n_ref[...], 0.0)
        cur_v = d_ref[...]
        acc = cur_v * w_ref[3:4, :]
        for j in range(1, 4):
            acc = acc + _shift_up(cur_v, halo, j) * w_ref[3 - j:4 - j, :]
        o_ref[...] = acc.astype(BF16)

    return _pcall(
        body, name=name, grid=(ch // CONV_TC, nb, seq // CONV_TM),
        in_specs=[cur, nxt, wspec], out_specs=cur,
        out_shape=jax.ShapeDtypeStruct(dpre.shape, BF16),
        compiler_params=_params("parallel", "parallel", "parallel"),
    )(dpre, dpre, w4)


def _step_sizes(raw, tb_ref):
    shift = (LANE - GROUP_SSM_HEADS * pl.program_id(0)) % LANE
    v = pltpu.roll(raw + tb_ref[...], shift, 1)
    own = lax.broadcasted_iota(jnp.int32, v.shape, 1) < GROUP_SSM_HEADS
    sp = jnp.maximum(v, 0.0) + jnp.log1p(jnp.exp(-jnp.abs(v)))
    return jnp.where(own, sp, 0.0), jnp.where(own, _sigmoid(v), 0.0)


def _group_lanes(t):
    pads = [(0, 0)] * (t.ndim - 1) + [(0, LANE - GROUP_SSM_HEADS)]
    return jnp.stack([jnp.pad(t[..., GROUP_SSM_HEADS * g:GROUP_SSM_HEADS * (g + 1)], pads) for g in range(SSM_GROUPS)])


def _ungroup_lanes(t):
    return jnp.concatenate([t[g][..., :GROUP_SSM_HEADS] for g in range(SSM_GROUPS)], axis=-1)


def _decays(dt, al_ref):
    row = lax.broadcasted_iota(jnp.int32, (CHUNK, CHUNK), 0)
    col = lax.broadcasted_iota(jnp.int32, (CHUNK, CHUNK), 1)
    tril = (row >= col).astype(BF16)
    triu = (row <= col).astype(BF16)
    arow = -jnp.exp(al_ref[...])
    hi, mid, lo = _split3(dt * arow)
    down = lambda t: jnp.dot(tril, t, preferred_element_type=F32)
    across = lambda t: lax.dot_general(t, triu, TN, preferred_element_type=F32)
    acs = (down(hi) + down(mid)) + down(lo)
    acs_t = (across(hi) + across(mid)) + across(lo)
    return arow, acs, acs_t, row >= col, triu


STEP_CHUNKS = 8


def _ssd_specs(nb, seq):
    nc = seq // CHUNK
    hw = GROUP_SSM_HEADS * HEAD_DIM
    rows, steps = STEP_CHUNKS * CHUNK, nc // STEP_CHUNKS

    def mk(rev):
        cidx = (lambda c: steps - 1 - c) if rev else (lambda c: c)
        wide = pl.BlockSpec((None, rows, hw), lambda g, b, c: (b, cidx(c), g))
        xbc = pl.BlockSpec((None, rows, XBC_GROUP), lambda g, b, c: (b, cidx(c), g))
        lanes = pl.BlockSpec((None, None, rows, LANE), lambda g, b, c: (g, b, cidx(c), 0))
        prev = pl.BlockSpec((None, STEP_CHUNKS, None, D_STATE, hw), lambda g, b, c: (b, cidx(c), g, 0, 0))
        raw = pl.BlockSpec((None, rows, LANE), lambda g, b, c: (b, cidx(c), 0))
        return wide, xbc, lanes, prev, raw

    grow = pl.BlockSpec((None, 1, LANE), lambda g, b, c: (g, 0, 0))
    nwspec = pl.BlockSpec((1, hw), lambda g, b, c: (0, g))
    tbspec = pl.BlockSpec((1, LANE), lambda g, b, c: (0, 0))
    return nc, steps, hw, mk, grow, nwspec, tbspec


def _head_expand():
    hw = GROUP_SSM_HEADS * HEAD_DIM
    r = lax.broadcasted_iota(jnp.int32, (LANE, hw), 0)
    c = lax.broadcasted_iota(jnp.int32, (LANE, hw), 1)
    return ((c // HEAD_DIM) == r).astype(BF16)


def _split3(v):
    hi = v.astype(BF16)
    rest = v - hi.astype(F32)
    mid = rest.astype(BF16)
    return hi, mid, (rest - mid.astype(F32)).astype(BF16)


def _to_channels(v, e):
    hi, mid, lo = _split3(v)
    dot = lambda t: jnp.dot(t, e, preferred_element_type=F32)
    return (dot(hi) + dot(mid)) + dot(lo)


def _to_heads(w, e):
    hi, mid, lo = _split3(w)
    dot = lambda t: lax.dot_general(t, e, (((1,), (1,)), ((), ())), preferred_element_type=F32)
    return (dot(hi) + dot(mid)) + dot(lo)


def _row8(v):
    return jnp.broadcast_to(v, (8, v.shape[1]))


def _ssd_chunk_setup(dt, al_ref, ds_ref):
    arow, acs, acs_t, causal, triu = _decays(dt, al_ref)
    e = _head_expand()
    dtx = _to_channels(dt, e)
    acsx = _to_channels(acs, e)
    lastx = acsx[CHUNK - 1:CHUNK, :]
    dskx = _to_channels(_row8(ds_ref[...]), e)[0:1, :]
    return arow, acs, acs_t, causal, triu, e, dtx, acsx, lastx, dskx


def _ssd_fwd(xbc, dt_raw, dt_bias_row, z, alog_g, dskip_g, normw):
    nb, seq, _ = xbc.shape
    nc, steps, hw, mk, grow, nwspec, tbspec = _ssd_specs(nb, seq)
    wide, xbc_spec, lanes, prev, raw = mk(False)
    tn = (((0,), (0,)), ((), ()))

    def body(xbc_ref, dt_ref, tb_ref, z_ref, al_ref, ds_ref, nw_ref, ys_ref, y_ref, sp_ref, st_ref):
        @pl.when(pl.program_id(2) == 0)
        def _():
            st_ref[...] = jnp.zeros_like(st_ref)

        for ci in range(STEP_CHUNKS):
            chunk(ci, xbc_ref, dt_ref, tb_ref, z_ref, al_ref, ds_ref, nw_ref, ys_ref, y_ref, sp_ref, st_ref)

    def chunk(ci, xbc_ref, dt_ref, tb_ref, z_ref, al_ref, ds_ref, nw_ref, ys_ref, y_ref, sp_ref, st_ref):
        rows = slice(ci * CHUNK, (ci + 1) * CHUNK)
        dt, _ = _step_sizes(dt_ref[rows, :], tb_ref)
        _, acs, acs_t, causal, _, _, dtx, acsx, lastx, dskx = _ssd_chunk_setup(dt, al_ref, ds_ref)
        bmat = xbc_ref[rows, GROUP_CH:GROUP_CH + D_STATE].astype(BF16)
        cmat = xbc_ref[rows, GROUP_CH + D_STATE:].astype(BF16)
        cb = lax.dot_general(cmat, bmat, (((1,), (1,)), ((), ())), preferred_element_type=F32)
        x = xbc_ref[rows, :GROUP_CH]
        xdt = x * dtx
        xdt16 = xdt.astype(BF16)
        first_head = lax.broadcasted_iota(jnp.int32, (CHUNK, LANE), 1) < HEAD_DIM
        pairs = []
        for hp in range(GROUP_SSM_HEADS // 2):
            xp = xdt16[:, hp * LANE:(hp + 1) * LANE]
            two = []
            for j in (2 * hp, 2 * hp + 1):
                lmat = jnp.exp(jnp.where(causal, acs[:, j:j + 1] - acs_t[j:j + 1, :], -jnp.inf))
                two.append(jnp.dot((cb * lmat).astype(BF16), xp, preferred_element_type=F32))
            pairs.append(jnp.where(first_head, two[0], two[1]))
        yd = jnp.concatenate(pairs, axis=1)
        s_prev = st_ref[...]
        s16 = s_prev.astype(BF16)
        sp_ref[ci] = s16
        yo = jnp.dot(cmat, s16, preferred_element_type=F32) * jnp.exp(acsx)
        sts = lax.dot_general(bmat, (xdt * jnp.exp(lastx - acsx)).astype(BF16), tn, preferred_element_type=F32)
        st_ref[...] = s_prev * jnp.exp(lastx) + sts
        y = yd + yo + dskx * x
        zz = z_ref[rows, :]
        u = y * (zz * _sigmoid(zz))
        rn = lax.rsqrt(jnp.mean(u * u, -1, keepdims=True) + RMS_EPS)
        ys_ref[rows, :] = (u * rn * nw_ref[...]).astype(BF16)
        y_ref[rows, :] = y

    return _pcall(
        body, name="ssd_fwd", grid=(SSM_GROUPS, nb, steps),
        in_specs=[xbc_spec, raw, tbspec, wide, grow, grow, nwspec],
        out_specs=[wide, wide, prev],
        out_shape=[jax.ShapeDtypeStruct((nb, seq, D_INNER), BF16), jax.ShapeDtypeStruct((nb, seq, D_INNER), F32),
                   jax.ShapeDtypeStruct((nb, nc, SSM_GROUPS, D_STATE, hw), BF16)],
        scratch_shapes=[pltpu.VMEM((D_STATE, hw), F32)],
        compiler_params=_params("parallel", "parallel", "arbitrary"),
    )(xbc, dt_raw, dt_bias_row, z, alog_g, dskip_g, normw)


def _ssd_bwd(xbc, dt_raw, dt_bias_row, z, y, dys, sprev, alog_g, dskip_g, normw):
    nb, seq, _ = xbc.shape
    nc, steps, hw, mk, grow, nwspec, tbspec = _ssd_specs(nb, seq)
    wide, xbc_spec, lanes, prev, raw = mk(True)
    nt = (((1,), (1,)), ((), ()))
    tn = (((0,), (0,)), ((), ()))

    def body(xbc_ref, dt_ref, tb_ref, z_ref, y_ref, dys_ref, sp_ref, al_ref, ds_ref, nw_ref,
             dxbc_ref, ddt_ref, dz_ref, small_ref, dnw_ref, g_ref):
        b, c = pl.program_id(1), pl.program_id(2)

        @pl.when((b == 0) & (c == 0))
        def _():
            small_ref[...] = jnp.zeros_like(small_ref)
            dnw_ref[...] = jnp.zeros_like(dnw_ref)

        @pl.when(c == 0)
        def _():
            g_ref[...] = jnp.zeros_like(g_ref)

        for ci in reversed(range(STEP_CHUNKS)):
            chunk(ci, xbc_ref, dt_ref, tb_ref, z_ref, y_ref, dys_ref, sp_ref, al_ref, ds_ref, nw_ref,
                  dxbc_ref, ddt_ref, dz_ref, small_ref, dnw_ref, g_ref)

    def chunk(ci, xbc_ref, dt_ref, tb_ref, z_ref, y_ref, dys_ref, sp_ref, al_ref, ds_ref, nw_ref,
              dxbc_ref, ddt_ref, dz_ref, small_ref, dnw_ref, g_ref):
        rows = slice(ci * CHUNK, (ci + 1) * CHUNK)
        yv, zz, dys_v, nw = y_ref[rows, :], z_ref[rows, :], dys_ref[rows, :], nw_ref[...]
        sz = _sigmoid(zz)
        silu = zz * sz
        u = yv * silu
        rn = lax.rsqrt(jnp.mean(u * u, -1, keepdims=True) + RMS_EPS)
        gn = dys_v * nw
        du = rn * gn - u * (rn * rn * rn) * jnp.mean(u * gn, -1, keepdims=True)
        dnw_ref[...] += jnp.sum(dys_v * u * rn, 0, keepdims=True)
        dy = du * silu
        dz_ref[rows, :] = du * yv * (sz * (1.0 + zz * (1.0 - sz)))

        dt, sg = _step_sizes(dt_ref[rows, :], tb_ref)
        arow, acs, acs_t, causal, triu, e, dtx, acsx, lastx, dskx = _ssd_chunk_setup(dt, al_ref, ds_ref)
        dfsx = jnp.exp(acsx)
        dtex = jnp.exp(lastx - acsx)
        bmat = xbc_ref[rows, GROUP_CH:GROUP_CH + D_STATE].astype(BF16)
        cmat = xbc_ref[rows, GROUP_CH + D_STATE:].astype(BF16)
        cb = lax.dot_general(cmat, bmat, nt, preferred_element_type=F32)
        x = xbc_ref[rows, :GROUP_CH]
        xdt = x * dtx
        xdt16 = xdt.astype(BF16)
        xdte = xdt * dtex
        dy16 = dy.astype(BF16)
        dyd = dy * dfsx
        dyd16 = dyd.astype(BF16)
        s16 = sp_ref[ci]
        g = g_ref[...]
        g16 = g.astype(BF16)
        cs = jnp.dot(cmat, s16, preferred_element_type=F32)
        dc_off = lax.dot_general(dyd16, s16, nt, preferred_element_type=F32)
        g_here = lax.dot_general(cmat, dyd16, tn, preferred_element_type=F32)
        bg = jnp.dot(bmat, g16, preferred_element_type=F32)
        db_st = lax.dot_general(xdte.astype(BF16), g16, nt, preferred_element_type=F32)
        ddte_w = bg * xdte
        dcd = _to_heads(_row8(jnp.sum(g * s16.astype(F32), 0, keepdims=True)), e)[0:1, :]
        lane = lax.broadcasted_iota(jnp.int32, (CHUNK, LANE), 1)
        first_head = lane < HEAD_DIM
        sub = lax.broadcasted_iota(jnp.int32, (CHUNK, LANE), 0)
        dacs = jnp.zeros((CHUNK, LANE), F32)
        colsums = jnp.zeros((CHUNK, LANE), F32)
        dcb = jnp.zeros((CHUNK, CHUNK), F32)
        pairs = []
        for hp in range(GROUP_SSM_HEADS // 2):
            xp = xdt16[:, hp * LANE:(hp + 1) * LANE]
            dyp = dy16[:, hp * LANE:(hp + 1) * LANE]
            two = []
            for idx, j in enumerate((2 * hp, 2 * hp + 1)):
                lmat = jnp.exp(jnp.where(causal, acs[:, j:j + 1] - acs_t[j:j + 1, :], -jnp.inf))
                mf = cb * lmat
                dy_h = jnp.where(first_head if idx == 0 else jnp.logical_not(first_head), dyp, jnp.zeros_like(dyp))
                dm = lax.dot_general(dy_h, xp, nt, preferred_element_type=F32)
                two.append(lax.dot_general(mf.astype(BF16), dyp, tn, preferred_element_type=F32))
                wmat = dm * mf
                dcb = dcb + dm * lmat
                dacs = jnp.where(lane == j, jnp.sum(wmat, -1, keepdims=True), dacs)
                colsums = jnp.where(sub == j, jnp.sum(wmat, 0, keepdims=True), colsums)
            pairs.append(jnp.where(first_head, two[0], two[1]))
        dxdt = bg * dtex + jnp.concatenate(pairs, axis=1)
        dacs = dacs - colsums.T + _to_heads(dyd * cs - ddte_w, e)
        cd_row = jnp.exp(acs[CHUNK - 1:CHUNK, :])
        tail = _to_heads(_row8(jnp.sum(ddte_w, 0, keepdims=True)), e)[0:1, :] + dcd * cd_row
        dacs = dacs + jnp.where(sub == CHUNK - 1, tail, 0.0)
        d_hi, d_mid, d_lo = _split3(dacs)
        up = lambda t: jnp.dot(triu, t, preferred_element_type=F32)
        da = (up(d_hi) + up(d_mid)) + up(d_lo)
        ddt_raw = (da * arow + _to_heads(dxdt * x, e)) * sg
        ddt_ref[rows, :] = ddt_raw
        small_ref[0:1, :] += jnp.sum(da * dt, 0, keepdims=True) * arow
        small_ref[1:2, :] += _to_heads(_row8(jnp.sum(dy * x, 0, keepdims=True)), e)[0:1, :]
        small_ref[2:3, :] += jnp.sum(ddt_raw, 0, keepdims=True)
        dcb16 = dcb.astype(BF16)
        dxbc_ref[rows, GROUP_CH + D_STATE:] = dc_off + jnp.dot(dcb16, bmat, preferred_element_type=F32)
        dxbc_ref[rows, GROUP_CH:GROUP_CH + D_STATE] = db_st + lax.dot_general(dcb16, cmat, tn,
                                                                               preferred_element_type=F32)
        dxbc_ref[rows, :GROUP_CH] = dxdt * dtx + dskx * dy
        g_ref[...] = g * jnp.exp(lastx) + g_here

    return _pcall(
        body, name="ssd_bwd", grid=(SSM_GROUPS, nb, steps),
        in_specs=[xbc_spec, raw, tbspec, wide, wide, wide, prev, grow, grow, nwspec],
        out_specs=[xbc_spec, lanes, wide,
                   pl.BlockSpec((None, 8, LANE), lambda g, b, c: (g, 0, 0)), nwspec],
        out_shape=[jax.ShapeDtypeStruct((nb, seq, CONV_DIM), F32),
                   jax.ShapeDtypeStruct((SSM_GROUPS, nb, seq, LANE), F32),
                   jax.ShapeDtypeStruct((nb, seq, D_INNER), F32),
                   jax.ShapeDtypeStruct((SSM_GROUPS, 8, LANE), F32),
                   jax.ShapeDtypeStruct((1, D_INNER), F32)],
        scratch_shapes=[pltpu.VMEM((D_STATE, hw), F32)],
        compiler_params=_params("parallel", "arbitrary", "arbitrary"),
    )(xbc, dt_raw, dt_bias_row, z, y, dys, sprev, alog_g, dskip_g, normw)


EW_TM = 256


def _merge_fwd(oa16, y_ssm16, w_bra, w_brb, gm, bgate):
    nb, seq, _ = oa16.shape

    def body(oa_ref, ys_ref, wa_ref, wb_ref, ga_ref, gb_ref, bg_ref, a_ref, b_ref, o_ref):
        y_a = jnp.dot(oa_ref[...], wa_ref[...], preferred_element_type=F32)
        y_b = jnp.dot(ys_ref[...], wb_ref[...], preferred_element_type=F32)
        a_ref[...] = y_a
        b_ref[...] = y_b
        sa = _sigmoid(ga_ref[...] + bg_ref[0:1, :])
        sb = _sigmoid(gb_ref[...] + bg_ref[1:2, :])
        o_ref[...] = (sa * y_a + sb * y_b).astype(BF16)

    spec = pl.BlockSpec((None, EW_TM, D_MODEL), lambda b, i: (b, i, 0))
    spec1 = pl.BlockSpec((None, EW_TM, D_MODEL), lambda b, i: (b, i, 1))
    return _pcall(
        body, name="merge_fwd", grid=(nb, seq // EW_TM),
        in_specs=[_tok_spec(EW_TM, ATT_OUT), _tok_spec(EW_TM, D_INNER), _whole(w_bra, True), _whole(w_brb, True),
                  spec, spec1, pl.BlockSpec((8, D_MODEL), lambda b, i: (0, 0))],
        out_specs=[spec] * 3,
        out_shape=[jax.ShapeDtypeStruct((nb, seq, D_MODEL), F32)] * 2 + [jax.ShapeDtypeStruct((nb, seq, D_MODEL), BF16)],
        compiler_params=_params("parallel", "parallel"),
    )(oa16, y_ssm16, w_bra, w_brb, gm, gm, bgate)


def _merge_bwd(dpre16, w_out16, y_a, y_b, gm, bgate):
    nb, seq, _ = y_a.shape

    def body(dp_ref, w_ref, a_ref, b_ref, ga_ref, gb_ref, bg_ref, dya_ref, dyb_ref, dg_ref, s_ref):
        @pl.when((pl.program_id(0) == 0) & (pl.program_id(1) == 0))
        def _():
            s_ref[...] = jnp.zeros_like(s_ref)

        dm = lax.dot_general(dp_ref[...], w_ref[...], NT, preferred_element_type=F32)
        sa = _sigmoid(ga_ref[...] + bg_ref[0:1, :])
        sb = _sigmoid(gb_ref[...] + bg_ref[1:2, :])
        dya_ref[...] = (dm * sa).astype(BF16)
        dyb_ref[...] = (dm * sb).astype(BF16)
        dga = dm * a_ref[...] * (sa * (1.0 - sa))
        dgb = dm * b_ref[...] * (sb * (1.0 - sb))
        dg_ref[:, :D_MODEL] = dga.astype(BF16)
        dg_ref[:, D_MODEL:] = dgb.astype(BF16)
        s_ref[0:1, :] += jnp.sum(dga, 0, keepdims=True)
        s_ref[1:2, :] += jnp.sum(dgb, 0, keepdims=True)

    spec = pl.BlockSpec((None, EW_TM, D_MODEL), lambda b, i: (b, i, 0))
    spec1 = pl.BlockSpec((None, EW_TM, D_MODEL), lambda b, i: (b, i, 1))
    small = pl.BlockSpec((8, D_MODEL), lambda b, i: (0, 0))
    return _pcall(
        body, name="merge_bwd", grid=(nb, seq // EW_TM),
        in_specs=[spec, _whole(w_out16, True), spec, spec, spec, spec1, small],
        out_specs=[spec, spec, pl.BlockSpec((None, EW_TM, 2 * D_MODEL), lambda b, i: (b, i, 0)), small],
        out_shape=[jax.ShapeDtypeStruct((nb, seq, D_MODEL), BF16), jax.ShapeDtypeStruct((nb, seq, D_MODEL), BF16),
                   jax.ShapeDtypeStruct((nb, seq, 2 * D_MODEL), BF16), jax.ShapeDtypeStruct((8, D_MODEL), F32)],
        compiler_params=_params("arbitrary", "arbitrary"),
    )(dpre16, w_out16, y_a, y_b, gm, gm, bgate)


def _ln_loss(x, merged16, w_out16, gp, p16, w_ple16, target, bgate, ln_g, ln_b):
    nb, seq, _ = x.shape

    def body(x_ref, m_ref, wo_ref, gp_ref, p_ref, wp_ref, t_ref, bg_ref, g_ref, b_ref,
             dx_ref, dp_ref, dpw_ref, dgp_ref, s_ref):
        @pl.when((pl.program_id(0) == 0) & (pl.program_id(1) == 0))
        def _():
            s_ref[...] = jnp.zeros_like(s_ref)

        sp = _sigmoid(gp_ref[...] + bg_ref[2:3, :])
        pw = jnp.dot(p_ref[...], wp_ref[...], preferred_element_type=F32)
        mix = jnp.dot(m_ref[...], wo_ref[...], preferred_element_type=F32)
        pre = ALPHA * x_ref[...] + mix + sp * pw
        mu = jnp.mean(pre, -1, keepdims=True)
        cen = pre - mu
        rstd = lax.rsqrt(jnp.mean(cen * cen, -1, keepdims=True) + LN_EPS)
        xhat = cen * rstd
        err = xhat * g_ref[...] + b_ref[...] - t_ref[...]
        dy = err * (1.0 / D_MODEL)
        dxh = dy * g_ref[...]
        dpre = rstd * (dxh - jnp.mean(dxh, -1, keepdims=True) - xhat * jnp.mean(dxh * xhat, -1, keepdims=True))
        dx_ref[...] = ALPHA * dpre
        dp_ref[...] = dpre.astype(BF16)
        dpw_ref[...] = (dpre * sp).astype(BF16)
        dgp = dpre * pw * (sp * (1.0 - sp))
        dgp_ref[...] = dgp.astype(BF16)
        s_ref[0:1, :] += jnp.sum(dy * xhat, 0, keepdims=True)
        s_ref[1:2, :] += jnp.sum(dy, 0, keepdims=True)
        s_ref[2:3, :] += jnp.sum(dgp, 0, keepdims=True)
        s_ref[3:4, :] += jnp.sum(err * err, 0, keepdims=True)

    spec = pl.BlockSpec((None, EW_TM, D_MODEL), lambda b, i: (b, i, 0))
    small = pl.BlockSpec((8, D_MODEL), lambda b, i: (0, 0))
    row = pl.BlockSpec((1, D_MODEL), lambda b, i: (0, 0))
    return _pcall(
        body, name="ln_loss", grid=(nb, seq // EW_TM),
        in_specs=[spec, spec, _whole(w_out16, True), spec, pl.BlockSpec((None, EW_TM, PLE_DIM), lambda b, i: (b, i, 0)),
                  _whole(w_ple16, True), spec, small, row, row],
        out_specs=[spec] * 4 + [small],
        out_shape=[jax.ShapeDtypeStruct((nb, seq, D_MODEL), F32)] + [jax.ShapeDtypeStruct((nb, seq, D_MODEL), BF16)] * 3
        + [jax.ShapeDtypeStruct((8, D_MODEL), F32)],
        compiler_params=_params("arbitrary", "arbitrary"),
    )(x, merged16, w_out16, gp, p16, w_ple16, target, bgate, ln_g, ln_b)


def _adamw(w, g, m, v, name):
    rows, cols = w.shape
    tr = _row_tile(rows, cols, 8, 5 << 19)
    c1 = 1.0 - ADAM_B1 ** ADAM_STEP
    c2 = 1.0 - ADAM_B2 ** ADAM_STEP

    def body(w_ref, g_ref, m_ref, v_ref, d_ref, nm_ref, nv_ref):
        gv = g_ref[...]
        nm = ADAM_B1 * m_ref[...] + (1.0 - ADAM_B1) * gv
        nv = ADAM_B2 * v_ref[...] + (1.0 - ADAM_B2) * (gv * gv)
        d_ref[...] = -ADAM_LR * ((nm / c1) / (jnp.sqrt(nv / c2) + ADAM_EPS) + ADAM_WD * w_ref[...])
        nm_ref[...] = nm
        nv_ref[...] = nv

    spec = pl.BlockSpec((tr, cols), lambda i: (i, 0))
    return _pcall(
        body, name=name, grid=(rows // tr,), in_specs=[spec] * 4, out_specs=[spec] * 3,
        out_shape=[jax.ShapeDtypeStruct(w.shape, F32)] * 3, compiler_params=_params("parallel"),
    )(w, g, m, v)


def _sum_rows(parts, out_dtype, name):
    rows, cols = parts[0].shape
    tr = rows
    for cand in range(16, rows, 16):
        if rows % cand == 0 and cand * cols * 4 <= (1 << 20):
            tr = cand
    n = len(parts)

    def body(*refs):
        acc = refs[0][...].astype(F32)
        for r in refs[1:n]:
            acc = acc + r[...].astype(F32)
        refs[n][...] = acc.astype(out_dtype)

    spec = pl.BlockSpec((tr, cols), lambda i: (i, 0))
    return _pcall(
        body, name=name, grid=(rows // tr,), in_specs=[spec] * n, out_specs=spec,
        out_shape=jax.ShapeDtypeStruct((rows, cols), out_dtype), compiler_params=_params("parallel"),
    )(*parts)


def _place():
    return lax.axis_index("x"), lax.axis_index("y"), lax.axis_index("c")


def _other_chips(x, y):
    return [(1 - x, y), (x, 1 - y), (1 - x, 1 - y)]


def _remote(src, dst, send_sem, recv_sem, to):
    return pltpu.make_async_remote_copy(src_ref=src, dst_ref=dst, send_sem=send_sem, recv_sem=recv_sem,
                                        device_id=to, device_id_type=MESH)


ANY = pl.BlockSpec(memory_space=pl.ANY)
D2D_CHUNK_BYTES = 512 * 1024
ICI_CHUNK_BYTES = 2 * 1024 * 1024


def _row_chunks(rows, row_bytes, chunk_bytes=D2D_CHUNK_BYTES):
    per = max(16, chunk_bytes // row_bytes // 16 * 16)
    return [(s, min(per, rows - s)) for s in range(0, rows, per)]


def _row_tile(rows, cols, align, limit=1 << 21):
    best = None
    for cand in range(align, rows + 1, align):
        if rows % cand == 0 and cand * cols * 4 <= limit:
            best = cand
    return best or rows


def _allgather_pieces(pieces):
    n = len(pieces)
    halves = [_row_chunks(p.shape[0] // 2, p.shape[1] * p.dtype.itemsize, ICI_CHUNK_BYTES) for p in pieces]
    entries = [(a, q, s, m, j) for a in range(n) for q, (s, m) in enumerate(halves[a]) for j in range(3)]
    slot = {(a, q, j): k for k, (a, q, _, _, j) in enumerate(entries)}
    n_ici = len(entries)

    def body(*refs):
        ins, outs = refs[:n], refs[n:2 * n]
        send_sems, recv_sems = refs[2 * n:]
        x, y, c = _place()
        me = 2 * x + y
        sibling = (x, y, 1 - c)
        chips = _other_chips(x, y)

        def landed(a, s, m, j, core):
            half = ins[a].shape[0] // 2
            return outs[a].at[2 * chips[j][0] + chips[j][1], pl.ds(core * half + s, m)]

        sent = []
        for k, (a, q, s, m, j) in enumerate(entries):
            if j < 2:
                half = ins[a].shape[0] // 2
                cp = _remote(ins[a].at[pl.ds(c * half + s, m)], outs[a].at[me, pl.ds(c * half + s, m)],
                             send_sems.at[k], recv_sems.at[k], (*chips[j], c))
                cp.start()
                sent.append(cp)

        def pass_to_sibling(k, blk):
            fw = _remote(blk, blk, send_sems.at[n_ici + k], recv_sems.at[n_ici + k], sibling)
            fw.start()
            sent.append(fw)

        for k, (a, q, s, m, j) in enumerate(entries):
            if j < 2:
                blk = landed(a, s, m, j, c)
                _remote(blk, blk, send_sems.at[k], recv_sems.at[k], (*chips[j], c)).wait_recv()
                first = q < (len(halves[a]) + 1) // 2
                if (j == 0) == first:
                    on = slot[(a, q, 2)]
                    rl = _remote(blk, blk, send_sems.at[on], recv_sems.at[on], (*chips[1 - j], c))
                    rl.start()
                    sent.append(rl)
                pass_to_sibling(k, blk)
        for k, (a, q, s, m, j) in enumerate(entries):
            if j == 2:
                blk = landed(a, s, m, j, c)
                _remote(blk, blk, send_sems.at[k], recv_sems.at[k], (*chips[j], c)).wait_recv()
                pass_to_sibling(k, blk)
        for k, (a, q, s, m, j) in enumerate(entries):
            blk = landed(a, s, m, j, 1 - c)
            _remote(blk, blk, send_sems.at[n_ici + k], recv_sems.at[n_ici + k], sibling).wait_recv()
        for cp in sent:
            cp.wait_send()

    gathered = _pcall(
        body, name="allgather_weights", in_specs=[ANY] * n, out_specs=[ANY] * n,
        out_shape=[jax.ShapeDtypeStruct((4,) + p.shape, p.dtype) for p in pieces],
        scratch_shapes=[pltpu.SemaphoreType.DMA((2 * n_ici,)), pltpu.SemaphoreType.DMA((2 * n_ici,))],
        compiler_params=pltpu.CompilerParams(has_side_effects=True),
    )(*pieces)
    x, y, _ = _place()
    return [lax.dynamic_update_slice(g, p[None], (2 * x + y, 0, 0)) for g, p in zip(gathered, pieces)]


def _sibling_exchange(grads):
    n = len(grads)
    chunks = [_row_chunks(g.shape[1] // 2, g.shape[2] * g.dtype.itemsize) for g in grads]
    n_sem = 4 * sum(len(ch) for ch in chunks)

    def body(*refs):
        ins, gots = refs[:n], refs[n:2 * n]
        send_sems, recv_sems = refs[2 * n:]
        x, y, c = _place()
        sibling = (x, y, 1 - c)
        work = []
        for a in range(n):
            half = ins[a].shape[1] // 2
            for piece in range(4):
                for s, m in chunks[a]:
                    k = len(work)
                    cp = _remote(ins[a].at[piece, pl.ds((1 - c) * half + s, m)], gots[a].at[piece, pl.ds(s, m)],
                                 send_sems.at[k], recv_sems.at[k], sibling)
                    cp.start()
                    work.append(cp)
        for cp in work:
            cp.wait()

    return _pcall(
        body, name="grad_sibling_exchange", in_specs=[ANY] * n, out_specs=[ANY] * n,
        out_shape=[jax.ShapeDtypeStruct((4, g.shape[1] // 2, g.shape[2]), g.dtype) for g in grads],
        scratch_shapes=[pltpu.SemaphoreType.DMA((n_sem,)), pltpu.SemaphoreType.DMA((n_sem,))],
        compiler_params=pltpu.CompilerParams(has_side_effects=True),
    )(*grads)


def _sibling_gather(fulls):
    n = len(fulls)
    chunks = [_row_chunks(f.shape[0] // 2, f.shape[1] * f.dtype.itemsize) for f in fulls]
    n_sem = sum(len(ch) for ch in chunks)

    def body(*refs):
        outs = refs[n:2 * n]
        send_sems, recv_sems = refs[2 * n:]
        x, y, c = _place()
        sibling = (x, y, 1 - c)
        work = []
        for a in range(n):
            h = outs[a].shape[0] // 2
            for s, m in chunks[a]:
                k = len(work)
                mine = outs[a].at[pl.ds(c * h + s, m)]
                cp = _remote(mine, mine, send_sems.at[k], recv_sems.at[k], sibling)
                cp.start()
                work.append((a, s, m, cp))
        for k, (a, s, m, cp) in enumerate(work):
            h = outs[a].shape[0] // 2
            cp.wait_send()
            theirs = outs[a].at[pl.ds((1 - c) * h + s, m)]
            _remote(theirs, theirs, send_sems.at[k], recv_sems.at[k], sibling).wait_recv()

    return _pcall(
        body, name="grad_sibling_gather", in_specs=[ANY] * n, out_specs=[ANY] * n,
        out_shape=[jax.ShapeDtypeStruct(f.shape, f.dtype) for f in fulls],
        input_output_aliases={a: a for a in range(n)},
        scratch_shapes=[pltpu.SemaphoreType.DMA((n_sem,)), pltpu.SemaphoreType.DMA((n_sem,))],
        compiler_params=pltpu.CompilerParams(has_side_effects=True),
    )(*fulls)


def _pair_sum(grad, got, place, name):
    _, rows, cols = grad.shape
    half = rows // 2
    tr = _row_tile(half, cols, 16)

    def body(p_ref, a_ref, b_ref, o_ref):
        o_ref[...] = (a_ref[...].astype(F32) + b_ref[...].astype(F32)).astype(BF16)

    return _pcall(
        body, name=name,
        grid_spec=pltpu.PrefetchScalarGridSpec(
            num_scalar_prefetch=1, grid=(4, half // tr),
            in_specs=[pl.BlockSpec((None, tr, cols), lambda k, i, p: (k, p[1] * (half // tr) + i, 0)),
                      pl.BlockSpec((None, tr, cols), lambda k, i, p: (k, i, 0))],
            out_specs=pl.BlockSpec((None, tr, cols), lambda k, i, p: (k, i, 0))),
        out_shape=jax.ShapeDtypeStruct((4, half, cols), BF16),
        compiler_params=_params("parallel", "parallel"),
    )(place, grad, got)


def _chip_sum(sums, got, place, name):
    _, h, cols = sums.shape
    tr = _row_tile(h, cols, 16)

    def body(p_ref, own_ref, g0, g1, g2, o_ref):
        o_ref[...] = ((own_ref[...].astype(F32) + g0[...].astype(F32)) + g1[...].astype(F32)) + g2[...].astype(F32)

    gspec = lambda j: pl.BlockSpec((None, tr, cols), lambda i, p: (j, i, 0))
    return _pcall(
        body, name=name,
        grid_spec=pltpu.PrefetchScalarGridSpec(
            num_scalar_prefetch=1, grid=(h // tr,),
            in_specs=[pl.BlockSpec((None, tr, cols), lambda i, p: (p[0], i, 0)), gspec(0), gspec(1), gspec(2)],
            out_specs=pl.BlockSpec((tr, cols), lambda i, p: (p[1] * (h // tr) + i, 0))),
        out_shape=jax.ShapeDtypeStruct((2 * h, cols), F32),
        compiler_params=_params("parallel"),
    )(place, sums, got, got, got)


def _allgather8(buf, name):
    rows = buf.shape[0]

    def body(in_ref, out_ref, send_sems, recv_sems):
        x, y, c = _place()
        me = 4 * x + 2 * y + c
        out_ref[me] = in_ref[...]
        work = []
        for rel in range(1, 8):
            fx, fy, fc = (rel >> 2) & 1, (rel >> 1) & 1, rel & 1
            to = (x ^ fx, y ^ fy, c ^ fc)
            cp = _remote(in_ref, out_ref.at[me], send_sems.at[rel - 1], recv_sems.at[rel - 1], to)
            cp.start()
            work.append((cp, 4 * to[0] + 2 * to[1] + to[2]))
        for rel, (cp, frm) in enumerate(work):
            cp.wait_send()
            blk = out_ref.at[frm]
            _remote(blk, blk, send_sems.at[rel], recv_sems.at[rel], (x, y, c)).wait_recv()

    return _pcall(
        body, name=name, in_specs=[pl.BlockSpec(memory_space=pltpu.VMEM)],
        out_specs=pl.BlockSpec(memory_space=pltpu.VMEM),
        out_shape=jax.ShapeDtypeStruct((8, rows, LANE), F32),
        scratch_shapes=[pltpu.SemaphoreType.DMA((7,)), pltpu.SemaphoreType.DMA((7,))],
        compiler_params=pltpu.CompilerParams(has_side_effects=True),
    )(buf)


def _pack_rows(arrs):
    flats = [a.reshape(-1).astype(F32) for a in arrs]
    starts = np.cumsum([0] + [-(-f.shape[0] // LANE) * LANE for f in flats])
    total = -(-int(starts[-1]) // (8 * LANE)) * 8 * LANE
    flat = sum(jnp.pad(f, (int(s), total - int(s) - f.shape[0])) for f, s in zip(flats, starts))
    return flat.reshape(total // LANE, LANE)


def _unpack_rows(buf, shapes):
    flat = buf.reshape(-1)
    outs, off = [], 0
    for s in shapes:
        n = int(np.prod(s))
        outs.append(flat[off:off + n].reshape(s))
        off += -(-n // LANE) * LANE
    return outs


def _local_grads(x, p, target, wseg, w_br16, w_out16, w_ple16, b_gate, conv_w, conv_b, dt_bias, a_log, d_skip,
                 ssm_norm_w, ln_g, ln_b, rel_bias, finish_dx):
    nb, seq, _ = x.shape
    bmaps = jnp.asarray(_bucket_maps())
    bias = _bias_tables(rel_bias, bmaps)
    bgate8 = jnp.pad(b_gate, ((0, 5), (0, 0)))
    dils = [d for _, d in PATTERNS]

    x16p = _token_orders(x, dils[1:])
    x16 = x16p[0]
    p16 = p.astype(BF16)
    qkv = [_proj(x16p[g], [wseg["qkv%d" % g]], BF16, "proj_qkv%d" % g, True, 2 * MM_TM)[0].reshape(
        nb, dils[g], seq // dils[g], -1) for g in range(3)]
    nat = {}
    for gi, (group, tm) in enumerate(NAT_GROUPS):
        outs = _proj(x16, [wseg[s] for s in group], F32, "proj_nat%d" % gi, True, tm)
        nat.update(zip(group, outs))
    att = [_attn_fwd(qkv[g], bias, g, dils[g], "attn_fwd%d" % g) for g in range(3)]
    oa, o_att, lse = _combine_fwd(att[0][0], att[0][1], att[1:], nat["gatt"])

    conv_wg, conv_bg = _xbc_group_order(conv_w), _xbc_group_order(conv_b)
    act = _conv_fwd(nat["xbc"], conv_wg, conv_bg, "conv_fwd")
    dt_bias_row = jnp.pad(dt_bias, ((0, 0), (0, LANE - SSM_HEADS)))
    alog_g, dskip_g = _group_lanes(a_log), _group_lanes(d_skip)
    y_ssm, y_all, sprev = _ssd_fwd(act, nat["dt"], dt_bias_row, nat["z"], alog_g, dskip_g, ssm_norm_w)

    w_bra, w_brb = w_br16[:ATT_OUT], w_br16[ATT_OUT:]
    y_a, y_b, merged = _merge_fwd(oa, y_ssm, w_bra, w_brb, nat["gm"], bgate8)

    dx, dpre16, dpw16, dgp16, ln_sums = _ln_loss(x, merged, w_out16, nat["gp"], p16, w_ple16, target, bgate8,
                                                 ln_g, ln_b)
    loss_sum = (0.5 / D_MODEL) * jnp.sum(ln_sums[3])
    dya16, dyb16, dgm16, mg_sums = _merge_bwd(dpre16, w_out16, y_a, y_b, nat["gm"], bgate8)
    dys = _dx([dyb16], [w_brb], [], "dx_yssm")
    g_w_out, = _dw(merged, [dpre16], BF16, "dw_out")
    g_w_br = jnp.concatenate([_dw(oa, [dya16], BF16, "dw_bra")[0], _dw(y_ssm, [dyb16], BF16, "dw_brb")[0]], axis=0)
    g_w_ple, = _dw(p16, [dpw16], BF16, "dw_ple")

    do_att, dgatt16, own_order = _combine_bwd(dya16, w_bra, nat["gatt"], o_att, lse, dils[1:])
    dseg = {"gatt": dgatt16, "gm": dgm16, "gp": dgp16}
    dbias = []
    for g in range(3):
        cotangent = (do_att, o_att, lse) if g == 0 else (own_order[2 * g - 2], own_order[2 * g - 1])
        dqkv, db = _attn_bwd(qkv[g], bias, g, cotangent, dils[g],
                             "attn_bwd%d" % g)
        dseg["qkv%d" % g] = dqkv.reshape(nb, seq, -1)
        dbias.append(db)
    g_rel = _bias_grad(jnp.concatenate(dbias, axis=0), bmaps)[:, 0, :NUM_BUCKETS].T

    dact, ddtg, dz, ssd_small, g_normw = _ssd_bwd(
        act, nat["dt"], dt_bias_row, nat["z"], y_all, dys, sprev, alog_g, dskip_g, ssm_norm_w)
    dseg["z"] = dz
    dseg["dt"] = jnp.pad(_ungroup_lanes(ddtg), ((0, 0), (0, 0), (0, LANE - SSM_HEADS)))
    dpre, conv_sums = _conv_bwd_pre(dact, nat["xbc"], conv_wg, conv_bg, "conv_bwd")
    dseg["xbc"] = _conv_bwd_x(dpre, conv_wg, "conv_bwd_x")
    csum = _xbc_reference_order(conv_sums)

    dx_own = [_dx([dseg["qkv%d" % g]], [wseg["qkv%d" % g]], [], "dx_qkv%d" % g, True).reshape(
        nb, dils[g], seq // dils[g], D_MODEL) for g in (1, 2)]
    dwseg = {"qkv%d" % g: _dw(x16p[g], [dseg["qkv%d" % g]], BF16, "dw_qkv%d" % g, True)[0] for g in range(3)}
    for gi, group in enumerate(DW_GROUPS):
        dwseg.update(zip(group, _dw(x16, [dseg[s] for s in group], BF16, "dw_nat%d" % gi, True)))
    names = ["qkv0"] + [s for group, _ in NAT_GROUPS for s in group]
    dx = finish_dx([dseg[s] for s in names], [wseg[s] for s in names], [dx], dx_own, dwseg, g_w_br, g_w_out, g_w_ple)

    small = dict(
        b_gate=jnp.stack([mg_sums[0], mg_sums[1], ln_sums[2]]),
        conv_w=csum[0:4], conv_b=csum[4:5],
        dt_bias=_ungroup_lanes(ssd_small[:, 2:3, :]), a_log=_ungroup_lanes(ssd_small[:, 0:1, :]),
        d_skip=_ungroup_lanes(ssd_small[:, 1:2, :]), ssm_norm_w=g_normw,
        ln_g=ln_sums[0:1], ln_b=ln_sums[1:2], rel_bias=g_rel)
    return loss_sum, dx, small


DX_TM = 256
SMALL_ORDER = ("b_gate", "conv_w", "conv_b", "dt_bias", "a_log", "d_skip", "ssm_norm_w", "ln_g", "ln_b", "rel_bias")
SMALL_FULL_SHAPES = dict(b_gate=(3, 1024), conv_w=(4, 3072), conv_b=(1, 3072), dt_bias=(1, 32), a_log=(1, 32),
                         d_skip=(1, 32), ssm_norm_w=(1, 2048), ln_g=(1, 1024), ln_b=(1, 1024), rel_bias=(32, 36))


def kernel(x, p, w_in, b_gate, conv_w, conv_b, dt_bias, a_log, d_skip, ssm_norm_w, w_branch, w_out, w_ple, ln_g, ln_b, rel_bias, loss_target, m_w_in, m_b_gate, m_conv_w, m_conv_b, m_dt_bias, m_a_log, m_d_skip, m_ssm_norm_w, m_w_branch, m_w_out, m_w_ple, m_ln_g, m_ln_b, m_rel_bias, v_w_in, v_b_gate, v_conv_w, v_conv_b, v_dt_bias, v_a_log, v_d_skip, v_ssm_norm_w, v_w_branch, v_w_out, v_w_ple, v_ln_g, v_ln_b, v_rel_bias):
    cx, cy, cc = _place()
    chip = 2 * cx + cy
    dev = 4 * cx + 2 * cy + cc

    w_in_t = jnp.transpose(w_in[0])
    win16 = _shard_to_window(w_in_t, chip)
    g_win, g_br, g_out, g_ple = _allgather_pieces(
        [win16, w_branch[0].astype(BF16), w_out[0].astype(BF16), w_ple[0].astype(BF16)])
    wseg = _assemble(g_win)
    w_br16 = g_br.reshape(4 * 704, D_MODEL)
    w_out16 = g_out.reshape(D_MODEL, D_MODEL)
    w_ple16 = jnp.transpose(g_ple, (1, 0, 2)).reshape(PLE_DIM, D_MODEL)
    shards = _allgather8(_pack_rows([b_gate[0], conv_w[0]]), "allgather_small_params")
    per_chip = [_unpack_rows(shards[2 * k], [(3, 256), (4, 768)]) for k in range(4)]
    b_gate_full = _join_last([pc[0] for pc in per_chip])
    conv_w_full = _join_last([pc[1] for pc in per_chip])

    place = jnp.stack([chip, cc]).astype(jnp.int32)
    reduced = []

    def finish_dx(dhs, ws, accs, own_order_accs, dwseg, d_br, d_out, d_ple):
        grads = [_pack(dwseg), d_br.reshape(4, 704, D_MODEL), d_out.reshape(4, 256, D_MODEL),
                 jnp.transpose(d_ple.reshape(PLE_DIM, 4, 256), (1, 0, 2))]
        got = _sibling_exchange(grads)
        chip_sums = [_pair_sum(g, t, place, "grad_pair_sum_%d" % i) for i, (g, t) in enumerate(zip(grads, got))]
        dx, others = _dx(dhs, ws, accs, "dx_w_in_and_grad_chip_scatter", True, DX_TM, chip_sums, own_order_accs)
        fulls = [_chip_sum(s, t, place, "grad_chip_sum_%d" % i) for i, (s, t) in enumerate(zip(chip_sums, others))]
        reduced.extend(_sibling_gather(fulls))
        return dx

    loss_sum, grad_x, small = _local_grads(
        x, p[0], loss_target, wseg, w_br16, w_out16, w_ple16, b_gate_full, conv_w_full, conv_b, dt_bias, a_log,
        d_skip, ssm_norm_w, ln_g, ln_b, rel_bias, finish_dx)
    big = reduced
    g_w_in = lax.optimization_barrier(_window_to_shard(big[0], chip))
    g_w_branch, g_w_out, g_w_ple = big[1], big[2], big[3]
    parts = _allgather8(_pack_rows([small[n] for n in SMALL_ORDER] + [loss_sum.reshape(1, 1)]),
                        "allgather_small_grads")
    small_sum = _sum_rows([parts[i] for i in range(8)], F32, "small_grad_sum")
    *reduced_small, loss = _unpack_rows(small_sum, [SMALL_FULL_SHAPES[n] for n in SMALL_ORDER] + [(1, 1)])
    loss = loss.reshape(())
    sg = dict(zip(SMALL_ORDER, reduced_small))
    sg["b_gate"] = lax.dynamic_slice_in_dim(sg["b_gate"], chip * 256, 256, axis=1)
    sg["conv_w"] = lax.dynamic_slice_in_dim(sg["conv_w"], chip * 768, 768, axis=1)
    del dev

    upd = {}
    upd["w_in"] = [jnp.transpose(t) for t in _adamw(w_in_t, g_w_in, jnp.transpose(m_w_in[0]),
                                                      jnp.transpose(v_w_in[0]), "adamw_w_in")]
    upd["w_branch"] = _adamw(w_branch[0], g_w_branch, m_w_branch[0], v_w_branch[0], "adamw_w_branch")
    upd["w_out"] = _adamw(w_out[0], g_w_out, m_w_out[0], v_w_out[0], "adamw_w_out")
    upd["w_ple"] = _adamw(w_ple[0], g_w_ple, m_w_ple[0], v_w_ple[0], "adamw_w_ple")
    small_w = dict(b_gate=b_gate, conv_w=conv_w, conv_b=conv_b, dt_bias=dt_bias, a_log=a_log, d_skip=d_skip,
                   ssm_norm_w=ssm_norm_w, ln_g=ln_g, ln_b=ln_b, rel_bias=rel_bias)
    small_m = dict(b_gate=m_b_gate, conv_w=m_conv_w, conv_b=m_conv_b, dt_bias=m_dt_bias, a_log=m_a_log,
                   d_skip=m_d_skip, ssm_norm_w=m_ssm_norm_w, ln_g=m_ln_g, ln_b=m_ln_b, rel_bias=m_rel_bias)
    small_v = dict(b_gate=v_b_gate, conv_w=v_conv_w, conv_b=v_conv_b, dt_bias=v_dt_bias, a_log=v_a_log,
                   d_skip=v_d_skip, ssm_norm_w=v_ssm_norm_w, ln_g=v_ln_g, ln_b=v_ln_b, rel_bias=v_rel_bias)
    shapes = [small_w[n].shape for n in SMALL_ORDER]
    s_delta, s_m, s_v = _adamw(_pack_rows([small_w[n] for n in SMALL_ORDER]), _pack_rows([sg[n] for n in SMALL_ORDER]),
                               _pack_rows([small_m[n] for n in SMALL_ORDER]), _pack_rows([small_v[n] for n in SMALL_ORDER]),
                               "adamw_small")
    for i, n in enumerate(SMALL_ORDER):
        upd[n] = tuple(_unpack_rows(t, shapes)[i] for t in (s_delta, s_m, s_v))
        sg[n] = sg[n].reshape(small_w[n].shape)

    order = ("w_in", "b_gate", "conv_w", "conv_b", "dt_bias", "a_log", "d_skip", "ssm_norm_w", "w_branch", "w_out",
             "w_ple", "ln_g", "ln_b", "rel_bias")
    grads = dict(sg, w_in=jnp.transpose(g_w_in)[None],w_branch=g_w_branch[None], w_out=g_w_out[None], w_ple=g_w_ple[None])
    lead = lambda n, t: t[None] if n in ("w_in", "w_branch", "w_out", "w_ple") else t
    return (loss, grad_x, *[grads[n] for n in order], *[lead(n, upd[n][0]) for n in order],
            *[lead(n, upd[n][1]) for n in order], *[lead(n, upd[n][2]) for n in order])
```

```python
import math

import numpy as np
import jax
import jax.numpy as jnp
from jax import lax
from jax.experimental import pallas as pl
from jax.experimental.pallas import tpu as pltpu

F32, BF16 = jnp.float32, jnp.bfloat16

D_MODEL = 1024
HEAD_DIM = 64
GROUP_HEADS = 12
ATT_OUT = GROUP_HEADS * HEAD_DIM
PATTERNS = ((128, 1), (512, 4), (2048, 16))
BAND = 128
NUM_BUCKETS = 32
MAX_DISTANCE = 2048
D_INNER = 2048
SSM_HEADS = 32
SSM_GROUPS = 4
GROUP_SSM_HEADS = SSM_HEADS // SSM_GROUPS
D_STATE = 128
CHUNK = 128
PLE_DIM = 256
ALPHA = 2.0 ** 0.25
LN_EPS = 1e-5
RMS_EPS = 1e-5
ADAM_LR, ADAM_B1, ADAM_B2, ADAM_EPS, ADAM_WD, ADAM_STEP = 0.001, 0.9, 0.999, 1e-08, 0.01, 10
NEG = -1e30

QKV_W = 3 * ATT_OUT
IN_COLS = 15904
SHARD_COLS = IN_COLS // 4
DT_COL = 12800
ROW_TILE = 16
WIN_ROWS = 4000


def _win_offset(k):
    return (k * SHARD_COLS) % ROW_TILE


def _win_start(k):
    return k * SHARD_COLS - _win_offset(k)

VMEM_LIMIT_BYTES = 56 * 1024 * 1024
LANE = 128
MESH = pl.DeviceIdType.MESH
NT = (((1,), (1,)), ((), ()))
TN = (((0,), (0,)), ((), ()))


def _pcall(body, **kw):
    return pl.pallas_call(body, **kw)


def _params(*sem):
    return pltpu.CompilerParams(dimension_semantics=sem, vmem_limit_bytes=VMEM_LIMIT_BYTES)


def _sigmoid(v):
    return jax.nn.sigmoid(v)


MM_TM = 512


def _tok_spec(tm, width):
    return pl.BlockSpec((None, tm, width), lambda b, i: (b, i, 0))


def _whole(arr, single_buffer=False):
    mode = dict(pipeline_mode=pl.Buffered(1)) if single_buffer else {}
    return pl.BlockSpec(arr.shape, lambda b, i: (0,) * arr.ndim, **mode)


def _proj(a3, ws, out_dtype, name, w_rows_are_outputs=False, tm=MM_TM):
    nb, seq, kdim = a3.shape
    nw = len(ws)
    widths = [w.shape[0] if w_rows_are_outputs else w.shape[1] for w in ws]

    def body(*refs):
        a = refs[0][...].astype(BF16)
        for w_ref, o_ref in zip(refs[1:1 + nw], refs[1 + nw:]):
            if w_rows_are_outputs:
                v = lax.dot_general(a, w_ref[...], NT, preferred_element_type=F32)
            else:
                v = jnp.dot(a, w_ref[...], preferred_element_type=F32)
            o_ref[...] = v.astype(out_dtype)

    return _pcall(
        body, name=name, grid=(nb, seq // tm),
        in_specs=[_tok_spec(tm, kdim)] + [_whole(w, True) for w in ws],
        out_specs=[_tok_spec(tm, n) for n in widths],
        out_shape=[jax.ShapeDtypeStruct((nb, seq, n), out_dtype) for n in widths],
        compiler_params=_params("parallel", "parallel"),
    )(a3, *ws)


def _dx(dhs, ws, accs, name, w_rows_are_outputs=False, tm=MM_TM, scatter=None, own_order_accs=()):
    nb, seq, _ = dhs[0].shape
    nd, nacc, npa = len(dhs), len(accs), len(own_order_accs)
    kout = ws[0].shape[1] if w_rows_are_outputs else ws[0].shape[0]
    sums = scatter or []
    ns = len(sums)
    chunks = [_row_chunks(s.shape[1], s.shape[2] * s.dtype.itemsize, ICI_CHUNK_BYTES) for s in sums]
    n_sem = 3 * sum(len(ch) for ch in chunks)
    grid = (nb, seq // tm)
    ntile = kout // LANE if npa else 0

    def body(*refs):
        n_in = 2 * nd + nacc + npa
        sum_refs, o_ref, got_refs = refs[n_in:n_in + ns], refs[n_in + ns], refs[n_in + ns + 1:n_in + 2 * ns + 1]
        tile_refs = refs[n_in + 2 * ns + 1:n_in + 2 * ns + 1 + ntile]

        def copies():
            send_sems, recv_sems = refs[-2], refs[-1]
            x, y, c = _place()
            out = []
            for a in range(ns):
                for s, m in chunks[a]:
                    for j, (cx, cy) in enumerate(_other_chips(x, y)):
                        k = len(out)
                        out.append(_remote(sum_refs[a].at[2 * cx + cy, pl.ds(s, m)], got_refs[a].at[j, pl.ds(s, m)],
                                           send_sems.at[k], recv_sems.at[k], (cx, cy, c)))
            return out

        if ns:
            @pl.when((pl.program_id(0) == 0) & (pl.program_id(1) == 0))
            def _():
                for cp in copies():
                    cp.start()

        v = None
        for dh_ref, w_ref in zip(refs[:nd], refs[nd:2 * nd]):
            dh = dh_ref[...].astype(BF16)
            if w_rows_are_outputs:
                t = jnp.dot(dh, w_ref[...], preferred_element_type=F32)
            else:
                t = lax.dot_general(dh, w_ref[...], NT, preferred_element_type=F32)
            v = t if v is None else v + t
        for a_ref in refs[2 * nd:2 * nd + nacc]:
            v = v + a_ref[...]
        for p_ref in refs[2 * nd + nacc:n_in]:
            v = v + _natural_rows(p_ref, tile_refs)
        o_ref[...] = v

        if ns:
            @pl.when((pl.program_id(0) == grid[0] - 1) & (pl.program_id(1) == grid[1] - 1))
            def _():
                for cp in copies():
                    cp.wait()

    out = _pcall(
        body, name=name, grid=grid,
        in_specs=[_tok_spec(tm, dh.shape[-1]) for dh in dhs] + [_whole(w, True) for w in ws]
        + [_tok_spec(tm, kout)] * nacc
        + [pl.BlockSpec((None, p.shape[1], tm // p.shape[1], kout), lambda b, i: (b, 0, i, 0)) for p in own_order_accs]
        + [ANY] * ns,
        out_specs=[_tok_spec(tm, kout)] + [ANY] * ns,
        out_shape=[jax.ShapeDtypeStruct((nb, seq, kout), F32)]
        + [jax.ShapeDtypeStruct((3,) + s.shape[1:], s.dtype) for s in sums],
        input_output_aliases={2 * nd: 0} if nacc else {},
        scratch_shapes=[pltpu.VMEM((tm, LANE), F32)] * ntile
        + ([pltpu.SemaphoreType.DMA((n_sem,)), pltpu.SemaphoreType.DMA((n_sem,))] if ns else []),
        compiler_params=pltpu.CompilerParams(
            dimension_semantics=("arbitrary", "arbitrary") if ns else ("parallel", "parallel"),
            vmem_limit_bytes=VMEM_LIMIT_BYTES, has_side_effects=bool(ns)),
    )(*dhs, *ws, *accs, *own_order_accs, *sums)
    return (out[0], list(out[1:])) if ns else out[0]


def _dw(a3, dhs, out_dtype, name, rows_are_outputs=False):
    nb, seq, kdim = a3.shape
    nd = len(dhs)
    grid = (nb, seq // MM_TM)
    shapes = [(dh.shape[-1], kdim) if rows_are_outputs else (kdim, dh.shape[-1]) for dh in dhs]

    def body(*refs):
        b, i = pl.program_id(0), pl.program_id(1)
        dh_refs, o_refs, acc_refs = refs[1:1 + nd], refs[1 + nd:1 + 2 * nd], refs[1 + 2 * nd:]

        @pl.when((b == 0) & (i == 0))
        def _():
            for acc_ref in acc_refs:
                acc_ref[...] = jnp.zeros_like(acc_ref)

        a = refs[0][...].astype(BF16)
        for dh_ref, acc_ref in zip(dh_refs, acc_refs):
            dh = dh_ref[...].astype(BF16)
            acc_ref[...] += lax.dot_general(*((dh, a) if rows_are_outputs else (a, dh)), TN,
                                            preferred_element_type=F32)

        @pl.when((b == grid[0] - 1) & (i == grid[1] - 1))
        def _():
            for o_ref, acc_ref in zip(o_refs, acc_refs):
                o_ref[...] = acc_ref[...].astype(out_dtype)

    return _pcall(
        body, name=name, grid=grid,
        in_specs=[_tok_spec(MM_TM, kdim)] + [_tok_spec(MM_TM, dh.shape[-1]) for dh in dhs],
        out_specs=[pl.BlockSpec(s, lambda b, i: (0, 0)) for s in shapes],
        out_shape=[jax.ShapeDtypeStruct(s, out_dtype) for s in shapes],
        scratch_shapes=[pltpu.VMEM(s, F32) for s in shapes],
        compiler_params=_params("arbitrary", "arbitrary"),
    )(a3, *dhs)


def _qkv_rows(g):
    return [(part * QKV_W + g * ATT_OUT + hp * LANE, LANE) for hp in range(ATT_OUT // LANE) for part in range(3)]


XBC_START = 3 * QKV_W + ATT_OUT + D_INNER
GROUP_CH = GROUP_SSM_HEADS * HEAD_DIM
XBC_GROUP = GROUP_CH + 2 * D_STATE
CONV_DIM = SSM_GROUPS * XBC_GROUP


def _xbc_ranges():
    out = []
    for g in range(SSM_GROUPS):
        out += [(g * GROUP_CH, GROUP_CH), (D_INNER + g * D_STATE, D_STATE),
                (D_INNER + SSM_GROUPS * D_STATE + g * D_STATE, D_STATE)]
    return out


def _join_last(parts):
    widths = [t.shape[-1] for t in parts]
    total, lead = sum(widths), [(0, 0)] * (parts[0].ndim - 1)
    starts = np.cumsum([0] + widths)
    return sum(jnp.pad(t, lead + [(int(s), total - int(s) - w)]) for t, s, w in zip(parts, starts, widths))


def _xbc_group_order(t):
    return _join_last([t[..., s:s + n] for s, n in _xbc_ranges()])


def _xbc_reference_order(t):
    g = lambda off, n: [t[..., k * XBC_GROUP + off:k * XBC_GROUP + off + n] for k in range(SSM_GROUPS)]
    return _join_last(g(0, GROUP_CH) + g(GROUP_CH, D_STATE) + g(GROUP_CH + D_STATE, D_STATE))


def _segments():
    one = lambda name, start, rows: (name, [(start, rows)], max(rows, LANE))
    return [("qkv%d" % g, _qkv_rows(g), QKV_W) for g in range(3)] + [
        one("gatt", 3 * QKV_W, ATT_OUT), one("z", 3 * QKV_W + ATT_OUT, D_INNER),
        ("xbc", [(XBC_START + s, n) for s, n in _xbc_ranges()], CONV_DIM), one("dt", DT_COL, SSM_HEADS),
        one("gm", DT_COL + SSM_HEADS, 2 * D_MODEL), one("gp", DT_COL + SSM_HEADS + 2 * D_MODEL, D_MODEL)]


LAYOUT_TC = 256
NAT_GROUPS = ((("gatt", "z", "dt", "gp"), 512), (("xbc", "gm"), 512))
DW_GROUPS = (("gatt", "z", "dt", "gp"), ("xbc",), ("gm",))


def _assemble(win):
    segs = _segments()

    def body(win_ref, *outs):
        def pieces(start, rows):
            t, end = start, start + rows
            while t < end:
                k = min(t // SHARD_COLS, 3)
                shard_end = (k + 1) * SHARD_COLS
                if k < 3 and shard_end % ROW_TILE and t == shard_end - shard_end % ROW_TILE:
                    lo = t - _win_start(k)
                    yield win_ref[k, lo:lo + ROW_TILE, :] + win_ref[k + 1, 0:ROW_TILE, :]
                    t += ROW_TILE
                    continue
                upto = min(end, shard_end - shard_end % ROW_TILE if k < 3 else end)
                yield win_ref[k, t - _win_start(k):upto - _win_start(k), :]
                t = upto

        for (_, ranges, total), o_ref in zip(segs, outs):
            off = 0
            for start, rows in ranges:
                for part in pieces(start, rows):
                    o_ref[off:off + part.shape[0], :] = part
                    off += part.shape[0]
            if off < total:
                o_ref[off:total, :] = jnp.zeros((total - off, o_ref.shape[1]), BF16)

    outs = _pcall(
        body, name="assemble_w_in", grid=(D_MODEL // LAYOUT_TC,),
        in_specs=[pl.BlockSpec((4, WIN_ROWS, LAYOUT_TC), lambda i: (0, 0, i))],
        out_specs=[pl.BlockSpec((total, LAYOUT_TC), lambda i: (0, i)) for _, _, total in segs],
        out_shape=[jax.ShapeDtypeStruct((total, D_MODEL), BF16) for _, _, total in segs],
        compiler_params=_params("parallel"),
    )(win)
    return {name: o for (name, _, _), o in zip(segs, outs)}


def _pack(dsegs):
    segs = _segments()

    def body(*refs):
        ins, o_ref = refs[:-1], refs[-1]
        tail = IN_COLS - _win_start(3)
        o_ref[3, tail:, :] = jnp.zeros((WIN_ROWS - tail, o_ref.shape[2]), BF16)
        for (_, ranges, _), s_ref in zip(segs, ins):
            off = 0
            for start, rows in ranges:
                for k in range(4):
                    lo = _win_start(k)
                    a, b = max(start, lo), min(start + rows, lo + WIN_ROWS)
                    if a < b:
                        o_ref[k, a - lo:b - lo, :] = s_ref[off + a - start:off + b - start, :]
                off += rows

    return _pcall(
        body, name="pack_dw_in", grid=(D_MODEL // LAYOUT_TC,),
        in_specs=[pl.BlockSpec((total, LAYOUT_TC), lambda i: (0, i)) for _, _, total in segs],
        out_specs=pl.BlockSpec((4, WIN_ROWS, LAYOUT_TC), lambda i: (0, 0, i)),
        out_shape=jax.ShapeDtypeStruct((4, WIN_ROWS, D_MODEL), BF16),
        compiler_params=_params("parallel"),
    )(*[dsegs[name] for name, _, _ in segs])


def _shard_to_window(shard_t, k):
    def at(off):
        return lambda w: jnp.pad(w.astype(BF16), ((off, WIN_ROWS - SHARD_COLS - off), (0, 0)))

    return lax.cond(k % 2 == 1, at(_win_offset(1)), at(_win_offset(0)), shard_t)


def _window_to_shard(win, k):
    return lax.dynamic_slice(win, ((k % 2) * _win_offset(1), 0), (SHARD_COLS, D_MODEL))


def _bucket_maps():
    qi = np.arange(8)[:, None]
    kj = np.arange(2 * BAND)[None, :]
    delta = qi + BAND - kj
    maps = []
    for window, dil in PATTERNS:
        valid = (delta >= 0) & (delta <= window // dil)
        dist = np.maximum(delta, 0) * dil
        max_exact = NUM_BUCKETS // 2
        d_f = np.maximum(dist, 1).astype(np.float32)
        large = max_exact + (np.log(d_f / np.float32(max_exact)) / np.float32(math.log(MAX_DISTANCE / max_exact))
                             * np.float32(NUM_BUCKETS - max_exact)).astype(np.int32)
        large = np.minimum(large, NUM_BUCKETS - 1)
        bucket = np.where(dist < max_exact, dist, large)
        maps.append(np.where(valid, bucket, -1).astype(np.int32))
    return np.stack(maps)


def _bias_tables(rel_bias, bmaps):
    def body(rb_ref, bm_ref, o_ref):
        g = pl.program_id(0)
        bm = bm_ref[...]
        for hh in range(GROUP_HEADS):
            acc = jnp.full(bm.shape, NEG, F32)
            for b in range(NUM_BUCKETS):
                acc = jnp.where(bm == b, rb_ref[b, g * GROUP_HEADS + hh], acc)
            for a in range(BAND // 8):
                o_ref[hh, 8 * a:8 * a + 8, :] = acc if a == 0 else pltpu.roll(acc, 8 * a, 1)

    return _pcall(
        body, name="bias_tables", grid=(3,),
        in_specs=[pl.BlockSpec(memory_space=pltpu.SMEM),
                  pl.BlockSpec((None, 8, 2 * BAND), lambda g: (g, 0, 0))],
        out_specs=pl.BlockSpec((GROUP_HEADS, BAND, 2 * BAND), lambda g: (g, 0, 0)),
        out_shape=jax.ShapeDtypeStruct((3 * GROUP_HEADS, BAND, 2 * BAND), F32),
        compiler_params=_params("parallel"),
    )(rel_bias, bmaps)


def _bias_grad(dbias, bmaps):
    def body(db_ref, bm_ref, o_ref):
        bm = bm_ref[...]
        lane = lax.broadcasted_iota(jnp.int32, (1, LANE), 1)
        for hh in range(GROUP_HEADS):
            db = db_ref[hh, 0:8, :]
            for a in range(1, BAND // 8):
                db = db + pltpu.roll(db_ref[hh, 8 * a:8 * a + 8, :], 2 * BAND - 8 * a, 1)
            vec = jnp.zeros((1, LANE), F32)
            for b in range(NUM_BUCKETS):
                s = jnp.sum(jnp.where(bm == b, db, 0.0), keepdims=True)
                vec = jnp.where(lane == b, s, vec)
            o_ref[hh] = vec

    return _pcall(
        body, name="bias_grad", grid=(3,),
        in_specs=[pl.BlockSpec((GROUP_HEADS, BAND, 2 * BAND), lambda g: (g, 0, 0)),
                  pl.BlockSpec((None, 8, 2 * BAND), lambda g: (g, 0, 0))],
        out_specs=pl.BlockSpec((GROUP_HEADS, 1, LANE), lambda g: (g, 0, 0)),
        out_shape=jax.ShapeDtypeStruct((3 * GROUP_HEADS, 1, LANE), F32),
        compiler_params=_params("parallel"),
    )(dbias, bmaps)


def _rows(n):
    if isinstance(n, int):
        return pl.ds(n * BAND, BAND)
    return pl.ds(pl.multiple_of(n * BAND, BAND), BAND)


def _for_blocks(blocks, nblk, per, carry):
    carry = blocks([0], carry, False)
    start = 1 + (nblk - 1) % per
    for n in range(1, start):
        carry = blocks([n], carry, True)
    trips = (nblk - start) // per
    if trips > 0:
        carry = lax.fori_loop(
            0, trips, lambda t, c: blocks([start + t * per + u for u in range(per)], c, True), carry)
    return carry


def _pairs_per_step(d):
    return {1: 3, 4: 6, 16: 6}[d]


def _bias_spec(group, hps):
    first = group * GROUP_HEADS // (2 * hps)
    return pl.BlockSpec((2 * hps, BAND, 2 * BAND), lambda hp, b, r: (first + hp, 0, 0))


def _attn_fwd(qkv4, bias, group, d, name):
    nb, _, sub, _ = qkv4.shape
    nblk = sub // BAND
    scale = HEAD_DIM ** -0.5
    npair = ATT_OUT // LANE
    hps = _pairs_per_step(d)
    compact = d > 1

    def body(qkv_ref, bias_ref, o_ref, l_ref):
        def blocks(ns, carry, with_prev):
            chains = [(bi, i, h) for bi in range(len(ns)) for i in range(hps) for h in range(2)]
            first_head = lax.broadcasted_iota(jnp.int32, (BAND, LANE), 1) < HEAD_DIM
            pair = lambda n, i, part: qkv_ref[_rows(n), (3 * i + part) * LANE:(3 * i + part + 1) * LANE]
            scores = []
            for bi, i, h in chains:
                n = ns[bi]
                qp = pair(n, i, 0) * scale
                q = jnp.where(first_head if h == 0 else jnp.logical_not(first_head), qp, jnp.zeros_like(qp))
                s_c = lax.dot_general(q, pair(n, i, 1), NT, preferred_element_type=F32) + bias_ref[2 * i + h, :, BAND:]
                s_p = None
                if with_prev:
                    s_p = lax.dot_general(q, pair(n - 1, i, 1), NT,
                                          preferred_element_type=F32) + bias_ref[2 * i + h, :, :BAND]
                scores.append((s_c, s_p))
            probs = []
            for s_c, s_p in scores:
                m = jnp.max(s_c, -1, keepdims=True)
                if with_prev:
                    m = jnp.maximum(m, jnp.max(s_p, -1, keepdims=True))
                e_c = jnp.exp(s_c - m)
                den = jnp.sum(e_c, -1, keepdims=True)
                e_p = None
                if with_prev:
                    e_p = jnp.exp(s_p - m)
                    den = den + jnp.sum(e_p, -1, keepdims=True)
                    e_p = e_p.astype(BF16)
                probs.append((e_c.astype(BF16), e_p, den, m))
            outs = {}
            for (bi, i, h), (e_c, e_p, den, m) in zip(chains, probs):
                n = ns[bi]
                acc = jnp.dot(e_c, pair(n, i, 2), preferred_element_type=F32)
                if with_prev:
                    acc = acc + jnp.dot(e_p, pair(n - 1, i, 2), preferred_element_type=F32)
                outs[(bi, i, h)] = (acc / den, m + jnp.log(den))
            lane = lax.broadcasted_iota(jnp.int32, (BAND, LANE), 1)
            for bi, n in enumerate(ns):
                per_head = jnp.zeros((BAND, LANE), F32)
                for i in range(hps):
                    o_ref[_rows(n), i * LANE:(i + 1) * LANE] = jnp.where(first_head, outs[(bi, i, 0)][0],
                                                                         outs[(bi, i, 1)][0])
                    if compact:
                        for h in range(2):
                            per_head = jnp.where(lane == 2 * i + h, outs[(bi, i, h)][1], per_head)
                    else:
                        l_ref[_rows(n), i * LANE:(i + 1) * LANE] = jnp.where(first_head, outs[(bi, i, 0)][1],
                                                                             outs[(bi, i, 1)][1])
                if compact:
                    l_ref[_rows(n), :] = per_head
            return carry

        _for_blocks(blocks, nblk, 2 if hps == 1 else 1, 0)

    in_specs = [pl.BlockSpec((None, None, sub, 3 * LANE * hps), lambda hp, b, r: (b, r, 0, hp)),
                _bias_spec(group, hps)]
    if compact:
        return _pcall(
            body, name=name, grid=(1, nb, d), in_specs=in_specs,
            out_specs=[pl.BlockSpec((None, None, sub, ATT_OUT), lambda hp, b, r: (b, r, 0, 0)),
                       pl.BlockSpec((None, None, sub, LANE), lambda hp, b, r: (b, r, 0, 0))],
            out_shape=[jax.ShapeDtypeStruct((nb, d, sub, ATT_OUT), F32), jax.ShapeDtypeStruct((nb, d, sub, LANE), F32)],
            compiler_params=_params("parallel", "parallel", "parallel"),
        )(qkv4, bias)
    ospec = pl.BlockSpec((None, sub, hps * LANE), lambda hp, b, r: (b, 0, r * (npair // hps) + hp))
    return _pcall(
        body, name=name, grid=(npair // hps, nb, d), in_specs=in_specs, out_specs=[ospec, ospec],
        out_shape=[jax.ShapeDtypeStruct((nb, sub, d * ATT_OUT), F32)] * 2,
        compiler_params=_params("parallel", "parallel", "parallel"),
    )(qkv4, bias)


STAT_LSE_LANE = 16


def _attn_bwd(qkv4, bias, group, cotangent, d, name):
    nb, _, sub, _ = qkv4.shape
    nblk = sub // BAND
    scale = HEAD_DIM ** -0.5
    npair = ATT_OUT // LANE
    hps = _pairs_per_step(d)
    compact = d > 1

    def body(qkv_ref, bias_ref, *rest):
        do_ref, dqkv_ref, db_ref = rest[0], rest[-2], rest[-1]
        b, r = pl.program_id(1), pl.program_id(2)

        @pl.when((b == 0) & (r == 0))
        def _():
            db_ref[...] = jnp.zeros_like(db_ref)

        def blocks(ns, carry, with_prev):
            sides = (0, 1) if with_prev else (0,)
            chains = [(bi, i, h, sd) for bi in range(len(ns)) for i in range(hps) for h in range(2) for sd in sides]
            first_head = lax.broadcasted_iota(jnp.int32, (BAND, LANE), 1) < HEAD_DIM
            own = lambda h, t: jnp.where(first_head if h == 0 else jnp.logical_not(first_head), t, jnp.zeros_like(t))
            pair = lambda rows, i, part: qkv_ref[rows, (3 * i + part) * LANE:(3 * i + part + 1) * LANE]
            key_rows = lambda bi, sd: _rows(ns[bi] - sd)
            qs = {}
            for bi in range(len(ns)):
                for i in range(hps):
                    q_pair = pair(_rows(ns[bi]), i, 0) * scale
                    do = do_ref[_rows(ns[bi]), i * LANE:(i + 1) * LANE]
                    do16 = do.astype(BF16)
                    for h in range(2):
                        if compact:
                            st_ref, head = rest[1], 2 * i + h
                            ebar = st_ref[_rows(ns[bi]), head:head + 1]
                            lcol = st_ref[_rows(ns[bi]), STAT_LSE_LANE + head:STAT_LSE_LANE + head + 1]
                        else:
                            ebar = jnp.sum(own(h, do * rest[1][_rows(ns[bi]), i * LANE:(i + 1) * LANE]), -1, keepdims=True)
                            lcol = rest[2][_rows(ns[bi]), i * LANE + h * HEAD_DIM:i * LANE + h * HEAD_DIM + 1]
                        qs[(bi, i, h)] = (own(h, q_pair), q_pair, own(h, do16), do16, ebar, lcol)
            raw = []
            for bi, i, h, sd in chains:
                q, _, do_h, _, _, _ = qs[(bi, i, h)]
                bias_blk = bias_ref[2 * i + h, :, :BAND] if sd else bias_ref[2 * i + h, :, BAND:]
                s = lax.dot_general(q, pair(key_rows(bi, sd), i, 1), NT, preferred_element_type=F32) + bias_blk
                dp = lax.dot_general(do_h, pair(key_rows(bi, sd), i, 2), NT, preferred_element_type=F32)
                raw.append((s, dp))
            soft = []
            for (bi, i, h, sd), (s, dp) in zip(chains, raw):
                ebar, lcol = qs[(bi, i, h)][4:]
                p = jnp.exp(s - lcol)
                ds = p * (dp - ebar)
                if sd:
                    db_ref[2 * i + h, :, :BAND] += ds
                else:
                    db_ref[2 * i + h, :, BAND:] += ds
                soft.append((p.astype(BF16), ds.astype(BF16)))
            grads = {}
            for (bi, i, h, sd), (p16, ds16) in zip(chains, soft):
                _, q_pair, _, do16 = qs[(bi, i, h)][:4]
                grads[(bi, i, h, sd)] = (
                    jnp.dot(ds16, pair(key_rows(bi, sd), i, 1), preferred_element_type=F32),
                    lax.dot_general(ds16, q_pair, TN, preferred_element_type=F32),
                    lax.dot_general(p16, do16, TN, preferred_element_type=F32))
            both = lambda bi, i, sd, which: jnp.where(first_head, grads[(bi, i, 0, sd)][which],
                                                      grads[(bi, i, 1, sd)][which])
            carry = list(carry) if carry is not None else None
            for bi, n in enumerate(ns):
                for i in range(hps):
                    base = 3 * LANE * i
                    dq = both(bi, i, 0, 0)
                    if with_prev:
                        dq = dq + both(bi, i, 1, 0)
                        dqkv_ref[_rows(n - 1), base + LANE:base + 2 * LANE] = (
                            carry[2 * i] + both(bi, i, 1, 1)).astype(BF16)
                        dqkv_ref[_rows(n - 1), base + 2 * LANE:base + 3 * LANE] = (
                            carry[2 * i + 1] + both(bi, i, 1, 2)).astype(BF16)
                    dqkv_ref[_rows(n), base:base + LANE] = (dq * scale).astype(BF16)
                carry = [t for i in range(hps) for t in (both(bi, i, 0, 1), both(bi, i, 0, 2))]
            return tuple(carry)

        carry = _for_blocks(blocks, nblk, 2 if hps == 1 else 1, None)
        for i in range(hps):
            base = 3 * LANE * i
            dqkv_ref[_rows(nblk - 1), base + LANE:base + 2 * LANE] = carry[2 * i].astype(BF16)
            dqkv_ref[_rows(nblk - 1), base + 2 * LANE:base + 3 * LANE] = carry[2 * i + 1].astype(BF16)

    qspec = pl.BlockSpec((None, None, sub, 3 * LANE * hps), lambda hp, b, r: (b, r, 0, hp))
    bspec = pl.BlockSpec((2 * hps, BAND, 2 * BAND), lambda hp, b, r: (hp, 0, 0))
    if compact:
        cspecs = [pl.BlockSpec((None, None, sub, ATT_OUT), lambda hp, b, r: (b, r, 0, 0)),
                  pl.BlockSpec((None, None, sub, LANE), lambda hp, b, r: (b, r, 0, 0))]
    else:
        cspecs = [pl.BlockSpec((None, sub, hps * LANE), lambda hp, b, r: (b, 0, r * (npair // hps) + hp))] * 3
    return _pcall(
        body, name=name, grid=(npair // hps, nb, d),
        in_specs=[qspec, _bias_spec(group, hps)] + cspecs, out_specs=[qspec, bspec],
        out_shape=[jax.ShapeDtypeStruct(qkv4.shape, BF16),
                   jax.ShapeDtypeStruct((GROUP_HEADS, BAND, 2 * BAND), F32)],
        compiler_params=_params("parallel", "arbitrary", "arbitrary"),
    )(qkv4, bias, *cotangent)


def _head_lanes(first_lane, one_channel):
    c = lax.broadcasted_iota(jnp.int32, (ATT_OUT, LANE), 0)
    lane = lax.broadcasted_iota(jnp.int32, (ATT_OUT, LANE), 1)
    hit = lane == first_lane + c // HEAD_DIM
    if one_channel:
        hit = hit & (c % HEAD_DIM == 0)
    return hit.astype(BF16)


def _exact_dot(v, m01, dims=None):
    parts = _split3(v)
    if dims is None:
        dot = lambda t: jnp.dot(t, m01, preferred_element_type=F32)
    else:
        dot = lambda t: lax.dot_general(t, m01, dims, preferred_element_type=F32)
    return (dot(parts[0]) + dot(parts[1])) + dot(parts[2])


def _store_own_order(value, tile_refs, out_ref):
    d, per, width = out_ref.shape
    for j in range(width // LANE):
        tile_refs[j][...] = value[:, j * LANE:(j + 1) * LANE]
    for r in range(d):
        rows = pl.ds(r, per, stride=d)
        for j in range(width // LANE):
            out_ref[r, :, j * LANE:(j + 1) * LANE] = tile_refs[j][rows, :].astype(out_ref.dtype)


def _token_orders(x, dilations):
    nb, seq, kdim = x.shape
    tm = 512

    def body(x_ref, nat_ref, *rest):
        outs, tile_refs = rest[:len(dilations)], rest[len(dilations):]
        xv = x_ref[...]
        nat_ref[...] = xv.astype(BF16)
        for o_ref in outs:
            _store_own_order(xv, tile_refs, o_ref)

    outs = _pcall(
        body, name="token_orders", grid=(nb, seq // tm), in_specs=[_tok_spec(tm, kdim)],
        out_specs=[_tok_spec(tm, kdim)]
        + [pl.BlockSpec((None, d, tm // d, kdim), lambda b, i: (b, 0, i, 0)) for d in dilations],
        out_shape=[jax.ShapeDtypeStruct((nb, seq, kdim), BF16)]
        + [jax.ShapeDtypeStruct((nb, d, seq // d, kdim), BF16) for d in dilations],
        scratch_shapes=[pltpu.VMEM((tm, LANE), F32)] * (kdim // LANE),
        compiler_params=_params("parallel", "parallel"),
    )(x)
    return [outs[0]] + [o.reshape(nb, seq, kdim) for o in outs[1:]]


def _natural_rows(p_ref, tile_refs):
    d, per, width = p_ref.shape
    for r in range(d):
        rows = pl.ds(r, per, stride=d)
        for j in range(width // LANE):
            tile_refs[j][rows, :] = p_ref[r, :, j * LANE:(j + 1) * LANE]
    return jnp.concatenate([tile_refs[j][...] for j in range(width // LANE)], axis=1)


def _combine_fwd(o0, l0, dilated, gatt):
    nb, seq, _ = gatt.shape
    tm = 512
    ntile = ATT_OUT // LANE

    def body(o0_ref, l0_ref, o1_ref, l1_ref, o2_ref, l2_ref, g_ref, oa_ref, oatt_ref, lse_ref, *tile_refs):
        spread = _head_lanes(0, False)
        l0v = l0_ref[...]
        l1v = _exact_dot(_natural_rows(l1_ref, tile_refs), spread, NT)
        l2v = _exact_dot(_natural_rows(l2_ref, tile_refs), spread, NT)
        m = jnp.maximum(jnp.maximum(l0v, l1v), l2v)
        tot = m + jnp.log(jnp.exp(l0v - m) + jnp.exp(l1v - m) + jnp.exp(l2v - m))
        o = jnp.exp(l0v - tot) * o0_ref[...]
        o = o + jnp.exp(l1v - tot) * _natural_rows(o1_ref, tile_refs)
        o = o + jnp.exp(l2v - tot) * _natural_rows(o2_ref, tile_refs)
        g = g_ref[...]
        oa_ref[...] = (o * (g * _sigmoid(g))).astype(BF16)
        oatt_ref[...] = o
        lse_ref[...] = tot

    spec = pl.BlockSpec((None, tm, ATT_OUT), lambda b, i: (b, i, 0))
    own = lambda t: pl.BlockSpec((None, t.shape[1], tm // t.shape[1], t.shape[3]), lambda b, i: (b, 0, i, 0))
    (o1, l1), (o2, l2) = dilated
    return _pcall(
        body, name="attn_combine", grid=(nb, seq // tm),
        in_specs=[spec, spec, own(o1), own(l1), own(o2), own(l2), spec], out_specs=[spec] * 3,
        out_shape=[jax.ShapeDtypeStruct((nb, seq, ATT_OUT), BF16), jax.ShapeDtypeStruct((nb, seq, ATT_OUT), F32),
                   jax.ShapeDtypeStruct((nb, seq, ATT_OUT), F32)],
        scratch_shapes=[pltpu.VMEM((tm, LANE), F32)] * ntile,
        compiler_params=_params("parallel", "parallel"),
    )(o0, l0, o1, l1, o2, l2, gatt)


def _combine_bwd(dya16, w_bra, gatt, o_att, lse, dilations):
    nb, seq, _ = gatt.shape
    tm = 512

    def body(dya_ref, w_ref, g_ref, o_ref, l_ref, do_ref, dg_ref, *rest):
        ntile = ATT_OUT // LANE
        outs, tile_refs = rest[:-ntile], rest[-ntile:]
        doa = lax.dot_general(dya_ref[...], w_ref[...], NT, preferred_element_type=F32)
        g = g_ref[...]
        sg = _sigmoid(g)
        do = doa * (g * sg)
        do_ref[...] = do
        stats = (_exact_dot(do * o_ref[...], _head_lanes(0, False))
                 + _exact_dot(l_ref[...], _head_lanes(STAT_LSE_LANE, True)))
        dg_ref[...] = (doa * o_ref[...] * (sg * (1.0 + g * (1.0 - sg)))).astype(BF16)
        for k in range(len(dilations)):
            _store_own_order(do, tile_refs, outs[2 * k])
            _store_own_order(stats, tile_refs, outs[2 * k + 1])

    spec = pl.BlockSpec((None, tm, ATT_OUT), lambda b, i: (b, i, 0))
    own = lambda d, width: pl.BlockSpec((None, d, tm // d, width), lambda b, i: (b, 0, i, 0))
    outs = _pcall(
        body, name="attn_combine_bwd", grid=(nb, seq // tm),
        in_specs=[_tok_spec(tm, D_MODEL), _whole(w_bra, True)] + [spec] * 3,
        out_specs=[spec, spec] + [own(d, w) for d in dilations for w in (ATT_OUT, LANE)],
        out_shape=[jax.ShapeDtypeStruct((nb, seq, ATT_OUT), F32), jax.ShapeDtypeStruct((nb, seq, ATT_OUT), BF16)]
        + [jax.ShapeDtypeStruct((nb, d, seq // d, w), t) for d in dilations for w, t in ((ATT_OUT, BF16), (LANE, F32))],
        scratch_shapes=[pltpu.VMEM((tm, LANE), F32)] * (ATT_OUT // LANE),
        compiler_params=_params("parallel", "parallel"),
    )(dya16, w_bra, gatt, o_att, lse)
    return outs[0], outs[1], outs[2:]


CONV_TM = 1024
CONV_TC = 1024


def _shift_down(cur, halo, k):
    rolled = pltpu.roll(cur, k, 0)
    hro = pltpu.roll(halo, k, 0)
    row = lax.broadcasted_iota(jnp.int32, hro.shape, 0)
    return jnp.concatenate([jnp.where(row < k, hro, rolled[:8]), rolled[8:]], axis=0)


def _shift_up(cur, halo, k):
    n = cur.shape[0]
    rolled = pltpu.roll(cur, n - k, 0)
    hro = pltpu.roll(halo, 8 - k, 0)
    row = lax.broadcasted_iota(jnp.int32, hro.shape, 0)
    return jnp.concatenate([rolled[:n - 8], jnp.where(row >= 8 - k, hro, rolled[n - 8:])], axis=0)


def _conv_pre(cur, halo, w_ref, b_ref):
    acc = cur * w_ref[3:4, :] + b_ref[...]
    for k in range(1, 4):
        acc = acc + _shift_down(cur, halo, k) * w_ref[3 - k:4 - k, :]
    return acc


def _conv_specs(seq):
    nblk = seq // CONV_TM
    cur = pl.BlockSpec((None, CONV_TM, CONV_TC), lambda cb, b, i: (b, i, cb))
    prev = pl.BlockSpec((None, 8, CONV_TC), lambda cb, b, i: (b, jnp.maximum(i * (CONV_TM // 8) - 1, 0), cb))
    nxt = pl.BlockSpec((None, 8, CONV_TC),
                       lambda cb, b, i: (b, jnp.minimum((i + 1) * (CONV_TM // 8), seq // 8 - 1), cb))
    wspec = pl.BlockSpec((4, CONV_TC), lambda cb, b, i: (0, cb))
    bspec = pl.BlockSpec((1, CONV_TC), lambda cb, b, i: (0, cb))
    return nblk, cur, prev, nxt, wspec, bspec


def _conv_fwd(xin, w4, bias, name):
    nb, seq, ch = xin.shape
    _, cur, prev, _, wspec, bspec = _conv_specs(seq)

    def body(x_ref, h_ref, w_ref, b_ref, o_ref):
        halo = jnp.where(pl.program_id(2) > 0, h_ref[...], 0.0)
        pre = _conv_pre(x_ref[...], halo, w_ref, b_ref)
        o_ref[...] = pre * _sigmoid(pre)

    return _pcall(
        body, name=name, grid=(ch // CONV_TC, nb, seq // CONV_TM),
        in_specs=[cur, prev, wspec, bspec], out_specs=cur,
        out_shape=jax.ShapeDtypeStruct(xin.shape, F32),
        compiler_params=_params("parallel", "parallel", "parallel"),
    )(xin, xin, w4, bias)


def _conv_bwd_pre(dact, xin, w4, bias, name):
    nb, seq, ch = xin.shape
    _, cur, prev, _, wspec, bspec = _conv_specs(seq)

    def body(da_ref, x_ref, h_ref, w_ref, b_ref, dp_ref, s_ref):
        b, i = pl.program_id(1), pl.program_id(2)

        @pl.when((b == 0) & (i == 0))
        def _():
            s_ref[...] = jnp.zeros_like(s_ref)

        halo = jnp.where(i > 0, h_ref[...], 0.0)
        x = x_ref[...]
        pre = _conv_pre(x, halo, w_ref, b_ref)
        sg = _sigmoid(pre)
        dpre = da_ref[...] * (sg * (1.0 + pre * (1.0 - sg)))
        dp_ref[...] = dpre
        s_ref[3:4, :] += jnp.sum(dpre * x, 0, keepdims=True)
        for k in range(1, 4):
            s_ref[3 - k:4 - k, :] += jnp.sum(dpre * _shift_down(x, halo, k), 0, keepdims=True)
        s_ref[4:5, :] += jnp.sum(dpre, 0, keepdims=True)

    return _pcall(
        body, name=name, grid=(ch // CONV_TC, nb, seq // CONV_TM),
        in_specs=[cur, cur, prev, wspec, bspec],
        out_specs=[cur, pl.BlockSpec((8, CONV_TC), lambda cb, b, i: (0, cb))],
        out_shape=[jax.ShapeDtypeStruct(xin.shape, F32), jax.ShapeDtypeStruct((8, ch), F32)],
        compiler_params=_params("parallel", "arbitrary", "arbitrary"),
    )(dact, xin, xin, w4, bias)


def _conv_bwd_x(dpre, w4, name):
    nb, seq, ch = dpre.shape
    nblk, cur, _, nxt, wspec, _ = _conv_specs(seq)

    def body(d_ref, n_ref, w_ref, o_ref):
        halo = jnp.where(pl.program_id(2) < nblk - 1, n_ref[...], 0.0)
        cur_v = d_ref[...]
        acc = cur_v * w_ref[3:4, :]
        for j in range(1, 4):
            acc = acc + _shift_up(cur_v, halo, j) * w_ref[3 - j:4 - j, :]
        o_ref[...] = acc.astype(BF16)

    return _pcall(
        body, name=name, grid=(ch // CONV_TC, nb, seq // CONV_TM),
        in_specs=[cur, nxt, wspec], out_specs=cur,
        out_shape=jax.ShapeDtypeStruct(dpre.shape, BF16),
        compiler_params=_params("parallel", "parallel", "parallel"),
    )(dpre, dpre, w4)


def _step_sizes(raw, tb_ref):
    shift = (LANE - GROUP_SSM_HEADS * pl.program_id(0)) % LANE
    v = pltpu.roll(raw + tb_ref[...], shift, 1)
    own = lax.broadcasted_iota(jnp.int32, v.shape, 1) < GROUP_SSM_HEADS
    sp = jnp.maximum(v, 0.0) + jnp.log1p(jnp.exp(-jnp.abs(v)))
    return jnp.where(own, sp, 0.0), jnp.where(own, _sigmoid(v), 0.0)


def _group_lanes(t):
    pads = [(0, 0)] * (t.ndim - 1) + [(0, LANE - GROUP_SSM_HEADS)]
    return jnp.stack([jnp.pad(t[..., GROUP_SSM_HEADS * g:GROUP_SSM_HEADS * (g + 1)], pads) for g in range(SSM_GROUPS)])


def _ungroup_lanes(t):
    return jnp.concatenate([t[g][..., :GROUP_SSM_HEADS] for g in range(SSM_GROUPS)], axis=-1)


def _decays(dt, al_ref):
    row = lax.broadcasted_iota(jnp.int32, (CHUNK, CHUNK), 0)
    col = lax.broadcasted_iota(jnp.int32, (CHUNK, CHUNK), 1)
    tril = (row >= col).astype(BF16)
    triu = (row <= col).astype(BF16)
    arow = -jnp.exp(al_ref[...])
    hi, mid, lo = _split3(dt * arow)
    down = lambda t: jnp.dot(tril, t, preferred_element_type=F32)
    across = lambda t: lax.dot_general(t, triu, TN, preferred_element_type=F32)
    acs = (down(hi) + down(mid)) + down(lo)
    acs_t = (across(hi) + across(mid)) + across(lo)
    return arow, acs, acs_t, row >= col, triu


STEP_CHUNKS = 8


def _ssd_specs(nb, seq):
    nc = seq // CHUNK
    hw = GROUP_SSM_HEADS * HEAD_DIM
    rows, steps = STEP_CHUNKS * CHUNK, nc // STEP_CHUNKS

    def mk(rev):
        cidx = (lambda c: steps - 1 - c) if rev else (lambda c: c)
        wide = pl.BlockSpec((None, rows, hw), lambda g, b, c: (b, cidx(c), g))
        xbc = pl.BlockSpec((None, rows, XBC_GROUP), lambda g, b, c: (b, cidx(c), g))
        lanes = pl.BlockSpec((None, None, rows, LANE), lambda g, b, c: (g, b, cidx(c), 0))
        prev = pl.BlockSpec((None, STEP_CHUNKS, None, D_STATE, hw), lambda g, b, c: (b, cidx(c), g, 0, 0))
        raw = pl.BlockSpec((None, rows, LANE), lambda g, b, c: (b, cidx(c), 0))
        return wide, xbc, lanes, prev, raw

    grow = pl.BlockSpec((None, 1, LANE), lambda g, b, c: (g, 0, 0))
    nwspec = pl.BlockSpec((1, hw), lambda g, b, c: (0, g))
    tbspec = pl.BlockSpec((1, LANE), lambda g, b, c: (0, 0))
    return nc, steps, hw, mk, grow, nwspec, tbspec


def _head_expand():
    hw = GROUP_SSM_HEADS * HEAD_DIM
    r = lax.broadcasted_iota(jnp.int32, (LANE, hw), 0)
    c = lax.broadcasted_iota(jnp.int32, (LANE, hw), 1)
    return ((c // HEAD_DIM) == r).astype(BF16)


def _split3(v):
    hi = v.astype(BF16)
    rest = v - hi.astype(F32)
    mid = rest.astype(BF16)
    return hi, mid, (rest - mid.astype(F32)).astype(BF16)


def _to_channels(v, e):
    hi, mid, lo = _split3(v)
    dot = lambda t: jnp.dot(t, e, preferred_element_type=F32)
    return (dot(hi) + dot(mid)) + dot(lo)


def _to_heads(w, e):
    hi, mid, lo = _split3(w)
    dot = lambda t: lax.dot_general(t, e, (((1,), (1,)), ((), ())), preferred_element_type=F32)
    return (dot(hi) + dot(mid)) + dot(lo)


def _row8(v):
    return jnp.broadcast_to(v, (8, v.shape[1]))


def _ssd_chunk_setup(dt, al_ref, ds_ref):
    arow, acs, acs_t, causal, triu = _decays(dt, al_ref)
    e = _head_expand()
    dtx = _to_channels(dt, e)
    acsx = _to_channels(acs, e)
    lastx = acsx[CHUNK - 1:CHUNK, :]
    dskx = _to_channels(_row8(ds_ref[...]), e)[0:1, :]
    return arow, acs, acs_t, causal, triu, e, dtx, acsx, lastx, dskx


def _ssd_fwd(xbc, dt_raw, dt_bias_row, z, alog_g, dskip_g, normw):
    nb, seq, _ = xbc.shape
    nc, steps, hw, mk, grow, nwspec, tbspec = _ssd_specs(nb, seq)
    wide, xbc_spec, lanes, prev, raw = mk(False)
    tn = (((0,), (0,)), ((), ()))

    def body(xbc_ref, dt_ref, tb_ref, z_ref, al_ref, ds_ref, nw_ref, ys_ref, y_ref, sp_ref, st_ref):
        @pl.when(pl.program_id(2) == 0)
        def _():
            st_ref[...] = jnp.zeros_like(st_ref)

        for ci in range(STEP_CHUNKS):
            chunk(ci, xbc_ref, dt_ref, tb_ref, z_ref, al_ref, ds_ref, nw_ref, ys_ref, y_ref, sp_ref, st_ref)

    def chunk(ci, xbc_ref, dt_ref, tb_ref, z_ref, al_ref, ds_ref, nw_ref, ys_ref, y_ref, sp_ref, st_ref):
        rows = slice(ci * CHUNK, (ci + 1) * CHUNK)
        dt, _ = _step_sizes(dt_ref[rows, :], tb_ref)
        _, acs, acs_t, causal, _, _, dtx, acsx, lastx, dskx = _ssd_chunk_setup(dt, al_ref, ds_ref)
        bmat = xbc_ref[rows, GROUP_CH:GROUP_CH + D_STATE].astype(BF16)
        cmat = xbc_ref[rows, GROUP_CH + D_STATE:].astype(BF16)
        cb = lax.dot_general(cmat, bmat, (((1,), (1,)), ((), ())), preferred_element_type=F32)
        x = xbc_ref[rows, :GROUP_CH]
        xdt = x * dtx
        xdt16 = xdt.astype(BF16)
        first_head = lax.broadcasted_iota(jnp.int32, (CHUNK, LANE), 1) < HEAD_DIM
        pairs = []
        for hp in range(GROUP_SSM_HEADS // 2):
            xp = xdt16[:, hp * LANE:(hp + 1) * LANE]
            two = []
            for j in (2 * hp, 2 * hp + 1):
                lmat = jnp.exp(jnp.where(causal, acs[:, j:j + 1] - acs_t[j:j + 1, :], -jnp.inf))
                two.append(jnp.dot((cb * lmat).astype(BF16), xp, preferred_element_type=F32))
            pairs.append(jnp.where(first_head, two[0], two[1]))
        yd = jnp.concatenate(pairs, axis=1)
        s_prev = st_ref[...]
        s16 = s_prev.astype(BF16)
        sp_ref[ci] = s16
        yo = jnp.dot(cmat, s16, preferred_element_type=F32) * jnp.exp(acsx)
        sts = lax.dot_general(bmat, (xdt * jnp.exp(lastx - acsx)).astype(BF16), tn, preferred_element_type=F32)
        st_ref[...] = s_prev * jnp.exp(lastx) + sts
        y = yd + yo + dskx * x
        zz = z_ref[rows, :]
        u = y * (zz * _sigmoid(zz))
        rn = lax.rsqrt(jnp.mean(u * u, -1, keepdims=True) + RMS_EPS)
        ys_ref[rows, :] = (u * rn * nw_ref[...]).astype(BF16)
        y_ref[rows, :] = y

    return _pcall(
        body, name="ssd_fwd", grid=(SSM_GROUPS, nb, steps),
        in_specs=[xbc_spec, raw, tbspec, wide, grow, grow, nwspec],
        out_specs=[wide, wide, prev],
        out_shape=[jax.ShapeDtypeStruct((nb, seq, D_INNER), BF16), jax.ShapeDtypeStruct((nb, seq, D_INNER), F32),
                   jax.ShapeDtypeStruct((nb, nc, SSM_GROUPS, D_STATE, hw), BF16)],
        scratch_shapes=[pltpu.VMEM((D_STATE, hw), F32)],
        compiler_params=_params("parallel", "parallel", "arbitrary"),
    )(xbc, dt_raw, dt_bias_row, z, alog_g, dskip_g, normw)


def _ssd_bwd(xbc, dt_raw, dt_bias_row, z, y, dys, sprev, alog_g, dskip_g, normw):
    nb, seq, _ = xbc.shape
    nc, steps, hw, mk, grow, nwspec, tbspec = _ssd_specs(nb, seq)
    wide, xbc_spec, lanes, prev, raw = mk(True)
    nt = (((1,), (1,)), ((), ()))
    tn = (((0,), (0,)), ((), ()))

    def body(xbc_ref, dt_ref, tb_ref, z_ref, y_ref, dys_ref, sp_ref, al_ref, ds_ref, nw_ref,
             dxbc_ref, ddt_ref, dz_ref, small_ref, dnw_ref, g_ref):
        b, c = pl.program_id(1), pl.program_id(2)

        @pl.when((b == 0) & (c == 0))
        def _():
            small_ref[...] = jnp.zeros_like(small_ref)
            dnw_ref[...] = jnp.zeros_like(dnw_ref)

        @pl.when(c == 0)
        def _():
            g_ref[...] = jnp.zeros_like(g_ref)

        for ci in reversed(range(STEP_CHUNKS)):
            chunk(ci, xbc_ref, dt_ref, tb_ref, z_ref, y_ref, dys_ref, sp_ref, al_ref, ds_ref, nw_ref,
                  dxbc_ref, ddt_ref, dz_ref, small_ref, dnw_ref, g_ref)

    def chunk(ci, xbc_ref, dt_ref, tb_ref, z_ref, y_ref, dys_ref, sp_ref, al_ref, ds_ref, nw_ref,
              dxbc_ref, ddt_ref, dz_ref, small_ref, dnw_ref, g_ref):
        rows = slice(ci * CHUNK, (ci + 1) * CHUNK)
        yv, zz, dys_v, nw = y_ref[rows, :], z_ref[rows, :], dys_ref[rows, :], nw_ref[...]
        sz = _sigmoid(zz)
        silu = zz * sz
        u = yv * silu
        rn = lax.rsqrt(jnp.mean(u * u, -1, keepdims=True) + RMS_EPS)
        gn = dys_v * nw
        du = rn * gn - u * (rn * rn * rn) * jnp.mean(u * gn, -1, keepdims=True)
        dnw_ref[...] += jnp.sum(dys_v * u * rn, 0, keepdims=True)
        dy = du * silu
        dz_ref[rows, :] = du * yv * (sz * (1.0 + zz * (1.0 - sz)))

        dt, sg = _step_sizes(dt_ref[rows, :], tb_ref)
        arow, acs, acs_t, causal, triu, e, dtx, acsx, lastx, dskx = _ssd_chunk_setup(dt, al_ref, ds_ref)
        dfsx = jnp.exp(acsx)
        dtex = jnp.exp(lastx - acsx)
        bmat = xbc_ref[rows, GROUP_CH:GROUP_CH + D_STATE].astype(BF16)
        cmat = xbc_ref[rows, GROUP_CH + D_STATE:].astype(BF16)
        cb = lax.dot_general(cmat, bmat, nt, preferred_element_type=F32)
        x = xbc_ref[rows, :GROUP_CH]
        xdt = x * dtx
        xdt16 = xdt.astype(BF16)
        xdte = xdt * dtex
        dy16 = dy.astype(BF16)
        dyd = dy * dfsx
        dyd16 = dyd.astype(BF16)
        s16 = sp_ref[ci]
        g = g_ref[...]
        g16 = g.astype(BF16)
        cs = jnp.dot(cmat, s16, preferred_element_type=F32)
        dc_off = lax.dot_general(dyd16, s16, nt, preferred_element_type=F32)
        g_here = lax.dot_general(cmat, dyd16, tn, preferred_element_type=F32)
        bg = jnp.dot(bmat, g16, preferred_element_type=F32)
        db_st = lax.dot_general(xdte.astype(BF16), g16, nt, preferred_element_type=F32)
        ddte_w = bg * xdte
        dcd = _to_heads(_row8(jnp.sum(g * s16.astype(F32), 0, keepdims=True)), e)[0:1, :]
        lane = lax.broadcasted_iota(jnp.int32, (CHUNK, LANE), 1)
        first_head = lane < HEAD_DIM
        sub = lax.broadcasted_iota(jnp.int32, (CHUNK, LANE), 0)
        dacs = jnp.zeros((CHUNK, LANE), F32)
        colsums = jnp.zeros((CHUNK, LANE), F32)
        dcb = jnp.zeros((CHUNK, CHUNK), F32)
        pairs = []
        for hp in range(GROUP_SSM_HEADS // 2):
            xp = xdt16[:, hp * LANE:(hp + 1) * LANE]
            dyp = dy16[:, hp * LANE:(hp + 1) * LANE]
            two = []
            for idx, j in enumerate((2 * hp, 2 * hp + 1)):
                lmat = jnp.exp(jnp.where(causal, acs[:, j:j + 1] - acs_t[j:j + 1, :], -jnp.inf))
                mf = cb * lmat
                dy_h = jnp.where(first_head if idx == 0 else jnp.logical_not(first_head), dyp, jnp.zeros_like(dyp))
                dm = lax.dot_general(dy_h, xp, nt, preferred_element_type=F32)
                two.append(lax.dot_general(mf.astype(BF16), dyp, tn, preferred_element_type=F32))
                wmat = dm * mf
                dcb = dcb + dm * lmat
                dacs = jnp.where(lane == j, jnp.sum(wmat, -1, keepdims=True), dacs)
                colsums = jnp.where(sub == j, jnp.sum(wmat, 0, keepdims=True), colsums)
            pairs.append(jnp.where(first_head, two[0], two[1]))
        dxdt = bg * dtex + jnp.concatenate(pairs, axis=1)
        dacs = dacs - colsums.T + _to_heads(dyd * cs - ddte_w, e)
        cd_row = jnp.exp(acs[CHUNK - 1:CHUNK, :])
        tail = _to_heads(_row8(jnp.sum(ddte_w, 0, keepdims=True)), e)[0:1, :] + dcd * cd_row
        dacs = dacs + jnp.where(sub == CHUNK - 1, tail, 0.0)
        d_hi, d_mid, d_lo = _split3(dacs)
        up = lambda t: jnp.dot(triu, t, preferred_element_type=F32)
        da = (up(d_hi) + up(d_mid)) + up(d_lo)
        ddt_raw = (da * arow + _to_heads(dxdt * x, e)) * sg
        ddt_ref[rows, :] = ddt_raw
        small_ref[0:1, :] += jnp.sum(da * dt, 0, keepdims=True) * arow
        small_ref[1:2, :] += _to_heads(_row8(jnp.sum(dy * x, 0, keepdims=True)), e)[0:1, :]
        small_ref[2:3, :] += jnp.sum(ddt_raw, 0, keepdims=True)
        dcb16 = dcb.astype(BF16)
        dxbc_ref[rows, GROUP_CH + D_STATE:] = dc_off + jnp.dot(dcb16, bmat, preferred_element_type=F32)
        dxbc_ref[rows, GROUP_CH:GROUP_CH + D_STATE] = db_st + lax.dot_general(dcb16, cmat, tn,
                                                                               preferred_element_type=F32)
        dxbc_ref[rows, :GROUP_CH] = dxdt * dtx + dskx * dy
        g_ref[...] = g * jnp.exp(lastx) + g_here

    return _pcall(
        body, name="ssd_bwd", grid=(SSM_GROUPS, nb, steps),
        in_specs=[xbc_spec, raw, tbspec, wide, wide, wide, prev, grow, grow, nwspec],
        out_specs=[xbc_spec, lanes, wide,
                   pl.BlockSpec((None, 8, LANE), lambda g, b, c: (g, 0, 0)), nwspec],
        out_shape=[jax.ShapeDtypeStruct((nb, seq, CONV_DIM), F32),
                   jax.ShapeDtypeStruct((SSM_GROUPS, nb, seq, LANE), F32),
                   jax.ShapeDtypeStruct((nb, seq, D_INNER), F32),
                   jax.ShapeDtypeStruct((SSM_GROUPS, 8, LANE), F32),
                   jax.ShapeDtypeStruct((1, D_INNER), F32)],
        scratch_shapes=[pltpu.VMEM((D_STATE, hw), F32)],
        compiler_params=_params("parallel", "arbitrary", "arbitrary"),
    )(xbc, dt_raw, dt_bias_row, z, y, dys, sprev, alog_g, dskip_g, normw)


EW_TM = 256


def _merge_fwd(oa16, y_ssm16, w_bra, w_brb, gm, bgate):
    nb, seq, _ = oa16.shape

    def body(oa_ref, ys_ref, wa_ref, wb_ref, ga_ref, gb_ref, bg_ref, a_ref, b_ref, o_ref):
        y_a = jnp.dot(oa_ref[...], wa_ref[...], preferred_element_type=F32)
        y_b = jnp.dot(ys_ref[...], wb_ref[...], preferred_element_type=F32)
        a_ref[...] = y_a
        b_ref[...] = y_b
        sa = _sigmoid(ga_ref[...] + bg_ref[0:1, :])
        sb = _sigmoid(gb_ref[...] + bg_ref[1:2, :])
        o_ref[...] = (sa * y_a + sb * y_b).astype(BF16)

    spec = pl.BlockSpec((None, EW_TM, D_MODEL), lambda b, i: (b, i, 0))
    spec1 = pl.BlockSpec((None, EW_TM, D_MODEL), lambda b, i: (b, i, 1))
    return _pcall(
        body, name="merge_fwd", grid=(nb, seq // EW_TM),
        in_specs=[_tok_spec(EW_TM, ATT_OUT), _tok_spec(EW_TM, D_INNER), _whole(w_bra, True), _whole(w_brb, True),
                  spec, spec1, pl.BlockSpec((8, D_MODEL), lambda b, i: (0, 0))],
        out_specs=[spec] * 3,
        out_shape=[jax.ShapeDtypeStruct((nb, seq, D_MODEL), F32)] * 2 + [jax.ShapeDtypeStruct((nb, seq, D_MODEL), BF16)],
        compiler_params=_params("parallel", "parallel"),
    )(oa16, y_ssm16, w_bra, w_brb, gm, gm, bgate)


def _merge_bwd(dpre16, w_out16, y_a, y_b, gm, bgate):
    nb, seq, _ = y_a.shape

    def body(dp_ref, w_ref, a_ref, b_ref, ga_ref, gb_ref, bg_ref, dya_ref, dyb_ref, dg_ref, s_ref):
        @pl.when((pl.program_id(0) == 0) & (pl.program_id(1) == 0))
        def _():
            s_ref[...] = jnp.zeros_like(s_ref)

        dm = lax.dot_general(dp_ref[...], w_ref[...], NT, preferred_element_type=F32)
        sa = _sigmoid(ga_ref[...] + bg_ref[0:1, :])
        sb = _sigmoid(gb_ref[...] + bg_ref[1:2, :])
        dya_ref[...] = (dm * sa).astype(BF16)
        dyb_ref[...] = (dm * sb).astype(BF16)
        dga = dm * a_ref[...] * (sa * (1.0 - sa))
        dgb = dm * b_ref[...] * (sb * (1.0 - sb))
        dg_ref[:, :D_MODEL] = dga.astype(BF16)
        dg_ref[:, D_MODEL:] = dgb.astype(BF16)
        s_ref[0:1, :] += jnp.sum(dga, 0, keepdims=True)
        s_ref[1:2, :] += jnp.sum(dgb, 0, keepdims=True)

    spec = pl.BlockSpec((None, EW_TM, D_MODEL), lambda b, i: (b, i, 0))
    spec1 = pl.BlockSpec((None, EW_TM, D_MODEL), lambda b, i: (b, i, 1))
    small = pl.BlockSpec((8, D_MODEL), lambda b, i: (0, 0))
    return _pcall(
        body, name="merge_bwd", grid=(nb, seq // EW_TM),
        in_specs=[spec, _whole(w_out16, True), spec, spec, spec, spec1, small],
        out_specs=[spec, spec, pl.BlockSpec((None, EW_TM, 2 * D_MODEL), lambda b, i: (b, i, 0)), small],
        out_shape=[jax.ShapeDtypeStruct((nb, seq, D_MODEL), BF16), jax.ShapeDtypeStruct((nb, seq, D_MODEL), BF16),
                   jax.ShapeDtypeStruct((nb, seq, 2 * D_MODEL), BF16), jax.ShapeDtypeStruct((8, D_MODEL), F32)],
        compiler_params=_params("arbitrary", "arbitrary"),
    )(dpre16, w_out16, y_a, y_b, gm, gm, bgate)


def _ln_loss(x, merged16, w_out16, gp, p16, w_ple16, target, bgate, ln_g, ln_b):
    nb, seq, _ = x.shape

    def body(x_ref, m_ref, wo_ref, gp_ref, p_ref, wp_ref, t_ref, bg_ref, g_ref, b_ref,
             dx_ref, dp_ref, dpw_ref, dgp_ref, s_ref):
        @pl.when((pl.program_id(0) == 0) & (pl.program_id(1) == 0))
        def _():
            s_ref[...] = jnp.zeros_like(s_ref)

        sp = _sigmoid(gp_ref[...] + bg_ref[2:3, :])
        pw = jnp.dot(p_ref[...], wp_ref[...], preferred_element_type=F32)
        mix = jnp.dot(m_ref[...], wo_ref[...], preferred_element_type=F32)
        pre = ALPHA * x_ref[...] + mix + sp * pw
        mu = jnp.mean(pre, -1, keepdims=True)
        cen = pre - mu
        rstd = lax.rsqrt(jnp.mean(cen * cen, -1, keepdims=True) + LN_EPS)
        xhat = cen * rstd
        err = xhat * g_ref[...] + b_ref[...] - t_ref[...]
        dy = err * (1.0 / D_MODEL)
        dxh = dy * g_ref[...]
        dpre = rstd * (dxh - jnp.mean(dxh, -1, keepdims=True) - xhat * jnp.mean(dxh * xhat, -1, keepdims=True))
        dx_ref[...] = ALPHA * dpre
        dp_ref[...] = dpre.astype(BF16)
        dpw_ref[...] = (dpre * sp).astype(BF16)
        dgp = dpre * pw * (sp * (1.0 - sp))
        dgp_ref[...] = dgp.astype(BF16)
        s_ref[0:1, :] += jnp.sum(dy * xhat, 0, keepdims=True)
        s_ref[1:2, :] += jnp.sum(dy, 0, keepdims=True)
        s_ref[2:3, :] += jnp.sum(dgp, 0, keepdims=True)
        s_ref[3:4, :] += jnp.sum(err * err, 0, keepdims=True)

    spec = pl.BlockSpec((None, EW_TM, D_MODEL), lambda b, i: (b, i, 0))
    small = pl.BlockSpec((8, D_MODEL), lambda b, i: (0, 0))
    row = pl.BlockSpec((1, D_MODEL), lambda b, i: (0, 0))
    return _pcall(
        body, name="ln_loss", grid=(nb, seq // EW_TM),
        in_specs=[spec, spec, _whole(w_out16, True), spec, pl.BlockSpec((None, EW_TM, PLE_DIM), lambda b, i: (b, i, 0)),
                  _whole(w_ple16, True), spec, small, row, row],
        out_specs=[spec] * 4 + [small],
        out_shape=[jax.ShapeDtypeStruct((nb, seq, D_MODEL), F32)] + [jax.ShapeDtypeStruct((nb, seq, D_MODEL), BF16)] * 3
        + [jax.ShapeDtypeStruct((8, D_MODEL), F32)],
        compiler_params=_params("arbitrary", "arbitrary"),
    )(x, merged16, w_out16, gp, p16, w_ple16, target, bgate, ln_g, ln_b)


def _adamw_update(w_ref, g_ref, m_ref, v_ref, d_ref, nm_ref, nv_ref):
    c1 = 1.0 - ADAM_B1 ** ADAM_STEP
    c2 = 1.0 - ADAM_B2 ** ADAM_STEP
    gv = g_ref[...]
    nm = ADAM_B1 * m_ref[...] + (1.0 - ADAM_B1) * gv
    nv = ADAM_B2 * v_ref[...] + (1.0 - ADAM_B2) * (gv * gv)
    d_ref[...] = -ADAM_LR * ((nm / c1) / (jnp.sqrt(nv / c2) + ADAM_EPS) + ADAM_WD * w_ref[...])
    nm_ref[...] = nm
    nv_ref[...] = nv


def _adamw(w, g, m, v, name):
    rows, cols = w.shape
    tr = _row_tile(rows, cols, 8, 5 << 19)

    def body(*refs):
        _adamw_update(*refs)

    spec = pl.BlockSpec((tr, cols), lambda i: (i, 0))
    return _pcall(
        body, name=name, grid=(rows // tr,), in_specs=[spec] * 4, out_specs=[spec] * 3,
        out_shape=[jax.ShapeDtypeStruct(w.shape, F32)] * 3, compiler_params=_params("parallel"),
    )(w, g, m, v)


def _adamw_small(ws, gs, ms, vs, name):
    n = len(ws)

    def body(*refs):
        for i in range(n):
            _adamw_update(*[refs[k * n + i] for k in range(7)])

    outs = _pcall(body, name=name, out_shape=[jax.ShapeDtypeStruct(w.shape, F32) for w in ws] * 3,
                  compiler_params=_params())(*ws, *gs, *ms, *vs)
    return outs[:n], outs[n:2 * n], outs[2 * n:]


def _sum_rows(parts, out_dtype, name):
    rows, cols = parts[0].shape
    tr = rows
    for cand in range(16, rows, 16):
        if rows % cand == 0 and cand * cols * 4 <= (1 << 20):
            tr = cand
    n = len(parts)

    def body(*refs):
        acc = refs[0][...].astype(F32)
        for r in refs[1:n]:
            acc = acc + r[...].astype(F32)
        refs[n][...] = acc.astype(out_dtype)

    spec = pl.BlockSpec((tr, cols), lambda i: (i, 0))
    return _pcall(
        body, name=name, grid=(rows // tr,), in_specs=[spec] * n, out_specs=spec,
        out_shape=jax.ShapeDtypeStruct((rows, cols), out_dtype), compiler_params=_params("parallel"),
    )(*parts)


def _place():
    return lax.axis_index("x"), lax.axis_index("y"), lax.axis_index("c")


def _other_chips(x, y):
    return [(1 - x, y), (x, 1 - y), (1 - x, 1 - y)]


def _remote(src, dst, send_sem, recv_sem, to):
    return pltpu.make_async_remote_copy(src_ref=src, dst_ref=dst, send_sem=send_sem, recv_sem=recv_sem,
                                        device_id=to, device_id_type=MESH)


ANY = pl.BlockSpec(memory_space=pl.ANY)
D2D_CHUNK_BYTES = 512 * 1024
ICI_CHUNK_BYTES = 2 * 1024 * 1024


def _row_chunks(rows, row_bytes, chunk_bytes=D2D_CHUNK_BYTES):
    per = max(16, chunk_bytes // row_bytes // 16 * 16)
    return [(s, min(per, rows - s)) for s in range(0, rows, per)]


def _row_tile(rows, cols, align, limit=1 << 21):
    best = None
    for cand in range(align, rows + 1, align):
        if rows % cand == 0 and cand * cols * 4 <= limit:
            best = cand
    return best or rows


def _allgather_pieces(pieces):
    n = len(pieces)
    halves = [_row_chunks(p.shape[0] // 2, p.shape[1] * p.dtype.itemsize, ICI_CHUNK_BYTES) for p in pieces]
    entries = [(a, q, s, m, j) for a in range(n) for q, (s, m) in enumerate(halves[a]) for j in range(3)]
    slot = {(a, q, j): k for k, (a, q, _, _, j) in enumerate(entries)}
    n_ici = len(entries)

    def body(*refs):
        ins, outs = refs[:n], refs[n:2 * n]
        send_sems, recv_sems = refs[2 * n:]
        x, y, c = _place()
        me = 2 * x + y
        sibling = (x, y, 1 - c)
        chips = _other_chips(x, y)

        def landed(a, s, m, j, core):
            half = ins[a].shape[0] // 2
            return outs[a].at[2 * chips[j][0] + chips[j][1], pl.ds(core * half + s, m)]

        sent = []
        for k, (a, q, s, m, j) in enumerate(entries):
            if j < 2:
                half = ins[a].shape[0] // 2
                cp = _remote(ins[a].at[pl.ds(c * half + s, m)], outs[a].at[me, pl.ds(c * half + s, m)],
                             send_sems.at[k], recv_sems.at[k], (*chips[j], c))
                cp.start()
                sent.append(cp)

        def pass_to_sibling(k, blk):
            fw = _remote(blk, blk, send_sems.at[n_ici + k], recv_sems.at[n_ici + k], sibling)
            fw.start()
            sent.append(fw)

        for k, (a, q, s, m, j) in enumerate(entries):
            if j < 2:
                blk = landed(a, s, m, j, c)
                _remote(blk, blk, send_sems.at[k], recv_sems.at[k], (*chips[j], c)).wait_recv()
                first = q < (len(halves[a]) + 1) // 2
                if (j == 0) == first:
                    on = slot[(a, q, 2)]
                    rl = _remote(blk, blk, send_sems.at[on], recv_sems.at[on], (*chips[1 - j], c))
                    rl.start()
                    sent.append(rl)
                pass_to_sibling(k, blk)
        for k, (a, q, s, m, j) in enumerate(entries):
            if j == 2:
                blk = landed(a, s, m, j, c)
                _remote(blk, blk, send_sems.at[k], recv_sems.at[k], (*chips[j], c)).wait_recv()
                pass_to_sibling(k, blk)
        for k, (a, q, s, m, j) in enumerate(entries):
            blk = landed(a, s, m, j, 1 - c)
            _remote(blk, blk, send_sems.at[n_ici + k], recv_sems.at[n_ici + k], sibling).wait_recv()
        for cp in sent:
            cp.wait_send()

    gathered = _pcall(
        body, name="allgather_weights", in_specs=[ANY] * n, out_specs=[ANY] * n,
        out_shape=[jax.ShapeDtypeStruct((4,) + p.shape, p.dtype) for p in pieces],
        scratch_shapes=[pltpu.SemaphoreType.DMA((2 * n_ici,)), pltpu.SemaphoreType.DMA((2 * n_ici,))],
        compiler_params=pltpu.CompilerParams(has_side_effects=True),
    )(*pieces)
    x, y, _ = _place()
    return [lax.dynamic_update_slice(g, p[None], (2 * x + y, 0, 0)) for g, p in zip(gathered, pieces)]


def _sibling_exchange(grads):
    n = len(grads)
    chunks = [_row_chunks(g.shape[1] // 2, g.shape[2] * g.dtype.itemsize) for g in grads]
    n_sem = 4 * sum(len(ch) for ch in chunks)

    def body(*refs):
        ins, gots = refs[:n], refs[n:2 * n]
        send_sems, recv_sems = refs[2 * n:]
        x, y, c = _place()
        sibling = (x, y, 1 - c)
        work = []
        for a in range(n):
            half = ins[a].shape[1] // 2
            for piece in range(4):
                for s, m in chunks[a]:
                    k = len(work)
                    cp = _remote(ins[a].at[piece, pl.ds((1 - c) * half + s, m)], gots[a].at[piece, pl.ds(s, m)],
                                 send_sems.at[k], recv_sems.at[k], sibling)
                    cp.start()
                    work.append(cp)
        for cp in work:
            cp.wait()

    return _pcall(
        body, name="grad_sibling_exchange", in_specs=[ANY] * n, out_specs=[ANY] * n,
        out_shape=[jax.ShapeDtypeStruct((4, g.shape[1] // 2, g.shape[2]), g.dtype) for g in grads],
        scratch_shapes=[pltpu.SemaphoreType.DMA((n_sem,)), pltpu.SemaphoreType.DMA((n_sem,))],
        compiler_params=pltpu.CompilerParams(has_side_effects=True),
    )(*grads)


def _sibling_gather(fulls):
    n = len(fulls)
    chunks = [_row_chunks(f.shape[0] // 2, f.shape[1] * f.dtype.itemsize) for f in fulls]
    n_sem = sum(len(ch) for ch in chunks)

    def body(*refs):
        outs = refs[n:2 * n]
        send_sems, recv_sems = refs[2 * n:]
        x, y, c = _place()
        sibling = (x, y, 1 - c)
        work = []
        for a in range(n):
            h = outs[a].shape[0] // 2
            for s, m in chunks[a]:
                k = len(work)
                mine = outs[a].at[pl.ds(c * h + s, m)]
                cp = _remote(mine, mine, send_sems.at[k], recv_sems.at[k], sibling)
                cp.start()
                work.append((a, s, m, cp))
        for k, (a, s, m, cp) in enumerate(work):
            h = outs[a].shape[0] // 2
            cp.wait_send()
            theirs = outs[a].at[pl.ds((1 - c) * h + s, m)]
            _remote(theirs, theirs, send_sems.at[k], recv_sems.at[k], sibling).wait_recv()

    return _pcall(
        body, name="grad_sibling_gather", in_specs=[ANY] * n, out_specs=[ANY] * n,
        out_shape=[jax.ShapeDtypeStruct(f.shape, f.dtype) for f in fulls],
        input_output_aliases={a: a for a in range(n)},
        scratch_shapes=[pltpu.SemaphoreType.DMA((n_sem,)), pltpu.SemaphoreType.DMA((n_sem,))],
        compiler_params=pltpu.CompilerParams(has_side_effects=True),
    )(*fulls)


def _pair_sum(grad, got, place, name):
    _, rows, cols = grad.shape
    half = rows // 2
    tr = _row_tile(half, cols, 16)

    def body(p_ref, a_ref, b_ref, o_ref):
        o_ref[...] = (a_ref[...].astype(F32) + b_ref[...].astype(F32)).astype(BF16)

    return _pcall(
        body, name=name,
        grid_spec=pltpu.PrefetchScalarGridSpec(
            num_scalar_prefetch=1, grid=(4, half // tr),
            in_specs=[pl.BlockSpec((None, tr, cols), lambda k, i, p: (k, p[1] * (half // tr) + i, 0)),
                      pl.BlockSpec((None, tr, cols), lambda k, i, p: (k, i, 0))],
            out_specs=pl.BlockSpec((None, tr, cols), lambda k, i, p: (k, i, 0))),
        out_shape=jax.ShapeDtypeStruct((4, half, cols), BF16),
        compiler_params=_params("parallel", "parallel"),
    )(place, grad, got)


def _chip_sum(sums, got, place, name):
    _, h, cols = sums.shape
    tr = _row_tile(h, cols, 16)

    def body(p_ref, own_ref, g0, g1, g2, o_ref):
        o_ref[...] = ((own_ref[...].astype(F32) + g0[...].astype(F32)) + g1[...].astype(F32)) + g2[...].astype(F32)

    gspec = lambda j: pl.BlockSpec((None, tr, cols), lambda i, p: (j, i, 0))
    return _pcall(
        body, name=name,
        grid_spec=pltpu.PrefetchScalarGridSpec(
            num_scalar_prefetch=1, grid=(h // tr,),
            in_specs=[pl.BlockSpec((None, tr, cols), lambda i, p: (p[0], i, 0)), gspec(0), gspec(1), gspec(2)],
            out_specs=pl.BlockSpec((tr, cols), lambda i, p: (p[1] * (h // tr) + i, 0))),
        out_shape=jax.ShapeDtypeStruct((2 * h, cols), F32),
        compiler_params=_params("parallel"),
    )(place, sums, got, got, got)


def _allgather8(buf, name):
    rows = buf.shape[0]

    def body(in_ref, out_ref, send_sems, recv_sems):
        x, y, c = _place()
        me = 4 * x + 2 * y + c
        out_ref[me] = in_ref[...]
        work = []
        for rel in range(1, 8):
            fx, fy, fc = (rel >> 2) & 1, (rel >> 1) & 1, rel & 1
            to = (x ^ fx, y ^ fy, c ^ fc)
            cp = _remote(in_ref, out_ref.at[me], send_sems.at[rel - 1], recv_sems.at[rel - 1], to)
            cp.start()
            work.append((cp, 4 * to[0] + 2 * to[1] + to[2]))
        for rel, (cp, frm) in enumerate(work):
            cp.wait_send()
            blk = out_ref.at[frm]
            _remote(blk, blk, send_sems.at[rel], recv_sems.at[rel], (x, y, c)).wait_recv()

    return _pcall(
        body, name=name, in_specs=[pl.BlockSpec(memory_space=pltpu.VMEM)],
        out_specs=pl.BlockSpec(memory_space=pltpu.VMEM),
        out_shape=jax.ShapeDtypeStruct((8, rows, LANE), F32),
        scratch_shapes=[pltpu.SemaphoreType.DMA((7,)), pltpu.SemaphoreType.DMA((7,))],
        compiler_params=pltpu.CompilerParams(has_side_effects=True),
    )(buf)


def _pack_rows(arrs):
    flats = [a.reshape(-1).astype(F32) for a in arrs]
    starts = np.cumsum([0] + [-(-f.shape[0] // LANE) * LANE for f in flats])
    total = -(-int(starts[-1]) // (8 * LANE)) * 8 * LANE
    flat = sum(jnp.pad(f, (int(s), total - int(s) - f.shape[0])) for f, s in zip(flats, starts))
    return flat.reshape(total // LANE, LANE)


def _unpack_rows(buf, shapes):
    flat = buf.reshape(-1)
    outs, off = [], 0
    for s in shapes:
        n = int(np.prod(s))
        outs.append(flat[off:off + n].reshape(s))
        off += -(-n // LANE) * LANE
    return outs


def _local_grads(x, p, target, wseg, w_br16, w_out16, w_ple16, b_gate, conv_w, conv_b, dt_bias, a_log, d_skip,
                 ssm_norm_w, ln_g, ln_b, rel_bias, finish_dx):
    nb, seq, _ = x.shape
    bmaps = jnp.asarray(_bucket_maps())
    bias = _bias_tables(rel_bias, bmaps)
    bgate8 = jnp.pad(b_gate, ((0, 5), (0, 0)))
    dils = [d for _, d in PATTERNS]

    x16p = _token_orders(x, dils[1:])
    x16 = x16p[0]
    p16 = p.astype(BF16)
    qkv = [_proj(x16p[g], [wseg["qkv%d" % g]], BF16, "proj_qkv%d" % g, True, 2 * MM_TM)[0].reshape(
        nb, dils[g], seq // dils[g], -1) for g in range(3)]
    nat = {}
    for gi, (group, tm) in enumerate(NAT_GROUPS):
        outs = _proj(x16, [wseg[s] for s in group], F32, "proj_nat%d" % gi, True, tm)
        nat.update(zip(group, outs))
    att = [_attn_fwd(qkv[g], bias, g, dils[g], "attn_fwd%d" % g) for g in range(3)]
    oa, o_att, lse = _combine_fwd(att[0][0], att[0][1], att[1:], nat["gatt"])

    conv_wg, conv_bg = _xbc_group_order(conv_w), _xbc_group_order(conv_b)
    act = _conv_fwd(nat["xbc"], conv_wg, conv_bg, "conv_fwd")
    dt_bias_row = jnp.pad(dt_bias, ((0, 0), (0, LANE - SSM_HEADS)))
    alog_g, dskip_g = _group_lanes(a_log), _group_lanes(d_skip)
    y_ssm, y_all, sprev = _ssd_fwd(act, nat["dt"], dt_bias_row, nat["z"], alog_g, dskip_g, ssm_norm_w)

    w_bra, w_brb = w_br16[:ATT_OUT], w_br16[ATT_OUT:]
    y_a, y_b, merged = _merge_fwd(oa, y_ssm, w_bra, w_brb, nat["gm"], bgate8)

    dx, dpre16, dpw16, dgp16, ln_sums = _ln_loss(x, merged, w_out16, nat["gp"], p16, w_ple16, target, bgate8,
                                                 ln_g, ln_b)
    loss_sum = (0.5 / D_MODEL) * jnp.sum(ln_sums[3])
    dya16, dyb16, dgm16, mg_sums = _merge_bwd(dpre16, w_out16, y_a, y_b, nat["gm"], bgate8)
    dys = _dx([dyb16], [w_brb], [], "dx_yssm")
    g_w_out, = _dw(merged, [dpre16], BF16, "dw_out")
    g_w_br = jnp.concatenate([_dw(oa, [dya16], BF16, "dw_bra")[0], _dw(y_ssm, [dyb16], BF16, "dw_brb")[0]], axis=0)
    g_w_ple, = _dw(p16, [dpw16], BF16, "dw_ple")

    do_att, dgatt16, own_order = _combine_bwd(dya16, w_bra, nat["gatt"], o_att, lse, dils[1:])
    dseg = {"gatt": dgatt16, "gm": dgm16, "gp": dgp16}
    dbias = []
    for g in range(3):
        cotangent = (do_att, o_att, lse) if g == 0 else (own_order[2 * g - 2], own_order[2 * g - 1])
        dqkv, db = _attn_bwd(qkv[g], bias, g, cotangent, dils[g],
                             "attn_bwd%d" % g)
        dseg["qkv%d" % g] = dqkv.reshape(nb, seq, -1)
        dbias.append(db)
    g_rel = _bias_grad(jnp.concatenate(dbias, axis=0), bmaps)[:, 0, :NUM_BUCKETS].T

    dact, ddtg, dz, ssd_small, g_normw = _ssd_bwd(
        act, nat["dt"], dt_bias_row, nat["z"], y_all, dys, sprev, alog_g, dskip_g, ssm_norm_w)
    dseg["z"] = dz
    dseg["dt"] = jnp.pad(_ungroup_lanes(ddtg), ((0, 0), (0, 0), (0, LANE - SSM_HEADS)))
    dpre, conv_sums = _conv_bwd_pre(dact, nat["xbc"], conv_wg, conv_bg, "conv_bwd")
    dseg["xbc"] = _conv_bwd_x(dpre, conv_wg, "conv_bwd_x")
    csum = _xbc_reference_order(conv_sums)

    dx_own = [_dx([dseg["qkv%d" % g]], [wseg["qkv%d" % g]], [], "dx_qkv%d" % g, True).reshape(
        nb, dils[g], seq // dils[g], D_MODEL) for g in (1, 2)]
    dwseg = {"qkv%d" % g: _dw(x16p[g], [dseg["qkv%d" % g]], BF16, "dw_qkv%d" % g, True)[0] for g in range(3)}
    for gi, group in enumerate(DW_GROUPS):
        dwseg.update(zip(group, _dw(x16, [dseg[s] for s in group], BF16, "dw_nat%d" % gi, True)))
    names = ["qkv0"] + [s for group, _ in NAT_GROUPS for s in group]
    dx = finish_dx([dseg[s] for s in names], [wseg[s] for s in names], [dx], dx_own, dwseg, g_w_br, g_w_out, g_w_ple)

    small = dict(
        b_gate=jnp.stack([mg_sums[0], mg_sums[1], ln_sums[2]]),
        conv_w=csum[0:4], conv_b=csum[4:5],
        dt_bias=_ungroup_lanes(ssd_small[:, 2:3, :]), a_log=_ungroup_lanes(ssd_small[:, 0:1, :]),
        d_skip=_ungroup_lanes(ssd_small[:, 1:2, :]), ssm_norm_w=g_normw,
        ln_g=ln_sums[0:1], ln_b=ln_sums[1:2], rel_bias=g_rel)
    return loss_sum, dx, small


DX_TM = 256
SMALL_ORDER = ("b_gate", "conv_w", "conv_b", "dt_bias", "a_log", "d_skip", "ssm_norm_w", "ln_g", "ln_b", "rel_bias")
SMALL_FULL_SHAPES = dict(b_gate=(3, 1024), conv_w=(4, 3072), conv_b=(1, 3072), dt_bias=(1, 32), a_log=(1, 32),
                         d_skip=(1, 32), ssm_norm_w=(1, 2048), ln_g=(1, 1024), ln_b=(1, 1024), rel_bias=(32, 36))


def kernel(x, p, w_in, b_gate, conv_w, conv_b, dt_bias, a_log, d_skip, ssm_norm_w, w_branch, w_out, w_ple, ln_g, ln_b, rel_bias, loss_target, m_w_in, m_b_gate, m_conv_w, m_conv_b, m_dt_bias, m_a_log, m_d_skip, m_ssm_norm_w, m_w_branch, m_w_out, m_w_ple, m_ln_g, m_ln_b, m_rel_bias, v_w_in, v_b_gate, v_conv_w, v_conv_b, v_dt_bias, v_a_log, v_d_skip, v_ssm_norm_w, v_w_branch, v_w_out, v_w_ple, v_ln_g, v_ln_b, v_rel_bias):
    cx, cy, cc = _place()
    chip = 2 * cx + cy
    dev = 4 * cx + 2 * cy + cc

    w_in_t = jnp.transpose(w_in[0])
    win16 = _shard_to_window(w_in_t, chip)
    g_win, g_br, g_out, g_ple = _allgather_pieces(
        [win16, w_branch[0].astype(BF16), w_out[0].astype(BF16), w_ple[0].astype(BF16)])
    wseg = _assemble(g_win)
    w_br16 = g_br.reshape(4 * 704, D_MODEL)
    w_out16 = g_out.reshape(D_MODEL, D_MODEL)
    w_ple16 = jnp.transpose(g_ple, (1, 0, 2)).reshape(PLE_DIM, D_MODEL)
    shards = _allgather8(_pack_rows([b_gate[0], conv_w[0]]), "allgather_small_params")
    per_chip = [_unpack_rows(shards[2 * k], [(3, 256), (4, 768)]) for k in range(4)]
    b_gate_full = _join_last([pc[0] for pc in per_chip])
    conv_w_full = _join_last([pc[1] for pc in per_chip])

    place = jnp.stack([chip, cc]).astype(jnp.int32)
    reduced = []

    def finish_dx(dhs, ws, accs, own_order_accs, dwseg, d_br, d_out, d_ple):
        grads = [_pack(dwseg), d_br.reshape(4, 704, D_MODEL), d_out.reshape(4, 256, D_MODEL),
                 jnp.transpose(d_ple.reshape(PLE_DIM, 4, 256), (1, 0, 2))]
        got = _sibling_exchange(grads)
        chip_sums = [_pair_sum(g, t, place, "grad_pair_sum_%d" % i) for i, (g, t) in enumerate(zip(grads, got))]
        dx, others = _dx(dhs, ws, accs, "dx_w_in_and_grad_chip_scatter", True, DX_TM, chip_sums, own_order_accs)
        fulls = [_chip_sum(s, t, place, "grad_chip_sum_%d" % i) for i, (s, t) in enumerate(zip(chip_sums, others))]
        reduced.extend(_sibling_gather(fulls))
        return dx

    loss_sum, grad_x, small = _local_grads(
        x, p[0], loss_target, wseg, w_br16, w_out16, w_ple16, b_gate_full, conv_w_full, conv_b, dt_bias, a_log,
        d_skip, ssm_norm_w, ln_g, ln_b, rel_bias, finish_dx)
    big = reduced
    g_w_in = lax.optimization_barrier(_window_to_shard(big[0], chip))
    g_w_branch, g_w_out, g_w_ple = big[1], big[2], big[3]
    parts = _allgather8(_pack_rows([small[n] for n in SMALL_ORDER] + [loss_sum.reshape(1, 1)]),
                        "allgather_small_grads")
    small_sum = _sum_rows([parts[i] for i in range(8)], F32, "small_grad_sum")
    *reduced_small, loss = _unpack_rows(small_sum, [SMALL_FULL_SHAPES[n] for n in SMALL_ORDER] + [(1, 1)])
    loss = loss.reshape(())
    sg = dict(zip(SMALL_ORDER, reduced_small))
    sg["b_gate"] = lax.dynamic_slice_in_dim(sg["b_gate"], chip * 256, 256, axis=1)
    sg["conv_w"] = lax.dynamic_slice_in_dim(sg["conv_w"], chip * 768, 768, axis=1)
    del dev

    upd = {}
    upd["w_in"] = [jnp.transpose(t) for t in _adamw(w_in_t, g_w_in, jnp.transpose(m_w_in[0]),
                                                      jnp.transpose(v_w_in[0]), "adamw_w_in")]
    upd["w_branch"] = _adamw(w_branch[0], g_w_branch, m_w_branch[0], v_w_branch[0], "adamw_w_branch")
    upd["w_out"] = _adamw(w_out[0], g_w_out, m_w_out[0], v_w_out[0], "adamw_w_out")
    upd["w_ple"] = _adamw(w_ple[0], g_w_ple, m_w_ple[0], v_w_ple[0], "adamw_w_ple")
    small_w = dict(b_gate=b_gate, conv_w=conv_w, conv_b=conv_b, dt_bias=dt_bias, a_log=a_log, d_skip=d_skip,
                   ssm_norm_w=ssm_norm_w, ln_g=ln_g, ln_b=ln_b, rel_bias=rel_bias)
    small_m = dict(b_gate=m_b_gate, conv_w=m_conv_w, conv_b=m_conv_b, dt_bias=m_dt_bias, a_log=m_a_log,
                   d_skip=m_d_skip, ssm_norm_w=m_ssm_norm_w, ln_g=m_ln_g, ln_b=m_ln_b, rel_bias=m_rel_bias)
    small_v = dict(b_gate=v_b_gate, conv_w=v_conv_w, conv_b=v_conv_b, dt_bias=v_dt_bias, a_log=v_a_log,
                   d_skip=v_d_skip, ssm_norm_w=v_ssm_norm_w, ln_g=v_ln_g, ln_b=v_ln_b, rel_bias=v_rel_bias)
    for n in SMALL_ORDER:
        sg[n] = sg[n].reshape(small_w[n].shape)
    s_delta, s_m, s_v = _adamw_small(*[[t[n] for n in SMALL_ORDER] for t in (small_w, sg, small_m, small_v)],
                                     "adamw_small")
    for i, n in enumerate(SMALL_ORDER):
        upd[n] = (s_delta[i], s_m[i], s_v[i])

    order = ("w_in", "b_gate", "conv_w", "conv_b", "dt_bias", "a_log", "d_skip", "ssm_norm_w", "w_branch", "w_out",
             "w_ple", "ln_g", "ln_b", "rel_bias")
    grads = dict(sg, w_in=jnp.transpose(g_w_in)[None],w_branch=g_w_branch[None], w_out=g_w_out[None], w_ple=g_w_ple[None])
    lead = lambda n, t: t[None] if n in ("w_in", "w_branch", "w_out", "w_ple") else t
    return (loss, grad_x, *[grads[n] for n in order], *[lead(n, upd[n][0]) for n in order],
            *[lead(n, upd[n][1]) for n in order], *[lead(n, upd[n][2]) for n in order])
```

```python
import math

import numpy as np
import jax
import jax.numpy as jnp
from jax import lax
from jax.experimental import pallas as pl
from jax.experimental.pallas import tpu as pltpu

F32, BF16 = jnp.float32, jnp.bfloat16

D_MODEL = 1024
HEAD_DIM = 64
GROUP_HEADS = 12
ATT_OUT = GROUP_HEADS * HEAD_DIM
PATTERNS = ((128, 1), (512, 4), (2048, 16))
BAND = 128
NUM_BUCKETS = 32
MAX_DISTANCE = 2048
D_INNER = 2048
SSM_HEADS = 32
SSM_GROUPS = 4
GROUP_SSM_HEADS = SSM_HEADS // SSM_GROUPS
D_STATE = 128
CHUNK = 128
PLE_DIM = 256
ALPHA = 2.0 ** 0.25
LN_EPS = 1e-5
RMS_EPS = 1e-5
ADAM_LR, ADAM_B1, ADAM_B2, ADAM_EPS, ADAM_WD, ADAM_STEP = 0.001, 0.9, 0.999, 1e-08, 0.01, 10
NEG = -1e30

QKV_W = 3 * ATT_OUT
IN_COLS = 15904
SHARD_COLS = IN_COLS // 4
DT_COL = 12800
ROW_TILE = 16
WIN_ROWS = 4000


def _win_offset(k):
    return (k * SHARD_COLS) % ROW_TILE


def _win_start(k):
    return k * SHARD_COLS - _win_offset(k)

VMEM_LIMIT_BYTES = 56 * 1024 * 1024
LANE = 128
MESH = pl.DeviceIdType.MESH
NT = (((1,), (1,)), ((), ()))
TN = (((0,), (0,)), ((), ()))


def _pcall(body, **kw):
    return pl.pallas_call(body, **kw)


def _params(*sem):
    return pltpu.CompilerParams(dimension_semantics=sem, vmem_limit_bytes=VMEM_LIMIT_BYTES)


def _sigmoid(v):
    return jax.nn.sigmoid(v)


MM_TM = 512


def _tok_spec(tm, width):
    return pl.BlockSpec((None, tm, width), lambda b, i: (b, i, 0))


def _whole(arr, single_buffer=False):
    mode = dict(pipeline_mode=pl.Buffered(1)) if single_buffer else {}
    return pl.BlockSpec(arr.shape, lambda b, i: (0,) * arr.ndim, **mode)


def _proj(a3, ws, out_dtype, name, w_rows_are_outputs=False, tm=MM_TM):
    nb, seq, kdim = a3.shape
    nw = len(ws)
    widths = [w.shape[0] if w_rows_are_outputs else w.shape[1] for w in ws]

    def body(*refs):
        a = refs[0][...].astype(BF16)
        for w_ref, o_ref in zip(refs[1:1 + nw], refs[1 + nw:]):
            if w_rows_are_outputs:
                v = lax.dot_general(a, w_ref[...], NT, preferred_element_type=F32)
            else:
                v = jnp.dot(a, w_ref[...], preferred_element_type=F32)
            o_ref[...] = v.astype(out_dtype)

    return _pcall(
        body, name=name, grid=(nb, seq // tm),
        in_specs=[_tok_spec(tm, kdim)] + [_whole(w, True) for w in ws],
        out_specs=[_tok_spec(tm, n) for n in widths],
        out_shape=[jax.ShapeDtypeStruct((nb, seq, n), out_dtype) for n in widths],
        compiler_params=_params("parallel", "parallel"),
    )(a3, *ws)


def _dx(dhs, ws, accs, name, w_rows_are_outputs=False, tm=MM_TM, scatter=None, own_order_accs=(),
        own_order_dhs=()):
    nb, seq, _ = dhs[0].shape
    nd, nacc, npa, npd = len(dhs), len(accs), len(own_order_accs), len(own_order_dhs)
    kout = ws[0].shape[1] if w_rows_are_outputs else ws[0].shape[0]
    sums = scatter or []
    ns = len(sums)
    chunks = [_row_chunks(s.shape[1], s.shape[2] * s.dtype.itemsize, ICI_CHUNK_BYTES) for s in sums]
    n_sem = 3 * sum(len(ch) for ch in chunks)
    grid = (nb, seq // tm)
    ntile = kout // LANE if npa or npd else 0

    def body(*refs):
        n_own = 2 * nd + nacc + npa
        n_in = n_own + 2 * npd
        sum_refs, o_ref, got_refs = refs[n_in:n_in + ns], refs[n_in + ns], refs[n_in + ns + 1:n_in + 2 * ns + 1]
        tile_refs = refs[n_in + 2 * ns + 1:n_in + 2 * ns + 1 + ntile]

        def copies():
            send_sems, recv_sems = refs[-2], refs[-1]
            x, y, c = _place()
            out = []
            for a in range(ns):
                for s, m in chunks[a]:
                    for j, (cx, cy) in enumerate(_other_chips(x, y)):
                        k = len(out)
                        out.append(_remote(sum_refs[a].at[2 * cx + cy, pl.ds(s, m)], got_refs[a].at[j, pl.ds(s, m)],
                                           send_sems.at[k], recv_sems.at[k], (cx, cy, c)))
            return out

        if ns:
            @pl.when((pl.program_id(0) == 0) & (pl.program_id(1) == 0))
            def _():
                for cp in copies():
                    cp.start()

        v = None
        for dh_ref, w_ref in zip(refs[:nd], refs[nd:2 * nd]):
            dh = dh_ref[...].astype(BF16)
            if w_rows_are_outputs:
                t = jnp.dot(dh, w_ref[...], preferred_element_type=F32)
            else:
                t = lax.dot_general(dh, w_ref[...], NT, preferred_element_type=F32)
            v = t if v is None else v + t
        for a_ref in refs[2 * nd:2 * nd + nacc]:
            v = v + a_ref[...]
        for p_ref in refs[2 * nd + nacc:n_own]:
            v = v + _natural_rows(p_ref, tile_refs)
        for q_ref, w_ref in zip(refs[n_own:n_own + npd], refs[n_own + npd:n_in]):
            d, per, n = q_ref.shape
            dh = q_ref[...].reshape(d * per, n).astype(BF16)
            if w_rows_are_outputs:
                t = jnp.dot(dh, w_ref[...], preferred_element_type=F32)
            else:
                t = lax.dot_general(dh, w_ref[...], NT, preferred_element_type=F32)
            v = v + _natural_value(t, d, tile_refs)
        o_ref[...] = v

        if ns:
            @pl.when((pl.program_id(0) == grid[0] - 1) & (pl.program_id(1) == grid[1] - 1))
            def _():
                for cp in copies():
                    cp.wait()

    out = _pcall(
        body, name=name, grid=grid,
        in_specs=[_tok_spec(tm, dh.shape[-1]) for dh in dhs] + [_whole(w, True) for w in ws]
        + [_tok_spec(tm, kout)] * nacc
        + [pl.BlockSpec((None, p.shape[1], tm // p.shape[1], kout), lambda b, i: (b, 0, i, 0)) for p in own_order_accs]
        + [pl.BlockSpec((None, q.shape[1], tm // q.shape[1], q.shape[3]), lambda b, i: (b, 0, i, 0))
           for q, _ in own_order_dhs]
        + [_whole(w, True) for _, w in own_order_dhs]
        + [ANY] * ns,
        out_specs=[_tok_spec(tm, kout)] + [ANY] * ns,
        out_shape=[jax.ShapeDtypeStruct((nb, seq, kout), F32)]
        + [jax.ShapeDtypeStruct((3,) + s.shape[1:], s.dtype) for s in sums],
        input_output_aliases={2 * nd: 0} if nacc else {},
        scratch_shapes=[pltpu.VMEM((tm, LANE), F32)] * ntile
        + ([pltpu.SemaphoreType.DMA((n_sem,)), pltpu.SemaphoreType.DMA((n_sem,))] if ns else []),
        compiler_params=pltpu.CompilerParams(
            dimension_semantics=("arbitrary", "arbitrary") if ns else ("parallel", "parallel"),
            vmem_limit_bytes=VMEM_LIMIT_BYTES, has_side_effects=bool(ns)),
    )(*dhs, *ws, *accs, *own_order_accs, *[q for q, _ in own_order_dhs], *[w for _, w in own_order_dhs], *sums)
    return (out[0], list(out[1:])) if ns else out[0]


def _dw(a3, dhs, out_dtype, name, rows_are_outputs=False):
    nb, seq, kdim = a3.shape
    nd = len(dhs)
    grid = (nb, seq // MM_TM)
    shapes = [(dh.shape[-1], kdim) if rows_are_outputs else (kdim, dh.shape[-1]) for dh in dhs]

    def body(*refs):
        b, i = pl.program_id(0), pl.program_id(1)
        dh_refs, o_refs, acc_refs = refs[1:1 + nd], refs[1 + nd:1 + 2 * nd], refs[1 + 2 * nd:]

        @pl.when((b == 0) & (i == 0))
        def _():
            for acc_ref in acc_refs:
                acc_ref[...] = jnp.zeros_like(acc_ref)

        a = refs[0][...].astype(BF16)
        for dh_ref, acc_ref in zip(dh_refs, acc_refs):
            dh = dh_ref[...].astype(BF16)
            acc_ref[...] += lax.dot_general(*((dh, a) if rows_are_outputs else (a, dh)), TN,
                                            preferred_element_type=F32)

        @pl.when((b == grid[0] - 1) & (i == grid[1] - 1))
        def _():
            for o_ref, acc_ref in zip(o_refs, acc_refs):
                o_ref[...] = acc_ref[...].astype(out_dtype)

    return _pcall(
        body, name=name, grid=grid,
        in_specs=[_tok_spec(MM_TM, kdim)] + [_tok_spec(MM_TM, dh.shape[-1]) for dh in dhs],
        out_specs=[pl.BlockSpec(s, lambda b, i: (0, 0)) for s in shapes],
        out_shape=[jax.ShapeDtypeStruct(s, out_dtype) for s in shapes],
        scratch_shapes=[pltpu.VMEM(s, F32) for s in shapes],
        compiler_params=_params("arbitrary", "arbitrary"),
    )(a3, *dhs)


def _qkv_rows(g):
    return [(part * QKV_W + g * ATT_OUT + hp * LANE, LANE) for hp in range(ATT_OUT // LANE) for part in range(3)]


XBC_START = 3 * QKV_W + ATT_OUT + D_INNER
GROUP_CH = GROUP_SSM_HEADS * HEAD_DIM
XBC_GROUP = GROUP_CH + 2 * D_STATE
CONV_DIM = SSM_GROUPS * XBC_GROUP


def _xbc_ranges():
    out = []
    for g in range(SSM_GROUPS):
        out += [(g * GROUP_CH, GROUP_CH), (D_INNER + g * D_STATE, D_STATE),
                (D_INNER + SSM_GROUPS * D_STATE + g * D_STATE, D_STATE)]
    return out


def _join_last(parts):
    widths = [t.shape[-1] for t in parts]
    total, lead = sum(widths), [(0, 0)] * (parts[0].ndim - 1)
    starts = np.cumsum([0] + widths)
    return sum(jnp.pad(t, lead + [(int(s), total - int(s) - w)]) for t, s, w in zip(parts, starts, widths))


def _xbc_group_order(t):
    return _join_last([t[..., s:s + n] for s, n in _xbc_ranges()])


def _xbc_reference_order(t):
    g = lambda off, n: [t[..., k * XBC_GROUP + off:k * XBC_GROUP + off + n] for k in range(SSM_GROUPS)]
    return _join_last(g(0, GROUP_CH) + g(GROUP_CH, D_STATE) + g(GROUP_CH + D_STATE, D_STATE))


def _segments():
    one = lambda name, start, rows: (name, [(start, rows)], max(rows, LANE))
    return [("qkv%d" % g, _qkv_rows(g), QKV_W) for g in range(3)] + [
        one("gatt", 3 * QKV_W, ATT_OUT), one("z", 3 * QKV_W + ATT_OUT, D_INNER),
        ("xbc", [(XBC_START + s, n) for s, n in _xbc_ranges()], CONV_DIM), one("dt", DT_COL, SSM_HEADS),
        one("gm", DT_COL + SSM_HEADS, 2 * D_MODEL), one("gp", DT_COL + SSM_HEADS + 2 * D_MODEL, D_MODEL)]


LAYOUT_TC = 256
NAT_GROUPS = ((("gatt", "z", "dt", "gp"), 512), (("xbc", "gm"), 512))
DW_GROUPS = (("gatt", "z", "dt", "gp"), ("xbc",), ("gm",))


def _assemble(win):
    segs = _segments()

    def body(win_ref, *outs):
        def pieces(start, rows):
            t, end = start, start + rows
            while t < end:
                k = min(t // SHARD_COLS, 3)
                shard_end = (k + 1) * SHARD_COLS
                if k < 3 and shard_end % ROW_TILE and t == shard_end - shard_end % ROW_TILE:
                    lo = t - _win_start(k)
                    yield win_ref[k, lo:lo + ROW_TILE, :] + win_ref[k + 1, 0:ROW_TILE, :]
                    t += ROW_TILE
                    continue
                upto = min(end, shard_end - shard_end % ROW_TILE if k < 3 else end)
                yield win_ref[k, t - _win_start(k):upto - _win_start(k), :]
                t = upto

        for (_, ranges, total), o_ref in zip(segs, outs):
            off = 0
            for start, rows in ranges:
                for part in pieces(start, rows):
                    o_ref[off:off + part.shape[0], :] = part
                    off += part.shape[0]
            if off < total:
                o_ref[off:total, :] = jnp.zeros((total - off, o_ref.shape[1]), BF16)

    outs = _pcall(
        body, name="assemble_w_in", grid=(D_MODEL // LAYOUT_TC,),
        in_specs=[pl.BlockSpec((4, WIN_ROWS, LAYOUT_TC), lambda i: (0, 0, i))],
        out_specs=[pl.BlockSpec((total, LAYOUT_TC), lambda i: (0, i)) for _, _, total in segs],
        out_shape=[jax.ShapeDtypeStruct((total, D_MODEL), BF16) for _, _, total in segs],
        compiler_params=_params("parallel"),
    )(win)
    return {name: o for (name, _, _), o in zip(segs, outs)}


def _pack(dsegs):
    segs = _segments()

    def body(*refs):
        ins, o_ref = refs[:-1], refs[-1]
        tail = IN_COLS - _win_start(3)
        o_ref[3, tail:, :] = jnp.zeros((WIN_ROWS - tail, o_ref.shape[2]), BF16)
        for (_, ranges, _), s_ref in zip(segs, ins):
            off = 0
            for start, rows in ranges:
                for k in range(4):
                    lo = _win_start(k)
                    a, b = max(start, lo), min(start + rows, lo + WIN_ROWS)
                    if a < b:
                        o_ref[k, a - lo:b - lo, :] = s_ref[off + a - start:off + b - start, :]
                off += rows

    return _pcall(
        body, name="pack_dw_in", grid=(D_MODEL // LAYOUT_TC,),
        in_specs=[pl.BlockSpec((total, LAYOUT_TC), lambda i: (0, i)) for _, _, total in segs],
        out_specs=pl.BlockSpec((4, WIN_ROWS, LAYOUT_TC), lambda i: (0, 0, i)),
        out_shape=jax.ShapeDtypeStruct((4, WIN_ROWS, D_MODEL), BF16),
        compiler_params=_params("parallel"),
    )(*[dsegs[name] for name, _, _ in segs])


def _shard_to_window(shard_t, k):
    def at(off):
        return lambda w: jnp.pad(w.astype(BF16), ((off, WIN_ROWS - SHARD_COLS - off), (0, 0)))

    return lax.cond(k % 2 == 1, at(_win_offset(1)), at(_win_offset(0)), shard_t)


def _window_to_shard(win, k):
    return lax.dynamic_slice(win, ((k % 2) * _win_offset(1), 0), (SHARD_COLS, D_MODEL))


def _bucket_maps():
    qi = np.arange(8)[:, None]
    kj = np.arange(2 * BAND)[None, :]
    delta = qi + BAND - kj
    maps = []
    for window, dil in PATTERNS:
        valid = (delta >= 0) & (delta <= window // dil)
        dist = np.maximum(delta, 0) * dil
        max_exact = NUM_BUCKETS // 2
        d_f = np.maximum(dist, 1).astype(np.float32)
        large = max_exact + (np.log(d_f / np.float32(max_exact)) / np.float32(math.log(MAX_DISTANCE / max_exact))
                             * np.float32(NUM_BUCKETS - max_exact)).astype(np.int32)
        large = np.minimum(large, NUM_BUCKETS - 1)
        bucket = np.where(dist < max_exact, dist, large)
        maps.append(np.where(valid, bucket, -1).astype(np.int32))
    return np.stack(maps)


def _bias_tables(rel_bias, bmaps):
    def body(rb_ref, bm_ref, o_ref):
        g = pl.program_id(0)
        bm = bm_ref[...]
        for hh in range(GROUP_HEADS):
            acc = jnp.full(bm.shape, NEG, F32)
            for b in range(NUM_BUCKETS):
                acc = jnp.where(bm == b, rb_ref[b, g * GROUP_HEADS + hh], acc)
            for a in range(BAND // 8):
                o_ref[hh, 8 * a:8 * a + 8, :] = acc if a == 0 else pltpu.roll(acc, 8 * a, 1)

    return _pcall(
        body, name="bias_tables", grid=(3,),
        in_specs=[pl.BlockSpec(memory_space=pltpu.SMEM),
                  pl.BlockSpec((None, 8, 2 * BAND), lambda g: (g, 0, 0))],
        out_specs=pl.BlockSpec((GROUP_HEADS, BAND, 2 * BAND), lambda g: (g, 0, 0)),
        out_shape=jax.ShapeDtypeStruct((3 * GROUP_HEADS, BAND, 2 * BAND), F32),
        compiler_params=_params("parallel"),
    )(rel_bias, bmaps)


def _bias_grad(dbias, bmaps):
    def body(db_ref, bm_ref, o_ref):
        bm = bm_ref[...]
        lane = lax.broadcasted_iota(jnp.int32, (1, LANE), 1)
        for hh in range(GROUP_HEADS):
            db = db_ref[hh, 0:8, :]
            for a in range(1, BAND // 8):
                db = db + pltpu.roll(db_ref[hh, 8 * a:8 * a + 8, :], 2 * BAND - 8 * a, 1)
            vec = jnp.zeros((1, LANE), F32)
            for b in range(NUM_BUCKETS):
                s = jnp.sum(jnp.where(bm == b, db, 0.0), keepdims=True)
                vec = jnp.where(lane == b, s, vec)
            o_ref[hh] = vec

    return _pcall(
        body, name="bias_grad", grid=(3,),
        in_specs=[pl.BlockSpec((GROUP_HEADS, BAND, 2 * BAND), lambda g: (g, 0, 0)),
                  pl.BlockSpec((None, 8, 2 * BAND), lambda g: (g, 0, 0))],
        out_specs=pl.BlockSpec((GROUP_HEADS, 1, LANE), lambda g: (g, 0, 0)),
        out_shape=jax.ShapeDtypeStruct((3 * GROUP_HEADS, 1, LANE), F32),
        compiler_params=_params("parallel"),
    )(dbias, bmaps)


def _rows(n):
    if isinstance(n, int):
        return pl.ds(n * BAND, BAND)
    return pl.ds(pl.multiple_of(n * BAND, BAND), BAND)


def _for_blocks(blocks, nblk, per, carry):
    carry = blocks([0], carry, False)
    start = 1 + (nblk - 1) % per
    for n in range(1, start):
        carry = blocks([n], carry, True)
    trips = (nblk - start) // per
    if trips > 0:
        carry = lax.fori_loop(
            0, trips, lambda t, c: blocks([start + t * per + u for u in range(per)], c, True), carry)
    return carry


def _pairs_per_step(d):
    return {1: 3, 4: 6, 16: 6}[d]


def _bias_spec(group, hps):
    first = group * GROUP_HEADS // (2 * hps)
    return pl.BlockSpec((2 * hps, BAND, 2 * BAND), lambda hp, b, r: (first + hp, 0, 0))


def _attn_fwd(qkv4, bias, group, d, name):
    nb, _, sub, _ = qkv4.shape
    nblk = sub // BAND
    scale = HEAD_DIM ** -0.5
    npair = ATT_OUT // LANE
    hps = _pairs_per_step(d)
    compact = d > 1

    def body(qkv_ref, bias_ref, o_ref, l_ref):
        def blocks(ns, carry, with_prev):
            chains = [(bi, i, h) for bi in range(len(ns)) for i in range(hps) for h in range(2)]
            first_head = lax.broadcasted_iota(jnp.int32, (BAND, LANE), 1) < HEAD_DIM
            pair = lambda n, i, part: qkv_ref[_rows(n), (3 * i + part) * LANE:(3 * i + part + 1) * LANE]
            scores = []
            for bi, i, h in chains:
                n = ns[bi]
                qp = pair(n, i, 0) * scale
                q = jnp.where(first_head if h == 0 else jnp.logical_not(first_head), qp, jnp.zeros_like(qp))
                s_c = lax.dot_general(q, pair(n, i, 1), NT, preferred_element_type=F32) + bias_ref[2 * i + h, :, BAND:]
                s_p = None
                if with_prev:
                    s_p = lax.dot_general(q, pair(n - 1, i, 1), NT,
                                          preferred_element_type=F32) + bias_ref[2 * i + h, :, :BAND]
                scores.append((s_c, s_p))
            probs = []
            for s_c, s_p in scores:
                m = jnp.max(s_c, -1, keepdims=True)
                if with_prev:
                    m = jnp.maximum(m, jnp.max(s_p, -1, keepdims=True))
                e_c = jnp.exp(s_c - m)
                den = jnp.sum(e_c, -1, keepdims=True)
                e_p = None
                if with_prev:
                    e_p = jnp.exp(s_p - m)
                    den = den + jnp.sum(e_p, -1, keepdims=True)
                    e_p = e_p.astype(BF16)
                probs.append((e_c.astype(BF16), e_p, den, m))
            outs = {}
            for (bi, i, h), (e_c, e_p, den, m) in zip(chains, probs):
                n = ns[bi]
                acc = jnp.dot(e_c, pair(n, i, 2), preferred_element_type=F32)
                if with_prev:
                    acc = acc + jnp.dot(e_p, pair(n - 1, i, 2), preferred_element_type=F32)
                outs[(bi, i, h)] = (acc / den, m + jnp.log(den))
            lane = lax.broadcasted_iota(jnp.int32, (BAND, LANE), 1)
            for bi, n in enumerate(ns):
                per_head = jnp.zeros((BAND, LANE), F32)
                for i in range(hps):
                    o_ref[_rows(n), i * LANE:(i + 1) * LANE] = jnp.where(first_head, outs[(bi, i, 0)][0],
                                                                         outs[(bi, i, 1)][0])
                    if compact:
                        for h in range(2):
                            per_head = jnp.where(lane == 2 * i + h, outs[(bi, i, h)][1], per_head)
                    else:
                        l_ref[_rows(n), i * LANE:(i + 1) * LANE] = jnp.where(first_head, outs[(bi, i, 0)][1],
                                                                             outs[(bi, i, 1)][1])
                if compact:
                    l_ref[_rows(n), :] = per_head
            return carry

        _for_blocks(blocks, nblk, 2 if hps == 1 else 1, 0)

    in_specs = [pl.BlockSpec((None, None, sub, 3 * LANE * hps), lambda hp, b, r: (b, r, 0, hp)),
                _bias_spec(group, hps)]
    if compact:
        return _pcall(
            body, name=name, grid=(1, nb, d), in_specs=in_specs,
            out_specs=[pl.BlockSpec((None, None, sub, ATT_OUT), lambda hp, b, r: (b, r, 0, 0)),
                       pl.BlockSpec((None, None, sub, LANE), lambda hp, b, r: (b, r, 0, 0))],
            out_shape=[jax.ShapeDtypeStruct((nb, d, sub, ATT_OUT), F32), jax.ShapeDtypeStruct((nb, d, sub, LANE), F32)],
            compiler_params=_params("parallel", "parallel", "parallel"),
        )(qkv4, bias)
    ospec = pl.BlockSpec((None, sub, hps * LANE), lambda hp, b, r: (b, 0, r * (npair // hps) + hp))
    return _pcall(
        body, name=name, grid=(npair // hps, nb, d), in_specs=in_specs, out_specs=[ospec, ospec],
        out_shape=[jax.ShapeDtypeStruct((nb, sub, d * ATT_OUT), F32)] * 2,
        compiler_params=_params("parallel", "parallel", "parallel"),
    )(qkv4, bias)


STAT_LSE_LANE = 16


def _attn_bwd(qkv4, bias, group, cotangent, d, name):
    nb, _, sub, _ = qkv4.shape
    nblk = sub // BAND
    scale = HEAD_DIM ** -0.5
    npair = ATT_OUT // LANE
    hps = _pairs_per_step(d)
    compact = d > 1

    def body(qkv_ref, bias_ref, *rest):
        do_ref, dqkv_ref, db_ref = rest[0], rest[-2], rest[-1]
        b, r = pl.program_id(1), pl.program_id(2)

        @pl.when((b == 0) & (r == 0))
        def _():
            db_ref[...] = jnp.zeros_like(db_ref)

        def blocks(ns, carry, with_prev):
            sides = (0, 1) if with_prev else (0,)
            chains = [(bi, i, h, sd) for bi in range(len(ns)) for i in range(hps) for h in range(2) for sd in sides]
            first_head = lax.broadcasted_iota(jnp.int32, (BAND, LANE), 1) < HEAD_DIM
            own = lambda h, t: jnp.where(first_head if h == 0 else jnp.logical_not(first_head), t, jnp.zeros_like(t))
            pair = lambda rows, i, part: qkv_ref[rows, (3 * i + part) * LANE:(3 * i + part + 1) * LANE]
            key_rows = lambda bi, sd: _rows(ns[bi] - sd)
            qs = {}
            for bi in range(len(ns)):
                for i in range(hps):
                    q_pair = pair(_rows(ns[bi]), i, 0) * scale
                    do = do_ref[_rows(ns[bi]), i * LANE:(i + 1) * LANE]
                    do16 = do.astype(BF16)
                    for h in range(2):
                        if compact:
                            st_ref, head = rest[1], 2 * i + h
                            ebar = st_ref[_rows(ns[bi]), head:head + 1]
                            lcol = st_ref[_rows(ns[bi]), STAT_LSE_LANE + head:STAT_LSE_LANE + head + 1]
                        else:
                            ebar = jnp.sum(own(h, do * rest[1][_rows(ns[bi]), i * LANE:(i + 1) * LANE]), -1, keepdims=True)
                            lcol = rest[2][_rows(ns[bi]), i * LANE + h * HEAD_DIM:i * LANE + h * HEAD_DIM + 1]
                        qs[(bi, i, h)] = (own(h, q_pair), q_pair, own(h, do16), do16, ebar, lcol)
            raw = []
            for bi, i, h, sd in chains:
                q, _, do_h, _, _, _ = qs[(bi, i, h)]
                bias_blk = bias_ref[2 * i + h, :, :BAND] if sd else bias_ref[2 * i + h, :, BAND:]
                s = lax.dot_general(q, pair(key_rows(bi, sd), i, 1), NT, preferred_element_type=F32) + bias_blk
                dp = lax.dot_general(do_h, pair(key_rows(bi, sd), i, 2), NT, preferred_element_type=F32)
                raw.append((s, dp))
            soft = []
            for (bi, i, h, sd), (s, dp) in zip(chains, raw):
                ebar, lcol = qs[(bi, i, h)][4:]
                p = jnp.exp(s - lcol)
                ds = p * (dp - ebar)
                if sd:
                    db_ref[2 * i + h, :, :BAND] += ds
                else:
                    db_ref[2 * i + h, :, BAND:] += ds
                soft.append((p.astype(BF16), ds.astype(BF16)))
            grads = {}
            for (bi, i, h, sd), (p16, ds16) in zip(chains, soft):
                _, q_pair, _, do16 = qs[(bi, i, h)][:4]
                grads[(bi, i, h, sd)] = (
                    jnp.dot(ds16, pair(key_rows(bi, sd), i, 1), preferred_element_type=F32),
                    lax.dot_general(ds16, q_pair, TN, preferred_element_type=F32),
                    lax.dot_general(p16, do16, TN, preferred_element_type=F32))
            both = lambda bi, i, sd, which: jnp.where(first_head, grads[(bi, i, 0, sd)][which],
                                                      grads[(bi, i, 1, sd)][which])
            carry = list(carry) if carry is not None else None
            for bi, n in enumerate(ns):
                for i in range(hps):
                    base = 3 * LANE * i
                    dq = both(bi, i, 0, 0)
                    if with_prev:
                        dq = dq + both(bi, i, 1, 0)
                        dqkv_ref[_rows(n - 1), base + LANE:base + 2 * LANE] = (
                            carry[2 * i] + both(bi, i, 1, 1)).astype(BF16)
                        dqkv_ref[_rows(n - 1), base + 2 * LANE:base + 3 * LANE] = (
                            carry[2 * i + 1] + both(bi, i, 1, 2)).astype(BF16)
                    dqkv_ref[_rows(n), base:base + LANE] = (dq * scale).astype(BF16)
                carry = [t for i in range(hps) for t in (both(bi, i, 0, 1), both(bi, i, 0, 2))]
            return tuple(carry)

        carry = _for_blocks(blocks, nblk, 2 if hps == 1 else 1, None)
        for i in range(hps):
            base = 3 * LANE * i
            dqkv_ref[_rows(nblk - 1), base + LANE:base + 2 * LANE] = carry[2 * i].astype(BF16)
            dqkv_ref[_rows(nblk - 1), base + 2 * LANE:base + 3 * LANE] = carry[2 * i + 1].astype(BF16)

    qspec = pl.BlockSpec((None, None, sub, 3 * LANE * hps), lambda hp, b, r: (b, r, 0, hp))
    bspec = pl.BlockSpec((2 * hps, BAND, 2 * BAND), lambda hp, b, r: (hp, 0, 0))
    if compact:
        cspecs = [pl.BlockSpec((None, None, sub, ATT_OUT), lambda hp, b, r: (b, r, 0, 0)),
                  pl.BlockSpec((None, None, sub, LANE), lambda hp, b, r: (b, r, 0, 0))]
    else:
        cspecs = [pl.BlockSpec((None, sub, hps * LANE), lambda hp, b, r: (b, 0, r * (npair // hps) + hp))] * 3
    return _pcall(
        body, name=name, grid=(npair // hps, nb, d),
        in_specs=[qspec, _bias_spec(group, hps)] + cspecs, out_specs=[qspec, bspec],
        out_shape=[jax.ShapeDtypeStruct(qkv4.shape, BF16),
                   jax.ShapeDtypeStruct((GROUP_HEADS, BAND, 2 * BAND), F32)],
        compiler_params=_params("parallel", "arbitrary", "arbitrary"),
    )(qkv4, bias, *cotangent)


def _head_lanes(first_lane, one_channel):
    c = lax.broadcasted_iota(jnp.int32, (ATT_OUT, LANE), 0)
    lane = lax.broadcasted_iota(jnp.int32, (ATT_OUT, LANE), 1)
    hit = lane == first_lane + c // HEAD_DIM
    if one_channel:
        hit = hit & (c % HEAD_DIM == 0)
    return hit.astype(BF16)


def _exact_dot(v, m01, dims=None):
    parts = _split3(v)
    if dims is None:
        dot = lambda t: jnp.dot(t, m01, preferred_element_type=F32)
    else:
        dot = lambda t: lax.dot_general(t, m01, dims, preferred_element_type=F32)
    return (dot(parts[0]) + dot(parts[1])) + dot(parts[2])


def _store_own_order(value, tile_refs, out_ref):
    d, per, width = out_ref.shape
    for j in range(width // LANE):
        tile_refs[j][...] = value[:, j * LANE:(j + 1) * LANE]
    for r in range(d):
        rows = pl.ds(r, per, stride=d)
        for j in range(width // LANE):
            out_ref[r, :, j * LANE:(j + 1) * LANE] = tile_refs[j][rows, :].astype(out_ref.dtype)


def _token_orders(x, dilations):
    nb, seq, kdim = x.shape
    tm = 512

    def body(x_ref, nat_ref, *rest):
        outs, tile_refs = rest[:len(dilations)], rest[len(dilations):]
        xv = x_ref[...]
        nat_ref[...] = xv.astype(BF16)
        for o_ref in outs:
            _store_own_order(xv, tile_refs, o_ref)

    outs = _pcall(
        body, name="token_orders", grid=(nb, seq // tm), in_specs=[_tok_spec(tm, kdim)],
        out_specs=[_tok_spec(tm, kdim)]
        + [pl.BlockSpec((None, d, tm // d, kdim), lambda b, i: (b, 0, i, 0)) for d in dilations],
        out_shape=[jax.ShapeDtypeStruct((nb, seq, kdim), BF16)]
        + [jax.ShapeDtypeStruct((nb, d, seq // d, kdim), BF16) for d in dilations],
        scratch_shapes=[pltpu.VMEM((tm, LANE), F32)] * (kdim // LANE),
        compiler_params=_params("parallel", "parallel"),
    )(x)
    return [outs[0]] + [o.reshape(nb, seq, kdim) for o in outs[1:]]


def _natural_rows(p_ref, tile_refs):
    d, per, width = p_ref.shape
    for r in range(d):
        rows = pl.ds(r, per, stride=d)
        for j in range(width // LANE):
            tile_refs[j][rows, :] = p_ref[r, :, j * LANE:(j + 1) * LANE]
    return jnp.concatenate([tile_refs[j][...] for j in range(width // LANE)], axis=1)


def _natural_value(value, d, tile_refs):
    total, width = value.shape
    per = total // d
    for r in range(d):
        rows = pl.ds(r, per, stride=d)
        for j in range(width // LANE):
            tile_refs[j][rows, :] = value[r * per:(r + 1) * per, j * LANE:(j + 1) * LANE]
    return jnp.concatenate([tile_refs[j][...] for j in range(width // LANE)], axis=1)


def _combine_fwd(o0, l0, dilated, gatt):
    nb, seq, _ = gatt.shape
    tm = 512
    ntile = ATT_OUT // LANE

    def body(o0_ref, l0_ref, o1_ref, l1_ref, o2_ref, l2_ref, g_ref, oa_ref, oatt_ref, lse_ref, *tile_refs):
        spread = _head_lanes(0, False)
        l0v = l0_ref[...]
        l1v = _exact_dot(_natural_rows(l1_ref, tile_refs), spread, NT)
        l2v = _exact_dot(_natural_rows(l2_ref, tile_refs), spread, NT)
        m = jnp.maximum(jnp.maximum(l0v, l1v), l2v)
        tot = m + jnp.log(jnp.exp(l0v - m) + jnp.exp(l1v - m) + jnp.exp(l2v - m))
        o = jnp.exp(l0v - tot) * o0_ref[...]
        o = o + jnp.exp(l1v - tot) * _natural_rows(o1_ref, tile_refs)
        o = o + jnp.exp(l2v - tot) * _natural_rows(o2_ref, tile_refs)
        g = g_ref[...]
        oa_ref[...] = (o * (g * _sigmoid(g))).astype(BF16)
        oatt_ref[...] = o
        lse_ref[...] = tot

    spec = pl.BlockSpec((None, tm, ATT_OUT), lambda b, i: (b, i, 0))
    own = lambda t: pl.BlockSpec((None, t.shape[1], tm // t.shape[1], t.shape[3]), lambda b, i: (b, 0, i, 0))
    (o1, l1), (o2, l2) = dilated
    return _pcall(
        body, name="attn_combine", grid=(nb, seq // tm),
        in_specs=[spec, spec, own(o1), own(l1), own(o2), own(l2), spec], out_specs=[spec] * 3,
        out_shape=[jax.ShapeDtypeStruct((nb, seq, ATT_OUT), BF16), jax.ShapeDtypeStruct((nb, seq, ATT_OUT), F32),
                   jax.ShapeDtypeStruct((nb, seq, ATT_OUT), F32)],
        scratch_shapes=[pltpu.VMEM((tm, LANE), F32)] * ntile,
        compiler_params=_params("parallel", "parallel"),
    )(o0, l0, o1, l1, o2, l2, gatt)


def _combine_bwd(dya16, w_bra, gatt, o_att, lse, dilations):
    nb, seq, _ = gatt.shape
    tm = 512

    def body(dya_ref, w_ref, g_ref, o_ref, l_ref, do_ref, dg_ref, *rest):
        ntile = ATT_OUT // LANE
        outs, tile_refs = rest[:-ntile], rest[-ntile:]
        doa = lax.dot_general(dya_ref[...], w_ref[...], NT, preferred_element_type=F32)
        g = g_ref[...]
        sg = _sigmoid(g)
        do = doa * (g * sg)
        do_ref[...] = do
        stats = (_exact_dot(do * o_ref[...], _head_lanes(0, False))
                 + _exact_dot(l_ref[...], _head_lanes(STAT_LSE_LANE, True)))
        dg_ref[...] = (doa * o_ref[...] * (sg * (1.0 + g * (1.0 - sg)))).astype(BF16)
        for k in range(len(dilations)):
            _store_own_order(do, tile_refs, outs[2 * k])
            _store_own_order(stats, tile_refs, outs[2 * k + 1])

    spec = pl.BlockSpec((None, tm, ATT_OUT), lambda b, i: (b, i, 0))
    own = lambda d, width: pl.BlockSpec((None, d, tm // d, width), lambda b, i: (b, 0, i, 0))
    outs = _pcall(
        body, name="attn_combine_bwd", grid=(nb, seq // tm),
        in_specs=[_tok_spec(tm, D_MODEL), _whole(w_bra, True)] + [spec] * 3,
        out_specs=[spec, spec] + [own(d, w) for d in dilations for w in (ATT_OUT, LANE)],
        out_shape=[jax.ShapeDtypeStruct((nb, seq, ATT_OUT), F32), jax.ShapeDtypeStruct((nb, seq, ATT_OUT), BF16)]
        + [jax.ShapeDtypeStruct((nb, d, seq // d, w), t) for d in dilations for w, t in ((ATT_OUT, BF16), (LANE, F32))],
        scratch_shapes=[pltpu.VMEM((tm, LANE), F32)] * (ATT_OUT // LANE),
        compiler_params=_params("parallel", "parallel"),
    )(dya16, w_bra, gatt, o_att, lse)
    return outs[0], outs[1], outs[2:]


CONV_TM = 1024
CONV_TC = 1024


def _shift_down(cur, halo, k):
    rolled = pltpu.roll(cur, k, 0)
    hro = pltpu.roll(halo, k, 0)
    row = lax.broadcasted_iota(jnp.int32, hro.shape, 0)
    return jnp.concatenate([jnp.where(row < k, hro, rolled[:8]), rolled[8:]], axis=0)


def _shift_up(cur, halo, k):
    n = cur.shape[0]
    rolled = pltpu.roll(cur, n - k, 0)
    hro = pltpu.roll(halo, 8 - k, 0)
    row = lax.broadcasted_iota(jnp.int32, hro.shape, 0)
    return jnp.concatenate([rolled[:n - 8], jnp.where(row >= 8 - k, hro, rolled[n - 8:])], axis=0)


def _conv_pre(cur, halo, w_ref, b_ref):
    acc = cur * w_ref[3:4, :] + b_ref[...]
    for k in range(1, 4):
        acc = acc + _shift_down(cur, halo, k) * w_ref[3 - k:4 - k, :]
    return acc


def _conv_specs(seq):
    nblk = seq // CONV_TM
    cur = pl.BlockSpec((None, CONV_TM, CONV_TC), lambda cb, b, i: (b, i, cb))
    prev = pl.BlockSpec((None, 8, CONV_TC), lambda cb, b, i: (b, jnp.maximum(i * (CONV_TM // 8) - 1, 0), cb))
    nxt = pl.BlockSpec((None, 8, CONV_TC),
                       lambda cb, b, i: (b, jnp.minimum((i + 1) * (CONV_TM // 8), seq // 8 - 1), cb))
    wspec = pl.BlockSpec((4, CONV_TC), lambda cb, b, i: (0, cb))
    bspec = pl.BlockSpec((1, CONV_TC), lambda cb, b, i: (0, cb))
    return nblk, cur, prev, nxt, wspec, bspec


def _conv_fwd(xin, w4, bias, name):
    nb, seq, ch = xin.shape
    _, cur, prev, _, wspec, bspec = _conv_specs(seq)

    def body(x_ref, h_ref, w_ref, b_ref, o_ref):
        halo = jnp.where(pl.program_id(2) > 0, h_ref[...], 0.0)
        pre = _conv_pre(x_ref[...], halo, w_ref, b_ref)
        o_ref[...] = pre * _sigmoid(pre)

    return _pcall(
        body, name=name, grid=(ch // CONV_TC, nb, seq // CONV_TM),
        in_specs=[cur, prev, wspec, bspec], out_specs=cur,
        out_shape=jax.ShapeDtypeStruct(xin.shape, F32),
        compiler_params=_params("parallel", "parallel", "parallel"),
    )(xin, xin, w4, bias)


def _conv_bwd_pre(dact, xin, w4, bias, name):
    nb, seq, ch = xin.shape
    _, cur, prev, _, wspec, bspec = _conv_specs(seq)

    def body(da_ref, x_ref, h_ref, w_ref, b_ref, dp_ref, s_ref):
        b, i = pl.program_id(1), pl.program_id(2)

        @pl.when((b == 0) & (i == 0))
        def _():
            s_ref[...] = jnp.zeros_like(s_ref)

        halo = jnp.where(i > 0, h_ref[...], 0.0)
        x = x_ref[...]
        pre = _conv_pre(x, halo, w_ref, b_ref)
        sg = _sigmoid(pre)
        dpre = da_ref[...] * (sg * (1.0 + pre * (1.0 - sg)))
        dp_ref[...] = dpre
        s_ref[3:4, :] += jnp.sum(dpre * x, 0, keepdims=True)
        for k in range(1, 4):
            s_ref[3 - k:4 - k, :] += jnp.sum(dpre * _shift_down(x, halo, k), 0, keepdims=True)
        s_ref[4:5, :] += jnp.sum(dpre, 0, keepdims=True)

    return _pcall(
        body, name=name, grid=(ch // CONV_TC, nb, seq // CONV_TM),
        in_specs=[cur, cur, prev, wspec, bspec],
        out_specs=[cur, pl.BlockSpec((8, CONV_TC), lambda cb, b, i: (0, cb))],
        out_shape=[jax.ShapeDtypeStruct(xin.shape, F32), jax.ShapeDtypeStruct((8, ch), F32)],
        compiler_params=_params("parallel", "arbitrary", "arbitrary"),
    )(dact, xin, xin, w4, bias)


def _conv_bwd_x(dpre, w4, name):
    nb, seq, ch = dpre.shape
    nblk, cur, _, nxt, wspec, _ = _conv_specs(seq)

    def body(d_ref, n_ref, w_ref, o_ref):
        halo = jnp.where(pl.program_id(2) < nblk - 1, n_ref[...], 0.0)
        cur_v = d_ref[...]
        acc = cur_v * w_ref[3:4, :]
        for j in range(1, 4):
            acc = acc + _shift_up(cur_v, halo, j) * w_ref[3 - j:4 - j, :]
        o_ref[...] = acc.astype(BF16)

    return _pcall(
        body, name=name, grid=(ch // CONV_TC, nb, seq // CONV_TM),
        in_specs=[cur, nxt, wspec], out_specs=cur,
        out_shape=jax.ShapeDtypeStruct(dpre.shape, BF16),
        compiler_params=_params("parallel", "parallel", "parallel"),
    )(dpre, dpre, w4)


def _step_sizes(raw, tb_ref):
    shift = (LANE - GROUP_SSM_HEADS * pl.program_id(0)) % LANE
    v = pltpu.roll(raw + tb_ref[...], shift, 1)
    own = lax.broadcasted_iota(jnp.int32, v.shape, 1) < GROUP_SSM_HEADS
    sp = jnp.maximum(v, 0.0) + jnp.log1p(jnp.exp(-jnp.abs(v)))
    return jnp.where(own, sp, 0.0), jnp.where(own, _sigmoid(v), 0.0)


def _group_lanes(t):
    pads = [(0, 0)] * (t.ndim - 1) + [(0, LANE - GROUP_SSM_HEADS)]
    return jnp.stack([jnp.pad(t[..., GROUP_SSM_HEADS * g:GROUP_SSM_HEADS * (g + 1)], pads) for g in range(SSM_GROUPS)])


def _ungroup_lanes(t):
    return jnp.concatenate([t[g][..., :GROUP_SSM_HEADS] for g in range(SSM_GROUPS)], axis=-1)


def _decays(dt, al_ref):
    row = lax.broadcasted_iota(jnp.int32, (CHUNK, CHUNK), 0)
    col = lax.broadcasted_iota(jnp.int32, (CHUNK, CHUNK), 1)
    tril = (row >= col).astype(BF16)
    triu = (row <= col).astype(BF16)
    arow = -jnp.exp(al_ref[...])
    hi, mid, lo = _split3(dt * arow)
    down = lambda t: jnp.dot(tril, t, preferred_element_type=F32)
    across = lambda t: lax.dot_general(t, triu, TN, preferred_element_type=F32)
    acs = (down(hi) + down(mid)) + down(lo)
    acs_t = (across(hi) + across(mid)) + across(lo)
    return arow, acs, acs_t, row >= col, triu


STEP_CHUNKS = 8


def _ssd_specs(nb, seq):
    nc = seq // CHUNK
    hw = GROUP_SSM_HEADS * HEAD_DIM
    rows, steps = STEP_CHUNKS * CHUNK, nc // STEP_CHUNKS

    def mk(rev):
        cidx = (lambda c: steps - 1 - c) if rev else (lambda c: c)
        wide = pl.BlockSpec((None, rows, hw), lambda g, b, c: (b, cidx(c), g))
        xbc = pl.BlockSpec((None, rows, XBC_GROUP), lambda g, b, c: (b, cidx(c), g))
        lanes = pl.BlockSpec((None, None, rows, LANE), lambda g, b, c: (g, b, cidx(c), 0))
        prev = pl.BlockSpec((None, STEP_CHUNKS, None, D_STATE, hw), lambda g, b, c: (b, cidx(c), g, 0, 0))
        raw = pl.BlockSpec((None, rows, LANE), lambda g, b, c: (b, cidx(c), 0))
        return wide, xbc, lanes, prev, raw

    grow = pl.BlockSpec((None, 1, LANE), lambda g, b, c: (g, 0, 0))
    nwspec = pl.BlockSpec((1, hw), lambda g, b, c: (0, g))
    tbspec = pl.BlockSpec((1, LANE), lambda g, b, c: (0, 0))
    return nc, steps, hw, mk, grow, nwspec, tbspec


def _head_expand():
    hw = GROUP_SSM_HEADS * HEAD_DIM
    r = lax.broadcasted_iota(jnp.int32, (LANE, hw), 0)
    c = lax.broadcasted_iota(jnp.int32, (LANE, hw), 1)
    return ((c // HEAD_DIM) == r).astype(BF16)


def _split3(v):
    hi = v.astype(BF16)
    rest = v - hi.astype(F32)
    mid = rest.astype(BF16)
    return hi, mid, (rest - mid.astype(F32)).astype(BF16)


def _to_channels(v, e):
    hi, mid, lo = _split3(v)
    dot = lambda t: jnp.dot(t, e, preferred_element_type=F32)
    return (dot(hi) + dot(mid)) + dot(lo)


def _to_heads(w, e):
    hi, mid, lo = _split3(w)
    dot = lambda t: lax.dot_general(t, e, (((1,), (1,)), ((), ())), preferred_element_type=F32)
    return (dot(hi) + dot(mid)) + dot(lo)


def _row8(v):
    return jnp.broadcast_to(v, (8, v.shape[1]))


def _ssd_chunk_setup(dt, al_ref, ds_ref):
    arow, acs, acs_t, causal, triu = _decays(dt, al_ref)
    e = _head_expand()
    dtx = _to_channels(dt, e)
    acsx = _to_channels(acs, e)
    lastx = acsx[CHUNK - 1:CHUNK, :]
    dskx = _to_channels(_row8(ds_ref[...]), e)[0:1, :]
    return arow, acs, acs_t, causal, triu, e, dtx, acsx, lastx, dskx


def _ssd_fwd(xbc, dt_raw, dt_bias_row, z, alog_g, dskip_g, normw):
    nb, seq, _ = xbc.shape
    nc, steps, hw, mk, grow, nwspec, tbspec = _ssd_specs(nb, seq)
    wide, xbc_spec, lanes, prev, raw = mk(False)
    tn = (((0,), (0,)), ((), ()))

    def body(xbc_ref, dt_ref, tb_ref, z_ref, al_ref, ds_ref, nw_ref, ys_ref, y_ref, sp_ref, st_ref):
        @pl.when(pl.program_id(2) == 0)
        def _():
            st_ref[...] = jnp.zeros_like(st_ref)

        for ci in range(STEP_CHUNKS):
            chunk(ci, xbc_ref, dt_ref, tb_ref, z_ref, al_ref, ds_ref, nw_ref, ys_ref, y_ref, sp_ref, st_ref)

    def chunk(ci, xbc_ref, dt_ref, tb_ref, z_ref, al_ref, ds_ref, nw_ref, ys_ref, y_ref, sp_ref, st_ref):
        rows = slice(ci * CHUNK, (ci + 1) * CHUNK)
        dt, _ = _step_sizes(dt_ref[rows, :], tb_ref)
        _, acs, acs_t, causal, _, _, dtx, acsx, lastx, dskx = _ssd_chunk_setup(dt, al_ref, ds_ref)
        bmat = xbc_ref[rows, GROUP_CH:GROUP_CH + D_STATE].astype(BF16)
        cmat = xbc_ref[rows, GROUP_CH + D_STATE:].astype(BF16)
        cb = lax.dot_general(cmat, bmat, (((1,), (1,)), ((), ())), preferred_element_type=F32)
        x = xbc_ref[rows, :GROUP_CH]
        xdt = x * dtx
        xdt16 = xdt.astype(BF16)
        first_head = lax.broadcasted_iota(jnp.int32, (CHUNK, LANE), 1) < HEAD_DIM
        pairs = []
        for hp in range(GROUP_SSM_HEADS // 2):
            xp = xdt16[:, hp * LANE:(hp + 1) * LANE]
            two = []
            for j in (2 * hp, 2 * hp + 1):
                lmat = jnp.exp(jnp.where(causal, acs[:, j:j + 1] - acs_t[j:j + 1, :], -jnp.inf))
                two.append(jnp.dot((cb * lmat).astype(BF16), xp, preferred_element_type=F32))
            pairs.append(jnp.where(first_head, two[0], two[1]))
        yd = jnp.concatenate(pairs, axis=1)
        s_prev = st_ref[...]
        s16 = s_prev.astype(BF16)
        sp_ref[ci] = s16
        yo = jnp.dot(cmat, s16, preferred_element_type=F32) * jnp.exp(acsx)
        sts = lax.dot_general(bmat, (xdt * jnp.exp(lastx - acsx)).astype(BF16), tn, preferred_element_type=F32)
        st_ref[...] = s_prev * jnp.exp(lastx) + sts
        y = yd + yo + dskx * x
        zz = z_ref[rows, :]
        u = y * (zz * _sigmoid(zz))
        rn = lax.rsqrt(jnp.mean(u * u, -1, keepdims=True) + RMS_EPS)
        ys_ref[rows, :] = (u * rn * nw_ref[...]).astype(BF16)
        y_ref[rows, :] = y

    return _pcall(
        body, name="ssd_fwd", grid=(SSM_GROUPS, nb, steps),
        in_specs=[xbc_spec, raw, tbspec, wide, grow, grow, nwspec],
        out_specs=[wide, wide, prev],
        out_shape=[jax.ShapeDtypeStruct((nb, seq, D_INNER), BF16), jax.ShapeDtypeStruct((nb, seq, D_INNER), F32),
                   jax.ShapeDtypeStruct((nb, nc, SSM_GROUPS, D_STATE, hw), BF16)],
        scratch_shapes=[pltpu.VMEM((D_STATE, hw), F32)],
        compiler_params=_params("parallel", "parallel", "arbitrary"),
    )(xbc, dt_raw, dt_bias_row, z, alog_g, dskip_g, normw)


def _ssd_bwd(xbc, dt_raw, dt_bias_row, z, y, dys, sprev, alog_g, dskip_g, normw):
    nb, seq, _ = xbc.shape
    nc, steps, hw, mk, grow, nwspec, tbspec = _ssd_specs(nb, seq)
    wide, xbc_spec, lanes, prev, raw = mk(True)
    nt = (((1,), (1,)), ((), ()))
    tn = (((0,), (0,)), ((), ()))

    def body(xbc_ref, dt_ref, tb_ref, z_ref, y_ref, dys_ref, sp_ref, al_ref, ds_ref, nw_ref,
             dxbc_ref, ddt_ref, dz_ref, small_ref, dnw_ref, g_ref):
        b, c = pl.program_id(1), pl.program_id(2)

        @pl.when((b == 0) & (c == 0))
        def _():
            small_ref[...] = jnp.zeros_like(small_ref)
            dnw_ref[...] = jnp.zeros_like(dnw_ref)

        @pl.when(c == 0)
        def _():
            g_ref[...] = jnp.zeros_like(g_ref)

        for ci in reversed(range(STEP_CHUNKS)):
            chunk(ci, xbc_ref, dt_ref, tb_ref, z_ref, y_ref, dys_ref, sp_ref, al_ref, ds_ref, nw_ref,
                  dxbc_ref, ddt_ref, dz_ref, small_ref, dnw_ref, g_ref)

    def chunk(ci, xbc_ref, dt_ref, tb_ref, z_ref, y_ref, dys_ref, sp_ref, al_ref, ds_ref, nw_ref,
              dxbc_ref, ddt_ref, dz_ref, small_ref, dnw_ref, g_ref):
        rows = slice(ci * CHUNK, (ci + 1) * CHUNK)
        yv, zz, dys_v, nw = y_ref[rows, :], z_ref[rows, :], dys_ref[rows, :], nw_ref[...]
        sz = _sigmoid(zz)
        silu = zz * sz
        u = yv * silu
        rn = lax.rsqrt(jnp.mean(u * u, -1, keepdims=True) + RMS_EPS)
        gn = dys_v * nw
        du = rn * gn - u * (rn * rn * rn) * jnp.mean(u * gn, -1, keepdims=True)
        dnw_ref[...] += jnp.sum(dys_v * u * rn, 0, keepdims=True)
        dy = du * silu
        dz_ref[rows, :] = du * yv * (sz * (1.0 + zz * (1.0 - sz)))

        dt, sg = _step_sizes(dt_ref[rows, :], tb_ref)
        arow, acs, acs_t, causal, triu, e, dtx, acsx, lastx, dskx = _ssd_chunk_setup(dt, al_ref, ds_ref)
        dfsx = jnp.exp(acsx)
        dtex = jnp.exp(lastx - acsx)
        bmat = xbc_ref[rows, GROUP_CH:GROUP_CH + D_STATE].astype(BF16)
        cmat = xbc_ref[rows, GROUP_CH + D_STATE:].astype(BF16)
        cb = lax.dot_general(cmat, bmat, nt, preferred_element_type=F32)
        x = xbc_ref[rows, :GROUP_CH]
        xdt = x * dtx
        xdt16 = xdt.astype(BF16)
        xdte = xdt * dtex
        dy16 = dy.astype(BF16)
        dyd = dy * dfsx
        dyd16 = dyd.astype(BF16)
        s16 = sp_ref[ci]
        g = g_ref[...]
        g16 = g.astype(BF16)
        cs = jnp.dot(cmat, s16, preferred_element_type=F32)
        dc_off = lax.dot_general(dyd16, s16, nt, preferred_element_type=F32)
        g_here = lax.dot_general(cmat, dyd16, tn, preferred_element_type=F32)
        bg = jnp.dot(bmat, g16, preferred_element_type=F32)
        db_st = lax.dot_general(xdte.astype(BF16), g16, nt, preferred_element_type=F32)
        ddte_w = bg * xdte
        dcd = _to_heads(_row8(jnp.sum(g * s16.astype(F32), 0, keepdims=True)), e)[0:1, :]
        lane = lax.broadcasted_iota(jnp.int32, (CHUNK, LANE), 1)
        first_head = lane < HEAD_DIM
        sub = lax.broadcasted_iota(jnp.int32, (CHUNK, LANE), 0)
        dacs = jnp.zeros((CHUNK, LANE), F32)
        colsums = jnp.zeros((CHUNK, LANE), F32)
        dcb = jnp.zeros((CHUNK, CHUNK), F32)
        pairs = []
        for hp in range(GROUP_SSM_HEADS // 2):
            xp = xdt16[:, hp * LANE:(hp + 1) * LANE]
            dyp = dy16[:, hp * LANE:(hp + 1) * LANE]
            two = []
            for idx, j in enumerate((2 * hp, 2 * hp + 1)):
                lmat = jnp.exp(jnp.where(causal, acs[:, j:j + 1] - acs_t[j:j + 1, :], -jnp.inf))
                mf = cb * lmat
                dy_h = jnp.where(first_head if idx == 0 else jnp.logical_not(first_head), dyp, jnp.zeros_like(dyp))
                dm = lax.dot_general(dy_h, xp, nt, preferred_element_type=F32)
                two.append(lax.dot_general(mf.astype(BF16), dyp, tn, preferred_element_type=F32))
                wmat = dm * mf
                dcb = dcb + dm * lmat
                dacs = jnp.where(lane == j, jnp.sum(wmat, -1, keepdims=True), dacs)
                colsums = jnp.where(sub == j, jnp.sum(wmat, 0, keepdims=True), colsums)
            pairs.append(jnp.where(first_head, two[0], two[1]))
        dxdt = bg * dtex + jnp.concatenate(pairs, axis=1)
        dacs = dacs - colsums.T + _to_heads(dyd * cs - ddte_w, e)
        cd_row = jnp.exp(acs[CHUNK - 1:CHUNK, :])
        tail = _to_heads(_row8(jnp.sum(ddte_w, 0, keepdims=True)), e)[0:1, :] + dcd * cd_row
        dacs = dacs + jnp.where(sub == CHUNK - 1, tail, 0.0)
        d_hi, d_mid, d_lo = _split3(dacs)
        up = lambda t: jnp.dot(triu, t, preferred_element_type=F32)
        da = (up(d_hi) + up(d_mid)) + up(d_lo)
        ddt_raw = (da * arow + _to_heads(dxdt * x, e)) * sg
        ddt_ref[rows, :] = ddt_raw
        small_ref[0:1, :] += jnp.sum(da * dt, 0, keepdims=True) * arow
        small_ref[1:2, :] += _to_heads(_row8(jnp.sum(dy * x, 0, keepdims=True)), e)[0:1, :]
        small_ref[2:3, :] += jnp.sum(ddt_raw, 0, keepdims=True)
        dcb16 = dcb.astype(BF16)
        dxbc_ref[rows, GROUP_CH + D_STATE:] = dc_off + jnp.dot(dcb16, bmat, preferred_element_type=F32)
        dxbc_ref[rows, GROUP_CH:GROUP_CH + D_STATE] = db_st + lax.dot_general(dcb16, cmat, tn,
                                                                               preferred_element_type=F32)
        dxbc_ref[rows, :GROUP_CH] = dxdt * dtx + dskx * dy
        g_ref[...] = g * jnp.exp(lastx) + g_here

    return _pcall(
        body, name="ssd_bwd", grid=(SSM_GROUPS, nb, steps),
        in_specs=[xbc_spec, raw, tbspec, wide, wide, wide, prev, grow, grow, nwspec],
        out_specs=[xbc_spec, lanes, wide,
                   pl.BlockSpec((None, 8, LANE), lambda g, b, c: (g, 0, 0)), nwspec],
        out_shape=[jax.ShapeDtypeStruct((nb, seq, CONV_DIM), F32),
                   jax.ShapeDtypeStruct((SSM_GROUPS, nb, seq, LANE), F32),
                   jax.ShapeDtypeStruct((nb, seq, D_INNER), F32),
                   jax.ShapeDtypeStruct((SSM_GROUPS, 8, LANE), F32),
                   jax.ShapeDtypeStruct((1, D_INNER), F32)],
        scratch_shapes=[pltpu.VMEM((D_STATE, hw), F32)],
        compiler_params=_params("parallel", "arbitrary", "arbitrary"),
    )(xbc, dt_raw, dt_bias_row, z, y, dys, sprev, alog_g, dskip_g, normw)


EW_TM = 256


def _merge_fwd(oa16, y_ssm16, w_bra, w_brb, gm, bgate):
    nb, seq, _ = oa16.shape

    def body(oa_ref, ys_ref, wa_ref, wb_ref, ga_ref, gb_ref, bg_ref, a_ref, b_ref, o_ref):
        y_a = jnp.dot(oa_ref[...], wa_ref[...], preferred_element_type=F32)
        y_b = jnp.dot(ys_ref[...], wb_ref[...], preferred_element_type=F32)
        a_ref[...] = y_a
        b_ref[...] = y_b
        sa = _sigmoid(ga_ref[...] + bg_ref[0:1, :])
        sb = _sigmoid(gb_ref[...] + bg_ref[1:2, :])
        o_ref[...] = (sa * y_a + sb * y_b).astype(BF16)

    spec = pl.BlockSpec((None, EW_TM, D_MODEL), lambda b, i: (b, i, 0))
    spec1 = pl.BlockSpec((None, EW_TM, D_MODEL), lambda b, i: (b, i, 1))
    return _pcall(
        body, name="merge_fwd", grid=(nb, seq // EW_TM),
        in_specs=[_tok_spec(EW_TM, ATT_OUT), _tok_spec(EW_TM, D_INNER), _whole(w_bra, True), _whole(w_brb, True),
                  spec, spec1, pl.BlockSpec((8, D_MODEL), lambda b, i: (0, 0))],
        out_specs=[spec] * 3,
        out_shape=[jax.ShapeDtypeStruct((nb, seq, D_MODEL), F32)] * 2 + [jax.ShapeDtypeStruct((nb, seq, D_MODEL), BF16)],
        compiler_params=_params("parallel", "parallel"),
    )(oa16, y_ssm16, w_bra, w_brb, gm, gm, bgate)


def _merge_bwd(dpre16, w_out16, y_a, y_b, gm, bgate):
    nb, seq, _ = y_a.shape

    def body(dp_ref, w_ref, a_ref, b_ref, ga_ref, gb_ref, bg_ref, dya_ref, dyb_ref, dg_ref, s_ref):
        @pl.when((pl.program_id(0) == 0) & (pl.program_id(1) == 0))
        def _():
            s_ref[...] = jnp.zeros_like(s_ref)

        dm = lax.dot_general(dp_ref[...], w_ref[...], NT, preferred_element_type=F32)
        sa = _sigmoid(ga_ref[...] + bg_ref[0:1, :])
        sb = _sigmoid(gb_ref[...] + bg_ref[1:2, :])
        dya_ref[...] = (dm * sa).astype(BF16)
        dyb_ref[...] = (dm * sb).astype(BF16)
        dga = dm * a_ref[...] * (sa * (1.0 - sa))
        dgb = dm * b_ref[...] * (sb * (1.0 - sb))
        dg_ref[:, :D_MODEL] = dga.astype(BF16)
        dg_ref[:, D_MODEL:] = dgb.astype(BF16)
        s_ref[0:1, :] += jnp.sum(dga, 0, keepdims=True)
        s_ref[1:2, :] += jnp.sum(dgb, 0, keepdims=True)

    spec = pl.BlockSpec((None, EW_TM, D_MODEL), lambda b, i: (b, i, 0))
    spec1 = pl.BlockSpec((None, EW_TM, D_MODEL), lambda b, i: (b, i, 1))
    small = pl.BlockSpec((8, D_MODEL), lambda b, i: (0, 0))
    return _pcall(
        body, name="merge_bwd", grid=(nb, seq // EW_TM),
        in_specs=[spec, _whole(w_out16, True), spec, spec, spec, spec1, small],
        out_specs=[spec, spec, pl.BlockSpec((None, EW_TM, 2 * D_MODEL), lambda b, i: (b, i, 0)), small],
        out_shape=[jax.ShapeDtypeStruct((nb, seq, D_MODEL), BF16), jax.ShapeDtypeStruct((nb, seq, D_MODEL), BF16),
                   jax.ShapeDtypeStruct((nb, seq, 2 * D_MODEL), BF16), jax.ShapeDtypeStruct((8, D_MODEL), F32)],
        compiler_params=_params("arbitrary", "arbitrary"),
    )(dpre16, w_out16, y_a, y_b, gm, gm, bgate)


def _ln_loss(x, merged16, w_out16, gp, p16, w_ple16, target, bgate, ln_g, ln_b):
    nb, seq, _ = x.shape

    def body(x_ref, m_ref, wo_ref, gp_ref, p_ref, wp_ref, t_ref, bg_ref, g_ref, b_ref,
             dx_ref, dp_ref, dpw_ref, dgp_ref, s_ref):
        @pl.when((pl.program_id(0) == 0) & (pl.program_id(1) == 0))
        def _():
            s_ref[...] = jnp.zeros_like(s_ref)

        sp = _sigmoid(gp_ref[...] + bg_ref[2:3, :])
        pw = jnp.dot(p_ref[...], wp_ref[...], preferred_element_type=F32)
        mix = jnp.dot(m_ref[...], wo_ref[...], preferred_element_type=F32)
        pre = ALPHA * x_ref[...] + mix + sp * pw
        mu = jnp.mean(pre, -1, keepdims=True)
        cen = pre - mu
        rstd = lax.rsqrt(jnp.mean(cen * cen, -1, keepdims=True) + LN_EPS)
        xhat = cen * rstd
        err = xhat * g_ref[...] + b_ref[...] - t_ref[...]
        dy = err * (1.0 / D_MODEL)
        dxh = dy * g_ref[...]
        dpre = rstd * (dxh - jnp.mean(dxh, -1, keepdims=True) - xhat * jnp.mean(dxh * xhat, -1, keepdims=True))
        dx_ref[...] = ALPHA * dpre
        dp_ref[...] = dpre.astype(BF16)
        dpw_ref[...] = (dpre * sp).astype(BF16)
        dgp = dpre * pw * (sp * (1.0 - sp))
        dgp_ref[...] = dgp.astype(BF16)
        s_ref[0:1, :] += jnp.sum(dy * xhat, 0, keepdims=True)
        s_ref[1:2, :] += jnp.sum(dy, 0, keepdims=True)
        s_ref[2:3, :] += jnp.sum(dgp, 0, keepdims=True)
        s_ref[3:4, :] += jnp.sum(err * err, 0, keepdims=True)

    spec = pl.BlockSpec((None, EW_TM, D_MODEL), lambda b, i: (b, i, 0))
    small = pl.BlockSpec((8, D_MODEL), lambda b, i: (0, 0))
    row = pl.BlockSpec((1, D_MODEL), lambda b, i: (0, 0))
    return _pcall(
        body, name="ln_loss", grid=(nb, seq // EW_TM),
        in_specs=[spec, spec, _whole(w_out16, True), spec, pl.BlockSpec((None, EW_TM, PLE_DIM), lambda b, i: (b, i, 0)),
                  _whole(w_ple16, True), spec, small, row, row],
        out_specs=[spec] * 4 + [small],
        out_shape=[jax.ShapeDtypeStruct((nb, seq, D_MODEL), F32)] + [jax.ShapeDtypeStruct((nb, seq, D_MODEL), BF16)] * 3
        + [jax.ShapeDtypeStruct((8, D_MODEL), F32)],
        compiler_params=_params("arbitrary", "arbitrary"),
    )(x, merged16, w_out16, gp, p16, w_ple16, target, bgate, ln_g, ln_b)


def _adamw_update(w_ref, g_ref, m_ref, v_ref, d_ref, nm_ref, nv_ref):
    c1 = 1.0 - ADAM_B1 ** ADAM_STEP
    c2 = 1.0 - ADAM_B2 ** ADAM_STEP
    gv = g_ref[...]
    nm = ADAM_B1 * m_ref[...] + (1.0 - ADAM_B1) * gv
    nv = ADAM_B2 * v_ref[...] + (1.0 - ADAM_B2) * (gv * gv)
    d_ref[...] = -ADAM_LR * ((nm / c1) / (jnp.sqrt(nv / c2) + ADAM_EPS) + ADAM_WD * w_ref[...])
    nm_ref[...] = nm
    nv_ref[...] = nv


def _adamw(w, g, m, v, name):
    rows, cols = w.shape
    tr = _row_tile(rows, cols, 8, 5 << 19)

    def body(*refs):
        _adamw_update(*refs)

    spec = pl.BlockSpec((tr, cols), lambda i: (i, 0))
    return _pcall(
        body, name=name, grid=(rows // tr,), in_specs=[spec] * 4, out_specs=[spec] * 3,
        out_shape=[jax.ShapeDtypeStruct(w.shape, F32)] * 3, compiler_params=_params("parallel"),
    )(w, g, m, v)


def _adamw_small(ws, gs, ms, vs, name):
    n = len(ws)

    def body(*refs):
        for i in range(n):
            _adamw_update(*[refs[k * n + i] for k in range(7)])

    outs = _pcall(body, name=name, out_shape=[jax.ShapeDtypeStruct(w.shape, F32) for w in ws] * 3,
                  compiler_params=_params())(*ws, *gs, *ms, *vs)
    return outs[:n], outs[n:2 * n], outs[2 * n:]


def _sum_rows(parts, out_dtype, name):
    rows, cols = parts[0].shape
    tr = rows
    for cand in range(16, rows, 16):
        if rows % cand == 0 and cand * cols * 4 <= (1 << 20):
            tr = cand
    n = len(parts)

    def body(*refs):
        acc = refs[0][...].astype(F32)
        for r in refs[1:n]:
            acc = acc + r[...].astype(F32)
        refs[n][...] = acc.astype(out_dtype)

    spec = pl.BlockSpec((tr, cols), lambda i: (i, 0))
    return _pcall(
        body, name=name, grid=(rows // tr,), in_specs=[spec] * n, out_specs=spec,
        out_shape=jax.ShapeDtypeStruct((rows, cols), out_dtype), compiler_params=_params("parallel"),
    )(*parts)


def _place():
    return lax.axis_index("x"), lax.axis_index("y"), lax.axis_index("c")


def _other_chips(x, y):
    return [(1 - x, y), (x, 1 - y), (1 - x, 1 - y)]


def _remote(src, dst, send_sem, recv_sem, to):
    return pltpu.make_async_remote_copy(src_ref=src, dst_ref=dst, send_sem=send_sem, recv_sem=recv_sem,
                                        device_id=to, device_id_type=MESH)


ANY = pl.BlockSpec(memory_space=pl.ANY)
D2D_CHUNK_BYTES = 512 * 1024
ICI_CHUNK_BYTES = 2 * 1024 * 1024


def _row_chunks(rows, row_bytes, chunk_bytes=D2D_CHUNK_BYTES):
    per = max(16, chunk_bytes // row_bytes // 16 * 16)
    return [(s, min(per, rows - s)) for s in range(0, rows, per)]


def _row_tile(rows, cols, align, limit=1 << 21):
    best = None
    for cand in range(align, rows + 1, align):
        if rows % cand == 0 and cand * cols * 4 <= limit:
            best = cand
    return best or rows


def _allgather_pieces(pieces):
    n = len(pieces)
    halves = [_row_chunks(p.shape[0] // 2, p.shape[1] * p.dtype.itemsize, ICI_CHUNK_BYTES) for p in pieces]
    entries = [(a, q, s, m, j) for a in range(n) for q, (s, m) in enumerate(halves[a]) for j in range(3)]
    slot = {(a, q, j): k for k, (a, q, _, _, j) in enumerate(entries)}
    n_ici = len(entries)

    def body(*refs):
        ins, outs = refs[:n], refs[n:2 * n]
        send_sems, recv_sems = refs[2 * n:]
        x, y, c = _place()
        me = 2 * x + y
        sibling = (x, y, 1 - c)
        chips = _other_chips(x, y)

        def landed(a, s, m, j, core):
            half = ins[a].shape[0] // 2
            return outs[a].at[2 * chips[j][0] + chips[j][1], pl.ds(core * half + s, m)]

        sent = []
        for k, (a, q, s, m, j) in enumerate(entries):
            if j < 2:
                half = ins[a].shape[0] // 2
                cp = _remote(ins[a].at[pl.ds(c * half + s, m)], outs[a].at[me, pl.ds(c * half + s, m)],
                             send_sems.at[k], recv_sems.at[k], (*chips[j], c))
                cp.start()
                sent.append(cp)

        def pass_to_sibling(k, blk):
            fw = _remote(blk, blk, send_sems.at[n_ici + k], recv_sems.at[n_ici + k], sibling)
            fw.start()
            sent.append(fw)

        for k, (a, q, s, m, j) in enumerate(entries):
            if j < 2:
                blk = landed(a, s, m, j, c)
                _remote(blk, blk, send_sems.at[k], recv_sems.at[k], (*chips[j], c)).wait_recv()
                first = q < (len(halves[a]) + 1) // 2
                if (j == 0) == first:
                    on = slot[(a, q, 2)]
                    rl = _remote(blk, blk, send_sems.at[on], recv_sems.at[on], (*chips[1 - j], c))
                    rl.start()
                    sent.append(rl)
                pass_to_sibling(k, blk)
        for k, (a, q, s, m, j) in enumerate(entries):
            if j == 2:
                blk = landed(a, s, m, j, c)
                _remote(blk, blk, send_sems.at[k], recv_sems.at[k], (*chips[j], c)).wait_recv()
                pass_to_sibling(k, blk)
        for k, (a, q, s, m, j) in enumerate(entries):
            blk = landed(a, s, m, j, 1 - c)
            _remote(blk, blk, send_sems.at[n_ici + k], recv_sems.at[n_ici + k], sibling).wait_recv()
        for cp in sent:
            cp.wait_send()

    gathered = _pcall(
        body, name="allgather_weights", in_specs=[ANY] * n, out_specs=[ANY] * n,
        out_shape=[jax.ShapeDtypeStruct((4,) + p.shape, p.dtype) for p in pieces],
        scratch_shapes=[pltpu.SemaphoreType.DMA((2 * n_ici,)), pltpu.SemaphoreType.DMA((2 * n_ici,))],
        compiler_params=pltpu.CompilerParams(has_side_effects=True),
    )(*pieces)
    x, y, _ = _place()
    return [lax.dynamic_update_slice(g, p[None], (2 * x + y, 0, 0)) for g, p in zip(gathered, pieces)]


def _sibling_exchange(grads):
    n = len(grads)
    chunks = [_row_chunks(g.shape[1] // 2, g.shape[2] * g.dtype.itemsize) for g in grads]
    n_sem = 4 * sum(len(ch) for ch in chunks)

    def body(*refs):
        ins, gots = refs[:n], refs[n:2 * n]
        send_sems, recv_sems = refs[2 * n:]
        x, y, c = _place()
        sibling = (x, y, 1 - c)
        work = []
        for a in range(n):
            half = ins[a].shape[1] // 2
            for piece in range(4):
                for s, m in chunks[a]:
                    k = len(work)
                    cp = _remote(ins[a].at[piece, pl.ds((1 - c) * half + s, m)], gots[a].at[piece, pl.ds(s, m)],
                                 send_sems.at[k], recv_sems.at[k], sibling)
                    cp.start()
                    work.append(cp)
        for cp in work:
            cp.wait()

    return _pcall(
        body, name="grad_sibling_exchange", in_specs=[ANY] * n, out_specs=[ANY] * n,
        out_shape=[jax.ShapeDtypeStruct((4, g.shape[1] // 2, g.shape[2]), g.dtype) for g in grads],
        scratch_shapes=[pltpu.SemaphoreType.DMA((n_sem,)), pltpu.SemaphoreType.DMA((n_sem,))],
        compiler_params=pltpu.CompilerParams(has_side_effects=True),
    )(*grads)


def _sibling_gather(fulls):
    n = len(fulls)
    chunks = [_row_chunks(f.shape[0] // 2, f.shape[1] * f.dtype.itemsize) for f in fulls]
    n_sem = sum(len(ch) for ch in chunks)

    def body(*refs):
        outs = refs[n:2 * n]
        send_sems, recv_sems = refs[2 * n:]
        x, y, c = _place()
        sibling = (x, y, 1 - c)
        work = []
        for a in range(n):
            h = outs[a].shape[0] // 2
            for s, m in chunks[a]:
                k = len(work)
                mine = outs[a].at[pl.ds(c * h + s, m)]
                cp = _remote(mine, mine, send_sems.at[k], recv_sems.at[k], sibling)
                cp.start()
                work.append((a, s, m, cp))
        for k, (a, s, m, cp) in enumerate(work):
            h = outs[a].shape[0] // 2
            cp.wait_send()
            theirs = outs[a].at[pl.ds((1 - c) * h + s, m)]
            _remote(theirs, theirs, send_sems.at[k], recv_sems.at[k], sibling).wait_recv()

    return _pcall(
        body, name="grad_sibling_gather", in_specs=[ANY] * n, out_specs=[ANY] * n,
        out_shape=[jax.ShapeDtypeStruct(f.shape, f.dtype) for f in fulls],
        input_output_aliases={a: a for a in range(n)},
        scratch_shapes=[pltpu.SemaphoreType.DMA((n_sem,)), pltpu.SemaphoreType.DMA((n_sem,))],
        compiler_params=pltpu.CompilerParams(has_side_effects=True),
    )(*fulls)


def _pair_sum(grad, got, place, name):
    _, rows, cols = grad.shape
    half = rows // 2
    tr = _row_tile(half, cols, 16)

    def body(p_ref, a_ref, b_ref, o_ref):
        o_ref[...] = (a_ref[...].astype(F32) + b_ref[...].astype(F32)).astype(BF16)

    return _pcall(
        body, name=name,
        grid_spec=pltpu.PrefetchScalarGridSpec(
            num_scalar_prefetch=1, grid=(4, half // tr),
            in_specs=[pl.BlockSpec((None, tr, cols), lambda k, i, p: (k, p[1] * (half // tr) + i, 0)),
                      pl.BlockSpec((None, tr, cols), lambda k, i, p: (k, i, 0))],
            out_specs=pl.BlockSpec((None, tr, cols), lambda k, i, p: (k, i, 0))),
        out_shape=jax.ShapeDtypeStruct((4, half, cols), BF16),
        compiler_params=_params("parallel", "parallel"),
    )(place, grad, got)


def _chip_sum(sums, got, place, name):
    _, h, cols = sums.shape
    tr = _row_tile(h, cols, 16)

    def body(p_ref, own_ref, g0, g1, g2, o_ref):
        o_ref[...] = ((own_ref[...].astype(F32) + g0[...].astype(F32)) + g1[...].astype(F32)) + g2[...].astype(F32)

    gspec = lambda j: pl.BlockSpec((None, tr, cols), lambda i, p: (j, i, 0))
    return _pcall(
        body, name=name,
        grid_spec=pltpu.PrefetchScalarGridSpec(
            num_scalar_prefetch=1, grid=(h // tr,),
            in_specs=[pl.BlockSpec((None, tr, cols), lambda i, p: (p[0], i, 0)), gspec(0), gspec(1), gspec(2)],
            out_specs=pl.BlockSpec((tr, cols), lambda i, p: (p[1] * (h // tr) + i, 0))),
        out_shape=jax.ShapeDtypeStruct((2 * h, cols), F32),
        compiler_params=_params("parallel"),
    )(place, sums, got, got, got)


def _allgather8(buf, name):
    rows = buf.shape[0]

    def body(in_ref, out_ref, send_sems, recv_sems):
        x, y, c = _place()
        me = 4 * x + 2 * y + c
        out_ref[me] = in_ref[...]
        work = []
        for rel in range(1, 8):
            fx, fy, fc = (rel >> 2) & 1, (rel >> 1) & 1, rel & 1
            to = (x ^ fx, y ^ fy, c ^ fc)
            cp = _remote(in_ref, out_ref.at[me], send_sems.at[rel - 1], recv_sems.at[rel - 1], to)
            cp.start()
            work.append((cp, 4 * to[0] + 2 * to[1] + to[2]))
        for rel, (cp, frm) in enumerate(work):
            cp.wait_send()
            blk = out_ref.at[frm]
            _remote(blk, blk, send_sems.at[rel], recv_sems.at[rel], (x, y, c)).wait_recv()

    return _pcall(
        body, name=name, in_specs=[pl.BlockSpec(memory_space=pltpu.VMEM)],
        out_specs=pl.BlockSpec(memory_space=pltpu.VMEM),
        out_shape=jax.ShapeDtypeStruct((8, rows, LANE), F32),
        scratch_shapes=[pltpu.SemaphoreType.DMA((7,)), pltpu.SemaphoreType.DMA((7,))],
        compiler_params=pltpu.CompilerParams(has_side_effects=True),
    )(buf)


def _pack_rows(arrs):
    flats = [a.reshape(-1).astype(F32) for a in arrs]
    starts = np.cumsum([0] + [-(-f.shape[0] // LANE) * LANE for f in flats])
    total = -(-int(starts[-1]) // (8 * LANE)) * 8 * LANE
    flat = sum(jnp.pad(f, (int(s), total - int(s) - f.shape[0])) for f, s in zip(flats, starts))
    return flat.reshape(total // LANE, LANE)


def _unpack_rows(buf, shapes):
    flat = buf.reshape(-1)
    outs, off = [], 0
    for s in shapes:
        n = int(np.prod(s))
        outs.append(flat[off:off + n].reshape(s))
        off += -(-n // LANE) * LANE
    return outs


def _local_grads(x, p, target, wseg, w_br16, w_out16, w_ple16, b_gate, conv_w, conv_b, dt_bias, a_log, d_skip,
                 ssm_norm_w, ln_g, ln_b, rel_bias, finish_dx):
    nb, seq, _ = x.shape
    bmaps = jnp.asarray(_bucket_maps())
    bias = _bias_tables(rel_bias, bmaps)
    bgate8 = jnp.pad(b_gate, ((0, 5), (0, 0)))
    dils = [d for _, d in PATTERNS]

    x16p = _token_orders(x, dils[1:])
    x16 = x16p[0]
    p16 = p.astype(BF16)
    qkv = [_proj(x16p[g], [wseg["qkv%d" % g]], BF16, "proj_qkv%d" % g, True, 2 * MM_TM)[0].reshape(
        nb, dils[g], seq // dils[g], -1) for g in range(3)]
    nat = {}
    for gi, (group, tm) in enumerate(NAT_GROUPS):
        outs = _proj(x16, [wseg[s] for s in group], F32, "proj_nat%d" % gi, True, tm)
        nat.update(zip(group, outs))
    att = [_attn_fwd(qkv[g], bias, g, dils[g], "attn_fwd%d" % g) for g in range(3)]
    oa, o_att, lse = _combine_fwd(att[0][0], att[0][1], att[1:], nat["gatt"])

    conv_wg, conv_bg = _xbc_group_order(conv_w), _xbc_group_order(conv_b)
    act = _conv_fwd(nat["xbc"], conv_wg, conv_bg, "conv_fwd")
    dt_bias_row = jnp.pad(dt_bias, ((0, 0), (0, LANE - SSM_HEADS)))
    alog_g, dskip_g = _group_lanes(a_log), _group_lanes(d_skip)
    y_ssm, y_all, sprev = _ssd_fwd(act, nat["dt"], dt_bias_row, nat["z"], alog_g, dskip_g, ssm_norm_w)

    w_bra, w_brb = w_br16[:ATT_OUT], w_br16[ATT_OUT:]
    y_a, y_b, merged = _merge_fwd(oa, y_ssm, w_bra, w_brb, nat["gm"], bgate8)

    dx, dpre16, dpw16, dgp16, ln_sums = _ln_loss(x, merged, w_out16, nat["gp"], p16, w_ple16, target, bgate8,
                                                 ln_g, ln_b)
    loss_sum = (0.5 / D_MODEL) * jnp.sum(ln_sums[3])
    dya16, dyb16, dgm16, mg_sums = _merge_bwd(dpre16, w_out16, y_a, y_b, nat["gm"], bgate8)
    dys = _dx([dyb16], [w_brb], [], "dx_yssm")
    g_w_out, = _dw(merged, [dpre16], BF16, "dw_out")
    g_w_br = jnp.concatenate([_dw(oa, [dya16], BF16, "dw_bra")[0], _dw(y_ssm, [dyb16], BF16, "dw_brb")[0]], axis=0)
    g_w_ple, = _dw(p16, [dpw16], BF16, "dw_ple")

    do_att, dgatt16, own_order = _combine_bwd(dya16, w_bra, nat["gatt"], o_att, lse, dils[1:])
    dseg = {"gatt": dgatt16, "gm": dgm16, "gp": dgp16}
    dbias = []
    for g in range(3):
        cotangent = (do_att, o_att, lse) if g == 0 else (own_order[2 * g - 2], own_order[2 * g - 1])
        dqkv, db = _attn_bwd(qkv[g], bias, g, cotangent, dils[g],
                             "attn_bwd%d" % g)
        dseg["qkv%d" % g] = dqkv.reshape(nb, seq, -1)
        dbias.append(db)
    g_rel = _bias_grad(jnp.concatenate(dbias, axis=0), bmaps)[:, 0, :NUM_BUCKETS].T

    dact, ddtg, dz, ssd_small, g_normw = _ssd_bwd(
        act, nat["dt"], dt_bias_row, nat["z"], y_all, dys, sprev, alog_g, dskip_g, ssm_norm_w)
    dseg["z"] = dz
    dseg["dt"] = jnp.pad(_ungroup_lanes(ddtg), ((0, 0), (0, 0), (0, LANE - SSM_HEADS)))
    dpre, conv_sums = _conv_bwd_pre(dact, nat["xbc"], conv_wg, conv_bg, "conv_bwd")
    dseg["xbc"] = _conv_bwd_x(dpre, conv_wg, "conv_bwd_x")
    csum = _xbc_reference_order(conv_sums)

    dx_own = [_dx([dseg["qkv2"]], [wseg["qkv2"]], [], "dx_qkv2", True).reshape(nb, dils[2], seq // dils[2], D_MODEL)]
    dh_own = [(dseg["qkv1"].reshape(nb, dils[1], seq // dils[1], -1), wseg["qkv1"])]
    dwseg = {"qkv%d" % g: _dw(x16p[g], [dseg["qkv%d" % g]], BF16, "dw_qkv%d" % g, True)[0] for g in range(3)}
    for gi, group in enumerate(DW_GROUPS):
        dwseg.update(zip(group, _dw(x16, [dseg[s] for s in group], BF16, "dw_nat%d" % gi, True)))
    names = ["qkv0"] + [s for group, _ in NAT_GROUPS for s in group]
    dx = finish_dx([dseg[s] for s in names], [wseg[s] for s in names], [dx], dx_own, dh_own, dwseg, g_w_br, g_w_out,
                   g_w_ple)

    small = dict(
        b_gate=jnp.stack([mg_sums[0], mg_sums[1], ln_sums[2]]),
        conv_w=csum[0:4], conv_b=csum[4:5],
        dt_bias=_ungroup_lanes(ssd_small[:, 2:3, :]), a_log=_ungroup_lanes(ssd_small[:, 0:1, :]),
        d_skip=_ungroup_lanes(ssd_small[:, 1:2, :]), ssm_norm_w=g_normw,
        ln_g=ln_sums[0:1], ln_b=ln_sums[1:2], rel_bias=g_rel)
    return loss_sum, dx, small


DX_TM = 256
SMALL_ORDER = ("b_gate", "conv_w", "conv_b", "dt_bias", "a_log", "d_skip", "ssm_norm_w", "ln_g", "ln_b", "rel_bias")
SMALL_FULL_SHAPES = dict(b_gate=(3, 1024), conv_w=(4, 3072), conv_b=(1, 3072), dt_bias=(1, 32), a_log=(1, 32),
                         d_skip=(1, 32), ssm_norm_w=(1, 2048), ln_g=(1, 1024), ln_b=(1, 1024), rel_bias=(32, 36))


def kernel(x, p, w_in, b_gate, conv_w, conv_b, dt_bias, a_log, d_skip, ssm_norm_w, w_branch, w_out, w_ple, ln_g, ln_b, rel_bias, loss_target, m_w_in, m_b_gate, m_conv_w, m_conv_b, m_dt_bias, m_a_log, m_d_skip, m_ssm_norm_w, m_w_branch, m_w_out, m_w_ple, m_ln_g, m_ln_b, m_rel_bias, v_w_in, v_b_gate, v_conv_w, v_conv_b, v_dt_bias, v_a_log, v_d_skip, v_ssm_norm_w, v_w_branch, v_w_out, v_w_ple, v_ln_g, v_ln_b, v_rel_bias):
    cx, cy, cc = _place()
    chip = 2 * cx + cy
    dev = 4 * cx + 2 * cy + cc

    w_in_t = jnp.transpose(w_in[0])
    win16 = _shard_to_window(w_in_t, chip)
    g_win, g_br, g_out, g_ple = _allgather_pieces(
        [win16, w_branch[0].astype(BF16), w_out[0].astype(BF16), w_ple[0].astype(BF16)])
    wseg = _assemble(g_win)
    w_br16 = g_br.reshape(4 * 704, D_MODEL)
    w_out16 = g_out.reshape(D_MODEL, D_MODEL)
    w_ple16 = jnp.transpose(g_ple, (1, 0, 2)).reshape(PLE_DIM, D_MODEL)
    shards = _allgather8(_pack_rows([b_gate[0], conv_w[0]]), "allgather_small_params")
    per_chip = [_unpack_rows(shards[2 * k], [(3, 256), (4, 768)]) for k in range(4)]
    b_gate_full = _join_last([pc[0] for pc in per_chip])
    conv_w_full = _join_last([pc[1] for pc in per_chip])

    place = jnp.stack([chip, cc]).astype(jnp.int32)
    reduced = []

    def finish_dx(dhs, ws, accs, own_order_accs, own_order_dhs, dwseg, d_br, d_out, d_ple):
        grads = [_pack(dwseg), d_br.reshape(4, 704, D_MODEL), d_out.reshape(4, 256, D_MODEL),
                 jnp.transpose(d_ple.reshape(PLE_DIM, 4, 256), (1, 0, 2))]
        got = _sibling_exchange(grads)
        chip_sums = [_pair_sum(g, t, place, "grad_pair_sum_%d" % i) for i, (g, t) in enumerate(zip(grads, got))]
        dx, others = _dx(dhs, ws, accs, "dx_w_in_and_grad_chip_scatter", True, DX_TM, chip_sums, own_order_accs,
                         own_order_dhs)
        fulls = [_chip_sum(s, t, place, "grad_chip_sum_%d" % i) for i, (s, t) in enumerate(zip(chip_sums, others))]
        reduced.extend(_sibling_gather(fulls))
        return dx

    loss_sum, grad_x, small = _local_grads(
        x, p[0], loss_target, wseg, w_br16, w_out16, w_ple16, b_gate_full, conv_w_full, conv_b, dt_bias, a_log,
        d_skip, ssm_norm_w, ln_g, ln_b, rel_bias, finish_dx)
    big = reduced
    g_w_in = lax.optimization_barrier(_window_to_shard(big[0], chip))
    g_w_branch, g_w_out, g_w_ple = big[1], big[2], big[3]
    parts = _allgather8(_pack_rows([small[n] for n in SMALL_ORDER] + [loss_sum.reshape(1, 1)]),
                        "allgather_small_grads")
    small_sum = _sum_rows([parts[i] for i in range(8)], F32, "small_grad_sum")
    *reduced_small, loss = _unpack_rows(small_sum, [SMALL_FULL_SHAPES[n] for n in SMALL_ORDER] + [(1, 1)])
    loss = loss.reshape(())
    sg = dict(zip(SMALL_ORDER, reduced_small))
    sg["b_gate"] = lax.dynamic_slice_in_dim(sg["b_gate"], chip * 256, 256, axis=1)
    sg["conv_w"] = lax.dynamic_slice_in_dim(sg["conv_w"], chip * 768, 768, axis=1)
    del dev

    upd = {}
    upd["w_in"] = [jnp.transpose(t) for t in _adamw(w_in_t, g_w_in, jnp.transpose(m_w_in[0]),
                                                      jnp.transpose(v_w_in[0]), "adamw_w_in")]
    upd["w_branch"] = _adamw(w_branch[0], g_w_branch, m_w_branch[0], v_w_branch[0], "adamw_w_branch")
    upd["w_out"] = _adamw(w_out[0], g_w_out, m_w_out[0], v_w_out[0], "adamw_w_out")
    upd["w_ple"] = _adamw(w_ple[0], g_w_ple, m_w_ple[0], v_w_ple[0], "adamw_w_ple")
    small_w = dict(b_gate=b_gate, conv_w=conv_w, conv_b=conv_b, dt_bias=dt_bias, a_log=a_log, d_skip=d_skip,
                   ssm_norm_w=ssm_norm_w, ln_g=ln_g, ln_b=ln_b, rel_bias=rel_bias)
    small_m = dict(b_gate=m_b_gate, conv_w=m_conv_w, conv_b=m_conv_b, dt_bias=m_dt_bias, a_log=m_a_log,
                   d_skip=m_d_skip, ssm_norm_w=m_ssm_norm_w, ln_g=m_ln_g, ln_b=m_ln_b, rel_bias=m_rel_bias)
    small_v = dict(b_gate=v_b_gate, conv_w=v_conv_w, conv_b=v_conv_b, dt_bias=v_dt_bias, a_log=v_a_log,
                   d_skip=v_d_skip, ssm_norm_w=v_ssm_norm_w, ln_g=v_ln_g, ln_b=v_ln_b, rel_bias=v_rel_bias)
    for n in SMALL_ORDER:
        sg[n] = sg[n].reshape(small_w[n].shape)
    s_delta, s_m, s_v = _adamw_small(*[[t[n] for n in SMALL_ORDER] for t in (small_w, sg, small_m, small_v)],
                                     "adamw_small")
    for i, n in enumerate(SMALL_ORDER):
        upd[n] = (s_delta[i], s_m[i], s_v[i])

    order = ("w_in", "b_gate", "conv_w", "conv_b", "dt_bias", "a_log", "d_skip", "ssm_norm_w", "w_branch", "w_out",
             "w_ple", "ln_g", "ln_b", "rel_bias")
    grads = dict(sg, w_in=jnp.transpose(g_w_in)[None],w_branch=g_w_branch[None], w_out=g_w_out[None], w_ple=g_w_ple[None])
    lead = lambda n, t: t[None] if n in ("w_in", "w_branch", "w_out", "w_ple") else t
    return (loss, grad_x, *[grads[n] for n in order], *[lead(n, upd[n][0]) for n in order],
            *[lead(n, upd[n][1]) for n in order], *[lead(n, upd[n][2]) for n in order])
```

```python
import math

import numpy as np
import jax
import jax.numpy as jnp
from jax import lax
from jax.experimental import pallas as pl
from jax.experimental.pallas import tpu as pltpu

F32, BF16 = jnp.float32, jnp.bfloat16

D_MODEL = 1024
HEAD_DIM = 64
GROUP_HEADS = 12
ATT_OUT = GROUP_HEADS * HEAD_DIM
PATTERNS = ((128, 1), (512, 4), (2048, 16))
BAND = 128
NUM_BUCKETS = 32
MAX_DISTANCE = 2048
D_INNER = 2048
SSM_HEADS = 32
SSM_GROUPS = 4
GROUP_SSM_HEADS = SSM_HEADS // SSM_GROUPS
D_STATE = 128
CHUNK = 128
PLE_DIM = 256
ALPHA = 2.0 ** 0.25
LN_EPS = 1e-5
RMS_EPS = 1e-5
ADAM_LR, ADAM_B1, ADAM_B2, ADAM_EPS, ADAM_WD, ADAM_STEP = 0.001, 0.9, 0.999, 1e-08, 0.01, 10
NEG = -1e30

QKV_W = 3 * ATT_OUT
IN_COLS = 15904
SHARD_COLS = IN_COLS // 4
DT_COL = 12800
ROW_TILE = 16
WIN_ROWS = 4000


def _win_offset(k):
    return (k * SHARD_COLS) % ROW_TILE


def _win_start(k):
    return k * SHARD_COLS - _win_offset(k)

VMEM_LIMIT_BYTES = 56 * 1024 * 1024
LANE = 128
MESH = pl.DeviceIdType.MESH
NT = (((1,), (1,)), ((), ()))
TN = (((0,), (0,)), ((), ()))


def _pcall(body, **kw):
    return pl.pallas_call(body, **kw)


def _params(*sem):
    return pltpu.CompilerParams(dimension_semantics=sem, vmem_limit_bytes=VMEM_LIMIT_BYTES)


def _sigmoid(v):
    return jax.nn.sigmoid(v)


MM_TM = 512


def _tok_spec(tm, width):
    return pl.BlockSpec((None, tm, width), lambda b, i: (b, i, 0))


def _whole(arr, single_buffer=False):
    mode = dict(pipeline_mode=pl.Buffered(1)) if single_buffer else {}
    return pl.BlockSpec(arr.shape, lambda b, i: (0,) * arr.ndim, **mode)


def _proj(a3, ws, out_dtype, name, w_rows_are_outputs=False, tm=MM_TM):
    nb, seq, kdim = a3.shape
    nw = len(ws)
    widths = [w.shape[0] if w_rows_are_outputs else w.shape[1] for w in ws]

    def body(*refs):
        a = refs[0][...].astype(BF16)
        for w_ref, o_ref in zip(refs[1:1 + nw], refs[1 + nw:]):
            if w_rows_are_outputs:
                v = lax.dot_general(a, w_ref[...], NT, preferred_element_type=F32)
            else:
                v = jnp.dot(a, w_ref[...], preferred_element_type=F32)
            o_ref[...] = v.astype(out_dtype)

    return _pcall(
        body, name=name, grid=(nb, seq // tm),
        in_specs=[_tok_spec(tm, kdim)] + [_whole(w, True) for w in ws],
        out_specs=[_tok_spec(tm, n) for n in widths],
        out_shape=[jax.ShapeDtypeStruct((nb, seq, n), out_dtype) for n in widths],
        compiler_params=_params("parallel", "parallel"),
    )(a3, *ws)


def _dx(dhs, ws, accs, name, w_rows_are_outputs=False, tm=MM_TM, scatter=None, own_order_accs=(),
        own_order_dhs=()):
    nb, seq, _ = dhs[0].shape
    nd, nacc, npa, npd = len(dhs), len(accs), len(own_order_accs), len(own_order_dhs)
    kout = ws[0].shape[1] if w_rows_are_outputs else ws[0].shape[0]
    sums = scatter or []
    ns = len(sums)
    chunks = [_row_chunks(s.shape[1], s.shape[2] * s.dtype.itemsize, ICI_CHUNK_BYTES) for s in sums]
    n_sem = 3 * sum(len(ch) for ch in chunks)
    grid = (nb, seq // tm)
    ntile = kout // LANE if npa or npd else 0

    def body(*refs):
        n_own = 2 * nd + nacc + npa
        n_in = n_own + 2 * npd
        sum_refs, o_ref, got_refs = refs[n_in:n_in + ns], refs[n_in + ns], refs[n_in + ns + 1:n_in + 2 * ns + 1]
        tile_refs = refs[n_in + 2 * ns + 1:n_in + 2 * ns + 1 + ntile]

        def copies():
            send_sems, recv_sems = refs[-2], refs[-1]
            x, y, c = _place()
            out = []
            for a in range(ns):
                for s, m in chunks[a]:
                    for j, (cx, cy) in enumerate(_other_chips(x, y)):
                        k = len(out)
                        out.append(_remote(sum_refs[a].at[2 * cx + cy, pl.ds(s, m)], got_refs[a].at[j, pl.ds(s, m)],
                                           send_sems.at[k], recv_sems.at[k], (cx, cy, c)))
            return out

        if ns:
            @pl.when((pl.program_id(0) == 0) & (pl.program_id(1) == 0))
            def _():
                for cp in copies():
                    cp.start()

        v = None
        for dh_ref, w_ref in zip(refs[:nd], refs[nd:2 * nd]):
            dh = dh_ref[...].astype(BF16)
            if w_rows_are_outputs:
                t = jnp.dot(dh, w_ref[...], preferred_element_type=F32)
            else:
                t = lax.dot_general(dh, w_ref[...], NT, preferred_element_type=F32)
            v = t if v is None else v + t
        for a_ref in refs[2 * nd:2 * nd + nacc]:
            v = v + a_ref[...]
        for p_ref in refs[2 * nd + nacc:n_own]:
            v = v + _natural_rows(p_ref, tile_refs)
        for q_ref, w_ref in zip(refs[n_own:n_own + npd], refs[n_own + npd:n_in]):
            d, per, n = q_ref.shape
            dh = q_ref[...].reshape(d * per, n).astype(BF16)
            if w_rows_are_outputs:
                t = jnp.dot(dh, w_ref[...], preferred_element_type=F32)
            else:
                t = lax.dot_general(dh, w_ref[...], NT, preferred_element_type=F32)
            v = v + _natural_value(t, d, tile_refs)
        o_ref[...] = v

        if ns:
            @pl.when((pl.program_id(0) == grid[0] - 1) & (pl.program_id(1) == grid[1] - 1))
            def _():
                for cp in copies():
                    cp.wait()

    out = _pcall(
        body, name=name, grid=grid,
        in_specs=[_tok_spec(tm, dh.shape[-1]) for dh in dhs] + [_whole(w, True) for w in ws]
        + [_tok_spec(tm, kout)] * nacc
        + [pl.BlockSpec((None, p.shape[1], tm // p.shape[1], kout), lambda b, i: (b, 0, i, 0)) for p in own_order_accs]
        + [pl.BlockSpec((None, q.shape[1], tm // q.shape[1], q.shape[3]), lambda b, i: (b, 0, i, 0))
           for q, _ in own_order_dhs]
        + [_whole(w, True) for _, w in own_order_dhs]
        + [ANY] * ns,
        out_specs=[_tok_spec(tm, kout)] + [ANY] * ns,
        out_shape=[jax.ShapeDtypeStruct((nb, seq, kout), F32)]
        + [jax.ShapeDtypeStruct((3,) + s.shape[1:], s.dtype) for s in sums],
        input_output_aliases={2 * nd: 0} if nacc else {},
        scratch_shapes=[pltpu.VMEM((tm, LANE), F32)] * ntile
        + ([pltpu.SemaphoreType.DMA((n_sem,)), pltpu.SemaphoreType.DMA((n_sem,))] if ns else []),
        compiler_params=pltpu.CompilerParams(
            dimension_semantics=("arbitrary", "arbitrary") if ns else ("parallel", "parallel"),
            vmem_limit_bytes=VMEM_LIMIT_BYTES, has_side_effects=bool(ns)),
    )(*dhs, *ws, *accs, *own_order_accs, *[q for q, _ in own_order_dhs], *[w for _, w in own_order_dhs], *sums)
    return (out[0], list(out[1:])) if ns else out[0]


def _dw(a3, dhs, out_dtype, name, rows_are_outputs=False):
    nb, seq, kdim = a3.shape
    nd = len(dhs)
    grid = (nb, seq // MM_TM)
    shapes = [(dh.shape[-1], kdim) if rows_are_outputs else (kdim, dh.shape[-1]) for dh in dhs]

    def body(*refs):
        b, i = pl.program_id(0), pl.program_id(1)
        dh_refs, o_refs, acc_refs = refs[1:1 + nd], refs[1 + nd:1 + 2 * nd], refs[1 + 2 * nd:]

        @pl.when((b == 0) & (i == 0))
        def _():
            for acc_ref in acc_refs:
                acc_ref[...] = jnp.zeros_like(acc_ref)

        a = refs[0][...].astype(BF16)
        for dh_ref, acc_ref in zip(dh_refs, acc_refs):
            dh = dh_ref[...].astype(BF16)
            acc_ref[...] += lax.dot_general(*((dh, a) if rows_are_outputs else (a, dh)), TN,
                                            preferred_element_type=F32)

        @pl.when((b == grid[0] - 1) & (i == grid[1] - 1))
        def _():
            for o_ref, acc_ref in zip(o_refs, acc_refs):
                o_ref[...] = acc_ref[...].astype(out_dtype)

    return _pcall(
        body, name=name, grid=grid,
        in_specs=[_tok_spec(MM_TM, kdim)] + [_tok_spec(MM_TM, dh.shape[-1]) for dh in dhs],
        out_specs=[pl.BlockSpec(s, lambda b, i: (0, 0)) for s in shapes],
        out_shape=[jax.ShapeDtypeStruct(s, out_dtype) for s in shapes],
        scratch_shapes=[pltpu.VMEM(s, F32) for s in shapes],
        compiler_params=_params("arbitrary", "arbitrary"),
    )(a3, *dhs)


def _qkv_rows(g):
    return [(part * QKV_W + g * ATT_OUT + hp * LANE, LANE) for hp in range(ATT_OUT // LANE) for part in range(3)]


XBC_START = 3 * QKV_W + ATT_OUT + D_INNER
GROUP_CH = GROUP_SSM_HEADS * HEAD_DIM
XBC_GROUP = GROUP_CH + 2 * D_STATE
CONV_DIM = SSM_GROUPS * XBC_GROUP


def _xbc_ranges():
    out = []
    for g in range(SSM_GROUPS):
        out += [(g * GROUP_CH, GROUP_CH), (D_INNER + g * D_STATE, D_STATE),
                (D_INNER + SSM_GROUPS * D_STATE + g * D_STATE, D_STATE)]
    return out


def _join_last(parts):
    widths = [t.shape[-1] for t in parts]
    total, lead = sum(widths), [(0, 0)] * (parts[0].ndim - 1)
    starts = np.cumsum([0] + widths)
    return sum(jnp.pad(t, lead + [(int(s), total - int(s) - w)]) for t, s, w in zip(parts, starts, widths))


def _xbc_group_order(t):
    return _join_last([t[..., s:s + n] for s, n in _xbc_ranges()])


def _xbc_reference_order(t):
    g = lambda off, n: [t[..., k * XBC_GROUP + off:k * XBC_GROUP + off + n] for k in range(SSM_GROUPS)]
    return _join_last(g(0, GROUP_CH) + g(GROUP_CH, D_STATE) + g(GROUP_CH + D_STATE, D_STATE))


def _segments():
    one = lambda name, start, rows: (name, [(start, rows)], max(rows, LANE))
    return [("qkv%d" % g, _qkv_rows(g), QKV_W) for g in range(3)] + [
        one("gatt", 3 * QKV_W, ATT_OUT), one("z", 3 * QKV_W + ATT_OUT, D_INNER),
        ("xbc", [(XBC_START + s, n) for s, n in _xbc_ranges()], CONV_DIM), one("dt", DT_COL, SSM_HEADS),
        one("gm", DT_COL + SSM_HEADS, 2 * D_MODEL), one("gp", DT_COL + SSM_HEADS + 2 * D_MODEL, D_MODEL)]


LAYOUT_TC = 256
NAT_GROUPS = ((("gatt", "z", "dt", "gp"), 512), (("xbc", "gm"), 512))
DW_GROUPS = (("gatt", "z", "dt", "gp"), ("xbc",), ("gm",))


def _assemble(win):
    segs = _segments()

    def body(win_ref, *outs):
        def pieces(start, rows):
            t, end = start, start + rows
            while t < end:
                k = min(t // SHARD_COLS, 3)
                shard_end = (k + 1) * SHARD_COLS
                if k < 3 and shard_end % ROW_TILE and t == shard_end - shard_end % ROW_TILE:
                    lo = t - _win_start(k)
                    yield win_ref[k, lo:lo + ROW_TILE, :] + win_ref[k + 1, 0:ROW_TILE, :]
                    t += ROW_TILE
                    continue
                upto = min(end, shard_end - shard_end % ROW_TILE if k < 3 else end)
                yield win_ref[k, t - _win_start(k):upto - _win_start(k), :]
                t = upto

        for (_, ranges, total), o_ref in zip(segs, outs):
            off = 0
            for start, rows in ranges:
                for part in pieces(start, rows):
                    o_ref[off:off + part.shape[0], :] = part
                    off += part.shape[0]
            if off < total:
                o_ref[off:total, :] = jnp.zeros((total - off, o_ref.shape[1]), BF16)

    outs = _pcall(
        body, name="assemble_w_in", grid=(D_MODEL // LAYOUT_TC,),
        in_specs=[pl.BlockSpec((4, WIN_ROWS, LAYOUT_TC), lambda i: (0, 0, i))],
        out_specs=[pl.BlockSpec((total, LAYOUT_TC), lambda i: (0, i)) for _, _, total in segs],
        out_shape=[jax.ShapeDtypeStruct((total, D_MODEL), BF16) for _, _, total in segs],
        compiler_params=_params("parallel"),
    )(win)
    return {name: o for (name, _, _), o in zip(segs, outs)}


def _pack(dsegs):
    segs = _segments()

    def body(*refs):
        ins, o_ref = refs[:-1], refs[-1]
        tail = IN_COLS - _win_start(3)
        o_ref[3, tail:, :] = jnp.zeros((WIN_ROWS - tail, o_ref.shape[2]), BF16)
        for (_, ranges, _), s_ref in zip(segs, ins):
            off = 0
            for start, rows in ranges:
                for k in range(4):
                    lo = _win_start(k)
                    a, b = max(start, lo), min(start + rows, lo + WIN_ROWS)
                    if a < b:
                        o_ref[k, a - lo:b - lo, :] = s_ref[off + a - start:off + b - start, :]
                off += rows

    return _pcall(
        body, name="pack_dw_in", grid=(D_MODEL // LAYOUT_TC,),
        in_specs=[pl.BlockSpec((total, LAYOUT_TC), lambda i: (0, i)) for _, _, total in segs],
        out_specs=pl.BlockSpec((4, WIN_ROWS, LAYOUT_TC), lambda i: (0, 0, i)),
        out_shape=jax.ShapeDtypeStruct((4, WIN_ROWS, D_MODEL), BF16),
        compiler_params=_params("parallel"),
    )(*[dsegs[name] for name, _, _ in segs])


def _shard_to_window(shard_t, k):
    def at(off):
        return lambda w: jnp.pad(w.astype(BF16), ((off, WIN_ROWS - SHARD_COLS - off), (0, 0)))

    return lax.cond(k % 2 == 1, at(_win_offset(1)), at(_win_offset(0)), shard_t)


def _window_to_shard(win, k):
    return lax.dynamic_slice(win, ((k % 2) * _win_offset(1), 0), (SHARD_COLS, D_MODEL))


def _bucket_maps():
    qi = np.arange(8)[:, None]
    kj = np.arange(2 * BAND)[None, :]
    delta = qi + BAND - kj
    maps = []
    for window, dil in PATTERNS:
        valid = (delta >= 0) & (delta <= window // dil)
        dist = np.maximum(delta, 0) * dil
        max_exact = NUM_BUCKETS // 2
        d_f = np.maximum(dist, 1).astype(np.float32)
        large = max_exact + (np.log(d_f / np.float32(max_exact)) / np.float32(math.log(MAX_DISTANCE / max_exact))
                             * np.float32(NUM_BUCKETS - max_exact)).astype(np.int32)
        large = np.minimum(large, NUM_BUCKETS - 1)
        bucket = np.where(dist < max_exact, dist, large)
        maps.append(np.where(valid, bucket, -1).astype(np.int32))
    return np.stack(maps)


def _bias_tables(rel_bias, bmaps):
    def body(rb_ref, bm_ref, o_ref):
        g = pl.program_id(0)
        bm = bm_ref[...]
        for hh in range(GROUP_HEADS):
            acc = jnp.full(bm.shape, NEG, F32)
            for b in range(NUM_BUCKETS):
                acc = jnp.where(bm == b, rb_ref[b, g * GROUP_HEADS + hh], acc)
            for a in range(BAND // 8):
                o_ref[hh, 8 * a:8 * a + 8, :] = acc if a == 0 else pltpu.roll(acc, 8 * a, 1)

    return _pcall(
        body, name="bias_tables", grid=(3,),
        in_specs=[pl.BlockSpec(memory_space=pltpu.SMEM),
                  pl.BlockSpec((None, 8, 2 * BAND), lambda g: (g, 0, 0))],
        out_specs=pl.BlockSpec((GROUP_HEADS, BAND, 2 * BAND), lambda g: (g, 0, 0)),
        out_shape=jax.ShapeDtypeStruct((3 * GROUP_HEADS, BAND, 2 * BAND), F32),
        compiler_params=_params("parallel"),
    )(rel_bias, bmaps)


def _bias_grad(dbias, bmaps):
    def body(db_ref, bm_ref, o_ref):
        bm = bm_ref[...]
        lane = lax.broadcasted_iota(jnp.int32, (1, LANE), 1)
        for hh in range(GROUP_HEADS):
            db = db_ref[hh, 0:8, :]
            for a in range(1, BAND // 8):
                db = db + pltpu.roll(db_ref[hh, 8 * a:8 * a + 8, :], 2 * BAND - 8 * a, 1)
            vec = jnp.zeros((1, LANE), F32)
            for b in range(NUM_BUCKETS):
                s = jnp.sum(jnp.where(bm == b, db, 0.0), keepdims=True)
                vec = jnp.where(lane == b, s, vec)
            o_ref[hh] = vec

    return _pcall(
        body, name="bias_grad", grid=(3,),
        in_specs=[pl.BlockSpec((GROUP_HEADS, BAND, 2 * BAND), lambda g: (g, 0, 0)),
                  pl.BlockSpec((None, 8, 2 * BAND), lambda g: (g, 0, 0))],
        out_specs=pl.BlockSpec((GROUP_HEADS, 1, LANE), lambda g: (g, 0, 0)),
        out_shape=jax.ShapeDtypeStruct((3 * GROUP_HEADS, 1, LANE), F32),
        compiler_params=_params("parallel"),
    )(dbias, bmaps)


def _rows(n):
    if isinstance(n, int):
        return pl.ds(n * BAND, BAND)
    return pl.ds(pl.multiple_of(n * BAND, BAND), BAND)


def _for_blocks(blocks, nblk, per, carry):
    carry = blocks([0], carry, False)
    start = 1 + (nblk - 1) % per
    for n in range(1, start):
        carry = blocks([n], carry, True)
    trips = (nblk - start) // per
    if trips > 0:
        carry = lax.fori_loop(
            0, trips, lambda t, c: blocks([start + t * per + u for u in range(per)], c, True), carry)
    return carry


def _pairs_per_step(d):
    return {1: 3, 4: 6, 16: 6}[d]


def _bias_spec(group, hps):
    first = group * GROUP_HEADS // (2 * hps)
    return pl.BlockSpec((2 * hps, BAND, 2 * BAND), lambda hp, b, r: (first + hp, 0, 0))


def _attn_fwd(qkv4, bias, group, d, name):
    nb, _, sub, _ = qkv4.shape
    nblk = sub // BAND
    scale = HEAD_DIM ** -0.5
    npair = ATT_OUT // LANE
    hps = _pairs_per_step(d)
    compact = d > 1

    def body(qkv_ref, bias_ref, o_ref, l_ref):
        def blocks(ns, carry, with_prev):
            chains = [(bi, i, h) for bi in range(len(ns)) for i in range(hps) for h in range(2)]
            first_head = lax.broadcasted_iota(jnp.int32, (BAND, LANE), 1) < HEAD_DIM
            pair = lambda n, i, part: qkv_ref[_rows(n), (3 * i + part) * LANE:(3 * i + part + 1) * LANE]
            scores = []
            for bi, i, h in chains:
                n = ns[bi]
                qp = pair(n, i, 0) * scale
                q = jnp.where(first_head if h == 0 else jnp.logical_not(first_head), qp, jnp.zeros_like(qp))
                s_c = lax.dot_general(q, pair(n, i, 1), NT, preferred_element_type=F32) + bias_ref[2 * i + h, :, BAND:]
                s_p = None
                if with_prev:
                    s_p = lax.dot_general(q, pair(n - 1, i, 1), NT,
                                          preferred_element_type=F32) + bias_ref[2 * i + h, :, :BAND]
                scores.append((s_c, s_p))
            probs = []
            for s_c, s_p in scores:
                m = jnp.max(s_c, -1, keepdims=True)
                if with_prev:
                    m = jnp.maximum(m, jnp.max(s_p, -1, keepdims=True))
                e_c = jnp.exp(s_c - m)
                den = jnp.sum(e_c, -1, keepdims=True)
                e_p = None
                if with_prev:
                    e_p = jnp.exp(s_p - m)
                    den = den + jnp.sum(e_p, -1, keepdims=True)
                    e_p = e_p.astype(BF16)
                probs.append((e_c.astype(BF16), e_p, den, m))
            outs = {}
            for (bi, i, h), (e_c, e_p, den, m) in zip(chains, probs):
                n = ns[bi]
                acc = jnp.dot(e_c, pair(n, i, 2), preferred_element_type=F32)
                if with_prev:
                    acc = acc + jnp.dot(e_p, pair(n - 1, i, 2), preferred_element_type=F32)
                outs[(bi, i, h)] = (acc / den, m + jnp.log(den))
            lane = lax.broadcasted_iota(jnp.int32, (BAND, LANE), 1)
            for bi, n in enumerate(ns):
                per_head = jnp.zeros((BAND, LANE), F32)
                for i in range(hps):
                    o_ref[_rows(n), i * LANE:(i + 1) * LANE] = jnp.where(first_head, outs[(bi, i, 0)][0],
                                                                         outs[(bi, i, 1)][0])
                    if compact:
                        for h in range(2):
                            per_head = jnp.where(lane == 2 * i + h, outs[(bi, i, h)][1], per_head)
                    else:
                        l_ref[_rows(n), i * LANE:(i + 1) * LANE] = jnp.where(first_head, outs[(bi, i, 0)][1],
                                                                             outs[(bi, i, 1)][1])
                if compact:
                    l_ref[_rows(n), :] = per_head
            return carry

        _for_blocks(blocks, nblk, 2 if hps == 1 else 1, 0)

    in_specs = [pl.BlockSpec((None, None, sub, 3 * LANE * hps), lambda hp, b, r: (b, r, 0, hp)),
                _bias_spec(group, hps)]
    if compact:
        return _pcall(
            body, name=name, grid=(1, nb, d), in_specs=in_specs,
            out_specs=[pl.BlockSpec((None, None, sub, ATT_OUT), lambda hp, b, r: (b, r, 0, 0)),
                       pl.BlockSpec((None, None, sub, LANE), lambda hp, b, r: (b, r, 0, 0))],
            out_shape=[jax.ShapeDtypeStruct((nb, d, sub, ATT_OUT), F32), jax.ShapeDtypeStruct((nb, d, sub, LANE), F32)],
            compiler_params=_params("parallel", "parallel", "parallel"),
        )(qkv4, bias)
    ospec = pl.BlockSpec((None, sub, hps * LANE), lambda hp, b, r: (b, 0, r * (npair // hps) + hp))
    return _pcall(
        body, name=name, grid=(npair // hps, nb, d), in_specs=in_specs, out_specs=[ospec, ospec],
        out_shape=[jax.ShapeDtypeStruct((nb, sub, d * ATT_OUT), F32)] * 2,
        compiler_params=_params("parallel", "parallel", "parallel"),
    )(qkv4, bias)


STAT_LSE_LANE = 16


def _attn_bwd(qkv4, bias, group, cotangent, d, name):
    nb, _, sub, _ = qkv4.shape
    nblk = sub // BAND
    scale = HEAD_DIM ** -0.5
    npair = ATT_OUT // LANE
    hps = _pairs_per_step(d)
    compact = d > 1

    def body(qkv_ref, bias_ref, *rest):
        do_ref, dqkv_ref, db_ref = rest[0], rest[-2], rest[-1]
        b, r = pl.program_id(1), pl.program_id(2)

        @pl.when((b == 0) & (r == 0))
        def _():
            db_ref[...] = jnp.zeros_like(db_ref)

        def blocks(ns, carry, with_prev):
            sides = (0, 1) if with_prev else (0,)
            chains = [(bi, i, h, sd) for bi in range(len(ns)) for i in range(hps) for h in range(2) for sd in sides]
            first_head = lax.broadcasted_iota(jnp.int32, (BAND, LANE), 1) < HEAD_DIM
            own = lambda h, t: jnp.where(first_head if h == 0 else jnp.logical_not(first_head), t, jnp.zeros_like(t))
            pair = lambda rows, i, part: qkv_ref[rows, (3 * i + part) * LANE:(3 * i + part + 1) * LANE]
            key_rows = lambda bi, sd: _rows(ns[bi] - sd)
            qs = {}
            for bi in range(len(ns)):
                for i in range(hps):
                    q_pair = pair(_rows(ns[bi]), i, 0) * scale
                    do = do_ref[_rows(ns[bi]), i * LANE:(i + 1) * LANE]
                    do16 = do.astype(BF16)
                    for h in range(2):
                        if compact:
                            st_ref, head = rest[1], 2 * i + h
                            ebar = st_ref[_rows(ns[bi]), head:head + 1]
                            lcol = st_ref[_rows(ns[bi]), STAT_LSE_LANE + head:STAT_LSE_LANE + head + 1]
                        else:
                            ebar = jnp.sum(own(h, do * rest[1][_rows(ns[bi]), i * LANE:(i + 1) * LANE]), -1, keepdims=True)
                            lcol = rest[2][_rows(ns[bi]), i * LANE + h * HEAD_DIM:i * LANE + h * HEAD_DIM + 1]
                        qs[(bi, i, h)] = (own(h, q_pair), q_pair, own(h, do16), do16, ebar, lcol)
            raw = []
            for bi, i, h, sd in chains:
                q, _, do_h, _, _, _ = qs[(bi, i, h)]
                bias_blk = bias_ref[2 * i + h, :, :BAND] if sd else bias_ref[2 * i + h, :, BAND:]
                s = lax.dot_general(q, pair(key_rows(bi, sd), i, 1), NT, preferred_element_type=F32) + bias_blk
                dp = lax.dot_general(do_h, pair(key_rows(bi, sd), i, 2), NT, preferred_element_type=F32)
                raw.append((s, dp))
            soft = []
            for (bi, i, h, sd), (s, dp) in zip(chains, raw):
                ebar, lcol = qs[(bi, i, h)][4:]
                p = jnp.exp(s - lcol)
                ds = p * (dp - ebar)
                if sd:
                    db_ref[2 * i + h, :, :BAND] += ds
                else:
                    db_ref[2 * i + h, :, BAND:] += ds
                soft.append((p.astype(BF16), ds.astype(BF16)))
            grads = {}
            for (bi, i, h, sd), (p16, ds16) in zip(chains, soft):
                _, q_pair, _, do16 = qs[(bi, i, h)][:4]
                grads[(bi, i, h, sd)] = (
                    jnp.dot(ds16, pair(key_rows(bi, sd), i, 1), preferred_element_type=F32),
                    lax.dot_general(ds16, q_pair, TN, preferred_element_type=F32),
                    lax.dot_general(p16, do16, TN, preferred_element_type=F32))
            both = lambda bi, i, sd, which: jnp.where(first_head, grads[(bi, i, 0, sd)][which],
                                                      grads[(bi, i, 1, sd)][which])
            carry = list(carry) if carry is not None else None
            for bi, n in enumerate(ns):
                for i in range(hps):
                    base = 3 * LANE * i
                    dq = both(bi, i, 0, 0)
                    if with_prev:
                        dq = dq + both(bi, i, 1, 0)
                        dqkv_ref[_rows(n - 1), base + LANE:base + 2 * LANE] = (
                            carry[2 * i] + both(bi, i, 1, 1)).astype(BF16)
                        dqkv_ref[_rows(n - 1), base + 2 * LANE:base + 3 * LANE] = (
                            carry[2 * i + 1] + both(bi, i, 1, 2)).astype(BF16)
                    dqkv_ref[_rows(n), base:base + LANE] = (dq * scale).astype(BF16)
                carry = [t for i in range(hps) for t in (both(bi, i, 0, 1), both(bi, i, 0, 2))]
            return tuple(carry)

        carry = _for_blocks(blocks, nblk, 2 if hps == 1 else 1, None)
        for i in range(hps):
            base = 3 * LANE * i
            dqkv_ref[_rows(nblk - 1), base + LANE:base + 2 * LANE] = carry[2 * i].astype(BF16)
            dqkv_ref[_rows(nblk - 1), base + 2 * LANE:base + 3 * LANE] = carry[2 * i + 1].astype(BF16)

    qspec = pl.BlockSpec((None, None, sub, 3 * LANE * hps), lambda hp, b, r: (b, r, 0, hp))
    bspec = pl.BlockSpec((2 * hps, BAND, 2 * BAND), lambda hp, b, r: (hp, 0, 0))
    if compact:
        cspecs = [pl.BlockSpec((None, None, sub, ATT_OUT), lambda hp, b, r: (b, r, 0, 0)),
                  pl.BlockSpec((None, None, sub, LANE), lambda hp, b, r: (b, r, 0, 0))]
    else:
        cspecs = [pl.BlockSpec((None, sub, hps * LANE), lambda hp, b, r: (b, 0, r * (npair // hps) + hp))] * 3
    return _pcall(
        body, name=name, grid=(npair // hps, nb, d),
        in_specs=[qspec, _bias_spec(group, hps)] + cspecs, out_specs=[qspec, bspec],
        out_shape=[jax.ShapeDtypeStruct(qkv4.shape, BF16),
                   jax.ShapeDtypeStruct((GROUP_HEADS, BAND, 2 * BAND), F32)],
        compiler_params=_params("parallel", "arbitrary", "arbitrary"),
    )(qkv4, bias, *cotangent)


def _head_lanes(first_lane, one_channel):
    c = lax.broadcasted_iota(jnp.int32, (ATT_OUT, LANE), 0)
    lane = lax.broadcasted_iota(jnp.int32, (ATT_OUT, LANE), 1)
    hit = lane == first_lane + c // HEAD_DIM
    if one_channel:
        hit = hit & (c % HEAD_DIM == 0)
    return hit.astype(BF16)


def _exact_dot(v, m01, dims=None):
    parts = _split3(v)
    if dims is None:
        dot = lambda t: jnp.dot(t, m01, preferred_element_type=F32)
    else:
        dot = lambda t: lax.dot_general(t, m01, dims, preferred_element_type=F32)
    return (dot(parts[0]) + dot(parts[1])) + dot(parts[2])


def _store_own_order(value, tile_refs, out_ref):
    d, per, width = out_ref.shape
    for j in range(width // LANE):
        tile_refs[j][...] = value[:, j * LANE:(j + 1) * LANE]
    for r in range(d):
        rows = pl.ds(r, per, stride=d)
        for j in range(width // LANE):
            out_ref[r, :, j * LANE:(j + 1) * LANE] = tile_refs[j][rows, :].astype(out_ref.dtype)


def _token_orders(x, dilations):
    nb, seq, kdim = x.shape
    tm = 512

    def body(x_ref, nat_ref, *rest):
        outs, tile_refs = rest[:len(dilations)], rest[len(dilations):]
        xv = x_ref[...]
        nat_ref[...] = xv.astype(BF16)
        for o_ref in outs:
            _store_own_order(xv, tile_refs, o_ref)

    outs = _pcall(
        body, name="token_orders", grid=(nb, seq // tm), in_specs=[_tok_spec(tm, kdim)],
        out_specs=[_tok_spec(tm, kdim)]
        + [pl.BlockSpec((None, d, tm // d, kdim), lambda b, i: (b, 0, i, 0)) for d in dilations],
        out_shape=[jax.ShapeDtypeStruct((nb, seq, kdim), BF16)]
        + [jax.ShapeDtypeStruct((nb, d, seq // d, kdim), BF16) for d in dilations],
        scratch_shapes=[pltpu.VMEM((tm, LANE), F32)] * (kdim // LANE),
        compiler_params=_params("parallel", "parallel"),
    )(x)
    return [outs[0]] + [o.reshape(nb, seq, kdim) for o in outs[1:]]


def _natural_rows(p_ref, tile_refs):
    d, per, width = p_ref.shape
    for r in range(d):
        rows = pl.ds(r, per, stride=d)
        for j in range(width // LANE):
            tile_refs[j][rows, :] = p_ref[r, :, j * LANE:(j + 1) * LANE]
    return jnp.concatenate([tile_refs[j][...] for j in range(width // LANE)], axis=1)


def _natural_value(value, d, tile_refs):
    total, width = value.shape
    per = total // d
    for r in range(d):
        rows = pl.ds(r, per, stride=d)
        for j in range(width // LANE):
            tile_refs[j][rows, :] = value[r * per:(r + 1) * per, j * LANE:(j + 1) * LANE]
    return jnp.concatenate([tile_refs[j][...] for j in range(width // LANE)], axis=1)


def _combine_fwd(o0, l0, dilated, gatt):
    nb, seq, _ = gatt.shape
    tm = 512
    ntile = ATT_OUT // LANE

    def body(o0_ref, l0_ref, o1_ref, l1_ref, o2_ref, l2_ref, g_ref, oa_ref, oatt_ref, lse_ref, *tile_refs):
        spread = _head_lanes(0, False)
        l0v = l0_ref[...]
        l1v = _exact_dot(_natural_rows(l1_ref, tile_refs), spread, NT)
        l2v = _exact_dot(_natural_rows(l2_ref, tile_refs), spread, NT)
        m = jnp.maximum(jnp.maximum(l0v, l1v), l2v)
        tot = m + jnp.log(jnp.exp(l0v - m) + jnp.exp(l1v - m) + jnp.exp(l2v - m))
        o = jnp.exp(l0v - tot) * o0_ref[...]
        o = o + jnp.exp(l1v - tot) * _natural_rows(o1_ref, tile_refs)
        o = o + jnp.exp(l2v - tot) * _natural_rows(o2_ref, tile_refs)
        g = g_ref[...]
        oa_ref[...] = (o * (g * _sigmoid(g))).astype(BF16)
        oatt_ref[...] = o
        lse_ref[...] = tot

    spec = pl.BlockSpec((None, tm, ATT_OUT), lambda b, i: (b, i, 0))
    own = lambda t: pl.BlockSpec((None, t.shape[1], tm // t.shape[1], t.shape[3]), lambda b, i: (b, 0, i, 0))
    (o1, l1), (o2, l2) = dilated
    return _pcall(
        body, name="attn_combine", grid=(nb, seq // tm),
        in_specs=[spec, spec, own(o1), own(l1), own(o2), own(l2), spec], out_specs=[spec] * 3,
        out_shape=[jax.ShapeDtypeStruct((nb, seq, ATT_OUT), BF16), jax.ShapeDtypeStruct((nb, seq, ATT_OUT), F32),
                   jax.ShapeDtypeStruct((nb, seq, ATT_OUT), F32)],
        scratch_shapes=[pltpu.VMEM((tm, LANE), F32)] * ntile,
        compiler_params=_params("parallel", "parallel"),
    )(o0, l0, o1, l1, o2, l2, gatt)


def _combine_bwd(dya16, w_bra, gatt, o_att, lse, dilations):
    nb, seq, _ = gatt.shape
    tm = 512

    def body(dya_ref, w_ref, g_ref, o_ref, l_ref, do_ref, dg_ref, *rest):
        ntile = ATT_OUT // LANE
        outs, tile_refs = rest[:-ntile], rest[-ntile:]
        doa = lax.dot_general(dya_ref[...], w_ref[...], NT, preferred_element_type=F32)
        g = g_ref[...]
        sg = _sigmoid(g)
        do = doa * (g * sg)
        do_ref[...] = do
        stats = (_exact_dot(do * o_ref[...], _head_lanes(0, False))
                 + _exact_dot(l_ref[...], _head_lanes(STAT_LSE_LANE, True)))
        dg_ref[...] = (doa * o_ref[...] * (sg * (1.0 + g * (1.0 - sg)))).astype(BF16)
        for k in range(len(dilations)):
            _store_own_order(do, tile_refs, outs[2 * k])
            _store_own_order(stats, tile_refs, outs[2 * k + 1])

    spec = pl.BlockSpec((None, tm, ATT_OUT), lambda b, i: (b, i, 0))
    own = lambda d, width: pl.BlockSpec((None, d, tm // d, width), lambda b, i: (b, 0, i, 0))
    outs = _pcall(
        body, name="attn_combine_bwd", grid=(nb, seq // tm),
        in_specs=[_tok_spec(tm, D_MODEL), _whole(w_bra, True)] + [spec] * 3,
        out_specs=[spec, spec] + [own(d, w) for d in dilations for w in (ATT_OUT, LANE)],
        out_shape=[jax.ShapeDtypeStruct((nb, seq, ATT_OUT), F32), jax.ShapeDtypeStruct((nb, seq, ATT_OUT), BF16)]
        + [jax.ShapeDtypeStruct((nb, d, seq // d, w), t) for d in dilations for w, t in ((ATT_OUT, BF16), (LANE, F32))],
        scratch_shapes=[pltpu.VMEM((tm, LANE), F32)] * (ATT_OUT // LANE),
        compiler_params=_params("parallel", "parallel"),
    )(dya16, w_bra, gatt, o_att, lse)
    return outs[0], outs[1], outs[2:]


CONV_TM = 1024
CONV_TC = 1024


def _shift_down(cur, halo, k):
    rolled = pltpu.roll(cur, k, 0)
    hro = pltpu.roll(halo, k, 0)
    row = lax.broadcasted_iota(jnp.int32, hro.shape, 0)
    return jnp.concatenate([jnp.where(row < k, hro, rolled[:8]), rolled[8:]], axis=0)


def _shift_up(cur, halo, k):
    n = cur.shape[0]
    rolled = pltpu.roll(cur, n - k, 0)
    hro = pltpu.roll(halo, 8 - k, 0)
    row = lax.broadcasted_iota(jnp.int32, hro.shape, 0)
    return jnp.concatenate([rolled[:n - 8], jnp.where(row >= 8 - k, hro, rolled[n - 8:])], axis=0)


def _conv_pre(cur, halo, w_ref, b_ref):
    acc = cur * w_ref[3:4, :] + b_ref[...]
    for k in range(1, 4):
        acc = acc + _shift_down(cur, halo, k) * w_ref[3 - k:4 - k, :]
    return acc


def _conv_specs(seq):
    nblk = seq // CONV_TM
    cur = pl.BlockSpec((None, CONV_TM, CONV_TC), lambda cb, b, i: (b, i, cb))
    prev = pl.BlockSpec((None, 8, CONV_TC), lambda cb, b, i: (b, jnp.maximum(i * (CONV_TM // 8) - 1, 0), cb))
    nxt = pl.BlockSpec((None, 8, CONV_TC),
                       lambda cb, b, i: (b, jnp.minimum((i + 1) * (CONV_TM // 8), seq // 8 - 1), cb))
    wspec = pl.BlockSpec((4, CONV_TC), lambda cb, b, i: (0, cb))
    bspec = pl.BlockSpec((1, CONV_TC), lambda cb, b, i: (0, cb))
    return nblk, cur, prev, nxt, wspec, bspec


def _conv_fwd(xin, w4, bias, name):
    nb, seq, ch = xin.shape
    _, cur, prev, _, wspec, bspec = _conv_specs(seq)

    def body(x_ref, h_ref, w_ref, b_ref, o_ref):
        halo = jnp.where(pl.program_id(2) > 0, h_ref[...], 0.0)
        pre = _conv_pre(x_ref[...], halo, w_ref, b_ref)
        o_ref[...] = pre * _sigmoid(pre)

    return _pcall(
        body, name=name, grid=(ch // CONV_TC, nb, seq // CONV_TM),
        in_specs=[cur, prev, wspec, bspec], out_specs=cur,
        out_shape=jax.ShapeDtypeStruct(xin.shape, F32),
        compiler_params=_params("parallel", "parallel", "parallel"),
    )(xin, xin, w4, bias)


def _conv_bwd_pre(dact, xin, w4, bias, name):
    nb, seq, ch = xin.shape
    _, cur, prev, _, wspec, bspec = _conv_specs(seq)

    def body(da_ref, x_ref, h_ref, w_ref, b_ref, dp_ref, s_ref):
        b, i = pl.program_id(1), pl.program_id(2)

        @pl.when((b == 0) & (i == 0))
        def _():
            s_ref[...] = jnp.zeros_like(s_ref)

        halo = jnp.where(i > 0, h_ref[...], 0.0)
        x = x_ref[...]
        pre = _conv_pre(x, halo, w_ref, b_ref)
        sg = _sigmoid(pre)
        dpre = da_ref[...] * (sg * (1.0 + pre * (1.0 - sg)))
        dp_ref[...] = dpre
        s_ref[3:4, :] += jnp.sum(dpre * x, 0, keepdims=True)
        for k in range(1, 4):
            s_ref[3 - k:4 - k, :] += jnp.sum(dpre * _shift_down(x, halo, k), 0, keepdims=True)
        s_ref[4:5, :] += jnp.sum(dpre, 0, keepdims=True)

    return _pcall(
        body, name=name, grid=(ch // CONV_TC, nb, seq // CONV_TM),
        in_specs=[cur, cur, prev, wspec, bspec],
        out_specs=[cur, pl.BlockSpec((8, CONV_TC), lambda cb, b, i: (0, cb))],
        out_shape=[jax.ShapeDtypeStruct(xin.shape, F32), jax.ShapeDtypeStruct((8, ch), F32)],
        compiler_params=_params("parallel", "arbitrary", "arbitrary"),
    )(dact, xin, xin, w4, bias)


def _conv_bwd_x(dpre, w4, name):
    nb, seq, ch = dpre.shape
    nblk, cur, _, nxt, wspec, _ = _conv_specs(seq)

    def body(d_ref, n_ref, w_ref, o_ref):
        halo = jnp.where(pl.program_id(2) < nblk - 1, n_ref[...], 0.0)
        cur_v = d_ref[...]
        acc = cur_v * w_ref[3:4, :]
        for j in range(1, 4):
            acc = acc + _shift_up(cur_v, halo, j) * w_ref[3 - j:4 - j, :]
        o_ref[...] = acc.astype(BF16)

    return _pcall(
        body, name=name, grid=(ch // CONV_TC, nb, seq // CONV_TM),
        in_specs=[cur, nxt, wspec], out_specs=cur,
        out_shape=jax.ShapeDtypeStruct(dpre.shape, BF16),
        compiler_params=_params("parallel", "parallel", "parallel"),
    )(dpre, dpre, w4)


def _step_sizes(raw, tb_ref):
    shift = (LANE - GROUP_SSM_HEADS * pl.program_id(0)) % LANE
    v = pltpu.roll(raw + tb_ref[...], shift, 1)
    own = lax.broadcasted_iota(jnp.int32, v.shape, 1) < GROUP_SSM_HEADS
    sp = jnp.maximum(v, 0.0) + jnp.log1p(jnp.exp(-jnp.abs(v)))
    return jnp.where(own, sp, 0.0), jnp.where(own, _sigmoid(v), 0.0)


def _group_lanes(t):
    pads = [(0, 0)] * (t.ndim - 1) + [(0, LANE - GROUP_SSM_HEADS)]
    return jnp.stack([jnp.pad(t[..., GROUP_SSM_HEADS * g:GROUP_SSM_HEADS * (g + 1)], pads) for g in range(SSM_GROUPS)])


def _ungroup_lanes(t):
    return jnp.concatenate([t[g][..., :GROUP_SSM_HEADS] for g in range(SSM_GROUPS)], axis=-1)


def _decays(dt, al_ref):
    row = lax.broadcasted_iota(jnp.int32, (CHUNK, CHUNK), 0)
    col = lax.broadcasted_iota(jnp.int32, (CHUNK, CHUNK), 1)
    tril = (row >= col).astype(BF16)
    triu = (row <= col).astype(BF16)
    arow = -jnp.exp(al_ref[...])
    hi, mid, lo = _split3(dt * arow)
    down = lambda t: jnp.dot(tril, t, preferred_element_type=F32)
    across = lambda t: lax.dot_general(t, triu, TN, preferred_element_type=F32)
    acs = (down(hi) + down(mid)) + down(lo)
    acs_t = (across(hi) + across(mid)) + across(lo)
    return arow, acs, acs_t, row >= col, triu


STEP_CHUNKS = 8


def _ssd_specs(nb, seq):
    nc = seq // CHUNK
    hw = GROUP_SSM_HEADS * HEAD_DIM
    rows, steps = STEP_CHUNKS * CHUNK, nc // STEP_CHUNKS

    def mk(rev):
        cidx = (lambda c: steps - 1 - c) if rev else (lambda c: c)
        wide = pl.BlockSpec((None, rows, hw), lambda g, b, c: (b, cidx(c), g))
        xbc = pl.BlockSpec((None, rows, XBC_GROUP), lambda g, b, c: (b, cidx(c), g))
        lanes = pl.BlockSpec((None, None, rows, LANE), lambda g, b, c: (g, b, cidx(c), 0))
        prev = pl.BlockSpec((None, STEP_CHUNKS, None, D_STATE, hw), lambda g, b, c: (b, cidx(c), g, 0, 0))
        raw = pl.BlockSpec((None, rows, LANE), lambda g, b, c: (b, cidx(c), 0))
        return wide, xbc, lanes, prev, raw

    grow = pl.BlockSpec((None, 1, LANE), lambda g, b, c: (g, 0, 0))
    nwspec = pl.BlockSpec((1, hw), lambda g, b, c: (0, g))
    tbspec = pl.BlockSpec((1, LANE), lambda g, b, c: (0, 0))
    return nc, steps, hw, mk, grow, nwspec, tbspec


def _head_expand():
    hw = GROUP_SSM_HEADS * HEAD_DIM
    r = lax.broadcasted_iota(jnp.int32, (LANE, hw), 0)
    c = lax.broadcasted_iota(jnp.int32, (LANE, hw), 1)
    return ((c // HEAD_DIM) == r).astype(BF16)


def _split3(v):
    hi = v.astype(BF16)
    rest = v - hi.astype(F32)
    mid = rest.astype(BF16)
    return hi, mid, (rest - mid.astype(F32)).astype(BF16)


def _to_channels(v, e):
    hi, mid, lo = _split3(v)
    dot = lambda t: jnp.dot(t, e, preferred_element_type=F32)
    return (dot(hi) + dot(mid)) + dot(lo)


def _to_heads(w, e):
    hi, mid, lo = _split3(w)
    dot = lambda t: lax.dot_general(t, e, (((1,), (1,)), ((), ())), preferred_element_type=F32)
    return (dot(hi) + dot(mid)) + dot(lo)


def _row8(v):
    return jnp.broadcast_to(v, (8, v.shape[1]))


def _ssd_chunk_setup(dt, al_ref, ds_ref):
    arow, acs, acs_t, causal, triu = _decays(dt, al_ref)
    e = _head_expand()
    dtx = _to_channels(dt, e)
    acsx = _to_channels(acs, e)
    lastx = acsx[CHUNK - 1:CHUNK, :]
    dskx = _to_channels(_row8(ds_ref[...]), e)[0:1, :]
    return arow, acs, acs_t, causal, triu, e, dtx, acsx, lastx, dskx


def _ssd_fwd(xbc, dt_raw, dt_bias_row, z, alog_g, dskip_g, normw):
    nb, seq, _ = xbc.shape
    nc, steps, hw, mk, grow, nwspec, tbspec = _ssd_specs(nb, seq)
    wide, xbc_spec, lanes, prev, raw = mk(False)
    tn = (((0,), (0,)), ((), ()))

    def body(xbc_ref, dt_ref, tb_ref, z_ref, al_ref, ds_ref, nw_ref, ys_ref, y_ref, sp_ref, st_ref):
        @pl.when(pl.program_id(2) == 0)
        def _():
            st_ref[...] = jnp.zeros_like(st_ref)

        for ci in range(STEP_CHUNKS):
            chunk(ci, xbc_ref, dt_ref, tb_ref, z_ref, al_ref, ds_ref, nw_ref, ys_ref, y_ref, sp_ref, st_ref)

    def chunk(ci, xbc_ref, dt_ref, tb_ref, z_ref, al_ref, ds_ref, nw_ref, ys_ref, y_ref, sp_ref, st_ref):
        rows = slice(ci * CHUNK, (ci + 1) * CHUNK)
        dt, _ = _step_sizes(dt_ref[rows, :], tb_ref)
        _, acs, acs_t, causal, _, _, dtx, acsx, lastx, dskx = _ssd_chunk_setup(dt, al_ref, ds_ref)
        bmat = xbc_ref[rows, GROUP_CH:GROUP_CH + D_STATE].astype(BF16)
        cmat = xbc_ref[rows, GROUP_CH + D_STATE:].astype(BF16)
        cb = lax.dot_general(cmat, bmat, (((1,), (1,)), ((), ())), preferred_element_type=F32)
        x = xbc_ref[rows, :GROUP_CH]
        xdt = x * dtx
        xdt16 = xdt.astype(BF16)
        first_head = lax.broadcasted_iota(jnp.int32, (CHUNK, LANE), 1) < HEAD_DIM
        pairs = []
        for hp in range(GROUP_SSM_HEADS // 2):
            xp = xdt16[:, hp * LANE:(hp + 1) * LANE]
            two = []
            for j in (2 * hp, 2 * hp + 1):
                lmat = jnp.exp(jnp.where(causal, acs[:, j:j + 1] - acs_t[j:j + 1, :], -jnp.inf))
                two.append(jnp.dot((cb * lmat).astype(BF16), xp, preferred_element_type=F32))
            pairs.append(jnp.where(first_head, two[0], two[1]))
        yd = jnp.concatenate(pairs, axis=1)
        s_prev = st_ref[...]
        s16 = s_prev.astype(BF16)
        sp_ref[ci] = s16
        yo = jnp.dot(cmat, s16, preferred_element_type=F32) * jnp.exp(acsx)
        sts = lax.dot_general(bmat, (xdt * jnp.exp(lastx - acsx)).astype(BF16), tn, preferred_element_type=F32)
        st_ref[...] = s_prev * jnp.exp(lastx) + sts
        y = yd + yo + dskx * x
        zz = z_ref[rows, :]
        u = y * (zz * _sigmoid(zz))
        rn = lax.rsqrt(jnp.mean(u * u, -1, keepdims=True) + RMS_EPS)
        ys_ref[rows, :] = (u * rn * nw_ref[...]).astype(BF16)
        y_ref[rows, :] = y

    return _pcall(
        body, name="ssd_fwd", grid=(SSM_GROUPS, nb, steps),
        in_specs=[xbc_spec, raw, tbspec, wide, grow, grow, nwspec],
        out_specs=[wide, wide, prev],
        out_shape=[jax.ShapeDtypeStruct((nb, seq, D_INNER), BF16), jax.ShapeDtypeStruct((nb, seq, D_INNER), F32),
                   jax.ShapeDtypeStruct((nb, nc, SSM_GROUPS, D_STATE, hw), BF16)],
        scratch_shapes=[pltpu.VMEM((D_STATE, hw), F32)],
        compiler_params=_params("parallel", "parallel", "arbitrary"),
    )(xbc, dt_raw, dt_bias_row, z, alog_g, dskip_g, normw)


def _ssd_bwd(xbc, dt_raw, dt_bias_row, z, y, dys, sprev, alog_g, dskip_g, normw):
    nb, seq, _ = xbc.shape
    nc, steps, hw, mk, grow, nwspec, tbspec = _ssd_specs(nb, seq)
    wide, xbc_spec, lanes, prev, raw = mk(True)
    nt = (((1,), (1,)), ((), ()))
    tn = (((0,), (0,)), ((), ()))

    def body(xbc_ref, dt_ref, tb_ref, z_ref, y_ref, dys_ref, sp_ref, al_ref, ds_ref, nw_ref,
             dxbc_ref, ddt_ref, dz_ref, small_ref, dnw_ref, g_ref):
        b, c = pl.program_id(1), pl.program_id(2)

        @pl.when((b == 0) & (c == 0))
        def _():
            small_ref[...] = jnp.zeros_like(small_ref)
            dnw_ref[...] = jnp.zeros_like(dnw_ref)

        @pl.when(c == 0)
        def _():
            g_ref[...] = jnp.zeros_like(g_ref)

        for ci in reversed(range(STEP_CHUNKS)):
            chunk(ci, xbc_ref, dt_ref, tb_ref, z_ref, y_ref, dys_ref, sp_ref, al_ref, ds_ref, nw_ref,
                  dxbc_ref, ddt_ref, dz_ref, small_ref, dnw_ref, g_ref)

    def chunk(ci, xbc_ref, dt_ref, tb_ref, z_ref, y_ref, dys_ref, sp_ref, al_ref, ds_ref, nw_ref,
              dxbc_ref, ddt_ref, dz_ref, small_ref, dnw_ref, g_ref):
        rows = slice(ci * CHUNK, (ci + 1) * CHUNK)
        yv, zz, dys_v, nw = y_ref[rows, :], z_ref[rows, :], dys_ref[rows, :], nw_ref[...]
        sz = _sigmoid(zz)
        silu = zz * sz
        u = yv * silu
        rn = lax.rsqrt(jnp.mean(u * u, -1, keepdims=True) + RMS_EPS)
        gn = dys_v * nw
        du = rn * gn - u * (rn * rn * rn) * jnp.mean(u * gn, -1, keepdims=True)
        dnw_ref[...] += jnp.sum(dys_v * u * rn, 0, keepdims=True)
        dy = du * silu
        dz_ref[rows, :] = du * yv * (sz * (1.0 + zz * (1.0 - sz)))

        dt, sg = _step_sizes(dt_ref[rows, :], tb_ref)
        arow, acs, acs_t, causal, triu, e, dtx, acsx, lastx, dskx = _ssd_chunk_setup(dt, al_ref, ds_ref)
        dfsx = jnp.exp(acsx)
        dtex = jnp.exp(lastx - acsx)
        bmat = xbc_ref[rows, GROUP_CH:GROUP_CH + D_STATE].astype(BF16)
        cmat = xbc_ref[rows, GROUP_CH + D_STATE:].astype(BF16)
        cb = lax.dot_general(cmat, bmat, nt, preferred_element_type=F32)
        x = xbc_ref[rows, :GROUP_CH]
        xdt = x * dtx
        xdt16 = xdt.astype(BF16)
        xdte = xdt * dtex
        dy16 = dy.astype(BF16)
        dyd = dy * dfsx
        dyd16 = dyd.astype(BF16)
        s16 = sp_ref[ci]
        g = g_ref[...]
        g16 = g.astype(BF16)
        cs = jnp.dot(cmat, s16, preferred_element_type=F32)
        dc_off = lax.dot_general(dyd16, s16, nt, preferred_element_type=F32)
        g_here = lax.dot_general(cmat, dyd16, tn, preferred_element_type=F32)
        bg = jnp.dot(bmat, g16, preferred_element_type=F32)
        db_st = lax.dot_general(xdte.astype(BF16), g16, nt, preferred_element_type=F32)
        ddte_w = bg * xdte
        dcd = _to_heads(_row8(jnp.sum(g * s16.astype(F32), 0, keepdims=True)), e)[0:1, :]
        lane = lax.broadcasted_iota(jnp.int32, (CHUNK, LANE), 1)
        first_head = lane < HEAD_DIM
        sub = lax.broadcasted_iota(jnp.int32, (CHUNK, LANE), 0)
        dacs = jnp.zeros((CHUNK, LANE), F32)
        colsums = jnp.zeros((CHUNK, LANE), F32)
        dcb = jnp.zeros((CHUNK, CHUNK), F32)
        pairs = []
        for hp in range(GROUP_SSM_HEADS // 2):
            xp = xdt16[:, hp * LANE:(hp + 1) * LANE]
            dyp = dy16[:, hp * LANE:(hp + 1) * LANE]
            two = []
            for idx, j in enumerate((2 * hp, 2 * hp + 1)):
                lmat = jnp.exp(jnp.where(causal, acs[:, j:j + 1] - acs_t[j:j + 1, :], -jnp.inf))
                mf = cb * lmat
                dy_h = jnp.where(first_head if idx == 0 else jnp.logical_not(first_head), dyp, jnp.zeros_like(dyp))
                dm = lax.dot_general(dy_h, xp, nt, preferred_element_type=F32)
                two.append(lax.dot_general(mf.astype(BF16), dyp, tn, preferred_element_type=F32))
                wmat = dm * mf
                dcb = dcb + dm * lmat
                dacs = jnp.where(lane == j, jnp.sum(wmat, -1, keepdims=True), dacs)
                colsums = jnp.where(sub == j, jnp.sum(wmat, 0, keepdims=True), colsums)
            pairs.append(jnp.where(first_head, two[0], two[1]))
        dxdt = bg * dtex + jnp.concatenate(pairs, axis=1)
        dacs = dacs - colsums.T + _to_heads(dyd * cs - ddte_w, e)
        cd_row = jnp.exp(acs[CHUNK - 1:CHUNK, :])
        tail = _to_heads(_row8(jnp.sum(ddte_w, 0, keepdims=True)), e)[0:1, :] + dcd * cd_row
        dacs = dacs + jnp.where(sub == CHUNK - 1, tail, 0.0)
        d_hi, d_mid, d_lo = _split3(dacs)
        up = lambda t: jnp.dot(triu, t, preferred_element_type=F32)
        da = (up(d_hi) + up(d_mid)) + up(d_lo)
        ddt_raw = (da * arow + _to_heads(dxdt * x, e)) * sg
        ddt_ref[rows, :] = ddt_raw
        small_ref[0:1, :] += jnp.sum(da * dt, 0, keepdims=True) * arow
        small_ref[1:2, :] += _to_heads(_row8(jnp.sum(dy * x, 0, keepdims=True)), e)[0:1, :]
        small_ref[2:3, :] += jnp.sum(ddt_raw, 0, keepdims=True)
        dcb16 = dcb.astype(BF16)
        dxbc_ref[rows, GROUP_CH + D_STATE:] = dc_off + jnp.dot(dcb16, bmat, preferred_element_type=F32)
        dxbc_ref[rows, GROUP_CH:GROUP_CH + D_STATE] = db_st + lax.dot_general(dcb16, cmat, tn,
                                                                               preferred_element_type=F32)
        dxbc_ref[rows, :GROUP_CH] = dxdt * dtx + dskx * dy
        g_ref[...] = g * jnp.exp(lastx) + g_here

    return _pcall(
        body, name="ssd_bwd", grid=(SSM_GROUPS, nb, steps),
        in_specs=[xbc_spec, raw, tbspec, wide, wide, wide, prev, grow, grow, nwspec],
        out_specs=[xbc_spec, lanes, wide,
                   pl.BlockSpec((None, 8, LANE), lambda g, b, c: (g, 0, 0)), nwspec],
        out_shape=[jax.ShapeDtypeStruct((nb, seq, CONV_DIM), F32),
                   jax.ShapeDtypeStruct((SSM_GROUPS, nb, seq, LANE), F32),
                   jax.ShapeDtypeStruct((nb, seq, D_INNER), F32),
                   jax.ShapeDtypeStruct((SSM_GROUPS, 8, LANE), F32),
                   jax.ShapeDtypeStruct((1, D_INNER), F32)],
        scratch_shapes=[pltpu.VMEM((D_STATE, hw), F32)],
        compiler_params=_params("parallel", "arbitrary", "arbitrary"),
    )(xbc, dt_raw, dt_bias_row, z, y, dys, sprev, alog_g, dskip_g, normw)


EW_TM = 256


def _merge_fwd(oa16, y_ssm16, w_bra, w_brb, gm, bgate):
    nb, seq, _ = oa16.shape

    def body(oa_ref, ys_ref, wa_ref, wb_ref, ga_ref, gb_ref, bg_ref, a_ref, b_ref, o_ref):
        y_a = jnp.dot(oa_ref[...], wa_ref[...], preferred_element_type=F32)
        y_b = jnp.dot(ys_ref[...], wb_ref[...], preferred_element_type=F32)
        a_ref[...] = y_a
        b_ref[...] = y_b
        sa = _sigmoid(ga_ref[...] + bg_ref[0:1, :])
        sb = _sigmoid(gb_ref[...] + bg_ref[1:2, :])
        o_ref[...] = (sa * y_a + sb * y_b).astype(BF16)

    spec = pl.BlockSpec((None, EW_TM, D_MODEL), lambda b, i: (b, i, 0))
    spec1 = pl.BlockSpec((None, EW_TM, D_MODEL), lambda b, i: (b, i, 1))
    return _pcall(
        body, name="merge_fwd", grid=(nb, seq // EW_TM),
        in_specs=[_tok_spec(EW_TM, ATT_OUT), _tok_spec(EW_TM, D_INNER), _whole(w_bra, True), _whole(w_brb, True),
                  spec, spec1, pl.BlockSpec((8, D_MODEL), lambda b, i: (0, 0))],
        out_specs=[spec] * 3,
        out_shape=[jax.ShapeDtypeStruct((nb, seq, D_MODEL), F32)] * 2 + [jax.ShapeDtypeStruct((nb, seq, D_MODEL), BF16)],
        compiler_params=_params("parallel", "parallel"),
    )(oa16, y_ssm16, w_bra, w_brb, gm, gm, bgate)


def _merge_bwd(dpre16, w_out16, y_a, y_b, gm, bgate):
    nb, seq, _ = y_a.shape

    def body(dp_ref, w_ref, a_ref, b_ref, ga_ref, gb_ref, bg_ref, dya_ref, dyb_ref, dg_ref, s_ref):
        @pl.when((pl.program_id(0) == 0) & (pl.program_id(1) == 0))
        def _():
            s_ref[...] = jnp.zeros_like(s_ref)

        dm = lax.dot_general(dp_ref[...], w_ref[...], NT, preferred_element_type=F32)
        sa = _sigmoid(ga_ref[...] + bg_ref[0:1, :])
        sb = _sigmoid(gb_ref[...] + bg_ref[1:2, :])
        dya_ref[...] = (dm * sa).astype(BF16)
        dyb_ref[...] = (dm * sb).astype(BF16)
        dga = dm * a_ref[...] * (sa * (1.0 - sa))
        dgb = dm * b_ref[...] * (sb * (1.0 - sb))
        dg_ref[:, :D_MODEL] = dga.astype(BF16)
        dg_ref[:, D_MODEL:] = dgb.astype(BF16)
        s_ref[0:1, :] += jnp.sum(dga, 0, keepdims=True)
        s_ref[1:2, :] += jnp.sum(dgb, 0, keepdims=True)

    spec = pl.BlockSpec((None, EW_TM, D_MODEL), lambda b, i: (b, i, 0))
    spec1 = pl.BlockSpec((None, EW_TM, D_MODEL), lambda b, i: (b, i, 1))
    small = pl.BlockSpec((8, D_MODEL), lambda b, i: (0, 0))
    return _pcall(
        body, name="merge_bwd", grid=(nb, seq // EW_TM),
        in_specs=[spec, _whole(w_out16, True), spec, spec, spec, spec1, small],
        out_specs=[spec, spec, pl.BlockSpec((None, EW_TM, 2 * D_MODEL), lambda b, i: (b, i, 0)), small],
        out_shape=[jax.ShapeDtypeStruct((nb, seq, D_MODEL), BF16), jax.ShapeDtypeStruct((nb, seq, D_MODEL), BF16),
                   jax.ShapeDtypeStruct((nb, seq, 2 * D_MODEL), BF16), jax.ShapeDtypeStruct((8, D_MODEL), F32)],
        compiler_params=_params("arbitrary", "arbitrary"),
    )(dpre16, w_out16, y_a, y_b, gm, gm, bgate)


def _ln_loss(x, merged16, w_out16, gp, p16, w_ple16, target, bgate, ln_g, ln_b):
    nb, seq, _ = x.shape

    def body(x_ref, m_ref, wo_ref, gp_ref, p_ref, wp_ref, t_ref, bg_ref, g_ref, b_ref,
             dx_ref, dp_ref, dpw_ref, dgp_ref, s_ref):
        @pl.when((pl.program_id(0) == 0) & (pl.program_id(1) == 0))
        def _():
            s_ref[...] = jnp.zeros_like(s_ref)

        sp = _sigmoid(gp_ref[...] + bg_ref[2:3, :])
        pw = jnp.dot(p_ref[...], wp_ref[...], preferred_element_type=F32)
        mix = jnp.dot(m_ref[...], wo_ref[...], preferred_element_type=F32)
        pre = ALPHA * x_ref[...] + mix + sp * pw
        mu = jnp.mean(pre, -1, keepdims=True)
        cen = pre - mu
        rstd = lax.rsqrt(jnp.mean(cen * cen, -1, keepdims=True) + LN_EPS)
        xhat = cen * rstd
        err = xhat * g_ref[...] + b_ref[...] - t_ref[...]
        dy = err * (1.0 / D_MODEL)
        dxh = dy * g_ref[...]
        dpre = rstd * (dxh - jnp.mean(dxh, -1, keepdims=True) - xhat * jnp.mean(dxh * xhat, -1, keepdims=True))
        dx_ref[...] = ALPHA * dpre
        dp_ref[...] = dpre.astype(BF16)
        dpw_ref[...] = (dpre * sp).astype(BF16)
        dgp = dpre * pw * (sp * (1.0 - sp))
        dgp_ref[...] = dgp.astype(BF16)
        s_ref[0:1, :] += jnp.sum(dy * xhat, 0, keepdims=True)
        s_ref[1:2, :] += jnp.sum(dy, 0, keepdims=True)
        s_ref[2:3, :] += jnp.sum(dgp, 0, keepdims=True)
        s_ref[3:4, :] += jnp.sum(err * err, 0, keepdims=True)

    spec = pl.BlockSpec((None, EW_TM, D_MODEL), lambda b, i: (b, i, 0))
    small = pl.BlockSpec((8, D_MODEL), lambda b, i: (0, 0))
    row = pl.BlockSpec((1, D_MODEL), lambda b, i: (0, 0))
    return _pcall(
        body, name="ln_loss", grid=(nb, seq // EW_TM),
        in_specs=[spec, spec, _whole(w_out16, True), spec, pl.BlockSpec((None, EW_TM, PLE_DIM), lambda b, i: (b, i, 0)),
                  _whole(w_ple16, True), spec, small, row, row],
        out_specs=[spec] * 4 + [small],
        out_shape=[jax.ShapeDtypeStruct((nb, seq, D_MODEL), F32)] + [jax.ShapeDtypeStruct((nb, seq, D_MODEL), BF16)] * 3
        + [jax.ShapeDtypeStruct((8, D_MODEL), F32)],
        compiler_params=_params("arbitrary", "arbitrary"),
    )(x, merged16, w_out16, gp, p16, w_ple16, target, bgate, ln_g, ln_b)


def _adamw_update(w_ref, g_ref, m_ref, v_ref, d_ref, nm_ref, nv_ref):
    c1 = 1.0 - ADAM_B1 ** ADAM_STEP
    c2 = 1.0 - ADAM_B2 ** ADAM_STEP
    gv = g_ref[...]
    nm = ADAM_B1 * m_ref[...] + (1.0 - ADAM_B1) * gv
    nv = ADAM_B2 * v_ref[...] + (1.0 - ADAM_B2) * (gv * gv)
    d_ref[...] = -ADAM_LR * ((nm / c1) / (jnp.sqrt(nv / c2) + ADAM_EPS) + ADAM_WD * w_ref[...])
    nm_ref[...] = nm
    nv_ref[...] = nv


def _adamw(w, g, m, v, name):
    rows, cols = w.shape
    tr = _row_tile(rows, cols, 8, 5 << 19)

    def body(*refs):
        _adamw_update(*refs)

    spec = pl.BlockSpec((tr, cols), lambda i: (i, 0))
    return _pcall(
        body, name=name, grid=(rows // tr,), in_specs=[spec] * 4, out_specs=[spec] * 3,
        out_shape=[jax.ShapeDtypeStruct(w.shape, F32)] * 3, compiler_params=_params("parallel"),
    )(w, g, m, v)


def _adamw_small(ws, gs, ms, vs, name):
    n = len(ws)

    def body(*refs):
        for i in range(n):
            _adamw_update(*[refs[k * n + i] for k in range(7)])

    outs = _pcall(body, name=name, out_shape=[jax.ShapeDtypeStruct(w.shape, F32) for w in ws] * 3,
                  compiler_params=_params())(*ws, *gs, *ms, *vs)
    return outs[:n], outs[n:2 * n], outs[2 * n:]


def _sum_rows(parts, out_dtype, name):
    rows, cols = parts[0].shape
    tr = rows
    for cand in range(16, rows, 16):
        if rows % cand == 0 and cand * cols * 4 <= (1 << 20):
            tr = cand
    n = len(parts)

    def body(*refs):
        acc = refs[0][...].astype(F32)
        for r in refs[1:n]:
            acc = acc + r[...].astype(F32)
        refs[n][...] = acc.astype(out_dtype)

    spec = pl.BlockSpec((tr, cols), lambda i: (i, 0))
    return _pcall(
        body, name=name, grid=(rows // tr,), in_specs=[spec] * n, out_specs=spec,
        out_shape=jax.ShapeDtypeStruct((rows, cols), out_dtype), compiler_params=_params("parallel"),
    )(*parts)


def _place():
    return lax.axis_index("x"), lax.axis_index("y"), lax.axis_index("c")


def _other_chips(x, y):
    return [(1 - x, y), (x, 1 - y), (1 - x, 1 - y)]


def _remote(src, dst, send_sem, recv_sem, to):
    return pltpu.make_async_remote_copy(src_ref=src, dst_ref=dst, send_sem=send_sem, recv_sem=recv_sem,
                                        device_id=to, device_id_type=MESH)


ANY = pl.BlockSpec(memory_space=pl.ANY)
D2D_CHUNK_BYTES = 512 * 1024
ICI_CHUNK_BYTES = 2 * 1024 * 1024


def _row_chunks(rows, row_bytes, chunk_bytes=D2D_CHUNK_BYTES):
    per = max(16, chunk_bytes // row_bytes // 16 * 16)
    return [(s, min(per, rows - s)) for s in range(0, rows, per)]


def _row_tile(rows, cols, align, limit=1 << 21):
    best = None
    for cand in range(align, rows + 1, align):
        if rows % cand == 0 and cand * cols * 4 <= limit:
            best = cand
    return best or rows


def _allgather_pieces(pieces):
    n = len(pieces)
    halves = [_row_chunks(p.shape[0] // 2, p.shape[1] * p.dtype.itemsize, ICI_CHUNK_BYTES) for p in pieces]
    entries = [(a, q, s, m, j) for a in range(n) for q, (s, m) in enumerate(halves[a]) for j in range(3)]
    slot = {(a, q, j): k for k, (a, q, _, _, j) in enumerate(entries)}
    n_ici = len(entries)

    def body(*refs):
        ins, outs = refs[:n], refs[n:2 * n]
        send_sems, recv_sems = refs[2 * n:]
        x, y, c = _place()
        me = 2 * x + y
        sibling = (x, y, 1 - c)
        chips = _other_chips(x, y)

        def landed(a, s, m, j, core):
            half = ins[a].shape[0] // 2
            return outs[a].at[2 * chips[j][0] + chips[j][1], pl.ds(core * half + s, m)]

        sent = []
        for k, (a, q, s, m, j) in enumerate(entries):
            if j < 2:
                half = ins[a].shape[0] // 2
                cp = _remote(ins[a].at[pl.ds(c * half + s, m)], outs[a].at[me, pl.ds(c * half + s, m)],
                             send_sems.at[k], recv_sems.at[k], (*chips[j], c))
                cp.start()
                sent.append(cp)

        def pass_to_sibling(k, blk):
            fw = _remote(blk, blk, send_sems.at[n_ici + k], recv_sems.at[n_ici + k], sibling)
            fw.start()
            sent.append(fw)

        for k, (a, q, s, m, j) in enumerate(entries):
            if j < 2:
                blk = landed(a, s, m, j, c)
                _remote(blk, blk, send_sems.at[k], recv_sems.at[k], (*chips[j], c)).wait_recv()
                first = q < (len(halves[a]) + 1) // 2
                if (j == 0) == first:
                    on = slot[(a, q, 2)]
                    rl = _remote(blk, blk, send_sems.at[on], recv_sems.at[on], (*chips[1 - j], c))
                    rl.start()
                    sent.append(rl)
                pass_to_sibling(k, blk)
        for k, (a, q, s, m, j) in enumerate(entries):
            if j == 2:
                blk = landed(a, s, m, j, c)
                _remote(blk, blk, send_sems.at[k], recv_sems.at[k], (*chips[j], c)).wait_recv()
                pass_to_sibling(k, blk)
        for k, (a, q, s, m, j) in enumerate(entries):
            blk = landed(a, s, m, j, 1 - c)
            _remote(blk, blk, send_sems.at[n_ici + k], recv_sems.at[n_ici + k], sibling).wait_recv()
        for cp in sent:
            cp.wait_send()

    gathered = _pcall(
        body, name="allgather_weights", in_specs=[ANY] * n, out_specs=[ANY] * n,
        out_shape=[jax.ShapeDtypeStruct((4,) + p.shape, p.dtype) for p in pieces],
        scratch_shapes=[pltpu.SemaphoreType.DMA((2 * n_ici,)), pltpu.SemaphoreType.DMA((2 * n_ici,))],
        compiler_params=pltpu.CompilerParams(has_side_effects=True),
    )(*pieces)
    x, y, _ = _place()
    return [lax.dynamic_update_slice(g, p[None], (2 * x + y, 0, 0)) for g, p in zip(gathered, pieces)]


def _sibling_exchange(grads):
    n = len(grads)
    chunks = [_row_chunks(g.shape[1] // 2, g.shape[2] * g.dtype.itemsize) for g in grads]
    n_sem = 4 * sum(len(ch) for ch in chunks)

    def body(*refs):
        ins, gots = refs[:n], refs[n:2 * n]
        send_sems, recv_sems = refs[2 * n:]
        x, y, c = _place()
        sibling = (x, y, 1 - c)
        work = []
        for a in range(n):
            half = ins[a].shape[1] // 2
            for piece in range(4):
                for s, m in chunks[a]:
                    k = len(work)
                    cp = _remote(ins[a].at[piece, pl.ds((1 - c) * half + s, m)], gots[a].at[piece, pl.ds(s, m)],
                                 send_sems.at[k], recv_sems.at[k], sibling)
                    cp.start()
                    work.append(cp)
        for cp in work:
            cp.wait()

    return _pcall(
        body, name="grad_sibling_exchange", in_specs=[ANY] * n, out_specs=[ANY] * n,
        out_shape=[jax.ShapeDtypeStruct((4, g.shape[1] // 2, g.shape[2]), g.dtype) for g in grads],
        scratch_shapes=[pltpu.SemaphoreType.DMA((n_sem,)), pltpu.SemaphoreType.DMA((n_sem,))],
        compiler_params=pltpu.CompilerParams(has_side_effects=True),
    )(*grads)


def _sibling_gather(fulls):
    n = len(fulls)
    chunks = [_row_chunks(f.shape[0] // 2, f.shape[1] * f.dtype.itemsize) for f in fulls]
    n_sem = sum(len(ch) for ch in chunks)

    def body(*refs):
        outs = refs[n:2 * n]
        send_sems, recv_sems = refs[2 * n:]
        x, y, c = _place()
        sibling = (x, y, 1 - c)
        work = []
        for a in range(n):
            h = outs[a].shape[0] // 2
            for s, m in chunks[a]:
                k = len(work)
                mine = outs[a].at[pl.ds(c * h + s, m)]
                cp = _remote(mine, mine, send_sems.at[k], recv_sems.at[k], sibling)
                cp.start()
                work.append((a, s, m, cp))
        for k, (a, s, m, cp) in enumerate(work):
            h = outs[a].shape[0] // 2
            cp.wait_send()
            theirs = outs[a].at[pl.ds((1 - c) * h + s, m)]
            _remote(theirs, theirs, send_sems.at[k], recv_sems.at[k], sibling).wait_recv()

    return _pcall(
        body, name="grad_sibling_gather", in_specs=[ANY] * n, out_specs=[ANY] * n,
        out_shape=[jax.ShapeDtypeStruct(f.shape, f.dtype) for f in fulls],
        input_output_aliases={a: a for a in range(n)},
        scratch_shapes=[pltpu.SemaphoreType.DMA((n_sem,)), pltpu.SemaphoreType.DMA((n_sem,))],
        compiler_params=pltpu.CompilerParams(has_side_effects=True),
    )(*fulls)


def _pair_sum(grad, got, place, name):
    _, rows, cols = grad.shape
    half = rows // 2
    tr = _row_tile(half, cols, 16)

    def body(p_ref, a_ref, b_ref, o_ref):
        o_ref[...] = (a_ref[...].astype(F32) + b_ref[...].astype(F32)).astype(BF16)

    return _pcall(
        body, name=name,
        grid_spec=pltpu.PrefetchScalarGridSpec(
            num_scalar_prefetch=1, grid=(4, half // tr),
            in_specs=[pl.BlockSpec((None, tr, cols), lambda k, i, p: (k, p[1] * (half // tr) + i, 0)),
                      pl.BlockSpec((None, tr, cols), lambda k, i, p: (k, i, 0))],
            out_specs=pl.BlockSpec((None, tr, cols), lambda k, i, p: (k, i, 0))),
        out_shape=jax.ShapeDtypeStruct((4, half, cols), BF16),
        compiler_params=_params("parallel", "parallel"),
    )(place, grad, got)


def _chip_sum(sums, got, place, name):
    _, h, cols = sums.shape
    tr = _row_tile(h, cols, 16)

    def body(p_ref, own_ref, g0, g1, g2, o_ref):
        o_ref[...] = ((own_ref[...].astype(F32) + g0[...].astype(F32)) + g1[...].astype(F32)) + g2[...].astype(F32)

    gspec = lambda j: pl.BlockSpec((None, tr, cols), lambda i, p: (j, i, 0))
    return _pcall(
        body, name=name,
        grid_spec=pltpu.PrefetchScalarGridSpec(
            num_scalar_prefetch=1, grid=(h // tr,),
            in_specs=[pl.BlockSpec((None, tr, cols), lambda i, p: (p[0], i, 0)), gspec(0), gspec(1), gspec(2)],
            out_specs=pl.BlockSpec((tr, cols), lambda i, p: (p[1] * (h // tr) + i, 0))),
        out_shape=jax.ShapeDtypeStruct((2 * h, cols), F32),
        compiler_params=_params("parallel"),
    )(place, sums, got, got, got)


def _allgather8(buf, name):
    rows = buf.shape[0]

    def body(in_ref, out_ref, send_sems, recv_sems):
        x, y, c = _place()
        me = 4 * x + 2 * y + c
        out_ref[me] = in_ref[...]
        work = []
        for rel in range(1, 8):
            fx, fy, fc = (rel >> 2) & 1, (rel >> 1) & 1, rel & 1
            to = (x ^ fx, y ^ fy, c ^ fc)
            cp = _remote(in_ref, out_ref.at[me], send_sems.at[rel - 1], recv_sems.at[rel - 1], to)
            cp.start()
            work.append((cp, 4 * to[0] + 2 * to[1] + to[2]))
        for rel, (cp, frm) in enumerate(work):
            cp.wait_send()
            blk = out_ref.at[frm]
            _remote(blk, blk, send_sems.at[rel], recv_sems.at[rel], (x, y, c)).wait_recv()

    return _pcall(
        body, name=name, in_specs=[pl.BlockSpec(memory_space=pltpu.VMEM)],
        out_specs=pl.BlockSpec(memory_space=pltpu.VMEM),
        out_shape=jax.ShapeDtypeStruct((8, rows, LANE), F32),
        scratch_shapes=[pltpu.SemaphoreType.DMA((7,)), pltpu.SemaphoreType.DMA((7,))],
        compiler_params=pltpu.CompilerParams(has_side_effects=True),
    )(buf)


def _pack_rows(arrs):
    flats = [a.reshape(-1).astype(F32) for a in arrs]
    starts = np.cumsum([0] + [-(-f.shape[0] // LANE) * LANE for f in flats])
    total = -(-int(starts[-1]) // (8 * LANE)) * 8 * LANE
    flat = sum(jnp.pad(f, (int(s), total - int(s) - f.shape[0])) for f, s in zip(flats, starts))
    return flat.reshape(total // LANE, LANE)


def _unpack_rows(buf, shapes):
    flat = buf.reshape(-1)
    outs, off = [], 0
    for s in shapes:
        n = int(np.prod(s))
        outs.append(flat[off:off + n].reshape(s))
        off += -(-n // LANE) * LANE
    return outs


def _local_grads(x, p, target, wseg, w_br16, w_out16, w_ple16, b_gate, conv_w, conv_b, dt_bias, a_log, d_skip,
                 ssm_norm_w, ln_g, ln_b, rel_bias, finish_dx):
    nb, seq, _ = x.shape
    bmaps = jnp.asarray(_bucket_maps())
    bias = _bias_tables(rel_bias, bmaps)
    bgate8 = jnp.pad(b_gate, ((0, 5), (0, 0)))
    dils = [d for _, d in PATTERNS]

    x16p = _token_orders(x, dils[1:])
    x16 = x16p[0]
    p16 = p.astype(BF16)
    qkv = [_proj(x16p[g], [wseg["qkv%d" % g]], BF16, "proj_qkv%d" % g, True, 2 * MM_TM)[0].reshape(
        nb, dils[g], seq // dils[g], -1) for g in range(3)]
    nat = {}
    for gi, (group, tm) in enumerate(NAT_GROUPS):
        outs = _proj(x16, [wseg[s] for s in group], F32, "proj_nat%d" % gi, True, tm)
        nat.update(zip(group, outs))
    att = [_attn_fwd(qkv[g], bias, g, dils[g], "attn_fwd%d" % g) for g in range(3)]
    oa, o_att, lse = _combine_fwd(att[0][0], att[0][1], att[1:], nat["gatt"])

    conv_wg, conv_bg = _xbc_group_order(conv_w), _xbc_group_order(conv_b)
    act = _conv_fwd(nat["xbc"], conv_wg, conv_bg, "conv_fwd")
    dt_bias_row = jnp.pad(dt_bias, ((0, 0), (0, LANE - SSM_HEADS)))
    alog_g, dskip_g = _group_lanes(a_log), _group_lanes(d_skip)
    y_ssm, y_all, sprev = _ssd_fwd(act, nat["dt"], dt_bias_row, nat["z"], alog_g, dskip_g, ssm_norm_w)

    w_bra, w_brb = w_br16[:ATT_OUT], w_br16[ATT_OUT:]
    y_a, y_b, merged = _merge_fwd(oa, y_ssm, w_bra, w_brb, nat["gm"], bgate8)

    dx, dpre16, dpw16, dgp16, ln_sums = _ln_loss(x, merged, w_out16, nat["gp"], p16, w_ple16, target, bgate8,
                                                 ln_g, ln_b)
    loss_sum = (0.5 / D_MODEL) * jnp.sum(ln_sums[3])
    dya16, dyb16, dgm16, mg_sums = _merge_bwd(dpre16, w_out16, y_a, y_b, nat["gm"], bgate8)
    dys = _dx([dyb16], [w_brb], [], "dx_yssm")
    g_w_out, = _dw(merged, [dpre16], BF16, "dw_out")
    g_w_br = jnp.concatenate([_dw(oa, [dya16], BF16, "dw_bra")[0], _dw(y_ssm, [dyb16], BF16, "dw_brb")[0]], axis=0)
    g_w_ple, = _dw(p16, [dpw16], BF16, "dw_ple")

    do_att, dgatt16, own_order = _combine_bwd(dya16, w_bra, nat["gatt"], o_att, lse, dils[1:])
    dseg = {"gatt": dgatt16, "gm": dgm16, "gp": dgp16}
    dbias = []
    for g in range(3):
        cotangent = (do_att, o_att, lse) if g == 0 else (own_order[2 * g - 2], own_order[2 * g - 1])
        dqkv, db = _attn_bwd(qkv[g], bias, g, cotangent, dils[g],
                             "attn_bwd%d" % g)
        dseg["qkv%d" % g] = dqkv.reshape(nb, seq, -1)
        dbias.append(db)
    g_rel = _bias_grad(jnp.concatenate(dbias, axis=0), bmaps)[:, 0, :NUM_BUCKETS].T

    dact, ddtg, dz, ssd_small, g_normw = _ssd_bwd(
        act, nat["dt"], dt_bias_row, nat["z"], y_all, dys, sprev, alog_g, dskip_g, ssm_norm_w)
    dseg["z"] = dz
    dseg["dt"] = jnp.pad(_ungroup_lanes(ddtg), ((0, 0), (0, 0), (0, LANE - SSM_HEADS)))
    dpre, conv_sums = _conv_bwd_pre(dact, nat["xbc"], conv_wg, conv_bg, "conv_bwd")
    dseg["xbc"] = _conv_bwd_x(dpre, conv_wg, "conv_bwd_x")
    csum = _xbc_reference_order(conv_sums)

    dx_own = []
    dh_own = [(dseg["qkv%d" % g].reshape(nb, dils[g], seq // dils[g], -1), wseg["qkv%d" % g]) for g in (1, 2)]
    dwseg = {"qkv%d" % g: _dw(x16p[g], [dseg["qkv%d" % g]], BF16, "dw_qkv%d" % g, True)[0] for g in range(3)}
    for gi, group in enumerate(DW_GROUPS):
        dwseg.update(zip(group, _dw(x16, [dseg[s] for s in group], BF16, "dw_nat%d" % gi, True)))
    names = ["qkv0"] + [s for group, _ in NAT_GROUPS for s in group]
    dx = finish_dx([dseg[s] for s in names], [wseg[s] for s in names], [dx], dx_own, dh_own, dwseg, g_w_br, g_w_out,
                   g_w_ple)

    small = dict(
        b_gate=jnp.stack([mg_sums[0], mg_sums[1], ln_sums[2]]),
        conv_w=csum[0:4], conv_b=csum[4:5],
        dt_bias=_ungroup_lanes(ssd_small[:, 2:3, :]), a_log=_ungroup_lanes(ssd_small[:, 0:1, :]),
        d_skip=_ungroup_lanes(ssd_small[:, 1:2, :]), ssm_norm_w=g_normw,
        ln_g=ln_sums[0:1], ln_b=ln_sums[1:2], rel_bias=g_rel)
    return loss_sum, dx, small


DX_TM = 256
SMALL_ORDER = ("b_gate", "conv_w", "conv_b", "dt_bias", "a_log", "d_skip", "ssm_norm_w", "ln_g", "ln_b", "rel_bias")
SMALL_FULL_SHAPES = dict(b_gate=(3, 1024), conv_w=(4, 3072), conv_b=(1, 3072), dt_bias=(1, 32), a_log=(1, 32),
                         d_skip=(1, 32), ssm_norm_w=(1, 2048), ln_g=(1, 1024), ln_b=(1, 1024), rel_bias=(32, 36))


def kernel(x, p, w_in, b_gate, conv_w, conv_b, dt_bias, a_log, d_skip, ssm_norm_w, w_branch, w_out, w_ple, ln_g, ln_b, rel_bias, loss_target, m_w_in, m_b_gate, m_conv_w, m_conv_b, m_dt_bias, m_a_log, m_d_skip, m_ssm_norm_w, m_w_branch, m_w_out, m_w_ple, m_ln_g, m_ln_b, m_rel_bias, v_w_in, v_b_gate, v_conv_w, v_conv_b, v_dt_bias, v_a_log, v_d_skip, v_ssm_norm_w, v_w_branch, v_w_out, v_w_ple, v_ln_g, v_ln_b, v_rel_bias):
    cx, cy, cc = _place()
    chip = 2 * cx + cy
    dev = 4 * cx + 2 * cy + cc

    w_in_t = jnp.transpose(w_in[0])
    win16 = _shard_to_window(w_in_t, chip)
    g_win, g_br, g_out, g_ple = _allgather_pieces(
        [win16, w_branch[0].astype(BF16), w_out[0].astype(BF16), w_ple[0].astype(BF16)])
    wseg = _assemble(g_win)
    w_br16 = g_br.reshape(4 * 704, D_MODEL)
    w_out16 = g_out.reshape(D_MODEL, D_MODEL)
    w_ple16 = jnp.transpose(g_ple, (1, 0, 2)).reshape(PLE_DIM, D_MODEL)
    shards = _allgather8(_pack_rows([b_gate[0], conv_w[0]]), "allgather_small_params")
    per_chip = [_unpack_rows(shards[2 * k], [(3, 256), (4, 768)]) for k in range(4)]
    b_gate_full = _join_last([pc[0] for pc in per_chip])
    conv_w_full = _join_last([pc[1] for pc in per_chip])

    place = jnp.stack([chip, cc]).astype(jnp.int32)
    reduced = []

    def finish_dx(dhs, ws, accs, own_order_accs, own_order_dhs, dwseg, d_br, d_out, d_ple):
        grads = [_pack(dwseg), d_br.reshape(4, 704, D_MODEL), d_out.reshape(4, 256, D_MODEL),
                 jnp.transpose(d_ple.reshape(PLE_DIM, 4, 256), (1, 0, 2))]
        got = _sibling_exchange(grads)
        chip_sums = [_pair_sum(g, t, place, "grad_pair_sum_%d" % i) for i, (g, t) in enumerate(zip(grads, got))]
        dx, others = _dx(dhs, ws, accs, "dx_w_in_and_grad_chip_scatter", True, DX_TM, chip_sums, own_order_accs,
                         own_order_dhs)
        fulls = [_chip_sum(s, t, place, "grad_chip_sum_%d" % i) for i, (s, t) in enumerate(zip(chip_sums, others))]
        reduced.extend(_sibling_gather(fulls))
        return dx

    loss_sum, grad_x, small = _local_grads(
        x, p[0], loss_target, wseg, w_br16, w_out16, w_ple16, b_gate_full, conv_w_full, conv_b, dt_bias, a_log,
        d_skip, ssm_norm_w, ln_g, ln_b, rel_bias, finish_dx)
    big = reduced
    g_w_in = lax.optimization_barrier(_window_to_shard(big[0], chip))
    g_w_branch, g_w_out, g_w_ple = big[1], big[2], big[3]
    parts = _allgather8(_pack_rows([small[n] for n in SMALL_ORDER] + [loss_sum.reshape(1, 1)]),
                        "allgather_small_grads")
    small_sum = _sum_rows([parts[i] for i in range(8)], F32, "small_grad_sum")
    *reduced_small, loss = _unpack_rows(small_sum, [SMALL_FULL_SHAPES[n] for n in SMALL_ORDER] + [(1, 1)])
    loss = loss.reshape(())
    sg = dict(zip(SMALL_ORDER, reduced_small))
    sg["b_gate"] = lax.dynamic_slice_in_dim(sg["b_gate"], chip * 256, 256, axis=1)
    sg["conv_w"] = lax.dynamic_slice_in_dim(sg["conv_w"], chip * 768, 768, axis=1)
    del dev

    upd = {}
    upd["w_in"] = [jnp.transpose(t) for t in _adamw(w_in_t, g_w_in, jnp.transpose(m_w_in[0]),
                                                      jnp.transpose(v_w_in[0]), "adamw_w_in")]
    upd["w_branch"] = _adamw(w_branch[0], g_w_branch, m_w_branch[0], v_w_branch[0], "adamw_w_branch")
    upd["w_out"] = _adamw(w_out[0], g_w_out, m_w_out[0], v_w_out[0], "adamw_w_out")
    upd["w_ple"] = _adamw(w_ple[0], g_w_ple, m_w_ple[0], v_w_ple[0], "adamw_w_ple")
    small_w = dict(b_gate=b_gate, conv_w=conv_w, conv_b=conv_b, dt_bias=dt_bias, a_log=a_log, d_skip=d_skip,
                   ssm_norm_w=ssm_norm_w, ln_g=ln_g, ln_b=ln_b, rel_bias=rel_bias)
    small_m = dict(b_gate=m_b_gate, conv_w=m_conv_w, conv_b=m_conv_b, dt_bias=m_dt_bias, a_log=m_a_log,
                   d_skip=m_d_skip, ssm_norm_w=m_ssm_norm_w, ln_g=m_ln_g, ln_b=m_ln_b, rel_bias=m_rel_bias)
    small_v = dict(b_gate=v_b_gate, conv_w=v_conv_w, conv_b=v_conv_b, dt_bias=v_dt_bias, a_log=v_a_log,
                   d_skip=v_d_skip, ssm_norm_w=v_ssm_norm_w, ln_g=v_ln_g, ln_b=v_ln_b, rel_bias=v_rel_bias)
    for n in SMALL_ORDER:
        sg[n] = sg[n].reshape(small_w[n].shape)
    s_delta, s_m, s_v = _adamw_small(*[[t[n] for n in SMALL_ORDER] for t in (small_w, sg, small_m, small_v)],
                                     "adamw_small")
    for i, n in enumerate(SMALL_ORDER):
        upd[n] = (s_delta[i], s_m[i], s_v[i])

    order = ("w_in", "b_gate", "conv_w", "conv_b", "dt_bias", "a_log", "d_skip", "ssm_norm_w", "w_branch", "w_out",
             "w_ple", "ln_g", "ln_b", "rel_bias")
    grads = dict(sg, w_in=jnp.transpose(g_w_in)[None],w_branch=g_w_branch[None], w_out=g_w_out[None], w_ple=g_w_ple[None])
    lead = lambda n, t: t[None] if n in ("w_in", "w_branch", "w_out", "w_ple") else t
    return (loss, grad_x, *[grads[n] for n in order], *[lead(n, upd[n][0]) for n in order],
            *[lead(n, upd[n][1]) for n in order], *[lead(n, upd[n][2]) for n in order])
```

```python
import math

import numpy as np
import jax
import jax.numpy as jnp
from jax import lax
from jax.experimental import pallas as pl
from jax.experimental.pallas import tpu as pltpu

F32, BF16 = jnp.float32, jnp.bfloat16

D_MODEL = 1024
HEAD_DIM = 64
GROUP_HEADS = 12
ATT_OUT = GROUP_HEADS * HEAD_DIM
PATTERNS = ((128, 1), (512, 4), (2048, 16))
BAND = 128
NUM_BUCKETS = 32
MAX_DISTANCE = 2048
D_INNER = 2048
SSM_HEADS = 32
SSM_GROUPS = 4
GROUP_SSM_HEADS = SSM_HEADS // SSM_GROUPS
D_STATE = 128
CHUNK = 128
PLE_DIM = 256
ALPHA = 2.0 ** 0.25
LN_EPS = 1e-5
RMS_EPS = 1e-5
ADAM_LR, ADAM_B1, ADAM_B2, ADAM_EPS, ADAM_WD, ADAM_STEP = 0.001, 0.9, 0.999, 1e-08, 0.01, 10
NEG = -1e30

QKV_W = 3 * ATT_OUT
IN_COLS = 15904
SHARD_COLS = IN_COLS // 4
DT_COL = 12800
ROW_TILE = 16
WIN_ROWS = 4000


def _win_offset(k):
    return (k * SHARD_COLS) % ROW_TILE


def _win_start(k):
    return k * SHARD_COLS - _win_offset(k)

VMEM_LIMIT_BYTES = 56 * 1024 * 1024
LANE = 128
MESH = pl.DeviceIdType.MESH
NT = (((1,), (1,)), ((), ()))
TN = (((0,), (0,)), ((), ()))


def _pcall(body, **kw):
    return pl.pallas_call(body, **kw)


def _params(*sem):
    return pltpu.CompilerParams(dimension_semantics=sem, vmem_limit_bytes=VMEM_LIMIT_BYTES)


def _sigmoid(v):
    return jax.nn.sigmoid(v)


MM_TM = 512


def _tok_spec(tm, width):
    return pl.BlockSpec((None, tm, width), lambda b, i: (b, i, 0))


def _whole(arr, single_buffer=False):
    mode = dict(pipeline_mode=pl.Buffered(1)) if single_buffer else {}
    return pl.BlockSpec(arr.shape, lambda b, i: (0,) * arr.ndim, **mode)


def _proj(a3, ws, out_dtype, name, w_rows_are_outputs=False, tm=MM_TM):
    nb, seq, kdim = a3.shape
    nw = len(ws)
    widths = [w.shape[0] if w_rows_are_outputs else w.shape[1] for w in ws]

    def body(*refs):
        a = refs[0][...].astype(BF16)
        for w_ref, o_ref in zip(refs[1:1 + nw], refs[1 + nw:]):
            if w_rows_are_outputs:
                v = lax.dot_general(a, w_ref[...], NT, preferred_element_type=F32)
            else:
                v = jnp.dot(a, w_ref[...], preferred_element_type=F32)
            o_ref[...] = v.astype(out_dtype)

    return _pcall(
        body, name=name, grid=(nb, seq // tm),
        in_specs=[_tok_spec(tm, kdim)] + [_whole(w, True) for w in ws],
        out_specs=[_tok_spec(tm, n) for n in widths],
        out_shape=[jax.ShapeDtypeStruct((nb, seq, n), out_dtype) for n in widths],
        compiler_params=_params("parallel", "parallel"),
    )(a3, *ws)


def _dx(dhs, ws, accs, name, w_rows_are_outputs=False, tm=MM_TM, scatter=None, own_order_dhs=()):
    nb, seq, _ = dhs[0].shape
    nd, nacc, npd = len(dhs), len(accs), len(own_order_dhs)
    kout = ws[0].shape[1] if w_rows_are_outputs else ws[0].shape[0]
    sums = scatter or []
    ns = len(sums)
    chunks = [_row_chunks(s.shape[1], s.shape[2] * s.dtype.itemsize, ICI_CHUNK_BYTES) for s in sums]
    n_sem = 3 * sum(len(ch) for ch in chunks)
    grid = (nb, seq // tm)
    ntile = kout // LANE if npd else 0

    def body(*refs):
        n_own = 2 * nd + nacc
        n_in = n_own + 2 * npd
        sum_refs, o_ref, got_refs = refs[n_in:n_in + ns], refs[n_in + ns], refs[n_in + ns + 1:n_in + 2 * ns + 1]
        tile_refs = refs[n_in + 2 * ns + 1:n_in + 2 * ns + 1 + ntile]

        def copies():
            send_sems, recv_sems = refs[-2], refs[-1]
            x, y, c = _place()
            out = []
            for a in range(ns):
                for s, m in chunks[a]:
                    for j, (cx, cy) in enumerate(_other_chips(x, y)):
                        k = len(out)
                        out.append(_remote(sum_refs[a].at[2 * cx + cy, pl.ds(s, m)], got_refs[a].at[j, pl.ds(s, m)],
                                           send_sems.at[k], recv_sems.at[k], (cx, cy, c)))
            return out

        if ns:
            @pl.when((pl.program_id(0) == 0) & (pl.program_id(1) == 0))
            def _():
                for cp in copies():
                    cp.start()

        v = None
        for dh_ref, w_ref in zip(refs[:nd], refs[nd:2 * nd]):
            dh = dh_ref[...].astype(BF16)
            if w_rows_are_outputs:
                t = jnp.dot(dh, w_ref[...], preferred_element_type=F32)
            else:
                t = lax.dot_general(dh, w_ref[...], NT, preferred_element_type=F32)
            v = t if v is None else v + t
        for a_ref in refs[2 * nd:2 * nd + nacc]:
            v = v + a_ref[...]
        for q_ref, w_ref in zip(refs[n_own:n_own + npd], refs[n_own + npd:n_in]):
            d, per, n = q_ref.shape
            dh = q_ref[...].reshape(d * per, n).astype(BF16)
            if w_rows_are_outputs:
                t = jnp.dot(dh, w_ref[...], preferred_element_type=F32)
            else:
                t = lax.dot_general(dh, w_ref[...], NT, preferred_element_type=F32)
            v = v + _natural_value(t, d, tile_refs)
        o_ref[...] = v

        if ns:
            @pl.when((pl.program_id(0) == grid[0] - 1) & (pl.program_id(1) == grid[1] - 1))
            def _():
                for cp in copies():
                    cp.wait()

    out = _pcall(
        body, name=name, grid=grid,
        in_specs=[_tok_spec(tm, dh.shape[-1]) for dh in dhs] + [_whole(w, True) for w in ws]
        + [_tok_spec(tm, kout)] * nacc
        + [pl.BlockSpec((None, q.shape[1], tm // q.shape[1], q.shape[3]), lambda b, i: (b, 0, i, 0))
           for q, _ in own_order_dhs]
        + [_whole(w, True) for _, w in own_order_dhs]
        + [ANY] * ns,
        out_specs=[_tok_spec(tm, kout)] + [ANY] * ns,
        out_shape=[jax.ShapeDtypeStruct((nb, seq, kout), F32)]
        + [jax.ShapeDtypeStruct((3,) + s.shape[1:], s.dtype) for s in sums],
        input_output_aliases={2 * nd: 0} if nacc else {},
        scratch_shapes=[pltpu.VMEM((tm, LANE), F32)] * ntile
        + ([pltpu.SemaphoreType.DMA((n_sem,)), pltpu.SemaphoreType.DMA((n_sem,))] if ns else []),
        compiler_params=pltpu.CompilerParams(
            dimension_semantics=("arbitrary", "arbitrary") if ns else ("parallel", "parallel"),
            vmem_limit_bytes=VMEM_LIMIT_BYTES, has_side_effects=bool(ns)),
    )(*dhs, *ws, *accs, *[q for q, _ in own_order_dhs], *[w for _, w in own_order_dhs], *sums)
    return (out[0], list(out[1:])) if ns else out[0]


def _dw(a3, dhs, out_dtype, name, rows_are_outputs=False):
    nb, seq, kdim = a3.shape
    nd = len(dhs)
    grid = (nb, seq // MM_TM)
    shapes = [(dh.shape[-1], kdim) if rows_are_outputs else (kdim, dh.shape[-1]) for dh in dhs]

    def body(*refs):
        b, i = pl.program_id(0), pl.program_id(1)
        dh_refs, o_refs, acc_refs = refs[1:1 + nd], refs[1 + nd:1 + 2 * nd], refs[1 + 2 * nd:]

        @pl.when((b == 0) & (i == 0))
        def _():
            for acc_ref in acc_refs:
                acc_ref[...] = jnp.zeros_like(acc_ref)

        a = refs[0][...].astype(BF16)
        for dh_ref, acc_ref in zip(dh_refs, acc_refs):
            dh = dh_ref[...].astype(BF16)
            acc_ref[...] += lax.dot_general(*((dh, a) if rows_are_outputs else (a, dh)), TN,
                                            preferred_element_type=F32)

        @pl.when((b == grid[0] - 1) & (i == grid[1] - 1))
        def _():
            for o_ref, acc_ref in zip(o_refs, acc_refs):
                o_ref[...] = acc_ref[...].astype(out_dtype)

    return _pcall(
        body, name=name, grid=grid,
        in_specs=[_tok_spec(MM_TM, kdim)] + [_tok_spec(MM_TM, dh.shape[-1]) for dh in dhs],
        out_specs=[pl.BlockSpec(s, lambda b, i: (0, 0)) for s in shapes],
        out_shape=[jax.ShapeDtypeStruct(s, out_dtype) for s in shapes],
        scratch_shapes=[pltpu.VMEM(s, F32) for s in shapes],
        compiler_params=_params("arbitrary", "arbitrary"),
    )(a3, *dhs)


def _dw_stacked(groups, out_dtype, name):
    pairs = [pr for g in groups for pr in g]
    nb, seq, _ = pairs[0][0].shape
    npair, ng = len(pairs), len(groups)
    grid = (nb, seq // MM_TM)

    def body(*refs):
        b, i = pl.program_id(0), pl.program_id(1)
        o_refs, acc_refs = refs[2 * npair:2 * npair + ng], refs[2 * npair + ng:]

        @pl.when((b == 0) & (i == 0))
        def _():
            for acc_ref in acc_refs:
                acc_ref[...] = jnp.zeros_like(acc_ref)

        for j, acc_ref in enumerate(acc_refs):
            acc_ref[...] += lax.dot_general(refs[2 * j][...].astype(BF16), refs[2 * j + 1][...].astype(BF16), TN,
                                            preferred_element_type=F32)

        @pl.when((b == grid[0] - 1) & (i == grid[1] - 1))
        def _():
            accs = iter(acc_refs)
            for o_ref, g in zip(o_refs, groups):
                row = 0
                for a3, _ in g:
                    o_ref[row:row + a3.shape[-1], :] = next(accs)[...].astype(out_dtype)
                    row += a3.shape[-1]

    shapes = [(sum(a3.shape[-1] for a3, _ in g), g[0][1].shape[-1]) for g in groups]
    return _pcall(
        body, name=name, grid=grid,
        in_specs=[_tok_spec(MM_TM, t.shape[-1]) for pr in pairs for t in pr],
        out_specs=[pl.BlockSpec(s, lambda b, i: (0, 0)) for s in shapes],
        out_shape=[jax.ShapeDtypeStruct(s, out_dtype) for s in shapes],
        scratch_shapes=[pltpu.VMEM((a3.shape[-1], dh.shape[-1]), F32) for a3, dh in pairs],
        compiler_params=_params("arbitrary", "arbitrary"),
    )(*[t for pr in pairs for t in pr])


def _qkv_rows(g):
    return [(part * QKV_W + g * ATT_OUT + hp * LANE, LANE) for hp in range(ATT_OUT // LANE) for part in range(3)]


XBC_START = 3 * QKV_W + ATT_OUT + D_INNER
GROUP_CH = GROUP_SSM_HEADS * HEAD_DIM
XBC_GROUP = GROUP_CH + 2 * D_STATE
CONV_DIM = SSM_GROUPS * XBC_GROUP


def _xbc_ranges():
    out = []
    for g in range(SSM_GROUPS):
        out += [(g * GROUP_CH, GROUP_CH), (D_INNER + g * D_STATE, D_STATE),
                (D_INNER + SSM_GROUPS * D_STATE + g * D_STATE, D_STATE)]
    return out


def _join_last(parts):
    widths = [t.shape[-1] for t in parts]
    total, lead = sum(widths), [(0, 0)] * (parts[0].ndim - 1)
    starts = np.cumsum([0] + widths)
    return sum(jnp.pad(t, lead + [(int(s), total - int(s) - w)]) for t, s, w in zip(parts, starts, widths))


def _xbc_group_order(t):
    return _join_last([t[..., s:s + n] for s, n in _xbc_ranges()])


def _xbc_reference_order(t):
    g = lambda off, n: [t[..., k * XBC_GROUP + off:k * XBC_GROUP + off + n] for k in range(SSM_GROUPS)]
    return _join_last(g(0, GROUP_CH) + g(GROUP_CH, D_STATE) + g(GROUP_CH + D_STATE, D_STATE))


def _segments():
    one = lambda name, start, rows: (name, [(start, rows)], max(rows, LANE))
    return [("qkv%d" % g, _qkv_rows(g), QKV_W) for g in range(3)] + [
        one("gatt", 3 * QKV_W, ATT_OUT), one("z", 3 * QKV_W + ATT_OUT, D_INNER),
        ("xbc", [(XBC_START + s, n) for s, n in _xbc_ranges()], CONV_DIM), one("dt", DT_COL, SSM_HEADS),
        one("gm", DT_COL + SSM_HEADS, 2 * D_MODEL), one("gp", DT_COL + SSM_HEADS + 2 * D_MODEL, D_MODEL)]


LAYOUT_TC = 256
NAT_GROUPS = ((("gatt", "z", "dt", "gp"), 512), (("xbc", "gm"), 512))
DW_GROUPS = (("gatt", "z", "dt", "gp"), ("xbc",), ("gm",))


def _assemble(win):
    segs = _segments()

    def body(win_ref, *outs):
        def pieces(start, rows):
            t, end = start, start + rows
            while t < end:
                k = min(t // SHARD_COLS, 3)
                shard_end = (k + 1) * SHARD_COLS
                if k < 3 and shard_end % ROW_TILE and t == shard_end - shard_end % ROW_TILE:
                    lo = t - _win_start(k)
                    yield win_ref[k, lo:lo + ROW_TILE, :] + win_ref[k + 1, 0:ROW_TILE, :]
                    t += ROW_TILE
                    continue
                upto = min(end, shard_end - shard_end % ROW_TILE if k < 3 else end)
                yield win_ref[k, t - _win_start(k):upto - _win_start(k), :]
                t = upto

        for (_, ranges, total), o_ref in zip(segs, outs):
            off = 0
            for start, rows in ranges:
                for part in pieces(start, rows):
                    o_ref[off:off + part.shape[0], :] = part
                    off += part.shape[0]
            if off < total:
                o_ref[off:total, :] = jnp.zeros((total - off, o_ref.shape[1]), BF16)

    outs = _pcall(
        body, name="assemble_w_in", grid=(D_MODEL // LAYOUT_TC,),
        in_specs=[pl.BlockSpec((4, WIN_ROWS, LAYOUT_TC), lambda i: (0, 0, i))],
        out_specs=[pl.BlockSpec((total, LAYOUT_TC), lambda i: (0, i)) for _, _, total in segs],
        out_shape=[jax.ShapeDtypeStruct((total, D_MODEL), BF16) for _, _, total in segs],
        compiler_params=_params("parallel"),
    )(win)
    return {name: o for (name, _, _), o in zip(segs, outs)}


def _pack(dsegs):
    segs = _segments()

    def body(*refs):
        ins, o_ref = refs[:-1], refs[-1]
        tail = IN_COLS - _win_start(3)
        o_ref[3, tail:, :] = jnp.zeros((WIN_ROWS - tail, o_ref.shape[2]), BF16)
        for (_, ranges, _), s_ref in zip(segs, ins):
            off = 0
            for start, rows in ranges:
                for k in range(4):
                    lo = _win_start(k)
                    a, b = max(start, lo), min(start + rows, lo + WIN_ROWS)
                    if a < b:
                        o_ref[k, a - lo:b - lo, :] = s_ref[off + a - start:off + b - start, :]
                off += rows

    return _pcall(
        body, name="pack_dw_in", grid=(D_MODEL // LAYOUT_TC,),
        in_specs=[pl.BlockSpec((total, LAYOUT_TC), lambda i: (0, i)) for _, _, total in segs],
        out_specs=pl.BlockSpec((4, WIN_ROWS, LAYOUT_TC), lambda i: (0, 0, i)),
        out_shape=jax.ShapeDtypeStruct((4, WIN_ROWS, D_MODEL), BF16),
        compiler_params=_params("parallel"),
    )(*[dsegs[name] for name, _, _ in segs])


def _shard_to_window(shard_t, k):
    def at(off):
        return lambda w: jnp.pad(w.astype(BF16), ((off, WIN_ROWS - SHARD_COLS - off), (0, 0)))

    return lax.cond(k % 2 == 1, at(_win_offset(1)), at(_win_offset(0)), shard_t)


def _window_to_shard(win, k):
    return lax.dynamic_slice(win, ((k % 2) * _win_offset(1), 0), (SHARD_COLS, D_MODEL))


def _bucket_maps():
    qi = np.arange(8)[:, None]
    kj = np.arange(2 * BAND)[None, :]
    delta = qi + BAND - kj
    maps = []
    for window, dil in PATTERNS:
        valid = (delta >= 0) & (delta <= window // dil)
        dist = np.maximum(delta, 0) * dil
        max_exact = NUM_BUCKETS // 2
        d_f = np.maximum(dist, 1).astype(np.float32)
        large = max_exact + (np.log(d_f / np.float32(max_exact)) / np.float32(math.log(MAX_DISTANCE / max_exact))
                             * np.float32(NUM_BUCKETS - max_exact)).astype(np.int32)
        large = np.minimum(large, NUM_BUCKETS - 1)
        bucket = np.where(dist < max_exact, dist, large)
        maps.append(np.where(valid, bucket, -1).astype(np.int32))
    return np.stack(maps)


def _bias_tables(rel_bias, bmaps):
    def body(rb_ref, bm_ref, o_ref):
        g = pl.program_id(0)
        bm = bm_ref[...]
        for hh in range(GROUP_HEADS):
            acc = jnp.full(bm.shape, NEG, F32)
            for b in range(NUM_BUCKETS):
                acc = jnp.where(bm == b, rb_ref[b, g * GROUP_HEADS + hh], acc)
            for a in range(BAND // 8):
                o_ref[hh, 8 * a:8 * a + 8, :] = acc if a == 0 else pltpu.roll(acc, 8 * a, 1)

    return _pcall(
        body, name="bias_tables", grid=(3,),
        in_specs=[pl.BlockSpec(memory_space=pltpu.SMEM),
                  pl.BlockSpec((None, 8, 2 * BAND), lambda g: (g, 0, 0))],
        out_specs=pl.BlockSpec((GROUP_HEADS, BAND, 2 * BAND), lambda g: (g, 0, 0)),
        out_shape=jax.ShapeDtypeStruct((3 * GROUP_HEADS, BAND, 2 * BAND), F32),
        compiler_params=_params("parallel"),
    )(rel_bias, bmaps)


def _bias_grad(dbias, bmaps):
    def body(db_ref, bm_ref, o_ref):
        bm = bm_ref[...]
        lane = lax.broadcasted_iota(jnp.int32, (1, LANE), 1)
        for hh in range(GROUP_HEADS):
            db = db_ref[hh, 0:8, :]
            for a in range(1, BAND // 8):
                db = db + pltpu.roll(db_ref[hh, 8 * a:8 * a + 8, :], 2 * BAND - 8 * a, 1)
            vec = jnp.zeros((1, LANE), F32)
            for b in range(NUM_BUCKETS):
                s = jnp.sum(jnp.where(bm == b, db, 0.0), keepdims=True)
                vec = jnp.where(lane == b, s, vec)
            o_ref[hh] = vec

    return _pcall(
        body, name="bias_grad", grid=(3,),
        in_specs=[pl.BlockSpec((GROUP_HEADS, BAND, 2 * BAND), lambda g: (g, 0, 0)),
                  pl.BlockSpec((None, 8, 2 * BAND), lambda g: (g, 0, 0))],
        out_specs=pl.BlockSpec((GROUP_HEADS, 1, LANE), lambda g: (g, 0, 0)),
        out_shape=jax.ShapeDtypeStruct((3 * GROUP_HEADS, 1, LANE), F32),
        compiler_params=_params("parallel"),
    )(dbias, bmaps)


def _rows(n):
    if isinstance(n, int):
        return pl.ds(n * BAND, BAND)
    return pl.ds(pl.multiple_of(n * BAND, BAND), BAND)


def _for_blocks(blocks, nblk, per, carry):
    carry = blocks([0], carry, False)
    start = 1 + (nblk - 1) % per
    for n in range(1, start):
        carry = blocks([n], carry, True)
    trips = (nblk - start) // per
    if trips > 0:
        carry = lax.fori_loop(
            0, trips, lambda t, c: blocks([start + t * per + u for u in range(per)], c, True), carry)
    return carry


def _pairs_per_step(d):
    return {1: 3, 4: 6, 16: 6}[d]


def _bias_spec(group, hps):
    first = group * GROUP_HEADS // (2 * hps)
    return pl.BlockSpec((2 * hps, BAND, 2 * BAND), lambda hp, b, r: (first + hp, 0, 0))


def _attn_fwd(qkv4, bias, group, d, name):
    nb, _, sub, _ = qkv4.shape
    nblk = sub // BAND
    scale = HEAD_DIM ** -0.5
    npair = ATT_OUT // LANE
    hps = _pairs_per_step(d)
    compact = d > 1

    def body(qkv_ref, bias_ref, o_ref, l_ref):
        def blocks(ns, carry, with_prev):
            chains = [(bi, i, h) for bi in range(len(ns)) for i in range(hps) for h in range(2)]
            first_head = lax.broadcasted_iota(jnp.int32, (BAND, LANE), 1) < HEAD_DIM
            pair = lambda n, i, part: qkv_ref[_rows(n), (3 * i + part) * LANE:(3 * i + part + 1) * LANE]
            scores = []
            for bi, i, h in chains:
                n = ns[bi]
                qp = pair(n, i, 0) * scale
                q = jnp.where(first_head if h == 0 else jnp.logical_not(first_head), qp, jnp.zeros_like(qp))
                s_c = lax.dot_general(q, pair(n, i, 1), NT, preferred_element_type=F32) + bias_ref[2 * i + h, :, BAND:]
                s_p = None
                if with_prev:
                    s_p = lax.dot_general(q, pair(n - 1, i, 1), NT,
                                          preferred_element_type=F32) + bias_ref[2 * i + h, :, :BAND]
                scores.append((s_c, s_p))
            probs = []
            for s_c, s_p in scores:
                m = jnp.max(s_c, -1, keepdims=True)
                if with_prev:
                    m = jnp.maximum(m, jnp.max(s_p, -1, keepdims=True))
                e_c = jnp.exp(s_c - m)
                den = jnp.sum(e_c, -1, keepdims=True)
                e_p = None
                if with_prev:
                    e_p = jnp.exp(s_p - m)
                    den = den + jnp.sum(e_p, -1, keepdims=True)
                    e_p = e_p.astype(BF16)
                probs.append((e_c.astype(BF16), e_p, den, m))
            outs = {}
            for (bi, i, h), (e_c, e_p, den, m) in zip(chains, probs):
                n = ns[bi]
                acc = jnp.dot(e_c, pair(n, i, 2), preferred_element_type=F32)
                if with_prev:
                    acc = acc + jnp.dot(e_p, pair(n - 1, i, 2), preferred_element_type=F32)
                outs[(bi, i, h)] = (acc / den, m + jnp.log(den))
            lane = lax.broadcasted_iota(jnp.int32, (BAND, LANE), 1)
            for bi, n in enumerate(ns):
                per_head = jnp.zeros((BAND, LANE), F32)
                for i in range(hps):
                    o_ref[_rows(n), i * LANE:(i + 1) * LANE] = jnp.where(first_head, outs[(bi, i, 0)][0],
                                                                         outs[(bi, i, 1)][0])
                    if compact:
                        for h in range(2):
                            per_head = jnp.where(lane == 2 * i + h, outs[(bi, i, h)][1], per_head)
                    else:
                        l_ref[_rows(n), i * LANE:(i + 1) * LANE] = jnp.where(first_head, outs[(bi, i, 0)][1],
                                                                             outs[(bi, i, 1)][1])
                if compact:
                    l_ref[_rows(n), :] = per_head
            return carry

        _for_blocks(blocks, nblk, 2 if hps == 1 else 1, 0)

    in_specs = [pl.BlockSpec((None, None, sub, 3 * LANE * hps), lambda hp, b, r: (b, r, 0, hp)),
                _bias_spec(group, hps)]
    if compact:
        return _pcall(
            body, name=name, grid=(1, nb, d), in_specs=in_specs,
            out_specs=[pl.BlockSpec((None, None, sub, ATT_OUT), lambda hp, b, r: (b, r, 0, 0)),
                       pl.BlockSpec((None, None, sub, LANE), lambda hp, b, r: (b, r, 0, 0))],
            out_shape=[jax.ShapeDtypeStruct((nb, d, sub, ATT_OUT), F32), jax.ShapeDtypeStruct((nb, d, sub, LANE), F32)],
            compiler_params=_params("parallel", "parallel", "parallel"),
        )(qkv4, bias)
    ospec = pl.BlockSpec((None, sub, hps * LANE), lambda hp, b, r: (b, 0, r * (npair // hps) + hp))
    return _pcall(
        body, name=name, grid=(npair // hps, nb, d), in_specs=in_specs, out_specs=[ospec, ospec],
        out_shape=[jax.ShapeDtypeStruct((nb, sub, d * ATT_OUT), F32)] * 2,
        compiler_params=_params("parallel", "parallel", "parallel"),
    )(qkv4, bias)


STAT_LSE_LANE = 16


def _attn_bwd(qkv4, bias, group, cotangent, d, name):
    nb, _, sub, _ = qkv4.shape
    nblk = sub // BAND
    scale = HEAD_DIM ** -0.5
    npair = ATT_OUT // LANE
    hps = _pairs_per_step(d)
    compact = d > 1

    def body(qkv_ref, bias_ref, *rest):
        do_ref, dqkv_ref, db_ref = rest[0], rest[-2], rest[-1]
        b, r = pl.program_id(1), pl.program_id(2)

        @pl.when((b == 0) & (r == 0))
        def _():
            db_ref[...] = jnp.zeros_like(db_ref)

        def blocks(ns, carry, with_prev):
            sides = (0, 1) if with_prev else (0,)
            chains = [(bi, i, h, sd) for bi in range(len(ns)) for i in range(hps) for h in range(2) for sd in sides]
            first_head = lax.broadcasted_iota(jnp.int32, (BAND, LANE), 1) < HEAD_DIM
            own = lambda h, t: jnp.where(first_head if h == 0 else jnp.logical_not(first_head), t, jnp.zeros_like(t))
            pair = lambda rows, i, part: qkv_ref[rows, (3 * i + part) * LANE:(3 * i + part + 1) * LANE]
            key_rows = lambda bi, sd: _rows(ns[bi] - sd)
            qs = {}
            for bi in range(len(ns)):
                for i in range(hps):
                    q_pair = pair(_rows(ns[bi]), i, 0) * scale
                    do = do_ref[_rows(ns[bi]), i * LANE:(i + 1) * LANE]
                    do16 = do.astype(BF16)
                    for h in range(2):
                        if compact:
                            st_ref, head = rest[1], 2 * i + h
                            ebar = st_ref[_rows(ns[bi]), head:head + 1]
                            lcol = st_ref[_rows(ns[bi]), STAT_LSE_LANE + head:STAT_LSE_LANE + head + 1]
                        else:
                            ebar = jnp.sum(own(h, do * rest[1][_rows(ns[bi]), i * LANE:(i + 1) * LANE]), -1, keepdims=True)
                            lcol = rest[2][_rows(ns[bi]), i * LANE + h * HEAD_DIM:i * LANE + h * HEAD_DIM + 1]
                        qs[(bi, i, h)] = (own(h, q_pair), q_pair, own(h, do16), do16, ebar, lcol)
            raw = []
            for bi, i, h, sd in chains:
                q, _, do_h, _, _, _ = qs[(bi, i, h)]
                bias_blk = bias_ref[2 * i + h, :, :BAND] if sd else bias_ref[2 * i + h, :, BAND:]
                s = lax.dot_general(q, pair(key_rows(bi, sd), i, 1), NT, preferred_element_type=F32) + bias_blk
                dp = lax.dot_general(do_h, pair(key_rows(bi, sd), i, 2), NT, preferred_element_type=F32)
                raw.append((s, dp))
            soft = []
            for (bi, i, h, sd), (s, dp) in zip(chains, raw):
                ebar, lcol = qs[(bi, i, h)][4:]
                p = jnp.exp(s - lcol)
                ds = p * (dp - ebar)
                if sd:
                    db_ref[2 * i + h, :, :BAND] += ds
                else:
                    db_ref[2 * i + h, :, BAND:] += ds
                soft.append((p.astype(BF16), ds.astype(BF16)))
            grads = {}
            for (bi, i, h, sd), (p16, ds16) in zip(chains, soft):
                _, q_pair, _, do16 = qs[(bi, i, h)][:4]
                grads[(bi, i, h, sd)] = (
                    jnp.dot(ds16, pair(key_rows(bi, sd), i, 1), preferred_element_type=F32),
                    lax.dot_general(ds16, q_pair, TN, preferred_element_type=F32),
                    lax.dot_general(p16, do16, TN, preferred_element_type=F32))
            both = lambda bi, i, sd, which: jnp.where(first_head, grads[(bi, i, 0, sd)][which],
                                                      grads[(bi, i, 1, sd)][which])
            carry = list(carry) if carry is not None else None
            for bi, n in enumerate(ns):
                for i in range(hps):
                    base = 3 * LANE * i
                    dq = both(bi, i, 0, 0)
                    if with_prev:
                        dq = dq + both(bi, i, 1, 0)
                        dqkv_ref[_rows(n - 1), base + LANE:base + 2 * LANE] = (
                            carry[2 * i] + both(bi, i, 1, 1)).astype(BF16)
                        dqkv_ref[_rows(n - 1), base + 2 * LANE:base + 3 * LANE] = (
                            carry[2 * i + 1] + both(bi, i, 1, 2)).astype(BF16)
                    dqkv_ref[_rows(n), base:base + LANE] = (dq * scale).astype(BF16)
                carry = [t for i in range(hps) for t in (both(bi, i, 0, 1), both(bi, i, 0, 2))]
            return tuple(carry)

        carry = _for_blocks(blocks, nblk, 2 if hps == 1 else 1, None)
        for i in range(hps):
            base = 3 * LANE * i
            dqkv_ref[_rows(nblk - 1), base + LANE:base + 2 * LANE] = carry[2 * i].astype(BF16)
            dqkv_ref[_rows(nblk - 1), base + 2 * LANE:base + 3 * LANE] = carry[2 * i + 1].astype(BF16)

    qspec = pl.BlockSpec((None, None, sub, 3 * LANE * hps), lambda hp, b, r: (b, r, 0, hp))
    bspec = pl.BlockSpec((2 * hps, BAND, 2 * BAND), lambda hp, b, r: (hp, 0, 0))
    if compact:
        cspecs = [pl.BlockSpec((None, None, sub, ATT_OUT), lambda hp, b, r: (b, r, 0, 0)),
                  pl.BlockSpec((None, None, sub, LANE), lambda hp, b, r: (b, r, 0, 0))]
    else:
        cspecs = [pl.BlockSpec((None, sub, hps * LANE), lambda hp, b, r: (b, 0, r * (npair // hps) + hp))] * 3
    return _pcall(
        body, name=name, grid=(npair // hps, nb, d),
        in_specs=[qspec, _bias_spec(group, hps)] + cspecs, out_specs=[qspec, bspec],
        out_shape=[jax.ShapeDtypeStruct(qkv4.shape, BF16),
                   jax.ShapeDtypeStruct((GROUP_HEADS, BAND, 2 * BAND), F32)],
        compiler_params=_params("parallel", "arbitrary", "arbitrary"),
    )(qkv4, bias, *cotangent)


def _head_lanes(first_lane, one_channel):
    c = lax.broadcasted_iota(jnp.int32, (ATT_OUT, LANE), 0)
    lane = lax.broadcasted_iota(jnp.int32, (ATT_OUT, LANE), 1)
    hit = lane == first_lane + c // HEAD_DIM
    if one_channel:
        hit = hit & (c % HEAD_DIM == 0)
    return hit.astype(BF16)


def _exact_dot(v, m01, dims=None):
    parts = _split3(v)
    if dims is None:
        dot = lambda t: jnp.dot(t, m01, preferred_element_type=F32)
    else:
        dot = lambda t: lax.dot_general(t, m01, dims, preferred_element_type=F32)
    return (dot(parts[0]) + dot(parts[1])) + dot(parts[2])


def _store_own_order(value, tile_refs, out_ref):
    d, per, width = out_ref.shape
    for j in range(width // LANE):
        tile_refs[j][...] = value[:, j * LANE:(j + 1) * LANE]
    for r in range(d):
        rows = pl.ds(r, per, stride=d)
        for j in range(width // LANE):
            out_ref[r, :, j * LANE:(j + 1) * LANE] = tile_refs[j][rows, :].astype(out_ref.dtype)


def _token_orders(x, dilations):
    nb, seq, kdim = x.shape
    tm = 512

    def body(x_ref, nat_ref, *rest):
        outs, tile_refs = rest[:len(dilations)], rest[len(dilations):]
        xv = x_ref[...]
        nat_ref[...] = xv.astype(BF16)
        for o_ref in outs:
            _store_own_order(xv, tile_refs, o_ref)

    outs = _pcall(
        body, name="token_orders", grid=(nb, seq // tm), in_specs=[_tok_spec(tm, kdim)],
        out_specs=[_tok_spec(tm, kdim)]
        + [pl.BlockSpec((None, d, tm // d, kdim), lambda b, i: (b, 0, i, 0)) for d in dilations],
        out_shape=[jax.ShapeDtypeStruct((nb, seq, kdim), BF16)]
        + [jax.ShapeDtypeStruct((nb, d, seq // d, kdim), BF16) for d in dilations],
        scratch_shapes=[pltpu.VMEM((tm, LANE), F32)] * (kdim // LANE),
        compiler_params=_params("parallel", "parallel"),
    )(x)
    return [outs[0]] + [o.reshape(nb, seq, kdim) for o in outs[1:]]


def _natural_rows(p_ref, tile_refs):
    d, per, width = p_ref.shape
    for r in range(d):
        rows = pl.ds(r, per, stride=d)
        for j in range(width // LANE):
            tile_refs[j][rows, :] = p_ref[r, :, j * LANE:(j + 1) * LANE]
    return jnp.concatenate([tile_refs[j][...] for j in range(width // LANE)], axis=1)


def _natural_value(value, d, tile_refs):
    total, width = value.shape
    per = total // d
    for r in range(d):
        rows = pl.ds(r, per, stride=d)
        for j in range(width // LANE):
            tile_refs[j][rows, :] = value[r * per:(r + 1) * per, j * LANE:(j + 1) * LANE]
    return jnp.concatenate([tile_refs[j][...] for j in range(width // LANE)], axis=1)


def _combine_fwd(o0, l0, dilated, gatt):
    nb, seq, _ = gatt.shape
    tm = 512
    ntile = ATT_OUT // LANE

    def body(o0_ref, l0_ref, o1_ref, l1_ref, o2_ref, l2_ref, g_ref, oa_ref, oatt_ref, lse_ref, *tile_refs):
        spread = _head_lanes(0, False)
        l0v = l0_ref[...]
        l1v = _exact_dot(_natural_rows(l1_ref, tile_refs), spread, NT)
        l2v = _exact_dot(_natural_rows(l2_ref, tile_refs), spread, NT)
        m = jnp.maximum(jnp.maximum(l0v, l1v), l2v)
        tot = m + jnp.log(jnp.exp(l0v - m) + jnp.exp(l1v - m) + jnp.exp(l2v - m))
        o = jnp.exp(l0v - tot) * o0_ref[...]
        o = o + jnp.exp(l1v - tot) * _natural_rows(o1_ref, tile_refs)
        o = o + jnp.exp(l2v - tot) * _natural_rows(o2_ref, tile_refs)
        g = g_ref[...]
        oa_ref[...] = (o * (g * _sigmoid(g))).astype(BF16)
        oatt_ref[...] = o
        lse_ref[...] = tot

    spec = pl.BlockSpec((None, tm, ATT_OUT), lambda b, i: (b, i, 0))
    own = lambda t: pl.BlockSpec((None, t.shape[1], tm // t.shape[1], t.shape[3]), lambda b, i: (b, 0, i, 0))
    (o1, l1), (o2, l2) = dilated
    return _pcall(
        body, name="attn_combine", grid=(nb, seq // tm),
        in_specs=[spec, spec, own(o1), own(l1), own(o2), own(l2), spec], out_specs=[spec] * 3,
        out_shape=[jax.ShapeDtypeStruct((nb, seq, ATT_OUT), BF16), jax.ShapeDtypeStruct((nb, seq, ATT_OUT), F32),
                   jax.ShapeDtypeStruct((nb, seq, ATT_OUT), F32)],
        scratch_shapes=[pltpu.VMEM((tm, LANE), F32)] * ntile,
        compiler_params=_params("parallel", "parallel"),
    )(o0, l0, o1, l1, o2, l2, gatt)


def _combine_bwd(dya16, w_bra, gatt, o_att, lse, dilations):
    nb, seq, _ = gatt.shape
    tm = 512

    def body(dya_ref, w_ref, g_ref, o_ref, l_ref, do_ref, dg_ref, *rest):
        ntile = ATT_OUT // LANE
        outs, tile_refs = rest[:-ntile], rest[-ntile:]
        doa = lax.dot_general(dya_ref[...], w_ref[...], NT, preferred_element_type=F32)
        g = g_ref[...]
        sg = _sigmoid(g)
        do = doa * (g * sg)
        do_ref[...] = do
        stats = (_exact_dot(do * o_ref[...], _head_lanes(0, False))
                 + _exact_dot(l_ref[...], _head_lanes(STAT_LSE_LANE, True)))
        dg_ref[...] = (doa * o_ref[...] * (sg * (1.0 + g * (1.0 - sg)))).astype(BF16)
        for k in range(len(dilations)):
            _store_own_order(do, tile_refs, outs[2 * k])
            _store_own_order(stats, tile_refs, outs[2 * k + 1])

    spec = pl.BlockSpec((None, tm, ATT_OUT), lambda b, i: (b, i, 0))
    own = lambda d, width: pl.BlockSpec((None, d, tm // d, width), lambda b, i: (b, 0, i, 0))
    outs = _pcall(
        body, name="attn_combine_bwd", grid=(nb, seq // tm),
        in_specs=[_tok_spec(tm, D_MODEL), _whole(w_bra, True)] + [spec] * 3,
        out_specs=[spec, spec] + [own(d, w) for d in dilations for w in (ATT_OUT, LANE)],
        out_shape=[jax.ShapeDtypeStruct((nb, seq, ATT_OUT), F32), jax.ShapeDtypeStruct((nb, seq, ATT_OUT), BF16)]
        + [jax.ShapeDtypeStruct((nb, d, seq // d, w), t) for d in dilations for w, t in ((ATT_OUT, BF16), (LANE, F32))],
        scratch_shapes=[pltpu.VMEM((tm, LANE), F32)] * (ATT_OUT // LANE),
        compiler_params=_params("parallel", "parallel"),
    )(dya16, w_bra, gatt, o_att, lse)
    return outs[0], outs[1], outs[2:]


CONV_TM = 1024
CONV_TC = 1024


def _shift_down(cur, halo, k):
    rolled = pltpu.roll(cur, k, 0)
    hro = pltpu.roll(halo, k, 0)
    row = lax.broadcasted_iota(jnp.int32, hro.shape, 0)
    return jnp.concatenate([jnp.where(row < k, hro, rolled[:8]), rolled[8:]], axis=0)


def _shift_up(cur, halo, k):
    n = cur.shape[0]
    rolled = pltpu.roll(cur, n - k, 0)
    hro = pltpu.roll(halo, 8 - k, 0)
    row = lax.broadcasted_iota(jnp.int32, hro.shape, 0)
    return jnp.concatenate([rolled[:n - 8], jnp.where(row >= 8 - k, hro, rolled[n - 8:])], axis=0)


def _conv_pre(cur, halo, w_ref, b_ref):
    acc = cur * w_ref[3:4, :] + b_ref[...]
    for k in range(1, 4):
        acc = acc + _shift_down(cur, halo, k) * w_ref[3 - k:4 - k, :]
    return acc


def _conv_specs(seq):
    nblk = seq // CONV_TM
    cur = pl.BlockSpec((None, CONV_TM, CONV_TC), lambda cb, b, i: (b, i, cb))
    prev = pl.BlockSpec((None, 8, CONV_TC), lambda cb, b, i: (b, jnp.maximum(i * (CONV_TM // 8) - 1, 0), cb))
    nxt = pl.BlockSpec((None, 8, CONV_TC),
                       lambda cb, b, i: (b, jnp.minimum((i + 1) * (CONV_TM // 8), seq // 8 - 1), cb))
    wspec = pl.BlockSpec((4, CONV_TC), lambda cb, b, i: (0, cb))
    bspec = pl.BlockSpec((1, CONV_TC), lambda cb, b, i: (0, cb))
    return nblk, cur, prev, nxt, wspec, bspec


def _conv_fwd(xin, w4, bias, name):
    nb, seq, ch = xin.shape
    _, cur, prev, _, wspec, bspec = _conv_specs(seq)

    def body(x_ref, h_ref, w_ref, b_ref, o_ref):
        halo = jnp.where(pl.program_id(2) > 0, h_ref[...], 0.0)
        pre = _conv_pre(x_ref[...], halo, w_ref, b_ref)
        o_ref[...] = pre * _sigmoid(pre)

    return _pcall(
        body, name=name, grid=(ch // CONV_TC, nb, seq // CONV_TM),
        in_specs=[cur, prev, wspec, bspec], out_specs=cur,
        out_shape=jax.ShapeDtypeStruct(xin.shape, F32),
        compiler_params=_params("parallel", "parallel", "parallel"),
    )(xin, xin, w4, bias)


def _conv_bwd_pre(dact, xin, w4, bias, name):
    nb, seq, ch = xin.shape
    _, cur, prev, _, wspec, bspec = _conv_specs(seq)

    def body(da_ref, x_ref, h_ref, w_ref, b_ref, dp_ref, s_ref):
        b, i = pl.program_id(1), pl.program_id(2)

        @pl.when((b == 0) & (i == 0))
        def _():
            s_ref[...] = jnp.zeros_like(s_ref)

        halo = jnp.where(i > 0, h_ref[...], 0.0)
        x = x_ref[...]
        pre = _conv_pre(x, halo, w_ref, b_ref)
        sg = _sigmoid(pre)
        dpre = da_ref[...] * (sg * (1.0 + pre * (1.0 - sg)))
        dp_ref[...] = dpre
        s_ref[3:4, :] += jnp.sum(dpre * x, 0, keepdims=True)
        for k in range(1, 4):
            s_ref[3 - k:4 - k, :] += jnp.sum(dpre * _shift_down(x, halo, k), 0, keepdims=True)
        s_ref[4:5, :] += jnp.sum(dpre, 0, keepdims=True)

    return _pcall(
        body, name=name, grid=(ch // CONV_TC, nb, seq // CONV_TM),
        in_specs=[cur, cur, prev, wspec, bspec],
        out_specs=[cur, pl.BlockSpec((8, CONV_TC), lambda cb, b, i: (0, cb))],
        out_shape=[jax.ShapeDtypeStruct(xin.shape, F32), jax.ShapeDtypeStruct((8, ch), F32)],
        compiler_params=_params("parallel", "arbitrary", "arbitrary"),
    )(dact, xin, xin, w4, bias)


def _conv_bwd_x(dpre, w4, name):
    nb, seq, ch = dpre.shape
    nblk, cur, _, nxt, wspec, _ = _conv_specs(seq)

    def body(d_ref, n_ref, w_ref, o_ref):
        halo = jnp.where(pl.program_id(2) < nblk - 1, n_ref[...], 0.0)
        cur_v = d_ref[...]
        acc = cur_v * w_ref[3:4, :]
        for j in range(1, 4):
            acc = acc + _shift_up(cur_v, halo, j) * w_ref[3 - j:4 - j, :]
        o_ref[...] = acc.astype(BF16)

    return _pcall(
        body, name=name, grid=(ch // CONV_TC, nb, seq // CONV_TM),
        in_specs=[cur, nxt, wspec], out_specs=cur,
        out_shape=jax.ShapeDtypeStruct(dpre.shape, BF16),
        compiler_params=_params("parallel", "parallel", "parallel"),
    )(dpre, dpre, w4)


def _step_sizes(raw, tb_ref):
    shift = (LANE - GROUP_SSM_HEADS * pl.program_id(0)) % LANE
    v = pltpu.roll(raw + tb_ref[...], shift, 1)
    own = lax.broadcasted_iota(jnp.int32, v.shape, 1) < GROUP_SSM_HEADS
    sp = jnp.maximum(v, 0.0) + jnp.log1p(jnp.exp(-jnp.abs(v)))
    return jnp.where(own, sp, 0.0), jnp.where(own, _sigmoid(v), 0.0)


def _group_lanes(t):
    pads = [(0, 0)] * (t.ndim - 1) + [(0, LANE - GROUP_SSM_HEADS)]
    return jnp.stack([jnp.pad(t[..., GROUP_SSM_HEADS * g:GROUP_SSM_HEADS * (g + 1)], pads) for g in range(SSM_GROUPS)])


def _ungroup_lanes(t):
    return jnp.concatenate([t[g][..., :GROUP_SSM_HEADS] for g in range(SSM_GROUPS)], axis=-1)


def _decays(dt, al_ref):
    row = lax.broadcasted_iota(jnp.int32, (CHUNK, CHUNK), 0)
    col = lax.broadcasted_iota(jnp.int32, (CHUNK, CHUNK), 1)
    tril = (row >= col).astype(BF16)
    triu = (row <= col).astype(BF16)
    arow = -jnp.exp(al_ref[...])
    hi, mid, lo = _split3(dt * arow)
    down = lambda t: jnp.dot(tril, t, preferred_element_type=F32)
    across = lambda t: lax.dot_general(t, triu, TN, preferred_element_type=F32)
    acs = (down(hi) + down(mid)) + down(lo)
    acs_t = (across(hi) + across(mid)) + across(lo)
    return arow, acs, acs_t, row >= col, triu


STEP_CHUNKS = 8


def _ssd_specs(nb, seq):
    nc = seq // CHUNK
    hw = GROUP_SSM_HEADS * HEAD_DIM
    rows, steps = STEP_CHUNKS * CHUNK, nc // STEP_CHUNKS

    def mk(rev):
        cidx = (lambda c: steps - 1 - c) if rev else (lambda c: c)
        wide = pl.BlockSpec((None, rows, hw), lambda g, b, c: (b, cidx(c), g))
        xbc = pl.BlockSpec((None, rows, XBC_GROUP), lambda g, b, c: (b, cidx(c), g))
        lanes = pl.BlockSpec((None, None, rows, LANE), lambda g, b, c: (g, b, cidx(c), 0))
        prev = pl.BlockSpec((None, STEP_CHUNKS, None, D_STATE, hw), lambda g, b, c: (b, cidx(c), g, 0, 0))
        raw = pl.BlockSpec((None, rows, LANE), lambda g, b, c: (b, cidx(c), 0))
        return wide, xbc, lanes, prev, raw

    grow = pl.BlockSpec((None, 1, LANE), lambda g, b, c: (g, 0, 0))
    nwspec = pl.BlockSpec((1, hw), lambda g, b, c: (0, g))
    tbspec = pl.BlockSpec((1, LANE), lambda g, b, c: (0, 0))
    return nc, steps, hw, mk, grow, nwspec, tbspec


def _head_expand():
    hw = GROUP_SSM_HEADS * HEAD_DIM
    r = lax.broadcasted_iota(jnp.int32, (LANE, hw), 0)
    c = lax.broadcasted_iota(jnp.int32, (LANE, hw), 1)
    return ((c // HEAD_DIM) == r).astype(BF16)


def _split3(v):
    hi = v.astype(BF16)
    rest = v - hi.astype(F32)
    mid = rest.astype(BF16)
    return hi, mid, (rest - mid.astype(F32)).astype(BF16)


def _to_channels(v, e):
    hi, mid, lo = _split3(v)
    dot = lambda t: jnp.dot(t, e, preferred_element_type=F32)
    return (dot(hi) + dot(mid)) + dot(lo)


def _to_heads(w, e):
    hi, mid, lo = _split3(w)
    dot = lambda t: lax.dot_general(t, e, (((1,), (1,)), ((), ())), preferred_element_type=F32)
    return (dot(hi) + dot(mid)) + dot(lo)


def _row8(v):
    return jnp.broadcast_to(v, (8, v.shape[1]))


def _ssd_chunk_setup(dt, al_ref, ds_ref):
    arow, acs, acs_t, causal, triu = _decays(dt, al_ref)
    e = _head_expand()
    dtx = _to_channels(dt, e)
    acsx = _to_channels(acs, e)
    lastx = acsx[CHUNK - 1:CHUNK, :]
    dskx = _to_channels(_row8(ds_ref[...]), e)[0:1, :]
    return arow, acs, acs_t, causal, triu, e, dtx, acsx, lastx, dskx


def _ssd_fwd(xbc, dt_raw, dt_bias_row, z, alog_g, dskip_g, normw):
    nb, seq, _ = xbc.shape
    nc, steps, hw, mk, grow, nwspec, tbspec = _ssd_specs(nb, seq)
    wide, xbc_spec, lanes, prev, raw = mk(False)
    tn = (((0,), (0,)), ((), ()))

    def body(xbc_ref, dt_ref, tb_ref, z_ref, al_ref, ds_ref, nw_ref, ys_ref, y_ref, sp_ref, st_ref):
        @pl.when(pl.program_id(2) == 0)
        def _():
            st_ref[...] = jnp.zeros_like(st_ref)

        for ci in range(STEP_CHUNKS):
            chunk(ci, xbc_ref, dt_ref, tb_ref, z_ref, al_ref, ds_ref, nw_ref, ys_ref, y_ref, sp_ref, st_ref)

    def chunk(ci, xbc_ref, dt_ref, tb_ref, z_ref, al_ref, ds_ref, nw_ref, ys_ref, y_ref, sp_ref, st_ref):
        rows = slice(ci * CHUNK, (ci + 1) * CHUNK)
        dt, _ = _step_sizes(dt_ref[rows, :], tb_ref)
        _, acs, acs_t, causal, _, _, dtx, acsx, lastx, dskx = _ssd_chunk_setup(dt, al_ref, ds_ref)
        bmat = xbc_ref[rows, GROUP_CH:GROUP_CH + D_STATE].astype(BF16)
        cmat = xbc_ref[rows, GROUP_CH + D_STATE:].astype(BF16)
        cb = lax.dot_general(cmat, bmat, (((1,), (1,)), ((), ())), preferred_element_type=F32)
        x = xbc_ref[rows, :GROUP_CH]
        xdt = x * dtx
        xdt16 = xdt.astype(BF16)
        first_head = lax.broadcasted_iota(jnp.int32, (CHUNK, LANE), 1) < HEAD_DIM
        pairs = []
        for hp in range(GROUP_SSM_HEADS // 2):
            xp = xdt16[:, hp * LANE:(hp + 1) * LANE]
            two = []
            for j in (2 * hp, 2 * hp + 1):
                lmat = jnp.exp(jnp.where(causal, acs[:, j:j + 1] - acs_t[j:j + 1, :], -jnp.inf))
                two.append(jnp.dot((cb * lmat).astype(BF16), xp, preferred_element_type=F32))
            pairs.append(jnp.where(first_head, two[0], two[1]))
        yd = jnp.concatenate(pairs, axis=1)
        s_prev = st_ref[...]
        s16 = s_prev.astype(BF16)
        sp_ref[ci] = s16
        yo = jnp.dot(cmat, s16, preferred_element_type=F32) * jnp.exp(acsx)
        sts = lax.dot_general(bmat, (xdt * jnp.exp(lastx - acsx)).astype(BF16), tn, preferred_element_type=F32)
        st_ref[...] = s_prev * jnp.exp(lastx) + sts
        y = yd + yo + dskx * x
        zz = z_ref[rows, :]
        u = y * (zz * _sigmoid(zz))
        rn = lax.rsqrt(jnp.mean(u * u, -1, keepdims=True) + RMS_EPS)
        ys_ref[rows, :] = (u * rn * nw_ref[...]).astype(BF16)
        y_ref[rows, :] = y

    return _pcall(
        body, name="ssd_fwd", grid=(SSM_GROUPS, nb, steps),
        in_specs=[xbc_spec, raw, tbspec, wide, grow, grow, nwspec],
        out_specs=[wide, wide, prev],
        out_shape=[jax.ShapeDtypeStruct((nb, seq, D_INNER), BF16), jax.ShapeDtypeStruct((nb, seq, D_INNER), F32),
                   jax.ShapeDtypeStruct((nb, nc, SSM_GROUPS, D_STATE, hw), BF16)],
        scratch_shapes=[pltpu.VMEM((D_STATE, hw), F32)],
        compiler_params=_params("parallel", "parallel", "arbitrary"),
    )(xbc, dt_raw, dt_bias_row, z, alog_g, dskip_g, normw)


def _ssd_bwd(xbc, dt_raw, dt_bias_row, z, y, dys, sprev, alog_g, dskip_g, normw):
    nb, seq, _ = xbc.shape
    nc, steps, hw, mk, grow, nwspec, tbspec = _ssd_specs(nb, seq)
    wide, xbc_spec, lanes, prev, raw = mk(True)
    nt = (((1,), (1,)), ((), ()))
    tn = (((0,), (0,)), ((), ()))

    def body(xbc_ref, dt_ref, tb_ref, z_ref, y_ref, dys_ref, sp_ref, al_ref, ds_ref, nw_ref,
             dxbc_ref, ddt_ref, dz_ref, small_ref, dnw_ref, g_ref):
        b, c = pl.program_id(1), pl.program_id(2)

        @pl.when((b == 0) & (c == 0))
        def _():
            small_ref[...] = jnp.zeros_like(small_ref)
            dnw_ref[...] = jnp.zeros_like(dnw_ref)

        @pl.when(c == 0)
        def _():
            g_ref[...] = jnp.zeros_like(g_ref)

        for ci in reversed(range(STEP_CHUNKS)):
            chunk(ci, xbc_ref, dt_ref, tb_ref, z_ref, y_ref, dys_ref, sp_ref, al_ref, ds_ref, nw_ref,
                  dxbc_ref, ddt_ref, dz_ref, small_ref, dnw_ref, g_ref)

    def chunk(ci, xbc_ref, dt_ref, tb_ref, z_ref, y_ref, dys_ref, sp_ref, al_ref, ds_ref, nw_ref,
              dxbc_ref, ddt_ref, dz_ref, small_ref, dnw_ref, g_ref):
        rows = slice(ci * CHUNK, (ci + 1) * CHUNK)
        yv, zz, dys_v, nw = y_ref[rows, :], z_ref[rows, :], dys_ref[rows, :], nw_ref[...]
        sz = _sigmoid(zz)
        silu = zz * sz
        u = yv * silu
        rn = lax.rsqrt(jnp.mean(u * u, -1, keepdims=True) + RMS_EPS)
        gn = dys_v * nw
        du = rn * gn - u * (rn * rn * rn) * jnp.mean(u * gn, -1, keepdims=True)
        dnw_ref[...] += jnp.sum(dys_v * u * rn, 0, keepdims=True)
        dy = du * silu
        dz_ref[rows, :] = du * yv * (sz * (1.0 + zz * (1.0 - sz)))

        dt, sg = _step_sizes(dt_ref[rows, :], tb_ref)
        arow, acs, acs_t, causal, triu, e, dtx, acsx, lastx, dskx = _ssd_chunk_setup(dt, al_ref, ds_ref)
        dfsx = jnp.exp(acsx)
        dtex = jnp.exp(lastx - acsx)
        bmat = xbc_ref[rows, GROUP_CH:GROUP_CH + D_STATE].astype(BF16)
        cmat = xbc_ref[rows, GROUP_CH + D_STATE:].astype(BF16)
        cb = lax.dot_general(cmat, bmat, nt, preferred_element_type=F32)
        x = xbc_ref[rows, :GROUP_CH]
        xdt = x * dtx
        xdt16 = xdt.astype(BF16)
        xdte = xdt * dtex
        dy16 = dy.astype(BF16)
        dyd = dy * dfsx
        dyd16 = dyd.astype(BF16)
        s16 = sp_ref[ci]
        g = g_ref[...]
        g16 = g.astype(BF16)
        cs = jnp.dot(cmat, s16, preferred_element_type=F32)
        dc_off = lax.dot_general(dyd16, s16, nt, preferred_element_type=F32)
        g_here = lax.dot_general(cmat, dyd16, tn, preferred_element_type=F32)
        bg = jnp.dot(bmat, g16, preferred_element_type=F32)
        db_st = lax.dot_general(xdte.astype(BF16), g16, nt, preferred_element_type=F32)
        ddte_w = bg * xdte
        dcd = _to_heads(_row8(jnp.sum(g * s16.astype(F32), 0, keepdims=True)), e)[0:1, :]
        lane = lax.broadcasted_iota(jnp.int32, (CHUNK, LANE), 1)
        first_head = lane < HEAD_DIM
        sub = lax.broadcasted_iota(jnp.int32, (CHUNK, LANE), 0)
        dacs = jnp.zeros((CHUNK, LANE), F32)
        colsums = jnp.zeros((CHUNK, LANE), F32)
        dcb = jnp.zeros((CHUNK, CHUNK), F32)
        pairs = []
        for hp in range(GROUP_SSM_HEADS // 2):
            xp = xdt16[:, hp * LANE:(hp + 1) * LANE]
            dyp = dy16[:, hp * LANE:(hp + 1) * LANE]
            two = []
            for idx, j in enumerate((2 * hp, 2 * hp + 1)):
                lmat = jnp.exp(jnp.where(causal, acs[:, j:j + 1] - acs_t[j:j + 1, :], -jnp.inf))
                mf = cb * lmat
                dy_h = jnp.where(first_head if idx == 0 else jnp.logical_not(first_head), dyp, jnp.zeros_like(dyp))
                dm = lax.dot_general(dy_h, xp, nt, preferred_element_type=F32)
                two.append(lax.dot_general(mf.astype(BF16), dyp, tn, preferred_element_type=F32))
                wmat = dm * mf
                dcb = dcb + dm * lmat
                dacs = jnp.where(lane == j, jnp.sum(wmat, -1, keepdims=True), dacs)
                colsums = jnp.where(sub == j, jnp.sum(wmat, 0, keepdims=True), colsums)
            pairs.append(jnp.where(first_head, two[0], two[1]))
        dxdt = bg * dtex + jnp.concatenate(pairs, axis=1)
        dacs = dacs - colsums.T + _to_heads(dyd * cs - ddte_w, e)
        cd_row = jnp.exp(acs[CHUNK - 1:CHUNK, :])
        tail = _to_heads(_row8(jnp.sum(ddte_w, 0, keepdims=True)), e)[0:1, :] + dcd * cd_row
        dacs = dacs + jnp.where(sub == CHUNK - 1, tail, 0.0)
        d_hi, d_mid, d_lo = _split3(dacs)
        up = lambda t: jnp.dot(triu, t, preferred_element_type=F32)
        da = (up(d_hi) + up(d_mid)) + up(d_lo)
        ddt_raw = (da * arow + _to_heads(dxdt * x, e)) * sg
        ddt_ref[rows, :] = ddt_raw
        small_ref[0:1, :] += jnp.sum(da * dt, 0, keepdims=True) * arow
        small_ref[1:2, :] += _to_heads(_row8(jnp.sum(dy * x, 0, keepdims=True)), e)[0:1, :]
        small_ref[2:3, :] += jnp.sum(ddt_raw, 0, keepdims=True)
        dcb16 = dcb.astype(BF16)
        dxbc_ref[rows, GROUP_CH + D_STATE:] = dc_off + jnp.dot(dcb16, bmat, preferred_element_type=F32)
        dxbc_ref[rows, GROUP_CH:GROUP_CH + D_STATE] = db_st + lax.dot_general(dcb16, cmat, tn,
                                                                               preferred_element_type=F32)
        dxbc_ref[rows, :GROUP_CH] = dxdt * dtx + dskx * dy
        g_ref[...] = g * jnp.exp(lastx) + g_here

    return _pcall(
        body, name="ssd_bwd", grid=(SSM_GROUPS, nb, steps),
        in_specs=[xbc_spec, raw, tbspec, wide, wide, wide, prev, grow, grow, nwspec],
        out_specs=[xbc_spec, lanes, wide,
                   pl.BlockSpec((None, 8, LANE), lambda g, b, c: (g, 0, 0)), nwspec],
        out_shape=[jax.ShapeDtypeStruct((nb, seq, CONV_DIM), F32),
                   jax.ShapeDtypeStruct((SSM_GROUPS, nb, seq, LANE), F32),
                   jax.ShapeDtypeStruct((nb, seq, D_INNER), F32),
                   jax.ShapeDtypeStruct((SSM_GROUPS, 8, LANE), F32),
                   jax.ShapeDtypeStruct((1, D_INNER), F32)],
        scratch_shapes=[pltpu.VMEM((D_STATE, hw), F32)],
        compiler_params=_params("parallel", "arbitrary", "arbitrary"),
    )(xbc, dt_raw, dt_bias_row, z, y, dys, sprev, alog_g, dskip_g, normw)


EW_TM = 256


def _merge_fwd(oa16, y_ssm16, w_bra, w_brb, gm, bgate):
    nb, seq, _ = oa16.shape

    def body(oa_ref, ys_ref, wa_ref, wb_ref, ga_ref, gb_ref, bg_ref, a_ref, b_ref, o_ref):
        y_a = jnp.dot(oa_ref[...], wa_ref[...], preferred_element_type=F32)
        y_b = jnp.dot(ys_ref[...], wb_ref[...], preferred_element_type=F32)
        a_ref[...] = y_a
        b_ref[...] = y_b
        sa = _sigmoid(ga_ref[...] + bg_ref[0:1, :])
        sb = _sigmoid(gb_ref[...] + bg_ref[1:2, :])
        o_ref[...] = (sa * y_a + sb * y_b).astype(BF16)

    spec = pl.BlockSpec((None, EW_TM, D_MODEL), lambda b, i: (b, i, 0))
    spec1 = pl.BlockSpec((None, EW_TM, D_MODEL), lambda b, i: (b, i, 1))
    return _pcall(
        body, name="merge_fwd", grid=(nb, seq // EW_TM),
        in_specs=[_tok_spec(EW_TM, ATT_OUT), _tok_spec(EW_TM, D_INNER), _whole(w_bra, True), _whole(w_brb, True),
                  spec, spec1, pl.BlockSpec((8, D_MODEL), lambda b, i: (0, 0))],
        out_specs=[spec] * 3,
        out_shape=[jax.ShapeDtypeStruct((nb, seq, D_MODEL), F32)] * 2 + [jax.ShapeDtypeStruct((nb, seq, D_MODEL), BF16)],
        compiler_params=_params("parallel", "parallel"),
    )(oa16, y_ssm16, w_bra, w_brb, gm, gm, bgate)


def _merge_bwd(dpre16, w_out16, y_a, y_b, gm, bgate):
    nb, seq, _ = y_a.shape

    def body(dp_ref, w_ref, a_ref, b_ref, ga_ref, gb_ref, bg_ref, dya_ref, dyb_ref, dg_ref, s_ref):
        @pl.when((pl.program_id(0) == 0) & (pl.program_id(1) == 0))
        def _():
            s_ref[...] = jnp.zeros_like(s_ref)

        dm = lax.dot_general(dp_ref[...], w_ref[...], NT, preferred_element_type=F32)
        sa = _sigmoid(ga_ref[...] + bg_ref[0:1, :])
        sb = _sigmoid(gb_ref[...] + bg_ref[1:2, :])
        dya_ref[...] = (dm * sa).astype(BF16)
        dyb_ref[...] = (dm * sb).astype(BF16)
        dga = dm * a_ref[...] * (sa * (1.0 - sa))
        dgb = dm * b_ref[...] * (sb * (1.0 - sb))
        dg_ref[:, :D_MODEL] = dga.astype(BF16)
        dg_ref[:, D_MODEL:] = dgb.astype(BF16)
        s_ref[0:1, :] += jnp.sum(dga, 0, keepdims=True)
        s_ref[1:2, :] += jnp.sum(dgb, 0, keepdims=True)

    spec = pl.BlockSpec((None, EW_TM, D_MODEL), lambda b, i: (b, i, 0))
    spec1 = pl.BlockSpec((None, EW_TM, D_MODEL), lambda b, i: (b, i, 1))
    small = pl.BlockSpec((8, D_MODEL), lambda b, i: (0, 0))
    return _pcall(
        body, name="merge_bwd", grid=(nb, seq // EW_TM),
        in_specs=[spec, _whole(w_out16, True), spec, spec, spec, spec1, small],
        out_specs=[spec, spec, pl.BlockSpec((None, EW_TM, 2 * D_MODEL), lambda b, i: (b, i, 0)), small],
        out_shape=[jax.ShapeDtypeStruct((nb, seq, D_MODEL), BF16), jax.ShapeDtypeStruct((nb, seq, D_MODEL), BF16),
                   jax.ShapeDtypeStruct((nb, seq, 2 * D_MODEL), BF16), jax.ShapeDtypeStruct((8, D_MODEL), F32)],
        compiler_params=_params("arbitrary", "arbitrary"),
    )(dpre16, w_out16, y_a, y_b, gm, gm, bgate)


def _ln_loss(x, merged16, w_out16, gp, p16, w_ple16, target, bgate, ln_g, ln_b):
    nb, seq, _ = x.shape

    def body(x_ref, m_ref, wo_ref, gp_ref, p_ref, wp_ref, t_ref, bg_ref, g_ref, b_ref,
             dx_ref, dp_ref, dpw_ref, dgp_ref, s_ref):
        @pl.when((pl.program_id(0) == 0) & (pl.program_id(1) == 0))
        def _():
            s_ref[...] = jnp.zeros_like(s_ref)

        sp = _sigmoid(gp_ref[...] + bg_ref[2:3, :])
        pw = jnp.dot(p_ref[...], wp_ref[...], preferred_element_type=F32)
        mix = jnp.dot(m_ref[...], wo_ref[...], preferred_element_type=F32)
        pre = ALPHA * x_ref[...] + mix + sp * pw
        mu = jnp.mean(pre, -1, keepdims=True)
        cen = pre - mu
        rstd = lax.rsqrt(jnp.mean(cen * cen, -1, keepdims=True) + LN_EPS)
        xhat = cen * rstd
        err = xhat * g_ref[...] + b_ref[...] - t_ref[...]
        dy = err * (1.0 / D_MODEL)
        dxh = dy * g_ref[...]
        dpre = rstd * (dxh - jnp.mean(dxh, -1, keepdims=True) - xhat * jnp.mean(dxh * xhat, -1, keepdims=True))
        dx_ref[...] = ALPHA * dpre
        dp_ref[...] = dpre.astype(BF16)
        dpw_ref[...] = (dpre * sp).astype(BF16)
        dgp = dpre * pw * (sp * (1.0 - sp))
        dgp_ref[...] = dgp.astype(BF16)
        s_ref[0:1, :] += jnp.sum(dy * xhat, 0, keepdims=True)
        s_ref[1:2, :] += jnp.sum(dy, 0, keepdims=True)
        s_ref[2:3, :] += jnp.sum(dgp, 0, keepdims=True)
        s_ref[3:4, :] += jnp.sum(err * err, 0, keepdims=True)

    spec = pl.BlockSpec((None, EW_TM, D_MODEL), lambda b, i: (b, i, 0))
    small = pl.BlockSpec((8, D_MODEL), lambda b, i: (0, 0))
    row = pl.BlockSpec((1, D_MODEL), lambda b, i: (0, 0))
    return _pcall(
        body, name="ln_loss", grid=(nb, seq // EW_TM),
        in_specs=[spec, spec, _whole(w_out16, True), spec, pl.BlockSpec((None, EW_TM, PLE_DIM), lambda b, i: (b, i, 0)),
                  _whole(w_ple16, True), spec, small, row, row],
        out_specs=[spec] * 4 + [small],
        out_shape=[jax.ShapeDtypeStruct((nb, seq, D_MODEL), F32)] + [jax.ShapeDtypeStruct((nb, seq, D_MODEL), BF16)] * 3
        + [jax.ShapeDtypeStruct((8, D_MODEL), F32)],
        compiler_params=_params("arbitrary", "arbitrary"),
    )(x, merged16, w_out16, gp, p16, w_ple16, target, bgate, ln_g, ln_b)


def _adamw_update(w_ref, g_ref, m_ref, v_ref, d_ref, nm_ref, nv_ref):
    c1 = 1.0 - ADAM_B1 ** ADAM_STEP
    c2 = 1.0 - ADAM_B2 ** ADAM_STEP
    gv = g_ref[...]
    nm = ADAM_B1 * m_ref[...] + (1.0 - ADAM_B1) * gv
    nv = ADAM_B2 * v_ref[...] + (1.0 - ADAM_B2) * (gv * gv)
    d_ref[...] = -ADAM_LR * ((nm / c1) / (jnp.sqrt(nv / c2) + ADAM_EPS) + ADAM_WD * w_ref[...])
    nm_ref[...] = nm
    nv_ref[...] = nv


def _adamw(w, g, m, v, name):
    rows, cols = w.shape
    tr = _row_tile(rows, cols, 8, 5 << 19)

    def body(*refs):
        _adamw_update(*refs)

    spec = pl.BlockSpec((tr, cols), lambda i: (i, 0))
    return _pcall(
        body, name=name, grid=(rows // tr,), in_specs=[spec] * 4, out_specs=[spec] * 3,
        out_shape=[jax.ShapeDtypeStruct(w.shape, F32)] * 3, compiler_params=_params("parallel"),
    )(w, g, m, v)


def _adamw_small(ws, gs, ms, vs, name):
    n = len(ws)

    def body(*refs):
        for i in range(n):
            _adamw_update(*[refs[k * n + i] for k in range(7)])

    outs = _pcall(body, name=name, out_shape=[jax.ShapeDtypeStruct(w.shape, F32) for w in ws] * 3,
                  compiler_params=_params())(*ws, *gs, *ms, *vs)
    return outs[:n], outs[n:2 * n], outs[2 * n:]


def _sum_rows(parts, out_dtype, name):
    rows, cols = parts[0].shape
    tr = rows
    for cand in range(16, rows, 16):
        if rows % cand == 0 and cand * cols * 4 <= (1 << 20):
            tr = cand
    n = len(parts)

    def body(*refs):
        acc = refs[0][...].astype(F32)
        for r in refs[1:n]:
            acc = acc + r[...].astype(F32)
        refs[n][...] = acc.astype(out_dtype)

    spec = pl.BlockSpec((tr, cols), lambda i: (i, 0))
    return _pcall(
        body, name=name, grid=(rows // tr,), in_specs=[spec] * n, out_specs=spec,
        out_shape=jax.ShapeDtypeStruct((rows, cols), out_dtype), compiler_params=_params("parallel"),
    )(*parts)


def _place():
    return lax.axis_index("x"), lax.axis_index("y"), lax.axis_index("c")


def _other_chips(x, y):
    return [(1 - x, y), (x, 1 - y), (1 - x, 1 - y)]


def _remote(src, dst, send_sem, recv_sem, to):
    return pltpu.make_async_remote_copy(src_ref=src, dst_ref=dst, send_sem=send_sem, recv_sem=recv_sem,
                                        device_id=to, device_id_type=MESH)


ANY = pl.BlockSpec(memory_space=pl.ANY)
D2D_CHUNK_BYTES = 512 * 1024
ICI_CHUNK_BYTES = 2 * 1024 * 1024


def _row_chunks(rows, row_bytes, chunk_bytes=D2D_CHUNK_BYTES):
    per = max(16, chunk_bytes // row_bytes // 16 * 16)
    return [(s, min(per, rows - s)) for s in range(0, rows, per)]


def _row_tile(rows, cols, align, limit=1 << 21):
    best = None
    for cand in range(align, rows + 1, align):
        if rows % cand == 0 and cand * cols * 4 <= limit:
            best = cand
    return best or rows


def _allgather_pieces(pieces):
    n = len(pieces)
    halves = [_row_chunks(p.shape[0] // 2, p.shape[1] * p.dtype.itemsize, ICI_CHUNK_BYTES) for p in pieces]
    entries = [(a, q, s, m, j) for a in range(n) for q, (s, m) in enumerate(halves[a]) for j in range(3)]
    slot = {(a, q, j): k for k, (a, q, _, _, j) in enumerate(entries)}
    n_ici = len(entries)

    def body(*refs):
        ins, outs = refs[:n], refs[n:2 * n]
        send_sems, recv_sems = refs[2 * n:]
        x, y, c = _place()
        me = 2 * x + y
        sibling = (x, y, 1 - c)
        chips = _other_chips(x, y)

        def landed(a, s, m, j, core):
            half = ins[a].shape[0] // 2
            return outs[a].at[2 * chips[j][0] + chips[j][1], pl.ds(core * half + s, m)]

        sent = []
        for k, (a, q, s, m, j) in enumerate(entries):
            if j < 2:
                half = ins[a].shape[0] // 2
                cp = _remote(ins[a].at[pl.ds(c * half + s, m)], outs[a].at[me, pl.ds(c * half + s, m)],
                             send_sems.at[k], recv_sems.at[k], (*chips[j], c))
                cp.start()
                sent.append(cp)

        def pass_to_sibling(k, blk):
            fw = _remote(blk, blk, send_sems.at[n_ici + k], recv_sems.at[n_ici + k], sibling)
            fw.start()
            sent.append(fw)

        for k, (a, q, s, m, j) in enumerate(entries):
            if j < 2:
                blk = landed(a, s, m, j, c)
                _remote(blk, blk, send_sems.at[k], recv_sems.at[k], (*chips[j], c)).wait_recv()
                first = q < (len(halves[a]) + 1) // 2
                if (j == 0) == first:
                    on = slot[(a, q, 2)]
                    rl = _remote(blk, blk, send_sems.at[on], recv_sems.at[on], (*chips[1 - j], c))
                    rl.start()
                    sent.append(rl)
                pass_to_sibling(k, blk)
        for k, (a, q, s, m, j) in enumerate(entries):
            if j == 2:
                blk = landed(a, s, m, j, c)
                _remote(blk, blk, send_sems.at[k], recv_sems.at[k], (*chips[j], c)).wait_recv()
                pass_to_sibling(k, blk)
        for k, (a, q, s, m, j) in enumerate(entries):
            blk = landed(a, s, m, j, 1 - c)
            _remote(blk, blk, send_sems.at[n_ici + k], recv_sems.at[n_ici + k], sibling).wait_recv()
        for cp in sent:
            cp.wait_send()

    gathered = _pcall(
        body, name="allgather_weights", in_specs=[ANY] * n, out_specs=[ANY] * n,
        out_shape=[jax.ShapeDtypeStruct((4,) + p.shape, p.dtype) for p in pieces],
        scratch_shapes=[pltpu.SemaphoreType.DMA((2 * n_ici,)), pltpu.SemaphoreType.DMA((2 * n_ici,))],
        compiler_params=pltpu.CompilerParams(has_side_effects=True),
    )(*pieces)
    x, y, _ = _place()
    return [lax.dynamic_update_slice(g, p[None], (2 * x + y, 0, 0)) for g, p in zip(gathered, pieces)]


def _sibling_exchange(grads):
    n = len(grads)
    chunks = [_row_chunks(g.shape[1] // 2, g.shape[2] * g.dtype.itemsize) for g in grads]
    n_sem = 4 * sum(len(ch) for ch in chunks)

    def body(*refs):
        ins, gots = refs[:n], refs[n:2 * n]
        send_sems, recv_sems = refs[2 * n:]
        x, y, c = _place()
        sibling = (x, y, 1 - c)
        work = []
        for a in range(n):
            half = ins[a].shape[1] // 2
            for piece in range(4):
                for s, m in chunks[a]:
                    k = len(work)
                    cp = _remote(ins[a].at[piece, pl.ds((1 - c) * half + s, m)], gots[a].at[piece, pl.ds(s, m)],
                                 send_sems.at[k], recv_sems.at[k], sibling)
                    cp.start()
                    work.append(cp)
        for cp in work:
            cp.wait()

    return _pcall(
        body, name="grad_sibling_exchange", in_specs=[ANY] * n, out_specs=[ANY] * n,
        out_shape=[jax.ShapeDtypeStruct((4, g.shape[1] // 2, g.shape[2]), g.dtype) for g in grads],
        scratch_shapes=[pltpu.SemaphoreType.DMA((n_sem,)), pltpu.SemaphoreType.DMA((n_sem,))],
        compiler_params=pltpu.CompilerParams(has_side_effects=True),
    )(*grads)


def _sibling_gather(fulls):
    n = len(fulls)
    chunks = [_row_chunks(f.shape[0] // 2, f.shape[1] * f.dtype.itemsize) for f in fulls]
    n_sem = sum(len(ch) for ch in chunks)

    def body(*refs):
        outs = refs[n:2 * n]
        send_sems, recv_sems = refs[2 * n:]
        x, y, c = _place()
        sibling = (x, y, 1 - c)
        work = []
        for a in range(n):
            h = outs[a].shape[0] // 2
            for s, m in chunks[a]:
                k = len(work)
                mine = outs[a].at[pl.ds(c * h + s, m)]
                cp = _remote(mine, mine, send_sems.at[k], recv_sems.at[k], sibling)
                cp.start()
                work.append((a, s, m, cp))
        for k, (a, s, m, cp) in enumerate(work):
            h = outs[a].shape[0] // 2
            cp.wait_send()
            theirs = outs[a].at[pl.ds((1 - c) * h + s, m)]
            _remote(theirs, theirs, send_sems.at[k], recv_sems.at[k], sibling).wait_recv()

    return _pcall(
        body, name="grad_sibling_gather", in_specs=[ANY] * n, out_specs=[ANY] * n,
        out_shape=[jax.ShapeDtypeStruct(f.shape, f.dtype) for f in fulls],
        input_output_aliases={a: a for a in range(n)},
        scratch_shapes=[pltpu.SemaphoreType.DMA((n_sem,)), pltpu.SemaphoreType.DMA((n_sem,))],
        compiler_params=pltpu.CompilerParams(has_side_effects=True),
    )(*fulls)


def _pair_sum(grad, got, place, name):
    _, rows, cols = grad.shape
    half = rows // 2
    tr = _row_tile(half, cols, 16)

    def body(p_ref, a_ref, b_ref, o_ref):
        o_ref[...] = (a_ref[...].astype(F32) + b_ref[...].astype(F32)).astype(BF16)

    return _pcall(
        body, name=name,
        grid_spec=pltpu.PrefetchScalarGridSpec(
            num_scalar_prefetch=1, grid=(4, half // tr),
            in_specs=[pl.BlockSpec((None, tr, cols), lambda k, i, p: (k, p[1] * (half // tr) + i, 0)),
                      pl.BlockSpec((None, tr, cols), lambda k, i, p: (k, i, 0))],
            out_specs=pl.BlockSpec((None, tr, cols), lambda k, i, p: (k, i, 0))),
        out_shape=jax.ShapeDtypeStruct((4, half, cols), BF16),
        compiler_params=_params("parallel", "parallel"),
    )(place, grad, got)


def _chip_sum(sums, got, place, name):
    _, h, cols = sums.shape
    tr = _row_tile(h, cols, 16)

    def body(p_ref, own_ref, g0, g1, g2, o_ref):
        o_ref[...] = ((own_ref[...].astype(F32) + g0[...].astype(F32)) + g1[...].astype(F32)) + g2[...].astype(F32)

    gspec = lambda j: pl.BlockSpec((None, tr, cols), lambda i, p: (j, i, 0))
    return _pcall(
        body, name=name,
        grid_spec=pltpu.PrefetchScalarGridSpec(
            num_scalar_prefetch=1, grid=(h // tr,),
            in_specs=[pl.BlockSpec((None, tr, cols), lambda i, p: (p[0], i, 0)), gspec(0), gspec(1), gspec(2)],
            out_specs=pl.BlockSpec((tr, cols), lambda i, p: (p[1] * (h // tr) + i, 0))),
        out_shape=jax.ShapeDtypeStruct((2 * h, cols), F32),
        compiler_params=_params("parallel"),
    )(place, sums, got, got, got)


def _allgather8(buf, name):
    rows = buf.shape[0]

    def body(in_ref, out_ref, send_sems, recv_sems):
        x, y, c = _place()
        me = 4 * x + 2 * y + c
        out_ref[me] = in_ref[...]
        work = []
        for rel in range(1, 8):
            fx, fy, fc = (rel >> 2) & 1, (rel >> 1) & 1, rel & 1
            to = (x ^ fx, y ^ fy, c ^ fc)
            cp = _remote(in_ref, out_ref.at[me], send_sems.at[rel - 1], recv_sems.at[rel - 1], to)
            cp.start()
            work.append((cp, 4 * to[0] + 2 * to[1] + to[2]))
        for rel, (cp, frm) in enumerate(work):
            cp.wait_send()
            blk = out_ref.at[frm]
            _remote(blk, blk, send_sems.at[rel], recv_sems.at[rel], (x, y, c)).wait_recv()

    return _pcall(
        body, name=name, in_specs=[pl.BlockSpec(memory_space=pltpu.VMEM)],
        out_specs=pl.BlockSpec(memory_space=pltpu.VMEM),
        out_shape=jax.ShapeDtypeStruct((8, rows, LANE), F32),
        scratch_shapes=[pltpu.SemaphoreType.DMA((7,)), pltpu.SemaphoreType.DMA((7,))],
        compiler_params=pltpu.CompilerParams(has_side_effects=True),
    )(buf)


def _pack_rows(arrs):
    flats = [a.reshape(-1).astype(F32) for a in arrs]
    starts = np.cumsum([0] + [-(-f.shape[0] // LANE) * LANE for f in flats])
    total = -(-int(starts[-1]) // (8 * LANE)) * 8 * LANE
    flat = sum(jnp.pad(f, (int(s), total - int(s) - f.shape[0])) for f, s in zip(flats, starts))
    return flat.reshape(total // LANE, LANE)


def _unpack_rows(buf, shapes):
    flat = buf.reshape(-1)
    outs, off = [], 0
    for s in shapes:
        n = int(np.prod(s))
        outs.append(flat[off:off + n].reshape(s))
        off += -(-n // LANE) * LANE
    return outs


def _local_grads(x, p, target, wseg, w_br16, w_out16, w_ple16, b_gate, conv_w, conv_b, dt_bias, a_log, d_skip,
                 ssm_norm_w, ln_g, ln_b, rel_bias, finish_dx):
    nb, seq, _ = x.shape
    bmaps = jnp.asarray(_bucket_maps())
    bias = _bias_tables(rel_bias, bmaps)
    bgate8 = jnp.pad(b_gate, ((0, 5), (0, 0)))
    dils = [d for _, d in PATTERNS]

    x16p = _token_orders(x, dils[1:])
    x16 = x16p[0]
    p16 = p.astype(BF16)
    qkv = [_proj(x16p[g], [wseg["qkv%d" % g]], BF16, "proj_qkv%d" % g, True, 2 * MM_TM)[0].reshape(
        nb, dils[g], seq // dils[g], -1) for g in range(3)]
    nat = {}
    for gi, (group, tm) in enumerate(NAT_GROUPS):
        outs = _proj(x16, [wseg[s] for s in group], F32, "proj_nat%d" % gi, True, tm)
        nat.update(zip(group, outs))
    att = [_attn_fwd(qkv[g], bias, g, dils[g], "attn_fwd%d" % g) for g in range(3)]
    oa, o_att, lse = _combine_fwd(att[0][0], att[0][1], att[1:], nat["gatt"])

    conv_wg, conv_bg = _xbc_group_order(conv_w), _xbc_group_order(conv_b)
    act = _conv_fwd(nat["xbc"], conv_wg, conv_bg, "conv_fwd")
    dt_bias_row = jnp.pad(dt_bias, ((0, 0), (0, LANE - SSM_HEADS)))
    alog_g, dskip_g = _group_lanes(a_log), _group_lanes(d_skip)
    y_ssm, y_all, sprev = _ssd_fwd(act, nat["dt"], dt_bias_row, nat["z"], alog_g, dskip_g, ssm_norm_w)

    w_bra, w_brb = w_br16[:ATT_OUT], w_br16[ATT_OUT:]
    y_a, y_b, merged = _merge_fwd(oa, y_ssm, w_bra, w_brb, nat["gm"], bgate8)

    dx, dpre16, dpw16, dgp16, ln_sums = _ln_loss(x, merged, w_out16, nat["gp"], p16, w_ple16, target, bgate8,
                                                 ln_g, ln_b)
    loss_sum = (0.5 / D_MODEL) * jnp.sum(ln_sums[3])
    dya16, dyb16, dgm16, mg_sums = _merge_bwd(dpre16, w_out16, y_a, y_b, nat["gm"], bgate8)
    dys = _dx([dyb16], [w_brb], [], "dx_yssm")
    g_w_br, g_w_out, g_w_ple = _dw_stacked(
        [[(oa, dya16), (y_ssm, dyb16)], [(merged, dpre16)], [(p16, dpw16)]], BF16, "dw_branch_out_ple")

    do_att, dgatt16, own_order = _combine_bwd(dya16, w_bra, nat["gatt"], o_att, lse, dils[1:])
    dseg = {"gatt": dgatt16, "gm": dgm16, "gp": dgp16}
    dbias = []
    for g in range(3):
        cotangent = (do_att, o_att, lse) if g == 0 else (own_order[2 * g - 2], own_order[2 * g - 1])
        dqkv, db = _attn_bwd(qkv[g], bias, g, cotangent, dils[g],
                             "attn_bwd%d" % g)
        dseg["qkv%d" % g] = dqkv.reshape(nb, seq, -1)
        dbias.append(db)
    g_rel = _bias_grad(jnp.concatenate(dbias, axis=0), bmaps)[:, 0, :NUM_BUCKETS].T

    dact, ddtg, dz, ssd_small, g_normw = _ssd_bwd(
        act, nat["dt"], dt_bias_row, nat["z"], y_all, dys, sprev, alog_g, dskip_g, ssm_norm_w)
    dseg["z"] = dz
    dseg["dt"] = jnp.pad(_ungroup_lanes(ddtg), ((0, 0), (0, 0), (0, LANE - SSM_HEADS)))
    dpre, conv_sums = _conv_bwd_pre(dact, nat["xbc"], conv_wg, conv_bg, "conv_bwd")
    dseg["xbc"] = _conv_bwd_x(dpre, conv_wg, "conv_bwd_x")
    csum = _xbc_reference_order(conv_sums)

    dh_own = [(dseg["qkv%d" % g].reshape(nb, dils[g], seq // dils[g], -1), wseg["qkv%d" % g]) for g in (1, 2)]
    dwseg = {"qkv%d" % g: _dw(x16p[g], [dseg["qkv%d" % g]], BF16, "dw_qkv%d" % g, True)[0] for g in range(3)}
    for gi, group in enumerate(DW_GROUPS):
        dwseg.update(zip(group, _dw(x16, [dseg[s] for s in group], BF16, "dw_nat%d" % gi, True)))
    names = ["qkv0"] + [s for group, _ in NAT_GROUPS for s in group]
    dx = finish_dx([dseg[s] for s in names], [wseg[s] for s in names], [dx], dh_own, dwseg, g_w_br, g_w_out, g_w_ple)

    small = dict(
        b_gate=jnp.stack([mg_sums[0], mg_sums[1], ln_sums[2]]),
        conv_w=csum[0:4], conv_b=csum[4:5],
        dt_bias=_ungroup_lanes(ssd_small[:, 2:3, :]), a_log=_ungroup_lanes(ssd_small[:, 0:1, :]),
        d_skip=_ungroup_lanes(ssd_small[:, 1:2, :]), ssm_norm_w=g_normw,
        ln_g=ln_sums[0:1], ln_b=ln_sums[1:2], rel_bias=g_rel)
    return loss_sum, dx, small


DX_TM = 256
SMALL_ORDER = ("b_gate", "conv_w", "conv_b", "dt_bias", "a_log", "d_skip", "ssm_norm_w", "ln_g", "ln_b", "rel_bias")
SMALL_FULL_SHAPES = dict(b_gate=(3, 1024), conv_w=(4, 3072), conv_b=(1, 3072), dt_bias=(1, 32), a_log=(1, 32),
                         d_skip=(1, 32), ssm_norm_w=(1, 2048), ln_g=(1, 1024), ln_b=(1, 1024), rel_bias=(32, 36))


def kernel(x, p, w_in, b_gate, conv_w, conv_b, dt_bias, a_log, d_skip, ssm_norm_w, w_branch, w_out, w_ple, ln_g, ln_b, rel_bias, loss_target, m_w_in, m_b_gate, m_conv_w, m_conv_b, m_dt_bias, m_a_log, m_d_skip, m_ssm_norm_w, m_w_branch, m_w_out, m_w_ple, m_ln_g, m_ln_b, m_rel_bias, v_w_in, v_b_gate, v_conv_w, v_conv_b, v_dt_bias, v_a_log, v_d_skip, v_ssm_norm_w, v_w_branch, v_w_out, v_w_ple, v_ln_g, v_ln_b, v_rel_bias):
    cx, cy, cc = _place()
    chip = 2 * cx + cy
    dev = 4 * cx + 2 * cy + cc

    w_in_t = jnp.transpose(w_in[0])
    win16 = _shard_to_window(w_in_t, chip)
    g_win, g_br, g_out, g_ple = _allgather_pieces(
        [win16, w_branch[0].astype(BF16), w_out[0].astype(BF16), w_ple[0].astype(BF16)])
    wseg = _assemble(g_win)
    w_br16 = g_br.reshape(4 * 704, D_MODEL)
    w_out16 = g_out.reshape(D_MODEL, D_MODEL)
    w_ple16 = jnp.transpose(g_ple, (1, 0, 2)).reshape(PLE_DIM, D_MODEL)
    shards = _allgather8(_pack_rows([b_gate[0], conv_w[0]]), "allgather_small_params")
    per_chip = [_unpack_rows(shards[2 * k], [(3, 256), (4, 768)]) for k in range(4)]
    b_gate_full = _join_last([pc[0] for pc in per_chip])
    conv_w_full = _join_last([pc[1] for pc in per_chip])

    place = jnp.stack([chip, cc]).astype(jnp.int32)
    reduced = []

    def finish_dx(dhs, ws, accs, own_order_dhs, dwseg, d_br, d_out, d_ple):
        grads = [_pack(dwseg), d_br.reshape(4, 704, D_MODEL), d_out.reshape(4, 256, D_MODEL),
                 jnp.transpose(d_ple.reshape(PLE_DIM, 4, 256), (1, 0, 2))]
        got = _sibling_exchange(grads)
        chip_sums = [_pair_sum(g, t, place, "grad_pair_sum_%d" % i) for i, (g, t) in enumerate(zip(grads, got))]
        dx, others = _dx(dhs, ws, accs, "dx_w_in_and_grad_chip_scatter", True, DX_TM, chip_sums, own_order_dhs)
        fulls = [_chip_sum(s, t, place, "grad_chip_sum_%d" % i) for i, (s, t) in enumerate(zip(chip_sums, others))]
        reduced.extend(_sibling_gather(fulls))
        return dx

    loss_sum, grad_x, small = _local_grads(
        x, p[0], loss_target, wseg, w_br16, w_out16, w_ple16, b_gate_full, conv_w_full, conv_b, dt_bias, a_log,
        d_skip, ssm_norm_w, ln_g, ln_b, rel_bias, finish_dx)
    big = reduced
    g_w_in = lax.optimization_barrier(_window_to_shard(big[0], chip))
    g_w_branch, g_w_out, g_w_ple = big[1], big[2], big[3]
    parts = _allgather8(_pack_rows([small[n] for n in SMALL_ORDER] + [loss_sum.reshape(1, 1)]),
                        "allgather_small_grads")
    small_sum = _sum_rows([parts[i] for i in range(8)], F32, "small_grad_sum")
    *reduced_small, loss = _unpack_rows(small_sum, [SMALL_FULL_SHAPES[n] for n in SMALL_ORDER] + [(1, 1)])
    loss = loss.reshape(())
    sg = dict(zip(SMALL_ORDER, reduced_small))
    sg["b_gate"] = lax.dynamic_slice_in_dim(sg["b_gate"], chip * 256, 256, axis=1)
    sg["conv_w"] = lax.dynamic_slice_in_dim(sg["conv_w"], chip * 768, 768, axis=1)
    del dev

    upd = {}
    upd["w_in"] = [jnp.transpose(t) for t in _adamw(w_in_t, g_w_in, jnp.transpose(m_w_in[0]),
                                                      jnp.transpose(v_w_in[0]), "adamw_w_in")]
    upd["w_branch"] = _adamw(w_branch[0], g_w_branch, m_w_branch[0], v_w_branch[0], "adamw_w_branch")
    upd["w_out"] = _adamw(w_out[0], g_w_out, m_w_out[0], v_w_out[0], "adamw_w_out")
    upd["w_ple"] = _adamw(w_ple[0], g_w_ple, m_w_ple[0], v_w_ple[0], "adamw_w_ple")
    small_w = dict(b_gate=b_gate, conv_w=conv_w, conv_b=conv_b, dt_bias=dt_bias, a_log=a_log, d_skip=d_skip,
                   ssm_norm_w=ssm_norm_w, ln_g=ln_g, ln_b=ln_b, rel_bias=rel_bias)
    small_m = dict(b_gate=m_b_gate, conv_w=m_conv_w, conv_b=m_conv_b, dt_bias=m_dt_bias, a_log=m_a_log,
                   d_skip=m_d_skip, ssm_norm_w=m_ssm_norm_w, ln_g=m_ln_g, ln_b=m_ln_b, rel_bias=m_rel_bias)
    small_v = dict(b_gate=v_b_gate, conv_w=v_conv_w, conv_b=v_conv_b, dt_bias=v_dt_bias, a_log=v_a_log,
                   d_skip=v_d_skip, ssm_norm_w=v_ssm_norm_w, ln_g=v_ln_g, ln_b=v_ln_b, rel_bias=v_rel_bias)
    for n in SMALL_ORDER:
        sg[n] = sg[n].reshape(small_w[n].shape)
    s_delta, s_m, s_v = _adamw_small(*[[t[n] for n in SMALL_ORDER] for t in (small_w, sg, small_m, small_v)],
                                     "adamw_small")
    for i, n in enumerate(SMALL_ORDER):
        upd[n] = (s_delta[i], s_m[i], s_v[i])

    order = ("w_in", "b_gate", "conv_w", "conv_b", "dt_bias", "a_log", "d_skip", "ssm_norm_w", "w_branch", "w_out",
             "w_ple", "ln_g", "ln_b", "rel_bias")
    grads = dict(sg, w_in=jnp.transpose(g_w_in)[None],w_branch=g_w_branch[None], w_out=g_w_out[None], w_ple=g_w_ple[None])
    lead = lambda n, t: t[None] if n in ("w_in", "w_branch", "w_out", "w_ple") else t
    return (loss, grad_x, *[grads[n] for n in order], *[lead(n, upd[n][0]) for n in order],
            *[lead(n, upd[n][1]) for n in order], *[lead(n, upd[n][2]) for n in order])
```

```python
import math

import numpy as np
import jax
import jax.numpy as jnp
from jax import lax
from jax.experimental import pallas as pl
from jax.experimental.pallas import tpu as pltpu

F32, BF16 = jnp.float32, jnp.bfloat16

D_MODEL = 1024
HEAD_DIM = 64
GROUP_HEADS = 12
ATT_OUT = GROUP_HEADS * HEAD_DIM
PATTERNS = ((128, 1), (512, 4), (2048, 16))
BAND = 128
NUM_BUCKETS = 32
MAX_DISTANCE = 2048
D_INNER = 2048
SSM_HEADS = 32
SSM_GROUPS = 4
GROUP_SSM_HEADS = SSM_HEADS // SSM_GROUPS
D_STATE = 128
CHUNK = 128
PLE_DIM = 256
ALPHA = 2.0 ** 0.25
LN_EPS = 1e-5
RMS_EPS = 1e-5
ADAM_LR, ADAM_B1, ADAM_B2, ADAM_EPS, ADAM_WD, ADAM_STEP = 0.001, 0.9, 0.999, 1e-08, 0.01, 10
NEG = -1e30

QKV_W = 3 * ATT_OUT
IN_COLS = 15904
SHARD_COLS = IN_COLS // 4
DT_COL = 12800
ROW_TILE = 16
WIN_ROWS = 4000


def _win_offset(k):
    return (k * SHARD_COLS) % ROW_TILE


def _win_start(k):
    return k * SHARD_COLS - _win_offset(k)

VMEM_LIMIT_BYTES = 56 * 1024 * 1024
LANE = 128
MESH = pl.DeviceIdType.MESH
NT = (((1,), (1,)), ((), ()))
TN = (((0,), (0,)), ((), ()))


def _pcall(body, **kw):
    return pl.pallas_call(body, **kw)


def _params(*sem):
    return pltpu.CompilerParams(dimension_semantics=sem, vmem_limit_bytes=VMEM_LIMIT_BYTES)


def _sigmoid(v):
    return jax.nn.sigmoid(v)


MM_TM = 512


def _tok_spec(tm, width):
    return pl.BlockSpec((None, tm, width), lambda b, i: (b, i, 0))


def _whole(arr, single_buffer=False):
    mode = dict(pipeline_mode=pl.Buffered(1)) if single_buffer else {}
    return pl.BlockSpec(arr.shape, lambda b, i: (0,) * arr.ndim, **mode)


def _proj(a3, ws, out_dtype, name, w_rows_are_outputs=False, tm=MM_TM):
    nb, seq, kdim = a3.shape
    nw = len(ws)
    widths = [w.shape[0] if w_rows_are_outputs else w.shape[1] for w in ws]

    def body(*refs):
        a = refs[0][...].astype(BF16)
        for w_ref, o_ref in zip(refs[1:1 + nw], refs[1 + nw:]):
            if w_rows_are_outputs:
                v = lax.dot_general(a, w_ref[...], NT, preferred_element_type=F32)
            else:
                v = jnp.dot(a, w_ref[...], preferred_element_type=F32)
            o_ref[...] = v.astype(out_dtype)

    return _pcall(
        body, name=name, grid=(nb, seq // tm),
        in_specs=[_tok_spec(tm, kdim)] + [_whole(w, True) for w in ws],
        out_specs=[_tok_spec(tm, n) for n in widths],
        out_shape=[jax.ShapeDtypeStruct((nb, seq, n), out_dtype) for n in widths],
        compiler_params=_params("parallel", "parallel"),
    )(a3, *ws)


def _dx(dhs, ws, accs, name, w_rows_are_outputs=False, tm=MM_TM, scatter=None, own_order_dhs=()):
    nb, seq, _ = dhs[0].shape
    nd, nacc, npd = len(dhs), len(accs), len(own_order_dhs)
    kout = ws[0].shape[1] if w_rows_are_outputs else ws[0].shape[0]
    sums = scatter or []
    ns = len(sums)
    chunks = [_row_chunks(s.shape[1], s.shape[2] * s.dtype.itemsize, ICI_CHUNK_BYTES) for s in sums]
    n_sem = 3 * sum(len(ch) for ch in chunks)
    grid = (nb, seq // tm)
    ntile = kout // LANE if npd else 0

    def body(*refs):
        n_own = 2 * nd + nacc
        n_in = n_own + 2 * npd
        sum_refs, o_ref, got_refs = refs[n_in:n_in + ns], refs[n_in + ns], refs[n_in + ns + 1:n_in + 2 * ns + 1]
        tile_refs = refs[n_in + 2 * ns + 1:n_in + 2 * ns + 1 + ntile]

        def copies():
            send_sems, recv_sems = refs[-2], refs[-1]
            x, y, c = _place()
            out = []
            for a in range(ns):
                for s, m in chunks[a]:
                    for j, (cx, cy) in enumerate(_other_chips(x, y)):
                        k = len(out)
                        out.append(_remote(sum_refs[a].at[2 * cx + cy, pl.ds(s, m)], got_refs[a].at[j, pl.ds(s, m)],
                                           send_sems.at[k], recv_sems.at[k], (cx, cy, c)))
            return out

        if ns:
            @pl.when((pl.program_id(0) == 0) & (pl.program_id(1) == 0))
            def _():
                for cp in copies():
                    cp.start()

        v = None
        for dh_ref, w_ref in zip(refs[:nd], refs[nd:2 * nd]):
            dh = dh_ref[...].astype(BF16)
            if w_rows_are_outputs:
                t = jnp.dot(dh, w_ref[...], preferred_element_type=F32)
            else:
                t = lax.dot_general(dh, w_ref[...], NT, preferred_element_type=F32)
            v = t if v is None else v + t
        for a_ref in refs[2 * nd:2 * nd + nacc]:
            v = v + a_ref[...]
        for q_ref, w_ref in zip(refs[n_own:n_own + npd], refs[n_own + npd:n_in]):
            d, per, n = q_ref.shape
            dh = q_ref[...].reshape(d * per, n).astype(BF16)
            if w_rows_are_outputs:
                t = jnp.dot(dh, w_ref[...], preferred_element_type=F32)
            else:
                t = lax.dot_general(dh, w_ref[...], NT, preferred_element_type=F32)
            v = v + _natural_value(t, d, tile_refs)
        o_ref[...] = v

        if ns:
            @pl.when((pl.program_id(0) == grid[0] - 1) & (pl.program_id(1) == grid[1] - 1))
            def _():
                for cp in copies():
                    cp.wait()

    out = _pcall(
        body, name=name, grid=grid,
        in_specs=[_tok_spec(tm, dh.shape[-1]) for dh in dhs] + [_whole(w, True) for w in ws]
        + [_tok_spec(tm, kout)] * nacc
        + [pl.BlockSpec((None, q.shape[1], tm // q.shape[1], q.shape[3]), lambda b, i: (b, 0, i, 0))
           for q, _ in own_order_dhs]
        + [_whole(w, True) for _, w in own_order_dhs]
        + [ANY] * ns,
        out_specs=[_tok_spec(tm, kout)] + [ANY] * ns,
        out_shape=[jax.ShapeDtypeStruct((nb, seq, kout), F32)]
        + [jax.ShapeDtypeStruct((3,) + s.shape[1:], s.dtype) for s in sums],
        input_output_aliases={2 * nd: 0} if nacc else {},
        scratch_shapes=[pltpu.VMEM((tm, LANE), F32)] * ntile
        + ([pltpu.SemaphoreType.DMA((n_sem,)), pltpu.SemaphoreType.DMA((n_sem,))] if ns else []),
        compiler_params=pltpu.CompilerParams(
            dimension_semantics=("arbitrary", "arbitrary") if ns else ("parallel", "parallel"),
            vmem_limit_bytes=VMEM_LIMIT_BYTES, has_side_effects=bool(ns)),
    )(*dhs, *ws, *accs, *[q for q, _ in own_order_dhs], *[w for _, w in own_order_dhs], *sums)
    return (out[0], list(out[1:])) if ns else out[0]


def _dw(a3, dhs, out_dtype, name, rows_are_outputs=False):
    nb, seq, kdim = a3.shape
    nd = len(dhs)
    grid = (nb, seq // MM_TM)
    shapes = [(dh.shape[-1], kdim) if rows_are_outputs else (kdim, dh.shape[-1]) for dh in dhs]

    def body(*refs):
        b, i = pl.program_id(0), pl.program_id(1)
        dh_refs, o_refs, acc_refs = refs[1:1 + nd], refs[1 + nd:1 + 2 * nd], refs[1 + 2 * nd:]

        @pl.when((b == 0) & (i == 0))
        def _():
            for acc_ref in acc_refs:
                acc_ref[...] = jnp.zeros_like(acc_ref)

        a = refs[0][...].astype(BF16)
        for dh_ref, acc_ref in zip(dh_refs, acc_refs):
            dh = dh_ref[...].astype(BF16)
            acc_ref[...] += lax.dot_general(*((dh, a) if rows_are_outputs else (a, dh)), TN,
                                            preferred_element_type=F32)

        @pl.when((b == grid[0] - 1) & (i == grid[1] - 1))
        def _():
            for o_ref, acc_ref in zip(o_refs, acc_refs):
                o_ref[...] = acc_ref[...].astype(out_dtype)

    return _pcall(
        body, name=name, grid=grid,
        in_specs=[_tok_spec(MM_TM, kdim)] + [_tok_spec(MM_TM, dh.shape[-1]) for dh in dhs],
        out_specs=[pl.BlockSpec(s, lambda b, i: (0, 0)) for s in shapes],
        out_shape=[jax.ShapeDtypeStruct(s, out_dtype) for s in shapes],
        scratch_shapes=[pltpu.VMEM(s, F32) for s in shapes],
        compiler_params=_params("arbitrary", "arbitrary"),
    )(a3, *dhs)


def _dw_stacked(groups, out_dtype, name):
    pairs = [pr for g in groups for pr in g]
    nb, seq, _ = pairs[0][0].shape
    npair, ng = len(pairs), len(groups)
    grid = (nb, seq // MM_TM)

    def body(*refs):
        b, i = pl.program_id(0), pl.program_id(1)
        o_refs, acc_refs = refs[2 * npair:2 * npair + ng], refs[2 * npair + ng:]

        @pl.when((b == 0) & (i == 0))
        def _():
            for acc_ref in acc_refs:
                acc_ref[...] = jnp.zeros_like(acc_ref)

        for j, acc_ref in enumerate(acc_refs):
            acc_ref[...] += lax.dot_general(refs[2 * j][...].astype(BF16), refs[2 * j + 1][...].astype(BF16), TN,
                                            preferred_element_type=F32)

        @pl.when((b == grid[0] - 1) & (i == grid[1] - 1))
        def _():
            accs = iter(acc_refs)
            for o_ref, g in zip(o_refs, groups):
                row = 0
                for a3, _ in g:
                    o_ref[row:row + a3.shape[-1], :] = next(accs)[...].astype(out_dtype)
                    row += a3.shape[-1]

    shapes = [(sum(a3.shape[-1] for a3, _ in g), g[0][1].shape[-1]) for g in groups]
    return _pcall(
        body, name=name, grid=grid,
        in_specs=[_tok_spec(MM_TM, t.shape[-1]) for pr in pairs for t in pr],
        out_specs=[pl.BlockSpec(s, lambda b, i: (0, 0)) for s in shapes],
        out_shape=[jax.ShapeDtypeStruct(s, out_dtype) for s in shapes],
        scratch_shapes=[pltpu.VMEM((a3.shape[-1], dh.shape[-1]), F32) for a3, dh in pairs],
        compiler_params=_params("arbitrary", "arbitrary"),
    )(*[t for pr in pairs for t in pr])


def _qkv_rows(g):
    return [(part * QKV_W + g * ATT_OUT + hp * LANE, LANE) for hp in range(ATT_OUT // LANE) for part in range(3)]


XBC_START = 3 * QKV_W + ATT_OUT + D_INNER
GROUP_CH = GROUP_SSM_HEADS * HEAD_DIM
XBC_GROUP = GROUP_CH + 2 * D_STATE
CONV_DIM = SSM_GROUPS * XBC_GROUP


def _xbc_ranges():
    out = []
    for g in range(SSM_GROUPS):
        out += [(g * GROUP_CH, GROUP_CH), (D_INNER + g * D_STATE, D_STATE),
                (D_INNER + SSM_GROUPS * D_STATE + g * D_STATE, D_STATE)]
    return out


def _join_last(parts):
    widths = [t.shape[-1] for t in parts]
    total, lead = sum(widths), [(0, 0)] * (parts[0].ndim - 1)
    starts = np.cumsum([0] + widths)
    return sum(jnp.pad(t, lead + [(int(s), total - int(s) - w)]) for t, s, w in zip(parts, starts, widths))


def _xbc_group_order(t):
    return _join_last([t[..., s:s + n] for s, n in _xbc_ranges()])


def _xbc_reference_order(t):
    g = lambda off, n: [t[..., k * XBC_GROUP + off:k * XBC_GROUP + off + n] for k in range(SSM_GROUPS)]
    return _join_last(g(0, GROUP_CH) + g(GROUP_CH, D_STATE) + g(GROUP_CH + D_STATE, D_STATE))


def _segments():
    one = lambda name, start, rows: (name, [(start, rows)], max(rows, LANE))
    return [("qkv%d" % g, _qkv_rows(g), QKV_W) for g in range(3)] + [
        one("gatt", 3 * QKV_W, ATT_OUT), one("z", 3 * QKV_W + ATT_OUT, D_INNER),
        ("xbc", [(XBC_START + s, n) for s, n in _xbc_ranges()], CONV_DIM), one("dt", DT_COL, SSM_HEADS),
        one("gm", DT_COL + SSM_HEADS, 2 * D_MODEL), one("gp", DT_COL + SSM_HEADS + 2 * D_MODEL, D_MODEL)]


LAYOUT_TC = 256
NAT_GROUPS = ((("gatt", "z", "dt", "gp"), 512), (("xbc", "gm"), 512))
DW_GROUPS = (("gatt", "z", "dt", "gp"), ("xbc",), ("gm",))


def _assemble(win):
    segs = _segments()

    def body(win_ref, *outs):
        def pieces(start, rows):
            t, end = start, start + rows
            while t < end:
                k = min(t // SHARD_COLS, 3)
                shard_end = (k + 1) * SHARD_COLS
                if k < 3 and shard_end % ROW_TILE and t == shard_end - shard_end % ROW_TILE:
                    lo = t - _win_start(k)
                    yield win_ref[k, lo:lo + ROW_TILE, :] + win_ref[k + 1, 0:ROW_TILE, :]
                    t += ROW_TILE
                    continue
                upto = min(end, shard_end - shard_end % ROW_TILE if k < 3 else end)
                yield win_ref[k, t - _win_start(k):upto - _win_start(k), :]
                t = upto

        for (_, ranges, total), o_ref in zip(segs, outs):
            off = 0
            for start, rows in ranges:
                for part in pieces(start, rows):
                    o_ref[off:off + part.shape[0], :] = part
                    off += part.shape[0]
            if off < total:
                o_ref[off:total, :] = jnp.zeros((total - off, o_ref.shape[1]), BF16)

    outs = _pcall(
        body, name="assemble_w_in", grid=(D_MODEL // LAYOUT_TC,),
        in_specs=[pl.BlockSpec((4, WIN_ROWS, LAYOUT_TC), lambda i: (0, 0, i))],
        out_specs=[pl.BlockSpec((total, LAYOUT_TC), lambda i: (0, i)) for _, _, total in segs],
        out_shape=[jax.ShapeDtypeStruct((total, D_MODEL), BF16) for _, _, total in segs],
        compiler_params=_params("parallel"),
    )(win)
    return {name: o for (name, _, _), o in zip(segs, outs)}


def _pack(dsegs):
    segs = _segments()

    def body(*refs):
        ins, o_ref = refs[:-1], refs[-1]
        tail = IN_COLS - _win_start(3)
        o_ref[3, tail:, :] = jnp.zeros((WIN_ROWS - tail, o_ref.shape[2]), BF16)
        for (_, ranges, _), s_ref in zip(segs, ins):
            off = 0
            for start, rows in ranges:
                for k in range(4):
                    lo = _win_start(k)
                    a, b = max(start, lo), min(start + rows, lo + WIN_ROWS)
                    if a < b:
                        o_ref[k, a - lo:b - lo, :] = s_ref[off + a - start:off + b - start, :]
                off += rows

    return _pcall(
        body, name="pack_dw_in", grid=(D_MODEL // LAYOUT_TC,),
        in_specs=[pl.BlockSpec((total, LAYOUT_TC), lambda i: (0, i)) for _, _, total in segs],
        out_specs=pl.BlockSpec((4, WIN_ROWS, LAYOUT_TC), lambda i: (0, 0, i)),
        out_shape=jax.ShapeDtypeStruct((4, WIN_ROWS, D_MODEL), BF16),
        compiler_params=_params("parallel"),
    )(*[dsegs[name] for name, _, _ in segs])


def _shard_to_window(shard_t, k):
    def at(off):
        return lambda w: jnp.pad(w.astype(BF16), ((off, WIN_ROWS - SHARD_COLS - off), (0, 0)))

    return lax.cond(k % 2 == 1, at(_win_offset(1)), at(_win_offset(0)), shard_t)


def _window_to_shard(win, k):
    return lax.dynamic_slice(win, ((k % 2) * _win_offset(1), 0), (SHARD_COLS, D_MODEL))


def _bucket_maps():
    qi = np.arange(8)[:, None]
    kj = np.arange(2 * BAND)[None, :]
    delta = qi + BAND - kj
    maps = []
    for window, dil in PATTERNS:
        valid = (delta >= 0) & (delta <= window // dil)
        dist = np.maximum(delta, 0) * dil
        max_exact = NUM_BUCKETS // 2
        d_f = np.maximum(dist, 1).astype(np.float32)
        large = max_exact + (np.log(d_f / np.float32(max_exact)) / np.float32(math.log(MAX_DISTANCE / max_exact))
                             * np.float32(NUM_BUCKETS - max_exact)).astype(np.int32)
        large = np.minimum(large, NUM_BUCKETS - 1)
        bucket = np.where(dist < max_exact, dist, large)
        maps.append(np.where(valid, bucket, -1).astype(np.int32))
    return np.stack(maps)


def _bias_tables(rel_bias, bmaps):
    def body(rb_ref, bm_ref, o_ref):
        g = pl.program_id(0)
        bm = bm_ref[...]
        for hh in range(GROUP_HEADS):
            acc = jnp.full(bm.shape, NEG, F32)
            for b in range(NUM_BUCKETS):
                acc = jnp.where(bm == b, rb_ref[b, g * GROUP_HEADS + hh], acc)
            for a in range(BAND // 8):
                o_ref[hh, 8 * a:8 * a + 8, :] = acc if a == 0 else pltpu.roll(acc, 8 * a, 1)

    return _pcall(
        body, name="bias_tables", grid=(3,),
        in_specs=[pl.BlockSpec(memory_space=pltpu.SMEM),
                  pl.BlockSpec((None, 8, 2 * BAND), lambda g: (g, 0, 0))],
        out_specs=pl.BlockSpec((GROUP_HEADS, BAND, 2 * BAND), lambda g: (g, 0, 0)),
        out_shape=jax.ShapeDtypeStruct((3 * GROUP_HEADS, BAND, 2 * BAND), F32),
        compiler_params=_params("parallel"),
    )(rel_bias, bmaps)


def _bias_grad(dbias, bmaps):
    def body(db_ref, bm_ref, o_ref):
        bm = bm_ref[...]
        lane = lax.broadcasted_iota(jnp.int32, (1, LANE), 1)
        for hh in range(GROUP_HEADS):
            db = db_ref[hh, 0:8, :]
            for a in range(1, BAND // 8):
                db = db + pltpu.roll(db_ref[hh, 8 * a:8 * a + 8, :], 2 * BAND - 8 * a, 1)
            vec = jnp.zeros((1, LANE), F32)
            for b in range(NUM_BUCKETS):
                s = jnp.sum(jnp.where(bm == b, db, 0.0), keepdims=True)
                vec = jnp.where(lane == b, s, vec)
            o_ref[hh] = vec

    return _pcall(
        body, name="bias_grad", grid=(3,),
        in_specs=[pl.BlockSpec((GROUP_HEADS, BAND, 2 * BAND), lambda g: (g, 0, 0)),
                  pl.BlockSpec((None, 8, 2 * BAND), lambda g: (g, 0, 0))],
        out_specs=pl.BlockSpec((GROUP_HEADS, 1, LANE), lambda g: (g, 0, 0)),
        out_shape=jax.ShapeDtypeStruct((3 * GROUP_HEADS, 1, LANE), F32),
        compiler_params=_params("parallel"),
    )(dbias, bmaps)


def _rows(n):
    if isinstance(n, int):
        return pl.ds(n * BAND, BAND)
    return pl.ds(pl.multiple_of(n * BAND, BAND), BAND)


def _for_blocks(blocks, nblk, per, carry):
    carry = blocks([0], carry, False)
    start = 1 + (nblk - 1) % per
    for n in range(1, start):
        carry = blocks([n], carry, True)
    trips = (nblk - start) // per
    if trips > 0:
        carry = lax.fori_loop(
            0, trips, lambda t, c: blocks([start + t * per + u for u in range(per)], c, True), carry)
    return carry


def _pairs_per_step(d):
    return {1: 3, 4: 6, 16: 6}[d]


def _bias_spec(group, hps):
    first = group * GROUP_HEADS // (2 * hps)
    return pl.BlockSpec((2 * hps, BAND, 2 * BAND), lambda hp, b, r: (first + hp, 0, 0))


def _attn_fwd(qkv4, bias, group, d, name):
    nb, _, sub, _ = qkv4.shape
    nblk = sub // BAND
    scale = HEAD_DIM ** -0.5
    npair = ATT_OUT // LANE
    hps = _pairs_per_step(d)
    compact = d > 1

    def body(qkv_ref, bias_ref, o_ref, l_ref):
        def blocks(ns, carry, with_prev):
            chains = [(bi, i, h) for bi in range(len(ns)) for i in range(hps) for h in range(2)]
            first_head = lax.broadcasted_iota(jnp.int32, (BAND, LANE), 1) < HEAD_DIM
            pair = lambda n, i, part: qkv_ref[_rows(n), (3 * i + part) * LANE:(3 * i + part + 1) * LANE]
            scores = []
            for bi, i, h in chains:
                n = ns[bi]
                qp = pair(n, i, 0) * scale
                q = jnp.where(first_head if h == 0 else jnp.logical_not(first_head), qp, jnp.zeros_like(qp))
                s_c = lax.dot_general(q, pair(n, i, 1), NT, preferred_element_type=F32) + bias_ref[2 * i + h, :, BAND:]
                s_p = None
                if with_prev:
                    s_p = lax.dot_general(q, pair(n - 1, i, 1), NT,
                                          preferred_element_type=F32) + bias_ref[2 * i + h, :, :BAND]
                scores.append((s_c, s_p))
            probs = []
            for s_c, s_p in scores:
                m = jnp.max(s_c, -1, keepdims=True)
                if with_prev:
                    m = jnp.maximum(m, jnp.max(s_p, -1, keepdims=True))
                e_c = jnp.exp(s_c - m)
                den = jnp.sum(e_c, -1, keepdims=True)
                e_p = None
                if with_prev:
                    e_p = jnp.exp(s_p - m)
                    den = den + jnp.sum(e_p, -1, keepdims=True)
                    e_p = e_p.astype(BF16)
                probs.append((e_c.astype(BF16), e_p, den, m))
            outs = {}
            for (bi, i, h), (e_c, e_p, den, m) in zip(chains, probs):
                n = ns[bi]
                acc = jnp.dot(e_c, pair(n, i, 2), preferred_element_type=F32)
                if with_prev:
                    acc = acc + jnp.dot(e_p, pair(n - 1, i, 2), preferred_element_type=F32)
                outs[(bi, i, h)] = (acc / den, m + jnp.log(den))
            lane = lax.broadcasted_iota(jnp.int32, (BAND, LANE), 1)
            for bi, n in enumerate(ns):
                per_head = jnp.zeros((BAND, LANE), F32)
                for i in range(hps):
                    o_ref[_rows(n), i * LANE:(i + 1) * LANE] = jnp.where(first_head, outs[(bi, i, 0)][0],
                                                                         outs[(bi, i, 1)][0])
                    if compact:
                        for h in range(2):
                            per_head = jnp.where(lane == 2 * i + h, outs[(bi, i, h)][1], per_head)
                    else:
                        l_ref[_rows(n), i * LANE:(i + 1) * LANE] = jnp.where(first_head, outs[(bi, i, 0)][1],
                                                                             outs[(bi, i, 1)][1])
                if compact:
                    l_ref[_rows(n), :] = per_head
            return carry

        _for_blocks(blocks, nblk, 2 if hps == 1 else 1, 0)

    in_specs = [pl.BlockSpec((None, None, sub, 3 * LANE * hps), lambda hp, b, r: (b, r, 0, hp)),
                _bias_spec(group, hps)]
    if compact:
        return _pcall(
            body, name=name, grid=(1, nb, d), in_specs=in_specs,
            out_specs=[pl.BlockSpec((None, None, sub, ATT_OUT), lambda hp, b, r: (b, r, 0, 0)),
                       pl.BlockSpec((None, None, sub, LANE), lambda hp, b, r: (b, r, 0, 0))],
            out_shape=[jax.ShapeDtypeStruct((nb, d, sub, ATT_OUT), F32), jax.ShapeDtypeStruct((nb, d, sub, LANE), F32)],
            compiler_params=_params("parallel", "parallel", "parallel"),
        )(qkv4, bias)
    ospec = pl.BlockSpec((None, sub, hps * LANE), lambda hp, b, r: (b, 0, r * (npair // hps) + hp))
    return _pcall(
        body, name=name, grid=(npair // hps, nb, d), in_specs=in_specs, out_specs=[ospec, ospec],
        out_shape=[jax.ShapeDtypeStruct((nb, sub, d * ATT_OUT), F32)] * 2,
        compiler_params=_params("parallel", "parallel", "parallel"),
    )(qkv4, bias)


STAT_LSE_LANE = 16


def _attn_bwd(qkv4, bias, group, cotangent, d, name):
    nb, _, sub, _ = qkv4.shape
    nblk = sub // BAND
    scale = HEAD_DIM ** -0.5
    npair = ATT_OUT // LANE
    hps = _pairs_per_step(d)
    compact = d > 1

    def body(qkv_ref, bias_ref, *rest):
        do_ref, dqkv_ref, db_ref = rest[0], rest[-2], rest[-1]
        b, r = pl.program_id(1), pl.program_id(2)

        @pl.when((b == 0) & (r == 0))
        def _():
            db_ref[...] = jnp.zeros_like(db_ref)

        def blocks(ns, carry, with_prev):
            sides = (0, 1) if with_prev else (0,)
            chains = [(bi, i, h, sd) for bi in range(len(ns)) for i in range(hps) for h in range(2) for sd in sides]
            first_head = lax.broadcasted_iota(jnp.int32, (BAND, LANE), 1) < HEAD_DIM
            own = lambda h, t: jnp.where(first_head if h == 0 else jnp.logical_not(first_head), t, jnp.zeros_like(t))
            pair = lambda rows, i, part: qkv_ref[rows, (3 * i + part) * LANE:(3 * i + part + 1) * LANE]
            key_rows = lambda bi, sd: _rows(ns[bi] - sd)
            qs = {}
            for bi in range(len(ns)):
                for i in range(hps):
                    q_pair = pair(_rows(ns[bi]), i, 0) * scale
                    do = do_ref[_rows(ns[bi]), i * LANE:(i + 1) * LANE]
                    do16 = do.astype(BF16)
                    for h in range(2):
                        if compact:
                            st_ref, head = rest[1], 2 * i + h
                            ebar = st_ref[_rows(ns[bi]), head:head + 1]
                            lcol = st_ref[_rows(ns[bi]), STAT_LSE_LANE + head:STAT_LSE_LANE + head + 1]
                        else:
                            ebar = jnp.sum(own(h, do * rest[1][_rows(ns[bi]), i * LANE:(i + 1) * LANE]), -1, keepdims=True)
                            lcol = rest[2][_rows(ns[bi]), i * LANE + h * HEAD_DIM:i * LANE + h * HEAD_DIM + 1]
                        qs[(bi, i, h)] = (own(h, q_pair), q_pair, own(h, do16), do16, ebar, lcol)
            raw = []
            for bi, i, h, sd in chains:
                q, _, do_h, _, _, _ = qs[(bi, i, h)]
                bias_blk = bias_ref[2 * i + h, :, :BAND] if sd else bias_ref[2 * i + h, :, BAND:]
                s = lax.dot_general(q, pair(key_rows(bi, sd), i, 1), NT, preferred_element_type=F32) + bias_blk
                dp = lax.dot_general(do_h, pair(key_rows(bi, sd), i, 2), NT, preferred_element_type=F32)
                raw.append((s, dp))
            soft = []
            for (bi, i, h, sd), (s, dp) in zip(chains, raw):
                ebar, lcol = qs[(bi, i, h)][4:]
                p = jnp.exp(s - lcol)
                ds = p * (dp - ebar)
                if sd:
                    db_ref[2 * i + h, :, :BAND] += ds
                else:
                    db_ref[2 * i + h, :, BAND:] += ds
                soft.append((p.astype(BF16), ds.astype(BF16)))
            grads = {}
            for (bi, i, h, sd), (p16, ds16) in zip(chains, soft):
                _, q_pair, _, do16 = qs[(bi, i, h)][:4]
                grads[(bi, i, h, sd)] = (
                    jnp.dot(ds16, pair(key_rows(bi, sd), i, 1), preferred_element_type=F32),
                    lax.dot_general(ds16, q_pair, TN, preferred_element_type=F32),
                    lax.dot_general(p16, do16, TN, preferred_element_type=F32))
            both = lambda bi, i, sd, which: jnp.where(first_head, grads[(bi, i, 0, sd)][which],
                                                      grads[(bi, i, 1, sd)][which])
            carry = list(carry) if carry is not None else None
            for bi, n in enumerate(ns):
                for i in range(hps):
                    base = 3 * LANE * i
                    dq = both(bi, i, 0, 0)
                    if with_prev:
                        dq = dq + both(bi, i, 1, 0)
                        dqkv_ref[_rows(n - 1), base + LANE:base + 2 * LANE] = (
                            carry[2 * i] + both(bi, i, 1, 1)).astype(BF16)
                        dqkv_ref[_rows(n - 1), base + 2 * LANE:base + 3 * LANE] = (
                            carry[2 * i + 1] + both(bi, i, 1, 2)).astype(BF16)
                    dqkv_ref[_rows(n), base:base + LANE] = (dq * scale).astype(BF16)
                carry = [t for i in range(hps) for t in (both(bi, i, 0, 1), both(bi, i, 0, 2))]
            return tuple(carry)

        carry = _for_blocks(blocks, nblk, 2 if hps == 1 else 1, None)
        for i in range(hps):
            base = 3 * LANE * i
            dqkv_ref[_rows(nblk - 1), base + LANE:base + 2 * LANE] = carry[2 * i].astype(BF16)
            dqkv_ref[_rows(nblk - 1), base + 2 * LANE:base + 3 * LANE] = carry[2 * i + 1].astype(BF16)

    qspec = pl.BlockSpec((None, None, sub, 3 * LANE * hps), lambda hp, b, r: (b, r, 0, hp))
    bspec = pl.BlockSpec((2 * hps, BAND, 2 * BAND), lambda hp, b, r: (hp, 0, 0))
    if compact:
        cspecs = [pl.BlockSpec((None, None, sub, ATT_OUT), lambda hp, b, r: (b, r, 0, 0)),
                  pl.BlockSpec((None, None, sub, LANE), lambda hp, b, r: (b, r, 0, 0))]
    else:
        cspecs = [pl.BlockSpec((None, sub, hps * LANE), lambda hp, b, r: (b, 0, r * (npair // hps) + hp))] * 3
    return _pcall(
        body, name=name, grid=(npair // hps, nb, d),
        in_specs=[qspec, _bias_spec(group, hps)] + cspecs, out_specs=[qspec, bspec],
        out_shape=[jax.ShapeDtypeStruct(qkv4.shape, BF16),
                   jax.ShapeDtypeStruct((GROUP_HEADS, BAND, 2 * BAND), F32)],
        compiler_params=_params("parallel", "arbitrary", "arbitrary"),
    )(qkv4, bias, *cotangent)


def _head_lanes(first_lane, one_channel):
    c = lax.broadcasted_iota(jnp.int32, (ATT_OUT, LANE), 0)
    lane = lax.broadcasted_iota(jnp.int32, (ATT_OUT, LANE), 1)
    hit = lane == first_lane + c // HEAD_DIM
    if one_channel:
        hit = hit & (c % HEAD_DIM == 0)
    return hit.astype(BF16)


def _exact_dot(v, m01, dims=None):
    parts = _split3(v)
    if dims is None:
        dot = lambda t: jnp.dot(t, m01, preferred_element_type=F32)
    else:
        dot = lambda t: lax.dot_general(t, m01, dims, preferred_element_type=F32)
    return (dot(parts[0]) + dot(parts[1])) + dot(parts[2])


def _store_own_order(value, tile_refs, out_ref):
    d, per, width = out_ref.shape
    for j in range(width // LANE):
        tile_refs[j][...] = value[:, j * LANE:(j + 1) * LANE]
    for r in range(d):
        rows = pl.ds(r, per, stride=d)
        for j in range(width // LANE):
            out_ref[r, :, j * LANE:(j + 1) * LANE] = tile_refs[j][rows, :].astype(out_ref.dtype)


def _natural_rows(p_ref, tile_refs):
    d, per, width = p_ref.shape
    for r in range(d):
        rows = pl.ds(r, per, stride=d)
        for j in range(width // LANE):
            tile_refs[j][rows, :] = p_ref[r, :, j * LANE:(j + 1) * LANE]
    return jnp.concatenate([tile_refs[j][...] for j in range(width // LANE)], axis=1)


def _natural_value(value, d, tile_refs):
    total, width = value.shape
    per = total // d
    for r in range(d):
        rows = pl.ds(r, per, stride=d)
        for j in range(width // LANE):
            tile_refs[j][rows, :] = value[r * per:(r + 1) * per, j * LANE:(j + 1) * LANE]
    return jnp.concatenate([tile_refs[j][...] for j in range(width // LANE)], axis=1)


def _combine_fwd(o0, l0, dilated, gatt):
    nb, seq, _ = gatt.shape
    tm = 512
    ntile = ATT_OUT // LANE

    def body(o0_ref, l0_ref, o1_ref, l1_ref, o2_ref, l2_ref, g_ref, oa_ref, oatt_ref, lse_ref, *tile_refs):
        spread = _head_lanes(0, False)
        l0v = l0_ref[...]
        l1v = _exact_dot(_natural_rows(l1_ref, tile_refs), spread, NT)
        l2v = _exact_dot(_natural_rows(l2_ref, tile_refs), spread, NT)
        m = jnp.maximum(jnp.maximum(l0v, l1v), l2v)
        tot = m + jnp.log(jnp.exp(l0v - m) + jnp.exp(l1v - m) + jnp.exp(l2v - m))
        o = jnp.exp(l0v - tot) * o0_ref[...]
        o = o + jnp.exp(l1v - tot) * _natural_rows(o1_ref, tile_refs)
        o = o + jnp.exp(l2v - tot) * _natural_rows(o2_ref, tile_refs)
        g = g_ref[...]
        oa_ref[...] = (o * (g * _sigmoid(g))).astype(BF16)
        oatt_ref[...] = o
        lse_ref[...] = tot

    spec = pl.BlockSpec((None, tm, ATT_OUT), lambda b, i: (b, i, 0))
    own = lambda t: pl.BlockSpec((None, t.shape[1], tm // t.shape[1], t.shape[3]), lambda b, i: (b, 0, i, 0))
    (o1, l1), (o2, l2) = dilated
    return _pcall(
        body, name="attn_combine", grid=(nb, seq // tm),
        in_specs=[spec, spec, own(o1), own(l1), own(o2), own(l2), spec], out_specs=[spec] * 3,
        out_shape=[jax.ShapeDtypeStruct((nb, seq, ATT_OUT), BF16), jax.ShapeDtypeStruct((nb, seq, ATT_OUT), F32),
                   jax.ShapeDtypeStruct((nb, seq, ATT_OUT), F32)],
        scratch_shapes=[pltpu.VMEM((tm, LANE), F32)] * ntile,
        compiler_params=_params("parallel", "parallel"),
    )(o0, l0, o1, l1, o2, l2, gatt)


def _combine_bwd(dya16, w_bra, gatt, o_att, lse, dilations):
    nb, seq, _ = gatt.shape
    tm = 512

    def body(dya_ref, w_ref, g_ref, o_ref, l_ref, do_ref, dg_ref, *rest):
        ntile = ATT_OUT // LANE
        outs, tile_refs = rest[:-ntile], rest[-ntile:]
        doa = lax.dot_general(dya_ref[...], w_ref[...], NT, preferred_element_type=F32)
        g = g_ref[...]
        sg = _sigmoid(g)
        do = doa * (g * sg)
        do_ref[...] = do
        stats = (_exact_dot(do * o_ref[...], _head_lanes(0, False))
                 + _exact_dot(l_ref[...], _head_lanes(STAT_LSE_LANE, True)))
        dg_ref[...] = (doa * o_ref[...] * (sg * (1.0 + g * (1.0 - sg)))).astype(BF16)
        for k in range(len(dilations)):
            _store_own_order(do, tile_refs, outs[2 * k])
            _store_own_order(stats, tile_refs, outs[2 * k + 1])

    spec = pl.BlockSpec((None, tm, ATT_OUT), lambda b, i: (b, i, 0))
    own = lambda d, width: pl.BlockSpec((None, d, tm // d, width), lambda b, i: (b, 0, i, 0))
    outs = _pcall(
        body, name="attn_combine_bwd", grid=(nb, seq // tm),
        in_specs=[_tok_spec(tm, D_MODEL), _whole(w_bra, True)] + [spec] * 3,
        out_specs=[spec, spec] + [own(d, w) for d in dilations for w in (ATT_OUT, LANE)],
        out_shape=[jax.ShapeDtypeStruct((nb, seq, ATT_OUT), F32), jax.ShapeDtypeStruct((nb, seq, ATT_OUT), BF16)]
        + [jax.ShapeDtypeStruct((nb, d, seq // d, w), t) for d in dilations for w, t in ((ATT_OUT, BF16), (LANE, F32))],
        scratch_shapes=[pltpu.VMEM((tm, LANE), F32)] * (ATT_OUT // LANE),
        compiler_params=_params("parallel", "parallel"),
    )(dya16, w_bra, gatt, o_att, lse)
    return outs[0], outs[1], outs[2:]


CONV_TM = 1024
CONV_TC = 1024


def _shift_down(cur, halo, k):
    rolled = pltpu.roll(cur, k, 0)
    hro = pltpu.roll(halo, k, 0)
    row = lax.broadcasted_iota(jnp.int32, hro.shape, 0)
    return jnp.concatenate([jnp.where(row < k, hro, rolled[:8]), rolled[8:]], axis=0)


def _shift_up(cur, halo, k):
    n = cur.shape[0]
    rolled = pltpu.roll(cur, n - k, 0)
    hro = pltpu.roll(halo, 8 - k, 0)
    row = lax.broadcasted_iota(jnp.int32, hro.shape, 0)
    return jnp.concatenate([rolled[:n - 8], jnp.where(row >= 8 - k, hro, rolled[n - 8:])], axis=0)


def _conv_pre(cur, halo, w_ref, b_ref):
    acc = cur * w_ref[3:4, :] + b_ref[...]
    for k in range(1, 4):
        acc = acc + _shift_down(cur, halo, k) * w_ref[3 - k:4 - k, :]
    return acc


def _conv_specs(seq):
    nblk = seq // CONV_TM
    cur = pl.BlockSpec((None, CONV_TM, CONV_TC), lambda cb, b, i: (b, i, cb))
    prev = pl.BlockSpec((None, 8, CONV_TC), lambda cb, b, i: (b, jnp.maximum(i * (CONV_TM // 8) - 1, 0), cb))
    nxt = pl.BlockSpec((None, 8, CONV_TC),
                       lambda cb, b, i: (b, jnp.minimum((i + 1) * (CONV_TM // 8), seq // 8 - 1), cb))
    wspec = pl.BlockSpec((4, CONV_TC), lambda cb, b, i: (0, cb))
    bspec = pl.BlockSpec((1, CONV_TC), lambda cb, b, i: (0, cb))
    return nblk, cur, prev, nxt, wspec, bspec


def _conv_fwd(xin, w4, bias, name):
    nb, seq, ch = xin.shape
    _, cur, prev, _, wspec, bspec = _conv_specs(seq)

    def body(x_ref, h_ref, w_ref, b_ref, o_ref):
        halo = jnp.where(pl.program_id(2) > 0, h_ref[...], 0.0)
        pre = _conv_pre(x_ref[...], halo, w_ref, b_ref)
        o_ref[...] = pre * _sigmoid(pre)

    return _pcall(
        body, name=name, grid=(ch // CONV_TC, nb, seq // CONV_TM),
        in_specs=[cur, prev, wspec, bspec], out_specs=cur,
        out_shape=jax.ShapeDtypeStruct(xin.shape, F32),
        compiler_params=_params("parallel", "parallel", "parallel"),
    )(xin, xin, w4, bias)


def _conv_bwd_pre(dact, xin, w4, bias, name):
    nb, seq, ch = xin.shape
    _, cur, prev, _, wspec, bspec = _conv_specs(seq)

    def body(da_ref, x_ref, h_ref, w_ref, b_ref, dp_ref, s_ref):
        b, i = pl.program_id(1), pl.program_id(2)

        @pl.when((b == 0) & (i == 0))
        def _():
            s_ref[...] = jnp.zeros_like(s_ref)

        halo = jnp.where(i > 0, h_ref[...], 0.0)
        x = x_ref[...]
        pre = _conv_pre(x, halo, w_ref, b_ref)
        sg = _sigmoid(pre)
        dpre = da_ref[...] * (sg * (1.0 + pre * (1.0 - sg)))
        dp_ref[...] = dpre
        s_ref[3:4, :] += jnp.sum(dpre * x, 0, keepdims=True)
        for k in range(1, 4):
            s_ref[3 - k:4 - k, :] += jnp.sum(dpre * _shift_down(x, halo, k), 0, keepdims=True)
        s_ref[4:5, :] += jnp.sum(dpre, 0, keepdims=True)

    return _pcall(
        body, name=name, grid=(ch // CONV_TC, nb, seq // CONV_TM),
        in_specs=[cur, cur, prev, wspec, bspec],
        out_specs=[cur, pl.BlockSpec((8, CONV_TC), lambda cb, b, i: (0, cb))],
        out_shape=[jax.ShapeDtypeStruct(xin.shape, F32), jax.ShapeDtypeStruct((8, ch), F32)],
        compiler_params=_params("parallel", "arbitrary", "arbitrary"),
    )(dact, xin, xin, w4, bias)


def _conv_bwd_x(dpre, w4, name):
    nb, seq, ch = dpre.shape
    nblk, cur, _, nxt, wspec, _ = _conv_specs(seq)

    def body(d_ref, n_ref, w_ref, o_ref):
        halo = jnp.where(pl.program_id(2) < nblk - 1, n_ref[...], 0.0)
        cur_v = d_ref[...]
        acc = cur_v * w_ref[3:4, :]
        for j in range(1, 4):
            acc = acc + _shift_up(cur_v, halo, j) * w_ref[3 - j:4 - j, :]
        o_ref[...] = acc.astype(BF16)

    return _pcall(
        body, name=name, grid=(ch // CONV_TC, nb, seq // CONV_TM),
        in_specs=[cur, nxt, wspec], out_specs=cur,
        out_shape=jax.ShapeDtypeStruct(dpre.shape, BF16),
        compiler_params=_params("parallel", "parallel", "parallel"),
    )(dpre, dpre, w4)


def _step_sizes(raw, tb_ref):
    shift = (LANE - GROUP_SSM_HEADS * pl.program_id(0)) % LANE
    v = pltpu.roll(raw + tb_ref[...], shift, 1)
    own = lax.broadcasted_iota(jnp.int32, v.shape, 1) < GROUP_SSM_HEADS
    sp = jnp.maximum(v, 0.0) + jnp.log1p(jnp.exp(-jnp.abs(v)))
    return jnp.where(own, sp, 0.0), jnp.where(own, _sigmoid(v), 0.0)


def _group_lanes(t):
    pads = [(0, 0)] * (t.ndim - 1) + [(0, LANE - GROUP_SSM_HEADS)]
    return jnp.stack([jnp.pad(t[..., GROUP_SSM_HEADS * g:GROUP_SSM_HEADS * (g + 1)], pads) for g in range(SSM_GROUPS)])


def _ungroup_lanes(t):
    return jnp.concatenate([t[g][..., :GROUP_SSM_HEADS] for g in range(SSM_GROUPS)], axis=-1)


def _decays(dt, al_ref):
    row = lax.broadcasted_iota(jnp.int32, (CHUNK, CHUNK), 0)
    col = lax.broadcasted_iota(jnp.int32, (CHUNK, CHUNK), 1)
    tril = (row >= col).astype(BF16)
    triu = (row <= col).astype(BF16)
    arow = -jnp.exp(al_ref[...])
    hi, mid, lo = _split3(dt * arow)
    down = lambda t: jnp.dot(tril, t, preferred_element_type=F32)
    across = lambda t: lax.dot_general(t, triu, TN, preferred_element_type=F32)
    acs = (down(hi) + down(mid)) + down(lo)
    acs_t = (across(hi) + across(mid)) + across(lo)
    return arow, acs, acs_t, row >= col, triu


STEP_CHUNKS = 8


def _ssd_specs(nb, seq):
    nc = seq // CHUNK
    hw = GROUP_SSM_HEADS * HEAD_DIM
    rows, steps = STEP_CHUNKS * CHUNK, nc // STEP_CHUNKS

    def mk(rev):
        cidx = (lambda c: steps - 1 - c) if rev else (lambda c: c)
        wide = pl.BlockSpec((None, rows, hw), lambda g, b, c: (b, cidx(c), g))
        xbc = pl.BlockSpec((None, rows, XBC_GROUP), lambda g, b, c: (b, cidx(c), g))
        lanes = pl.BlockSpec((None, None, rows, LANE), lambda g, b, c: (g, b, cidx(c), 0))
        prev = pl.BlockSpec((None, STEP_CHUNKS, None, D_STATE, hw), lambda g, b, c: (b, cidx(c), g, 0, 0))
        raw = pl.BlockSpec((None, rows, LANE), lambda g, b, c: (b, cidx(c), 0))
        return wide, xbc, lanes, prev, raw

    grow = pl.BlockSpec((None, 1, LANE), lambda g, b, c: (g, 0, 0))
    nwspec = pl.BlockSpec((1, hw), lambda g, b, c: (0, g))
    tbspec = pl.BlockSpec((1, LANE), lambda g, b, c: (0, 0))
    return nc, steps, hw, mk, grow, nwspec, tbspec


def _head_expand():
    hw = GROUP_SSM_HEADS * HEAD_DIM
    r = lax.broadcasted_iota(jnp.int32, (LANE, hw), 0)
    c = lax.broadcasted_iota(jnp.int32, (LANE, hw), 1)
    return ((c // HEAD_DIM) == r).astype(BF16)


def _split3(v):
    hi = v.astype(BF16)
    rest = v - hi.astype(F32)
    mid = rest.astype(BF16)
    return hi, mid, (rest - mid.astype(F32)).astype(BF16)


def _to_channels(v, e):
    hi, mid, lo = _split3(v)
    dot = lambda t: jnp.dot(t, e, preferred_element_type=F32)
    return (dot(hi) + dot(mid)) + dot(lo)


def _to_heads(w, e):
    hi, mid, lo = _split3(w)
    dot = lambda t: lax.dot_general(t, e, (((1,), (1,)), ((), ())), preferred_element_type=F32)
    return (dot(hi) + dot(mid)) + dot(lo)


def _row8(v):
    return jnp.broadcast_to(v, (8, v.shape[1]))


def _ssd_chunk_setup(dt, al_ref, ds_ref):
    arow, acs, acs_t, causal, triu = _decays(dt, al_ref)
    e = _head_expand()
    dtx = _to_channels(dt, e)
    acsx = _to_channels(acs, e)
    lastx = acsx[CHUNK - 1:CHUNK, :]
    dskx = _to_channels(_row8(ds_ref[...]), e)[0:1, :]
    return arow, acs, acs_t, causal, triu, e, dtx, acsx, lastx, dskx


def _ssd_fwd(xbc, dt_raw, dt_bias_row, z, alog_g, dskip_g, normw):
    nb, seq, _ = xbc.shape
    nc, steps, hw, mk, grow, nwspec, tbspec = _ssd_specs(nb, seq)
    wide, xbc_spec, lanes, prev, raw = mk(False)
    tn = (((0,), (0,)), ((), ()))

    def body(xbc_ref, dt_ref, tb_ref, z_ref, al_ref, ds_ref, nw_ref, ys_ref, y_ref, sp_ref, st_ref):
        @pl.when(pl.program_id(2) == 0)
        def _():
            st_ref[...] = jnp.zeros_like(st_ref)

        for ci in range(STEP_CHUNKS):
            chunk(ci, xbc_ref, dt_ref, tb_ref, z_ref, al_ref, ds_ref, nw_ref, ys_ref, y_ref, sp_ref, st_ref)

    def chunk(ci, xbc_ref, dt_ref, tb_ref, z_ref, al_ref, ds_ref, nw_ref, ys_ref, y_ref, sp_ref, st_ref):
        rows = slice(ci * CHUNK, (ci + 1) * CHUNK)
        dt, _ = _step_sizes(dt_ref[rows, :], tb_ref)
        _, acs, acs_t, causal, _, _, dtx, acsx, lastx, dskx = _ssd_chunk_setup(dt, al_ref, ds_ref)
        bmat = xbc_ref[rows, GROUP_CH:GROUP_CH + D_STATE].astype(BF16)
        cmat = xbc_ref[rows, GROUP_CH + D_STATE:].astype(BF16)
        cb = lax.dot_general(cmat, bmat, (((1,), (1,)), ((), ())), preferred_element_type=F32)
        x = xbc_ref[rows, :GROUP_CH]
        xdt = x * dtx
        xdt16 = xdt.astype(BF16)
        first_head = lax.broadcasted_iota(jnp.int32, (CHUNK, LANE), 1) < HEAD_DIM
        pairs = []
        for hp in range(GROUP_SSM_HEADS // 2):
            xp = xdt16[:, hp * LANE:(hp + 1) * LANE]
            two = []
            for j in (2 * hp, 2 * hp + 1):
                lmat = jnp.exp(jnp.where(causal, acs[:, j:j + 1] - acs_t[j:j + 1, :], -jnp.inf))
                two.append(jnp.dot((cb * lmat).astype(BF16), xp, preferred_element_type=F32))
            pairs.append(jnp.where(first_head, two[0], two[1]))
        yd = jnp.concatenate(pairs, axis=1)
        s_prev = st_ref[...]
        s16 = s_prev.astype(BF16)
        sp_ref[ci] = s16
        yo = jnp.dot(cmat, s16, preferred_element_type=F32) * jnp.exp(acsx)
        sts = lax.dot_general(bmat, (xdt * jnp.exp(lastx - acsx)).astype(BF16), tn, preferred_element_type=F32)
        st_ref[...] = s_prev * jnp.exp(lastx) + sts
        y = yd + yo + dskx * x
        zz = z_ref[rows, :]
        u = y * (zz * _sigmoid(zz))
        rn = lax.rsqrt(jnp.mean(u * u, -1, keepdims=True) + RMS_EPS)
        ys_ref[rows, :] = (u * rn * nw_ref[...]).astype(BF16)
        y_ref[rows, :] = y

    return _pcall(
        body, name="ssd_fwd", grid=(SSM_GROUPS, nb, steps),
        in_specs=[xbc_spec, raw, tbspec, wide, grow, grow, nwspec],
        out_specs=[wide, wide, prev],
        out_shape=[jax.ShapeDtypeStruct((nb, seq, D_INNER), BF16), jax.ShapeDtypeStruct((nb, seq, D_INNER), F32),
                   jax.ShapeDtypeStruct((nb, nc, SSM_GROUPS, D_STATE, hw), BF16)],
        scratch_shapes=[pltpu.VMEM((D_STATE, hw), F32)],
        compiler_params=_params("parallel", "parallel", "arbitrary"),
    )(xbc, dt_raw, dt_bias_row, z, alog_g, dskip_g, normw)


def _ssd_bwd(xbc, dt_raw, dt_bias_row, z, y, dys, sprev, alog_g, dskip_g, normw):
    nb, seq, _ = xbc.shape
    nc, steps, hw, mk, grow, nwspec, tbspec = _ssd_specs(nb, seq)
    wide, xbc_spec, lanes, prev, raw = mk(True)
    nt = (((1,), (1,)), ((), ()))
    tn = (((0,), (0,)), ((), ()))

    def body(xbc_ref, dt_ref, tb_ref, z_ref, y_ref, dys_ref, sp_ref, al_ref, ds_ref, nw_ref,
             dxbc_ref, ddt_ref, dz_ref, small_ref, dnw_ref, g_ref):
        b, c = pl.program_id(1), pl.program_id(2)

        @pl.when((b == 0) & (c == 0))
        def _():
            small_ref[...] = jnp.zeros_like(small_ref)
            dnw_ref[...] = jnp.zeros_like(dnw_ref)

        @pl.when(c == 0)
        def _():
            g_ref[...] = jnp.zeros_like(g_ref)

        for ci in reversed(range(STEP_CHUNKS)):
            chunk(ci, xbc_ref, dt_ref, tb_ref, z_ref, y_ref, dys_ref, sp_ref, al_ref, ds_ref, nw_ref,
                  dxbc_ref, ddt_ref, dz_ref, small_ref, dnw_ref, g_ref)

    def chunk(ci, xbc_ref, dt_ref, tb_ref, z_ref, y_ref, dys_ref, sp_ref, al_ref, ds_ref, nw_ref,
              dxbc_ref, ddt_ref, dz_ref, small_ref, dnw_ref, g_ref):
        rows = slice(ci * CHUNK, (ci + 1) * CHUNK)
        yv, zz, dys_v, nw = y_ref[rows, :], z_ref[rows, :], dys_ref[rows, :], nw_ref[...]
        sz = _sigmoid(zz)
        silu = zz * sz
        u = yv * silu
        rn = lax.rsqrt(jnp.mean(u * u, -1, keepdims=True) + RMS_EPS)
        gn = dys_v * nw
        du = rn * gn - u * (rn * rn * rn) * jnp.mean(u * gn, -1, keepdims=True)
        dnw_ref[...] += jnp.sum(dys_v * u * rn, 0, keepdims=True)
        dy = du * silu
        dz_ref[rows, :] = du * yv * (sz * (1.0 + zz * (1.0 - sz)))

        dt, sg = _step_sizes(dt_ref[rows, :], tb_ref)
        arow, acs, acs_t, causal, triu, e, dtx, acsx, lastx, dskx = _ssd_chunk_setup(dt, al_ref, ds_ref)
        dfsx = jnp.exp(acsx)
        dtex = jnp.exp(lastx - acsx)
        bmat = xbc_ref[rows, GROUP_CH:GROUP_CH + D_STATE].astype(BF16)
        cmat = xbc_ref[rows, GROUP_CH + D_STATE:].astype(BF16)
        cb = lax.dot_general(cmat, bmat, nt, preferred_element_type=F32)
        x = xbc_ref[rows, :GROUP_CH]
        xdt = x * dtx
        xdt16 = xdt.astype(BF16)
        xdte = xdt * dtex
        dy16 = dy.astype(BF16)
        dyd = dy * dfsx
        dyd16 = dyd.astype(BF16)
        s16 = sp_ref[ci]
        g = g_ref[...]
        g16 = g.astype(BF16)
        cs = jnp.dot(cmat, s16, preferred_element_type=F32)
        dc_off = lax.dot_general(dyd16, s16, nt, preferred_element_type=F32)
        g_here = lax.dot_general(cmat, dyd16, tn, preferred_element_type=F32)
        bg = jnp.dot(bmat, g16, preferred_element_type=F32)
        db_st = lax.dot_general(xdte.astype(BF16), g16, nt, preferred_element_type=F32)
        ddte_w = bg * xdte
        dcd = _to_heads(_row8(jnp.sum(g * s16.astype(F32), 0, keepdims=True)), e)[0:1, :]
        lane = lax.broadcasted_iota(jnp.int32, (CHUNK, LANE), 1)
        first_head = lane < HEAD_DIM
        sub = lax.broadcasted_iota(jnp.int32, (CHUNK, LANE), 0)
        dacs = jnp.zeros((CHUNK, LANE), F32)
        colsums = jnp.zeros((CHUNK, LANE), F32)
        dcb = jnp.zeros((CHUNK, CHUNK), F32)
        pairs = []
        for hp in range(GROUP_SSM_HEADS // 2):
            xp = xdt16[:, hp * LANE:(hp + 1) * LANE]
            dyp = dy16[:, hp * LANE:(hp + 1) * LANE]
            two = []
            for idx, j in enumerate((2 * hp, 2 * hp + 1)):
                lmat = jnp.exp(jnp.where(causal, acs[:, j:j + 1] - acs_t[j:j + 1, :], -jnp.inf))
                mf = cb * lmat
                dy_h = jnp.where(first_head if idx == 0 else jnp.logical_not(first_head), dyp, jnp.zeros_like(dyp))
                dm = lax.dot_general(dy_h, xp, nt, preferred_element_type=F32)
                two.append(lax.dot_general(mf.astype(BF16), dyp, tn, preferred_element_type=F32))
                wmat = dm * mf
                dcb = dcb + dm * lmat
                dacs = jnp.where(lane == j, jnp.sum(wmat, -1, keepdims=True), dacs)
                colsums = jnp.where(sub == j, jnp.sum(wmat, 0, keepdims=True), colsums)
            pairs.append(jnp.where(first_head, two[0], two[1]))
        dxdt = bg * dtex + jnp.concatenate(pairs, axis=1)
        dacs = dacs - colsums.T + _to_heads(dyd * cs - ddte_w, e)
        cd_row = jnp.exp(acs[CHUNK - 1:CHUNK, :])
        tail = _to_heads(_row8(jnp.sum(ddte_w, 0, keepdims=True)), e)[0:1, :] + dcd * cd_row
        dacs = dacs + jnp.where(sub == CHUNK - 1, tail, 0.0)
        d_hi, d_mid, d_lo = _split3(dacs)
        up = lambda t: jnp.dot(triu, t, preferred_element_type=F32)
        da = (up(d_hi) + up(d_mid)) + up(d_lo)
        ddt_raw = (da * arow + _to_heads(dxdt * x, e)) * sg
        ddt_ref[rows, :] = ddt_raw
        small_ref[0:1, :] += jnp.sum(da * dt, 0, keepdims=True) * arow
        small_ref[1:2, :] += _to_heads(_row8(jnp.sum(dy * x, 0, keepdims=True)), e)[0:1, :]
        small_ref[2:3, :] += jnp.sum(ddt_raw, 0, keepdims=True)
        dcb16 = dcb.astype(BF16)
        dxbc_ref[rows, GROUP_CH + D_STATE:] = dc_off + jnp.dot(dcb16, bmat, preferred_element_type=F32)
        dxbc_ref[rows, GROUP_CH:GROUP_CH + D_STATE] = db_st + lax.dot_general(dcb16, cmat, tn,
                                                                               preferred_element_type=F32)
        dxbc_ref[rows, :GROUP_CH] = dxdt * dtx + dskx * dy
        g_ref[...] = g * jnp.exp(lastx) + g_here

    return _pcall(
        body, name="ssd_bwd", grid=(SSM_GROUPS, nb, steps),
        in_specs=[xbc_spec, raw, tbspec, wide, wide, wide, prev, grow, grow, nwspec],
        out_specs=[xbc_spec, lanes, wide,
                   pl.BlockSpec((None, 8, LANE), lambda g, b, c: (g, 0, 0)), nwspec],
        out_shape=[jax.ShapeDtypeStruct((nb, seq, CONV_DIM), F32),
                   jax.ShapeDtypeStruct((SSM_GROUPS, nb, seq, LANE), F32),
                   jax.ShapeDtypeStruct((nb, seq, D_INNER), F32),
                   jax.ShapeDtypeStruct((SSM_GROUPS, 8, LANE), F32),
                   jax.ShapeDtypeStruct((1, D_INNER), F32)],
        scratch_shapes=[pltpu.VMEM((D_STATE, hw), F32)],
        compiler_params=_params("parallel", "arbitrary", "arbitrary"),
    )(xbc, dt_raw, dt_bias_row, z, y, dys, sprev, alog_g, dskip_g, normw)


EW_TM = 256


def _merge_fwd(oa16, y_ssm16, w_bra, w_brb, gm, bgate):
    nb, seq, _ = oa16.shape

    def body(oa_ref, ys_ref, wa_ref, wb_ref, ga_ref, gb_ref, bg_ref, a_ref, b_ref, o_ref):
        y_a = jnp.dot(oa_ref[...], wa_ref[...], preferred_element_type=F32)
        y_b = jnp.dot(ys_ref[...], wb_ref[...], preferred_element_type=F32)
        a_ref[...] = y_a
        b_ref[...] = y_b
        sa = _sigmoid(ga_ref[...] + bg_ref[0:1, :])
        sb = _sigmoid(gb_ref[...] + bg_ref[1:2, :])
        o_ref[...] = (sa * y_a + sb * y_b).astype(BF16)

    spec = pl.BlockSpec((None, EW_TM, D_MODEL), lambda b, i: (b, i, 0))
    spec1 = pl.BlockSpec((None, EW_TM, D_MODEL), lambda b, i: (b, i, 1))
    return _pcall(
        body, name="merge_fwd", grid=(nb, seq // EW_TM),
        in_specs=[_tok_spec(EW_TM, ATT_OUT), _tok_spec(EW_TM, D_INNER), _whole(w_bra, True), _whole(w_brb, True),
                  spec, spec1, pl.BlockSpec((8, D_MODEL), lambda b, i: (0, 0))],
        out_specs=[spec] * 3,
        out_shape=[jax.ShapeDtypeStruct((nb, seq, D_MODEL), F32)] * 2 + [jax.ShapeDtypeStruct((nb, seq, D_MODEL), BF16)],
        compiler_params=_params("parallel", "parallel"),
    )(oa16, y_ssm16, w_bra, w_brb, gm, gm, bgate)


def _merge_bwd(dpre16, w_out16, y_a, y_b, gm, bgate):
    nb, seq, _ = y_a.shape

    def body(dp_ref, w_ref, a_ref, b_ref, ga_ref, gb_ref, bg_ref, dya_ref, dyb_ref, dg_ref, s_ref):
        @pl.when((pl.program_id(0) == 0) & (pl.program_id(1) == 0))
        def _():
            s_ref[...] = jnp.zeros_like(s_ref)

        dm = lax.dot_general(dp_ref[...], w_ref[...], NT, preferred_element_type=F32)
        sa = _sigmoid(ga_ref[...] + bg_ref[0:1, :])
        sb = _sigmoid(gb_ref[...] + bg_ref[1:2, :])
        dya_ref[...] = (dm * sa).astype(BF16)
        dyb_ref[...] = (dm * sb).astype(BF16)
        dga = dm * a_ref[...] * (sa * (1.0 - sa))
        dgb = dm * b_ref[...] * (sb * (1.0 - sb))
        dg_ref[:, :D_MODEL] = dga.astype(BF16)
        dg_ref[:, D_MODEL:] = dgb.astype(BF16)
        s_ref[0:1, :] += jnp.sum(dga, 0, keepdims=True)
        s_ref[1:2, :] += jnp.sum(dgb, 0, keepdims=True)

    spec = pl.BlockSpec((None, EW_TM, D_MODEL), lambda b, i: (b, i, 0))
    spec1 = pl.BlockSpec((None, EW_TM, D_MODEL), lambda b, i: (b, i, 1))
    small = pl.BlockSpec((8, D_MODEL), lambda b, i: (0, 0))
    return _pcall(
        body, name="merge_bwd", grid=(nb, seq // EW_TM),
        in_specs=[spec, _whole(w_out16, True), spec, spec, spec, spec1, small],
        out_specs=[spec, spec, pl.BlockSpec((None, EW_TM, 2 * D_MODEL), lambda b, i: (b, i, 0)), small],
        out_shape=[jax.ShapeDtypeStruct((nb, seq, D_MODEL), BF16), jax.ShapeDtypeStruct((nb, seq, D_MODEL), BF16),
                   jax.ShapeDtypeStruct((nb, seq, 2 * D_MODEL), BF16), jax.ShapeDtypeStruct((8, D_MODEL), F32)],
        compiler_params=_params("arbitrary", "arbitrary"),
    )(dpre16, w_out16, y_a, y_b, gm, gm, bgate)


def _ln_loss(x, merged16, w_out16, gp, p16, w_ple16, target, bgate, ln_g, ln_b):
    nb, seq, _ = x.shape

    def body(x_ref, m_ref, wo_ref, gp_ref, p_ref, wp_ref, t_ref, bg_ref, g_ref, b_ref,
             dx_ref, dp_ref, dpw_ref, dgp_ref, s_ref):
        @pl.when((pl.program_id(0) == 0) & (pl.program_id(1) == 0))
        def _():
            s_ref[...] = jnp.zeros_like(s_ref)

        sp = _sigmoid(gp_ref[...] + bg_ref[2:3, :])
        pw = jnp.dot(p_ref[...], wp_ref[...], preferred_element_type=F32)
        mix = jnp.dot(m_ref[...], wo_ref[...], preferred_element_type=F32)
        pre = ALPHA * x_ref[...] + mix + sp * pw
        mu = jnp.mean(pre, -1, keepdims=True)
        cen = pre - mu
        rstd = lax.rsqrt(jnp.mean(cen * cen, -1, keepdims=True) + LN_EPS)
        xhat = cen * rstd
        err = xhat * g_ref[...] + b_ref[...] - t_ref[...]
        dy = err * (1.0 / D_MODEL)
        dxh = dy * g_ref[...]
        dpre = rstd * (dxh - jnp.mean(dxh, -1, keepdims=True) - xhat * jnp.mean(dxh * xhat, -1, keepdims=True))
        dx_ref[...] = ALPHA * dpre
        dp_ref[...] = dpre.astype(BF16)
        dpw_ref[...] = (dpre * sp).astype(BF16)
        dgp = dpre * pw * (sp * (1.0 - sp))
        dgp_ref[...] = dgp.astype(BF16)
        s_ref[0:1, :] += jnp.sum(dy * xhat, 0, keepdims=True)
        s_ref[1:2, :] += jnp.sum(dy, 0, keepdims=True)
        s_ref[2:3, :] += jnp.sum(dgp, 0, keepdims=True)
        s_ref[3:4, :] += jnp.sum(err * err, 0, keepdims=True)

    spec = pl.BlockSpec((None, EW_TM, D_MODEL), lambda b, i: (b, i, 0))
    small = pl.BlockSpec((8, D_MODEL), lambda b, i: (0, 0))
    row = pl.BlockSpec((1, D_MODEL), lambda b, i: (0, 0))
    return _pcall(
        body, name="ln_loss", grid=(nb, seq // EW_TM),
        in_specs=[spec, spec, _whole(w_out16, True), spec, pl.BlockSpec((None, EW_TM, PLE_DIM), lambda b, i: (b, i, 0)),
                  _whole(w_ple16, True), spec, small, row, row],
        out_specs=[spec] * 4 + [small],
        out_shape=[jax.ShapeDtypeStruct((nb, seq, D_MODEL), F32)] + [jax.ShapeDtypeStruct((nb, seq, D_MODEL), BF16)] * 3
        + [jax.ShapeDtypeStruct((8, D_MODEL), F32)],
        compiler_params=_params("arbitrary", "arbitrary"),
    )(x, merged16, w_out16, gp, p16, w_ple16, target, bgate, ln_g, ln_b)


def _adamw_update(w_ref, g_ref, m_ref, v_ref, d_ref, nm_ref, nv_ref):
    c1 = 1.0 - ADAM_B1 ** ADAM_STEP
    c2 = 1.0 - ADAM_B2 ** ADAM_STEP
    gv = g_ref[...]
    nm = ADAM_B1 * m_ref[...] + (1.0 - ADAM_B1) * gv
    nv = ADAM_B2 * v_ref[...] + (1.0 - ADAM_B2) * (gv * gv)
    d_ref[...] = -ADAM_LR * ((nm / c1) / (jnp.sqrt(nv / c2) + ADAM_EPS) + ADAM_WD * w_ref[...])
    nm_ref[...] = nm
    nv_ref[...] = nv


def _adamw(w, g, m, v, name):
    rows, cols = w.shape
    tr = _row_tile(rows, cols, 8, 5 << 19)

    def body(*refs):
        _adamw_update(*refs)

    spec = pl.BlockSpec((tr, cols), lambda i: (i, 0))
    return _pcall(
        body, name=name, grid=(rows // tr,), in_specs=[spec] * 4, out_specs=[spec] * 3,
        out_shape=[jax.ShapeDtypeStruct(w.shape, F32)] * 3, compiler_params=_params("parallel"),
    )(w, g, m, v)


def _adamw_small(ws, gs, ms, vs, name):
    n = len(ws)

    def body(*refs):
        for i in range(n):
            _adamw_update(*[refs[k * n + i] for k in range(7)])

    outs = _pcall(body, name=name, out_shape=[jax.ShapeDtypeStruct(w.shape, F32) for w in ws] * 3,
                  compiler_params=_params())(*ws, *gs, *ms, *vs)
    return outs[:n], outs[n:2 * n], outs[2 * n:]


def _sum_rows(parts, out_dtype, name):
    rows, cols = parts[0].shape
    tr = rows
    for cand in range(16, rows, 16):
        if rows % cand == 0 and cand * cols * 4 <= (1 << 20):
            tr = cand
    n = len(parts)

    def body(*refs):
        acc = refs[0][...].astype(F32)
        for r in refs[1:n]:
            acc = acc + r[...].astype(F32)
        refs[n][...] = acc.astype(out_dtype)

    spec = pl.BlockSpec((tr, cols), lambda i: (i, 0))
    return _pcall(
        body, name=name, grid=(rows // tr,), in_specs=[spec] * n, out_specs=spec,
        out_shape=jax.ShapeDtypeStruct((rows, cols), out_dtype), compiler_params=_params("parallel"),
    )(*parts)


def _place():
    return lax.axis_index("x"), lax.axis_index("y"), lax.axis_index("c")


def _other_chips(x, y):
    return [(1 - x, y), (x, 1 - y), (1 - x, 1 - y)]


def _remote(src, dst, send_sem, recv_sem, to):
    return pltpu.make_async_remote_copy(src_ref=src, dst_ref=dst, send_sem=send_sem, recv_sem=recv_sem,
                                        device_id=to, device_id_type=MESH)


ANY = pl.BlockSpec(memory_space=pl.ANY)
D2D_CHUNK_BYTES = 512 * 1024
ICI_CHUNK_BYTES = 2 * 1024 * 1024


def _row_chunks(rows, row_bytes, chunk_bytes=D2D_CHUNK_BYTES):
    per = max(16, chunk_bytes // row_bytes // 16 * 16)
    return [(s, min(per, rows - s)) for s in range(0, rows, per)]


def _row_tile(rows, cols, align, limit=1 << 21):
    best = None
    for cand in range(align, rows + 1, align):
        if rows % cand == 0 and cand * cols * 4 <= limit:
            best = cand
    return best or rows


TOKEN_TM = 512


def _allgather_pieces(pieces, x, dilations):
    n, nd = len(pieces), len(dilations)
    nb, seq, kdim = x.shape
    halves = [_row_chunks(p.shape[0] // 2, p.shape[1] * p.dtype.itemsize, ICI_CHUNK_BYTES) for p in pieces]
    entries = [(a, q, s, m, j) for a in range(n) for q, (s, m) in enumerate(halves[a]) for j in range(3)]
    slot = {(a, q, j): k for k, (a, q, _, _, j) in enumerate(entries)}
    n_ici = len(entries)

    def body(*refs):
        ins, x_hbm, outs, order_hbm = refs[:n], refs[n], refs[n + 1:2 * n + 1], refs[2 * n + 1:2 * n + 2 + nd]
        send_sems, recv_sems = refs[2 * n + 2 + nd:2 * n + 4 + nd]
        tile_refs = refs[2 * n + 4 + nd:]
        x, y, c = _place()
        me = 2 * x + y
        sibling = (x, y, 1 - c)
        chips = _other_chips(x, y)

        def landed(a, s, m, j, core):
            half = ins[a].shape[0] // 2
            return outs[a].at[2 * chips[j][0] + chips[j][1], pl.ds(core * half + s, m)]

        sent = []
        for k, (a, q, s, m, j) in enumerate(entries):
            if j < 2:
                half = ins[a].shape[0] // 2
                cp = _remote(ins[a].at[pl.ds(c * half + s, m)], outs[a].at[me, pl.ds(c * half + s, m)],
                             send_sems.at[k], recv_sems.at[k], (*chips[j], c))
                cp.start()
                sent.append(cp)

        def reorder(x_ref, nat_ref, *own_refs):
            xv = x_ref[0]
            nat_ref[0] = xv.astype(BF16)
            for o_ref in own_refs:
                _store_own_order(xv, tile_refs, o_ref.at[0])

        pltpu.emit_pipeline(
            reorder, grid=(nb, seq // TOKEN_TM),
            in_specs=[pl.BlockSpec((1, TOKEN_TM, kdim), lambda b, i: (b, i, 0))],
            out_specs=[pl.BlockSpec((1, TOKEN_TM, kdim), lambda b, i: (b, i, 0))]
            + [pl.BlockSpec((1, d, TOKEN_TM // d, kdim), lambda b, i: (b, 0, i, 0)) for d in dilations],
        )(x_hbm, *order_hbm)

        def pass_to_sibling(k, blk):
            fw = _remote(blk, blk, send_sems.at[n_ici + k], recv_sems.at[n_ici + k], sibling)
            fw.start()
            sent.append(fw)

        for k, (a, q, s, m, j) in enumerate(entries):
            if j < 2:
                blk = landed(a, s, m, j, c)
                _remote(blk, blk, send_sems.at[k], recv_sems.at[k], (*chips[j], c)).wait_recv()
                first = q < (len(halves[a]) + 1) // 2
                if (j == 0) == first:
                    on = slot[(a, q, 2)]
                    rl = _remote(blk, blk, send_sems.at[on], recv_sems.at[on], (*chips[1 - j], c))
                    rl.start()
                    sent.append(rl)
                pass_to_sibling(k, blk)
        for k, (a, q, s, m, j) in enumerate(entries):
            if j == 2:
                blk = landed(a, s, m, j, c)
                _remote(blk, blk, send_sems.at[k], recv_sems.at[k], (*chips[j], c)).wait_recv()
                pass_to_sibling(k, blk)
        for k, (a, q, s, m, j) in enumerate(entries):
            blk = landed(a, s, m, j, 1 - c)
            _remote(blk, blk, send_sems.at[n_ici + k], recv_sems.at[n_ici + k], sibling).wait_recv()
        for cp in sent:
            cp.wait_send()

    gathered = _pcall(
        body, name="allgather_weights", in_specs=[ANY] * (n + 1), out_specs=[ANY] * (n + 1 + nd),
        out_shape=[jax.ShapeDtypeStruct((4,) + p.shape, p.dtype) for p in pieces]
        + [jax.ShapeDtypeStruct((nb, seq, kdim), BF16)]
        + [jax.ShapeDtypeStruct((nb, d, seq // d, kdim), BF16) for d in dilations],
        scratch_shapes=[pltpu.SemaphoreType.DMA((2 * n_ici,)), pltpu.SemaphoreType.DMA((2 * n_ici,))]
        + [pltpu.VMEM((TOKEN_TM, LANE), F32)] * (kdim // LANE),
        compiler_params=pltpu.CompilerParams(has_side_effects=True, vmem_limit_bytes=VMEM_LIMIT_BYTES),
    )(*pieces, x)
    gathered, orders = gathered[:n], gathered[n:]
    x16p = [orders[0]] + [o.reshape(nb, seq, kdim) for o in orders[1:]]
    cx, cy, _ = _place()
    return [lax.dynamic_update_slice(g, p[None], (2 * cx + cy, 0, 0)) for g, p in zip(gathered, pieces)], x16p


def _sibling_exchange(grads):
    n = len(grads)
    chunks = [_row_chunks(g.shape[1] // 2, g.shape[2] * g.dtype.itemsize) for g in grads]
    n_sem = 4 * sum(len(ch) for ch in chunks)

    def body(*refs):
        ins, gots = refs[:n], refs[n:2 * n]
        send_sems, recv_sems = refs[2 * n:]
        x, y, c = _place()
        sibling = (x, y, 1 - c)
        work = []
        for a in range(n):
            half = ins[a].shape[1] // 2
            for piece in range(4):
                for s, m in chunks[a]:
                    k = len(work)
                    cp = _remote(ins[a].at[piece, pl.ds((1 - c) * half + s, m)], gots[a].at[piece, pl.ds(s, m)],
                                 send_sems.at[k], recv_sems.at[k], sibling)
                    cp.start()
                    work.append(cp)
        for cp in work:
            cp.wait()

    return _pcall(
        body, name="grad_sibling_exchange", in_specs=[ANY] * n, out_specs=[ANY] * n,
        out_shape=[jax.ShapeDtypeStruct((4, g.shape[1] // 2, g.shape[2]), g.dtype) for g in grads],
        scratch_shapes=[pltpu.SemaphoreType.DMA((n_sem,)), pltpu.SemaphoreType.DMA((n_sem,))],
        compiler_params=pltpu.CompilerParams(has_side_effects=True),
    )(*grads)


def _sibling_gather(fulls):
    n = len(fulls)
    chunks = [_row_chunks(f.shape[0] // 2, f.shape[1] * f.dtype.itemsize) for f in fulls]
    n_sem = sum(len(ch) for ch in chunks)

    def body(*refs):
        outs = refs[n:2 * n]
        send_sems, recv_sems = refs[2 * n:]
        x, y, c = _place()
        sibling = (x, y, 1 - c)
        work = []
        for a in range(n):
            h = outs[a].shape[0] // 2
            for s, m in chunks[a]:
                k = len(work)
                mine = outs[a].at[pl.ds(c * h + s, m)]
                cp = _remote(mine, mine, send_sems.at[k], recv_sems.at[k], sibling)
                cp.start()
                work.append((a, s, m, cp))
        for k, (a, s, m, cp) in enumerate(work):
            h = outs[a].shape[0] // 2
            cp.wait_send()
            theirs = outs[a].at[pl.ds((1 - c) * h + s, m)]
            _remote(theirs, theirs, send_sems.at[k], recv_sems.at[k], sibling).wait_recv()

    return _pcall(
        body, name="grad_sibling_gather", in_specs=[ANY] * n, out_specs=[ANY] * n,
        out_shape=[jax.ShapeDtypeStruct(f.shape, f.dtype) for f in fulls],
        input_output_aliases={a: a for a in range(n)},
        scratch_shapes=[pltpu.SemaphoreType.DMA((n_sem,)), pltpu.SemaphoreType.DMA((n_sem,))],
        compiler_params=pltpu.CompilerParams(has_side_effects=True),
    )(*fulls)


def _pair_sum(grad, got, place, name):
    _, rows, cols = grad.shape
    half = rows // 2
    tr = _row_tile(half, cols, 16)

    def body(p_ref, a_ref, b_ref, o_ref):
        o_ref[...] = (a_ref[...].astype(F32) + b_ref[...].astype(F32)).astype(BF16)

    return _pcall(
        body, name=name,
        grid_spec=pltpu.PrefetchScalarGridSpec(
            num_scalar_prefetch=1, grid=(4, half // tr),
            in_specs=[pl.BlockSpec((None, tr, cols), lambda k, i, p: (k, p[1] * (half // tr) + i, 0)),
                      pl.BlockSpec((None, tr, cols), lambda k, i, p: (k, i, 0))],
            out_specs=pl.BlockSpec((None, tr, cols), lambda k, i, p: (k, i, 0))),
        out_shape=jax.ShapeDtypeStruct((4, half, cols), BF16),
        compiler_params=_params("parallel", "parallel"),
    )(place, grad, got)


def _chip_sum(sums, got, place, name):
    _, h, cols = sums.shape
    tr = _row_tile(h, cols, 16)

    def body(p_ref, own_ref, g0, g1, g2, o_ref):
        o_ref[...] = ((own_ref[...].astype(F32) + g0[...].astype(F32)) + g1[...].astype(F32)) + g2[...].astype(F32)

    gspec = lambda j: pl.BlockSpec((None, tr, cols), lambda i, p: (j, i, 0))
    return _pcall(
        body, name=name,
        grid_spec=pltpu.PrefetchScalarGridSpec(
            num_scalar_prefetch=1, grid=(h // tr,),
            in_specs=[pl.BlockSpec((None, tr, cols), lambda i, p: (p[0], i, 0)), gspec(0), gspec(1), gspec(2)],
            out_specs=pl.BlockSpec((tr, cols), lambda i, p: (p[1] * (h // tr) + i, 0))),
        out_shape=jax.ShapeDtypeStruct((2 * h, cols), F32),
        compiler_params=_params("parallel"),
    )(place, sums, got, got, got)


def _allgather8(buf, name):
    rows = buf.shape[0]

    def body(in_ref, out_ref, send_sems, recv_sems):
        x, y, c = _place()
        me = 4 * x + 2 * y + c
        out_ref[me] = in_ref[...]
        work = []
        for rel in range(1, 8):
            fx, fy, fc = (rel >> 2) & 1, (rel >> 1) & 1, rel & 1
            to = (x ^ fx, y ^ fy, c ^ fc)
            cp = _remote(in_ref, out_ref.at[me], send_sems.at[rel - 1], recv_sems.at[rel - 1], to)
            cp.start()
            work.append((cp, 4 * to[0] + 2 * to[1] + to[2]))
        for rel, (cp, frm) in enumerate(work):
            cp.wait_send()
            blk = out_ref.at[frm]
            _remote(blk, blk, send_sems.at[rel], recv_sems.at[rel], (x, y, c)).wait_recv()

    return _pcall(
        body, name=name, in_specs=[pl.BlockSpec(memory_space=pltpu.VMEM)],
        out_specs=pl.BlockSpec(memory_space=pltpu.VMEM),
        out_shape=jax.ShapeDtypeStruct((8, rows, LANE), F32),
        scratch_shapes=[pltpu.SemaphoreType.DMA((7,)), pltpu.SemaphoreType.DMA((7,))],
        compiler_params=pltpu.CompilerParams(has_side_effects=True),
    )(buf)


def _pack_rows(arrs):
    flats = [a.reshape(-1).astype(F32) for a in arrs]
    starts = np.cumsum([0] + [-(-f.shape[0] // LANE) * LANE for f in flats])
    total = -(-int(starts[-1]) // (8 * LANE)) * 8 * LANE
    flat = sum(jnp.pad(f, (int(s), total - int(s) - f.shape[0])) for f, s in zip(flats, starts))
    return flat.reshape(total // LANE, LANE)


def _unpack_rows(buf, shapes):
    flat = buf.reshape(-1)
    outs, off = [], 0
    for s in shapes:
        n = int(np.prod(s))
        outs.append(flat[off:off + n].reshape(s))
        off += -(-n // LANE) * LANE
    return outs


def _local_grads(x, p, target, wseg, w_br16, w_out16, w_ple16, b_gate, conv_w, conv_b, dt_bias, a_log, d_skip,
                 ssm_norm_w, ln_g, ln_b, rel_bias, finish_dx, x16p):
    nb, seq, _ = x.shape
    bmaps = jnp.asarray(_bucket_maps())
    bias = _bias_tables(rel_bias, bmaps)
    bgate8 = jnp.pad(b_gate, ((0, 5), (0, 0)))
    dils = [d for _, d in PATTERNS]

    x16 = x16p[0]
    p16 = p.astype(BF16)
    qkv = [_proj(x16p[g], [wseg["qkv%d" % g]], BF16, "proj_qkv%d" % g, True, 2 * MM_TM)[0].reshape(
        nb, dils[g], seq // dils[g], -1) for g in range(3)]
    nat = {}
    for gi, (group, tm) in enumerate(NAT_GROUPS):
        outs = _proj(x16, [wseg[s] for s in group], F32, "proj_nat%d" % gi, True, tm)
        nat.update(zip(group, outs))
    att = [_attn_fwd(qkv[g], bias, g, dils[g], "attn_fwd%d" % g) for g in range(3)]
    oa, o_att, lse = _combine_fwd(att[0][0], att[0][1], att[1:], nat["gatt"])

    conv_wg, conv_bg = _xbc_group_order(conv_w), _xbc_group_order(conv_b)
    act = _conv_fwd(nat["xbc"], conv_wg, conv_bg, "conv_fwd")
    dt_bias_row = jnp.pad(dt_bias, ((0, 0), (0, LANE - SSM_HEADS)))
    alog_g, dskip_g = _group_lanes(a_log), _group_lanes(d_skip)
    y_ssm, y_all, sprev = _ssd_fwd(act, nat["dt"], dt_bias_row, nat["z"], alog_g, dskip_g, ssm_norm_w)

    w_bra, w_brb = w_br16[:ATT_OUT], w_br16[ATT_OUT:]
    y_a, y_b, merged = _merge_fwd(oa, y_ssm, w_bra, w_brb, nat["gm"], bgate8)

    dx, dpre16, dpw16, dgp16, ln_sums = _ln_loss(x, merged, w_out16, nat["gp"], p16, w_ple16, target, bgate8,
                                                 ln_g, ln_b)
    loss_sum = (0.5 / D_MODEL) * jnp.sum(ln_sums[3])
    dya16, dyb16, dgm16, mg_sums = _merge_bwd(dpre16, w_out16, y_a, y_b, nat["gm"], bgate8)
    dys = _dx([dyb16], [w_brb], [], "dx_yssm")
    g_w_br, g_w_out, g_w_ple = _dw_stacked(
        [[(oa, dya16), (y_ssm, dyb16)], [(merged, dpre16)], [(p16, dpw16)]], BF16, "dw_branch_out_ple")

    do_att, dgatt16, own_order = _combine_bwd(dya16, w_bra, nat["gatt"], o_att, lse, dils[1:])
    dseg = {"gatt": dgatt16, "gm": dgm16, "gp": dgp16}
    dbias = []
    for g in range(3):
        cotangent = (do_att, o_att, lse) if g == 0 else (own_order[2 * g - 2], own_order[2 * g - 1])
        dqkv, db = _attn_bwd(qkv[g], bias, g, cotangent, dils[g],
                             "attn_bwd%d" % g)
        dseg["qkv%d" % g] = dqkv.reshape(nb, seq, -1)
        dbias.append(db)
    g_rel = _bias_grad(jnp.concatenate(dbias, axis=0), bmaps)[:, 0, :NUM_BUCKETS].T

    dact, ddtg, dz, ssd_small, g_normw = _ssd_bwd(
        act, nat["dt"], dt_bias_row, nat["z"], y_all, dys, sprev, alog_g, dskip_g, ssm_norm_w)
    dseg["z"] = dz
    dseg["dt"] = jnp.pad(_ungroup_lanes(ddtg), ((0, 0), (0, 0), (0, LANE - SSM_HEADS)))
    dpre, conv_sums = _conv_bwd_pre(dact, nat["xbc"], conv_wg, conv_bg, "conv_bwd")
    dseg["xbc"] = _conv_bwd_x(dpre, conv_wg, "conv_bwd_x")
    csum = _xbc_reference_order(conv_sums)

    dh_own = [(dseg["qkv%d" % g].reshape(nb, dils[g], seq // dils[g], -1), wseg["qkv%d" % g]) for g in (1, 2)]
    dwseg = {"qkv%d" % g: _dw(x16p[g], [dseg["qkv%d" % g]], BF16, "dw_qkv%d" % g, True)[0] for g in range(3)}
    for gi, group in enumerate(DW_GROUPS):
        dwseg.update(zip(group, _dw(x16, [dseg[s] for s in group], BF16, "dw_nat%d" % gi, True)))
    names = ["qkv0"] + [s for group, _ in NAT_GROUPS for s in group]
    dx = finish_dx([dseg[s] for s in names], [wseg[s] for s in names], [dx], dh_own, dwseg, g_w_br, g_w_out, g_w_ple)

    small = dict(
        b_gate=jnp.stack([mg_sums[0], mg_sums[1], ln_sums[2]]),
        conv_w=csum[0:4], conv_b=csum[4:5],
        dt_bias=_ungroup_lanes(ssd_small[:, 2:3, :]), a_log=_ungroup_lanes(ssd_small[:, 0:1, :]),
        d_skip=_ungroup_lanes(ssd_small[:, 1:2, :]), ssm_norm_w=g_normw,
        ln_g=ln_sums[0:1], ln_b=ln_sums[1:2], rel_bias=g_rel)
    return loss_sum, dx, small


DX_TM = 256
SMALL_ORDER = ("b_gate", "conv_w", "conv_b", "dt_bias", "a_log", "d_skip", "ssm_norm_w", "ln_g", "ln_b", "rel_bias")
SMALL_FULL_SHAPES = dict(b_gate=(3, 1024), conv_w=(4, 3072), conv_b=(1, 3072), dt_bias=(1, 32), a_log=(1, 32),
                         d_skip=(1, 32), ssm_norm_w=(1, 2048), ln_g=(1, 1024), ln_b=(1, 1024), rel_bias=(32, 36))


def kernel(x, p, w_in, b_gate, conv_w, conv_b, dt_bias, a_log, d_skip, ssm_norm_w, w_branch, w_out, w_ple, ln_g, ln_b, rel_bias, loss_target, m_w_in, m_b_gate, m_conv_w, m_conv_b, m_dt_bias, m_a_log, m_d_skip, m_ssm_norm_w, m_w_branch, m_w_out, m_w_ple, m_ln_g, m_ln_b, m_rel_bias, v_w_in, v_b_gate, v_conv_w, v_conv_b, v_dt_bias, v_a_log, v_d_skip, v_ssm_norm_w, v_w_branch, v_w_out, v_w_ple, v_ln_g, v_ln_b, v_rel_bias):
    cx, cy, cc = _place()
    chip = 2 * cx + cy
    dev = 4 * cx + 2 * cy + cc

    w_in_t = jnp.transpose(w_in[0])
    win16 = _shard_to_window(w_in_t, chip)
    (g_win, g_br, g_out, g_ple), x16p = _allgather_pieces(
        [win16, w_branch[0].astype(BF16), w_out[0].astype(BF16), w_ple[0].astype(BF16)], x,
        [d for _, d in PATTERNS][1:])
    wseg = _assemble(g_win)
    w_br16 = g_br.reshape(4 * 704, D_MODEL)
    w_out16 = g_out.reshape(D_MODEL, D_MODEL)
    w_ple16 = jnp.transpose(g_ple, (1, 0, 2)).reshape(PLE_DIM, D_MODEL)
    shards = _allgather8(_pack_rows([b_gate[0], conv_w[0]]), "allgather_small_params")
    per_chip = [_unpack_rows(shards[2 * k], [(3, 256), (4, 768)]) for k in range(4)]
    b_gate_full = _join_last([pc[0] for pc in per_chip])
    conv_w_full = _join_last([pc[1] for pc in per_chip])

    place = jnp.stack([chip, cc]).astype(jnp.int32)
    reduced = []

    def finish_dx(dhs, ws, accs, own_order_dhs, dwseg, d_br, d_out, d_ple):
        grads = [_pack(dwseg), d_br.reshape(4, 704, D_MODEL), d_out.reshape(4, 256, D_MODEL),
                 jnp.transpose(d_ple.reshape(PLE_DIM, 4, 256), (1, 0, 2))]
        got = _sibling_exchange(grads)
        chip_sums = [_pair_sum(g, t, place, "grad_pair_sum_%d" % i) for i, (g, t) in enumerate(zip(grads, got))]
        dx, others = _dx(dhs, ws, accs, "dx_w_in_and_grad_chip_scatter", True, DX_TM, chip_sums, own_order_dhs)
        fulls = [_chip_sum(s, t, place, "grad_chip_sum_%d" % i) for i, (s, t) in enumerate(zip(chip_sums, others))]
        reduced.extend(_sibling_gather(fulls))
        return dx

    loss_sum, grad_x, small = _local_grads(
        x, p[0], loss_target, wseg, w_br16, w_out16, w_ple16, b_gate_full, conv_w_full, conv_b, dt_bias, a_log,
        d_skip, ssm_norm_w, ln_g, ln_b, rel_bias, finish_dx, x16p)
    big = reduced
    g_w_in = lax.optimization_barrier(_window_to_shard(big[0], chip))
    g_w_branch, g_w_out, g_w_ple = big[1], big[2], big[3]
    parts = _allgather8(_pack_rows([small[n] for n in SMALL_ORDER] + [loss_sum.reshape(1, 1)]),
                        "allgather_small_grads")
    small_sum = _sum_rows([parts[i] for i in range(8)], F32, "small_grad_sum")
    *reduced_small, loss = _unpack_rows(small_sum, [SMALL_FULL_SHAPES[n] for n in SMALL_ORDER] + [(1, 1)])
    loss = loss.reshape(())
    sg = dict(zip(SMALL_ORDER, reduced_small))
    sg["b_gate"] = lax.dynamic_slice_in_dim(sg["b_gate"], chip * 256, 256, axis=1)
    sg["conv_w"] = lax.dynamic_slice_in_dim(sg["conv_w"], chip * 768, 768, axis=1)
    del dev

    upd = {}
    upd["w_in"] = [jnp.transpose(t) for t in _adamw(w_in_t, g_w_in, jnp.transpose(m_w_in[0]),
                                                      jnp.transpose(v_w_in[0]), "adamw_w_in")]
    upd["w_branch"] = _adamw(w_branch[0], g_w_branch, m_w_branch[0], v_w_branch[0], "adamw_w_branch")
    upd["w_out"] = _adamw(w_out[0], g_w_out, m_w_out[0], v_w_out[0], "adamw_w_out")
    upd["w_ple"] = _adamw(w_ple[0], g_w_ple, m_w_ple[0], v_w_ple[0], "adamw_w_ple")
    small_w = dict(b_gate=b_gate, conv_w=conv_w, conv_b=conv_b, dt_bias=dt_bias, a_log=a_log, d_skip=d_skip,
                   ssm_norm_w=ssm_norm_w, ln_g=ln_g, ln_b=ln_b, rel_bias=rel_bias)
    small_m = dict(b_gate=m_b_gate, conv_w=m_conv_w, conv_b=m_conv_b, dt_bias=m_dt_bias, a_log=m_a_log,
                   d_skip=m_d_skip, ssm_norm_w=m_ssm_norm_w, ln_g=m_ln_g, ln_b=m_ln_b, rel_bias=m_rel_bias)
    small_v = dict(b_gate=v_b_gate, conv_w=v_conv_w, conv_b=v_conv_b, dt_bias=v_dt_bias, a_log=v_a_log,
                   d_skip=v_d_skip, ssm_norm_w=v_ssm_norm_w, ln_g=v_ln_g, ln_b=v_ln_b, rel_bias=v_rel_bias)
    for n in SMALL_ORDER:
        sg[n] = sg[n].reshape(small_w[n].shape)
    s_delta, s_m, s_v = _adamw_small(*[[t[n] for n in SMALL_ORDER] for t in (small_w, sg, small_m, small_v)],
                                     "adamw_small")
    for i, n in enumerate(SMALL_ORDER):
        upd[n] = (s_delta[i], s_m[i], s_v[i])

    order = ("w_in", "b_gate", "conv_w", "conv_b", "dt_bias", "a_log", "d_skip", "ssm_norm_w", "w_branch", "w_out",
             "w_ple", "ln_g", "ln_b", "rel_bias")
    grads = dict(sg, w_in=jnp.transpose(g_w_in)[None],w_branch=g_w_branch[None], w_out=g_w_out[None], w_ple=g_w_ple[None])
    lead = lambda n, t: t[None] if n in ("w_in", "w_branch", "w_out", "w_ple") else t
    return (loss, grad_x, *[grads[n] for n in order], *[lead(n, upd[n][0]) for n in order],
            *[lead(n, upd[n][1]) for n in order], *[lead(n, upd[n][2]) for n in order])
```

```python
import math

import numpy as np
import jax
import jax.numpy as jnp
from jax import lax
from jax.experimental import pallas as pl
from jax.experimental.pallas import tpu as pltpu

F32, BF16 = jnp.float32, jnp.bfloat16

D_MODEL = 1024
HEAD_DIM = 64
GROUP_HEADS = 12
ATT_OUT = GROUP_HEADS * HEAD_DIM
PATTERNS = ((128, 1), (512, 4), (2048, 16))
BAND = 128
NUM_BUCKETS = 32
MAX_DISTANCE = 2048
D_INNER = 2048
SSM_HEADS = 32
SSM_GROUPS = 4
GROUP_SSM_HEADS = SSM_HEADS // SSM_GROUPS
D_STATE = 128
CHUNK = 128
PLE_DIM = 256
ALPHA = 2.0 ** 0.25
LN_EPS = 1e-5
RMS_EPS = 1e-5
ADAM_LR, ADAM_B1, ADAM_B2, ADAM_EPS, ADAM_WD, ADAM_STEP = 0.001, 0.9, 0.999, 1e-08, 0.01, 10
NEG = -1e30

QKV_W = 3 * ATT_OUT
IN_COLS = 15904
SHARD_COLS = IN_COLS // 4
DT_COL = 12800
ROW_TILE = 16
WIN_ROWS = 4000


def _win_offset(k):
    return (k * SHARD_COLS) % ROW_TILE


def _win_start(k):
    return k * SHARD_COLS - _win_offset(k)

VMEM_LIMIT_BYTES = 56 * 1024 * 1024
LANE = 128
MESH = pl.DeviceIdType.MESH
NT = (((1,), (1,)), ((), ()))
TN = (((0,), (0,)), ((), ()))


def _pcall(body, **kw):
    return pl.pallas_call(body, **kw)


def _params(*sem):
    return pltpu.CompilerParams(dimension_semantics=sem, vmem_limit_bytes=VMEM_LIMIT_BYTES)


def _sigmoid(v):
    return jax.nn.sigmoid(v)


MM_TM = 512


def _tok_spec(tm, width):
    return pl.BlockSpec((None, tm, width), lambda b, i: (b, i, 0))


def _whole(arr, single_buffer=False):
    mode = dict(pipeline_mode=pl.Buffered(1)) if single_buffer else {}
    return pl.BlockSpec(arr.shape, lambda b, i: (0,) * arr.ndim, **mode)


def _proj(a3, ws, out_dtype, name, w_rows_are_outputs=False, tm=MM_TM):
    nb, seq, kdim = a3.shape
    nw = len(ws)
    widths = [w.shape[0] if w_rows_are_outputs else w.shape[1] for w in ws]

    def body(*refs):
        a = refs[0][...].astype(BF16)
        for w_ref, o_ref in zip(refs[1:1 + nw], refs[1 + nw:]):
            if w_rows_are_outputs:
                v = lax.dot_general(a, w_ref[...], NT, preferred_element_type=F32)
            else:
                v = jnp.dot(a, w_ref[...], preferred_element_type=F32)
            o_ref[...] = v.astype(out_dtype)

    return _pcall(
        body, name=name, grid=(nb, seq // tm),
        in_specs=[_tok_spec(tm, kdim)] + [_whole(w, True) for w in ws],
        out_specs=[_tok_spec(tm, n) for n in widths],
        out_shape=[jax.ShapeDtypeStruct((nb, seq, n), out_dtype) for n in widths],
        compiler_params=_params("parallel", "parallel"),
    )(a3, *ws)


def _dx(dhs, ws, accs, name, w_rows_are_outputs=False, tm=MM_TM, scatter=None, own_order_dhs=()):
    nb, seq, _ = dhs[0].shape
    nd, nacc, npd = len(dhs), len(accs), len(own_order_dhs)
    kout = ws[0].shape[1] if w_rows_are_outputs else ws[0].shape[0]
    sums = scatter or []
    ns = len(sums)
    chunks = [_row_chunks(s.shape[1], s.shape[2] * s.dtype.itemsize, ICI_CHUNK_BYTES) for s in sums]
    n_sem = 3 * sum(len(ch) for ch in chunks)
    grid = (nb, seq // tm)
    ntile = kout // LANE if npd else 0

    def body(*refs):
        n_own = 2 * nd + nacc
        n_in = n_own + 2 * npd
        sum_refs, o_ref, got_refs = refs[n_in:n_in + ns], refs[n_in + ns], refs[n_in + ns + 1:n_in + 2 * ns + 1]
        tile_refs = refs[n_in + 2 * ns + 1:n_in + 2 * ns + 1 + ntile]

        def copies():
            send_sems, recv_sems = refs[-2], refs[-1]
            x, y, c = _place()
            out = []
            for a in range(ns):
                for s, m in chunks[a]:
                    for j, (cx, cy) in enumerate(_other_chips(x, y)):
                        k = len(out)
                        out.append(_remote(sum_refs[a].at[2 * cx + cy, pl.ds(s, m)], got_refs[a].at[j, pl.ds(s, m)],
                                           send_sems.at[k], recv_sems.at[k], (cx, cy, c)))
            return out

        if ns:
            @pl.when((pl.program_id(0) == 0) & (pl.program_id(1) == 0))
            def _():
                for cp in copies():
                    cp.start()

        v = None
        for dh_ref, w_ref in zip(refs[:nd], refs[nd:2 * nd]):
            dh = dh_ref[...].astype(BF16)
            if w_rows_are_outputs:
                t = jnp.dot(dh, w_ref[...], preferred_element_type=F32)
            else:
                t = lax.dot_general(dh, w_ref[...], NT, preferred_element_type=F32)
            v = t if v is None else v + t
        for a_ref in refs[2 * nd:2 * nd + nacc]:
            v = v + a_ref[...]
        for q_ref, w_ref in zip(refs[n_own:n_own + npd], refs[n_own + npd:n_in]):
            d, per, n = q_ref.shape
            dh = q_ref[...].reshape(d * per, n).astype(BF16)
            if w_rows_are_outputs:
                t = jnp.dot(dh, w_ref[...], preferred_element_type=F32)
            else:
                t = lax.dot_general(dh, w_ref[...], NT, preferred_element_type=F32)
            v = v + _natural_value(t, d, tile_refs)
        o_ref[...] = v

        if ns:
            @pl.when((pl.program_id(0) == grid[0] - 1) & (pl.program_id(1) == grid[1] - 1))
            def _():
                for cp in copies():
                    cp.wait()

    out = _pcall(
        body, name=name, grid=grid,
        in_specs=[_tok_spec(tm, dh.shape[-1]) for dh in dhs] + [_whole(w, True) for w in ws]
        + [_tok_spec(tm, kout)] * nacc
        + [pl.BlockSpec((None, q.shape[1], tm // q.shape[1], q.shape[3]), lambda b, i: (b, 0, i, 0))
           for q, _ in own_order_dhs]
        + [_whole(w, True) for _, w in own_order_dhs]
        + [ANY] * ns,
        out_specs=[_tok_spec(tm, kout)] + [ANY] * ns,
        out_shape=[jax.ShapeDtypeStruct((nb, seq, kout), F32)]
        + [jax.ShapeDtypeStruct((3,) + s.shape[1:], s.dtype) for s in sums],
        input_output_aliases={2 * nd: 0} if nacc else {},
        scratch_shapes=[pltpu.VMEM((tm, LANE), F32)] * ntile
        + ([pltpu.SemaphoreType.DMA((n_sem,)), pltpu.SemaphoreType.DMA((n_sem,))] if ns else []),
        compiler_params=pltpu.CompilerParams(
            dimension_semantics=("arbitrary", "arbitrary") if ns else ("parallel", "parallel"),
            vmem_limit_bytes=VMEM_LIMIT_BYTES, has_side_effects=bool(ns)),
    )(*dhs, *ws, *accs, *[q for q, _ in own_order_dhs], *[w for _, w in own_order_dhs], *sums)
    return (out[0], list(out[1:])) if ns else out[0]


def _dw(a3, dhs, out_dtype, name, rows_are_outputs=False):
    nb, seq, kdim = a3.shape
    nd = len(dhs)
    grid = (nb, seq // MM_TM)
    shapes = [(dh.shape[-1], kdim) if rows_are_outputs else (kdim, dh.shape[-1]) for dh in dhs]

    def body(*refs):
        b, i = pl.program_id(0), pl.program_id(1)
        dh_refs, o_refs, acc_refs = refs[1:1 + nd], refs[1 + nd:1 + 2 * nd], refs[1 + 2 * nd:]

        @pl.when((b == 0) & (i == 0))
        def _():
            for acc_ref in acc_refs:
                acc_ref[...] = jnp.zeros_like(acc_ref)

        a = refs[0][...].astype(BF16)
        for dh_ref, acc_ref in zip(dh_refs, acc_refs):
            dh = dh_ref[...].astype(BF16)
            acc_ref[...] += lax.dot_general(*((dh, a) if rows_are_outputs else (a, dh)), TN,
                                            preferred_element_type=F32)

        @pl.when((b == grid[0] - 1) & (i == grid[1] - 1))
        def _():
            for o_ref, acc_ref in zip(o_refs, acc_refs):
                o_ref[...] = acc_ref[...].astype(out_dtype)

    return _pcall(
        body, name=name, grid=grid,
        in_specs=[_tok_spec(MM_TM, kdim)] + [_tok_spec(MM_TM, dh.shape[-1]) for dh in dhs],
        out_specs=[pl.BlockSpec(s, lambda b, i: (0, 0)) for s in shapes],
        out_shape=[jax.ShapeDtypeStruct(s, out_dtype) for s in shapes],
        scratch_shapes=[pltpu.VMEM(s, F32) for s in shapes],
        compiler_params=_params("arbitrary", "arbitrary"),
    )(a3, *dhs)


def _dw_stacked(groups, out_dtype, name):
    pairs = [pr for g in groups for pr in g]
    nb, seq, _ = pairs[0][0].shape
    npair, ng = len(pairs), len(groups)
    grid = (nb, seq // MM_TM)

    def body(*refs):
        b, i = pl.program_id(0), pl.program_id(1)
        o_refs, acc_refs = refs[2 * npair:2 * npair + ng], refs[2 * npair + ng:]

        @pl.when((b == 0) & (i == 0))
        def _():
            for acc_ref in acc_refs:
                acc_ref[...] = jnp.zeros_like(acc_ref)

        for j, acc_ref in enumerate(acc_refs):
            acc_ref[...] += lax.dot_general(refs[2 * j][...].astype(BF16), refs[2 * j + 1][...].astype(BF16), TN,
                                            preferred_element_type=F32)

        @pl.when((b == grid[0] - 1) & (i == grid[1] - 1))
        def _():
            accs = iter(acc_refs)
            for o_ref, g in zip(o_refs, groups):
                row = 0
                for a3, _ in g:
                    o_ref[row:row + a3.shape[-1], :] = next(accs)[...].astype(out_dtype)
                    row += a3.shape[-1]

    shapes = [(sum(a3.shape[-1] for a3, _ in g), g[0][1].shape[-1]) for g in groups]
    return _pcall(
        body, name=name, grid=grid,
        in_specs=[_tok_spec(MM_TM, t.shape[-1]) for pr in pairs for t in pr],
        out_specs=[pl.BlockSpec(s, lambda b, i: (0, 0)) for s in shapes],
        out_shape=[jax.ShapeDtypeStruct(s, out_dtype) for s in shapes],
        scratch_shapes=[pltpu.VMEM((a3.shape[-1], dh.shape[-1]), F32) for a3, dh in pairs],
        compiler_params=_params("arbitrary", "arbitrary"),
    )(*[t for pr in pairs for t in pr])


def _qkv_rows(g):
    return [(part * QKV_W + g * ATT_OUT + hp * LANE, LANE) for hp in range(ATT_OUT // LANE) for part in range(3)]


XBC_START = 3 * QKV_W + ATT_OUT + D_INNER
GROUP_CH = GROUP_SSM_HEADS * HEAD_DIM
XBC_GROUP = GROUP_CH + 2 * D_STATE
CONV_DIM = SSM_GROUPS * XBC_GROUP


def _xbc_ranges():
    out = []
    for g in range(SSM_GROUPS):
        out += [(g * GROUP_CH, GROUP_CH), (D_INNER + g * D_STATE, D_STATE),
                (D_INNER + SSM_GROUPS * D_STATE + g * D_STATE, D_STATE)]
    return out


def _join_last(parts):
    widths = [t.shape[-1] for t in parts]
    total, lead = sum(widths), [(0, 0)] * (parts[0].ndim - 1)
    starts = np.cumsum([0] + widths)
    return sum(jnp.pad(t, lead + [(int(s), total - int(s) - w)]) for t, s, w in zip(parts, starts, widths))


def _xbc_group_order(t):
    return _join_last([t[..., s:s + n] for s, n in _xbc_ranges()])


def _xbc_reference_order(t):
    g = lambda off, n: [t[..., k * XBC_GROUP + off:k * XBC_GROUP + off + n] for k in range(SSM_GROUPS)]
    return _join_last(g(0, GROUP_CH) + g(GROUP_CH, D_STATE) + g(GROUP_CH + D_STATE, D_STATE))


def _segments():
    one = lambda name, start, rows: (name, [(start, rows)], max(rows, LANE))
    return [("qkv%d" % g, _qkv_rows(g), QKV_W) for g in range(3)] + [
        one("gatt", 3 * QKV_W, ATT_OUT), one("z", 3 * QKV_W + ATT_OUT, D_INNER),
        ("xbc", [(XBC_START + s, n) for s, n in _xbc_ranges()], CONV_DIM), one("dt", DT_COL, SSM_HEADS),
        one("gm", DT_COL + SSM_HEADS, 2 * D_MODEL), one("gp", DT_COL + SSM_HEADS + 2 * D_MODEL, D_MODEL)]


LAYOUT_TC = 256
NAT_GROUPS = ((("gatt", "z", "dt", "gp"), 512), (("xbc", "gm"), 512))
DW_GROUPS = (("gatt", "z", "dt", "gp"), ("xbc",), ("gm",))


def _assemble(win):
    segs = _segments()

    def body(win_ref, *outs):
        def pieces(start, rows):
            t, end = start, start + rows
            while t < end:
                k = min(t // SHARD_COLS, 3)
                shard_end = (k + 1) * SHARD_COLS
                if k < 3 and shard_end % ROW_TILE and t == shard_end - shard_end % ROW_TILE:
                    lo = t - _win_start(k)
                    yield win_ref[k, lo:lo + ROW_TILE, :] + win_ref[k + 1, 0:ROW_TILE, :]
                    t += ROW_TILE
                    continue
                upto = min(end, shard_end - shard_end % ROW_TILE if k < 3 else end)
                yield win_ref[k, t - _win_start(k):upto - _win_start(k), :]
                t = upto

        for (_, ranges, total), o_ref in zip(segs, outs):
            off = 0
            for start, rows in ranges:
                for part in pieces(start, rows):
                    o_ref[off:off + part.shape[0], :] = part
                    off += part.shape[0]
            if off < total:
                o_ref[off:total, :] = jnp.zeros((total - off, o_ref.shape[1]), BF16)

    outs = _pcall(
        body, name="assemble_w_in", grid=(D_MODEL // LAYOUT_TC,),
        in_specs=[pl.BlockSpec((4, WIN_ROWS, LAYOUT_TC), lambda i: (0, 0, i))],
        out_specs=[pl.BlockSpec((total, LAYOUT_TC), lambda i: (0, i)) for _, _, total in segs],
        out_shape=[jax.ShapeDtypeStruct((total, D_MODEL), BF16) for _, _, total in segs],
        compiler_params=_params("parallel"),
    )(win)
    return {name: o for (name, _, _), o in zip(segs, outs)}


def _pack(dsegs):
    segs = _segments()

    def body(*refs):
        ins, o_ref = refs[:-1], refs[-1]
        tail = IN_COLS - _win_start(3)
        o_ref[3, tail:, :] = jnp.zeros((WIN_ROWS - tail, o_ref.shape[2]), BF16)
        for (_, ranges, _), s_ref in zip(segs, ins):
            off = 0
            for start, rows in ranges:
                for k in range(4):
                    lo = _win_start(k)
                    a, b = max(start, lo), min(start + rows, lo + WIN_ROWS)
                    if a < b:
                        o_ref[k, a - lo:b - lo, :] = s_ref[off + a - start:off + b - start, :]
                off += rows

    return _pcall(
        body, name="pack_dw_in", grid=(D_MODEL // LAYOUT_TC,),
        in_specs=[pl.BlockSpec((total, LAYOUT_TC), lambda i: (0, i)) for _, _, total in segs],
        out_specs=pl.BlockSpec((4, WIN_ROWS, LAYOUT_TC), lambda i: (0, 0, i)),
        out_shape=jax.ShapeDtypeStruct((4, WIN_ROWS, D_MODEL), BF16),
        compiler_params=_params("parallel"),
    )(*[dsegs[name] for name, _, _ in segs])


def _shard_to_window(shard_t, k):
    def at(off):
        return lambda w: jnp.pad(w.astype(BF16), ((off, WIN_ROWS - SHARD_COLS - off), (0, 0)))

    return lax.cond(k % 2 == 1, at(_win_offset(1)), at(_win_offset(0)), shard_t)


def _window_to_shard(win, k):
    return lax.dynamic_slice(win, ((k % 2) * _win_offset(1), 0), (SHARD_COLS, D_MODEL))


def _bucket_maps():
    qi = np.arange(8)[:, None]
    kj = np.arange(2 * BAND)[None, :]
    delta = qi + BAND - kj
    maps = []
    for window, dil in PATTERNS:
        valid = (delta >= 0) & (delta <= window // dil)
        dist = np.maximum(delta, 0) * dil
        max_exact = NUM_BUCKETS // 2
        d_f = np.maximum(dist, 1).astype(np.float32)
        large = max_exact + (np.log(d_f / np.float32(max_exact)) / np.float32(math.log(MAX_DISTANCE / max_exact))
                             * np.float32(NUM_BUCKETS - max_exact)).astype(np.int32)
        large = np.minimum(large, NUM_BUCKETS - 1)
        bucket = np.where(dist < max_exact, dist, large)
        maps.append(np.where(valid, bucket, -1).astype(np.int32))
    return np.stack(maps)


def _bias_tables(rel_bias, bmaps):
    def body(rb_ref, bm_ref, o_ref):
        g = pl.program_id(0)
        bm = bm_ref[...]
        for hh in range(GROUP_HEADS):
            acc = jnp.full(bm.shape, NEG, F32)
            for b in range(NUM_BUCKETS):
                acc = jnp.where(bm == b, rb_ref[b, g * GROUP_HEADS + hh], acc)
            for a in range(BAND // 8):
                o_ref[hh, 8 * a:8 * a + 8, :] = acc if a == 0 else pltpu.roll(acc, 8 * a, 1)

    return _pcall(
        body, name="bias_tables", grid=(3,),
        in_specs=[pl.BlockSpec(memory_space=pltpu.SMEM),
                  pl.BlockSpec((None, 8, 2 * BAND), lambda g: (g, 0, 0))],
        out_specs=pl.BlockSpec((GROUP_HEADS, BAND, 2 * BAND), lambda g: (g, 0, 0)),
        out_shape=jax.ShapeDtypeStruct((3 * GROUP_HEADS, BAND, 2 * BAND), F32),
        compiler_params=_params("parallel"),
    )(rel_bias, bmaps)


def _bias_grad(dbias, bmaps):
    def body(db_ref, bm_ref, o_ref):
        bm = bm_ref[...]
        lane = lax.broadcasted_iota(jnp.int32, (1, LANE), 1)
        for hh in range(GROUP_HEADS):
            db = db_ref[hh, 0:8, :]
            for a in range(1, BAND // 8):
                db = db + pltpu.roll(db_ref[hh, 8 * a:8 * a + 8, :], 2 * BAND - 8 * a, 1)
            vec = jnp.zeros((1, LANE), F32)
            for b in range(NUM_BUCKETS):
                s = jnp.sum(jnp.where(bm == b, db, 0.0), keepdims=True)
                vec = jnp.where(lane == b, s, vec)
            o_ref[hh] = vec

    return _pcall(
        body, name="bias_grad", grid=(3,),
        in_specs=[pl.BlockSpec((GROUP_HEADS, BAND, 2 * BAND), lambda g: (g, 0, 0)),
                  pl.BlockSpec((None, 8, 2 * BAND), lambda g: (g, 0, 0))],
        out_specs=pl.BlockSpec((GROUP_HEADS, 1, LANE), lambda g: (g, 0, 0)),
        out_shape=jax.ShapeDtypeStruct((3 * GROUP_HEADS, 1, LANE), F32),
        compiler_params=_params("parallel"),
    )(dbias, bmaps)


def _rows(n):
    if isinstance(n, int):
        return pl.ds(n * BAND, BAND)
    return pl.ds(pl.multiple_of(n * BAND, BAND), BAND)


def _for_blocks(blocks, nblk, per, carry):
    carry = blocks([0], carry, False)
    start = 1 + (nblk - 1) % per
    for n in range(1, start):
        carry = blocks([n], carry, True)
    trips = (nblk - start) // per
    if trips > 0:
        carry = lax.fori_loop(
            0, trips, lambda t, c: blocks([start + t * per + u for u in range(per)], c, True), carry)
    return carry


def _pairs_per_step(d):
    return {1: 3, 4: 6, 16: 6}[d]


def _bias_spec(group, hps):
    first = group * GROUP_HEADS // (2 * hps)
    return pl.BlockSpec((2 * hps, BAND, 2 * BAND), lambda hp, b, r: (first + hp, 0, 0))


def _attn_fwd(qkv4, bias, group, d, name):
    nb, _, sub, _ = qkv4.shape
    nblk = sub // BAND
    scale = HEAD_DIM ** -0.5
    npair = ATT_OUT // LANE
    hps = _pairs_per_step(d)
    compact = d > 1

    def body(qkv_ref, bias_ref, o_ref, l_ref):
        def blocks(ns, carry, with_prev):
            chains = [(bi, i, h) for bi in range(len(ns)) for i in range(hps) for h in range(2)]
            first_head = lax.broadcasted_iota(jnp.int32, (BAND, LANE), 1) < HEAD_DIM
            pair = lambda n, i, part: qkv_ref[_rows(n), (3 * i + part) * LANE:(3 * i + part + 1) * LANE]
            scores = []
            for bi, i, h in chains:
                n = ns[bi]
                qp = pair(n, i, 0) * scale
                q = jnp.where(first_head if h == 0 else jnp.logical_not(first_head), qp, jnp.zeros_like(qp))
                s_c = lax.dot_general(q, pair(n, i, 1), NT, preferred_element_type=F32) + bias_ref[2 * i + h, :, BAND:]
                s_p = None
                if with_prev:
                    s_p = lax.dot_general(q, pair(n - 1, i, 1), NT,
                                          preferred_element_type=F32) + bias_ref[2 * i + h, :, :BAND]
                scores.append((s_c, s_p))
            probs = []
            for s_c, s_p in scores:
                m = jnp.max(s_c, -1, keepdims=True)
                if with_prev:
                    m = jnp.maximum(m, jnp.max(s_p, -1, keepdims=True))
                e_c = jnp.exp(s_c - m)
                den = jnp.sum(e_c, -1, keepdims=True)
                e_p = None
                if with_prev:
                    e_p = jnp.exp(s_p - m)
                    den = den + jnp.sum(e_p, -1, keepdims=True)
                    e_p = e_p.astype(BF16)
                probs.append((e_c.astype(BF16), e_p, den, m))
            outs = {}
            for (bi, i, h), (e_c, e_p, den, m) in zip(chains, probs):
                n = ns[bi]
                acc = jnp.dot(e_c, pair(n, i, 2), preferred_element_type=F32)
                if with_prev:
                    acc = acc + jnp.dot(e_p, pair(n - 1, i, 2), preferred_element_type=F32)
                outs[(bi, i, h)] = (acc / den, m + jnp.log(den))
            lane = lax.broadcasted_iota(jnp.int32, (BAND, LANE), 1)
            for bi, n in enumerate(ns):
                per_head = jnp.zeros((BAND, LANE), F32)
                for i in range(hps):
                    o_ref[_rows(n), i * LANE:(i + 1) * LANE] = jnp.where(first_head, outs[(bi, i, 0)][0],
                                                                         outs[(bi, i, 1)][0])
                    if compact:
                        for h in range(2):
                            per_head = jnp.where(lane == 2 * i + h, outs[(bi, i, h)][1], per_head)
                    else:
                        l_ref[_rows(n), i * LANE:(i + 1) * LANE] = jnp.where(first_head, outs[(bi, i, 0)][1],
                                                                             outs[(bi, i, 1)][1])
                if compact:
                    l_ref[_rows(n), :] = per_head
            return carry

        _for_blocks(blocks, nblk, 2 if hps == 1 else 1, 0)

    in_specs = [pl.BlockSpec((None, None, sub, 3 * LANE * hps), lambda hp, b, r: (b, r, 0, hp)),
                _bias_spec(group, hps)]
    if compact:
        return _pcall(
            body, name=name, grid=(1, nb, d), in_specs=in_specs,
            out_specs=[pl.BlockSpec((None, None, sub, ATT_OUT), lambda hp, b, r: (b, r, 0, 0)),
                       pl.BlockSpec((None, None, sub, LANE), lambda hp, b, r: (b, r, 0, 0))],
            out_shape=[jax.ShapeDtypeStruct((nb, d, sub, ATT_OUT), F32), jax.ShapeDtypeStruct((nb, d, sub, LANE), F32)],
            compiler_params=_params("parallel", "parallel", "parallel"),
        )(qkv4, bias)
    ospec = pl.BlockSpec((None, sub, hps * LANE), lambda hp, b, r: (b, 0, r * (npair // hps) + hp))
    return _pcall(
        body, name=name, grid=(npair // hps, nb, d), in_specs=in_specs, out_specs=[ospec, ospec],
        out_shape=[jax.ShapeDtypeStruct((nb, sub, d * ATT_OUT), F32)] * 2,
        compiler_params=_params("parallel", "parallel", "parallel"),
    )(qkv4, bias)


STAT_LSE_LANE = 16


def _attn_bwd(qkv4, bias, group, cotangent, d, name):
    nb, _, sub, _ = qkv4.shape
    nblk = sub // BAND
    scale = HEAD_DIM ** -0.5
    npair = ATT_OUT // LANE
    hps = _pairs_per_step(d)
    compact = d > 1

    def body(qkv_ref, bias_ref, *rest):
        do_ref, dqkv_ref, db_ref = rest[0], rest[-2], rest[-1]
        b, r = pl.program_id(1), pl.program_id(2)

        @pl.when((b == 0) & (r == 0))
        def _():
            db_ref[...] = jnp.zeros_like(db_ref)

        def blocks(ns, carry, with_prev):
            sides = (0, 1) if with_prev else (0,)
            chains = [(bi, i, h, sd) for bi in range(len(ns)) for i in range(hps) for h in range(2) for sd in sides]
            first_head = lax.broadcasted_iota(jnp.int32, (BAND, LANE), 1) < HEAD_DIM
            own = lambda h, t: jnp.where(first_head if h == 0 else jnp.logical_not(first_head), t, jnp.zeros_like(t))
            pair = lambda rows, i, part: qkv_ref[rows, (3 * i + part) * LANE:(3 * i + part + 1) * LANE]
            key_rows = lambda bi, sd: _rows(ns[bi] - sd)
            qs = {}
            for bi in range(len(ns)):
                for i in range(hps):
                    q_pair = pair(_rows(ns[bi]), i, 0) * scale
                    do = do_ref[_rows(ns[bi]), i * LANE:(i + 1) * LANE]
                    do16 = do.astype(BF16)
                    for h in range(2):
                        if compact:
                            st_ref, head = rest[1], 2 * i + h
                            ebar = st_ref[_rows(ns[bi]), head:head + 1]
                            lcol = st_ref[_rows(ns[bi]), STAT_LSE_LANE + head:STAT_LSE_LANE + head + 1]
                        else:
                            ebar = jnp.sum(own(h, do * rest[1][_rows(ns[bi]), i * LANE:(i + 1) * LANE]), -1, keepdims=True)
                            lcol = rest[2][_rows(ns[bi]), i * LANE + h * HEAD_DIM:i * LANE + h * HEAD_DIM + 1]
                        qs[(bi, i, h)] = (own(h, q_pair), q_pair, own(h, do16), do16, ebar, lcol)
            raw = []
            for bi, i, h, sd in chains:
                q, _, do_h, _, _, _ = qs[(bi, i, h)]
                bias_blk = bias_ref[2 * i + h, :, :BAND] if sd else bias_ref[2 * i + h, :, BAND:]
                s = lax.dot_general(q, pair(key_rows(bi, sd), i, 1), NT, preferred_element_type=F32) + bias_blk
                dp = lax.dot_general(do_h, pair(key_rows(bi, sd), i, 2), NT, preferred_element_type=F32)
                raw.append((s, dp))
            soft = []
            for (bi, i, h, sd), (s, dp) in zip(chains, raw):
                ebar, lcol = qs[(bi, i, h)][4:]
                p = jnp.exp(s - lcol)
                ds = p * (dp - ebar)
                if sd:
                    db_ref[2 * i + h, :, :BAND] += ds
                else:
                    db_ref[2 * i + h, :, BAND:] += ds
                soft.append((p.astype(BF16), ds.astype(BF16)))
            grads = {}
            for (bi, i, h, sd), (p16, ds16) in zip(chains, soft):
                _, q_pair, _, do16 = qs[(bi, i, h)][:4]
                grads[(bi, i, h, sd)] = (
                    jnp.dot(ds16, pair(key_rows(bi, sd), i, 1), preferred_element_type=F32),
                    lax.dot_general(ds16, q_pair, TN, preferred_element_type=F32),
                    lax.dot_general(p16, do16, TN, preferred_element_type=F32))
            both = lambda bi, i, sd, which: jnp.where(first_head, grads[(bi, i, 0, sd)][which],
                                                      grads[(bi, i, 1, sd)][which])
            carry = list(carry) if carry is not None else None
            for bi, n in enumerate(ns):
                for i in range(hps):
                    base = 3 * LANE * i
                    dq = both(bi, i, 0, 0)
                    if with_prev:
                        dq = dq + both(bi, i, 1, 0)
                        dqkv_ref[_rows(n - 1), base + LANE:base + 2 * LANE] = (
                            carry[2 * i] + both(bi, i, 1, 1)).astype(BF16)
                        dqkv_ref[_rows(n - 1), base + 2 * LANE:base + 3 * LANE] = (
                            carry[2 * i + 1] + both(bi, i, 1, 2)).astype(BF16)
                    dqkv_ref[_rows(n), base:base + LANE] = (dq * scale).astype(BF16)
                carry = [t for i in range(hps) for t in (both(bi, i, 0, 1), both(bi, i, 0, 2))]
            return tuple(carry)

        carry = _for_blocks(blocks, nblk, 2 if hps == 1 else 1, None)
        for i in range(hps):
            base = 3 * LANE * i
            dqkv_ref[_rows(nblk - 1), base + LANE:base + 2 * LANE] = carry[2 * i].astype(BF16)
            dqkv_ref[_rows(nblk - 1), base + 2 * LANE:base + 3 * LANE] = carry[2 * i + 1].astype(BF16)

    qspec = pl.BlockSpec((None, None, sub, 3 * LANE * hps), lambda hp, b, r: (b, r, 0, hp))
    bspec = pl.BlockSpec((2 * hps, BAND, 2 * BAND), lambda hp, b, r: (hp, 0, 0))
    if compact:
        cspecs = [pl.BlockSpec((None, None, sub, ATT_OUT), lambda hp, b, r: (b, r, 0, 0)),
                  pl.BlockSpec((None, None, sub, LANE), lambda hp, b, r: (b, r, 0, 0))]
    else:
        cspecs = [pl.BlockSpec((None, sub, hps * LANE), lambda hp, b, r: (b, 0, r * (npair // hps) + hp))] * 3
    return _pcall(
        body, name=name, grid=(npair // hps, nb, d),
        in_specs=[qspec, _bias_spec(group, hps)] + cspecs, out_specs=[qspec, bspec],
        out_shape=[jax.ShapeDtypeStruct(qkv4.shape, BF16),
                   jax.ShapeDtypeStruct((GROUP_HEADS, BAND, 2 * BAND), F32)],
        compiler_params=_params("parallel", "arbitrary", "arbitrary"),
    )(qkv4, bias, *cotangent)


def _head_lanes(first_lane, one_channel):
    c = lax.broadcasted_iota(jnp.int32, (ATT_OUT, LANE), 0)
    lane = lax.broadcasted_iota(jnp.int32, (ATT_OUT, LANE), 1)
    hit = lane == first_lane + c // HEAD_DIM
    if one_channel:
        hit = hit & (c % HEAD_DIM == 0)
    return hit.astype(BF16)


def _exact_dot(v, m01, dims=None):
    parts = _split3(v)
    if dims is None:
        dot = lambda t: jnp.dot(t, m01, preferred_element_type=F32)
    else:
        dot = lambda t: lax.dot_general(t, m01, dims, preferred_element_type=F32)
    return (dot(parts[0]) + dot(parts[1])) + dot(parts[2])


def _store_own_order(value, tile_refs, out_ref):
    d, per, width = out_ref.shape
    for j in range(width // LANE):
        tile_refs[j][...] = value[:, j * LANE:(j + 1) * LANE]
    for r in range(d):
        rows = pl.ds(r, per, stride=d)
        for j in range(width // LANE):
            out_ref[r, :, j * LANE:(j + 1) * LANE] = tile_refs[j][rows, :].astype(out_ref.dtype)


def _natural_rows(p_ref, tile_refs):
    d, per, width = p_ref.shape
    for r in range(d):
        rows = pl.ds(r, per, stride=d)
        for j in range(width // LANE):
            tile_refs[j][rows, :] = p_ref[r, :, j * LANE:(j + 1) * LANE]
    return jnp.concatenate([tile_refs[j][...] for j in range(width // LANE)], axis=1)


def _natural_value(value, d, tile_refs):
    total, width = value.shape
    per = total // d
    for r in range(d):
        rows = pl.ds(r, per, stride=d)
        for j in range(width // LANE):
            tile_refs[j][rows, :] = value[r * per:(r + 1) * per, j * LANE:(j + 1) * LANE]
    return jnp.concatenate([tile_refs[j][...] for j in range(width // LANE)], axis=1)


def _combine_fwd(o0, l0, dilated, gatt):
    nb, seq, _ = gatt.shape
    tm = 512
    ntile = ATT_OUT // LANE

    def body(o0_ref, l0_ref, o1_ref, l1_ref, o2_ref, l2_ref, g_ref, oa_ref, oatt_ref, lse_ref, *tile_refs):
        spread = _head_lanes(0, False)
        l0v = l0_ref[...]
        l1v = _exact_dot(_natural_rows(l1_ref, tile_refs), spread, NT)
        l2v = _exact_dot(_natural_rows(l2_ref, tile_refs), spread, NT)
        m = jnp.maximum(jnp.maximum(l0v, l1v), l2v)
        tot = m + jnp.log(jnp.exp(l0v - m) + jnp.exp(l1v - m) + jnp.exp(l2v - m))
        o = jnp.exp(l0v - tot) * o0_ref[...]
        o = o + jnp.exp(l1v - tot) * _natural_rows(o1_ref, tile_refs)
        o = o + jnp.exp(l2v - tot) * _natural_rows(o2_ref, tile_refs)
        g = g_ref[...]
        oa_ref[...] = (o * (g * _sigmoid(g))).astype(BF16)
        oatt_ref[...] = o
        lse_ref[...] = tot

    spec = pl.BlockSpec((None, tm, ATT_OUT), lambda b, i: (b, i, 0))
    own = lambda t: pl.BlockSpec((None, t.shape[1], tm // t.shape[1], t.shape[3]), lambda b, i: (b, 0, i, 0))
    (o1, l1), (o2, l2) = dilated
    return _pcall(
        body, name="attn_combine", grid=(nb, seq // tm),
        in_specs=[spec, spec, own(o1), own(l1), own(o2), own(l2), spec], out_specs=[spec] * 3,
        out_shape=[jax.ShapeDtypeStruct((nb, seq, ATT_OUT), BF16), jax.ShapeDtypeStruct((nb, seq, ATT_OUT), F32),
                   jax.ShapeDtypeStruct((nb, seq, ATT_OUT), F32)],
        scratch_shapes=[pltpu.VMEM((tm, LANE), F32)] * ntile,
        compiler_params=_params("parallel", "parallel"),
    )(o0, l0, o1, l1, o2, l2, gatt)


def _combine_bwd(dya16, w_bra, gatt, o_att, lse, dilations):
    nb, seq, _ = gatt.shape
    tm = 512

    def body(dya_ref, w_ref, g_ref, o_ref, l_ref, do_ref, dg_ref, *rest):
        ntile = ATT_OUT // LANE
        outs, tile_refs = rest[:-ntile], rest[-ntile:]
        doa = lax.dot_general(dya_ref[...], w_ref[...], NT, preferred_element_type=F32)
        g = g_ref[...]
        sg = _sigmoid(g)
        do = doa * (g * sg)
        do_ref[...] = do
        stats = (_exact_dot(do * o_ref[...], _head_lanes(0, False))
                 + _exact_dot(l_ref[...], _head_lanes(STAT_LSE_LANE, True)))
        dg_ref[...] = (doa * o_ref[...] * (sg * (1.0 + g * (1.0 - sg)))).astype(BF16)
        for k in range(len(dilations)):
            _store_own_order(do, tile_refs, outs[2 * k])
            _store_own_order(stats, tile_refs, outs[2 * k + 1])

    spec = pl.BlockSpec((None, tm, ATT_OUT), lambda b, i: (b, i, 0))
    own = lambda d, width: pl.BlockSpec((None, d, tm // d, width), lambda b, i: (b, 0, i, 0))
    outs = _pcall(
        body, name="attn_combine_bwd", grid=(nb, seq // tm),
        in_specs=[_tok_spec(tm, D_MODEL), _whole(w_bra, True)] + [spec] * 3,
        out_specs=[spec, spec] + [own(d, w) for d in dilations for w in (ATT_OUT, LANE)],
        out_shape=[jax.ShapeDtypeStruct((nb, seq, ATT_OUT), F32), jax.ShapeDtypeStruct((nb, seq, ATT_OUT), BF16)]
        + [jax.ShapeDtypeStruct((nb, d, seq // d, w), t) for d in dilations for w, t in ((ATT_OUT, BF16), (LANE, F32))],
        scratch_shapes=[pltpu.VMEM((tm, LANE), F32)] * (ATT_OUT // LANE),
        compiler_params=_params("parallel", "parallel"),
    )(dya16, w_bra, gatt, o_att, lse)
    return outs[0], outs[1], outs[2:]


CONV_TM = 1024
CONV_TC = 1024


def _shift_down(cur, halo, k):
    rolled = pltpu.roll(cur, k, 0)
    hro = pltpu.roll(halo, k, 0)
    row = lax.broadcasted_iota(jnp.int32, hro.shape, 0)
    return jnp.concatenate([jnp.where(row < k, hro, rolled[:8]), rolled[8:]], axis=0)


def _shift_up(cur, halo, k):
    n = cur.shape[0]
    rolled = pltpu.roll(cur, n - k, 0)
    hro = pltpu.roll(halo, 8 - k, 0)
    row = lax.broadcasted_iota(jnp.int32, hro.shape, 0)
    return jnp.concatenate([rolled[:n - 8], jnp.where(row >= 8 - k, hro, rolled[n - 8:])], axis=0)


def _conv_pre(cur, halo, w_ref, b_ref):
    acc = cur * w_ref[3:4, :] + b_ref[...]
    for k in range(1, 4):
        acc = acc + _shift_down(cur, halo, k) * w_ref[3 - k:4 - k, :]
    return acc


def _conv_specs(seq):
    nblk = seq // CONV_TM
    cur = pl.BlockSpec((None, CONV_TM, CONV_TC), lambda cb, b, i: (b, i, cb))
    prev = pl.BlockSpec((None, 8, CONV_TC), lambda cb, b, i: (b, jnp.maximum(i * (CONV_TM // 8) - 1, 0), cb))
    nxt = pl.BlockSpec((None, 8, CONV_TC),
                       lambda cb, b, i: (b, jnp.minimum((i + 1) * (CONV_TM // 8), seq // 8 - 1), cb))
    wspec = pl.BlockSpec((4, CONV_TC), lambda cb, b, i: (0, cb))
    bspec = pl.BlockSpec((1, CONV_TC), lambda cb, b, i: (0, cb))
    return nblk, cur, prev, nxt, wspec, bspec


def _conv_fwd(xin, w4, bias, name):
    nb, seq, ch = xin.shape
    _, cur, prev, _, wspec, bspec = _conv_specs(seq)

    def body(x_ref, h_ref, w_ref, b_ref, o_ref):
        halo = jnp.where(pl.program_id(2) > 0, h_ref[...], 0.0)
        pre = _conv_pre(x_ref[...], halo, w_ref, b_ref)
        o_ref[...] = pre * _sigmoid(pre)

    return _pcall(
        body, name=name, grid=(ch // CONV_TC, nb, seq // CONV_TM),
        in_specs=[cur, prev, wspec, bspec], out_specs=cur,
        out_shape=jax.ShapeDtypeStruct(xin.shape, F32),
        compiler_params=_params("parallel", "parallel", "parallel"),
    )(xin, xin, w4, bias)


def _conv_bwd_pre(dact, xin, w4, bias, name):
    nb, seq, ch = xin.shape
    _, cur, prev, _, wspec, bspec = _conv_specs(seq)

    def body(da_ref, x_ref, h_ref, w_ref, b_ref, dp_ref, s_ref):
        b, i = pl.program_id(1), pl.program_id(2)

        @pl.when((b == 0) & (i == 0))
        def _():
            s_ref[...] = jnp.zeros_like(s_ref)

        halo = jnp.where(i > 0, h_ref[...], 0.0)
        x = x_ref[...]
        pre = _conv_pre(x, halo, w_ref, b_ref)
        sg = _sigmoid(pre)
        dpre = da_ref[...] * (sg * (1.0 + pre * (1.0 - sg)))
        dp_ref[...] = dpre
        s_ref[3:4, :] += jnp.sum(dpre * x, 0, keepdims=True)
        for k in range(1, 4):
            s_ref[3 - k:4 - k, :] += jnp.sum(dpre * _shift_down(x, halo, k), 0, keepdims=True)
        s_ref[4:5, :] += jnp.sum(dpre, 0, keepdims=True)

    return _pcall(
        body, name=name, grid=(ch // CONV_TC, nb, seq // CONV_TM),
        in_specs=[cur, cur, prev, wspec, bspec],
        out_specs=[cur, pl.BlockSpec((8, CONV_TC), lambda cb, b, i: (0, cb))],
        out_shape=[jax.ShapeDtypeStruct(xin.shape, F32), jax.ShapeDtypeStruct((8, ch), F32)],
        compiler_params=_params("parallel", "arbitrary", "arbitrary"),
    )(dact, xin, xin, w4, bias)


def _conv_bwd_x(dpre, w4, name):
    nb, seq, ch = dpre.shape
    nblk, cur, _, nxt, wspec, _ = _conv_specs(seq)

    def body(d_ref, n_ref, w_ref, o_ref):
        halo = jnp.where(pl.program_id(2) < nblk - 1, n_ref[...], 0.0)
        cur_v = d_ref[...]
        acc = cur_v * w_ref[3:4, :]
        for j in range(1, 4):
            acc = acc + _shift_up(cur_v, halo, j) * w_ref[3 - j:4 - j, :]
        o_ref[...] = acc.astype(BF16)

    return _pcall(
        body, name=name, grid=(ch // CONV_TC, nb, seq // CONV_TM),
        in_specs=[cur, nxt, wspec], out_specs=cur,
        out_shape=jax.ShapeDtypeStruct(dpre.shape, BF16),
        compiler_params=_params("parallel", "parallel", "parallel"),
    )(dpre, dpre, w4)


def _step_sizes(raw, tb_ref):
    shift = (LANE - GROUP_SSM_HEADS * pl.program_id(0)) % LANE
    v = pltpu.roll(raw + tb_ref[...], shift, 1)
    own = lax.broadcasted_iota(jnp.int32, v.shape, 1) < GROUP_SSM_HEADS
    sp = jnp.maximum(v, 0.0) + jnp.log1p(jnp.exp(-jnp.abs(v)))
    return jnp.where(own, sp, 0.0), jnp.where(own, _sigmoid(v), 0.0)


def _group_lanes(t):
    pads = [(0, 0)] * (t.ndim - 1) + [(0, LANE - GROUP_SSM_HEADS)]
    return jnp.stack([jnp.pad(t[..., GROUP_SSM_HEADS * g:GROUP_SSM_HEADS * (g + 1)], pads) for g in range(SSM_GROUPS)])


def _ungroup_lanes(t):
    return jnp.concatenate([t[g][..., :GROUP_SSM_HEADS] for g in range(SSM_GROUPS)], axis=-1)


def _decays(dt, al_ref):
    row = lax.broadcasted_iota(jnp.int32, (CHUNK, CHUNK), 0)
    col = lax.broadcasted_iota(jnp.int32, (CHUNK, CHUNK), 1)
    tril = (row >= col).astype(BF16)
    triu = (row <= col).astype(BF16)
    arow = -jnp.exp(al_ref[...])
    hi, mid, lo = _split3(dt * arow)
    down = lambda t: jnp.dot(tril, t, preferred_element_type=F32)
    across = lambda t: lax.dot_general(t, triu, TN, preferred_element_type=F32)
    acs = (down(hi) + down(mid)) + down(lo)
    acs_t = (across(hi) + across(mid)) + across(lo)
    return arow, acs, acs_t, row >= col, triu


STEP_CHUNKS = 8


def _ssd_specs(nb, seq):
    nc = seq // CHUNK
    hw = GROUP_SSM_HEADS * HEAD_DIM
    rows, steps = STEP_CHUNKS * CHUNK, nc // STEP_CHUNKS

    def mk(rev):
        cidx = (lambda c: steps - 1 - c) if rev else (lambda c: c)
        wide = pl.BlockSpec((None, rows, hw), lambda g, b, c: (b, cidx(c), g))
        xbc = pl.BlockSpec((None, rows, XBC_GROUP), lambda g, b, c: (b, cidx(c), g))
        lanes = pl.BlockSpec((None, None, rows, LANE), lambda g, b, c: (g, b, cidx(c), 0))
        prev = pl.BlockSpec((None, STEP_CHUNKS, None, D_STATE, hw), lambda g, b, c: (b, cidx(c), g, 0, 0))
        raw = pl.BlockSpec((None, rows, LANE), lambda g, b, c: (b, cidx(c), 0))
        return wide, xbc, lanes, prev, raw

    grow = pl.BlockSpec((None, 1, LANE), lambda g, b, c: (g, 0, 0))
    nwspec = pl.BlockSpec((1, hw), lambda g, b, c: (0, g))
    tbspec = pl.BlockSpec((1, LANE), lambda g, b, c: (0, 0))
    return nc, steps, hw, mk, grow, nwspec, tbspec


def _head_expand():
    hw = GROUP_SSM_HEADS * HEAD_DIM
    r = lax.broadcasted_iota(jnp.int32, (LANE, hw), 0)
    c = lax.broadcasted_iota(jnp.int32, (LANE, hw), 1)
    return ((c // HEAD_DIM) == r).astype(BF16)


def _split3(v):
    hi = v.astype(BF16)
    rest = v - hi.astype(F32)
    mid = rest.astype(BF16)
    return hi, mid, (rest - mid.astype(F32)).astype(BF16)


def _to_channels(v, e):
    hi, mid, lo = _split3(v)
    dot = lambda t: jnp.dot(t, e, preferred_element_type=F32)
    return (dot(hi) + dot(mid)) + dot(lo)


def _to_heads(w, e):
    hi, mid, lo = _split3(w)
    dot = lambda t: lax.dot_general(t, e, (((1,), (1,)), ((), ())), preferred_element_type=F32)
    return (dot(hi) + dot(mid)) + dot(lo)


def _row8(v):
    return jnp.broadcast_to(v, (8, v.shape[1]))


def _ssd_chunk_setup(dt, al_ref, ds_ref):
    arow, acs, acs_t, causal, triu = _decays(dt, al_ref)
    e = _head_expand()
    dtx = _to_channels(dt, e)
    acsx = _to_channels(acs, e)
    lastx = acsx[CHUNK - 1:CHUNK, :]
    dskx = _to_channels(_row8(ds_ref[...]), e)[0:1, :]
    return arow, acs, acs_t, causal, triu, e, dtx, acsx, lastx, dskx


def _ssd_fwd(xbc, dt_raw, dt_bias_row, z, alog_g, dskip_g, normw):
    nb, seq, _ = xbc.shape
    nc, steps, hw, mk, grow, nwspec, tbspec = _ssd_specs(nb, seq)
    wide, xbc_spec, lanes, prev, raw = mk(False)
    tn = (((0,), (0,)), ((), ()))

    def body(xbc_ref, dt_ref, tb_ref, z_ref, al_ref, ds_ref, nw_ref, ys_ref, y_ref, sp_ref, st_ref):
        @pl.when(pl.program_id(2) == 0)
        def _():
            st_ref[...] = jnp.zeros_like(st_ref)

        for ci in range(STEP_CHUNKS):
            chunk(ci, xbc_ref, dt_ref, tb_ref, z_ref, al_ref, ds_ref, nw_ref, ys_ref, y_ref, sp_ref, st_ref)

    def chunk(ci, xbc_ref, dt_ref, tb_ref, z_ref, al_ref, ds_ref, nw_ref, ys_ref, y_ref, sp_ref, st_ref):
        rows = slice(ci * CHUNK, (ci + 1) * CHUNK)
        dt, _ = _step_sizes(dt_ref[rows, :], tb_ref)
        _, acs, acs_t, causal, _, _, dtx, acsx, lastx, dskx = _ssd_chunk_setup(dt, al_ref, ds_ref)
        bmat = xbc_ref[rows, GROUP_CH:GROUP_CH + D_STATE].astype(BF16)
        cmat = xbc_ref[rows, GROUP_CH + D_STATE:].astype(BF16)
        cb = lax.dot_general(cmat, bmat, (((1,), (1,)), ((), ())), preferred_element_type=F32)
        x = xbc_ref[rows, :GROUP_CH]
        xdt = x * dtx
        xdt16 = xdt.astype(BF16)
        first_head = lax.broadcasted_iota(jnp.int32, (CHUNK, LANE), 1) < HEAD_DIM
        pairs = []
        for hp in range(GROUP_SSM_HEADS // 2):
            xp = xdt16[:, hp * LANE:(hp + 1) * LANE]
            two = []
            for j in (2 * hp, 2 * hp + 1):
                lmat = jnp.exp(jnp.where(causal, acs[:, j:j + 1] - acs_t[j:j + 1, :], -jnp.inf))
                two.append(jnp.dot((cb * lmat).astype(BF16), xp, preferred_element_type=F32))
            pairs.append(jnp.where(first_head, two[0], two[1]))
        yd = jnp.concatenate(pairs, axis=1)
        s_prev = st_ref[...]
        s16 = s_prev.astype(BF16)
        sp_ref[ci] = s16
        yo = jnp.dot(cmat, s16, preferred_element_type=F32) * jnp.exp(acsx)
        sts = lax.dot_general(bmat, (xdt * jnp.exp(lastx - acsx)).astype(BF16), tn, preferred_element_type=F32)
        st_ref[...] = s_prev * jnp.exp(lastx) + sts
        y = yd + yo + dskx * x
        zz = z_ref[rows, :]
        u = y * (zz * _sigmoid(zz))
        rn = lax.rsqrt(jnp.mean(u * u, -1, keepdims=True) + RMS_EPS)
        ys_ref[rows, :] = (u * rn * nw_ref[...]).astype(BF16)
        y_ref[rows, :] = y

    return _pcall(
        body, name="ssd_fwd", grid=(SSM_GROUPS, nb, steps),
        in_specs=[xbc_spec, raw, tbspec, wide, grow, grow, nwspec],
        out_specs=[wide, wide, prev],
        out_shape=[jax.ShapeDtypeStruct((nb, seq, D_INNER), BF16), jax.ShapeDtypeStruct((nb, seq, D_INNER), F32),
                   jax.ShapeDtypeStruct((nb, nc, SSM_GROUPS, D_STATE, hw), BF16)],
        scratch_shapes=[pltpu.VMEM((D_STATE, hw), F32)],
        compiler_params=_params("parallel", "parallel", "arbitrary"),
    )(xbc, dt_raw, dt_bias_row, z, alog_g, dskip_g, normw)


def _ssd_bwd(xbc, dt_raw, dt_bias_row, z, y, dys, sprev, alog_g, dskip_g, normw):
    nb, seq, _ = xbc.shape
    nc, steps, hw, mk, grow, nwspec, tbspec = _ssd_specs(nb, seq)
    wide, xbc_spec, lanes, prev, raw = mk(True)
    nt = (((1,), (1,)), ((), ()))
    tn = (((0,), (0,)), ((), ()))

    def body(xbc_ref, dt_ref, tb_ref, z_ref, y_ref, dys_ref, sp_ref, al_ref, ds_ref, nw_ref,
             dxbc_ref, ddt_ref, dz_ref, small_ref, dnw_ref, g_ref):
        b, c = pl.program_id(1), pl.program_id(2)

        @pl.when((b == 0) & (c == 0))
        def _():
            small_ref[...] = jnp.zeros_like(small_ref)
            dnw_ref[...] = jnp.zeros_like(dnw_ref)

        @pl.when(c == 0)
        def _():
            g_ref[...] = jnp.zeros_like(g_ref)

        for ci in reversed(range(STEP_CHUNKS)):
            chunk(ci, xbc_ref, dt_ref, tb_ref, z_ref, y_ref, dys_ref, sp_ref, al_ref, ds_ref, nw_ref,
                  dxbc_ref, ddt_ref, dz_ref, small_ref, dnw_ref, g_ref)

    def chunk(ci, xbc_ref, dt_ref, tb_ref, z_ref, y_ref, dys_ref, sp_ref, al_ref, ds_ref, nw_ref,
              dxbc_ref, ddt_ref, dz_ref, small_ref, dnw_ref, g_ref):
        rows = slice(ci * CHUNK, (ci + 1) * CHUNK)
        yv, zz, dys_v, nw = y_ref[rows, :], z_ref[rows, :], dys_ref[rows, :], nw_ref[...]
        sz = _sigmoid(zz)
        silu = zz * sz
        u = yv * silu
        rn = lax.rsqrt(jnp.mean(u * u, -1, keepdims=True) + RMS_EPS)
        gn = dys_v * nw
        du = rn * gn - u * (rn * rn * rn) * jnp.mean(u * gn, -1, keepdims=True)
        dnw_ref[...] += jnp.sum(dys_v * u * rn, 0, keepdims=True)
        dy = du * silu
        dz_ref[rows, :] = du * yv * (sz * (1.0 + zz * (1.0 - sz)))

        dt, sg = _step_sizes(dt_ref[rows, :], tb_ref)
        arow, acs, acs_t, causal, triu, e, dtx, acsx, lastx, dskx = _ssd_chunk_setup(dt, al_ref, ds_ref)
        dfsx = jnp.exp(acsx)
        dtex = jnp.exp(lastx - acsx)
        bmat = xbc_ref[rows, GROUP_CH:GROUP_CH + D_STATE].astype(BF16)
        cmat = xbc_ref[rows, GROUP_CH + D_STATE:].astype(BF16)
        cb = lax.dot_general(cmat, bmat, nt, preferred_element_type=F32)
        x = xbc_ref[rows, :GROUP_CH]
        xdt = x * dtx
        xdt16 = xdt.astype(BF16)
        xdte = xdt * dtex
        dy16 = dy.astype(BF16)
        dyd = dy * dfsx
        dyd16 = dyd.astype(BF16)
        s16 = sp_ref[ci]
        g = g_ref[...]
        g16 = g.astype(BF16)
        cs = jnp.dot(cmat, s16, preferred_element_type=F32)
        dc_off = lax.dot_general(dyd16, s16, nt, preferred_element_type=F32)
        g_here = lax.dot_general(cmat, dyd16, tn, preferred_element_type=F32)
        bg = jnp.dot(bmat, g16, preferred_element_type=F32)
        db_st = lax.dot_general(xdte.astype(BF16), g16, nt, preferred_element_type=F32)
        ddte_w = bg * xdte
        dcd = _to_heads(_row8(jnp.sum(g * s16.astype(F32), 0, keepdims=True)), e)[0:1, :]
        lane = lax.broadcasted_iota(jnp.int32, (CHUNK, LANE), 1)
        first_head = lane < HEAD_DIM
        sub = lax.broadcasted_iota(jnp.int32, (CHUNK, LANE), 0)
        dacs = jnp.zeros((CHUNK, LANE), F32)
        colsums = jnp.zeros((CHUNK, LANE), F32)
        dcb = jnp.zeros((CHUNK, CHUNK), F32)
        pairs = []
        for hp in range(GROUP_SSM_HEADS // 2):
            xp = xdt16[:, hp * LANE:(hp + 1) * LANE]
            dyp = dy16[:, hp * LANE:(hp + 1) * LANE]
            two = []
            for idx, j in enumerate((2 * hp, 2 * hp + 1)):
                lmat = jnp.exp(jnp.where(causal, acs[:, j:j + 1] - acs_t[j:j + 1, :], -jnp.inf))
                mf = cb * lmat
                dy_h = jnp.where(first_head if idx == 0 else jnp.logical_not(first_head), dyp, jnp.zeros_like(dyp))
                dm = lax.dot_general(dy_h, xp, nt, preferred_element_type=F32)
                two.append(lax.dot_general(mf.astype(BF16), dyp, tn, preferred_element_type=F32))
                wmat = dm * mf
                dcb = dcb + dm * lmat
                dacs = jnp.where(lane == j, jnp.sum(wmat, -1, keepdims=True), dacs)
                colsums = jnp.where(sub == j, jnp.sum(wmat, 0, keepdims=True), colsums)
            pairs.append(jnp.where(first_head, two[0], two[1]))
        dxdt = bg * dtex + jnp.concatenate(pairs, axis=1)
        dacs = dacs - colsums.T + _to_heads(dyd * cs - ddte_w, e)
        cd_row = jnp.exp(acs[CHUNK - 1:CHUNK, :])
        tail = _to_heads(_row8(jnp.sum(ddte_w, 0, keepdims=True)), e)[0:1, :] + dcd * cd_row
        dacs = dacs + jnp.where(sub == CHUNK - 1, tail, 0.0)
        d_hi, d_mid, d_lo = _split3(dacs)
        up = lambda t: jnp.dot(triu, t, preferred_element_type=F32)
        da = (up(d_hi) + up(d_mid)) + up(d_lo)
        ddt_raw = (da * arow + _to_heads(dxdt * x, e)) * sg
        ddt_ref[rows, :] = ddt_raw
        small_ref[0:1, :] += jnp.sum(da * dt, 0, keepdims=True) * arow
        small_ref[1:2, :] += _to_heads(_row8(jnp.sum(dy * x, 0, keepdims=True)), e)[0:1, :]
        small_ref[2:3, :] += jnp.sum(ddt_raw, 0, keepdims=True)
        dcb16 = dcb.astype(BF16)
        dxbc_ref[rows, GROUP_CH + D_STATE:] = dc_off + jnp.dot(dcb16, bmat, preferred_element_type=F32)
        dxbc_ref[rows, GROUP_CH:GROUP_CH + D_STATE] = db_st + lax.dot_general(dcb16, cmat, tn,
                                                                               preferred_element_type=F32)
        dxbc_ref[rows, :GROUP_CH] = dxdt * dtx + dskx * dy
        g_ref[...] = g * jnp.exp(lastx) + g_here

    return _pcall(
        body, name="ssd_bwd", grid=(SSM_GROUPS, nb, steps),
        in_specs=[xbc_spec, raw, tbspec, wide, wide, wide, prev, grow, grow, nwspec],
        out_specs=[xbc_spec, lanes, wide,
                   pl.BlockSpec((None, 8, LANE), lambda g, b, c: (g, 0, 0)), nwspec],
        out_shape=[jax.ShapeDtypeStruct((nb, seq, CONV_DIM), F32),
                   jax.ShapeDtypeStruct((SSM_GROUPS, nb, seq, LANE), F32),
                   jax.ShapeDtypeStruct((nb, seq, D_INNER), F32),
                   jax.ShapeDtypeStruct((SSM_GROUPS, 8, LANE), F32),
                   jax.ShapeDtypeStruct((1, D_INNER), F32)],
        scratch_shapes=[pltpu.VMEM((D_STATE, hw), F32)],
        compiler_params=_params("parallel", "arbitrary", "arbitrary"),
    )(xbc, dt_raw, dt_bias_row, z, y, dys, sprev, alog_g, dskip_g, normw)


EW_TM = 256


def _merge_fwd(oa16, y_ssm16, w_bra, w_brb, gm, bgate):
    nb, seq, _ = oa16.shape

    def body(oa_ref, ys_ref, wa_ref, wb_ref, ga_ref, gb_ref, bg_ref, a_ref, b_ref, o_ref):
        y_a = jnp.dot(oa_ref[...], wa_ref[...], preferred_element_type=F32)
        y_b = jnp.dot(ys_ref[...], wb_ref[...], preferred_element_type=F32)
        a_ref[...] = y_a
        b_ref[...] = y_b
        sa = _sigmoid(ga_ref[...] + bg_ref[0:1, :])
        sb = _sigmoid(gb_ref[...] + bg_ref[1:2, :])
        o_ref[...] = (sa * y_a + sb * y_b).astype(BF16)

    spec = pl.BlockSpec((None, EW_TM, D_MODEL), lambda b, i: (b, i, 0))
    spec1 = pl.BlockSpec((None, EW_TM, D_MODEL), lambda b, i: (b, i, 1))
    return _pcall(
        body, name="merge_fwd", grid=(nb, seq // EW_TM),
        in_specs=[_tok_spec(EW_TM, ATT_OUT), _tok_spec(EW_TM, D_INNER), _whole(w_bra, True), _whole(w_brb, True),
                  spec, spec1, pl.BlockSpec((8, D_MODEL), lambda b, i: (0, 0))],
        out_specs=[spec] * 3,
        out_shape=[jax.ShapeDtypeStruct((nb, seq, D_MODEL), F32)] * 2 + [jax.ShapeDtypeStruct((nb, seq, D_MODEL), BF16)],
        compiler_params=_params("parallel", "parallel"),
    )(oa16, y_ssm16, w_bra, w_brb, gm, gm, bgate)


def _merge_bwd(dpre16, w_out16, y_a, y_b, gm, bgate):
    nb, seq, _ = y_a.shape

    def body(dp_ref, w_ref, a_ref, b_ref, ga_ref, gb_ref, bg_ref, dya_ref, dyb_ref, dg_ref, s_ref):
        @pl.when((pl.program_id(0) == 0) & (pl.program_id(1) == 0))
        def _():
            s_ref[...] = jnp.zeros_like(s_ref)

        dm = lax.dot_general(dp_ref[...], w_ref[...], NT, preferred_element_type=F32)
        sa = _sigmoid(ga_ref[...] + bg_ref[0:1, :])
        sb = _sigmoid(gb_ref[...] + bg_ref[1:2, :])
        dya_ref[...] = (dm * sa).astype(BF16)
        dyb_ref[...] = (dm * sb).astype(BF16)
        dga = dm * a_ref[...] * (sa * (1.0 - sa))
        dgb = dm * b_ref[...] * (sb * (1.0 - sb))
        dg_ref[:, :D_MODEL] = dga.astype(BF16)
        dg_ref[:, D_MODEL:] = dgb.astype(BF16)
        s_ref[0:1, :] += jnp.sum(dga, 0, keepdims=True)
        s_ref[1:2, :] += jnp.sum(dgb, 0, keepdims=True)

    spec = pl.BlockSpec((None, EW_TM, D_MODEL), lambda b, i: (b, i, 0))
    spec1 = pl.BlockSpec((None, EW_TM, D_MODEL), lambda b, i: (b, i, 1))
    small = pl.BlockSpec((8, D_MODEL), lambda b, i: (0, 0))
    return _pcall(
        body, name="merge_bwd", grid=(nb, seq // EW_TM),
        in_specs=[spec, _whole(w_out16, True), spec, spec, spec, spec1, small],
        out_specs=[spec, spec, pl.BlockSpec((None, EW_TM, 2 * D_MODEL), lambda b, i: (b, i, 0)), small],
        out_shape=[jax.ShapeDtypeStruct((nb, seq, D_MODEL), BF16), jax.ShapeDtypeStruct((nb, seq, D_MODEL), BF16),
                   jax.ShapeDtypeStruct((nb, seq, 2 * D_MODEL), BF16), jax.ShapeDtypeStruct((8, D_MODEL), F32)],
        compiler_params=_params("arbitrary", "arbitrary"),
    )(dpre16, w_out16, y_a, y_b, gm, gm, bgate)


def _ln_loss(x, merged16, w_out16, gp, p16, w_ple16, target, bgate, ln_g, ln_b):
    nb, seq, _ = x.shape

    def body(x_ref, m_ref, wo_ref, gp_ref, p_ref, wp_ref, t_ref, bg_ref, g_ref, b_ref,
             dx_ref, dp_ref, dpw_ref, dgp_ref, s_ref):
        @pl.when((pl.program_id(0) == 0) & (pl.program_id(1) == 0))
        def _():
            s_ref[...] = jnp.zeros_like(s_ref)

        sp = _sigmoid(gp_ref[...] + bg_ref[2:3, :])
        pw = jnp.dot(p_ref[...], wp_ref[...], preferred_element_type=F32)
        mix = jnp.dot(m_ref[...], wo_ref[...], preferred_element_type=F32)
        pre = ALPHA * x_ref[...] + mix + sp * pw
        mu = jnp.mean(pre, -1, keepdims=True)
        cen = pre - mu
        rstd = lax.rsqrt(jnp.mean(cen * cen, -1, keepdims=True) + LN_EPS)
        xhat = cen * rstd
        err = xhat * g_ref[...] + b_ref[...] - t_ref[...]
        dy = err * (1.0 / D_MODEL)
        dxh = dy * g_ref[...]
        dpre = rstd * (dxh - jnp.mean(dxh, -1, keepdims=True) - xhat * jnp.mean(dxh * xhat, -1, keepdims=True))
        dx_ref[...] = ALPHA * dpre
        dp_ref[...] = dpre.astype(BF16)
        dpw_ref[...] = (dpre * sp).astype(BF16)
        dgp = dpre * pw * (sp * (1.0 - sp))
        dgp_ref[...] = dgp.astype(BF16)
        s_ref[0:1, :] += jnp.sum(dy * xhat, 0, keepdims=True)
        s_ref[1:2, :] += jnp.sum(dy, 0, keepdims=True)
        s_ref[2:3, :] += jnp.sum(dgp, 0, keepdims=True)
        s_ref[3:4, :] += jnp.sum(err * err, 0, keepdims=True)

    spec = pl.BlockSpec((None, EW_TM, D_MODEL), lambda b, i: (b, i, 0))
    small = pl.BlockSpec((8, D_MODEL), lambda b, i: (0, 0))
    row = pl.BlockSpec((1, D_MODEL), lambda b, i: (0, 0))
    return _pcall(
        body, name="ln_loss", grid=(nb, seq // EW_TM),
        in_specs=[spec, spec, _whole(w_out16, True), spec, pl.BlockSpec((None, EW_TM, PLE_DIM), lambda b, i: (b, i, 0)),
                  _whole(w_ple16, True), spec, small, row, row],
        out_specs=[spec] * 4 + [small],
        out_shape=[jax.ShapeDtypeStruct((nb, seq, D_MODEL), F32)] + [jax.ShapeDtypeStruct((nb, seq, D_MODEL), BF16)] * 3
        + [jax.ShapeDtypeStruct((8, D_MODEL), F32)],
        compiler_params=_params("arbitrary", "arbitrary"),
    )(x, merged16, w_out16, gp, p16, w_ple16, target, bgate, ln_g, ln_b)


def _adamw_update(w_ref, g_ref, m_ref, v_ref, d_ref, nm_ref, nv_ref):
    c1 = 1.0 - ADAM_B1 ** ADAM_STEP
    c2 = 1.0 - ADAM_B2 ** ADAM_STEP
    gv = g_ref[...]
    nm = ADAM_B1 * m_ref[...] + (1.0 - ADAM_B1) * gv
    nv = ADAM_B2 * v_ref[...] + (1.0 - ADAM_B2) * (gv * gv)
    d_ref[...] = -ADAM_LR * ((nm / c1) / (jnp.sqrt(nv / c2) + ADAM_EPS) + ADAM_WD * w_ref[...])
    nm_ref[...] = nm
    nv_ref[...] = nv


def _adamw(w, g, m, v, name):
    rows, cols = w.shape
    tr = _row_tile(rows, cols, 8, 5 << 19)

    def body(*refs):
        _adamw_update(*refs)

    spec = pl.BlockSpec((tr, cols), lambda i: (i, 0))
    return _pcall(
        body, name=name, grid=(rows // tr,), in_specs=[spec] * 4, out_specs=[spec] * 3,
        out_shape=[jax.ShapeDtypeStruct(w.shape, F32)] * 3, compiler_params=_params("parallel"),
    )(w, g, m, v)


def _adamw_small(ws, gs, ms, vs, name):
    n = len(ws)

    def body(*refs):
        for i in range(n):
            _adamw_update(*[refs[k * n + i] for k in range(7)])

    outs = _pcall(body, name=name, out_shape=[jax.ShapeDtypeStruct(w.shape, F32) for w in ws] * 3,
                  compiler_params=_params())(*ws, *gs, *ms, *vs)
    return outs[:n], outs[n:2 * n], outs[2 * n:]


def _sum_rows(parts, out_dtype, name):
    rows, cols = parts[0].shape
    tr = rows
    for cand in range(16, rows, 16):
        if rows % cand == 0 and cand * cols * 4 <= (1 << 20):
            tr = cand
    n = len(parts)

    def body(*refs):
        acc = refs[0][...].astype(F32)
        for r in refs[1:n]:
            acc = acc + r[...].astype(F32)
        refs[n][...] = acc.astype(out_dtype)

    spec = pl.BlockSpec((tr, cols), lambda i: (i, 0))
    return _pcall(
        body, name=name, grid=(rows // tr,), in_specs=[spec] * n, out_specs=spec,
        out_shape=jax.ShapeDtypeStruct((rows, cols), out_dtype), compiler_params=_params("parallel"),
    )(*parts)


def _place():
    return lax.axis_index("x"), lax.axis_index("y"), lax.axis_index("c")


def _other_chips(x, y):
    return [(1 - x, y), (x, 1 - y), (1 - x, 1 - y)]


def _remote(src, dst, send_sem, recv_sem, to):
    return pltpu.make_async_remote_copy(src_ref=src, dst_ref=dst, send_sem=send_sem, recv_sem=recv_sem,
                                        device_id=to, device_id_type=MESH)


ANY = pl.BlockSpec(memory_space=pl.ANY)
D2D_CHUNK_BYTES = 512 * 1024
ICI_CHUNK_BYTES = 2 * 1024 * 1024


def _row_chunks(rows, row_bytes, chunk_bytes=D2D_CHUNK_BYTES):
    per = max(16, chunk_bytes // row_bytes // 16 * 16)
    return [(s, min(per, rows - s)) for s in range(0, rows, per)]


def _row_tile(rows, cols, align, limit=1 << 21):
    best = None
    for cand in range(align, rows + 1, align):
        if rows % cand == 0 and cand * cols * 4 <= limit:
            best = cand
    return best or rows


TOKEN_TM = 512


def _allgather_pieces(pieces, x, dilations):
    n, nd = len(pieces), len(dilations)
    nb, seq, kdim = x.shape
    halves = [_row_chunks(p.shape[0] // 2, p.shape[1] * p.dtype.itemsize, ICI_CHUNK_BYTES) for p in pieces]
    entries = [(a, q, s, m, j) for a in range(n) for q, (s, m) in enumerate(halves[a]) for j in range(3)]
    slot = {(a, q, j): k for k, (a, q, _, _, j) in enumerate(entries)}
    n_ici = len(entries)

    def body(*refs):
        ins, x_hbm, outs, order_hbm = refs[:n], refs[n], refs[n + 1:2 * n + 1], refs[2 * n + 1:2 * n + 2 + nd]
        send_sems, recv_sems = refs[2 * n + 2 + nd:2 * n + 4 + nd]
        tile_refs = refs[2 * n + 4 + nd:]
        x, y, c = _place()
        me = 2 * x + y
        sibling = (x, y, 1 - c)
        chips = _other_chips(x, y)

        def landed(a, s, m, j, core):
            half = ins[a].shape[0] // 2
            return outs[a].at[2 * chips[j][0] + chips[j][1], pl.ds(core * half + s, m)]

        sent = []
        for k, (a, q, s, m, j) in enumerate(entries):
            if j < 2:
                half = ins[a].shape[0] // 2
                cp = _remote(ins[a].at[pl.ds(c * half + s, m)], outs[a].at[me, pl.ds(c * half + s, m)],
                             send_sems.at[k], recv_sems.at[k], (*chips[j], c))
                cp.start()
                sent.append(cp)

        def reorder(x_ref, nat_ref, *own_refs):
            xv = x_ref[0]
            nat_ref[0] = xv.astype(BF16)
            for o_ref in own_refs:
                _store_own_order(xv, tile_refs, o_ref.at[0])

        pltpu.emit_pipeline(
            reorder, grid=(nb, seq // TOKEN_TM),
            in_specs=[pl.BlockSpec((1, TOKEN_TM, kdim), lambda b, i: (b, i, 0))],
            out_specs=[pl.BlockSpec((1, TOKEN_TM, kdim), lambda b, i: (b, i, 0))]
            + [pl.BlockSpec((1, d, TOKEN_TM // d, kdim), lambda b, i: (b, 0, i, 0)) for d in dilations],
        )(x_hbm, *order_hbm)

        def pass_to_sibling(k, blk):
            fw = _remote(blk, blk, send_sems.at[n_ici + k], recv_sems.at[n_ici + k], sibling)
            fw.start()
            sent.append(fw)

        for k, (a, q, s, m, j) in enumerate(entries):
            if j < 2:
                blk = landed(a, s, m, j, c)
                _remote(blk, blk, send_sems.at[k], recv_sems.at[k], (*chips[j], c)).wait_recv()
                first = q < (len(halves[a]) + 1) // 2
                if (j == 0) == first:
                    on = slot[(a, q, 2)]
                    rl = _remote(blk, blk, send_sems.at[on], recv_sems.at[on], (*chips[1 - j], c))
                    rl.start()
                    sent.append(rl)
                pass_to_sibling(k, blk)
        for k, (a, q, s, m, j) in enumerate(entries):
            if j == 2:
                blk = landed(a, s, m, j, c)
                _remote(blk, blk, send_sems.at[k], recv_sems.at[k], (*chips[j], c)).wait_recv()
                pass_to_sibling(k, blk)
        for k, (a, q, s, m, j) in enumerate(entries):
            blk = landed(a, s, m, j, 1 - c)
            _remote(blk, blk, send_sems.at[n_ici + k], recv_sems.at[n_ici + k], sibling).wait_recv()
        for cp in sent:
            cp.wait_send()

    gathered = _pcall(
        body, name="allgather_weights", in_specs=[ANY] * (n + 1), out_specs=[ANY] * (n + 1 + nd),
        out_shape=[jax.ShapeDtypeStruct((4,) + p.shape, p.dtype) for p in pieces]
        + [jax.ShapeDtypeStruct((nb, seq, kdim), BF16)]
        + [jax.ShapeDtypeStruct((nb, d, seq // d, kdim), BF16) for d in dilations],
        scratch_shapes=[pltpu.SemaphoreType.DMA((2 * n_ici,)), pltpu.SemaphoreType.DMA((2 * n_ici,))]
        + [pltpu.VMEM((TOKEN_TM, LANE), F32)] * (kdim // LANE),
        compiler_params=pltpu.CompilerParams(has_side_effects=True, vmem_limit_bytes=VMEM_LIMIT_BYTES),
    )(*pieces, x)
    gathered, orders = gathered[:n], gathered[n:]
    x16p = [orders[0]] + [o.reshape(nb, seq, kdim) for o in orders[1:]]
    cx, cy, _ = _place()
    return [lax.dynamic_update_slice(g, p[None], (2 * cx + cy, 0, 0)) for g, p in zip(gathered, pieces)], x16p


def _sibling_exchange(grads):
    n = len(grads)
    chunks = [_row_chunks(g.shape[1] // 2, g.shape[2] * g.dtype.itemsize) for g in grads]
    n_sem = 4 * sum(len(ch) for ch in chunks)

    def body(*refs):
        ins, gots = refs[:n], refs[n:2 * n]
        send_sems, recv_sems = refs[2 * n:]
        x, y, c = _place()
        sibling = (x, y, 1 - c)
        work = []
        for a in range(n):
            half = ins[a].shape[1] // 2
            for piece in range(4):
                for s, m in chunks[a]:
                    k = len(work)
                    cp = _remote(ins[a].at[piece, pl.ds((1 - c) * half + s, m)], gots[a].at[piece, pl.ds(s, m)],
                                 send_sems.at[k], recv_sems.at[k], sibling)
                    cp.start()
                    work.append(cp)
        for cp in work:
            cp.wait()

    return _pcall(
        body, name="grad_sibling_exchange", in_specs=[ANY] * n, out_specs=[ANY] * n,
        out_shape=[jax.ShapeDtypeStruct((4, g.shape[1] // 2, g.shape[2]), g.dtype) for g in grads],
        scratch_shapes=[pltpu.SemaphoreType.DMA((n_sem,)), pltpu.SemaphoreType.DMA((n_sem,))],
        compiler_params=pltpu.CompilerParams(has_side_effects=True),
    )(*grads)


def _sibling_gather(fulls):
    n = len(fulls)
    chunks = [_row_chunks(f.shape[0] // 2, f.shape[1] * f.dtype.itemsize) for f in fulls]
    n_sem = sum(len(ch) for ch in chunks)

    def body(*refs):
        outs = refs[n:2 * n]
        send_sems, recv_sems = refs[2 * n:]
        x, y, c = _place()
        sibling = (x, y, 1 - c)
        work = []
        for a in range(n):
            h = outs[a].shape[0] // 2
            for s, m in chunks[a]:
                k = len(work)
                mine = outs[a].at[pl.ds(c * h + s, m)]
                cp = _remote(mine, mine, send_sems.at[k], recv_sems.at[k], sibling)
                cp.start()
                work.append((a, s, m, cp))
        for k, (a, s, m, cp) in enumerate(work):
            h = outs[a].shape[0] // 2
            cp.wait_send()
            theirs = outs[a].at[pl.ds((1 - c) * h + s, m)]
            _remote(theirs, theirs, send_sems.at[k], recv_sems.at[k], sibling).wait_recv()

    return _pcall(
        body, name="grad_sibling_gather", in_specs=[ANY] * n, out_specs=[ANY] * n,
        out_shape=[jax.ShapeDtypeStruct(f.shape, f.dtype) for f in fulls],
        input_output_aliases={a: a for a in range(n)},
        scratch_shapes=[pltpu.SemaphoreType.DMA((n_sem,)), pltpu.SemaphoreType.DMA((n_sem,))],
        compiler_params=pltpu.CompilerParams(has_side_effects=True),
    )(*fulls)


def _pair_sum(grad, got, place, name):
    _, rows, cols = grad.shape
    half = rows // 2
    tr = _row_tile(half, cols, 16)

    def body(p_ref, a_ref, b_ref, o_ref):
        o_ref[...] = (a_ref[...].astype(F32) + b_ref[...].astype(F32)).astype(BF16)

    return _pcall(
        body, name=name,
        grid_spec=pltpu.PrefetchScalarGridSpec(
            num_scalar_prefetch=1, grid=(4, half // tr),
            in_specs=[pl.BlockSpec((None, tr, cols), lambda k, i, p: (k, p[1] * (half // tr) + i, 0)),
                      pl.BlockSpec((None, tr, cols), lambda k, i, p: (k, i, 0))],
            out_specs=pl.BlockSpec((None, tr, cols), lambda k, i, p: (k, i, 0))),
        out_shape=jax.ShapeDtypeStruct((4, half, cols), BF16),
        compiler_params=_params("parallel", "parallel"),
    )(place, grad, got)


def _chip_sum(sums, got, place, name):
    _, h, cols = sums.shape
    tr = _row_tile(h, cols, 16)

    def body(p_ref, own_ref, g0, g1, g2, o_ref):
        o_ref[...] = ((own_ref[...].astype(F32) + g0[...].astype(F32)) + g1[...].astype(F32)) + g2[...].astype(F32)

    gspec = lambda j: pl.BlockSpec((None, tr, cols), lambda i, p: (j, i, 0))
    return _pcall(
        body, name=name,
        grid_spec=pltpu.PrefetchScalarGridSpec(
            num_scalar_prefetch=1, grid=(h // tr,),
            in_specs=[pl.BlockSpec((None, tr, cols), lambda i, p: (p[0], i, 0)), gspec(0), gspec(1), gspec(2)],
            out_specs=pl.BlockSpec((tr, cols), lambda i, p: (p[1] * (h // tr) + i, 0))),
        out_shape=jax.ShapeDtypeStruct((2 * h, cols), F32),
        compiler_params=_params("parallel"),
    )(place, sums, got, got, got)


def _allgather8(buf, name):
    rows = buf.shape[0]

    def body(in_ref, out_ref, send_sems, recv_sems):
        x, y, c = _place()
        me = 4 * x + 2 * y + c
        out_ref[me] = in_ref[...]
        work = []
        for rel in range(1, 8):
            fx, fy, fc = (rel >> 2) & 1, (rel >> 1) & 1, rel & 1
            to = (x ^ fx, y ^ fy, c ^ fc)
            cp = _remote(in_ref, out_ref.at[me], send_sems.at[rel - 1], recv_sems.at[rel - 1], to)
            cp.start()
            work.append((cp, 4 * to[0] + 2 * to[1] + to[2]))
        for rel, (cp, frm) in enumerate(work):
            cp.wait_send()
            blk = out_ref.at[frm]
            _remote(blk, blk, send_sems.at[rel], recv_sems.at[rel], (x, y, c)).wait_recv()

    return _pcall(
        body, name=name, in_specs=[pl.BlockSpec(memory_space=pltpu.VMEM)],
        out_specs=pl.BlockSpec(memory_space=pltpu.VMEM),
        out_shape=jax.ShapeDtypeStruct((8, rows, LANE), F32),
        scratch_shapes=[pltpu.SemaphoreType.DMA((7,)), pltpu.SemaphoreType.DMA((7,))],
        compiler_params=pltpu.CompilerParams(has_side_effects=True),
    )(buf)


def _pack_rows(arrs):
    flats = [a.reshape(-1).astype(F32) for a in arrs]
    starts = np.cumsum([0] + [-(-f.shape[0] // LANE) * LANE for f in flats])
    total = -(-int(starts[-1]) // (8 * LANE)) * 8 * LANE
    flat = sum(jnp.pad(f, (int(s), total - int(s) - f.shape[0])) for f, s in zip(flats, starts))
    return flat.reshape(total // LANE, LANE)


def _unpack_rows(buf, shapes):
    flat = buf.reshape(-1)
    outs, off = [], 0
    for s in shapes:
        n = int(np.prod(s))
        outs.append(flat[off:off + n].reshape(s))
        off += -(-n // LANE) * LANE
    return outs


def _local_grads(x, p, target, wseg, w_br16, w_out16, w_ple16, b_gate, conv_w, conv_b, dt_bias, a_log, d_skip,
                 ssm_norm_w, ln_g, ln_b, rel_bias, finish_dx, x16p):
    nb, seq, _ = x.shape
    bmaps = jnp.asarray(_bucket_maps())
    bias = _bias_tables(rel_bias, bmaps)
    bgate8 = jnp.pad(b_gate, ((0, 5), (0, 0)))
    dils = [d for _, d in PATTERNS]

    x16 = x16p[0]
    p16 = p.astype(BF16)
    qkv = [_proj(x16p[g], [wseg["qkv%d" % g]], BF16, "proj_qkv%d" % g, True, 2 * MM_TM)[0].reshape(
        nb, dils[g], seq // dils[g], -1) for g in range(3)]
    nat = {}
    for gi, (group, tm) in enumerate(NAT_GROUPS):
        outs = _proj(x16, [wseg[s] for s in group], F32, "proj_nat%d" % gi, True, tm)
        nat.update(zip(group, outs))
    att = [_attn_fwd(qkv[g], bias, g, dils[g], "attn_fwd%d" % g) for g in range(3)]
    oa, o_att, lse = _combine_fwd(att[0][0], att[0][1], att[1:], nat["gatt"])

    conv_wg, conv_bg = _xbc_group_order(conv_w), _xbc_group_order(conv_b)
    act = _conv_fwd(nat["xbc"], conv_wg, conv_bg, "conv_fwd")
    dt_bias_row = jnp.pad(dt_bias, ((0, 0), (0, LANE - SSM_HEADS)))
    alog_g, dskip_g = _group_lanes(a_log), _group_lanes(d_skip)
    y_ssm, y_all, sprev = _ssd_fwd(act, nat["dt"], dt_bias_row, nat["z"], alog_g, dskip_g, ssm_norm_w)

    w_bra, w_brb = w_br16[:ATT_OUT], w_br16[ATT_OUT:]
    y_a, y_b, merged = _merge_fwd(oa, y_ssm, w_bra, w_brb, nat["gm"], bgate8)

    dx, dpre16, dpw16, dgp16, ln_sums = _ln_loss(x, merged, w_out16, nat["gp"], p16, w_ple16, target, bgate8,
                                                 ln_g, ln_b)
    loss_sum = (0.5 / D_MODEL) * jnp.sum(ln_sums[3])
    dya16, dyb16, dgm16, mg_sums = _merge_bwd(dpre16, w_out16, y_a, y_b, nat["gm"], bgate8)
    dys = _dx([dyb16], [w_brb], [], "dx_yssm")
    g_w_br, g_w_out, g_w_ple = _dw_stacked(
        [[(oa, dya16), (y_ssm, dyb16)], [(merged, dpre16)], [(p16, dpw16)]], BF16, "dw_branch_out_ple")

    do_att, dgatt16, own_order = _combine_bwd(dya16, w_bra, nat["gatt"], o_att, lse, dils[1:])
    dseg = {"gatt": dgatt16, "gm": dgm16, "gp": dgp16}
    dbias = []
    for g in range(3):
        cotangent = (do_att, o_att, lse) if g == 0 else (own_order[2 * g - 2], own_order[2 * g - 1])
        dqkv, db = _attn_bwd(qkv[g], bias, g, cotangent, dils[g],
                             "attn_bwd%d" % g)
        dseg["qkv%d" % g] = dqkv.reshape(nb, seq, -1)
        dbias.append(db)
    g_rel = _bias_grad(jnp.concatenate(dbias, axis=0), bmaps)[:, 0, :NUM_BUCKETS].T

    dact, ddtg, dz, ssd_small, g_normw = _ssd_bwd(
        act, nat["dt"], dt_bias_row, nat["z"], y_all, dys, sprev, alog_g, dskip_g, ssm_norm_w)
    dseg["z"] = dz
    dseg["dt"] = jnp.pad(_ungroup_lanes(ddtg), ((0, 0), (0, 0), (0, LANE - SSM_HEADS)))
    dpre, conv_sums = _conv_bwd_pre(dact, nat["xbc"], conv_wg, conv_bg, "conv_bwd")
    dseg["xbc"] = _conv_bwd_x(dpre, conv_wg, "conv_bwd_x")
    csum = _xbc_reference_order(conv_sums)

    dh_own = [(dseg["qkv%d" % g].reshape(nb, dils[g], seq // dils[g], -1), wseg["qkv%d" % g]) for g in (1, 2)]
    dwseg = {"qkv%d" % g: _dw(x16p[g], [dseg["qkv%d" % g]], BF16, "dw_qkv%d" % g, True)[0] for g in range(3)}
    for gi, group in enumerate(DW_GROUPS):
        dwseg.update(zip(group, _dw(x16, [dseg[s] for s in group], BF16, "dw_nat%d" % gi, True)))
    names = ["qkv0"] + [s for group, _ in NAT_GROUPS for s in group]
    dx = finish_dx([dseg[s] for s in names], [wseg[s] for s in names], [dx], dh_own, dwseg, g_w_br, g_w_out, g_w_ple)

    small = dict(
        b_gate=jnp.stack([mg_sums[0], mg_sums[1], ln_sums[2]]),
        conv_w=csum[0:4], conv_b=csum[4:5],
        dt_bias=_ungroup_lanes(ssd_small[:, 2:3, :]), a_log=_ungroup_lanes(ssd_small[:, 0:1, :]),
        d_skip=_ungroup_lanes(ssd_small[:, 1:2, :]), ssm_norm_w=g_normw,
        ln_g=ln_sums[0:1], ln_b=ln_sums[1:2], rel_bias=g_rel)
    return loss_sum, dx, small


DX_TM = 256
SMALL_ORDER = ("b_gate", "conv_w", "conv_b", "dt_bias", "a_log", "d_skip", "ssm_norm_w", "ln_g", "ln_b", "rel_bias")
SMALL_FULL_SHAPES = dict(b_gate=(3, 1024), conv_w=(4, 3072), conv_b=(1, 3072), dt_bias=(1, 32), a_log=(1, 32),
                         d_skip=(1, 32), ssm_norm_w=(1, 2048), ln_g=(1, 1024), ln_b=(1, 1024), rel_bias=(32, 36))


def kernel(x, p, w_in, b_gate, conv_w, conv_b, dt_bias, a_log, d_skip, ssm_norm_w, w_branch, w_out, w_ple, ln_g, ln_b, rel_bias, loss_target, m_w_in, m_b_gate, m_conv_w, m_conv_b, m_dt_bias, m_a_log, m_d_skip, m_ssm_norm_w, m_w_branch, m_w_out, m_w_ple, m_ln_g, m_ln_b, m_rel_bias, v_w_in, v_b_gate, v_conv_w, v_conv_b, v_dt_bias, v_a_log, v_d_skip, v_ssm_norm_w, v_w_branch, v_w_out, v_w_ple, v_ln_g, v_ln_b, v_rel_bias):
    cx, cy, cc = _place()
    chip = 2 * cx + cy
    dev = 4 * cx + 2 * cy + cc

    w_in_t = jnp.transpose(w_in[0])
    win16 = _shard_to_window(w_in_t, chip)
    (g_win, g_br, g_out, g_ple, g_small), x16p = _allgather_pieces(
        [win16, w_branch[0].astype(BF16), w_out[0].astype(BF16), w_ple[0].astype(BF16),
         _pack_rows([b_gate[0], conv_w[0]])], x, [d for _, d in PATTERNS][1:])
    wseg = _assemble(g_win)
    w_br16 = g_br.reshape(4 * 704, D_MODEL)
    w_out16 = g_out.reshape(D_MODEL, D_MODEL)
    w_ple16 = jnp.transpose(g_ple, (1, 0, 2)).reshape(PLE_DIM, D_MODEL)
    per_chip = [_unpack_rows(g_small[k], [(3, 256), (4, 768)]) for k in range(4)]
    b_gate_full = _join_last([pc[0] for pc in per_chip])
    conv_w_full = _join_last([pc[1] for pc in per_chip])

    place = jnp.stack([chip, cc]).astype(jnp.int32)
    reduced = []

    def finish_dx(dhs, ws, accs, own_order_dhs, dwseg, d_br, d_out, d_ple):
        grads = [_pack(dwseg), d_br.reshape(4, 704, D_MODEL), d_out.reshape(4, 256, D_MODEL),
                 jnp.transpose(d_ple.reshape(PLE_DIM, 4, 256), (1, 0, 2))]
        got = _sibling_exchange(grads)
        chip_sums = [_pair_sum(g, t, place, "grad_pair_sum_%d" % i) for i, (g, t) in enumerate(zip(grads, got))]
        dx, others = _dx(dhs, ws, accs, "dx_w_in_and_grad_chip_scatter", True, DX_TM, chip_sums, own_order_dhs)
        fulls = [_chip_sum(s, t, place, "grad_chip_sum_%d" % i) for i, (s, t) in enumerate(zip(chip_sums, others))]
        reduced.extend(_sibling_gather(fulls))
        return dx

    loss_sum, grad_x, small = _local_grads(
        x, p[0], loss_target, wseg, w_br16, w_out16, w_ple16, b_gate_full, conv_w_full, conv_b, dt_bias, a_log,
        d_skip, ssm_norm_w, ln_g, ln_b, rel_bias, finish_dx, x16p)
    big = reduced
    g_w_in = lax.optimization_barrier(_window_to_shard(big[0], chip))
    g_w_branch, g_w_out, g_w_ple = big[1], big[2], big[3]
    parts = _allgather8(_pack_rows([small[n] for n in SMALL_ORDER] + [loss_sum.reshape(1, 1)]),
                        "allgather_small_grads")
    small_sum = _sum_rows([parts[i] for i in range(8)], F32, "small_grad_sum")
    *reduced_small, loss = _unpack_rows(small_sum, [SMALL_FULL_SHAPES[n] for n in SMALL_ORDER] + [(1, 1)])
    loss = loss.reshape(())
    sg = dict(zip(SMALL_ORDER, reduced_small))
    sg["b_gate"] = lax.dynamic_slice_in_dim(sg["b_gate"], chip * 256, 256, axis=1)
    sg["conv_w"] = lax.dynamic_slice_in_dim(sg["conv_w"], chip * 768, 768, axis=1)
    del dev

    upd = {}
    upd["w_in"] = [jnp.transpose(t) for t in _adamw(w_in_t, g_w_in, jnp.transpose(m_w_in[0]),
                                                      jnp.transpose(v_w_in[0]), "adamw_w_in")]
    upd["w_branch"] = _adamw(w_branch[0], g_w_branch, m_w_branch[0], v_w_branch[0], "adamw_w_branch")
    upd["w_out"] = _adamw(w_out[0], g_w_out, m_w_out[0], v_w_out[0], "adamw_w_out")
    upd["w_ple"] = _adamw(w_ple[0], g_w_ple, m_w_ple[0], v_w_ple[0], "adamw_w_ple")
    small_w = dict(b_gate=b_gate, conv_w=conv_w, conv_b=conv_b, dt_bias=dt_bias, a_log=a_log, d_skip=d_skip,
                   ssm_norm_w=ssm_norm_w, ln_g=ln_g, ln_b=ln_b, rel_bias=rel_bias)
    small_m = dict(b_gate=m_b_gate, conv_w=m_conv_w, conv_b=m_conv_b, dt_bias=m_dt_bias, a_log=m_a_log,
                   d_skip=m_d_skip, ssm_norm_w=m_ssm_norm_w, ln_g=m_ln_g, ln_b=m_ln_b, rel_bias=m_rel_bias)
    small_v = dict(b_gate=v_b_gate, conv_w=v_conv_w, conv_b=v_conv_b, dt_bias=v_dt_bias, a_log=v_a_log,
                   d_skip=v_d_skip, ssm_norm_w=v_ssm_norm_w, ln_g=v_ln_g, ln_b=v_ln_b, rel_bias=v_rel_bias)
    for n in SMALL_ORDER:
        sg[n] = sg[n].reshape(small_w[n].shape)
    s_delta, s_m, s_v = _adamw_small(*[[t[n] for n in SMALL_ORDER] for t in (small_w, sg, small_m, small_v)],
                                     "adamw_small")
    for i, n in enumerate(SMALL_ORDER):
        upd[n] = (s_delta[i], s_m[i], s_v[i])

    order = ("w_in", "b_gate", "conv_w", "conv_b", "dt_bias", "a_log", "d_skip", "ssm_norm_w", "w_branch", "w_out",
             "w_ple", "ln_g", "ln_b", "rel_bias")
    grads = dict(sg, w_in=jnp.transpose(g_w_in)[None],w_branch=g_w_branch[None], w_out=g_w_out[None], w_ple=g_w_ple[None])
    lead = lambda n, t: t[None] if n in ("w_in", "w_branch", "w_out", "w_ple") else t
    return (loss, grad_x, *[grads[n] for n in order], *[lead(n, upd[n][0]) for n in order],
            *[lead(n, upd[n][1]) for n in order], *[lead(n, upd[n][2]) for n in order])
```

```python
import math

import numpy as np
import jax
import jax.numpy as jnp
from jax import lax
from jax.experimental import pallas as pl
from jax.experimental.pallas import tpu as pltpu

F32, BF16 = jnp.float32, jnp.bfloat16

D_MODEL = 1024
HEAD_DIM = 64
GROUP_HEADS = 12
ATT_OUT = GROUP_HEADS * HEAD_DIM
PATTERNS = ((128, 1), (512, 4), (2048, 16))
BAND = 128
NUM_BUCKETS = 32
MAX_DISTANCE = 2048
D_INNER = 2048
SSM_HEADS = 32
SSM_GROUPS = 4
GROUP_SSM_HEADS = SSM_HEADS // SSM_GROUPS
D_STATE = 128
CHUNK = 128
PLE_DIM = 256
ALPHA = 2.0 ** 0.25
LN_EPS = 1e-5
RMS_EPS = 1e-5
ADAM_LR, ADAM_B1, ADAM_B2, ADAM_EPS, ADAM_WD, ADAM_STEP = 0.001, 0.9, 0.999, 1e-08, 0.01, 10
NEG = -1e30

QKV_W = 3 * ATT_OUT
IN_COLS = 15904
SHARD_COLS = IN_COLS // 4
DT_COL = 12800
ROW_TILE = 16
WIN_ROWS = 4000


def _win_offset(k):
    return (k * SHARD_COLS) % ROW_TILE


def _win_start(k):
    return k * SHARD_COLS - _win_offset(k)

VMEM_LIMIT_BYTES = 56 * 1024 * 1024
LANE = 128
MESH = pl.DeviceIdType.MESH
NT = (((1,), (1,)), ((), ()))
TN = (((0,), (0,)), ((), ()))


def _pcall(body, **kw):
    return pl.pallas_call(body, **kw)


def _params(*sem):
    return pltpu.CompilerParams(dimension_semantics=sem, vmem_limit_bytes=VMEM_LIMIT_BYTES)


def _sigmoid(v):
    return jax.nn.sigmoid(v)


MM_TM = 512


def _tok_spec(tm, width):
    return pl.BlockSpec((None, tm, width), lambda b, i: (b, i, 0))


def _whole(arr, single_buffer=False):
    mode = dict(pipeline_mode=pl.Buffered(1)) if single_buffer else {}
    return pl.BlockSpec(arr.shape, lambda b, i: (0,) * arr.ndim, **mode)


def _proj(a3, ws, out_dtype, name, w_rows_are_outputs=False, tm=MM_TM):
    nb, seq, kdim = a3.shape
    nw = len(ws)
    widths = [w.shape[0] if w_rows_are_outputs else w.shape[1] for w in ws]

    def body(*refs):
        a = refs[0][...].astype(BF16)
        for w_ref, o_ref in zip(refs[1:1 + nw], refs[1 + nw:]):
            if w_rows_are_outputs:
                v = lax.dot_general(a, w_ref[...], NT, preferred_element_type=F32)
            else:
                v = jnp.dot(a, w_ref[...], preferred_element_type=F32)
            o_ref[...] = v.astype(out_dtype)

    return _pcall(
        body, name=name, grid=(nb, seq // tm),
        in_specs=[_tok_spec(tm, kdim)] + [_whole(w, True) for w in ws],
        out_specs=[_tok_spec(tm, n) for n in widths],
        out_shape=[jax.ShapeDtypeStruct((nb, seq, n), out_dtype) for n in widths],
        compiler_params=_params("parallel", "parallel"),
    )(a3, *ws)


def _dx(dhs, ws, accs, name, w_rows_are_outputs=False, tm=MM_TM, scatter=None, own_order_dhs=()):
    nb, seq, _ = dhs[0].shape
    nd, nacc, npd = len(dhs), len(accs), len(own_order_dhs)
    kout = ws[0].shape[1] if w_rows_are_outputs else ws[0].shape[0]
    sums = scatter or []
    ns = len(sums)
    chunks = [_row_chunks(s.shape[1], s.shape[2] * s.dtype.itemsize, ICI_CHUNK_BYTES) for s in sums]
    n_sem = 3 * sum(len(ch) for ch in chunks)
    grid = (nb, seq // tm)
    ntile = kout // LANE if npd else 0

    def body(*refs):
        n_own = 2 * nd + nacc
        n_in = n_own + 2 * npd
        sum_refs, o_ref, got_refs = refs[n_in:n_in + ns], refs[n_in + ns], refs[n_in + ns + 1:n_in + 2 * ns + 1]
        tile_refs = refs[n_in + 2 * ns + 1:n_in + 2 * ns + 1 + ntile]

        def copies():
            send_sems, recv_sems = refs[-2], refs[-1]
            x, y, c = _place()
            out = []
            for a in range(ns):
                for s, m in chunks[a]:
                    for j, (cx, cy) in enumerate(_other_chips(x, y)):
                        k = len(out)
                        out.append(_remote(sum_refs[a].at[2 * cx + cy, pl.ds(s, m)], got_refs[a].at[j, pl.ds(s, m)],
                                           send_sems.at[k], recv_sems.at[k], (cx, cy, c)))
            return out

        if ns:
            @pl.when((pl.program_id(0) == 0) & (pl.program_id(1) == 0))
            def _():
                for cp in copies():
                    cp.start()

        v = None
        for dh_ref, w_ref in zip(refs[:nd], refs[nd:2 * nd]):
            dh = dh_ref[...].astype(BF16)
            if w_rows_are_outputs:
                t = jnp.dot(dh, w_ref[...], preferred_element_type=F32)
            else:
                t = lax.dot_general(dh, w_ref[...], NT, preferred_element_type=F32)
            v = t if v is None else v + t
        for a_ref in refs[2 * nd:2 * nd + nacc]:
            v = v + a_ref[...]
        for q_ref, w_ref in zip(refs[n_own:n_own + npd], refs[n_own + npd:n_in]):
            d, per, n = q_ref.shape
            dh = q_ref[...].reshape(d * per, n).astype(BF16)
            if w_rows_are_outputs:
                t = jnp.dot(dh, w_ref[...], preferred_element_type=F32)
            else:
                t = lax.dot_general(dh, w_ref[...], NT, preferred_element_type=F32)
            v = v + _natural_value(t, d, tile_refs)
        o_ref[...] = v

        if ns:
            @pl.when((pl.program_id(0) == grid[0] - 1) & (pl.program_id(1) == grid[1] - 1))
            def _():
                for cp in copies():
                    cp.wait()

    out = _pcall(
        body, name=name, grid=grid,
        in_specs=[_tok_spec(tm, dh.shape[-1]) for dh in dhs] + [_whole(w, True) for w in ws]
        + [_tok_spec(tm, kout)] * nacc
        + [pl.BlockSpec((None, q.shape[1], tm // q.shape[1], q.shape[3]), lambda b, i: (b, 0, i, 0))
           for q, _ in own_order_dhs]
        + [_whole(w, True) for _, w in own_order_dhs]
        + [ANY] * ns,
        out_specs=[_tok_spec(tm, kout)] + [ANY] * ns,
        out_shape=[jax.ShapeDtypeStruct((nb, seq, kout), F32)]
        + [jax.ShapeDtypeStruct((3,) + s.shape[1:], s.dtype) for s in sums],
        input_output_aliases={2 * nd: 0} if nacc else {},
        scratch_shapes=[pltpu.VMEM((tm, LANE), F32)] * ntile
        + ([pltpu.SemaphoreType.DMA((n_sem,)), pltpu.SemaphoreType.DMA((n_sem,))] if ns else []),
        compiler_params=pltpu.CompilerParams(
            dimension_semantics=("arbitrary", "arbitrary") if ns else ("parallel", "parallel"),
            vmem_limit_bytes=VMEM_LIMIT_BYTES, has_side_effects=bool(ns)),
    )(*dhs, *ws, *accs, *[q for q, _ in own_order_dhs], *[w for _, w in own_order_dhs], *sums)
    return (out[0], list(out[1:])) if ns else out[0]


def _dw(a3, dhs, out_dtype, name, rows_are_outputs=False):
    nb, seq, kdim = a3.shape
    nd = len(dhs)
    grid = (nb, seq // MM_TM)
    shapes = [(dh.shape[-1], kdim) if rows_are_outputs else (kdim, dh.shape[-1]) for dh in dhs]

    def body(*refs):
        b, i = pl.program_id(0), pl.program_id(1)
        dh_refs, o_refs, acc_refs = refs[1:1 + nd], refs[1 + nd:1 + 2 * nd], refs[1 + 2 * nd:]

        @pl.when((b == 0) & (i == 0))
        def _():
            for acc_ref in acc_refs:
                acc_ref[...] = jnp.zeros_like(acc_ref)

        a = refs[0][...].astype(BF16)
        for dh_ref, acc_ref in zip(dh_refs, acc_refs):
            dh = dh_ref[...].astype(BF16)
            acc_ref[...] += lax.dot_general(*((dh, a) if rows_are_outputs else (a, dh)), TN,
                                            preferred_element_type=F32)

        @pl.when((b == grid[0] - 1) & (i == grid[1] - 1))
        def _():
            for o_ref, acc_ref in zip(o_refs, acc_refs):
                o_ref[...] = acc_ref[...].astype(out_dtype)

    return _pcall(
        body, name=name, grid=grid,
        in_specs=[_tok_spec(MM_TM, kdim)] + [_tok_spec(MM_TM, dh.shape[-1]) for dh in dhs],
        out_specs=[pl.BlockSpec(s, lambda b, i: (0, 0)) for s in shapes],
        out_shape=[jax.ShapeDtypeStruct(s, out_dtype) for s in shapes],
        scratch_shapes=[pltpu.VMEM(s, F32) for s in shapes],
        compiler_params=_params("arbitrary", "arbitrary"),
    )(a3, *dhs)


def _dw_stacked(groups, out_dtype, name):
    pairs = [pr for g in groups for pr in g]
    nb, seq, _ = pairs[0][0].shape
    npair, ng = len(pairs), len(groups)
    grid = (nb, seq // MM_TM)

    def body(*refs):
        b, i = pl.program_id(0), pl.program_id(1)
        o_refs, acc_refs = refs[2 * npair:2 * npair + ng], refs[2 * npair + ng:]

        @pl.when((b == 0) & (i == 0))
        def _():
            for acc_ref in acc_refs:
                acc_ref[...] = jnp.zeros_like(acc_ref)

        for j, acc_ref in enumerate(acc_refs):
            acc_ref[...] += lax.dot_general(refs[2 * j][...].astype(BF16), refs[2 * j + 1][...].astype(BF16), TN,
                                            preferred_element_type=F32)

        @pl.when((b == grid[0] - 1) & (i == grid[1] - 1))
        def _():
            accs = iter(acc_refs)
            for o_ref, g in zip(o_refs, groups):
                row = 0
                for a3, _ in g:
                    o_ref[row:row + a3.shape[-1], :] = next(accs)[...].astype(out_dtype)
                    row += a3.shape[-1]

    shapes = [(sum(a3.shape[-1] for a3, _ in g), g[0][1].shape[-1]) for g in groups]
    return _pcall(
        body, name=name, grid=grid,
        in_specs=[_tok_spec(MM_TM, t.shape[-1]) for pr in pairs for t in pr],
        out_specs=[pl.BlockSpec(s, lambda b, i: (0, 0)) for s in shapes],
        out_shape=[jax.ShapeDtypeStruct(s, out_dtype) for s in shapes],
        scratch_shapes=[pltpu.VMEM((a3.shape[-1], dh.shape[-1]), F32) for a3, dh in pairs],
        compiler_params=_params("arbitrary", "arbitrary"),
    )(*[t for pr in pairs for t in pr])


def _qkv_rows(g):
    return [(part * QKV_W + g * ATT_OUT + hp * LANE, LANE) for hp in range(ATT_OUT // LANE) for part in range(3)]


XBC_START = 3 * QKV_W + ATT_OUT + D_INNER
GROUP_CH = GROUP_SSM_HEADS * HEAD_DIM
XBC_GROUP = GROUP_CH + 2 * D_STATE
CONV_DIM = SSM_GROUPS * XBC_GROUP


def _xbc_ranges():
    out = []
    for g in range(SSM_GROUPS):
        out += [(g * GROUP_CH, GROUP_CH), (D_INNER + g * D_STATE, D_STATE),
                (D_INNER + SSM_GROUPS * D_STATE + g * D_STATE, D_STATE)]
    return out


def _join_last(parts):
    widths = [t.shape[-1] for t in parts]
    total, lead = sum(widths), [(0, 0)] * (parts[0].ndim - 1)
    starts = np.cumsum([0] + widths)
    return sum(jnp.pad(t, lead + [(int(s), total - int(s) - w)]) for t, s, w in zip(parts, starts, widths))


def _xbc_group_order(t):
    return _join_last([t[..., s:s + n] for s, n in _xbc_ranges()])


def _xbc_reference_order(t):
    g = lambda off, n: [t[..., k * XBC_GROUP + off:k * XBC_GROUP + off + n] for k in range(SSM_GROUPS)]
    return _join_last(g(0, GROUP_CH) + g(GROUP_CH, D_STATE) + g(GROUP_CH + D_STATE, D_STATE))


def _segments():
    one = lambda name, start, rows: (name, [(start, rows)], max(rows, LANE))
    return [("qkv%d" % g, _qkv_rows(g), QKV_W) for g in range(3)] + [
        one("gatt", 3 * QKV_W, ATT_OUT), one("z", 3 * QKV_W + ATT_OUT, D_INNER),
        ("xbc", [(XBC_START + s, n) for s, n in _xbc_ranges()], CONV_DIM), one("dt", DT_COL, SSM_HEADS),
        one("gm", DT_COL + SSM_HEADS, 2 * D_MODEL), one("gp", DT_COL + SSM_HEADS + 2 * D_MODEL, D_MODEL)]


LAYOUT_TC = 256
NAT_GROUPS = ((("gatt", "z", "dt", "gp"), 512), (("xbc", "gm"), 512))
DW_GROUPS = (("gatt", "z", "dt", "gp"), ("xbc",), ("gm",))


def _assemble(win):
    segs = _segments()

    def body(win_ref, *outs):
        def pieces(start, rows):
            t, end = start, start + rows
            while t < end:
                k = min(t // SHARD_COLS, 3)
                shard_end = (k + 1) * SHARD_COLS
                if k < 3 and shard_end % ROW_TILE and t == shard_end - shard_end % ROW_TILE:
                    lo = t - _win_start(k)
                    yield win_ref[k, lo:lo + ROW_TILE, :] + win_ref[k + 1, 0:ROW_TILE, :]
                    t += ROW_TILE
                    continue
                upto = min(end, shard_end - shard_end % ROW_TILE if k < 3 else end)
                yield win_ref[k, t - _win_start(k):upto - _win_start(k), :]
                t = upto

        for (_, ranges, total), o_ref in zip(segs, outs):
            off = 0
            for start, rows in ranges:
                for part in pieces(start, rows):
                    o_ref[off:off + part.shape[0], :] = part
                    off += part.shape[0]
            if off < total:
                o_ref[off:total, :] = jnp.zeros((total - off, o_ref.shape[1]), BF16)

    outs = _pcall(
        body, name="assemble_w_in", grid=(D_MODEL // LAYOUT_TC,),
        in_specs=[pl.BlockSpec((4, WIN_ROWS, LAYOUT_TC), lambda i: (0, 0, i))],
        out_specs=[pl.BlockSpec((total, LAYOUT_TC), lambda i: (0, i)) for _, _, total in segs],
        out_shape=[jax.ShapeDtypeStruct((total, D_MODEL), BF16) for _, _, total in segs],
        compiler_params=_params("parallel"),
    )(win)
    return {name: o for (name, _, _), o in zip(segs, outs)}


def _pack(dsegs):
    segs = _segments()

    def body(*refs):
        ins, o_ref = refs[:-1], refs[-1]
        tail = IN_COLS - _win_start(3)
        o_ref[3, tail:, :] = jnp.zeros((WIN_ROWS - tail, o_ref.shape[2]), BF16)
        for (_, ranges, _), s_ref in zip(segs, ins):
            off = 0
            for start, rows in ranges:
                for k in range(4):
                    lo = _win_start(k)
                    a, b = max(start, lo), min(start + rows, lo + WIN_ROWS)
                    if a < b:
                        o_ref[k, a - lo:b - lo, :] = s_ref[off + a - start:off + b - start, :]
                off += rows

    return _pcall(
        body, name="pack_dw_in", grid=(D_MODEL // LAYOUT_TC,),
        in_specs=[pl.BlockSpec((total, LAYOUT_TC), lambda i: (0, i)) for _, _, total in segs],
        out_specs=pl.BlockSpec((4, WIN_ROWS, LAYOUT_TC), lambda i: (0, 0, i)),
        out_shape=jax.ShapeDtypeStruct((4, WIN_ROWS, D_MODEL), BF16),
        compiler_params=_params("parallel"),
    )(*[dsegs[name] for name, _, _ in segs])


def _shard_to_window(shard_t, k):
    def at(off):
        return lambda w: jnp.pad(w.astype(BF16), ((off, WIN_ROWS - SHARD_COLS - off), (0, 0)))

    return lax.cond(k % 2 == 1, at(_win_offset(1)), at(_win_offset(0)), shard_t)


def _window_to_shard(win, k):
    return lax.dynamic_slice(win, ((k % 2) * _win_offset(1), 0), (SHARD_COLS, D_MODEL))


def _bucket_maps():
    qi = np.arange(8)[:, None]
    kj = np.arange(2 * BAND)[None, :]
    delta = qi + BAND - kj
    maps = []
    for window, dil in PATTERNS:
        valid = (delta >= 0) & (delta <= window // dil)
        dist = np.maximum(delta, 0) * dil
        max_exact = NUM_BUCKETS // 2
        d_f = np.maximum(dist, 1).astype(np.float32)
        large = max_exact + (np.log(d_f / np.float32(max_exact)) / np.float32(math.log(MAX_DISTANCE / max_exact))
                             * np.float32(NUM_BUCKETS - max_exact)).astype(np.int32)
        large = np.minimum(large, NUM_BUCKETS - 1)
        bucket = np.where(dist < max_exact, dist, large)
        maps.append(np.where(valid, bucket, -1).astype(np.int32))
    return np.stack(maps)


def _bias_tables(rel_bias, bmaps):
    def body(rb_ref, bm_ref, o_ref):
        g = pl.program_id(0)
        bm = bm_ref[...]
        for hh in range(GROUP_HEADS):
            acc = jnp.full(bm.shape, NEG, F32)
            for b in range(NUM_BUCKETS):
                acc = jnp.where(bm == b, rb_ref[b, g * GROUP_HEADS + hh], acc)
            for a in range(BAND // 8):
                o_ref[hh, 8 * a:8 * a + 8, :] = acc if a == 0 else pltpu.roll(acc, 8 * a, 1)

    return _pcall(
        body, name="bias_tables", grid=(3,),
        in_specs=[pl.BlockSpec(memory_space=pltpu.SMEM),
                  pl.BlockSpec((None, 8, 2 * BAND), lambda g: (g, 0, 0))],
        out_specs=pl.BlockSpec((GROUP_HEADS, BAND, 2 * BAND), lambda g: (g, 0, 0)),
        out_shape=jax.ShapeDtypeStruct((3 * GROUP_HEADS, BAND, 2 * BAND), F32),
        compiler_params=_params("parallel"),
    )(rel_bias, bmaps)


def _bias_grad(dbias, bmaps):
    def body(db_ref, bm_ref, o_ref):
        bm = bm_ref[...]
        lane = lax.broadcasted_iota(jnp.int32, (1, LANE), 1)
        for hh in range(GROUP_HEADS):
            db = db_ref[hh, 0:8, :]
            for a in range(1, BAND // 8):
                db = db + pltpu.roll(db_ref[hh, 8 * a:8 * a + 8, :], 2 * BAND - 8 * a, 1)
            vec = jnp.zeros((1, LANE), F32)
            for b in range(NUM_BUCKETS):
                s = jnp.sum(jnp.where(bm == b, db, 0.0), keepdims=True)
                vec = jnp.where(lane == b, s, vec)
            o_ref[hh] = vec

    return _pcall(
        body, name="bias_grad", grid=(3,),
        in_specs=[pl.BlockSpec((GROUP_HEADS, BAND, 2 * BAND), lambda g: (g, 0, 0)),
                  pl.BlockSpec((None, 8, 2 * BAND), lambda g: (g, 0, 0))],
        out_specs=pl.BlockSpec((GROUP_HEADS, 1, LANE), lambda g: (g, 0, 0)),
        out_shape=jax.ShapeDtypeStruct((3 * GROUP_HEADS, 1, LANE), F32),
        compiler_params=_params("parallel"),
    )(dbias, bmaps)


def _rows(n):
    if isinstance(n, int):
        return pl.ds(n * BAND, BAND)
    return pl.ds(pl.multiple_of(n * BAND, BAND), BAND)


def _for_blocks(blocks, nblk, per, carry):
    carry = blocks([0], carry, False)
    start = 1 + (nblk - 1) % per
    for n in range(1, start):
        carry = blocks([n], carry, True)
    trips = (nblk - start) // per
    if trips > 0:
        carry = lax.fori_loop(
            0, trips, lambda t, c: blocks([start + t * per + u for u in range(per)], c, True), carry)
    return carry


def _pairs_per_step(d):
    return {1: 3, 4: 6, 16: 6}[d]


def _bias_spec(group, hps):
    first = group * GROUP_HEADS // (2 * hps)
    return pl.BlockSpec((2 * hps, BAND, 2 * BAND), lambda hp, b, r: (first + hp, 0, 0))


def _attn_fwd(qkv4, bias, group, d, name):
    nb, _, sub, _ = qkv4.shape
    nblk = sub // BAND
    scale = HEAD_DIM ** -0.5
    npair = ATT_OUT // LANE
    hps = _pairs_per_step(d)
    compact = d > 1

    def body(qkv_ref, bias_ref, o_ref, l_ref):
        def blocks(ns, carry, with_prev):
            chains = [(bi, i, h) for bi in range(len(ns)) for i in range(hps) for h in range(2)]
            first_head = lax.broadcasted_iota(jnp.int32, (BAND, LANE), 1) < HEAD_DIM
            pair = lambda n, i, part: qkv_ref[_rows(n), (3 * i + part) * LANE:(3 * i + part + 1) * LANE]
            scores = []
            for bi, i, h in chains:
                n = ns[bi]
                qp = pair(n, i, 0) * scale
                q = jnp.where(first_head if h == 0 else jnp.logical_not(first_head), qp, jnp.zeros_like(qp))
                s_c = lax.dot_general(q, pair(n, i, 1), NT, preferred_element_type=F32) + bias_ref[2 * i + h, :, BAND:]
                s_p = None
                if with_prev:
                    s_p = lax.dot_general(q, pair(n - 1, i, 1), NT,
                                          preferred_element_type=F32) + bias_ref[2 * i + h, :, :BAND]
                scores.append((s_c, s_p))
            probs = []
            for s_c, s_p in scores:
                m = jnp.max(s_c, -1, keepdims=True)
                if with_prev:
                    m = jnp.maximum(m, jnp.max(s_p, -1, keepdims=True))
                e_c = jnp.exp(s_c - m)
                den = jnp.sum(e_c, -1, keepdims=True)
                e_p = None
                if with_prev:
                    e_p = jnp.exp(s_p - m)
                    den = den + jnp.sum(e_p, -1, keepdims=True)
                    e_p = e_p.astype(BF16)
                probs.append((e_c.astype(BF16), e_p, den, m))
            outs = {}
            for (bi, i, h), (e_c, e_p, den, m) in zip(chains, probs):
                n = ns[bi]
                acc = jnp.dot(e_c, pair(n, i, 2), preferred_element_type=F32)
                if with_prev:
                    acc = acc + jnp.dot(e_p, pair(n - 1, i, 2), preferred_element_type=F32)
                outs[(bi, i, h)] = (acc / den, m + jnp.log(den))
            lane = lax.broadcasted_iota(jnp.int32, (BAND, LANE), 1)
            for bi, n in enumerate(ns):
                per_head = jnp.zeros((BAND, LANE), F32)
                for i in range(hps):
                    o_ref[_rows(n), i * LANE:(i + 1) * LANE] = jnp.where(first_head, outs[(bi, i, 0)][0],
                                                                         outs[(bi, i, 1)][0])
                    if compact:
                        for h in range(2):
                            per_head = jnp.where(lane == 2 * i + h, outs[(bi, i, h)][1], per_head)
                    else:
                        l_ref[_rows(n), i * LANE:(i + 1) * LANE] = jnp.where(first_head, outs[(bi, i, 0)][1],
                                                                             outs[(bi, i, 1)][1])
                if compact:
                    l_ref[_rows(n), :] = per_head
            return carry

        _for_blocks(blocks, nblk, 2 if hps == 1 else 1, 0)

    in_specs = [pl.BlockSpec((None, None, sub, 3 * LANE * hps), lambda hp, b, r: (b, r, 0, hp)),
                _bias_spec(group, hps)]
    if compact:
        return _pcall(
            body, name=name, grid=(1, nb, d), in_specs=in_specs,
            out_specs=[pl.BlockSpec((None, None, sub, ATT_OUT), lambda hp, b, r: (b, r, 0, 0)),
                       pl.BlockSpec((None, None, sub, LANE), lambda hp, b, r: (b, r, 0, 0))],
            out_shape=[jax.ShapeDtypeStruct((nb, d, sub, ATT_OUT), F32), jax.ShapeDtypeStruct((nb, d, sub, LANE), F32)],
            compiler_params=_params("parallel", "parallel", "parallel"),
        )(qkv4, bias)
    ospec = pl.BlockSpec((None, sub, hps * LANE), lambda hp, b, r: (b, 0, r * (npair // hps) + hp))
    return _pcall(
        body, name=name, grid=(npair // hps, nb, d), in_specs=in_specs, out_specs=[ospec, ospec],
        out_shape=[jax.ShapeDtypeStruct((nb, sub, d * ATT_OUT), F32)] * 2,
        compiler_params=_params("parallel", "parallel", "parallel"),
    )(qkv4, bias)


STAT_LSE_LANE = 16


def _attn_bwd(qkv4, bias, group, cotangent, d, name):
    nb, _, sub, _ = qkv4.shape
    nblk = sub // BAND
    scale = HEAD_DIM ** -0.5
    npair = ATT_OUT // LANE
    hps = _pairs_per_step(d)
    compact = d > 1

    def body(qkv_ref, bias_ref, *rest):
        do_ref, dqkv_ref, db_ref = rest[0], rest[-2], rest[-1]
        b, r = pl.program_id(1), pl.program_id(2)

        @pl.when((b == 0) & (r == 0))
        def _():
            db_ref[...] = jnp.zeros_like(db_ref)

        def blocks(ns, carry, with_prev):
            sides = (0, 1) if with_prev else (0,)
            chains = [(bi, i, h, sd) for bi in range(len(ns)) for i in range(hps) for h in range(2) for sd in sides]
            first_head = lax.broadcasted_iota(jnp.int32, (BAND, LANE), 1) < HEAD_DIM
            own = lambda h, t: jnp.where(first_head if h == 0 else jnp.logical_not(first_head), t, jnp.zeros_like(t))
            pair = lambda rows, i, part: qkv_ref[rows, (3 * i + part) * LANE:(3 * i + part + 1) * LANE]
            key_rows = lambda bi, sd: _rows(ns[bi] - sd)
            qs = {}
            for bi in range(len(ns)):
                for i in range(hps):
                    q_pair = pair(_rows(ns[bi]), i, 0) * scale
                    do = do_ref[_rows(ns[bi]), i * LANE:(i + 1) * LANE]
                    do16 = do.astype(BF16)
                    for h in range(2):
                        if compact:
                            st_ref, head = rest[1], 2 * i + h
                            ebar = st_ref[_rows(ns[bi]), head:head + 1]
                            lcol = st_ref[_rows(ns[bi]), STAT_LSE_LANE + head:STAT_LSE_LANE + head + 1]
                        else:
                            ebar = jnp.sum(own(h, do * rest[1][_rows(ns[bi]), i * LANE:(i + 1) * LANE]), -1, keepdims=True)
                            lcol = rest[2][_rows(ns[bi]), i * LANE + h * HEAD_DIM:i * LANE + h * HEAD_DIM + 1]
                        qs[(bi, i, h)] = (own(h, q_pair), q_pair, own(h, do16), do16, ebar, lcol)
            raw = []
            for bi, i, h, sd in chains:
                q, _, do_h, _, _, _ = qs[(bi, i, h)]
                bias_blk = bias_ref[2 * i + h, :, :BAND] if sd else bias_ref[2 * i + h, :, BAND:]
                s = lax.dot_general(q, pair(key_rows(bi, sd), i, 1), NT, preferred_element_type=F32) + bias_blk
                dp = lax.dot_general(do_h, pair(key_rows(bi, sd), i, 2), NT, preferred_element_type=F32)
                raw.append((s, dp))
            soft = []
            for (bi, i, h, sd), (s, dp) in zip(chains, raw):
                ebar, lcol = qs[(bi, i, h)][4:]
                p = jnp.exp(s - lcol)
                ds = p * (dp - ebar)
                if sd:
                    db_ref[2 * i + h, :, :BAND] += ds
                else:
                    db_ref[2 * i + h, :, BAND:] += ds
                soft.append((p.astype(BF16), ds.astype(BF16)))
            grads = {}
            for (bi, i, h, sd), (p16, ds16) in zip(chains, soft):
                _, q_pair, _, do16 = qs[(bi, i, h)][:4]
                grads[(bi, i, h, sd)] = (
                    jnp.dot(ds16, pair(key_rows(bi, sd), i, 1), preferred_element_type=F32),
                    lax.dot_general(ds16, q_pair, TN, preferred_element_type=F32),
                    lax.dot_general(p16, do16, TN, preferred_element_type=F32))
            both = lambda bi, i, sd, which: jnp.where(first_head, grads[(bi, i, 0, sd)][which],
                                                      grads[(bi, i, 1, sd)][which])
            carry = list(carry) if carry is not None else None
            for bi, n in enumerate(ns):
                for i in range(hps):
                    base = 3 * LANE * i
                    dq = both(bi, i, 0, 0)
                    if with_prev:
                        dq = dq + both(bi, i, 1, 0)
                        dqkv_ref[_rows(n - 1), base + LANE:base + 2 * LANE] = (
                            carry[2 * i] + both(bi, i, 1, 1)).astype(BF16)
                        dqkv_ref[_rows(n - 1), base + 2 * LANE:base + 3 * LANE] = (
                            carry[2 * i + 1] + both(bi, i, 1, 2)).astype(BF16)
                    dqkv_ref[_rows(n), base:base + LANE] = (dq * scale).astype(BF16)
                carry = [t for i in range(hps) for t in (both(bi, i, 0, 1), both(bi, i, 0, 2))]
            return tuple(carry)

        carry = _for_blocks(blocks, nblk, 2 if hps == 1 else 1, None)
        for i in range(hps):
            base = 3 * LANE * i
            dqkv_ref[_rows(nblk - 1), base + LANE:base + 2 * LANE] = carry[2 * i].astype(BF16)
            dqkv_ref[_rows(nblk - 1), base + 2 * LANE:base + 3 * LANE] = carry[2 * i + 1].astype(BF16)

    qspec = pl.BlockSpec((None, None, sub, 3 * LANE * hps), lambda hp, b, r: (b, r, 0, hp))
    bspec = pl.BlockSpec((2 * hps, BAND, 2 * BAND), lambda hp, b, r: (hp, 0, 0))
    if compact:
        cspecs = [pl.BlockSpec((None, None, sub, ATT_OUT), lambda hp, b, r: (b, r, 0, 0)),
                  pl.BlockSpec((None, None, sub, LANE), lambda hp, b, r: (b, r, 0, 0))]
    else:
        cspecs = [pl.BlockSpec((None, sub, hps * LANE), lambda hp, b, r: (b, 0, r * (npair // hps) + hp))] * 3
    return _pcall(
        body, name=name, grid=(npair // hps, nb, d),
        in_specs=[qspec, _bias_spec(group, hps)] + cspecs, out_specs=[qspec, bspec],
        out_shape=[jax.ShapeDtypeStruct(qkv4.shape, BF16),
                   jax.ShapeDtypeStruct((GROUP_HEADS, BAND, 2 * BAND), F32)],
        compiler_params=_params("parallel", "arbitrary", "arbitrary"),
    )(qkv4, bias, *cotangent)


def _head_lanes(first_lane, one_channel):
    c = lax.broadcasted_iota(jnp.int32, (ATT_OUT, LANE), 0)
    lane = lax.broadcasted_iota(jnp.int32, (ATT_OUT, LANE), 1)
    hit = lane == first_lane + c // HEAD_DIM
    if one_channel:
        hit = hit & (c % HEAD_DIM == 0)
    return hit.astype(BF16)


def _exact_dot(v, m01, dims=None):
    parts = _split3(v)
    if dims is None:
        dot = lambda t: jnp.dot(t, m01, preferred_element_type=F32)
    else:
        dot = lambda t: lax.dot_general(t, m01, dims, preferred_element_type=F32)
    return (dot(parts[0]) + dot(parts[1])) + dot(parts[2])


def _store_own_order(value, tile_refs, out_ref):
    d, per, width = out_ref.shape
    for j in range(width // LANE):
        tile_refs[j][...] = value[:, j * LANE:(j + 1) * LANE]
    for r in range(d):
        rows = pl.ds(r, per, stride=d)
        for j in range(width // LANE):
            out_ref[r, :, j * LANE:(j + 1) * LANE] = tile_refs[j][rows, :].astype(out_ref.dtype)


def _natural_rows(p_ref, tile_refs):
    d, per, width = p_ref.shape
    for r in range(d):
        rows = pl.ds(r, per, stride=d)
        for j in range(width // LANE):
            tile_refs[j][rows, :] = p_ref[r, :, j * LANE:(j + 1) * LANE]
    return jnp.concatenate([tile_refs[j][...] for j in range(width // LANE)], axis=1)


def _natural_value(value, d, tile_refs):
    total, width = value.shape
    per = total // d
    for r in range(d):
        rows = pl.ds(r, per, stride=d)
        for j in range(width // LANE):
            tile_refs[j][rows, :] = value[r * per:(r + 1) * per, j * LANE:(j + 1) * LANE]
    return jnp.concatenate([tile_refs[j][...] for j in range(width // LANE)], axis=1)


def _combine_fwd(o0, l0, dilated, gatt):
    nb, seq, _ = gatt.shape
    tm = 512
    ntile = ATT_OUT // LANE

    def body(o0_ref, l0_ref, o1_ref, l1_ref, o2_ref, l2_ref, g_ref, oa_ref, oatt_ref, lse_ref, *tile_refs):
        spread = _head_lanes(0, False)
        l0v = l0_ref[...]
        l1v = _exact_dot(_natural_rows(l1_ref, tile_refs), spread, NT)
        l2v = _exact_dot(_natural_rows(l2_ref, tile_refs), spread, NT)
        m = jnp.maximum(jnp.maximum(l0v, l1v), l2v)
        tot = m + jnp.log(jnp.exp(l0v - m) + jnp.exp(l1v - m) + jnp.exp(l2v - m))
        o = jnp.exp(l0v - tot) * o0_ref[...]
        o = o + jnp.exp(l1v - tot) * _natural_rows(o1_ref, tile_refs)
        o = o + jnp.exp(l2v - tot) * _natural_rows(o2_ref, tile_refs)
        g = g_ref[...]
        oa_ref[...] = (o * (g * _sigmoid(g))).astype(BF16)
        oatt_ref[...] = o
        lse_ref[...] = tot

    spec = pl.BlockSpec((None, tm, ATT_OUT), lambda b, i: (b, i, 0))
    own = lambda t: pl.BlockSpec((None, t.shape[1], tm // t.shape[1], t.shape[3]), lambda b, i: (b, 0, i, 0))
    (o1, l1), (o2, l2) = dilated
    return _pcall(
        body, name="attn_combine", grid=(nb, seq // tm),
        in_specs=[spec, spec, own(o1), own(l1), own(o2), own(l2), spec], out_specs=[spec] * 3,
        out_shape=[jax.ShapeDtypeStruct((nb, seq, ATT_OUT), BF16), jax.ShapeDtypeStruct((nb, seq, ATT_OUT), F32),
                   jax.ShapeDtypeStruct((nb, seq, ATT_OUT), F32)],
        scratch_shapes=[pltpu.VMEM((tm, LANE), F32)] * ntile,
        compiler_params=_params("parallel", "parallel"),
    )(o0, l0, o1, l1, o2, l2, gatt)


def _combine_bwd(dya16, w_bra, gatt, o_att, lse, dilations):
    nb, seq, _ = gatt.shape
    tm = 512

    def body(dya_ref, w_ref, g_ref, o_ref, l_ref, do_ref, dg_ref, *rest):
        ntile = ATT_OUT // LANE
        outs, tile_refs = rest[:-ntile], rest[-ntile:]
        doa = lax.dot_general(dya_ref[...], w_ref[...], NT, preferred_element_type=F32)
        g = g_ref[...]
        sg = _sigmoid(g)
        do = doa * (g * sg)
        do_ref[...] = do
        stats = (_exact_dot(do * o_ref[...], _head_lanes(0, False))
                 + _exact_dot(l_ref[...], _head_lanes(STAT_LSE_LANE, True)))
        dg_ref[...] = (doa * o_ref[...] * (sg * (1.0 + g * (1.0 - sg)))).astype(BF16)
        for k in range(len(dilations)):
            _store_own_order(do, tile_refs, outs[2 * k])
            _store_own_order(stats, tile_refs, outs[2 * k + 1])

    spec = pl.BlockSpec((None, tm, ATT_OUT), lambda b, i: (b, i, 0))
    own = lambda d, width: pl.BlockSpec((None, d, tm // d, width), lambda b, i: (b, 0, i, 0))
    outs = _pcall(
        body, name="attn_combine_bwd", grid=(nb, seq // tm),
        in_specs=[_tok_spec(tm, D_MODEL), _whole(w_bra, True)] + [spec] * 3,
        out_specs=[spec, spec] + [own(d, w) for d in dilations for w in (ATT_OUT, LANE)],
        out_shape=[jax.ShapeDtypeStruct((nb, seq, ATT_OUT), F32), jax.ShapeDtypeStruct((nb, seq, ATT_OUT), BF16)]
        + [jax.ShapeDtypeStruct((nb, d, seq // d, w), t) for d in dilations for w, t in ((ATT_OUT, BF16), (LANE, F32))],
        scratch_shapes=[pltpu.VMEM((tm, LANE), F32)] * (ATT_OUT // LANE),
        compiler_params=_params("parallel", "parallel"),
    )(dya16, w_bra, gatt, o_att, lse)
    return outs[0], outs[1], outs[2:]


CONV_TM = 1024
CONV_TC = 1024


def _shift_down(cur, halo, k):
    rolled = pltpu.roll(cur, k, 0)
    hro = pltpu.roll(halo, k, 0)
    row = lax.broadcasted_iota(jnp.int32, hro.shape, 0)
    return jnp.concatenate([jnp.where(row < k, hro, rolled[:8]), rolled[8:]], axis=0)


def _shift_up(cur, halo, k):
    n = cur.shape[0]
    rolled = pltpu.roll(cur, n - k, 0)
    hro = pltpu.roll(halo, 8 - k, 0)
    row = lax.broadcasted_iota(jnp.int32, hro.shape, 0)
    return jnp.concatenate([rolled[:n - 8], jnp.where(row >= 8 - k, hro, rolled[n - 8:])], axis=0)


def _conv_pre(cur, halo, w_ref, b_ref):
    acc = cur * w_ref[3:4, :] + b_ref[...]
    for k in range(1, 4):
        acc = acc + _shift_down(cur, halo, k) * w_ref[3 - k:4 - k, :]
    return acc


def _conv_specs(seq):
    nblk = seq // CONV_TM
    cur = pl.BlockSpec((None, CONV_TM, CONV_TC), lambda cb, b, i: (b, i, cb))
    prev = pl.BlockSpec((None, 8, CONV_TC), lambda cb, b, i: (b, jnp.maximum(i * (CONV_TM // 8) - 1, 0), cb))
    nxt = pl.BlockSpec((None, 8, CONV_TC),
                       lambda cb, b, i: (b, jnp.minimum((i + 1) * (CONV_TM // 8), seq // 8 - 1), cb))
    wspec = pl.BlockSpec((4, CONV_TC), lambda cb, b, i: (0, cb))
    bspec = pl.BlockSpec((1, CONV_TC), lambda cb, b, i: (0, cb))
    return nblk, cur, prev, nxt, wspec, bspec


def _conv_fwd(xin, w4, bias, name):
    nb, seq, ch = xin.shape
    _, cur, prev, _, wspec, bspec = _conv_specs(seq)

    def body(x_ref, h_ref, w_ref, b_ref, o_ref):
        halo = jnp.where(pl.program_id(2) > 0, h_ref[...], 0.0)
        pre = _conv_pre(x_ref[...], halo, w_ref, b_ref)
        o_ref[...] = pre * _sigmoid(pre)

    return _pcall(
        body, name=name, grid=(ch // CONV_TC, nb, seq // CONV_TM),
        in_specs=[cur, prev, wspec, bspec], out_specs=cur,
        out_shape=jax.ShapeDtypeStruct(xin.shape, F32),
        compiler_params=_params("parallel", "parallel", "parallel"),
    )(xin, xin, w4, bias)


def _conv_bwd_pre(dact, xin, w4, bias, name):
    nb, seq, ch = xin.shape
    _, cur, prev, _, wspec, bspec = _conv_specs(seq)

    def body(da_ref, x_ref, h_ref, w_ref, b_ref, dp_ref, s_ref):
        b, i = pl.program_id(1), pl.program_id(2)

        @pl.when((b == 0) & (i == 0))
        def _():
            s_ref[...] = jnp.zeros_like(s_ref)

        halo = jnp.where(i > 0, h_ref[...], 0.0)
        x = x_ref[...]
        pre = _conv_pre(x, halo, w_ref, b_ref)
        sg = _sigmoid(pre)
        dpre = da_ref[...] * (sg * (1.0 + pre * (1.0 - sg)))
        dp_ref[...] = dpre
        s_ref[3:4, :] += jnp.sum(dpre * x, 0, keepdims=True)
        for k in range(1, 4):
            s_ref[3 - k:4 - k, :] += jnp.sum(dpre * _shift_down(x, halo, k), 0, keepdims=True)
        s_ref[4:5, :] += jnp.sum(dpre, 0, keepdims=True)

    return _pcall(
        body, name=name, grid=(ch // CONV_TC, nb, seq // CONV_TM),
        in_specs=[cur, cur, prev, wspec, bspec],
        out_specs=[cur, pl.BlockSpec((8, CONV_TC), lambda cb, b, i: (0, cb))],
        out_shape=[jax.ShapeDtypeStruct(xin.shape, F32), jax.ShapeDtypeStruct((8, ch), F32)],
        compiler_params=_params("parallel", "arbitrary", "arbitrary"),
    )(dact, xin, xin, w4, bias)


def _conv_bwd_x(dpre, w4, name):
    nb, seq, ch = dpre.shape
    nblk, cur, _, nxt, wspec, _ = _conv_specs(seq)

    def body(d_ref, n_ref, w_ref, o_ref):
        halo = jnp.where(pl.program_id(2) < nblk - 1, n_ref[...], 0.0)
        cur_v = d_ref[...]
        acc = cur_v * w_ref[3:4, :]
        for j in range(1, 4):
            acc = acc + _shift_up(cur_v, halo, j) * w_ref[3 - j:4 - j, :]
        o_ref[...] = acc.astype(BF16)

    return _pcall(
        body, name=name, grid=(ch // CONV_TC, nb, seq // CONV_TM),
        in_specs=[cur, nxt, wspec], out_specs=cur,
        out_shape=jax.ShapeDtypeStruct(dpre.shape, BF16),
        compiler_params=_params("parallel", "parallel", "parallel"),
    )(dpre, dpre, w4)


def _step_sizes(raw, tb_ref):
    shift = (LANE - GROUP_SSM_HEADS * pl.program_id(0)) % LANE
    v = pltpu.roll(raw + tb_ref[...], shift, 1)
    own = lax.broadcasted_iota(jnp.int32, v.shape, 1) < GROUP_SSM_HEADS
    sp = jnp.maximum(v, 0.0) + jnp.log1p(jnp.exp(-jnp.abs(v)))
    return jnp.where(own, sp, 0.0), jnp.where(own, _sigmoid(v), 0.0)


def _group_lanes(t):
    pads = [(0, 0)] * (t.ndim - 1) + [(0, LANE - GROUP_SSM_HEADS)]
    return jnp.stack([jnp.pad(t[..., GROUP_SSM_HEADS * g:GROUP_SSM_HEADS * (g + 1)], pads) for g in range(SSM_GROUPS)])


def _ungroup_lanes(t):
    return jnp.concatenate([t[g][..., :GROUP_SSM_HEADS] for g in range(SSM_GROUPS)], axis=-1)


def _decays(dt, al_ref):
    row = lax.broadcasted_iota(jnp.int32, (CHUNK, CHUNK), 0)
    col = lax.broadcasted_iota(jnp.int32, (CHUNK, CHUNK), 1)
    tril = (row >= col).astype(BF16)
    triu = (row <= col).astype(BF16)
    arow = -jnp.exp(al_ref[...])
    hi, mid, lo = _split3(dt * arow)
    down = lambda t: jnp.dot(tril, t, preferred_element_type=F32)
    across = lambda t: lax.dot_general(t, triu, TN, preferred_element_type=F32)
    acs = (down(hi) + down(mid)) + down(lo)
    acs_t = (across(hi) + across(mid)) + across(lo)
    return arow, acs, acs_t, row >= col, triu


STEP_CHUNKS = 8


def _ssd_specs(nb, seq):
    nc = seq // CHUNK
    hw = GROUP_SSM_HEADS * HEAD_DIM
    rows, steps = STEP_CHUNKS * CHUNK, nc // STEP_CHUNKS

    def mk(rev):
        cidx = (lambda c: steps - 1 - c) if rev else (lambda c: c)
        wide = pl.BlockSpec((None, rows, hw), lambda g, b, c: (b, cidx(c), g))
        xbc = pl.BlockSpec((None, rows, XBC_GROUP), lambda g, b, c: (b, cidx(c), g))
        lanes = pl.BlockSpec((None, None, rows, LANE), lambda g, b, c: (g, b, cidx(c), 0))
        prev = pl.BlockSpec((None, STEP_CHUNKS, None, D_STATE, hw), lambda g, b, c: (b, cidx(c), g, 0, 0))
        raw = pl.BlockSpec((None, rows, LANE), lambda g, b, c: (b, cidx(c), 0))
        return wide, xbc, lanes, prev, raw

    grow = pl.BlockSpec((None, 1, LANE), lambda g, b, c: (g, 0, 0))
    nwspec = pl.BlockSpec((1, hw), lambda g, b, c: (0, g))
    tbspec = pl.BlockSpec((1, LANE), lambda g, b, c: (0, 0))
    return nc, steps, hw, mk, grow, nwspec, tbspec


def _head_expand():
    hw = GROUP_SSM_HEADS * HEAD_DIM
    r = lax.broadcasted_iota(jnp.int32, (LANE, hw), 0)
    c = lax.broadcasted_iota(jnp.int32, (LANE, hw), 1)
    return ((c // HEAD_DIM) == r).astype(BF16)


def _split3(v):
    hi = v.astype(BF16)
    rest = v - hi.astype(F32)
    mid = rest.astype(BF16)
    return hi, mid, (rest - mid.astype(F32)).astype(BF16)


def _to_channels(v, e):
    hi, mid, lo = _split3(v)
    dot = lambda t: jnp.dot(t, e, preferred_element_type=F32)
    return (dot(hi) + dot(mid)) + dot(lo)


def _to_heads(w, e):
    hi, mid, lo = _split3(w)
    dot = lambda t: lax.dot_general(t, e, (((1,), (1,)), ((), ())), preferred_element_type=F32)
    return (dot(hi) + dot(mid)) + dot(lo)


def _row8(v):
    return jnp.broadcast_to(v, (8, v.shape[1]))


def _ssd_chunk_setup(dt, al_ref, ds_ref):
    arow, acs, acs_t, causal, triu = _decays(dt, al_ref)
    e = _head_expand()
    dtx = _to_channels(dt, e)
    acsx = _to_channels(acs, e)
    lastx = acsx[CHUNK - 1:CHUNK, :]
    dskx = _to_channels(_row8(ds_ref[...]), e)[0:1, :]
    return arow, acs, acs_t, causal, triu, e, dtx, acsx, lastx, dskx


def _ssd_fwd(xbc, dt_raw, dt_bias_row, z, alog_g, dskip_g, normw):
    nb, seq, _ = xbc.shape
    nc, steps, hw, mk, grow, nwspec, tbspec = _ssd_specs(nb, seq)
    wide, xbc_spec, lanes, prev, raw = mk(False)
    tn = (((0,), (0,)), ((), ()))

    def body(xbc_ref, dt_ref, tb_ref, z_ref, al_ref, ds_ref, nw_ref, ys_ref, y_ref, sp_ref, st_ref):
        @pl.when(pl.program_id(2) == 0)
        def _():
            st_ref[...] = jnp.zeros_like(st_ref)

        for ci in range(STEP_CHUNKS):
            chunk(ci, xbc_ref, dt_ref, tb_ref, z_ref, al_ref, ds_ref, nw_ref, ys_ref, y_ref, sp_ref, st_ref)

    def chunk(ci, xbc_ref, dt_ref, tb_ref, z_ref, al_ref, ds_ref, nw_ref, ys_ref, y_ref, sp_ref, st_ref):
        rows = slice(ci * CHUNK, (ci + 1) * CHUNK)
        dt, _ = _step_sizes(dt_ref[rows, :], tb_ref)
        _, acs, acs_t, causal, _, _, dtx, acsx, lastx, dskx = _ssd_chunk_setup(dt, al_ref, ds_ref)
        bmat = xbc_ref[rows, GROUP_CH:GROUP_CH + D_STATE].astype(BF16)
        cmat = xbc_ref[rows, GROUP_CH + D_STATE:].astype(BF16)
        cb = lax.dot_general(cmat, bmat, (((1,), (1,)), ((), ())), preferred_element_type=F32)
        x = xbc_ref[rows, :GROUP_CH]
        xdt = x * dtx
        xdt16 = xdt.astype(BF16)
        first_head = lax.broadcasted_iota(jnp.int32, (CHUNK, LANE), 1) < HEAD_DIM
        pairs = []
        for hp in range(GROUP_SSM_HEADS // 2):
            xp = xdt16[:, hp * LANE:(hp + 1) * LANE]
            two = []
            for j in (2 * hp, 2 * hp + 1):
                lmat = jnp.exp(jnp.where(causal, acs[:, j:j + 1] - acs_t[j:j + 1, :], -jnp.inf))
                two.append(jnp.dot((cb * lmat).astype(BF16), xp, preferred_element_type=F32))
            pairs.append(jnp.where(first_head, two[0], two[1]))
        yd = jnp.concatenate(pairs, axis=1)
        s_prev = st_ref[...]
        s16 = s_prev.astype(BF16)
        sp_ref[ci] = s16
        yo = jnp.dot(cmat, s16, preferred_element_type=F32) * jnp.exp(acsx)
        sts = lax.dot_general(bmat, (xdt * jnp.exp(lastx - acsx)).astype(BF16), tn, preferred_element_type=F32)
        st_ref[...] = s_prev * jnp.exp(lastx) + sts
        y = yd + yo + dskx * x
        zz = z_ref[rows, :]
        u = y * (zz * _sigmoid(zz))
        rn = lax.rsqrt(jnp.mean(u * u, -1, keepdims=True) + RMS_EPS)
        ys_ref[rows, :] = (u * rn * nw_ref[...]).astype(BF16)
        y_ref[rows, :] = y

    return _pcall(
        body, name="ssd_fwd", grid=(SSM_GROUPS, nb, steps),
        in_specs=[xbc_spec, raw, tbspec, wide, grow, grow, nwspec],
        out_specs=[wide, wide, prev],
        out_shape=[jax.ShapeDtypeStruct((nb, seq, D_INNER), BF16), jax.ShapeDtypeStruct((nb, seq, D_INNER), F32),
                   jax.ShapeDtypeStruct((nb, nc, SSM_GROUPS, D_STATE, hw), BF16)],
        scratch_shapes=[pltpu.VMEM((D_STATE, hw), F32)],
        compiler_params=_params("parallel", "parallel", "arbitrary"),
    )(xbc, dt_raw, dt_bias_row, z, alog_g, dskip_g, normw)


def _ssd_bwd(xbc, dt_raw, dt_bias_row, z, y, dys, sprev, alog_g, dskip_g, normw):
    nb, seq, _ = xbc.shape
    nc, steps, hw, mk, grow, nwspec, tbspec = _ssd_specs(nb, seq)
    wide, xbc_spec, lanes, prev, raw = mk(True)
    nt = (((1,), (1,)), ((), ()))
    tn = (((0,), (0,)), ((), ()))

    def body(xbc_ref, dt_ref, tb_ref, z_ref, y_ref, dys_ref, sp_ref, al_ref, ds_ref, nw_ref,
             dxbc_ref, ddt_ref, dz_ref, small_ref, dnw_ref, g_ref):
        b, c = pl.program_id(1), pl.program_id(2)

        @pl.when((b == 0) & (c == 0))
        def _():
            small_ref[...] = jnp.zeros_like(small_ref)
            dnw_ref[...] = jnp.zeros_like(dnw_ref)

        @pl.when(c == 0)
        def _():
            g_ref[...] = jnp.zeros_like(g_ref)

        for ci in reversed(range(STEP_CHUNKS)):
            chunk(ci, xbc_ref, dt_ref, tb_ref, z_ref, y_ref, dys_ref, sp_ref, al_ref, ds_ref, nw_ref,
                  dxbc_ref, ddt_ref, dz_ref, small_ref, dnw_ref, g_ref)

    def chunk(ci, xbc_ref, dt_ref, tb_ref, z_ref, y_ref, dys_ref, sp_ref, al_ref, ds_ref, nw_ref,
              dxbc_ref, ddt_ref, dz_ref, small_ref, dnw_ref, g_ref):
        rows = slice(ci * CHUNK, (ci + 1) * CHUNK)
        yv, zz, dys_v, nw = y_ref[rows, :], z_ref[rows, :], dys_ref[rows, :], nw_ref[...]
        sz = _sigmoid(zz)
        silu = zz * sz
        u = yv * silu
        rn = lax.rsqrt(jnp.mean(u * u, -1, keepdims=True) + RMS_EPS)
        gn = dys_v * nw
        du = rn * gn - u * (rn * rn * rn) * jnp.mean(u * gn, -1, keepdims=True)
        dnw_ref[...] += jnp.sum(dys_v * u * rn, 0, keepdims=True)
        dy = du * silu
        dz_ref[rows, :] = du * yv * (sz * (1.0 + zz * (1.0 - sz)))

        dt, sg = _step_sizes(dt_ref[rows, :], tb_ref)
        arow, acs, acs_t, causal, triu, e, dtx, acsx, lastx, dskx = _ssd_chunk_setup(dt, al_ref, ds_ref)
        dfsx = jnp.exp(acsx)
        dtex = jnp.exp(lastx - acsx)
        bmat = xbc_ref[rows, GROUP_CH:GROUP_CH + D_STATE].astype(BF16)
        cmat = xbc_ref[rows, GROUP_CH + D_STATE:].astype(BF16)
        cb = lax.dot_general(cmat, bmat, nt, preferred_element_type=F32)
        x = xbc_ref[rows, :GROUP_CH]
        xdt = x * dtx
        xdt16 = xdt.astype(BF16)
        xdte = xdt * dtex
        dy16 = dy.astype(BF16)
        dyd = dy * dfsx
        dyd16 = dyd.astype(BF16)
        s16 = sp_ref[ci]
        g = g_ref[...]
        g16 = g.astype(BF16)
        cs = jnp.dot(cmat, s16, preferred_element_type=F32)
        dc_off = lax.dot_general(dyd16, s16, nt, preferred_element_type=F32)
        g_here = lax.dot_general(cmat, dyd16, tn, preferred_element_type=F32)
        bg = jnp.dot(bmat, g16, preferred_element_type=F32)
        db_st = lax.dot_general(xdte.astype(BF16), g16, nt, preferred_element_type=F32)
        ddte_w = bg * xdte
        dcd = _to_heads(_row8(jnp.sum(g * s16.astype(F32), 0, keepdims=True)), e)[0:1, :]
        lane = lax.broadcasted_iota(jnp.int32, (CHUNK, LANE), 1)
        first_head = lane < HEAD_DIM
        sub = lax.broadcasted_iota(jnp.int32, (CHUNK, LANE), 0)
        dacs = jnp.zeros((CHUNK, LANE), F32)
        colsums = jnp.zeros((CHUNK, LANE), F32)
        dcb = jnp.zeros((CHUNK, CHUNK), F32)
        pairs = []
        for hp in range(GROUP_SSM_HEADS // 2):
            xp = xdt16[:, hp * LANE:(hp + 1) * LANE]
            dyp = dy16[:, hp * LANE:(hp + 1) * LANE]
            two = []
            for idx, j in enumerate((2 * hp, 2 * hp + 1)):
                lmat = jnp.exp(jnp.where(causal, acs[:, j:j + 1] - acs_t[j:j + 1, :], -jnp.inf))
                mf = cb * lmat
                dy_h = jnp.where(first_head if idx == 0 else jnp.logical_not(first_head), dyp, jnp.zeros_like(dyp))
                dm = lax.dot_general(dy_h, xp, nt, preferred_element_type=F32)
                two.append(lax.dot_general(mf.astype(BF16), dyp, tn, preferred_element_type=F32))
                wmat = dm * mf
                dcb = dcb + dm * lmat
                dacs = jnp.where(lane == j, jnp.sum(wmat, -1, keepdims=True), dacs)
                colsums = jnp.where(sub == j, jnp.sum(wmat, 0, keepdims=True), colsums)
            pairs.append(jnp.where(first_head, two[0], two[1]))
        dxdt = bg * dtex + jnp.concatenate(pairs, axis=1)
        dacs = dacs - colsums.T + _to_heads(dyd * cs - ddte_w, e)
        cd_row = jnp.exp(acs[CHUNK - 1:CHUNK, :])
        tail = _to_heads(_row8(jnp.sum(ddte_w, 0, keepdims=True)), e)[0:1, :] + dcd * cd_row
        dacs = dacs + jnp.where(sub == CHUNK - 1, tail, 0.0)
        d_hi, d_mid, d_lo = _split3(dacs)
        up = lambda t: jnp.dot(triu, t, preferred_element_type=F32)
        da = (up(d_hi) + up(d_mid)) + up(d_lo)
        ddt_raw = (da * arow + _to_heads(dxdt * x, e)) * sg
        ddt_ref[rows, :] = ddt_raw
        small_ref[0:1, :] += jnp.sum(da * dt, 0, keepdims=True) * arow
        small_ref[1:2, :] += _to_heads(_row8(jnp.sum(dy * x, 0, keepdims=True)), e)[0:1, :]
        small_ref[2:3, :] += jnp.sum(ddt_raw, 0, keepdims=True)
        dcb16 = dcb.astype(BF16)
        dxbc_ref[rows, GROUP_CH + D_STATE:] = dc_off + jnp.dot(dcb16, bmat, preferred_element_type=F32)
        dxbc_ref[rows, GROUP_CH:GROUP_CH + D_STATE] = db_st + lax.dot_general(dcb16, cmat, tn,
                                                                               preferred_element_type=F32)
        dxbc_ref[rows, :GROUP_CH] = dxdt * dtx + dskx * dy
        g_ref[...] = g * jnp.exp(lastx) + g_here

    return _pcall(
        body, name="ssd_bwd", grid=(SSM_GROUPS, nb, steps),
        in_specs=[xbc_spec, raw, tbspec, wide, wide, wide, prev, grow, grow, nwspec],
        out_specs=[xbc_spec, lanes, wide,
                   pl.BlockSpec((None, 8, LANE), lambda g, b, c: (g, 0, 0)), nwspec],
        out_shape=[jax.ShapeDtypeStruct((nb, seq, CONV_DIM), F32),
                   jax.ShapeDtypeStruct((SSM_GROUPS, nb, seq, LANE), F32),
                   jax.ShapeDtypeStruct((nb, seq, D_INNER), F32),
                   jax.ShapeDtypeStruct((SSM_GROUPS, 8, LANE), F32),
                   jax.ShapeDtypeStruct((1, D_INNER), F32)],
        scratch_shapes=[pltpu.VMEM((D_STATE, hw), F32)],
        compiler_params=_params("parallel", "arbitrary", "arbitrary"),
    )(xbc, dt_raw, dt_bias_row, z, y, dys, sprev, alog_g, dskip_g, normw)


EW_TM = 256


def _merge_fwd(oa16, y_ssm16, w_bra, w_brb, gm, bgate):
    nb, seq, _ = oa16.shape

    def body(oa_ref, ys_ref, wa_ref, wb_ref, ga_ref, gb_ref, bg_ref, a_ref, b_ref, o_ref):
        y_a = jnp.dot(oa_ref[...], wa_ref[...], preferred_element_type=F32)
        y_b = jnp.dot(ys_ref[...], wb_ref[...], preferred_element_type=F32)
        a_ref[...] = y_a
        b_ref[...] = y_b
        sa = _sigmoid(ga_ref[...] + bg_ref[0:1, :])
        sb = _sigmoid(gb_ref[...] + bg_ref[1:2, :])
        o_ref[...] = (sa * y_a + sb * y_b).astype(BF16)

    spec = pl.BlockSpec((None, EW_TM, D_MODEL), lambda b, i: (b, i, 0))
    spec1 = pl.BlockSpec((None, EW_TM, D_MODEL), lambda b, i: (b, i, 1))
    return _pcall(
        body, name="merge_fwd", grid=(nb, seq // EW_TM),
        in_specs=[_tok_spec(EW_TM, ATT_OUT), _tok_spec(EW_TM, D_INNER), _whole(w_bra, True), _whole(w_brb, True),
                  spec, spec1, pl.BlockSpec((8, D_MODEL), lambda b, i: (0, 0))],
        out_specs=[spec] * 3,
        out_shape=[jax.ShapeDtypeStruct((nb, seq, D_MODEL), F32)] * 2 + [jax.ShapeDtypeStruct((nb, seq, D_MODEL), BF16)],
        compiler_params=_params("parallel", "parallel"),
    )(oa16, y_ssm16, w_bra, w_brb, gm, gm, bgate)


def _merge_bwd(dpre16, w_out16, y_a, y_b, gm, bgate):
    nb, seq, _ = y_a.shape

    def body(dp_ref, w_ref, a_ref, b_ref, ga_ref, gb_ref, bg_ref, dya_ref, dyb_ref, dg_ref, s_ref):
        @pl.when((pl.program_id(0) == 0) & (pl.program_id(1) == 0))
        def _():
            s_ref[...] = jnp.zeros_like(s_ref)

        dm = lax.dot_general(dp_ref[...], w_ref[...], NT, preferred_element_type=F32)
        sa = _sigmoid(ga_ref[...] + bg_ref[0:1, :])
        sb = _sigmoid(gb_ref[...] + bg_ref[1:2, :])
        dya_ref[...] = (dm * sa).astype(BF16)
        dyb_ref[...] = (dm * sb).astype(BF16)
        dga = dm * a_ref[...] * (sa * (1.0 - sa))
        dgb = dm * b_ref[...] * (sb * (1.0 - sb))
        dg_ref[:, :D_MODEL] = dga.astype(BF16)
        dg_ref[:, D_MODEL:] = dgb.astype(BF16)
        s_ref[0:1, :] += jnp.sum(dga, 0, keepdims=True)
        s_ref[1:2, :] += jnp.sum(dgb, 0, keepdims=True)

    spec = pl.BlockSpec((None, EW_TM, D_MODEL), lambda b, i: (b, i, 0))
    spec1 = pl.BlockSpec((None, EW_TM, D_MODEL), lambda b, i: (b, i, 1))
    small = pl.BlockSpec((8, D_MODEL), lambda b, i: (0, 0))
    return _pcall(
        body, name="merge_bwd", grid=(nb, seq // EW_TM),
        in_specs=[spec, _whole(w_out16, True), spec, spec, spec, spec1, small],
        out_specs=[spec, spec, pl.BlockSpec((None, EW_TM, 2 * D_MODEL), lambda b, i: (b, i, 0)), small],
        out_shape=[jax.ShapeDtypeStruct((nb, seq, D_MODEL), BF16), jax.ShapeDtypeStruct((nb, seq, D_MODEL), BF16),
                   jax.ShapeDtypeStruct((nb, seq, 2 * D_MODEL), BF16), jax.ShapeDtypeStruct((8, D_MODEL), F32)],
        compiler_params=_params("arbitrary", "arbitrary"),
    )(dpre16, w_out16, y_a, y_b, gm, gm, bgate)


def _ln_loss(x, merged16, w_out16, gp, p16, w_ple16, target, bgate, ln_g, ln_b):
    nb, seq, _ = x.shape

    def body(x_ref, m_ref, wo_ref, gp_ref, p_ref, wp_ref, t_ref, bg_ref, g_ref, b_ref,
             dx_ref, dp_ref, dpw_ref, dgp_ref, s_ref):
        @pl.when((pl.program_id(0) == 0) & (pl.program_id(1) == 0))
        def _():
            s_ref[...] = jnp.zeros_like(s_ref)

        sp = _sigmoid(gp_ref[...] + bg_ref[2:3, :])
        pw = jnp.dot(p_ref[...], wp_ref[...], preferred_element_type=F32)
        mix = jnp.dot(m_ref[...], wo_ref[...], preferred_element_type=F32)
        pre = ALPHA * x_ref[...] + mix + sp * pw
        mu = jnp.mean(pre, -1, keepdims=True)
        cen = pre - mu
        rstd = lax.rsqrt(jnp.mean(cen * cen, -1, keepdims=True) + LN_EPS)
        xhat = cen * rstd
        err = xhat * g_ref[...] + b_ref[...] - t_ref[...]
        dy = err * (1.0 / D_MODEL)
        dxh = dy * g_ref[...]
        dpre = rstd * (dxh - jnp.mean(dxh, -1, keepdims=True) - xhat * jnp.mean(dxh * xhat, -1, keepdims=True))
        dx_ref[...] = ALPHA * dpre
        dp_ref[...] = dpre.astype(BF16)
        dpw_ref[...] = (dpre * sp).astype(BF16)
        dgp = dpre * pw * (sp * (1.0 - sp))
        dgp_ref[...] = dgp.astype(BF16)
        s_ref[0:1, :] += jnp.sum(dy * xhat, 0, keepdims=True)
        s_ref[1:2, :] += jnp.sum(dy, 0, keepdims=True)
        s_ref[2:3, :] += jnp.sum(dgp, 0, keepdims=True)
        s_ref[3:4, :] += jnp.sum(err * err, 0, keepdims=True)

    spec = pl.BlockSpec((None, EW_TM, D_MODEL), lambda b, i: (b, i, 0))
    small = pl.BlockSpec((8, D_MODEL), lambda b, i: (0, 0))
    row = pl.BlockSpec((1, D_MODEL), lambda b, i: (0, 0))
    return _pcall(
        body, name="ln_loss", grid=(nb, seq // EW_TM),
        in_specs=[spec, spec, _whole(w_out16, True), spec, pl.BlockSpec((None, EW_TM, PLE_DIM), lambda b, i: (b, i, 0)),
                  _whole(w_ple16, True), spec, small, row, row],
        out_specs=[spec] * 4 + [small],
        out_shape=[jax.ShapeDtypeStruct((nb, seq, D_MODEL), F32)] + [jax.ShapeDtypeStruct((nb, seq, D_MODEL), BF16)] * 3
        + [jax.ShapeDtypeStruct((8, D_MODEL), F32)],
        compiler_params=_params("arbitrary", "arbitrary"),
    )(x, merged16, w_out16, gp, p16, w_ple16, target, bgate, ln_g, ln_b)


def _adamw_update(w_ref, g_ref, m_ref, v_ref, d_ref, nm_ref, nv_ref):
    c1 = 1.0 - ADAM_B1 ** ADAM_STEP
    c2 = 1.0 - ADAM_B2 ** ADAM_STEP
    gv = g_ref[...]
    nm = ADAM_B1 * m_ref[...] + (1.0 - ADAM_B1) * gv
    nv = ADAM_B2 * v_ref[...] + (1.0 - ADAM_B2) * (gv * gv)
    d_ref[...] = -ADAM_LR * ((nm / c1) / (jnp.sqrt(nv / c2) + ADAM_EPS) + ADAM_WD * w_ref[...])
    nm_ref[...] = nm
    nv_ref[...] = nv


def _adamw(w, g, m, v, name):
    rows, cols = w.shape
    tr = _row_tile(rows, cols, 8, 5 << 19)

    def body(*refs):
        _adamw_update(*refs)

    spec = pl.BlockSpec((tr, cols), lambda i: (i, 0))
    return _pcall(
        body, name=name, grid=(rows // tr,), in_specs=[spec] * 4, out_specs=[spec] * 3,
        out_shape=[jax.ShapeDtypeStruct(w.shape, F32)] * 3, compiler_params=_params("parallel"),
    )(w, g, m, v)


def _adamw_small(ws, gs, ms, vs, name):
    n = len(ws)

    def body(*refs):
        for i in range(n):
            _adamw_update(*[refs[k * n + i] for k in range(7)])

    outs = _pcall(body, name=name, out_shape=[jax.ShapeDtypeStruct(w.shape, F32) for w in ws] * 3,
                  compiler_params=_params())(*ws, *gs, *ms, *vs)
    return outs[:n], outs[n:2 * n], outs[2 * n:]


def _place():
    return lax.axis_index("x"), lax.axis_index("y"), lax.axis_index("c")


def _other_chips(x, y):
    return [(1 - x, y), (x, 1 - y), (1 - x, 1 - y)]


def _remote(src, dst, send_sem, recv_sem, to):
    return pltpu.make_async_remote_copy(src_ref=src, dst_ref=dst, send_sem=send_sem, recv_sem=recv_sem,
                                        device_id=to, device_id_type=MESH)


ANY = pl.BlockSpec(memory_space=pl.ANY)
D2D_CHUNK_BYTES = 512 * 1024
ICI_CHUNK_BYTES = 2 * 1024 * 1024


def _row_chunks(rows, row_bytes, chunk_bytes=D2D_CHUNK_BYTES):
    per = max(16, chunk_bytes // row_bytes // 16 * 16)
    return [(s, min(per, rows - s)) for s in range(0, rows, per)]


def _row_tile(rows, cols, align, limit=1 << 21):
    best = None
    for cand in range(align, rows + 1, align):
        if rows % cand == 0 and cand * cols * 4 <= limit:
            best = cand
    return best or rows


TOKEN_TM = 512


def _allgather_pieces(pieces, x, dilations):
    n, nd = len(pieces), len(dilations)
    nb, seq, kdim = x.shape
    halves = [_row_chunks(p.shape[0] // 2, p.shape[1] * p.dtype.itemsize, ICI_CHUNK_BYTES) for p in pieces]
    entries = [(a, q, s, m, j) for a in range(n) for q, (s, m) in enumerate(halves[a]) for j in range(3)]
    slot = {(a, q, j): k for k, (a, q, _, _, j) in enumerate(entries)}
    n_ici = len(entries)

    def body(*refs):
        ins, x_hbm, outs, order_hbm = refs[:n], refs[n], refs[n + 1:2 * n + 1], refs[2 * n + 1:2 * n + 2 + nd]
        send_sems, recv_sems = refs[2 * n + 2 + nd:2 * n + 4 + nd]
        tile_refs = refs[2 * n + 4 + nd:]
        x, y, c = _place()
        me = 2 * x + y
        sibling = (x, y, 1 - c)
        chips = _other_chips(x, y)

        def landed(a, s, m, j, core):
            half = ins[a].shape[0] // 2
            return outs[a].at[2 * chips[j][0] + chips[j][1], pl.ds(core * half + s, m)]

        sent = []
        for k, (a, q, s, m, j) in enumerate(entries):
            if j < 2:
                half = ins[a].shape[0] // 2
                cp = _remote(ins[a].at[pl.ds(c * half + s, m)], outs[a].at[me, pl.ds(c * half + s, m)],
                             send_sems.at[k], recv_sems.at[k], (*chips[j], c))
                cp.start()
                sent.append(cp)

        def reorder(x_ref, nat_ref, *own_refs):
            xv = x_ref[0]
            nat_ref[0] = xv.astype(BF16)
            for o_ref in own_refs:
                _store_own_order(xv, tile_refs, o_ref.at[0])

        pltpu.emit_pipeline(
            reorder, grid=(nb, seq // TOKEN_TM),
            in_specs=[pl.BlockSpec((1, TOKEN_TM, kdim), lambda b, i: (b, i, 0))],
            out_specs=[pl.BlockSpec((1, TOKEN_TM, kdim), lambda b, i: (b, i, 0))]
            + [pl.BlockSpec((1, d, TOKEN_TM // d, kdim), lambda b, i: (b, 0, i, 0)) for d in dilations],
        )(x_hbm, *order_hbm)

        def pass_to_sibling(k, blk):
            fw = _remote(blk, blk, send_sems.at[n_ici + k], recv_sems.at[n_ici + k], sibling)
            fw.start()
            sent.append(fw)

        for k, (a, q, s, m, j) in enumerate(entries):
            if j < 2:
                blk = landed(a, s, m, j, c)
                _remote(blk, blk, send_sems.at[k], recv_sems.at[k], (*chips[j], c)).wait_recv()
                first = q < (len(halves[a]) + 1) // 2
                if (j == 0) == first:
                    on = slot[(a, q, 2)]
                    rl = _remote(blk, blk, send_sems.at[on], recv_sems.at[on], (*chips[1 - j], c))
                    rl.start()
                    sent.append(rl)
                pass_to_sibling(k, blk)
        for k, (a, q, s, m, j) in enumerate(entries):
            if j == 2:
                blk = landed(a, s, m, j, c)
                _remote(blk, blk, send_sems.at[k], recv_sems.at[k], (*chips[j], c)).wait_recv()
                pass_to_sibling(k, blk)
        for k, (a, q, s, m, j) in enumerate(entries):
            blk = landed(a, s, m, j, 1 - c)
            _remote(blk, blk, send_sems.at[n_ici + k], recv_sems.at[n_ici + k], sibling).wait_recv()
        for cp in sent:
            cp.wait_send()

    gathered = _pcall(
        body, name="allgather_weights", in_specs=[ANY] * (n + 1), out_specs=[ANY] * (n + 1 + nd),
        out_shape=[jax.ShapeDtypeStruct((4,) + p.shape, p.dtype) for p in pieces]
        + [jax.ShapeDtypeStruct((nb, seq, kdim), BF16)]
        + [jax.ShapeDtypeStruct((nb, d, seq // d, kdim), BF16) for d in dilations],
        scratch_shapes=[pltpu.SemaphoreType.DMA((2 * n_ici,)), pltpu.SemaphoreType.DMA((2 * n_ici,))]
        + [pltpu.VMEM((TOKEN_TM, LANE), F32)] * (kdim // LANE),
        compiler_params=pltpu.CompilerParams(has_side_effects=True, vmem_limit_bytes=VMEM_LIMIT_BYTES),
    )(*pieces, x)
    gathered, orders = gathered[:n], gathered[n:]
    x16p = [orders[0]] + [o.reshape(nb, seq, kdim) for o in orders[1:]]
    cx, cy, _ = _place()
    return [lax.dynamic_update_slice(g, p[None], (2 * cx + cy, 0, 0)) for g, p in zip(gathered, pieces)], x16p


def _sibling_exchange(grads):
    n = len(grads)
    chunks = [_row_chunks(g.shape[1] // 2, g.shape[2] * g.dtype.itemsize) for g in grads]
    n_sem = 4 * sum(len(ch) for ch in chunks)

    def body(*refs):
        ins, gots = refs[:n], refs[n:2 * n]
        send_sems, recv_sems = refs[2 * n:]
        x, y, c = _place()
        sibling = (x, y, 1 - c)
        work = []
        for a in range(n):
            half = ins[a].shape[1] // 2
            for piece in range(4):
                for s, m in chunks[a]:
                    k = len(work)
                    cp = _remote(ins[a].at[piece, pl.ds((1 - c) * half + s, m)], gots[a].at[piece, pl.ds(s, m)],
                                 send_sems.at[k], recv_sems.at[k], sibling)
                    cp.start()
                    work.append(cp)
        for cp in work:
            cp.wait()

    return _pcall(
        body, name="grad_sibling_exchange", in_specs=[ANY] * n, out_specs=[ANY] * n,
        out_shape=[jax.ShapeDtypeStruct((4, g.shape[1] // 2, g.shape[2]), g.dtype) for g in grads],
        scratch_shapes=[pltpu.SemaphoreType.DMA((n_sem,)), pltpu.SemaphoreType.DMA((n_sem,))],
        compiler_params=pltpu.CompilerParams(has_side_effects=True),
    )(*grads)


def _sibling_gather(fulls):
    n = len(fulls)
    chunks = [_row_chunks(f.shape[0] // 2, f.shape[1] * f.dtype.itemsize) for f in fulls]
    n_sem = sum(len(ch) for ch in chunks)

    def body(*refs):
        outs = refs[n:2 * n]
        send_sems, recv_sems = refs[2 * n:]
        x, y, c = _place()
        sibling = (x, y, 1 - c)
        work = []
        for a in range(n):
            h = outs[a].shape[0] // 2
            for s, m in chunks[a]:
                k = len(work)
                mine = outs[a].at[pl.ds(c * h + s, m)]
                cp = _remote(mine, mine, send_sems.at[k], recv_sems.at[k], sibling)
                cp.start()
                work.append((a, s, m, cp))
        for k, (a, s, m, cp) in enumerate(work):
            h = outs[a].shape[0] // 2
            cp.wait_send()
            theirs = outs[a].at[pl.ds((1 - c) * h + s, m)]
            _remote(theirs, theirs, send_sems.at[k], recv_sems.at[k], sibling).wait_recv()

    return _pcall(
        body, name="grad_sibling_gather", in_specs=[ANY] * n, out_specs=[ANY] * n,
        out_shape=[jax.ShapeDtypeStruct(f.shape, f.dtype) for f in fulls],
        input_output_aliases={a: a for a in range(n)},
        scratch_shapes=[pltpu.SemaphoreType.DMA((n_sem,)), pltpu.SemaphoreType.DMA((n_sem,))],
        compiler_params=pltpu.CompilerParams(has_side_effects=True),
    )(*fulls)


def _pair_sum(grad, got, place, name):
    _, rows, cols = grad.shape
    half = rows // 2
    tr = _row_tile(half, cols, 16)

    def body(p_ref, a_ref, b_ref, o_ref):
        o_ref[...] = (a_ref[...].astype(F32) + b_ref[...].astype(F32)).astype(BF16)

    return _pcall(
        body, name=name,
        grid_spec=pltpu.PrefetchScalarGridSpec(
            num_scalar_prefetch=1, grid=(4, half // tr),
            in_specs=[pl.BlockSpec((None, tr, cols), lambda k, i, p: (k, p[1] * (half // tr) + i, 0)),
                      pl.BlockSpec((None, tr, cols), lambda k, i, p: (k, i, 0))],
            out_specs=pl.BlockSpec((None, tr, cols), lambda k, i, p: (k, i, 0))),
        out_shape=jax.ShapeDtypeStruct((4, half, cols), BF16),
        compiler_params=_params("parallel", "parallel"),
    )(place, grad, got)


def _chip_sum(sums, got, place, name):
    _, h, cols = sums.shape
    tr = _row_tile(h, cols, 16)

    def body(p_ref, own_ref, g0, g1, g2, o_ref):
        o_ref[...] = ((own_ref[...].astype(F32) + g0[...].astype(F32)) + g1[...].astype(F32)) + g2[...].astype(F32)

    gspec = lambda j: pl.BlockSpec((None, tr, cols), lambda i, p: (j, i, 0))
    return _pcall(
        body, name=name,
        grid_spec=pltpu.PrefetchScalarGridSpec(
            num_scalar_prefetch=1, grid=(h // tr,),
            in_specs=[pl.BlockSpec((None, tr, cols), lambda i, p: (p[0], i, 0)), gspec(0), gspec(1), gspec(2)],
            out_specs=pl.BlockSpec((tr, cols), lambda i, p: (p[1] * (h // tr) + i, 0))),
        out_shape=jax.ShapeDtypeStruct((2 * h, cols), F32),
        compiler_params=_params("parallel"),
    )(place, sums, got, got, got)


def _allsum8(buf, name):
    rows = buf.shape[0]

    def body(in_ref, out_ref, all_ref, send_sems, recv_sems):
        x, y, c = _place()
        me = 4 * x + 2 * y + c
        all_ref[me] = in_ref[...]
        work = []
        for rel in range(1, 8):
            fx, fy, fc = (rel >> 2) & 1, (rel >> 1) & 1, rel & 1
            to = (x ^ fx, y ^ fy, c ^ fc)
            cp = _remote(in_ref, all_ref.at[me], send_sems.at[rel - 1], recv_sems.at[rel - 1], to)
            cp.start()
            work.append((cp, 4 * to[0] + 2 * to[1] + to[2]))
        for rel, (cp, frm) in enumerate(work):
            cp.wait_send()
            blk = all_ref.at[frm]
            _remote(blk, blk, send_sems.at[rel], recv_sems.at[rel], (x, y, c)).wait_recv()
        acc = all_ref[0]
        for k in range(1, 8):
            acc = acc + all_ref[k]
        out_ref[...] = acc

    return _pcall(
        body, name=name, in_specs=[pl.BlockSpec(memory_space=pltpu.VMEM)],
        out_specs=pl.BlockSpec(memory_space=pltpu.VMEM),
        out_shape=jax.ShapeDtypeStruct((rows, LANE), F32),
        scratch_shapes=[pltpu.VMEM((8, rows, LANE), F32), pltpu.SemaphoreType.DMA((7,)),
                        pltpu.SemaphoreType.DMA((7,))],
        compiler_params=pltpu.CompilerParams(has_side_effects=True),
    )(buf)


def _pack_rows(arrs):
    flats = [a.reshape(-1).astype(F32) for a in arrs]
    starts = np.cumsum([0] + [-(-f.shape[0] // LANE) * LANE for f in flats])
    total = -(-int(starts[-1]) // (8 * LANE)) * 8 * LANE
    flat = sum(jnp.pad(f, (int(s), total - int(s) - f.shape[0])) for f, s in zip(flats, starts))
    return flat.reshape(total // LANE, LANE)


def _unpack_rows(buf, shapes):
    flat = buf.reshape(-1)
    outs, off = [], 0
    for s in shapes:
        n = int(np.prod(s))
        outs.append(flat[off:off + n].reshape(s))
        off += -(-n // LANE) * LANE
    return outs


def _local_grads(x, p, target, wseg, w_br16, w_out16, w_ple16, b_gate, conv_w, conv_b, dt_bias, a_log, d_skip,
                 ssm_norm_w, ln_g, ln_b, rel_bias, finish_dx, x16p):
    nb, seq, _ = x.shape
    bmaps = jnp.asarray(_bucket_maps())
    bias = _bias_tables(rel_bias, bmaps)
    bgate8 = jnp.pad(b_gate, ((0, 5), (0, 0)))
    dils = [d for _, d in PATTERNS]

    x16 = x16p[0]
    p16 = p.astype(BF16)
    qkv = [_proj(x16p[g], [wseg["qkv%d" % g]], BF16, "proj_qkv%d" % g, True, 2 * MM_TM)[0].reshape(
        nb, dils[g], seq // dils[g], -1) for g in range(3)]
    nat = {}
    for gi, (group, tm) in enumerate(NAT_GROUPS):
        outs = _proj(x16, [wseg[s] for s in group], F32, "proj_nat%d" % gi, True, tm)
        nat.update(zip(group, outs))
    att = [_attn_fwd(qkv[g], bias, g, dils[g], "attn_fwd%d" % g) for g in range(3)]
    oa, o_att, lse = _combine_fwd(att[0][0], att[0][1], att[1:], nat["gatt"])

    conv_wg, conv_bg = _xbc_group_order(conv_w), _xbc_group_order(conv_b)
    act = _conv_fwd(nat["xbc"], conv_wg, conv_bg, "conv_fwd")
    dt_bias_row = jnp.pad(dt_bias, ((0, 0), (0, LANE - SSM_HEADS)))
    alog_g, dskip_g = _group_lanes(a_log), _group_lanes(d_skip)
    y_ssm, y_all, sprev = _ssd_fwd(act, nat["dt"], dt_bias_row, nat["z"], alog_g, dskip_g, ssm_norm_w)

    w_bra, w_brb = w_br16[:ATT_OUT], w_br16[ATT_OUT:]
    y_a, y_b, merged = _merge_fwd(oa, y_ssm, w_bra, w_brb, nat["gm"], bgate8)

    dx, dpre16, dpw16, dgp16, ln_sums = _ln_loss(x, merged, w_out16, nat["gp"], p16, w_ple16, target, bgate8,
                                                 ln_g, ln_b)
    loss_sum = (0.5 / D_MODEL) * jnp.sum(ln_sums[3])
    dya16, dyb16, dgm16, mg_sums = _merge_bwd(dpre16, w_out16, y_a, y_b, nat["gm"], bgate8)
    dys = _dx([dyb16], [w_brb], [], "dx_yssm")
    g_w_br, g_w_out, g_w_ple = _dw_stacked(
        [[(oa, dya16), (y_ssm, dyb16)], [(merged, dpre16)], [(p16, dpw16)]], BF16, "dw_branch_out_ple")

    do_att, dgatt16, own_order = _combine_bwd(dya16, w_bra, nat["gatt"], o_att, lse, dils[1:])
    dseg = {"gatt": dgatt16, "gm": dgm16, "gp": dgp16}
    dbias = []
    for g in range(3):
        cotangent = (do_att, o_att, lse) if g == 0 else (own_order[2 * g - 2], own_order[2 * g - 1])
        dqkv, db = _attn_bwd(qkv[g], bias, g, cotangent, dils[g],
                             "attn_bwd%d" % g)
        dseg["qkv%d" % g] = dqkv.reshape(nb, seq, -1)
        dbias.append(db)
    g_rel = _bias_grad(jnp.concatenate(dbias, axis=0), bmaps)[:, 0, :NUM_BUCKETS].T

    dact, ddtg, dz, ssd_small, g_normw = _ssd_bwd(
        act, nat["dt"], dt_bias_row, nat["z"], y_all, dys, sprev, alog_g, dskip_g, ssm_norm_w)
    dseg["z"] = dz
    dseg["dt"] = jnp.pad(_ungroup_lanes(ddtg), ((0, 0), (0, 0), (0, LANE - SSM_HEADS)))
    dpre, conv_sums = _conv_bwd_pre(dact, nat["xbc"], conv_wg, conv_bg, "conv_bwd")
    dseg["xbc"] = _conv_bwd_x(dpre, conv_wg, "conv_bwd_x")
    csum = _xbc_reference_order(conv_sums)

    dh_own = [(dseg["qkv%d" % g].reshape(nb, dils[g], seq // dils[g], -1), wseg["qkv%d" % g]) for g in (1, 2)]
    dwseg = {"qkv%d" % g: _dw(x16p[g], [dseg["qkv%d" % g]], BF16, "dw_qkv%d" % g, True)[0] for g in range(3)}
    for gi, group in enumerate(DW_GROUPS):
        dwseg.update(zip(group, _dw(x16, [dseg[s] for s in group], BF16, "dw_nat%d" % gi, True)))
    names = ["qkv0"] + [s for group, _ in NAT_GROUPS for s in group]
    dx = finish_dx([dseg[s] for s in names], [wseg[s] for s in names], [dx], dh_own, dwseg, g_w_br, g_w_out, g_w_ple)

    small = dict(
        b_gate=jnp.stack([mg_sums[0], mg_sums[1], ln_sums[2]]),
        conv_w=csum[0:4], conv_b=csum[4:5],
        dt_bias=_ungroup_lanes(ssd_small[:, 2:3, :]), a_log=_ungroup_lanes(ssd_small[:, 0:1, :]),
        d_skip=_ungroup_lanes(ssd_small[:, 1:2, :]), ssm_norm_w=g_normw,
        ln_g=ln_sums[0:1], ln_b=ln_sums[1:2], rel_bias=g_rel)
    return loss_sum, dx, small


DX_TM = 256
SMALL_ORDER = ("b_gate", "conv_w", "conv_b", "dt_bias", "a_log", "d_skip", "ssm_norm_w", "ln_g", "ln_b", "rel_bias")
SMALL_FULL_SHAPES = dict(b_gate=(3, 1024), conv_w=(4, 3072), conv_b=(1, 3072), dt_bias=(1, 32), a_log=(1, 32),
                         d_skip=(1, 32), ssm_norm_w=(1, 2048), ln_g=(1, 1024), ln_b=(1, 1024), rel_bias=(32, 36))


def kernel(x, p, w_in, b_gate, conv_w, conv_b, dt_bias, a_log, d_skip, ssm_norm_w, w_branch, w_out, w_ple, ln_g, ln_b, rel_bias, loss_target, m_w_in, m_b_gate, m_conv_w, m_conv_b, m_dt_bias, m_a_log, m_d_skip, m_ssm_norm_w, m_w_branch, m_w_out, m_w_ple, m_ln_g, m_ln_b, m_rel_bias, v_w_in, v_b_gate, v_conv_w, v_conv_b, v_dt_bias, v_a_log, v_d_skip, v_ssm_norm_w, v_w_branch, v_w_out, v_w_ple, v_ln_g, v_ln_b, v_rel_bias):
    cx, cy, cc = _place()
    chip = 2 * cx + cy
    dev = 4 * cx + 2 * cy + cc

    w_in_t = jnp.transpose(w_in[0])
    win16 = _shard_to_window(w_in_t, chip)
    (g_win, g_br, g_out, g_ple, g_small), x16p = _allgather_pieces(
        [win16, w_branch[0].astype(BF16), w_out[0].astype(BF16), w_ple[0].astype(BF16),
         _pack_rows([b_gate[0], conv_w[0]])], x, [d for _, d in PATTERNS][1:])
    wseg = _assemble(g_win)
    w_br16 = g_br.reshape(4 * 704, D_MODEL)
    w_out16 = g_out.reshape(D_MODEL, D_MODEL)
    w_ple16 = jnp.transpose(g_ple, (1, 0, 2)).reshape(PLE_DIM, D_MODEL)
    per_chip = [_unpack_rows(g_small[k], [(3, 256), (4, 768)]) for k in range(4)]
    b_gate_full = _join_last([pc[0] for pc in per_chip])
    conv_w_full = _join_last([pc[1] for pc in per_chip])

    place = jnp.stack([chip, cc]).astype(jnp.int32)
    reduced = []

    def finish_dx(dhs, ws, accs, own_order_dhs, dwseg, d_br, d_out, d_ple):
        grads = [_pack(dwseg), d_br.reshape(4, 704, D_MODEL), d_out.reshape(4, 256, D_MODEL),
                 jnp.transpose(d_ple.reshape(PLE_DIM, 4, 256), (1, 0, 2))]
        got = _sibling_exchange(grads)
        chip_sums = [_pair_sum(g, t, place, "grad_pair_sum_%d" % i) for i, (g, t) in enumerate(zip(grads, got))]
        dx, others = _dx(dhs, ws, accs, "dx_w_in_and_grad_chip_scatter", True, DX_TM, chip_sums, own_order_dhs)
        fulls = [_chip_sum(s, t, place, "grad_chip_sum_%d" % i) for i, (s, t) in enumerate(zip(chip_sums, others))]
        reduced.extend(_sibling_gather(fulls))
        return dx

    loss_sum, grad_x, small = _local_grads(
        x, p[0], loss_target, wseg, w_br16, w_out16, w_ple16, b_gate_full, conv_w_full, conv_b, dt_bias, a_log,
        d_skip, ssm_norm_w, ln_g, ln_b, rel_bias, finish_dx, x16p)
    big = reduced
    g_w_in = lax.optimization_barrier(_window_to_shard(big[0], chip))
    g_w_branch, g_w_out, g_w_ple = big[1], big[2], big[3]
    small_sum = _allsum8(_pack_rows([small[n] for n in SMALL_ORDER] + [loss_sum.reshape(1, 1)]),
                         "allsum_small_grads")
    *reduced_small, loss = _unpack_rows(small_sum, [SMALL_FULL_SHAPES[n] for n in SMALL_ORDER] + [(1, 1)])
    loss = loss.reshape(())
    sg = dict(zip(SMALL_ORDER, reduced_small))
    sg["b_gate"] = lax.dynamic_slice_in_dim(sg["b_gate"], chip * 256, 256, axis=1)
    sg["conv_w"] = lax.dynamic_slice_in_dim(sg["conv_w"], chip * 768, 768, axis=1)
    del dev

    upd = {}
    upd["w_in"] = [jnp.transpose(t) for t in _adamw(w_in_t, g_w_in, jnp.transpose(m_w_in[0]),
                                                      jnp.transpose(v_w_in[0]), "adamw_w_in")]
    upd["w_branch"] = _adamw(w_branch[0], g_w_branch, m_w_branch[0], v_w_branch[0], "adamw_w_branch")
    upd["w_out"] = _adamw(w_out[0], g_w_out, m_w_out[0], v_w_out[0], "adamw_w_out")
    upd["w_ple"] = _adamw(w_ple[0], g_w_ple, m_w_ple[0], v_w_ple[0], "adamw_w_ple")
    small_w = dict(b_gate=b_gate, conv_w=conv_w, conv_b=conv_b, dt_bias=dt_bias, a_log=a_log, d_skip=d_skip,
                   ssm_norm_w=ssm_norm_w, ln_g=ln_g, ln_b=ln_b, rel_bias=rel_bias)
    small_m = dict(b_gate=m_b_gate, conv_w=m_conv_w, conv_b=m_conv_b, dt_bias=m_dt_bias, a_log=m_a_log,
                   d_skip=m_d_skip, ssm_norm_w=m_ssm_norm_w, ln_g=m_ln_g, ln_b=m_ln_b, rel_bias=m_rel_bias)
    small_v = dict(b_gate=v_b_gate, conv_w=v_conv_w, conv_b=v_conv_b, dt_bias=v_dt_bias, a_log=v_a_log,
                   d_skip=v_d_skip, ssm_norm_w=v_ssm_norm_w, ln_g=v_ln_g, ln_b=v_ln_b, rel_bias=v_rel_bias)
    for n in SMALL_ORDER:
        sg[n] = sg[n].reshape(small_w[n].shape)
    s_delta, s_m, s_v = _adamw_small(*[[t[n] for n in SMALL_ORDER] for t in (small_w, sg, small_m, small_v)],
                                     "adamw_small")
    for i, n in enumerate(SMALL_ORDER):
        upd[n] = (s_delta[i], s_m[i], s_v[i])

    order = ("w_in", "b_gate", "conv_w", "conv_b", "dt_bias", "a_log", "d_skip", "ssm_norm_w", "w_branch", "w_out",
             "w_ple", "ln_g", "ln_b", "rel_bias")
    grads = dict(sg, w_in=jnp.transpose(g_w_in)[None],w_branch=g_w_branch[None], w_out=g_w_out[None], w_ple=g_w_ple[None])
    lead = lambda n, t: t[None] if n in ("w_in", "w_branch", "w_out", "w_ple") else t
    return (loss, grad_x, *[grads[n] for n in order], *[lead(n, upd[n][0]) for n in order],
            *[lead(n, upd[n][1]) for n in order], *[lead(n, upd[n][2]) for n in order])
```
